```python
import math
import jax, jax.numpy as jnp
from jax import lax
import numpy as np

D_MODEL = 1024
BATCH = 8
SEQ = 2048
DEPTH = 1

CHUNK = 64
Q_BLOCK = 128
ATTN_WIDTH = D_MODEL // 2
ATTN_HEAD_DIM = 64
ATTN_HEADS = ATTN_WIDTH // ATTN_HEAD_DIM
SSM_WIDTH = D_MODEL - ATTN_WIDTH
SSM_GROUP = 16
SSM_GROUPS = SSM_WIDTH // SSM_GROUP
SSM_STATE = 64
D_FF = 2816
CONV_WIDTH = 3
IN_COLS = 3 * ATTN_WIDTH + ATTN_HEADS + SSM_WIDTH
EPS = 1e-6
NEG_INF = -1e30

kernel_name = "hybrid_fox_s5_convffn_block"


def _rms_norm(x, g):
    xf = x.astype(jnp.float32)
    y = xf * lax.rsqrt(jnp.mean(xf * xf, axis=-1, keepdims=True) + EPS)
    return (y * g.astype(jnp.float32)).astype(x.dtype)


def _forgetting_attention(q, k, v, log_f):
    seq = q.shape[2]
    c = jnp.cumsum(log_f, axis=-1)
    scale = ATTN_HEAD_DIM ** -0.5
    outs = []
    for start in range(0, seq, Q_BLOCK):
        end = start + Q_BLOCK
        qb = q[:, :, start:end]
        kb = k[:, :, :end]
        vb = v[:, :, :end]
        s = jnp.einsum('bhqd,bhkd->bhqk', qb, kb) * scale
        s = s + c[:, :, start:end, None] - c[:, :, None, :end]
        qpos = jnp.arange(start, end)[:, None]
        kpos = jnp.arange(end)[None, :]
        s = jnp.where(kpos <= qpos, s, NEG_INF)
        p = jax.nn.softmax(s, axis=-1)
        outs.append(jnp.einsum('bhqk,bhkd->bhqd', p, vb))
    return jnp.concatenate(outs, axis=2)


def _ssm_combine(e1, e2):
    a1r, a1i, b1r, b1i = e1
    a2r, a2i, b2r, b2i = e2
    ar = a2r * a1r - a2i * a1i
    ai = a2r * a1i + a2i * a1r
    br = a2r * b1r - a2i * b1i + b2r
    bi = a2r * b1i + a2i * b1r + b2i
    return (ar, ai, br, bi)


def _s5_ssm(u, lambda_re, lambda_im, log_step, b_re, b_im, c_re, c_im, d_skip):
    seq = u.shape[1]
    lr = lambda_re.astype(jnp.float32)
    li = lambda_im.astype(jnp.float32)
    step = jnp.exp(log_step.astype(jnp.float32))[:, None]
    er = jnp.exp(lr * step)
    ab_re = er * jnp.cos(li * step)
    ab_im = er * jnp.sin(li * step)
    num_re = ab_re - 1.0
    num_im = ab_im
    den = lr * lr + li * li
    f_re = (num_re * lr + num_im * li) / den
    f_im = (num_im * lr - num_re * li) / den
    br = b_re.astype(jnp.float32)
    bi = b_im.astype(jnp.float32)
    bb_re = f_re[:, :, None] * br - f_im[:, :, None] * bi
    bb_im = f_re[:, :, None] * bi + f_im[:, :, None] * br
    bu_re = jnp.einsum('bsgh,gph->bsgp', u, bb_re)
    bu_im = jnp.einsum('bsgh,gph->bsgp', u, bb_im)
    a_re = jnp.broadcast_to(ab_re[None, None], (1, seq) + ab_re.shape)
    a_im = jnp.broadcast_to(ab_im[None, None], (1, seq) + ab_im.shape)
    _, _, x_re, x_im = lax.associative_scan(_ssm_combine, (a_re, a_im, bu_re, bu_im), axis=1)
    y = (jnp.einsum('bsgp,ghp->bsgh', x_re, c_re.astype(jnp.float32))
         - jnp.einsum('bsgp,ghp->bsgh', x_im, c_im.astype(jnp.float32)))
    return y + d_skip.astype(jnp.float32) * u


def _causal_dwconv(h, w, b):
    seq = h.shape[1]
    hp = jnp.pad(h, ((0, 0), (CONV_WIDTH - 1, 0), (0, 0)))
    y = b
    for j in range(CONV_WIDTH):
        y = y + hp[:, j:j + seq] * w[j]
    return y


def _fwd_setup_inputs(seed: int = 0) -> dict:
    key = jax.random.key(seed)
    ks = jax.random.split(key, 24)
    f32 = jnp.float32
    n = lambda k, shape, s: jax.random.normal(k, shape, f32) * s
    x = n(ks[0], (BATCH, SEQ, D_MODEL), 1.0)
    g_mix = 1.0 + n(ks[1], (D_MODEL,), 0.02)
    w_in = n(ks[2], (D_MODEL, IN_COLS), D_MODEL ** -0.5)
    b_f = 2.0 + n(ks[3], (ATTN_HEADS,), 0.1)
    g_q = 1.0 + n(ks[4], (ATTN_HEAD_DIM,), 0.02)
    g_k = 1.0 + n(ks[5], (ATTN_HEAD_DIM,), 0.02)
    lambda_re = -0.5 + n(ks[6], (SSM_GROUPS, SSM_STATE), 0.01)
    lambda_im = (math.pi * jnp.arange(SSM_STATE, dtype=f32))[None, :] + n(ks[7], (SSM_GROUPS, SSM_STATE), 0.01)
    log_step = jax.random.uniform(ks[8], (SSM_GROUPS,), f32, math.log(1e-3), math.log(1e-1))
    b_re = n(ks[9], (SSM_GROUPS, SSM_STATE, SSM_GROUP), (2.0 * SSM_GROUP) ** -0.5)
    b_im = n(ks[10], (SSM_GROUPS, SSM_STATE, SSM_GROUP), (2.0 * SSM_GROUP) ** -0.5)
    c_re = n(ks[11], (SSM_GROUPS, SSM_GROUP, SSM_STATE), (2.0 * SSM_STATE) ** -0.5)
    c_im = n(ks[12], (SSM_GROUPS, SSM_GROUP, SSM_STATE), (2.0 * SSM_STATE) ** -0.5)
    d_skip = n(ks[13], (SSM_GROUPS, SSM_GROUP), 1.0)
    w_glu = n(ks[14], (SSM_WIDTH, SSM_WIDTH), SSM_WIDTH ** -0.5)
    b_glu = n(ks[15], (SSM_WIDTH,), 0.02)
    g_attn_out = 1.0 + n(ks[16], (ATTN_WIDTH,), 0.02)
    g_ssm_out = 1.0 + n(ks[17], (SSM_WIDTH,), 0.02)
    w_out = n(ks[18], (D_MODEL, D_MODEL), D_MODEL ** -0.5)
    g_ffn = 1.0 + n(ks[19], (D_MODEL,), 0.02)
    w_up = n(ks[20], (D_MODEL, 2 * D_FF), D_MODEL ** -0.5)
    conv_w = n(ks[21], (CONV_WIDTH, 2 * D_FF), CONV_WIDTH ** -0.5)
    conv_b = n(ks[22], (2 * D_FF,), 0.02)
    w_down = n(ks[23], (D_FF, D_MODEL), D_FF ** -0.5)
    return {"x": x, "g_mix": g_mix, "w_in": w_in, "b_f": b_f, "g_q": g_q, "g_k": g_k,
            "lambda_re": lambda_re, "lambda_im": lambda_im, "log_step": log_step,
            "b_re": b_re, "b_im": b_im, "c_re": c_re, "c_im": c_im, "d_skip": d_skip,
            "w_glu": w_glu, "b_glu": b_glu, "g_attn_out": g_attn_out, "g_ssm_out": g_ssm_out,
            "w_out": w_out, "g_ffn": g_ffn, "w_up": w_up, "conv_w": conv_w, "conv_b": conv_b,
            "w_down": w_down}


def _fwd_reference(x, g_mix, w_in, b_f, g_q, g_k, lambda_re, lambda_im, log_step, b_re, b_im,
              c_re, c_im, d_skip, w_glu, b_glu, g_attn_out, g_ssm_out, w_out, g_ffn,
              w_up, conv_w, conv_b, w_down):
    bsz, seq, _ = x.shape
    f32 = jnp.float32
    for _layer in range(DEPTH):
        h = _rms_norm(x, g_mix)
        z = h @ w_in
        o = 0
        q = z[..., o:o + ATTN_WIDTH]; o += ATTN_WIDTH
        k = z[..., o:o + ATTN_WIDTH]; o += ATTN_WIDTH
        v = z[..., o:o + ATTN_WIDTH]; o += ATTN_WIDTH
        f_logit = z[..., o:o + ATTN_HEADS]; o += ATTN_HEADS
        u = z[..., o:o + SSM_WIDTH]

        heads = lambda t: t.reshape(bsz, seq, ATTN_HEADS, ATTN_HEAD_DIM)
        qh = _rms_norm(heads(q), g_q).astype(f32).transpose(0, 2, 1, 3)
        kh = _rms_norm(heads(k), g_k).astype(f32).transpose(0, 2, 1, 3)
        vh = heads(v).astype(f32).transpose(0, 2, 1, 3)
        log_f = jax.nn.log_sigmoid((f_logit + b_f).astype(f32)).transpose(0, 2, 1)
        att = _forgetting_attention(qh, kh, vh, log_f)
        att = att.transpose(0, 2, 1, 3).reshape(bsz, seq, ATTN_WIDTH).astype(x.dtype)

        ug = u.astype(f32).reshape(bsz, seq, SSM_GROUPS, SSM_GROUP)
        y = _s5_ssm(ug, lambda_re, lambda_im, log_step, b_re, b_im, c_re, c_im, d_skip)
        y = jax.nn.gelu(y.reshape(bsz, seq, SSM_WIDTH))
        y = y * jax.nn.sigmoid(y @ w_glu.astype(f32) + b_glu.astype(f32))
        ssm = y.astype(x.dtype)

        mixed = jnp.concatenate([_rms_norm(att, g_attn_out), _rms_norm(ssm, g_ssm_out)], axis=-1)
        x = x + mixed @ w_out

        h2 = _rms_norm(x, g_ffn)
        hh = _causal_dwconv(h2 @ w_up, conv_w, conv_b)
        gate = hh[..., :D_FF]
        val = hh[..., D_FF:]
        x = x + (jax.nn.silu(gate) * val) @ w_down
    return x


import jax as _jax
import jax.numpy as _jnp

TWIN_FORMAT = 'train_step'
FWD_PARAMS = ['x', 'g_mix', 'w_in', 'b_f', 'g_q', 'g_k', 'lambda_re', 'lambda_im', 'log_step', 'b_re', 'b_im', 'c_re', 'c_im', 'd_skip', 'w_glu', 'b_glu', 'g_attn_out', 'g_ssm_out', 'w_out', 'g_ffn', 'w_up', 'conv_w', 'conv_b', 'w_down']
TWIN_WEIGHTS = ['g_mix', 'w_in', 'b_f', 'g_q', 'g_k', 'lambda_re', 'lambda_im', 'log_step', 'b_re', 'b_im', 'c_re', 'c_im', 'd_skip', 'w_glu', 'b_glu', 'g_attn_out', 'g_ssm_out', 'w_out', 'g_ffn', 'w_up', 'conv_w', 'conv_b', 'w_down']
TWIN_DIFF_INPUT = 'x'
TWIN_INPUTS = ['x', 'g_mix', 'w_in', 'b_f', 'g_q', 'g_k', 'lambda_re', 'lambda_im', 'log_step', 'b_re', 'b_im', 'c_re', 'c_im', 'd_skip', 'w_glu', 'b_glu', 'g_attn_out', 'g_ssm_out', 'w_out', 'g_ffn', 'w_up', 'conv_w', 'conv_b', 'w_down', 'loss_target', 'm_g_mix', 'm_w_in', 'm_b_f', 'm_g_q', 'm_g_k', 'm_lambda_re', 'm_lambda_im', 'm_log_step', 'm_b_re', 'm_b_im', 'm_c_re', 'm_c_im', 'm_d_skip', 'm_w_glu', 'm_b_glu', 'm_g_attn_out', 'm_g_ssm_out', 'm_w_out', 'm_g_ffn', 'm_w_up', 'm_conv_w', 'm_conv_b', 'm_w_down', 'v_g_mix', 'v_w_in', 'v_b_f', 'v_g_q', 'v_g_k', 'v_lambda_re', 'v_lambda_im', 'v_log_step', 'v_b_re', 'v_b_im', 'v_c_re', 'v_c_im', 'v_d_skip', 'v_w_glu', 'v_b_glu', 'v_g_attn_out', 'v_g_ssm_out', 'v_w_out', 'v_g_ffn', 'v_w_up', 'v_conv_w', 'v_conv_b', 'v_w_down']
TWIN_OUTPUTS = ['loss', 'grad_x', 'grad_g_mix', 'grad_w_in', 'grad_b_f', 'grad_g_q', 'grad_g_k', 'grad_lambda_re', 'grad_lambda_im', 'grad_log_step', 'grad_b_re', 'grad_b_im', 'grad_c_re', 'grad_c_im', 'grad_d_skip', 'grad_w_glu', 'grad_b_glu', 'grad_g_attn_out', 'grad_g_ssm_out', 'grad_w_out', 'grad_g_ffn', 'grad_w_up', 'grad_conv_w', 'grad_conv_b', 'grad_w_down', 'delta_g_mix', 'delta_w_in', 'delta_b_f', 'delta_g_q', 'delta_g_k', 'delta_lambda_re', 'delta_lambda_im', 'delta_log_step', 'delta_b_re', 'delta_b_im', 'delta_c_re', 'delta_c_im', 'delta_d_skip', 'delta_w_glu', 'delta_b_glu', 'delta_g_attn_out', 'delta_g_ssm_out', 'delta_w_out', 'delta_g_ffn', 'delta_w_up', 'delta_conv_w', 'delta_conv_b', 'delta_w_down', 'new_m_g_mix', 'new_m_w_in', 'new_m_b_f', 'new_m_g_q', 'new_m_g_k', 'new_m_lambda_re', 'new_m_lambda_im', 'new_m_log_step', 'new_m_b_re', 'new_m_b_im', 'new_m_c_re', 'new_m_c_im', 'new_m_d_skip', 'new_m_w_glu', 'new_m_b_glu', 'new_m_g_attn_out', 'new_m_g_ssm_out', 'new_m_w_out', 'new_m_g_ffn', 'new_m_w_up', 'new_m_conv_w', 'new_m_conv_b', 'new_m_w_down', 'new_v_g_mix', 'new_v_w_in', 'new_v_b_f', 'new_v_g_q', 'new_v_g_k', 'new_v_lambda_re', 'new_v_lambda_im', 'new_v_log_step', 'new_v_b_re', 'new_v_b_im', 'new_v_c_re', 'new_v_c_im', 'new_v_d_skip', 'new_v_w_glu', 'new_v_b_glu', 'new_v_g_attn_out', 'new_v_g_ssm_out', 'new_v_w_out', 'new_v_g_ffn', 'new_v_w_up', 'new_v_conv_w', 'new_v_conv_b', 'new_v_w_down']
TWIN_LEAF_KINDS = {'loss': 'loss', 'grad_x': 'grad_x', 'grad_g_mix': 'grad_w', 'grad_w_in': 'grad_w', 'grad_b_f': 'grad_w', 'grad_g_q': 'grad_w', 'grad_g_k': 'grad_w', 'grad_lambda_re': 'grad_w', 'grad_lambda_im': 'grad_w', 'grad_log_step': 'grad_w', 'grad_b_re': 'grad_w', 'grad_b_im': 'grad_w', 'grad_c_re': 'grad_w', 'grad_c_im': 'grad_w', 'grad_d_skip': 'grad_w', 'grad_w_glu': 'grad_w', 'grad_b_glu': 'grad_w', 'grad_g_attn_out': 'grad_w', 'grad_g_ssm_out': 'grad_w', 'grad_w_out': 'grad_w', 'grad_g_ffn': 'grad_w', 'grad_w_up': 'grad_w', 'grad_conv_w': 'grad_w', 'grad_conv_b': 'grad_w', 'grad_w_down': 'grad_w', 'delta_g_mix': 'delta_w', 'delta_w_in': 'delta_w', 'delta_b_f': 'delta_w', 'delta_g_q': 'delta_w', 'delta_g_k': 'delta_w', 'delta_lambda_re': 'delta_w', 'delta_lambda_im': 'delta_w', 'delta_log_step': 'delta_w', 'delta_b_re': 'delta_w', 'delta_b_im': 'delta_w', 'delta_c_re': 'delta_w', 'delta_c_im': 'delta_w', 'delta_d_skip': 'delta_w', 'delta_w_glu': 'delta_w', 'delta_b_glu': 'delta_w', 'delta_g_attn_out': 'delta_w', 'delta_g_ssm_out': 'delta_w', 'delta_w_out': 'delta_w', 'delta_g_ffn': 'delta_w', 'delta_w_up': 'delta_w', 'delta_conv_w': 'delta_w', 'delta_conv_b': 'delta_w', 'delta_w_down': 'delta_w', 'new_m_g_mix': 'new_m', 'new_m_w_in': 'new_m', 'new_m_b_f': 'new_m', 'new_m_g_q': 'new_m', 'new_m_g_k': 'new_m', 'new_m_lambda_re': 'new_m', 'new_m_lambda_im': 'new_m', 'new_m_log_step': 'new_m', 'new_m_b_re': 'new_m', 'new_m_b_im': 'new_m', 'new_m_c_re': 'new_m', 'new_m_c_im': 'new_m', 'new_m_d_skip': 'new_m', 'new_m_w_glu': 'new_m', 'new_m_b_glu': 'new_m', 'new_m_g_attn_out': 'new_m', 'new_m_g_ssm_out': 'new_m', 'new_m_w_out': 'new_m', 'new_m_g_ffn': 'new_m', 'new_m_w_up': 'new_m', 'new_m_conv_w': 'new_m', 'new_m_conv_b': 'new_m', 'new_m_w_down': 'new_m', 'new_v_g_mix': 'new_v', 'new_v_w_in': 'new_v', 'new_v_b_f': 'new_v', 'new_v_g_q': 'new_v', 'new_v_g_k': 'new_v', 'new_v_lambda_re': 'new_v', 'new_v_lambda_im': 'new_v', 'new_v_log_step': 'new_v', 'new_v_b_re': 'new_v', 'new_v_b_im': 'new_v', 'new_v_c_re': 'new_v', 'new_v_c_im': 'new_v', 'new_v_d_skip': 'new_v', 'new_v_w_glu': 'new_v', 'new_v_b_glu': 'new_v', 'new_v_g_attn_out': 'new_v', 'new_v_g_ssm_out': 'new_v', 'new_v_w_out': 'new_v', 'new_v_g_ffn': 'new_v', 'new_v_w_up': 'new_v', 'new_v_conv_w': 'new_v', 'new_v_conv_b': 'new_v', 'new_v_w_down': 'new_v'}


def _forward(args):
    return _fwd_reference(*[args[k] for k in FWD_PARAMS])


def _output_shape():
    out = _jax.eval_shape(lambda: _forward(_fwd_setup_inputs(0)))
    return out.shape, out.dtype

N_MICROBATCH = 1
ADAM_LR = 0.001
ADAM_B1 = 0.9
ADAM_B2 = 0.999
ADAM_EPS = 1e-08
ADAM_WD = 0.01
ADAM_STEP = 10
PER_EXAMPLE_BATCH_AXIS = {'x': 0, 'loss_target': 0}
SHARED_INPUTS = []
_WEIGHT_DTYPES = {'g_mix': _jnp.float32, 'w_in': _jnp.float32, 'b_f': _jnp.float32, 'g_q': _jnp.float32, 'g_k': _jnp.float32, 'lambda_re': _jnp.float32, 'lambda_im': _jnp.float32, 'log_step': _jnp.float32, 'b_re': _jnp.float32, 'b_im': _jnp.float32, 'c_re': _jnp.float32, 'c_im': _jnp.float32, 'd_skip': _jnp.float32, 'w_glu': _jnp.float32, 'b_glu': _jnp.float32, 'g_attn_out': _jnp.float32, 'g_ssm_out': _jnp.float32, 'w_out': _jnp.float32, 'g_ffn': _jnp.float32, 'w_up': _jnp.float32, 'conv_w': _jnp.float32, 'conv_b': _jnp.float32, 'w_down': _jnp.float32}
MOMENT_SCALE = {'g_mix': 5.314287e-01, 'w_in': 3.789597e-01, 'b_f': 2.268201e+00, 'g_q': 7.699165e-01, 'g_k': 7.687246e-01, 'lambda_re': 1.659759e-02, 'lambda_im': 1.844282e-02, 'log_step': 7.947585e+00, 'b_re': 1.196990e-02, 'b_im': 1.199946e-02, 'c_re': 2.269420e-02, 'c_im': 2.527660e-02, 'd_skip': 3.895509e+00, 'w_glu': 5.525169e-01, 'b_glu': 1.633712e+00, 'g_attn_out': 1.610327e+01, 'g_ssm_out': 2.945813e+01, 'w_out': 2.580725e+00, 'g_ffn': 1.417842e+01, 'w_up': 7.079486e-01, 'conv_w': 2.299150e+00, 'conv_b': 2.101681e+00, 'w_down': 3.945840e-01}


def _to_microbatches(a, axis):
    t = _jnp.moveaxis(a, axis, 0)
    t = t.reshape((N_MICROBATCH, t.shape[0] // N_MICROBATCH) + t.shape[1:])
    return _jnp.moveaxis(t, 1, axis + 1)


def setup_inputs(seed: int = 0) -> dict:
    inp = _fwd_setup_inputs(seed)
    key = _jax.random.fold_in(_jax.random.key(seed), 7919)
    shape, _ = _output_shape()
    out = dict(inp)
    out["loss_target"] = _jax.random.normal(_jax.random.fold_in(key, 0), shape, _jnp.float32)
    for i, name in enumerate(TWIN_WEIGHTS):
        w = inp[name].astype(_jnp.float32)
        if MOMENT_SCALE is None:
            s = _jnp.sqrt(_jnp.mean(_jnp.square(w)) + 1e-30)
        else:
            s = MOMENT_SCALE[name]
        km, kv = _jax.random.split(_jax.random.fold_in(key, i + 1))
        out[name] = w
        out["m_" + name] = s * _jax.random.normal(km, w.shape, _jnp.float32)
        out["v_" + name] = (s * s) * _jax.random.uniform(kv, w.shape, _jnp.float32, 0.5, 1.5)
    if N_MICROBATCH > 1:
        for name, axis in PER_EXAMPLE_BATCH_AXIS.items():
            out[name] = _to_microbatches(out[name], axis)
    return {'x': out['x'], 'g_mix': out['g_mix'], 'w_in': out['w_in'], 'b_f': out['b_f'], 'g_q': out['g_q'], 'g_k': out['g_k'], 'lambda_re': out['lambda_re'], 'lambda_im': out['lambda_im'], 'log_step': out['log_step'], 'b_re': out['b_re'], 'b_im': out['b_im'], 'c_re': out['c_re'], 'c_im': out['c_im'], 'd_skip': out['d_skip'], 'w_glu': out['w_glu'], 'b_glu': out['b_glu'], 'g_attn_out': out['g_attn_out'], 'g_ssm_out': out['g_ssm_out'], 'w_out': out['w_out'], 'g_ffn': out['g_ffn'], 'w_up': out['w_up'], 'conv_w': out['conv_w'], 'conv_b': out['conv_b'], 'w_down': out['w_down'], 'loss_target': out['loss_target'], 'm_g_mix': out['m_g_mix'], 'm_w_in': out['m_w_in'], 'm_b_f': out['m_b_f'], 'm_g_q': out['m_g_q'], 'm_g_k': out['m_g_k'], 'm_lambda_re': out['m_lambda_re'], 'm_lambda_im': out['m_lambda_im'], 'm_log_step': out['m_log_step'], 'm_b_re': out['m_b_re'], 'm_b_im': out['m_b_im'], 'm_c_re': out['m_c_re'], 'm_c_im': out['m_c_im'], 'm_d_skip': out['m_d_skip'], 'm_w_glu': out['m_w_glu'], 'm_b_glu': out['m_b_glu'], 'm_g_attn_out': out['m_g_attn_out'], 'm_g_ssm_out': out['m_g_ssm_out'], 'm_w_out': out['m_w_out'], 'm_g_ffn': out['m_g_ffn'], 'm_w_up': out['m_w_up'], 'm_conv_w': out['m_conv_w'], 'm_conv_b': out['m_conv_b'], 'm_w_down': out['m_w_down'], 'v_g_mix': out['v_g_mix'], 'v_w_in': out['v_w_in'], 'v_b_f': out['v_b_f'], 'v_g_q': out['v_g_q'], 'v_g_k': out['v_g_k'], 'v_lambda_re': out['v_lambda_re'], 'v_lambda_im': out['v_lambda_im'], 'v_log_step': out['v_log_step'], 'v_b_re': out['v_b_re'], 'v_b_im': out['v_b_im'], 'v_c_re': out['v_c_re'], 'v_c_im': out['v_c_im'], 'v_d_skip': out['v_d_skip'], 'v_w_glu': out['v_w_glu'], 'v_b_glu': out['v_b_glu'], 'v_g_attn_out': out['v_g_attn_out'], 'v_g_ssm_out': out['v_g_ssm_out'], 'v_w_out': out['v_w_out'], 'v_g_ffn': out['v_g_ffn'], 'v_w_up': out['v_w_up'], 'v_conv_w': out['v_conv_w'], 'v_conv_b': out['v_conv_b'], 'v_w_down': out['v_w_down']}


def _loss(weights, diff, rest, loss_target):
    with _jax.named_scope("forward"):
        args = {**rest, TWIN_DIFF_INPUT: diff, **{k: w.astype(_WEIGHT_DTYPES[k]) for k, w in weights.items()}}
        y = _forward(args)
    with _jax.named_scope("loss_head"):
        err = _jnp.square(y.astype(_jnp.float32) - loss_target)
        return 0.5 * _jnp.sum(_jnp.mean(err, axis=-1)) if err.ndim else 0.5 * err


def _adamw(w, g, m, v):
    m = ADAM_B1 * m + (1.0 - ADAM_B1) * g
    v = ADAM_B2 * v + (1.0 - ADAM_B2) * _jnp.square(g)
    m_hat = m / (1.0 - ADAM_B1 ** ADAM_STEP)
    v_hat = v / (1.0 - ADAM_B2 ** ADAM_STEP)
    delta = -ADAM_LR * (m_hat / (_jnp.sqrt(v_hat) + ADAM_EPS) + ADAM_WD * w)
    return delta, m, v


def reference(x, g_mix, w_in, b_f, g_q, g_k, lambda_re, lambda_im, log_step, b_re, b_im, c_re, c_im, d_skip, w_glu, b_glu, g_attn_out, g_ssm_out, w_out, g_ffn, w_up, conv_w, conv_b, w_down, loss_target, m_g_mix, m_w_in, m_b_f, m_g_q, m_g_k, m_lambda_re, m_lambda_im, m_log_step, m_b_re, m_b_im, m_c_re, m_c_im, m_d_skip, m_w_glu, m_b_glu, m_g_attn_out, m_g_ssm_out, m_w_out, m_g_ffn, m_w_up, m_conv_w, m_conv_b, m_w_down, v_g_mix, v_w_in, v_b_f, v_g_q, v_g_k, v_lambda_re, v_lambda_im, v_log_step, v_b_re, v_b_im, v_c_re, v_c_im, v_d_skip, v_w_glu, v_b_glu, v_g_attn_out, v_g_ssm_out, v_w_out, v_g_ffn, v_w_up, v_conv_w, v_conv_b, v_w_down):
    given = dict(x=x, g_mix=g_mix, w_in=w_in, b_f=b_f, g_q=g_q, g_k=g_k, lambda_re=lambda_re, lambda_im=lambda_im, log_step=log_step, b_re=b_re, b_im=b_im, c_re=c_re, c_im=c_im, d_skip=d_skip, w_glu=w_glu, b_glu=b_glu, g_attn_out=g_attn_out, g_ssm_out=g_ssm_out, w_out=w_out, g_ffn=g_ffn, w_up=w_up, conv_w=conv_w, conv_b=conv_b, w_down=w_down, loss_target=loss_target, m_g_mix=m_g_mix, m_w_in=m_w_in, m_b_f=m_b_f, m_g_q=m_g_q, m_g_k=m_g_k, m_lambda_re=m_lambda_re, m_lambda_im=m_lambda_im, m_log_step=m_log_step, m_b_re=m_b_re, m_b_im=m_b_im, m_c_re=m_c_re, m_c_im=m_c_im, m_d_skip=m_d_skip, m_w_glu=m_w_glu, m_b_glu=m_b_glu, m_g_attn_out=m_g_attn_out, m_g_ssm_out=m_g_ssm_out, m_w_out=m_w_out, m_g_ffn=m_g_ffn, m_w_up=m_w_up, m_conv_w=m_conv_w, m_conv_b=m_conv_b, m_w_down=m_w_down, v_g_mix=v_g_mix, v_w_in=v_w_in, v_b_f=v_b_f, v_g_q=v_g_q, v_g_k=v_g_k, v_lambda_re=v_lambda_re, v_lambda_im=v_lambda_im, v_log_step=v_log_step, v_b_re=v_b_re, v_b_im=v_b_im, v_c_re=v_c_re, v_c_im=v_c_im, v_d_skip=v_d_skip, v_w_glu=v_w_glu, v_b_glu=v_b_glu, v_g_attn_out=v_g_attn_out, v_g_ssm_out=v_g_ssm_out, v_w_out=v_w_out, v_g_ffn=v_g_ffn, v_w_up=v_w_up, v_conv_w=v_conv_w, v_conv_b=v_conv_b, v_w_down=v_w_down)
    weights = {n: given[n] for n in TWIN_WEIGHTS}
    shared = {n: given[n] for n in SHARED_INPUTS}
    per_example = {n: given[n] for n in ['x']}
    grad_fn = _jax.value_and_grad(_loss, argnums=(0, 1))

    def one_microbatch(ex, loss_target):
        ex = dict(ex)
        diff = ex.pop(TWIN_DIFF_INPUT)
        return grad_fn(weights, diff, {**shared, **ex}, loss_target)

    if N_MICROBATCH == 1:
        loss, (grad_w, grad_x) = one_microbatch(per_example, given["loss_target"])
    else:
        def body(carry, xs):
            loss_sum, grad_sum = carry
            l_k, (gw_k, gx_k) = one_microbatch(xs[0], xs[1])
            with _jax.named_scope("update"):
                return (loss_sum + l_k, _jax.tree.map(_jnp.add, grad_sum, gw_k)), gx_k

        init = (_jnp.zeros((), _jnp.float32), _jax.tree.map(_jnp.zeros_like, weights))
        (loss, grad_w), grad_x = _jax.lax.scan(body, init, (per_example, given["loss_target"]))
    with _jax.named_scope("update"):
        delta_w, new_m, new_v = {}, {}, {}
        for n in TWIN_WEIGHTS:
            delta_w[n], new_m[n], new_v[n] = _adamw(weights[n], grad_w[n], given["m_" + n], given["v_" + n])
    return (loss, grad_x, *[grad_w[n] for n in TWIN_WEIGHTS], *[delta_w[n] for n in TWIN_WEIGHTS],
            *[new_m[n] for n in TWIN_WEIGHTS], *[new_v[n] for n in TWIN_WEIGHTS])
```

```python
import math

import jax
import jax.numpy as jnp
from jax import lax
from jax.experimental import pallas as pl
from jax.experimental.pallas import tpu as pltpu

F32 = jnp.float32
BF16 = jnp.bfloat16

D_MODEL = 1024
HEADS = 8
HEAD_DIM = 64
ATTN_W = 512
SSM_W = 512
SSM_GROUPS = 32
SSM_GROUP = 16
SSM_STATE = 64
N_STATE = SSM_GROUPS * SSM_STATE
D_FF = 2816
IN_COLS = 2056
Z_COLS = 2176
U_COL0 = 1536
F_COL0 = 2048
EPS = 1e-6
NEG_INF = -1e30
N_CHIPS = 4
LANES = 128
SUBLANES = 8
SSM_CHUNKS = 4
CHUNK_U = SSM_W // SSM_CHUNKS
CHUNK_S = N_STATE // SSM_CHUNKS
STRIP = 128
N_STRIPS = D_FF // STRIP

ROWS_IN, ROWS_GLU, ROWS_OUT, ROWS_UP, ROWS_DOWN = 514, 64, 256, 1408, 704
OFF_GLU = ROWS_IN
OFF_OUT = OFF_GLU + ROWS_GLU
OFF_UP = OFF_OUT + ROWS_OUT
OFF_DOWN = OFF_UP + ROWS_UP
OFF_SPARE = OFF_DOWN + ROWS_DOWN
PACK_ROWS = 2976
HALF_ROWS = PACK_ROWS // 2
CONVW_ROWS = 9

ADAM_LR = 0.001
ADAM_B1 = 0.9
ADAM_B2 = 0.999
ADAM_EPS = 1e-08
ADAM_WD = 0.01
ADAM_STEP = 10

VMEM_LIMIT = 56 * 1024 * 1024
MESH = pl.DeviceIdType.MESH


def _pallas(body, **kw):
    return pl.pallas_call(body, **kw)


def _pcall(body, *, name, out_shape, in_specs, out_specs, grid=(), scratch_shapes=(), dims=None):
    params = pltpu.CompilerParams(dimension_semantics=dims, vmem_limit_bytes=VMEM_LIMIT)
    return _pallas(body, name=name, grid=grid, in_specs=in_specs, out_specs=out_specs,
                   out_shape=out_shape, scratch_shapes=scratch_shapes, compiler_params=params)


def _sds(shape, dtype=F32):
    return jax.ShapeDtypeStruct(shape, dtype)


def _dot(a, b):
    return jnp.dot(a, b, preferred_element_type=F32)


def _dot_nt(a, b):
    return lax.dot_general(a, b, (((1,), (1,)), ((), ())), preferred_element_type=F32)


def _dot_tn(a, b):
    return lax.dot_general(a, b, (((0,), (0,)), ((), ())), preferred_element_type=F32)


def _split3(x):
    hi = x.astype(BF16)
    r = x - hi.astype(F32)
    mid = r.astype(BF16)
    lo = (r - mid.astype(F32)).astype(BF16)
    return hi, mid, lo


def _dot_exact_r(x, m01):
    hi, mid, lo = _split3(x)
    return _dot(hi, m01) + _dot(mid, m01) + _dot(lo, m01)


def _dot_exact_l(m01, x):
    hi, mid, lo = _split3(x)
    return _dot(m01, hi) + _dot(m01, mid) + _dot(m01, lo)


def _sigmoid(x):
    return 1.0 / (1.0 + jnp.exp(-x))


def _rms(x, g):
    r = lax.rsqrt(jnp.mean(x * x, axis=-1, keepdims=True) + EPS)
    return x * r * g


def _rms_bwd(x, g, dy):
    r = lax.rsqrt(jnp.mean(x * x, axis=-1, keepdims=True) + EPS)
    w = dy * g
    dx = r * w - x * (r * r * r) * jnp.mean(w * x, axis=-1, keepdims=True)
    dg = jnp.sum(dy * x * r, axis=0, keepdims=True)
    return dx, dg


_GELU_K = math.sqrt(2.0 / math.pi)
_GELU_C = 0.044715


def _gelu(y):
    return y * (0.5 * (1.0 + jnp.tanh(_GELU_K * (y + _GELU_C * (y * y * y)))))


def _gelu_grad(y):
    t = jnp.tanh(_GELU_K * (y + _GELU_C * (y * y * y)))
    return 0.5 * (1.0 + t) + 0.5 * y * (1.0 - t * t) * (_GELU_K * (1.0 + 3.0 * _GELU_C * y * y))


def _tile(n, pref):
    if n <= pref:
        return n
    divs = [t for t in range(LANES, n + 1, LANES) if n % t == 0]
    below = [t for t in divs if t <= pref]
    if below and 2 * below[-1] >= pref:
        return below[-1]
    above = [t for t in divs if t > pref]
    return above[0] if above else n


def _row_tile(s):
    return min(256, s)


def _mm(a, b, *, name, ta=False, tb=False, out_dtype=F32, tm=512, tn=1024, tk=512):
    if ta:
        kk, m = a.shape
    else:
        m, kk = a.shape
    n = b.shape[0] if tb else b.shape[1]
    tm, tn, tk = _tile(m, tm), _tile(n, tn), _tile(kk, tk)
    nk = kk // tk

    def body(a_ref, b_ref, o_ref, acc_ref):
        k = pl.program_id(2)

        @pl.when(k == 0)
        def _():
            acc_ref[...] = jnp.zeros_like(acc_ref)

        av = a_ref[...].astype(BF16)
        bv = b_ref[...].astype(BF16)
        if ta:
            acc_ref[...] += _dot_tn(av, bv)
        elif tb:
            acc_ref[...] += _dot_nt(av, bv)
        else:
            acc_ref[...] += _dot(av, bv)

        @pl.when(k == nk - 1)
        def _():
            o_ref[...] = acc_ref[...].astype(o_ref.dtype)

    a_spec = pl.BlockSpec((tk, tm), lambda i, j, k: (k, i)) if ta else pl.BlockSpec((tm, tk), lambda i, j, k: (i, k))
    b_spec = pl.BlockSpec((tn, tk), lambda i, j, k: (j, k)) if tb else pl.BlockSpec((tk, tn), lambda i, j, k: (k, j))
    return _pcall(body, name=name, grid=(m // tm, n // tn, nk), in_specs=[a_spec, b_spec],
                  out_specs=pl.BlockSpec((tm, tn), lambda i, j, k: (i, j)), out_shape=_sds((m, n), out_dtype),
                  scratch_shapes=[pltpu.VMEM((tm, tn), F32)], dims=("parallel", "parallel", "arbitrary"))(a, b)


def _in_proj(x, g_mix, w_in_r):
    s = x.shape[0]
    tm = _row_tile(s)

    def body(x_ref, g_ref, w_ref, h_ref, z_ref):
        h = _rms(x_ref[...], g_ref[...]).astype(BF16)
        h_ref[...] = h
        z_ref[...] = _dot(h, w_ref[...])

    return _pcall(body, name="in_proj", grid=(s // tm,),
                  in_specs=[pl.BlockSpec((tm, D_MODEL), lambda i: (i, 0)), pl.BlockSpec((1, D_MODEL), lambda i: (0, 0)),
                            pl.BlockSpec((D_MODEL, Z_COLS), lambda i: (0, 0))],
                  out_specs=[pl.BlockSpec((tm, D_MODEL), lambda i: (i, 0)), pl.BlockSpec((tm, Z_COLS), lambda i: (i, 0))],
                  out_shape=[_sds((s, D_MODEL), BF16), _sds((s, Z_COLS))], dims=("parallel",))(x, g_mix, w_in_r)


def _attn_prep(z, gq, gk, bf, gg):
    s = z.shape[0]
    tm = _row_tile(s)

    def body(z_ref, gq_ref, gk_ref, bf_ref, gg_ref, qn_ref, kn_ref, vb_ref, ub_ref, c_ref, carry_ref):
        i = pl.program_id(0)

        @pl.when(i == 0)
        def _():
            carry_ref[...] = jnp.zeros_like(carry_ref)

        gg_m = gg_ref[...]

        def head_norm(t, g):
            ssq = _dot_exact_r(t * t, gg_m)
            return t * lax.rsqrt(ssq * (1.0 / HEAD_DIM) + EPS) * g

        qn_ref[...] = head_norm(z_ref[:, 0:ATTN_W], gq_ref[...]).astype(BF16)
        kn_ref[...] = head_norm(z_ref[:, ATTN_W:2 * ATTN_W], gk_ref[...]).astype(BF16)
        vb_ref[...] = z_ref[:, 2 * ATTN_W:3 * ATTN_W].astype(BF16)
        ub_ref[...] = z_ref[:, U_COL0:U_COL0 + SSM_W].astype(BF16)
        fl = z_ref[:, F_COL0:F_COL0 + LANES] + bf_ref[...]
        lf = jnp.minimum(fl, 0.0) - jnp.log1p(jnp.exp(-jnp.abs(fl)))
        row = lax.broadcasted_iota(jnp.int32, (tm, tm), 0)
        col = lax.broadcasted_iota(jnp.int32, (tm, tm), 1)
        tri = (row >= col).astype(BF16)
        c = _dot_exact_l(tri, lf) + carry_ref[...]
        c_ref[...] = c
        carry_ref[...] = c[tm - 1:tm, :]

    row_spec = lambda w: pl.BlockSpec((tm, w), lambda i: (i, 0))
    const = lambda shape: pl.BlockSpec(shape, lambda i: (0, 0))
    return _pcall(body, name="attn_prep", grid=(s // tm,),
                  in_specs=[row_spec(Z_COLS), const((1, ATTN_W)), const((1, ATTN_W)), const((1, LANES)), const((ATTN_W, ATTN_W))],
                  out_specs=[row_spec(ATTN_W)] * 4 + [row_spec(LANES)],
                  out_shape=[_sds((s, ATTN_W), BF16)] * 4 + [_sds((s, LANES))],
                  scratch_shapes=[pltpu.VMEM((1, LANES), F32)], dims=("arbitrary",))(z, gq, gk, bf, gg)


def _attn_fwd(qh, kh, vh, crow):
    _, s, _ = qh.shape
    tq = _row_tile(s)
    scale = HEAD_DIM ** -0.5

    def body(q_ref, k_ref, v_ref, c_ref, o_ref, lse_ref):
        i = pl.program_id(1)
        q = q_ref[0]
        t_idx = i * tq + lax.broadcasted_iota(jnp.int32, (tq, tq), 0)
        s_loc = lax.broadcasted_iota(jnp.int32, (tq, tq), 1)

        def step(j, carry):
            m, l, acc = carry
            off = pl.multiple_of(j * tq, tq)
            ks = k_ref[0, pl.ds(off, tq), :]
            vs = v_ref[0, pl.ds(off, tq), :]
            cs = c_ref[0, :, pl.ds(off, tq)]
            sc = _dot_nt(q, ks) * scale - cs
            sc = jnp.where(s_loc + j * tq <= t_idx, sc, NEG_INF)
            m_new = jnp.maximum(m, jnp.max(sc, axis=-1, keepdims=True))
            p = jnp.exp(sc - m_new)
            alpha = jnp.exp(m - m_new)
            l = alpha * l + jnp.sum(p, axis=-1, keepdims=True)
            acc = alpha * acc + _dot(p.astype(BF16), vs)
            return m_new, l, acc

        init = (jnp.full((tq, 1), NEG_INF, F32), jnp.zeros((tq, 1), F32), jnp.zeros((tq, HEAD_DIM), F32))
        m, l, acc = lax.fori_loop(0, i + 1, step, init)
        o_ref[0] = acc / l
        lse_ref[0] = m + jnp.log(l)

    blk = pl.BlockSpec((1, tq, HEAD_DIM), lambda h, i: (h, i, 0))
    full = pl.BlockSpec((1, s, HEAD_DIM), lambda h, i: (h, 0, 0))
    return _pcall(body, name="attn_fwd", grid=(HEADS, s // tq),
                  in_specs=[blk, full, full, pl.BlockSpec((1, 1, s), lambda h, i: (h, 0, 0))],
                  out_specs=[blk, pl.BlockSpec((1, tq, 1), lambda h, i: (h, i, 0))],
                  out_shape=[_sds((HEADS, s, HEAD_DIM)), _sds((HEADS, s, 1))],
                  dims=("parallel", "parallel"))(qh, kh, vh, crow)


def _ssm_param_fn(lr, li, ls, br, bi):
    step = jnp.exp(ls)
    er = jnp.exp(lr * step)
    ab_re = er * jnp.cos(li * step)
    ab_im = er * jnp.sin(li * step)
    num_re = ab_re - 1.0
    num_im = ab_im
    den = lr * lr + li * li
    f_re = (num_re * lr + num_im * li) / den
    f_im = (num_im * lr - num_re * li) / den
    bb_re = f_re * br - f_im * bi
    bb_im = f_re * bi + f_im * br
    return ab_re, ab_im, bb_re, bb_im


_PARAM_SHAPE = (SSM_GROUPS * SSM_GROUP, SSM_STATE)


def _ssm_params(lr, li, ls, br, bi):
    def body(lr_ref, li_ref, ls_ref, br_ref, bi_ref, ar_ref, ai_ref, bbr_ref, bbi_ref):
        ar, ai, bbr, bbi = _ssm_param_fn(lr_ref[...], li_ref[...], ls_ref[...], br_ref[...], bi_ref[...])
        ar_ref[...] = ar
        ai_ref[...] = ai
        bbr_ref[...] = bbr
        bbi_ref[...] = bbi

    spec = pl.BlockSpec(_PARAM_SHAPE, lambda: (0, 0))
    return _pcall(body, name="ssm_params", in_specs=[spec] * 5, out_specs=[spec] * 4,
                  out_shape=[_sds(_PARAM_SHAPE)] * 4)(lr, li, ls, br, bi)


def _ssm_params_bwd(lr, li, ls, br, bi, dar, dai, dbbr, dbbi, expand_t):
    def body(lr_ref, li_ref, ls_ref, br_ref, bi_ref, dar_ref, dai_ref, dbbr_ref, dbbi_ref, et_ref,
             dlr_ref, dli_ref, dls_ref, dbr_ref, dbi_ref):
        _, vjp = jax.vjp(_ssm_param_fn, lr_ref[...], li_ref[...], ls_ref[...], br_ref[...], bi_ref[...])
        dlr, dli, dls, dbr, dbi = vjp((dar_ref[...], dai_ref[...], dbbr_ref[...], dbbi_ref[...]))
        et = et_ref[...]
        dlr_ref[...] = _dot_exact_l(et, dlr)
        dli_ref[...] = _dot_exact_l(et, dli)
        dls_ref[...] = jnp.sum(_dot_exact_l(et, dls), axis=-1, keepdims=True)
        dbr_ref[...] = dbr
        dbi_ref[...] = dbi

    spec = pl.BlockSpec(_PARAM_SHAPE, lambda: (0, 0))
    gspec = pl.BlockSpec((SSM_GROUPS, SSM_STATE), lambda: (0, 0))
    return _pcall(body, name="ssm_params_bwd",
                  in_specs=[spec] * 9 + [pl.BlockSpec((SSM_GROUPS, _PARAM_SHAPE[0]), lambda: (0, 0))],
                  out_specs=[gspec, gspec, pl.BlockSpec((SSM_GROUPS, 1), lambda: (0, 0)), spec, spec],
                  out_shape=[_sds((SSM_GROUPS, SSM_STATE))] * 2 + [_sds((SSM_GROUPS, 1))] + [_sds(_PARAM_SHAPE)] * 2,
                  )(lr, li, ls, br, bi, dar, dai, dbbr, dbbi, expand_t)


def _cmul(ar, ai, br, bi):
    return ar * br - ai * bi, ar * bi + ai * br


def _scan_consts(ar, ai, width, reverse):
    row = lax.broadcasted_iota(jnp.int32, (SUBLANES, width), 0)
    pw = [(ar, ai)]
    for _ in range(SUBLANES - 1):
        pw.append(_cmul(pw[-1][0], pw[-1][1], ar, ai))
    steps = []
    for d in (1, 2, 4):
        keep = (row < SUBLANES - d) if reverse else (row >= d)
        steps.append((d, jnp.where(keep, pw[d - 1][0], 0.0), jnp.where(keep, pw[d - 1][1], 0.0)))
    pr = jnp.zeros((SUBLANES, width), F32)
    pi = jnp.zeros((SUBLANES, width), F32)
    for r in range(SUBLANES):
        e = (SUBLANES - r) if reverse else (r + 1)
        pr = jnp.where(row == r, pw[e - 1][0], pr)
        pi = jnp.where(row == r, pw[e - 1][1], pi)
    return steps, pr, pi


def _scan_tile(xr, xi, cr, ci, consts, reverse):
    steps, pr, pi = consts
    for d, mr, mi in steps:
        sh = (SUBLANES - d) if reverse else d
        sr = pltpu.roll(xr, sh, 0)
        si = pltpu.roll(xi, sh, 0)
        xr, xi = xr + mr * sr - mi * si, xi + mr * si + mi * sr
    return xr + pr * cr - pi * ci, xi + pr * ci + pi * cr


def _ssm_fwd(ub, z, bbr, bbi, ar, ai, ccr, cci, dsk):
    s = ub.shape[0]
    tm = _row_tile(s)
    nt = tm // SUBLANES

    def body(ub_ref, u_ref, bbr_ref, bbi_ref, ar_ref, ai_ref, ccr_ref, cci_ref, dsk_ref,
             xr_ref, xi_ref, y_ref, cr_s, ci_s):
        i = pl.program_id(1)

        @pl.when(i == 0)
        def _():
            cr_s[...] = jnp.zeros_like(cr_s)
            ci_s[...] = jnp.zeros_like(ci_s)

        u_b = ub_ref[...]
        xr_ref[...] = _dot(u_b, bbr_ref[0])
        xi_ref[...] = _dot(u_b, bbi_ref[0])
        consts = _scan_consts(ar_ref[0], ai_ref[0], CHUNK_S, False)

        def tile(k, carry):
            cr, ci = carry
            sl = pl.ds(pl.multiple_of(k * SUBLANES, SUBLANES), SUBLANES)
            xr, xi = _scan_tile(xr_ref[sl, :], xi_ref[sl, :], cr, ci, consts, False)
            xr_ref[sl, :] = xr
            xi_ref[sl, :] = xi
            return xr[SUBLANES - 1:SUBLANES, :], xi[SUBLANES - 1:SUBLANES, :]

        cr, ci = lax.fori_loop(0, nt, tile, (cr_s[...], ci_s[...]))
        cr_s[...] = cr
        ci_s[...] = ci
        y_ref[...] = (_dot(xr_ref[...].astype(BF16), ccr_ref[0]) - _dot(xi_ref[...].astype(BF16), cci_ref[0])
                      + dsk_ref[...] * u_ref[...])

    ucol0 = U_COL0 // CHUNK_U
    wspec = lambda a, b: pl.BlockSpec((1, a, b), lambda j, i: (j, 0, 0))
    return _pcall(body, name="ssm_fwd", grid=(SSM_CHUNKS, s // tm),
                  in_specs=[pl.BlockSpec((tm, CHUNK_U), lambda j, i: (i, j)),
                            pl.BlockSpec((tm, CHUNK_U), lambda j, i: (i, ucol0 + j)),
                            wspec(CHUNK_U, CHUNK_S), wspec(CHUNK_U, CHUNK_S), wspec(1, CHUNK_S), wspec(1, CHUNK_S),
                            wspec(CHUNK_S, CHUNK_U), wspec(CHUNK_S, CHUNK_U),
                            pl.BlockSpec((1, CHUNK_U), lambda j, i: (0, j))],
                  out_specs=[pl.BlockSpec((tm, CHUNK_S), lambda j, i: (i, j)), pl.BlockSpec((tm, CHUNK_S), lambda j, i: (i, j)),
                             pl.BlockSpec((tm, CHUNK_U), lambda j, i: (i, j))],
                  out_shape=[_sds((s, N_STATE)), _sds((s, N_STATE)), _sds((s, SSM_W))],
                  scratch_shapes=[pltpu.VMEM((1, CHUNK_S), F32)] * 2,
                  dims=("parallel", "arbitrary"))(ub, z, bbr, bbi, ar, ai, ccr, cci, dsk)


def _ssm_glu(y, w_glu, b_glu):
    ge = _gelu(y)
    sg = _sigmoid(_dot(ge.astype(BF16), w_glu) + b_glu)
    return ge, sg


def _mix_out(y, att, x, w_glu, b_glu, g_att, g_ssm, w_out, g_ffn):
    s = x.shape[0]
    tm = _row_tile(s)

    def body(y_ref, att_ref, x_ref, wg_ref, bg_ref, ga_ref, gs_ref, wo_ref, gf_ref, x1_ref, mix_ref, h2_ref):
        ge, sg = _ssm_glu(y_ref[...], wg_ref[...], bg_ref[...])
        ms = _rms(ge * sg, gs_ref[...]).astype(BF16)
        ma = _rms(att_ref[...], ga_ref[...]).astype(BF16)
        mix_ref[:, 0:ATTN_W] = ma
        mix_ref[:, ATTN_W:D_MODEL] = ms
        x1 = x_ref[...] + (_dot(ma, wo_ref[0:ATTN_W, :]) + _dot(ms, wo_ref[ATTN_W:D_MODEL, :]))
        x1_ref[...] = x1
        h2_ref[...] = _rms(x1, gf_ref[...]).astype(BF16)

    row = lambda w: pl.BlockSpec((tm, w), lambda i: (i, 0))
    const = lambda a, b: pl.BlockSpec((a, b), lambda i: (0, 0))
    return _pcall(body, name="mix_out", grid=(s // tm,),
                  in_specs=[row(SSM_W), row(ATTN_W), row(D_MODEL), const(SSM_W, SSM_W), const(1, SSM_W), const(1, ATTN_W),
                            const(1, SSM_W), const(D_MODEL, D_MODEL), const(1, D_MODEL)],
                  out_specs=[row(D_MODEL)] * 3,
                  out_shape=[_sds((s, D_MODEL)), _sds((s, D_MODEL), BF16), _sds((s, D_MODEL), BF16)],
                  dims=("parallel",))(y, att, x, w_glu, b_glu, g_att, g_ssm, w_out, g_ffn)


def _conv_rows(pad_ref, w, b, s):
    y = b + pad_ref[pl.ds(SUBLANES - 2, s), :] * w[0:1, :]
    y = y + pad_ref[pl.ds(SUBLANES - 1, s), :] * w[1:2, :]
    return y + pad_ref[pl.ds(SUBLANES, s), :] * w[2:3, :]


def _conv_act(up, conv_w, conv_b):
    s = up.shape[0]

    def body(ug_ref, uv_ref, wg_ref, wv_ref, bg_ref, bv_ref, act_ref, pg_ref, pv_ref):
        zero = jnp.zeros((SUBLANES, STRIP), F32)
        pg_ref[0:SUBLANES, :] = zero
        pv_ref[0:SUBLANES, :] = zero
        pg_ref[pl.ds(SUBLANES, s), :] = ug_ref[...]
        pv_ref[pl.ds(SUBLANES, s), :] = uv_ref[...]
        hg = _conv_rows(pg_ref, wg_ref[...], bg_ref[...], s)
        hv = _conv_rows(pv_ref, wv_ref[...], bv_ref[...], s)
        act_ref[...] = (hg * _sigmoid(hg) * hv).astype(BF16)

    strip = lambda off: pl.BlockSpec((s, STRIP), lambda j: (0, j + off))
    wsp = lambda off: pl.BlockSpec((3, STRIP), lambda j: (0, j + off))
    bsp = lambda off: pl.BlockSpec((1, STRIP), lambda j: (0, j + off))
    return _pcall(body, name="conv_act", grid=(N_STRIPS,),
                  in_specs=[strip(0), strip(N_STRIPS), wsp(0), wsp(N_STRIPS), bsp(0), bsp(N_STRIPS)],
                  out_specs=pl.BlockSpec((s, STRIP), lambda j: (0, j)), out_shape=_sds((s, D_FF), BF16),
                  scratch_shapes=[pltpu.VMEM((s + SUBLANES, STRIP), F32)] * 2,
                  dims=("parallel",))(up, up, conv_w, conv_w, conv_b, conv_b)


def _down_loss(act, w_down, x1, tgt):
    s = x1.shape[0]
    tm = _row_tile(s)

    def body(a_ref, w_ref, x1_ref, t_ref, dy_ref, dyb_ref, loss_ref):
        i = pl.program_id(0)

        @pl.when(i == 0)
        def _():
            loss_ref[...] = jnp.zeros_like(loss_ref)

        diff = x1_ref[...] + _dot(a_ref[...], w_ref[...]) - t_ref[...]
        dy = diff * (1.0 / D_MODEL)
        dy_ref[...] = dy
        dyb_ref[...] = dy.astype(BF16)
        loss_ref[...] += 0.5 * jnp.sum(diff * dy)

    row = lambda w: pl.BlockSpec((tm, w), lambda i: (i, 0))
    return _pcall(body, name="down_loss", grid=(s // tm,),
                  in_specs=[row(D_FF), pl.BlockSpec((D_FF, D_MODEL), lambda i: (0, 0)), row(D_MODEL), row(D_MODEL)],
                  out_specs=[row(D_MODEL), row(D_MODEL), pl.BlockSpec((SUBLANES, LANES), lambda i: (0, 0))],
                  out_shape=[_sds((s, D_MODEL)), _sds((s, D_MODEL), BF16), _sds((SUBLANES, LANES))],
                  dims=("arbitrary",))(act, w_down, x1, tgt)


def _conv_act_bwd(up, dact, conv_w, conv_b):
    s = up.shape[0]

    def body(ug_ref, uv_ref, uo_ref, da_ref, wg_ref, wv_ref, wo_ref, bg_ref, bv_ref, dup_ref, dcw_ref,
             pg_ref, pv_ref, pd_ref):
        t = pl.program_id(0)
        zero = jnp.zeros((SUBLANES, STRIP), F32)
        pg_ref[0:SUBLANES, :] = zero
        pv_ref[0:SUBLANES, :] = zero
        pg_ref[pl.ds(SUBLANES, s), :] = ug_ref[...]
        pv_ref[pl.ds(SUBLANES, s), :] = uv_ref[...]
        hg = _conv_rows(pg_ref, wg_ref[...], bg_ref[...], s)
        hv = _conv_rows(pv_ref, wv_ref[...], bv_ref[...], s)
        sg = _sigmoid(hg)
        da = da_ref[...]
        d_gate = da * hv * (sg * (1.0 + hg * (1.0 - sg)))
        d_val = da * (hg * sg)
        dh = jnp.where(t == 0, d_gate, d_val)
        pg_ref[pl.ds(SUBLANES, s), :] = uo_ref[...]
        pd_ref[pl.ds(0, s), :] = dh
        pd_ref[pl.ds(s, SUBLANES), :] = zero
        w = wo_ref[...]
        dup = dh * w[2:3, :] + pd_ref[pl.ds(1, s), :] * w[1:2, :] + pd_ref[pl.ds(2, s), :] * w[0:1, :]
        dup_ref[...] = dup.astype(BF16)
        rows = [jnp.sum(dh * pg_ref[pl.ds(SUBLANES - 2 + k, s), :], axis=0, keepdims=True) for k in range(3)]
        rows.append(jnp.sum(dh, axis=0, keepdims=True))
        rid = lax.broadcasted_iota(jnp.int32, (SUBLANES, STRIP), 0)
        out = jnp.zeros((SUBLANES, STRIP), F32)
        for k, r in enumerate(rows):
            out = jnp.where(rid == k, r, out)
        dcw_ref[...] = out

    strip = lambda f: pl.BlockSpec((s, STRIP), f)
    wsp = lambda f: pl.BlockSpec((3, STRIP), f)
    bsp = lambda f: pl.BlockSpec((1, STRIP), f)
    gate = lambda t, j: (0, j)
    val = lambda t, j: (0, j + N_STRIPS)
    own = lambda t, j: (0, t * N_STRIPS + j)
    return _pcall(body, name="conv_act_bwd", grid=(2, N_STRIPS),
                  in_specs=[strip(gate), strip(val), strip(own), strip(gate), wsp(gate), wsp(val), wsp(own), bsp(gate), bsp(val)],
                  out_specs=[strip(own), pl.BlockSpec((SUBLANES, STRIP), own)],
                  out_shape=[_sds((s, 2 * D_FF), BF16), _sds((SUBLANES, 2 * D_FF))],
                  scratch_shapes=[pltpu.VMEM((s + SUBLANES, STRIP), F32)] * 3,
                  dims=("parallel", "parallel"))(up, up, up, dact, conv_w, conv_w, conv_w, conv_b, conv_b)


def _mix_bwd(dy, dh2, x1, g_ffn, w_out, y, att, w_glu, b_glu, g_att, g_ssm):
    s = dy.shape[0]
    tm = _row_tile(s)

    def body(dy_ref, dh2_ref, x1_ref, gf_ref, wo_ref, y_ref, att_ref, wg_ref, bg_ref, ga_ref, gs_ref,
             dx1_ref, dx1b_ref, datt_ref, dys_ref, dwg_ref, dgf_ref, dga_ref, dgs_ref, dbg_ref):
        i = pl.program_id(0)

        @pl.when(i == 0)
        def _():
            for r in (dwg_ref, dgf_ref, dga_ref, dgs_ref, dbg_ref):
                r[...] = jnp.zeros_like(r)

        dxn, dgf = _rms_bwd(x1_ref[...], gf_ref[...], dh2_ref[...])
        dx1 = dy_ref[...] + dxn
        dx1_ref[...] = dx1
        dx1b = dx1.astype(BF16)
        dx1b_ref[...] = dx1b
        dgf_ref[...] += dgf
        dma = _dot_nt(dx1b, wo_ref[0:ATTN_W, :])
        dms = _dot_nt(dx1b, wo_ref[ATTN_W:D_MODEL, :])
        datt, dga = _rms_bwd(att_ref[...], ga_ref[...], dma)
        datt_ref[...] = datt
        dga_ref[...] += dga
        yv = y_ref[...]
        ge, sg = _ssm_glu(yv, wg_ref[...], bg_ref[...])
        dssm, dgs = _rms_bwd(ge * sg, gs_ref[...], dms)
        dgs_ref[...] += dgs
        dgl = dssm * ge * sg * (1.0 - sg)
        dglb = dgl.astype(BF16)
        dge = dssm * sg + _dot_nt(dglb, wg_ref[...])
        dbg_ref[...] += jnp.sum(dgl, axis=0, keepdims=True)
        dwg_ref[...] += _dot_tn(ge.astype(BF16), dglb)
        dys_ref[...] = dge * _gelu_grad(yv)

    row = lambda w: pl.BlockSpec((tm, w), lambda i: (i, 0))
    const = lambda a, b: pl.BlockSpec((a, b), lambda i: (0, 0))
    return _pcall(body, name="mix_bwd", grid=(s // tm,),
                  in_specs=[row(D_MODEL), row(D_MODEL), row(D_MODEL), const(1, D_MODEL), const(D_MODEL, D_MODEL), row(SSM_W),
                            row(ATTN_W), const(SSM_W, SSM_W), const(1, SSM_W), const(1, ATTN_W), const(1, SSM_W)],
                  out_specs=[row(D_MODEL), row(D_MODEL), row(ATTN_W), row(SSM_W), const(SSM_W, SSM_W), const(1, D_MODEL),
                             const(1, ATTN_W), const(1, SSM_W), const(1, SSM_W)],
                  out_shape=[_sds((s, D_MODEL)), _sds((s, D_MODEL), BF16), _sds((s, ATTN_W)), _sds((s, SSM_W)),
                             _sds((SSM_W, SSM_W)), _sds((1, D_MODEL)), _sds((1, ATTN_W)), _sds((1, SSM_W)), _sds((1, SSM_W))],
                  dims=("arbitrary",))(dy, dh2, x1, g_ffn, w_out, y, att, w_glu, b_glu, g_att, g_ssm)


def _ssm_bwd(dys, z, ub, xr, xi, bbr, bbi, ar, ai, ccr, cci, dsk):
    s = dys.shape[0]
    tm = _row_tile(s)
    nb = s // tm
    nt = tm // SUBLANES

    def body(dy_ref, u_ref, ub_ref, xr_ref, xi_ref, xrp_ref, xip_ref, bbr_ref, bbi_ref, ar_ref, ai_ref, ccr_ref,
             cci_ref, dsk_ref, du_ref, dbbr_ref, dbbi_ref, dccr_ref, dcci_ref, dar_ref, dai_ref, dd_ref,
             gr_s, gi_s, cr_s, ci_s, accr_s, acci_s):
        i = pl.program_id(1)
        first_block = i == nb - 1

        @pl.when(i == 0)
        def _():
            for r in (cr_s, ci_s, accr_s, acci_s, dbbr_ref, dbbi_ref, dccr_ref, dcci_ref, dd_ref):
                r[...] = jnp.zeros_like(r)

        dy = dy_ref[...]
        dyb = dy.astype(BF16)
        gr_s[...] = _dot_nt(dyb, ccr_ref[0])
        gi_s[...] = -_dot_nt(dyb, cci_ref[0])
        consts = _scan_consts(ar_ref[0], -ai_ref[0], CHUNK_S, True)
        row = lax.broadcasted_iota(jnp.int32, (SUBLANES, CHUNK_S), 0)

        def tile(kk, carry):
            cr, ci, accr, acci = carry
            k = nt - 1 - kk
            sl = pl.ds(pl.multiple_of(k * SUBLANES, SUBLANES), SUBLANES)
            gr, gi = _scan_tile(gr_s[sl, :], gi_s[sl, :], cr, ci, consts, True)
            gr_s[sl, :] = gr
            gi_s[sl, :] = gi
            slp = pl.ds(pl.multiple_of(jnp.maximum(k - 1, 0) * SUBLANES, SUBLANES), SUBLANES)
            inner = k > 0
            pr_t = jnp.where(inner, xr_ref[slp, :], xrp_ref[...])
            pi_t = jnp.where(inner, xi_ref[slp, :], xip_ref[...])
            live = jnp.logical_or(inner, jnp.logical_not(first_block))
            top_r = jnp.where(live, pltpu.roll(pr_t, 1, 0), 0.0)
            top_i = jnp.where(live, pltpu.roll(pi_t, 1, 0), 0.0)
            xpr = jnp.where(row == 0, top_r, pltpu.roll(xr_ref[sl, :], 1, 0))
            xpi = jnp.where(row == 0, top_i, pltpu.roll(xi_ref[sl, :], 1, 0))
            accr = accr + gr * xpr + gi * xpi
            acci = acci + gi * xpr - gr * xpi
            return gr[0:1, :], gi[0:1, :], accr, acci

        zeros = jnp.zeros((SUBLANES, CHUNK_S), F32)
        cr, ci, accr, acci = lax.fori_loop(0, nt, tile, (cr_s[...], ci_s[...], zeros, zeros))
        cr_s[...] = cr
        ci_s[...] = ci
        accr_s[...] += accr
        acci_s[...] += acci
        grb = gr_s[...].astype(BF16)
        gib = gi_s[...].astype(BF16)
        u_b = ub_ref[...]
        du_ref[...] = _dot_nt(grb, bbr_ref[0]) + _dot_nt(gib, bbi_ref[0]) + dsk_ref[...] * dy
        dbbr_ref[0] += _dot_tn(u_b, grb)
        dbbi_ref[0] += _dot_tn(u_b, gib)
        dccr_ref[0] += _dot_tn(xr_ref[...].astype(BF16), dyb)
        dcci_ref[0] -= _dot_tn(xi_ref[...].astype(BF16), dyb)
        dd_ref[...] += jnp.sum(dy * u_ref[...], axis=0, keepdims=True)

        @pl.when(i == nb - 1)
        def _():
            dar_ref[0] = jnp.sum(accr_s[...], axis=0, keepdims=True)
            dai_ref[0] = jnp.sum(acci_s[...], axis=0, keepdims=True)

    ucol0 = U_COL0 // CHUNK_U
    tiles_per_block = tm // SUBLANES
    rb = lambda i: nb - 1 - i
    wspec = lambda a, b: pl.BlockSpec((1, a, b), lambda j, i: (j, 0, 0))
    xblk = pl.BlockSpec((tm, CHUNK_S), lambda j, i: (rb(i), j))
    xprev = pl.BlockSpec((SUBLANES, CHUNK_S), lambda j, i: (jnp.maximum(rb(i) * tiles_per_block - 1, 0), j))
    ublk = pl.BlockSpec((tm, CHUNK_U), lambda j, i: (rb(i), j))
    return _pcall(body, name="ssm_bwd", grid=(SSM_CHUNKS, nb),
                  in_specs=[ublk, pl.BlockSpec((tm, CHUNK_U), lambda j, i: (rb(i), ucol0 + j)), ublk, xblk, xblk, xprev, xprev,
                            wspec(CHUNK_U, CHUNK_S), wspec(CHUNK_U, CHUNK_S), wspec(1, CHUNK_S), wspec(1, CHUNK_S),
                            wspec(CHUNK_S, CHUNK_U), wspec(CHUNK_S, CHUNK_U), pl.BlockSpec((1, CHUNK_U), lambda j, i: (0, j))],
                  out_specs=[ublk, wspec(CHUNK_U, CHUNK_S), wspec(CHUNK_U, CHUNK_S), wspec(CHUNK_S, CHUNK_U),
                             wspec(CHUNK_S, CHUNK_U), wspec(1, CHUNK_S), wspec(1, CHUNK_S),
                             pl.BlockSpec((1, CHUNK_U), lambda j, i: (0, j))],
                  out_shape=[_sds((s, SSM_W)), _sds((SSM_CHUNKS, CHUNK_U, CHUNK_S)), _sds((SSM_CHUNKS, CHUNK_U, CHUNK_S)),
                             _sds((SSM_CHUNKS, CHUNK_S, CHUNK_U)), _sds((SSM_CHUNKS, CHUNK_S, CHUNK_U)),
                             _sds((SSM_CHUNKS, 1, CHUNK_S)), _sds((SSM_CHUNKS, 1, CHUNK_S)), _sds((1, SSM_W))],
                  scratch_shapes=[pltpu.VMEM((tm, CHUNK_S), F32)] * 2 + [pltpu.VMEM((1, CHUNK_S), F32)] * 2
                                 + [pltpu.VMEM((SUBLANES, CHUNK_S), F32)] * 2,
                  dims=("parallel", "arbitrary"))(dys, z, ub, xr, xi, xr, xi, bbr, bbi, ar, ai, ccr, cci, dsk)


def _attn_dl(qh, kh, vh, crow, lse, doh):
    _, s, _ = qh.shape
    tq = _row_tile(s)
    scale = HEAD_DIM ** -0.5

    def body(q_ref, k_ref, v_ref, c_ref, lse_ref, do_ref, dl_ref):
        i = pl.program_id(1)
        q = q_ref[0]
        dob = do_ref[0].astype(BF16)
        lse_i = lse_ref[0]
        t_idx = i * tq + lax.broadcasted_iota(jnp.int32, (tq, tq), 0)
        s_loc = lax.broadcasted_iota(jnp.int32, (tq, tq), 1)

        def step(j, dl):
            off = pl.multiple_of(j * tq, tq)
            ks = k_ref[0, pl.ds(off, tq), :]
            vs = v_ref[0, pl.ds(off, tq), :]
            sc = _dot_nt(q, ks) * scale - c_ref[0, :, pl.ds(off, tq)]
            p = jnp.where(s_loc + j * tq <= t_idx, jnp.exp(sc - lse_i), 0.0)
            return dl + jnp.sum(p * _dot_nt(dob, vs), axis=-1, keepdims=True)

        dl_ref[0] = lax.fori_loop(0, i + 1, step, jnp.zeros((tq, 1), F32))

    blk = pl.BlockSpec((1, tq, HEAD_DIM), lambda h, i: (h, i, 0))
    full = pl.BlockSpec((1, s, HEAD_DIM), lambda h, i: (h, 0, 0))
    col = pl.BlockSpec((1, tq, 1), lambda h, i: (h, i, 0))
    return _pcall(body, name="attn_dl", grid=(HEADS, s // tq),
                  in_specs=[blk, full, full, pl.BlockSpec((1, 1, s), lambda h, i: (h, 0, 0)), col, blk],
                  out_specs=col, out_shape=_sds((HEADS, s, 1)), dims=("parallel", "parallel"))(qh, kh, vh, crow, lse, doh)


def _attn_bwd(qh, kh, vh, crow, dl, lse, doh):
    _, s, _ = qh.shape
    tq = _row_tile(s)
    nq = s // tq
    scale = HEAD_DIM ** -0.5

    def body(q_ref, k_ref, v_ref, c_ref, dl_ref, lse_ref, do_ref, dq_ref, dk_ref, dv_ref, dc_ref):
        j = pl.program_id(1)

        @pl.when(j == 0)
        def _():
            dq_ref[...] = jnp.zeros_like(dq_ref)

        ks = k_ref[0]
        vs = v_ref[0]
        cs = c_ref[0]
        s_idx = j * tq + lax.broadcasted_iota(jnp.int32, (tq, tq), 1)
        t_loc = lax.broadcasted_iota(jnp.int32, (tq, tq), 0)

        def step(i, carry):
            dk, dv, dc = carry
            rows = pl.ds(pl.multiple_of(i * tq, tq), tq)
            q = q_ref[0, rows, :]
            do = do_ref[0, rows, :]
            dob = do.astype(BF16)
            sc = _dot_nt(q, ks) * scale - cs
            p = jnp.where(s_idx <= t_loc + i * tq, jnp.exp(sc - lse_ref[0, rows, :]), 0.0)
            dp = _dot_nt(dob, vs)
            ds = p * (dp - dl_ref[0, rows, :])
            dsb = ds.astype(BF16)
            dv = dv + _dot_tn(p.astype(BF16), dob)
            dk = dk + _dot_tn(dsb, q) * scale
            dq_ref[0, rows, :] += _dot(dsb, ks) * scale
            dc = dc - jnp.sum(ds, axis=0, keepdims=True)
            return dk, dv, dc

        init = (jnp.zeros((tq, HEAD_DIM), F32), jnp.zeros((tq, HEAD_DIM), F32), jnp.zeros((1, tq), F32))
        dk, dv, dc = lax.fori_loop(j, nq, step, init)
        dk_ref[0] = dk
        dv_ref[0] = dv
        dc_ref[0] = dc

    blk = pl.BlockSpec((1, tq, HEAD_DIM), lambda h, j: (h, j, 0))
    full = pl.BlockSpec((1, s, HEAD_DIM), lambda h, j: (h, 0, 0))
    cblk = pl.BlockSpec((1, 1, tq), lambda h, j: (h, 0, j))
    col = pl.BlockSpec((1, s, 1), lambda h, j: (h, 0, 0))
    return _pcall(body, name="attn_bwd", grid=(HEADS, nq),
                  in_specs=[full, blk, blk, cblk, col, col, full],
                  out_specs=[full, blk, blk, cblk],
                  out_shape=[_sds((HEADS, s, HEAD_DIM))] * 3 + [_sds((HEADS, 1, s))],
                  dims=("parallel", "arbitrary"))(qh, kh, vh, crow, dl, lse, doh)


def _prep_bwd(z, dqn, dkn, dv, du, dc, gq, gk, bf, gg):
    s = z.shape[0]
    tm = _row_tile(s)
    nb = s // tm

    def body(z_ref, dqn_ref, dkn_ref, dv_ref, du_ref, dc_ref, gq_ref, gk_ref, bf_ref, gg_ref,
             dz_ref, dgq_ref, dgk_ref, dbf_ref, carry_ref):
        i = pl.program_id(0)

        @pl.when(i == 0)
        def _():
            for r in (dgq_ref, dgk_ref, dbf_ref, carry_ref):
                r[...] = jnp.zeros_like(r)

        gg_m = gg_ref[...]

        def head_norm_bwd(t, g, dn):
            r = lax.rsqrt(_dot_exact_r(t * t, gg_m) * (1.0 / HEAD_DIM) + EPS)
            w = dn * g
            mean_wt = _dot_exact_r(w * t, gg_m) * (1.0 / HEAD_DIM)
            return r * w - t * (r * r * r) * mean_wt, jnp.sum(dn * t * r, axis=0, keepdims=True)

        dq, dgq = head_norm_bwd(z_ref[:, 0:ATTN_W], gq_ref[...], dqn_ref[...])
        dk, dgk = head_norm_bwd(z_ref[:, ATTN_W:2 * ATTN_W], gk_ref[...], dkn_ref[...])
        dgq_ref[...] += dgq
        dgk_ref[...] += dgk
        row = lax.broadcasted_iota(jnp.int32, (tm, tm), 0)
        col = lax.broadcasted_iota(jnp.int32, (tm, tm), 1)
        triu = (col >= row).astype(BF16)
        dlf = _dot_exact_l(triu, dc_ref[...]) + carry_ref[...]
        carry_ref[...] = dlf[0:1, :]
        fl = z_ref[:, F_COL0:F_COL0 + LANES] + bf_ref[...]
        df = dlf * _sigmoid(-fl)
        dbf_ref[...] += jnp.sum(df, axis=0, keepdims=True)
        dz_ref[:, 0:ATTN_W] = dq.astype(BF16)
        dz_ref[:, ATTN_W:2 * ATTN_W] = dk.astype(BF16)
        dz_ref[:, 2 * ATTN_W:3 * ATTN_W] = dv_ref[...].astype(BF16)
        dz_ref[:, U_COL0:U_COL0 + SSM_W] = du_ref[...].astype(BF16)
        dz_ref[:, F_COL0:F_COL0 + LANES] = df.astype(BF16)

    row_spec = lambda w: pl.BlockSpec((tm, w), lambda i: (nb - 1 - i, 0))
    const = lambda shape: pl.BlockSpec(shape, lambda i: (0, 0))
    return _pcall(body, name="prep_bwd", grid=(nb,),
                  in_specs=[row_spec(Z_COLS)] + [row_spec(ATTN_W)] * 4 + [row_spec(LANES), const((1, ATTN_W)),
                            const((1, ATTN_W)), const((1, LANES)), const((ATTN_W, ATTN_W))],
                  out_specs=[row_spec(Z_COLS), const((1, ATTN_W)), const((1, ATTN_W)), const((1, LANES))],
                  out_shape=[_sds((s, Z_COLS), BF16), _sds((1, ATTN_W)), _sds((1, ATTN_W)), _sds((1, LANES))],
                  scratch_shapes=[pltpu.VMEM((1, LANES), F32)], dims=("arbitrary",))(z, dqn, dkn, dv, du, dc, gq, gk, bf, gg)


def _in_norm_bwd(x, g_mix, dh, dx1):
    s = x.shape[0]
    tm = _row_tile(s)

    def body(x_ref, g_ref, dh_ref, dx1_ref, dx_ref, dg_ref):
        i = pl.program_id(0)

        @pl.when(i == 0)
        def _():
            dg_ref[...] = jnp.zeros_like(dg_ref)

        dxn, dg = _rms_bwd(x_ref[...], g_ref[...], dh_ref[...])
        dx_ref[...] = dx1_ref[...] + dxn
        dg_ref[...] += dg

    row = pl.BlockSpec((tm, D_MODEL), lambda i: (i, 0))
    vec = pl.BlockSpec((1, D_MODEL), lambda i: (0, 0))
    return _pcall(body, name="in_norm_bwd", grid=(s // tm,), in_specs=[row, vec, row, row], out_specs=[row, vec],
                  out_shape=[_sds((s, D_MODEL)), _sds((1, D_MODEL))], dims=("arbitrary",))(x, g_mix, dh, dx1)


def _adamw(w, g, m, v, *, name):
    r, c = w.shape
    tr = r
    for cand in (256, 176, 128, 64):
        if r > cand and r % cand == 0:
            tr = cand
            break

    def body(w_ref, g_ref, m_ref, v_ref, d_ref, mo_ref, vo_ref):
        gv = g_ref[...]
        mn = ADAM_B1 * m_ref[...] + (1.0 - ADAM_B1) * gv
        vn = ADAM_B2 * v_ref[...] + (1.0 - ADAM_B2) * (gv * gv)
        m_hat = mn / (1.0 - ADAM_B1 ** ADAM_STEP)
        v_hat = vn / (1.0 - ADAM_B2 ** ADAM_STEP)
        d_ref[...] = -ADAM_LR * (m_hat / (jnp.sqrt(v_hat) + ADAM_EPS) + ADAM_WD * w_ref[...])
        mo_ref[...] = mn
        vo_ref[...] = vn

    spec = pl.BlockSpec((tr, c), lambda i: (i, 0))
    return _pcall(body, name=name, grid=(r // tr,), in_specs=[spec] * 4, out_specs=[spec] * 3,
                  out_shape=[_sds((r, c))] * 3, dims=("parallel",))(w, g, m, v)


_ADD_ROWS = HALF_ROWS // 6


def _add_sibling(grads, landed, core):
    nblk = HALF_ROWS // _ADD_ROWS

    def body(core_ref, g_ref, l_ref, o_ref):
        o_ref[...] = g_ref[...] + l_ref[...]

    blk = (1, _ADD_ROWS, D_MODEL)
    grid_spec = pltpu.PrefetchScalarGridSpec(
        num_scalar_prefetch=1, grid=(N_CHIPS, nblk),
        in_specs=[pl.BlockSpec(blk, lambda j, i, core_ref: (j, core_ref[0] * nblk + i, 0)),
                  pl.BlockSpec(blk, lambda j, i, core_ref: (j, i, 0))],
        out_specs=pl.BlockSpec(blk, lambda j, i, core_ref: (j, i, 0)))
    return _pallas(body, name="add_sibling", grid_spec=grid_spec,
                          out_shape=_sds((N_CHIPS, HALF_ROWS, D_MODEL)),
                          compiler_params=pltpu.CompilerParams(dimension_semantics=("parallel", "parallel"),
                                                               vmem_limit_bytes=VMEM_LIMIT))(core, grads, landed)


def _sum_chips(parts):
    def body(p_ref, o_ref):
        o_ref[...] = ((p_ref[0] + p_ref[1]) + p_ref[2]) + p_ref[3]

    return _pcall(body, name="sum_chips", grid=(HALF_ROWS // _ADD_ROWS,),
                  in_specs=[pl.BlockSpec((N_CHIPS, _ADD_ROWS, D_MODEL), lambda i: (0, i, 0))],
                  out_specs=pl.BlockSpec((_ADD_ROWS, D_MODEL), lambda i: (i, 0)),
                  out_shape=_sds((HALF_ROWS, D_MODEL)), dims=("parallel",))(parts)


_HBM = pl.BlockSpec(memory_space=pltpu.HBM)


def _place():
    x, y, c = lax.axis_index("x"), lax.axis_index("y"), lax.axis_index("c")
    chips = [(1 - x, y), (x, 1 - y), (1 - x, 1 - y)]
    return x, y, c, chips


def _rcopy(src, dst, send_sem, recv_sem, to):
    return pltpu.make_async_remote_copy(src_ref=src, dst_ref=dst, send_sem=send_sem, recv_sem=recv_sem,
                                        device_id=to, device_id_type=MESH)


def _gather_shards(packed):
    def body(src_ref, out_ref, send_sems, recv_sems, local_sem):
        x, y, c, chips = _place()
        sibling = (x, y, 1 - c)
        me = 2 * x + y

        def half(chip, hc):
            return out_ref.at[chip, pl.ds(hc * HALF_ROWS, HALF_ROWS), :]

        mine = pltpu.make_async_copy(src_ref, out_ref.at[me], local_sem)
        mine.start()
        my_half = src_ref.at[pl.ds(c * HALF_ROWS, HALF_ROWS), :]
        first = [_rcopy(my_half, half(me, c), send_sems.at[k], recv_sems.at[k], (cx, cy, c))
                 for k, (cx, cy) in enumerate(chips)]
        for cp in first:
            cp.start()
        passed = []
        for k, (cx, cy) in enumerate(chips):
            landed = half(2 * cx + cy, c)
            _rcopy(landed, landed, send_sems.at[k], recv_sems.at[k], sibling).wait_recv()
            fwd = _rcopy(landed, landed, send_sems.at[3 + k], recv_sems.at[3 + k], sibling)
            fwd.start()
            passed.append(fwd)
        for k, (cx, cy) in enumerate(chips):
            other = half(2 * cx + cy, 1 - c)
            _rcopy(other, other, send_sems.at[3 + k], recv_sems.at[3 + k], sibling).wait_recv()
        for cp in first + passed:
            cp.wait_send()
        mine.wait()

    return _pallas(body, name="gather_shards", in_specs=[_HBM], out_specs=_HBM,
                          out_shape=_sds((N_CHIPS, PACK_ROWS, D_MODEL), BF16),
                          scratch_shapes=[pltpu.SemaphoreType.DMA((6,)), pltpu.SemaphoreType.DMA((6,)),
                                          pltpu.SemaphoreType.DMA])(packed)


def _swap_halves(grads):
    def body(g_ref, land_ref, send_sem, recv_sem):
        x, y, c, _ = _place()
        theirs = g_ref.at[:, pl.ds((1 - c) * HALF_ROWS, HALF_ROWS), :]
        cp = _rcopy(theirs, land_ref, send_sem, recv_sem, (x, y, 1 - c))
        cp.start()
        cp.wait()

    return _pallas(body, name="swap_halves", in_specs=[_HBM], out_specs=_HBM,
                          out_shape=_sds((N_CHIPS, HALF_ROWS, D_MODEL)),
                          scratch_shapes=[pltpu.SemaphoreType.DMA, pltpu.SemaphoreType.DMA])(grads)


def _scatter_chips(chip_sum):
    def body(p_ref, land_ref, send_sems, recv_sems, local_sem):
        x, y, c, chips = _place()
        me = 2 * x + y
        own = pltpu.make_async_copy(p_ref.at[me], land_ref.at[me], local_sem)
        own.start()
        sends = [_rcopy(p_ref.at[2 * cx + cy], land_ref.at[me], send_sems.at[k], recv_sems.at[k], (cx, cy, c))
                 for k, (cx, cy) in enumerate(chips)]
        for cp in sends:
            cp.start()
        for k, (cx, cy) in enumerate(chips):
            slot = land_ref.at[2 * cx + cy]
            _rcopy(slot, slot, send_sems.at[k], recv_sems.at[k], (cx, cy, c)).wait_recv()
        for cp in sends:
            cp.wait_send()
        own.wait()

    return _pallas(body, name="scatter_chips", in_specs=[_HBM], out_specs=_HBM,
                          out_shape=_sds((N_CHIPS, HALF_ROWS, D_MODEL)),
                          scratch_shapes=[pltpu.SemaphoreType.DMA((3,)), pltpu.SemaphoreType.DMA((3,)),
                                          pltpu.SemaphoreType.DMA])(chip_sum)


def _join_halves(half_sum):
    def body(h_ref, out_ref, send_sem, recv_sem, local_sem):
        x, y, c, _ = _place()
        mine = out_ref.at[pl.ds(c * HALF_ROWS, HALF_ROWS), :]
        theirs = out_ref.at[pl.ds((1 - c) * HALF_ROWS, HALF_ROWS), :]
        loc = pltpu.make_async_copy(h_ref, mine, local_sem)
        loc.start()
        cp = _rcopy(h_ref, mine, send_sem, recv_sem, (x, y, 1 - c))
        cp.start()
        _rcopy(h_ref, theirs, send_sem, recv_sem, (x, y, 1 - c)).wait_recv()
        cp.wait_send()
        loc.wait()

    return _pallas(body, name="join_halves", in_specs=[_HBM], out_specs=_HBM,
                          out_shape=_sds((PACK_ROWS, D_MODEL)),
                          scratch_shapes=[pltpu.SemaphoreType.DMA, pltpu.SemaphoreType.DMA,
                                          pltpu.SemaphoreType.DMA])(half_sum)


def _allreduce_small(v):
    m_per = v.shape[0]

    def body(v_ref, out_ref, all_ref, send_sems, recv_sems, local_sem):
        x, y, c, chips = _place()
        me, sibling = (x, y, c), (x, y, 1 - c)

        def rows(px, py, pc):
            return all_ref.at[pl.ds((4 * px + 2 * py + pc) * m_per, m_per), :]

        def copy(k, block, to, src=None):
            return _rcopy(rows(*block) if src is None else src, rows(*block), send_sems.at[k], recv_sems.at[k], to)

        mine = pltpu.make_async_copy(v_ref, rows(*me), local_sem)
        mine.start()
        first = [copy(0, me, sibling, src=v_ref)]
        first += [copy(1 + k, me, (*chip, c), src=v_ref) for k, chip in enumerate(chips)]
        for cp in first:
            cp.start()
        passed = [copy(4 + k, (*chip, c), sibling) for k, chip in enumerate(chips)]
        for k, chip in enumerate(chips):
            copy(1 + k, (*chip, c), me).wait_recv()
            passed[k].start()
        copy(0, sibling, me).wait_recv()
        for k, chip in enumerate(chips):
            copy(4 + k, (*chip, 1 - c), me).wait_recv()
        for cp in first + passed:
            cp.wait_send()
        mine.wait()
        acc = all_ref[pl.ds(0, m_per), :]
        for d in range(1, 8):
            acc = acc + all_ref[pl.ds(d * m_per, m_per), :]
        out_ref[...] = acc

    vm = pl.BlockSpec(memory_space=pltpu.VMEM)
    return _pallas(body, name="allreduce_small", in_specs=[vm], out_specs=vm, out_shape=_sds((m_per, LANES)),
                          scratch_shapes=[pltpu.VMEM((8 * m_per, LANES), F32), pltpu.SemaphoreType.DMA((7,)),
                                          pltpu.SemaphoreType.DMA((7,)), pltpu.SemaphoreType.DMA],
                          compiler_params=pltpu.CompilerParams(vmem_limit_bytes=VMEM_LIMIT))(v)


def _to_heads(t):
    s = t.shape[0]
    return t.reshape(s, HEADS, HEAD_DIM).transpose(1, 0, 2)


def _from_heads(t):
    s = t.shape[1]
    return t.transpose(1, 0, 2).reshape(s, HEADS * HEAD_DIM)


def _reorder_in_cols(w):
    pad = jnp.zeros((w.shape[0], Z_COLS - IN_COLS), w.dtype)
    return jnp.concatenate([w[:, :3 * ATTN_W], w[:, 3 * ATTN_W + HEADS:], w[:, 3 * ATTN_W:3 * ATTN_W + HEADS], pad], axis=1)


def _restore_in_cols(w):
    return jnp.concatenate([w[:, :3 * ATTN_W], w[:, F_COL0:F_COL0 + HEADS], w[:, U_COL0:U_COL0 + SSM_W]], axis=1)


def _block_diag(blocks):
    j, g, a, b = blocks.shape
    eye = jnp.eye(g, dtype=bool)[None, :, None, :, None]
    return jnp.where(eye, blocks[:, :, :, None, :], jnp.zeros((), blocks.dtype)).reshape(j, g * a, g * b)


def _diag_blocks(m, a, b):
    j = m.shape[0]
    g = m.shape[1] // a
    t = m.reshape(j, g, a, g, b)
    eye = jnp.eye(g, dtype=bool)[None, :, None, :, None]
    return jnp.sum(jnp.where(eye, t, 0.0), axis=3)


def _pack_rows(parts, rows, dtype):
    used = sum(p.shape[0] for p in parts)
    return jnp.concatenate([p.astype(dtype) for p in parts] + [jnp.zeros((rows - used, D_MODEL), dtype)], axis=0)


_SMALL = (("g_mix", (1024,)), ("b_f", (8,)), ("g_q", (64,)), ("g_k", (64,)), ("lambda_re", (32, 64)),
          ("lambda_im", (32, 64)), ("log_step", (32,)), ("b_re", (32, 64, 16)), ("b_im", (32, 64, 16)),
          ("c_re", (32, 16, 64)), ("c_im", (32, 16, 64)), ("d_skip", (32, 16)), ("b_glu", (512,)),
          ("g_attn_out", (512,)), ("g_ssm_out", (512,)), ("g_ffn", (1024,)), ("conv_b", (5632,)))


def _small_rows(shape):
    return -(-math.prod(shape) // LANES)


def _pack_small(arrs, extra=()):
    parts = []
    for a in list(arrs) + list(extra):
        flat = a.reshape(-1)
        rows = -(-flat.shape[0] // LANES)
        parts.append(jnp.pad(flat, (0, rows * LANES - flat.shape[0])).reshape(rows, LANES))
    total = sum(p.shape[0] for p in parts)
    pad = -total % SUBLANES
    if pad:
        parts.append(jnp.zeros((pad, LANES), F32))
    return jnp.concatenate(parts, axis=0)


def _unpack_small(buf, shapes):
    out, r = [], 0
    for shape in shapes:
        n = math.prod(shape)
        rows = -(-n // LANES)
        out.append(buf[r:r + rows].reshape(-1)[:n].reshape(shape))
        r += rows
    return out


def _local_step(x, tgt, w_in_r, w_glu_b, w_out_b, w_up_b, w_down_b, conv_w_full, p):
    s = x.shape[0]
    row = lambda v: v.reshape(1, -1)
    g_mix, g_ffn = row(p["g_mix"]), row(p["g_ffn"])
    g_att, g_ssm, b_glu, conv_b = row(p["g_attn_out"]), row(p["g_ssm_out"]), row(p["b_glu"]), row(p["conv_b"])
    gq = row(jnp.tile(p["g_q"], HEADS))
    gk = row(jnp.tile(p["g_k"], HEADS))
    bf = row(jnp.pad(p["b_f"], (0, LANES - HEADS)))
    gg = jnp.kron(jnp.eye(HEADS, dtype=F32), jnp.ones((HEAD_DIM, HEAD_DIM), F32)).astype(BF16)
    dsk = row(p["d_skip"])

    rep = lambda a: jnp.repeat(a, SSM_GROUP, axis=0)
    lr, li = rep(p["lambda_re"]), rep(p["lambda_im"])
    ls = rep(jnp.broadcast_to(p["log_step"][:, None], (SSM_GROUPS, SSM_STATE)))
    bt_re = p["b_re"].transpose(0, 2, 1).reshape(_PARAM_SHAPE)
    bt_im = p["b_im"].transpose(0, 2, 1).reshape(_PARAM_SHAPE)
    a_re_rep, a_im_rep, bb_re, bb_im = _ssm_params(lr, li, ls, bt_re, bt_im)
    ar = a_re_rep[::SSM_GROUP].reshape(SSM_CHUNKS, 1, CHUNK_S)
    ai = a_im_rep[::SSM_GROUP].reshape(SSM_CHUNKS, 1, CHUNK_S)
    chunked = lambda t: t.reshape(SSM_CHUNKS, SSM_GROUPS // SSM_CHUNKS, SSM_GROUP, SSM_STATE)
    bbr = _block_diag(chunked(bb_re)).astype(BF16)
    bbi = _block_diag(chunked(bb_im)).astype(BF16)
    to_cc = lambda c: _block_diag(chunked(c).transpose(0, 1, 3, 2)).astype(BF16)
    ccr, cci = to_cc(p["c_re"]), to_cc(p["c_im"])

    hb, z = _in_proj(x, g_mix, w_in_r)
    qn, kn, vb, ub, c128 = _attn_prep(z, gq, gk, bf, gg)
    qh, kh, vh = _to_heads(qn), _to_heads(kn), _to_heads(vb)
    crow = c128[:, :HEADS].T.reshape(HEADS, 1, s)
    oh, lse = _attn_fwd(qh, kh, vh, crow)
    att = _from_heads(oh)
    xr, xi, y = _ssm_fwd(ub, z, bbr, bbi, ar, ai, ccr, cci, dsk)
    x1, mixb, h2b = _mix_out(y, att, x, w_glu_b, b_glu, g_att, g_ssm, w_out_b, g_ffn)
    up = _mm(h2b, w_up_b, name="ffn_up")
    act = _conv_act(up, conv_w_full, conv_b)
    dy, dyb, loss_blk = _down_loss(act, w_down_b, x1, tgt)

    d_w_down = _mm(act, dyb, ta=True, name="d_w_down")
    dact = _mm(dyb, w_down_b, tb=True, name="d_act")
    dupb, dcw = _conv_act_bwd(up, dact, conv_w_full, conv_b)
    d_w_up = _mm(h2b, dupb, ta=True, name="d_w_up")
    dh2 = _mm(dupb, w_up_b, tb=True, name="d_h2")
    dx1, dx1b, datt, dys, d_w_glu, d_g_ffn, d_g_att, d_g_ssm, d_b_glu = _mix_bwd(
        dy, dh2, x1, g_ffn, w_out_b, y, att, w_glu_b, b_glu, g_att, g_ssm)
    d_w_out = _mm(mixb, dx1b, ta=True, name="d_w_out")
    du, dbbr, dbbi, dccr, dcci, dar, dai, dd = _ssm_bwd(dys, z, ub, xr, xi, bbr, bbi, ar, ai, ccr, cci, dsk)
    doh = _to_heads(datt)
    dqh, dkh, dvh, dcrow = _attn_bwd(qh, kh, vh, crow, _attn_dl(qh, kh, vh, crow, lse, doh), lse, doh)
    dc128 = jnp.pad(dcrow.reshape(HEADS, s).T, ((0, 0), (0, LANES - HEADS)))
    dzb, d_gq, d_gk, d_bf = _prep_bwd(z, _from_heads(dqh), _from_heads(dkh), _from_heads(dvh), du, dc128, gq, gk, bf, gg)
    d_w_in_r = _mm(hb, dzb, ta=True, name="d_w_in")
    dh = _mm(dzb, w_in_r, tb=True, name="d_h")
    dx, d_g_mix = _in_norm_bwd(x, g_mix, dh, dx1)

    unchunk = lambda t: t.reshape(_PARAM_SHAPE)
    dbb_re = unchunk(_diag_blocks(dbbr, SSM_GROUP, SSM_STATE))
    dbb_im = unchunk(_diag_blocks(dbbi, SSM_GROUP, SSM_STATE))
    first_row = (jnp.arange(_PARAM_SHAPE[0]) % SSM_GROUP == 0)[:, None]
    da_re = jnp.where(first_row, rep(dar.reshape(SSM_GROUPS, SSM_STATE)), 0.0)
    da_im = jnp.where(first_row, rep(dai.reshape(SSM_GROUPS, SSM_STATE)), 0.0)
    expand_t = (jnp.arange(SSM_GROUPS)[:, None] == (jnp.arange(_PARAM_SHAPE[0]) // SSM_GROUP)[None, :]).astype(BF16)
    d_lr, d_li, d_ls, d_bt_re, d_bt_im = _ssm_params_bwd(lr, li, ls, bt_re, bt_im, da_re, da_im, dbb_re, dbb_im, expand_t)
    from_bt = lambda t: t.reshape(SSM_GROUPS, SSM_GROUP, SSM_STATE).transpose(0, 2, 1)
    from_cc = lambda t: _diag_blocks(t, SSM_STATE, SSM_GROUP).transpose(0, 1, 3, 2).reshape(SSM_GROUPS, SSM_GROUP, SSM_STATE)

    small = {
        "g_mix": d_g_mix, "b_f": d_bf[0, :HEADS], "g_q": d_gq.reshape(HEADS, HEAD_DIM).sum(0),
        "g_k": d_gk.reshape(HEADS, HEAD_DIM).sum(0), "lambda_re": d_lr, "lambda_im": d_li, "log_step": d_ls,
        "b_re": from_bt(d_bt_re), "b_im": from_bt(d_bt_im), "c_re": from_cc(dccr), "c_im": from_cc(dcci),
        "d_skip": dd, "b_glu": d_b_glu, "g_attn_out": d_g_att, "g_ssm_out": d_g_ssm, "g_ffn": d_g_ffn,
        "conv_b": dcw[3],
    }
    big = {"w_in": _restore_in_cols(d_w_in_r), "w_glu": d_w_glu, "w_out": d_w_out, "w_up": d_w_up, "w_down": d_w_down}
    return loss_blk[0, 0], dx, big, small, dcw[0:3]


def kernel(x, g_mix, w_in, b_f, g_q, g_k, lambda_re, lambda_im, log_step, b_re, b_im, c_re, c_im, d_skip, w_glu, b_glu, g_attn_out, g_ssm_out, w_out, g_ffn, w_up, conv_w, conv_b, w_down, loss_target, m_g_mix, m_w_in, m_b_f, m_g_q, m_g_k, m_lambda_re, m_lambda_im, m_log_step, m_b_re, m_b_im, m_c_re, m_c_im, m_d_skip, m_w_glu, m_b_glu, m_g_attn_out, m_g_ssm_out, m_w_out, m_g_ffn, m_w_up, m_conv_w, m_conv_b, m_w_down, v_g_mix, v_w_in, v_b_f, v_g_q, v_g_k, v_lambda_re, v_lambda_im, v_log_step, v_b_re, v_b_im, v_c_re, v_c_im, v_d_skip, v_w_glu, v_b_glu, v_g_attn_out, v_g_ssm_out, v_w_out, v_g_ffn, v_w_up, v_conv_w, v_conv_b, v_w_down):
    args = dict(locals())
    order = ["g_mix", "w_in", "b_f", "g_q", "g_k", "lambda_re", "lambda_im", "log_step", "b_re", "b_im", "c_re", "c_im",
             "d_skip", "w_glu", "b_glu", "g_attn_out", "g_ssm_out", "w_out", "g_ffn", "w_up", "conv_w", "conv_b", "w_down"]
    cx, cy, cc = lax.axis_index("x"), lax.axis_index("y"), lax.axis_index("c")
    chip = 2 * cx + cy

    convw_bits = lax.bitcast_convert_type(conv_w, BF16).reshape(-1)
    convw_rows = jnp.pad(convw_bits, (0, CONVW_ROWS * D_MODEL - convw_bits.shape[0])).reshape(CONVW_ROWS, D_MODEL)
    packed = _pack_rows([w_in.reshape(ROWS_IN, D_MODEL), w_glu.reshape(ROWS_GLU, D_MODEL), w_out,
                         w_up.reshape(ROWS_UP, D_MODEL), w_down, convw_rows], PACK_ROWS, BF16)
    wg = _gather_shards(packed)
    cols = lambda t, n: t.reshape(N_CHIPS, D_MODEL, n).transpose(1, 0, 2).reshape(D_MODEL, N_CHIPS * n)
    w_in_r = _reorder_in_cols(cols(wg[:, 0:OFF_GLU], IN_COLS // N_CHIPS))
    w_glu_b = wg[:, OFF_GLU:OFF_OUT].reshape(SSM_W, SSM_W)
    w_out_b = wg[:, OFF_OUT:OFF_UP].reshape(D_MODEL, D_MODEL)
    w_up_b = cols(wg[:, OFF_UP:OFF_DOWN], 2 * D_FF // N_CHIPS)
    w_down_b = wg[:, OFF_DOWN:OFF_SPARE].reshape(D_FF, D_MODEL)
    cw_bits = wg[:, OFF_SPARE:OFF_SPARE + CONVW_ROWS].reshape(N_CHIPS, -1)[:, :3 * 1408 * 2].reshape(N_CHIPS, 3, 1408, 2)
    conv_w_full = lax.bitcast_convert_type(cw_bits, F32).transpose(1, 0, 2).reshape(3, 2 * D_FF)

    loss_part, dx, big, small, d_conv_w = _local_step(x[0], loss_target[0], w_in_r, w_glu_b, w_out_b, w_up_b, w_down_b,
                                                      conv_w_full, args)
    loss = lax.psum(loss_part, ("x", "y", "c"))

    shard_cols = lambda t: t.reshape(D_MODEL, N_CHIPS, -1).transpose(1, 0, 2)
    grads = jnp.concatenate([
        shard_cols(big["w_in"]).reshape(N_CHIPS, ROWS_IN, D_MODEL),
        big["w_glu"].reshape(N_CHIPS, ROWS_GLU, D_MODEL),
        big["w_out"].reshape(N_CHIPS, ROWS_OUT, D_MODEL),
        shard_cols(big["w_up"]).reshape(N_CHIPS, ROWS_UP, D_MODEL),
        big["w_down"].reshape(N_CHIPS, ROWS_DOWN, D_MODEL),
        jnp.zeros((N_CHIPS, PACK_ROWS - OFF_SPARE, D_MODEL), F32)], axis=1)
    landed = _swap_halves(grads)
    chip_sum = _add_sibling(grads, landed, cc.reshape(1).astype(jnp.int32))
    parts = _scatter_chips(chip_sum)
    red = _join_halves(_sum_chips(parts))
    g_big = {"w_in": red[0:OFF_GLU].reshape(D_MODEL, IN_COLS // N_CHIPS),
             "w_glu": red[OFF_GLU:OFF_OUT].reshape(SSM_W // N_CHIPS, SSM_W),
             "w_out": red[OFF_OUT:OFF_UP],
             "w_up": red[OFF_UP:OFF_DOWN].reshape(D_MODEL, 2 * D_FF // N_CHIPS),
             "w_down": red[OFF_DOWN:OFF_SPARE]}

    small_names = [n for n, _ in _SMALL]
    small_shapes = [sh for _, sh in _SMALL]
    gsum = _allreduce_small(_pack_small([small[n] for n in small_names], extra=[d_conv_w]))
    g_small = _unpack_small(gsum, small_shapes + [(3, 2 * D_FF)])
    g_conv_w = lax.dynamic_slice_in_dim(g_small[-1], chip * (2 * D_FF // N_CHIPS), 2 * D_FF // N_CHIPS, axis=1)
    g_small = dict(zip(small_names, g_small[:-1]))

    grad, delta, new_m, new_v = {}, {}, {}, {}
    for n in ("w_in", "w_glu", "w_out", "w_up", "w_down"):
        grad[n] = g_big[n]
        delta[n], new_m[n], new_v[n] = _adamw(args[n], g_big[n], args["m_" + n], args["v_" + n], name="adamw_" + n)
    grad["conv_w"] = g_conv_w
    delta["conv_w"], new_m["conv_w"], new_v["conv_w"] = _adamw(conv_w, g_conv_w, m_conv_w, v_conv_w, name="adamw_conv_w")
    ws = _pack_small([args[n] for n in small_names])
    ms = _pack_small([args["m_" + n] for n in small_names])
    vs = _pack_small([args["v_" + n] for n in small_names])
    gs = _pack_small([g_small[n] for n in small_names])
    ds_, mn_, vn_ = _adamw(ws, gs, ms, vs, name="adamw_small")
    for n, d_, m_, v_ in zip(small_names, _unpack_small(ds_, small_shapes), _unpack_small(mn_, small_shapes),
                             _unpack_small(vn_, small_shapes)):
        grad[n], delta[n], new_m[n], new_v[n] = g_small[n], d_, m_, v_

    return (loss, dx[None], *[grad[n] for n in order], *[delta[n] for n in order], *[new_m[n] for n in order],
            *[new_v[n] for n in order])
```

```python
import math

import jax
import jax.numpy as jnp
from jax import lax
from jax.experimental import pallas as pl
from jax.experimental.pallas import tpu as pltpu

F32 = jnp.float32
BF16 = jnp.bfloat16

D_MODEL = 1024
HEADS = 8
HEAD_DIM = 64
ATTN_W = 512
SSM_W = 512
SSM_GROUPS = 32
SSM_GROUP = 16
SSM_STATE = 64
N_STATE = SSM_GROUPS * SSM_STATE
D_FF = 2816
IN_COLS = 2056
Z_COLS = 2176
U_COL0 = 1536
F_COL0 = 2048
EPS = 1e-6
NEG_INF = -1e30
N_CHIPS = 4
LANES = 128
SUBLANES = 8
SSM_CHUNKS = 4
CHUNK_U = SSM_W // SSM_CHUNKS
CHUNK_S = N_STATE // SSM_CHUNKS
STRIP = 128
N_STRIPS = D_FF // STRIP

ROWS_IN, ROWS_GLU, ROWS_OUT, ROWS_UP, ROWS_DOWN = 514, 64, 256, 1408, 704
OFF_GLU = ROWS_IN
OFF_OUT = OFF_GLU + ROWS_GLU
OFF_UP = OFF_OUT + ROWS_OUT
OFF_DOWN = OFF_UP + ROWS_UP
OFF_SPARE = OFF_DOWN + ROWS_DOWN
PACK_ROWS = 2976
HALF_ROWS = PACK_ROWS // 2
CONVW_ROWS = 9

ADAM_LR = 0.001
ADAM_B1 = 0.9
ADAM_B2 = 0.999
ADAM_EPS = 1e-08
ADAM_WD = 0.01
ADAM_STEP = 10

VMEM_LIMIT = 56 * 1024 * 1024
MESH = pl.DeviceIdType.MESH


def _pallas(body, **kw):
    return pl.pallas_call(body, **kw)


def _pcall(body, *, name, out_shape, in_specs, out_specs, grid=(), scratch_shapes=(), dims=None):
    params = pltpu.CompilerParams(dimension_semantics=dims, vmem_limit_bytes=VMEM_LIMIT)
    return _pallas(body, name=name, grid=grid, in_specs=in_specs, out_specs=out_specs,
                   out_shape=out_shape, scratch_shapes=scratch_shapes, compiler_params=params)


def _sds(shape, dtype=F32):
    return jax.ShapeDtypeStruct(shape, dtype)


def _dot(a, b):
    return jnp.dot(a, b, preferred_element_type=F32)


def _dot_nt(a, b):
    return lax.dot_general(a, b, (((1,), (1,)), ((), ())), preferred_element_type=F32)


def _dot_tn(a, b):
    return lax.dot_general(a, b, (((0,), (0,)), ((), ())), preferred_element_type=F32)


def _split3(x):
    hi = x.astype(BF16)
    r = x - hi.astype(F32)
    mid = r.astype(BF16)
    lo = (r - mid.astype(F32)).astype(BF16)
    return hi, mid, lo


def _dot_exact_r(x, m01):
    hi, mid, lo = _split3(x)
    return _dot(hi, m01) + _dot(mid, m01) + _dot(lo, m01)


def _dot_exact_l(m01, x):
    hi, mid, lo = _split3(x)
    return _dot(m01, hi) + _dot(m01, mid) + _dot(m01, lo)


def _sigmoid(x):
    return 1.0 / (1.0 + jnp.exp(-x))


def _rms(x, g):
    r = lax.rsqrt(jnp.mean(x * x, axis=-1, keepdims=True) + EPS)
    return x * r * g


def _rms_bwd(x, g, dy):
    r = lax.rsqrt(jnp.mean(x * x, axis=-1, keepdims=True) + EPS)
    w = dy * g
    dx = r * w - x * (r * r * r) * jnp.mean(w * x, axis=-1, keepdims=True)
    dg = jnp.sum(dy * x * r, axis=0, keepdims=True)
    return dx, dg


_GELU_K = math.sqrt(2.0 / math.pi)
_GELU_C = 0.044715


def _gelu(y):
    return y * (0.5 * (1.0 + jnp.tanh(_GELU_K * (y + _GELU_C * (y * y * y)))))


def _gelu_grad(y):
    t = jnp.tanh(_GELU_K * (y + _GELU_C * (y * y * y)))
    return 0.5 * (1.0 + t) + 0.5 * y * (1.0 - t * t) * (_GELU_K * (1.0 + 3.0 * _GELU_C * y * y))


def _tile(n, pref):
    if n <= pref:
        return n
    divs = [t for t in range(LANES, n + 1, LANES) if n % t == 0]
    below = [t for t in divs if t <= pref]
    if below and 2 * below[-1] >= pref:
        return below[-1]
    above = [t for t in divs if t > pref]
    return above[0] if above else n


def _row_tile(s):
    return min(256, s)


def _mm(a, b, *, name, tm, tn, tk, ta=False, tb=False):
    if ta:
        kk, m = a.shape
    else:
        m, kk = a.shape
    n = b.shape[0] if tb else b.shape[1]
    tm, tn, tk = _tile(m, tm), _tile(n, tn), _tile(kk, tk)

    def body(a_ref, b_ref, o_ref):
        k = pl.program_id(2)
        if ta:
            part = _dot_tn(a_ref[...], b_ref[...])
        elif tb:
            part = _dot_nt(a_ref[...], b_ref[...])
        else:
            part = _dot(a_ref[...], b_ref[...])

        @pl.when(k == 0)
        def _():
            o_ref[...] = part

        @pl.when(k > 0)
        def _():
            o_ref[...] += part

    a_spec = pl.BlockSpec((tk, tm), lambda i, j, k: (k, i)) if ta else pl.BlockSpec((tm, tk), lambda i, j, k: (i, k))
    b_spec = pl.BlockSpec((tn, tk), lambda i, j, k: (j, k)) if tb else pl.BlockSpec((tk, tn), lambda i, j, k: (k, j))
    return _pcall(body, name=name, grid=(m // tm, n // tn, kk // tk), in_specs=[a_spec, b_spec],
                  out_specs=pl.BlockSpec((tm, tn), lambda i, j, k: (i, j)), out_shape=_sds((m, n)),
                  dims=("parallel", "parallel", "arbitrary"))(a, b)


def _in_proj(x, g_mix, w_in_r):
    s = x.shape[0]
    tm = _row_tile(s)

    def body(x_ref, g_ref, w_ref, h_ref, z_ref):
        h = _rms(x_ref[...], g_ref[...]).astype(BF16)
        h_ref[...] = h
        z_ref[...] = _dot(h, w_ref[...])

    return _pcall(body, name="in_proj", grid=(s // tm,),
                  in_specs=[pl.BlockSpec((tm, D_MODEL), lambda i: (i, 0)), pl.BlockSpec((1, D_MODEL), lambda i: (0, 0)),
                            pl.BlockSpec((D_MODEL, Z_COLS), lambda i: (0, 0))],
                  out_specs=[pl.BlockSpec((tm, D_MODEL), lambda i: (i, 0)), pl.BlockSpec((tm, Z_COLS), lambda i: (i, 0))],
                  out_shape=[_sds((s, D_MODEL), BF16), _sds((s, Z_COLS))], dims=("parallel",))(x, g_mix, w_in_r)


def _attn_prep(z, gq, gk, bf, gg):
    s = z.shape[0]
    tm = _row_tile(s)

    def body(z_ref, gq_ref, gk_ref, bf_ref, gg_ref, qn_ref, kn_ref, vb_ref, ub_ref, c_ref, carry_ref):
        i = pl.program_id(0)

        @pl.when(i == 0)
        def _():
            carry_ref[...] = jnp.zeros_like(carry_ref)

        gg_m = gg_ref[...]

        def head_norm(t, g):
            ssq = _dot_exact_r(t * t, gg_m)
            return t * lax.rsqrt(ssq * (1.0 / HEAD_DIM) + EPS) * g

        qn_ref[...] = head_norm(z_ref[:, 0:ATTN_W], gq_ref[...]).astype(BF16)
        kn_ref[...] = head_norm(z_ref[:, ATTN_W:2 * ATTN_W], gk_ref[...]).astype(BF16)
        vb_ref[...] = z_ref[:, 2 * ATTN_W:3 * ATTN_W].astype(BF16)
        ub_ref[...] = z_ref[:, U_COL0:U_COL0 + SSM_W].astype(BF16)
        fl = z_ref[:, F_COL0:F_COL0 + LANES] + bf_ref[...]
        lf = jnp.minimum(fl, 0.0) - jnp.log1p(jnp.exp(-jnp.abs(fl)))
        row = lax.broadcasted_iota(jnp.int32, (tm, tm), 0)
        col = lax.broadcasted_iota(jnp.int32, (tm, tm), 1)
        tri = (row >= col).astype(BF16)
        c = _dot_exact_l(tri, lf) + carry_ref[...]
        c_ref[...] = c
        carry_ref[...] = c[tm - 1:tm, :]

    row_spec = lambda w: pl.BlockSpec((tm, w), lambda i: (i, 0))
    const = lambda shape: pl.BlockSpec(shape, lambda i: (0, 0))
    return _pcall(body, name="attn_prep", grid=(s // tm,),
                  in_specs=[row_spec(Z_COLS), const((1, ATTN_W)), const((1, ATTN_W)), const((1, LANES)), const((ATTN_W, ATTN_W))],
                  out_specs=[row_spec(ATTN_W)] * 4 + [row_spec(LANES)],
                  out_shape=[_sds((s, ATTN_W), BF16)] * 4 + [_sds((s, LANES))],
                  scratch_shapes=[pltpu.VMEM((1, LANES), F32)], dims=("arbitrary",))(z, gq, gk, bf, gg)


def _attn_fwd(qh, kh, vh, crow):
    _, s, _ = qh.shape
    tq = _row_tile(s)
    scale = HEAD_DIM ** -0.5

    def body(q_ref, k_ref, v_ref, c_ref, o_ref, lse_ref):
        i = pl.program_id(1)
        q = q_ref[0]

        def block(j, carry, diagonal):
            m, l, acc = carry
            off = pl.multiple_of(j * tq, tq)
            ks = k_ref[0, pl.ds(off, tq), :]
            vs = v_ref[0, pl.ds(off, tq), :]
            cs = c_ref[0, :, pl.ds(off, tq)]
            sc = _dot_nt(q, ks) * scale - cs
            if diagonal:
                causal = lax.broadcasted_iota(jnp.int32, (tq, tq), 1) <= lax.broadcasted_iota(jnp.int32, (tq, tq), 0)
                sc = jnp.where(causal, sc, NEG_INF)
            m_new = jnp.maximum(m, jnp.max(sc, axis=-1, keepdims=True))
            p = jnp.exp(sc - m_new)
            alpha = jnp.exp(m - m_new)
            l = alpha * l + jnp.sum(p, axis=-1, keepdims=True)
            acc = alpha * acc + _dot(p.astype(BF16), vs)
            return m_new, l, acc

        init = (jnp.full((tq, 1), NEG_INF, F32), jnp.zeros((tq, 1), F32), jnp.zeros((tq, HEAD_DIM), F32))
        carry = lax.fori_loop(0, i, lambda j, c: block(j, c, False), init)
        m, l, acc = block(i, carry, True)
        o_ref[0] = acc / l
        lse_ref[0] = m + jnp.log(l)

    blk = pl.BlockSpec((1, tq, HEAD_DIM), lambda h, i: (h, i, 0))
    full = pl.BlockSpec((1, s, HEAD_DIM), lambda h, i: (h, 0, 0))
    return _pcall(body, name="attn_fwd", grid=(HEADS, s // tq),
                  in_specs=[blk, full, full, pl.BlockSpec((1, 1, s), lambda h, i: (h, 0, 0))],
                  out_specs=[blk, pl.BlockSpec((1, tq, 1), lambda h, i: (h, i, 0))],
                  out_shape=[_sds((HEADS, s, HEAD_DIM)), _sds((HEADS, s, 1))],
                  dims=("parallel", "parallel"))(qh, kh, vh, crow)


def _ssm_param_fn(lr, li, ls, br, bi):
    step = jnp.exp(ls)
    er = jnp.exp(lr * step)
    ab_re = er * jnp.cos(li * step)
    ab_im = er * jnp.sin(li * step)
    num_re = ab_re - 1.0
    num_im = ab_im
    den = lr * lr + li * li
    f_re = (num_re * lr + num_im * li) / den
    f_im = (num_im * lr - num_re * li) / den
    bb_re = f_re * br - f_im * bi
    bb_im = f_re * bi + f_im * br
    return ab_re, ab_im, bb_re, bb_im


_PARAM_SHAPE = (SSM_GROUPS * SSM_GROUP, SSM_STATE)


def _ssm_params(lr, li, ls, br, bi):
    def body(lr_ref, li_ref, ls_ref, br_ref, bi_ref, ar_ref, ai_ref, bbr_ref, bbi_ref):
        ar, ai, bbr, bbi = _ssm_param_fn(lr_ref[...], li_ref[...], ls_ref[...], br_ref[...], bi_ref[...])
        ar_ref[...] = ar
        ai_ref[...] = ai
        bbr_ref[...] = bbr
        bbi_ref[...] = bbi

    spec = pl.BlockSpec(_PARAM_SHAPE, lambda: (0, 0))
    return _pcall(body, name="ssm_params", in_specs=[spec] * 5, out_specs=[spec] * 4,
                  out_shape=[_sds(_PARAM_SHAPE)] * 4)(lr, li, ls, br, bi)


def _ssm_params_bwd(lr, li, ls, br, bi, dar, dai, dbbr, dbbi, expand_t):
    def body(lr_ref, li_ref, ls_ref, br_ref, bi_ref, dar_ref, dai_ref, dbbr_ref, dbbi_ref, et_ref,
             dlr_ref, dli_ref, dls_ref, dbr_ref, dbi_ref):
        _, vjp = jax.vjp(_ssm_param_fn, lr_ref[...], li_ref[...], ls_ref[...], br_ref[...], bi_ref[...])
        dlr, dli, dls, dbr, dbi = vjp((dar_ref[...], dai_ref[...], dbbr_ref[...], dbbi_ref[...]))
        et = et_ref[...]
        dlr_ref[...] = _dot_exact_l(et, dlr)
        dli_ref[...] = _dot_exact_l(et, dli)
        dls_ref[...] = jnp.sum(_dot_exact_l(et, dls), axis=-1, keepdims=True)
        dbr_ref[...] = dbr
        dbi_ref[...] = dbi

    spec = pl.BlockSpec(_PARAM_SHAPE, lambda: (0, 0))
    gspec = pl.BlockSpec((SSM_GROUPS, SSM_STATE), lambda: (0, 0))
    return _pcall(body, name="ssm_params_bwd",
                  in_specs=[spec] * 9 + [pl.BlockSpec((SSM_GROUPS, _PARAM_SHAPE[0]), lambda: (0, 0))],
                  out_specs=[gspec, gspec, pl.BlockSpec((SSM_GROUPS, 1), lambda: (0, 0)), spec, spec],
                  out_shape=[_sds((SSM_GROUPS, SSM_STATE))] * 2 + [_sds((SSM_GROUPS, 1))] + [_sds(_PARAM_SHAPE)] * 2,
                  )(lr, li, ls, br, bi, dar, dai, dbbr, dbbi, expand_t)


def _cmul(ar, ai, br, bi):
    return ar * br - ai * bi, ar * bi + ai * br


def _scan_consts(ar, ai, width, reverse):
    row = lax.broadcasted_iota(jnp.int32, (SUBLANES, width), 0)
    pw = [(ar, ai)]
    for _ in range(SUBLANES - 1):
        pw.append(_cmul(pw[-1][0], pw[-1][1], ar, ai))
    steps = []
    for d in (1, 2, 4):
        keep = (row < SUBLANES - d) if reverse else (row >= d)
        steps.append((d, jnp.where(keep, pw[d - 1][0], 0.0), jnp.where(keep, pw[d - 1][1], 0.0)))
    pr = jnp.zeros((SUBLANES, width), F32)
    pi = jnp.zeros((SUBLANES, width), F32)
    for r in range(SUBLANES):
        e = (SUBLANES - r) if reverse else (r + 1)
        pr = jnp.where(row == r, pw[e - 1][0], pr)
        pi = jnp.where(row == r, pw[e - 1][1], pi)
    return steps, pr, pi


def _scan_tile(xr, xi, cr, ci, consts, reverse):
    steps, pr, pi = consts
    for d, mr, mi in steps:
        sh = (SUBLANES - d) if reverse else d
        sr = pltpu.roll(xr, sh, 0)
        si = pltpu.roll(xi, sh, 0)
        xr, xi = xr + mr * sr - mi * si, xi + mr * si + mi * sr
    return xr + pr * cr - pi * ci, xi + pr * ci + pi * cr


def _ssm_fwd(ub, z, bbr, bbi, ar, ai, ccr, cci, dsk):
    s = ub.shape[0]
    tm = _row_tile(s)
    nt = tm // SUBLANES

    def body(ub_ref, u_ref, bbr_ref, bbi_ref, ar_ref, ai_ref, ccr_ref, cci_ref, dsk_ref,
             xr_ref, xi_ref, y_ref, cr_s, ci_s):
        i = pl.program_id(1)

        @pl.when(i == 0)
        def _():
            cr_s[...] = jnp.zeros_like(cr_s)
            ci_s[...] = jnp.zeros_like(ci_s)

        u_b = ub_ref[...]
        xr_ref[...] = _dot(u_b, bbr_ref[0])
        xi_ref[...] = _dot(u_b, bbi_ref[0])
        consts = _scan_consts(ar_ref[0], ai_ref[0], CHUNK_S, False)

        def tile(k, carry):
            cr, ci = carry
            sl = pl.ds(pl.multiple_of(k * SUBLANES, SUBLANES), SUBLANES)
            xr, xi = _scan_tile(xr_ref[sl, :], xi_ref[sl, :], cr, ci, consts, False)
            xr_ref[sl, :] = xr
            xi_ref[sl, :] = xi
            return xr[SUBLANES - 1:SUBLANES, :], xi[SUBLANES - 1:SUBLANES, :]

        cr, ci = lax.fori_loop(0, nt, tile, (cr_s[...], ci_s[...]))
        cr_s[...] = cr
        ci_s[...] = ci
        y_ref[...] = (_dot(xr_ref[...].astype(BF16), ccr_ref[0]) - _dot(xi_ref[...].astype(BF16), cci_ref[0])
                      + dsk_ref[...] * u_ref[...])

    ucol0 = U_COL0 // CHUNK_U
    wspec = lambda a, b: pl.BlockSpec((1, a, b), lambda j, i: (j, 0, 0))
    return _pcall(body, name="ssm_fwd", grid=(SSM_CHUNKS, s // tm),
                  in_specs=[pl.BlockSpec((tm, CHUNK_U), lambda j, i: (i, j)),
                            pl.BlockSpec((tm, CHUNK_U), lambda j, i: (i, ucol0 + j)),
                            wspec(CHUNK_U, CHUNK_S), wspec(CHUNK_U, CHUNK_S), wspec(1, CHUNK_S), wspec(1, CHUNK_S),
                            wspec(CHUNK_S, CHUNK_U), wspec(CHUNK_S, CHUNK_U),
                            pl.BlockSpec((1, CHUNK_U), lambda j, i: (0, j))],
                  out_specs=[pl.BlockSpec((tm, CHUNK_S), lambda j, i: (i, j)), pl.BlockSpec((tm, CHUNK_S), lambda j, i: (i, j)),
                             pl.BlockSpec((tm, CHUNK_U), lambda j, i: (i, j))],
                  out_shape=[_sds((s, N_STATE)), _sds((s, N_STATE)), _sds((s, SSM_W))],
                  scratch_shapes=[pltpu.VMEM((1, CHUNK_S), F32)] * 2,
                  dims=("parallel", "arbitrary"))(ub, z, bbr, bbi, ar, ai, ccr, cci, dsk)


def _ssm_glu(y, w_glu, b_glu):
    ge = _gelu(y)
    sg = _sigmoid(_dot(ge.astype(BF16), w_glu) + b_glu)
    return ge, sg


def _mix_out(y, att, x, w_glu, b_glu, g_att, g_ssm, w_out, g_ffn):
    s = x.shape[0]
    tm = _row_tile(s)

    def body(y_ref, att_ref, x_ref, wg_ref, bg_ref, ga_ref, gs_ref, wo_ref, gf_ref, x1_ref, mix_ref, h2_ref):
        ge, sg = _ssm_glu(y_ref[...], wg_ref[...], bg_ref[...])
        ms = _rms(ge * sg, gs_ref[...]).astype(BF16)
        ma = _rms(att_ref[...], ga_ref[...]).astype(BF16)
        mix_ref[:, 0:ATTN_W] = ma
        mix_ref[:, ATTN_W:D_MODEL] = ms
        x1 = x_ref[...] + (_dot(ma, wo_ref[0:ATTN_W, :]) + _dot(ms, wo_ref[ATTN_W:D_MODEL, :]))
        x1_ref[...] = x1
        h2_ref[...] = _rms(x1, gf_ref[...]).astype(BF16)

    row = lambda w: pl.BlockSpec((tm, w), lambda i: (i, 0))
    const = lambda a, b: pl.BlockSpec((a, b), lambda i: (0, 0))
    return _pcall(body, name="mix_out", grid=(s // tm,),
                  in_specs=[row(SSM_W), row(ATTN_W), row(D_MODEL), const(SSM_W, SSM_W), const(1, SSM_W), const(1, ATTN_W),
                            const(1, SSM_W), const(D_MODEL, D_MODEL), const(1, D_MODEL)],
                  out_specs=[row(D_MODEL)] * 3,
                  out_shape=[_sds((s, D_MODEL)), _sds((s, D_MODEL), BF16), _sds((s, D_MODEL), BF16)],
                  dims=("parallel",))(y, att, x, w_glu, b_glu, g_att, g_ssm, w_out, g_ffn)


def _conv_rows(pad_ref, w, b, s):
    y = b + pad_ref[pl.ds(SUBLANES - 2, s), :] * w[0:1, :]
    y = y + pad_ref[pl.ds(SUBLANES - 1, s), :] * w[1:2, :]
    return y + pad_ref[pl.ds(SUBLANES, s), :] * w[2:3, :]


def _conv_act(up, conv_w, conv_b):
    s = up.shape[0]

    def body(ug_ref, uv_ref, wg_ref, wv_ref, bg_ref, bv_ref, act_ref, pg_ref, pv_ref):
        zero = jnp.zeros((SUBLANES, STRIP), F32)
        pg_ref[0:SUBLANES, :] = zero
        pv_ref[0:SUBLANES, :] = zero
        pg_ref[pl.ds(SUBLANES, s), :] = ug_ref[...]
        pv_ref[pl.ds(SUBLANES, s), :] = uv_ref[...]
        hg = _conv_rows(pg_ref, wg_ref[...], bg_ref[...], s)
        hv = _conv_rows(pv_ref, wv_ref[...], bv_ref[...], s)
        act_ref[...] = (hg * _sigmoid(hg) * hv).astype(BF16)

    strip = lambda off: pl.BlockSpec((s, STRIP), lambda j: (0, j + off))
    wsp = lambda off: pl.BlockSpec((3, STRIP), lambda j: (0, j + off))
    bsp = lambda off: pl.BlockSpec((1, STRIP), lambda j: (0, j + off))
    return _pcall(body, name="conv_act", grid=(N_STRIPS,),
                  in_specs=[strip(0), strip(N_STRIPS), wsp(0), wsp(N_STRIPS), bsp(0), bsp(N_STRIPS)],
                  out_specs=pl.BlockSpec((s, STRIP), lambda j: (0, j)), out_shape=_sds((s, D_FF), BF16),
                  scratch_shapes=[pltpu.VMEM((s + SUBLANES, STRIP), F32)] * 2,
                  dims=("parallel",))(up, up, conv_w, conv_w, conv_b, conv_b)


def _down_loss(act, w_down, x1, tgt):
    s = x1.shape[0]
    tm = _row_tile(s)

    def body(a_ref, w_ref, x1_ref, t_ref, dy_ref, dyb_ref, loss_ref):
        i = pl.program_id(0)

        @pl.when(i == 0)
        def _():
            loss_ref[...] = jnp.zeros_like(loss_ref)

        diff = x1_ref[...] + _dot(a_ref[...], w_ref[...]) - t_ref[...]
        dy = diff * (1.0 / D_MODEL)
        dy_ref[...] = dy
        dyb_ref[...] = dy.astype(BF16)
        loss_ref[...] += 0.5 * jnp.sum(diff * dy)

    row = lambda w: pl.BlockSpec((tm, w), lambda i: (i, 0))
    return _pcall(body, name="down_loss", grid=(s // tm,),
                  in_specs=[row(D_FF), pl.BlockSpec((D_FF, D_MODEL), lambda i: (0, 0)), row(D_MODEL), row(D_MODEL)],
                  out_specs=[row(D_MODEL), row(D_MODEL), pl.BlockSpec((SUBLANES, LANES), lambda i: (0, 0))],
                  out_shape=[_sds((s, D_MODEL)), _sds((s, D_MODEL), BF16), _sds((SUBLANES, LANES))],
                  dims=("arbitrary",))(act, w_down, x1, tgt)


def _conv_act_bwd(up, dact, conv_w, conv_b):
    s = up.shape[0]

    def body(ug_ref, uv_ref, uo_ref, da_ref, wg_ref, wv_ref, wo_ref, bg_ref, bv_ref, dup_ref, dcw_ref,
             pg_ref, pv_ref, pd_ref):
        t = pl.program_id(0)
        zero = jnp.zeros((SUBLANES, STRIP), F32)
        pg_ref[0:SUBLANES, :] = zero
        pv_ref[0:SUBLANES, :] = zero
        pg_ref[pl.ds(SUBLANES, s), :] = ug_ref[...]
        pv_ref[pl.ds(SUBLANES, s), :] = uv_ref[...]
        hg = _conv_rows(pg_ref, wg_ref[...], bg_ref[...], s)
        hv = _conv_rows(pv_ref, wv_ref[...], bv_ref[...], s)
        sg = _sigmoid(hg)
        da = da_ref[...]
        d_gate = da * hv * (sg * (1.0 + hg * (1.0 - sg)))
        d_val = da * (hg * sg)
        dh = jnp.where(t == 0, d_gate, d_val)
        pg_ref[pl.ds(SUBLANES, s), :] = uo_ref[...]
        pd_ref[pl.ds(0, s), :] = dh
        pd_ref[pl.ds(s, SUBLANES), :] = zero
        w = wo_ref[...]
        dup = dh * w[2:3, :] + pd_ref[pl.ds(1, s), :] * w[1:2, :] + pd_ref[pl.ds(2, s), :] * w[0:1, :]
        dup_ref[...] = dup.astype(BF16)
        rows = [jnp.sum(dh * pg_ref[pl.ds(SUBLANES - 2 + k, s), :], axis=0, keepdims=True) for k in range(3)]
        rows.append(jnp.sum(dh, axis=0, keepdims=True))
        rid = lax.broadcasted_iota(jnp.int32, (SUBLANES, STRIP), 0)
        out = jnp.zeros((SUBLANES, STRIP), F32)
        for k, r in enumerate(rows):
            out = jnp.where(rid == k, r, out)
        dcw_ref[...] = out

    strip = lambda f: pl.BlockSpec((s, STRIP), f)
    wsp = lambda f: pl.BlockSpec((3, STRIP), f)
    bsp = lambda f: pl.BlockSpec((1, STRIP), f)
    gate = lambda t, j: (0, j)
    val = lambda t, j: (0, j + N_STRIPS)
    own = lambda t, j: (0, t * N_STRIPS + j)
    return _pcall(body, name="conv_act_bwd", grid=(2, N_STRIPS),
                  in_specs=[strip(gate), strip(val), strip(own), strip(gate), wsp(gate), wsp(val), wsp(own), bsp(gate), bsp(val)],
                  out_specs=[strip(own), pl.BlockSpec((SUBLANES, STRIP), own)],
                  out_shape=[_sds((s, 2 * D_FF), BF16), _sds((SUBLANES, 2 * D_FF))],
                  scratch_shapes=[pltpu.VMEM((s + SUBLANES, STRIP), F32)] * 3,
                  dims=("parallel", "parallel"))(up, up, up, dact, conv_w, conv_w, conv_w, conv_b, conv_b)


def _mix_bwd(dy, dh2, x1, g_ffn, w_out, y, att, w_glu, b_glu, g_att, g_ssm):
    s = dy.shape[0]
    tm = _row_tile(s)

    def body(dy_ref, dh2_ref, x1_ref, gf_ref, wo_ref, y_ref, att_ref, wg_ref, bg_ref, ga_ref, gs_ref,
             dx1_ref, dx1b_ref, datt_ref, dys_ref, dwg_ref, dgf_ref, dga_ref, dgs_ref, dbg_ref):
        i = pl.program_id(0)

        @pl.when(i == 0)
        def _():
            for r in (dwg_ref, dgf_ref, dga_ref, dgs_ref, dbg_ref):
                r[...] = jnp.zeros_like(r)

        dxn, dgf = _rms_bwd(x1_ref[...], gf_ref[...], dh2_ref[...])
        dx1 = dy_ref[...] + dxn
        dx1_ref[...] = dx1
        dx1b = dx1.astype(BF16)
        dx1b_ref[...] = dx1b
        dgf_ref[...] += dgf
        dma = _dot_nt(dx1b, wo_ref[0:ATTN_W, :])
        dms = _dot_nt(dx1b, wo_ref[ATTN_W:D_MODEL, :])
        datt, dga = _rms_bwd(att_ref[...], ga_ref[...], dma)
        datt_ref[...] = datt
        dga_ref[...] += dga
        yv = y_ref[...]
        ge, sg = _ssm_glu(yv, wg_ref[...], bg_ref[...])
        dssm, dgs = _rms_bwd(ge * sg, gs_ref[...], dms)
        dgs_ref[...] += dgs
        dgl = dssm * ge * sg * (1.0 - sg)
        dglb = dgl.astype(BF16)
        dge = dssm * sg + _dot_nt(dglb, wg_ref[...])
        dbg_ref[...] += jnp.sum(dgl, axis=0, keepdims=True)
        dwg_ref[...] += _dot_tn(ge.astype(BF16), dglb)
        dys_ref[...] = dge * _gelu_grad(yv)

    row = lambda w: pl.BlockSpec((tm, w), lambda i: (i, 0))
    const = lambda a, b: pl.BlockSpec((a, b), lambda i: (0, 0))
    return _pcall(body, name="mix_bwd", grid=(s // tm,),
                  in_specs=[row(D_MODEL), row(D_MODEL), row(D_MODEL), const(1, D_MODEL), const(D_MODEL, D_MODEL), row(SSM_W),
                            row(ATTN_W), const(SSM_W, SSM_W), const(1, SSM_W), const(1, ATTN_W), const(1, SSM_W)],
                  out_specs=[row(D_MODEL), row(D_MODEL), row(ATTN_W), row(SSM_W), const(SSM_W, SSM_W), const(1, D_MODEL),
                             const(1, ATTN_W), const(1, SSM_W), const(1, SSM_W)],
                  out_shape=[_sds((s, D_MODEL)), _sds((s, D_MODEL), BF16), _sds((s, ATTN_W)), _sds((s, SSM_W)),
                             _sds((SSM_W, SSM_W)), _sds((1, D_MODEL)), _sds((1, ATTN_W)), _sds((1, SSM_W)), _sds((1, SSM_W))],
                  dims=("arbitrary",))(dy, dh2, x1, g_ffn, w_out, y, att, w_glu, b_glu, g_att, g_ssm)


def _ssm_bwd(dys, z, ub, xr, xi, bbr, bbi, ar, ai, ccr, cci, dsk):
    s = dys.shape[0]
    tm = _row_tile(s)
    nb = s // tm
    nt = tm // SUBLANES

    def body(dy_ref, u_ref, ub_ref, xr_ref, xi_ref, xrp_ref, xip_ref, bbr_ref, bbi_ref, ar_ref, ai_ref, ccr_ref,
             cci_ref, dsk_ref, du_ref, dbbr_ref, dbbi_ref, dccr_ref, dcci_ref, dar_ref, dai_ref, dd_ref,
             gr_s, gi_s, cr_s, ci_s, accr_s, acci_s):
        i = pl.program_id(1)
        first_block = i == nb - 1

        @pl.when(i == 0)
        def _():
            for r in (cr_s, ci_s, accr_s, acci_s, dbbr_ref, dbbi_ref, dccr_ref, dcci_ref, dd_ref):
                r[...] = jnp.zeros_like(r)

        dy = dy_ref[...]
        dyb = dy.astype(BF16)
        gr_s[...] = _dot_nt(dyb, ccr_ref[0])
        gi_s[...] = -_dot_nt(dyb, cci_ref[0])
        consts = _scan_consts(ar_ref[0], -ai_ref[0], CHUNK_S, True)
        row = lax.broadcasted_iota(jnp.int32, (SUBLANES, CHUNK_S), 0)

        def tile(kk, carry):
            cr, ci, accr, acci = carry
            k = nt - 1 - kk
            sl = pl.ds(pl.multiple_of(k * SUBLANES, SUBLANES), SUBLANES)
            gr, gi = _scan_tile(gr_s[sl, :], gi_s[sl, :], cr, ci, consts, True)
            gr_s[sl, :] = gr
            gi_s[sl, :] = gi
            slp = pl.ds(pl.multiple_of(jnp.maximum(k - 1, 0) * SUBLANES, SUBLANES), SUBLANES)
            inner = k > 0
            pr_t = jnp.where(inner, xr_ref[slp, :], xrp_ref[...])
            pi_t = jnp.where(inner, xi_ref[slp, :], xip_ref[...])
            live = jnp.logical_or(inner, jnp.logical_not(first_block))
            top_r = jnp.where(live, pltpu.roll(pr_t, 1, 0), 0.0)
            top_i = jnp.where(live, pltpu.roll(pi_t, 1, 0), 0.0)
            xpr = jnp.where(row == 0, top_r, pltpu.roll(xr_ref[sl, :], 1, 0))
            xpi = jnp.where(row == 0, top_i, pltpu.roll(xi_ref[sl, :], 1, 0))
            accr = accr + gr * xpr + gi * xpi
            acci = acci + gi * xpr - gr * xpi
            return gr[0:1, :], gi[0:1, :], accr, acci

        zeros = jnp.zeros((SUBLANES, CHUNK_S), F32)
        cr, ci, accr, acci = lax.fori_loop(0, nt, tile, (cr_s[...], ci_s[...], zeros, zeros))
        cr_s[...] = cr
        ci_s[...] = ci
        accr_s[...] += accr
        acci_s[...] += acci
        grb = gr_s[...].astype(BF16)
        gib = gi_s[...].astype(BF16)
        u_b = ub_ref[...]
        du_ref[...] = _dot_nt(grb, bbr_ref[0]) + _dot_nt(gib, bbi_ref[0]) + dsk_ref[...] * dy
        dbbr_ref[0] += _dot_tn(u_b, grb)
        dbbi_ref[0] += _dot_tn(u_b, gib)
        dccr_ref[0] += _dot_tn(xr_ref[...].astype(BF16), dyb)
        dcci_ref[0] -= _dot_tn(xi_ref[...].astype(BF16), dyb)
        dd_ref[...] += jnp.sum(dy * u_ref[...], axis=0, keepdims=True)

        @pl.when(i == nb - 1)
        def _():
            dar_ref[0] = jnp.sum(accr_s[...], axis=0, keepdims=True)
            dai_ref[0] = jnp.sum(acci_s[...], axis=0, keepdims=True)

    ucol0 = U_COL0 // CHUNK_U
    tiles_per_block = tm // SUBLANES
    rb = lambda i: nb - 1 - i
    wspec = lambda a, b: pl.BlockSpec((1, a, b), lambda j, i: (j, 0, 0))
    xblk = pl.BlockSpec((tm, CHUNK_S), lambda j, i: (rb(i), j))
    xprev = pl.BlockSpec((SUBLANES, CHUNK_S), lambda j, i: (jnp.maximum(rb(i) * tiles_per_block - 1, 0), j))
    ublk = pl.BlockSpec((tm, CHUNK_U), lambda j, i: (rb(i), j))
    return _pcall(body, name="ssm_bwd", grid=(SSM_CHUNKS, nb),
                  in_specs=[ublk, pl.BlockSpec((tm, CHUNK_U), lambda j, i: (rb(i), ucol0 + j)), ublk, xblk, xblk, xprev, xprev,
                            wspec(CHUNK_U, CHUNK_S), wspec(CHUNK_U, CHUNK_S), wspec(1, CHUNK_S), wspec(1, CHUNK_S),
                            wspec(CHUNK_S, CHUNK_U), wspec(CHUNK_S, CHUNK_U), pl.BlockSpec((1, CHUNK_U), lambda j, i: (0, j))],
                  out_specs=[ublk, wspec(CHUNK_U, CHUNK_S), wspec(CHUNK_U, CHUNK_S), wspec(CHUNK_S, CHUNK_U),
                             wspec(CHUNK_S, CHUNK_U), wspec(1, CHUNK_S), wspec(1, CHUNK_S),
                             pl.BlockSpec((1, CHUNK_U), lambda j, i: (0, j))],
                  out_shape=[_sds((s, SSM_W)), _sds((SSM_CHUNKS, CHUNK_U, CHUNK_S)), _sds((SSM_CHUNKS, CHUNK_U, CHUNK_S)),
                             _sds((SSM_CHUNKS, CHUNK_S, CHUNK_U)), _sds((SSM_CHUNKS, CHUNK_S, CHUNK_U)),
                             _sds((SSM_CHUNKS, 1, CHUNK_S)), _sds((SSM_CHUNKS, 1, CHUNK_S)), _sds((1, SSM_W))],
                  scratch_shapes=[pltpu.VMEM((tm, CHUNK_S), F32)] * 2 + [pltpu.VMEM((1, CHUNK_S), F32)] * 2
                                 + [pltpu.VMEM((SUBLANES, CHUNK_S), F32)] * 2,
                  dims=("parallel", "arbitrary"))(dys, z, ub, xr, xi, xr, xi, bbr, bbi, ar, ai, ccr, cci, dsk)


def _attn_probs(q, ks, cs, lse, scale, diagonal):
    p = jnp.exp(_dot_nt(q, ks) * scale - cs - lse)
    if diagonal:
        tq, tk = p.shape
        causal = lax.broadcasted_iota(jnp.int32, (tq, tk), 1) <= lax.broadcasted_iota(jnp.int32, (tq, tk), 0)
        p = jnp.where(causal, p, 0.0)
    return p


def _attn_dl(qh, kh, vh, crow, lse, doh):
    _, s, _ = qh.shape
    tq = _row_tile(s)
    scale = HEAD_DIM ** -0.5

    def body(q_ref, k_ref, v_ref, c_ref, lse_ref, do_ref, dl_ref):
        i = pl.program_id(1)
        q = q_ref[0]
        dob = do_ref[0].astype(BF16)
        lse_i = lse_ref[0]

        def block(j, dl, diagonal):
            off = pl.multiple_of(j * tq, tq)
            ks = k_ref[0, pl.ds(off, tq), :]
            vs = v_ref[0, pl.ds(off, tq), :]
            p = _attn_probs(q, ks, c_ref[0, :, pl.ds(off, tq)], lse_i, scale, diagonal)
            return dl + jnp.sum(p * _dot_nt(dob, vs), axis=-1, keepdims=True)

        dl = lax.fori_loop(0, i, lambda j, c: block(j, c, False), jnp.zeros((tq, 1), F32))
        dl_ref[0] = block(i, dl, True)

    blk = pl.BlockSpec((1, tq, HEAD_DIM), lambda h, i: (h, i, 0))
    full = pl.BlockSpec((1, s, HEAD_DIM), lambda h, i: (h, 0, 0))
    col = pl.BlockSpec((1, tq, 1), lambda h, i: (h, i, 0))
    return _pcall(body, name="attn_dl", grid=(HEADS, s // tq),
                  in_specs=[blk, full, full, pl.BlockSpec((1, 1, s), lambda h, i: (h, 0, 0)), col, blk],
                  out_specs=col, out_shape=_sds((HEADS, s, 1)), dims=("parallel", "parallel"))(qh, kh, vh, crow, lse, doh)


def _attn_bwd(qh, kh, vh, crow, dl, lse, doh):
    _, s, _ = qh.shape
    tq = _row_tile(s)
    nq = s // tq
    scale = HEAD_DIM ** -0.5

    def body(q_ref, k_ref, v_ref, c_ref, dl_ref, lse_ref, do_ref, dq_ref, dk_ref, dv_ref, dc_ref):
        j = pl.program_id(1)

        @pl.when(j == 0)
        def _():
            dq_ref[...] = jnp.zeros_like(dq_ref)

        ks = k_ref[0]
        vs = v_ref[0]
        cs = c_ref[0]

        def block(i, carry, diagonal):
            dk, dv, dc = carry
            rows = pl.ds(pl.multiple_of(i * tq, tq), tq)
            q = q_ref[0, rows, :]
            dob = do_ref[0, rows, :].astype(BF16)
            p = _attn_probs(q, ks, cs, lse_ref[0, rows, :], scale, diagonal)
            dp = _dot_nt(dob, vs)
            ds = p * (dp - dl_ref[0, rows, :])
            dsb = ds.astype(BF16)
            dv = dv + _dot_tn(p.astype(BF16), dob)
            dk = dk + _dot_tn(dsb, q) * scale
            dq_ref[0, rows, :] += _dot(dsb, ks) * scale
            dc = dc - jnp.sum(ds, axis=0, keepdims=True)
            return dk, dv, dc

        init = (jnp.zeros((tq, HEAD_DIM), F32), jnp.zeros((tq, HEAD_DIM), F32), jnp.zeros((1, tq), F32))
        carry = block(j, init, True)
        dk, dv, dc = lax.fori_loop(j + 1, nq, lambda i, c: block(i, c, False), carry)
        dk_ref[0] = dk
        dv_ref[0] = dv
        dc_ref[0] = dc

    blk = pl.BlockSpec((1, tq, HEAD_DIM), lambda h, j: (h, j, 0))
    full = pl.BlockSpec((1, s, HEAD_DIM), lambda h, j: (h, 0, 0))
    cblk = pl.BlockSpec((1, 1, tq), lambda h, j: (h, 0, j))
    col = pl.BlockSpec((1, s, 1), lambda h, j: (h, 0, 0))
    return _pcall(body, name="attn_bwd", grid=(HEADS, nq),
                  in_specs=[full, blk, blk, cblk, col, col, full],
                  out_specs=[full, blk, blk, cblk],
                  out_shape=[_sds((HEADS, s, HEAD_DIM))] * 3 + [_sds((HEADS, 1, s))],
                  dims=("parallel", "arbitrary"))(qh, kh, vh, crow, dl, lse, doh)


def _prep_bwd(z, dqn, dkn, dv, du, dc, gq, gk, bf, gg):
    s = z.shape[0]
    tm = _row_tile(s)
    nb = s // tm

    def body(z_ref, dqn_ref, dkn_ref, dv_ref, du_ref, dc_ref, gq_ref, gk_ref, bf_ref, gg_ref,
             dz_ref, dgq_ref, dgk_ref, dbf_ref, carry_ref):
        i = pl.program_id(0)

        @pl.when(i == 0)
        def _():
            for r in (dgq_ref, dgk_ref, dbf_ref, carry_ref):
                r[...] = jnp.zeros_like(r)

        gg_m = gg_ref[...]

        def head_norm_bwd(t, g, dn):
            r = lax.rsqrt(_dot_exact_r(t * t, gg_m) * (1.0 / HEAD_DIM) + EPS)
            w = dn * g
            mean_wt = _dot_exact_r(w * t, gg_m) * (1.0 / HEAD_DIM)
            return r * w - t * (r * r * r) * mean_wt, jnp.sum(dn * t * r, axis=0, keepdims=True)

        dq, dgq = head_norm_bwd(z_ref[:, 0:ATTN_W], gq_ref[...], dqn_ref[...])
        dk, dgk = head_norm_bwd(z_ref[:, ATTN_W:2 * ATTN_W], gk_ref[...], dkn_ref[...])
        dgq_ref[...] += dgq
        dgk_ref[...] += dgk
        row = lax.broadcasted_iota(jnp.int32, (tm, tm), 0)
        col = lax.broadcasted_iota(jnp.int32, (tm, tm), 1)
        triu = (col >= row).astype(BF16)
        dlf = _dot_exact_l(triu, dc_ref[...]) + carry_ref[...]
        carry_ref[...] = dlf[0:1, :]
        fl = z_ref[:, F_COL0:F_COL0 + LANES] + bf_ref[...]
        df = dlf * _sigmoid(-fl)
        dbf_ref[...] += jnp.sum(df, axis=0, keepdims=True)
        dz_ref[:, 0:ATTN_W] = dq.astype(BF16)
        dz_ref[:, ATTN_W:2 * ATTN_W] = dk.astype(BF16)
        dz_ref[:, 2 * ATTN_W:3 * ATTN_W] = dv_ref[...].astype(BF16)
        dz_ref[:, U_COL0:U_COL0 + SSM_W] = du_ref[...].astype(BF16)
        dz_ref[:, F_COL0:F_COL0 + LANES] = df.astype(BF16)

    row_spec = lambda w: pl.BlockSpec((tm, w), lambda i: (nb - 1 - i, 0))
    const = lambda shape: pl.BlockSpec(shape, lambda i: (0, 0))
    return _pcall(body, name="prep_bwd", grid=(nb,),
                  in_specs=[row_spec(Z_COLS)] + [row_spec(ATTN_W)] * 4 + [row_spec(LANES), const((1, ATTN_W)),
                            const((1, ATTN_W)), const((1, LANES)), const((ATTN_W, ATTN_W))],
                  out_specs=[row_spec(Z_COLS), const((1, ATTN_W)), const((1, ATTN_W)), const((1, LANES))],
                  out_shape=[_sds((s, Z_COLS), BF16), _sds((1, ATTN_W)), _sds((1, ATTN_W)), _sds((1, LANES))],
                  scratch_shapes=[pltpu.VMEM((1, LANES), F32)], dims=("arbitrary",))(z, dqn, dkn, dv, du, dc, gq, gk, bf, gg)


def _in_norm_bwd(x, g_mix, dh, dx1):
    s = x.shape[0]
    tm = _row_tile(s)

    def body(x_ref, g_ref, dh_ref, dx1_ref, dx_ref, dg_ref):
        i = pl.program_id(0)

        @pl.when(i == 0)
        def _():
            dg_ref[...] = jnp.zeros_like(dg_ref)

        dxn, dg = _rms_bwd(x_ref[...], g_ref[...], dh_ref[...])
        dx_ref[...] = dx1_ref[...] + dxn
        dg_ref[...] += dg

    row = pl.BlockSpec((tm, D_MODEL), lambda i: (i, 0))
    vec = pl.BlockSpec((1, D_MODEL), lambda i: (0, 0))
    return _pcall(body, name="in_norm_bwd", grid=(s // tm,), in_specs=[row, vec, row, row], out_specs=[row, vec],
                  out_shape=[_sds((s, D_MODEL)), _sds((1, D_MODEL))], dims=("arbitrary",))(x, g_mix, dh, dx1)


def _adamw(w, g, m, v, *, name):
    r, c = w.shape
    tr = r
    for cand in (256, 176, 128, 64):
        if r > cand and r % cand == 0:
            tr = cand
            break

    def body(w_ref, g_ref, m_ref, v_ref, d_ref, mo_ref, vo_ref):
        gv = g_ref[...]
        mn = ADAM_B1 * m_ref[...] + (1.0 - ADAM_B1) * gv
        vn = ADAM_B2 * v_ref[...] + (1.0 - ADAM_B2) * (gv * gv)
        m_hat = mn / (1.0 - ADAM_B1 ** ADAM_STEP)
        v_hat = vn / (1.0 - ADAM_B2 ** ADAM_STEP)
        d_ref[...] = -ADAM_LR * (m_hat / (jnp.sqrt(v_hat) + ADAM_EPS) + ADAM_WD * w_ref[...])
        mo_ref[...] = mn
        vo_ref[...] = vn

    spec = pl.BlockSpec((tr, c), lambda i: (i, 0))
    return _pcall(body, name=name, grid=(r // tr,), in_specs=[spec] * 4, out_specs=[spec] * 3,
                  out_shape=[_sds((r, c))] * 3, dims=("parallel",))(w, g, m, v)


_ADD_ROWS = HALF_ROWS // 3


def _add_sibling(grads, landed, core):
    nblk = HALF_ROWS // _ADD_ROWS

    def body(core_ref, g_ref, l_ref, o_ref):
        o_ref[...] = (g_ref[...] + l_ref[...]).astype(BF16)

    blk = (1, _ADD_ROWS, D_MODEL)
    grid_spec = pltpu.PrefetchScalarGridSpec(
        num_scalar_prefetch=1, grid=(N_CHIPS, nblk),
        in_specs=[pl.BlockSpec(blk, lambda j, i, core_ref: (j, core_ref[0] * nblk + i, 0)),
                  pl.BlockSpec(blk, lambda j, i, core_ref: (j, i, 0))],
        out_specs=pl.BlockSpec(blk, lambda j, i, core_ref: (j, i, 0)))
    return _pallas(body, name="add_sibling", grid_spec=grid_spec,
                   out_shape=_sds((N_CHIPS, HALF_ROWS, D_MODEL), BF16),
                   compiler_params=pltpu.CompilerParams(dimension_semantics=("parallel", "parallel"),
                                                        vmem_limit_bytes=VMEM_LIMIT))(core, grads, landed)


def _sum_chips(parts):
    def body(p_ref, o_ref):
        o_ref[...] = ((p_ref[0].astype(F32) + p_ref[1].astype(F32)) + p_ref[2].astype(F32)) + p_ref[3].astype(F32)

    return _pcall(body, name="sum_chips", grid=(HALF_ROWS // _ADD_ROWS,),
                  in_specs=[pl.BlockSpec((N_CHIPS, _ADD_ROWS, D_MODEL), lambda i: (0, i, 0))],
                  out_specs=pl.BlockSpec((_ADD_ROWS, D_MODEL), lambda i: (i, 0)),
                  out_shape=_sds((HALF_ROWS, D_MODEL)), dims=("parallel",))(parts)


_HBM = pl.BlockSpec(memory_space=pltpu.HBM)


def _place():
    x, y, c = lax.axis_index("x"), lax.axis_index("y"), lax.axis_index("c")
    chips = [(1 - x, y), (x, 1 - y), (1 - x, 1 - y)]
    return x, y, c, chips


def _rcopy(src, dst, send_sem, recv_sem, to):
    return pltpu.make_async_remote_copy(src_ref=src, dst_ref=dst, send_sem=send_sem, recv_sem=recv_sem,
                                        device_id=to, device_id_type=MESH)


def _gather_shards(packed):
    def body(src_ref, out_ref, send_sems, recv_sems, local_sem):
        x, y, c, chips = _place()
        sibling = (x, y, 1 - c)
        me = 2 * x + y

        def half(chip, hc):
            return out_ref.at[chip, pl.ds(hc * HALF_ROWS, HALF_ROWS), :]

        mine = pltpu.make_async_copy(src_ref, out_ref.at[me], local_sem)
        mine.start()
        my_half = src_ref.at[pl.ds(c * HALF_ROWS, HALF_ROWS), :]
        first = [_rcopy(my_half, half(me, c), send_sems.at[k], recv_sems.at[k], (cx, cy, c))
                 for k, (cx, cy) in enumerate(chips)]
        for cp in first:
            cp.start()
        passed = []
        for k, (cx, cy) in enumerate(chips):
            landed = half(2 * cx + cy, c)
            _rcopy(landed, landed, send_sems.at[k], recv_sems.at[k], sibling).wait_recv()
            fwd = _rcopy(landed, landed, send_sems.at[3 + k], recv_sems.at[3 + k], sibling)
            fwd.start()
            passed.append(fwd)
        for k, (cx, cy) in enumerate(chips):
            other = half(2 * cx + cy, 1 - c)
            _rcopy(other, other, send_sems.at[3 + k], recv_sems.at[3 + k], sibling).wait_recv()
        for cp in first + passed:
            cp.wait_send()
        mine.wait()

    return _pallas(body, name="gather_shards", in_specs=[_HBM], out_specs=_HBM,
                          out_shape=_sds((N_CHIPS, PACK_ROWS, D_MODEL), BF16),
                          scratch_shapes=[pltpu.SemaphoreType.DMA((6,)), pltpu.SemaphoreType.DMA((6,)),
                                          pltpu.SemaphoreType.DMA])(packed)


def _swap_halves(grads):
    def body(g_ref, land_ref, send_sem, recv_sem):
        x, y, c, _ = _place()
        theirs = g_ref.at[:, pl.ds((1 - c) * HALF_ROWS, HALF_ROWS), :]
        cp = _rcopy(theirs, land_ref, send_sem, recv_sem, (x, y, 1 - c))
        cp.start()
        cp.wait()

    return _pallas(body, name="swap_halves", in_specs=[_HBM], out_specs=_HBM,
                          out_shape=_sds((N_CHIPS, HALF_ROWS, D_MODEL)),
                          scratch_shapes=[pltpu.SemaphoreType.DMA, pltpu.SemaphoreType.DMA])(grads)


def _scatter_chips(chip_sum):
    def body(p_ref, land_ref, send_sems, recv_sems, local_sem):
        x, y, c, chips = _place()
        me = 2 * x + y
        own = pltpu.make_async_copy(p_ref.at[me], land_ref.at[me], local_sem)
        own.start()
        sends = [_rcopy(p_ref.at[2 * cx + cy], land_ref.at[me], send_sems.at[k], recv_sems.at[k], (cx, cy, c))
                 for k, (cx, cy) in enumerate(chips)]
        for cp in sends:
            cp.start()
        for k, (cx, cy) in enumerate(chips):
            slot = land_ref.at[2 * cx + cy]
            _rcopy(slot, slot, send_sems.at[k], recv_sems.at[k], (cx, cy, c)).wait_recv()
        for cp in sends:
            cp.wait_send()
        own.wait()

    return _pallas(body, name="scatter_chips", in_specs=[_HBM], out_specs=_HBM,
                          out_shape=_sds((N_CHIPS, HALF_ROWS, D_MODEL), chip_sum.dtype),
                          scratch_shapes=[pltpu.SemaphoreType.DMA((3,)), pltpu.SemaphoreType.DMA((3,)),
                                          pltpu.SemaphoreType.DMA])(chip_sum)


def _join_halves(half_sum):
    def body(h_ref, out_ref, send_sem, recv_sem, local_sem):
        x, y, c, _ = _place()
        mine = out_ref.at[pl.ds(c * HALF_ROWS, HALF_ROWS), :]
        theirs = out_ref.at[pl.ds((1 - c) * HALF_ROWS, HALF_ROWS), :]
        loc = pltpu.make_async_copy(h_ref, mine, local_sem)
        loc.start()
        cp = _rcopy(h_ref, mine, send_sem, recv_sem, (x, y, 1 - c))
        cp.start()
        _rcopy(h_ref, theirs, send_sem, recv_sem, (x, y, 1 - c)).wait_recv()
        cp.wait_send()
        loc.wait()

    return _pallas(body, name="join_halves", in_specs=[_HBM], out_specs=_HBM,
                          out_shape=_sds((PACK_ROWS, D_MODEL)),
                          scratch_shapes=[pltpu.SemaphoreType.DMA, pltpu.SemaphoreType.DMA,
                                          pltpu.SemaphoreType.DMA])(half_sum)


def _allreduce_small(v):
    m_per = v.shape[0]

    def body(v_ref, out_ref, all_ref, send_sems, recv_sems, local_sem):
        x, y, c, chips = _place()
        me, sibling = (x, y, c), (x, y, 1 - c)

        def rows(px, py, pc):
            return all_ref.at[pl.ds((4 * px + 2 * py + pc) * m_per, m_per), :]

        def copy(k, block, to, src=None):
            return _rcopy(rows(*block) if src is None else src, rows(*block), send_sems.at[k], recv_sems.at[k], to)

        mine = pltpu.make_async_copy(v_ref, rows(*me), local_sem)
        mine.start()
        first = [copy(0, me, sibling, src=v_ref)]
        first += [copy(1 + k, me, (*chip, c), src=v_ref) for k, chip in enumerate(chips)]
        for cp in first:
            cp.start()
        passed = [copy(4 + k, (*chip, c), sibling) for k, chip in enumerate(chips)]
        for k, chip in enumerate(chips):
            copy(1 + k, (*chip, c), me).wait_recv()
            passed[k].start()
        copy(0, sibling, me).wait_recv()
        for k, chip in enumerate(chips):
            copy(4 + k, (*chip, 1 - c), me).wait_recv()
        for cp in first + passed:
            cp.wait_send()
        mine.wait()
        acc = all_ref[pl.ds(0, m_per), :]
        for d in range(1, 8):
            acc = acc + all_ref[pl.ds(d * m_per, m_per), :]
        out_ref[...] = acc

    vm = pl.BlockSpec(memory_space=pltpu.VMEM)
    return _pallas(body, name="allreduce_small", in_specs=[vm], out_specs=vm, out_shape=_sds((m_per, LANES)),
                          scratch_shapes=[pltpu.VMEM((8 * m_per, LANES), F32), pltpu.SemaphoreType.DMA((7,)),
                                          pltpu.SemaphoreType.DMA((7,)), pltpu.SemaphoreType.DMA],
                          compiler_params=pltpu.CompilerParams(vmem_limit_bytes=VMEM_LIMIT))(v)


def _to_heads(t):
    s = t.shape[0]
    return t.reshape(s, HEADS, HEAD_DIM).transpose(1, 0, 2)


def _from_heads(t):
    s = t.shape[1]
    return t.transpose(1, 0, 2).reshape(s, HEADS * HEAD_DIM)


def _reorder_in_cols(w):
    pad = jnp.zeros((w.shape[0], Z_COLS - IN_COLS), w.dtype)
    return jnp.concatenate([w[:, :3 * ATTN_W], w[:, 3 * ATTN_W + HEADS:], w[:, 3 * ATTN_W:3 * ATTN_W + HEADS], pad], axis=1)


def _restore_in_cols(w):
    return jnp.concatenate([w[:, :3 * ATTN_W], w[:, F_COL0:F_COL0 + HEADS], w[:, U_COL0:U_COL0 + SSM_W]], axis=1)


def _block_diag(blocks):
    j, g, a, b = blocks.shape
    eye = jnp.eye(g, dtype=bool)[None, :, None, :, None]
    return jnp.where(eye, blocks[:, :, :, None, :], jnp.zeros((), blocks.dtype)).reshape(j, g * a, g * b)


def _diag_blocks(m, a, b):
    j = m.shape[0]
    g = m.shape[1] // a
    t = m.reshape(j, g, a, g, b)
    eye = jnp.eye(g, dtype=bool)[None, :, None, :, None]
    return jnp.sum(jnp.where(eye, t, 0.0), axis=3)


def _pack_rows(parts, rows, dtype):
    used = sum(p.shape[0] for p in parts)
    return jnp.concatenate([p.astype(dtype) for p in parts] + [jnp.zeros((rows - used, D_MODEL), dtype)], axis=0)


_SMALL = (("g_mix", (1024,)), ("b_f", (8,)), ("g_q", (64,)), ("g_k", (64,)), ("lambda_re", (32, 64)),
          ("lambda_im", (32, 64)), ("log_step", (32,)), ("b_re", (32, 64, 16)), ("b_im", (32, 64, 16)),
          ("c_re", (32, 16, 64)), ("c_im", (32, 16, 64)), ("d_skip", (32, 16)), ("b_glu", (512,)),
          ("g_attn_out", (512,)), ("g_ssm_out", (512,)), ("g_ffn", (1024,)), ("conv_b", (5632,)))


def _small_rows(shape):
    return -(-math.prod(shape) // LANES)


def _pack_small(arrs, extra=()):
    parts = []
    for a in list(arrs) + list(extra):
        flat = a.reshape(-1)
        rows = -(-flat.shape[0] // LANES)
        parts.append(jnp.pad(flat, (0, rows * LANES - flat.shape[0])).reshape(rows, LANES))
    total = sum(p.shape[0] for p in parts)
    pad = -total % SUBLANES
    if pad:
        parts.append(jnp.zeros((pad, LANES), F32))
    return jnp.concatenate(parts, axis=0)


def _unpack_small(buf, shapes):
    out, r = [], 0
    for shape in shapes:
        n = math.prod(shape)
        rows = -(-n // LANES)
        out.append(buf[r:r + rows].reshape(-1)[:n].reshape(shape))
        r += rows
    return out


def _local_step(x, tgt, w_in_r, w_glu_b, w_out_b, w_up_b, w_down_b, conv_w_full, p):
    s = x.shape[0]
    row = lambda v: v.reshape(1, -1)
    g_mix, g_ffn = row(p["g_mix"]), row(p["g_ffn"])
    g_att, g_ssm, b_glu, conv_b = row(p["g_attn_out"]), row(p["g_ssm_out"]), row(p["b_glu"]), row(p["conv_b"])
    gq = row(jnp.tile(p["g_q"], HEADS))
    gk = row(jnp.tile(p["g_k"], HEADS))
    bf = row(jnp.pad(p["b_f"], (0, LANES - HEADS)))
    gg = jnp.kron(jnp.eye(HEADS, dtype=F32), jnp.ones((HEAD_DIM, HEAD_DIM), F32)).astype(BF16)
    dsk = row(p["d_skip"])

    rep = lambda a: jnp.repeat(a, SSM_GROUP, axis=0)
    lr, li = rep(p["lambda_re"]), rep(p["lambda_im"])
    ls = rep(jnp.broadcast_to(p["log_step"][:, None], (SSM_GROUPS, SSM_STATE)))
    bt_re = p["b_re"].transpose(0, 2, 1).reshape(_PARAM_SHAPE)
    bt_im = p["b_im"].transpose(0, 2, 1).reshape(_PARAM_SHAPE)
    a_re_rep, a_im_rep, bb_re, bb_im = _ssm_params(lr, li, ls, bt_re, bt_im)
    ar = a_re_rep[::SSM_GROUP].reshape(SSM_CHUNKS, 1, CHUNK_S)
    ai = a_im_rep[::SSM_GROUP].reshape(SSM_CHUNKS, 1, CHUNK_S)
    chunked = lambda t: t.reshape(SSM_CHUNKS, SSM_GROUPS // SSM_CHUNKS, SSM_GROUP, SSM_STATE)
    bbr = _block_diag(chunked(bb_re)).astype(BF16)
    bbi = _block_diag(chunked(bb_im)).astype(BF16)
    to_cc = lambda c: _block_diag(chunked(c).transpose(0, 1, 3, 2)).astype(BF16)
    ccr, cci = to_cc(p["c_re"]), to_cc(p["c_im"])

    hb, z = _in_proj(x, g_mix, w_in_r)
    qn, kn, vb, ub, c128 = _attn_prep(z, gq, gk, bf, gg)
    qh, kh, vh = _to_heads(qn), _to_heads(kn), _to_heads(vb)
    crow = c128[:, :HEADS].T.reshape(HEADS, 1, s)
    oh, lse = _attn_fwd(qh, kh, vh, crow)
    att = _from_heads(oh)
    xr, xi, y = _ssm_fwd(ub, z, bbr, bbi, ar, ai, ccr, cci, dsk)
    x1, mixb, h2b = _mix_out(y, att, x, w_glu_b, b_glu, g_att, g_ssm, w_out_b, g_ffn)
    up = _mm(h2b, w_up_b, name="ffn_up", tm=1024, tn=1408, tk=1024)
    act = _conv_act(up, conv_w_full, conv_b)
    dy, dyb, loss_blk = _down_loss(act, w_down_b, x1, tgt)

    d_w_down = _mm(act, dyb, ta=True, name="d_w_down", tm=1408, tn=1024, tk=2048)
    dact = _mm(dyb, w_down_b, tb=True, name="d_act", tm=1024, tn=1408, tk=1024)
    dupb, dcw = _conv_act_bwd(up, dact, conv_w_full, conv_b)
    d_w_up = _mm(h2b, dupb, ta=True, name="d_w_up", tm=1024, tn=1408, tk=2048)
    dh2 = _mm(dupb, w_up_b, tb=True, name="d_h2", tm=1024, tn=1024, tk=1408)
    dx1, dx1b, datt, dys, d_w_glu, d_g_ffn, d_g_att, d_g_ssm, d_b_glu = _mix_bwd(
        dy, dh2, x1, g_ffn, w_out_b, y, att, w_glu_b, b_glu, g_att, g_ssm)
    d_w_out = _mm(mixb, dx1b, ta=True, name="d_w_out", tm=1024, tn=1024, tk=2048)
    du, dbbr, dbbi, dccr, dcci, dar, dai, dd = _ssm_bwd(dys, z, ub, xr, xi, bbr, bbi, ar, ai, ccr, cci, dsk)
    doh = _to_heads(datt)
    dqh, dkh, dvh, dcrow = _attn_bwd(qh, kh, vh, crow, _attn_dl(qh, kh, vh, crow, lse, doh), lse, doh)
    dc128 = jnp.pad(dcrow.reshape(HEADS, s).T, ((0, 0), (0, LANES - HEADS)))
    dzb, d_gq, d_gk, d_bf = _prep_bwd(z, _from_heads(dqh), _from_heads(dkh), _from_heads(dvh), du, dc128, gq, gk, bf, gg)
    d_w_in_r = _mm(hb, dzb, ta=True, name="d_w_in", tm=512, tn=Z_COLS, tk=2048)
    dh = _mm(dzb, w_in_r, tb=True, name="d_h", tm=1024, tn=1024, tk=Z_COLS)
    dx, d_g_mix = _in_norm_bwd(x, g_mix, dh, dx1)

    unchunk = lambda t: t.reshape(_PARAM_SHAPE)
    dbb_re = unchunk(_diag_blocks(dbbr, SSM_GROUP, SSM_STATE))
    dbb_im = unchunk(_diag_blocks(dbbi, SSM_GROUP, SSM_STATE))
    first_row = (jnp.arange(_PARAM_SHAPE[0]) % SSM_GROUP == 0)[:, None]
    da_re = jnp.where(first_row, rep(dar.reshape(SSM_GROUPS, SSM_STATE)), 0.0)
    da_im = jnp.where(first_row, rep(dai.reshape(SSM_GROUPS, SSM_STATE)), 0.0)
    expand_t = (jnp.arange(SSM_GROUPS)[:, None] == (jnp.arange(_PARAM_SHAPE[0]) // SSM_GROUP)[None, :]).astype(BF16)
    d_lr, d_li, d_ls, d_bt_re, d_bt_im = _ssm_params_bwd(lr, li, ls, bt_re, bt_im, da_re, da_im, dbb_re, dbb_im, expand_t)
    from_bt = lambda t: t.reshape(SSM_GROUPS, SSM_GROUP, SSM_STATE).transpose(0, 2, 1)
    from_cc = lambda t: _diag_blocks(t, SSM_STATE, SSM_GROUP).transpose(0, 1, 3, 2).reshape(SSM_GROUPS, SSM_GROUP, SSM_STATE)

    small = {
        "g_mix": d_g_mix, "b_f": d_bf[0, :HEADS], "g_q": d_gq.reshape(HEADS, HEAD_DIM).sum(0),
        "g_k": d_gk.reshape(HEADS, HEAD_DIM).sum(0), "lambda_re": d_lr, "lambda_im": d_li, "log_step": d_ls,
        "b_re": from_bt(d_bt_re), "b_im": from_bt(d_bt_im), "c_re": from_cc(dccr), "c_im": from_cc(dcci),
        "d_skip": dd, "b_glu": d_b_glu, "g_attn_out": d_g_att, "g_ssm_out": d_g_ssm, "g_ffn": d_g_ffn,
        "conv_b": dcw[3],
    }
    big = {"w_in": _restore_in_cols(d_w_in_r), "w_glu": d_w_glu, "w_out": d_w_out, "w_up": d_w_up, "w_down": d_w_down}
    return loss_blk[0, 0], dx, big, small, dcw[0:3]


def kernel(x, g_mix, w_in, b_f, g_q, g_k, lambda_re, lambda_im, log_step, b_re, b_im, c_re, c_im, d_skip, w_glu, b_glu, g_attn_out, g_ssm_out, w_out, g_ffn, w_up, conv_w, conv_b, w_down, loss_target, m_g_mix, m_w_in, m_b_f, m_g_q, m_g_k, m_lambda_re, m_lambda_im, m_log_step, m_b_re, m_b_im, m_c_re, m_c_im, m_d_skip, m_w_glu, m_b_glu, m_g_attn_out, m_g_ssm_out, m_w_out, m_g_ffn, m_w_up, m_conv_w, m_conv_b, m_w_down, v_g_mix, v_w_in, v_b_f, v_g_q, v_g_k, v_lambda_re, v_lambda_im, v_log_step, v_b_re, v_b_im, v_c_re, v_c_im, v_d_skip, v_w_glu, v_b_glu, v_g_attn_out, v_g_ssm_out, v_w_out, v_g_ffn, v_w_up, v_conv_w, v_conv_b, v_w_down):
    args = dict(locals())
    order = ["g_mix", "w_in", "b_f", "g_q", "g_k", "lambda_re", "lambda_im", "log_step", "b_re", "b_im", "c_re", "c_im",
             "d_skip", "w_glu", "b_glu", "g_attn_out", "g_ssm_out", "w_out", "g_ffn", "w_up", "conv_w", "conv_b", "w_down"]
    cx, cy, cc = lax.axis_index("x"), lax.axis_index("y"), lax.axis_index("c")
    chip = 2 * cx + cy

    convw_bits = lax.bitcast_convert_type(conv_w, BF16).reshape(-1)
    convw_rows = jnp.pad(convw_bits, (0, CONVW_ROWS * D_MODEL - convw_bits.shape[0])).reshape(CONVW_ROWS, D_MODEL)
    packed = _pack_rows([w_in.reshape(ROWS_IN, D_MODEL), w_glu.reshape(ROWS_GLU, D_MODEL), w_out,
                         w_up.reshape(ROWS_UP, D_MODEL), w_down, convw_rows], PACK_ROWS, BF16)
    wg = _gather_shards(packed)
    cols = lambda t, n: t.reshape(N_CHIPS, D_MODEL, n).transpose(1, 0, 2).reshape(D_MODEL, N_CHIPS * n)
    w_in_r = _reorder_in_cols(cols(wg[:, 0:OFF_GLU], IN_COLS // N_CHIPS))
    w_glu_b = wg[:, OFF_GLU:OFF_OUT].reshape(SSM_W, SSM_W)
    w_out_b = wg[:, OFF_OUT:OFF_UP].reshape(D_MODEL, D_MODEL)
    w_up_b = cols(wg[:, OFF_UP:OFF_DOWN], 2 * D_FF // N_CHIPS)
    w_down_b = wg[:, OFF_DOWN:OFF_SPARE].reshape(D_FF, D_MODEL)
    cw_bits = wg[:, OFF_SPARE:OFF_SPARE + CONVW_ROWS].reshape(N_CHIPS, -1)[:, :3 * 1408 * 2].reshape(N_CHIPS, 3, 1408, 2)
    conv_w_full = lax.bitcast_convert_type(cw_bits, F32).transpose(1, 0, 2).reshape(3, 2 * D_FF)

    loss_part, dx, big, small, d_conv_w = _local_step(x[0], loss_target[0], w_in_r, w_glu_b, w_out_b, w_up_b, w_down_b,
                                                      conv_w_full, args)
    loss = lax.psum(loss_part, ("x", "y", "c"))

    shard_cols = lambda t: t.reshape(D_MODEL, N_CHIPS, -1).transpose(1, 0, 2)
    grads = jnp.concatenate([
        shard_cols(big["w_in"]).reshape(N_CHIPS, ROWS_IN, D_MODEL),
        big["w_glu"].reshape(N_CHIPS, ROWS_GLU, D_MODEL),
        big["w_out"].reshape(N_CHIPS, ROWS_OUT, D_MODEL),
        shard_cols(big["w_up"]).reshape(N_CHIPS, ROWS_UP, D_MODEL),
        big["w_down"].reshape(N_CHIPS, ROWS_DOWN, D_MODEL),
        jnp.zeros((N_CHIPS, PACK_ROWS - OFF_SPARE, D_MODEL), F32)], axis=1)
    landed = _swap_halves(grads)
    chip_sum = _add_sibling(grads, landed, cc.reshape(1).astype(jnp.int32))
    parts = _scatter_chips(chip_sum)
    red = _join_halves(_sum_chips(parts))
    g_big = {"w_in": red[0:OFF_GLU].reshape(D_MODEL, IN_COLS // N_CHIPS),
             "w_glu": red[OFF_GLU:OFF_OUT].reshape(SSM_W // N_CHIPS, SSM_W),
             "w_out": red[OFF_OUT:OFF_UP],
             "w_up": red[OFF_UP:OFF_DOWN].reshape(D_MODEL, 2 * D_FF // N_CHIPS),
             "w_down": red[OFF_DOWN:OFF_SPARE]}

    small_names = [n for n, _ in _SMALL]
    small_shapes = [sh for _, sh in _SMALL]
    gsum = _allreduce_small(_pack_small([small[n] for n in small_names], extra=[d_conv_w]))
    g_small = _unpack_small(gsum, small_shapes + [(3, 2 * D_FF)])
    g_conv_w = lax.dynamic_slice_in_dim(g_small[-1], chip * (2 * D_FF // N_CHIPS), 2 * D_FF // N_CHIPS, axis=1)
    g_small = dict(zip(small_names, g_small[:-1]))

    grad, delta, new_m, new_v = {}, {}, {}, {}
    for n in ("w_in", "w_glu", "w_out", "w_up", "w_down"):
        grad[n] = g_big[n]
        delta[n], new_m[n], new_v[n] = _adamw(args[n], g_big[n], args["m_" + n], args["v_" + n], name="adamw_" + n)
    grad["conv_w"] = g_conv_w
    delta["conv_w"], new_m["conv_w"], new_v["conv_w"] = _adamw(conv_w, g_conv_w, m_conv_w, v_conv_w, name="adamw_conv_w")
    ws = _pack_small([args[n] for n in small_names])
    ms = _pack_small([args["m_" + n] for n in small_names])
    vs = _pack_small([args["v_" + n] for n in small_names])
    gs = _pack_small([g_small[n] for n in small_names])
    ds_, mn_, vn_ = _adamw(ws, gs, ms, vs, name="adamw_small")
    for n, d_, m_, v_ in zip(small_names, _unpack_small(ds_, small_shapes), _unpack_small(mn_, small_shapes),
                             _unpack_small(vn_, small_shapes)):
        grad[n], delta[n], new_m[n], new_v[n] = g_small[n], d_, m_, v_

    return (loss, dx[None], *[grad[n] for n in order], *[delta[n] for n in order], *[new_m[n] for n in order],
            *[new_v[n] for n in order])
```

```python
import math

import jax
import jax.numpy as jnp
from jax import lax
from jax.experimental import pallas as pl
from jax.experimental.pallas import tpu as pltpu

F32 = jnp.float32
BF16 = jnp.bfloat16

D_MODEL = 1024
HEADS = 8
HEAD_DIM = 64
ATTN_W = 512
SSM_W = 512
SSM_GROUPS = 32
SSM_GROUP = 16
SSM_STATE = 64
N_STATE = SSM_GROUPS * SSM_STATE
D_FF = 2816
IN_COLS = 2056
Z_COLS = 2176
U_COL0 = 1536
F_COL0 = 2048
EPS = 1e-6
NEG_INF = -1e30
N_CHIPS = 4
LANES = 128
SUBLANES = 8
SSM_CHUNKS = 4
CHUNK_U = SSM_W // SSM_CHUNKS
CHUNK_S = N_STATE // SSM_CHUNKS
STRIP = 128
N_STRIPS = D_FF // STRIP

ROWS_IN, ROWS_GLU, ROWS_OUT, ROWS_UP, ROWS_DOWN = 514, 64, 256, 1408, 704
OFF_GLU = ROWS_IN
OFF_OUT = OFF_GLU + ROWS_GLU
OFF_UP = OFF_OUT + ROWS_OUT
OFF_DOWN = OFF_UP + ROWS_UP
OFF_SPARE = OFF_DOWN + ROWS_DOWN
PACK_ROWS = 2976
HALF_ROWS = PACK_ROWS // 2
CONVW_ROWS = 9

ADAM_LR = 0.001
ADAM_B1 = 0.9
ADAM_B2 = 0.999
ADAM_EPS = 1e-08
ADAM_WD = 0.01
ADAM_STEP = 10

VMEM_LIMIT = 56 * 1024 * 1024
MESH = pl.DeviceIdType.MESH


def _pallas(body, **kw):
    return pl.pallas_call(body, **kw)


def _pcall(body, *, name, out_shape, in_specs, out_specs, grid=(), scratch_shapes=(), dims=None):
    params = pltpu.CompilerParams(dimension_semantics=dims, vmem_limit_bytes=VMEM_LIMIT)
    return _pallas(body, name=name, grid=grid, in_specs=in_specs, out_specs=out_specs,
                   out_shape=out_shape, scratch_shapes=scratch_shapes, compiler_params=params)


def _sds(shape, dtype=F32):
    return jax.ShapeDtypeStruct(shape, dtype)


def _dot(a, b):
    return jnp.dot(a, b, preferred_element_type=F32)


def _dot_nt(a, b):
    return lax.dot_general(a, b, (((1,), (1,)), ((), ())), preferred_element_type=F32)


def _dot_tn(a, b):
    return lax.dot_general(a, b, (((0,), (0,)), ((), ())), preferred_element_type=F32)


def _split3(x):
    hi = x.astype(BF16)
    r = x - hi.astype(F32)
    mid = r.astype(BF16)
    lo = (r - mid.astype(F32)).astype(BF16)
    return hi, mid, lo


def _dot_exact_r(x, m01):
    hi, mid, lo = _split3(x)
    return _dot(hi, m01) + _dot(mid, m01) + _dot(lo, m01)


def _dot_exact_l(m01, x):
    hi, mid, lo = _split3(x)
    return _dot(m01, hi) + _dot(m01, mid) + _dot(m01, lo)


def _sigmoid(x):
    return 1.0 / (1.0 + jnp.exp(-x))


def _rms(x, g):
    r = lax.rsqrt(jnp.mean(x * x, axis=-1, keepdims=True) + EPS)
    return x * r * g


def _rms_bwd(x, g, dy):
    r = lax.rsqrt(jnp.mean(x * x, axis=-1, keepdims=True) + EPS)
    w = dy * g
    dx = r * w - x * (r * r * r) * jnp.mean(w * x, axis=-1, keepdims=True)
    dg = jnp.sum(dy * x * r, axis=0, keepdims=True)
    return dx, dg


_GELU_K = math.sqrt(2.0 / math.pi)
_GELU_C = 0.044715


def _gelu(y):
    return y * (0.5 * (1.0 + jnp.tanh(_GELU_K * (y + _GELU_C * (y * y * y)))))


def _gelu_grad(y):
    t = jnp.tanh(_GELU_K * (y + _GELU_C * (y * y * y)))
    return 0.5 * (1.0 + t) + 0.5 * y * (1.0 - t * t) * (_GELU_K * (1.0 + 3.0 * _GELU_C * y * y))


def _tile(n, pref):
    if n <= pref:
        return n
    divs = [t for t in range(LANES, n + 1, LANES) if n % t == 0]
    below = [t for t in divs if t <= pref]
    if below and 2 * below[-1] >= pref:
        return below[-1]
    above = [t for t in divs if t > pref]
    return above[0] if above else n


def _row_tile(s):
    return min(256, s)


def _mm(a, b, *, name, tm, tn, tk, ta=False, tb=False):
    if ta:
        kk, m = a.shape
    else:
        m, kk = a.shape
    n = b.shape[0] if tb else b.shape[1]
    tm, tn, tk = _tile(m, tm), _tile(n, tn), _tile(kk, tk)

    def body(a_ref, b_ref, o_ref):
        k = pl.program_id(2)
        if ta:
            part = _dot_tn(a_ref[...], b_ref[...])
        elif tb:
            part = _dot_nt(a_ref[...], b_ref[...])
        else:
            part = _dot(a_ref[...], b_ref[...])

        @pl.when(k == 0)
        def _():
            o_ref[...] = part

        @pl.when(k > 0)
        def _():
            o_ref[...] += part

    a_spec = pl.BlockSpec((tk, tm), lambda i, j, k: (k, i)) if ta else pl.BlockSpec((tm, tk), lambda i, j, k: (i, k))
    b_spec = pl.BlockSpec((tn, tk), lambda i, j, k: (j, k)) if tb else pl.BlockSpec((tk, tn), lambda i, j, k: (k, j))
    return _pcall(body, name=name, grid=(m // tm, n // tn, kk // tk), in_specs=[a_spec, b_spec],
                  out_specs=pl.BlockSpec((tm, tn), lambda i, j, k: (i, j)), out_shape=_sds((m, n)),
                  dims=("parallel", "parallel", "arbitrary"))(a, b)


def _in_proj(x, g_mix, w_in_r):
    s = x.shape[0]
    tm = _row_tile(s)

    def body(x_ref, g_ref, w_ref, h_ref, z_ref):
        h = _rms(x_ref[...], g_ref[...]).astype(BF16)
        h_ref[...] = h
        z_ref[...] = _dot(h, w_ref[...])

    return _pcall(body, name="in_proj", grid=(s // tm,),
                  in_specs=[pl.BlockSpec((tm, D_MODEL), lambda i: (i, 0)), pl.BlockSpec((1, D_MODEL), lambda i: (0, 0)),
                            pl.BlockSpec((D_MODEL, Z_COLS), lambda i: (0, 0))],
                  out_specs=[pl.BlockSpec((tm, D_MODEL), lambda i: (i, 0)), pl.BlockSpec((tm, Z_COLS), lambda i: (i, 0))],
                  out_shape=[_sds((s, D_MODEL), BF16), _sds((s, Z_COLS))], dims=("parallel",))(x, g_mix, w_in_r)


def _attn_prep(z, gq, gk, bf, gg):
    s = z.shape[0]
    tm = _row_tile(s)

    def body(z_ref, gq_ref, gk_ref, bf_ref, gg_ref, qn_ref, kn_ref, vb_ref, ub_ref, c_ref, carry_ref):
        i = pl.program_id(0)

        @pl.when(i == 0)
        def _():
            carry_ref[...] = jnp.zeros_like(carry_ref)

        gg_m = gg_ref[...]

        def head_norm(t, g):
            ssq = _dot_exact_r(t * t, gg_m)
            return t * lax.rsqrt(ssq * (1.0 / HEAD_DIM) + EPS) * g

        qn_ref[...] = head_norm(z_ref[:, 0:ATTN_W], gq_ref[...]).astype(BF16)
        kn_ref[...] = head_norm(z_ref[:, ATTN_W:2 * ATTN_W], gk_ref[...]).astype(BF16)
        vb_ref[...] = z_ref[:, 2 * ATTN_W:3 * ATTN_W].astype(BF16)
        ub_ref[...] = z_ref[:, U_COL0:U_COL0 + SSM_W].astype(BF16)
        fl = z_ref[:, F_COL0:F_COL0 + LANES] + bf_ref[...]
        lf = jnp.minimum(fl, 0.0) - jnp.log1p(jnp.exp(-jnp.abs(fl)))
        row = lax.broadcasted_iota(jnp.int32, (tm, tm), 0)
        col = lax.broadcasted_iota(jnp.int32, (tm, tm), 1)
        tri = (row >= col).astype(BF16)
        c = _dot_exact_l(tri, lf) + carry_ref[...]
        c_ref[...] = c
        carry_ref[...] = c[tm - 1:tm, :]

    row_spec = lambda w: pl.BlockSpec((tm, w), lambda i: (i, 0))
    const = lambda shape: pl.BlockSpec(shape, lambda i: (0, 0))
    return _pcall(body, name="attn_prep", grid=(s // tm,),
                  in_specs=[row_spec(Z_COLS), const((1, ATTN_W)), const((1, ATTN_W)), const((1, LANES)), const((ATTN_W, ATTN_W))],
                  out_specs=[row_spec(ATTN_W)] * 4 + [row_spec(LANES)],
                  out_shape=[_sds((s, ATTN_W), BF16)] * 4 + [_sds((s, LANES))],
                  scratch_shapes=[pltpu.VMEM((1, LANES), F32)], dims=("arbitrary",))(z, gq, gk, bf, gg)


def _attn_fwd(qh, kh, vh, crow):
    _, s, _ = qh.shape
    tq = _row_tile(s)
    scale = HEAD_DIM ** -0.5

    def body(q_ref, k_ref, v_ref, c_ref, o_ref, lse_ref):
        i = pl.program_id(1)
        q = q_ref[0]

        def block(j, carry, diagonal):
            m, l, acc = carry
            off = pl.multiple_of(j * tq, tq)
            ks = k_ref[0, pl.ds(off, tq), :]
            vs = v_ref[0, pl.ds(off, tq), :]
            cs = c_ref[0, :, pl.ds(off, tq)]
            sc = _dot_nt(q, ks) * scale - cs
            if diagonal:
                causal = lax.broadcasted_iota(jnp.int32, (tq, tq), 1) <= lax.broadcasted_iota(jnp.int32, (tq, tq), 0)
                sc = jnp.where(causal, sc, NEG_INF)
            m_new = jnp.maximum(m, jnp.max(sc, axis=-1, keepdims=True))
            p = jnp.exp(sc - m_new)
            alpha = jnp.exp(m - m_new)
            l = alpha * l + jnp.sum(p, axis=-1, keepdims=True)
            acc = alpha * acc + _dot(p.astype(BF16), vs)
            return m_new, l, acc

        init = (jnp.full((tq, 1), NEG_INF, F32), jnp.zeros((tq, 1), F32), jnp.zeros((tq, HEAD_DIM), F32))
        carry = lax.fori_loop(0, i, lambda j, c: block(j, c, False), init)
        m, l, acc = block(i, carry, True)
        o_ref[0] = acc / l
        lse_ref[0] = m + jnp.log(l)

    blk = pl.BlockSpec((1, tq, HEAD_DIM), lambda h, i: (h, i, 0))
    full = pl.BlockSpec((1, s, HEAD_DIM), lambda h, i: (h, 0, 0))
    return _pcall(body, name="attn_fwd", grid=(HEADS, s // tq),
                  in_specs=[blk, full, full, pl.BlockSpec((1, 1, s), lambda h, i: (h, 0, 0))],
                  out_specs=[blk, pl.BlockSpec((1, tq, 1), lambda h, i: (h, i, 0))],
                  out_shape=[_sds((HEADS, s, HEAD_DIM)), _sds((HEADS, s, 1))],
                  dims=("parallel", "parallel"))(qh, kh, vh, crow)


def _ssm_param_fn(lr, li, ls, br, bi):
    step = jnp.exp(ls)
    er = jnp.exp(lr * step)
    ab_re = er * jnp.cos(li * step)
    ab_im = er * jnp.sin(li * step)
    num_re = ab_re - 1.0
    num_im = ab_im
    den = lr * lr + li * li
    f_re = (num_re * lr + num_im * li) / den
    f_im = (num_im * lr - num_re * li) / den
    bb_re = f_re * br - f_im * bi
    bb_im = f_re * bi + f_im * br
    return ab_re, ab_im, bb_re, bb_im


_PARAM_SHAPE = (SSM_GROUPS * SSM_GROUP, SSM_STATE)


def _ssm_params(lr, li, ls, br, bi):
    def body(lr_ref, li_ref, ls_ref, br_ref, bi_ref, ar_ref, ai_ref, bbr_ref, bbi_ref):
        ar, ai, bbr, bbi = _ssm_param_fn(lr_ref[...], li_ref[...], ls_ref[...], br_ref[...], bi_ref[...])
        ar_ref[...] = ar
        ai_ref[...] = ai
        bbr_ref[...] = bbr
        bbi_ref[...] = bbi

    spec = pl.BlockSpec(_PARAM_SHAPE, lambda: (0, 0))
    return _pcall(body, name="ssm_params", in_specs=[spec] * 5, out_specs=[spec] * 4,
                  out_shape=[_sds(_PARAM_SHAPE)] * 4)(lr, li, ls, br, bi)


def _ssm_params_bwd(lr, li, ls, br, bi, dar, dai, dbbr, dbbi, expand_t):
    def body(lr_ref, li_ref, ls_ref, br_ref, bi_ref, dar_ref, dai_ref, dbbr_ref, dbbi_ref, et_ref,
             dlr_ref, dli_ref, dls_ref, dbr_ref, dbi_ref):
        _, vjp = jax.vjp(_ssm_param_fn, lr_ref[...], li_ref[...], ls_ref[...], br_ref[...], bi_ref[...])
        dlr, dli, dls, dbr, dbi = vjp((dar_ref[...], dai_ref[...], dbbr_ref[...], dbbi_ref[...]))
        et = et_ref[...]
        dlr_ref[...] = _dot_exact_l(et, dlr)
        dli_ref[...] = _dot_exact_l(et, dli)
        dls_ref[...] = jnp.sum(_dot_exact_l(et, dls), axis=-1, keepdims=True)
        dbr_ref[...] = dbr
        dbi_ref[...] = dbi

    spec = pl.BlockSpec(_PARAM_SHAPE, lambda: (0, 0))
    gspec = pl.BlockSpec((SSM_GROUPS, SSM_STATE), lambda: (0, 0))
    return _pcall(body, name="ssm_params_bwd",
                  in_specs=[spec] * 9 + [pl.BlockSpec((SSM_GROUPS, _PARAM_SHAPE[0]), lambda: (0, 0))],
                  out_specs=[gspec, gspec, pl.BlockSpec((SSM_GROUPS, 1), lambda: (0, 0)), spec, spec],
                  out_shape=[_sds((SSM_GROUPS, SSM_STATE))] * 2 + [_sds((SSM_GROUPS, 1))] + [_sds(_PARAM_SHAPE)] * 2,
                  )(lr, li, ls, br, bi, dar, dai, dbbr, dbbi, expand_t)


def _cmul(ar, ai, br, bi):
    return ar * br - ai * bi, ar * bi + ai * br


def _scan_consts(ar, ai, width, reverse):
    row = lax.broadcasted_iota(jnp.int32, (SUBLANES, width), 0)
    pw = [(ar, ai)]
    for _ in range(SUBLANES - 1):
        pw.append(_cmul(pw[-1][0], pw[-1][1], ar, ai))
    steps = []
    for d in (1, 2, 4):
        keep = (row < SUBLANES - d) if reverse else (row >= d)
        steps.append((d, jnp.where(keep, pw[d - 1][0], 0.0), jnp.where(keep, pw[d - 1][1], 0.0)))
    pr = jnp.zeros((SUBLANES, width), F32)
    pi = jnp.zeros((SUBLANES, width), F32)
    for r in range(SUBLANES):
        e = (SUBLANES - r) if reverse else (r + 1)
        pr = jnp.where(row == r, pw[e - 1][0], pr)
        pi = jnp.where(row == r, pw[e - 1][1], pi)
    return steps, pr, pi


def _scan_tile(xr, xi, cr, ci, consts, reverse):
    steps, pr, pi = consts
    for d, mr, mi in steps:
        sh = (SUBLANES - d) if reverse else d
        sr = pltpu.roll(xr, sh, 0)
        si = pltpu.roll(xi, sh, 0)
        xr, xi = xr + mr * sr - mi * si, xi + mr * si + mi * sr
    return xr + pr * cr - pi * ci, xi + pr * ci + pi * cr


def _ssm_fwd(ub, z, bbr, bbi, ar, ai, ccr, cci, dsk):
    s = ub.shape[0]
    tm = _row_tile(s)
    nt = tm // SUBLANES

    def body(ub_ref, u_ref, bbr_ref, bbi_ref, ar_ref, ai_ref, ccr_ref, cci_ref, dsk_ref,
             xr_ref, xi_ref, y_ref, cr_s, ci_s):
        i = pl.program_id(1)

        @pl.when(i == 0)
        def _():
            cr_s[...] = jnp.zeros_like(cr_s)
            ci_s[...] = jnp.zeros_like(ci_s)

        u_b = ub_ref[...]
        xr_ref[...] = _dot(u_b, bbr_ref[0])
        xi_ref[...] = _dot(u_b, bbi_ref[0])
        consts = _scan_consts(ar_ref[0], ai_ref[0], CHUNK_S, False)

        def tile(k, carry):
            cr, ci = carry
            sl = pl.ds(pl.multiple_of(k * SUBLANES, SUBLANES), SUBLANES)
            xr, xi = _scan_tile(xr_ref[sl, :], xi_ref[sl, :], cr, ci, consts, False)
            xr_ref[sl, :] = xr
            xi_ref[sl, :] = xi
            return xr[SUBLANES - 1:SUBLANES, :], xi[SUBLANES - 1:SUBLANES, :]

        cr, ci = lax.fori_loop(0, nt, tile, (cr_s[...], ci_s[...]))
        cr_s[...] = cr
        ci_s[...] = ci
        y_ref[...] = (_dot(xr_ref[...].astype(BF16), ccr_ref[0]) - _dot(xi_ref[...].astype(BF16), cci_ref[0])
                      + dsk_ref[...] * u_ref[...])

    ucol0 = U_COL0 // CHUNK_U
    wspec = lambda a, b: pl.BlockSpec((1, a, b), lambda j, i: (j, 0, 0))
    return _pcall(body, name="ssm_fwd", grid=(SSM_CHUNKS, s // tm),
                  in_specs=[pl.BlockSpec((tm, CHUNK_U), lambda j, i: (i, j)),
                            pl.BlockSpec((tm, CHUNK_U), lambda j, i: (i, ucol0 + j)),
                            wspec(CHUNK_U, CHUNK_S), wspec(CHUNK_U, CHUNK_S), wspec(1, CHUNK_S), wspec(1, CHUNK_S),
                            wspec(CHUNK_S, CHUNK_U), wspec(CHUNK_S, CHUNK_U),
                            pl.BlockSpec((1, CHUNK_U), lambda j, i: (0, j))],
                  out_specs=[pl.BlockSpec((tm, CHUNK_S), lambda j, i: (i, j)), pl.BlockSpec((tm, CHUNK_S), lambda j, i: (i, j)),
                             pl.BlockSpec((tm, CHUNK_U), lambda j, i: (i, j))],
                  out_shape=[_sds((s, N_STATE)), _sds((s, N_STATE)), _sds((s, SSM_W))],
                  scratch_shapes=[pltpu.VMEM((1, CHUNK_S), F32)] * 2,
                  dims=("parallel", "arbitrary"))(ub, z, bbr, bbi, ar, ai, ccr, cci, dsk)


def _ssm_glu(y, w_glu, b_glu):
    ge = _gelu(y)
    sg = _sigmoid(_dot(ge.astype(BF16), w_glu) + b_glu)
    return ge, sg


def _mix_out(y, att, x, w_glu, b_glu, g_att, g_ssm, w_out, g_ffn):
    s = x.shape[0]
    tm = _row_tile(s)

    def body(y_ref, att_ref, x_ref, wg_ref, bg_ref, ga_ref, gs_ref, wo_ref, gf_ref, x1_ref, mix_ref, h2_ref):
        ge, sg = _ssm_glu(y_ref[...], wg_ref[...], bg_ref[...])
        ms = _rms(ge * sg, gs_ref[...]).astype(BF16)
        ma = _rms(att_ref[...], ga_ref[...]).astype(BF16)
        mix_ref[:, 0:ATTN_W] = ma
        mix_ref[:, ATTN_W:D_MODEL] = ms
        x1 = x_ref[...] + (_dot(ma, wo_ref[0:ATTN_W, :]) + _dot(ms, wo_ref[ATTN_W:D_MODEL, :]))
        x1_ref[...] = x1
        h2_ref[...] = _rms(x1, gf_ref[...]).astype(BF16)

    row = lambda w: pl.BlockSpec((tm, w), lambda i: (i, 0))
    const = lambda a, b: pl.BlockSpec((a, b), lambda i: (0, 0))
    return _pcall(body, name="mix_out", grid=(s // tm,),
                  in_specs=[row(SSM_W), row(ATTN_W), row(D_MODEL), const(SSM_W, SSM_W), const(1, SSM_W), const(1, ATTN_W),
                            const(1, SSM_W), const(D_MODEL, D_MODEL), const(1, D_MODEL)],
                  out_specs=[row(D_MODEL)] * 3,
                  out_shape=[_sds((s, D_MODEL)), _sds((s, D_MODEL), BF16), _sds((s, D_MODEL), BF16)],
                  dims=("parallel",))(y, att, x, w_glu, b_glu, g_att, g_ssm, w_out, g_ffn)


def _conv_rows(pad_ref, w, b, s):
    y = b + pad_ref[pl.ds(SUBLANES - 2, s), :] * w[0:1, :]
    y = y + pad_ref[pl.ds(SUBLANES - 1, s), :] * w[1:2, :]
    return y + pad_ref[pl.ds(SUBLANES, s), :] * w[2:3, :]


def _conv_act(up, conv_w, conv_b):
    s = up.shape[0]

    def body(ug_ref, uv_ref, wg_ref, wv_ref, bg_ref, bv_ref, act_ref, pg_ref, pv_ref):
        zero = jnp.zeros((SUBLANES, STRIP), F32)
        pg_ref[0:SUBLANES, :] = zero
        pv_ref[0:SUBLANES, :] = zero
        pg_ref[pl.ds(SUBLANES, s), :] = ug_ref[...]
        pv_ref[pl.ds(SUBLANES, s), :] = uv_ref[...]
        hg = _conv_rows(pg_ref, wg_ref[...], bg_ref[...], s)
        hv = _conv_rows(pv_ref, wv_ref[...], bv_ref[...], s)
        act_ref[...] = (hg * _sigmoid(hg) * hv).astype(BF16)

    strip = lambda off: pl.BlockSpec((s, STRIP), lambda j: (0, j + off))
    wsp = lambda off: pl.BlockSpec((3, STRIP), lambda j: (0, j + off))
    bsp = lambda off: pl.BlockSpec((1, STRIP), lambda j: (0, j + off))
    return _pcall(body, name="conv_act", grid=(N_STRIPS,),
                  in_specs=[strip(0), strip(N_STRIPS), wsp(0), wsp(N_STRIPS), bsp(0), bsp(N_STRIPS)],
                  out_specs=pl.BlockSpec((s, STRIP), lambda j: (0, j)), out_shape=_sds((s, D_FF), BF16),
                  scratch_shapes=[pltpu.VMEM((s + SUBLANES, STRIP), F32)] * 2,
                  dims=("parallel",))(up, up, conv_w, conv_w, conv_b, conv_b)


def _down_loss(act, w_down, x1, tgt):
    s = x1.shape[0]
    tm = _row_tile(s)

    def body(a_ref, w_ref, x1_ref, t_ref, dy_ref, dyb_ref, loss_ref):
        i = pl.program_id(0)

        @pl.when(i == 0)
        def _():
            loss_ref[...] = jnp.zeros_like(loss_ref)

        diff = x1_ref[...] + _dot(a_ref[...], w_ref[...]) - t_ref[...]
        dy = diff * (1.0 / D_MODEL)
        dy_ref[...] = dy
        dyb_ref[...] = dy.astype(BF16)
        loss_ref[...] += 0.5 * jnp.sum(diff * dy)

    row = lambda w: pl.BlockSpec((tm, w), lambda i: (i, 0))
    return _pcall(body, name="down_loss", grid=(s // tm,),
                  in_specs=[row(D_FF), pl.BlockSpec((D_FF, D_MODEL), lambda i: (0, 0)), row(D_MODEL), row(D_MODEL)],
                  out_specs=[row(D_MODEL), row(D_MODEL), pl.BlockSpec((SUBLANES, LANES), lambda i: (0, 0))],
                  out_shape=[_sds((s, D_MODEL)), _sds((s, D_MODEL), BF16), _sds((SUBLANES, LANES))],
                  dims=("arbitrary",))(act, w_down, x1, tgt)


def _conv_act_bwd(up, dact, conv_w, conv_b):
    s = up.shape[0]

    def body(ug_ref, uv_ref, uo_ref, da_ref, wg_ref, wv_ref, wo_ref, bg_ref, bv_ref, dup_ref, dcw_ref,
             pg_ref, pv_ref, pd_ref):
        t = pl.program_id(0)
        zero = jnp.zeros((SUBLANES, STRIP), F32)
        pg_ref[0:SUBLANES, :] = zero
        pv_ref[0:SUBLANES, :] = zero
        pg_ref[pl.ds(SUBLANES, s), :] = ug_ref[...]
        pv_ref[pl.ds(SUBLANES, s), :] = uv_ref[...]
        hg = _conv_rows(pg_ref, wg_ref[...], bg_ref[...], s)
        hv = _conv_rows(pv_ref, wv_ref[...], bv_ref[...], s)
        sg = _sigmoid(hg)
        da = da_ref[...]
        d_gate = da * hv * (sg * (1.0 + hg * (1.0 - sg)))
        d_val = da * (hg * sg)
        dh = jnp.where(t == 0, d_gate, d_val)
        pg_ref[pl.ds(SUBLANES, s), :] = uo_ref[...]
        pd_ref[pl.ds(0, s), :] = dh
        pd_ref[pl.ds(s, SUBLANES), :] = zero
        w = wo_ref[...]
        dup = dh * w[2:3, :] + pd_ref[pl.ds(1, s), :] * w[1:2, :] + pd_ref[pl.ds(2, s), :] * w[0:1, :]
        dup_ref[...] = dup.astype(BF16)
        rows = [jnp.sum(dh * pg_ref[pl.ds(SUBLANES - 2 + k, s), :], axis=0, keepdims=True) for k in range(3)]
        rows.append(jnp.sum(dh, axis=0, keepdims=True))
        rid = lax.broadcasted_iota(jnp.int32, (SUBLANES, STRIP), 0)
        out = jnp.zeros((SUBLANES, STRIP), F32)
        for k, r in enumerate(rows):
            out = jnp.where(rid == k, r, out)
        dcw_ref[...] = out

    strip = lambda f: pl.BlockSpec((s, STRIP), f)
    wsp = lambda f: pl.BlockSpec((3, STRIP), f)
    bsp = lambda f: pl.BlockSpec((1, STRIP), f)
    gate = lambda t, j: (0, j)
    val = lambda t, j: (0, j + N_STRIPS)
    own = lambda t, j: (0, t * N_STRIPS + j)
    return _pcall(body, name="conv_act_bwd", grid=(2, N_STRIPS),
                  in_specs=[strip(gate), strip(val), strip(own), strip(gate), wsp(gate), wsp(val), wsp(own), bsp(gate), bsp(val)],
                  out_specs=[strip(own), pl.BlockSpec((SUBLANES, STRIP), own)],
                  out_shape=[_sds((s, 2 * D_FF), BF16), _sds((SUBLANES, 2 * D_FF))],
                  scratch_shapes=[pltpu.VMEM((s + SUBLANES, STRIP), F32)] * 3,
                  dims=("parallel", "parallel"))(up, up, up, dact, conv_w, conv_w, conv_w, conv_b, conv_b)


def _mix_bwd(dy, dh2, x1, g_ffn, w_out, y, att, w_glu, b_glu, g_att, g_ssm):
    s = dy.shape[0]
    tm = _row_tile(s)

    def body(dy_ref, dh2_ref, x1_ref, gf_ref, wo_ref, y_ref, att_ref, wg_ref, bg_ref, ga_ref, gs_ref,
             dx1_ref, dx1b_ref, datt_ref, dys_ref, dwg_ref, dgf_ref, dga_ref, dgs_ref, dbg_ref):
        i = pl.program_id(0)

        @pl.when(i == 0)
        def _():
            for r in (dwg_ref, dgf_ref, dga_ref, dgs_ref, dbg_ref):
                r[...] = jnp.zeros_like(r)

        dxn, dgf = _rms_bwd(x1_ref[...], gf_ref[...], dh2_ref[...])
        dx1 = dy_ref[...] + dxn
        dx1_ref[...] = dx1
        dx1b = dx1.astype(BF16)
        dx1b_ref[...] = dx1b
        dgf_ref[...] += dgf
        dma = _dot_nt(dx1b, wo_ref[0:ATTN_W, :])
        dms = _dot_nt(dx1b, wo_ref[ATTN_W:D_MODEL, :])
        datt, dga = _rms_bwd(att_ref[...], ga_ref[...], dma)
        datt_ref[...] = datt
        dga_ref[...] += dga
        yv = y_ref[...]
        ge, sg = _ssm_glu(yv, wg_ref[...], bg_ref[...])
        dssm, dgs = _rms_bwd(ge * sg, gs_ref[...], dms)
        dgs_ref[...] += dgs
        dgl = dssm * ge * sg * (1.0 - sg)
        dglb = dgl.astype(BF16)
        dge = dssm * sg + _dot_nt(dglb, wg_ref[...])
        dbg_ref[...] += jnp.sum(dgl, axis=0, keepdims=True)
        dwg_ref[...] += _dot_tn(ge.astype(BF16), dglb)
        dys_ref[...] = dge * _gelu_grad(yv)

    row = lambda w: pl.BlockSpec((tm, w), lambda i: (i, 0))
    const = lambda a, b: pl.BlockSpec((a, b), lambda i: (0, 0))
    return _pcall(body, name="mix_bwd", grid=(s // tm,),
                  in_specs=[row(D_MODEL), row(D_MODEL), row(D_MODEL), const(1, D_MODEL), const(D_MODEL, D_MODEL), row(SSM_W),
                            row(ATTN_W), const(SSM_W, SSM_W), const(1, SSM_W), const(1, ATTN_W), const(1, SSM_W)],
                  out_specs=[row(D_MODEL), row(D_MODEL), row(ATTN_W), row(SSM_W), const(SSM_W, SSM_W), const(1, D_MODEL),
                             const(1, ATTN_W), const(1, SSM_W), const(1, SSM_W)],
                  out_shape=[_sds((s, D_MODEL)), _sds((s, D_MODEL), BF16), _sds((s, ATTN_W)), _sds((s, SSM_W)),
                             _sds((SSM_W, SSM_W)), _sds((1, D_MODEL)), _sds((1, ATTN_W)), _sds((1, SSM_W)), _sds((1, SSM_W))],
                  dims=("arbitrary",))(dy, dh2, x1, g_ffn, w_out, y, att, w_glu, b_glu, g_att, g_ssm)


def _ssm_bwd(dys, z, ub, xr, xi, bbr, bbi, ar, ai, ccr, cci, dsk):
    s = dys.shape[0]
    tm = _row_tile(s)
    nb = s // tm
    nt = tm // SUBLANES

    def body(dy_ref, u_ref, ub_ref, xr_ref, xi_ref, xrp_ref, xip_ref, bbr_ref, bbi_ref, ar_ref, ai_ref, ccr_ref,
             cci_ref, dsk_ref, du_ref, dbbr_ref, dbbi_ref, dccr_ref, dcci_ref, dar_ref, dai_ref, dd_ref,
             gr_s, gi_s, cr_s, ci_s, accr_s, acci_s):
        i = pl.program_id(1)
        first_block = i == nb - 1

        @pl.when(i == 0)
        def _():
            for r in (cr_s, ci_s, accr_s, acci_s, dbbr_ref, dbbi_ref, dccr_ref, dcci_ref, dd_ref):
                r[...] = jnp.zeros_like(r)

        dy = dy_ref[...]
        dyb = dy.astype(BF16)
        gr_s[...] = _dot_nt(dyb, ccr_ref[0])
        gi_s[...] = -_dot_nt(dyb, cci_ref[0])
        consts = _scan_consts(ar_ref[0], -ai_ref[0], CHUNK_S, True)
        row = lax.broadcasted_iota(jnp.int32, (SUBLANES, CHUNK_S), 0)

        def tile(kk, carry):
            cr, ci, accr, acci = carry
            k = nt - 1 - kk
            sl = pl.ds(pl.multiple_of(k * SUBLANES, SUBLANES), SUBLANES)
            gr, gi = _scan_tile(gr_s[sl, :], gi_s[sl, :], cr, ci, consts, True)
            gr_s[sl, :] = gr
            gi_s[sl, :] = gi
            slp = pl.ds(pl.multiple_of(jnp.maximum(k - 1, 0) * SUBLANES, SUBLANES), SUBLANES)
            inner = k > 0
            pr_t = jnp.where(inner, xr_ref[slp, :], xrp_ref[...])
            pi_t = jnp.where(inner, xi_ref[slp, :], xip_ref[...])
            live = jnp.logical_or(inner, jnp.logical_not(first_block))
            top_r = jnp.where(live, pltpu.roll(pr_t, 1, 0), 0.0)
            top_i = jnp.where(live, pltpu.roll(pi_t, 1, 0), 0.0)
            xpr = jnp.where(row == 0, top_r, pltpu.roll(xr_ref[sl, :], 1, 0))
            xpi = jnp.where(row == 0, top_i, pltpu.roll(xi_ref[sl, :], 1, 0))
            accr = accr + gr * xpr + gi * xpi
            acci = acci + gi * xpr - gr * xpi
            return gr[0:1, :], gi[0:1, :], accr, acci

        zeros = jnp.zeros((SUBLANES, CHUNK_S), F32)
        cr, ci, accr, acci = lax.fori_loop(0, nt, tile, (cr_s[...], ci_s[...], zeros, zeros))
        cr_s[...] = cr
        ci_s[...] = ci
        accr_s[...] += accr
        acci_s[...] += acci
        grb = gr_s[...].astype(BF16)
        gib = gi_s[...].astype(BF16)
        u_b = ub_ref[...]
        du_ref[...] = _dot_nt(grb, bbr_ref[0]) + _dot_nt(gib, bbi_ref[0]) + dsk_ref[...] * dy
        dbbr_ref[0] += _dot_tn(u_b, grb)
        dbbi_ref[0] += _dot_tn(u_b, gib)
        dccr_ref[0] += _dot_tn(xr_ref[...].astype(BF16), dyb)
        dcci_ref[0] -= _dot_tn(xi_ref[...].astype(BF16), dyb)
        dd_ref[...] += jnp.sum(dy * u_ref[...], axis=0, keepdims=True)

        @pl.when(i == nb - 1)
        def _():
            dar_ref[0] = jnp.sum(accr_s[...], axis=0, keepdims=True)
            dai_ref[0] = jnp.sum(acci_s[...], axis=0, keepdims=True)

    ucol0 = U_COL0 // CHUNK_U
    tiles_per_block = tm // SUBLANES
    rb = lambda i: nb - 1 - i
    wspec = lambda a, b: pl.BlockSpec((1, a, b), lambda j, i: (j, 0, 0))
    xblk = pl.BlockSpec((tm, CHUNK_S), lambda j, i: (rb(i), j))
    xprev = pl.BlockSpec((SUBLANES, CHUNK_S), lambda j, i: (jnp.maximum(rb(i) * tiles_per_block - 1, 0), j))
    ublk = pl.BlockSpec((tm, CHUNK_U), lambda j, i: (rb(i), j))
    return _pcall(body, name="ssm_bwd", grid=(SSM_CHUNKS, nb),
                  in_specs=[ublk, pl.BlockSpec((tm, CHUNK_U), lambda j, i: (rb(i), ucol0 + j)), ublk, xblk, xblk, xprev, xprev,
                            wspec(CHUNK_U, CHUNK_S), wspec(CHUNK_U, CHUNK_S), wspec(1, CHUNK_S), wspec(1, CHUNK_S),
                            wspec(CHUNK_S, CHUNK_U), wspec(CHUNK_S, CHUNK_U), pl.BlockSpec((1, CHUNK_U), lambda j, i: (0, j))],
                  out_specs=[ublk, wspec(CHUNK_U, CHUNK_S), wspec(CHUNK_U, CHUNK_S), wspec(CHUNK_S, CHUNK_U),
                             wspec(CHUNK_S, CHUNK_U), wspec(1, CHUNK_S), wspec(1, CHUNK_S),
                             pl.BlockSpec((1, CHUNK_U), lambda j, i: (0, j))],
                  out_shape=[_sds((s, SSM_W)), _sds((SSM_CHUNKS, CHUNK_U, CHUNK_S)), _sds((SSM_CHUNKS, CHUNK_U, CHUNK_S)),
                             _sds((SSM_CHUNKS, CHUNK_S, CHUNK_U)), _sds((SSM_CHUNKS, CHUNK_S, CHUNK_U)),
                             _sds((SSM_CHUNKS, 1, CHUNK_S)), _sds((SSM_CHUNKS, 1, CHUNK_S)), _sds((1, SSM_W))],
                  scratch_shapes=[pltpu.VMEM((tm, CHUNK_S), F32)] * 2 + [pltpu.VMEM((1, CHUNK_S), F32)] * 2
                                 + [pltpu.VMEM((SUBLANES, CHUNK_S), F32)] * 2,
                  dims=("parallel", "arbitrary"))(dys, z, ub, xr, xi, xr, xi, bbr, bbi, ar, ai, ccr, cci, dsk)


def _attn_probs(q, ks, cs, lse, scale, diagonal):
    p = jnp.exp(_dot_nt(q, ks) * scale - cs - lse)
    if diagonal:
        tq, tk = p.shape
        causal = lax.broadcasted_iota(jnp.int32, (tq, tk), 1) <= lax.broadcasted_iota(jnp.int32, (tq, tk), 0)
        p = jnp.where(causal, p, 0.0)
    return p


def _attn_dl(qh, kh, vh, crow, lse, doh):
    _, s, _ = qh.shape
    tq = _row_tile(s)
    scale = HEAD_DIM ** -0.5

    def body(q_ref, k_ref, v_ref, c_ref, lse_ref, do_ref, dl_ref):
        i = pl.program_id(1)
        q = q_ref[0]
        dob = do_ref[0].astype(BF16)
        lse_i = lse_ref[0]

        def block(j, dl, diagonal):
            off = pl.multiple_of(j * tq, tq)
            ks = k_ref[0, pl.ds(off, tq), :]
            vs = v_ref[0, pl.ds(off, tq), :]
            p = _attn_probs(q, ks, c_ref[0, :, pl.ds(off, tq)], lse_i, scale, diagonal)
            return dl + jnp.sum(p * _dot_nt(dob, vs), axis=-1, keepdims=True)

        dl = lax.fori_loop(0, i, lambda j, c: block(j, c, False), jnp.zeros((tq, 1), F32))
        dl_ref[0] = block(i, dl, True)

    blk = pl.BlockSpec((1, tq, HEAD_DIM), lambda h, i: (h, i, 0))
    full = pl.BlockSpec((1, s, HEAD_DIM), lambda h, i: (h, 0, 0))
    col = pl.BlockSpec((1, tq, 1), lambda h, i: (h, i, 0))
    return _pcall(body, name="attn_dl", grid=(HEADS, s // tq),
                  in_specs=[blk, full, full, pl.BlockSpec((1, 1, s), lambda h, i: (h, 0, 0)), col, blk],
                  out_specs=col, out_shape=_sds((HEADS, s, 1)), dims=("parallel", "parallel"))(qh, kh, vh, crow, lse, doh)


def _attn_bwd(qh, kh, vh, crow, dl, lse, doh):
    _, s, _ = qh.shape
    tq = _row_tile(s)
    nq = s // tq
    scale = HEAD_DIM ** -0.5

    def body(q_ref, k_ref, v_ref, c_ref, dl_ref, lse_ref, do_ref, dq_ref, dk_ref, dv_ref, dc_ref):
        j = pl.program_id(1)

        @pl.when(j == 0)
        def _():
            dq_ref[...] = jnp.zeros_like(dq_ref)

        ks = k_ref[0]
        vs = v_ref[0]
        cs = c_ref[0]

        def block(i, carry, diagonal):
            dk, dv, dc = carry
            rows = pl.ds(pl.multiple_of(i * tq, tq), tq)
            q = q_ref[0, rows, :]
            dob = do_ref[0, rows, :].astype(BF16)
            p = _attn_probs(q, ks, cs, lse_ref[0, rows, :], scale, diagonal)
            dp = _dot_nt(dob, vs)
            ds = p * (dp - dl_ref[0, rows, :])
            dsb = ds.astype(BF16)
            dv = dv + _dot_tn(p.astype(BF16), dob)
            dk = dk + _dot_tn(dsb, q) * scale
            dq_ref[0, rows, :] += _dot(dsb, ks) * scale
            dc = dc - jnp.sum(ds, axis=0, keepdims=True)
            return dk, dv, dc

        init = (jnp.zeros((tq, HEAD_DIM), F32), jnp.zeros((tq, HEAD_DIM), F32), jnp.zeros((1, tq), F32))
        carry = block(j, init, True)
        dk, dv, dc = lax.fori_loop(j + 1, nq, lambda i, c: block(i, c, False), carry)
        dk_ref[0] = dk
        dv_ref[0] = dv
        dc_ref[0] = dc

    blk = pl.BlockSpec((1, tq, HEAD_DIM), lambda h, j: (h, j, 0))
    full = pl.BlockSpec((1, s, HEAD_DIM), lambda h, j: (h, 0, 0))
    cblk = pl.BlockSpec((1, 1, tq), lambda h, j: (h, 0, j))
    col = pl.BlockSpec((1, s, 1), lambda h, j: (h, 0, 0))
    return _pcall(body, name="attn_bwd", grid=(HEADS, nq),
                  in_specs=[full, blk, blk, cblk, col, col, full],
                  out_specs=[full, blk, blk, cblk],
                  out_shape=[_sds((HEADS, s, HEAD_DIM))] * 3 + [_sds((HEADS, 1, s))],
                  dims=("parallel", "arbitrary"))(qh, kh, vh, crow, dl, lse, doh)


def _prep_bwd(z, dqn, dkn, dv, du, dc, gq, gk, bf, gg):
    s = z.shape[0]
    tm = _row_tile(s)
    nb = s // tm

    def body(z_ref, dqn_ref, dkn_ref, dv_ref, du_ref, dc_ref, gq_ref, gk_ref, bf_ref, gg_ref,
             dz_ref, dgq_ref, dgk_ref, dbf_ref, carry_ref):
        i = pl.program_id(0)

        @pl.when(i == 0)
        def _():
            for r in (dgq_ref, dgk_ref, dbf_ref, carry_ref):
                r[...] = jnp.zeros_like(r)

        gg_m = gg_ref[...]

        def head_norm_bwd(t, g, dn):
            r = lax.rsqrt(_dot_exact_r(t * t, gg_m) * (1.0 / HEAD_DIM) + EPS)
            w = dn * g
            mean_wt = _dot_exact_r(w * t, gg_m) * (1.0 / HEAD_DIM)
            return r * w - t * (r * r * r) * mean_wt, jnp.sum(dn * t * r, axis=0, keepdims=True)

        dq, dgq = head_norm_bwd(z_ref[:, 0:ATTN_W], gq_ref[...], dqn_ref[...])
        dk, dgk = head_norm_bwd(z_ref[:, ATTN_W:2 * ATTN_W], gk_ref[...], dkn_ref[...])
        dgq_ref[...] += dgq
        dgk_ref[...] += dgk
        row = lax.broadcasted_iota(jnp.int32, (tm, tm), 0)
        col = lax.broadcasted_iota(jnp.int32, (tm, tm), 1)
        triu = (col >= row).astype(BF16)
        dlf = _dot_exact_l(triu, dc_ref[...]) + carry_ref[...]
        carry_ref[...] = dlf[0:1, :]
        fl = z_ref[:, F_COL0:F_COL0 + LANES] + bf_ref[...]
        df = dlf * _sigmoid(-fl)
        dbf_ref[...] += jnp.sum(df, axis=0, keepdims=True)
        dz_ref[:, 0:ATTN_W] = dq.astype(BF16)
        dz_ref[:, ATTN_W:2 * ATTN_W] = dk.astype(BF16)
        dz_ref[:, 2 * ATTN_W:3 * ATTN_W] = dv_ref[...].astype(BF16)
        dz_ref[:, U_COL0:U_COL0 + SSM_W] = du_ref[...].astype(BF16)
        dz_ref[:, F_COL0:F_COL0 + LANES] = df.astype(BF16)

    row_spec = lambda w: pl.BlockSpec((tm, w), lambda i: (nb - 1 - i, 0))
    const = lambda shape: pl.BlockSpec(shape, lambda i: (0, 0))
    return _pcall(body, name="prep_bwd", grid=(nb,),
                  in_specs=[row_spec(Z_COLS)] + [row_spec(ATTN_W)] * 4 + [row_spec(LANES), const((1, ATTN_W)),
                            const((1, ATTN_W)), const((1, LANES)), const((ATTN_W, ATTN_W))],
                  out_specs=[row_spec(Z_COLS), const((1, ATTN_W)), const((1, ATTN_W)), const((1, LANES))],
                  out_shape=[_sds((s, Z_COLS), BF16), _sds((1, ATTN_W)), _sds((1, ATTN_W)), _sds((1, LANES))],
                  scratch_shapes=[pltpu.VMEM((1, LANES), F32)], dims=("arbitrary",))(z, dqn, dkn, dv, du, dc, gq, gk, bf, gg)


def _in_norm_bwd(x, g_mix, dh, dx1):
    s = x.shape[0]
    tm = _row_tile(s)

    def body(x_ref, g_ref, dh_ref, dx1_ref, dx_ref, dg_ref):
        i = pl.program_id(0)

        @pl.when(i == 0)
        def _():
            dg_ref[...] = jnp.zeros_like(dg_ref)

        dxn, dg = _rms_bwd(x_ref[...], g_ref[...], dh_ref[...])
        dx_ref[...] = dx1_ref[...] + dxn
        dg_ref[...] += dg

    row = pl.BlockSpec((tm, D_MODEL), lambda i: (i, 0))
    vec = pl.BlockSpec((1, D_MODEL), lambda i: (0, 0))
    return _pcall(body, name="in_norm_bwd", grid=(s // tm,), in_specs=[row, vec, row, row], out_specs=[row, vec],
                  out_shape=[_sds((s, D_MODEL)), _sds((1, D_MODEL))], dims=("arbitrary",))(x, g_mix, dh, dx1)


def _adamw(w, g, m, v, *, name):
    r, c = w.shape
    tr = r
    for cand in (256, 176, 128, 64):
        if r > cand and r % cand == 0:
            tr = cand
            break

    def body(w_ref, g_ref, m_ref, v_ref, d_ref, mo_ref, vo_ref):
        gv = g_ref[...]
        mn = ADAM_B1 * m_ref[...] + (1.0 - ADAM_B1) * gv
        vn = ADAM_B2 * v_ref[...] + (1.0 - ADAM_B2) * (gv * gv)
        m_hat = mn / (1.0 - ADAM_B1 ** ADAM_STEP)
        v_hat = vn / (1.0 - ADAM_B2 ** ADAM_STEP)
        d_ref[...] = -ADAM_LR * (m_hat / (jnp.sqrt(v_hat) + ADAM_EPS) + ADAM_WD * w_ref[...])
        mo_ref[...] = mn
        vo_ref[...] = vn

    spec = pl.BlockSpec((tr, c), lambda i: (i, 0))
    return _pcall(body, name=name, grid=(r // tr,), in_specs=[spec] * 4, out_specs=[spec] * 3,
                  out_shape=[_sds((r, c))] * 3, dims=("parallel",))(w, g, m, v)


_ADD_ROWS = HALF_ROWS // 3


def _add_sibling(grads, landed, core):
    nblk = HALF_ROWS // _ADD_ROWS

    def body(core_ref, g_ref, l_ref, o_ref):
        o_ref[...] = (g_ref[...] + l_ref[...]).astype(BF16)

    blk = (1, _ADD_ROWS, D_MODEL)
    grid_spec = pltpu.PrefetchScalarGridSpec(
        num_scalar_prefetch=1, grid=(N_CHIPS, nblk),
        in_specs=[pl.BlockSpec(blk, lambda j, i, core_ref: (j, core_ref[0] * nblk + i, 0)),
                  pl.BlockSpec(blk, lambda j, i, core_ref: (j, i, 0))],
        out_specs=pl.BlockSpec(blk, lambda j, i, core_ref: (j, i, 0)))
    return _pallas(body, name="add_sibling", grid_spec=grid_spec,
                   out_shape=_sds((N_CHIPS, HALF_ROWS, D_MODEL), BF16),
                   compiler_params=pltpu.CompilerParams(dimension_semantics=("parallel", "parallel"),
                                                        vmem_limit_bytes=VMEM_LIMIT))(core, grads, landed)


def _sum_chips(chip_sum, parts, place):
    nblk = HALF_ROWS // _ADD_ROWS

    def body(place_ref, own_ref, a_ref, b_ref, c_ref, o_ref):
        o_ref[...] = ((own_ref[0].astype(F32) + a_ref[0].astype(F32)) + b_ref[0].astype(F32)) + c_ref[0].astype(F32)

    blk = (1, _ADD_ROWS, D_MODEL)
    other = lambda k: pl.BlockSpec(blk, lambda i, place_ref: ((place_ref[0] + k) % N_CHIPS, i, 0))
    grid_spec = pltpu.PrefetchScalarGridSpec(
        num_scalar_prefetch=1, grid=(nblk,),
        in_specs=[other(0), other(1), other(2), other(3)],
        out_specs=pl.BlockSpec((_ADD_ROWS, D_MODEL), lambda i, place_ref: (place_ref[1] * nblk + i, 0)))
    return _pallas(body, name="sum_chips", grid_spec=grid_spec, out_shape=_sds((PACK_ROWS, D_MODEL)),
                   compiler_params=pltpu.CompilerParams(dimension_semantics=("parallel",),
                                                        vmem_limit_bytes=VMEM_LIMIT))(place, chip_sum, parts, parts, parts)


_HBM = pl.BlockSpec(memory_space=pltpu.HBM)


def _place():
    x, y, c = lax.axis_index("x"), lax.axis_index("y"), lax.axis_index("c")
    chips = [(1 - x, y), (x, 1 - y), (1 - x, 1 - y)]
    return x, y, c, chips


def _rcopy(src, dst, send_sem, recv_sem, to):
    return pltpu.make_async_remote_copy(src_ref=src, dst_ref=dst, send_sem=send_sem, recv_sem=recv_sem,
                                        device_id=to, device_id_type=MESH)


def _gather_shards(packed):
    def body(src_ref, out_ref, send_sems, recv_sems):
        x, y, c, chips = _place()
        sibling = (x, y, 1 - c)
        me = 2 * x + y

        def half(chip, hc):
            return out_ref.at[chip, pl.ds(hc * HALF_ROWS, HALF_ROWS), :]

        my_half = src_ref.at[pl.ds(c * HALF_ROWS, HALF_ROWS), :]
        first = [_rcopy(my_half, half(me, c), send_sems.at[k], recv_sems.at[k], (cx, cy, c))
                 for k, (cx, cy) in enumerate(chips)]
        for cp in first:
            cp.start()
        passed = []
        for k, (cx, cy) in enumerate(chips):
            landed = half(2 * cx + cy, c)
            _rcopy(landed, landed, send_sems.at[k], recv_sems.at[k], sibling).wait_recv()
            fwd = _rcopy(landed, landed, send_sems.at[3 + k], recv_sems.at[3 + k], sibling)
            fwd.start()
            passed.append(fwd)
        for k, (cx, cy) in enumerate(chips):
            other = half(2 * cx + cy, 1 - c)
            _rcopy(other, other, send_sems.at[3 + k], recv_sems.at[3 + k], sibling).wait_recv()
        for cp in first + passed:
            cp.wait_send()

    return _pallas(body, name="gather_shards", in_specs=[_HBM], out_specs=_HBM,
                   out_shape=_sds((N_CHIPS, PACK_ROWS, D_MODEL), BF16),
                   scratch_shapes=[pltpu.SemaphoreType.DMA((6,)), pltpu.SemaphoreType.DMA((6,))])(packed)


def _swap_halves(grads):
    def body(g_ref, land_ref, send_sem, recv_sem):
        x, y, c, _ = _place()
        theirs = g_ref.at[:, pl.ds((1 - c) * HALF_ROWS, HALF_ROWS), :]
        cp = _rcopy(theirs, land_ref, send_sem, recv_sem, (x, y, 1 - c))
        cp.start()
        cp.wait()

    return _pallas(body, name="swap_halves", in_specs=[_HBM], out_specs=_HBM,
                          out_shape=_sds((N_CHIPS, HALF_ROWS, D_MODEL)),
                          scratch_shapes=[pltpu.SemaphoreType.DMA, pltpu.SemaphoreType.DMA])(grads)


def _scatter_chips(chip_sum):
    def body(p_ref, land_ref, send_sems, recv_sems):
        x, y, c, chips = _place()
        me = 2 * x + y
        sends = [_rcopy(p_ref.at[2 * cx + cy], land_ref.at[me], send_sems.at[k], recv_sems.at[k], (cx, cy, c))
                 for k, (cx, cy) in enumerate(chips)]
        for cp in sends:
            cp.start()
        for k, (cx, cy) in enumerate(chips):
            slot = land_ref.at[2 * cx + cy]
            _rcopy(slot, slot, send_sems.at[k], recv_sems.at[k], (cx, cy, c)).wait_recv()
        for cp in sends:
            cp.wait_send()

    return _pallas(body, name="scatter_chips", in_specs=[_HBM], out_specs=_HBM,
                   out_shape=_sds((N_CHIPS, HALF_ROWS, D_MODEL), chip_sum.dtype),
                   scratch_shapes=[pltpu.SemaphoreType.DMA((3,)), pltpu.SemaphoreType.DMA((3,))])(chip_sum)


def _join_halves(red):
    def body(in_ref, out_ref, send_sem, recv_sem):
        x, y, c, _ = _place()
        mine = out_ref.at[pl.ds(c * HALF_ROWS, HALF_ROWS), :]
        theirs = out_ref.at[pl.ds((1 - c) * HALF_ROWS, HALF_ROWS), :]
        cp = _rcopy(mine, mine, send_sem, recv_sem, (x, y, 1 - c))
        cp.start()
        _rcopy(theirs, theirs, send_sem, recv_sem, (x, y, 1 - c)).wait_recv()
        cp.wait_send()

    return _pallas(body, name="join_halves", in_specs=[_HBM], out_specs=_HBM, out_shape=_sds((PACK_ROWS, D_MODEL)),
                   input_output_aliases={0: 0},
                   scratch_shapes=[pltpu.SemaphoreType.DMA, pltpu.SemaphoreType.DMA])(red)


def _allreduce_small(v):
    m_per = v.shape[0]

    def body(v_ref, out_ref, all_ref, send_sems, recv_sems, local_sem):
        x, y, c, chips = _place()
        me, sibling = (x, y, c), (x, y, 1 - c)

        def rows(px, py, pc):
            return all_ref.at[pl.ds((4 * px + 2 * py + pc) * m_per, m_per), :]

        def copy(k, block, to, src=None):
            return _rcopy(rows(*block) if src is None else src, rows(*block), send_sems.at[k], recv_sems.at[k], to)

        mine = pltpu.make_async_copy(v_ref, rows(*me), local_sem)
        mine.start()
        first = [copy(0, me, sibling, src=v_ref)]
        first += [copy(1 + k, me, (*chip, c), src=v_ref) for k, chip in enumerate(chips)]
        for cp in first:
            cp.start()
        passed = [copy(4 + k, (*chip, c), sibling) for k, chip in enumerate(chips)]
        for k, chip in enumerate(chips):
            copy(1 + k, (*chip, c), me).wait_recv()
            passed[k].start()
        copy(0, sibling, me).wait_recv()
        for k, chip in enumerate(chips):
            copy(4 + k, (*chip, 1 - c), me).wait_recv()
        for cp in first + passed:
            cp.wait_send()
        mine.wait()
        acc = all_ref[pl.ds(0, m_per), :]
        for d in range(1, 8):
            acc = acc + all_ref[pl.ds(d * m_per, m_per), :]
        out_ref[...] = acc

    vm = pl.BlockSpec(memory_space=pltpu.VMEM)
    return _pallas(body, name="allreduce_small", in_specs=[vm], out_specs=vm, out_shape=_sds((m_per, LANES)),
                          scratch_shapes=[pltpu.VMEM((8 * m_per, LANES), F32), pltpu.SemaphoreType.DMA((7,)),
                                          pltpu.SemaphoreType.DMA((7,)), pltpu.SemaphoreType.DMA],
                          compiler_params=pltpu.CompilerParams(vmem_limit_bytes=VMEM_LIMIT))(v)


def _to_heads(t):
    s = t.shape[0]
    return t.reshape(s, HEADS, HEAD_DIM).transpose(1, 0, 2)


def _from_heads(t):
    s = t.shape[1]
    return t.transpose(1, 0, 2).reshape(s, HEADS * HEAD_DIM)


def _reorder_in_cols(w):
    pad = jnp.zeros((w.shape[0], Z_COLS - IN_COLS), w.dtype)
    return jnp.concatenate([w[:, :3 * ATTN_W], w[:, 3 * ATTN_W + HEADS:], w[:, 3 * ATTN_W:3 * ATTN_W + HEADS], pad], axis=1)


def _restore_in_cols(w):
    return jnp.concatenate([w[:, :3 * ATTN_W], w[:, F_COL0:F_COL0 + HEADS], w[:, U_COL0:U_COL0 + SSM_W]], axis=1)


def _block_diag(blocks):
    j, g, a, b = blocks.shape
    eye = jnp.eye(g, dtype=bool)[None, :, None, :, None]
    return jnp.where(eye, blocks[:, :, :, None, :], jnp.zeros((), blocks.dtype)).reshape(j, g * a, g * b)


def _diag_blocks(m, a, b):
    j = m.shape[0]
    g = m.shape[1] // a
    t = m.reshape(j, g, a, g, b)
    eye = jnp.eye(g, dtype=bool)[None, :, None, :, None]
    return jnp.sum(jnp.where(eye, t, 0.0), axis=3)


def _pack_rows(parts, rows, dtype):
    used = sum(p.shape[0] for p in parts)
    return jnp.concatenate([p.astype(dtype) for p in parts] + [jnp.zeros((rows - used, D_MODEL), dtype)], axis=0)


_SMALL = (("g_mix", (1024,)), ("b_f", (8,)), ("g_q", (64,)), ("g_k", (64,)), ("lambda_re", (32, 64)),
          ("lambda_im", (32, 64)), ("log_step", (32,)), ("b_re", (32, 64, 16)), ("b_im", (32, 64, 16)),
          ("c_re", (32, 16, 64)), ("c_im", (32, 16, 64)), ("d_skip", (32, 16)), ("b_glu", (512,)),
          ("g_attn_out", (512,)), ("g_ssm_out", (512,)), ("g_ffn", (1024,)), ("conv_b", (5632,)))


def _small_rows(shape):
    return -(-math.prod(shape) // LANES)


def _pack_small(arrs, extra=()):
    parts = []
    for a in list(arrs) + list(extra):
        flat = a.reshape(-1)
        rows = -(-flat.shape[0] // LANES)
        parts.append(jnp.pad(flat, (0, rows * LANES - flat.shape[0])).reshape(rows, LANES))
    total = sum(p.shape[0] for p in parts)
    pad = -total % SUBLANES
    if pad:
        parts.append(jnp.zeros((pad, LANES), F32))
    return jnp.concatenate(parts, axis=0)


def _unpack_small(buf, shapes):
    out, r = [], 0
    for shape in shapes:
        n = math.prod(shape)
        rows = -(-n // LANES)
        out.append(buf[r:r + rows].reshape(-1)[:n].reshape(shape))
        r += rows
    return out


def _local_step(x, tgt, w_in_r, w_glu_b, w_out_b, w_up_b, w_down_b, conv_w_full, p):
    s = x.shape[0]
    row = lambda v: v.reshape(1, -1)
    g_mix, g_ffn = row(p["g_mix"]), row(p["g_ffn"])
    g_att, g_ssm, b_glu, conv_b = row(p["g_attn_out"]), row(p["g_ssm_out"]), row(p["b_glu"]), row(p["conv_b"])
    gq = row(jnp.tile(p["g_q"], HEADS))
    gk = row(jnp.tile(p["g_k"], HEADS))
    bf = row(jnp.pad(p["b_f"], (0, LANES - HEADS)))
    gg = jnp.kron(jnp.eye(HEADS, dtype=F32), jnp.ones((HEAD_DIM, HEAD_DIM), F32)).astype(BF16)
    dsk = row(p["d_skip"])

    rep = lambda a: jnp.repeat(a, SSM_GROUP, axis=0)
    lr, li = rep(p["lambda_re"]), rep(p["lambda_im"])
    ls = rep(jnp.broadcast_to(p["log_step"][:, None], (SSM_GROUPS, SSM_STATE)))
    bt_re = p["b_re"].transpose(0, 2, 1).reshape(_PARAM_SHAPE)
    bt_im = p["b_im"].transpose(0, 2, 1).reshape(_PARAM_SHAPE)
    a_re_rep, a_im_rep, bb_re, bb_im = _ssm_params(lr, li, ls, bt_re, bt_im)
    ar = a_re_rep[::SSM_GROUP].reshape(SSM_CHUNKS, 1, CHUNK_S)
    ai = a_im_rep[::SSM_GROUP].reshape(SSM_CHUNKS, 1, CHUNK_S)
    chunked = lambda t: t.reshape(SSM_CHUNKS, SSM_GROUPS // SSM_CHUNKS, SSM_GROUP, SSM_STATE)
    bbr = _block_diag(chunked(bb_re)).astype(BF16)
    bbi = _block_diag(chunked(bb_im)).astype(BF16)
    to_cc = lambda c: _block_diag(chunked(c).transpose(0, 1, 3, 2)).astype(BF16)
    ccr, cci = to_cc(p["c_re"]), to_cc(p["c_im"])

    hb, z = _in_proj(x, g_mix, w_in_r)
    qn, kn, vb, ub, c128 = _attn_prep(z, gq, gk, bf, gg)
    qh, kh, vh = _to_heads(qn), _to_heads(kn), _to_heads(vb)
    crow = c128[:, :HEADS].T.reshape(HEADS, 1, s)
    oh, lse = _attn_fwd(qh, kh, vh, crow)
    att = _from_heads(oh)
    xr, xi, y = _ssm_fwd(ub, z, bbr, bbi, ar, ai, ccr, cci, dsk)
    x1, mixb, h2b = _mix_out(y, att, x, w_glu_b, b_glu, g_att, g_ssm, w_out_b, g_ffn)
    up = _mm(h2b, w_up_b, name="ffn_up", tm=1024, tn=1408, tk=1024)
    act = _conv_act(up, conv_w_full, conv_b)
    dy, dyb, loss_blk = _down_loss(act, w_down_b, x1, tgt)

    d_w_down = _mm(act, dyb, ta=True, name="d_w_down", tm=1408, tn=1024, tk=2048)
    dact = _mm(dyb, w_down_b, tb=True, name="d_act", tm=1024, tn=1408, tk=1024)
    dupb, dcw = _conv_act_bwd(up, dact, conv_w_full, conv_b)
    d_w_up = _mm(h2b, dupb, ta=True, name="d_w_up", tm=1024, tn=1408, tk=2048)
    dh2 = _mm(dupb, w_up_b, tb=True, name="d_h2", tm=1024, tn=1024, tk=1408)
    dx1, dx1b, datt, dys, d_w_glu, d_g_ffn, d_g_att, d_g_ssm, d_b_glu = _mix_bwd(
        dy, dh2, x1, g_ffn, w_out_b, y, att, w_glu_b, b_glu, g_att, g_ssm)
    d_w_out = _mm(mixb, dx1b, ta=True, name="d_w_out", tm=1024, tn=1024, tk=2048)
    du, dbbr, dbbi, dccr, dcci, dar, dai, dd = _ssm_bwd(dys, z, ub, xr, xi, bbr, bbi, ar, ai, ccr, cci, dsk)
    doh = _to_heads(datt)
    dqh, dkh, dvh, dcrow = _attn_bwd(qh, kh, vh, crow, _attn_dl(qh, kh, vh, crow, lse, doh), lse, doh)
    dc128 = jnp.pad(dcrow.reshape(HEADS, s).T, ((0, 0), (0, LANES - HEADS)))
    dzb, d_gq, d_gk, d_bf = _prep_bwd(z, _from_heads(dqh), _from_heads(dkh), _from_heads(dvh), du, dc128, gq, gk, bf, gg)
    d_w_in_r = _mm(hb, dzb, ta=True, name="d_w_in", tm=512, tn=Z_COLS, tk=2048)
    dh = _mm(dzb, w_in_r, tb=True, name="d_h", tm=1024, tn=1024, tk=Z_COLS)
    dx, d_g_mix = _in_norm_bwd(x, g_mix, dh, dx1)

    unchunk = lambda t: t.reshape(_PARAM_SHAPE)
    dbb_re = unchunk(_diag_blocks(dbbr, SSM_GROUP, SSM_STATE))
    dbb_im = unchunk(_diag_blocks(dbbi, SSM_GROUP, SSM_STATE))
    first_row = (jnp.arange(_PARAM_SHAPE[0]) % SSM_GROUP == 0)[:, None]
    da_re = jnp.where(first_row, rep(dar.reshape(SSM_GROUPS, SSM_STATE)), 0.0)
    da_im = jnp.where(first_row, rep(dai.reshape(SSM_GROUPS, SSM_STATE)), 0.0)
    expand_t = (jnp.arange(SSM_GROUPS)[:, None] == (jnp.arange(_PARAM_SHAPE[0]) // SSM_GROUP)[None, :]).astype(BF16)
    d_lr, d_li, d_ls, d_bt_re, d_bt_im = _ssm_params_bwd(lr, li, ls, bt_re, bt_im, da_re, da_im, dbb_re, dbb_im, expand_t)
    from_bt = lambda t: t.reshape(SSM_GROUPS, SSM_GROUP, SSM_STATE).transpose(0, 2, 1)
    from_cc = lambda t: _diag_blocks(t, SSM_STATE, SSM_GROUP).transpose(0, 1, 3, 2).reshape(SSM_GROUPS, SSM_GROUP, SSM_STATE)

    small = {
        "g_mix": d_g_mix, "b_f": d_bf[0, :HEADS], "g_q": d_gq.reshape(HEADS, HEAD_DIM).sum(0),
        "g_k": d_gk.reshape(HEADS, HEAD_DIM).sum(0), "lambda_re": d_lr, "lambda_im": d_li, "log_step": d_ls,
        "b_re": from_bt(d_bt_re), "b_im": from_bt(d_bt_im), "c_re": from_cc(dccr), "c_im": from_cc(dcci),
        "d_skip": dd, "b_glu": d_b_glu, "g_attn_out": d_g_att, "g_ssm_out": d_g_ssm, "g_ffn": d_g_ffn,
        "conv_b": dcw[3],
    }
    big = {"w_in": _restore_in_cols(d_w_in_r), "w_glu": d_w_glu, "w_out": d_w_out, "w_up": d_w_up, "w_down": d_w_down}
    return loss_blk[0, 0], dx, big, small, dcw[0:3]


def kernel(x, g_mix, w_in, b_f, g_q, g_k, lambda_re, lambda_im, log_step, b_re, b_im, c_re, c_im, d_skip, w_glu, b_glu, g_attn_out, g_ssm_out, w_out, g_ffn, w_up, conv_w, conv_b, w_down, loss_target, m_g_mix, m_w_in, m_b_f, m_g_q, m_g_k, m_lambda_re, m_lambda_im, m_log_step, m_b_re, m_b_im, m_c_re, m_c_im, m_d_skip, m_w_glu, m_b_glu, m_g_attn_out, m_g_ssm_out, m_w_out, m_g_ffn, m_w_up, m_conv_w, m_conv_b, m_w_down, v_g_mix, v_w_in, v_b_f, v_g_q, v_g_k, v_lambda_re, v_lambda_im, v_log_step, v_b_re, v_b_im, v_c_re, v_c_im, v_d_skip, v_w_glu, v_b_glu, v_g_attn_out, v_g_ssm_out, v_w_out, v_g_ffn, v_w_up, v_conv_w, v_conv_b, v_w_down):
    args = dict(locals())
    order = ["g_mix", "w_in", "b_f", "g_q", "g_k", "lambda_re", "lambda_im", "log_step", "b_re", "b_im", "c_re", "c_im",
             "d_skip", "w_glu", "b_glu", "g_attn_out", "g_ssm_out", "w_out", "g_ffn", "w_up", "conv_w", "conv_b", "w_down"]
    cx, cy, cc = lax.axis_index("x"), lax.axis_index("y"), lax.axis_index("c")
    chip = 2 * cx + cy

    convw_bits = lax.bitcast_convert_type(conv_w, BF16).reshape(-1)
    convw_rows = jnp.pad(convw_bits, (0, CONVW_ROWS * D_MODEL - convw_bits.shape[0])).reshape(CONVW_ROWS, D_MODEL)
    packed = _pack_rows([w_in.reshape(ROWS_IN, D_MODEL), w_glu.reshape(ROWS_GLU, D_MODEL), w_out,
                         w_up.reshape(ROWS_UP, D_MODEL), w_down, convw_rows], PACK_ROWS, BF16)
    wg = lax.dynamic_update_slice(_gather_shards(packed), packed[None], (chip, 0, 0))
    cols = lambda t, n: t.reshape(N_CHIPS, D_MODEL, n).transpose(1, 0, 2).reshape(D_MODEL, N_CHIPS * n)
    w_in_r = _reorder_in_cols(cols(wg[:, 0:OFF_GLU], IN_COLS // N_CHIPS))
    w_glu_b = wg[:, OFF_GLU:OFF_OUT].reshape(SSM_W, SSM_W)
    w_out_b = wg[:, OFF_OUT:OFF_UP].reshape(D_MODEL, D_MODEL)
    w_up_b = cols(wg[:, OFF_UP:OFF_DOWN], 2 * D_FF // N_CHIPS)
    w_down_b = wg[:, OFF_DOWN:OFF_SPARE].reshape(D_FF, D_MODEL)
    cw_bits = wg[:, OFF_SPARE:OFF_SPARE + CONVW_ROWS].reshape(N_CHIPS, -1)[:, :3 * 1408 * 2].reshape(N_CHIPS, 3, 1408, 2)
    conv_w_full = lax.bitcast_convert_type(cw_bits, F32).transpose(1, 0, 2).reshape(3, 2 * D_FF)

    loss_part, dx, big, small, d_conv_w = _local_step(x[0], loss_target[0], w_in_r, w_glu_b, w_out_b, w_up_b, w_down_b,
                                                      conv_w_full, args)
    loss = lax.psum(loss_part, ("x", "y", "c"))

    shard_cols = lambda t: t.reshape(D_MODEL, N_CHIPS, -1).transpose(1, 0, 2)
    grads = jnp.concatenate([
        shard_cols(big["w_in"]).reshape(N_CHIPS, ROWS_IN, D_MODEL),
        big["w_glu"].reshape(N_CHIPS, ROWS_GLU, D_MODEL),
        big["w_out"].reshape(N_CHIPS, ROWS_OUT, D_MODEL),
        shard_cols(big["w_up"]).reshape(N_CHIPS, ROWS_UP, D_MODEL),
        big["w_down"].reshape(N_CHIPS, ROWS_DOWN, D_MODEL),
        jnp.zeros((N_CHIPS, PACK_ROWS - OFF_SPARE, D_MODEL), F32)], axis=1)
    landed = _swap_halves(grads)
    chip_sum = _add_sibling(grads, landed, cc.reshape(1).astype(jnp.int32))
    parts = _scatter_chips(chip_sum)
    red = _join_halves(_sum_chips(chip_sum, parts, jnp.stack([chip, cc]).astype(jnp.int32)))
    g_big = {"w_in": red[0:OFF_GLU].reshape(D_MODEL, IN_COLS // N_CHIPS),
             "w_glu": red[OFF_GLU:OFF_OUT].reshape(SSM_W // N_CHIPS, SSM_W),
             "w_out": red[OFF_OUT:OFF_UP],
             "w_up": red[OFF_UP:OFF_DOWN].reshape(D_MODEL, 2 * D_FF // N_CHIPS),
             "w_down": red[OFF_DOWN:OFF_SPARE]}

    small_names = [n for n, _ in _SMALL]
    small_shapes = [sh for _, sh in _SMALL]
    gsum = _allreduce_small(_pack_small([small[n] for n in small_names], extra=[d_conv_w]))
    g_small = _unpack_small(gsum, small_shapes + [(3, 2 * D_FF)])
    g_conv_w = lax.dynamic_slice_in_dim(g_small[-1], chip * (2 * D_FF // N_CHIPS), 2 * D_FF // N_CHIPS, axis=1)
    g_small = dict(zip(small_names, g_small[:-1]))

    grad, delta, new_m, new_v = {}, {}, {}, {}
    for n in ("w_in", "w_glu", "w_out", "w_up", "w_down"):
        grad[n] = g_big[n]
        delta[n], new_m[n], new_v[n] = _adamw(args[n], g_big[n], args["m_" + n], args["v_" + n], name="adamw_" + n)
    grad["conv_w"] = g_conv_w
    delta["conv_w"], new_m["conv_w"], new_v["conv_w"] = _adamw(conv_w, g_conv_w, m_conv_w, v_conv_w, name="adamw_conv_w")
    ws = _pack_small([args[n] for n in small_names])
    ms = _pack_small([args["m_" + n] for n in small_names])
    vs = _pack_small([args["v_" + n] for n in small_names])
    gs = _pack_small([g_small[n] for n in small_names])
    ds_, mn_, vn_ = _adamw(ws, gs, ms, vs, name="adamw_small")
    for n, d_, m_, v_ in zip(small_names, _unpack_small(ds_, small_shapes), _unpack_small(mn_, small_shapes),
                             _unpack_small(vn_, small_shapes)):
        grad[n], delta[n], new_m[n], new_v[n] = g_small[n], d_, m_, v_

    return (loss, dx[None], *[grad[n] for n in order], *[delta[n] for n in order], *[new_m[n] for n in order],
            *[new_v[n] for n in order])
```

```python
import math

import jax
import jax.numpy as jnp
from jax import lax
from jax.experimental import pallas as pl
from jax.experimental.pallas import tpu as pltpu

F32 = jnp.float32
BF16 = jnp.bfloat16

D_MODEL = 1024
HEADS = 8
HEAD_DIM = 64
ATTN_W = 512
SSM_W = 512
SSM_GROUPS = 32
SSM_GROUP = 16
SSM_STATE = 64
N_STATE = SSM_GROUPS * SSM_STATE
D_FF = 2816
IN_COLS = 2056
Z_COLS = 2176
U_COL0 = 1536
F_COL0 = 2048
EPS = 1e-6
NEG_INF = -1e30
N_CHIPS = 4
LANES = 128
SUBLANES = 8
SSM_CHUNKS = 4
CHUNK_U = SSM_W // SSM_CHUNKS
CHUNK_S = N_STATE // SSM_CHUNKS
STRIP = 128
N_STRIPS = D_FF // STRIP

ROWS_IN, ROWS_GLU, ROWS_OUT, ROWS_UP, ROWS_DOWN = 514, 64, 256, 1408, 704
OFF_GLU = ROWS_IN
OFF_OUT = OFF_GLU + ROWS_GLU
OFF_UP = OFF_OUT + ROWS_OUT
OFF_DOWN = OFF_UP + ROWS_UP
OFF_SPARE = OFF_DOWN + ROWS_DOWN
PACK_ROWS = 2976
HALF_ROWS = PACK_ROWS // 2
CONVW_ROWS = 9

ADAM_LR = 0.001
ADAM_B1 = 0.9
ADAM_B2 = 0.999
ADAM_EPS = 1e-08
ADAM_WD = 0.01
ADAM_STEP = 10

VMEM_LIMIT = 56 * 1024 * 1024
MESH = pl.DeviceIdType.MESH


def _pallas(body, **kw):
    return pl.pallas_call(body, **kw)


def _pcall(body, *, name, out_shape, in_specs, out_specs, grid=(), scratch_shapes=(), dims=None):
    params = pltpu.CompilerParams(dimension_semantics=dims, vmem_limit_bytes=VMEM_LIMIT)
    return _pallas(body, name=name, grid=grid, in_specs=in_specs, out_specs=out_specs,
                   out_shape=out_shape, scratch_shapes=scratch_shapes, compiler_params=params)


def _sds(shape, dtype=F32):
    return jax.ShapeDtypeStruct(shape, dtype)


def _dot(a, b):
    return jnp.dot(a, b, preferred_element_type=F32)


def _dot_nt(a, b):
    return lax.dot_general(a, b, (((1,), (1,)), ((), ())), preferred_element_type=F32)


def _dot_tn(a, b):
    return lax.dot_general(a, b, (((0,), (0,)), ((), ())), preferred_element_type=F32)


def _split3(x):
    hi = x.astype(BF16)
    r = x - hi.astype(F32)
    mid = r.astype(BF16)
    lo = (r - mid.astype(F32)).astype(BF16)
    return hi, mid, lo


def _dot_exact_r(x, m01):
    hi, mid, lo = _split3(x)
    return _dot(hi, m01) + _dot(mid, m01) + _dot(lo, m01)


def _dot_exact_l(m01, x):
    hi, mid, lo = _split3(x)
    return _dot(m01, hi) + _dot(m01, mid) + _dot(m01, lo)


def _sigmoid(x):
    return 1.0 / (1.0 + jnp.exp(-x))


def _rms(x, g):
    r = lax.rsqrt(jnp.mean(x * x, axis=-1, keepdims=True) + EPS)
    return x * r * g


def _rms_bwd(x, g, dy):
    r = lax.rsqrt(jnp.mean(x * x, axis=-1, keepdims=True) + EPS)
    w = dy * g
    dx = r * w - x * (r * r * r) * jnp.mean(w * x, axis=-1, keepdims=True)
    dg = jnp.sum(dy * x * r, axis=0, keepdims=True)
    return dx, dg


_GELU_K = math.sqrt(2.0 / math.pi)
_GELU_C = 0.044715


def _gelu(y):
    return y * (0.5 * (1.0 + jnp.tanh(_GELU_K * (y + _GELU_C * (y * y * y)))))


def _gelu_grad(y):
    t = jnp.tanh(_GELU_K * (y + _GELU_C * (y * y * y)))
    return 0.5 * (1.0 + t) + 0.5 * y * (1.0 - t * t) * (_GELU_K * (1.0 + 3.0 * _GELU_C * y * y))


def _tile(n, pref):
    if n <= pref:
        return n
    divs = [t for t in range(LANES, n + 1, LANES) if n % t == 0]
    below = [t for t in divs if t <= pref]
    if below and 2 * below[-1] >= pref:
        return below[-1]
    above = [t for t in divs if t > pref]
    return above[0] if above else n


def _row_tile(s):
    return min(256, s)


def _mm(a, b, *, name, tm, tn, tk, ta=False, tb=False):
    if ta:
        kk, m = a.shape
    else:
        m, kk = a.shape
    n = b.shape[0] if tb else b.shape[1]
    tm, tn, tk = _tile(m, tm), _tile(n, tn), _tile(kk, tk)

    def body(a_ref, b_ref, o_ref):
        k = pl.program_id(2)
        if ta:
            part = _dot_tn(a_ref[...], b_ref[...])
        elif tb:
            part = _dot_nt(a_ref[...], b_ref[...])
        else:
            part = _dot(a_ref[...], b_ref[...])

        @pl.when(k == 0)
        def _():
            o_ref[...] = part

        @pl.when(k > 0)
        def _():
            o_ref[...] += part

    a_spec = pl.BlockSpec((tk, tm), lambda i, j, k: (k, i)) if ta else pl.BlockSpec((tm, tk), lambda i, j, k: (i, k))
    b_spec = pl.BlockSpec((tn, tk), lambda i, j, k: (j, k)) if tb else pl.BlockSpec((tk, tn), lambda i, j, k: (k, j))
    return _pcall(body, name=name, grid=(m // tm, n // tn, kk // tk), in_specs=[a_spec, b_spec],
                  out_specs=pl.BlockSpec((tm, tn), lambda i, j, k: (i, j)), out_shape=_sds((m, n)),
                  dims=("parallel", "parallel", "arbitrary"))(a, b)


def _in_proj(x, g_mix, w_in_r):
    s = x.shape[0]
    tm = _row_tile(s)

    def body(x_ref, g_ref, w_ref, h_ref, z_ref):
        h = _rms(x_ref[...], g_ref[...]).astype(BF16)
        h_ref[...] = h
        z_ref[...] = _dot(h, w_ref[...])

    return _pcall(body, name="in_proj", grid=(s // tm,),
                  in_specs=[pl.BlockSpec((tm, D_MODEL), lambda i: (i, 0)), pl.BlockSpec((1, D_MODEL), lambda i: (0, 0)),
                            pl.BlockSpec((D_MODEL, Z_COLS), lambda i: (0, 0))],
                  out_specs=[pl.BlockSpec((tm, D_MODEL), lambda i: (i, 0)), pl.BlockSpec((tm, Z_COLS), lambda i: (i, 0))],
                  out_shape=[_sds((s, D_MODEL), BF16), _sds((s, Z_COLS))], dims=("parallel",))(x, g_mix, w_in_r)


def _attn_prep(z, gq, gk, bf, gg):
    s = z.shape[0]
    tm = _row_tile(s)

    def body(z_ref, gq_ref, gk_ref, bf_ref, gg_ref, qn_ref, kn_ref, vb_ref, ub_ref, c_ref, carry_ref):
        i = pl.program_id(0)

        @pl.when(i == 0)
        def _():
            carry_ref[...] = jnp.zeros_like(carry_ref)

        gg_m = gg_ref[...]

        def head_norm(t, g):
            ssq = _dot_exact_r(t * t, gg_m)
            return t * lax.rsqrt(ssq * (1.0 / HEAD_DIM) + EPS) * g

        qn_ref[...] = head_norm(z_ref[:, 0:ATTN_W], gq_ref[...]).astype(BF16)
        kn_ref[...] = head_norm(z_ref[:, ATTN_W:2 * ATTN_W], gk_ref[...]).astype(BF16)
        vb_ref[...] = z_ref[:, 2 * ATTN_W:3 * ATTN_W].astype(BF16)
        ub_ref[...] = z_ref[:, U_COL0:U_COL0 + SSM_W].astype(BF16)
        fl = z_ref[:, F_COL0:F_COL0 + LANES] + bf_ref[...]
        lf = jnp.minimum(fl, 0.0) - jnp.log1p(jnp.exp(-jnp.abs(fl)))
        row = lax.broadcasted_iota(jnp.int32, (tm, tm), 0)
        col = lax.broadcasted_iota(jnp.int32, (tm, tm), 1)
        tri = (row >= col).astype(BF16)
        c = _dot_exact_l(tri, lf) + carry_ref[...]
        c_ref[...] = c
        carry_ref[...] = c[tm - 1:tm, :]

    row_spec = lambda w: pl.BlockSpec((tm, w), lambda i: (i, 0))
    const = lambda shape: pl.BlockSpec(shape, lambda i: (0, 0))
    return _pcall(body, name="attn_prep", grid=(s // tm,),
                  in_specs=[row_spec(Z_COLS), const((1, ATTN_W)), const((1, ATTN_W)), const((1, LANES)), const((ATTN_W, ATTN_W))],
                  out_specs=[row_spec(ATTN_W)] * 4 + [row_spec(LANES)],
                  out_shape=[_sds((s, ATTN_W), BF16)] * 4 + [_sds((s, LANES))],
                  scratch_shapes=[pltpu.VMEM((1, LANES), F32)], dims=("arbitrary",))(z, gq, gk, bf, gg)


def _attn_fwd(qh, kh, vh, crow):
    _, s, _ = qh.shape
    tq = _row_tile(s)
    scale = HEAD_DIM ** -0.5

    def body(q_ref, k_ref, v_ref, c_ref, o_ref, lse_ref):
        i = pl.program_id(1)
        q = q_ref[0]

        def block(j, carry, diagonal):
            m, l, acc = carry
            off = pl.multiple_of(j * tq, tq)
            ks = k_ref[0, pl.ds(off, tq), :]
            vs = v_ref[0, pl.ds(off, tq), :]
            cs = c_ref[0, :, pl.ds(off, tq)]
            sc = _dot_nt(q, ks) * scale - cs
            if diagonal:
                causal = lax.broadcasted_iota(jnp.int32, (tq, tq), 1) <= lax.broadcasted_iota(jnp.int32, (tq, tq), 0)
                sc = jnp.where(causal, sc, NEG_INF)
            m_new = jnp.maximum(m, jnp.max(sc, axis=-1, keepdims=True))
            p = jnp.exp(sc - m_new)
            alpha = jnp.exp(m - m_new)
            l = alpha * l + jnp.sum(p, axis=-1, keepdims=True)
            acc = alpha * acc + _dot(p.astype(BF16), vs)
            return m_new, l, acc

        init = (jnp.full((tq, 1), NEG_INF, F32), jnp.zeros((tq, 1), F32), jnp.zeros((tq, HEAD_DIM), F32))
        carry = lax.fori_loop(0, i, lambda j, c: block(j, c, False), init)
        m, l, acc = block(i, carry, True)
        o_ref[0] = acc / l
        lse_ref[0] = m + jnp.log(l)

    blk = pl.BlockSpec((1, tq, HEAD_DIM), lambda h, i: (h, i, 0))
    full = pl.BlockSpec((1, s, HEAD_DIM), lambda h, i: (h, 0, 0))
    return _pcall(body, name="attn_fwd", grid=(HEADS, s // tq),
                  in_specs=[blk, full, full, pl.BlockSpec((1, 1, s), lambda h, i: (h, 0, 0))],
                  out_specs=[blk, pl.BlockSpec((1, tq, 1), lambda h, i: (h, i, 0))],
                  out_shape=[_sds((HEADS, s, HEAD_DIM)), _sds((HEADS, s, 1))],
                  dims=("parallel", "parallel"))(qh, kh, vh, crow)


def _ssm_param_fn(lr, li, ls, br, bi):
    step = jnp.exp(ls)
    er = jnp.exp(lr * step)
    ab_re = er * jnp.cos(li * step)
    ab_im = er * jnp.sin(li * step)
    num_re = ab_re - 1.0
    num_im = ab_im
    den = lr * lr + li * li
    f_re = (num_re * lr + num_im * li) / den
    f_im = (num_im * lr - num_re * li) / den
    bb_re = f_re * br - f_im * bi
    bb_im = f_re * bi + f_im * br
    return ab_re, ab_im, bb_re, bb_im


_PARAM_SHAPE = (SSM_GROUPS * SSM_GROUP, SSM_STATE)


def _ssm_params(lr, li, ls, br, bi):
    def body(lr_ref, li_ref, ls_ref, br_ref, bi_ref, ar_ref, ai_ref, bbr_ref, bbi_ref):
        ar, ai, bbr, bbi = _ssm_param_fn(lr_ref[...], li_ref[...], ls_ref[...], br_ref[...], bi_ref[...])
        ar_ref[...] = ar
        ai_ref[...] = ai
        bbr_ref[...] = bbr
        bbi_ref[...] = bbi

    spec = pl.BlockSpec(_PARAM_SHAPE, lambda: (0, 0))
    return _pcall(body, name="ssm_params", in_specs=[spec] * 5, out_specs=[spec] * 4,
                  out_shape=[_sds(_PARAM_SHAPE)] * 4)(lr, li, ls, br, bi)


def _ssm_params_bwd(lr, li, ls, br, bi, dar, dai, dbbr, dbbi, expand_t):
    def body(lr_ref, li_ref, ls_ref, br_ref, bi_ref, dar_ref, dai_ref, dbbr_ref, dbbi_ref, et_ref,
             dlr_ref, dli_ref, dls_ref, dbr_ref, dbi_ref):
        _, vjp = jax.vjp(_ssm_param_fn, lr_ref[...], li_ref[...], ls_ref[...], br_ref[...], bi_ref[...])
        dlr, dli, dls, dbr, dbi = vjp((dar_ref[...], dai_ref[...], dbbr_ref[...], dbbi_ref[...]))
        et = et_ref[...]
        dlr_ref[...] = _dot_exact_l(et, dlr)
        dli_ref[...] = _dot_exact_l(et, dli)
        dls_ref[...] = jnp.sum(_dot_exact_l(et, dls), axis=-1, keepdims=True)
        dbr_ref[...] = dbr
        dbi_ref[...] = dbi

    spec = pl.BlockSpec(_PARAM_SHAPE, lambda: (0, 0))
    gspec = pl.BlockSpec((SSM_GROUPS, SSM_STATE), lambda: (0, 0))
    return _pcall(body, name="ssm_params_bwd",
                  in_specs=[spec] * 9 + [pl.BlockSpec((SSM_GROUPS, _PARAM_SHAPE[0]), lambda: (0, 0))],
                  out_specs=[gspec, gspec, pl.BlockSpec((SSM_GROUPS, 1), lambda: (0, 0)), spec, spec],
                  out_shape=[_sds((SSM_GROUPS, SSM_STATE))] * 2 + [_sds((SSM_GROUPS, 1))] + [_sds(_PARAM_SHAPE)] * 2,
                  )(lr, li, ls, br, bi, dar, dai, dbbr, dbbi, expand_t)


def _cmul(ar, ai, br, bi):
    return ar * br - ai * bi, ar * bi + ai * br


def _scan_consts(ar, ai, width, reverse):
    row = lax.broadcasted_iota(jnp.int32, (SUBLANES, width), 0)
    pw = [(ar, ai)]
    for _ in range(SUBLANES - 1):
        pw.append(_cmul(pw[-1][0], pw[-1][1], ar, ai))
    steps = []
    for d in (1, 2, 4):
        keep = (row < SUBLANES - d) if reverse else (row >= d)
        steps.append((d, jnp.where(keep, pw[d - 1][0], 0.0), jnp.where(keep, pw[d - 1][1], 0.0)))
    pr = jnp.zeros((SUBLANES, width), F32)
    pi = jnp.zeros((SUBLANES, width), F32)
    for r in range(SUBLANES):
        e = (SUBLANES - r) if reverse else (r + 1)
        pr = jnp.where(row == r, pw[e - 1][0], pr)
        pi = jnp.where(row == r, pw[e - 1][1], pi)
    return steps, pr, pi


def _scan_tile(xr, xi, cr, ci, consts, reverse):
    steps, pr, pi = consts
    for d, mr, mi in steps:
        sh = (SUBLANES - d) if reverse else d
        sr = pltpu.roll(xr, sh, 0)
        si = pltpu.roll(xi, sh, 0)
        xr, xi = xr + mr * sr - mi * si, xi + mr * si + mi * sr
    return xr + pr * cr - pi * ci, xi + pr * ci + pi * cr


def _ssm_fwd(ub, z, bbr, bbi, ar, ai, ccr, cci, dsk):
    s = ub.shape[0]
    tm = _row_tile(s)
    nt = tm // SUBLANES

    def body(ub_ref, u_ref, bbr_ref, bbi_ref, ar_ref, ai_ref, ccr_ref, cci_ref, dsk_ref,
             xr_ref, xi_ref, y_ref, cr_s, ci_s):
        i = pl.program_id(1)

        @pl.when(i == 0)
        def _():
            cr_s[...] = jnp.zeros_like(cr_s)
            ci_s[...] = jnp.zeros_like(ci_s)

        u_b = ub_ref[...]
        xr_ref[...] = _dot(u_b, bbr_ref[0])
        xi_ref[...] = _dot(u_b, bbi_ref[0])
        consts = _scan_consts(ar_ref[0], ai_ref[0], CHUNK_S, False)

        def tile(k, carry):
            cr, ci = carry
            sl = pl.ds(pl.multiple_of(k * SUBLANES, SUBLANES), SUBLANES)
            xr, xi = _scan_tile(xr_ref[sl, :], xi_ref[sl, :], cr, ci, consts, False)
            xr_ref[sl, :] = xr
            xi_ref[sl, :] = xi
            return xr[SUBLANES - 1:SUBLANES, :], xi[SUBLANES - 1:SUBLANES, :]

        cr, ci = lax.fori_loop(0, nt, tile, (cr_s[...], ci_s[...]))
        cr_s[...] = cr
        ci_s[...] = ci
        y_ref[...] = (_dot(xr_ref[...].astype(BF16), ccr_ref[0]) - _dot(xi_ref[...].astype(BF16), cci_ref[0])
                      + dsk_ref[...] * u_ref[...])

    ucol0 = U_COL0 // CHUNK_U
    wspec = lambda a, b: pl.BlockSpec((1, a, b), lambda j, i: (j, 0, 0))
    return _pcall(body, name="ssm_fwd", grid=(SSM_CHUNKS, s // tm),
                  in_specs=[pl.BlockSpec((tm, CHUNK_U), lambda j, i: (i, j)),
                            pl.BlockSpec((tm, CHUNK_U), lambda j, i: (i, ucol0 + j)),
                            wspec(CHUNK_U, CHUNK_S), wspec(CHUNK_U, CHUNK_S), wspec(1, CHUNK_S), wspec(1, CHUNK_S),
                            wspec(CHUNK_S, CHUNK_U), wspec(CHUNK_S, CHUNK_U),
                            pl.BlockSpec((1, CHUNK_U), lambda j, i: (0, j))],
                  out_specs=[pl.BlockSpec((tm, CHUNK_S), lambda j, i: (i, j)), pl.BlockSpec((tm, CHUNK_S), lambda j, i: (i, j)),
                             pl.BlockSpec((tm, CHUNK_U), lambda j, i: (i, j))],
                  out_shape=[_sds((s, N_STATE)), _sds((s, N_STATE)), _sds((s, SSM_W))],
                  scratch_shapes=[pltpu.VMEM((1, CHUNK_S), F32)] * 2,
                  dims=("parallel", "arbitrary"))(ub, z, bbr, bbi, ar, ai, ccr, cci, dsk)


def _ssm_glu(y, w_glu, b_glu):
    ge = _gelu(y)
    sg = _sigmoid(_dot(ge.astype(BF16), w_glu) + b_glu)
    return ge, sg


def _mix_out(y, att, x, w_glu, b_glu, g_att, g_ssm, w_out, g_ffn):
    s = x.shape[0]
    tm = _row_tile(s)

    def body(y_ref, att_ref, x_ref, wg_ref, bg_ref, ga_ref, gs_ref, wo_ref, gf_ref, x1_ref, mix_ref, h2_ref):
        ge, sg = _ssm_glu(y_ref[...], wg_ref[...], bg_ref[...])
        ms = _rms(ge * sg, gs_ref[...]).astype(BF16)
        ma = _rms(att_ref[...], ga_ref[...]).astype(BF16)
        mix_ref[:, 0:ATTN_W] = ma
        mix_ref[:, ATTN_W:D_MODEL] = ms
        x1 = x_ref[...] + (_dot(ma, wo_ref[0:ATTN_W, :]) + _dot(ms, wo_ref[ATTN_W:D_MODEL, :]))
        x1_ref[...] = x1
        h2_ref[...] = _rms(x1, gf_ref[...]).astype(BF16)

    row = lambda w: pl.BlockSpec((tm, w), lambda i: (i, 0))
    const = lambda a, b: pl.BlockSpec((a, b), lambda i: (0, 0))
    return _pcall(body, name="mix_out", grid=(s // tm,),
                  in_specs=[row(SSM_W), row(ATTN_W), row(D_MODEL), const(SSM_W, SSM_W), const(1, SSM_W), const(1, ATTN_W),
                            const(1, SSM_W), const(D_MODEL, D_MODEL), const(1, D_MODEL)],
                  out_specs=[row(D_MODEL)] * 3,
                  out_shape=[_sds((s, D_MODEL)), _sds((s, D_MODEL), BF16), _sds((s, D_MODEL), BF16)],
                  dims=("parallel",))(y, att, x, w_glu, b_glu, g_att, g_ssm, w_out, g_ffn)


def _conv_rows(pad_ref, w, b, s):
    y = b + pad_ref[pl.ds(SUBLANES - 2, s), :] * w[0:1, :]
    y = y + pad_ref[pl.ds(SUBLANES - 1, s), :] * w[1:2, :]
    return y + pad_ref[pl.ds(SUBLANES, s), :] * w[2:3, :]


def _conv_act(up, conv_w, conv_b):
    s = up.shape[0]

    def body(ug_ref, uv_ref, wg_ref, wv_ref, bg_ref, bv_ref, act_ref, pg_ref, pv_ref):
        zero = jnp.zeros((SUBLANES, STRIP), F32)
        pg_ref[0:SUBLANES, :] = zero
        pv_ref[0:SUBLANES, :] = zero
        pg_ref[pl.ds(SUBLANES, s), :] = ug_ref[...]
        pv_ref[pl.ds(SUBLANES, s), :] = uv_ref[...]
        hg = _conv_rows(pg_ref, wg_ref[...], bg_ref[...], s)
        hv = _conv_rows(pv_ref, wv_ref[...], bv_ref[...], s)
        act_ref[...] = (hg * _sigmoid(hg) * hv).astype(BF16)

    strip = lambda off: pl.BlockSpec((s, STRIP), lambda j: (0, j + off))
    wsp = lambda off: pl.BlockSpec((3, STRIP), lambda j: (0, j + off))
    bsp = lambda off: pl.BlockSpec((1, STRIP), lambda j: (0, j + off))
    return _pcall(body, name="conv_act", grid=(N_STRIPS,),
                  in_specs=[strip(0), strip(N_STRIPS), wsp(0), wsp(N_STRIPS), bsp(0), bsp(N_STRIPS)],
                  out_specs=pl.BlockSpec((s, STRIP), lambda j: (0, j)), out_shape=_sds((s, D_FF), BF16),
                  scratch_shapes=[pltpu.VMEM((s + SUBLANES, STRIP), F32)] * 2,
                  dims=("parallel",))(up, up, conv_w, conv_w, conv_b, conv_b)


def _down_loss(act, w_down, x1, tgt):
    s = x1.shape[0]
    tm = _row_tile(s)

    def body(a_ref, w_ref, x1_ref, t_ref, dy_ref, dyb_ref, loss_ref):
        i = pl.program_id(0)

        @pl.when(i == 0)
        def _():
            loss_ref[...] = jnp.zeros_like(loss_ref)

        diff = x1_ref[...] + _dot(a_ref[...], w_ref[...]) - t_ref[...]
        dy = diff * (1.0 / D_MODEL)
        dy_ref[...] = dy
        dyb_ref[...] = dy.astype(BF16)
        loss_ref[...] += 0.5 * jnp.sum(diff * dy)

    row = lambda w: pl.BlockSpec((tm, w), lambda i: (i, 0))
    return _pcall(body, name="down_loss", grid=(s // tm,),
                  in_specs=[row(D_FF), pl.BlockSpec((D_FF, D_MODEL), lambda i: (0, 0)), row(D_MODEL), row(D_MODEL)],
                  out_specs=[row(D_MODEL), row(D_MODEL), pl.BlockSpec((SUBLANES, LANES), lambda i: (0, 0))],
                  out_shape=[_sds((s, D_MODEL)), _sds((s, D_MODEL), BF16), _sds((SUBLANES, LANES))],
                  dims=("arbitrary",))(act, w_down, x1, tgt)


def _conv_act_bwd(up, dact, conv_w, conv_b):
    s = up.shape[0]

    def body(ug_ref, uv_ref, uo_ref, da_ref, wg_ref, wv_ref, wo_ref, bg_ref, bv_ref, dup_ref, dcw_ref,
             pg_ref, pv_ref, pd_ref):
        t = pl.program_id(0)
        zero = jnp.zeros((SUBLANES, STRIP), F32)
        pg_ref[0:SUBLANES, :] = zero
        pv_ref[0:SUBLANES, :] = zero
        pg_ref[pl.ds(SUBLANES, s), :] = ug_ref[...]
        pv_ref[pl.ds(SUBLANES, s), :] = uv_ref[...]
        hg = _conv_rows(pg_ref, wg_ref[...], bg_ref[...], s)
        hv = _conv_rows(pv_ref, wv_ref[...], bv_ref[...], s)
        sg = _sigmoid(hg)
        da = da_ref[...]
        d_gate = da * hv * (sg * (1.0 + hg * (1.0 - sg)))
        d_val = da * (hg * sg)
        dh = jnp.where(t == 0, d_gate, d_val)
        pg_ref[pl.ds(SUBLANES, s), :] = uo_ref[...]
        pd_ref[pl.ds(0, s), :] = dh
        pd_ref[pl.ds(s, SUBLANES), :] = zero
        w = wo_ref[...]
        dup = dh * w[2:3, :] + pd_ref[pl.ds(1, s), :] * w[1:2, :] + pd_ref[pl.ds(2, s), :] * w[0:1, :]
        dup_ref[...] = dup.astype(BF16)
        rows = [jnp.sum(dh * pg_ref[pl.ds(SUBLANES - 2 + k, s), :], axis=0, keepdims=True) for k in range(3)]
        rows.append(jnp.sum(dh, axis=0, keepdims=True))
        rid = lax.broadcasted_iota(jnp.int32, (SUBLANES, STRIP), 0)
        out = jnp.zeros((SUBLANES, STRIP), F32)
        for k, r in enumerate(rows):
            out = jnp.where(rid == k, r, out)
        dcw_ref[...] = out

    strip = lambda f: pl.BlockSpec((s, STRIP), f)
    wsp = lambda f: pl.BlockSpec((3, STRIP), f)
    bsp = lambda f: pl.BlockSpec((1, STRIP), f)
    gate = lambda t, j: (0, j)
    val = lambda t, j: (0, j + N_STRIPS)
    own = lambda t, j: (0, t * N_STRIPS + j)
    return _pcall(body, name="conv_act_bwd", grid=(2, N_STRIPS),
                  in_specs=[strip(gate), strip(val), strip(own), strip(gate), wsp(gate), wsp(val), wsp(own), bsp(gate), bsp(val)],
                  out_specs=[strip(own), pl.BlockSpec((SUBLANES, STRIP), own)],
                  out_shape=[_sds((s, 2 * D_FF), BF16), _sds((SUBLANES, 2 * D_FF))],
                  scratch_shapes=[pltpu.VMEM((s + SUBLANES, STRIP), F32)] * 3,
                  dims=("parallel", "parallel"))(up, up, up, dact, conv_w, conv_w, conv_w, conv_b, conv_b)


def _mix_bwd(dy, dh2, x1, g_ffn, w_out, y, att, w_glu, b_glu, g_att, g_ssm):
    s = dy.shape[0]
    tm = _row_tile(s)

    def body(dy_ref, dh2_ref, x1_ref, gf_ref, wo_ref, y_ref, att_ref, wg_ref, bg_ref, ga_ref, gs_ref,
             dx1_ref, dx1b_ref, datt_ref, dys_ref, dwg_ref, dgf_ref, dga_ref, dgs_ref, dbg_ref):
        i = pl.program_id(0)

        @pl.when(i == 0)
        def _():
            for r in (dwg_ref, dgf_ref, dga_ref, dgs_ref, dbg_ref):
                r[...] = jnp.zeros_like(r)

        dxn, dgf = _rms_bwd(x1_ref[...], gf_ref[...], dh2_ref[...])
        dx1 = dy_ref[...] + dxn
        dx1_ref[...] = dx1
        dx1b = dx1.astype(BF16)
        dx1b_ref[...] = dx1b
        dgf_ref[...] += dgf
        dma = _dot_nt(dx1b, wo_ref[0:ATTN_W, :])
        dms = _dot_nt(dx1b, wo_ref[ATTN_W:D_MODEL, :])
        datt, dga = _rms_bwd(att_ref[...], ga_ref[...], dma)
        datt_ref[...] = datt
        dga_ref[...] += dga
        yv = y_ref[...]
        ge, sg = _ssm_glu(yv, wg_ref[...], bg_ref[...])
        dssm, dgs = _rms_bwd(ge * sg, gs_ref[...], dms)
        dgs_ref[...] += dgs
        dgl = dssm * ge * sg * (1.0 - sg)
        dglb = dgl.astype(BF16)
        dge = dssm * sg + _dot_nt(dglb, wg_ref[...])
        dbg_ref[...] += jnp.sum(dgl, axis=0, keepdims=True)
        dwg_ref[...] += _dot_tn(ge.astype(BF16), dglb)
        dys_ref[...] = dge * _gelu_grad(yv)

    row = lambda w: pl.BlockSpec((tm, w), lambda i: (i, 0))
    const = lambda a, b: pl.BlockSpec((a, b), lambda i: (0, 0))
    return _pcall(body, name="mix_bwd", grid=(s // tm,),
                  in_specs=[row(D_MODEL), row(D_MODEL), row(D_MODEL), const(1, D_MODEL), const(D_MODEL, D_MODEL), row(SSM_W),
                            row(ATTN_W), const(SSM_W, SSM_W), const(1, SSM_W), const(1, ATTN_W), const(1, SSM_W)],
                  out_specs=[row(D_MODEL), row(D_MODEL), row(ATTN_W), row(SSM_W), const(SSM_W, SSM_W), const(1, D_MODEL),
                             const(1, ATTN_W), const(1, SSM_W), const(1, SSM_W)],
                  out_shape=[_sds((s, D_MODEL)), _sds((s, D_MODEL), BF16), _sds((s, ATTN_W)), _sds((s, SSM_W)),
                             _sds((SSM_W, SSM_W)), _sds((1, D_MODEL)), _sds((1, ATTN_W)), _sds((1, SSM_W)), _sds((1, SSM_W))],
                  dims=("arbitrary",))(dy, dh2, x1, g_ffn, w_out, y, att, w_glu, b_glu, g_att, g_ssm)


def _ssm_bwd(dys, z, ub, xr, xi, bbr, bbi, ar, ai, ccr, cci, dsk):
    s = dys.shape[0]
    tm = _row_tile(s)
    nb = s // tm
    nt = tm // SUBLANES

    def body(dy_ref, u_ref, ub_ref, xr_ref, xi_ref, xrp_ref, xip_ref, bbr_ref, bbi_ref, ar_ref, ai_ref, ccr_ref,
             cci_ref, dsk_ref, du_ref, dbbr_ref, dbbi_ref, dccr_ref, dcci_ref, dar_ref, dai_ref, dd_ref,
             gr_s, gi_s, cr_s, ci_s, accr_s, acci_s):
        i = pl.program_id(1)
        first_block = i == nb - 1

        @pl.when(i == 0)
        def _():
            for r in (cr_s, ci_s, accr_s, acci_s, dbbr_ref, dbbi_ref, dccr_ref, dcci_ref, dd_ref):
                r[...] = jnp.zeros_like(r)

        dy = dy_ref[...]
        dyb = dy.astype(BF16)
        gr_s[...] = _dot_nt(dyb, ccr_ref[0])
        gi_s[...] = -_dot_nt(dyb, cci_ref[0])
        consts = _scan_consts(ar_ref[0], -ai_ref[0], CHUNK_S, True)
        row = lax.broadcasted_iota(jnp.int32, (SUBLANES, CHUNK_S), 0)

        def tile(kk, carry):
            cr, ci, accr, acci = carry
            k = nt - 1 - kk
            sl = pl.ds(pl.multiple_of(k * SUBLANES, SUBLANES), SUBLANES)
            gr, gi = _scan_tile(gr_s[sl, :], gi_s[sl, :], cr, ci, consts, True)
            gr_s[sl, :] = gr
            gi_s[sl, :] = gi
            slp = pl.ds(pl.multiple_of(jnp.maximum(k - 1, 0) * SUBLANES, SUBLANES), SUBLANES)
            inner = k > 0
            pr_t = jnp.where(inner, xr_ref[slp, :], xrp_ref[...])
            pi_t = jnp.where(inner, xi_ref[slp, :], xip_ref[...])
            live = jnp.logical_or(inner, jnp.logical_not(first_block))
            top_r = jnp.where(live, pltpu.roll(pr_t, 1, 0), 0.0)
            top_i = jnp.where(live, pltpu.roll(pi_t, 1, 0), 0.0)
            xpr = jnp.where(row == 0, top_r, pltpu.roll(xr_ref[sl, :], 1, 0))
            xpi = jnp.where(row == 0, top_i, pltpu.roll(xi_ref[sl, :], 1, 0))
            accr = accr + gr * xpr + gi * xpi
            acci = acci + gi * xpr - gr * xpi
            return gr[0:1, :], gi[0:1, :], accr, acci

        zeros = jnp.zeros((SUBLANES, CHUNK_S), F32)
        cr, ci, accr, acci = lax.fori_loop(0, nt, tile, (cr_s[...], ci_s[...], zeros, zeros))
        cr_s[...] = cr
        ci_s[...] = ci
        accr_s[...] += accr
        acci_s[...] += acci
        grb = gr_s[...].astype(BF16)
        gib = gi_s[...].astype(BF16)
        u_b = ub_ref[...]
        du_ref[...] = _dot_nt(grb, bbr_ref[0]) + _dot_nt(gib, bbi_ref[0]) + dsk_ref[...] * dy
        dbbr_ref[0] += _dot_tn(u_b, grb)
        dbbi_ref[0] += _dot_tn(u_b, gib)
        dccr_ref[0] += _dot_tn(xr_ref[...].astype(BF16), dyb)
        dcci_ref[0] -= _dot_tn(xi_ref[...].astype(BF16), dyb)
        dd_ref[...] += jnp.sum(dy * u_ref[...], axis=0, keepdims=True)

        @pl.when(i == nb - 1)
        def _():
            dar_ref[0] = jnp.sum(accr_s[...], axis=0, keepdims=True)
            dai_ref[0] = jnp.sum(acci_s[...], axis=0, keepdims=True)

    ucol0 = U_COL0 // CHUNK_U
    tiles_per_block = tm // SUBLANES
    rb = lambda i: nb - 1 - i
    wspec = lambda a, b: pl.BlockSpec((1, a, b), lambda j, i: (j, 0, 0))
    xblk = pl.BlockSpec((tm, CHUNK_S), lambda j, i: (rb(i), j))
    xprev = pl.BlockSpec((SUBLANES, CHUNK_S), lambda j, i: (jnp.maximum(rb(i) * tiles_per_block - 1, 0), j))
    ublk = pl.BlockSpec((tm, CHUNK_U), lambda j, i: (rb(i), j))
    return _pcall(body, name="ssm_bwd", grid=(SSM_CHUNKS, nb),
                  in_specs=[ublk, pl.BlockSpec((tm, CHUNK_U), lambda j, i: (rb(i), ucol0 + j)), ublk, xblk, xblk, xprev, xprev,
                            wspec(CHUNK_U, CHUNK_S), wspec(CHUNK_U, CHUNK_S), wspec(1, CHUNK_S), wspec(1, CHUNK_S),
                            wspec(CHUNK_S, CHUNK_U), wspec(CHUNK_S, CHUNK_U), pl.BlockSpec((1, CHUNK_U), lambda j, i: (0, j))],
                  out_specs=[ublk, wspec(CHUNK_U, CHUNK_S), wspec(CHUNK_U, CHUNK_S), wspec(CHUNK_S, CHUNK_U),
                             wspec(CHUNK_S, CHUNK_U), wspec(1, CHUNK_S), wspec(1, CHUNK_S),
                             pl.BlockSpec((1, CHUNK_U), lambda j, i: (0, j))],
                  out_shape=[_sds((s, SSM_W)), _sds((SSM_CHUNKS, CHUNK_U, CHUNK_S)), _sds((SSM_CHUNKS, CHUNK_U, CHUNK_S)),
                             _sds((SSM_CHUNKS, CHUNK_S, CHUNK_U)), _sds((SSM_CHUNKS, CHUNK_S, CHUNK_U)),
                             _sds((SSM_CHUNKS, 1, CHUNK_S)), _sds((SSM_CHUNKS, 1, CHUNK_S)), _sds((1, SSM_W))],
                  scratch_shapes=[pltpu.VMEM((tm, CHUNK_S), F32)] * 2 + [pltpu.VMEM((1, CHUNK_S), F32)] * 2
                                 + [pltpu.VMEM((SUBLANES, CHUNK_S), F32)] * 2,
                  dims=("parallel", "arbitrary"))(dys, z, ub, xr, xi, xr, xi, bbr, bbi, ar, ai, ccr, cci, dsk)


def _attn_probs(q, ks, cs, lse, scale, diagonal):
    p = jnp.exp(_dot_nt(q, ks) * scale - cs - lse)
    if diagonal:
        tq, tk = p.shape
        causal = lax.broadcasted_iota(jnp.int32, (tq, tk), 1) <= lax.broadcasted_iota(jnp.int32, (tq, tk), 0)
        p = jnp.where(causal, p, 0.0)
    return p


def _attn_dl(qh, kh, vh, crow, lse, doh):
    _, s, _ = qh.shape
    tq = _row_tile(s)
    scale = HEAD_DIM ** -0.5

    def body(q_ref, k_ref, v_ref, c_ref, lse_ref, do_ref, dl_ref):
        i = pl.program_id(1)
        q = q_ref[0]
        dob = do_ref[0].astype(BF16)
        lse_i = lse_ref[0]

        def block(j, dl, diagonal):
            off = pl.multiple_of(j * tq, tq)
            ks = k_ref[0, pl.ds(off, tq), :]
            vs = v_ref[0, pl.ds(off, tq), :]
            p = _attn_probs(q, ks, c_ref[0, :, pl.ds(off, tq)], lse_i, scale, diagonal)
            return dl + jnp.sum(p * _dot_nt(dob, vs), axis=-1, keepdims=True)

        dl = lax.fori_loop(0, i, lambda j, c: block(j, c, False), jnp.zeros((tq, 1), F32))
        dl_ref[0] = block(i, dl, True)

    blk = pl.BlockSpec((1, tq, HEAD_DIM), lambda h, i: (h, i, 0))
    full = pl.BlockSpec((1, s, HEAD_DIM), lambda h, i: (h, 0, 0))
    col = pl.BlockSpec((1, tq, 1), lambda h, i: (h, i, 0))
    return _pcall(body, name="attn_dl", grid=(HEADS, s // tq),
                  in_specs=[blk, full, full, pl.BlockSpec((1, 1, s), lambda h, i: (h, 0, 0)), col, blk],
                  out_specs=col, out_shape=_sds((HEADS, s, 1)), dims=("parallel", "parallel"))(qh, kh, vh, crow, lse, doh)


def _attn_bwd(qh, kh, vh, crow, dl, lse, doh):
    _, s, _ = qh.shape
    tq = _row_tile(s)
    nq = s // tq
    scale = HEAD_DIM ** -0.5

    def body(q_ref, k_ref, v_ref, c_ref, dl_ref, lse_ref, do_ref, dq_ref, dk_ref, dv_ref, dc_ref):
        j = pl.program_id(1)

        @pl.when(j == 0)
        def _():
            dq_ref[...] = jnp.zeros_like(dq_ref)

        ks = k_ref[0]
        vs = v_ref[0]
        cs = c_ref[0]

        def block(i, carry, diagonal):
            dk, dv, dc = carry
            rows = pl.ds(pl.multiple_of(i * tq, tq), tq)
            q = q_ref[0, rows, :]
            dob = do_ref[0, rows, :].astype(BF16)
            p = _attn_probs(q, ks, cs, lse_ref[0, rows, :], scale, diagonal)
            dp = _dot_nt(dob, vs)
            ds = p * (dp - dl_ref[0, rows, :])
            dsb = ds.astype(BF16)
            dv = dv + _dot_tn(p.astype(BF16), dob)
            dk = dk + _dot_tn(dsb, q) * scale
            dq_ref[0, rows, :] += _dot(dsb, ks) * scale
            dc = dc - jnp.sum(ds, axis=0, keepdims=True)
            return dk, dv, dc

        init = (jnp.zeros((tq, HEAD_DIM), F32), jnp.zeros((tq, HEAD_DIM), F32), jnp.zeros((1, tq), F32))
        carry = block(j, init, True)
        dk, dv, dc = lax.fori_loop(j + 1, nq, lambda i, c: block(i, c, False), carry)
        dk_ref[0] = dk
        dv_ref[0] = dv
        dc_ref[0] = dc

    blk = pl.BlockSpec((1, tq, HEAD_DIM), lambda h, j: (h, j, 0))
    full = pl.BlockSpec((1, s, HEAD_DIM), lambda h, j: (h, 0, 0))
    cblk = pl.BlockSpec((1, 1, tq), lambda h, j: (h, 0, j))
    col = pl.BlockSpec((1, s, 1), lambda h, j: (h, 0, 0))
    return _pcall(body, name="attn_bwd", grid=(HEADS, nq),
                  in_specs=[full, blk, blk, cblk, col, col, full],
                  out_specs=[full, blk, blk, cblk],
                  out_shape=[_sds((HEADS, s, HEAD_DIM))] * 3 + [_sds((HEADS, 1, s))],
                  dims=("parallel", "arbitrary"))(qh, kh, vh, crow, dl, lse, doh)


def _prep_bwd(z, dqn, dkn, dv, du, dc, gq, gk, bf, gg):
    s = z.shape[0]
    tm = _row_tile(s)
    nb = s // tm

    def body(z_ref, dqn_ref, dkn_ref, dv_ref, du_ref, dc_ref, gq_ref, gk_ref, bf_ref, gg_ref,
             dz_ref, dgq_ref, dgk_ref, dbf_ref, carry_ref):
        i = pl.program_id(0)

        @pl.when(i == 0)
        def _():
            for r in (dgq_ref, dgk_ref, dbf_ref, carry_ref):
                r[...] = jnp.zeros_like(r)

        gg_m = gg_ref[...]

        def head_norm_bwd(t, g, dn):
            r = lax.rsqrt(_dot_exact_r(t * t, gg_m) * (1.0 / HEAD_DIM) + EPS)
            w = dn * g
            mean_wt = _dot_exact_r(w * t, gg_m) * (1.0 / HEAD_DIM)
            return r * w - t * (r * r * r) * mean_wt, jnp.sum(dn * t * r, axis=0, keepdims=True)

        dq, dgq = head_norm_bwd(z_ref[:, 0:ATTN_W], gq_ref[...], dqn_ref[...])
        dk, dgk = head_norm_bwd(z_ref[:, ATTN_W:2 * ATTN_W], gk_ref[...], dkn_ref[...])
        dgq_ref[...] += dgq
        dgk_ref[...] += dgk
        row = lax.broadcasted_iota(jnp.int32, (tm, tm), 0)
        col = lax.broadcasted_iota(jnp.int32, (tm, tm), 1)
        triu = (col >= row).astype(BF16)
        dlf = _dot_exact_l(triu, dc_ref[...]) + carry_ref[...]
        carry_ref[...] = dlf[0:1, :]
        fl = z_ref[:, F_COL0:F_COL0 + LANES] + bf_ref[...]
        df = dlf * _sigmoid(-fl)
        dbf_ref[...] += jnp.sum(df, axis=0, keepdims=True)
        dz_ref[:, 0:ATTN_W] = dq.astype(BF16)
        dz_ref[:, ATTN_W:2 * ATTN_W] = dk.astype(BF16)
        dz_ref[:, 2 * ATTN_W:3 * ATTN_W] = dv_ref[...].astype(BF16)
        dz_ref[:, U_COL0:U_COL0 + SSM_W] = du_ref[...].astype(BF16)
        dz_ref[:, F_COL0:F_COL0 + LANES] = df.astype(BF16)

    row_spec = lambda w: pl.BlockSpec((tm, w), lambda i: (nb - 1 - i, 0))
    const = lambda shape: pl.BlockSpec(shape, lambda i: (0, 0))
    return _pcall(body, name="prep_bwd", grid=(nb,),
                  in_specs=[row_spec(Z_COLS)] + [row_spec(ATTN_W)] * 4 + [row_spec(LANES), const((1, ATTN_W)),
                            const((1, ATTN_W)), const((1, LANES)), const((ATTN_W, ATTN_W))],
                  out_specs=[row_spec(Z_COLS), const((1, ATTN_W)), const((1, ATTN_W)), const((1, LANES))],
                  out_shape=[_sds((s, Z_COLS), BF16), _sds((1, ATTN_W)), _sds((1, ATTN_W)), _sds((1, LANES))],
                  scratch_shapes=[pltpu.VMEM((1, LANES), F32)], dims=("arbitrary",))(z, dqn, dkn, dv, du, dc, gq, gk, bf, gg)


def _in_norm_bwd(x, g_mix, dh, dx1):
    s = x.shape[0]
    tm = _row_tile(s)

    def body(x_ref, g_ref, dh_ref, dx1_ref, dx_ref, dg_ref):
        i = pl.program_id(0)

        @pl.when(i == 0)
        def _():
            dg_ref[...] = jnp.zeros_like(dg_ref)

        dxn, dg = _rms_bwd(x_ref[...], g_ref[...], dh_ref[...])
        dx_ref[...] = dx1_ref[...] + dxn
        dg_ref[...] += dg

    row = pl.BlockSpec((tm, D_MODEL), lambda i: (i, 0))
    vec = pl.BlockSpec((1, D_MODEL), lambda i: (0, 0))
    return _pcall(body, name="in_norm_bwd", grid=(s // tm,), in_specs=[row, vec, row, row], out_specs=[row, vec],
                  out_shape=[_sds((s, D_MODEL)), _sds((1, D_MODEL))], dims=("arbitrary",))(x, g_mix, dh, dx1)


def _adamw_refs(w_ref, g_ref, m_ref, v_ref, d_ref, mo_ref, vo_ref):
    gv = g_ref[...]
    mn = ADAM_B1 * m_ref[...] + (1.0 - ADAM_B1) * gv
    vn = ADAM_B2 * v_ref[...] + (1.0 - ADAM_B2) * (gv * gv)
    m_hat = mn / (1.0 - ADAM_B1 ** ADAM_STEP)
    v_hat = vn / (1.0 - ADAM_B2 ** ADAM_STEP)
    d_ref[...] = -ADAM_LR * (m_hat / (jnp.sqrt(v_hat) + ADAM_EPS) + ADAM_WD * w_ref[...])
    mo_ref[...] = mn
    vo_ref[...] = vn


def _adamw_small(ws, gs, ms, vs):
    n = len(ws)

    def body(*refs):
        ins, outs = refs[:4 * n], refs[4 * n:]
        for i in range(n):
            _adamw_refs(ins[i], ins[n + i], ins[2 * n + i], ins[3 * n + i], *outs[3 * i:3 * i + 3])

    vm = pl.BlockSpec(memory_space=pltpu.VMEM)
    out_shape = [_sds(w.shape) for w in ws for _ in range(3)]
    return _pallas(body, name="adamw_small", in_specs=[vm] * (4 * n), out_specs=[vm] * (3 * n), out_shape=out_shape,
                   compiler_params=pltpu.CompilerParams(vmem_limit_bytes=VMEM_LIMIT))(*ws, *gs, *ms, *vs)


def _adamw(w, g, m, v, *, name):
    r, c = w.shape
    tr = r
    for cand in (256, 176, 128, 64):
        if r > cand and r % cand == 0:
            tr = cand
            break

    def body(w_ref, g_ref, m_ref, v_ref, d_ref, mo_ref, vo_ref):
        _adamw_refs(w_ref, g_ref, m_ref, v_ref, d_ref, mo_ref, vo_ref)

    spec = pl.BlockSpec((tr, c), lambda i: (i, 0))
    return _pcall(body, name=name, grid=(r // tr,), in_specs=[spec] * 4, out_specs=[spec] * 3,
                  out_shape=[_sds((r, c))] * 3, dims=("parallel",))(w, g, m, v)


def _prefetch_call(body, *, name, grid, in_specs, out_specs, out_shape, operands):
    grid_spec = pltpu.PrefetchScalarGridSpec(num_scalar_prefetch=1, grid=grid, in_specs=in_specs, out_specs=out_specs)
    params = pltpu.CompilerParams(dimension_semantics=("parallel",) * len(grid), vmem_limit_bytes=VMEM_LIMIT)
    return _pallas(body, name=name, grid_spec=grid_spec, out_shape=out_shape, compiler_params=params)(*operands)


def _half_rows_tile(hr):
    return hr if hr <= 256 else 176 if hr % 176 == 0 else 256


def _add_half(g, landed, place, *, name):
    def body(place_ref, g_ref, l_ref, o_ref):
        own = g_ref[0] if len(g_ref.shape) == 4 else g_ref[...]
        o_ref[...] = (own + l_ref[...]).astype(BF16)

    if g.ndim == 4:
        _, _, hr, c = g.shape
        tr = _half_rows_tile(hr)
        blk = (1, tr, c)
        return _prefetch_call(
            body, name=name, grid=(N_CHIPS, hr // tr),
            in_specs=[pl.BlockSpec((1,) + blk, lambda j, i, p: (j, p[1], i, 0)), pl.BlockSpec(blk, lambda j, i, p: (j, i, 0))],
            out_specs=pl.BlockSpec(blk, lambda j, i, p: (j, i, 0)), out_shape=_sds(landed.shape, BF16),
            operands=(place, g, landed))
    hr, c = landed.shape
    tr, tc = 256, _tile(c, 2176)
    nb = hr // tr
    return _prefetch_call(
        body, name=name, grid=(nb, c // tc),
        in_specs=[pl.BlockSpec((tr, tc), lambda i, j, p: (p[1] * nb + i, j)), pl.BlockSpec((tr, tc), lambda i, j, p: (i, j))],
        out_specs=pl.BlockSpec((tr, tc), lambda i, j, p: (i, j)), out_shape=_sds(landed.shape, BF16),
        operands=(place, g, landed))


def _sum_chips(chip_sum, lands, place, *, name, tc, window_stride=0):
    _, hr, c = lands.shape
    tr = _half_rows_tile(hr)
    nb = hr // tr
    ncb = c // tc

    def body(place_ref, own_ref, a_ref, b_ref, c_ref, o_ref):
        own = own_ref[0] if len(own_ref.shape) == 3 else own_ref[...]
        o_ref[...] = ((own.astype(F32) + a_ref[0].astype(F32)) + b_ref[0].astype(F32)) + c_ref[0].astype(F32)

    land = lambda k: pl.BlockSpec((1, tr, tc), lambda i, j, p: ((p[0] + k) % N_CHIPS, i, j))
    if chip_sum.ndim == 3:
        own_spec = land(0)
    else:
        stride = window_stride // tc
        own_spec = pl.BlockSpec((tr, tc), lambda i, j, p: (i, p[0] * stride + j))
    return _prefetch_call(
        body, name=name, grid=(nb, ncb), in_specs=[own_spec, land(1), land(2), land(3)],
        out_specs=pl.BlockSpec((tr, tc), lambda i, j, p: (p[1] * nb + i, j)), out_shape=_sds((2 * hr, c)),
        operands=(place, chip_sum, lands, lands, lands))


_HBM = pl.BlockSpec(memory_space=pltpu.HBM)


def _place():
    x, y, c = lax.axis_index("x"), lax.axis_index("y"), lax.axis_index("c")
    chips = [(1 - x, y), (x, 1 - y), (1 - x, 1 - y)]
    return x, y, c, chips


def _rcopy(src, dst, send_sem, recv_sem, to):
    return pltpu.make_async_remote_copy(src_ref=src, dst_ref=dst, send_sem=send_sem, recv_sem=recv_sem,
                                        device_id=to, device_id_type=MESH)


N_BIG = 5
UP_COLS = 2 * D_FF // N_CHIPS
IN_WINDOW = 640
IN_STRIDE = 512


def _gather_weights(shards, conv_w):
    def body(*refs):
        srcs, cw_ref = refs[0:N_BIG], refs[N_BIG]
        outs, cw_out = refs[N_BIG + 1:2 * N_BIG + 1], refs[2 * N_BIG + 1]
        send_sems, recv_sems = refs[2 * N_BIG + 2:]
        x, y, c, chips = _place()
        sibling = (x, y, 1 - c)
        me = 2 * x + y

        def half(w, chip, hc):
            hr, cols = srcs[w].shape[0] // 2, srcs[w].shape[1]
            if len(outs[w].shape) == 2:
                return outs[w].at[pl.ds(hc * hr, hr), pl.ds(pl.multiple_of(chip * cols, LANES), cols)]
            return outs[w].at[chip, pl.ds(hc * hr, hr), :]

        first, passed = [], []
        for w in range(N_BIG):
            hr = srcs[w].shape[0] // 2
            for k, (cx, cy) in enumerate(chips):
                first.append(_rcopy(srcs[w].at[pl.ds(c * hr, hr), :], half(w, me, c), send_sems.at[3 * w + k],
                                    recv_sems.at[3 * w + k], (cx, cy, c)))
        n_first = len(first)
        for k, (cx, cy) in enumerate(chips):
            first.append(_rcopy(cw_ref, cw_out.at[me], send_sems.at[2 * n_first + k], recv_sems.at[2 * n_first + k],
                                (cx, cy, c)))
        for cp in first:
            cp.start()
        for w in range(N_BIG):
            for k, (cx, cy) in enumerate(chips):
                landed = half(w, 2 * cx + cy, c)
                _rcopy(landed, landed, send_sems.at[3 * w + k], recv_sems.at[3 * w + k], sibling).wait_recv()
                fwd = _rcopy(landed, landed, send_sems.at[n_first + 3 * w + k], recv_sems.at[n_first + 3 * w + k], sibling)
                fwd.start()
                passed.append(fwd)
        for w in range(N_BIG):
            for k, (cx, cy) in enumerate(chips):
                other = half(w, 2 * cx + cy, 1 - c)
                _rcopy(other, other, send_sems.at[n_first + 3 * w + k], recv_sems.at[n_first + 3 * w + k], sibling).wait_recv()
        for k, (cx, cy) in enumerate(chips):
            slot = cw_out.at[2 * cx + cy]
            _rcopy(slot, slot, send_sems.at[2 * n_first + k], recv_sems.at[2 * n_first + k], sibling).wait_recv()
        for cp in first + passed:
            cp.wait_send()

    out_shape = [_sds((shards[3].shape[0], N_CHIPS * UP_COLS), BF16) if w == 3 else _sds((N_CHIPS,) + shards[w].shape, BF16)
                 for w in range(N_BIG)] + [_sds((N_CHIPS,) + conv_w.shape)]
    n_sems = 2 * 3 * N_BIG + 3
    return _pallas(body, name="gather_weights", in_specs=[_HBM] * (N_BIG + 1), out_specs=[_HBM] * (N_BIG + 1),
                   out_shape=out_shape,
                   scratch_shapes=[pltpu.SemaphoreType.DMA((n_sems,)), pltpu.SemaphoreType.DMA((n_sems,))])(*shards, conv_w)


def _swap_halves(grads):
    def body(*refs):
        g_refs, land_refs = refs[0:N_BIG], refs[N_BIG:2 * N_BIG]
        send_sems, recv_sems = refs[2 * N_BIG:]
        x, y, c, _ = _place()
        cps = []
        for w in range(N_BIG):
            if len(g_refs[w].shape) == 4:
                theirs = g_refs[w].at[:, 1 - c]
            else:
                hr = g_refs[w].shape[0] // 2
                theirs = g_refs[w].at[pl.ds((1 - c) * hr, hr), :]
            cps.append(_rcopy(theirs, land_refs[w], send_sems.at[w], recv_sems.at[w], (x, y, 1 - c)))
        for cp in cps:
            cp.start()
        for cp in cps:
            cp.wait()

    out_shape = [_sds((g.shape[0], g.shape[2], g.shape[3])) if g.ndim == 4 else _sds((g.shape[0] // 2, g.shape[1]))
                 for g in grads]
    return _pallas(body, name="swap_halves", in_specs=[_HBM] * N_BIG, out_specs=[_HBM] * N_BIG, out_shape=out_shape,
                   scratch_shapes=[pltpu.SemaphoreType.DMA((N_BIG,)), pltpu.SemaphoreType.DMA((N_BIG,))])(*grads)


def _scatter_chips(chip_sums):
    windows = {0: (IN_STRIDE, IN_WINDOW), 3: (UP_COLS, UP_COLS)}

    def body(*refs):
        s_refs, land_refs = refs[0:N_BIG], refs[N_BIG:2 * N_BIG]
        send_sems, recv_sems = refs[2 * N_BIG:]
        x, y, c, chips = _place()
        me = 2 * x + y
        sends = []
        for w in range(N_BIG):
            for k, (cx, cy) in enumerate(chips):
                tgt = 2 * cx + cy
                if w in windows:
                    stride, width = windows[w]
                    part = s_refs[w].at[:, pl.ds(pl.multiple_of(tgt * stride, LANES), width)]
                else:
                    part = s_refs[w].at[tgt]
                sends.append(_rcopy(part, land_refs[w].at[me], send_sems.at[3 * w + k], recv_sems.at[3 * w + k], (cx, cy, c)))
        for cp in sends:
            cp.start()
        for w in range(N_BIG):
            for k, (cx, cy) in enumerate(chips):
                slot = land_refs[w].at[2 * cx + cy]
                _rcopy(slot, slot, send_sems.at[3 * w + k], recv_sems.at[3 * w + k], (cx, cy, c)).wait_recv()
        for cp in sends:
            cp.wait_send()

    out_shape = [_sds((N_CHIPS, s.shape[0], windows[w][1]), BF16) if w in windows else _sds(s.shape, BF16)
                 for w, s in enumerate(chip_sums)]
    return _pallas(body, name="scatter_chips", in_specs=[_HBM] * N_BIG, out_specs=[_HBM] * N_BIG, out_shape=out_shape,
                   scratch_shapes=[pltpu.SemaphoreType.DMA((3 * N_BIG,)), pltpu.SemaphoreType.DMA((3 * N_BIG,))])(*chip_sums)


def _join_halves(reds):
    def body(*refs):
        out_refs = refs[N_BIG:2 * N_BIG]
        send_sems, recv_sems = refs[2 * N_BIG:]
        x, y, c, _ = _place()
        cps = []
        for w in range(N_BIG):
            hr = out_refs[w].shape[0] // 2
            mine = out_refs[w].at[pl.ds(c * hr, hr), :]
            cps.append(_rcopy(mine, mine, send_sems.at[w], recv_sems.at[w], (x, y, 1 - c)))
        for cp in cps:
            cp.start()
        for w in range(N_BIG):
            hr = out_refs[w].shape[0] // 2
            theirs = out_refs[w].at[pl.ds((1 - c) * hr, hr), :]
            _rcopy(theirs, theirs, send_sems.at[w], recv_sems.at[w], (x, y, 1 - c)).wait_recv()
        for cp in cps:
            cp.wait_send()

    return _pallas(body, name="join_halves", in_specs=[_HBM] * N_BIG, out_specs=[_HBM] * N_BIG,
                   out_shape=[_sds(r.shape) for r in reds], input_output_aliases={w: w for w in range(N_BIG)},
                   scratch_shapes=[pltpu.SemaphoreType.DMA((N_BIG,)), pltpu.SemaphoreType.DMA((N_BIG,))])(*reds)


def _allreduce_small(v):
    m_per = v.shape[0]

    def body(v_ref, out_ref, all_ref, send_sems, recv_sems, local_sem):
        x, y, c, chips = _place()
        me, sibling = (x, y, c), (x, y, 1 - c)

        def rows(px, py, pc):
            return all_ref.at[pl.ds((4 * px + 2 * py + pc) * m_per, m_per), :]

        def copy(k, block, to, src=None):
            return _rcopy(rows(*block) if src is None else src, rows(*block), send_sems.at[k], recv_sems.at[k], to)

        mine = pltpu.make_async_copy(v_ref, rows(*me), local_sem)
        mine.start()
        first = [copy(0, me, sibling, src=v_ref)]
        first += [copy(1 + k, me, (*chip, c), src=v_ref) for k, chip in enumerate(chips)]
        for cp in first:
            cp.start()
        passed = [copy(4 + k, (*chip, c), sibling) for k, chip in enumerate(chips)]
        for k, chip in enumerate(chips):
            copy(1 + k, (*chip, c), me).wait_recv()
            passed[k].start()
        copy(0, sibling, me).wait_recv()
        for k, chip in enumerate(chips):
            copy(4 + k, (*chip, 1 - c), me).wait_recv()
        for cp in first + passed:
            cp.wait_send()
        mine.wait()
        acc = all_ref[pl.ds(0, m_per), :]
        for d in range(1, 8):
            acc = acc + all_ref[pl.ds(d * m_per, m_per), :]
        out_ref[...] = acc

    vm = pl.BlockSpec(memory_space=pltpu.VMEM)
    return _pallas(body, name="allreduce_small", in_specs=[vm], out_specs=vm, out_shape=_sds((m_per, LANES)),
                          scratch_shapes=[pltpu.VMEM((8 * m_per, LANES), F32), pltpu.SemaphoreType.DMA((7,)),
                                          pltpu.SemaphoreType.DMA((7,)), pltpu.SemaphoreType.DMA],
                          compiler_params=pltpu.CompilerParams(vmem_limit_bytes=VMEM_LIMIT))(v)


def _to_heads(t):
    s = t.shape[0]
    return t.reshape(s, HEADS, HEAD_DIM).transpose(1, 0, 2)


def _from_heads(t):
    s = t.shape[1]
    return t.transpose(1, 0, 2).reshape(s, HEADS * HEAD_DIM)


def _reorder_in_cols(w):
    pad = jnp.zeros((w.shape[0], Z_COLS - IN_COLS), w.dtype)
    return jnp.concatenate([w[:, :3 * ATTN_W], w[:, 3 * ATTN_W + HEADS:], w[:, 3 * ATTN_W:3 * ATTN_W + HEADS], pad], axis=1)


def _restore_in_cols(w):
    return jnp.concatenate([w[:, :3 * ATTN_W], w[:, F_COL0:F_COL0 + HEADS], w[:, U_COL0:U_COL0 + SSM_W]], axis=1)


def _block_diag(blocks):
    j, g, a, b = blocks.shape
    eye = jnp.eye(g, dtype=bool)[None, :, None, :, None]
    return jnp.where(eye, blocks[:, :, :, None, :], jnp.zeros((), blocks.dtype)).reshape(j, g * a, g * b)


def _diag_blocks(m, a, b):
    j = m.shape[0]
    g = m.shape[1] // a
    t = m.reshape(j, g, a, g, b)
    eye = jnp.eye(g, dtype=bool)[None, :, None, :, None]
    return jnp.sum(jnp.where(eye, t, 0.0), axis=3)


def _pack_rows(parts, rows, dtype):
    used = sum(p.shape[0] for p in parts)
    return jnp.concatenate([p.astype(dtype) for p in parts] + [jnp.zeros((rows - used, D_MODEL), dtype)], axis=0)


_SMALL = (("g_mix", (1024,)), ("b_f", (8,)), ("g_q", (64,)), ("g_k", (64,)), ("lambda_re", (32, 64)),
          ("lambda_im", (32, 64)), ("log_step", (32,)), ("b_re", (32, 64, 16)), ("b_im", (32, 64, 16)),
          ("c_re", (32, 16, 64)), ("c_im", (32, 16, 64)), ("d_skip", (32, 16)), ("b_glu", (512,)),
          ("g_attn_out", (512,)), ("g_ssm_out", (512,)), ("g_ffn", (1024,)), ("conv_b", (5632,)))


def _small_rows(shape):
    return -(-math.prod(shape) // LANES)


def _pack_small(arrs, extra=()):
    parts = []
    for a in list(arrs) + list(extra):
        flat = a.reshape(-1)
        rows = -(-flat.shape[0] // LANES)
        parts.append(jnp.pad(flat, (0, rows * LANES - flat.shape[0])).reshape(rows, LANES))
    total = sum(p.shape[0] for p in parts)
    pad = -total % SUBLANES
    if pad:
        parts.append(jnp.zeros((pad, LANES), F32))
    return jnp.concatenate(parts, axis=0)


def _unpack_small(buf, shapes):
    out, r = [], 0
    for shape in shapes:
        n = math.prod(shape)
        rows = -(-n // LANES)
        out.append(buf[r:r + rows].reshape(-1)[:n].reshape(shape))
        r += rows
    return out


def _local_step(x, tgt, w_in_r, w_glu_b, w_out_b, w_up_b, w_down_b, conv_w_full, p):
    s = x.shape[0]
    row = lambda v: v.reshape(1, -1)
    g_mix, g_ffn = row(p["g_mix"]), row(p["g_ffn"])
    g_att, g_ssm, b_glu, conv_b = row(p["g_attn_out"]), row(p["g_ssm_out"]), row(p["b_glu"]), row(p["conv_b"])
    gq = row(jnp.tile(p["g_q"], HEADS))
    gk = row(jnp.tile(p["g_k"], HEADS))
    bf = row(jnp.pad(p["b_f"], (0, LANES - HEADS)))
    gg = jnp.kron(jnp.eye(HEADS, dtype=F32), jnp.ones((HEAD_DIM, HEAD_DIM), F32)).astype(BF16)
    dsk = row(p["d_skip"])

    rep = lambda a: jnp.repeat(a, SSM_GROUP, axis=0)
    lr, li = rep(p["lambda_re"]), rep(p["lambda_im"])
    ls = rep(jnp.broadcast_to(p["log_step"][:, None], (SSM_GROUPS, SSM_STATE)))
    bt_re = p["b_re"].transpose(0, 2, 1).reshape(_PARAM_SHAPE)
    bt_im = p["b_im"].transpose(0, 2, 1).reshape(_PARAM_SHAPE)
    a_re_rep, a_im_rep, bb_re, bb_im = _ssm_params(lr, li, ls, bt_re, bt_im)
    ar = a_re_rep[::SSM_GROUP].reshape(SSM_CHUNKS, 1, CHUNK_S)
    ai = a_im_rep[::SSM_GROUP].reshape(SSM_CHUNKS, 1, CHUNK_S)
    chunked = lambda t: t.reshape(SSM_CHUNKS, SSM_GROUPS // SSM_CHUNKS, SSM_GROUP, SSM_STATE)
    bbr = _block_diag(chunked(bb_re)).astype(BF16)
    bbi = _block_diag(chunked(bb_im)).astype(BF16)
    to_cc = lambda c: _block_diag(chunked(c).transpose(0, 1, 3, 2)).astype(BF16)
    ccr, cci = to_cc(p["c_re"]), to_cc(p["c_im"])

    hb, z = _in_proj(x, g_mix, w_in_r)
    qn, kn, vb, ub, c128 = _attn_prep(z, gq, gk, bf, gg)
    qh, kh, vh = _to_heads(qn), _to_heads(kn), _to_heads(vb)
    crow = c128[:, :HEADS].T.reshape(HEADS, 1, s)
    oh, lse = _attn_fwd(qh, kh, vh, crow)
    att = _from_heads(oh)
    xr, xi, y = _ssm_fwd(ub, z, bbr, bbi, ar, ai, ccr, cci, dsk)
    x1, mixb, h2b = _mix_out(y, att, x, w_glu_b, b_glu, g_att, g_ssm, w_out_b, g_ffn)
    up = _mm(h2b, w_up_b, name="ffn_up", tm=1024, tn=1408, tk=1024)
    act = _conv_act(up, conv_w_full, conv_b)
    dy, dyb, loss_blk = _down_loss(act, w_down_b, x1, tgt)

    d_w_down = _mm(act, dyb, ta=True, name="d_w_down", tm=1408, tn=1024, tk=2048)
    dact = _mm(dyb, w_down_b, tb=True, name="d_act", tm=1024, tn=1408, tk=1024)
    dupb, dcw = _conv_act_bwd(up, dact, conv_w_full, conv_b)
    d_w_up = _mm(h2b, dupb, ta=True, name="d_w_up", tm=1024, tn=1408, tk=2048)
    dh2 = _mm(dupb, w_up_b, tb=True, name="d_h2", tm=1024, tn=1024, tk=1408)
    dx1, dx1b, datt, dys, d_w_glu, d_g_ffn, d_g_att, d_g_ssm, d_b_glu = _mix_bwd(
        dy, dh2, x1, g_ffn, w_out_b, y, att, w_glu_b, b_glu, g_att, g_ssm)
    d_w_out = _mm(mixb, dx1b, ta=True, name="d_w_out", tm=1024, tn=1024, tk=2048)
    du, dbbr, dbbi, dccr, dcci, dar, dai, dd = _ssm_bwd(dys, z, ub, xr, xi, bbr, bbi, ar, ai, ccr, cci, dsk)
    doh = _to_heads(datt)
    dqh, dkh, dvh, dcrow = _attn_bwd(qh, kh, vh, crow, _attn_dl(qh, kh, vh, crow, lse, doh), lse, doh)
    dc128 = jnp.pad(dcrow.reshape(HEADS, s).T, ((0, 0), (0, LANES - HEADS)))
    dzb, d_gq, d_gk, d_bf = _prep_bwd(z, _from_heads(dqh), _from_heads(dkh), _from_heads(dvh), du, dc128, gq, gk, bf, gg)
    d_w_in_r = _mm(hb, dzb, ta=True, name="d_w_in", tm=512, tn=Z_COLS, tk=2048)
    dh = _mm(dzb, w_in_r, tb=True, name="d_h", tm=1024, tn=1024, tk=Z_COLS)
    dx, d_g_mix = _in_norm_bwd(x, g_mix, dh, dx1)

    unchunk = lambda t: t.reshape(_PARAM_SHAPE)
    dbb_re = unchunk(_diag_blocks(dbbr, SSM_GROUP, SSM_STATE))
    dbb_im = unchunk(_diag_blocks(dbbi, SSM_GROUP, SSM_STATE))
    first_row = (jnp.arange(_PARAM_SHAPE[0]) % SSM_GROUP == 0)[:, None]
    da_re = jnp.where(first_row, rep(dar.reshape(SSM_GROUPS, SSM_STATE)), 0.0)
    da_im = jnp.where(first_row, rep(dai.reshape(SSM_GROUPS, SSM_STATE)), 0.0)
    expand_t = (jnp.arange(SSM_GROUPS)[:, None] == (jnp.arange(_PARAM_SHAPE[0]) // SSM_GROUP)[None, :]).astype(BF16)
    d_lr, d_li, d_ls, d_bt_re, d_bt_im = _ssm_params_bwd(lr, li, ls, bt_re, bt_im, da_re, da_im, dbb_re, dbb_im, expand_t)
    from_bt = lambda t: t.reshape(SSM_GROUPS, SSM_GROUP, SSM_STATE).transpose(0, 2, 1)
    from_cc = lambda t: _diag_blocks(t, SSM_STATE, SSM_GROUP).transpose(0, 1, 3, 2).reshape(SSM_GROUPS, SSM_GROUP, SSM_STATE)

    small = {
        "g_mix": d_g_mix, "b_f": d_bf[0, :HEADS], "g_q": d_gq.reshape(HEADS, HEAD_DIM).sum(0),
        "g_k": d_gk.reshape(HEADS, HEAD_DIM).sum(0), "lambda_re": d_lr, "lambda_im": d_li, "log_step": d_ls,
        "b_re": from_bt(d_bt_re), "b_im": from_bt(d_bt_im), "c_re": from_cc(dccr), "c_im": from_cc(dcci),
        "d_skip": dd, "b_glu": d_b_glu, "g_attn_out": d_g_att, "g_ssm_out": d_g_ssm, "g_ffn": d_g_ffn,
        "conv_b": dcw[3],
    }
    big = {"w_in": _restore_in_cols(d_w_in_r), "w_glu": d_w_glu, "w_out": d_w_out, "w_up": d_w_up, "w_down": d_w_down}
    return loss_blk[0, 0], dx, big, small, dcw[0:3]


def kernel(x, g_mix, w_in, b_f, g_q, g_k, lambda_re, lambda_im, log_step, b_re, b_im, c_re, c_im, d_skip, w_glu, b_glu, g_attn_out, g_ssm_out, w_out, g_ffn, w_up, conv_w, conv_b, w_down, loss_target, m_g_mix, m_w_in, m_b_f, m_g_q, m_g_k, m_lambda_re, m_lambda_im, m_log_step, m_b_re, m_b_im, m_c_re, m_c_im, m_d_skip, m_w_glu, m_b_glu, m_g_attn_out, m_g_ssm_out, m_w_out, m_g_ffn, m_w_up, m_conv_w, m_conv_b, m_w_down, v_g_mix, v_w_in, v_b_f, v_g_q, v_g_k, v_lambda_re, v_lambda_im, v_log_step, v_b_re, v_b_im, v_c_re, v_c_im, v_d_skip, v_w_glu, v_b_glu, v_g_attn_out, v_g_ssm_out, v_w_out, v_g_ffn, v_w_up, v_conv_w, v_conv_b, v_w_down):
    args = dict(locals())
    order = ["g_mix", "w_in", "b_f", "g_q", "g_k", "lambda_re", "lambda_im", "log_step", "b_re", "b_im", "c_re", "c_im",
             "d_skip", "w_glu", "b_glu", "g_attn_out", "g_ssm_out", "w_out", "g_ffn", "w_up", "conv_w", "conv_b", "w_down"]
    cx, cy, cc = lax.axis_index("x"), lax.axis_index("y"), lax.axis_index("c")
    chip = 2 * cx + cy

    big_names = ("w_in", "w_glu", "w_out", "w_up", "w_down")
    shards = [args[n].astype(BF16) for n in big_names]
    g_in, g_glu, g_out, g_up, g_down, g_cw = _gather_weights(shards, conv_w)
    own = lambda stacked, mine: lax.dynamic_update_slice(stacked, mine[None], (chip,) + (0,) * mine.ndim)
    g_in, g_glu, g_out, g_down = own(g_in, shards[0]), own(g_glu, shards[1]), own(g_out, shards[2]), own(g_down, shards[4])
    w_up_b = lax.dynamic_update_slice(g_up, shards[3], (0, chip * UP_COLS))
    w_in_r = _reorder_in_cols(g_in.transpose(1, 0, 2).reshape(D_MODEL, IN_COLS))
    w_glu_b = g_glu.reshape(SSM_W, SSM_W)
    w_out_b = g_out.reshape(D_MODEL, D_MODEL)
    w_down_b = g_down.reshape(D_FF, D_MODEL)
    conv_w_full = own(g_cw, conv_w).transpose(1, 0, 2).reshape(3, 2 * D_FF)

    loss_part, dx, big, small, d_conv_w = _local_step(x[0], loss_target[0], w_in_r, w_glu_b, w_out_b, w_up_b, w_down_b,
                                                      conv_w_full, args)
    loss = lax.psum(loss_part, ("x", "y", "c"))

    place = jnp.stack([chip, cc]).astype(jnp.int32)
    halves = lambda t: t.reshape(N_CHIPS, 2, t.shape[0] // (2 * N_CHIPS), t.shape[1])
    d_in = jnp.pad(big["w_in"], ((0, 0), (0, Z_COLS - IN_COLS)))
    grads = [d_in, halves(big["w_glu"]), halves(big["w_out"]), big["w_up"], halves(big["w_down"])]
    landed = _swap_halves(grads)
    sums = [_add_half(g, l, place, name="add_" + n) for g, l, n in zip(grads, landed, big_names)]
    lands = _scatter_chips(sums)
    col_blocks = (LANES, SSM_W, D_MODEL, UP_COLS, D_MODEL)
    strides = (IN_STRIDE, 0, 0, UP_COLS, 0)
    reds = _join_halves([_sum_chips(s, l, place, name="sum_" + n, tc=tc, window_stride=st)
                         for s, l, n, tc, st in zip(sums, lands, big_names, col_blocks, strides)])
    g_big = dict(zip(big_names, reds))
    g_big["w_in"] = lax.dynamic_slice_in_dim(reds[0], 2 * chip, IN_COLS // N_CHIPS, axis=1)

    small_names = [n for n, _ in _SMALL]
    small_shapes = [sh for _, sh in _SMALL]
    gsum = _allreduce_small(_pack_small([small[n] for n in small_names], extra=[d_conv_w]))
    g_small = _unpack_small(gsum, small_shapes + [(3, 2 * D_FF)])
    g_conv_w = lax.dynamic_slice_in_dim(g_small[-1], chip * (2 * D_FF // N_CHIPS), 2 * D_FF // N_CHIPS, axis=1)
    g_small = dict(zip(small_names, g_small[:-1]))

    grad, delta, new_m, new_v = {}, {}, {}, {}
    for n in ("w_in", "w_glu", "w_out", "w_up", "w_down"):
        grad[n] = g_big[n]
        delta[n], new_m[n], new_v[n] = _adamw(args[n], g_big[n], args["m_" + n], args["v_" + n], name="adamw_" + n)
    grad["conv_w"] = g_conv_w
    delta["conv_w"], new_m["conv_w"], new_v["conv_w"] = _adamw(conv_w, g_conv_w, m_conv_w, v_conv_w, name="adamw_conv_w")
    stepped = _adamw_small([args[n] for n in small_names], [g_small[n] for n in small_names],
                           [args["m_" + n] for n in small_names], [args["v_" + n] for n in small_names])
    for i, n in enumerate(small_names):
        grad[n] = g_small[n]
        delta[n], new_m[n], new_v[n] = stepped[3 * i:3 * i + 3]

    return (loss, dx[None], *[grad[n] for n in order], *[delta[n] for n in order], *[new_m[n] for n in order],
            *[new_v[n] for n in order])
```

```python
import math

import jax
import jax.numpy as jnp
from jax import lax
from jax.experimental import pallas as pl
from jax.experimental.pallas import tpu as pltpu

F32 = jnp.float32
BF16 = jnp.bfloat16

D_MODEL = 1024
HEADS = 8
HEAD_DIM = 64
ATTN_W = 512
SSM_W = 512
SSM_GROUPS = 32
SSM_GROUP = 16
SSM_STATE = 64
N_STATE = SSM_GROUPS * SSM_STATE
D_FF = 2816
IN_COLS = 2056
Z_COLS = 2176
U_COL0 = 1536
F_COL0 = 2048
EPS = 1e-6
NEG_INF = -1e30
N_CHIPS = 4
LANES = 128
SUBLANES = 8
SSM_CHUNKS = 4
CHUNK_U = SSM_W // SSM_CHUNKS
CHUNK_S = N_STATE // SSM_CHUNKS
HEADS_PER_STEP = 2
STRIP = 128
N_STRIPS = D_FF // STRIP

ROWS_IN, ROWS_GLU, ROWS_OUT, ROWS_UP, ROWS_DOWN = 514, 64, 256, 1408, 704
OFF_GLU = ROWS_IN
OFF_OUT = OFF_GLU + ROWS_GLU
OFF_UP = OFF_OUT + ROWS_OUT
OFF_DOWN = OFF_UP + ROWS_UP
OFF_SPARE = OFF_DOWN + ROWS_DOWN
PACK_ROWS = 2976
HALF_ROWS = PACK_ROWS // 2
CONVW_ROWS = 9

ADAM_LR = 0.001
ADAM_B1 = 0.9
ADAM_B2 = 0.999
ADAM_EPS = 1e-08
ADAM_WD = 0.01
ADAM_STEP = 10

VMEM_LIMIT = 56 * 1024 * 1024
MESH = pl.DeviceIdType.MESH


def _pallas(body, **kw):
    return pl.pallas_call(body, **kw)


def _pcall(body, *, name, out_shape, in_specs, out_specs, grid=(), scratch_shapes=(), dims=None):
    params = pltpu.CompilerParams(dimension_semantics=dims, vmem_limit_bytes=VMEM_LIMIT)
    return _pallas(body, name=name, grid=grid, in_specs=in_specs, out_specs=out_specs,
                   out_shape=out_shape, scratch_shapes=scratch_shapes, compiler_params=params)


def _sds(shape, dtype=F32):
    return jax.ShapeDtypeStruct(shape, dtype)


def _dot(a, b):
    return jnp.dot(a, b, preferred_element_type=F32)


def _dot_nt(a, b):
    return lax.dot_general(a, b, (((1,), (1,)), ((), ())), preferred_element_type=F32)


def _dot_tn(a, b):
    return lax.dot_general(a, b, (((0,), (0,)), ((), ())), preferred_element_type=F32)


def _split3(x):
    hi = x.astype(BF16)
    r = x - hi.astype(F32)
    mid = r.astype(BF16)
    lo = (r - mid.astype(F32)).astype(BF16)
    return hi, mid, lo


def _dot_exact_r(x, m01):
    hi, mid, lo = _split3(x)
    return _dot(hi, m01) + _dot(mid, m01) + _dot(lo, m01)


def _dot_exact_l(m01, x):
    hi, mid, lo = _split3(x)
    return _dot(m01, hi) + _dot(m01, mid) + _dot(m01, lo)


def _sigmoid(x):
    return 1.0 / (1.0 + jnp.exp(-x))


def _rms(x, g):
    r = lax.rsqrt(jnp.mean(x * x, axis=-1, keepdims=True) + EPS)
    return x * r * g


def _rms_bwd(x, g, dy):
    r = lax.rsqrt(jnp.mean(x * x, axis=-1, keepdims=True) + EPS)
    w = dy * g
    dx = r * w - x * (r * r * r) * jnp.mean(w * x, axis=-1, keepdims=True)
    dg = jnp.sum(dy * x * r, axis=0, keepdims=True)
    return dx, dg


_GELU_K = math.sqrt(2.0 / math.pi)
_GELU_C = 0.044715


def _gelu(y):
    return y * (0.5 * (1.0 + jnp.tanh(_GELU_K * (y + _GELU_C * (y * y * y)))))


def _gelu_grad(y):
    t = jnp.tanh(_GELU_K * (y + _GELU_C * (y * y * y)))
    return 0.5 * (1.0 + t) + 0.5 * y * (1.0 - t * t) * (_GELU_K * (1.0 + 3.0 * _GELU_C * y * y))


def _tile(n, pref):
    if n <= pref:
        return n
    divs = [t for t in range(LANES, n + 1, LANES) if n % t == 0]
    below = [t for t in divs if t <= pref]
    if below and 2 * below[-1] >= pref:
        return below[-1]
    above = [t for t in divs if t > pref]
    return above[0] if above else n


def _row_tile(s):
    return min(256, s)


def _mm(a, b, *, name, tm, tn, tk, ta=False, tb=False):
    if ta:
        kk, m = a.shape
    else:
        m, kk = a.shape
    n = b.shape[0] if tb else b.shape[1]
    tm, tn, tk = _tile(m, tm), _tile(n, tn), _tile(kk, tk)

    def body(a_ref, b_ref, o_ref):
        k = pl.program_id(2)
        if ta:
            part = _dot_tn(a_ref[...], b_ref[...])
        elif tb:
            part = _dot_nt(a_ref[...], b_ref[...])
        else:
            part = _dot(a_ref[...], b_ref[...])

        @pl.when(k == 0)
        def _():
            o_ref[...] = part

        @pl.when(k > 0)
        def _():
            o_ref[...] += part

    a_spec = pl.BlockSpec((tk, tm), lambda i, j, k: (k, i)) if ta else pl.BlockSpec((tm, tk), lambda i, j, k: (i, k))
    b_spec = pl.BlockSpec((tn, tk), lambda i, j, k: (j, k)) if tb else pl.BlockSpec((tk, tn), lambda i, j, k: (k, j))
    return _pcall(body, name=name, grid=(m // tm, n // tn, kk // tk), in_specs=[a_spec, b_spec],
                  out_specs=pl.BlockSpec((tm, tn), lambda i, j, k: (i, j)), out_shape=_sds((m, n)),
                  dims=("parallel", "parallel", "arbitrary"))(a, b)


def _in_proj(x, g_mix, w_in_r):
    s = x.shape[0]
    tm = _row_tile(s)

    def body(x_ref, g_ref, w_ref, h_ref, z_ref):
        h = _rms(x_ref[...], g_ref[...]).astype(BF16)
        h_ref[...] = h
        z_ref[...] = _dot(h, w_ref[...])

    return _pcall(body, name="in_proj", grid=(s // tm,),
                  in_specs=[pl.BlockSpec((tm, D_MODEL), lambda i: (i, 0)), pl.BlockSpec((1, D_MODEL), lambda i: (0, 0)),
                            pl.BlockSpec((D_MODEL, Z_COLS), lambda i: (0, 0))],
                  out_specs=[pl.BlockSpec((tm, D_MODEL), lambda i: (i, 0)), pl.BlockSpec((tm, Z_COLS), lambda i: (i, 0))],
                  out_shape=[_sds((s, D_MODEL), BF16), _sds((s, Z_COLS))], dims=("parallel",))(x, g_mix, w_in_r)


def _attn_prep(z, gq, gk, bf, gg):
    s = z.shape[0]
    tm = _row_tile(s)

    def body(z_ref, gq_ref, gk_ref, bf_ref, gg_ref, qn_ref, kn_ref, vb_ref, ub_ref, c_ref, carry_ref):
        i = pl.program_id(0)

        @pl.when(i == 0)
        def _():
            carry_ref[...] = jnp.zeros_like(carry_ref)

        gg_m = gg_ref[...]

        def head_norm(t, g):
            ssq = _dot_exact_r(t * t, gg_m)
            return t * lax.rsqrt(ssq * (1.0 / HEAD_DIM) + EPS) * g

        qn_ref[...] = head_norm(z_ref[:, 0:ATTN_W], gq_ref[...]).astype(BF16)
        kn_ref[...] = head_norm(z_ref[:, ATTN_W:2 * ATTN_W], gk_ref[...]).astype(BF16)
        vb_ref[...] = z_ref[:, 2 * ATTN_W:3 * ATTN_W].astype(BF16)
        ub_ref[...] = z_ref[:, U_COL0:U_COL0 + SSM_W].astype(BF16)
        fl = z_ref[:, F_COL0:F_COL0 + LANES] + bf_ref[...]
        lf = jnp.minimum(fl, 0.0) - jnp.log1p(jnp.exp(-jnp.abs(fl)))
        row = lax.broadcasted_iota(jnp.int32, (tm, tm), 0)
        col = lax.broadcasted_iota(jnp.int32, (tm, tm), 1)
        tri = (row >= col).astype(BF16)
        c = _dot_exact_l(tri, lf) + carry_ref[...]
        c_ref[...] = c
        carry_ref[...] = c[tm - 1:tm, :]

    row_spec = lambda w: pl.BlockSpec((tm, w), lambda i: (i, 0))
    const = lambda shape: pl.BlockSpec(shape, lambda i: (0, 0))
    return _pcall(body, name="attn_prep", grid=(s // tm,),
                  in_specs=[row_spec(Z_COLS), const((1, ATTN_W)), const((1, ATTN_W)), const((1, LANES)), const((ATTN_W, ATTN_W))],
                  out_specs=[row_spec(ATTN_W)] * 4 + [row_spec(LANES)],
                  out_shape=[_sds((s, ATTN_W), BF16)] * 4 + [_sds((s, LANES))],
                  scratch_shapes=[pltpu.VMEM((1, LANES), F32)], dims=("arbitrary",))(z, gq, gk, bf, gg)


def _attn_fwd(qh, kh, vh, crow):
    _, s, _ = qh.shape
    tq = _row_tile(s)
    scale = HEAD_DIM ** -0.5

    def body(q_ref, k_ref, v_ref, c_ref, o_ref, lse_ref):
        i = pl.program_id(1)

        def block(j, carry, diagonal):
            off = pl.multiple_of(j * tq, tq)
            out = []
            for hh in range(HEADS_PER_STEP):
                m, l, acc = carry[hh]
                sc = _dot_nt(q_ref[hh], k_ref[hh, pl.ds(off, tq), :]) * scale - c_ref[hh, :, pl.ds(off, tq)]
                if diagonal:
                    causal = lax.broadcasted_iota(jnp.int32, (tq, tq), 1) <= lax.broadcasted_iota(jnp.int32, (tq, tq), 0)
                    sc = jnp.where(causal, sc, NEG_INF)
                m_new = jnp.maximum(m, jnp.max(sc, axis=-1, keepdims=True))
                p = jnp.exp(sc - m_new)
                alpha = jnp.exp(m - m_new)
                l = alpha * l + jnp.sum(p, axis=-1, keepdims=True)
                acc = alpha * acc + _dot(p.astype(BF16), v_ref[hh, pl.ds(off, tq), :])
                out.append((m_new, l, acc))
            return tuple(out)

        one = (jnp.full((tq, 1), NEG_INF, F32), jnp.zeros((tq, 1), F32), jnp.zeros((tq, HEAD_DIM), F32))
        carry = lax.fori_loop(0, i, lambda j, c: block(j, c, False), (one,) * HEADS_PER_STEP)
        for hh, (m, l, acc) in enumerate(block(i, carry, True)):
            o_ref[hh] = acc / l
            lse_ref[hh] = m + jnp.log(l)

    hp = HEADS_PER_STEP
    blk = pl.BlockSpec((hp, tq, HEAD_DIM), lambda h, i: (h, i, 0))
    full = pl.BlockSpec((hp, s, HEAD_DIM), lambda h, i: (h, 0, 0))
    return _pcall(body, name="attn_fwd", grid=(HEADS // hp, s // tq),
                  in_specs=[blk, full, full, pl.BlockSpec((hp, 1, s), lambda h, i: (h, 0, 0))],
                  out_specs=[blk, pl.BlockSpec((hp, tq, 1), lambda h, i: (h, i, 0))],
                  out_shape=[_sds((HEADS, s, HEAD_DIM)), _sds((HEADS, s, 1))],
                  dims=("parallel", "parallel"))(qh, kh, vh, crow)


def _ssm_param_fn(lr, li, ls, br, bi):
    step = jnp.exp(ls)
    er = jnp.exp(lr * step)
    ab_re = er * jnp.cos(li * step)
    ab_im = er * jnp.sin(li * step)
    num_re = ab_re - 1.0
    num_im = ab_im
    den = lr * lr + li * li
    f_re = (num_re * lr + num_im * li) / den
    f_im = (num_im * lr - num_re * li) / den
    bb_re = f_re * br - f_im * bi
    bb_im = f_re * bi + f_im * br
    return ab_re, ab_im, bb_re, bb_im


_PARAM_SHAPE = (SSM_GROUPS * SSM_GROUP, SSM_STATE)


def _ssm_params(lr, li, ls, br, bi):
    def body(lr_ref, li_ref, ls_ref, br_ref, bi_ref, ar_ref, ai_ref, bbr_ref, bbi_ref):
        ar, ai, bbr, bbi = _ssm_param_fn(lr_ref[...], li_ref[...], ls_ref[...], br_ref[...], bi_ref[...])
        ar_ref[...] = ar
        ai_ref[...] = ai
        bbr_ref[...] = bbr
        bbi_ref[...] = bbi

    spec = pl.BlockSpec(_PARAM_SHAPE, lambda: (0, 0))
    return _pcall(body, name="ssm_params", in_specs=[spec] * 5, out_specs=[spec] * 4,
                  out_shape=[_sds(_PARAM_SHAPE)] * 4)(lr, li, ls, br, bi)


def _ssm_params_bwd(lr, li, ls, br, bi, dar, dai, dbbr, dbbi, expand_t):
    def body(lr_ref, li_ref, ls_ref, br_ref, bi_ref, dar_ref, dai_ref, dbbr_ref, dbbi_ref, et_ref,
             dlr_ref, dli_ref, dls_ref, dbr_ref, dbi_ref):
        _, vjp = jax.vjp(_ssm_param_fn, lr_ref[...], li_ref[...], ls_ref[...], br_ref[...], bi_ref[...])
        dlr, dli, dls, dbr, dbi = vjp((dar_ref[...], dai_ref[...], dbbr_ref[...], dbbi_ref[...]))
        et = et_ref[...]
        dlr_ref[...] = _dot_exact_l(et, dlr)
        dli_ref[...] = _dot_exact_l(et, dli)
        dls_ref[...] = jnp.sum(_dot_exact_l(et, dls), axis=-1, keepdims=True)
        dbr_ref[...] = dbr
        dbi_ref[...] = dbi

    spec = pl.BlockSpec(_PARAM_SHAPE, lambda: (0, 0))
    gspec = pl.BlockSpec((SSM_GROUPS, SSM_STATE), lambda: (0, 0))
    return _pcall(body, name="ssm_params_bwd",
                  in_specs=[spec] * 9 + [pl.BlockSpec((SSM_GROUPS, _PARAM_SHAPE[0]), lambda: (0, 0))],
                  out_specs=[gspec, gspec, pl.BlockSpec((SSM_GROUPS, 1), lambda: (0, 0)), spec, spec],
                  out_shape=[_sds((SSM_GROUPS, SSM_STATE))] * 2 + [_sds((SSM_GROUPS, 1))] + [_sds(_PARAM_SHAPE)] * 2,
                  )(lr, li, ls, br, bi, dar, dai, dbbr, dbbi, expand_t)


def _cmul(ar, ai, br, bi):
    return ar * br - ai * bi, ar * bi + ai * br


def _scan_consts(ar, ai, width, reverse):
    row = lax.broadcasted_iota(jnp.int32, (SUBLANES, width), 0)
    pw = [(ar, ai)]
    for _ in range(SUBLANES - 1):
        pw.append(_cmul(pw[-1][0], pw[-1][1], ar, ai))
    steps = []
    for d in (1, 2, 4):
        keep = (row < SUBLANES - d) if reverse else (row >= d)
        steps.append((d, jnp.where(keep, pw[d - 1][0], 0.0), jnp.where(keep, pw[d - 1][1], 0.0)))
    pr = jnp.zeros((SUBLANES, width), F32)
    pi = jnp.zeros((SUBLANES, width), F32)
    for r in range(SUBLANES):
        e = (SUBLANES - r) if reverse else (r + 1)
        pr = jnp.where(row == r, pw[e - 1][0], pr)
        pi = jnp.where(row == r, pw[e - 1][1], pi)
    return steps, pr, pi


def _scan_tile(xr, xi, cr, ci, consts, reverse):
    steps, pr, pi = consts
    for d, mr, mi in steps:
        sh = (SUBLANES - d) if reverse else d
        sr = pltpu.roll(xr, sh, 0)
        si = pltpu.roll(xi, sh, 0)
        xr, xi = xr + mr * sr - mi * si, xi + mr * si + mi * sr
    return xr + pr * cr - pi * ci, xi + pr * ci + pi * cr


def _ssm_fwd(ub, z, bbr, bbi, ar, ai, ccr, cci, dsk):
    s = ub.shape[0]
    tm = _row_tile(s)
    nt = tm // SUBLANES

    def body(ub_ref, u_ref, bbr_ref, bbi_ref, ar_ref, ai_ref, ccr_ref, cci_ref, dsk_ref,
             xr_ref, xi_ref, y_ref, cr_s, ci_s):
        i = pl.program_id(1)

        @pl.when(i == 0)
        def _():
            cr_s[...] = jnp.zeros_like(cr_s)
            ci_s[...] = jnp.zeros_like(ci_s)

        u_b = ub_ref[...]
        xr_ref[...] = _dot(u_b, bbr_ref[0])
        xi_ref[...] = _dot(u_b, bbi_ref[0])
        consts = _scan_consts(ar_ref[0], ai_ref[0], CHUNK_S, False)

        def tile(k, carry):
            cr, ci = carry
            sl = pl.ds(pl.multiple_of(k * SUBLANES, SUBLANES), SUBLANES)
            xr, xi = _scan_tile(xr_ref[sl, :], xi_ref[sl, :], cr, ci, consts, False)
            xr_ref[sl, :] = xr
            xi_ref[sl, :] = xi
            return xr[SUBLANES - 1:SUBLANES, :], xi[SUBLANES - 1:SUBLANES, :]

        cr, ci = lax.fori_loop(0, nt, tile, (cr_s[...], ci_s[...]))
        cr_s[...] = cr
        ci_s[...] = ci
        y_ref[...] = (_dot(xr_ref[...].astype(BF16), ccr_ref[0]) - _dot(xi_ref[...].astype(BF16), cci_ref[0])
                      + dsk_ref[...] * u_ref[...])

    ucol0 = U_COL0 // CHUNK_U
    wspec = lambda a, b: pl.BlockSpec((1, a, b), lambda j, i: (j, 0, 0))
    return _pcall(body, name="ssm_fwd", grid=(SSM_CHUNKS, s // tm),
                  in_specs=[pl.BlockSpec((tm, CHUNK_U), lambda j, i: (i, j)),
                            pl.BlockSpec((tm, CHUNK_U), lambda j, i: (i, ucol0 + j)),
                            wspec(CHUNK_U, CHUNK_S), wspec(CHUNK_U, CHUNK_S), wspec(1, CHUNK_S), wspec(1, CHUNK_S),
                            wspec(CHUNK_S, CHUNK_U), wspec(CHUNK_S, CHUNK_U),
                            pl.BlockSpec((1, CHUNK_U), lambda j, i: (0, j))],
                  out_specs=[pl.BlockSpec((tm, CHUNK_S), lambda j, i: (i, j)), pl.BlockSpec((tm, CHUNK_S), lambda j, i: (i, j)),
                             pl.BlockSpec((tm, CHUNK_U), lambda j, i: (i, j))],
                  out_shape=[_sds((s, N_STATE)), _sds((s, N_STATE)), _sds((s, SSM_W))],
                  scratch_shapes=[pltpu.VMEM((1, CHUNK_S), F32)] * 2,
                  dims=("parallel", "arbitrary"))(ub, z, bbr, bbi, ar, ai, ccr, cci, dsk)


def _ssm_glu(y, w_glu, b_glu):
    ge = _gelu(y)
    sg = _sigmoid(_dot(ge.astype(BF16), w_glu) + b_glu)
    return ge, sg


def _mix_out(y, att, x, w_glu, b_glu, g_att, g_ssm, w_out, g_ffn):
    s = x.shape[0]
    tm = _row_tile(s)

    def body(y_ref, att_ref, x_ref, wg_ref, bg_ref, ga_ref, gs_ref, wo_ref, gf_ref, x1_ref, mix_ref, h2_ref):
        ge, sg = _ssm_glu(y_ref[...], wg_ref[...], bg_ref[...])
        ms = _rms(ge * sg, gs_ref[...]).astype(BF16)
        ma = _rms(att_ref[...], ga_ref[...]).astype(BF16)
        mix_ref[:, 0:ATTN_W] = ma
        mix_ref[:, ATTN_W:D_MODEL] = ms
        x1 = x_ref[...] + (_dot(ma, wo_ref[0:ATTN_W, :]) + _dot(ms, wo_ref[ATTN_W:D_MODEL, :]))
        x1_ref[...] = x1
        h2_ref[...] = _rms(x1, gf_ref[...]).astype(BF16)

    row = lambda w: pl.BlockSpec((tm, w), lambda i: (i, 0))
    const = lambda a, b: pl.BlockSpec((a, b), lambda i: (0, 0))
    return _pcall(body, name="mix_out", grid=(s // tm,),
                  in_specs=[row(SSM_W), row(ATTN_W), row(D_MODEL), const(SSM_W, SSM_W), const(1, SSM_W), const(1, ATTN_W),
                            const(1, SSM_W), const(D_MODEL, D_MODEL), const(1, D_MODEL)],
                  out_specs=[row(D_MODEL)] * 3,
                  out_shape=[_sds((s, D_MODEL)), _sds((s, D_MODEL), BF16), _sds((s, D_MODEL), BF16)],
                  dims=("parallel",))(y, att, x, w_glu, b_glu, g_att, g_ssm, w_out, g_ffn)


def _conv_rows(pad_ref, w, b, s):
    y = b + pad_ref[pl.ds(SUBLANES - 2, s), :] * w[0:1, :]
    y = y + pad_ref[pl.ds(SUBLANES - 1, s), :] * w[1:2, :]
    return y + pad_ref[pl.ds(SUBLANES, s), :] * w[2:3, :]


def _conv_act(up, conv_w, conv_b):
    s = up.shape[0]

    def body(ug_ref, uv_ref, wg_ref, wv_ref, bg_ref, bv_ref, act_ref, pg_ref, pv_ref):
        zero = jnp.zeros((SUBLANES, STRIP), F32)
        pg_ref[0:SUBLANES, :] = zero
        pv_ref[0:SUBLANES, :] = zero
        pg_ref[pl.ds(SUBLANES, s), :] = ug_ref[...]
        pv_ref[pl.ds(SUBLANES, s), :] = uv_ref[...]
        hg = _conv_rows(pg_ref, wg_ref[...], bg_ref[...], s)
        hv = _conv_rows(pv_ref, wv_ref[...], bv_ref[...], s)
        act_ref[...] = (hg * _sigmoid(hg) * hv).astype(BF16)

    strip = lambda off: pl.BlockSpec((s, STRIP), lambda j: (0, j + off))
    wsp = lambda off: pl.BlockSpec((3, STRIP), lambda j: (0, j + off))
    bsp = lambda off: pl.BlockSpec((1, STRIP), lambda j: (0, j + off))
    return _pcall(body, name="conv_act", grid=(N_STRIPS,),
                  in_specs=[strip(0), strip(N_STRIPS), wsp(0), wsp(N_STRIPS), bsp(0), bsp(N_STRIPS)],
                  out_specs=pl.BlockSpec((s, STRIP), lambda j: (0, j)), out_shape=_sds((s, D_FF), BF16),
                  scratch_shapes=[pltpu.VMEM((s + SUBLANES, STRIP), F32)] * 2,
                  dims=("parallel",))(up, up, conv_w, conv_w, conv_b, conv_b)


def _down_loss(act, w_down, x1, tgt):
    s = x1.shape[0]
    tm = _row_tile(s)

    def body(a_ref, w_ref, x1_ref, t_ref, dy_ref, dyb_ref, loss_ref):
        i = pl.program_id(0)

        @pl.when(i == 0)
        def _():
            loss_ref[...] = jnp.zeros_like(loss_ref)

        diff = x1_ref[...] + _dot(a_ref[...], w_ref[...]) - t_ref[...]
        dy = diff * (1.0 / D_MODEL)
        dy_ref[...] = dy
        dyb_ref[...] = dy.astype(BF16)
        loss_ref[...] += 0.5 * jnp.sum(diff * dy)

    row = lambda w: pl.BlockSpec((tm, w), lambda i: (i, 0))
    return _pcall(body, name="down_loss", grid=(s // tm,),
                  in_specs=[row(D_FF), pl.BlockSpec((D_FF, D_MODEL), lambda i: (0, 0)), row(D_MODEL), row(D_MODEL)],
                  out_specs=[row(D_MODEL), row(D_MODEL), pl.BlockSpec((SUBLANES, LANES), lambda i: (0, 0))],
                  out_shape=[_sds((s, D_MODEL)), _sds((s, D_MODEL), BF16), _sds((SUBLANES, LANES))],
                  dims=("arbitrary",))(act, w_down, x1, tgt)


def _conv_act_bwd(up, dact, conv_w, conv_b):
    s = up.shape[0]

    def body(ug_ref, uv_ref, uo_ref, da_ref, wg_ref, wv_ref, wo_ref, bg_ref, bv_ref, dup_ref, dcw_ref,
             pg_ref, pv_ref, pd_ref):
        t = pl.program_id(0)
        zero = jnp.zeros((SUBLANES, STRIP), F32)
        pg_ref[0:SUBLANES, :] = zero
        pv_ref[0:SUBLANES, :] = zero
        pg_ref[pl.ds(SUBLANES, s), :] = ug_ref[...]
        pv_ref[pl.ds(SUBLANES, s), :] = uv_ref[...]
        hg = _conv_rows(pg_ref, wg_ref[...], bg_ref[...], s)
        hv = _conv_rows(pv_ref, wv_ref[...], bv_ref[...], s)
        sg = _sigmoid(hg)
        da = da_ref[...]
        d_gate = da * hv * (sg * (1.0 + hg * (1.0 - sg)))
        d_val = da * (hg * sg)
        dh = jnp.where(t == 0, d_gate, d_val)
        pg_ref[pl.ds(SUBLANES, s), :] = uo_ref[...]
        pd_ref[pl.ds(0, s), :] = dh
        pd_ref[pl.ds(s, SUBLANES), :] = zero
        w = wo_ref[...]
        dup = dh * w[2:3, :] + pd_ref[pl.ds(1, s), :] * w[1:2, :] + pd_ref[pl.ds(2, s), :] * w[0:1, :]
        dup_ref[...] = dup.astype(BF16)
        rows = [jnp.sum(dh * pg_ref[pl.ds(SUBLANES - 2 + k, s), :], axis=0, keepdims=True) for k in range(3)]
        rows.append(jnp.sum(dh, axis=0, keepdims=True))
        rid = lax.broadcasted_iota(jnp.int32, (SUBLANES, STRIP), 0)
        out = jnp.zeros((SUBLANES, STRIP), F32)
        for k, r in enumerate(rows):
            out = jnp.where(rid == k, r, out)
        dcw_ref[...] = out

    strip = lambda f: pl.BlockSpec((s, STRIP), f)
    wsp = lambda f: pl.BlockSpec((3, STRIP), f)
    bsp = lambda f: pl.BlockSpec((1, STRIP), f)
    gate = lambda t, j: (0, j)
    val = lambda t, j: (0, j + N_STRIPS)
    own = lambda t, j: (0, t * N_STRIPS + j)
    return _pcall(body, name="conv_act_bwd", grid=(2, N_STRIPS),
                  in_specs=[strip(gate), strip(val), strip(own), strip(gate), wsp(gate), wsp(val), wsp(own), bsp(gate), bsp(val)],
                  out_specs=[strip(own), pl.BlockSpec((SUBLANES, STRIP), own)],
                  out_shape=[_sds((s, 2 * D_FF), BF16), _sds((SUBLANES, 2 * D_FF))],
                  scratch_shapes=[pltpu.VMEM((s + SUBLANES, STRIP), F32)] * 3,
                  dims=("parallel", "parallel"))(up, up, up, dact, conv_w, conv_w, conv_w, conv_b, conv_b)


def _mix_bwd(dy, dh2, x1, g_ffn, w_out, y, att, w_glu, b_glu, g_att, g_ssm):
    s = dy.shape[0]
    tm = _row_tile(s)

    def body(dy_ref, dh2_ref, x1_ref, gf_ref, wo_ref, y_ref, att_ref, wg_ref, bg_ref, ga_ref, gs_ref,
             dx1_ref, dx1b_ref, datt_ref, dys_ref, dwg_ref, dgf_ref, dga_ref, dgs_ref, dbg_ref):
        i = pl.program_id(0)

        @pl.when(i == 0)
        def _():
            for r in (dwg_ref, dgf_ref, dga_ref, dgs_ref, dbg_ref):
                r[...] = jnp.zeros_like(r)

        dxn, dgf = _rms_bwd(x1_ref[...], gf_ref[...], dh2_ref[...])
        dx1 = dy_ref[...] + dxn
        dx1_ref[...] = dx1
        dx1b = dx1.astype(BF16)
        dx1b_ref[...] = dx1b
        dgf_ref[...] += dgf
        dma = _dot_nt(dx1b, wo_ref[0:ATTN_W, :])
        dms = _dot_nt(dx1b, wo_ref[ATTN_W:D_MODEL, :])
        datt, dga = _rms_bwd(att_ref[...], ga_ref[...], dma)
        datt_ref[...] = datt
        dga_ref[...] += dga
        yv = y_ref[...]
        ge, sg = _ssm_glu(yv, wg_ref[...], bg_ref[...])
        dssm, dgs = _rms_bwd(ge * sg, gs_ref[...], dms)
        dgs_ref[...] += dgs
        dgl = dssm * ge * sg * (1.0 - sg)
        dglb = dgl.astype(BF16)
        dge = dssm * sg + _dot_nt(dglb, wg_ref[...])
        dbg_ref[...] += jnp.sum(dgl, axis=0, keepdims=True)
        dwg_ref[...] += _dot_tn(ge.astype(BF16), dglb)
        dys_ref[...] = dge * _gelu_grad(yv)

    row = lambda w: pl.BlockSpec((tm, w), lambda i: (i, 0))
    const = lambda a, b: pl.BlockSpec((a, b), lambda i: (0, 0))
    return _pcall(body, name="mix_bwd", grid=(s // tm,),
                  in_specs=[row(D_MODEL), row(D_MODEL), row(D_MODEL), const(1, D_MODEL), const(D_MODEL, D_MODEL), row(SSM_W),
                            row(ATTN_W), const(SSM_W, SSM_W), const(1, SSM_W), const(1, ATTN_W), const(1, SSM_W)],
                  out_specs=[row(D_MODEL), row(D_MODEL), row(ATTN_W), row(SSM_W), const(SSM_W, SSM_W), const(1, D_MODEL),
                             const(1, ATTN_W), const(1, SSM_W), const(1, SSM_W)],
                  out_shape=[_sds((s, D_MODEL)), _sds((s, D_MODEL), BF16), _sds((s, ATTN_W)), _sds((s, SSM_W)),
                             _sds((SSM_W, SSM_W)), _sds((1, D_MODEL)), _sds((1, ATTN_W)), _sds((1, SSM_W)), _sds((1, SSM_W))],
                  dims=("arbitrary",))(dy, dh2, x1, g_ffn, w_out, y, att, w_glu, b_glu, g_att, g_ssm)


def _ssm_bwd(dys, z, ub, xr, xi, bbr, bbi, ar, ai, ccr, cci, dsk):
    s = dys.shape[0]
    tm = _row_tile(s)
    nb = s // tm
    nt = tm // SUBLANES

    def body(dy_ref, u_ref, ub_ref, xr_ref, xi_ref, xrp_ref, xip_ref, bbr_ref, bbi_ref, ar_ref, ai_ref, ccr_ref,
             cci_ref, dsk_ref, du_ref, dbbr_ref, dbbi_ref, dccr_ref, dcci_ref, dar_ref, dai_ref, dd_ref,
             gr_s, gi_s, cr_s, ci_s, accr_s, acci_s):
        i = pl.program_id(1)
        first_block = i == nb - 1

        @pl.when(i == 0)
        def _():
            for r in (cr_s, ci_s, accr_s, acci_s, dbbr_ref, dbbi_ref, dccr_ref, dcci_ref, dd_ref):
                r[...] = jnp.zeros_like(r)

        dy = dy_ref[...]
        dyb = dy.astype(BF16)
        gr_s[...] = _dot_nt(dyb, ccr_ref[0])
        gi_s[...] = -_dot_nt(dyb, cci_ref[0])
        consts = _scan_consts(ar_ref[0], -ai_ref[0], CHUNK_S, True)
        row = lax.broadcasted_iota(jnp.int32, (SUBLANES, CHUNK_S), 0)

        def tile(kk, carry):
            cr, ci, accr, acci = carry
            k = nt - 1 - kk
            sl = pl.ds(pl.multiple_of(k * SUBLANES, SUBLANES), SUBLANES)
            gr, gi = _scan_tile(gr_s[sl, :], gi_s[sl, :], cr, ci, consts, True)
            gr_s[sl, :] = gr
            gi_s[sl, :] = gi
            slp = pl.ds(pl.multiple_of(jnp.maximum(k - 1, 0) * SUBLANES, SUBLANES), SUBLANES)
            inner = k > 0
            pr_t = jnp.where(inner, xr_ref[slp, :], xrp_ref[...])
            pi_t = jnp.where(inner, xi_ref[slp, :], xip_ref[...])
            live = jnp.logical_or(inner, jnp.logical_not(first_block))
            top_r = jnp.where(live, pltpu.roll(pr_t, 1, 0), 0.0)
            top_i = jnp.where(live, pltpu.roll(pi_t, 1, 0), 0.0)
            xpr = jnp.where(row == 0, top_r, pltpu.roll(xr_ref[sl, :], 1, 0))
            xpi = jnp.where(row == 0, top_i, pltpu.roll(xi_ref[sl, :], 1, 0))
            accr = accr + gr * xpr + gi * xpi
            acci = acci + gi * xpr - gr * xpi
            return gr[0:1, :], gi[0:1, :], accr, acci

        zeros = jnp.zeros((SUBLANES, CHUNK_S), F32)
        cr, ci, accr, acci = lax.fori_loop(0, nt, tile, (cr_s[...], ci_s[...], zeros, zeros))
        cr_s[...] = cr
        ci_s[...] = ci
        accr_s[...] += accr
        acci_s[...] += acci
        grb = gr_s[...].astype(BF16)
        gib = gi_s[...].astype(BF16)
        u_b = ub_ref[...]
        du_ref[...] = _dot_nt(grb, bbr_ref[0]) + _dot_nt(gib, bbi_ref[0]) + dsk_ref[...] * dy
        dbbr_ref[0] += _dot_tn(u_b, grb)
        dbbi_ref[0] += _dot_tn(u_b, gib)
        dccr_ref[0] += _dot_tn(xr_ref[...].astype(BF16), dyb)
        dcci_ref[0] -= _dot_tn(xi_ref[...].astype(BF16), dyb)
        dd_ref[...] += jnp.sum(dy * u_ref[...], axis=0, keepdims=True)

        @pl.when(i == nb - 1)
        def _():
            dar_ref[0] = jnp.sum(accr_s[...], axis=0, keepdims=True)
            dai_ref[0] = jnp.sum(acci_s[...], axis=0, keepdims=True)

    ucol0 = U_COL0 // CHUNK_U
    tiles_per_block = tm // SUBLANES
    rb = lambda i: nb - 1 - i
    wspec = lambda a, b: pl.BlockSpec((1, a, b), lambda j, i: (j, 0, 0))
    xblk = pl.BlockSpec((tm, CHUNK_S), lambda j, i: (rb(i), j))
    xprev = pl.BlockSpec((SUBLANES, CHUNK_S), lambda j, i: (jnp.maximum(rb(i) * tiles_per_block - 1, 0), j))
    ublk = pl.BlockSpec((tm, CHUNK_U), lambda j, i: (rb(i), j))
    return _pcall(body, name="ssm_bwd", grid=(SSM_CHUNKS, nb),
                  in_specs=[ublk, pl.BlockSpec((tm, CHUNK_U), lambda j, i: (rb(i), ucol0 + j)), ublk, xblk, xblk, xprev, xprev,
                            wspec(CHUNK_U, CHUNK_S), wspec(CHUNK_U, CHUNK_S), wspec(1, CHUNK_S), wspec(1, CHUNK_S),
                            wspec(CHUNK_S, CHUNK_U), wspec(CHUNK_S, CHUNK_U), pl.BlockSpec((1, CHUNK_U), lambda j, i: (0, j))],
                  out_specs=[ublk, wspec(CHUNK_U, CHUNK_S), wspec(CHUNK_U, CHUNK_S), wspec(CHUNK_S, CHUNK_U),
                             wspec(CHUNK_S, CHUNK_U), wspec(1, CHUNK_S), wspec(1, CHUNK_S),
                             pl.BlockSpec((1, CHUNK_U), lambda j, i: (0, j))],
                  out_shape=[_sds((s, SSM_W)), _sds((SSM_CHUNKS, CHUNK_U, CHUNK_S)), _sds((SSM_CHUNKS, CHUNK_U, CHUNK_S)),
                             _sds((SSM_CHUNKS, CHUNK_S, CHUNK_U)), _sds((SSM_CHUNKS, CHUNK_S, CHUNK_U)),
                             _sds((SSM_CHUNKS, 1, CHUNK_S)), _sds((SSM_CHUNKS, 1, CHUNK_S)), _sds((1, SSM_W))],
                  scratch_shapes=[pltpu.VMEM((tm, CHUNK_S), F32)] * 2 + [pltpu.VMEM((1, CHUNK_S), F32)] * 2
                                 + [pltpu.VMEM((SUBLANES, CHUNK_S), F32)] * 2,
                  dims=("parallel", "arbitrary"))(dys, z, ub, xr, xi, xr, xi, bbr, bbi, ar, ai, ccr, cci, dsk)


def _attn_probs(q, ks, cs, lse, scale, diagonal):
    p = jnp.exp(_dot_nt(q, ks) * scale - cs - lse)
    if diagonal:
        tq, tk = p.shape
        causal = lax.broadcasted_iota(jnp.int32, (tq, tk), 1) <= lax.broadcasted_iota(jnp.int32, (tq, tk), 0)
        p = jnp.where(causal, p, 0.0)
    return p


def _attn_bwd(qh, kh, vh, crow, lse, doh):
    _, s, _ = qh.shape
    tq = _row_tile(s)
    nq = s // tq
    scale = HEAD_DIM ** -0.5
    hp = HEADS_PER_STEP

    def body(q_ref, k_ref, v_ref, c_ref, lse_ref, do_ref, dq_ref, dk_ref, dv_ref, dc_ref, p_s, dp_s):
        i = pl.program_id(1)

        @pl.when(i == 0)
        def _():
            for r in (dk_ref, dv_ref, dc_ref):
                r[...] = jnp.zeros_like(r)

        dobs = [do_ref[hh].astype(BF16) for hh in range(hp)]

        def first(j, dls, diagonal):
            off = pl.multiple_of(j * tq, tq)
            out = []
            for hh in range(hp):
                p = _attn_probs(q_ref[hh], k_ref[hh, pl.ds(off, tq), :], c_ref[hh, :, pl.ds(off, tq)], lse_ref[hh],
                                scale, diagonal)
                dp = _dot_nt(dobs[hh], v_ref[hh, pl.ds(off, tq), :])
                p_s[hh, j] = p
                dp_s[hh, j] = dp
                out.append(dls[hh] + jnp.sum(p * dp, axis=-1, keepdims=True))
            return tuple(out)

        zero_col = jnp.zeros((tq, 1), F32)
        dls = lax.fori_loop(0, i, lambda j, c: first(j, c, False), (zero_col,) * hp)
        dls = first(i, dls, True)

        def second(j, dqs):
            rows = pl.ds(pl.multiple_of(j * tq, tq), tq)
            out = []
            for hh in range(hp):
                p = p_s[hh, j]
                ds = p * (dp_s[hh, j] - dls[hh])
                dsb = ds.astype(BF16)
                dv_ref[hh, rows, :] += _dot_tn(p.astype(BF16), dobs[hh])
                dk_ref[hh, rows, :] += _dot_tn(dsb, q_ref[hh]) * scale
                dc_ref[hh, :, rows] -= jnp.sum(ds, axis=0, keepdims=True)
                out.append(dqs[hh] + _dot(dsb, k_ref[hh, rows, :]))
            return tuple(out)

        dqs = lax.fori_loop(0, i + 1, second, (jnp.zeros((tq, HEAD_DIM), F32),) * hp)
        for hh in range(hp):
            dq_ref[hh] = dqs[hh] * scale

    blk = pl.BlockSpec((hp, tq, HEAD_DIM), lambda h, i: (h, i, 0))
    full = pl.BlockSpec((hp, s, HEAD_DIM), lambda h, i: (h, 0, 0))
    crow_spec = pl.BlockSpec((hp, 1, s), lambda h, i: (h, 0, 0))
    return _pcall(body, name="attn_bwd", grid=(HEADS // hp, nq),
                  in_specs=[blk, full, full, crow_spec, pl.BlockSpec((hp, tq, 1), lambda h, i: (h, i, 0)), blk],
                  out_specs=[blk, full, full, crow_spec],
                  out_shape=[_sds((HEADS, s, HEAD_DIM))] * 3 + [_sds((HEADS, 1, s))],
                  scratch_shapes=[pltpu.VMEM((hp, nq, tq, tq), F32)] * 2,
                  dims=("parallel", "arbitrary"))(qh, kh, vh, crow, lse, doh)


def _prep_bwd(z, dqn, dkn, dv, du, dc, gq, gk, bf, gg):
    s = z.shape[0]
    tm = _row_tile(s)
    nb = s // tm

    def body(z_ref, dqn_ref, dkn_ref, dv_ref, du_ref, dc_ref, gq_ref, gk_ref, bf_ref, gg_ref,
             dz_ref, dgq_ref, dgk_ref, dbf_ref, carry_ref):
        i = pl.program_id(0)

        @pl.when(i == 0)
        def _():
            for r in (dgq_ref, dgk_ref, dbf_ref, carry_ref):
                r[...] = jnp.zeros_like(r)

        gg_m = gg_ref[...]

        def head_norm_bwd(t, g, dn):
            r = lax.rsqrt(_dot_exact_r(t * t, gg_m) * (1.0 / HEAD_DIM) + EPS)
            w = dn * g
            mean_wt = _dot_exact_r(w * t, gg_m) * (1.0 / HEAD_DIM)
            return r * w - t * (r * r * r) * mean_wt, jnp.sum(dn * t * r, axis=0, keepdims=True)

        dq, dgq = head_norm_bwd(z_ref[:, 0:ATTN_W], gq_ref[...], dqn_ref[...])
        dk, dgk = head_norm_bwd(z_ref[:, ATTN_W:2 * ATTN_W], gk_ref[...], dkn_ref[...])
        dgq_ref[...] += dgq
        dgk_ref[...] += dgk
        row = lax.broadcasted_iota(jnp.int32, (tm, tm), 0)
        col = lax.broadcasted_iota(jnp.int32, (tm, tm), 1)
        triu = (col >= row).astype(BF16)
        dlf = _dot_exact_l(triu, dc_ref[...]) + carry_ref[...]
        carry_ref[...] = dlf[0:1, :]
        fl = z_ref[:, F_COL0:F_COL0 + LANES] + bf_ref[...]
        df = dlf * _sigmoid(-fl)
        dbf_ref[...] += jnp.sum(df, axis=0, keepdims=True)
        dz_ref[:, 0:ATTN_W] = dq.astype(BF16)
        dz_ref[:, ATTN_W:2 * ATTN_W] = dk.astype(BF16)
        dz_ref[:, 2 * ATTN_W:3 * ATTN_W] = dv_ref[...].astype(BF16)
        dz_ref[:, U_COL0:U_COL0 + SSM_W] = du_ref[...].astype(BF16)
        dz_ref[:, F_COL0:F_COL0 + LANES] = df.astype(BF16)

    row_spec = lambda w: pl.BlockSpec((tm, w), lambda i: (nb - 1 - i, 0))
    const = lambda shape: pl.BlockSpec(shape, lambda i: (0, 0))
    return _pcall(body, name="prep_bwd", grid=(nb,),
                  in_specs=[row_spec(Z_COLS)] + [row_spec(ATTN_W)] * 4 + [row_spec(LANES), const((1, ATTN_W)),
                            const((1, ATTN_W)), const((1, LANES)), const((ATTN_W, ATTN_W))],
                  out_specs=[row_spec(Z_COLS), const((1, ATTN_W)), const((1, ATTN_W)), const((1, LANES))],
                  out_shape=[_sds((s, Z_COLS), BF16), _sds((1, ATTN_W)), _sds((1, ATTN_W)), _sds((1, LANES))],
                  scratch_shapes=[pltpu.VMEM((1, LANES), F32)], dims=("arbitrary",))(z, dqn, dkn, dv, du, dc, gq, gk, bf, gg)


def _in_norm_bwd(x, g_mix, dh, dx1):
    s = x.shape[0]
    tm = _row_tile(s)

    def body(x_ref, g_ref, dh_ref, dx1_ref, dx_ref, dg_ref):
        i = pl.program_id(0)

        @pl.when(i == 0)
        def _():
            dg_ref[...] = jnp.zeros_like(dg_ref)

        dxn, dg = _rms_bwd(x_ref[...], g_ref[...], dh_ref[...])
        dx_ref[...] = dx1_ref[...] + dxn
        dg_ref[...] += dg

    row = pl.BlockSpec((tm, D_MODEL), lambda i: (i, 0))
    vec = pl.BlockSpec((1, D_MODEL), lambda i: (0, 0))
    return _pcall(body, name="in_norm_bwd", grid=(s // tm,), in_specs=[row, vec, row, row], out_specs=[row, vec],
                  out_shape=[_sds((s, D_MODEL)), _sds((1, D_MODEL))], dims=("arbitrary",))(x, g_mix, dh, dx1)


def _adamw_refs(w_ref, g_ref, m_ref, v_ref, d_ref, mo_ref, vo_ref):
    gv = g_ref[...]
    mn = ADAM_B1 * m_ref[...] + (1.0 - ADAM_B1) * gv
    vn = ADAM_B2 * v_ref[...] + (1.0 - ADAM_B2) * (gv * gv)
    m_hat = mn / (1.0 - ADAM_B1 ** ADAM_STEP)
    v_hat = vn / (1.0 - ADAM_B2 ** ADAM_STEP)
    d_ref[...] = -ADAM_LR * (m_hat / (jnp.sqrt(v_hat) + ADAM_EPS) + ADAM_WD * w_ref[...])
    mo_ref[...] = mn
    vo_ref[...] = vn


def _adamw_small(ws, gs, ms, vs):
    n = len(ws)

    def body(*refs):
        ins, outs = refs[:4 * n], refs[4 * n:]
        for i in range(n):
            _adamw_refs(ins[i], ins[n + i], ins[2 * n + i], ins[3 * n + i], *outs[3 * i:3 * i + 3])

    vm = pl.BlockSpec(memory_space=pltpu.VMEM)
    out_shape = [_sds(w.shape) for w in ws for _ in range(3)]
    return _pallas(body, name="adamw_small", in_specs=[vm] * (4 * n), out_specs=[vm] * (3 * n), out_shape=out_shape,
                   compiler_params=pltpu.CompilerParams(vmem_limit_bytes=VMEM_LIMIT))(*ws, *gs, *ms, *vs)


def _adamw(w, g, m, v, *, name):
    r, c = w.shape
    tr = r
    for cand in (256, 176, 128, 64):
        if r > cand and r % cand == 0:
            tr = cand
            break

    def body(w_ref, g_ref, m_ref, v_ref, d_ref, mo_ref, vo_ref):
        _adamw_refs(w_ref, g_ref, m_ref, v_ref, d_ref, mo_ref, vo_ref)

    spec = pl.BlockSpec((tr, c), lambda i: (i, 0))
    return _pcall(body, name=name, grid=(r // tr,), in_specs=[spec] * 4, out_specs=[spec] * 3,
                  out_shape=[_sds((r, c))] * 3, dims=("parallel",))(w, g, m, v)


def _prefetch_call(body, *, name, grid, in_specs, out_specs, out_shape, operands):
    grid_spec = pltpu.PrefetchScalarGridSpec(num_scalar_prefetch=1, grid=grid, in_specs=in_specs, out_specs=out_specs)
    params = pltpu.CompilerParams(dimension_semantics=("parallel",) * len(grid), vmem_limit_bytes=VMEM_LIMIT)
    return _pallas(body, name=name, grid_spec=grid_spec, out_shape=out_shape, compiler_params=params)(*operands)


def _half_rows_tile(hr):
    return hr if hr <= 256 else 176 if hr % 176 == 0 else 256


def _add_half(g, landed, place, *, name):
    def body(place_ref, g_ref, l_ref, o_ref):
        own = g_ref[0] if len(g_ref.shape) == 4 else g_ref[...]
        o_ref[...] = (own + l_ref[...]).astype(BF16)

    if g.ndim == 4:
        _, _, hr, c = g.shape
        tr = _half_rows_tile(hr)
        blk = (1, tr, c)
        return _prefetch_call(
            body, name=name, grid=(N_CHIPS, hr // tr),
            in_specs=[pl.BlockSpec((1,) + blk, lambda j, i, p: (j, p[1], i, 0)), pl.BlockSpec(blk, lambda j, i, p: (j, i, 0))],
            out_specs=pl.BlockSpec(blk, lambda j, i, p: (j, i, 0)), out_shape=_sds(landed.shape, BF16),
            operands=(place, g, landed))
    hr, c = landed.shape
    tr, tc = 256, _tile(c, 2176)
    nb = hr // tr
    return _prefetch_call(
        body, name=name, grid=(nb, c // tc),
        in_specs=[pl.BlockSpec((tr, tc), lambda i, j, p: (p[1] * nb + i, j)), pl.BlockSpec((tr, tc), lambda i, j, p: (i, j))],
        out_specs=pl.BlockSpec((tr, tc), lambda i, j, p: (i, j)), out_shape=_sds(landed.shape, BF16),
        operands=(place, g, landed))


def _sum_chips(chip_sum, lands, place, *, name, tc, window_stride=0):
    _, hr, c = lands.shape
    tr = _half_rows_tile(hr)
    nb = hr // tr
    ncb = c // tc

    def body(place_ref, own_ref, a_ref, b_ref, c_ref, o_ref):
        own = own_ref[0] if len(own_ref.shape) == 3 else own_ref[...]
        o_ref[...] = ((own.astype(F32) + a_ref[0].astype(F32)) + b_ref[0].astype(F32)) + c_ref[0].astype(F32)

    land = lambda k: pl.BlockSpec((1, tr, tc), lambda i, j, p: ((p[0] + k) % N_CHIPS, i, j))
    if chip_sum.ndim == 3:
        own_spec = land(0)
    else:
        stride = window_stride // tc
        own_spec = pl.BlockSpec((tr, tc), lambda i, j, p: (i, p[0] * stride + j))
    return _prefetch_call(
        body, name=name, grid=(nb, ncb), in_specs=[own_spec, land(1), land(2), land(3)],
        out_specs=pl.BlockSpec((tr, tc), lambda i, j, p: (p[1] * nb + i, j)), out_shape=_sds((2 * hr, c)),
        operands=(place, chip_sum, lands, lands, lands))


_HBM = pl.BlockSpec(memory_space=pltpu.HBM)


def _place():
    x, y, c = lax.axis_index("x"), lax.axis_index("y"), lax.axis_index("c")
    chips = [(1 - x, y), (x, 1 - y), (1 - x, 1 - y)]
    return x, y, c, chips


def _rcopy(src, dst, send_sem, recv_sem, to):
    return pltpu.make_async_remote_copy(src_ref=src, dst_ref=dst, send_sem=send_sem, recv_sem=recv_sem,
                                        device_id=to, device_id_type=MESH)


N_BIG = 5
UP_COLS = 2 * D_FF // N_CHIPS
IN_WINDOW = 640
IN_STRIDE = 512


def _gather_weights(shards, conv_w):
    def body(*refs):
        srcs, cw_ref = refs[0:N_BIG], refs[N_BIG]
        outs, cw_out = refs[N_BIG + 1:2 * N_BIG + 1], refs[2 * N_BIG + 1]
        send_sems, recv_sems = refs[2 * N_BIG + 2:]
        x, y, c, chips = _place()
        sibling = (x, y, 1 - c)
        me = 2 * x + y

        def half(w, chip, hc):
            hr, cols = srcs[w].shape[0] // 2, srcs[w].shape[1]
            if len(outs[w].shape) == 2:
                return outs[w].at[pl.ds(hc * hr, hr), pl.ds(pl.multiple_of(chip * cols, LANES), cols)]
            return outs[w].at[chip, pl.ds(hc * hr, hr), :]

        first, passed = [], []
        for w in range(N_BIG):
            hr = srcs[w].shape[0] // 2
            for k, (cx, cy) in enumerate(chips):
                first.append(_rcopy(srcs[w].at[pl.ds(c * hr, hr), :], half(w, me, c), send_sems.at[3 * w + k],
                                    recv_sems.at[3 * w + k], (cx, cy, c)))
        n_first = len(first)
        for k, (cx, cy) in enumerate(chips):
            first.append(_rcopy(cw_ref, cw_out.at[me], send_sems.at[2 * n_first + k], recv_sems.at[2 * n_first + k],
                                (cx, cy, c)))
        for cp in first:
            cp.start()
        for w in range(N_BIG):
            for k, (cx, cy) in enumerate(chips):
                landed = half(w, 2 * cx + cy, c)
                _rcopy(landed, landed, send_sems.at[3 * w + k], recv_sems.at[3 * w + k], sibling).wait_recv()
                fwd = _rcopy(landed, landed, send_sems.at[n_first + 3 * w + k], recv_sems.at[n_first + 3 * w + k], sibling)
                fwd.start()
                passed.append(fwd)
        for w in range(N_BIG):
            for k, (cx, cy) in enumerate(chips):
                other = half(w, 2 * cx + cy, 1 - c)
                _rcopy(other, other, send_sems.at[n_first + 3 * w + k], recv_sems.at[n_first + 3 * w + k], sibling).wait_recv()
        for k, (cx, cy) in enumerate(chips):
            slot = cw_out.at[2 * cx + cy]
            _rcopy(slot, slot, send_sems.at[2 * n_first + k], recv_sems.at[2 * n_first + k], sibling).wait_recv()
        for cp in first + passed:
            cp.wait_send()

    out_shape = [_sds((shards[3].shape[0], N_CHIPS * UP_COLS), BF16) if w == 3 else _sds((N_CHIPS,) + shards[w].shape, BF16)
                 for w in range(N_BIG)] + [_sds((N_CHIPS,) + conv_w.shape)]
    n_sems = 2 * 3 * N_BIG + 3
    return _pallas(body, name="gather_weights", in_specs=[_HBM] * (N_BIG + 1), out_specs=[_HBM] * (N_BIG + 1),
                   out_shape=out_shape,
                   scratch_shapes=[pltpu.SemaphoreType.DMA((n_sems,)), pltpu.SemaphoreType.DMA((n_sems,))])(*shards, conv_w)


def _swap_halves(grads):
    def body(*refs):
        g_refs, land_refs = refs[0:N_BIG], refs[N_BIG:2 * N_BIG]
        send_sems, recv_sems = refs[2 * N_BIG:]
        x, y, c, _ = _place()
        cps = []
        for w in range(N_BIG):
            if len(g_refs[w].shape) == 4:
                theirs = g_refs[w].at[:, 1 - c]
            else:
                hr = g_refs[w].shape[0] // 2
                theirs = g_refs[w].at[pl.ds((1 - c) * hr, hr), :]
            cps.append(_rcopy(theirs, land_refs[w], send_sems.at[w], recv_sems.at[w], (x, y, 1 - c)))
        for cp in cps:
            cp.start()
        for cp in cps:
            cp.wait()

    out_shape = [_sds((g.shape[0], g.shape[2], g.shape[3])) if g.ndim == 4 else _sds((g.shape[0] // 2, g.shape[1]))
                 for g in grads]
    return _pallas(body, name="swap_halves", in_specs=[_HBM] * N_BIG, out_specs=[_HBM] * N_BIG, out_shape=out_shape,
                   scratch_shapes=[pltpu.SemaphoreType.DMA((N_BIG,)), pltpu.SemaphoreType.DMA((N_BIG,))])(*grads)


def _scatter_chips(chip_sums):
    windows = {0: (IN_STRIDE, IN_WINDOW), 3: (UP_COLS, UP_COLS)}

    def body(*refs):
        s_refs, land_refs = refs[0:N_BIG], refs[N_BIG:2 * N_BIG]
        send_sems, recv_sems = refs[2 * N_BIG:]
        x, y, c, chips = _place()
        me = 2 * x + y
        sends = []
        for w in range(N_BIG):
            for k, (cx, cy) in enumerate(chips):
                tgt = 2 * cx + cy
                if w in windows:
                    stride, width = windows[w]
                    part = s_refs[w].at[:, pl.ds(pl.multiple_of(tgt * stride, LANES), width)]
                else:
                    part = s_refs[w].at[tgt]
                sends.append(_rcopy(part, land_refs[w].at[me], send_sems.at[3 * w + k], recv_sems.at[3 * w + k], (cx, cy, c)))
        for cp in sends:
            cp.start()
        for w in range(N_BIG):
            for k, (cx, cy) in enumerate(chips):
                slot = land_refs[w].at[2 * cx + cy]
                _rcopy(slot, slot, send_sems.at[3 * w + k], recv_sems.at[3 * w + k], (cx, cy, c)).wait_recv()
        for cp in sends:
            cp.wait_send()

    out_shape = [_sds((N_CHIPS, s.shape[0], windows[w][1]), BF16) if w in windows else _sds(s.shape, BF16)
                 for w, s in enumerate(chip_sums)]
    return _pallas(body, name="scatter_chips", in_specs=[_HBM] * N_BIG, out_specs=[_HBM] * N_BIG, out_shape=out_shape,
                   scratch_shapes=[pltpu.SemaphoreType.DMA((3 * N_BIG,)), pltpu.SemaphoreType.DMA((3 * N_BIG,))])(*chip_sums)


def _join_halves(reds):
    def body(*refs):
        out_refs = refs[N_BIG:2 * N_BIG]
        send_sems, recv_sems = refs[2 * N_BIG:]
        x, y, c, _ = _place()
        cps = []
        for w in range(N_BIG):
            hr = out_refs[w].shape[0] // 2
            mine = out_refs[w].at[pl.ds(c * hr, hr), :]
            cps.append(_rcopy(mine, mine, send_sems.at[w], recv_sems.at[w], (x, y, 1 - c)))
        for cp in cps:
            cp.start()
        for w in range(N_BIG):
            hr = out_refs[w].shape[0] // 2
            theirs = out_refs[w].at[pl.ds((1 - c) * hr, hr), :]
            _rcopy(theirs, theirs, send_sems.at[w], recv_sems.at[w], (x, y, 1 - c)).wait_recv()
        for cp in cps:
            cp.wait_send()

    return _pallas(body, name="join_halves", in_specs=[_HBM] * N_BIG, out_specs=[_HBM] * N_BIG,
                   out_shape=[_sds(r.shape) for r in reds], input_output_aliases={w: w for w in range(N_BIG)},
                   scratch_shapes=[pltpu.SemaphoreType.DMA((N_BIG,)), pltpu.SemaphoreType.DMA((N_BIG,))])(*reds)


def _allreduce_small(v):
    m_per = v.shape[0]

    def body(v_ref, out_ref, all_ref, send_sems, recv_sems, local_sem):
        x, y, c, chips = _place()
        me, sibling = (x, y, c), (x, y, 1 - c)

        def rows(px, py, pc):
            return all_ref.at[pl.ds((4 * px + 2 * py + pc) * m_per, m_per), :]

        def copy(k, block, to, src=None):
            return _rcopy(rows(*block) if src is None else src, rows(*block), send_sems.at[k], recv_sems.at[k], to)

        mine = pltpu.make_async_copy(v_ref, rows(*me), local_sem)
        mine.start()
        first = [copy(0, me, sibling, src=v_ref)]
        first += [copy(1 + k, me, (*chip, c), src=v_ref) for k, chip in enumerate(chips)]
        for cp in first:
            cp.start()
        passed = [copy(4 + k, (*chip, c), sibling) for k, chip in enumerate(chips)]
        for k, chip in enumerate(chips):
            copy(1 + k, (*chip, c), me).wait_recv()
            passed[k].start()
        copy(0, sibling, me).wait_recv()
        for k, chip in enumerate(chips):
            copy(4 + k, (*chip, 1 - c), me).wait_recv()
        for cp in first + passed:
            cp.wait_send()
        mine.wait()
        acc = all_ref[pl.ds(0, m_per), :]
        for d in range(1, 8):
            acc = acc + all_ref[pl.ds(d * m_per, m_per), :]
        out_ref[...] = acc

    vm = pl.BlockSpec(memory_space=pltpu.VMEM)
    return _pallas(body, name="allreduce_small", in_specs=[vm], out_specs=vm, out_shape=_sds((m_per, LANES)),
                          scratch_shapes=[pltpu.VMEM((8 * m_per, LANES), F32), pltpu.SemaphoreType.DMA((7,)),
                                          pltpu.SemaphoreType.DMA((7,)), pltpu.SemaphoreType.DMA],
                          compiler_params=pltpu.CompilerParams(vmem_limit_bytes=VMEM_LIMIT))(v)


def _to_heads(t):
    s = t.shape[0]
    return t.reshape(s, HEADS, HEAD_DIM).transpose(1, 0, 2)


def _from_heads(t):
    s = t.shape[1]
    return t.transpose(1, 0, 2).reshape(s, HEADS * HEAD_DIM)


def _reorder_in_cols(w):
    pad = jnp.zeros((w.shape[0], Z_COLS - IN_COLS), w.dtype)
    return jnp.concatenate([w[:, :3 * ATTN_W], w[:, 3 * ATTN_W + HEADS:], w[:, 3 * ATTN_W:3 * ATTN_W + HEADS], pad], axis=1)


def _restore_in_cols(w):
    return jnp.concatenate([w[:, :3 * ATTN_W], w[:, F_COL0:F_COL0 + HEADS], w[:, U_COL0:U_COL0 + SSM_W]], axis=1)


def _block_diag(blocks):
    j, g, a, b = blocks.shape
    eye = jnp.eye(g, dtype=bool)[None, :, None, :, None]
    return jnp.where(eye, blocks[:, :, :, None, :], jnp.zeros((), blocks.dtype)).reshape(j, g * a, g * b)


def _diag_blocks(m, a, b):
    j = m.shape[0]
    g = m.shape[1] // a
    t = m.reshape(j, g, a, g, b)
    eye = jnp.eye(g, dtype=bool)[None, :, None, :, None]
    return jnp.sum(jnp.where(eye, t, 0.0), axis=3)


def _pack_rows(parts, rows, dtype):
    used = sum(p.shape[0] for p in parts)
    return jnp.concatenate([p.astype(dtype) for p in parts] + [jnp.zeros((rows - used, D_MODEL), dtype)], axis=0)


_SMALL = (("g_mix", (1024,)), ("b_f", (8,)), ("g_q", (64,)), ("g_k", (64,)), ("lambda_re", (32, 64)),
          ("lambda_im", (32, 64)), ("log_step", (32,)), ("b_re", (32, 64, 16)), ("b_im", (32, 64, 16)),
          ("c_re", (32, 16, 64)), ("c_im", (32, 16, 64)), ("d_skip", (32, 16)), ("b_glu", (512,)),
          ("g_attn_out", (512,)), ("g_ssm_out", (512,)), ("g_ffn", (1024,)), ("conv_b", (5632,)))


def _small_rows(shape):
    return -(-math.prod(shape) // LANES)


def _pack_small(arrs, extra=()):
    parts = []
    for a in list(arrs) + list(extra):
        flat = a.reshape(-1)
        rows = -(-flat.shape[0] // LANES)
        parts.append(jnp.pad(flat, (0, rows * LANES - flat.shape[0])).reshape(rows, LANES))
    total = sum(p.shape[0] for p in parts)
    pad = -total % SUBLANES
    if pad:
        parts.append(jnp.zeros((pad, LANES), F32))
    return jnp.concatenate(parts, axis=0)


def _unpack_small(buf, shapes):
    out, r = [], 0
    for shape in shapes:
        n = math.prod(shape)
        rows = -(-n // LANES)
        out.append(buf[r:r + rows].reshape(-1)[:n].reshape(shape))
        r += rows
    return out


def _local_step(x, tgt, w_in_r, w_glu_b, w_out_b, w_up_b, w_down_b, conv_w_full, p):
    s = x.shape[0]
    row = lambda v: v.reshape(1, -1)
    g_mix, g_ffn = row(p["g_mix"]), row(p["g_ffn"])
    g_att, g_ssm, b_glu, conv_b = row(p["g_attn_out"]), row(p["g_ssm_out"]), row(p["b_glu"]), row(p["conv_b"])
    gq = row(jnp.tile(p["g_q"], HEADS))
    gk = row(jnp.tile(p["g_k"], HEADS))
    bf = row(jnp.pad(p["b_f"], (0, LANES - HEADS)))
    gg = jnp.kron(jnp.eye(HEADS, dtype=F32), jnp.ones((HEAD_DIM, HEAD_DIM), F32)).astype(BF16)
    dsk = row(p["d_skip"])

    rep = lambda a: jnp.repeat(a, SSM_GROUP, axis=0)
    lr, li = rep(p["lambda_re"]), rep(p["lambda_im"])
    ls = rep(jnp.broadcast_to(p["log_step"][:, None], (SSM_GROUPS, SSM_STATE)))
    bt_re = p["b_re"].transpose(0, 2, 1).reshape(_PARAM_SHAPE)
    bt_im = p["b_im"].transpose(0, 2, 1).reshape(_PARAM_SHAPE)
    a_re_rep, a_im_rep, bb_re, bb_im = _ssm_params(lr, li, ls, bt_re, bt_im)
    ar = a_re_rep[::SSM_GROUP].reshape(SSM_CHUNKS, 1, CHUNK_S)
    ai = a_im_rep[::SSM_GROUP].reshape(SSM_CHUNKS, 1, CHUNK_S)
    chunked = lambda t: t.reshape(SSM_CHUNKS, SSM_GROUPS // SSM_CHUNKS, SSM_GROUP, SSM_STATE)
    bbr = _block_diag(chunked(bb_re)).astype(BF16)
    bbi = _block_diag(chunked(bb_im)).astype(BF16)
    to_cc = lambda c: _block_diag(chunked(c).transpose(0, 1, 3, 2)).astype(BF16)
    ccr, cci = to_cc(p["c_re"]), to_cc(p["c_im"])

    hb, z = _in_proj(x, g_mix, w_in_r)
    qn, kn, vb, ub, c128 = _attn_prep(z, gq, gk, bf, gg)
    qh, kh, vh = _to_heads(qn), _to_heads(kn), _to_heads(vb)
    crow = c128[:, :HEADS].T.reshape(HEADS, 1, s)
    oh, lse = _attn_fwd(qh, kh, vh, crow)
    att = _from_heads(oh)
    xr, xi, y = _ssm_fwd(ub, z, bbr, bbi, ar, ai, ccr, cci, dsk)
    x1, mixb, h2b = _mix_out(y, att, x, w_glu_b, b_glu, g_att, g_ssm, w_out_b, g_ffn)
    up = _mm(h2b, w_up_b, name="ffn_up", tm=1024, tn=1408, tk=1024)
    act = _conv_act(up, conv_w_full, conv_b)
    dy, dyb, loss_blk = _down_loss(act, w_down_b, x1, tgt)

    d_w_down = _mm(act, dyb, ta=True, name="d_w_down", tm=1408, tn=1024, tk=2048)
    dact = _mm(dyb, w_down_b, tb=True, name="d_act", tm=1024, tn=1408, tk=1024)
    dupb, dcw = _conv_act_bwd(up, dact, conv_w_full, conv_b)
    d_w_up = _mm(h2b, dupb, ta=True, name="d_w_up", tm=1024, tn=1408, tk=2048)
    dh2 = _mm(dupb, w_up_b, tb=True, name="d_h2", tm=1024, tn=1024, tk=1408)
    dx1, dx1b, datt, dys, d_w_glu, d_g_ffn, d_g_att, d_g_ssm, d_b_glu = _mix_bwd(
        dy, dh2, x1, g_ffn, w_out_b, y, att, w_glu_b, b_glu, g_att, g_ssm)
    d_w_out = _mm(mixb, dx1b, ta=True, name="d_w_out", tm=1024, tn=1024, tk=2048)
    du, dbbr, dbbi, dccr, dcci, dar, dai, dd = _ssm_bwd(dys, z, ub, xr, xi, bbr, bbi, ar, ai, ccr, cci, dsk)
    doh = _to_heads(datt)
    dqh, dkh, dvh, dcrow = _attn_bwd(qh, kh, vh, crow, lse, doh)
    dc128 = jnp.pad(dcrow.reshape(HEADS, s).T, ((0, 0), (0, LANES - HEADS)))
    dzb, d_gq, d_gk, d_bf = _prep_bwd(z, _from_heads(dqh), _from_heads(dkh), _from_heads(dvh), du, dc128, gq, gk, bf, gg)
    d_w_in_r = _mm(hb, dzb, ta=True, name="d_w_in", tm=512, tn=Z_COLS, tk=2048)
    dh = _mm(dzb, w_in_r, tb=True, name="d_h", tm=1024, tn=1024, tk=Z_COLS)
    dx, d_g_mix = _in_norm_bwd(x, g_mix, dh, dx1)

    unchunk = lambda t: t.reshape(_PARAM_SHAPE)
    dbb_re = unchunk(_diag_blocks(dbbr, SSM_GROUP, SSM_STATE))
    dbb_im = unchunk(_diag_blocks(dbbi, SSM_GROUP, SSM_STATE))
    first_row = (jnp.arange(_PARAM_SHAPE[0]) % SSM_GROUP == 0)[:, None]
    da_re = jnp.where(first_row, rep(dar.reshape(SSM_GROUPS, SSM_STATE)), 0.0)
    da_im = jnp.where(first_row, rep(dai.reshape(SSM_GROUPS, SSM_STATE)), 0.0)
    expand_t = (jnp.arange(SSM_GROUPS)[:, None] == (jnp.arange(_PARAM_SHAPE[0]) // SSM_GROUP)[None, :]).astype(BF16)
    d_lr, d_li, d_ls, d_bt_re, d_bt_im = _ssm_params_bwd(lr, li, ls, bt_re, bt_im, da_re, da_im, dbb_re, dbb_im, expand_t)
    from_bt = lambda t: t.reshape(SSM_GROUPS, SSM_GROUP, SSM_STATE).transpose(0, 2, 1)
    from_cc = lambda t: _diag_blocks(t, SSM_STATE, SSM_GROUP).transpose(0, 1, 3, 2).reshape(SSM_GROUPS, SSM_GROUP, SSM_STATE)

    small = {
        "g_mix": d_g_mix, "b_f": d_bf[0, :HEADS], "g_q": d_gq.reshape(HEADS, HEAD_DIM).sum(0),
        "g_k": d_gk.reshape(HEADS, HEAD_DIM).sum(0), "lambda_re": d_lr, "lambda_im": d_li, "log_step": d_ls,
        "b_re": from_bt(d_bt_re), "b_im": from_bt(d_bt_im), "c_re": from_cc(dccr), "c_im": from_cc(dcci),
        "d_skip": dd, "b_glu": d_b_glu, "g_attn_out": d_g_att, "g_ssm_out": d_g_ssm, "g_ffn": d_g_ffn,
        "conv_b": dcw[3],
    }
    big = {"w_in": _restore_in_cols(d_w_in_r), "w_glu": d_w_glu, "w_out": d_w_out, "w_up": d_w_up, "w_down": d_w_down}
    return loss_blk[0, 0], dx, big, small, dcw[0:3]


def kernel(x, g_mix, w_in, b_f, g_q, g_k, lambda_re, lambda_im, log_step, b_re, b_im, c_re, c_im, d_skip, w_glu, b_glu, g_attn_out, g_ssm_out, w_out, g_ffn, w_up, conv_w, conv_b, w_down, loss_target, m_g_mix, m_w_in, m_b_f, m_g_q, m_g_k, m_lambda_re, m_lambda_im, m_log_step, m_b_re, m_b_im, m_c_re, m_c_im, m_d_skip, m_w_glu, m_b_glu, m_g_attn_out, m_g_ssm_out, m_w_out, m_g_ffn, m_w_up, m_conv_w, m_conv_b, m_w_down, v_g_mix, v_w_in, v_b_f, v_g_q, v_g_k, v_lambda_re, v_lambda_im, v_log_step, v_b_re, v_b_im, v_c_re, v_c_im, v_d_skip, v_w_glu, v_b_glu, v_g_attn_out, v_g_ssm_out, v_w_out, v_g_ffn, v_w_up, v_conv_w, v_conv_b, v_w_down):
    args = dict(locals())
    order = ["g_mix", "w_in", "b_f", "g_q", "g_k", "lambda_re", "lambda_im", "log_step", "b_re", "b_im", "c_re", "c_im",
             "d_skip", "w_glu", "b_glu", "g_attn_out", "g_ssm_out", "w_out", "g_ffn", "w_up", "conv_w", "conv_b", "w_down"]
    cx, cy, cc = lax.axis_index("x"), lax.axis_index("y"), lax.axis_index("c")
    chip = 2 * cx + cy

    big_names = ("w_in", "w_glu", "w_out", "w_up", "w_down")
    shards = [args[n].astype(BF16) for n in big_names]
    g_in, g_glu, g_out, g_up, g_down, g_cw = _gather_weights(shards, conv_w)
    own = lambda stacked, mine: lax.dynamic_update_slice(stacked, mine[None], (chip,) + (0,) * mine.ndim)
    g_in, g_glu, g_out, g_down = own(g_in, shards[0]), own(g_glu, shards[1]), own(g_out, shards[2]), own(g_down, shards[4])
    w_up_b = lax.dynamic_update_slice(g_up, shards[3], (0, chip * UP_COLS))
    w_in_r = _reorder_in_cols(g_in.transpose(1, 0, 2).reshape(D_MODEL, IN_COLS))
    w_glu_b = g_glu.reshape(SSM_W, SSM_W)
    w_out_b = g_out.reshape(D_MODEL, D_MODEL)
    w_down_b = g_down.reshape(D_FF, D_MODEL)
    conv_w_full = own(g_cw, conv_w).transpose(1, 0, 2).reshape(3, 2 * D_FF)

    loss_part, dx, big, small, d_conv_w = _local_step(x[0], loss_target[0], w_in_r, w_glu_b, w_out_b, w_up_b, w_down_b,
                                                      conv_w_full, args)
    loss = lax.psum(loss_part, ("x", "y", "c"))

    place = jnp.stack([chip, cc]).astype(jnp.int32)
    halves = lambda t: t.reshape(N_CHIPS, 2, t.shape[0] // (2 * N_CHIPS), t.shape[1])
    d_in = jnp.pad(big["w_in"], ((0, 0), (0, Z_COLS - IN_COLS)))
    grads = [d_in, halves(big["w_glu"]), halves(big["w_out"]), big["w_up"], halves(big["w_down"])]
    landed = _swap_halves(grads)
    sums = [_add_half(g, l, place, name="add_" + n) for g, l, n in zip(grads, landed, big_names)]
    lands = _scatter_chips(sums)
    col_blocks = (LANES, SSM_W, D_MODEL, UP_COLS, D_MODEL)
    strides = (IN_STRIDE, 0, 0, UP_COLS, 0)
    reds = _join_halves([_sum_chips(s, l, place, name="sum_" + n, tc=tc, window_stride=st)
                         for s, l, n, tc, st in zip(sums, lands, big_names, col_blocks, strides)])
    g_big = dict(zip(big_names, reds))
    g_big["w_in"] = lax.dynamic_slice_in_dim(reds[0], 2 * chip, IN_COLS // N_CHIPS, axis=1)

    small_names = [n for n, _ in _SMALL]
    small_shapes = [sh for _, sh in _SMALL]
    gsum = _allreduce_small(_pack_small([small[n] for n in small_names], extra=[d_conv_w]))
    g_small = _unpack_small(gsum, small_shapes + [(3, 2 * D_FF)])
    g_conv_w = lax.dynamic_slice_in_dim(g_small[-1], chip * (2 * D_FF // N_CHIPS), 2 * D_FF // N_CHIPS, axis=1)
    g_small = dict(zip(small_names, g_small[:-1]))

    grad, delta, new_m, new_v = {}, {}, {}, {}
    for n in ("w_in", "w_glu", "w_out", "w_up", "w_down"):
        grad[n] = g_big[n]
        delta[n], new_m[n], new_v[n] = _adamw(args[n], g_big[n], args["m_" + n], args["v_" + n], name="adamw_" + n)
    grad["conv_w"] = g_conv_w
    delta["conv_w"], new_m["conv_w"], new_v["conv_w"] = _adamw(conv_w, g_conv_w, m_conv_w, v_conv_w, name="adamw_conv_w")
    stepped = _adamw_small([args[n] for n in small_names], [g_small[n] for n in small_names],
                           [args["m_" + n] for n in small_names], [args["v_" + n] for n in small_names])
    for i, n in enumerate(small_names):
        grad[n] = g_small[n]
        delta[n], new_m[n], new_v[n] = stepped[3 * i:3 * i + 3]

    return (loss, dx[None], *[grad[n] for n in order], *[delta[n] for n in order], *[new_m[n] for n in order],
            *[new_v[n] for n in order])
```

```python
import math

import jax
import jax.numpy as jnp
from jax import lax
from jax.experimental import pallas as pl
from jax.experimental.pallas import tpu as pltpu

F32 = jnp.float32
BF16 = jnp.bfloat16

D_MODEL = 1024
HEADS = 8
HEAD_DIM = 64
ATTN_W = 512
SSM_W = 512
SSM_GROUPS = 32
SSM_GROUP = 16
SSM_STATE = 64
N_STATE = SSM_GROUPS * SSM_STATE
D_FF = 2816
IN_COLS = 2056
Z_COLS = 2176
U_COL0 = 1536
F_COL0 = 2048
EPS = 1e-6
NEG_INF = -1e30
N_CHIPS = 4
LANES = 128
SUBLANES = 8
SSM_CHUNKS = 4
CHUNK_U = SSM_W // SSM_CHUNKS
CHUNK_S = N_STATE // SSM_CHUNKS
HEADS_PER_STEP = 2
STRIP = 128
N_STRIPS = D_FF // STRIP

ROWS_IN, ROWS_GLU, ROWS_OUT, ROWS_UP, ROWS_DOWN = 514, 64, 256, 1408, 704
OFF_GLU = ROWS_IN
OFF_OUT = OFF_GLU + ROWS_GLU
OFF_UP = OFF_OUT + ROWS_OUT
OFF_DOWN = OFF_UP + ROWS_UP
OFF_SPARE = OFF_DOWN + ROWS_DOWN
PACK_ROWS = 2976
HALF_ROWS = PACK_ROWS // 2
CONVW_ROWS = 9

ADAM_LR = 0.001
ADAM_B1 = 0.9
ADAM_B2 = 0.999
ADAM_EPS = 1e-08
ADAM_WD = 0.01
ADAM_STEP = 10

VMEM_LIMIT = 56 * 1024 * 1024
MESH = pl.DeviceIdType.MESH


def _pallas(body, **kw):
    return pl.pallas_call(body, **kw)


def _pcall(body, *, name, out_shape, in_specs, out_specs, grid=(), scratch_shapes=(), dims=None):
    params = pltpu.CompilerParams(dimension_semantics=dims, vmem_limit_bytes=VMEM_LIMIT)
    return _pallas(body, name=name, grid=grid, in_specs=in_specs, out_specs=out_specs,
                   out_shape=out_shape, scratch_shapes=scratch_shapes, compiler_params=params)


def _sds(shape, dtype=F32):
    return jax.ShapeDtypeStruct(shape, dtype)


def _dot(a, b):
    return jnp.dot(a, b, preferred_element_type=F32)


def _dot_nt(a, b):
    return lax.dot_general(a, b, (((1,), (1,)), ((), ())), preferred_element_type=F32)


def _dot_tn(a, b):
    return lax.dot_general(a, b, (((0,), (0,)), ((), ())), preferred_element_type=F32)


def _split3(x):
    hi = x.astype(BF16)
    r = x - hi.astype(F32)
    mid = r.astype(BF16)
    lo = (r - mid.astype(F32)).astype(BF16)
    return hi, mid, lo


def _dot_exact_r(x, m01):
    hi, mid, lo = _split3(x)
    return _dot(hi, m01) + _dot(mid, m01) + _dot(lo, m01)


def _dot_exact_l(m01, x):
    hi, mid, lo = _split3(x)
    return _dot(m01, hi) + _dot(m01, mid) + _dot(m01, lo)


def _sigmoid(x):
    return 1.0 / (1.0 + jnp.exp(-x))


def _rms(x, g):
    r = lax.rsqrt(jnp.mean(x * x, axis=-1, keepdims=True) + EPS)
    return x * r * g


def _rms_bwd(x, g, dy):
    r = lax.rsqrt(jnp.mean(x * x, axis=-1, keepdims=True) + EPS)
    w = dy * g
    dx = r * w - x * (r * r * r) * jnp.mean(w * x, axis=-1, keepdims=True)
    dg = jnp.sum(dy * x * r, axis=0, keepdims=True)
    return dx, dg


_GELU_K = math.sqrt(2.0 / math.pi)
_GELU_C = 0.044715


def _gelu(y):
    return y * (0.5 * (1.0 + jnp.tanh(_GELU_K * (y + _GELU_C * (y * y * y)))))


def _gelu_grad(y):
    t = jnp.tanh(_GELU_K * (y + _GELU_C * (y * y * y)))
    return 0.5 * (1.0 + t) + 0.5 * y * (1.0 - t * t) * (_GELU_K * (1.0 + 3.0 * _GELU_C * y * y))


def _tile(n, pref):
    if n <= pref:
        return n
    divs = [t for t in range(LANES, n + 1, LANES) if n % t == 0]
    below = [t for t in divs if t <= pref]
    if below and 2 * below[-1] >= pref:
        return below[-1]
    above = [t for t in divs if t > pref]
    return above[0] if above else n


def _row_tile(s):
    return min(256, s)


def _mm(a, b, *, name, tm, tn, tk, ta=False, tb=False, a_parts=1, b_parts=1):
    if a_parts > 1:
        m, kk = a.shape[1], a.shape[2] * a_parts
    elif ta:
        kk, m = a.shape
    else:
        m, kk = a.shape
    if b_parts > 1:
        n = b.shape[2] * b_parts
    else:
        n = b.shape[0] if tb else b.shape[1]
    tm, tn, tk = _tile(m, tm), _tile(n // b_parts, tn), _tile(kk // a_parts, tk)
    k_per, n_per = kk // a_parts // tk, n // b_parts // tn

    def body(a_ref, b_ref, o_ref):
        k = pl.program_id(2)
        if ta:
            part = _dot_tn(a_ref[...], b_ref[...])
        elif tb:
            part = _dot_nt(a_ref[...], b_ref[...])
        else:
            part = _dot(a_ref[...], b_ref[...])

        @pl.when(k == 0)
        def _():
            o_ref[...] = part

        @pl.when(k > 0)
        def _():
            o_ref[...] += part

    if a_parts > 1:
        a_spec = pl.BlockSpec((None, tm, tk), lambda i, j, k: (k // k_per, i, k % k_per))
    else:
        a_spec = pl.BlockSpec((tk, tm), lambda i, j, k: (k, i)) if ta else pl.BlockSpec((tm, tk), lambda i, j, k: (i, k))
    if b_parts > 1:
        b_spec = pl.BlockSpec((None, tk, tn), lambda i, j, k: (j // n_per, k, j % n_per))
    else:
        b_spec = pl.BlockSpec((tn, tk), lambda i, j, k: (j, k)) if tb else pl.BlockSpec((tk, tn), lambda i, j, k: (k, j))
    return _pcall(body, name=name, grid=(m // tm, n // tn, kk // tk), in_specs=[a_spec, b_spec],
                  out_specs=pl.BlockSpec((tm, tn), lambda i, j, k: (i, j)), out_shape=_sds((m, n)),
                  dims=("parallel", "parallel", "arbitrary"))(a, b)


def _in_proj(x, g_mix, w_in_r):
    s = x.shape[0]
    tm = _row_tile(s)

    def body(x_ref, g_ref, w_ref, h_ref, z_ref):
        h = _rms(x_ref[...], g_ref[...]).astype(BF16)
        h_ref[...] = h
        z_ref[...] = _dot(h, w_ref[...])

    return _pcall(body, name="in_proj", grid=(s // tm,),
                  in_specs=[pl.BlockSpec((tm, D_MODEL), lambda i: (i, 0)), pl.BlockSpec((1, D_MODEL), lambda i: (0, 0)),
                            pl.BlockSpec((D_MODEL, Z_COLS), lambda i: (0, 0))],
                  out_specs=[pl.BlockSpec((tm, D_MODEL), lambda i: (i, 0)), pl.BlockSpec((tm, Z_COLS), lambda i: (i, 0))],
                  out_shape=[_sds((s, D_MODEL), BF16), _sds((s, Z_COLS))], dims=("parallel",))(x, g_mix, w_in_r)


def _attn_prep(z, gq, gk, bf, gg):
    s = z.shape[0]
    tm = _row_tile(s)

    def body(z_ref, gq_ref, gk_ref, bf_ref, gg_ref, qn_ref, kn_ref, vb_ref, ub_ref, c_ref, carry_ref):
        i = pl.program_id(0)

        @pl.when(i == 0)
        def _():
            carry_ref[...] = jnp.zeros_like(carry_ref)

        gg_m = gg_ref[...]

        def head_norm(t, g):
            ssq = _dot_exact_r(t * t, gg_m)
            return t * lax.rsqrt(ssq * (1.0 / HEAD_DIM) + EPS) * g

        qn_ref[...] = head_norm(z_ref[:, 0:ATTN_W], gq_ref[...]).astype(BF16)
        kn_ref[...] = head_norm(z_ref[:, ATTN_W:2 * ATTN_W], gk_ref[...]).astype(BF16)
        vb_ref[...] = z_ref[:, 2 * ATTN_W:3 * ATTN_W].astype(BF16)
        ub_ref[...] = z_ref[:, U_COL0:U_COL0 + SSM_W].astype(BF16)
        fl = z_ref[:, F_COL0:F_COL0 + LANES] + bf_ref[...]
        lf = jnp.minimum(fl, 0.0) - jnp.log1p(jnp.exp(-jnp.abs(fl)))
        row = lax.broadcasted_iota(jnp.int32, (tm, tm), 0)
        col = lax.broadcasted_iota(jnp.int32, (tm, tm), 1)
        tri = (row >= col).astype(BF16)
        c = _dot_exact_l(tri, lf) + carry_ref[...]
        c_ref[...] = c
        carry_ref[...] = c[tm - 1:tm, :]

    row_spec = lambda w: pl.BlockSpec((tm, w), lambda i: (i, 0))
    const = lambda shape: pl.BlockSpec(shape, lambda i: (0, 0))
    return _pcall(body, name="attn_prep", grid=(s // tm,),
                  in_specs=[row_spec(Z_COLS), const((1, ATTN_W)), const((1, ATTN_W)), const((1, LANES)), const((ATTN_W, ATTN_W))],
                  out_specs=[row_spec(ATTN_W)] * 4 + [row_spec(LANES)],
                  out_shape=[_sds((s, ATTN_W), BF16)] * 4 + [_sds((s, LANES))],
                  scratch_shapes=[pltpu.VMEM((1, LANES), F32)], dims=("arbitrary",))(z, gq, gk, bf, gg)


def _attn_fwd(qh, kh, vh, crow):
    _, s, _ = qh.shape
    tq = _row_tile(s)
    scale = HEAD_DIM ** -0.5

    hp = HEADS_PER_STEP
    nq = s // tq
    fold = lambda t, op: op(t[:, :tq // 2], t[:, tq // 2:])

    def body(q_ref, k_ref, v_ref, c_ref, o_ref, lse_ref, s_s):
        i = pl.program_id(1)

        def first(j, ms, diagonal):
            off = pl.multiple_of(j * tq, tq)
            out = []
            for hh in range(hp):
                sc = _dot_nt(q_ref[hh], k_ref[hh, pl.ds(off, tq), :]) * scale - c_ref[hh, :, pl.ds(off, tq)]
                if diagonal:
                    causal = lax.broadcasted_iota(jnp.int32, (tq, tq), 1) <= lax.broadcasted_iota(jnp.int32, (tq, tq), 0)
                    sc = jnp.where(causal, sc, NEG_INF)
                s_s[hh, j] = sc
                out.append(jnp.maximum(ms[hh], fold(sc, jnp.maximum)))
            return tuple(out)

        ms = lax.fori_loop(0, i, lambda j, c: first(j, c, False), (jnp.full((tq, tq // 2), NEG_INF, F32),) * hp)
        ms = [jnp.max(t, axis=-1, keepdims=True) for t in first(i, ms, True)]

        def second(j, carry):
            rows = pl.ds(pl.multiple_of(j * tq, tq), tq)
            out = []
            for hh in range(hp):
                ls, acc = carry[hh]
                p = jnp.exp(s_s[hh, j] - ms[hh])
                out.append((ls + fold(p, jnp.add), acc + _dot(p.astype(BF16), v_ref[hh, rows, :])))
            return tuple(out)

        zero = (jnp.zeros((tq, tq // 2), F32), jnp.zeros((tq, HEAD_DIM), F32))
        for hh, (ls, acc) in enumerate(lax.fori_loop(0, i + 1, second, (zero,) * hp)):
            l = jnp.sum(ls, axis=-1, keepdims=True)
            o_ref[hh] = acc / l
            lse_ref[hh] = ms[hh] + jnp.log(l)

    blk = pl.BlockSpec((hp, tq, HEAD_DIM), lambda h, i: (h, i, 0))
    full = pl.BlockSpec((hp, s, HEAD_DIM), lambda h, i: (h, 0, 0))
    return _pcall(body, name="attn_fwd", grid=(HEADS // hp, nq),
                  in_specs=[blk, full, full, pl.BlockSpec((hp, 1, s), lambda h, i: (h, 0, 0))],
                  out_specs=[blk, pl.BlockSpec((hp, tq, 1), lambda h, i: (h, i, 0))],
                  out_shape=[_sds((HEADS, s, HEAD_DIM)), _sds((HEADS, s, 1))],
                  scratch_shapes=[pltpu.VMEM((hp, nq, tq, tq), F32)],
                  dims=("parallel", "parallel"))(qh, kh, vh, crow)


def _ssm_param_fn(lr, li, ls, br, bi):
    step = jnp.exp(ls)
    er = jnp.exp(lr * step)
    ab_re = er * jnp.cos(li * step)
    ab_im = er * jnp.sin(li * step)
    num_re = ab_re - 1.0
    num_im = ab_im
    den = lr * lr + li * li
    f_re = (num_re * lr + num_im * li) / den
    f_im = (num_im * lr - num_re * li) / den
    bb_re = f_re * br - f_im * bi
    bb_im = f_re * bi + f_im * br
    return ab_re, ab_im, bb_re, bb_im


_PARAM_SHAPE = (SSM_GROUPS * SSM_GROUP, SSM_STATE)


def _ssm_params(lr, li, ls, br, bi):
    def body(lr_ref, li_ref, ls_ref, br_ref, bi_ref, ar_ref, ai_ref, bbr_ref, bbi_ref):
        ar, ai, bbr, bbi = _ssm_param_fn(lr_ref[...], li_ref[...], ls_ref[...], br_ref[...], bi_ref[...])
        ar_ref[...] = ar
        ai_ref[...] = ai
        bbr_ref[...] = bbr
        bbi_ref[...] = bbi

    spec = pl.BlockSpec(_PARAM_SHAPE, lambda: (0, 0))
    return _pcall(body, name="ssm_params", in_specs=[spec] * 5, out_specs=[spec] * 4,
                  out_shape=[_sds(_PARAM_SHAPE)] * 4)(lr, li, ls, br, bi)


def _ssm_params_bwd(lr, li, ls, br, bi, dar, dai, dbbr, dbbi, expand_t):
    def body(lr_ref, li_ref, ls_ref, br_ref, bi_ref, dar_ref, dai_ref, dbbr_ref, dbbi_ref, et_ref,
             dlr_ref, dli_ref, dls_ref, dbr_ref, dbi_ref):
        _, vjp = jax.vjp(_ssm_param_fn, lr_ref[...], li_ref[...], ls_ref[...], br_ref[...], bi_ref[...])
        dlr, dli, dls, dbr, dbi = vjp((dar_ref[...], dai_ref[...], dbbr_ref[...], dbbi_ref[...]))
        et = et_ref[...]
        dlr_ref[...] = _dot_exact_l(et, dlr)
        dli_ref[...] = _dot_exact_l(et, dli)
        dls_ref[...] = jnp.sum(_dot_exact_l(et, dls), axis=-1, keepdims=True)
        dbr_ref[...] = dbr
        dbi_ref[...] = dbi

    spec = pl.BlockSpec(_PARAM_SHAPE, lambda: (0, 0))
    gspec = pl.BlockSpec((SSM_GROUPS, SSM_STATE), lambda: (0, 0))
    return _pcall(body, name="ssm_params_bwd",
                  in_specs=[spec] * 9 + [pl.BlockSpec((SSM_GROUPS, _PARAM_SHAPE[0]), lambda: (0, 0))],
                  out_specs=[gspec, gspec, pl.BlockSpec((SSM_GROUPS, 1), lambda: (0, 0)), spec, spec],
                  out_shape=[_sds((SSM_GROUPS, SSM_STATE))] * 2 + [_sds((SSM_GROUPS, 1))] + [_sds(_PARAM_SHAPE)] * 2,
                  )(lr, li, ls, br, bi, dar, dai, dbbr, dbbi, expand_t)


def _cmul(ar, ai, br, bi):
    return ar * br - ai * bi, ar * bi + ai * br


def _scan_consts(ar, ai, width, reverse):
    row = lax.broadcasted_iota(jnp.int32, (SUBLANES, width), 0)
    pw = [(ar, ai)]
    for _ in range(SUBLANES - 1):
        pw.append(_cmul(pw[-1][0], pw[-1][1], ar, ai))
    steps = []
    for d in (1, 2, 4):
        keep = (row < SUBLANES - d) if reverse else (row >= d)
        steps.append((d, jnp.where(keep, pw[d - 1][0], 0.0), jnp.where(keep, pw[d - 1][1], 0.0)))
    pr = jnp.zeros((SUBLANES, width), F32)
    pi = jnp.zeros((SUBLANES, width), F32)
    for r in range(SUBLANES):
        e = (SUBLANES - r) if reverse else (r + 1)
        pr = jnp.where(row == r, pw[e - 1][0], pr)
        pi = jnp.where(row == r, pw[e - 1][1], pi)
    return steps, pr, pi


def _scan_tile(xr, xi, cr, ci, consts, reverse):
    steps, pr, pi = consts
    for d, mr, mi in steps:
        sh = (SUBLANES - d) if reverse else d
        sr = pltpu.roll(xr, sh, 0)
        si = pltpu.roll(xi, sh, 0)
        xr, xi = xr + mr * sr - mi * si, xi + mr * si + mi * sr
    return xr + pr * cr - pi * ci, xi + pr * ci + pi * cr


def _ssm_fwd(ub, z, bbr, bbi, ar, ai, ccr, cci, dsk):
    s = ub.shape[0]
    tm = _row_tile(s)
    nt = tm // SUBLANES

    def body(ub_ref, u_ref, bbr_ref, bbi_ref, ar_ref, ai_ref, ccr_ref, cci_ref, dsk_ref,
             xr_ref, xi_ref, y_ref, cr_s, ci_s):
        i = pl.program_id(1)

        @pl.when(i == 0)
        def _():
            cr_s[...] = jnp.zeros_like(cr_s)
            ci_s[...] = jnp.zeros_like(ci_s)

        u_b = ub_ref[...]
        xr_ref[...] = _dot(u_b, bbr_ref[0])
        xi_ref[...] = _dot(u_b, bbi_ref[0])
        consts = _scan_consts(ar_ref[0], ai_ref[0], CHUNK_S, False)

        def tile(k, carry):
            cr, ci = carry
            sl = pl.ds(pl.multiple_of(k * SUBLANES, SUBLANES), SUBLANES)
            xr, xi = _scan_tile(xr_ref[sl, :], xi_ref[sl, :], cr, ci, consts, False)
            xr_ref[sl, :] = xr
            xi_ref[sl, :] = xi
            return xr[SUBLANES - 1:SUBLANES, :], xi[SUBLANES - 1:SUBLANES, :]

        cr, ci = lax.fori_loop(0, nt, tile, (cr_s[...], ci_s[...]))
        cr_s[...] = cr
        ci_s[...] = ci
        y_ref[...] = (_dot(xr_ref[...].astype(BF16), ccr_ref[0]) - _dot(xi_ref[...].astype(BF16), cci_ref[0])
                      + dsk_ref[...] * u_ref[...])

    ucol0 = U_COL0 // CHUNK_U
    wspec = lambda a, b: pl.BlockSpec((1, a, b), lambda j, i: (j, 0, 0))
    return _pcall(body, name="ssm_fwd", grid=(SSM_CHUNKS, s // tm),
                  in_specs=[pl.BlockSpec((tm, CHUNK_U), lambda j, i: (i, j)),
                            pl.BlockSpec((tm, CHUNK_U), lambda j, i: (i, ucol0 + j)),
                            wspec(CHUNK_U, CHUNK_S), wspec(CHUNK_U, CHUNK_S), wspec(1, CHUNK_S), wspec(1, CHUNK_S),
                            wspec(CHUNK_S, CHUNK_U), wspec(CHUNK_S, CHUNK_U),
                            pl.BlockSpec((1, CHUNK_U), lambda j, i: (0, j))],
                  out_specs=[pl.BlockSpec((tm, CHUNK_S), lambda j, i: (i, j)), pl.BlockSpec((tm, CHUNK_S), lambda j, i: (i, j)),
                             pl.BlockSpec((tm, CHUNK_U), lambda j, i: (i, j))],
                  out_shape=[_sds((s, N_STATE)), _sds((s, N_STATE)), _sds((s, SSM_W))],
                  scratch_shapes=[pltpu.VMEM((1, CHUNK_S), F32)] * 2,
                  dims=("parallel", "arbitrary"))(ub, z, bbr, bbi, ar, ai, ccr, cci, dsk)


def _ssm_glu(y, w_glu, b_glu):
    ge = _gelu(y)
    sg = _sigmoid(_dot(ge.astype(BF16), w_glu) + b_glu)
    return ge, sg


def _mix_out(y, att, x, w_glu, b_glu, g_att, g_ssm, w_out, g_ffn):
    s = x.shape[0]
    tm = _row_tile(s)

    def body(y_ref, att_ref, x_ref, wg_ref, bg_ref, ga_ref, gs_ref, wo_ref, gf_ref, x1_ref, mix_ref, h2_ref):
        ge, sg = _ssm_glu(y_ref[...], wg_ref[...], bg_ref[...])
        ms = _rms(ge * sg, gs_ref[...]).astype(BF16)
        ma = _rms(att_ref[...], ga_ref[...]).astype(BF16)
        mix_ref[:, 0:ATTN_W] = ma
        mix_ref[:, ATTN_W:D_MODEL] = ms
        x1 = x_ref[...] + (_dot(ma, wo_ref[0:ATTN_W, :]) + _dot(ms, wo_ref[ATTN_W:D_MODEL, :]))
        x1_ref[...] = x1
        h2_ref[...] = _rms(x1, gf_ref[...]).astype(BF16)

    row = lambda w: pl.BlockSpec((tm, w), lambda i: (i, 0))
    const = lambda a, b: pl.BlockSpec((a, b), lambda i: (0, 0))
    return _pcall(body, name="mix_out", grid=(s // tm,),
                  in_specs=[row(SSM_W), row(ATTN_W), row(D_MODEL), const(SSM_W, SSM_W), const(1, SSM_W), const(1, ATTN_W),
                            const(1, SSM_W), const(D_MODEL, D_MODEL), const(1, D_MODEL)],
                  out_specs=[row(D_MODEL)] * 3,
                  out_shape=[_sds((s, D_MODEL)), _sds((s, D_MODEL), BF16), _sds((s, D_MODEL), BF16)],
                  dims=("parallel",))(y, att, x, w_glu, b_glu, g_att, g_ssm, w_out, g_ffn)


CONV_CHUNK = 64


def _conv_rows(pad_ref, w, b, r0, n):
    y = b + pad_ref[pl.ds(r0 + SUBLANES - 2, n), :] * w[0:1, :]
    y = y + pad_ref[pl.ds(r0 + SUBLANES - 1, n), :] * w[1:2, :]
    return y + pad_ref[pl.ds(r0 + SUBLANES, n), :] * w[2:3, :]


def _fill_front_pad(pad_ref, strip_ref, s):
    pad_ref[0:SUBLANES, :] = jnp.zeros((SUBLANES, STRIP), F32)
    for r0 in range(0, s, CONV_CHUNK):
        pad_ref[pl.ds(SUBLANES + r0, CONV_CHUNK), :] = strip_ref[pl.ds(r0, CONV_CHUNK), :]


def _conv_act(up, conv_w, conv_b):
    s = up.shape[0]

    def body(ug_ref, uv_ref, wg_ref, wv_ref, bg_ref, bv_ref, act_ref, pg_ref, pv_ref):
        _fill_front_pad(pg_ref, ug_ref, s)
        _fill_front_pad(pv_ref, uv_ref, s)
        wg, wv, bg, bv = wg_ref[...], wv_ref[...], bg_ref[...], bv_ref[...]
        for r0 in range(0, s, CONV_CHUNK):
            hg = _conv_rows(pg_ref, wg, bg, r0, CONV_CHUNK)
            hv = _conv_rows(pv_ref, wv, bv, r0, CONV_CHUNK)
            act_ref[pl.ds(r0, CONV_CHUNK), :] = (hg * _sigmoid(hg) * hv).astype(BF16)

    strip = lambda off: pl.BlockSpec((s, STRIP), lambda j: (0, j + off))
    wsp = lambda off: pl.BlockSpec((3, STRIP), lambda j: (0, j + off))
    bsp = lambda off: pl.BlockSpec((1, STRIP), lambda j: (0, j + off))
    return _pcall(body, name="conv_act", grid=(N_STRIPS,),
                  in_specs=[strip(0), strip(N_STRIPS), wsp(0), wsp(N_STRIPS), bsp(0), bsp(N_STRIPS)],
                  out_specs=pl.BlockSpec((s, STRIP), lambda j: (0, j)), out_shape=_sds((s, D_FF), BF16),
                  scratch_shapes=[pltpu.VMEM((s + SUBLANES, STRIP), F32)] * 2,
                  dims=("parallel",))(up, up, conv_w, conv_w, conv_b, conv_b)


def _down_loss(act, w_down, x1, tgt):
    s = x1.shape[0]
    tm = _row_tile(s)

    def body(a_ref, w_ref, x1_ref, t_ref, dy_ref, dyb_ref, loss_ref):
        i = pl.program_id(0)

        @pl.when(i == 0)
        def _():
            loss_ref[...] = jnp.zeros_like(loss_ref)

        diff = x1_ref[...] + _dot(a_ref[...], w_ref[...]) - t_ref[...]
        dy = diff * (1.0 / D_MODEL)
        dy_ref[...] = dy
        dyb_ref[...] = dy.astype(BF16)
        loss_ref[...] += 0.5 * jnp.sum(diff * dy)

    row = lambda w: pl.BlockSpec((tm, w), lambda i: (i, 0))
    return _pcall(body, name="down_loss", grid=(s // tm,),
                  in_specs=[row(D_FF), pl.BlockSpec((D_FF, D_MODEL), lambda i: (0, 0)), row(D_MODEL), row(D_MODEL)],
                  out_specs=[row(D_MODEL), row(D_MODEL), pl.BlockSpec((SUBLANES, LANES), lambda i: (0, 0))],
                  out_shape=[_sds((s, D_MODEL)), _sds((s, D_MODEL), BF16), _sds((SUBLANES, LANES))],
                  dims=("arbitrary",))(act, w_down, x1, tgt)


def _conv_act_bwd(up, dact, conv_w, conv_b):
    s = up.shape[0]
    ch = CONV_CHUNK

    def body(ug_ref, uv_ref, da_ref, wg_ref, wv_ref, bg_ref, bv_ref, dup_ref, dcw_ref, pg_ref, pv_ref, dg_ref, dv_ref):
        _fill_front_pad(pg_ref, ug_ref, s)
        _fill_front_pad(pv_ref, uv_ref, s)
        zero = jnp.zeros((SUBLANES, STRIP), F32)
        dg_ref[pl.ds(s, SUBLANES), :] = zero
        dv_ref[pl.ds(s, SUBLANES), :] = zero
        wg, wv, bg, bv = wg_ref[...], wv_ref[...], bg_ref[...], bv_ref[...]
        tile_sum = lambda t: jnp.sum(t.reshape(ch // SUBLANES, SUBLANES, STRIP), axis=0)
        accs = [[zero] * 4, [zero] * 4]
        for r0 in range(0, s, ch):
            hg = _conv_rows(pg_ref, wg, bg, r0, ch)
            hv = _conv_rows(pv_ref, wv, bv, r0, ch)
            sg = _sigmoid(hg)
            da = da_ref[pl.ds(r0, ch), :]
            dhs = (da * hv * (sg * (1.0 + hg * (1.0 - sg))), da * (hg * sg))
            for half, (dh, d_ref, p_ref) in enumerate(zip(dhs, (dg_ref, dv_ref), (pg_ref, pv_ref))):
                d_ref[pl.ds(r0, ch), :] = dh
                for k in range(3):
                    accs[half][k] = accs[half][k] + tile_sum(dh * p_ref[pl.ds(r0 + SUBLANES - 2 + k, ch), :])
                accs[half][3] = accs[half][3] + tile_sum(dh)
        for half, (d_ref, w) in enumerate(((dg_ref, wg), (dv_ref, wv))):
            for r0 in range(0, s, ch):
                dup = (d_ref[pl.ds(r0, ch), :] * w[2:3, :] + d_ref[pl.ds(r0 + 1, ch), :] * w[1:2, :]
                       + d_ref[pl.ds(r0 + 2, ch), :] * w[0:1, :])
                dup_ref[half, pl.ds(r0, ch), :] = dup.astype(BF16)
            rid = lax.broadcasted_iota(jnp.int32, (SUBLANES, STRIP), 0)
            out = zero
            for k in range(4):
                out = jnp.where(rid == k, jnp.sum(accs[half][k], axis=0, keepdims=True), out)
            dcw_ref[half] = out

    strip = lambda off: pl.BlockSpec((s, STRIP), lambda j: (0, j + off))
    wsp = lambda off: pl.BlockSpec((3, STRIP), lambda j: (0, j + off))
    bsp = lambda off: pl.BlockSpec((1, STRIP), lambda j: (0, j + off))
    return _pcall(body, name="conv_act_bwd", grid=(N_STRIPS,),
                  in_specs=[strip(0), strip(N_STRIPS), strip(0), wsp(0), wsp(N_STRIPS), bsp(0), bsp(N_STRIPS)],
                  out_specs=[pl.BlockSpec((2, s, STRIP), lambda j: (0, 0, j)), pl.BlockSpec((2, SUBLANES, STRIP), lambda j: (0, 0, j))],
                  out_shape=[_sds((2, s, D_FF), BF16), _sds((2, SUBLANES, D_FF))],
                  scratch_shapes=[pltpu.VMEM((s + SUBLANES, STRIP), F32)] * 4,
                  dims=("parallel",))(up, up, dact, conv_w, conv_w, conv_b, conv_b)


def _mix_bwd(dy, dh2, x1, g_ffn, w_out, y, att, w_glu, b_glu, g_att, g_ssm):
    s = dy.shape[0]
    tm = _row_tile(s)

    def body(dy_ref, dh2_ref, x1_ref, gf_ref, wo_ref, y_ref, att_ref, wg_ref, bg_ref, ga_ref, gs_ref,
             dx1_ref, dx1b_ref, datt_ref, dys_ref, dwg_ref, dgf_ref, dga_ref, dgs_ref, dbg_ref):
        i = pl.program_id(0)

        @pl.when(i == 0)
        def _():
            for r in (dwg_ref, dgf_ref, dga_ref, dgs_ref, dbg_ref):
                r[...] = jnp.zeros_like(r)

        dxn, dgf = _rms_bwd(x1_ref[...], gf_ref[...], dh2_ref[...])
        dx1 = dy_ref[...] + dxn
        dx1_ref[...] = dx1
        dx1b = dx1.astype(BF16)
        dx1b_ref[...] = dx1b
        dgf_ref[...] += dgf
        dma = _dot_nt(dx1b, wo_ref[0:ATTN_W, :])
        dms = _dot_nt(dx1b, wo_ref[ATTN_W:D_MODEL, :])
        datt, dga = _rms_bwd(att_ref[...], ga_ref[...], dma)
        datt_ref[...] = datt
        dga_ref[...] += dga
        yv = y_ref[...]
        ge, sg = _ssm_glu(yv, wg_ref[...], bg_ref[...])
        dssm, dgs = _rms_bwd(ge * sg, gs_ref[...], dms)
        dgs_ref[...] += dgs
        dgl = dssm * ge * sg * (1.0 - sg)
        dglb = dgl.astype(BF16)
        dge = dssm * sg + _dot_nt(dglb, wg_ref[...])
        dbg_ref[...] += jnp.sum(dgl, axis=0, keepdims=True)
        dwg_ref[...] += _dot_tn(ge.astype(BF16), dglb)
        dys_ref[...] = dge * _gelu_grad(yv)

    row = lambda w: pl.BlockSpec((tm, w), lambda i: (i, 0))
    const = lambda a, b: pl.BlockSpec((a, b), lambda i: (0, 0))
    return _pcall(body, name="mix_bwd", grid=(s // tm,),
                  in_specs=[row(D_MODEL), row(D_MODEL), row(D_MODEL), const(1, D_MODEL), const(D_MODEL, D_MODEL), row(SSM_W),
                            row(ATTN_W), const(SSM_W, SSM_W), const(1, SSM_W), const(1, ATTN_W), const(1, SSM_W)],
                  out_specs=[row(D_MODEL), row(D_MODEL), row(ATTN_W), row(SSM_W), const(SSM_W, SSM_W), const(1, D_MODEL),
                             const(1, ATTN_W), const(1, SSM_W), const(1, SSM_W)],
                  out_shape=[_sds((s, D_MODEL)), _sds((s, D_MODEL), BF16), _sds((s, ATTN_W)), _sds((s, SSM_W)),
                             _sds((SSM_W, SSM_W)), _sds((1, D_MODEL)), _sds((1, ATTN_W)), _sds((1, SSM_W)), _sds((1, SSM_W))],
                  dims=("arbitrary",))(dy, dh2, x1, g_ffn, w_out, y, att, w_glu, b_glu, g_att, g_ssm)


def _ssm_bwd(dys, z, ub, xr, xi, bbr, bbi, ar, ai, ccr, cci, dsk):
    s = dys.shape[0]
    tm = _row_tile(s)
    nb = s // tm
    nt = tm // SUBLANES

    def body(dy_ref, u_ref, ub_ref, xr_ref, xi_ref, xrp_ref, xip_ref, bbr_ref, bbi_ref, ar_ref, ai_ref, ccr_ref,
             cci_ref, dsk_ref, du_ref, dbbr_ref, dbbi_ref, dccr_ref, dcci_ref, dar_ref, dai_ref, dd_ref,
             gr_s, gi_s, cr_s, ci_s, accr_s, acci_s):
        i = pl.program_id(1)
        first_block = i == nb - 1

        @pl.when(i == 0)
        def _():
            for r in (cr_s, ci_s, accr_s, acci_s, dbbr_ref, dbbi_ref, dccr_ref, dcci_ref, dd_ref):
                r[...] = jnp.zeros_like(r)

        dy = dy_ref[...]
        dyb = dy.astype(BF16)
        gr_s[...] = _dot_nt(dyb, ccr_ref[0])
        gi_s[...] = -_dot_nt(dyb, cci_ref[0])
        consts = _scan_consts(ar_ref[0], -ai_ref[0], CHUNK_S, True)
        row = lax.broadcasted_iota(jnp.int32, (SUBLANES, CHUNK_S), 0)

        def tile(kk, carry):
            cr, ci, accr, acci = carry
            k = nt - 1 - kk
            sl = pl.ds(pl.multiple_of(k * SUBLANES, SUBLANES), SUBLANES)
            gr, gi = _scan_tile(gr_s[sl, :], gi_s[sl, :], cr, ci, consts, True)
            gr_s[sl, :] = gr
            gi_s[sl, :] = gi
            slp = pl.ds(pl.multiple_of(jnp.maximum(k - 1, 0) * SUBLANES, SUBLANES), SUBLANES)
            inner = k > 0
            pr_t = jnp.where(inner, xr_ref[slp, :], xrp_ref[...])
            pi_t = jnp.where(inner, xi_ref[slp, :], xip_ref[...])
            live = jnp.logical_or(inner, jnp.logical_not(first_block))
            top_r = jnp.where(live, pltpu.roll(pr_t, 1, 0), 0.0)
            top_i = jnp.where(live, pltpu.roll(pi_t, 1, 0), 0.0)
            xpr = jnp.where(row == 0, top_r, pltpu.roll(xr_ref[sl, :], 1, 0))
            xpi = jnp.where(row == 0, top_i, pltpu.roll(xi_ref[sl, :], 1, 0))
            accr = accr + gr * xpr + gi * xpi
            acci = acci + gi * xpr - gr * xpi
            return gr[0:1, :], gi[0:1, :], accr, acci

        zeros = jnp.zeros((SUBLANES, CHUNK_S), F32)
        cr, ci, accr, acci = lax.fori_loop(0, nt, tile, (cr_s[...], ci_s[...], zeros, zeros))
        cr_s[...] = cr
        ci_s[...] = ci
        accr_s[...] += accr
        acci_s[...] += acci
        grb = gr_s[...].astype(BF16)
        gib = gi_s[...].astype(BF16)
        u_b = ub_ref[...]
        du_ref[...] = _dot_nt(grb, bbr_ref[0]) + _dot_nt(gib, bbi_ref[0]) + dsk_ref[...] * dy
        dbbr_ref[0] += _dot_tn(u_b, grb)
        dbbi_ref[0] += _dot_tn(u_b, gib)
        dccr_ref[0] += _dot_tn(xr_ref[...].astype(BF16), dyb)
        dcci_ref[0] -= _dot_tn(xi_ref[...].astype(BF16), dyb)
        dd_ref[...] += jnp.sum(dy * u_ref[...], axis=0, keepdims=True)

        @pl.when(i == nb - 1)
        def _():
            dar_ref[0] = jnp.sum(accr_s[...], axis=0, keepdims=True)
            dai_ref[0] = jnp.sum(acci_s[...], axis=0, keepdims=True)

    ucol0 = U_COL0 // CHUNK_U
    tiles_per_block = tm // SUBLANES
    rb = lambda i: nb - 1 - i
    wspec = lambda a, b: pl.BlockSpec((1, a, b), lambda j, i: (j, 0, 0))
    xblk = pl.BlockSpec((tm, CHUNK_S), lambda j, i: (rb(i), j))
    xprev = pl.BlockSpec((SUBLANES, CHUNK_S), lambda j, i: (jnp.maximum(rb(i) * tiles_per_block - 1, 0), j))
    ublk = pl.BlockSpec((tm, CHUNK_U), lambda j, i: (rb(i), j))
    return _pcall(body, name="ssm_bwd", grid=(SSM_CHUNKS, nb),
                  in_specs=[ublk, pl.BlockSpec((tm, CHUNK_U), lambda j, i: (rb(i), ucol0 + j)), ublk, xblk, xblk, xprev, xprev,
                            wspec(CHUNK_U, CHUNK_S), wspec(CHUNK_U, CHUNK_S), wspec(1, CHUNK_S), wspec(1, CHUNK_S),
                            wspec(CHUNK_S, CHUNK_U), wspec(CHUNK_S, CHUNK_U), pl.BlockSpec((1, CHUNK_U), lambda j, i: (0, j))],
                  out_specs=[ublk, wspec(CHUNK_U, CHUNK_S), wspec(CHUNK_U, CHUNK_S), wspec(CHUNK_S, CHUNK_U),
                             wspec(CHUNK_S, CHUNK_U), wspec(1, CHUNK_S), wspec(1, CHUNK_S),
                             pl.BlockSpec((1, CHUNK_U), lambda j, i: (0, j))],
                  out_shape=[_sds((s, SSM_W)), _sds((SSM_CHUNKS, CHUNK_U, CHUNK_S)), _sds((SSM_CHUNKS, CHUNK_U, CHUNK_S)),
                             _sds((SSM_CHUNKS, CHUNK_S, CHUNK_U)), _sds((SSM_CHUNKS, CHUNK_S, CHUNK_U)),
                             _sds((SSM_CHUNKS, 1, CHUNK_S)), _sds((SSM_CHUNKS, 1, CHUNK_S)), _sds((1, SSM_W))],
                  scratch_shapes=[pltpu.VMEM((tm, CHUNK_S), F32)] * 2 + [pltpu.VMEM((1, CHUNK_S), F32)] * 2
                                 + [pltpu.VMEM((SUBLANES, CHUNK_S), F32)] * 2,
                  dims=("parallel", "arbitrary"))(dys, z, ub, xr, xi, xr, xi, bbr, bbi, ar, ai, ccr, cci, dsk)


def _attn_probs(q, ks, cs, lse, scale, diagonal):
    p = jnp.exp(_dot_nt(q, ks) * scale - cs - lse)
    if diagonal:
        tq, tk = p.shape
        causal = lax.broadcasted_iota(jnp.int32, (tq, tk), 1) <= lax.broadcasted_iota(jnp.int32, (tq, tk), 0)
        p = jnp.where(causal, p, 0.0)
    return p


def _attn_bwd(qh, kh, vh, crow, lse, doh):
    _, s, _ = qh.shape
    tq = _row_tile(s)
    nq = s // tq
    scale = HEAD_DIM ** -0.5
    hp = HEADS_PER_STEP

    def body(q_ref, k_ref, v_ref, c_ref, lse_ref, do_ref, dq_ref, dk_ref, dv_ref, dc_ref, p_s, dp_s):
        i = pl.program_id(1)

        @pl.when(i == 0)
        def _():
            for r in (dk_ref, dv_ref, dc_ref):
                r[...] = jnp.zeros_like(r)

        dobs = [do_ref[hh].astype(BF16) for hh in range(hp)]

        def first(j, dls, diagonal):
            off = pl.multiple_of(j * tq, tq)
            out = []
            for hh in range(hp):
                p = _attn_probs(q_ref[hh], k_ref[hh, pl.ds(off, tq), :], c_ref[hh, :, pl.ds(off, tq)], lse_ref[hh],
                                scale, diagonal)
                dp = _dot_nt(dobs[hh], v_ref[hh, pl.ds(off, tq), :])
                p_s[hh, j] = p
                dp_s[hh, j] = dp
                out.append(dls[hh] + jnp.sum(p * dp, axis=-1, keepdims=True))
            return tuple(out)

        zero_col = jnp.zeros((tq, 1), F32)
        dls = lax.fori_loop(0, i, lambda j, c: first(j, c, False), (zero_col,) * hp)
        dls = first(i, dls, True)

        def second(j, dqs):
            rows = pl.ds(pl.multiple_of(j * tq, tq), tq)
            out = []
            for hh in range(hp):
                p = p_s[hh, j]
                ds = p * (dp_s[hh, j] - dls[hh])
                dsb = ds.astype(BF16)
                dv_ref[hh, rows, :] += _dot_tn(p.astype(BF16), dobs[hh])
                dk_ref[hh, rows, :] += _dot_tn(dsb, q_ref[hh]) * scale
                dc_ref[hh, :, rows] -= jnp.sum(ds, axis=0, keepdims=True)
                out.append(dqs[hh] + _dot(dsb, k_ref[hh, rows, :]))
            return tuple(out)

        dqs = lax.fori_loop(0, i + 1, second, (jnp.zeros((tq, HEAD_DIM), F32),) * hp)
        for hh in range(hp):
            dq_ref[hh] = dqs[hh] * scale

    blk = pl.BlockSpec((hp, tq, HEAD_DIM), lambda h, i: (h, i, 0))
    full = pl.BlockSpec((hp, s, HEAD_DIM), lambda h, i: (h, 0, 0))
    crow_spec = pl.BlockSpec((hp, 1, s), lambda h, i: (h, 0, 0))
    return _pcall(body, name="attn_bwd", grid=(HEADS // hp, nq),
                  in_specs=[blk, full, full, crow_spec, pl.BlockSpec((hp, tq, 1), lambda h, i: (h, i, 0)), blk],
                  out_specs=[blk, full, full, crow_spec],
                  out_shape=[_sds((HEADS, s, HEAD_DIM))] * 3 + [_sds((HEADS, 1, s))],
                  scratch_shapes=[pltpu.VMEM((hp, nq, tq, tq), F32)] * 2,
                  dims=("parallel", "arbitrary"))(qh, kh, vh, crow, lse, doh)


def _prep_bwd(z, dqn, dkn, dv, du, dc, gq, gk, bf, gg):
    s = z.shape[0]
    tm = _row_tile(s)
    nb = s // tm

    def body(z_ref, dqn_ref, dkn_ref, dv_ref, du_ref, dc_ref, gq_ref, gk_ref, bf_ref, gg_ref,
             dz_ref, dgq_ref, dgk_ref, dbf_ref, carry_ref):
        i = pl.program_id(0)

        @pl.when(i == 0)
        def _():
            for r in (dgq_ref, dgk_ref, dbf_ref, carry_ref):
                r[...] = jnp.zeros_like(r)

        gg_m = gg_ref[...]

        def head_norm_bwd(t, g, dn):
            r = lax.rsqrt(_dot_exact_r(t * t, gg_m) * (1.0 / HEAD_DIM) + EPS)
            w = dn * g
            mean_wt = _dot_exact_r(w * t, gg_m) * (1.0 / HEAD_DIM)
            return r * w - t * (r * r * r) * mean_wt, jnp.sum(dn * t * r, axis=0, keepdims=True)

        dq, dgq = head_norm_bwd(z_ref[:, 0:ATTN_W], gq_ref[...], dqn_ref[...])
        dk, dgk = head_norm_bwd(z_ref[:, ATTN_W:2 * ATTN_W], gk_ref[...], dkn_ref[...])
        dgq_ref[...] += dgq
        dgk_ref[...] += dgk
        row = lax.broadcasted_iota(jnp.int32, (tm, tm), 0)
        col = lax.broadcasted_iota(jnp.int32, (tm, tm), 1)
        triu = (col >= row).astype(BF16)
        dlf = _dot_exact_l(triu, dc_ref[...]) + carry_ref[...]
        carry_ref[...] = dlf[0:1, :]
        fl = z_ref[:, F_COL0:F_COL0 + LANES] + bf_ref[...]
        df = dlf * _sigmoid(-fl)
        dbf_ref[...] += jnp.sum(df, axis=0, keepdims=True)
        dz_ref[:, 0:ATTN_W] = dq.astype(BF16)
        dz_ref[:, ATTN_W:2 * ATTN_W] = dk.astype(BF16)
        dz_ref[:, 2 * ATTN_W:3 * ATTN_W] = dv_ref[...].astype(BF16)
        dz_ref[:, U_COL0:U_COL0 + SSM_W] = du_ref[...].astype(BF16)
        dz_ref[:, F_COL0:F_COL0 + LANES] = df.astype(BF16)

    row_spec = lambda w: pl.BlockSpec((tm, w), lambda i: (nb - 1 - i, 0))
    const = lambda shape: pl.BlockSpec(shape, lambda i: (0, 0))
    return _pcall(body, name="prep_bwd", grid=(nb,),
                  in_specs=[row_spec(Z_COLS)] + [row_spec(ATTN_W)] * 4 + [row_spec(LANES), const((1, ATTN_W)),
                            const((1, ATTN_W)), const((1, LANES)), const((ATTN_W, ATTN_W))],
                  out_specs=[row_spec(Z_COLS), const((1, ATTN_W)), const((1, ATTN_W)), const((1, LANES))],
                  out_shape=[_sds((s, Z_COLS), BF16), _sds((1, ATTN_W)), _sds((1, ATTN_W)), _sds((1, LANES))],
                  scratch_shapes=[pltpu.VMEM((1, LANES), F32)], dims=("arbitrary",))(z, dqn, dkn, dv, du, dc, gq, gk, bf, gg)


def _in_norm_bwd(x, g_mix, dh, dx1):
    s = x.shape[0]
    tm = _row_tile(s)

    def body(x_ref, g_ref, dh_ref, dx1_ref, dx_ref, dg_ref):
        i = pl.program_id(0)

        @pl.when(i == 0)
        def _():
            dg_ref[...] = jnp.zeros_like(dg_ref)

        dxn, dg = _rms_bwd(x_ref[...], g_ref[...], dh_ref[...])
        dx_ref[...] = dx1_ref[...] + dxn
        dg_ref[...] += dg

    row = pl.BlockSpec((tm, D_MODEL), lambda i: (i, 0))
    vec = pl.BlockSpec((1, D_MODEL), lambda i: (0, 0))
    return _pcall(body, name="in_norm_bwd", grid=(s // tm,), in_specs=[row, vec, row, row], out_specs=[row, vec],
                  out_shape=[_sds((s, D_MODEL)), _sds((1, D_MODEL))], dims=("arbitrary",))(x, g_mix, dh, dx1)


def _adamw_refs(w_ref, g_ref, m_ref, v_ref, d_ref, mo_ref, vo_ref):
    gv = g_ref[...]
    mn = ADAM_B1 * m_ref[...] + (1.0 - ADAM_B1) * gv
    vn = ADAM_B2 * v_ref[...] + (1.0 - ADAM_B2) * (gv * gv)
    m_hat = mn / (1.0 - ADAM_B1 ** ADAM_STEP)
    v_hat = vn / (1.0 - ADAM_B2 ** ADAM_STEP)
    d_ref[...] = -ADAM_LR * (m_hat / (jnp.sqrt(v_hat) + ADAM_EPS) + ADAM_WD * w_ref[...])
    mo_ref[...] = mn
    vo_ref[...] = vn


def _adamw_small(ws, gs, ms, vs):
    n = len(ws)

    def body(*refs):
        ins, outs = refs[:4 * n], refs[4 * n:]
        for i in range(n):
            _adamw_refs(ins[i], ins[n + i], ins[2 * n + i], ins[3 * n + i], *outs[3 * i:3 * i + 3])

    vm = pl.BlockSpec(memory_space=pltpu.VMEM)
    out_shape = [_sds(w.shape) for w in ws for _ in range(3)]
    return _pallas(body, name="adamw_small", in_specs=[vm] * (4 * n), out_specs=[vm] * (3 * n), out_shape=out_shape,
                   compiler_params=pltpu.CompilerParams(vmem_limit_bytes=VMEM_LIMIT))(*ws, *gs, *ms, *vs)


def _adamw(w, g, m, v, *, name):
    r, c = w.shape
    tr = r
    for cand in (256, 176, 128, 64):
        if r > cand and r % cand == 0:
            tr = cand
            break

    def body(w_ref, g_ref, m_ref, v_ref, d_ref, mo_ref, vo_ref):
        _adamw_refs(w_ref, g_ref, m_ref, v_ref, d_ref, mo_ref, vo_ref)

    spec = pl.BlockSpec((tr, c), lambda i: (i, 0))
    return _pcall(body, name=name, grid=(r // tr,), in_specs=[spec] * 4, out_specs=[spec] * 3,
                  out_shape=[_sds((r, c))] * 3, dims=("parallel",))(w, g, m, v)


def _prefetch_call(body, *, name, grid, in_specs, out_specs, out_shape, operands):
    grid_spec = pltpu.PrefetchScalarGridSpec(num_scalar_prefetch=1, grid=grid, in_specs=in_specs, out_specs=out_specs)
    params = pltpu.CompilerParams(dimension_semantics=("parallel",) * len(grid), vmem_limit_bytes=VMEM_LIMIT)
    return _pallas(body, name=name, grid_spec=grid_spec, out_shape=out_shape, compiler_params=params)(*operands)


def _half_rows_tile(hr):
    return hr if hr <= 256 else 176 if hr % 176 == 0 else 256


def _add_half(g, landed, place, *, name):
    def body(place_ref, g_ref, l_ref, o_ref):
        own = g_ref[0] if len(g_ref.shape) == 4 else g_ref[...]
        o_ref[...] = (own + l_ref[...]).astype(BF16)

    if g.ndim == 4:
        _, _, hr, c = g.shape
        tr = _half_rows_tile(hr)
        blk = (1, tr, c)
        return _prefetch_call(
            body, name=name, grid=(N_CHIPS, hr // tr),
            in_specs=[pl.BlockSpec((1,) + blk, lambda j, i, p: (j, p[1], i, 0)), pl.BlockSpec(blk, lambda j, i, p: (j, i, 0))],
            out_specs=pl.BlockSpec(blk, lambda j, i, p: (j, i, 0)), out_shape=_sds(landed.shape, BF16),
            operands=(place, g, landed))
    hr, c = landed.shape
    tr, tc = 256, _tile(c, 2176)
    nb = hr // tr
    return _prefetch_call(
        body, name=name, grid=(nb, c // tc),
        in_specs=[pl.BlockSpec((tr, tc), lambda i, j, p: (p[1] * nb + i, j)), pl.BlockSpec((tr, tc), lambda i, j, p: (i, j))],
        out_specs=pl.BlockSpec((tr, tc), lambda i, j, p: (i, j)), out_shape=_sds(landed.shape, BF16),
        operands=(place, g, landed))


def _sum_chips(chip_sum, lands, place, *, name, tc, window_stride=0):
    _, hr, c = lands.shape
    tr = _half_rows_tile(hr)
    nb = hr // tr
    ncb = c // tc

    def body(place_ref, own_ref, a_ref, b_ref, c_ref, o_ref):
        own = own_ref[0] if len(own_ref.shape) == 3 else own_ref[...]
        o_ref[...] = ((own.astype(F32) + a_ref[0].astype(F32)) + b_ref[0].astype(F32)) + c_ref[0].astype(F32)

    land = lambda k: pl.BlockSpec((1, tr, tc), lambda i, j, p: ((p[0] + k) % N_CHIPS, i, j))
    if chip_sum.ndim == 3:
        own_spec = land(0)
    else:
        stride = window_stride // tc
        own_spec = pl.BlockSpec((tr, tc), lambda i, j, p: (i, p[0] * stride + j))
    return _prefetch_call(
        body, name=name, grid=(nb, ncb), in_specs=[own_spec, land(1), land(2), land(3)],
        out_specs=pl.BlockSpec((tr, tc), lambda i, j, p: (p[1] * nb + i, j)), out_shape=_sds((2 * hr, c)),
        operands=(place, chip_sum, lands, lands, lands))


_HBM = pl.BlockSpec(memory_space=pltpu.HBM)


def _place():
    x, y, c = lax.axis_index("x"), lax.axis_index("y"), lax.axis_index("c")
    chips = [(1 - x, y), (x, 1 - y), (1 - x, 1 - y)]
    return x, y, c, chips


def _rcopy(src, dst, send_sem, recv_sem, to):
    return pltpu.make_async_remote_copy(src_ref=src, dst_ref=dst, send_sem=send_sem, recv_sem=recv_sem,
                                        device_id=to, device_id_type=MESH)


N_BIG = 5
UP_COLS = 2 * D_FF // N_CHIPS
IN_WINDOW = 640
IN_STRIDE = 512


def _gather_weights(shards, conv_w):
    def body(*refs):
        srcs, cw_ref = refs[0:N_BIG], refs[N_BIG]
        outs, cw_out = refs[N_BIG + 1:2 * N_BIG + 1], refs[2 * N_BIG + 1]
        send_sems, recv_sems, local_sem = refs[2 * N_BIG + 2:]
        x, y, c, chips = _place()
        sibling = (x, y, 1 - c)
        me = 2 * x + y
        own_up = pltpu.make_async_copy(
            srcs[3], outs[3].at[:, pl.ds(pl.multiple_of(me * UP_COLS, LANES), UP_COLS)], local_sem)
        own_up.start()

        def half(w, chip, hc):
            hr, cols = srcs[w].shape[0] // 2, srcs[w].shape[1]
            if len(outs[w].shape) == 2:
                return outs[w].at[pl.ds(hc * hr, hr), pl.ds(pl.multiple_of(chip * cols, LANES), cols)]
            return outs[w].at[chip, pl.ds(hc * hr, hr), :]

        first, passed = [], []
        for w in range(N_BIG):
            hr = srcs[w].shape[0] // 2
            for k, (cx, cy) in enumerate(chips):
                first.append(_rcopy(srcs[w].at[pl.ds(c * hr, hr), :], half(w, me, c), send_sems.at[3 * w + k],
                                    recv_sems.at[3 * w + k], (cx, cy, c)))
        n_first = len(first)
        for k, (cx, cy) in enumerate(chips):
            first.append(_rcopy(cw_ref, cw_out.at[me], send_sems.at[2 * n_first + k], recv_sems.at[2 * n_first + k],
                                (cx, cy, c)))
        for cp in first:
            cp.start()
        for w in range(N_BIG):
            for k, (cx, cy) in enumerate(chips):
                landed = half(w, 2 * cx + cy, c)
                _rcopy(landed, landed, send_sems.at[3 * w + k], recv_sems.at[3 * w + k], sibling).wait_recv()
                fwd = _rcopy(landed, landed, send_sems.at[n_first + 3 * w + k], recv_sems.at[n_first + 3 * w + k], sibling)
                fwd.start()
                passed.append(fwd)
        for w in range(N_BIG):
            for k, (cx, cy) in enumerate(chips):
                other = half(w, 2 * cx + cy, 1 - c)
                _rcopy(other, other, send_sems.at[n_first + 3 * w + k], recv_sems.at[n_first + 3 * w + k], sibling).wait_recv()
        for k, (cx, cy) in enumerate(chips):
            slot = cw_out.at[2 * cx + cy]
            _rcopy(slot, slot, send_sems.at[2 * n_first + k], recv_sems.at[2 * n_first + k], sibling).wait_recv()
        for cp in first + passed:
            cp.wait_send()
        own_up.wait()

    out_shape = [_sds((shards[3].shape[0], N_CHIPS * UP_COLS), BF16) if w == 3 else _sds((N_CHIPS,) + shards[w].shape, BF16)
                 for w in range(N_BIG)] + [_sds((N_CHIPS,) + conv_w.shape)]
    n_sems = 2 * 3 * N_BIG + 3
    return _pallas(body, name="gather_weights", in_specs=[_HBM] * (N_BIG + 1), out_specs=[_HBM] * (N_BIG + 1),
                   out_shape=out_shape,
                   scratch_shapes=[pltpu.SemaphoreType.DMA((n_sems,)), pltpu.SemaphoreType.DMA((n_sems,)),
                                   pltpu.SemaphoreType.DMA])(*shards, conv_w)


def _swap_halves(grads):
    def body(*refs):
        g_refs, land_refs = refs[0:N_BIG], refs[N_BIG:2 * N_BIG]
        send_sems, recv_sems = refs[2 * N_BIG:]
        x, y, c, _ = _place()
        cps = []
        for w in range(N_BIG):
            if len(g_refs[w].shape) == 4:
                theirs = g_refs[w].at[:, 1 - c]
            else:
                hr = g_refs[w].shape[0] // 2
                theirs = g_refs[w].at[pl.ds((1 - c) * hr, hr), :]
            cps.append(_rcopy(theirs, land_refs[w], send_sems.at[w], recv_sems.at[w], (x, y, 1 - c)))
        for cp in cps:
            cp.start()
        for cp in cps:
            cp.wait()

    out_shape = [_sds((g.shape[0], g.shape[2], g.shape[3])) if g.ndim == 4 else _sds((g.shape[0] // 2, g.shape[1]))
                 for g in grads]
    return _pallas(body, name="swap_halves", in_specs=[_HBM] * N_BIG, out_specs=[_HBM] * N_BIG, out_shape=out_shape,
                   scratch_shapes=[pltpu.SemaphoreType.DMA((N_BIG,)), pltpu.SemaphoreType.DMA((N_BIG,))])(*grads)


def _scatter_chips(chip_sums):
    windows = {0: (IN_STRIDE, IN_WINDOW), 3: (UP_COLS, UP_COLS)}

    def body(*refs):
        s_refs, land_refs = refs[0:N_BIG], refs[N_BIG:2 * N_BIG]
        send_sems, recv_sems = refs[2 * N_BIG:]
        x, y, c, chips = _place()
        me = 2 * x + y
        sends = []
        for w in range(N_BIG):
            for k, (cx, cy) in enumerate(chips):
                tgt = 2 * cx + cy
                if w in windows:
                    stride, width = windows[w]
                    part = s_refs[w].at[:, pl.ds(pl.multiple_of(tgt * stride, LANES), width)]
                else:
                    part = s_refs[w].at[tgt]
                sends.append(_rcopy(part, land_refs[w].at[me], send_sems.at[3 * w + k], recv_sems.at[3 * w + k], (cx, cy, c)))
        for cp in sends:
            cp.start()
        for w in range(N_BIG):
            for k, (cx, cy) in enumerate(chips):
                slot = land_refs[w].at[2 * cx + cy]
                _rcopy(slot, slot, send_sems.at[3 * w + k], recv_sems.at[3 * w + k], (cx, cy, c)).wait_recv()
        for cp in sends:
            cp.wait_send()

    out_shape = [_sds((N_CHIPS, s.shape[0], windows[w][1]), BF16) if w in windows else _sds(s.shape, BF16)
                 for w, s in enumerate(chip_sums)]
    return _pallas(body, name="scatter_chips", in_specs=[_HBM] * N_BIG, out_specs=[_HBM] * N_BIG, out_shape=out_shape,
                   scratch_shapes=[pltpu.SemaphoreType.DMA((3 * N_BIG,)), pltpu.SemaphoreType.DMA((3 * N_BIG,))])(*chip_sums)


def _join_halves(reds):
    def body(*refs):
        out_refs = refs[N_BIG:2 * N_BIG]
        send_sems, recv_sems = refs[2 * N_BIG:]
        x, y, c, _ = _place()
        cps = []
        for w in range(N_BIG):
            hr = out_refs[w].shape[0] // 2
            mine = out_refs[w].at[pl.ds(c * hr, hr), :]
            cps.append(_rcopy(mine, mine, send_sems.at[w], recv_sems.at[w], (x, y, 1 - c)))
        for cp in cps:
            cp.start()
        for w in range(N_BIG):
            hr = out_refs[w].shape[0] // 2
            theirs = out_refs[w].at[pl.ds((1 - c) * hr, hr), :]
            _rcopy(theirs, theirs, send_sems.at[w], recv_sems.at[w], (x, y, 1 - c)).wait_recv()
        for cp in cps:
            cp.wait_send()

    return _pallas(body, name="join_halves", in_specs=[_HBM] * N_BIG, out_specs=[_HBM] * N_BIG,
                   out_shape=[_sds(r.shape) for r in reds], input_output_aliases={w: w for w in range(N_BIG)},
                   scratch_shapes=[pltpu.SemaphoreType.DMA((N_BIG,)), pltpu.SemaphoreType.DMA((N_BIG,))])(*reds)


def _allreduce_small(v):
    m_per = v.shape[0]

    def body(v_ref, out_ref, all_ref, send_sems, recv_sems, local_sem):
        x, y, c, chips = _place()
        me, sibling = (x, y, c), (x, y, 1 - c)

        def rows(px, py, pc):
            return all_ref.at[pl.ds((4 * px + 2 * py + pc) * m_per, m_per), :]

        def copy(k, block, to, src=None):
            return _rcopy(rows(*block) if src is None else src, rows(*block), send_sems.at[k], recv_sems.at[k], to)

        mine = pltpu.make_async_copy(v_ref, rows(*me), local_sem)
        mine.start()
        first = [copy(0, me, sibling, src=v_ref)]
        first += [copy(1 + k, me, (*chip, c), src=v_ref) for k, chip in enumerate(chips)]
        for cp in first:
            cp.start()
        passed = [copy(4 + k, (*chip, c), sibling) for k, chip in enumerate(chips)]
        for k, chip in enumerate(chips):
            copy(1 + k, (*chip, c), me).wait_recv()
            passed[k].start()
        copy(0, sibling, me).wait_recv()
        for k, chip in enumerate(chips):
            copy(4 + k, (*chip, 1 - c), me).wait_recv()
        for cp in first + passed:
            cp.wait_send()
        mine.wait()
        acc = all_ref[pl.ds(0, m_per), :]
        for d in range(1, 8):
            acc = acc + all_ref[pl.ds(d * m_per, m_per), :]
        out_ref[...] = acc

    vm = pl.BlockSpec(memory_space=pltpu.VMEM)
    return _pallas(body, name="allreduce_small", in_specs=[vm], out_specs=vm, out_shape=_sds((m_per, LANES)),
                          scratch_shapes=[pltpu.VMEM((8 * m_per, LANES), F32), pltpu.SemaphoreType.DMA((7,)),
                                          pltpu.SemaphoreType.DMA((7,)), pltpu.SemaphoreType.DMA],
                          compiler_params=pltpu.CompilerParams(vmem_limit_bytes=VMEM_LIMIT))(v)


def _to_heads(t):
    s = t.shape[0]
    return t.reshape(s, HEADS, HEAD_DIM).transpose(1, 0, 2)


def _from_heads(t):
    s = t.shape[1]
    return t.transpose(1, 0, 2).reshape(s, HEADS * HEAD_DIM)


def _reorder_in_cols(w):
    pad = jnp.zeros((w.shape[0], Z_COLS - IN_COLS), w.dtype)
    return jnp.concatenate([w[:, :3 * ATTN_W], w[:, 3 * ATTN_W + HEADS:], w[:, 3 * ATTN_W:3 * ATTN_W + HEADS], pad], axis=1)


def _restore_in_cols(w):
    return jnp.concatenate([w[:, :3 * ATTN_W], w[:, F_COL0:F_COL0 + HEADS], w[:, U_COL0:U_COL0 + SSM_W]], axis=1)


def _block_diag(blocks):
    j, g, a, b = blocks.shape
    eye = jnp.eye(g, dtype=bool)[None, :, None, :, None]
    return jnp.where(eye, blocks[:, :, :, None, :], jnp.zeros((), blocks.dtype)).reshape(j, g * a, g * b)


def _diag_blocks(m, a, b):
    j = m.shape[0]
    g = m.shape[1] // a
    t = m.reshape(j, g, a, g, b)
    eye = jnp.eye(g, dtype=bool)[None, :, None, :, None]
    return jnp.sum(jnp.where(eye, t, 0.0), axis=3)


def _pack_rows(parts, rows, dtype):
    used = sum(p.shape[0] for p in parts)
    return jnp.concatenate([p.astype(dtype) for p in parts] + [jnp.zeros((rows - used, D_MODEL), dtype)], axis=0)


_SMALL = (("g_mix", (1024,)), ("b_f", (8,)), ("g_q", (64,)), ("g_k", (64,)), ("lambda_re", (32, 64)),
          ("lambda_im", (32, 64)), ("log_step", (32,)), ("b_re", (32, 64, 16)), ("b_im", (32, 64, 16)),
          ("c_re", (32, 16, 64)), ("c_im", (32, 16, 64)), ("d_skip", (32, 16)), ("b_glu", (512,)),
          ("g_attn_out", (512,)), ("g_ssm_out", (512,)), ("g_ffn", (1024,)), ("conv_b", (5632,)))


def _small_rows(shape):
    return -(-math.prod(shape) // LANES)


def _pack_small(arrs, extra=()):
    parts = []
    for a in list(arrs) + list(extra):
        flat = a.reshape(-1)
        rows = -(-flat.shape[0] // LANES)
        parts.append(jnp.pad(flat, (0, rows * LANES - flat.shape[0])).reshape(rows, LANES))
    total = sum(p.shape[0] for p in parts)
    pad = -total % SUBLANES
    if pad:
        parts.append(jnp.zeros((pad, LANES), F32))
    return jnp.concatenate(parts, axis=0)


def _unpack_small(buf, shapes):
    out, r = [], 0
    for shape in shapes:
        n = math.prod(shape)
        rows = -(-n // LANES)
        out.append(buf[r:r + rows].reshape(-1)[:n].reshape(shape))
        r += rows
    return out


def _local_step(x, tgt, w_in_r, w_glu_b, w_out_b, w_up_b, w_down_b, conv_w_full, p):
    s = x.shape[0]
    row = lambda v: v.reshape(1, -1)
    g_mix, g_ffn = row(p["g_mix"]), row(p["g_ffn"])
    g_att, g_ssm, b_glu, conv_b = row(p["g_attn_out"]), row(p["g_ssm_out"]), row(p["b_glu"]), row(p["conv_b"])
    gq = row(jnp.tile(p["g_q"], HEADS))
    gk = row(jnp.tile(p["g_k"], HEADS))
    bf = row(jnp.pad(p["b_f"], (0, LANES - HEADS)))
    gg = jnp.kron(jnp.eye(HEADS, dtype=F32), jnp.ones((HEAD_DIM, HEAD_DIM), F32)).astype(BF16)
    dsk = row(p["d_skip"])

    rep = lambda a: jnp.repeat(a, SSM_GROUP, axis=0)
    lr, li = rep(p["lambda_re"]), rep(p["lambda_im"])
    ls = rep(jnp.broadcast_to(p["log_step"][:, None], (SSM_GROUPS, SSM_STATE)))
    bt_re = p["b_re"].transpose(0, 2, 1).reshape(_PARAM_SHAPE)
    bt_im = p["b_im"].transpose(0, 2, 1).reshape(_PARAM_SHAPE)
    a_re_rep, a_im_rep, bb_re, bb_im = _ssm_params(lr, li, ls, bt_re, bt_im)
    ar = a_re_rep[::SSM_GROUP].reshape(SSM_CHUNKS, 1, CHUNK_S)
    ai = a_im_rep[::SSM_GROUP].reshape(SSM_CHUNKS, 1, CHUNK_S)
    chunked = lambda t: t.reshape(SSM_CHUNKS, SSM_GROUPS // SSM_CHUNKS, SSM_GROUP, SSM_STATE)
    bbr = _block_diag(chunked(bb_re)).astype(BF16)
    bbi = _block_diag(chunked(bb_im)).astype(BF16)
    to_cc = lambda c: _block_diag(chunked(c).transpose(0, 1, 3, 2)).astype(BF16)
    ccr, cci = to_cc(p["c_re"]), to_cc(p["c_im"])

    hb, z = _in_proj(x, g_mix, w_in_r)
    qn, kn, vb, ub, c128 = _attn_prep(z, gq, gk, bf, gg)
    qh, kh, vh = _to_heads(qn), _to_heads(kn), _to_heads(vb)
    crow = c128[:, :HEADS].T.reshape(HEADS, 1, s)
    oh, lse = _attn_fwd(qh, kh, vh, crow)
    att = _from_heads(oh)
    xr, xi, y = _ssm_fwd(ub, z, bbr, bbi, ar, ai, ccr, cci, dsk)
    x1, mixb, h2b = _mix_out(y, att, x, w_glu_b, b_glu, g_att, g_ssm, w_out_b, g_ffn)
    up = _mm(h2b, w_up_b, name="ffn_up", tm=1024, tn=1408, tk=1024)
    act = _conv_act(up, conv_w_full, conv_b)
    dy, dyb, loss_blk = _down_loss(act, w_down_b, x1, tgt)

    d_w_down = _mm(act, dyb, ta=True, name="d_w_down", tm=1408, tn=1024, tk=2048)
    dact = _mm(dyb, w_down_b, tb=True, name="d_act", tm=1024, tn=1408, tk=1024)
    dupb, dcw = _conv_act_bwd(up, dact, conv_w_full, conv_b)
    d_w_up = _mm(h2b, dupb, ta=True, b_parts=2, name="d_w_up", tm=1024, tn=1408, tk=2048)
    dh2 = _mm(dupb, w_up_b, tb=True, a_parts=2, name="d_h2", tm=1024, tn=1024, tk=1408)
    dx1, dx1b, datt, dys, d_w_glu, d_g_ffn, d_g_att, d_g_ssm, d_b_glu = _mix_bwd(
        dy, dh2, x1, g_ffn, w_out_b, y, att, w_glu_b, b_glu, g_att, g_ssm)
    d_w_out = _mm(mixb, dx1b, ta=True, name="d_w_out", tm=1024, tn=1024, tk=2048)
    du, dbbr, dbbi, dccr, dcci, dar, dai, dd = _ssm_bwd(dys, z, ub, xr, xi, bbr, bbi, ar, ai, ccr, cci, dsk)
    doh = _to_heads(datt)
    dqh, dkh, dvh, dcrow = _attn_bwd(qh, kh, vh, crow, lse, doh)
    dc128 = jnp.pad(dcrow.reshape(HEADS, s).T, ((0, 0), (0, LANES - HEADS)))
    dzb, d_gq, d_gk, d_bf = _prep_bwd(z, _from_heads(dqh), _from_heads(dkh), _from_heads(dvh), du, dc128, gq, gk, bf, gg)
    d_w_in_r = _mm(hb, dzb, ta=True, name="d_w_in", tm=512, tn=Z_COLS, tk=2048)
    dh = _mm(dzb, w_in_r, tb=True, name="d_h", tm=1024, tn=1024, tk=Z_COLS)
    dx, d_g_mix = _in_norm_bwd(x, g_mix, dh, dx1)

    unchunk = lambda t: t.reshape(_PARAM_SHAPE)
    dbb_re = unchunk(_diag_blocks(dbbr, SSM_GROUP, SSM_STATE))
    dbb_im = unchunk(_diag_blocks(dbbi, SSM_GROUP, SSM_STATE))
    first_row = (jnp.arange(_PARAM_SHAPE[0]) % SSM_GROUP == 0)[:, None]
    da_re = jnp.where(first_row, rep(dar.reshape(SSM_GROUPS, SSM_STATE)), 0.0)
    da_im = jnp.where(first_row, rep(dai.reshape(SSM_GROUPS, SSM_STATE)), 0.0)
    expand_t = (jnp.arange(SSM_GROUPS)[:, None] == (jnp.arange(_PARAM_SHAPE[0]) // SSM_GROUP)[None, :]).astype(BF16)
    d_lr, d_li, d_ls, d_bt_re, d_bt_im = _ssm_params_bwd(lr, li, ls, bt_re, bt_im, da_re, da_im, dbb_re, dbb_im, expand_t)
    from_bt = lambda t: t.reshape(SSM_GROUPS, SSM_GROUP, SSM_STATE).transpose(0, 2, 1)
    from_cc = lambda t: _diag_blocks(t, SSM_STATE, SSM_GROUP).transpose(0, 1, 3, 2).reshape(SSM_GROUPS, SSM_GROUP, SSM_STATE)

    small = {
        "g_mix": d_g_mix, "b_f": d_bf[0, :HEADS], "g_q": d_gq.reshape(HEADS, HEAD_DIM).sum(0),
        "g_k": d_gk.reshape(HEADS, HEAD_DIM).sum(0), "lambda_re": d_lr, "lambda_im": d_li, "log_step": d_ls,
        "b_re": from_bt(d_bt_re), "b_im": from_bt(d_bt_im), "c_re": from_cc(dccr), "c_im": from_cc(dcci),
        "d_skip": dd, "b_glu": d_b_glu, "g_attn_out": d_g_att, "g_ssm_out": d_g_ssm, "g_ffn": d_g_ffn,
        "conv_b": dcw[:, 3],
    }
    big = {"w_in": _restore_in_cols(d_w_in_r), "w_glu": d_w_glu, "w_out": d_w_out, "w_up": d_w_up, "w_down": d_w_down}
    return loss_blk[0, 0], dx, big, small, dcw[:, 0:3].transpose(1, 0, 2).reshape(3, 2 * D_FF)


def kernel(x, g_mix, w_in, b_f, g_q, g_k, lambda_re, lambda_im, log_step, b_re, b_im, c_re, c_im, d_skip, w_glu, b_glu, g_attn_out, g_ssm_out, w_out, g_ffn, w_up, conv_w, conv_b, w_down, loss_target, m_g_mix, m_w_in, m_b_f, m_g_q, m_g_k, m_lambda_re, m_lambda_im, m_log_step, m_b_re, m_b_im, m_c_re, m_c_im, m_d_skip, m_w_glu, m_b_glu, m_g_attn_out, m_g_ssm_out, m_w_out, m_g_ffn, m_w_up, m_conv_w, m_conv_b, m_w_down, v_g_mix, v_w_in, v_b_f, v_g_q, v_g_k, v_lambda_re, v_lambda_im, v_log_step, v_b_re, v_b_im, v_c_re, v_c_im, v_d_skip, v_w_glu, v_b_glu, v_g_attn_out, v_g_ssm_out, v_w_out, v_g_ffn, v_w_up, v_conv_w, v_conv_b, v_w_down):
    args = dict(locals())
    order = ["g_mix", "w_in", "b_f", "g_q", "g_k", "lambda_re", "lambda_im", "log_step", "b_re", "b_im", "c_re", "c_im",
             "d_skip", "w_glu", "b_glu", "g_attn_out", "g_ssm_out", "w_out", "g_ffn", "w_up", "conv_w", "conv_b", "w_down"]
    cx, cy, cc = lax.axis_index("x"), lax.axis_index("y"), lax.axis_index("c")
    chip = 2 * cx + cy

    big_names = ("w_in", "w_glu", "w_out", "w_up", "w_down")
    shards = [args[n].astype(BF16) for n in big_names]
    g_in, g_glu, g_out, w_up_b, g_down, g_cw = _gather_weights(shards, conv_w)
    own = lambda stacked, mine: lax.dynamic_update_slice(stacked, mine[None], (chip,) + (0,) * mine.ndim)
    g_in, g_glu, g_out, g_down = own(g_in, shards[0]), own(g_glu, shards[1]), own(g_out, shards[2]), own(g_down, shards[4])
    w_in_r = _reorder_in_cols(g_in.transpose(1, 0, 2).reshape(D_MODEL, IN_COLS))
    w_glu_b = g_glu.reshape(SSM_W, SSM_W)
    w_out_b = g_out.reshape(D_MODEL, D_MODEL)
    w_down_b = g_down.reshape(D_FF, D_MODEL)
    conv_w_full = own(g_cw, conv_w).transpose(1, 0, 2).reshape(3, 2 * D_FF)

    loss_part, dx, big, small, d_conv_w = _local_step(x[0], loss_target[0], w_in_r, w_glu_b, w_out_b, w_up_b, w_down_b,
                                                      conv_w_full, args)
    loss = lax.psum(loss_part, ("x", "y", "c"))

    place = jnp.stack([chip, cc]).astype(jnp.int32)
    halves = lambda t: t.reshape(N_CHIPS, 2, t.shape[0] // (2 * N_CHIPS), t.shape[1])
    d_in = jnp.pad(big["w_in"], ((0, 0), (0, Z_COLS - IN_COLS)))
    grads = [d_in, halves(big["w_glu"]), halves(big["w_out"]), big["w_up"], halves(big["w_down"])]
    landed = _swap_halves(grads)
    sums = [_add_half(g, l, place, name="add_" + n) for g, l, n in zip(grads, landed, big_names)]
    lands = _scatter_chips(sums)
    col_blocks = (LANES, SSM_W, D_MODEL, UP_COLS, D_MODEL)
    strides = (IN_STRIDE, 0, 0, UP_COLS, 0)
    reds = _join_halves([_sum_chips(s, l, place, name="sum_" + n, tc=tc, window_stride=st)
                         for s, l, n, tc, st in zip(sums, lands, big_names, col_blocks, strides)])
    g_big = dict(zip(big_names, reds))
    g_big["w_in"] = lax.dynamic_slice_in_dim(reds[0], 2 * chip, IN_COLS // N_CHIPS, axis=1)

    small_names = [n for n, _ in _SMALL]
    small_shapes = [sh for _, sh in _SMALL]
    gsum = _allreduce_small(_pack_small([small[n] for n in small_names], extra=[d_conv_w]))
    g_small = _unpack_small(gsum, small_shapes + [(3, 2 * D_FF)])
    g_conv_w = lax.dynamic_slice_in_dim(g_small[-1], chip * (2 * D_FF // N_CHIPS), 2 * D_FF // N_CHIPS, axis=1)
    g_small = dict(zip(small_names, g_small[:-1]))

    grad, delta, new_m, new_v = {}, {}, {}, {}
    for n in ("w_in", "w_glu", "w_out", "w_up", "w_down"):
        grad[n] = g_big[n]
        delta[n], new_m[n], new_v[n] = _adamw(args[n], g_big[n], args["m_" + n], args["v_" + n], name="adamw_" + n)
    grad["conv_w"] = g_conv_w
    delta["conv_w"], new_m["conv_w"], new_v["conv_w"] = _adamw(conv_w, g_conv_w, m_conv_w, v_conv_w, name="adamw_conv_w")
    stepped = _adamw_small([args[n] for n in small_names], [g_small[n] for n in small_names],
                           [args["m_" + n] for n in small_names], [args["v_" + n] for n in small_names])
    for i, n in enumerate(small_names):
        grad[n] = g_small[n]
        delta[n], new_m[n], new_v[n] = stepped[3 * i:3 * i + 3]

    return (loss, dx[None], *[grad[n] for n in order], *[delta[n] for n in order], *[new_m[n] for n in order],
            *[new_v[n] for n in order])
```

```python
import math

import jax
import jax.numpy as jnp
from jax import lax
from jax.experimental import pallas as pl
from jax.experimental.pallas import tpu as pltpu

F32 = jnp.float32
BF16 = jnp.bfloat16

D_MODEL = 1024
HEADS = 8
HEAD_DIM = 64
ATTN_W = 512
SSM_W = 512
SSM_GROUPS = 32
SSM_GROUP = 16
SSM_STATE = 64
N_STATE = SSM_GROUPS * SSM_STATE
D_FF = 2816
IN_COLS = 2056
Z_COLS = 2176
U_COL0 = 1536
F_COL0 = 2048
EPS = 1e-6
NEG_INF = -1e30
N_CHIPS = 4
LANES = 128
SUBLANES = 8
SSM_CHUNKS = 4
CHUNK_U = SSM_W // SSM_CHUNKS
CHUNK_S = N_STATE // SSM_CHUNKS
HEADS_PER_STEP = 2
STRIP = 128
N_STRIPS = D_FF // STRIP

ROWS_IN, ROWS_GLU, ROWS_OUT, ROWS_UP, ROWS_DOWN = 514, 64, 256, 1408, 704
OFF_GLU = ROWS_IN
OFF_OUT = OFF_GLU + ROWS_GLU
OFF_UP = OFF_OUT + ROWS_OUT
OFF_DOWN = OFF_UP + ROWS_UP
OFF_SPARE = OFF_DOWN + ROWS_DOWN
PACK_ROWS = 2976
HALF_ROWS = PACK_ROWS // 2
CONVW_ROWS = 9

ADAM_LR = 0.001
ADAM_B1 = 0.9
ADAM_B2 = 0.999
ADAM_EPS = 1e-08
ADAM_WD = 0.01
ADAM_STEP = 10

VMEM_LIMIT = 56 * 1024 * 1024
MESH = pl.DeviceIdType.MESH


def _pallas(body, **kw):
    return pl.pallas_call(body, **kw)


def _pcall(body, *, name, out_shape, in_specs, out_specs, grid=(), scratch_shapes=(), dims=None):
    params = pltpu.CompilerParams(dimension_semantics=dims, vmem_limit_bytes=VMEM_LIMIT)
    return _pallas(body, name=name, grid=grid, in_specs=in_specs, out_specs=out_specs,
                   out_shape=out_shape, scratch_shapes=scratch_shapes, compiler_params=params)


def _sds(shape, dtype=F32):
    return jax.ShapeDtypeStruct(shape, dtype)


def _dot(a, b):
    return jnp.dot(a, b, preferred_element_type=F32)


def _dot_nt(a, b):
    return lax.dot_general(a, b, (((1,), (1,)), ((), ())), preferred_element_type=F32)


def _dot_tn(a, b):
    return lax.dot_general(a, b, (((0,), (0,)), ((), ())), preferred_element_type=F32)


def _split3(x):
    hi = x.astype(BF16)
    r = x - hi.astype(F32)
    mid = r.astype(BF16)
    lo = (r - mid.astype(F32)).astype(BF16)
    return hi, mid, lo


def _dot_exact_r(x, m01):
    hi, mid, lo = _split3(x)
    return _dot(hi, m01) + _dot(mid, m01) + _dot(lo, m01)


def _dot_exact_l(m01, x):
    hi, mid, lo = _split3(x)
    return _dot(m01, hi) + _dot(m01, mid) + _dot(m01, lo)


def _sigmoid(x):
    return 1.0 / (1.0 + jnp.exp(-x))


def _rms(x, g):
    r = lax.rsqrt(jnp.mean(x * x, axis=-1, keepdims=True) + EPS)
    return x * r * g


def _rms_bwd(x, g, dy):
    r = lax.rsqrt(jnp.mean(x * x, axis=-1, keepdims=True) + EPS)
    w = dy * g
    dx = r * w - x * (r * r * r) * jnp.mean(w * x, axis=-1, keepdims=True)
    dg = jnp.sum(dy * x * r, axis=0, keepdims=True)
    return dx, dg


_GELU_K = math.sqrt(2.0 / math.pi)
_GELU_C = 0.044715


def _gelu(y):
    return y * (0.5 * (1.0 + jnp.tanh(_GELU_K * (y + _GELU_C * (y * y * y)))))


def _gelu_grad(y):
    t = jnp.tanh(_GELU_K * (y + _GELU_C * (y * y * y)))
    return 0.5 * (1.0 + t) + 0.5 * y * (1.0 - t * t) * (_GELU_K * (1.0 + 3.0 * _GELU_C * y * y))


def _tile(n, pref):
    if n <= pref:
        return n
    divs = [t for t in range(LANES, n + 1, LANES) if n % t == 0]
    below = [t for t in divs if t <= pref]
    if below and 2 * below[-1] >= pref:
        return below[-1]
    above = [t for t in divs if t > pref]
    return above[0] if above else n


def _row_tile(s):
    return min(256, s)


def _mm(a, b, *, name, tm, tn, tk, ta=False, tb=False, a_parts=1, b_parts=1):
    if a_parts > 1:
        m, kk = a.shape[1], a.shape[2] * a_parts
    elif ta:
        kk, m = a.shape
    else:
        m, kk = a.shape
    if b_parts > 1:
        n = b.shape[2] * b_parts
    else:
        n = b.shape[0] if tb else b.shape[1]
    tm, tn, tk = _tile(m, tm), _tile(n // b_parts, tn), _tile(kk // a_parts, tk)
    k_per, n_per = kk // a_parts // tk, n // b_parts // tn

    def body(a_ref, b_ref, o_ref):
        k = pl.program_id(2)
        if ta:
            part = _dot_tn(a_ref[...], b_ref[...])
        elif tb:
            part = _dot_nt(a_ref[...], b_ref[...])
        else:
            part = _dot(a_ref[...], b_ref[...])

        @pl.when(k == 0)
        def _():
            o_ref[...] = part

        @pl.when(k > 0)
        def _():
            o_ref[...] += part

    if a_parts > 1:
        a_spec = pl.BlockSpec((None, tm, tk), lambda i, j, k: (k // k_per, i, k % k_per))
    else:
        a_spec = pl.BlockSpec((tk, tm), lambda i, j, k: (k, i)) if ta else pl.BlockSpec((tm, tk), lambda i, j, k: (i, k))
    if b_parts > 1:
        b_spec = pl.BlockSpec((None, tk, tn), lambda i, j, k: (j // n_per, k, j % n_per))
    else:
        b_spec = pl.BlockSpec((tn, tk), lambda i, j, k: (j, k)) if tb else pl.BlockSpec((tk, tn), lambda i, j, k: (k, j))
    return _pcall(body, name=name, grid=(m // tm, n // tn, kk // tk), in_specs=[a_spec, b_spec],
                  out_specs=pl.BlockSpec((tm, tn), lambda i, j, k: (i, j)), out_shape=_sds((m, n)),
                  dims=("parallel", "parallel", "arbitrary"))(a, b)


def _in_proj(x, g_mix, w_in_r):
    s = x.shape[0]
    tm = _row_tile(s)

    def body(x_ref, g_ref, w_ref, h_ref, z_ref):
        h = _rms(x_ref[...], g_ref[...]).astype(BF16)
        h_ref[...] = h
        z_ref[...] = _dot(h, w_ref[...])

    return _pcall(body, name="in_proj", grid=(s // tm,),
                  in_specs=[pl.BlockSpec((tm, D_MODEL), lambda i: (i, 0)), pl.BlockSpec((1, D_MODEL), lambda i: (0, 0)),
                            pl.BlockSpec((D_MODEL, Z_COLS), lambda i: (0, 0))],
                  out_specs=[pl.BlockSpec((tm, D_MODEL), lambda i: (i, 0)), pl.BlockSpec((tm, Z_COLS), lambda i: (i, 0))],
                  out_shape=[_sds((s, D_MODEL), BF16), _sds((s, Z_COLS))], dims=("parallel",))(x, g_mix, w_in_r)


def _attn_prep(z, gq, gk, bf, gg):
    s = z.shape[0]
    tm = _row_tile(s)

    def body(z_ref, gq_ref, gk_ref, bf_ref, gg_ref, qn_ref, kn_ref, vb_ref, ub_ref, c_ref, carry_ref):
        i = pl.program_id(0)

        @pl.when(i == 0)
        def _():
            carry_ref[...] = jnp.zeros_like(carry_ref)

        gg_m = gg_ref[...]

        def head_norm(t, g):
            ssq = _dot_exact_r(t * t, gg_m)
            return t * lax.rsqrt(ssq * (1.0 / HEAD_DIM) + EPS) * g

        qn_ref[...] = head_norm(z_ref[:, 0:ATTN_W], gq_ref[...]).astype(BF16)
        kn_ref[...] = head_norm(z_ref[:, ATTN_W:2 * ATTN_W], gk_ref[...]).astype(BF16)
        vb_ref[...] = z_ref[:, 2 * ATTN_W:3 * ATTN_W].astype(BF16)
        ub_ref[...] = z_ref[:, U_COL0:U_COL0 + SSM_W].astype(BF16)
        fl = z_ref[:, F_COL0:F_COL0 + LANES] + bf_ref[...]
        lf = jnp.minimum(fl, 0.0) - jnp.log1p(jnp.exp(-jnp.abs(fl)))
        row = lax.broadcasted_iota(jnp.int32, (tm, tm), 0)
        col = lax.broadcasted_iota(jnp.int32, (tm, tm), 1)
        tri = (row >= col).astype(BF16)
        c = _dot_exact_l(tri, lf) + carry_ref[...]
        c_ref[...] = c
        carry_ref[...] = c[tm - 1:tm, :]

    row_spec = lambda w: pl.BlockSpec((tm, w), lambda i: (i, 0))
    const = lambda shape: pl.BlockSpec(shape, lambda i: (0, 0))
    return _pcall(body, name="attn_prep", grid=(s // tm,),
                  in_specs=[row_spec(Z_COLS), const((1, ATTN_W)), const((1, ATTN_W)), const((1, LANES)), const((ATTN_W, ATTN_W))],
                  out_specs=[row_spec(ATTN_W)] * 4 + [row_spec(LANES)],
                  out_shape=[_sds((s, ATTN_W), BF16)] * 4 + [_sds((s, LANES))],
                  scratch_shapes=[pltpu.VMEM((1, LANES), F32)], dims=("arbitrary",))(z, gq, gk, bf, gg)


def _attn_fwd(qh, kh, vh, crow, hosted=None):
    _, s, _ = qh.shape
    tq = _row_tile(s)
    scale = HEAD_DIM ** -0.5

    hp = HEADS_PER_STEP
    nq = s // tq
    fold = lambda t, op: op(t[:, :tq // 2], t[:, tq // 2:])

    def body(q_ref, k_ref, v_ref, c_ref, o_ref, lse_ref, s_s):
        i = pl.program_id(1)

        def first(j, ms, diagonal):
            off = pl.multiple_of(j * tq, tq)
            out = []
            for hh in range(hp):
                sc = _dot_nt(q_ref[hh], k_ref[hh, pl.ds(off, tq), :]) * scale - c_ref[hh, :, pl.ds(off, tq)]
                if diagonal:
                    causal = lax.broadcasted_iota(jnp.int32, (tq, tq), 1) <= lax.broadcasted_iota(jnp.int32, (tq, tq), 0)
                    sc = jnp.where(causal, sc, NEG_INF)
                s_s[hh, j] = sc
                out.append(jnp.maximum(ms[hh], fold(sc, jnp.maximum)))
            return tuple(out)

        ms = lax.fori_loop(0, i, lambda j, c: first(j, c, False), (jnp.full((tq, tq // 2), NEG_INF, F32),) * hp)
        ms = [jnp.max(t, axis=-1, keepdims=True) for t in first(i, ms, True)]

        def second(j, carry):
            rows = pl.ds(pl.multiple_of(j * tq, tq), tq)
            out = []
            for hh in range(hp):
                ls, acc = carry[hh]
                p = jnp.exp(s_s[hh, j] - ms[hh])
                out.append((ls + fold(p, jnp.add), acc + _dot(p.astype(BF16), v_ref[hh, rows, :])))
            return tuple(out)

        zero = (jnp.zeros((tq, tq // 2), F32), jnp.zeros((tq, HEAD_DIM), F32))
        for hh, (ls, acc) in enumerate(lax.fori_loop(0, i + 1, second, (zero,) * hp)):
            l = jnp.sum(ls, axis=-1, keepdims=True)
            o_ref[hh] = acc / l
            lse_ref[hh] = ms[hh] + jnp.log(l)

    blk = pl.BlockSpec((hp, tq, HEAD_DIM), lambda h, i: (h, i, 0))
    full = pl.BlockSpec((hp, s, HEAD_DIM), lambda h, i: (h, 0, 0))
    nh = HEADS // hp
    first = lambda: jnp.logical_and(pl.program_id(0) == 0, pl.program_id(1) == 0)
    last = lambda: jnp.logical_and(pl.program_id(0) == nh - 1, pl.program_id(1) == nq - 1)
    return _host_pcall(body, hosted, first, last, n_in=4, n_out=2, n_scratch=1, name="attn_fwd", grid=(nh, nq),
                       in_specs=[blk, full, full, pl.BlockSpec((hp, 1, s), lambda h, i: (h, 0, 0))],
                       out_specs=[blk, pl.BlockSpec((hp, tq, 1), lambda h, i: (h, i, 0))],
                       out_shape=[_sds((HEADS, s, HEAD_DIM)), _sds((HEADS, s, 1))],
                       scratch_shapes=[pltpu.VMEM((hp, nq, tq, tq), F32)],
                       dims=("parallel", "parallel"), operands=(qh, kh, vh, crow))


def _ssm_param_fn(lr, li, ls, br, bi):
    step = jnp.exp(ls)
    er = jnp.exp(lr * step)
    ab_re = er * jnp.cos(li * step)
    ab_im = er * jnp.sin(li * step)
    num_re = ab_re - 1.0
    num_im = ab_im
    den = lr * lr + li * li
    f_re = (num_re * lr + num_im * li) / den
    f_im = (num_im * lr - num_re * li) / den
    bb_re = f_re * br - f_im * bi
    bb_im = f_re * bi + f_im * br
    return ab_re, ab_im, bb_re, bb_im


_PARAM_SHAPE = (SSM_GROUPS * SSM_GROUP, SSM_STATE)


def _ssm_params(lr, li, ls, br, bi):
    def body(lr_ref, li_ref, ls_ref, br_ref, bi_ref, ar_ref, ai_ref, bbr_ref, bbi_ref):
        ar, ai, bbr, bbi = _ssm_param_fn(lr_ref[...], li_ref[...], ls_ref[...], br_ref[...], bi_ref[...])
        ar_ref[...] = ar
        ai_ref[...] = ai
        bbr_ref[...] = bbr
        bbi_ref[...] = bbi

    spec = pl.BlockSpec(_PARAM_SHAPE, lambda: (0, 0))
    return _pcall(body, name="ssm_params", in_specs=[spec] * 5, out_specs=[spec] * 4,
                  out_shape=[_sds(_PARAM_SHAPE)] * 4)(lr, li, ls, br, bi)


def _ssm_params_bwd(lr, li, ls, br, bi, dar, dai, dbbr, dbbi, expand_t):
    def body(lr_ref, li_ref, ls_ref, br_ref, bi_ref, dar_ref, dai_ref, dbbr_ref, dbbi_ref, et_ref,
             dlr_ref, dli_ref, dls_ref, dbr_ref, dbi_ref):
        _, vjp = jax.vjp(_ssm_param_fn, lr_ref[...], li_ref[...], ls_ref[...], br_ref[...], bi_ref[...])
        dlr, dli, dls, dbr, dbi = vjp((dar_ref[...], dai_ref[...], dbbr_ref[...], dbbi_ref[...]))
        et = et_ref[...]
        dlr_ref[...] = _dot_exact_l(et, dlr)
        dli_ref[...] = _dot_exact_l(et, dli)
        dls_ref[...] = jnp.sum(_dot_exact_l(et, dls), axis=-1, keepdims=True)
        dbr_ref[...] = dbr
        dbi_ref[...] = dbi

    spec = pl.BlockSpec(_PARAM_SHAPE, lambda: (0, 0))
    gspec = pl.BlockSpec((SSM_GROUPS, SSM_STATE), lambda: (0, 0))
    return _pcall(body, name="ssm_params_bwd",
                  in_specs=[spec] * 9 + [pl.BlockSpec((SSM_GROUPS, _PARAM_SHAPE[0]), lambda: (0, 0))],
                  out_specs=[gspec, gspec, pl.BlockSpec((SSM_GROUPS, 1), lambda: (0, 0)), spec, spec],
                  out_shape=[_sds((SSM_GROUPS, SSM_STATE))] * 2 + [_sds((SSM_GROUPS, 1))] + [_sds(_PARAM_SHAPE)] * 2,
                  )(lr, li, ls, br, bi, dar, dai, dbbr, dbbi, expand_t)


def _cmul(ar, ai, br, bi):
    return ar * br - ai * bi, ar * bi + ai * br


def _scan_consts(ar, ai, width, reverse):
    row = lax.broadcasted_iota(jnp.int32, (SUBLANES, width), 0)
    pw = [(ar, ai)]
    for _ in range(SUBLANES - 1):
        pw.append(_cmul(pw[-1][0], pw[-1][1], ar, ai))
    steps = []
    for d in (1, 2, 4):
        keep = (row < SUBLANES - d) if reverse else (row >= d)
        steps.append((d, jnp.where(keep, pw[d - 1][0], 0.0), jnp.where(keep, pw[d - 1][1], 0.0)))
    pr = jnp.zeros((SUBLANES, width), F32)
    pi = jnp.zeros((SUBLANES, width), F32)
    for r in range(SUBLANES):
        e = (SUBLANES - r) if reverse else (r + 1)
        pr = jnp.where(row == r, pw[e - 1][0], pr)
        pi = jnp.where(row == r, pw[e - 1][1], pi)
    return steps, pr, pi


def _scan_tile(xr, xi, cr, ci, consts, reverse):
    steps, pr, pi = consts
    for d, mr, mi in steps:
        sh = (SUBLANES - d) if reverse else d
        sr = pltpu.roll(xr, sh, 0)
        si = pltpu.roll(xi, sh, 0)
        xr, xi = xr + mr * sr - mi * si, xi + mr * si + mi * sr
    return xr + pr * cr - pi * ci, xi + pr * ci + pi * cr


def _ssm_fwd(ub, z, bbr, bbi, ar, ai, ccr, cci, dsk, hosted=None):
    s = ub.shape[0]
    tm = _row_tile(s)
    nt = tm // SUBLANES

    def body(ub_ref, u_ref, bbr_ref, bbi_ref, ar_ref, ai_ref, ccr_ref, cci_ref, dsk_ref,
             xr_ref, xi_ref, y_ref, cr_s, ci_s):
        i = pl.program_id(1)

        @pl.when(i == 0)
        def _():
            cr_s[...] = jnp.zeros_like(cr_s)
            ci_s[...] = jnp.zeros_like(ci_s)

        u_b = ub_ref[...]
        xr_ref[...] = _dot(u_b, bbr_ref[0])
        xi_ref[...] = _dot(u_b, bbi_ref[0])
        consts = _scan_consts(ar_ref[0], ai_ref[0], CHUNK_S, False)

        def tile(k, carry):
            cr, ci = carry
            sl = pl.ds(pl.multiple_of(k * SUBLANES, SUBLANES), SUBLANES)
            xr, xi = _scan_tile(xr_ref[sl, :], xi_ref[sl, :], cr, ci, consts, False)
            xr_ref[sl, :] = xr
            xi_ref[sl, :] = xi
            return xr[SUBLANES - 1:SUBLANES, :], xi[SUBLANES - 1:SUBLANES, :]

        cr, ci = lax.fori_loop(0, nt, tile, (cr_s[...], ci_s[...]))
        cr_s[...] = cr
        ci_s[...] = ci
        y_ref[...] = (_dot(xr_ref[...].astype(BF16), ccr_ref[0]) - _dot(xi_ref[...].astype(BF16), cci_ref[0])
                      + dsk_ref[...] * u_ref[...])

    ucol0 = U_COL0 // CHUNK_U
    wspec = lambda a, b: pl.BlockSpec((1, a, b), lambda j, i: (j, 0, 0))
    nb = s // tm
    first = lambda: jnp.logical_and(pl.program_id(0) == 0, pl.program_id(1) == 0)
    last = lambda: jnp.logical_and(pl.program_id(0) == SSM_CHUNKS - 1, pl.program_id(1) == nb - 1)
    return _host_pcall(
        body, hosted, first, last, n_in=9, n_out=3, n_scratch=2, name="ssm_fwd", grid=(SSM_CHUNKS, nb),
        in_specs=[pl.BlockSpec((tm, CHUNK_U), lambda j, i: (i, j)),
                  pl.BlockSpec((tm, CHUNK_U), lambda j, i: (i, ucol0 + j)),
                  wspec(CHUNK_U, CHUNK_S), wspec(CHUNK_U, CHUNK_S), wspec(1, CHUNK_S), wspec(1, CHUNK_S),
                  wspec(CHUNK_S, CHUNK_U), wspec(CHUNK_S, CHUNK_U),
                  pl.BlockSpec((1, CHUNK_U), lambda j, i: (0, j))],
        out_specs=[pl.BlockSpec((tm, CHUNK_S), lambda j, i: (i, j)), pl.BlockSpec((tm, CHUNK_S), lambda j, i: (i, j)),
                   pl.BlockSpec((tm, CHUNK_U), lambda j, i: (i, j))],
        out_shape=[_sds((s, N_STATE)), _sds((s, N_STATE)), _sds((s, SSM_W))],
        scratch_shapes=[pltpu.VMEM((1, CHUNK_S), F32)] * 2,
        dims=("parallel", "arbitrary"), operands=(ub, z, bbr, bbi, ar, ai, ccr, cci, dsk))


def _ssm_glu(y, w_glu, b_glu):
    ge = _gelu(y)
    sg = _sigmoid(_dot(ge.astype(BF16), w_glu) + b_glu)
    return ge, sg


def _mix_out(y, att, x, w_glu, b_glu, g_att, g_ssm, w_out, g_ffn):
    s = x.shape[0]
    tm = _row_tile(s)

    def body(y_ref, att_ref, x_ref, wg_ref, bg_ref, ga_ref, gs_ref, wo_ref, gf_ref, x1_ref, mix_ref, h2_ref):
        ge, sg = _ssm_glu(y_ref[...], wg_ref[...], bg_ref[...])
        ms = _rms(ge * sg, gs_ref[...]).astype(BF16)
        ma = _rms(att_ref[...], ga_ref[...]).astype(BF16)
        mix_ref[:, 0:ATTN_W] = ma
        mix_ref[:, ATTN_W:D_MODEL] = ms
        x1 = x_ref[...] + (_dot(ma, wo_ref[0:ATTN_W, :]) + _dot(ms, wo_ref[ATTN_W:D_MODEL, :]))
        x1_ref[...] = x1
        h2_ref[...] = _rms(x1, gf_ref[...]).astype(BF16)

    row = lambda w: pl.BlockSpec((tm, w), lambda i: (i, 0))
    const = lambda a, b: pl.BlockSpec((a, b), lambda i: (0, 0))
    return _pcall(body, name="mix_out", grid=(s // tm,),
                  in_specs=[row(SSM_W), row(ATTN_W), row(D_MODEL), const(SSM_W, SSM_W), const(1, SSM_W), const(1, ATTN_W),
                            const(1, SSM_W), const(D_MODEL, D_MODEL), const(1, D_MODEL)],
                  out_specs=[row(D_MODEL)] * 3,
                  out_shape=[_sds((s, D_MODEL)), _sds((s, D_MODEL), BF16), _sds((s, D_MODEL), BF16)],
                  dims=("parallel",))(y, att, x, w_glu, b_glu, g_att, g_ssm, w_out, g_ffn)


CONV_CHUNK = 64


def _conv_rows(pad_ref, w, b, r0, n):
    y = b + pad_ref[pl.ds(r0 + SUBLANES - 2, n), :] * w[0:1, :]
    y = y + pad_ref[pl.ds(r0 + SUBLANES - 1, n), :] * w[1:2, :]
    return y + pad_ref[pl.ds(r0 + SUBLANES, n), :] * w[2:3, :]


def _fill_front_pad(pad_ref, strip_ref, s):
    pad_ref[0:SUBLANES, :] = jnp.zeros((SUBLANES, STRIP), F32)
    for r0 in range(0, s, CONV_CHUNK):
        pad_ref[pl.ds(SUBLANES + r0, CONV_CHUNK), :] = strip_ref[pl.ds(r0, CONV_CHUNK), :]


def _conv_act(up, conv_w, conv_b):
    s = up.shape[0]

    def body(ug_ref, uv_ref, wg_ref, wv_ref, bg_ref, bv_ref, act_ref, pg_ref, pv_ref):
        _fill_front_pad(pg_ref, ug_ref, s)
        _fill_front_pad(pv_ref, uv_ref, s)
        wg, wv, bg, bv = wg_ref[...], wv_ref[...], bg_ref[...], bv_ref[...]
        for r0 in range(0, s, CONV_CHUNK):
            hg = _conv_rows(pg_ref, wg, bg, r0, CONV_CHUNK)
            hv = _conv_rows(pv_ref, wv, bv, r0, CONV_CHUNK)
            act_ref[pl.ds(r0, CONV_CHUNK), :] = (hg * _sigmoid(hg) * hv).astype(BF16)

    strip = lambda off: pl.BlockSpec((s, STRIP), lambda j: (0, j + off))
    wsp = lambda off: pl.BlockSpec((3, STRIP), lambda j: (0, j + off))
    bsp = lambda off: pl.BlockSpec((1, STRIP), lambda j: (0, j + off))
    return _pcall(body, name="conv_act", grid=(N_STRIPS,),
                  in_specs=[strip(0), strip(N_STRIPS), wsp(0), wsp(N_STRIPS), bsp(0), bsp(N_STRIPS)],
                  out_specs=pl.BlockSpec((s, STRIP), lambda j: (0, j)), out_shape=_sds((s, D_FF), BF16),
                  scratch_shapes=[pltpu.VMEM((s + SUBLANES, STRIP), F32)] * 2,
                  dims=("parallel",))(up, up, conv_w, conv_w, conv_b, conv_b)


def _down_loss(act, w_down, x1, tgt):
    s = x1.shape[0]
    tm = _row_tile(s)

    def body(a_ref, w_ref, x1_ref, t_ref, dy_ref, dyb_ref, loss_ref):
        i = pl.program_id(0)

        @pl.when(i == 0)
        def _():
            loss_ref[...] = jnp.zeros_like(loss_ref)

        diff = x1_ref[...] + _dot(a_ref[...], w_ref[...]) - t_ref[...]
        dy = diff * (1.0 / D_MODEL)
        dy_ref[...] = dy
        dyb_ref[...] = dy.astype(BF16)
        loss_ref[...] += 0.5 * jnp.sum(diff * dy)

    row = lambda w: pl.BlockSpec((tm, w), lambda i: (i, 0))
    return _pcall(body, name="down_loss", grid=(s // tm,),
                  in_specs=[row(D_FF), pl.BlockSpec((D_FF, D_MODEL), lambda i: (0, 0)), row(D_MODEL), row(D_MODEL)],
                  out_specs=[row(D_MODEL), row(D_MODEL), pl.BlockSpec((SUBLANES, LANES), lambda i: (0, 0))],
                  out_shape=[_sds((s, D_MODEL)), _sds((s, D_MODEL), BF16), _sds((SUBLANES, LANES))],
                  dims=("arbitrary",))(act, w_down, x1, tgt)


def _conv_act_bwd(up, dact, conv_w, conv_b):
    s = up.shape[0]
    ch = CONV_CHUNK

    def body(ug_ref, uv_ref, da_ref, wg_ref, wv_ref, bg_ref, bv_ref, dup_ref, dcw_ref, pg_ref, pv_ref, dg_ref, dv_ref):
        _fill_front_pad(pg_ref, ug_ref, s)
        _fill_front_pad(pv_ref, uv_ref, s)
        zero = jnp.zeros((SUBLANES, STRIP), F32)
        dg_ref[pl.ds(s, SUBLANES), :] = zero
        dv_ref[pl.ds(s, SUBLANES), :] = zero
        wg, wv, bg, bv = wg_ref[...], wv_ref[...], bg_ref[...], bv_ref[...]
        tile_sum = lambda t: jnp.sum(t.reshape(ch // SUBLANES, SUBLANES, STRIP), axis=0)
        accs = [[zero] * 4, [zero] * 4]
        for r0 in range(0, s, ch):
            hg = _conv_rows(pg_ref, wg, bg, r0, ch)
            hv = _conv_rows(pv_ref, wv, bv, r0, ch)
            sg = _sigmoid(hg)
            da = da_ref[pl.ds(r0, ch), :]
            dhs = (da * hv * (sg * (1.0 + hg * (1.0 - sg))), da * (hg * sg))
            for half, (dh, d_ref, p_ref) in enumerate(zip(dhs, (dg_ref, dv_ref), (pg_ref, pv_ref))):
                d_ref[pl.ds(r0, ch), :] = dh
                for k in range(3):
                    accs[half][k] = accs[half][k] + tile_sum(dh * p_ref[pl.ds(r0 + SUBLANES - 2 + k, ch), :])
                accs[half][3] = accs[half][3] + tile_sum(dh)
        for half, (d_ref, w) in enumerate(((dg_ref, wg), (dv_ref, wv))):
            for r0 in range(0, s, ch):
                dup = (d_ref[pl.ds(r0, ch), :] * w[2:3, :] + d_ref[pl.ds(r0 + 1, ch), :] * w[1:2, :]
                       + d_ref[pl.ds(r0 + 2, ch), :] * w[0:1, :])
                dup_ref[half, pl.ds(r0, ch), :] = dup.astype(BF16)
            rid = lax.broadcasted_iota(jnp.int32, (SUBLANES, STRIP), 0)
            out = zero
            for k in range(4):
                out = jnp.where(rid == k, jnp.sum(accs[half][k], axis=0, keepdims=True), out)
            dcw_ref[half] = out

    strip = lambda off: pl.BlockSpec((s, STRIP), lambda j: (0, j + off))
    wsp = lambda off: pl.BlockSpec((3, STRIP), lambda j: (0, j + off))
    bsp = lambda off: pl.BlockSpec((1, STRIP), lambda j: (0, j + off))
    return _pcall(body, name="conv_act_bwd", grid=(N_STRIPS,),
                  in_specs=[strip(0), strip(N_STRIPS), strip(0), wsp(0), wsp(N_STRIPS), bsp(0), bsp(N_STRIPS)],
                  out_specs=[pl.BlockSpec((2, s, STRIP), lambda j: (0, 0, j)), pl.BlockSpec((2, SUBLANES, STRIP), lambda j: (0, 0, j))],
                  out_shape=[_sds((2, s, D_FF), BF16), _sds((2, SUBLANES, D_FF))],
                  scratch_shapes=[pltpu.VMEM((s + SUBLANES, STRIP), F32)] * 4,
                  dims=("parallel",))(up, up, dact, conv_w, conv_w, conv_b, conv_b)


def _mix_bwd(dy, dh2, x1, g_ffn, w_out, y, att, w_glu, b_glu, g_att, g_ssm):
    s = dy.shape[0]
    tm = _row_tile(s)

    def body(dy_ref, dh2_ref, x1_ref, gf_ref, wo_ref, y_ref, att_ref, wg_ref, bg_ref, ga_ref, gs_ref,
             dx1_ref, dx1b_ref, datt_ref, dys_ref, dwg_ref, dgf_ref, dga_ref, dgs_ref, dbg_ref):
        i = pl.program_id(0)

        @pl.when(i == 0)
        def _():
            for r in (dwg_ref, dgf_ref, dga_ref, dgs_ref, dbg_ref):
                r[...] = jnp.zeros_like(r)

        dxn, dgf = _rms_bwd(x1_ref[...], gf_ref[...], dh2_ref[...])
        dx1 = dy_ref[...] + dxn
        dx1_ref[...] = dx1
        dx1b = dx1.astype(BF16)
        dx1b_ref[...] = dx1b
        dgf_ref[...] += dgf
        dma = _dot_nt(dx1b, wo_ref[0:ATTN_W, :])
        dms = _dot_nt(dx1b, wo_ref[ATTN_W:D_MODEL, :])
        datt, dga = _rms_bwd(att_ref[...], ga_ref[...], dma)
        datt_ref[...] = datt
        dga_ref[...] += dga
        yv = y_ref[...]
        ge, sg = _ssm_glu(yv, wg_ref[...], bg_ref[...])
        dssm, dgs = _rms_bwd(ge * sg, gs_ref[...], dms)
        dgs_ref[...] += dgs
        dgl = dssm * ge * sg * (1.0 - sg)
        dglb = dgl.astype(BF16)
        dge = dssm * sg + _dot_nt(dglb, wg_ref[...])
        dbg_ref[...] += jnp.sum(dgl, axis=0, keepdims=True)
        dwg_ref[...] += _dot_tn(ge.astype(BF16), dglb)
        dys_ref[...] = dge * _gelu_grad(yv)

    row = lambda w: pl.BlockSpec((tm, w), lambda i: (i, 0))
    const = lambda a, b: pl.BlockSpec((a, b), lambda i: (0, 0))
    return _pcall(body, name="mix_bwd", grid=(s // tm,),
                  in_specs=[row(D_MODEL), row(D_MODEL), row(D_MODEL), const(1, D_MODEL), const(D_MODEL, D_MODEL), row(SSM_W),
                            row(ATTN_W), const(SSM_W, SSM_W), const(1, SSM_W), const(1, ATTN_W), const(1, SSM_W)],
                  out_specs=[row(D_MODEL), row(D_MODEL), row(ATTN_W), row(SSM_W), const(SSM_W, SSM_W), const(1, D_MODEL),
                             const(1, ATTN_W), const(1, SSM_W), const(1, SSM_W)],
                  out_shape=[_sds((s, D_MODEL)), _sds((s, D_MODEL), BF16), _sds((s, ATTN_W)), _sds((s, SSM_W)),
                             _sds((SSM_W, SSM_W)), _sds((1, D_MODEL)), _sds((1, ATTN_W)), _sds((1, SSM_W)), _sds((1, SSM_W))],
                  dims=("arbitrary",))(dy, dh2, x1, g_ffn, w_out, y, att, w_glu, b_glu, g_att, g_ssm)


def _ssm_bwd(dys, z, ub, xr, xi, bbr, bbi, ar, ai, ccr, cci, dsk, hosted=None):
    s = dys.shape[0]
    tm = _row_tile(s)
    nb = s // tm
    nt = tm // SUBLANES

    def body(dy_ref, u_ref, ub_ref, xr_ref, xi_ref, xrp_ref, xip_ref, bbr_ref, bbi_ref, ar_ref, ai_ref, ccr_ref,
             cci_ref, dsk_ref, du_ref, dbbr_ref, dbbi_ref, dccr_ref, dcci_ref, dar_ref, dai_ref, dd_ref,
             gr_s, gi_s, cr_s, ci_s, accr_s, acci_s):
        i = pl.program_id(1)
        first_block = i == nb - 1

        @pl.when(i == 0)
        def _():
            for r in (cr_s, ci_s, accr_s, acci_s, dbbr_ref, dbbi_ref, dccr_ref, dcci_ref, dd_ref):
                r[...] = jnp.zeros_like(r)

        dy = dy_ref[...]
        dyb = dy.astype(BF16)
        gr_s[...] = _dot_nt(dyb, ccr_ref[0])
        gi_s[...] = -_dot_nt(dyb, cci_ref[0])
        consts = _scan_consts(ar_ref[0], -ai_ref[0], CHUNK_S, True)
        row = lax.broadcasted_iota(jnp.int32, (SUBLANES, CHUNK_S), 0)

        def tile(kk, carry):
            cr, ci, accr, acci = carry
            k = nt - 1 - kk
            sl = pl.ds(pl.multiple_of(k * SUBLANES, SUBLANES), SUBLANES)
            gr, gi = _scan_tile(gr_s[sl, :], gi_s[sl, :], cr, ci, consts, True)
            gr_s[sl, :] = gr
            gi_s[sl, :] = gi
            slp = pl.ds(pl.multiple_of(jnp.maximum(k - 1, 0) * SUBLANES, SUBLANES), SUBLANES)
            inner = k > 0
            pr_t = jnp.where(inner, xr_ref[slp, :], xrp_ref[...])
            pi_t = jnp.where(inner, xi_ref[slp, :], xip_ref[...])
            live = jnp.logical_or(inner, jnp.logical_not(first_block))
            top_r = jnp.where(live, pltpu.roll(pr_t, 1, 0), 0.0)
            top_i = jnp.where(live, pltpu.roll(pi_t, 1, 0), 0.0)
            xpr = jnp.where(row == 0, top_r, pltpu.roll(xr_ref[sl, :], 1, 0))
            xpi = jnp.where(row == 0, top_i, pltpu.roll(xi_ref[sl, :], 1, 0))
            accr = accr + gr * xpr + gi * xpi
            acci = acci + gi * xpr - gr * xpi
            return gr[0:1, :], gi[0:1, :], accr, acci

        zeros = jnp.zeros((SUBLANES, CHUNK_S), F32)
        cr, ci, accr, acci = lax.fori_loop(0, nt, tile, (cr_s[...], ci_s[...], zeros, zeros))
        cr_s[...] = cr
        ci_s[...] = ci
        accr_s[...] += accr
        acci_s[...] += acci
        grb = gr_s[...].astype(BF16)
        gib = gi_s[...].astype(BF16)
        u_b = ub_ref[...]
        du_ref[...] = _dot_nt(grb, bbr_ref[0]) + _dot_nt(gib, bbi_ref[0]) + dsk_ref[...] * dy
        dbbr_ref[0] += _dot_tn(u_b, grb)
        dbbi_ref[0] += _dot_tn(u_b, gib)
        dccr_ref[0] += _dot_tn(xr_ref[...].astype(BF16), dyb)
        dcci_ref[0] -= _dot_tn(xi_ref[...].astype(BF16), dyb)
        dd_ref[...] += jnp.sum(dy * u_ref[...], axis=0, keepdims=True)

        @pl.when(i == nb - 1)
        def _():
            dar_ref[0] = jnp.sum(accr_s[...], axis=0, keepdims=True)
            dai_ref[0] = jnp.sum(acci_s[...], axis=0, keepdims=True)

    ucol0 = U_COL0 // CHUNK_U
    tiles_per_block = tm // SUBLANES
    rb = lambda i: nb - 1 - i
    wspec = lambda a, b: pl.BlockSpec((1, a, b), lambda j, i: (j, 0, 0))
    xblk = pl.BlockSpec((tm, CHUNK_S), lambda j, i: (rb(i), j))
    xprev = pl.BlockSpec((SUBLANES, CHUNK_S), lambda j, i: (jnp.maximum(rb(i) * tiles_per_block - 1, 0), j))
    ublk = pl.BlockSpec((tm, CHUNK_U), lambda j, i: (rb(i), j))
    first = lambda: jnp.logical_and(pl.program_id(0) == 0, pl.program_id(1) == 0)
    last = lambda: jnp.logical_and(pl.program_id(0) == SSM_CHUNKS - 1, pl.program_id(1) == nb - 1)
    return _host_pcall(
        body, hosted, first, last, n_in=14, n_out=8, n_scratch=6, name="ssm_bwd", grid=(SSM_CHUNKS, nb),
        in_specs=[ublk, pl.BlockSpec((tm, CHUNK_U), lambda j, i: (rb(i), ucol0 + j)), ublk, xblk, xblk, xprev, xprev,
                  wspec(CHUNK_U, CHUNK_S), wspec(CHUNK_U, CHUNK_S), wspec(1, CHUNK_S), wspec(1, CHUNK_S),
                  wspec(CHUNK_S, CHUNK_U), wspec(CHUNK_S, CHUNK_U), pl.BlockSpec((1, CHUNK_U), lambda j, i: (0, j))],
        out_specs=[ublk, wspec(CHUNK_U, CHUNK_S), wspec(CHUNK_U, CHUNK_S), wspec(CHUNK_S, CHUNK_U),
                   wspec(CHUNK_S, CHUNK_U), wspec(1, CHUNK_S), wspec(1, CHUNK_S),
                   pl.BlockSpec((1, CHUNK_U), lambda j, i: (0, j))],
        out_shape=[_sds((s, SSM_W)), _sds((SSM_CHUNKS, CHUNK_U, CHUNK_S)), _sds((SSM_CHUNKS, CHUNK_U, CHUNK_S)),
                   _sds((SSM_CHUNKS, CHUNK_S, CHUNK_U)), _sds((SSM_CHUNKS, CHUNK_S, CHUNK_U)),
                   _sds((SSM_CHUNKS, 1, CHUNK_S)), _sds((SSM_CHUNKS, 1, CHUNK_S)), _sds((1, SSM_W))],
        scratch_shapes=[pltpu.VMEM((tm, CHUNK_S), F32)] * 2 + [pltpu.VMEM((1, CHUNK_S), F32)] * 2
                       + [pltpu.VMEM((SUBLANES, CHUNK_S), F32)] * 2,
        dims=("parallel", "arbitrary"), operands=(dys, z, ub, xr, xi, xr, xi, bbr, bbi, ar, ai, ccr, cci, dsk))


def _attn_probs(q, ks, cs, lse, scale, diagonal):
    p = jnp.exp(_dot_nt(q, ks) * scale - cs - lse)
    if diagonal:
        tq, tk = p.shape
        causal = lax.broadcasted_iota(jnp.int32, (tq, tk), 1) <= lax.broadcasted_iota(jnp.int32, (tq, tk), 0)
        p = jnp.where(causal, p, 0.0)
    return p


def _attn_bwd(qh, kh, vh, crow, lse, doh, hosted=None):
    _, s, _ = qh.shape
    tq = _row_tile(s)
    nq = s // tq
    scale = HEAD_DIM ** -0.5
    hp = HEADS_PER_STEP

    def body(q_ref, k_ref, v_ref, c_ref, lse_ref, do_ref, dq_ref, dk_ref, dv_ref, dc_ref, p_s, dp_s):
        i = pl.program_id(1)

        @pl.when(i == 0)
        def _():
            for r in (dk_ref, dv_ref, dc_ref):
                r[...] = jnp.zeros_like(r)

        dobs = [do_ref[hh].astype(BF16) for hh in range(hp)]

        def first(j, dls, diagonal):
            off = pl.multiple_of(j * tq, tq)
            out = []
            for hh in range(hp):
                p = _attn_probs(q_ref[hh], k_ref[hh, pl.ds(off, tq), :], c_ref[hh, :, pl.ds(off, tq)], lse_ref[hh],
                                scale, diagonal)
                dp = _dot_nt(dobs[hh], v_ref[hh, pl.ds(off, tq), :])
                p_s[hh, j] = p
                dp_s[hh, j] = dp
                out.append(dls[hh] + jnp.sum(p * dp, axis=-1, keepdims=True))
            return tuple(out)

        zero_col = jnp.zeros((tq, 1), F32)
        dls = lax.fori_loop(0, i, lambda j, c: first(j, c, False), (zero_col,) * hp)
        dls = first(i, dls, True)

        def second(j, dqs):
            rows = pl.ds(pl.multiple_of(j * tq, tq), tq)
            out = []
            for hh in range(hp):
                p = p_s[hh, j]
                ds = p * (dp_s[hh, j] - dls[hh])
                dsb = ds.astype(BF16)
                dv_ref[hh, rows, :] += _dot_tn(p.astype(BF16), dobs[hh])
                dk_ref[hh, rows, :] += _dot_tn(dsb, q_ref[hh]) * scale
                dc_ref[hh, :, rows] -= jnp.sum(ds, axis=0, keepdims=True)
                out.append(dqs[hh] + _dot(dsb, k_ref[hh, rows, :]))
            return tuple(out)

        dqs = lax.fori_loop(0, i + 1, second, (jnp.zeros((tq, HEAD_DIM), F32),) * hp)
        for hh in range(hp):
            dq_ref[hh] = dqs[hh] * scale

    blk = pl.BlockSpec((hp, tq, HEAD_DIM), lambda h, i: (h, i, 0))
    full = pl.BlockSpec((hp, s, HEAD_DIM), lambda h, i: (h, 0, 0))
    crow_spec = pl.BlockSpec((hp, 1, s), lambda h, i: (h, 0, 0))
    nh = HEADS // hp
    first = lambda: jnp.logical_and(pl.program_id(0) == 0, pl.program_id(1) == 0)
    last = lambda: jnp.logical_and(pl.program_id(0) == nh - 1, pl.program_id(1) == nq - 1)
    return _host_pcall(body, hosted, first, last, n_in=6, n_out=4, n_scratch=2, name="attn_bwd", grid=(nh, nq),
                       in_specs=[blk, full, full, crow_spec, pl.BlockSpec((hp, tq, 1), lambda h, i: (h, i, 0)), blk],
                       out_specs=[blk, full, full, crow_spec],
                       out_shape=[_sds((HEADS, s, HEAD_DIM))] * 3 + [_sds((HEADS, 1, s))],
                       scratch_shapes=[pltpu.VMEM((hp, nq, tq, tq), F32)] * 2,
                       dims=("parallel", "arbitrary"), operands=(qh, kh, vh, crow, lse, doh))


def _prep_bwd(z, dqn, dkn, dv, du, dc, gq, gk, bf, gg):
    s = z.shape[0]
    tm = _row_tile(s)
    nb = s // tm

    def body(z_ref, dqn_ref, dkn_ref, dv_ref, du_ref, dc_ref, gq_ref, gk_ref, bf_ref, gg_ref,
             dz_ref, dgq_ref, dgk_ref, dbf_ref, carry_ref):
        i = pl.program_id(0)

        @pl.when(i == 0)
        def _():
            for r in (dgq_ref, dgk_ref, dbf_ref, carry_ref):
                r[...] = jnp.zeros_like(r)

        gg_m = gg_ref[...]

        def head_norm_bwd(t, g, dn):
            r = lax.rsqrt(_dot_exact_r(t * t, gg_m) * (1.0 / HEAD_DIM) + EPS)
            w = dn * g
            mean_wt = _dot_exact_r(w * t, gg_m) * (1.0 / HEAD_DIM)
            return r * w - t * (r * r * r) * mean_wt, jnp.sum(dn * t * r, axis=0, keepdims=True)

        dq, dgq = head_norm_bwd(z_ref[:, 0:ATTN_W], gq_ref[...], dqn_ref[...])
        dk, dgk = head_norm_bwd(z_ref[:, ATTN_W:2 * ATTN_W], gk_ref[...], dkn_ref[...])
        dgq_ref[...] += dgq
        dgk_ref[...] += dgk
        row = lax.broadcasted_iota(jnp.int32, (tm, tm), 0)
        col = lax.broadcasted_iota(jnp.int32, (tm, tm), 1)
        triu = (col >= row).astype(BF16)
        dlf = _dot_exact_l(triu, dc_ref[...]) + carry_ref[...]
        carry_ref[...] = dlf[0:1, :]
        fl = z_ref[:, F_COL0:F_COL0 + LANES] + bf_ref[...]
        df = dlf * _sigmoid(-fl)
        dbf_ref[...] += jnp.sum(df, axis=0, keepdims=True)
        dz_ref[:, 0:ATTN_W] = dq.astype(BF16)
        dz_ref[:, ATTN_W:2 * ATTN_W] = dk.astype(BF16)
        dz_ref[:, 2 * ATTN_W:3 * ATTN_W] = dv_ref[...].astype(BF16)
        dz_ref[:, U_COL0:U_COL0 + SSM_W] = du_ref[...].astype(BF16)
        dz_ref[:, F_COL0:F_COL0 + LANES] = df.astype(BF16)

    row_spec = lambda w: pl.BlockSpec((tm, w), lambda i: (nb - 1 - i, 0))
    const = lambda shape: pl.BlockSpec(shape, lambda i: (0, 0))
    return _pcall(body, name="prep_bwd", grid=(nb,),
                  in_specs=[row_spec(Z_COLS)] + [row_spec(ATTN_W)] * 4 + [row_spec(LANES), const((1, ATTN_W)),
                            const((1, ATTN_W)), const((1, LANES)), const((ATTN_W, ATTN_W))],
                  out_specs=[row_spec(Z_COLS), const((1, ATTN_W)), const((1, ATTN_W)), const((1, LANES))],
                  out_shape=[_sds((s, Z_COLS), BF16), _sds((1, ATTN_W)), _sds((1, ATTN_W)), _sds((1, LANES))],
                  scratch_shapes=[pltpu.VMEM((1, LANES), F32)], dims=("arbitrary",))(z, dqn, dkn, dv, du, dc, gq, gk, bf, gg)


def _in_norm_bwd(x, g_mix, dh, dx1):
    s = x.shape[0]
    tm = _row_tile(s)

    def body(x_ref, g_ref, dh_ref, dx1_ref, dx_ref, dg_ref):
        i = pl.program_id(0)

        @pl.when(i == 0)
        def _():
            dg_ref[...] = jnp.zeros_like(dg_ref)

        dxn, dg = _rms_bwd(x_ref[...], g_ref[...], dh_ref[...])
        dx_ref[...] = dx1_ref[...] + dxn
        dg_ref[...] += dg

    row = pl.BlockSpec((tm, D_MODEL), lambda i: (i, 0))
    vec = pl.BlockSpec((1, D_MODEL), lambda i: (0, 0))
    return _pcall(body, name="in_norm_bwd", grid=(s // tm,), in_specs=[row, vec, row, row], out_specs=[row, vec],
                  out_shape=[_sds((s, D_MODEL)), _sds((1, D_MODEL))], dims=("arbitrary",))(x, g_mix, dh, dx1)


def _adamw_refs(w_ref, g_ref, m_ref, v_ref, d_ref, mo_ref, vo_ref):
    gv = g_ref[...]
    mn = ADAM_B1 * m_ref[...] + (1.0 - ADAM_B1) * gv
    vn = ADAM_B2 * v_ref[...] + (1.0 - ADAM_B2) * (gv * gv)
    m_hat = mn / (1.0 - ADAM_B1 ** ADAM_STEP)
    v_hat = vn / (1.0 - ADAM_B2 ** ADAM_STEP)
    d_ref[...] = -ADAM_LR * (m_hat / (jnp.sqrt(v_hat) + ADAM_EPS) + ADAM_WD * w_ref[...])
    mo_ref[...] = mn
    vo_ref[...] = vn


def _adamw_small(ws, gs, ms, vs):
    n = len(ws)

    def body(*refs):
        ins, outs = refs[:4 * n], refs[4 * n:]
        for i in range(n):
            _adamw_refs(ins[i], ins[n + i], ins[2 * n + i], ins[3 * n + i], *outs[3 * i:3 * i + 3])

    vm = pl.BlockSpec(memory_space=pltpu.VMEM)
    out_shape = [_sds(w.shape) for w in ws for _ in range(3)]
    return _pallas(body, name="adamw_small", in_specs=[vm] * (4 * n), out_specs=[vm] * (3 * n), out_shape=out_shape,
                   compiler_params=pltpu.CompilerParams(vmem_limit_bytes=VMEM_LIMIT))(*ws, *gs, *ms, *vs)


def _adamw(w, g, m, v, *, name):
    r, c = w.shape
    tr = r
    for cand in (256, 176, 128, 64):
        if r > cand and r % cand == 0:
            tr = cand
            break

    def body(w_ref, g_ref, m_ref, v_ref, d_ref, mo_ref, vo_ref):
        _adamw_refs(w_ref, g_ref, m_ref, v_ref, d_ref, mo_ref, vo_ref)

    spec = pl.BlockSpec((tr, c), lambda i: (i, 0))
    return _pcall(body, name=name, grid=(r // tr,), in_specs=[spec] * 4, out_specs=[spec] * 3,
                  out_shape=[_sds((r, c))] * 3, dims=("parallel",))(w, g, m, v)


def _prefetch_call(body, *, name, grid, in_specs, out_specs, out_shape, operands):
    grid_spec = pltpu.PrefetchScalarGridSpec(num_scalar_prefetch=1, grid=grid, in_specs=in_specs, out_specs=out_specs)
    params = pltpu.CompilerParams(dimension_semantics=("parallel",) * len(grid), vmem_limit_bytes=VMEM_LIMIT)
    return _pallas(body, name=name, grid_spec=grid_spec, out_shape=out_shape, compiler_params=params)(*operands)


def _half_rows_tile(hr):
    return hr if hr <= 256 else 176 if hr % 176 == 0 else 256


def _add_half(g, landed, place, *, name):
    def body(place_ref, g_ref, l_ref, o_ref):
        own = g_ref[0] if len(g_ref.shape) == 4 else g_ref[...]
        o_ref[...] = (own + l_ref[...]).astype(BF16)

    if g.ndim == 4:
        _, _, hr, c = g.shape
        tr = _half_rows_tile(hr)
        blk = (1, tr, c)
        return _prefetch_call(
            body, name=name, grid=(N_CHIPS, hr // tr),
            in_specs=[pl.BlockSpec((1,) + blk, lambda j, i, p: (j, p[1], i, 0)), pl.BlockSpec(blk, lambda j, i, p: (j, i, 0))],
            out_specs=pl.BlockSpec(blk, lambda j, i, p: (j, i, 0)), out_shape=_sds(landed.shape, BF16),
            operands=(place, g, landed))
    hr, c = landed.shape
    tr, tc = 256, _tile(c, 2176)
    nb = hr // tr
    return _prefetch_call(
        body, name=name, grid=(nb, c // tc),
        in_specs=[pl.BlockSpec((tr, tc), lambda i, j, p: (p[1] * nb + i, j)), pl.BlockSpec((tr, tc), lambda i, j, p: (i, j))],
        out_specs=pl.BlockSpec((tr, tc), lambda i, j, p: (i, j)), out_shape=_sds(landed.shape, BF16),
        operands=(place, g, landed))


def _sum_chips(chip_sum, lands, place, *, name, tc, window_stride=0):
    _, hr, c = lands.shape
    tr = _half_rows_tile(hr)
    nb = hr // tr
    ncb = c // tc

    def body(place_ref, own_ref, a_ref, b_ref, c_ref, o_ref):
        own = own_ref[0] if len(own_ref.shape) == 3 else own_ref[...]
        o_ref[...] = ((own.astype(F32) + a_ref[0].astype(F32)) + b_ref[0].astype(F32)) + c_ref[0].astype(F32)

    land = lambda k: pl.BlockSpec((1, tr, tc), lambda i, j, p: ((p[0] + k) % N_CHIPS, i, j))
    if chip_sum.ndim == 3:
        own_spec = land(0)
    else:
        stride = window_stride // tc
        own_spec = pl.BlockSpec((tr, tc), lambda i, j, p: (i, p[0] * stride + j))
    return _prefetch_call(
        body, name=name, grid=(nb, ncb), in_specs=[own_spec, land(1), land(2), land(3)],
        out_specs=pl.BlockSpec((tr, tc), lambda i, j, p: (p[1] * nb + i, j)), out_shape=_sds((2 * hr, c)),
        operands=(place, chip_sum, lands, lands, lands))


_HBM = pl.BlockSpec(memory_space=pltpu.HBM)


def _place():
    x, y, c = lax.axis_index("x"), lax.axis_index("y"), lax.axis_index("c")
    chips = [(1 - x, y), (x, 1 - y), (1 - x, 1 - y)]
    return x, y, c, chips


def _rcopy(src, dst, send_sem, recv_sem, to):
    return pltpu.make_async_remote_copy(src_ref=src, dst_ref=dst, send_sem=send_sem, recv_sem=recv_sem,
                                        device_id=to, device_id_type=MESH)


N_BIG = 5
UP_COLS = 2 * D_FF // N_CHIPS
IN_WINDOW = 640
IN_STRIDE = 512


class _Hosted:
    def __init__(self, operands, out_shapes, n_sems, start, finish, aliases=None, local_sems=0):
        self.operands, self.out_shapes, self.n_sems = list(operands), list(out_shapes), n_sems
        self.start, self.finish, self.aliases, self.local_sems = start, finish, dict(aliases or {}), local_sems

    def scratch(self):
        return ([pltpu.SemaphoreType.DMA((self.n_sems,)), pltpu.SemaphoreType.DMA((self.n_sems,))]
                + [pltpu.SemaphoreType.DMA] * self.local_sems)


def _run_hosted(hosted, *, name):
    n_in, n_out = len(hosted.operands), len(hosted.out_shapes)

    def body(*refs):
        parts = (refs[:n_in], refs[n_in:n_in + n_out], refs[n_in + n_out:])
        hosted.start(*parts)
        hosted.finish(*parts)

    return _pallas(body, name=name, in_specs=[_HBM] * n_in, out_specs=[_HBM] * n_out, out_shape=hosted.out_shapes,
                   input_output_aliases=hosted.aliases, scratch_shapes=hosted.scratch())(*hosted.operands)


def _host_pcall(core_body, hosted, first, last, *, n_in, n_out, n_scratch, name, grid, in_specs, out_specs, out_shape,
                scratch_shapes, dims, operands):
    if hosted is None:
        outs = _pcall(core_body, name=name, grid=grid, in_specs=in_specs, out_specs=out_specs, out_shape=out_shape,
                      scratch_shapes=scratch_shapes, dims=dims)(*operands)
        return outs, []
    hi, ho = len(hosted.operands), len(hosted.out_shapes)

    def body(*refs):
        a, b = n_in, n_in + hi
        c, d = b + n_out, b + n_out + ho
        e = d + n_scratch
        parts = (refs[a:b], refs[c:d], refs[e:])

        @pl.when(first())
        def _():
            hosted.start(*parts)

        core_body(*refs[:a], *refs[b:c], *refs[d:e])

        @pl.when(last())
        def _():
            hosted.finish(*parts)

    params = pltpu.CompilerParams(dimension_semantics=("arbitrary",) * len(grid), vmem_limit_bytes=VMEM_LIMIT)
    outs = _pallas(body, name=name, grid=grid, in_specs=list(in_specs) + [_HBM] * hi, out_specs=list(out_specs) + [_HBM] * ho,
                   out_shape=list(out_shape) + hosted.out_shapes, scratch_shapes=list(scratch_shapes) + hosted.scratch(),
                   input_output_aliases={n_in + a: n_out + b for a, b in hosted.aliases.items()},
                   compiler_params=params)(*operands, *hosted.operands)
    return outs[:n_out], outs[n_out:]


def _gather_slot(src, out, chip, hc):
    hr, cols = src.shape[0] // 2, src.shape[1]
    if len(out.shape) == 2:
        return out.at[pl.ds(hc * hr, hr), pl.ds(pl.multiple_of(chip * cols, LANES), cols)]
    return out.at[chip, pl.ds(hc * hr, hr), :]


def _gathered_shape(shard, by_cols):
    if by_cols:
        return _sds((shard.shape[0], N_CHIPS * shard.shape[1]), shard.dtype)
    return _sds((N_CHIPS,) + shard.shape, shard.dtype)


def _plan_gather_ici(shards, by_cols, whole=(), own_cols=()):
    n = len(shards)

    def copies(ins, outs, sems):
        send_sems, recv_sems = sems[0], sems[1]
        x, y, c, chips = _place()
        me = 2 * x + y
        sends, waits = [], []
        for w in range(n + len(whole)):
            for k, (cx, cy) in enumerate(chips):
                sem = (send_sems.at[3 * w + k], recv_sems.at[3 * w + k])
                if w < n:
                    hr = ins[w].shape[0] // 2
                    sends.append(_rcopy(ins[w].at[pl.ds(c * hr, hr), :], _gather_slot(ins[w], outs[w], me, c), *sem, (cx, cy, c)))
                    landed = _gather_slot(ins[w], outs[w], 2 * cx + cy, c)
                else:
                    sends.append(_rcopy(ins[w], outs[w].at[me], *sem, (cx, cy, c)))
                    landed = outs[w].at[2 * cx + cy]
                waits.append(_rcopy(landed, landed, *sem, (cx, cy, c)))
        local = [pltpu.make_async_copy(
            ins[w], outs[w].at[:, pl.ds(pl.multiple_of(me * ins[w].shape[1], LANES), ins[w].shape[1])], sems[2 + i])
            for i, w in enumerate(own_cols)]
        return sends, waits, local

    def start(ins, outs, sems):
        sends, _, local = copies(ins, outs, sems)
        for cp in local + sends:
            cp.start()

    def finish(ins, outs, sems):
        sends, waits, local = copies(ins, outs, sems)
        for cp in waits:
            cp.wait_recv()
        for cp in sends:
            cp.wait_send()
        for cp in local:
            cp.wait()

    out_shapes = [_gathered_shape(s, bc) for s, bc in zip(shards, by_cols)] + [_sds((N_CHIPS,) + a.shape, a.dtype) for a in whole]
    return _Hosted(list(shards) + list(whole), out_shapes, 3 * (n + len(whole)), start, finish, local_sems=len(own_cols))


def _plan_gather_d2d(bufs, shard_shapes):
    n = len(bufs)

    def copies(ins, outs, sems):
        send_sems, recv_sems = sems
        x, y, c, chips = _place()
        sibling = (x, y, 1 - c)
        sends, waits = [], []
        for w in range(n):
            for k, (cx, cy) in enumerate(chips):
                sem = (send_sems.at[3 * w + k], recv_sems.at[3 * w + k])
                landed = _gather_slot(shard_shapes[w], outs[w], 2 * cx + cy, c)
                other = _gather_slot(shard_shapes[w], outs[w], 2 * cx + cy, 1 - c)
                sends.append(_rcopy(landed, landed, *sem, sibling))
                waits.append(_rcopy(other, other, *sem, sibling))
        return sends, waits

    def start(ins, outs, sems):
        for cp in copies(ins, outs, sems)[0]:
            cp.start()

    def finish(ins, outs, sems):
        sends, waits = copies(ins, outs, sems)
        for cp in waits:
            cp.wait_recv()
        for cp in sends:
            cp.wait_send()

    return _Hosted(bufs, [_sds(b.shape, b.dtype) for b in bufs], 3 * n, start, finish, aliases={w: w for w in range(n)})


def _plan_swap(grads):
    def copies(ins, outs, sems):
        send_sems, recv_sems = sems
        x, y, c, _ = _place()
        cps = []
        for w, g_ref in enumerate(ins):
            if len(g_ref.shape) == 4:
                theirs = g_ref.at[:, 1 - c]
            else:
                hr = g_ref.shape[0] // 2
                theirs = g_ref.at[pl.ds((1 - c) * hr, hr), :]
            cps.append(_rcopy(theirs, outs[w], send_sems.at[w], recv_sems.at[w], (x, y, 1 - c)))
        return cps

    def start(ins, outs, sems):
        for cp in copies(ins, outs, sems):
            cp.start()

    def finish(ins, outs, sems):
        for cp in copies(ins, outs, sems):
            cp.wait()

    out_shapes = [_sds((g.shape[0], g.shape[2], g.shape[3])) if g.ndim == 4 else _sds((g.shape[0] // 2, g.shape[1]))
                  for g in grads]
    return _Hosted(grads, out_shapes, len(grads), start, finish)


def _plan_scatter(chip_sums, windows):
    def copies(ins, outs, sems):
        send_sems, recv_sems = sems
        x, y, c, chips = _place()
        me = 2 * x + y
        sends, waits = [], []
        for w, s_ref in enumerate(ins):
            for k, (cx, cy) in enumerate(chips):
                tgt = 2 * cx + cy
                if windows[w] is not None:
                    stride, width = windows[w]
                    part = s_ref.at[:, pl.ds(pl.multiple_of(tgt * stride, LANES), width)]
                else:
                    part = s_ref.at[tgt]
                sem = (send_sems.at[3 * w + k], recv_sems.at[3 * w + k])
                sends.append(_rcopy(part, outs[w].at[me], *sem, (cx, cy, c)))
                slot = outs[w].at[tgt]
                waits.append(_rcopy(slot, slot, *sem, (cx, cy, c)))
        return sends, waits

    def start(ins, outs, sems):
        for cp in copies(ins, outs, sems)[0]:
            cp.start()

    def finish(ins, outs, sems):
        sends, waits = copies(ins, outs, sems)
        for cp in waits:
            cp.wait_recv()
        for cp in sends:
            cp.wait_send()

    out_shapes = [_sds((N_CHIPS, s.shape[0], win[1]), BF16) if win is not None else _sds(s.shape, BF16)
                  for s, win in zip(chip_sums, windows)]
    return _Hosted(chip_sums, out_shapes, 3 * len(chip_sums), start, finish)


def _plan_join(reds):
    def copies(ins, outs, sems):
        send_sems, recv_sems = sems
        x, y, c, _ = _place()
        sends, waits = [], []
        for w, out in enumerate(outs):
            hr = out.shape[0] // 2
            mine = out.at[pl.ds(c * hr, hr), :]
            theirs = out.at[pl.ds((1 - c) * hr, hr), :]
            sends.append(_rcopy(mine, mine, send_sems.at[w], recv_sems.at[w], (x, y, 1 - c)))
            waits.append(_rcopy(theirs, theirs, send_sems.at[w], recv_sems.at[w], (x, y, 1 - c)))
        return sends, waits

    def start(ins, outs, sems):
        for cp in copies(ins, outs, sems)[0]:
            cp.start()

    def finish(ins, outs, sems):
        sends, waits = copies(ins, outs, sems)
        for cp in waits:
            cp.wait_recv()
        for cp in sends:
            cp.wait_send()

    return _Hosted(reds, [_sds(r.shape) for r in reds], len(reds), start, finish, aliases={w: w for w in range(len(reds))})


def _allreduce_small(v):
    m_per = v.shape[0]

    def body(v_ref, out_ref, all_ref, send_sems, recv_sems, local_sem):
        x, y, c, chips = _place()
        me, sibling = (x, y, c), (x, y, 1 - c)

        def rows(px, py, pc):
            return all_ref.at[pl.ds((4 * px + 2 * py + pc) * m_per, m_per), :]

        def copy(k, block, to, src=None):
            return _rcopy(rows(*block) if src is None else src, rows(*block), send_sems.at[k], recv_sems.at[k], to)

        mine = pltpu.make_async_copy(v_ref, rows(*me), local_sem)
        mine.start()
        first = [copy(0, me, sibling, src=v_ref)]
        first += [copy(1 + k, me, (*chip, c), src=v_ref) for k, chip in enumerate(chips)]
        for cp in first:
            cp.start()
        passed = [copy(4 + k, (*chip, c), sibling) for k, chip in enumerate(chips)]
        for k, chip in enumerate(chips):
            copy(1 + k, (*chip, c), me).wait_recv()
            passed[k].start()
        copy(0, sibling, me).wait_recv()
        for k, chip in enumerate(chips):
            copy(4 + k, (*chip, 1 - c), me).wait_recv()
        for cp in first + passed:
            cp.wait_send()
        mine.wait()
        acc = all_ref[pl.ds(0, m_per), :]
        for d in range(1, 8):
            acc = acc + all_ref[pl.ds(d * m_per, m_per), :]
        out_ref[...] = acc

    vm = pl.BlockSpec(memory_space=pltpu.VMEM)
    return _pallas(body, name="allreduce_small", in_specs=[vm], out_specs=vm, out_shape=_sds((m_per, LANES)),
                          scratch_shapes=[pltpu.VMEM((8 * m_per, LANES), F32), pltpu.SemaphoreType.DMA((7,)),
                                          pltpu.SemaphoreType.DMA((7,)), pltpu.SemaphoreType.DMA],
                          compiler_params=pltpu.CompilerParams(vmem_limit_bytes=VMEM_LIMIT))(v)


def _to_heads(t):
    s = t.shape[0]
    return t.reshape(s, HEADS, HEAD_DIM).transpose(1, 0, 2)


def _from_heads(t):
    s = t.shape[1]
    return t.transpose(1, 0, 2).reshape(s, HEADS * HEAD_DIM)


def _reorder_in_cols(w):
    pad = jnp.zeros((w.shape[0], Z_COLS - IN_COLS), w.dtype)
    return jnp.concatenate([w[:, :3 * ATTN_W], w[:, 3 * ATTN_W + HEADS:], w[:, 3 * ATTN_W:3 * ATTN_W + HEADS], pad], axis=1)


def _restore_in_cols(w):
    return jnp.concatenate([w[:, :3 * ATTN_W], w[:, F_COL0:F_COL0 + HEADS], w[:, U_COL0:U_COL0 + SSM_W]], axis=1)


def _block_diag(blocks):
    j, g, a, b = blocks.shape
    eye = jnp.eye(g, dtype=bool)[None, :, None, :, None]
    return jnp.where(eye, blocks[:, :, :, None, :], jnp.zeros((), blocks.dtype)).reshape(j, g * a, g * b)


def _diag_blocks(m, a, b):
    j = m.shape[0]
    g = m.shape[1] // a
    t = m.reshape(j, g, a, g, b)
    eye = jnp.eye(g, dtype=bool)[None, :, None, :, None]
    return jnp.sum(jnp.where(eye, t, 0.0), axis=3)


def _pack_rows(parts, rows, dtype):
    used = sum(p.shape[0] for p in parts)
    return jnp.concatenate([p.astype(dtype) for p in parts] + [jnp.zeros((rows - used, D_MODEL), dtype)], axis=0)


_SMALL = (("g_mix", (1024,)), ("b_f", (8,)), ("g_q", (64,)), ("g_k", (64,)), ("lambda_re", (32, 64)),
          ("lambda_im", (32, 64)), ("log_step", (32,)), ("b_re", (32, 64, 16)), ("b_im", (32, 64, 16)),
          ("c_re", (32, 16, 64)), ("c_im", (32, 16, 64)), ("d_skip", (32, 16)), ("b_glu", (512,)),
          ("g_attn_out", (512,)), ("g_ssm_out", (512,)), ("g_ffn", (1024,)), ("conv_b", (5632,)))


def _small_rows(shape):
    return -(-math.prod(shape) // LANES)


def _pack_small(arrs, extra=()):
    parts = []
    for a in list(arrs) + list(extra):
        flat = a.reshape(-1)
        rows = -(-flat.shape[0] // LANES)
        parts.append(jnp.pad(flat, (0, rows * LANES - flat.shape[0])).reshape(rows, LANES))
    total = sum(p.shape[0] for p in parts)
    pad = -total % SUBLANES
    if pad:
        parts.append(jnp.zeros((pad, LANES), F32))
    return jnp.concatenate(parts, axis=0)


def _unpack_small(buf, shapes):
    out, r = [], 0
    for shape in shapes:
        n = math.prod(shape)
        rows = -(-n // LANES)
        out.append(buf[r:r + rows].reshape(-1)[:n].reshape(shape))
        r += rows
    return out


def _halves(t):
    return t.reshape(N_CHIPS, 2, t.shape[0] // (2 * N_CHIPS), t.shape[1])


class _MeshComm:
    def __init__(self, args):
        x, y, self.core = lax.axis_index("x"), lax.axis_index("y"), lax.axis_index("c")
        self.chip = 2 * x + y
        self.place = jnp.stack([self.chip, self.core]).astype(jnp.int32)
        self.shards = {n: args[n].astype(BF16) for n in ("w_in", "w_glu", "w_out", "w_up", "w_down")}
        self.conv_w = args["conv_w"]

    def _own(self, stacked, mine):
        return lax.dynamic_update_slice(stacked, mine[None], (self.chip,) + (0,) * mine.ndim)

    def w_in(self):
        sh = self.shards["w_in"]
        (buf,) = _run_hosted(_plan_gather_ici([sh], [False]), name="gather_w_in")
        (buf,) = _run_hosted(_plan_gather_d2d([buf], [sh]), name="pass_w_in")
        return _reorder_in_cols(self._own(buf, sh).transpose(1, 0, 2).reshape(D_MODEL, IN_COLS))

    def gather_ici(self):
        self.rest = [self.shards[n] for n in ("w_glu", "w_out", "w_up", "w_down")]
        return _plan_gather_ici(self.rest, [False, False, True, False], whole=[self.conv_w], own_cols=[2])

    def gather_d2d(self, landed):
        self.g_cw = landed[4]
        return _plan_gather_d2d(list(landed[:4]), self.rest)

    def weights(self, gathered):
        g_glu, g_out, w_up_b, g_down = gathered
        own = self._own
        return (own(g_glu, self.rest[0]).reshape(SSM_W, SSM_W), own(g_out, self.rest[1]).reshape(D_MODEL, D_MODEL), w_up_b,
                own(g_down, self.rest[3]).reshape(D_FF, D_MODEL),
                own(self.g_cw, self.conv_w).transpose(1, 0, 2).reshape(3, 2 * D_FF))

    def swap(self, d_w_down, d_w_up):
        self.early = [_halves(d_w_down), d_w_up]
        return _plan_swap(self.early)

    def scatter(self, landed):
        self.early_sums = [_add_half(g, l, self.place, name="add_" + n)
                           for g, l, n in zip(self.early, landed, ("w_down", "w_up"))]
        return _plan_scatter(self.early_sums, [None, (UP_COLS, UP_COLS)])

    def reduce(self, early_lands, d_w_in, d_w_glu, d_w_out):
        d_in = jnp.pad(d_w_in, ((0, 0), (0, Z_COLS - IN_COLS)))
        late = [d_in, _halves(d_w_glu), _halves(d_w_out)]
        landed = _run_hosted(_plan_swap(late), name="swap_halves")
        sums = [_add_half(g, l, self.place, name="add_" + n) for g, l, n in zip(late, landed, ("w_in", "w_glu", "w_out"))]
        lands = _run_hosted(_plan_scatter(sums, [(IN_STRIDE, IN_WINDOW), None, None]), name="scatter_chips")
        todo = [(sums[0], lands[0], "w_in", LANES, IN_STRIDE), (sums[1], lands[1], "w_glu", SSM_W, 0),
                (sums[2], lands[2], "w_out", D_MODEL, 0), (self.early_sums[1], early_lands[1], "w_up", UP_COLS, UP_COLS),
                (self.early_sums[0], early_lands[0], "w_down", D_MODEL, 0)]
        reds = _run_hosted(_plan_join([_sum_chips(s, l, self.place, name="sum_" + n, tc=tc, window_stride=st)
                                       for s, l, n, tc, st in todo]), name="join_halves")
        g_big = dict(zip(("w_in", "w_glu", "w_out", "w_up", "w_down"), reds))
        g_big["w_in"] = lax.dynamic_slice_in_dim(reds[0], 2 * self.chip, IN_COLS // N_CHIPS, axis=1)
        return g_big


def _local_step(x, tgt, p, comm):
    s = x.shape[0]
    row = lambda v: v.reshape(1, -1)
    g_mix, g_ffn = row(p["g_mix"]), row(p["g_ffn"])
    g_att, g_ssm, b_glu, conv_b = row(p["g_attn_out"]), row(p["g_ssm_out"]), row(p["b_glu"]), row(p["conv_b"])
    gq = row(jnp.tile(p["g_q"], HEADS))
    gk = row(jnp.tile(p["g_k"], HEADS))
    bf = row(jnp.pad(p["b_f"], (0, LANES - HEADS)))
    gg = jnp.kron(jnp.eye(HEADS, dtype=F32), jnp.ones((HEAD_DIM, HEAD_DIM), F32)).astype(BF16)
    dsk = row(p["d_skip"])

    rep = lambda a: jnp.repeat(a, SSM_GROUP, axis=0)
    lr, li = rep(p["lambda_re"]), rep(p["lambda_im"])
    ls = rep(jnp.broadcast_to(p["log_step"][:, None], (SSM_GROUPS, SSM_STATE)))
    bt_re = p["b_re"].transpose(0, 2, 1).reshape(_PARAM_SHAPE)
    bt_im = p["b_im"].transpose(0, 2, 1).reshape(_PARAM_SHAPE)
    a_re_rep, a_im_rep, bb_re, bb_im = _ssm_params(lr, li, ls, bt_re, bt_im)
    ar = a_re_rep[::SSM_GROUP].reshape(SSM_CHUNKS, 1, CHUNK_S)
    ai = a_im_rep[::SSM_GROUP].reshape(SSM_CHUNKS, 1, CHUNK_S)
    chunked = lambda t: t.reshape(SSM_CHUNKS, SSM_GROUPS // SSM_CHUNKS, SSM_GROUP, SSM_STATE)
    bbr = _block_diag(chunked(bb_re)).astype(BF16)
    bbi = _block_diag(chunked(bb_im)).astype(BF16)
    to_cc = lambda c: _block_diag(chunked(c).transpose(0, 1, 3, 2)).astype(BF16)
    ccr, cci = to_cc(p["c_re"]), to_cc(p["c_im"])

    w_in_r = comm.w_in()
    hb, z = _in_proj(x, g_mix, w_in_r)
    qn, kn, vb, ub, c128 = _attn_prep(z, gq, gk, bf, gg)
    qh, kh, vh = _to_heads(qn), _to_heads(kn), _to_heads(vb)
    crow = c128[:, :HEADS].T.reshape(HEADS, 1, s)
    (oh, lse), landed = _attn_fwd(qh, kh, vh, crow, comm.gather_ici())
    att = _from_heads(oh)
    (xr, xi, y), gathered = _ssm_fwd(ub, z, bbr, bbi, ar, ai, ccr, cci, dsk, comm.gather_d2d(landed))
    w_glu_b, w_out_b, w_up_b, w_down_b, conv_w_full = comm.weights(gathered)
    x1, mixb, h2b = _mix_out(y, att, x, w_glu_b, b_glu, g_att, g_ssm, w_out_b, g_ffn)
    up = _mm(h2b, w_up_b, name="ffn_up", tm=1024, tn=1408, tk=1024)
    act = _conv_act(up, conv_w_full, conv_b)
    dy, dyb, loss_blk = _down_loss(act, w_down_b, x1, tgt)

    d_w_down = _mm(act, dyb, ta=True, name="d_w_down", tm=1408, tn=1024, tk=2048)
    dact = _mm(dyb, w_down_b, tb=True, name="d_act", tm=1024, tn=1408, tk=1024)
    dupb, dcw = _conv_act_bwd(up, dact, conv_w_full, conv_b)
    d_w_up = _mm(h2b, dupb, ta=True, b_parts=2, name="d_w_up", tm=1024, tn=1408, tk=2048)
    dh2 = _mm(dupb, w_up_b, tb=True, a_parts=2, name="d_h2", tm=1024, tn=1024, tk=1408)
    dx1, dx1b, datt, dys, d_w_glu, d_g_ffn, d_g_att, d_g_ssm, d_b_glu = _mix_bwd(
        dy, dh2, x1, g_ffn, w_out_b, y, att, w_glu_b, b_glu, g_att, g_ssm)
    d_w_out = _mm(mixb, dx1b, ta=True, name="d_w_out", tm=1024, tn=1024, tk=2048)
    (du, dbbr, dbbi, dccr, dcci, dar, dai, dd), swapped = _ssm_bwd(dys, z, ub, xr, xi, bbr, bbi, ar, ai, ccr, cci, dsk,
                                                                comm.swap(d_w_down, d_w_up))
    doh = _to_heads(datt)
    (dqh, dkh, dvh, dcrow), early_lands = _attn_bwd(qh, kh, vh, crow, lse, doh, comm.scatter(swapped))
    dc128 = jnp.pad(dcrow.reshape(HEADS, s).T, ((0, 0), (0, LANES - HEADS)))
    dzb, d_gq, d_gk, d_bf = _prep_bwd(z, _from_heads(dqh), _from_heads(dkh), _from_heads(dvh), du, dc128, gq, gk, bf, gg)
    d_w_in_r = _mm(hb, dzb, ta=True, name="d_w_in", tm=512, tn=Z_COLS, tk=2048)
    dh = _mm(dzb, w_in_r, tb=True, name="d_h", tm=1024, tn=1024, tk=Z_COLS)
    dx, d_g_mix = _in_norm_bwd(x, g_mix, dh, dx1)

    unchunk = lambda t: t.reshape(_PARAM_SHAPE)
    dbb_re = unchunk(_diag_blocks(dbbr, SSM_GROUP, SSM_STATE))
    dbb_im = unchunk(_diag_blocks(dbbi, SSM_GROUP, SSM_STATE))
    first_row = (jnp.arange(_PARAM_SHAPE[0]) % SSM_GROUP == 0)[:, None]
    da_re = jnp.where(first_row, rep(dar.reshape(SSM_GROUPS, SSM_STATE)), 0.0)
    da_im = jnp.where(first_row, rep(dai.reshape(SSM_GROUPS, SSM_STATE)), 0.0)
    expand_t = (jnp.arange(SSM_GROUPS)[:, None] == (jnp.arange(_PARAM_SHAPE[0]) // SSM_GROUP)[None, :]).astype(BF16)
    d_lr, d_li, d_ls, d_bt_re, d_bt_im = _ssm_params_bwd(lr, li, ls, bt_re, bt_im, da_re, da_im, dbb_re, dbb_im, expand_t)
    from_bt = lambda t: t.reshape(SSM_GROUPS, SSM_GROUP, SSM_STATE).transpose(0, 2, 1)
    from_cc = lambda t: _diag_blocks(t, SSM_STATE, SSM_GROUP).transpose(0, 1, 3, 2).reshape(SSM_GROUPS, SSM_GROUP, SSM_STATE)

    small = {
        "g_mix": d_g_mix, "b_f": d_bf[0, :HEADS], "g_q": d_gq.reshape(HEADS, HEAD_DIM).sum(0),
        "g_k": d_gk.reshape(HEADS, HEAD_DIM).sum(0), "lambda_re": d_lr, "lambda_im": d_li, "log_step": d_ls,
        "b_re": from_bt(d_bt_re), "b_im": from_bt(d_bt_im), "c_re": from_cc(dccr), "c_im": from_cc(dcci),
        "d_skip": dd, "b_glu": d_b_glu, "g_attn_out": d_g_att, "g_ssm_out": d_g_ssm, "g_ffn": d_g_ffn,
        "conv_b": dcw[:, 3],
    }
    big = {"w_in": _restore_in_cols(d_w_in_r), "w_glu": d_w_glu, "w_out": d_w_out, "w_up": d_w_up, "w_down": d_w_down}
    return loss_blk[0, 0], dx, big, small, dcw[:, 0:3].transpose(1, 0, 2).reshape(3, 2 * D_FF), early_lands


def kernel(x, g_mix, w_in, b_f, g_q, g_k, lambda_re, lambda_im, log_step, b_re, b_im, c_re, c_im, d_skip, w_glu, b_glu, g_attn_out, g_ssm_out, w_out, g_ffn, w_up, conv_w, conv_b, w_down, loss_target, m_g_mix, m_w_in, m_b_f, m_g_q, m_g_k, m_lambda_re, m_lambda_im, m_log_step, m_b_re, m_b_im, m_c_re, m_c_im, m_d_skip, m_w_glu, m_b_glu, m_g_attn_out, m_g_ssm_out, m_w_out, m_g_ffn, m_w_up, m_conv_w, m_conv_b, m_w_down, v_g_mix, v_w_in, v_b_f, v_g_q, v_g_k, v_lambda_re, v_lambda_im, v_log_step, v_b_re, v_b_im, v_c_re, v_c_im, v_d_skip, v_w_glu, v_b_glu, v_g_attn_out, v_g_ssm_out, v_w_out, v_g_ffn, v_w_up, v_conv_w, v_conv_b, v_w_down):
    args = dict(locals())
    order = ["g_mix", "w_in", "b_f", "g_q", "g_k", "lambda_re", "lambda_im", "log_step", "b_re", "b_im", "c_re", "c_im",
             "d_skip", "w_glu", "b_glu", "g_attn_out", "g_ssm_out", "w_out", "g_ffn", "w_up", "conv_w", "conv_b", "w_down"]
    comm = _MeshComm(args)
    chip = comm.chip
    loss_part, dx, big, small, d_conv_w, early_lands = _local_step(x[0], loss_target[0], args, comm)
    loss = lax.psum(loss_part, ("x", "y", "c"))

    g_big = comm.reduce(early_lands, big["w_in"], big["w_glu"], big["w_out"])

    small_names = [n for n, _ in _SMALL]
    small_shapes = [sh for _, sh in _SMALL]
    gsum = _allreduce_small(_pack_small([small[n] for n in small_names], extra=[d_conv_w]))
    g_small = _unpack_small(gsum, small_shapes + [(3, 2 * D_FF)])
    g_conv_w = lax.dynamic_slice_in_dim(g_small[-1], chip * (2 * D_FF // N_CHIPS), 2 * D_FF // N_CHIPS, axis=1)
    g_small = dict(zip(small_names, g_small[:-1]))

    grad, delta, new_m, new_v = {}, {}, {}, {}
    for n in ("w_in", "w_glu", "w_out", "w_up", "w_down"):
        grad[n] = g_big[n]
        delta[n], new_m[n], new_v[n] = _adamw(args[n], g_big[n], args["m_" + n], args["v_" + n], name="adamw_" + n)
    grad["conv_w"] = g_conv_w
    delta["conv_w"], new_m["conv_w"], new_v["conv_w"] = _adamw(conv_w, g_conv_w, m_conv_w, v_conv_w, name="adamw_conv_w")
    stepped = _adamw_small([args[n] for n in small_names], [g_small[n] for n in small_names],
                           [args["m_" + n] for n in small_names], [args["v_" + n] for n in small_names])
    for i, n in enumerate(small_names):
        grad[n] = g_small[n]
        delta[n], new_m[n], new_v[n] = stepped[3 * i:3 * i + 3]

    return (loss, dx[None], *[grad[n] for n in order], *[delta[n] for n in order], *[new_m[n] for n in order],
            *[new_v[n] for n in order])
```

```python
import math

import jax
import jax.numpy as jnp
from jax import lax
from jax.experimental import pallas as pl
from jax.experimental.pallas import tpu as pltpu

F32 = jnp.float32
BF16 = jnp.bfloat16

D_MODEL = 1024
HEADS = 8
HEAD_DIM = 64
ATTN_W = 512
SSM_W = 512
SSM_GROUPS = 32
SSM_GROUP = 16
SSM_STATE = 64
N_STATE = SSM_GROUPS * SSM_STATE
D_FF = 2816
IN_COLS = 2056
Z_COLS = 2176
U_COL0 = 1536
F_COL0 = 2048
EPS = 1e-6
NEG_INF = -1e30
N_CHIPS = 4
LANES = 128
SUBLANES = 8
SSM_CHUNKS = 4
CHUNK_U = SSM_W // SSM_CHUNKS
CHUNK_S = N_STATE // SSM_CHUNKS
HEADS_PER_STEP = 2
STRIP = 128
N_STRIPS = D_FF // STRIP

ROWS_IN, ROWS_GLU, ROWS_OUT, ROWS_UP, ROWS_DOWN = 514, 64, 256, 1408, 704
OFF_GLU = ROWS_IN
OFF_OUT = OFF_GLU + ROWS_GLU
OFF_UP = OFF_OUT + ROWS_OUT
OFF_DOWN = OFF_UP + ROWS_UP
OFF_SPARE = OFF_DOWN + ROWS_DOWN
PACK_ROWS = 2976
HALF_ROWS = PACK_ROWS // 2
CONVW_ROWS = 9

ADAM_LR = 0.001
ADAM_B1 = 0.9
ADAM_B2 = 0.999
ADAM_EPS = 1e-08
ADAM_WD = 0.01
ADAM_STEP = 10

VMEM_LIMIT = 56 * 1024 * 1024
MESH = pl.DeviceIdType.MESH


def _pallas(body, **kw):
    return pl.pallas_call(body, **kw)


def _pcall(body, *, name, out_shape, in_specs, out_specs, grid=(), scratch_shapes=(), dims=None):
    params = pltpu.CompilerParams(dimension_semantics=dims, vmem_limit_bytes=VMEM_LIMIT)
    return _pallas(body, name=name, grid=grid, in_specs=in_specs, out_specs=out_specs,
                   out_shape=out_shape, scratch_shapes=scratch_shapes, compiler_params=params)


def _sds(shape, dtype=F32):
    return jax.ShapeDtypeStruct(shape, dtype)


def _dot(a, b):
    return jnp.dot(a, b, preferred_element_type=F32)


def _dot_nt(a, b):
    return lax.dot_general(a, b, (((1,), (1,)), ((), ())), preferred_element_type=F32)


def _dot_tn(a, b):
    return lax.dot_general(a, b, (((0,), (0,)), ((), ())), preferred_element_type=F32)


def _split3(x):
    hi = x.astype(BF16)
    r = x - hi.astype(F32)
    mid = r.astype(BF16)
    lo = (r - mid.astype(F32)).astype(BF16)
    return hi, mid, lo


def _dot_exact_r(x, m01):
    hi, mid, lo = _split3(x)
    return _dot(hi, m01) + _dot(mid, m01) + _dot(lo, m01)


def _dot_exact_l(m01, x):
    hi, mid, lo = _split3(x)
    return _dot(m01, hi) + _dot(m01, mid) + _dot(m01, lo)


def _sigmoid(x):
    return 1.0 / (1.0 + jnp.exp(-x))


def _rms(x, g):
    r = lax.rsqrt(jnp.mean(x * x, axis=-1, keepdims=True) + EPS)
    return x * r * g


def _rms_bwd(x, g, dy):
    r = lax.rsqrt(jnp.mean(x * x, axis=-1, keepdims=True) + EPS)
    w = dy * g
    dx = r * w - x * (r * r * r) * jnp.mean(w * x, axis=-1, keepdims=True)
    dg = jnp.sum(dy * x * r, axis=0, keepdims=True)
    return dx, dg


_GELU_K = math.sqrt(2.0 / math.pi)
_GELU_C = 0.044715


def _gelu(y):
    return y * (0.5 * (1.0 + jnp.tanh(_GELU_K * (y + _GELU_C * (y * y * y)))))


def _gelu_grad(y):
    t = jnp.tanh(_GELU_K * (y + _GELU_C * (y * y * y)))
    return 0.5 * (1.0 + t) + 0.5 * y * (1.0 - t * t) * (_GELU_K * (1.0 + 3.0 * _GELU_C * y * y))


def _tile(n, pref):
    if n <= pref:
        return n
    divs = [t for t in range(LANES, n + 1, LANES) if n % t == 0]
    below = [t for t in divs if t <= pref]
    if below and 2 * below[-1] >= pref:
        return below[-1]
    above = [t for t in divs if t > pref]
    return above[0] if above else n


def _row_tile(s):
    return min(256, s)


def _mm(a, b, *, name, tm, tn, tk, ta=False, tb=False, a_parts=1, b_parts=1):
    if a_parts > 1:
        m, kk = a.shape[1], a.shape[2] * a_parts
    elif ta:
        kk, m = a.shape
    else:
        m, kk = a.shape
    if b_parts > 1:
        n = b.shape[2] * b_parts
    else:
        n = b.shape[0] if tb else b.shape[1]
    tm, tn, tk = _tile(m, tm), _tile(n // b_parts, tn), _tile(kk // a_parts, tk)
    k_per, n_per = kk // a_parts // tk, n // b_parts // tn

    def body(a_ref, b_ref, o_ref):
        k = pl.program_id(2)
        if ta:
            part = _dot_tn(a_ref[...], b_ref[...])
        elif tb:
            part = _dot_nt(a_ref[...], b_ref[...])
        else:
            part = _dot(a_ref[...], b_ref[...])

        @pl.when(k == 0)
        def _():
            o_ref[...] = part

        @pl.when(k > 0)
        def _():
            o_ref[...] += part

    if a_parts > 1:
        a_spec = pl.BlockSpec((None, tm, tk), lambda i, j, k: (k // k_per, i, k % k_per))
    else:
        a_spec = pl.BlockSpec((tk, tm), lambda i, j, k: (k, i)) if ta else pl.BlockSpec((tm, tk), lambda i, j, k: (i, k))
    if b_parts > 1:
        b_spec = pl.BlockSpec((None, tk, tn), lambda i, j, k: (j // n_per, k, j % n_per))
    else:
        b_spec = pl.BlockSpec((tn, tk), lambda i, j, k: (j, k)) if tb else pl.BlockSpec((tk, tn), lambda i, j, k: (k, j))
    return _pcall(body, name=name, grid=(m // tm, n // tn, kk // tk), in_specs=[a_spec, b_spec],
                  out_specs=pl.BlockSpec((tm, tn), lambda i, j, k: (i, j)), out_shape=_sds((m, n)),
                  dims=("parallel", "parallel", "arbitrary"))(a, b)


def _in_proj(x, g_mix, w_in_r):
    s = x.shape[0]
    tm = _row_tile(s)

    def body(x_ref, g_ref, w_ref, h_ref, z_ref):
        h = _rms(x_ref[...], g_ref[...]).astype(BF16)
        h_ref[...] = h
        z_ref[...] = _dot(h, w_ref[...])

    return _pcall(body, name="in_proj", grid=(s // tm,),
                  in_specs=[pl.BlockSpec((tm, D_MODEL), lambda i: (i, 0)), pl.BlockSpec((1, D_MODEL), lambda i: (0, 0)),
                            pl.BlockSpec((D_MODEL, Z_COLS), lambda i: (0, 0))],
                  out_specs=[pl.BlockSpec((tm, D_MODEL), lambda i: (i, 0)), pl.BlockSpec((tm, Z_COLS), lambda i: (i, 0))],
                  out_shape=[_sds((s, D_MODEL), BF16), _sds((s, Z_COLS))], dims=("parallel",))(x, g_mix, w_in_r)


def _attn_prep(z, gq, gk, bf, gg):
    s = z.shape[0]
    tm = _row_tile(s)

    def body(z_ref, gq_ref, gk_ref, bf_ref, gg_ref, qn_ref, kn_ref, vb_ref, ub_ref, c_ref, carry_ref):
        i = pl.program_id(0)

        @pl.when(i == 0)
        def _():
            carry_ref[...] = jnp.zeros_like(carry_ref)

        gg_m = gg_ref[...]

        def head_norm(t, g):
            ssq = _dot_exact_r(t * t, gg_m)
            return t * lax.rsqrt(ssq * (1.0 / HEAD_DIM) + EPS) * g

        qn_ref[...] = head_norm(z_ref[:, 0:ATTN_W], gq_ref[...]).astype(BF16)
        kn_ref[...] = head_norm(z_ref[:, ATTN_W:2 * ATTN_W], gk_ref[...]).astype(BF16)
        vb_ref[...] = z_ref[:, 2 * ATTN_W:3 * ATTN_W].astype(BF16)
        ub_ref[...] = z_ref[:, U_COL0:U_COL0 + SSM_W].astype(BF16)
        fl = z_ref[:, F_COL0:F_COL0 + LANES] + bf_ref[...]
        lf = jnp.minimum(fl, 0.0) - jnp.log1p(jnp.exp(-jnp.abs(fl)))
        row = lax.broadcasted_iota(jnp.int32, (tm, tm), 0)
        col = lax.broadcasted_iota(jnp.int32, (tm, tm), 1)
        tri = (row >= col).astype(BF16)
        c = _dot_exact_l(tri, lf) + carry_ref[...]
        c_ref[...] = c
        carry_ref[...] = c[tm - 1:tm, :]

    row_spec = lambda w: pl.BlockSpec((tm, w), lambda i: (i, 0))
    const = lambda shape: pl.BlockSpec(shape, lambda i: (0, 0))
    return _pcall(body, name="attn_prep", grid=(s // tm,),
                  in_specs=[row_spec(Z_COLS), const((1, ATTN_W)), const((1, ATTN_W)), const((1, LANES)), const((ATTN_W, ATTN_W))],
                  out_specs=[row_spec(ATTN_W)] * 4 + [row_spec(LANES)],
                  out_shape=[_sds((s, ATTN_W), BF16)] * 4 + [_sds((s, LANES))],
                  scratch_shapes=[pltpu.VMEM((1, LANES), F32)], dims=("arbitrary",))(z, gq, gk, bf, gg)


def _attn_fwd(qh, kh, vh, crow, hosted=None):
    _, s, _ = qh.shape
    tq = _row_tile(s)
    scale = HEAD_DIM ** -0.5

    hp = HEADS_PER_STEP
    nq = s // tq
    fold = lambda t, op: op(t[:, :tq // 2], t[:, tq // 2:])

    def body(q_ref, k_ref, v_ref, c_ref, o_ref, lse_ref, s_s):
        i = pl.program_id(1)

        def first(j, ms, diagonal):
            off = pl.multiple_of(j * tq, tq)
            out = []
            for hh in range(hp):
                sc = _dot_nt(q_ref[hh], k_ref[hh, pl.ds(off, tq), :]) * scale - c_ref[hh, :, pl.ds(off, tq)]
                if diagonal:
                    causal = lax.broadcasted_iota(jnp.int32, (tq, tq), 1) <= lax.broadcasted_iota(jnp.int32, (tq, tq), 0)
                    sc = jnp.where(causal, sc, NEG_INF)
                s_s[hh, j] = sc
                out.append(jnp.maximum(ms[hh], fold(sc, jnp.maximum)))
            return tuple(out)

        ms = lax.fori_loop(0, i, lambda j, c: first(j, c, False), (jnp.full((tq, tq // 2), NEG_INF, F32),) * hp)
        ms = [jnp.max(t, axis=-1, keepdims=True) for t in first(i, ms, True)]

        def second(j, carry):
            rows = pl.ds(pl.multiple_of(j * tq, tq), tq)
            out = []
            for hh in range(hp):
                ls, acc = carry[hh]
                p = jnp.exp(s_s[hh, j] - ms[hh])
                out.append((ls + fold(p, jnp.add), acc + _dot(p.astype(BF16), v_ref[hh, rows, :])))
            return tuple(out)

        zero = (jnp.zeros((tq, tq // 2), F32), jnp.zeros((tq, HEAD_DIM), F32))
        for hh, (ls, acc) in enumerate(lax.fori_loop(0, i + 1, second, (zero,) * hp)):
            l = jnp.sum(ls, axis=-1, keepdims=True)
            o_ref[hh] = acc / l
            lse_ref[hh] = ms[hh] + jnp.log(l)

    blk = pl.BlockSpec((hp, tq, HEAD_DIM), lambda h, i: (h, i, 0))
    full = pl.BlockSpec((hp, s, HEAD_DIM), lambda h, i: (h, 0, 0))
    nh = HEADS // hp
    first = lambda: jnp.logical_and(pl.program_id(0) == 0, pl.program_id(1) == 0)
    last = lambda: jnp.logical_and(pl.program_id(0) == nh - 1, pl.program_id(1) == nq - 1)
    return _host_pcall(body, hosted, first, last, n_in=4, n_out=2, n_scratch=1, name="attn_fwd", grid=(nh, nq),
                       in_specs=[blk, full, full, pl.BlockSpec((hp, 1, s), lambda h, i: (h, 0, 0))],
                       out_specs=[blk, pl.BlockSpec((hp, tq, 1), lambda h, i: (h, i, 0))],
                       out_shape=[_sds((HEADS, s, HEAD_DIM)), _sds((HEADS, s, 1))],
                       scratch_shapes=[pltpu.VMEM((hp, nq, tq, tq), F32)],
                       dims=("parallel", "parallel"), operands=(qh, kh, vh, crow))


def _ssm_param_fn(lr, li, ls, br, bi):
    step = jnp.exp(ls)
    er = jnp.exp(lr * step)
    ab_re = er * jnp.cos(li * step)
    ab_im = er * jnp.sin(li * step)
    num_re = ab_re - 1.0
    num_im = ab_im
    den = lr * lr + li * li
    f_re = (num_re * lr + num_im * li) / den
    f_im = (num_im * lr - num_re * li) / den
    bb_re = f_re * br - f_im * bi
    bb_im = f_re * bi + f_im * br
    return ab_re, ab_im, bb_re, bb_im


_PARAM_SHAPE = (SSM_GROUPS * SSM_GROUP, SSM_STATE)


def _ssm_params(lr, li, ls, br, bi):
    def body(lr_ref, li_ref, ls_ref, br_ref, bi_ref, ar_ref, ai_ref, bbr_ref, bbi_ref):
        ar, ai, bbr, bbi = _ssm_param_fn(lr_ref[...], li_ref[...], ls_ref[...], br_ref[...], bi_ref[...])
        ar_ref[...] = ar
        ai_ref[...] = ai
        bbr_ref[...] = bbr
        bbi_ref[...] = bbi

    spec = pl.BlockSpec(_PARAM_SHAPE, lambda: (0, 0))
    return _pcall(body, name="ssm_params", in_specs=[spec] * 5, out_specs=[spec] * 4,
                  out_shape=[_sds(_PARAM_SHAPE)] * 4)(lr, li, ls, br, bi)


def _ssm_params_bwd(lr, li, ls, br, bi, dar, dai, dbbr, dbbi, expand_t):
    def body(lr_ref, li_ref, ls_ref, br_ref, bi_ref, dar_ref, dai_ref, dbbr_ref, dbbi_ref, et_ref,
             dlr_ref, dli_ref, dls_ref, dbr_ref, dbi_ref):
        _, vjp = jax.vjp(_ssm_param_fn, lr_ref[...], li_ref[...], ls_ref[...], br_ref[...], bi_ref[...])
        dlr, dli, dls, dbr, dbi = vjp((dar_ref[...], dai_ref[...], dbbr_ref[...], dbbi_ref[...]))
        et = et_ref[...]
        dlr_ref[...] = _dot_exact_l(et, dlr)
        dli_ref[...] = _dot_exact_l(et, dli)
        dls_ref[...] = jnp.sum(_dot_exact_l(et, dls), axis=-1, keepdims=True)
        dbr_ref[...] = dbr
        dbi_ref[...] = dbi

    spec = pl.BlockSpec(_PARAM_SHAPE, lambda: (0, 0))
    gspec = pl.BlockSpec((SSM_GROUPS, SSM_STATE), lambda: (0, 0))
    return _pcall(body, name="ssm_params_bwd",
                  in_specs=[spec] * 9 + [pl.BlockSpec((SSM_GROUPS, _PARAM_SHAPE[0]), lambda: (0, 0))],
                  out_specs=[gspec, gspec, pl.BlockSpec((SSM_GROUPS, 1), lambda: (0, 0)), spec, spec],
                  out_shape=[_sds((SSM_GROUPS, SSM_STATE))] * 2 + [_sds((SSM_GROUPS, 1))] + [_sds(_PARAM_SHAPE)] * 2,
                  )(lr, li, ls, br, bi, dar, dai, dbbr, dbbi, expand_t)


def _cmul(ar, ai, br, bi):
    return ar * br - ai * bi, ar * bi + ai * br


def _scan_consts(ar, ai, width, reverse):
    row = lax.broadcasted_iota(jnp.int32, (SUBLANES, width), 0)
    pw = [(ar, ai)]
    for _ in range(SUBLANES - 1):
        pw.append(_cmul(pw[-1][0], pw[-1][1], ar, ai))
    steps = []
    for d in (1, 2, 4):
        keep = (row < SUBLANES - d) if reverse else (row >= d)
        steps.append((d, jnp.where(keep, pw[d - 1][0], 0.0), jnp.where(keep, pw[d - 1][1], 0.0)))
    pr = jnp.zeros((SUBLANES, width), F32)
    pi = jnp.zeros((SUBLANES, width), F32)
    for r in range(SUBLANES):
        e = (SUBLANES - r) if reverse else (r + 1)
        pr = jnp.where(row == r, pw[e - 1][0], pr)
        pi = jnp.where(row == r, pw[e - 1][1], pi)
    return steps, pr, pi


def _scan_tile(xr, xi, cr, ci, consts, reverse):
    steps, pr, pi = consts
    for d, mr, mi in steps:
        sh = (SUBLANES - d) if reverse else d
        sr = pltpu.roll(xr, sh, 0)
        si = pltpu.roll(xi, sh, 0)
        xr, xi = xr + mr * sr - mi * si, xi + mr * si + mi * sr
    return xr + pr * cr - pi * ci, xi + pr * ci + pi * cr


def _ssm_fwd(ub, z, bbr, bbi, ar, ai, ccr, cci, dsk, hosted=None):
    s = ub.shape[0]
    tm = _row_tile(s)
    nt = tm // SUBLANES

    def body(ub_ref, u_ref, bbr_ref, bbi_ref, ar_ref, ai_ref, ccr_ref, cci_ref, dsk_ref,
             xr_ref, xi_ref, y_ref, cr_s, ci_s):
        i = pl.program_id(1)

        @pl.when(i == 0)
        def _():
            cr_s[...] = jnp.zeros_like(cr_s)
            ci_s[...] = jnp.zeros_like(ci_s)

        u_b = ub_ref[...]
        xr_ref[...] = _dot(u_b, bbr_ref[0])
        xi_ref[...] = _dot(u_b, bbi_ref[0])
        consts = _scan_consts(ar_ref[0], ai_ref[0], CHUNK_S, False)

        def tile(k, carry):
            cr, ci = carry
            sl = pl.ds(pl.multiple_of(k * SUBLANES, SUBLANES), SUBLANES)
            xr, xi = _scan_tile(xr_ref[sl, :], xi_ref[sl, :], cr, ci, consts, False)
            xr_ref[sl, :] = xr
            xi_ref[sl, :] = xi
            return xr[SUBLANES - 1:SUBLANES, :], xi[SUBLANES - 1:SUBLANES, :]

        cr, ci = lax.fori_loop(0, nt, tile, (cr_s[...], ci_s[...]))
        cr_s[...] = cr
        ci_s[...] = ci
        y_ref[...] = (_dot(xr_ref[...].astype(BF16), ccr_ref[0]) - _dot(xi_ref[...].astype(BF16), cci_ref[0])
                      + dsk_ref[...] * u_ref[...])

    ucol0 = U_COL0 // CHUNK_U
    wspec = lambda a, b: pl.BlockSpec((1, a, b), lambda j, i: (j, 0, 0))
    nb = s // tm
    first = lambda: jnp.logical_and(pl.program_id(0) == 0, pl.program_id(1) == 0)
    last = lambda: jnp.logical_and(pl.program_id(0) == SSM_CHUNKS - 1, pl.program_id(1) == nb - 1)
    return _host_pcall(
        body, hosted, first, last, n_in=9, n_out=3, n_scratch=2, name="ssm_fwd", grid=(SSM_CHUNKS, nb),
        in_specs=[pl.BlockSpec((tm, CHUNK_U), lambda j, i: (i, j)),
                  pl.BlockSpec((tm, CHUNK_U), lambda j, i: (i, ucol0 + j)),
                  wspec(CHUNK_U, CHUNK_S), wspec(CHUNK_U, CHUNK_S), wspec(1, CHUNK_S), wspec(1, CHUNK_S),
                  wspec(CHUNK_S, CHUNK_U), wspec(CHUNK_S, CHUNK_U),
                  pl.BlockSpec((1, CHUNK_U), lambda j, i: (0, j))],
        out_specs=[pl.BlockSpec((tm, CHUNK_S), lambda j, i: (i, j)), pl.BlockSpec((tm, CHUNK_S), lambda j, i: (i, j)),
                   pl.BlockSpec((tm, CHUNK_U), lambda j, i: (i, j))],
        out_shape=[_sds((s, N_STATE)), _sds((s, N_STATE)), _sds((s, SSM_W))],
        scratch_shapes=[pltpu.VMEM((1, CHUNK_S), F32)] * 2,
        dims=("parallel", "arbitrary"), operands=(ub, z, bbr, bbi, ar, ai, ccr, cci, dsk))


def _ssm_glu(y, w_glu, b_glu):
    ge = _gelu(y)
    sg = _sigmoid(_dot(ge.astype(BF16), w_glu) + b_glu)
    return ge, sg


def _mix_out(y, att, x, w_glu, b_glu, g_att, g_ssm, w_out, g_ffn, hosted=None):
    s = x.shape[0]
    tm = _row_tile(s)

    def body(y_ref, att_ref, x_ref, wg_ref, bg_ref, ga_ref, gs_ref, wo_ref, gf_ref, x1_ref, mix_ref, h2_ref):
        ge, sg = _ssm_glu(y_ref[...], wg_ref[...], bg_ref[...])
        ms = _rms(ge * sg, gs_ref[...]).astype(BF16)
        ma = _rms(att_ref[...], ga_ref[...]).astype(BF16)
        mix_ref[:, 0:ATTN_W] = ma
        mix_ref[:, ATTN_W:D_MODEL] = ms
        x1 = x_ref[...] + (_dot(ma, wo_ref[0:ATTN_W, :]) + _dot(ms, wo_ref[ATTN_W:D_MODEL, :]))
        x1_ref[...] = x1
        h2_ref[...] = _rms(x1, gf_ref[...]).astype(BF16)

    row = lambda w: pl.BlockSpec((tm, w), lambda i: (i, 0))
    const = lambda a, b: pl.BlockSpec((a, b), lambda i: (0, 0))
    nb = s // tm
    return _host_pcall(body, hosted, lambda: pl.program_id(0) == 0, lambda: pl.program_id(0) == nb - 1,
                       n_in=9, n_out=3, n_scratch=0, name="mix_out", grid=(nb,),
                       in_specs=[row(SSM_W), row(ATTN_W), row(D_MODEL), const(SSM_W, SSM_W), const(1, SSM_W),
                                 const(1, ATTN_W), const(1, SSM_W), const(D_MODEL, D_MODEL), const(1, D_MODEL)],
                       out_specs=[row(D_MODEL)] * 3,
                       out_shape=[_sds((s, D_MODEL)), _sds((s, D_MODEL), BF16), _sds((s, D_MODEL), BF16)],
                       scratch_shapes=[], dims=("parallel",), operands=(y, att, x, w_glu, b_glu, g_att, g_ssm, w_out, g_ffn))


CONV_CHUNK = 64


def _conv_rows(pad_ref, w, b, r0, n):
    y = b + pad_ref[pl.ds(r0 + SUBLANES - 2, n), :] * w[0:1, :]
    y = y + pad_ref[pl.ds(r0 + SUBLANES - 1, n), :] * w[1:2, :]
    return y + pad_ref[pl.ds(r0 + SUBLANES, n), :] * w[2:3, :]


def _fill_front_pad(pad_ref, strip_ref, s):
    pad_ref[0:SUBLANES, :] = jnp.zeros((SUBLANES, STRIP), F32)
    for r0 in range(0, s, CONV_CHUNK):
        pad_ref[pl.ds(SUBLANES + r0, CONV_CHUNK), :] = strip_ref[pl.ds(r0, CONV_CHUNK), :]


def _conv_act(up, conv_w, conv_b):
    s = up.shape[0]

    def body(ug_ref, uv_ref, wg_ref, wv_ref, bg_ref, bv_ref, act_ref, pg_ref, pv_ref):
        _fill_front_pad(pg_ref, ug_ref, s)
        _fill_front_pad(pv_ref, uv_ref, s)
        wg, wv, bg, bv = wg_ref[...], wv_ref[...], bg_ref[...], bv_ref[...]
        for r0 in range(0, s, CONV_CHUNK):
            hg = _conv_rows(pg_ref, wg, bg, r0, CONV_CHUNK)
            hv = _conv_rows(pv_ref, wv, bv, r0, CONV_CHUNK)
            act_ref[pl.ds(r0, CONV_CHUNK), :] = (hg * _sigmoid(hg) * hv).astype(BF16)

    strip = lambda off: pl.BlockSpec((s, STRIP), lambda j: (0, j + off))
    wsp = lambda off: pl.BlockSpec((3, STRIP), lambda j: (0, j + off))
    bsp = lambda off: pl.BlockSpec((1, STRIP), lambda j: (0, j + off))
    return _pcall(body, name="conv_act", grid=(N_STRIPS,),
                  in_specs=[strip(0), strip(N_STRIPS), wsp(0), wsp(N_STRIPS), bsp(0), bsp(N_STRIPS)],
                  out_specs=pl.BlockSpec((s, STRIP), lambda j: (0, j)), out_shape=_sds((s, D_FF), BF16),
                  scratch_shapes=[pltpu.VMEM((s + SUBLANES, STRIP), F32)] * 2,
                  dims=("parallel",))(up, up, conv_w, conv_w, conv_b, conv_b)


def _down_loss(act, w_down, x1, tgt):
    s = x1.shape[0]
    tm = _row_tile(s)

    def body(a_ref, w_ref, x1_ref, t_ref, dy_ref, dyb_ref, loss_ref):
        i = pl.program_id(0)

        @pl.when(i == 0)
        def _():
            loss_ref[...] = jnp.zeros_like(loss_ref)

        diff = x1_ref[...] + _dot(a_ref[...], w_ref[...]) - t_ref[...]
        dy = diff * (1.0 / D_MODEL)
        dy_ref[...] = dy
        dyb_ref[...] = dy.astype(BF16)
        loss_ref[...] += 0.5 * jnp.sum(diff * dy)

    row = lambda w: pl.BlockSpec((tm, w), lambda i: (i, 0))
    return _pcall(body, name="down_loss", grid=(s // tm,),
                  in_specs=[row(D_FF), pl.BlockSpec((D_FF, D_MODEL), lambda i: (0, 0)), row(D_MODEL), row(D_MODEL)],
                  out_specs=[row(D_MODEL), row(D_MODEL), pl.BlockSpec((SUBLANES, LANES), lambda i: (0, 0))],
                  out_shape=[_sds((s, D_MODEL)), _sds((s, D_MODEL), BF16), _sds((SUBLANES, LANES))],
                  dims=("arbitrary",))(act, w_down, x1, tgt)


def _conv_act_bwd(up, dact, conv_w, conv_b):
    s = up.shape[0]
    ch = CONV_CHUNK

    def body(ug_ref, uv_ref, da_ref, wg_ref, wv_ref, bg_ref, bv_ref, dup_ref, dcw_ref, pg_ref, pv_ref, dg_ref, dv_ref):
        _fill_front_pad(pg_ref, ug_ref, s)
        _fill_front_pad(pv_ref, uv_ref, s)
        zero = jnp.zeros((SUBLANES, STRIP), F32)
        dg_ref[pl.ds(s, SUBLANES), :] = zero
        dv_ref[pl.ds(s, SUBLANES), :] = zero
        wg, wv, bg, bv = wg_ref[...], wv_ref[...], bg_ref[...], bv_ref[...]
        tile_sum = lambda t: jnp.sum(t.reshape(ch // SUBLANES, SUBLANES, STRIP), axis=0)
        accs = [[zero] * 4, [zero] * 4]
        for r0 in range(0, s, ch):
            hg = _conv_rows(pg_ref, wg, bg, r0, ch)
            hv = _conv_rows(pv_ref, wv, bv, r0, ch)
            sg = _sigmoid(hg)
            da = da_ref[pl.ds(r0, ch), :]
            dhs = (da * hv * (sg * (1.0 + hg * (1.0 - sg))), da * (hg * sg))
            for half, (dh, d_ref, p_ref) in enumerate(zip(dhs, (dg_ref, dv_ref), (pg_ref, pv_ref))):
                d_ref[pl.ds(r0, ch), :] = dh
                for k in range(3):
                    accs[half][k] = accs[half][k] + tile_sum(dh * p_ref[pl.ds(r0 + SUBLANES - 2 + k, ch), :])
                accs[half][3] = accs[half][3] + tile_sum(dh)
        for half, (d_ref, w) in enumerate(((dg_ref, wg), (dv_ref, wv))):
            for r0 in range(0, s, ch):
                dup = (d_ref[pl.ds(r0, ch), :] * w[2:3, :] + d_ref[pl.ds(r0 + 1, ch), :] * w[1:2, :]
                       + d_ref[pl.ds(r0 + 2, ch), :] * w[0:1, :])
                dup_ref[half, pl.ds(r0, ch), :] = dup.astype(BF16)
            rid = lax.broadcasted_iota(jnp.int32, (SUBLANES, STRIP), 0)
            out = zero
            for k in range(4):
                out = jnp.where(rid == k, jnp.sum(accs[half][k], axis=0, keepdims=True), out)
            dcw_ref[half] = out

    strip = lambda off: pl.BlockSpec((s, STRIP), lambda j: (0, j + off))
    wsp = lambda off: pl.BlockSpec((3, STRIP), lambda j: (0, j + off))
    bsp = lambda off: pl.BlockSpec((1, STRIP), lambda j: (0, j + off))
    return _pcall(body, name="conv_act_bwd", grid=(N_STRIPS,),
                  in_specs=[strip(0), strip(N_STRIPS), strip(0), wsp(0), wsp(N_STRIPS), bsp(0), bsp(N_STRIPS)],
                  out_specs=[pl.BlockSpec((2, s, STRIP), lambda j: (0, 0, j)), pl.BlockSpec((2, SUBLANES, STRIP), lambda j: (0, 0, j))],
                  out_shape=[_sds((2, s, D_FF), BF16), _sds((2, SUBLANES, D_FF))],
                  scratch_shapes=[pltpu.VMEM((s + SUBLANES, STRIP), F32)] * 4,
                  dims=("parallel",))(up, up, dact, conv_w, conv_w, conv_b, conv_b)


def _mix_bwd(dy, dh2, x1, g_ffn, w_out, y, att, w_glu, b_glu, g_att, g_ssm):
    s = dy.shape[0]
    tm = _row_tile(s)

    def body(dy_ref, dh2_ref, x1_ref, gf_ref, wo_ref, y_ref, att_ref, wg_ref, bg_ref, ga_ref, gs_ref,
             dx1_ref, dx1b_ref, datt_ref, dys_ref, dwg_ref, dgf_ref, dga_ref, dgs_ref, dbg_ref):
        i = pl.program_id(0)

        @pl.when(i == 0)
        def _():
            for r in (dwg_ref, dgf_ref, dga_ref, dgs_ref, dbg_ref):
                r[...] = jnp.zeros_like(r)

        dxn, dgf = _rms_bwd(x1_ref[...], gf_ref[...], dh2_ref[...])
        dx1 = dy_ref[...] + dxn
        dx1_ref[...] = dx1
        dx1b = dx1.astype(BF16)
        dx1b_ref[...] = dx1b
        dgf_ref[...] += dgf
        dma = _dot_nt(dx1b, wo_ref[0:ATTN_W, :])
        dms = _dot_nt(dx1b, wo_ref[ATTN_W:D_MODEL, :])
        datt, dga = _rms_bwd(att_ref[...], ga_ref[...], dma)
        datt_ref[...] = datt
        dga_ref[...] += dga
        yv = y_ref[...]
        ge, sg = _ssm_glu(yv, wg_ref[...], bg_ref[...])
        dssm, dgs = _rms_bwd(ge * sg, gs_ref[...], dms)
        dgs_ref[...] += dgs
        dgl = dssm * ge * sg * (1.0 - sg)
        dglb = dgl.astype(BF16)
        dge = dssm * sg + _dot_nt(dglb, wg_ref[...])
        dbg_ref[...] += jnp.sum(dgl, axis=0, keepdims=True)
        dwg_ref[...] += _dot_tn(ge.astype(BF16), dglb)
        dys_ref[...] = dge * _gelu_grad(yv)

    row = lambda w: pl.BlockSpec((tm, w), lambda i: (i, 0))
    const = lambda a, b: pl.BlockSpec((a, b), lambda i: (0, 0))
    return _pcall(body, name="mix_bwd", grid=(s // tm,),
                  in_specs=[row(D_MODEL), row(D_MODEL), row(D_MODEL), const(1, D_MODEL), const(D_MODEL, D_MODEL), row(SSM_W),
                            row(ATTN_W), const(SSM_W, SSM_W), const(1, SSM_W), const(1, ATTN_W), const(1, SSM_W)],
                  out_specs=[row(D_MODEL), row(D_MODEL), row(ATTN_W), row(SSM_W), const(SSM_W, SSM_W), const(1, D_MODEL),
                             const(1, ATTN_W), const(1, SSM_W), const(1, SSM_W)],
                  out_shape=[_sds((s, D_MODEL)), _sds((s, D_MODEL), BF16), _sds((s, ATTN_W)), _sds((s, SSM_W)),
                             _sds((SSM_W, SSM_W)), _sds((1, D_MODEL)), _sds((1, ATTN_W)), _sds((1, SSM_W)), _sds((1, SSM_W))],
                  dims=("arbitrary",))(dy, dh2, x1, g_ffn, w_out, y, att, w_glu, b_glu, g_att, g_ssm)


def _ssm_bwd(dys, z, ub, xr, xi, bbr, bbi, ar, ai, ccr, cci, dsk, hosted=None):
    s = dys.shape[0]
    tm = _row_tile(s)
    nb = s // tm
    nt = tm // SUBLANES

    def body(dy_ref, u_ref, ub_ref, xr_ref, xi_ref, xrp_ref, xip_ref, bbr_ref, bbi_ref, ar_ref, ai_ref, ccr_ref,
             cci_ref, dsk_ref, du_ref, dbbr_ref, dbbi_ref, dccr_ref, dcci_ref, dar_ref, dai_ref, dd_ref,
             gr_s, gi_s, cr_s, ci_s, accr_s, acci_s):
        i = pl.program_id(1)
        first_block = i == nb - 1

        @pl.when(i == 0)
        def _():
            for r in (cr_s, ci_s, accr_s, acci_s, dbbr_ref, dbbi_ref, dccr_ref, dcci_ref, dd_ref):
                r[...] = jnp.zeros_like(r)

        dy = dy_ref[...]
        dyb = dy.astype(BF16)
        gr_s[...] = _dot_nt(dyb, ccr_ref[0])
        gi_s[...] = -_dot_nt(dyb, cci_ref[0])
        consts = _scan_consts(ar_ref[0], -ai_ref[0], CHUNK_S, True)
        row = lax.broadcasted_iota(jnp.int32, (SUBLANES, CHUNK_S), 0)

        def tile(kk, carry):
            cr, ci, accr, acci = carry
            k = nt - 1 - kk
            sl = pl.ds(pl.multiple_of(k * SUBLANES, SUBLANES), SUBLANES)
            gr, gi = _scan_tile(gr_s[sl, :], gi_s[sl, :], cr, ci, consts, True)
            gr_s[sl, :] = gr
            gi_s[sl, :] = gi
            slp = pl.ds(pl.multiple_of(jnp.maximum(k - 1, 0) * SUBLANES, SUBLANES), SUBLANES)
            inner = k > 0
            pr_t = jnp.where(inner, xr_ref[slp, :], xrp_ref[...])
            pi_t = jnp.where(inner, xi_ref[slp, :], xip_ref[...])
            live = jnp.logical_or(inner, jnp.logical_not(first_block))
            top_r = jnp.where(live, pltpu.roll(pr_t, 1, 0), 0.0)
            top_i = jnp.where(live, pltpu.roll(pi_t, 1, 0), 0.0)
            xpr = jnp.where(row == 0, top_r, pltpu.roll(xr_ref[sl, :], 1, 0))
            xpi = jnp.where(row == 0, top_i, pltpu.roll(xi_ref[sl, :], 1, 0))
            accr = accr + gr * xpr + gi * xpi
            acci = acci + gi * xpr - gr * xpi
            return gr[0:1, :], gi[0:1, :], accr, acci

        zeros = jnp.zeros((SUBLANES, CHUNK_S), F32)
        cr, ci, accr, acci = lax.fori_loop(0, nt, tile, (cr_s[...], ci_s[...], zeros, zeros))
        cr_s[...] = cr
        ci_s[...] = ci
        accr_s[...] += accr
        acci_s[...] += acci
        grb = gr_s[...].astype(BF16)
        gib = gi_s[...].astype(BF16)
        u_b = ub_ref[...]
        du_ref[...] = _dot_nt(grb, bbr_ref[0]) + _dot_nt(gib, bbi_ref[0]) + dsk_ref[...] * dy
        dbbr_ref[0] += _dot_tn(u_b, grb)
        dbbi_ref[0] += _dot_tn(u_b, gib)
        dccr_ref[0] += _dot_tn(xr_ref[...].astype(BF16), dyb)
        dcci_ref[0] -= _dot_tn(xi_ref[...].astype(BF16), dyb)
        dd_ref[...] += jnp.sum(dy * u_ref[...], axis=0, keepdims=True)

        @pl.when(i == nb - 1)
        def _():
            dar_ref[0] = jnp.sum(accr_s[...], axis=0, keepdims=True)
            dai_ref[0] = jnp.sum(acci_s[...], axis=0, keepdims=True)

    ucol0 = U_COL0 // CHUNK_U
    tiles_per_block = tm // SUBLANES
    rb = lambda i: nb - 1 - i
    wspec = lambda a, b: pl.BlockSpec((1, a, b), lambda j, i: (j, 0, 0))
    xblk = pl.BlockSpec((tm, CHUNK_S), lambda j, i: (rb(i), j))
    xprev = pl.BlockSpec((SUBLANES, CHUNK_S), lambda j, i: (jnp.maximum(rb(i) * tiles_per_block - 1, 0), j))
    ublk = pl.BlockSpec((tm, CHUNK_U), lambda j, i: (rb(i), j))
    first = lambda: jnp.logical_and(pl.program_id(0) == 0, pl.program_id(1) == 0)
    last = lambda: jnp.logical_and(pl.program_id(0) == SSM_CHUNKS - 1, pl.program_id(1) == nb - 1)
    return _host_pcall(
        body, hosted, first, last, n_in=14, n_out=8, n_scratch=6, name="ssm_bwd", grid=(SSM_CHUNKS, nb),
        in_specs=[ublk, pl.BlockSpec((tm, CHUNK_U), lambda j, i: (rb(i), ucol0 + j)), ublk, xblk, xblk, xprev, xprev,
                  wspec(CHUNK_U, CHUNK_S), wspec(CHUNK_U, CHUNK_S), wspec(1, CHUNK_S), wspec(1, CHUNK_S),
                  wspec(CHUNK_S, CHUNK_U), wspec(CHUNK_S, CHUNK_U), pl.BlockSpec((1, CHUNK_U), lambda j, i: (0, j))],
        out_specs=[ublk, wspec(CHUNK_U, CHUNK_S), wspec(CHUNK_U, CHUNK_S), wspec(CHUNK_S, CHUNK_U),
                   wspec(CHUNK_S, CHUNK_U), wspec(1, CHUNK_S), wspec(1, CHUNK_S),
                   pl.BlockSpec((1, CHUNK_U), lambda j, i: (0, j))],
        out_shape=[_sds((s, SSM_W)), _sds((SSM_CHUNKS, CHUNK_U, CHUNK_S)), _sds((SSM_CHUNKS, CHUNK_U, CHUNK_S)),
                   _sds((SSM_CHUNKS, CHUNK_S, CHUNK_U)), _sds((SSM_CHUNKS, CHUNK_S, CHUNK_U)),
                   _sds((SSM_CHUNKS, 1, CHUNK_S)), _sds((SSM_CHUNKS, 1, CHUNK_S)), _sds((1, SSM_W))],
        scratch_shapes=[pltpu.VMEM((tm, CHUNK_S), F32)] * 2 + [pltpu.VMEM((1, CHUNK_S), F32)] * 2
                       + [pltpu.VMEM((SUBLANES, CHUNK_S), F32)] * 2,
        dims=("parallel", "arbitrary"), operands=(dys, z, ub, xr, xi, xr, xi, bbr, bbi, ar, ai, ccr, cci, dsk))


def _attn_probs(q, ks, cs, lse, scale, diagonal):
    p = jnp.exp(_dot_nt(q, ks) * scale - cs - lse)
    if diagonal:
        tq, tk = p.shape
        causal = lax.broadcasted_iota(jnp.int32, (tq, tk), 1) <= lax.broadcasted_iota(jnp.int32, (tq, tk), 0)
        p = jnp.where(causal, p, 0.0)
    return p


def _attn_bwd(qh, kh, vh, crow, lse, doh, hosted=None):
    _, s, _ = qh.shape
    tq = _row_tile(s)
    nq = s // tq
    scale = HEAD_DIM ** -0.5
    hp = HEADS_PER_STEP

    def body(q_ref, k_ref, v_ref, c_ref, lse_ref, do_ref, dq_ref, dk_ref, dv_ref, dc_ref, p_s, dp_s):
        i = pl.program_id(1)

        @pl.when(i == 0)
        def _():
            for r in (dk_ref, dv_ref, dc_ref):
                r[...] = jnp.zeros_like(r)

        dobs = [do_ref[hh].astype(BF16) for hh in range(hp)]

        def first(j, dls, diagonal):
            off = pl.multiple_of(j * tq, tq)
            out = []
            for hh in range(hp):
                p = _attn_probs(q_ref[hh], k_ref[hh, pl.ds(off, tq), :], c_ref[hh, :, pl.ds(off, tq)], lse_ref[hh],
                                scale, diagonal)
                dp = _dot_nt(dobs[hh], v_ref[hh, pl.ds(off, tq), :])
                p_s[hh, j] = p
                dp_s[hh, j] = dp
                out.append(dls[hh] + jnp.sum(p * dp, axis=-1, keepdims=True))
            return tuple(out)

        zero_col = jnp.zeros((tq, 1), F32)
        dls = lax.fori_loop(0, i, lambda j, c: first(j, c, False), (zero_col,) * hp)
        dls = first(i, dls, True)

        def second(j, dqs):
            rows = pl.ds(pl.multiple_of(j * tq, tq), tq)
            out = []
            for hh in range(hp):
                p = p_s[hh, j]
                ds = p * (dp_s[hh, j] - dls[hh])
                dsb = ds.astype(BF16)
                dv_ref[hh, rows, :] += _dot_tn(p.astype(BF16), dobs[hh])
                dk_ref[hh, rows, :] += _dot_tn(dsb, q_ref[hh]) * scale
                dc_ref[hh, :, rows] -= jnp.sum(ds, axis=0, keepdims=True)
                out.append(dqs[hh] + _dot(dsb, k_ref[hh, rows, :]))
            return tuple(out)

        dqs = lax.fori_loop(0, i + 1, second, (jnp.zeros((tq, HEAD_DIM), F32),) * hp)
        for hh in range(hp):
            dq_ref[hh] = dqs[hh] * scale

    blk = pl.BlockSpec((hp, tq, HEAD_DIM), lambda h, i: (h, i, 0))
    full = pl.BlockSpec((hp, s, HEAD_DIM), lambda h, i: (h, 0, 0))
    crow_spec = pl.BlockSpec((hp, 1, s), lambda h, i: (h, 0, 0))
    nh = HEADS // hp
    first = lambda: jnp.logical_and(pl.program_id(0) == 0, pl.program_id(1) == 0)
    last = lambda: jnp.logical_and(pl.program_id(0) == nh - 1, pl.program_id(1) == nq - 1)
    return _host_pcall(body, hosted, first, last, n_in=6, n_out=4, n_scratch=2, name="attn_bwd", grid=(nh, nq),
                       in_specs=[blk, full, full, crow_spec, pl.BlockSpec((hp, tq, 1), lambda h, i: (h, i, 0)), blk],
                       out_specs=[blk, full, full, crow_spec],
                       out_shape=[_sds((HEADS, s, HEAD_DIM))] * 3 + [_sds((HEADS, 1, s))],
                       scratch_shapes=[pltpu.VMEM((hp, nq, tq, tq), F32)] * 2,
                       dims=("parallel", "arbitrary"), operands=(qh, kh, vh, crow, lse, doh))


def _prep_bwd(z, dqn, dkn, dv, du, dc, gq, gk, bf, gg):
    s = z.shape[0]
    tm = _row_tile(s)
    nb = s // tm

    def body(z_ref, dqn_ref, dkn_ref, dv_ref, du_ref, dc_ref, gq_ref, gk_ref, bf_ref, gg_ref,
             dz_ref, dgq_ref, dgk_ref, dbf_ref, carry_ref):
        i = pl.program_id(0)

        @pl.when(i == 0)
        def _():
            for r in (dgq_ref, dgk_ref, dbf_ref, carry_ref):
                r[...] = jnp.zeros_like(r)

        gg_m = gg_ref[...]

        def head_norm_bwd(t, g, dn):
            r = lax.rsqrt(_dot_exact_r(t * t, gg_m) * (1.0 / HEAD_DIM) + EPS)
            w = dn * g
            mean_wt = _dot_exact_r(w * t, gg_m) * (1.0 / HEAD_DIM)
            return r * w - t * (r * r * r) * mean_wt, jnp.sum(dn * t * r, axis=0, keepdims=True)

        dq, dgq = head_norm_bwd(z_ref[:, 0:ATTN_W], gq_ref[...], dqn_ref[...])
        dk, dgk = head_norm_bwd(z_ref[:, ATTN_W:2 * ATTN_W], gk_ref[...], dkn_ref[...])
        dgq_ref[...] += dgq
        dgk_ref[...] += dgk
        row = lax.broadcasted_iota(jnp.int32, (tm, tm), 0)
        col = lax.broadcasted_iota(jnp.int32, (tm, tm), 1)
        triu = (col >= row).astype(BF16)
        dlf = _dot_exact_l(triu, dc_ref[...]) + carry_ref[...]
        carry_ref[...] = dlf[0:1, :]
        fl = z_ref[:, F_COL0:F_COL0 + LANES] + bf_ref[...]
        df = dlf * _sigmoid(-fl)
        dbf_ref[...] += jnp.sum(df, axis=0, keepdims=True)
        dz_ref[:, 0:ATTN_W] = dq.astype(BF16)
        dz_ref[:, ATTN_W:2 * ATTN_W] = dk.astype(BF16)
        dz_ref[:, 2 * ATTN_W:3 * ATTN_W] = dv_ref[...].astype(BF16)
        dz_ref[:, U_COL0:U_COL0 + SSM_W] = du_ref[...].astype(BF16)
        dz_ref[:, F_COL0:F_COL0 + LANES] = df.astype(BF16)

    row_spec = lambda w: pl.BlockSpec((tm, w), lambda i: (nb - 1 - i, 0))
    const = lambda shape: pl.BlockSpec(shape, lambda i: (0, 0))
    return _pcall(body, name="prep_bwd", grid=(nb,),
                  in_specs=[row_spec(Z_COLS)] + [row_spec(ATTN_W)] * 4 + [row_spec(LANES), const((1, ATTN_W)),
                            const((1, ATTN_W)), const((1, LANES)), const((ATTN_W, ATTN_W))],
                  out_specs=[row_spec(Z_COLS), const((1, ATTN_W)), const((1, ATTN_W)), const((1, LANES))],
                  out_shape=[_sds((s, Z_COLS), BF16), _sds((1, ATTN_W)), _sds((1, ATTN_W)), _sds((1, LANES))],
                  scratch_shapes=[pltpu.VMEM((1, LANES), F32)], dims=("arbitrary",))(z, dqn, dkn, dv, du, dc, gq, gk, bf, gg)


def _in_norm_bwd(x, g_mix, dh, dx1):
    s = x.shape[0]
    tm = _row_tile(s)

    def body(x_ref, g_ref, dh_ref, dx1_ref, dx_ref, dg_ref):
        i = pl.program_id(0)

        @pl.when(i == 0)
        def _():
            dg_ref[...] = jnp.zeros_like(dg_ref)

        dxn, dg = _rms_bwd(x_ref[...], g_ref[...], dh_ref[...])
        dx_ref[...] = dx1_ref[...] + dxn
        dg_ref[...] += dg

    row = pl.BlockSpec((tm, D_MODEL), lambda i: (i, 0))
    vec = pl.BlockSpec((1, D_MODEL), lambda i: (0, 0))
    return _pcall(body, name="in_norm_bwd", grid=(s // tm,), in_specs=[row, vec, row, row], out_specs=[row, vec],
                  out_shape=[_sds((s, D_MODEL)), _sds((1, D_MODEL))], dims=("arbitrary",))(x, g_mix, dh, dx1)


def _adamw_refs(w_ref, g_ref, m_ref, v_ref, d_ref, mo_ref, vo_ref):
    gv = g_ref[...]
    mn = ADAM_B1 * m_ref[...] + (1.0 - ADAM_B1) * gv
    vn = ADAM_B2 * v_ref[...] + (1.0 - ADAM_B2) * (gv * gv)
    m_hat = mn / (1.0 - ADAM_B1 ** ADAM_STEP)
    v_hat = vn / (1.0 - ADAM_B2 ** ADAM_STEP)
    d_ref[...] = -ADAM_LR * (m_hat / (jnp.sqrt(v_hat) + ADAM_EPS) + ADAM_WD * w_ref[...])
    mo_ref[...] = mn
    vo_ref[...] = vn


def _adamw_small(ws, gs, ms, vs):
    n = len(ws)

    def body(*refs):
        ins, outs = refs[:4 * n], refs[4 * n:]
        for i in range(n):
            _adamw_refs(ins[i], ins[n + i], ins[2 * n + i], ins[3 * n + i], *outs[3 * i:3 * i + 3])

    vm = pl.BlockSpec(memory_space=pltpu.VMEM)
    out_shape = [_sds(w.shape) for w in ws for _ in range(3)]
    return _pallas(body, name="adamw_small", in_specs=[vm] * (4 * n), out_specs=[vm] * (3 * n), out_shape=out_shape,
                   compiler_params=pltpu.CompilerParams(vmem_limit_bytes=VMEM_LIMIT))(*ws, *gs, *ms, *vs)


def _adamw(w, g, m, v, *, name):
    r, c = w.shape
    tr = r
    for cand in (256, 176, 128, 64):
        if r > cand and r % cand == 0:
            tr = cand
            break

    def body(w_ref, g_ref, m_ref, v_ref, d_ref, mo_ref, vo_ref):
        _adamw_refs(w_ref, g_ref, m_ref, v_ref, d_ref, mo_ref, vo_ref)

    spec = pl.BlockSpec((tr, c), lambda i: (i, 0))
    return _pcall(body, name=name, grid=(r // tr,), in_specs=[spec] * 4, out_specs=[spec] * 3,
                  out_shape=[_sds((r, c))] * 3, dims=("parallel",))(w, g, m, v)


def _prefetch_call(body, *, name, grid, in_specs, out_specs, out_shape, operands):
    grid_spec = pltpu.PrefetchScalarGridSpec(num_scalar_prefetch=1, grid=grid, in_specs=in_specs, out_specs=out_specs)
    params = pltpu.CompilerParams(dimension_semantics=("parallel",) * len(grid), vmem_limit_bytes=VMEM_LIMIT)
    return _pallas(body, name=name, grid_spec=grid_spec, out_shape=out_shape, compiler_params=params)(*operands)


def _half_rows_tile(hr):
    return hr if hr <= 256 else 176 if hr % 176 == 0 else 256


def _add_half(g, landed, place, *, name):
    def body(place_ref, g_ref, l_ref, o_ref):
        own = g_ref[0] if len(g_ref.shape) == 4 else g_ref[...]
        o_ref[...] = (own + l_ref[...]).astype(BF16)

    if g.ndim == 4:
        _, _, hr, c = g.shape
        tr = _half_rows_tile(hr)
        blk = (1, tr, c)
        return _prefetch_call(
            body, name=name, grid=(N_CHIPS, hr // tr),
            in_specs=[pl.BlockSpec((1,) + blk, lambda j, i, p: (j, p[1], i, 0)), pl.BlockSpec(blk, lambda j, i, p: (j, i, 0))],
            out_specs=pl.BlockSpec(blk, lambda j, i, p: (j, i, 0)), out_shape=_sds(landed.shape, BF16),
            operands=(place, g, landed))
    hr, c = landed.shape
    tr, tc = 256, _tile(c, 2176)
    nb = hr // tr
    return _prefetch_call(
        body, name=name, grid=(nb, c // tc),
        in_specs=[pl.BlockSpec((tr, tc), lambda i, j, p: (p[1] * nb + i, j)), pl.BlockSpec((tr, tc), lambda i, j, p: (i, j))],
        out_specs=pl.BlockSpec((tr, tc), lambda i, j, p: (i, j)), out_shape=_sds(landed.shape, BF16),
        operands=(place, g, landed))


def _sum_chips(chip_sum, lands, place, *, name, tc, window_stride=0):
    _, hr, c = lands.shape
    tr = _half_rows_tile(hr)
    nb = hr // tr
    ncb = c // tc

    def body(place_ref, own_ref, a_ref, b_ref, c_ref, o_ref):
        own = own_ref[0] if len(own_ref.shape) == 3 else own_ref[...]
        o_ref[...] = ((own.astype(F32) + a_ref[0].astype(F32)) + b_ref[0].astype(F32)) + c_ref[0].astype(F32)

    land = lambda k: pl.BlockSpec((1, tr, tc), lambda i, j, p: ((p[0] + k) % N_CHIPS, i, j))
    if chip_sum.ndim == 3:
        own_spec = land(0)
    else:
        stride = window_stride // tc
        own_spec = pl.BlockSpec((tr, tc), lambda i, j, p: (i, p[0] * stride + j))
    return _prefetch_call(
        body, name=name, grid=(nb, ncb), in_specs=[own_spec, land(1), land(2), land(3)],
        out_specs=pl.BlockSpec((tr, tc), lambda i, j, p: (p[1] * nb + i, j)), out_shape=_sds((2 * hr, c)),
        operands=(place, chip_sum, lands, lands, lands))


_HBM = pl.BlockSpec(memory_space=pltpu.HBM)


def _place():
    x, y, c = lax.axis_index("x"), lax.axis_index("y"), lax.axis_index("c")
    chips = [(1 - x, y), (x, 1 - y), (1 - x, 1 - y)]
    return x, y, c, chips


def _rcopy(src, dst, send_sem, recv_sem, to):
    return pltpu.make_async_remote_copy(src_ref=src, dst_ref=dst, send_sem=send_sem, recv_sem=recv_sem,
                                        device_id=to, device_id_type=MESH)


N_BIG = 5
UP_COLS = 2 * D_FF // N_CHIPS
IN_WINDOW = 640
IN_STRIDE = 512


class _Hosted:
    def __init__(self, operands, out_shapes, n_sems, start, finish, aliases=None, local_sems=0):
        self.operands, self.out_shapes, self.n_sems = list(operands), list(out_shapes), n_sems
        self.start, self.finish, self.aliases, self.local_sems = start, finish, dict(aliases or {}), local_sems

    def scratch(self):
        return ([pltpu.SemaphoreType.DMA((self.n_sems,)), pltpu.SemaphoreType.DMA((self.n_sems,))]
                + [pltpu.SemaphoreType.DMA] * self.local_sems)


def _both(a, b):
    na, nao, nas = len(a.operands), len(a.out_shapes), len(a.scratch())

    def start(ins, outs, sems):
        a.start(ins[:na], outs[:nao], sems[:nas])
        b.start(ins[na:], outs[nao:], sems[nas:])

    def finish(ins, outs, sems):
        a.finish(ins[:na], outs[:nao], sems[:nas])
        b.finish(ins[na:], outs[nao:], sems[nas:])

    both = _Hosted(a.operands + b.operands, a.out_shapes + b.out_shapes, 0, start, finish,
                   aliases={**a.aliases, **{na + i: nao + o for i, o in b.aliases.items()}})
    both.scratch = lambda: a.scratch() + b.scratch()
    return both


def _run_hosted(hosted, *, name):
    n_in, n_out = len(hosted.operands), len(hosted.out_shapes)

    def body(*refs):
        parts = (refs[:n_in], refs[n_in:n_in + n_out], refs[n_in + n_out:])
        hosted.start(*parts)
        hosted.finish(*parts)

    return _pallas(body, name=name, in_specs=[_HBM] * n_in, out_specs=[_HBM] * n_out, out_shape=hosted.out_shapes,
                   input_output_aliases=hosted.aliases, scratch_shapes=hosted.scratch())(*hosted.operands)


def _host_pcall(core_body, hosted, first, last, *, n_in, n_out, n_scratch, name, grid, in_specs, out_specs, out_shape,
                scratch_shapes, dims, operands):
    if hosted is None:
        outs = _pcall(core_body, name=name, grid=grid, in_specs=in_specs, out_specs=out_specs, out_shape=out_shape,
                      scratch_shapes=scratch_shapes, dims=dims)(*operands)
        return outs, []
    hi, ho = len(hosted.operands), len(hosted.out_shapes)

    def body(*refs):
        a, b = n_in, n_in + hi
        c, d = b + n_out, b + n_out + ho
        e = d + n_scratch
        parts = (refs[a:b], refs[c:d], refs[e:])

        @pl.when(first())
        def _():
            hosted.start(*parts)

        core_body(*refs[:a], *refs[b:c], *refs[d:e])

        @pl.when(last())
        def _():
            hosted.finish(*parts)

    params = pltpu.CompilerParams(dimension_semantics=("arbitrary",) * len(grid), vmem_limit_bytes=VMEM_LIMIT)
    outs = _pallas(body, name=name, grid=grid, in_specs=list(in_specs) + [_HBM] * hi, out_specs=list(out_specs) + [_HBM] * ho,
                   out_shape=list(out_shape) + hosted.out_shapes, scratch_shapes=list(scratch_shapes) + hosted.scratch(),
                   input_output_aliases={n_in + a: n_out + b for a, b in hosted.aliases.items()},
                   compiler_params=params)(*operands, *hosted.operands)
    return outs[:n_out], outs[n_out:]


def _gather_slot(src, out, chip, hc):
    hr, cols = src.shape[0] // 2, src.shape[1]
    if len(out.shape) == 2:
        return out.at[pl.ds(hc * hr, hr), pl.ds(pl.multiple_of(chip * cols, LANES), cols)]
    return out.at[chip, pl.ds(hc * hr, hr), :]


def _gathered_shape(shard, by_cols):
    if by_cols:
        return _sds((shard.shape[0], N_CHIPS * shard.shape[1]), shard.dtype)
    return _sds((N_CHIPS,) + shard.shape, shard.dtype)


def _plan_gather_ici(shards, by_cols, whole=(), own_cols=()):
    n = len(shards)

    def copies(ins, outs, sems):
        send_sems, recv_sems = sems[0], sems[1]
        x, y, c, chips = _place()
        me = 2 * x + y
        sends, waits = [], []
        for w in range(n + len(whole)):
            for k, (cx, cy) in enumerate(chips):
                sem = (send_sems.at[3 * w + k], recv_sems.at[3 * w + k])
                if w < n:
                    hr = ins[w].shape[0] // 2
                    sends.append(_rcopy(ins[w].at[pl.ds(c * hr, hr), :], _gather_slot(ins[w], outs[w], me, c), *sem, (cx, cy, c)))
                    landed = _gather_slot(ins[w], outs[w], 2 * cx + cy, c)
                else:
                    sends.append(_rcopy(ins[w], outs[w].at[me], *sem, (cx, cy, c)))
                    landed = outs[w].at[2 * cx + cy]
                waits.append(_rcopy(landed, landed, *sem, (cx, cy, c)))
        local = [pltpu.make_async_copy(
            ins[w], outs[w].at[:, pl.ds(pl.multiple_of(me * ins[w].shape[1], LANES), ins[w].shape[1])], sems[2 + i])
            for i, w in enumerate(own_cols)]
        return sends, waits, local

    def start(ins, outs, sems):
        sends, _, local = copies(ins, outs, sems)
        for cp in local + sends:
            cp.start()

    def finish(ins, outs, sems):
        sends, waits, local = copies(ins, outs, sems)
        for cp in waits:
            cp.wait_recv()
        for cp in sends:
            cp.wait_send()
        for cp in local:
            cp.wait()

    out_shapes = [_gathered_shape(s, bc) for s, bc in zip(shards, by_cols)] + [_sds((N_CHIPS,) + a.shape, a.dtype) for a in whole]
    return _Hosted(list(shards) + list(whole), out_shapes, 3 * (n + len(whole)), start, finish, local_sems=len(own_cols))


def _plan_gather_d2d(bufs, shard_shapes):
    n = len(bufs)

    def copies(ins, outs, sems):
        send_sems, recv_sems = sems
        x, y, c, chips = _place()
        sibling = (x, y, 1 - c)
        sends, waits = [], []
        for w in range(n):
            for k, (cx, cy) in enumerate(chips):
                sem = (send_sems.at[3 * w + k], recv_sems.at[3 * w + k])
                landed = _gather_slot(shard_shapes[w], outs[w], 2 * cx + cy, c)
                other = _gather_slot(shard_shapes[w], outs[w], 2 * cx + cy, 1 - c)
                sends.append(_rcopy(landed, landed, *sem, sibling))
                waits.append(_rcopy(other, other, *sem, sibling))
        return sends, waits

    def start(ins, outs, sems):
        for cp in copies(ins, outs, sems)[0]:
            cp.start()

    def finish(ins, outs, sems):
        sends, waits = copies(ins, outs, sems)
        for cp in waits:
            cp.wait_recv()
        for cp in sends:
            cp.wait_send()

    return _Hosted(bufs, [_sds(b.shape, b.dtype) for b in bufs], 3 * n, start, finish, aliases={w: w for w in range(n)})


def _plan_swap(grads):
    def copies(ins, outs, sems):
        send_sems, recv_sems = sems
        x, y, c, _ = _place()
        cps = []
        for w, g_ref in enumerate(ins):
            if len(g_ref.shape) == 4:
                theirs = g_ref.at[:, 1 - c]
            else:
                hr = g_ref.shape[0] // 2
                theirs = g_ref.at[pl.ds((1 - c) * hr, hr), :]
            cps.append(_rcopy(theirs, outs[w], send_sems.at[w], recv_sems.at[w], (x, y, 1 - c)))
        return cps

    def start(ins, outs, sems):
        for cp in copies(ins, outs, sems):
            cp.start()

    def finish(ins, outs, sems):
        for cp in copies(ins, outs, sems):
            cp.wait()

    out_shapes = [_sds((g.shape[0], g.shape[2], g.shape[3])) if g.ndim == 4 else _sds((g.shape[0] // 2, g.shape[1]))
                  for g in grads]
    return _Hosted(grads, out_shapes, len(grads), start, finish)


def _plan_scatter(chip_sums, windows):
    def copies(ins, outs, sems):
        send_sems, recv_sems = sems
        x, y, c, chips = _place()
        me = 2 * x + y
        sends, waits = [], []
        for w, s_ref in enumerate(ins):
            for k, (cx, cy) in enumerate(chips):
                tgt = 2 * cx + cy
                if windows[w] is not None:
                    stride, width = windows[w]
                    part = s_ref.at[:, pl.ds(pl.multiple_of(tgt * stride, LANES), width)]
                else:
                    part = s_ref.at[tgt]
                sem = (send_sems.at[3 * w + k], recv_sems.at[3 * w + k])
                sends.append(_rcopy(part, outs[w].at[me], *sem, (cx, cy, c)))
                slot = outs[w].at[tgt]
                waits.append(_rcopy(slot, slot, *sem, (cx, cy, c)))
        return sends, waits

    def start(ins, outs, sems):
        for cp in copies(ins, outs, sems)[0]:
            cp.start()

    def finish(ins, outs, sems):
        sends, waits = copies(ins, outs, sems)
        for cp in waits:
            cp.wait_recv()
        for cp in sends:
            cp.wait_send()

    out_shapes = [_sds((N_CHIPS, s.shape[0], win[1]), BF16) if win is not None else _sds(s.shape, BF16)
                  for s, win in zip(chip_sums, windows)]
    return _Hosted(chip_sums, out_shapes, 3 * len(chip_sums), start, finish)


def _plan_join(reds):
    def copies(ins, outs, sems):
        send_sems, recv_sems = sems
        x, y, c, _ = _place()
        sends, waits = [], []
        for w, out in enumerate(outs):
            hr = out.shape[0] // 2
            mine = out.at[pl.ds(c * hr, hr), :]
            theirs = out.at[pl.ds((1 - c) * hr, hr), :]
            sends.append(_rcopy(mine, mine, send_sems.at[w], recv_sems.at[w], (x, y, 1 - c)))
            waits.append(_rcopy(theirs, theirs, send_sems.at[w], recv_sems.at[w], (x, y, 1 - c)))
        return sends, waits

    def start(ins, outs, sems):
        for cp in copies(ins, outs, sems)[0]:
            cp.start()

    def finish(ins, outs, sems):
        sends, waits = copies(ins, outs, sems)
        for cp in waits:
            cp.wait_recv()
        for cp in sends:
            cp.wait_send()

    return _Hosted(reds, [_sds(r.shape) for r in reds], len(reds), start, finish, aliases={w: w for w in range(len(reds))})


def _allreduce_small(v):
    m_per = v.shape[0]

    def body(v_ref, out_ref, all_ref, send_sems, recv_sems, local_sem):
        x, y, c, chips = _place()
        me, sibling = (x, y, c), (x, y, 1 - c)

        def rows(px, py, pc):
            return all_ref.at[pl.ds((4 * px + 2 * py + pc) * m_per, m_per), :]

        def copy(k, block, to, src=None):
            return _rcopy(rows(*block) if src is None else src, rows(*block), send_sems.at[k], recv_sems.at[k], to)

        mine = pltpu.make_async_copy(v_ref, rows(*me), local_sem)
        mine.start()
        first = [copy(0, me, sibling, src=v_ref)]
        first += [copy(1 + k, me, (*chip, c), src=v_ref) for k, chip in enumerate(chips)]
        for cp in first:
            cp.start()
        passed = [copy(4 + k, (*chip, c), sibling) for k, chip in enumerate(chips)]
        for k, chip in enumerate(chips):
            copy(1 + k, (*chip, c), me).wait_recv()
            passed[k].start()
        copy(0, sibling, me).wait_recv()
        for k, chip in enumerate(chips):
            copy(4 + k, (*chip, 1 - c), me).wait_recv()
        for cp in first + passed:
            cp.wait_send()
        mine.wait()
        acc = all_ref[pl.ds(0, m_per), :]
        for d in range(1, 8):
            acc = acc + all_ref[pl.ds(d * m_per, m_per), :]
        out_ref[...] = acc

    vm = pl.BlockSpec(memory_space=pltpu.VMEM)
    return _pallas(body, name="allreduce_small", in_specs=[vm], out_specs=vm, out_shape=_sds((m_per, LANES)),
                          scratch_shapes=[pltpu.VMEM((8 * m_per, LANES), F32), pltpu.SemaphoreType.DMA((7,)),
                                          pltpu.SemaphoreType.DMA((7,)), pltpu.SemaphoreType.DMA],
                          compiler_params=pltpu.CompilerParams(vmem_limit_bytes=VMEM_LIMIT))(v)


def _to_heads(t):
    s = t.shape[0]
    return t.reshape(s, HEADS, HEAD_DIM).transpose(1, 0, 2)


def _from_heads(t):
    s = t.shape[1]
    return t.transpose(1, 0, 2).reshape(s, HEADS * HEAD_DIM)


def _reorder_in_cols(w):
    pad = jnp.zeros((w.shape[0], Z_COLS - IN_COLS), w.dtype)
    return jnp.concatenate([w[:, :3 * ATTN_W], w[:, 3 * ATTN_W + HEADS:], w[:, 3 * ATTN_W:3 * ATTN_W + HEADS], pad], axis=1)


def _restore_in_cols(w):
    return jnp.concatenate([w[:, :3 * ATTN_W], w[:, F_COL0:F_COL0 + HEADS], w[:, U_COL0:U_COL0 + SSM_W]], axis=1)


def _block_diag(blocks):
    j, g, a, b = blocks.shape
    eye = jnp.eye(g, dtype=bool)[None, :, None, :, None]
    return jnp.where(eye, blocks[:, :, :, None, :], jnp.zeros((), blocks.dtype)).reshape(j, g * a, g * b)


def _diag_blocks(m, a, b):
    j = m.shape[0]
    g = m.shape[1] // a
    t = m.reshape(j, g, a, g, b)
    eye = jnp.eye(g, dtype=bool)[None, :, None, :, None]
    return jnp.sum(jnp.where(eye, t, 0.0), axis=3)


def _pack_rows(parts, rows, dtype):
    used = sum(p.shape[0] for p in parts)
    return jnp.concatenate([p.astype(dtype) for p in parts] + [jnp.zeros((rows - used, D_MODEL), dtype)], axis=0)


_SMALL = (("g_mix", (1024,)), ("b_f", (8,)), ("g_q", (64,)), ("g_k", (64,)), ("lambda_re", (32, 64)),
          ("lambda_im", (32, 64)), ("log_step", (32,)), ("b_re", (32, 64, 16)), ("b_im", (32, 64, 16)),
          ("c_re", (32, 16, 64)), ("c_im", (32, 16, 64)), ("d_skip", (32, 16)), ("b_glu", (512,)),
          ("g_attn_out", (512,)), ("g_ssm_out", (512,)), ("g_ffn", (1024,)), ("conv_b", (5632,)))


def _small_rows(shape):
    return -(-math.prod(shape) // LANES)


def _pack_small(arrs, extra=()):
    parts = []
    for a in list(arrs) + list(extra):
        flat = a.reshape(-1)
        rows = -(-flat.shape[0] // LANES)
        parts.append(jnp.pad(flat, (0, rows * LANES - flat.shape[0])).reshape(rows, LANES))
    total = sum(p.shape[0] for p in parts)
    pad = -total % SUBLANES
    if pad:
        parts.append(jnp.zeros((pad, LANES), F32))
    return jnp.concatenate(parts, axis=0)


def _unpack_small(buf, shapes):
    out, r = [], 0
    for shape in shapes:
        n = math.prod(shape)
        rows = -(-n // LANES)
        out.append(buf[r:r + rows].reshape(-1)[:n].reshape(shape))
        r += rows
    return out


def _halves(t):
    return t.reshape(N_CHIPS, 2, t.shape[0] // (2 * N_CHIPS), t.shape[1])


class _MeshComm:
    def __init__(self, args):
        x, y, self.core = lax.axis_index("x"), lax.axis_index("y"), lax.axis_index("c")
        self.chip = 2 * x + y
        self.place = jnp.stack([self.chip, self.core]).astype(jnp.int32)
        self.shards = {n: args[n].astype(BF16) for n in ("w_in", "w_glu", "w_out", "w_up", "w_down")}
        self.conv_w = args["conv_w"]

    def _own(self, stacked, mine):
        return lax.dynamic_update_slice(stacked, mine[None], (self.chip,) + (0,) * mine.ndim)

    def w_in(self):
        sh = self.shards["w_in"]
        (buf,) = _run_hosted(_plan_gather_ici([sh], [False]), name="gather_w_in")
        (buf,) = _run_hosted(_plan_gather_d2d([buf], [sh]), name="pass_w_in")
        return _reorder_in_cols(self._own(buf, sh).transpose(1, 0, 2).reshape(D_MODEL, IN_COLS))

    def gather_first(self):
        self.mid = [self.shards[n] for n in ("w_glu", "w_out", "w_up")]
        return _plan_gather_ici(self.mid, [False, False, True], whole=[self.conv_w], own_cols=[2])

    def gather_second(self, landed):
        self.g_cw = landed[3]
        return _both(_plan_gather_d2d(list(landed[:3]), self.mid), _plan_gather_ici([self.shards["w_down"]], [False]))

    def weights(self, gathered):
        g_glu, g_out, w_up_b = gathered[:3]
        own = self._own
        return (own(g_glu, self.mid[0]).reshape(SSM_W, SSM_W), own(g_out, self.mid[1]).reshape(D_MODEL, D_MODEL), w_up_b,
                own(self.g_cw, self.conv_w).transpose(1, 0, 2).reshape(3, 2 * D_FF))

    def gather_third(self, gathered):
        return _plan_gather_d2d([gathered[3]], [self.shards["w_down"]])

    def w_down(self, passed):
        return self._own(passed[0], self.shards["w_down"]).reshape(D_FF, D_MODEL)

    def swap(self, d_w_down, d_w_up, d_w_glu, d_w_out):
        self.early = [_halves(d_w_down), d_w_up, _halves(d_w_glu), _halves(d_w_out)]
        return _plan_swap(self.early)

    def scatter(self, landed):
        self.early_sums = [_add_half(g, l, self.place, name="add_" + n)
                           for g, l, n in zip(self.early, landed, ("w_down", "w_up", "w_glu", "w_out"))]
        return _plan_scatter(self.early_sums, [None, (UP_COLS, UP_COLS), None, None])

    def reduce(self, early_lands, d_w_in):
        d_in = jnp.pad(d_w_in, ((0, 0), (0, Z_COLS - IN_COLS)))
        (landed,) = _run_hosted(_plan_swap([d_in]), name="swap_halves")
        sum_in = _add_half(d_in, landed, self.place, name="add_w_in")
        (land_in,) = _run_hosted(_plan_scatter([sum_in], [(IN_STRIDE, IN_WINDOW)]), name="scatter_chips")
        es, el = self.early_sums, early_lands
        todo = [(sum_in, land_in, "w_in", LANES, IN_STRIDE), (es[2], el[2], "w_glu", SSM_W, 0),
                (es[3], el[3], "w_out", D_MODEL, 0), (es[1], el[1], "w_up", UP_COLS, UP_COLS),
                (es[0], el[0], "w_down", D_MODEL, 0)]
        reds = _run_hosted(_plan_join([_sum_chips(s, l, self.place, name="sum_" + n, tc=tc, window_stride=st)
                                       for s, l, n, tc, st in todo]), name="join_halves")
        g_big = dict(zip(("w_in", "w_glu", "w_out", "w_up", "w_down"), reds))
        g_big["w_in"] = lax.dynamic_slice_in_dim(reds[0], 2 * self.chip, IN_COLS // N_CHIPS, axis=1)
        return g_big


def _local_step(x, tgt, p, comm):
    s = x.shape[0]
    row = lambda v: v.reshape(1, -1)
    g_mix, g_ffn = row(p["g_mix"]), row(p["g_ffn"])
    g_att, g_ssm, b_glu, conv_b = row(p["g_attn_out"]), row(p["g_ssm_out"]), row(p["b_glu"]), row(p["conv_b"])
    gq = row(jnp.tile(p["g_q"], HEADS))
    gk = row(jnp.tile(p["g_k"], HEADS))
    bf = row(jnp.pad(p["b_f"], (0, LANES - HEADS)))
    gg = jnp.kron(jnp.eye(HEADS, dtype=F32), jnp.ones((HEAD_DIM, HEAD_DIM), F32)).astype(BF16)
    dsk = row(p["d_skip"])

    rep = lambda a: jnp.repeat(a, SSM_GROUP, axis=0)
    lr, li = rep(p["lambda_re"]), rep(p["lambda_im"])
    ls = rep(jnp.broadcast_to(p["log_step"][:, None], (SSM_GROUPS, SSM_STATE)))
    bt_re = p["b_re"].transpose(0, 2, 1).reshape(_PARAM_SHAPE)
    bt_im = p["b_im"].transpose(0, 2, 1).reshape(_PARAM_SHAPE)
    a_re_rep, a_im_rep, bb_re, bb_im = _ssm_params(lr, li, ls, bt_re, bt_im)
    ar = a_re_rep[::SSM_GROUP].reshape(SSM_CHUNKS, 1, CHUNK_S)
    ai = a_im_rep[::SSM_GROUP].reshape(SSM_CHUNKS, 1, CHUNK_S)
    chunked = lambda t: t.reshape(SSM_CHUNKS, SSM_GROUPS // SSM_CHUNKS, SSM_GROUP, SSM_STATE)
    bbr = _block_diag(chunked(bb_re)).astype(BF16)
    bbi = _block_diag(chunked(bb_im)).astype(BF16)
    to_cc = lambda c: _block_diag(chunked(c).transpose(0, 1, 3, 2)).astype(BF16)
    ccr, cci = to_cc(p["c_re"]), to_cc(p["c_im"])

    w_in_r = comm.w_in()
    hb, z = _in_proj(x, g_mix, w_in_r)
    qn, kn, vb, ub, c128 = _attn_prep(z, gq, gk, bf, gg)
    qh, kh, vh = _to_heads(qn), _to_heads(kn), _to_heads(vb)
    crow = c128[:, :HEADS].T.reshape(HEADS, 1, s)
    (oh, lse), landed = _attn_fwd(qh, kh, vh, crow, comm.gather_first())
    att = _from_heads(oh)
    (xr, xi, y), gathered = _ssm_fwd(ub, z, bbr, bbi, ar, ai, ccr, cci, dsk, comm.gather_second(landed))
    w_glu_b, w_out_b, w_up_b, conv_w_full = comm.weights(gathered)
    (x1, mixb, h2b), passed = _mix_out(y, att, x, w_glu_b, b_glu, g_att, g_ssm, w_out_b, g_ffn, comm.gather_third(gathered))
    w_down_b = comm.w_down(passed)
    up = _mm(h2b, w_up_b, name="ffn_up", tm=1024, tn=1408, tk=1024)
    act = _conv_act(up, conv_w_full, conv_b)
    dy, dyb, loss_blk = _down_loss(act, w_down_b, x1, tgt)

    d_w_down = _mm(act, dyb, ta=True, name="d_w_down", tm=1408, tn=1024, tk=2048)
    dact = _mm(dyb, w_down_b, tb=True, name="d_act", tm=1024, tn=1408, tk=1024)
    dupb, dcw = _conv_act_bwd(up, dact, conv_w_full, conv_b)
    d_w_up = _mm(h2b, dupb, ta=True, b_parts=2, name="d_w_up", tm=1024, tn=1408, tk=2048)
    dh2 = _mm(dupb, w_up_b, tb=True, a_parts=2, name="d_h2", tm=1024, tn=1024, tk=1408)
    dx1, dx1b, datt, dys, d_w_glu, d_g_ffn, d_g_att, d_g_ssm, d_b_glu = _mix_bwd(
        dy, dh2, x1, g_ffn, w_out_b, y, att, w_glu_b, b_glu, g_att, g_ssm)
    d_w_out = _mm(mixb, dx1b, ta=True, name="d_w_out", tm=1024, tn=1024, tk=2048)
    (du, dbbr, dbbi, dccr, dcci, dar, dai, dd), swapped = _ssm_bwd(dys, z, ub, xr, xi, bbr, bbi, ar, ai, ccr, cci, dsk,
                                                                comm.swap(d_w_down, d_w_up, d_w_glu, d_w_out))
    doh = _to_heads(datt)
    (dqh, dkh, dvh, dcrow), early_lands = _attn_bwd(qh, kh, vh, crow, lse, doh, comm.scatter(swapped))
    dc128 = jnp.pad(dcrow.reshape(HEADS, s).T, ((0, 0), (0, LANES - HEADS)))
    dzb, d_gq, d_gk, d_bf = _prep_bwd(z, _from_heads(dqh), _from_heads(dkh), _from_heads(dvh), du, dc128, gq, gk, bf, gg)
    d_w_in_r = _mm(hb, dzb, ta=True, name="d_w_in", tm=512, tn=Z_COLS, tk=2048)
    dh = _mm(dzb, w_in_r, tb=True, name="d_h", tm=1024, tn=1024, tk=Z_COLS)
    dx, d_g_mix = _in_norm_bwd(x, g_mix, dh, dx1)

    unchunk = lambda t: t.reshape(_PARAM_SHAPE)
    dbb_re = unchunk(_diag_blocks(dbbr, SSM_GROUP, SSM_STATE))
    dbb_im = unchunk(_diag_blocks(dbbi, SSM_GROUP, SSM_STATE))
    first_row = (jnp.arange(_PARAM_SHAPE[0]) % SSM_GROUP == 0)[:, None]
    da_re = jnp.where(first_row, rep(dar.reshape(SSM_GROUPS, SSM_STATE)), 0.0)
    da_im = jnp.where(first_row, rep(dai.reshape(SSM_GROUPS, SSM_STATE)), 0.0)
    expand_t = (jnp.arange(SSM_GROUPS)[:, None] == (jnp.arange(_PARAM_SHAPE[0]) // SSM_GROUP)[None, :]).astype(BF16)
    d_lr, d_li, d_ls, d_bt_re, d_bt_im = _ssm_params_bwd(lr, li, ls, bt_re, bt_im, da_re, da_im, dbb_re, dbb_im, expand_t)
    from_bt = lambda t: t.reshape(SSM_GROUPS, SSM_GROUP, SSM_STATE).transpose(0, 2, 1)
    from_cc = lambda t: _diag_blocks(t, SSM_STATE, SSM_GROUP).transpose(0, 1, 3, 2).reshape(SSM_GROUPS, SSM_GROUP, SSM_STATE)

    small = {
        "g_mix": d_g_mix, "b_f": d_bf[0, :HEADS], "g_q": d_gq.reshape(HEADS, HEAD_DIM).sum(0),
        "g_k": d_gk.reshape(HEADS, HEAD_DIM).sum(0), "lambda_re": d_lr, "lambda_im": d_li, "log_step": d_ls,
        "b_re": from_bt(d_bt_re), "b_im": from_bt(d_bt_im), "c_re": from_cc(dccr), "c_im": from_cc(dcci),
        "d_skip": dd, "b_glu": d_b_glu, "g_attn_out": d_g_att, "g_ssm_out": d_g_ssm, "g_ffn": d_g_ffn,
        "conv_b": dcw[:, 3],
    }
    big = {"w_in": _restore_in_cols(d_w_in_r), "w_glu": d_w_glu, "w_out": d_w_out, "w_up": d_w_up, "w_down": d_w_down}
    return loss_blk[0, 0], dx, big, small, dcw[:, 0:3].transpose(1, 0, 2).reshape(3, 2 * D_FF), early_lands


def kernel(x, g_mix, w_in, b_f, g_q, g_k, lambda_re, lambda_im, log_step, b_re, b_im, c_re, c_im, d_skip, w_glu, b_glu, g_attn_out, g_ssm_out, w_out, g_ffn, w_up, conv_w, conv_b, w_down, loss_target, m_g_mix, m_w_in, m_b_f, m_g_q, m_g_k, m_lambda_re, m_lambda_im, m_log_step, m_b_re, m_b_im, m_c_re, m_c_im, m_d_skip, m_w_glu, m_b_glu, m_g_attn_out, m_g_ssm_out, m_w_out, m_g_ffn, m_w_up, m_conv_w, m_conv_b, m_w_down, v_g_mix, v_w_in, v_b_f, v_g_q, v_g_k, v_lambda_re, v_lambda_im, v_log_step, v_b_re, v_b_im, v_c_re, v_c_im, v_d_skip, v_w_glu, v_b_glu, v_g_attn_out, v_g_ssm_out, v_w_out, v_g_ffn, v_w_up, v_conv_w, v_conv_b, v_w_down):
    args = dict(locals())
    order = ["g_mix", "w_in", "b_f", "g_q", "g_k", "lambda_re", "lambda_im", "log_step", "b_re", "b_im", "c_re", "c_im",
             "d_skip", "w_glu", "b_glu", "g_attn_out", "g_ssm_out", "w_out", "g_ffn", "w_up", "conv_w", "conv_b", "w_down"]
    comm = _MeshComm(args)
    chip = comm.chip
    loss_part, dx, big, small, d_conv_w, early_lands = _local_step(x[0], loss_target[0], args, comm)
    loss = lax.psum(loss_part, ("x", "y", "c"))

    g_big = comm.reduce(early_lands, big["w_in"])

    small_names = [n for n, _ in _SMALL]
    small_shapes = [sh for _, sh in _SMALL]
    gsum = _allreduce_small(_pack_small([small[n] for n in small_names], extra=[d_conv_w]))
    g_small = _unpack_small(gsum, small_shapes + [(3, 2 * D_FF)])
    g_conv_w = lax.dynamic_slice_in_dim(g_small[-1], chip * (2 * D_FF // N_CHIPS), 2 * D_FF // N_CHIPS, axis=1)
    g_small = dict(zip(small_names, g_small[:-1]))

    grad, delta, new_m, new_v = {}, {}, {}, {}
    for n in ("w_in", "w_glu", "w_out", "w_up", "w_down"):
        grad[n] = g_big[n]
        delta[n], new_m[n], new_v[n] = _adamw(args[n], g_big[n], args["m_" + n], args["v_" + n], name="adamw_" + n)
    grad["conv_w"] = g_conv_w
    delta["conv_w"], new_m["conv_w"], new_v["conv_w"] = _adamw(conv_w, g_conv_w, m_conv_w, v_conv_w, name="adamw_conv_w")
    stepped = _adamw_small([args[n] for n in small_names], [g_small[n] for n in small_names],
                           [args["m_" + n] for n in small_names], [args["v_" + n] for n in small_names])
    for i, n in enumerate(small_names):
        grad[n] = g_small[n]
        delta[n], new_m[n], new_v[n] = stepped[3 * i:3 * i + 3]

    return (loss, dx[None], *[grad[n] for n in order], *[delta[n] for n in order], *[new_m[n] for n in order],
            *[new_v[n] for n in order])
```

```python
import math

import jax
import jax.numpy as jnp
from jax import lax
from jax.experimental import pallas as pl
from jax.experimental.pallas import tpu as pltpu

F32 = jnp.float32
BF16 = jnp.bfloat16

D_MODEL = 1024
HEADS = 8
HEAD_DIM = 64
ATTN_W = 512
SSM_W = 512
SSM_GROUPS = 32
SSM_GROUP = 16
SSM_STATE = 64
N_STATE = SSM_GROUPS * SSM_STATE
D_FF = 2816
IN_COLS = 2056
Z_COLS = 2176
U_COL0 = 1536
F_COL0 = 2048
EPS = 1e-6
NEG_INF = -1e30
N_CHIPS = 4
LANES = 128
SUBLANES = 8
SSM_CHUNKS = 4
CHUNK_U = SSM_W // SSM_CHUNKS
CHUNK_S = N_STATE // SSM_CHUNKS
HEADS_PER_STEP = 2
STRIP = 128
N_STRIPS = D_FF // STRIP

ROWS_IN, ROWS_GLU, ROWS_OUT, ROWS_UP, ROWS_DOWN = 514, 64, 256, 1408, 704
OFF_GLU = ROWS_IN
OFF_OUT = OFF_GLU + ROWS_GLU
OFF_UP = OFF_OUT + ROWS_OUT
OFF_DOWN = OFF_UP + ROWS_UP
OFF_SPARE = OFF_DOWN + ROWS_DOWN
PACK_ROWS = 2976
HALF_ROWS = PACK_ROWS // 2
CONVW_ROWS = 9

ADAM_LR = 0.001
ADAM_B1 = 0.9
ADAM_B2 = 0.999
ADAM_EPS = 1e-08
ADAM_WD = 0.01
ADAM_STEP = 10

VMEM_LIMIT = 56 * 1024 * 1024
MESH = pl.DeviceIdType.MESH


def _pallas(body, **kw):
    return pl.pallas_call(body, **kw)


def _pcall(body, *, name, out_shape, in_specs, out_specs, grid=(), scratch_shapes=(), dims=None):
    params = pltpu.CompilerParams(dimension_semantics=dims, vmem_limit_bytes=VMEM_LIMIT)
    return _pallas(body, name=name, grid=grid, in_specs=in_specs, out_specs=out_specs,
                   out_shape=out_shape, scratch_shapes=scratch_shapes, compiler_params=params)


def _sds(shape, dtype=F32):
    return jax.ShapeDtypeStruct(shape, dtype)


def _dot(a, b):
    return jnp.dot(a, b, preferred_element_type=F32)


def _dot_nt(a, b):
    return lax.dot_general(a, b, (((1,), (1,)), ((), ())), preferred_element_type=F32)


def _dot_tn(a, b):
    return lax.dot_general(a, b, (((0,), (0,)), ((), ())), preferred_element_type=F32)


def _split3(x):
    hi = x.astype(BF16)
    r = x - hi.astype(F32)
    mid = r.astype(BF16)
    lo = (r - mid.astype(F32)).astype(BF16)
    return hi, mid, lo


def _dot_exact_r(x, m01):
    hi, mid, lo = _split3(x)
    return _dot(hi, m01) + _dot(mid, m01) + _dot(lo, m01)


def _dot_exact_l(m01, x):
    hi, mid, lo = _split3(x)
    return _dot(m01, hi) + _dot(m01, mid) + _dot(m01, lo)


def _sigmoid(x):
    return 1.0 / (1.0 + jnp.exp(-x))


def _rms(x, g):
    r = lax.rsqrt(jnp.mean(x * x, axis=-1, keepdims=True) + EPS)
    return x * r * g


def _rms_bwd(x, g, dy):
    r = lax.rsqrt(jnp.mean(x * x, axis=-1, keepdims=True) + EPS)
    w = dy * g
    dx = r * w - x * (r * r * r) * jnp.mean(w * x, axis=-1, keepdims=True)
    dg = jnp.sum(dy * x * r, axis=0, keepdims=True)
    return dx, dg


_GELU_K = math.sqrt(2.0 / math.pi)
_GELU_C = 0.044715


def _gelu(y):
    return y * (0.5 * (1.0 + jnp.tanh(_GELU_K * (y + _GELU_C * (y * y * y)))))


def _gelu_grad(y):
    t = jnp.tanh(_GELU_K * (y + _GELU_C * (y * y * y)))
    return 0.5 * (1.0 + t) + 0.5 * y * (1.0 - t * t) * (_GELU_K * (1.0 + 3.0 * _GELU_C * y * y))


def _tile(n, pref):
    if n <= pref:
        return n
    divs = [t for t in range(LANES, n + 1, LANES) if n % t == 0]
    below = [t for t in divs if t <= pref]
    if below and 2 * below[-1] >= pref:
        return below[-1]
    above = [t for t in divs if t > pref]
    return above[0] if above else n


def _row_tile(s):
    return min(256, s)


def _mm(a, b, *, name, tm, tn, tk, ta=False, tb=False, a_parts=1, b_parts=1):
    if a_parts > 1:
        m, kk = a.shape[1], a.shape[2] * a_parts
    elif ta:
        kk, m = a.shape
    else:
        m, kk = a.shape
    if b_parts > 1:
        n = b.shape[2] * b_parts
    else:
        n = b.shape[0] if tb else b.shape[1]
    tm, tn, tk = _tile(m, tm), _tile(n // b_parts, tn), _tile(kk // a_parts, tk)
    k_per, n_per = kk // a_parts // tk, n // b_parts // tn

    def body(a_ref, b_ref, o_ref):
        k = pl.program_id(2)
        if ta:
            part = _dot_tn(a_ref[...], b_ref[...])
        elif tb:
            part = _dot_nt(a_ref[...], b_ref[...])
        else:
            part = _dot(a_ref[...], b_ref[...])

        @pl.when(k == 0)
        def _():
            o_ref[...] = part

        @pl.when(k > 0)
        def _():
            o_ref[...] += part

    if a_parts > 1:
        a_spec = pl.BlockSpec((None, tm, tk), lambda i, j, k: (k // k_per, i, k % k_per))
    else:
        a_spec = pl.BlockSpec((tk, tm), lambda i, j, k: (k, i)) if ta else pl.BlockSpec((tm, tk), lambda i, j, k: (i, k))
    if b_parts > 1:
        b_spec = pl.BlockSpec((None, tk, tn), lambda i, j, k: (j // n_per, k, j % n_per))
    else:
        b_spec = pl.BlockSpec((tn, tk), lambda i, j, k: (j, k)) if tb else pl.BlockSpec((tk, tn), lambda i, j, k: (k, j))
    return _pcall(body, name=name, grid=(m // tm, n // tn, kk // tk), in_specs=[a_spec, b_spec],
                  out_specs=pl.BlockSpec((tm, tn), lambda i, j, k: (i, j)), out_shape=_sds((m, n)),
                  dims=("parallel", "parallel", "arbitrary"))(a, b)


def _in_proj(x, g_mix, w_in_r):
    s = x.shape[0]
    tm = _row_tile(s)

    def body(x_ref, g_ref, w_ref, h_ref, z_ref):
        h = _rms(x_ref[...], g_ref[...]).astype(BF16)
        h_ref[...] = h
        z_ref[...] = _dot(h, w_ref[...])

    return _pcall(body, name="in_proj", grid=(s // tm,),
                  in_specs=[pl.BlockSpec((tm, D_MODEL), lambda i: (i, 0)), pl.BlockSpec((1, D_MODEL), lambda i: (0, 0)),
                            pl.BlockSpec((D_MODEL, Z_COLS), lambda i: (0, 0))],
                  out_specs=[pl.BlockSpec((tm, D_MODEL), lambda i: (i, 0)), pl.BlockSpec((tm, Z_COLS), lambda i: (i, 0))],
                  out_shape=[_sds((s, D_MODEL), BF16), _sds((s, Z_COLS))], dims=("parallel",))(x, g_mix, w_in_r)


def _split_heads(ref, val):
    for h in range(HEADS):
        ref[h] = val[:, h * HEAD_DIM:(h + 1) * HEAD_DIM].astype(ref.dtype)


def _merge_heads(ref):
    return jnp.concatenate([ref[h].astype(F32) for h in range(HEADS)], axis=-1)


def _attn_prep(z, gq, gk, bf, gg):
    s = z.shape[0]
    tm = _row_tile(s)

    def body(z_ref, gq_ref, gk_ref, bf_ref, gg_ref, qn_ref, kn_ref, vb_ref, ub_ref, c_ref, carry_ref):
        i = pl.program_id(0)

        @pl.when(i == 0)
        def _():
            carry_ref[...] = jnp.zeros_like(carry_ref)

        gg_m = gg_ref[...]

        def head_norm(t, g):
            ssq = _dot_exact_r(t * t, gg_m)
            return t * lax.rsqrt(ssq * (1.0 / HEAD_DIM) + EPS) * g

        _split_heads(qn_ref, head_norm(z_ref[:, 0:ATTN_W], gq_ref[...]))
        _split_heads(kn_ref, head_norm(z_ref[:, ATTN_W:2 * ATTN_W], gk_ref[...]))
        _split_heads(vb_ref, z_ref[:, 2 * ATTN_W:3 * ATTN_W])
        ub_ref[...] = z_ref[:, U_COL0:U_COL0 + SSM_W].astype(BF16)
        fl = z_ref[:, F_COL0:F_COL0 + LANES] + bf_ref[...]
        lf = jnp.minimum(fl, 0.0) - jnp.log1p(jnp.exp(-jnp.abs(fl)))
        row = lax.broadcasted_iota(jnp.int32, (tm, tm), 0)
        col = lax.broadcasted_iota(jnp.int32, (tm, tm), 1)
        tri = (row >= col).astype(BF16)
        c = _dot_exact_l(tri, lf) + carry_ref[...]
        c_ref[...] = c
        carry_ref[...] = c[tm - 1:tm, :]

    row_spec = lambda w: pl.BlockSpec((tm, w), lambda i: (i, 0))
    const = lambda shape: pl.BlockSpec(shape, lambda i: (0, 0))
    heads = pl.BlockSpec((HEADS, tm, HEAD_DIM), lambda i: (0, i, 0))
    return _pcall(body, name="attn_prep", grid=(s // tm,),
                  in_specs=[row_spec(Z_COLS), const((1, ATTN_W)), const((1, ATTN_W)), const((1, LANES)), const((ATTN_W, ATTN_W))],
                  out_specs=[heads] * 3 + [row_spec(ATTN_W), row_spec(LANES)],
                  out_shape=[_sds((HEADS, s, HEAD_DIM), BF16)] * 3 + [_sds((s, ATTN_W), BF16), _sds((s, LANES))],
                  scratch_shapes=[pltpu.VMEM((1, LANES), F32)], dims=("arbitrary",))(z, gq, gk, bf, gg)


def _attn_fwd(qh, kh, vh, crow, hosted=None):
    _, s, _ = qh.shape
    tq = _row_tile(s)
    scale = HEAD_DIM ** -0.5

    hp = HEADS_PER_STEP
    nq = s // tq
    fold = lambda t, op: op(t[:, :tq // 2], t[:, tq // 2:])

    def body(q_ref, k_ref, v_ref, c_ref, o_ref, lse_ref, s_s):
        i = pl.program_id(1)

        def first(j, ms, diagonal):
            off = pl.multiple_of(j * tq, tq)
            out = []
            for hh in range(hp):
                sc = _dot_nt(q_ref[hh], k_ref[hh, pl.ds(off, tq), :]) * scale - c_ref[hh, :, pl.ds(off, tq)]
                if diagonal:
                    causal = lax.broadcasted_iota(jnp.int32, (tq, tq), 1) <= lax.broadcasted_iota(jnp.int32, (tq, tq), 0)
                    sc = jnp.where(causal, sc, NEG_INF)
                s_s[hh, j] = sc
                out.append(jnp.maximum(ms[hh], fold(sc, jnp.maximum)))
            return tuple(out)

        ms = lax.fori_loop(0, i, lambda j, c: first(j, c, False), (jnp.full((tq, tq // 2), NEG_INF, F32),) * hp)
        ms = [jnp.max(t, axis=-1, keepdims=True) for t in first(i, ms, True)]

        def second(j, carry):
            rows = pl.ds(pl.multiple_of(j * tq, tq), tq)
            out = []
            for hh in range(hp):
                ls, acc = carry[hh]
                p = jnp.exp(s_s[hh, j] - ms[hh])
                out.append((ls + fold(p, jnp.add), acc + _dot(p.astype(BF16), v_ref[hh, rows, :])))
            return tuple(out)

        zero = (jnp.zeros((tq, tq // 2), F32), jnp.zeros((tq, HEAD_DIM), F32))
        for hh, (ls, acc) in enumerate(lax.fori_loop(0, i + 1, second, (zero,) * hp)):
            l = jnp.sum(ls, axis=-1, keepdims=True)
            o_ref[hh] = acc / l
            lse_ref[hh] = ms[hh] + jnp.log(l)

    blk = pl.BlockSpec((hp, tq, HEAD_DIM), lambda h, i: (h, i, 0))
    full = pl.BlockSpec((hp, s, HEAD_DIM), lambda h, i: (h, 0, 0))
    nh = HEADS // hp
    first = lambda: jnp.logical_and(pl.program_id(0) == 0, pl.program_id(1) == 0)
    last = lambda: jnp.logical_and(pl.program_id(0) == nh - 1, pl.program_id(1) == nq - 1)
    return _host_pcall(body, hosted, first, last, n_in=4, n_out=2, n_scratch=1, name="attn_fwd", grid=(nh, nq),
                       in_specs=[blk, full, full, pl.BlockSpec((hp, 1, s), lambda h, i: (h, 0, 0))],
                       out_specs=[blk, pl.BlockSpec((hp, tq, 1), lambda h, i: (h, i, 0))],
                       out_shape=[_sds((HEADS, s, HEAD_DIM)), _sds((HEADS, s, 1))],
                       scratch_shapes=[pltpu.VMEM((hp, nq, tq, tq), F32)],
                       dims=("parallel", "parallel"), operands=(qh, kh, vh, crow))


def _ssm_param_fn(lr, li, ls, br, bi):
    step = jnp.exp(ls)
    er = jnp.exp(lr * step)
    ab_re = er * jnp.cos(li * step)
    ab_im = er * jnp.sin(li * step)
    num_re = ab_re - 1.0
    num_im = ab_im
    den = lr * lr + li * li
    f_re = (num_re * lr + num_im * li) / den
    f_im = (num_im * lr - num_re * li) / den
    bb_re = f_re * br - f_im * bi
    bb_im = f_re * bi + f_im * br
    return ab_re, ab_im, bb_re, bb_im


_PARAM_SHAPE = (SSM_GROUPS * SSM_GROUP, SSM_STATE)


def _ssm_params(lr, li, ls, br, bi):
    def body(lr_ref, li_ref, ls_ref, br_ref, bi_ref, ar_ref, ai_ref, bbr_ref, bbi_ref):
        ar, ai, bbr, bbi = _ssm_param_fn(lr_ref[...], li_ref[...], ls_ref[...], br_ref[...], bi_ref[...])
        ar_ref[...] = ar
        ai_ref[...] = ai
        bbr_ref[...] = bbr
        bbi_ref[...] = bbi

    spec = pl.BlockSpec(_PARAM_SHAPE, lambda: (0, 0))
    return _pcall(body, name="ssm_params", in_specs=[spec] * 5, out_specs=[spec] * 4,
                  out_shape=[_sds(_PARAM_SHAPE)] * 4)(lr, li, ls, br, bi)


def _ssm_params_bwd(lr, li, ls, br, bi, dar, dai, dbbr, dbbi, expand_t):
    def body(lr_ref, li_ref, ls_ref, br_ref, bi_ref, dar_ref, dai_ref, dbbr_ref, dbbi_ref, et_ref,
             dlr_ref, dli_ref, dls_ref, dbr_ref, dbi_ref):
        _, vjp = jax.vjp(_ssm_param_fn, lr_ref[...], li_ref[...], ls_ref[...], br_ref[...], bi_ref[...])
        dlr, dli, dls, dbr, dbi = vjp((dar_ref[...], dai_ref[...], dbbr_ref[...], dbbi_ref[...]))
        et = et_ref[...]
        dlr_ref[...] = _dot_exact_l(et, dlr)
        dli_ref[...] = _dot_exact_l(et, dli)
        dls_ref[...] = jnp.sum(_dot_exact_l(et, dls), axis=-1, keepdims=True)
        dbr_ref[...] = dbr
        dbi_ref[...] = dbi

    spec = pl.BlockSpec(_PARAM_SHAPE, lambda: (0, 0))
    gspec = pl.BlockSpec((SSM_GROUPS, SSM_STATE), lambda: (0, 0))
    return _pcall(body, name="ssm_params_bwd",
                  in_specs=[spec] * 9 + [pl.BlockSpec((SSM_GROUPS, _PARAM_SHAPE[0]), lambda: (0, 0))],
                  out_specs=[gspec, gspec, pl.BlockSpec((SSM_GROUPS, 1), lambda: (0, 0)), spec, spec],
                  out_shape=[_sds((SSM_GROUPS, SSM_STATE))] * 2 + [_sds((SSM_GROUPS, 1))] + [_sds(_PARAM_SHAPE)] * 2,
                  )(lr, li, ls, br, bi, dar, dai, dbbr, dbbi, expand_t)


def _cmul(ar, ai, br, bi):
    return ar * br - ai * bi, ar * bi + ai * br


def _scan_consts(ar, ai, width, reverse):
    row = lax.broadcasted_iota(jnp.int32, (SUBLANES, width), 0)
    pw = [(ar, ai)]
    for _ in range(SUBLANES - 1):
        pw.append(_cmul(pw[-1][0], pw[-1][1], ar, ai))
    steps = []
    for d in (1, 2, 4):
        keep = (row < SUBLANES - d) if reverse else (row >= d)
        steps.append((d, jnp.where(keep, pw[d - 1][0], 0.0), jnp.where(keep, pw[d - 1][1], 0.0)))
    pr = jnp.zeros((SUBLANES, width), F32)
    pi = jnp.zeros((SUBLANES, width), F32)
    for r in range(SUBLANES):
        e = (SUBLANES - r) if reverse else (r + 1)
        pr = jnp.where(row == r, pw[e - 1][0], pr)
        pi = jnp.where(row == r, pw[e - 1][1], pi)
    return steps, pr, pi


def _scan_tile(xr, xi, cr, ci, consts, reverse):
    steps, pr, pi = consts
    for d, mr, mi in steps:
        sh = (SUBLANES - d) if reverse else d
        sr = pltpu.roll(xr, sh, 0)
        si = pltpu.roll(xi, sh, 0)
        xr, xi = xr + mr * sr - mi * si, xi + mr * si + mi * sr
    return xr + pr * cr - pi * ci, xi + pr * ci + pi * cr


def _ssm_fwd(ub, z, bbr, bbi, ar, ai, ccr, cci, dsk, hosted=None):
    s = ub.shape[0]
    tm = _row_tile(s)
    nt = tm // SUBLANES

    def body(ub_ref, u_ref, bbr_ref, bbi_ref, ar_ref, ai_ref, ccr_ref, cci_ref, dsk_ref,
             xr_ref, xi_ref, y_ref, cr_s, ci_s):
        i = pl.program_id(1)

        @pl.when(i == 0)
        def _():
            cr_s[...] = jnp.zeros_like(cr_s)
            ci_s[...] = jnp.zeros_like(ci_s)

        u_b = ub_ref[...]
        xr_ref[...] = _dot(u_b, bbr_ref[0])
        xi_ref[...] = _dot(u_b, bbi_ref[0])
        consts = _scan_consts(ar_ref[0], ai_ref[0], CHUNK_S, False)

        def tile(k, carry):
            cr, ci = carry
            sl = pl.ds(pl.multiple_of(k * SUBLANES, SUBLANES), SUBLANES)
            xr, xi = _scan_tile(xr_ref[sl, :], xi_ref[sl, :], cr, ci, consts, False)
            xr_ref[sl, :] = xr
            xi_ref[sl, :] = xi
            return xr[SUBLANES - 1:SUBLANES, :], xi[SUBLANES - 1:SUBLANES, :]

        cr, ci = lax.fori_loop(0, nt, tile, (cr_s[...], ci_s[...]))
        cr_s[...] = cr
        ci_s[...] = ci
        y_ref[...] = (_dot(xr_ref[...].astype(BF16), ccr_ref[0]) - _dot(xi_ref[...].astype(BF16), cci_ref[0])
                      + dsk_ref[...] * u_ref[...])

    ucol0 = U_COL0 // CHUNK_U
    wspec = lambda a, b: pl.BlockSpec((1, a, b), lambda j, i: (j, 0, 0))
    nb = s // tm
    first = lambda: jnp.logical_and(pl.program_id(0) == 0, pl.program_id(1) == 0)
    last = lambda: jnp.logical_and(pl.program_id(0) == SSM_CHUNKS - 1, pl.program_id(1) == nb - 1)
    return _host_pcall(
        body, hosted, first, last, n_in=9, n_out=3, n_scratch=2, name="ssm_fwd", grid=(SSM_CHUNKS, nb),
        in_specs=[pl.BlockSpec((tm, CHUNK_U), lambda j, i: (i, j)),
                  pl.BlockSpec((tm, CHUNK_U), lambda j, i: (i, ucol0 + j)),
                  wspec(CHUNK_U, CHUNK_S), wspec(CHUNK_U, CHUNK_S), wspec(1, CHUNK_S), wspec(1, CHUNK_S),
                  wspec(CHUNK_S, CHUNK_U), wspec(CHUNK_S, CHUNK_U),
                  pl.BlockSpec((1, CHUNK_U), lambda j, i: (0, j))],
        out_specs=[pl.BlockSpec((tm, CHUNK_S), lambda j, i: (i, j)), pl.BlockSpec((tm, CHUNK_S), lambda j, i: (i, j)),
                   pl.BlockSpec((tm, CHUNK_U), lambda j, i: (i, j))],
        out_shape=[_sds((s, N_STATE)), _sds((s, N_STATE)), _sds((s, SSM_W))],
        scratch_shapes=[pltpu.VMEM((1, CHUNK_S), F32)] * 2,
        dims=("parallel", "arbitrary"), operands=(ub, z, bbr, bbi, ar, ai, ccr, cci, dsk))


def _ssm_glu(y, w_glu, b_glu):
    ge = _gelu(y)
    sg = _sigmoid(_dot(ge.astype(BF16), w_glu) + b_glu)
    return ge, sg


def _mix_out(y, att, x, w_glu, b_glu, g_att, g_ssm, w_out, g_ffn, hosted=None):
    s = x.shape[0]
    tm = _row_tile(s)

    def body(y_ref, att_ref, x_ref, wg_ref, bg_ref, ga_ref, gs_ref, wo_ref, gf_ref, x1_ref, mix_ref, h2_ref):
        ge, sg = _ssm_glu(y_ref[...], wg_ref[...], bg_ref[...])
        ms = _rms(ge * sg, gs_ref[...]).astype(BF16)
        ma = _rms(_merge_heads(att_ref), ga_ref[...]).astype(BF16)
        mix_ref[:, 0:ATTN_W] = ma
        mix_ref[:, ATTN_W:D_MODEL] = ms
        x1 = x_ref[...] + (_dot(ma, wo_ref[0:ATTN_W, :]) + _dot(ms, wo_ref[ATTN_W:D_MODEL, :]))
        x1_ref[...] = x1
        h2_ref[...] = _rms(x1, gf_ref[...]).astype(BF16)

    row = lambda w: pl.BlockSpec((tm, w), lambda i: (i, 0))
    const = lambda a, b: pl.BlockSpec((a, b), lambda i: (0, 0))
    nb = s // tm
    return _host_pcall(body, hosted, lambda: pl.program_id(0) == 0, lambda: pl.program_id(0) == nb - 1,
                       n_in=9, n_out=3, n_scratch=0, name="mix_out", grid=(nb,),
                       in_specs=[row(SSM_W), pl.BlockSpec((HEADS, tm, HEAD_DIM), lambda i: (0, i, 0)), row(D_MODEL),
                                 const(SSM_W, SSM_W), const(1, SSM_W),
                                 const(1, ATTN_W), const(1, SSM_W), const(D_MODEL, D_MODEL), const(1, D_MODEL)],
                       out_specs=[row(D_MODEL)] * 3,
                       out_shape=[_sds((s, D_MODEL)), _sds((s, D_MODEL), BF16), _sds((s, D_MODEL), BF16)],
                       scratch_shapes=[], dims=("parallel",), operands=(y, att, x, w_glu, b_glu, g_att, g_ssm, w_out, g_ffn))


CONV_CHUNK = 64


def _conv_rows(pad_ref, w, b, r0, n):
    y = b + pad_ref[pl.ds(r0 + SUBLANES - 2, n), :] * w[0:1, :]
    y = y + pad_ref[pl.ds(r0 + SUBLANES - 1, n), :] * w[1:2, :]
    return y + pad_ref[pl.ds(r0 + SUBLANES, n), :] * w[2:3, :]


def _fill_front_pad(pad_ref, strip_ref, s):
    pad_ref[0:SUBLANES, :] = jnp.zeros((SUBLANES, STRIP), F32)
    for r0 in range(0, s, CONV_CHUNK):
        pad_ref[pl.ds(SUBLANES + r0, CONV_CHUNK), :] = strip_ref[pl.ds(r0, CONV_CHUNK), :]


def _conv_act(up, conv_w, conv_b):
    s = up.shape[0]

    def body(ug_ref, uv_ref, wg_ref, wv_ref, bg_ref, bv_ref, act_ref, pg_ref, pv_ref):
        _fill_front_pad(pg_ref, ug_ref, s)
        _fill_front_pad(pv_ref, uv_ref, s)
        wg, wv, bg, bv = wg_ref[...], wv_ref[...], bg_ref[...], bv_ref[...]
        for r0 in range(0, s, CONV_CHUNK):
            hg = _conv_rows(pg_ref, wg, bg, r0, CONV_CHUNK)
            hv = _conv_rows(pv_ref, wv, bv, r0, CONV_CHUNK)
            act_ref[pl.ds(r0, CONV_CHUNK), :] = (hg * _sigmoid(hg) * hv).astype(BF16)

    strip = lambda off: pl.BlockSpec((s, STRIP), lambda j: (0, j + off))
    wsp = lambda off: pl.BlockSpec((3, STRIP), lambda j: (0, j + off))
    bsp = lambda off: pl.BlockSpec((1, STRIP), lambda j: (0, j + off))
    return _pcall(body, name="conv_act", grid=(N_STRIPS,),
                  in_specs=[strip(0), strip(N_STRIPS), wsp(0), wsp(N_STRIPS), bsp(0), bsp(N_STRIPS)],
                  out_specs=pl.BlockSpec((s, STRIP), lambda j: (0, j)), out_shape=_sds((s, D_FF), BF16),
                  scratch_shapes=[pltpu.VMEM((s + SUBLANES, STRIP), F32)] * 2,
                  dims=("parallel",))(up, up, conv_w, conv_w, conv_b, conv_b)


def _down_loss(act, w_down, x1, tgt):
    s = x1.shape[0]
    tm = _row_tile(s)

    def body(a_ref, w_ref, x1_ref, t_ref, dy_ref, dyb_ref, loss_ref):
        i = pl.program_id(0)

        @pl.when(i == 0)
        def _():
            loss_ref[...] = jnp.zeros_like(loss_ref)

        diff = x1_ref[...] + _dot(a_ref[...], w_ref[...]) - t_ref[...]
        dy = diff * (1.0 / D_MODEL)
        dy_ref[...] = dy
        dyb_ref[...] = dy.astype(BF16)
        loss_ref[...] += 0.5 * jnp.sum(diff * dy)

    row = lambda w: pl.BlockSpec((tm, w), lambda i: (i, 0))
    return _pcall(body, name="down_loss", grid=(s // tm,),
                  in_specs=[row(D_FF), pl.BlockSpec((D_FF, D_MODEL), lambda i: (0, 0)), row(D_MODEL), row(D_MODEL)],
                  out_specs=[row(D_MODEL), row(D_MODEL), pl.BlockSpec((SUBLANES, LANES), lambda i: (0, 0))],
                  out_shape=[_sds((s, D_MODEL)), _sds((s, D_MODEL), BF16), _sds((SUBLANES, LANES))],
                  dims=("arbitrary",))(act, w_down, x1, tgt)


def _conv_act_bwd(up, dact, conv_w, conv_b):
    s = up.shape[0]
    ch = CONV_CHUNK

    def body(ug_ref, uv_ref, da_ref, wg_ref, wv_ref, bg_ref, bv_ref, dup_ref, dcw_ref, pg_ref, pv_ref, dg_ref, dv_ref):
        _fill_front_pad(pg_ref, ug_ref, s)
        _fill_front_pad(pv_ref, uv_ref, s)
        zero = jnp.zeros((SUBLANES, STRIP), F32)
        dg_ref[pl.ds(s, SUBLANES), :] = zero
        dv_ref[pl.ds(s, SUBLANES), :] = zero
        wg, wv, bg, bv = wg_ref[...], wv_ref[...], bg_ref[...], bv_ref[...]
        tile_sum = lambda t: jnp.sum(t.reshape(ch // SUBLANES, SUBLANES, STRIP), axis=0)
        accs = [[zero] * 4, [zero] * 4]
        for r0 in range(0, s, ch):
            hg = _conv_rows(pg_ref, wg, bg, r0, ch)
            hv = _conv_rows(pv_ref, wv, bv, r0, ch)
            sg = _sigmoid(hg)
            da = da_ref[pl.ds(r0, ch), :]
            dhs = (da * hv * (sg * (1.0 + hg * (1.0 - sg))), da * (hg * sg))
            for half, (dh, d_ref, p_ref) in enumerate(zip(dhs, (dg_ref, dv_ref), (pg_ref, pv_ref))):
                d_ref[pl.ds(r0, ch), :] = dh
                for k in range(3):
                    accs[half][k] = accs[half][k] + tile_sum(dh * p_ref[pl.ds(r0 + SUBLANES - 2 + k, ch), :])
                accs[half][3] = accs[half][3] + tile_sum(dh)
        for half, (d_ref, w) in enumerate(((dg_ref, wg), (dv_ref, wv))):
            for r0 in range(0, s, ch):
                dup = (d_ref[pl.ds(r0, ch), :] * w[2:3, :] + d_ref[pl.ds(r0 + 1, ch), :] * w[1:2, :]
                       + d_ref[pl.ds(r0 + 2, ch), :] * w[0:1, :])
                dup_ref[half, pl.ds(r0, ch), :] = dup.astype(BF16)
            rid = lax.broadcasted_iota(jnp.int32, (SUBLANES, STRIP), 0)
            out = zero
            for k in range(4):
                out = jnp.where(rid == k, jnp.sum(accs[half][k], axis=0, keepdims=True), out)
            dcw_ref[half] = out

    strip = lambda off: pl.BlockSpec((s, STRIP), lambda j: (0, j + off))
    wsp = lambda off: pl.BlockSpec((3, STRIP), lambda j: (0, j + off))
    bsp = lambda off: pl.BlockSpec((1, STRIP), lambda j: (0, j + off))
    return _pcall(body, name="conv_act_bwd", grid=(N_STRIPS,),
                  in_specs=[strip(0), strip(N_STRIPS), strip(0), wsp(0), wsp(N_STRIPS), bsp(0), bsp(N_STRIPS)],
                  out_specs=[pl.BlockSpec((2, s, STRIP), lambda j: (0, 0, j)), pl.BlockSpec((2, SUBLANES, STRIP), lambda j: (0, 0, j))],
                  out_shape=[_sds((2, s, D_FF), BF16), _sds((2, SUBLANES, D_FF))],
                  scratch_shapes=[pltpu.VMEM((s + SUBLANES, STRIP), F32)] * 4,
                  dims=("parallel",))(up, up, dact, conv_w, conv_w, conv_b, conv_b)


def _mix_bwd(dy, dh2, x1, g_ffn, w_out, y, att, w_glu, b_glu, g_att, g_ssm):
    s = dy.shape[0]
    tm = _row_tile(s)

    def body(dy_ref, dh2_ref, x1_ref, gf_ref, wo_ref, y_ref, att_ref, wg_ref, bg_ref, ga_ref, gs_ref,
             dx1_ref, dx1b_ref, datt_ref, dys_ref, dwg_ref, dgf_ref, dga_ref, dgs_ref, dbg_ref):
        i = pl.program_id(0)

        @pl.when(i == 0)
        def _():
            for r in (dwg_ref, dgf_ref, dga_ref, dgs_ref, dbg_ref):
                r[...] = jnp.zeros_like(r)

        dxn, dgf = _rms_bwd(x1_ref[...], gf_ref[...], dh2_ref[...])
        dx1 = dy_ref[...] + dxn
        dx1_ref[...] = dx1
        dx1b = dx1.astype(BF16)
        dx1b_ref[...] = dx1b
        dgf_ref[...] += dgf
        dma = _dot_nt(dx1b, wo_ref[0:ATTN_W, :])
        dms = _dot_nt(dx1b, wo_ref[ATTN_W:D_MODEL, :])
        datt, dga = _rms_bwd(_merge_heads(att_ref), ga_ref[...], dma)
        _split_heads(datt_ref, datt)
        dga_ref[...] += dga
        yv = y_ref[...]
        ge, sg = _ssm_glu(yv, wg_ref[...], bg_ref[...])
        dssm, dgs = _rms_bwd(ge * sg, gs_ref[...], dms)
        dgs_ref[...] += dgs
        dgl = dssm * ge * sg * (1.0 - sg)
        dglb = dgl.astype(BF16)
        dge = dssm * sg + _dot_nt(dglb, wg_ref[...])
        dbg_ref[...] += jnp.sum(dgl, axis=0, keepdims=True)
        dwg_ref[...] += _dot_tn(ge.astype(BF16), dglb)
        dys_ref[...] = dge * _gelu_grad(yv)

    row = lambda w: pl.BlockSpec((tm, w), lambda i: (i, 0))
    const = lambda a, b: pl.BlockSpec((a, b), lambda i: (0, 0))
    heads = pl.BlockSpec((HEADS, tm, HEAD_DIM), lambda i: (0, i, 0))
    return _pcall(body, name="mix_bwd", grid=(s // tm,),
                  in_specs=[row(D_MODEL), row(D_MODEL), row(D_MODEL), const(1, D_MODEL), const(D_MODEL, D_MODEL), row(SSM_W),
                            heads, const(SSM_W, SSM_W), const(1, SSM_W), const(1, ATTN_W), const(1, SSM_W)],
                  out_specs=[row(D_MODEL), row(D_MODEL), heads, row(SSM_W), const(SSM_W, SSM_W), const(1, D_MODEL),
                             const(1, ATTN_W), const(1, SSM_W), const(1, SSM_W)],
                  out_shape=[_sds((s, D_MODEL)), _sds((s, D_MODEL), BF16), _sds((HEADS, s, HEAD_DIM)), _sds((s, SSM_W)),
                             _sds((SSM_W, SSM_W)), _sds((1, D_MODEL)), _sds((1, ATTN_W)), _sds((1, SSM_W)), _sds((1, SSM_W))],
                  dims=("arbitrary",))(dy, dh2, x1, g_ffn, w_out, y, att, w_glu, b_glu, g_att, g_ssm)


def _ssm_bwd(dys, z, ub, xr, xi, bbr, bbi, ar, ai, ccr, cci, dsk, hosted=None):
    s = dys.shape[0]
    tm = _row_tile(s)
    nb = s // tm
    nt = tm // SUBLANES

    def body(dy_ref, u_ref, ub_ref, xr_ref, xi_ref, xrp_ref, xip_ref, bbr_ref, bbi_ref, ar_ref, ai_ref, ccr_ref,
             cci_ref, dsk_ref, du_ref, dbbr_ref, dbbi_ref, dccr_ref, dcci_ref, dar_ref, dai_ref, dd_ref,
             gr_s, gi_s, cr_s, ci_s, accr_s, acci_s):
        i = pl.program_id(1)
        first_block = i == nb - 1

        @pl.when(i == 0)
        def _():
            for r in (cr_s, ci_s, accr_s, acci_s, dbbr_ref, dbbi_ref, dccr_ref, dcci_ref, dd_ref):
                r[...] = jnp.zeros_like(r)

        dy = dy_ref[...]
        dyb = dy.astype(BF16)
        gr_s[...] = _dot_nt(dyb, ccr_ref[0])
        gi_s[...] = -_dot_nt(dyb, cci_ref[0])
        consts = _scan_consts(ar_ref[0], -ai_ref[0], CHUNK_S, True)
        row = lax.broadcasted_iota(jnp.int32, (SUBLANES, CHUNK_S), 0)

        def tile(kk, carry):
            cr, ci, accr, acci = carry
            k = nt - 1 - kk
            sl = pl.ds(pl.multiple_of(k * SUBLANES, SUBLANES), SUBLANES)
            gr, gi = _scan_tile(gr_s[sl, :], gi_s[sl, :], cr, ci, consts, True)
            gr_s[sl, :] = gr
            gi_s[sl, :] = gi
            slp = pl.ds(pl.multiple_of(jnp.maximum(k - 1, 0) * SUBLANES, SUBLANES), SUBLANES)
            inner = k > 0
            pr_t = jnp.where(inner, xr_ref[slp, :], xrp_ref[...])
            pi_t = jnp.where(inner, xi_ref[slp, :], xip_ref[...])
            live = jnp.logical_or(inner, jnp.logical_not(first_block))
            top_r = jnp.where(live, pltpu.roll(pr_t, 1, 0), 0.0)
            top_i = jnp.where(live, pltpu.roll(pi_t, 1, 0), 0.0)
            xpr = jnp.where(row == 0, top_r, pltpu.roll(xr_ref[sl, :], 1, 0))
            xpi = jnp.where(row == 0, top_i, pltpu.roll(xi_ref[sl, :], 1, 0))
            accr = accr + gr * xpr + gi * xpi
            acci = acci + gi * xpr - gr * xpi
            return gr[0:1, :], gi[0:1, :], accr, acci

        zeros = jnp.zeros((SUBLANES, CHUNK_S), F32)
        cr, ci, accr, acci = lax.fori_loop(0, nt, tile, (cr_s[...], ci_s[...], zeros, zeros))
        cr_s[...] = cr
        ci_s[...] = ci
        accr_s[...] += accr
        acci_s[...] += acci
        grb = gr_s[...].astype(BF16)
        gib = gi_s[...].astype(BF16)
        u_b = ub_ref[...]
        du_ref[...] = _dot_nt(grb, bbr_ref[0]) + _dot_nt(gib, bbi_ref[0]) + dsk_ref[...] * dy
        dbbr_ref[0] += _dot_tn(u_b, grb)
        dbbi_ref[0] += _dot_tn(u_b, gib)
        dccr_ref[0] += _dot_tn(xr_ref[...].astype(BF16), dyb)
        dcci_ref[0] -= _dot_tn(xi_ref[...].astype(BF16), dyb)
        dd_ref[...] += jnp.sum(dy * u_ref[...], axis=0, keepdims=True)

        @pl.when(i == nb - 1)
        def _():
            dar_ref[0] = jnp.sum(accr_s[...], axis=0, keepdims=True)
            dai_ref[0] = jnp.sum(acci_s[...], axis=0, keepdims=True)

    ucol0 = U_COL0 // CHUNK_U
    tiles_per_block = tm // SUBLANES
    rb = lambda i: nb - 1 - i
    wspec = lambda a, b: pl.BlockSpec((1, a, b), lambda j, i: (j, 0, 0))
    xblk = pl.BlockSpec((tm, CHUNK_S), lambda j, i: (rb(i), j))
    xprev = pl.BlockSpec((SUBLANES, CHUNK_S), lambda j, i: (jnp.maximum(rb(i) * tiles_per_block - 1, 0), j))
    ublk = pl.BlockSpec((tm, CHUNK_U), lambda j, i: (rb(i), j))
    first = lambda: jnp.logical_and(pl.program_id(0) == 0, pl.program_id(1) == 0)
    last = lambda: jnp.logical_and(pl.program_id(0) == SSM_CHUNKS - 1, pl.program_id(1) == nb - 1)
    return _host_pcall(
        body, hosted, first, last, n_in=14, n_out=8, n_scratch=6, name="ssm_bwd", grid=(SSM_CHUNKS, nb),
        in_specs=[ublk, pl.BlockSpec((tm, CHUNK_U), lambda j, i: (rb(i), ucol0 + j)), ublk, xblk, xblk, xprev, xprev,
                  wspec(CHUNK_U, CHUNK_S), wspec(CHUNK_U, CHUNK_S), wspec(1, CHUNK_S), wspec(1, CHUNK_S),
                  wspec(CHUNK_S, CHUNK_U), wspec(CHUNK_S, CHUNK_U), pl.BlockSpec((1, CHUNK_U), lambda j, i: (0, j))],
        out_specs=[ublk, wspec(CHUNK_U, CHUNK_S), wspec(CHUNK_U, CHUNK_S), wspec(CHUNK_S, CHUNK_U),
                   wspec(CHUNK_S, CHUNK_U), wspec(1, CHUNK_S), wspec(1, CHUNK_S),
                   pl.BlockSpec((1, CHUNK_U), lambda j, i: (0, j))],
        out_shape=[_sds((s, SSM_W)), _sds((SSM_CHUNKS, CHUNK_U, CHUNK_S)), _sds((SSM_CHUNKS, CHUNK_U, CHUNK_S)),
                   _sds((SSM_CHUNKS, CHUNK_S, CHUNK_U)), _sds((SSM_CHUNKS, CHUNK_S, CHUNK_U)),
                   _sds((SSM_CHUNKS, 1, CHUNK_S)), _sds((SSM_CHUNKS, 1, CHUNK_S)), _sds((1, SSM_W))],
        scratch_shapes=[pltpu.VMEM((tm, CHUNK_S), F32)] * 2 + [pltpu.VMEM((1, CHUNK_S), F32)] * 2
                       + [pltpu.VMEM((SUBLANES, CHUNK_S), F32)] * 2,
        dims=("parallel", "arbitrary"), operands=(dys, z, ub, xr, xi, xr, xi, bbr, bbi, ar, ai, ccr, cci, dsk))


def _attn_probs(q, ks, cs, lse, scale, diagonal):
    p = jnp.exp(_dot_nt(q, ks) * scale - cs - lse)
    if diagonal:
        tq, tk = p.shape
        causal = lax.broadcasted_iota(jnp.int32, (tq, tk), 1) <= lax.broadcasted_iota(jnp.int32, (tq, tk), 0)
        p = jnp.where(causal, p, 0.0)
    return p


def _attn_bwd(qh, kh, vh, crow, lse, doh, hosted=None):
    _, s, _ = qh.shape
    tq = _row_tile(s)
    nq = s // tq
    scale = HEAD_DIM ** -0.5
    hp = HEADS_PER_STEP

    def body(q_ref, k_ref, v_ref, c_ref, lse_ref, do_ref, dq_ref, dk_ref, dv_ref, dc_ref, p_s, dp_s):
        i = pl.program_id(1)

        @pl.when(i == 0)
        def _():
            for r in (dk_ref, dv_ref, dc_ref):
                r[...] = jnp.zeros_like(r)

        dobs = [do_ref[hh].astype(BF16) for hh in range(hp)]

        def first(j, dls, diagonal):
            off = pl.multiple_of(j * tq, tq)
            out = []
            for hh in range(hp):
                p = _attn_probs(q_ref[hh], k_ref[hh, pl.ds(off, tq), :], c_ref[hh, :, pl.ds(off, tq)], lse_ref[hh],
                                scale, diagonal)
                dp = _dot_nt(dobs[hh], v_ref[hh, pl.ds(off, tq), :])
                p_s[hh, j] = p
                dp_s[hh, j] = dp
                out.append(dls[hh] + jnp.sum(p * dp, axis=-1, keepdims=True))
            return tuple(out)

        zero_col = jnp.zeros((tq, 1), F32)
        dls = lax.fori_loop(0, i, lambda j, c: first(j, c, False), (zero_col,) * hp)
        dls = first(i, dls, True)

        def second(j, dqs):
            rows = pl.ds(pl.multiple_of(j * tq, tq), tq)
            out = []
            for hh in range(hp):
                p = p_s[hh, j]
                ds = p * (dp_s[hh, j] - dls[hh])
                dsb = ds.astype(BF16)
                dv_ref[hh, rows, :] += _dot_tn(p.astype(BF16), dobs[hh])
                dk_ref[hh, rows, :] += _dot_tn(dsb, q_ref[hh]) * scale
                dc_ref[hh, :, rows] -= jnp.sum(ds, axis=0, keepdims=True)
                out.append(dqs[hh] + _dot(dsb, k_ref[hh, rows, :]))
            return tuple(out)

        dqs = lax.fori_loop(0, i + 1, second, (jnp.zeros((tq, HEAD_DIM), F32),) * hp)
        for hh in range(hp):
            dq_ref[hh] = dqs[hh] * scale

    blk = pl.BlockSpec((hp, tq, HEAD_DIM), lambda h, i: (h, i, 0))
    full = pl.BlockSpec((hp, s, HEAD_DIM), lambda h, i: (h, 0, 0))
    crow_spec = pl.BlockSpec((hp, 1, s), lambda h, i: (h, 0, 0))
    nh = HEADS // hp
    first = lambda: jnp.logical_and(pl.program_id(0) == 0, pl.program_id(1) == 0)
    last = lambda: jnp.logical_and(pl.program_id(0) == nh - 1, pl.program_id(1) == nq - 1)
    return _host_pcall(body, hosted, first, last, n_in=6, n_out=4, n_scratch=2, name="attn_bwd", grid=(nh, nq),
                       in_specs=[blk, full, full, crow_spec, pl.BlockSpec((hp, tq, 1), lambda h, i: (h, i, 0)), blk],
                       out_specs=[blk, full, full, crow_spec],
                       out_shape=[_sds((HEADS, s, HEAD_DIM))] * 3 + [_sds((HEADS, 1, s))],
                       scratch_shapes=[pltpu.VMEM((hp, nq, tq, tq), F32)] * 2,
                       dims=("parallel", "arbitrary"), operands=(qh, kh, vh, crow, lse, doh))


def _prep_bwd(z, dqn, dkn, dv, du, dc, gq, gk, bf, gg):
    s = z.shape[0]
    tm = _row_tile(s)
    nb = s // tm

    def body(z_ref, dqn_ref, dkn_ref, dv_ref, du_ref, dc_ref, gq_ref, gk_ref, bf_ref, gg_ref,
             dz_ref, dgq_ref, dgk_ref, dbf_ref, carry_ref):
        i = pl.program_id(0)

        @pl.when(i == 0)
        def _():
            for r in (dgq_ref, dgk_ref, dbf_ref, carry_ref):
                r[...] = jnp.zeros_like(r)

        gg_m = gg_ref[...]

        def head_norm_bwd(t, g, dn):
            r = lax.rsqrt(_dot_exact_r(t * t, gg_m) * (1.0 / HEAD_DIM) + EPS)
            w = dn * g
            mean_wt = _dot_exact_r(w * t, gg_m) * (1.0 / HEAD_DIM)
            return r * w - t * (r * r * r) * mean_wt, jnp.sum(dn * t * r, axis=0, keepdims=True)

        dq, dgq = head_norm_bwd(z_ref[:, 0:ATTN_W], gq_ref[...], _merge_heads(dqn_ref))
        dk, dgk = head_norm_bwd(z_ref[:, ATTN_W:2 * ATTN_W], gk_ref[...], _merge_heads(dkn_ref))
        dgq_ref[...] += dgq
        dgk_ref[...] += dgk
        row = lax.broadcasted_iota(jnp.int32, (tm, tm), 0)
        col = lax.broadcasted_iota(jnp.int32, (tm, tm), 1)
        triu = (col >= row).astype(BF16)
        dlf = _dot_exact_l(triu, dc_ref[...]) + carry_ref[...]
        carry_ref[...] = dlf[0:1, :]
        fl = z_ref[:, F_COL0:F_COL0 + LANES] + bf_ref[...]
        df = dlf * _sigmoid(-fl)
        dbf_ref[...] += jnp.sum(df, axis=0, keepdims=True)
        dz_ref[:, 0:ATTN_W] = dq.astype(BF16)
        dz_ref[:, ATTN_W:2 * ATTN_W] = dk.astype(BF16)
        dz_ref[:, 2 * ATTN_W:3 * ATTN_W] = _merge_heads(dv_ref).astype(BF16)
        dz_ref[:, U_COL0:U_COL0 + SSM_W] = du_ref[...].astype(BF16)
        dz_ref[:, F_COL0:F_COL0 + LANES] = df.astype(BF16)

    row_spec = lambda w: pl.BlockSpec((tm, w), lambda i: (nb - 1 - i, 0))
    const = lambda shape: pl.BlockSpec(shape, lambda i: (0, 0))
    return _pcall(body, name="prep_bwd", grid=(nb,),
                  in_specs=[row_spec(Z_COLS)] + [pl.BlockSpec((HEADS, tm, HEAD_DIM), lambda i: (0, nb - 1 - i, 0))] * 3
                           + [row_spec(ATTN_W), row_spec(LANES), const((1, ATTN_W)),
                              const((1, ATTN_W)), const((1, LANES)), const((ATTN_W, ATTN_W))],
                  out_specs=[row_spec(Z_COLS), const((1, ATTN_W)), const((1, ATTN_W)), const((1, LANES))],
                  out_shape=[_sds((s, Z_COLS), BF16), _sds((1, ATTN_W)), _sds((1, ATTN_W)), _sds((1, LANES))],
                  scratch_shapes=[pltpu.VMEM((1, LANES), F32)], dims=("arbitrary",))(z, dqn, dkn, dv, du, dc, gq, gk, bf, gg)


def _in_norm_bwd(x, g_mix, dh, dx1):
    s = x.shape[0]
    tm = _row_tile(s)

    def body(x_ref, g_ref, dh_ref, dx1_ref, dx_ref, dg_ref):
        i = pl.program_id(0)

        @pl.when(i == 0)
        def _():
            dg_ref[...] = jnp.zeros_like(dg_ref)

        dxn, dg = _rms_bwd(x_ref[...], g_ref[...], dh_ref[...])
        dx_ref[...] = dx1_ref[...] + dxn
        dg_ref[...] += dg

    row = pl.BlockSpec((tm, D_MODEL), lambda i: (i, 0))
    vec = pl.BlockSpec((1, D_MODEL), lambda i: (0, 0))
    return _pcall(body, name="in_norm_bwd", grid=(s // tm,), in_specs=[row, vec, row, row], out_specs=[row, vec],
                  out_shape=[_sds((s, D_MODEL)), _sds((1, D_MODEL))], dims=("arbitrary",))(x, g_mix, dh, dx1)


def _adamw_refs(w_ref, g_ref, m_ref, v_ref, d_ref, mo_ref, vo_ref):
    gv = g_ref[...]
    mn = ADAM_B1 * m_ref[...] + (1.0 - ADAM_B1) * gv
    vn = ADAM_B2 * v_ref[...] + (1.0 - ADAM_B2) * (gv * gv)
    m_hat = mn / (1.0 - ADAM_B1 ** ADAM_STEP)
    v_hat = vn / (1.0 - ADAM_B2 ** ADAM_STEP)
    d_ref[...] = -ADAM_LR * (m_hat / (jnp.sqrt(v_hat) + ADAM_EPS) + ADAM_WD * w_ref[...])
    mo_ref[...] = mn
    vo_ref[...] = vn


def _adamw_small(ws, gs, ms, vs):
    n = len(ws)

    def body(*refs):
        ins, outs = refs[:4 * n], refs[4 * n:]
        for i in range(n):
            _adamw_refs(ins[i], ins[n + i], ins[2 * n + i], ins[3 * n + i], *outs[3 * i:3 * i + 3])

    vm = pl.BlockSpec(memory_space=pltpu.VMEM)
    out_shape = [_sds(w.shape) for w in ws for _ in range(3)]
    return _pallas(body, name="adamw_small", in_specs=[vm] * (4 * n), out_specs=[vm] * (3 * n), out_shape=out_shape,
                   compiler_params=pltpu.CompilerParams(vmem_limit_bytes=VMEM_LIMIT))(*ws, *gs, *ms, *vs)


def _adamw(w, g, m, v, *, name):
    r, c = w.shape
    tr = r
    for cand in (256, 176, 128, 64):
        if r > cand and r % cand == 0:
            tr = cand
            break

    def body(w_ref, g_ref, m_ref, v_ref, d_ref, mo_ref, vo_ref):
        _adamw_refs(w_ref, g_ref, m_ref, v_ref, d_ref, mo_ref, vo_ref)

    spec = pl.BlockSpec((tr, c), lambda i: (i, 0))
    return _pcall(body, name=name, grid=(r // tr,), in_specs=[spec] * 4, out_specs=[spec] * 3,
                  out_shape=[_sds((r, c))] * 3, dims=("parallel",))(w, g, m, v)


def _prefetch_call(body, *, name, grid, in_specs, out_specs, out_shape, operands):
    grid_spec = pltpu.PrefetchScalarGridSpec(num_scalar_prefetch=1, grid=grid, in_specs=in_specs, out_specs=out_specs)
    params = pltpu.CompilerParams(dimension_semantics=("parallel",) * len(grid), vmem_limit_bytes=VMEM_LIMIT)
    return _pallas(body, name=name, grid_spec=grid_spec, out_shape=out_shape, compiler_params=params)(*operands)


def _half_rows_tile(hr):
    return hr if hr <= 256 else 176 if hr % 176 == 0 else 256


def _add_half(g, landed, place, *, name):
    def body(place_ref, g_ref, l_ref, o_ref):
        own = g_ref[0] if len(g_ref.shape) == 4 else g_ref[...]
        o_ref[...] = (own + l_ref[...]).astype(BF16)

    if g.ndim == 4:
        _, _, hr, c = g.shape
        tr = _half_rows_tile(hr)
        blk = (1, tr, c)
        return _prefetch_call(
            body, name=name, grid=(N_CHIPS, hr // tr),
            in_specs=[pl.BlockSpec((1,) + blk, lambda j, i, p: (j, p[1], i, 0)), pl.BlockSpec(blk, lambda j, i, p: (j, i, 0))],
            out_specs=pl.BlockSpec(blk, lambda j, i, p: (j, i, 0)), out_shape=_sds(landed.shape, BF16),
            operands=(place, g, landed))
    hr, c = landed.shape
    tr, tc = 256, _tile(c, 2176)
    nb = hr // tr
    return _prefetch_call(
        body, name=name, grid=(nb, c // tc),
        in_specs=[pl.BlockSpec((tr, tc), lambda i, j, p: (p[1] * nb + i, j)), pl.BlockSpec((tr, tc), lambda i, j, p: (i, j))],
        out_specs=pl.BlockSpec((tr, tc), lambda i, j, p: (i, j)), out_shape=_sds(landed.shape, BF16),
        operands=(place, g, landed))


def _sum_chips(chip_sum, lands, place, *, name, tc, window_stride=0):
    _, hr, c = lands.shape
    tr = _half_rows_tile(hr)
    nb = hr // tr
    ncb = c // tc

    def body(place_ref, own_ref, a_ref, b_ref, c_ref, o_ref):
        own = own_ref[0] if len(own_ref.shape) == 3 else own_ref[...]
        o_ref[...] = ((own.astype(F32) + a_ref[0].astype(F32)) + b_ref[0].astype(F32)) + c_ref[0].astype(F32)

    land = lambda k: pl.BlockSpec((1, tr, tc), lambda i, j, p: ((p[0] + k) % N_CHIPS, i, j))
    if chip_sum.ndim == 3:
        own_spec = land(0)
    else:
        stride = window_stride // tc
        own_spec = pl.BlockSpec((tr, tc), lambda i, j, p: (i, p[0] * stride + j))
    return _prefetch_call(
        body, name=name, grid=(nb, ncb), in_specs=[own_spec, land(1), land(2), land(3)],
        out_specs=pl.BlockSpec((tr, tc), lambda i, j, p: (p[1] * nb + i, j)), out_shape=_sds((2 * hr, c)),
        operands=(place, chip_sum, lands, lands, lands))


_HBM = pl.BlockSpec(memory_space=pltpu.HBM)


def _place():
    x, y, c = lax.axis_index("x"), lax.axis_index("y"), lax.axis_index("c")
    chips = [(1 - x, y), (x, 1 - y), (1 - x, 1 - y)]
    return x, y, c, chips


def _rcopy(src, dst, send_sem, recv_sem, to):
    return pltpu.make_async_remote_copy(src_ref=src, dst_ref=dst, send_sem=send_sem, recv_sem=recv_sem,
                                        device_id=to, device_id_type=MESH)


N_BIG = 5
UP_COLS = 2 * D_FF // N_CHIPS
IN_WINDOW = 640
IN_STRIDE = 512


class _Hosted:
    def __init__(self, operands, out_shapes, n_sems, start, finish, aliases=None, local_sems=0):
        self.operands, self.out_shapes, self.n_sems = list(operands), list(out_shapes), n_sems
        self.start, self.finish, self.aliases, self.local_sems = start, finish, dict(aliases or {}), local_sems

    def scratch(self):
        return ([pltpu.SemaphoreType.DMA((self.n_sems,)), pltpu.SemaphoreType.DMA((self.n_sems,))]
                + [pltpu.SemaphoreType.DMA] * self.local_sems)


def _both(a, b):
    na, nao, nas = len(a.operands), len(a.out_shapes), len(a.scratch())

    def start(ins, outs, sems):
        a.start(ins[:na], outs[:nao], sems[:nas])
        b.start(ins[na:], outs[nao:], sems[nas:])

    def finish(ins, outs, sems):
        a.finish(ins[:na], outs[:nao], sems[:nas])
        b.finish(ins[na:], outs[nao:], sems[nas:])

    both = _Hosted(a.operands + b.operands, a.out_shapes + b.out_shapes, 0, start, finish,
                   aliases={**a.aliases, **{na + i: nao + o for i, o in b.aliases.items()}})
    both.scratch = lambda: a.scratch() + b.scratch()
    return both


def _run_hosted(hosted, *, name):
    n_in, n_out = len(hosted.operands), len(hosted.out_shapes)

    def body(*refs):
        parts = (refs[:n_in], refs[n_in:n_in + n_out], refs[n_in + n_out:])
        hosted.start(*parts)
        hosted.finish(*parts)

    return _pallas(body, name=name, in_specs=[_HBM] * n_in, out_specs=[_HBM] * n_out, out_shape=hosted.out_shapes,
                   input_output_aliases=hosted.aliases, scratch_shapes=hosted.scratch())(*hosted.operands)


def _host_pcall(core_body, hosted, first, last, *, n_in, n_out, n_scratch, name, grid, in_specs, out_specs, out_shape,
                scratch_shapes, dims, operands):
    if hosted is None:
        outs = _pcall(core_body, name=name, grid=grid, in_specs=in_specs, out_specs=out_specs, out_shape=out_shape,
                      scratch_shapes=scratch_shapes, dims=dims)(*operands)
        return outs, []
    hi, ho = len(hosted.operands), len(hosted.out_shapes)

    def body(*refs):
        a, b = n_in, n_in + hi
        c, d = b + n_out, b + n_out + ho
        e = d + n_scratch
        parts = (refs[a:b], refs[c:d], refs[e:])

        @pl.when(first())
        def _():
            hosted.start(*parts)

        core_body(*refs[:a], *refs[b:c], *refs[d:e])

        @pl.when(last())
        def _():
            hosted.finish(*parts)

    params = pltpu.CompilerParams(dimension_semantics=("arbitrary",) * len(grid), vmem_limit_bytes=VMEM_LIMIT)
    outs = _pallas(body, name=name, grid=grid, in_specs=list(in_specs) + [_HBM] * hi, out_specs=list(out_specs) + [_HBM] * ho,
                   out_shape=list(out_shape) + hosted.out_shapes, scratch_shapes=list(scratch_shapes) + hosted.scratch(),
                   input_output_aliases={n_in + a: n_out + b for a, b in hosted.aliases.items()},
                   compiler_params=params)(*operands, *hosted.operands)
    return outs[:n_out], outs[n_out:]


def _gather_slot(src, out, chip, hc):
    hr, cols = src.shape[0] // 2, src.shape[1]
    if len(out.shape) == 2:
        return out.at[pl.ds(hc * hr, hr), pl.ds(pl.multiple_of(chip * cols, LANES), cols)]
    return out.at[chip, pl.ds(hc * hr, hr), :]


def _gathered_shape(shard, by_cols):
    if by_cols:
        return _sds((shard.shape[0], N_CHIPS * shard.shape[1]), shard.dtype)
    return _sds((N_CHIPS,) + shard.shape, shard.dtype)


def _plan_gather_ici(shards, by_cols, whole=(), own_cols=()):
    n = len(shards)

    def copies(ins, outs, sems):
        send_sems, recv_sems = sems[0], sems[1]
        x, y, c, chips = _place()
        me = 2 * x + y
        sends, waits = [], []
        for w in range(n + len(whole)):
            for k, (cx, cy) in enumerate(chips):
                sem = (send_sems.at[3 * w + k], recv_sems.at[3 * w + k])
                if w < n:
                    hr = ins[w].shape[0] // 2
                    sends.append(_rcopy(ins[w].at[pl.ds(c * hr, hr), :], _gather_slot(ins[w], outs[w], me, c), *sem, (cx, cy, c)))
                    landed = _gather_slot(ins[w], outs[w], 2 * cx + cy, c)
                else:
                    sends.append(_rcopy(ins[w], outs[w].at[me], *sem, (cx, cy, c)))
                    landed = outs[w].at[2 * cx + cy]
                waits.append(_rcopy(landed, landed, *sem, (cx, cy, c)))
        local = [pltpu.make_async_copy(
            ins[w], outs[w].at[:, pl.ds(pl.multiple_of(me * ins[w].shape[1], LANES), ins[w].shape[1])], sems[2 + i])
            for i, w in enumerate(own_cols)]
        return sends, waits, local

    def start(ins, outs, sems):
        sends, _, local = copies(ins, outs, sems)
        for cp in local + sends:
            cp.start()

    def finish(ins, outs, sems):
        sends, waits, local = copies(ins, outs, sems)
        for cp in waits:
            cp.wait_recv()
        for cp in sends:
            cp.wait_send()
        for cp in local:
            cp.wait()

    out_shapes = [_gathered_shape(s, bc) for s, bc in zip(shards, by_cols)] + [_sds((N_CHIPS,) + a.shape, a.dtype) for a in whole]
    return _Hosted(list(shards) + list(whole), out_shapes, 3 * (n + len(whole)), start, finish, local_sems=len(own_cols))


def _plan_gather_d2d(bufs, shard_shapes):
    n = len(bufs)

    def copies(ins, outs, sems):
        send_sems, recv_sems = sems
        x, y, c, chips = _place()
        sibling = (x, y, 1 - c)
        sends, waits = [], []
        for w in range(n):
            for k, (cx, cy) in enumerate(chips):
                sem = (send_sems.at[3 * w + k], recv_sems.at[3 * w + k])
                landed = _gather_slot(shard_shapes[w], outs[w], 2 * cx + cy, c)
                other = _gather_slot(shard_shapes[w], outs[w], 2 * cx + cy, 1 - c)
                sends.append(_rcopy(landed, landed, *sem, sibling))
                waits.append(_rcopy(other, other, *sem, sibling))
        return sends, waits

    def start(ins, outs, sems):
        for cp in copies(ins, outs, sems)[0]:
            cp.start()

    def finish(ins, outs, sems):
        sends, waits = copies(ins, outs, sems)
        for cp in waits:
            cp.wait_recv()
        for cp in sends:
            cp.wait_send()

    return _Hosted(bufs, [_sds(b.shape, b.dtype) for b in bufs], 3 * n, start, finish, aliases={w: w for w in range(n)})


def _plan_swap(grads):
    def copies(ins, outs, sems):
        send_sems, recv_sems = sems
        x, y, c, _ = _place()
        cps = []
        for w, g_ref in enumerate(ins):
            if len(g_ref.shape) == 4:
                theirs = g_ref.at[:, 1 - c]
            else:
                hr = g_ref.shape[0] // 2
                theirs = g_ref.at[pl.ds((1 - c) * hr, hr), :]
            cps.append(_rcopy(theirs, outs[w], send_sems.at[w], recv_sems.at[w], (x, y, 1 - c)))
        return cps

    def start(ins, outs, sems):
        for cp in copies(ins, outs, sems):
            cp.start()

    def finish(ins, outs, sems):
        for cp in copies(ins, outs, sems):
            cp.wait()

    out_shapes = [_sds((g.shape[0], g.shape[2], g.shape[3])) if g.ndim == 4 else _sds((g.shape[0] // 2, g.shape[1]))
                  for g in grads]
    return _Hosted(grads, out_shapes, len(grads), start, finish)


def _plan_scatter(chip_sums, windows):
    def copies(ins, outs, sems):
        send_sems, recv_sems = sems
        x, y, c, chips = _place()
        me = 2 * x + y
        sends, waits = [], []
        for w, s_ref in enumerate(ins):
            for k, (cx, cy) in enumerate(chips):
                tgt = 2 * cx + cy
                if windows[w] is not None:
                    stride, width = windows[w]
                    part = s_ref.at[:, pl.ds(pl.multiple_of(tgt * stride, LANES), width)]
                else:
                    part = s_ref.at[tgt]
                sem = (send_sems.at[3 * w + k], recv_sems.at[3 * w + k])
                sends.append(_rcopy(part, outs[w].at[me], *sem, (cx, cy, c)))
                slot = outs[w].at[tgt]
                waits.append(_rcopy(slot, slot, *sem, (cx, cy, c)))
        return sends, waits

    def start(ins, outs, sems):
        for cp in copies(ins, outs, sems)[0]:
            cp.start()

    def finish(ins, outs, sems):
        sends, waits = copies(ins, outs, sems)
        for cp in waits:
            cp.wait_recv()
        for cp in sends:
            cp.wait_send()

    out_shapes = [_sds((N_CHIPS, s.shape[0], win[1]), BF16) if win is not None else _sds(s.shape, BF16)
                  for s, win in zip(chip_sums, windows)]
    return _Hosted(chip_sums, out_shapes, 3 * len(chip_sums), start, finish)


def _plan_join(reds):
    def copies(ins, outs, sems):
        send_sems, recv_sems = sems
        x, y, c, _ = _place()
        sends, waits = [], []
        for w, out in enumerate(outs):
            hr = out.shape[0] // 2
            mine = out.at[pl.ds(c * hr, hr), :]
            theirs = out.at[pl.ds((1 - c) * hr, hr), :]
            sends.append(_rcopy(mine, mine, send_sems.at[w], recv_sems.at[w], (x, y, 1 - c)))
            waits.append(_rcopy(theirs, theirs, send_sems.at[w], recv_sems.at[w], (x, y, 1 - c)))
        return sends, waits

    def start(ins, outs, sems):
        for cp in copies(ins, outs, sems)[0]:
            cp.start()

    def finish(ins, outs, sems):
        sends, waits = copies(ins, outs, sems)
        for cp in waits:
            cp.wait_recv()
        for cp in sends:
            cp.wait_send()

    return _Hosted(reds, [_sds(r.shape) for r in reds], len(reds), start, finish, aliases={w: w for w in range(len(reds))})


def _allreduce_small(v):
    m_per = v.shape[0]

    def body(v_ref, out_ref, all_ref, send_sems, recv_sems, local_sem):
        x, y, c, chips = _place()
        me, sibling = (x, y, c), (x, y, 1 - c)

        def rows(px, py, pc):
            return all_ref.at[pl.ds((4 * px + 2 * py + pc) * m_per, m_per), :]

        def copy(k, block, to, src=None):
            return _rcopy(rows(*block) if src is None else src, rows(*block), send_sems.at[k], recv_sems.at[k], to)

        mine = pltpu.make_async_copy(v_ref, rows(*me), local_sem)
        mine.start()
        first = [copy(0, me, sibling, src=v_ref)]
        first += [copy(1 + k, me, (*chip, c), src=v_ref) for k, chip in enumerate(chips)]
        for cp in first:
            cp.start()
        passed = [copy(4 + k, (*chip, c), sibling) for k, chip in enumerate(chips)]
        for k, chip in enumerate(chips):
            copy(1 + k, (*chip, c), me).wait_recv()
            passed[k].start()
        copy(0, sibling, me).wait_recv()
        for k, chip in enumerate(chips):
            copy(4 + k, (*chip, 1 - c), me).wait_recv()
        for cp in first + passed:
            cp.wait_send()
        mine.wait()
        acc = all_ref[pl.ds(0, m_per), :]
        for d in range(1, 8):
            acc = acc + all_ref[pl.ds(d * m_per, m_per), :]
        out_ref[...] = acc

    vm = pl.BlockSpec(memory_space=pltpu.VMEM)
    return _pallas(body, name="allreduce_small", in_specs=[vm], out_specs=vm, out_shape=_sds((m_per, LANES)),
                          scratch_shapes=[pltpu.VMEM((8 * m_per, LANES), F32), pltpu.SemaphoreType.DMA((7,)),
                                          pltpu.SemaphoreType.DMA((7,)), pltpu.SemaphoreType.DMA],
                          compiler_params=pltpu.CompilerParams(vmem_limit_bytes=VMEM_LIMIT))(v)


def _to_heads(t):
    s = t.shape[0]
    return t.reshape(s, HEADS, HEAD_DIM).transpose(1, 0, 2)


def _from_heads(t):
    s = t.shape[1]
    return t.transpose(1, 0, 2).reshape(s, HEADS * HEAD_DIM)


def _reorder_in_cols(w):
    pad = jnp.zeros((w.shape[0], Z_COLS - IN_COLS), w.dtype)
    return jnp.concatenate([w[:, :3 * ATTN_W], w[:, 3 * ATTN_W + HEADS:], w[:, 3 * ATTN_W:3 * ATTN_W + HEADS], pad], axis=1)


def _restore_in_cols(w):
    return jnp.concatenate([w[:, :3 * ATTN_W], w[:, F_COL0:F_COL0 + HEADS], w[:, U_COL0:U_COL0 + SSM_W]], axis=1)


def _block_diag(blocks):
    j, g, a, b = blocks.shape
    eye = jnp.eye(g, dtype=bool)[None, :, None, :, None]
    return jnp.where(eye, blocks[:, :, :, None, :], jnp.zeros((), blocks.dtype)).reshape(j, g * a, g * b)


def _diag_blocks(m, a, b):
    j = m.shape[0]
    g = m.shape[1] // a
    t = m.reshape(j, g, a, g, b)
    eye = jnp.eye(g, dtype=bool)[None, :, None, :, None]
    return jnp.sum(jnp.where(eye, t, 0.0), axis=3)


def _pack_rows(parts, rows, dtype):
    used = sum(p.shape[0] for p in parts)
    return jnp.concatenate([p.astype(dtype) for p in parts] + [jnp.zeros((rows - used, D_MODEL), dtype)], axis=0)


_SMALL = (("g_mix", (1024,)), ("b_f", (8,)), ("g_q", (64,)), ("g_k", (64,)), ("lambda_re", (32, 64)),
          ("lambda_im", (32, 64)), ("log_step", (32,)), ("b_re", (32, 64, 16)), ("b_im", (32, 64, 16)),
          ("c_re", (32, 16, 64)), ("c_im", (32, 16, 64)), ("d_skip", (32, 16)), ("b_glu", (512,)),
          ("g_attn_out", (512,)), ("g_ssm_out", (512,)), ("g_ffn", (1024,)), ("conv_b", (5632,)))


def _small_rows(shape):
    return -(-math.prod(shape) // LANES)


def _pack_small(arrs, extra=()):
    parts = []
    for a in list(arrs) + list(extra):
        flat = a.reshape(-1)
        rows = -(-flat.shape[0] // LANES)
        parts.append(jnp.pad(flat, (0, rows * LANES - flat.shape[0])).reshape(rows, LANES))
    total = sum(p.shape[0] for p in parts)
    pad = -total % SUBLANES
    if pad:
        parts.append(jnp.zeros((pad, LANES), F32))
    return jnp.concatenate(parts, axis=0)


def _unpack_small(buf, shapes):
    out, r = [], 0
    for shape in shapes:
        n = math.prod(shape)
        rows = -(-n // LANES)
        out.append(buf[r:r + rows].reshape(-1)[:n].reshape(shape))
        r += rows
    return out


def _halves(t):
    return t.reshape(N_CHIPS, 2, t.shape[0] // (2 * N_CHIPS), t.shape[1])


class _MeshComm:
    def __init__(self, args):
        x, y, self.core = lax.axis_index("x"), lax.axis_index("y"), lax.axis_index("c")
        self.chip = 2 * x + y
        self.place = jnp.stack([self.chip, self.core]).astype(jnp.int32)
        self.shards = {n: args[n].astype(BF16) for n in ("w_in", "w_glu", "w_out", "w_up", "w_down")}
        self.conv_w = args["conv_w"]

    def _own(self, stacked, mine):
        return lax.dynamic_update_slice(stacked, mine[None], (self.chip,) + (0,) * mine.ndim)

    def w_in(self):
        sh = self.shards["w_in"]
        (buf,) = _run_hosted(_plan_gather_ici([sh], [False]), name="gather_w_in")
        (buf,) = _run_hosted(_plan_gather_d2d([buf], [sh]), name="pass_w_in")
        return _reorder_in_cols(self._own(buf, sh).transpose(1, 0, 2).reshape(D_MODEL, IN_COLS))

    def gather_first(self):
        self.mid = [self.shards[n] for n in ("w_glu", "w_out", "w_up")]
        return _plan_gather_ici(self.mid, [False, False, True], whole=[self.conv_w], own_cols=[2])

    def gather_second(self, landed):
        self.g_cw = landed[3]
        return _both(_plan_gather_d2d(list(landed[:3]), self.mid), _plan_gather_ici([self.shards["w_down"]], [False]))

    def weights(self, gathered):
        g_glu, g_out, w_up_b = gathered[:3]
        own = self._own
        return (own(g_glu, self.mid[0]).reshape(SSM_W, SSM_W), own(g_out, self.mid[1]).reshape(D_MODEL, D_MODEL), w_up_b,
                own(self.g_cw, self.conv_w).transpose(1, 0, 2).reshape(3, 2 * D_FF))

    def gather_third(self, gathered):
        return _plan_gather_d2d([gathered[3]], [self.shards["w_down"]])

    def w_down(self, passed):
        return self._own(passed[0], self.shards["w_down"]).reshape(D_FF, D_MODEL)

    def swap(self, d_w_down, d_w_up, d_w_glu, d_w_out):
        self.early = [_halves(d_w_down), d_w_up, _halves(d_w_glu), _halves(d_w_out)]
        return _plan_swap(self.early)

    def scatter(self, landed):
        self.early_sums = [_add_half(g, l, self.place, name="add_" + n)
                           for g, l, n in zip(self.early, landed, ("w_down", "w_up", "w_glu", "w_out"))]
        return _plan_scatter(self.early_sums, [None, (UP_COLS, UP_COLS), None, None])

    def reduce(self, early_lands, d_w_in):
        d_in = jnp.pad(d_w_in, ((0, 0), (0, Z_COLS - IN_COLS)))
        (landed,) = _run_hosted(_plan_swap([d_in]), name="swap_halves")
        sum_in = _add_half(d_in, landed, self.place, name="add_w_in")
        (land_in,) = _run_hosted(_plan_scatter([sum_in], [(IN_STRIDE, IN_WINDOW)]), name="scatter_chips")
        es, el = self.early_sums, early_lands
        todo = [(sum_in, land_in, "w_in", LANES, IN_STRIDE), (es[2], el[2], "w_glu", SSM_W, 0),
                (es[3], el[3], "w_out", D_MODEL, 0), (es[1], el[1], "w_up", UP_COLS, UP_COLS),
                (es[0], el[0], "w_down", D_MODEL, 0)]
        reds = _run_hosted(_plan_join([_sum_chips(s, l, self.place, name="sum_" + n, tc=tc, window_stride=st)
                                       for s, l, n, tc, st in todo]), name="join_halves")
        g_big = dict(zip(("w_in", "w_glu", "w_out", "w_up", "w_down"), reds))
        g_big["w_in"] = lax.dynamic_slice_in_dim(reds[0], 2 * self.chip, IN_COLS // N_CHIPS, axis=1)
        return g_big


def _local_step(x, tgt, p, comm):
    s = x.shape[0]
    row = lambda v: v.reshape(1, -1)
    g_mix, g_ffn = row(p["g_mix"]), row(p["g_ffn"])
    g_att, g_ssm, b_glu, conv_b = row(p["g_attn_out"]), row(p["g_ssm_out"]), row(p["b_glu"]), row(p["conv_b"])
    gq = row(jnp.tile(p["g_q"], HEADS))
    gk = row(jnp.tile(p["g_k"], HEADS))
    bf = row(jnp.pad(p["b_f"], (0, LANES - HEADS)))
    gg = jnp.kron(jnp.eye(HEADS, dtype=F32), jnp.ones((HEAD_DIM, HEAD_DIM), F32)).astype(BF16)
    dsk = row(p["d_skip"])

    rep = lambda a: jnp.repeat(a, SSM_GROUP, axis=0)
    lr, li = rep(p["lambda_re"]), rep(p["lambda_im"])
    ls = rep(jnp.broadcast_to(p["log_step"][:, None], (SSM_GROUPS, SSM_STATE)))
    bt_re = p["b_re"].transpose(0, 2, 1).reshape(_PARAM_SHAPE)
    bt_im = p["b_im"].transpose(0, 2, 1).reshape(_PARAM_SHAPE)
    a_re_rep, a_im_rep, bb_re, bb_im = _ssm_params(lr, li, ls, bt_re, bt_im)
    ar = a_re_rep[::SSM_GROUP].reshape(SSM_CHUNKS, 1, CHUNK_S)
    ai = a_im_rep[::SSM_GROUP].reshape(SSM_CHUNKS, 1, CHUNK_S)
    chunked = lambda t: t.reshape(SSM_CHUNKS, SSM_GROUPS // SSM_CHUNKS, SSM_GROUP, SSM_STATE)
    bbr = _block_diag(chunked(bb_re)).astype(BF16)
    bbi = _block_diag(chunked(bb_im)).astype(BF16)
    to_cc = lambda c: _block_diag(chunked(c).transpose(0, 1, 3, 2)).astype(BF16)
    ccr, cci = to_cc(p["c_re"]), to_cc(p["c_im"])

    w_in_r = comm.w_in()
    hb, z = _in_proj(x, g_mix, w_in_r)
    qh, kh, vh, ub, c128 = _attn_prep(z, gq, gk, bf, gg)
    crow = c128[:, :HEADS].T.reshape(HEADS, 1, s)
    (oh, lse), landed = _attn_fwd(qh, kh, vh, crow, comm.gather_first())
    (xr, xi, y), gathered = _ssm_fwd(ub, z, bbr, bbi, ar, ai, ccr, cci, dsk, comm.gather_second(landed))
    w_glu_b, w_out_b, w_up_b, conv_w_full = comm.weights(gathered)
    (x1, mixb, h2b), passed = _mix_out(y, oh, x, w_glu_b, b_glu, g_att, g_ssm, w_out_b, g_ffn, comm.gather_third(gathered))
    w_down_b = comm.w_down(passed)
    up = _mm(h2b, w_up_b, name="ffn_up", tm=1024, tn=1408, tk=1024)
    act = _conv_act(up, conv_w_full, conv_b)
    dy, dyb, loss_blk = _down_loss(act, w_down_b, x1, tgt)

    d_w_down = _mm(act, dyb, ta=True, name="d_w_down", tm=1408, tn=1024, tk=2048)
    dact = _mm(dyb, w_down_b, tb=True, name="d_act", tm=1024, tn=1408, tk=1024)
    dupb, dcw = _conv_act_bwd(up, dact, conv_w_full, conv_b)
    d_w_up = _mm(h2b, dupb, ta=True, b_parts=2, name="d_w_up", tm=1024, tn=1408, tk=2048)
    dh2 = _mm(dupb, w_up_b, tb=True, a_parts=2, name="d_h2", tm=1024, tn=1024, tk=1408)
    dx1, dx1b, doh, dys, d_w_glu, d_g_ffn, d_g_att, d_g_ssm, d_b_glu = _mix_bwd(
        dy, dh2, x1, g_ffn, w_out_b, y, oh, w_glu_b, b_glu, g_att, g_ssm)
    d_w_out = _mm(mixb, dx1b, ta=True, name="d_w_out", tm=1024, tn=1024, tk=2048)
    (du, dbbr, dbbi, dccr, dcci, dar, dai, dd), swapped = _ssm_bwd(dys, z, ub, xr, xi, bbr, bbi, ar, ai, ccr, cci, dsk,
                                                                comm.swap(d_w_down, d_w_up, d_w_glu, d_w_out))
    (dqh, dkh, dvh, dcrow), early_lands = _attn_bwd(qh, kh, vh, crow, lse, doh, comm.scatter(swapped))
    dc128 = jnp.pad(dcrow.reshape(HEADS, s).T, ((0, 0), (0, LANES - HEADS)))
    dzb, d_gq, d_gk, d_bf = _prep_bwd(z, dqh, dkh, dvh, du, dc128, gq, gk, bf, gg)
    d_w_in_r = _mm(hb, dzb, ta=True, name="d_w_in", tm=512, tn=Z_COLS, tk=2048)
    dh = _mm(dzb, w_in_r, tb=True, name="d_h", tm=1024, tn=1024, tk=Z_COLS)
    dx, d_g_mix = _in_norm_bwd(x, g_mix, dh, dx1)

    unchunk = lambda t: t.reshape(_PARAM_SHAPE)
    dbb_re = unchunk(_diag_blocks(dbbr, SSM_GROUP, SSM_STATE))
    dbb_im = unchunk(_diag_blocks(dbbi, SSM_GROUP, SSM_STATE))
    first_row = (jnp.arange(_PARAM_SHAPE[0]) % SSM_GROUP == 0)[:, None]
    da_re = jnp.where(first_row, rep(dar.reshape(SSM_GROUPS, SSM_STATE)), 0.0)
    da_im = jnp.where(first_row, rep(dai.reshape(SSM_GROUPS, SSM_STATE)), 0.0)
    expand_t = (jnp.arange(SSM_GROUPS)[:, None] == (jnp.arange(_PARAM_SHAPE[0]) // SSM_GROUP)[None, :]).astype(BF16)
    d_lr, d_li, d_ls, d_bt_re, d_bt_im = _ssm_params_bwd(lr, li, ls, bt_re, bt_im, da_re, da_im, dbb_re, dbb_im, expand_t)
    from_bt = lambda t: t.reshape(SSM_GROUPS, SSM_GROUP, SSM_STATE).transpose(0, 2, 1)
    from_cc = lambda t: _diag_blocks(t, SSM_STATE, SSM_GROUP).transpose(0, 1, 3, 2).reshape(SSM_GROUPS, SSM_GROUP, SSM_STATE)

    small = {
        "g_mix": d_g_mix, "b_f": d_bf[0, :HEADS], "g_q": d_gq.reshape(HEADS, HEAD_DIM).sum(0),
        "g_k": d_gk.reshape(HEADS, HEAD_DIM).sum(0), "lambda_re": d_lr, "lambda_im": d_li, "log_step": d_ls,
        "b_re": from_bt(d_bt_re), "b_im": from_bt(d_bt_im), "c_re": from_cc(dccr), "c_im": from_cc(dcci),
        "d_skip": dd, "b_glu": d_b_glu, "g_attn_out": d_g_att, "g_ssm_out": d_g_ssm, "g_ffn": d_g_ffn,
        "conv_b": dcw[:, 3],
    }
    big = {"w_in": _restore_in_cols(d_w_in_r), "w_glu": d_w_glu, "w_out": d_w_out, "w_up": d_w_up, "w_down": d_w_down}
    return loss_blk[0, 0], dx, big, small, dcw[:, 0:3].transpose(1, 0, 2).reshape(3, 2 * D_FF), early_lands


def kernel(x, g_mix, w_in, b_f, g_q, g_k, lambda_re, lambda_im, log_step, b_re, b_im, c_re, c_im, d_skip, w_glu, b_glu, g_attn_out, g_ssm_out, w_out, g_ffn, w_up, conv_w, conv_b, w_down, loss_target, m_g_mix, m_w_in, m_b_f, m_g_q, m_g_k, m_lambda_re, m_lambda_im, m_log_step, m_b_re, m_b_im, m_c_re, m_c_im, m_d_skip, m_w_glu, m_b_glu, m_g_attn_out, m_g_ssm_out, m_w_out, m_g_ffn, m_w_up, m_conv_w, m_conv_b, m_w_down, v_g_mix, v_w_in, v_b_f, v_g_q, v_g_k, v_lambda_re, v_lambda_im, v_log_step, v_b_re, v_b_im, v_c_re, v_c_im, v_d_skip, v_w_glu, v_b_glu, v_g_attn_out, v_g_ssm_out, v_w_out, v_g_ffn, v_w_up, v_conv_w, v_conv_b, v_w_down):
    args = dict(locals())
    order = ["g_mix", "w_in", "b_f", "g_q", "g_k", "lambda_re", "lambda_im", "log_step", "b_re", "b_im", "c_re", "c_im",
             "d_skip", "w_glu", "b_glu", "g_attn_out", "g_ssm_out", "w_out", "g_ffn", "w_up", "conv_w", "conv_b", "w_down"]
    comm = _MeshComm(args)
    chip = comm.chip
    loss_part, dx, big, small, d_conv_w, early_lands = _local_step(x[0], loss_target[0], args, comm)
    loss = lax.psum(loss_part, ("x", "y", "c"))

    g_big = comm.reduce(early_lands, big["w_in"])

    small_names = [n for n, _ in _SMALL]
    small_shapes = [sh for _, sh in _SMALL]
    gsum = _allreduce_small(_pack_small([small[n] for n in small_names], extra=[d_conv_w]))
    g_small = _unpack_small(gsum, small_shapes + [(3, 2 * D_FF)])
    g_conv_w = lax.dynamic_slice_in_dim(g_small[-1], chip * (2 * D_FF // N_CHIPS), 2 * D_FF // N_CHIPS, axis=1)
    g_small = dict(zip(small_names, g_small[:-1]))

    grad, delta, new_m, new_v = {}, {}, {}, {}
    for n in ("w_in", "w_glu", "w_out", "w_up", "w_down"):
        grad[n] = g_big[n]
        delta[n], new_m[n], new_v[n] = _adamw(args[n], g_big[n], args["m_" + n], args["v_" + n], name="adamw_" + n)
    grad["conv_w"] = g_conv_w
    delta["conv_w"], new_m["conv_w"], new_v["conv_w"] = _adamw(conv_w, g_conv_w, m_conv_w, v_conv_w, name="adamw_conv_w")
    stepped = _adamw_small([args[n] for n in small_names], [g_small[n] for n in small_names],
                           [args["m_" + n] for n in small_names], [args["v_" + n] for n in small_names])
    for i, n in enumerate(small_names):
        grad[n] = g_small[n]
        delta[n], new_m[n], new_v[n] = stepped[3 * i:3 * i + 3]

    return (loss, dx[None], *[grad[n] for n in order], *[delta[n] for n in order], *[new_m[n] for n in order],
            *[new_v[n] for n in order])
```

```python
import math

import jax
import jax.numpy as jnp
from jax import lax
from jax.experimental import pallas as pl
from jax.experimental.pallas import tpu as pltpu

F32 = jnp.float32
BF16 = jnp.bfloat16

D_MODEL = 1024
HEADS = 8
HEAD_DIM = 64
ATTN_W = 512
SSM_W = 512
SSM_GROUPS = 32
SSM_GROUP = 16
SSM_STATE = 64
N_STATE = SSM_GROUPS * SSM_STATE
D_FF = 2816
IN_COLS = 2056
Z_COLS = 2176
F_COL0 = 1536
U_COL0 = 1544
EPS = 1e-6
NEG_INF = -1e30
N_CHIPS = 4
LANES = 128
SUBLANES = 8
SSM_CHUNKS = 4
CHUNK_U = SSM_W // SSM_CHUNKS
CHUNK_S = N_STATE // SSM_CHUNKS
HEADS_PER_STEP = 2
STRIP = 128
N_STRIPS = D_FF // STRIP

ROWS_IN, ROWS_GLU, ROWS_OUT, ROWS_UP, ROWS_DOWN = 514, 64, 256, 1408, 704
OFF_GLU = ROWS_IN
OFF_OUT = OFF_GLU + ROWS_GLU
OFF_UP = OFF_OUT + ROWS_OUT
OFF_DOWN = OFF_UP + ROWS_UP
OFF_SPARE = OFF_DOWN + ROWS_DOWN
PACK_ROWS = 2976
HALF_ROWS = PACK_ROWS // 2
CONVW_ROWS = 9

ADAM_LR = 0.001
ADAM_B1 = 0.9
ADAM_B2 = 0.999
ADAM_EPS = 1e-08
ADAM_WD = 0.01
ADAM_STEP = 10

VMEM_LIMIT = 56 * 1024 * 1024
MESH = pl.DeviceIdType.MESH


def _pallas(body, **kw):
    return pl.pallas_call(body, **kw)


def _pcall(body, *, name, out_shape, in_specs, out_specs, grid=(), scratch_shapes=(), dims=None):
    params = pltpu.CompilerParams(dimension_semantics=dims, vmem_limit_bytes=VMEM_LIMIT)
    return _pallas(body, name=name, grid=grid, in_specs=in_specs, out_specs=out_specs,
                   out_shape=out_shape, scratch_shapes=scratch_shapes, compiler_params=params)


def _sds(shape, dtype=F32):
    return jax.ShapeDtypeStruct(shape, dtype)


def _dot(a, b):
    return jnp.dot(a, b, preferred_element_type=F32)


def _dot_nt(a, b):
    return lax.dot_general(a, b, (((1,), (1,)), ((), ())), preferred_element_type=F32)


def _dot_tn(a, b):
    return lax.dot_general(a, b, (((0,), (0,)), ((), ())), preferred_element_type=F32)


def _split3(x):
    hi = x.astype(BF16)
    r = x - hi.astype(F32)
    mid = r.astype(BF16)
    lo = (r - mid.astype(F32)).astype(BF16)
    return hi, mid, lo


def _dot_exact_r(x, m01):
    hi, mid, lo = _split3(x)
    return _dot(hi, m01) + _dot(mid, m01) + _dot(lo, m01)


def _dot_exact_l(m01, x):
    hi, mid, lo = _split3(x)
    return _dot(m01, hi) + _dot(m01, mid) + _dot(m01, lo)


def _sigmoid(x):
    return 1.0 / (1.0 + jnp.exp(-x))


def _rms(x, g):
    r = lax.rsqrt(jnp.mean(x * x, axis=-1, keepdims=True) + EPS)
    return x * r * g


def _rms_bwd(x, g, dy):
    r = lax.rsqrt(jnp.mean(x * x, axis=-1, keepdims=True) + EPS)
    w = dy * g
    dx = r * w - x * (r * r * r) * jnp.mean(w * x, axis=-1, keepdims=True)
    dg = jnp.sum(dy * x * r, axis=0, keepdims=True)
    return dx, dg


_GELU_K = math.sqrt(2.0 / math.pi)
_GELU_C = 0.044715


def _gelu(y):
    return y * (0.5 * (1.0 + jnp.tanh(_GELU_K * (y + _GELU_C * (y * y * y)))))


def _gelu_grad(y):
    t = jnp.tanh(_GELU_K * (y + _GELU_C * (y * y * y)))
    return 0.5 * (1.0 + t) + 0.5 * y * (1.0 - t * t) * (_GELU_K * (1.0 + 3.0 * _GELU_C * y * y))


def _tile(n, pref):
    if n <= pref:
        return n
    divs = [t for t in range(LANES, n + 1, LANES) if n % t == 0]
    below = [t for t in divs if t <= pref]
    if below and 2 * below[-1] >= pref:
        return below[-1]
    above = [t for t in divs if t > pref]
    return above[0] if above else n


def _row_tile(s):
    return min(256, s)


def _mm(a, b, *, name, tm, tn, tk, ta=False, tb=False, a_parts=1, b_parts=1):
    if a_parts > 1:
        m, kk = a.shape[1], a.shape[2] * a_parts
    elif ta:
        kk, m = a.shape
    else:
        m, kk = a.shape
    if b_parts > 1:
        n = b.shape[2] * b_parts
    else:
        n = b.shape[0] if tb else b.shape[1]
    tm, tn, tk = _tile(m, tm), _tile(n // b_parts, tn), _tile(kk // a_parts, tk)
    k_per, n_per = kk // a_parts // tk, n // b_parts // tn

    def body(a_ref, b_ref, o_ref):
        k = pl.program_id(2)
        if ta:
            part = _dot_tn(a_ref[...], b_ref[...])
        elif tb:
            part = _dot_nt(a_ref[...], b_ref[...])
        else:
            part = _dot(a_ref[...], b_ref[...])

        @pl.when(k == 0)
        def _():
            o_ref[...] = part

        @pl.when(k > 0)
        def _():
            o_ref[...] += part

    if a_parts > 1:
        a_spec = pl.BlockSpec((None, tm, tk), lambda i, j, k: (k // k_per, i, k % k_per))
    else:
        a_spec = pl.BlockSpec((tk, tm), lambda i, j, k: (k, i)) if ta else pl.BlockSpec((tm, tk), lambda i, j, k: (i, k))
    if b_parts > 1:
        b_spec = pl.BlockSpec((None, tk, tn), lambda i, j, k: (j // n_per, k, j % n_per))
    else:
        b_spec = pl.BlockSpec((tn, tk), lambda i, j, k: (j, k)) if tb else pl.BlockSpec((tk, tn), lambda i, j, k: (k, j))
    return _pcall(body, name=name, grid=(m // tm, n // tn, kk // tk), in_specs=[a_spec, b_spec],
                  out_specs=pl.BlockSpec((tm, tn), lambda i, j, k: (i, j)), out_shape=_sds((m, n)),
                  dims=("parallel", "parallel", "arbitrary"))(a, b)


def _in_proj(x, g_mix, w_in_r):
    s = x.shape[0]
    tm = _row_tile(s)

    def body(x_ref, g_ref, w_ref, h_ref, z_ref):
        h = _rms(x_ref[...], g_ref[...]).astype(BF16)
        h_ref[...] = h
        z_ref[...] = _dot(h, w_ref[...])

    return _pcall(body, name="in_proj", grid=(s // tm,),
                  in_specs=[pl.BlockSpec((tm, D_MODEL), lambda i: (i, 0)), pl.BlockSpec((1, D_MODEL), lambda i: (0, 0)),
                            pl.BlockSpec((D_MODEL, Z_COLS), lambda i: (0, 0))],
                  out_specs=[pl.BlockSpec((tm, D_MODEL), lambda i: (i, 0)), pl.BlockSpec((tm, Z_COLS), lambda i: (i, 0))],
                  out_shape=[_sds((s, D_MODEL), BF16), _sds((s, Z_COLS))], dims=("parallel",))(x, g_mix, w_in_r)


def _split_heads(ref, val):
    for h in range(HEADS):
        ref[h] = val[:, h * HEAD_DIM:(h + 1) * HEAD_DIM].astype(ref.dtype)


def _merge_heads(ref):
    return jnp.concatenate([ref[h].astype(F32) for h in range(HEADS)], axis=-1)


def _forget_logits(z_ref, bf_ref):
    fl = z_ref[:, F_COL0:F_COL0 + LANES] + bf_ref[...]
    return jnp.where(lax.broadcasted_iota(jnp.int32, fl.shape, 1) < HEADS, fl, 0.0)


def _attn_prep(z, gq, gk, bf, gg):
    s = z.shape[0]
    tm = _row_tile(s)

    def body(z_ref, gq_ref, gk_ref, bf_ref, gg_ref, qn_ref, kn_ref, vb_ref, ub_ref, uf_ref, c_ref, carry_ref):
        i = pl.program_id(0)

        @pl.when(i == 0)
        def _():
            carry_ref[...] = jnp.zeros_like(carry_ref)

        gg_m = gg_ref[...]

        def head_norm(t, g):
            ssq = _dot_exact_r(t * t, gg_m)
            return t * lax.rsqrt(ssq * (1.0 / HEAD_DIM) + EPS) * g

        _split_heads(qn_ref, head_norm(z_ref[:, 0:ATTN_W], gq_ref[...]))
        _split_heads(kn_ref, head_norm(z_ref[:, ATTN_W:2 * ATTN_W], gk_ref[...]))
        _split_heads(vb_ref, z_ref[:, 2 * ATTN_W:3 * ATTN_W])
        u = z_ref[:, U_COL0:U_COL0 + SSM_W]
        uf_ref[...] = u
        ub_ref[...] = u.astype(BF16)
        fl = _forget_logits(z_ref, bf_ref)
        lf = jnp.minimum(fl, 0.0) - jnp.log1p(jnp.exp(-jnp.abs(fl)))
        row = lax.broadcasted_iota(jnp.int32, (tm, tm), 0)
        col = lax.broadcasted_iota(jnp.int32, (tm, tm), 1)
        tri = (row >= col).astype(BF16)
        c = _dot_exact_l(tri, lf) + carry_ref[...]
        c_ref[...] = c
        carry_ref[...] = c[tm - 1:tm, :]

    row_spec = lambda w: pl.BlockSpec((tm, w), lambda i: (i, 0))
    const = lambda shape: pl.BlockSpec(shape, lambda i: (0, 0))
    heads = pl.BlockSpec((HEADS, tm, HEAD_DIM), lambda i: (0, i, 0))
    return _pcall(body, name="attn_prep", grid=(s // tm,),
                  in_specs=[row_spec(Z_COLS), const((1, ATTN_W)), const((1, ATTN_W)), const((1, LANES)), const((ATTN_W, ATTN_W))],
                  out_specs=[heads] * 3 + [row_spec(SSM_W), row_spec(SSM_W), row_spec(LANES)],
                  out_shape=[_sds((HEADS, s, HEAD_DIM), BF16)] * 3 + [_sds((s, SSM_W), BF16), _sds((s, SSM_W)), _sds((s, LANES))],
                  scratch_shapes=[pltpu.VMEM((1, LANES), F32)], dims=("arbitrary",))(z, gq, gk, bf, gg)


def _attn_fwd(qh, kh, vh, crow, hosted=None):
    _, s, _ = qh.shape
    tq = _row_tile(s)
    scale = HEAD_DIM ** -0.5

    hp = HEADS_PER_STEP
    nq = s // tq
    fold = lambda t, op: op(t[:, :tq // 2], t[:, tq // 2:])

    def body(q_ref, k_ref, v_ref, c_ref, o_ref, lse_ref, s_s):
        i = pl.program_id(1)

        def first(j, ms, diagonal):
            off = pl.multiple_of(j * tq, tq)
            out = []
            for hh in range(hp):
                sc = _dot_nt(q_ref[hh], k_ref[hh, pl.ds(off, tq), :]) * scale - c_ref[hh, :, pl.ds(off, tq)]
                if diagonal:
                    causal = lax.broadcasted_iota(jnp.int32, (tq, tq), 1) <= lax.broadcasted_iota(jnp.int32, (tq, tq), 0)
                    sc = jnp.where(causal, sc, NEG_INF)
                s_s[hh, j] = sc
                out.append(jnp.maximum(ms[hh], fold(sc, jnp.maximum)))
            return tuple(out)

        ms = lax.fori_loop(0, i, lambda j, c: first(j, c, False), (jnp.full((tq, tq // 2), NEG_INF, F32),) * hp)
        ms = [jnp.max(t, axis=-1, keepdims=True) for t in first(i, ms, True)]

        def second(j, carry):
            rows = pl.ds(pl.multiple_of(j * tq, tq), tq)
            out = []
            for hh in range(hp):
                ls, acc = carry[hh]
                p = jnp.exp(s_s[hh, j] - ms[hh])
                out.append((ls + fold(p, jnp.add), acc + _dot(p.astype(BF16), v_ref[hh, rows, :])))
            return tuple(out)

        zero = (jnp.zeros((tq, tq // 2), F32), jnp.zeros((tq, HEAD_DIM), F32))
        for hh, (ls, acc) in enumerate(lax.fori_loop(0, i + 1, second, (zero,) * hp)):
            l = jnp.sum(ls, axis=-1, keepdims=True)
            o_ref[hh] = acc / l
            lse_ref[hh] = ms[hh] + jnp.log(l)

    blk = pl.BlockSpec((hp, tq, HEAD_DIM), lambda h, i: (h, i, 0))
    full = pl.BlockSpec((hp, s, HEAD_DIM), lambda h, i: (h, 0, 0))
    nh = HEADS // hp
    first = lambda: jnp.logical_and(pl.program_id(0) == 0, pl.program_id(1) == 0)
    last = lambda: jnp.logical_and(pl.program_id(0) == nh - 1, pl.program_id(1) == nq - 1)
    return _host_pcall(body, hosted, first, last, n_in=4, n_out=2, n_scratch=1, name="attn_fwd", grid=(nh, nq),
                       in_specs=[blk, full, full, pl.BlockSpec((hp, 1, s), lambda h, i: (h, 0, 0))],
                       out_specs=[blk, pl.BlockSpec((hp, tq, 1), lambda h, i: (h, i, 0))],
                       out_shape=[_sds((HEADS, s, HEAD_DIM)), _sds((HEADS, s, 1))],
                       scratch_shapes=[pltpu.VMEM((hp, nq, tq, tq), F32)],
                       dims=("parallel", "parallel"), operands=(qh, kh, vh, crow))


def _ssm_param_fn(lr, li, ls, br, bi):
    step = jnp.exp(ls)
    er = jnp.exp(lr * step)
    ab_re = er * jnp.cos(li * step)
    ab_im = er * jnp.sin(li * step)
    num_re = ab_re - 1.0
    num_im = ab_im
    den = lr * lr + li * li
    f_re = (num_re * lr + num_im * li) / den
    f_im = (num_im * lr - num_re * li) / den
    bb_re = f_re * br - f_im * bi
    bb_im = f_re * bi + f_im * br
    return ab_re, ab_im, bb_re, bb_im


_PARAM_SHAPE = (SSM_GROUPS * SSM_GROUP, SSM_STATE)


def _ssm_params(lr, li, ls, br, bi):
    def body(lr_ref, li_ref, ls_ref, br_ref, bi_ref, ar_ref, ai_ref, bbr_ref, bbi_ref):
        ar, ai, bbr, bbi = _ssm_param_fn(lr_ref[...], li_ref[...], ls_ref[...], br_ref[...], bi_ref[...])
        ar_ref[...] = ar
        ai_ref[...] = ai
        bbr_ref[...] = bbr
        bbi_ref[...] = bbi

    spec = pl.BlockSpec(_PARAM_SHAPE, lambda: (0, 0))
    return _pcall(body, name="ssm_params", in_specs=[spec] * 5, out_specs=[spec] * 4,
                  out_shape=[_sds(_PARAM_SHAPE)] * 4)(lr, li, ls, br, bi)


def _ssm_params_bwd(lr, li, ls, br, bi, dar, dai, dbbr, dbbi, expand_t):
    def body(lr_ref, li_ref, ls_ref, br_ref, bi_ref, dar_ref, dai_ref, dbbr_ref, dbbi_ref, et_ref,
             dlr_ref, dli_ref, dls_ref, dbr_ref, dbi_ref):
        _, vjp = jax.vjp(_ssm_param_fn, lr_ref[...], li_ref[...], ls_ref[...], br_ref[...], bi_ref[...])
        dlr, dli, dls, dbr, dbi = vjp((dar_ref[...], dai_ref[...], dbbr_ref[...], dbbi_ref[...]))
        et = et_ref[...]
        dlr_ref[...] = _dot_exact_l(et, dlr)
        dli_ref[...] = _dot_exact_l(et, dli)
        dls_ref[...] = jnp.sum(_dot_exact_l(et, dls), axis=-1, keepdims=True)
        dbr_ref[...] = dbr
        dbi_ref[...] = dbi

    spec = pl.BlockSpec(_PARAM_SHAPE, lambda: (0, 0))
    gspec = pl.BlockSpec((SSM_GROUPS, SSM_STATE), lambda: (0, 0))
    return _pcall(body, name="ssm_params_bwd",
                  in_specs=[spec] * 9 + [pl.BlockSpec((SSM_GROUPS, _PARAM_SHAPE[0]), lambda: (0, 0))],
                  out_specs=[gspec, gspec, pl.BlockSpec((SSM_GROUPS, 1), lambda: (0, 0)), spec, spec],
                  out_shape=[_sds((SSM_GROUPS, SSM_STATE))] * 2 + [_sds((SSM_GROUPS, 1))] + [_sds(_PARAM_SHAPE)] * 2,
                  )(lr, li, ls, br, bi, dar, dai, dbbr, dbbi, expand_t)


def _cmul(ar, ai, br, bi):
    return ar * br - ai * bi, ar * bi + ai * br


def _scan_consts(ar, ai, width, reverse):
    row = lax.broadcasted_iota(jnp.int32, (SUBLANES, width), 0)
    pw = [(ar, ai)]
    for _ in range(SUBLANES - 1):
        pw.append(_cmul(pw[-1][0], pw[-1][1], ar, ai))
    steps = []
    for d in (1, 2, 4):
        keep = (row < SUBLANES - d) if reverse else (row >= d)
        steps.append((d, jnp.where(keep, pw[d - 1][0], 0.0), jnp.where(keep, pw[d - 1][1], 0.0)))
    pr = jnp.zeros((SUBLANES, width), F32)
    pi = jnp.zeros((SUBLANES, width), F32)
    for r in range(SUBLANES):
        e = (SUBLANES - r) if reverse else (r + 1)
        pr = jnp.where(row == r, pw[e - 1][0], pr)
        pi = jnp.where(row == r, pw[e - 1][1], pi)
    return steps, pr, pi


def _scan_tile(xr, xi, cr, ci, consts, reverse):
    steps, pr, pi = consts
    for d, mr, mi in steps:
        sh = (SUBLANES - d) if reverse else d
        sr = pltpu.roll(xr, sh, 0)
        si = pltpu.roll(xi, sh, 0)
        xr, xi = xr + mr * sr - mi * si, xi + mr * si + mi * sr
    return xr + pr * cr - pi * ci, xi + pr * ci + pi * cr


def _ssm_fwd(ub, uf, bbr, bbi, ar, ai, ccr, cci, dsk, hosted=None):
    s = ub.shape[0]
    tm = _row_tile(s)
    nt = tm // SUBLANES

    def body(ub_ref, u_ref, bbr_ref, bbi_ref, ar_ref, ai_ref, ccr_ref, cci_ref, dsk_ref,
             xr_ref, xi_ref, y_ref, cr_s, ci_s):
        i = pl.program_id(1)

        @pl.when(i == 0)
        def _():
            cr_s[...] = jnp.zeros_like(cr_s)
            ci_s[...] = jnp.zeros_like(ci_s)

        u_b = ub_ref[...]
        xr_ref[...] = _dot(u_b, bbr_ref[0])
        xi_ref[...] = _dot(u_b, bbi_ref[0])
        consts = _scan_consts(ar_ref[0], ai_ref[0], CHUNK_S, False)

        def tile(k, carry):
            cr, ci = carry
            sl = pl.ds(pl.multiple_of(k * SUBLANES, SUBLANES), SUBLANES)
            xr, xi = _scan_tile(xr_ref[sl, :], xi_ref[sl, :], cr, ci, consts, False)
            xr_ref[sl, :] = xr
            xi_ref[sl, :] = xi
            return xr[SUBLANES - 1:SUBLANES, :], xi[SUBLANES - 1:SUBLANES, :]

        cr, ci = lax.fori_loop(0, nt, tile, (cr_s[...], ci_s[...]))
        cr_s[...] = cr
        ci_s[...] = ci
        y_ref[...] = (_dot(xr_ref[...].astype(BF16), ccr_ref[0]) - _dot(xi_ref[...].astype(BF16), cci_ref[0])
                      + dsk_ref[...] * u_ref[...])

    wspec = lambda a, b: pl.BlockSpec((1, a, b), lambda j, i: (j, 0, 0))
    nb = s // tm
    first = lambda: jnp.logical_and(pl.program_id(0) == 0, pl.program_id(1) == 0)
    last = lambda: jnp.logical_and(pl.program_id(0) == SSM_CHUNKS - 1, pl.program_id(1) == nb - 1)
    return _host_pcall(
        body, hosted, first, last, n_in=9, n_out=3, n_scratch=2, name="ssm_fwd", grid=(SSM_CHUNKS, nb),
        in_specs=[pl.BlockSpec((tm, CHUNK_U), lambda j, i: (i, j)),
                  pl.BlockSpec((tm, CHUNK_U), lambda j, i: (i, j)),
                  wspec(CHUNK_U, CHUNK_S), wspec(CHUNK_U, CHUNK_S), wspec(1, CHUNK_S), wspec(1, CHUNK_S),
                  wspec(CHUNK_S, CHUNK_U), wspec(CHUNK_S, CHUNK_U),
                  pl.BlockSpec((1, CHUNK_U), lambda j, i: (0, j))],
        out_specs=[pl.BlockSpec((tm, CHUNK_S), lambda j, i: (i, j)), pl.BlockSpec((tm, CHUNK_S), lambda j, i: (i, j)),
                   pl.BlockSpec((tm, CHUNK_U), lambda j, i: (i, j))],
        out_shape=[_sds((s, N_STATE)), _sds((s, N_STATE)), _sds((s, SSM_W))],
        scratch_shapes=[pltpu.VMEM((1, CHUNK_S), F32)] * 2,
        dims=("parallel", "arbitrary"), operands=(ub, uf, bbr, bbi, ar, ai, ccr, cci, dsk))


def _ssm_glu(y, w_glu, b_glu):
    ge = _gelu(y)
    sg = _sigmoid(_dot(ge.astype(BF16), w_glu) + b_glu)
    return ge, sg


def _mix_out(y, att, x, w_glu, b_glu, g_att, g_ssm, w_out, g_ffn, hosted=None):
    s = x.shape[0]
    tm = _row_tile(s)

    def body(y_ref, att_ref, x_ref, wg_ref, bg_ref, ga_ref, gs_ref, wo_ref, gf_ref, x1_ref, mix_ref, h2_ref):
        ge, sg = _ssm_glu(y_ref[...], wg_ref[...], bg_ref[...])
        ms = _rms(ge * sg, gs_ref[...]).astype(BF16)
        ma = _rms(_merge_heads(att_ref), ga_ref[...]).astype(BF16)
        mix_ref[:, 0:ATTN_W] = ma
        mix_ref[:, ATTN_W:D_MODEL] = ms
        x1 = x_ref[...] + (_dot(ma, wo_ref[0:ATTN_W, :]) + _dot(ms, wo_ref[ATTN_W:D_MODEL, :]))
        x1_ref[...] = x1
        h2_ref[...] = _rms(x1, gf_ref[...]).astype(BF16)

    row = lambda w: pl.BlockSpec((tm, w), lambda i: (i, 0))
    const = lambda a, b: pl.BlockSpec((a, b), lambda i: (0, 0))
    nb = s // tm
    return _host_pcall(body, hosted, lambda: pl.program_id(0) == 0, lambda: pl.program_id(0) == nb - 1,
                       n_in=9, n_out=3, n_scratch=0, name="mix_out", grid=(nb,),
                       in_specs=[row(SSM_W), pl.BlockSpec((HEADS, tm, HEAD_DIM), lambda i: (0, i, 0)), row(D_MODEL),
                                 const(SSM_W, SSM_W), const(1, SSM_W),
                                 const(1, ATTN_W), const(1, SSM_W), const(D_MODEL, D_MODEL), const(1, D_MODEL)],
                       out_specs=[row(D_MODEL)] * 3,
                       out_shape=[_sds((s, D_MODEL)), _sds((s, D_MODEL), BF16), _sds((s, D_MODEL), BF16)],
                       scratch_shapes=[], dims=("parallel",), operands=(y, att, x, w_glu, b_glu, g_att, g_ssm, w_out, g_ffn))


CONV_CHUNK = 64


def _conv_rows(pad_ref, w, b, r0, n):
    y = b + pad_ref[pl.ds(r0 + SUBLANES - 2, n), :] * w[0:1, :]
    y = y + pad_ref[pl.ds(r0 + SUBLANES - 1, n), :] * w[1:2, :]
    return y + pad_ref[pl.ds(r0 + SUBLANES, n), :] * w[2:3, :]


def _fill_front_pad(pad_ref, strip_ref, s):
    pad_ref[0:SUBLANES, :] = jnp.zeros((SUBLANES, STRIP), F32)
    for r0 in range(0, s, CONV_CHUNK):
        pad_ref[pl.ds(SUBLANES + r0, CONV_CHUNK), :] = strip_ref[pl.ds(r0, CONV_CHUNK), :]


def _conv_act(up, conv_w, conv_b):
    s = up.shape[0]

    def body(ug_ref, uv_ref, wg_ref, wv_ref, bg_ref, bv_ref, act_ref, pg_ref, pv_ref):
        _fill_front_pad(pg_ref, ug_ref, s)
        _fill_front_pad(pv_ref, uv_ref, s)
        wg, wv, bg, bv = wg_ref[...], wv_ref[...], bg_ref[...], bv_ref[...]
        for r0 in range(0, s, CONV_CHUNK):
            hg = _conv_rows(pg_ref, wg, bg, r0, CONV_CHUNK)
            hv = _conv_rows(pv_ref, wv, bv, r0, CONV_CHUNK)
            act_ref[pl.ds(r0, CONV_CHUNK), :] = (hg * _sigmoid(hg) * hv).astype(BF16)

    strip = lambda off: pl.BlockSpec((s, STRIP), lambda j: (0, j + off))
    wsp = lambda off: pl.BlockSpec((3, STRIP), lambda j: (0, j + off))
    bsp = lambda off: pl.BlockSpec((1, STRIP), lambda j: (0, j + off))
    return _pcall(body, name="conv_act", grid=(N_STRIPS,),
                  in_specs=[strip(0), strip(N_STRIPS), wsp(0), wsp(N_STRIPS), bsp(0), bsp(N_STRIPS)],
                  out_specs=pl.BlockSpec((s, STRIP), lambda j: (0, j)), out_shape=_sds((s, D_FF), BF16),
                  scratch_shapes=[pltpu.VMEM((s + SUBLANES, STRIP), F32)] * 2,
                  dims=("parallel",))(up, up, conv_w, conv_w, conv_b, conv_b)


def _down_loss(act, w_down, x1, tgt):
    s = x1.shape[0]
    tm = _row_tile(s)

    def body(a_ref, w_ref, x1_ref, t_ref, dy_ref, dyb_ref, loss_ref):
        i = pl.program_id(0)

        @pl.when(i == 0)
        def _():
            loss_ref[...] = jnp.zeros_like(loss_ref)

        diff = x1_ref[...] + _dot(a_ref[...], w_ref[...]) - t_ref[...]
        dy = diff * (1.0 / D_MODEL)
        dy_ref[...] = dy
        dyb_ref[...] = dy.astype(BF16)
        loss_ref[...] += 0.5 * jnp.sum(diff * dy)

    row = lambda w: pl.BlockSpec((tm, w), lambda i: (i, 0))
    return _pcall(body, name="down_loss", grid=(s // tm,),
                  in_specs=[row(D_FF), pl.BlockSpec((D_FF, D_MODEL), lambda i: (0, 0)), row(D_MODEL), row(D_MODEL)],
                  out_specs=[row(D_MODEL), row(D_MODEL), pl.BlockSpec((SUBLANES, LANES), lambda i: (0, 0))],
                  out_shape=[_sds((s, D_MODEL)), _sds((s, D_MODEL), BF16), _sds((SUBLANES, LANES))],
                  dims=("arbitrary",))(act, w_down, x1, tgt)


def _conv_act_bwd(up, dact, conv_w, conv_b):
    s = up.shape[0]
    ch = CONV_CHUNK

    def body(ug_ref, uv_ref, da_ref, wg_ref, wv_ref, bg_ref, bv_ref, dup_ref, dcw_ref, pg_ref, pv_ref, dg_ref, dv_ref):
        _fill_front_pad(pg_ref, ug_ref, s)
        _fill_front_pad(pv_ref, uv_ref, s)
        zero = jnp.zeros((SUBLANES, STRIP), F32)
        dg_ref[pl.ds(s, SUBLANES), :] = zero
        dv_ref[pl.ds(s, SUBLANES), :] = zero
        wg, wv, bg, bv = wg_ref[...], wv_ref[...], bg_ref[...], bv_ref[...]
        tile_sum = lambda t: jnp.sum(t.reshape(ch // SUBLANES, SUBLANES, STRIP), axis=0)
        accs = [[zero] * 4, [zero] * 4]
        for r0 in range(0, s, ch):
            hg = _conv_rows(pg_ref, wg, bg, r0, ch)
            hv = _conv_rows(pv_ref, wv, bv, r0, ch)
            sg = _sigmoid(hg)
            da = da_ref[pl.ds(r0, ch), :]
            dhs = (da * hv * (sg * (1.0 + hg * (1.0 - sg))), da * (hg * sg))
            for half, (dh, d_ref, p_ref) in enumerate(zip(dhs, (dg_ref, dv_ref), (pg_ref, pv_ref))):
                d_ref[pl.ds(r0, ch), :] = dh
                for k in range(3):
                    accs[half][k] = accs[half][k] + tile_sum(dh * p_ref[pl.ds(r0 + SUBLANES - 2 + k, ch), :])
                accs[half][3] = accs[half][3] + tile_sum(dh)
        for half, (d_ref, w) in enumerate(((dg_ref, wg), (dv_ref, wv))):
            for r0 in range(0, s, ch):
                dup = (d_ref[pl.ds(r0, ch), :] * w[2:3, :] + d_ref[pl.ds(r0 + 1, ch), :] * w[1:2, :]
                       + d_ref[pl.ds(r0 + 2, ch), :] * w[0:1, :])
                dup_ref[half, pl.ds(r0, ch), :] = dup.astype(BF16)
            rid = lax.broadcasted_iota(jnp.int32, (SUBLANES, STRIP), 0)
            out = zero
            for k in range(4):
                out = jnp.where(rid == k, jnp.sum(accs[half][k], axis=0, keepdims=True), out)
            dcw_ref[half] = out

    strip = lambda off: pl.BlockSpec((s, STRIP), lambda j: (0, j + off))
    wsp = lambda off: pl.BlockSpec((3, STRIP), lambda j: (0, j + off))
    bsp = lambda off: pl.BlockSpec((1, STRIP), lambda j: (0, j + off))
    return _pcall(body, name="conv_act_bwd", grid=(N_STRIPS,),
                  in_specs=[strip(0), strip(N_STRIPS), strip(0), wsp(0), wsp(N_STRIPS), bsp(0), bsp(N_STRIPS)],
                  out_specs=[pl.BlockSpec((2, s, STRIP), lambda j: (0, 0, j)), pl.BlockSpec((2, SUBLANES, STRIP), lambda j: (0, 0, j))],
                  out_shape=[_sds((2, s, D_FF), BF16), _sds((2, SUBLANES, D_FF))],
                  scratch_shapes=[pltpu.VMEM((s + SUBLANES, STRIP), F32)] * 4,
                  dims=("parallel",))(up, up, dact, conv_w, conv_w, conv_b, conv_b)


def _mix_bwd(dy, dh2, x1, g_ffn, w_out, y, att, w_glu, b_glu, g_att, g_ssm):
    s = dy.shape[0]
    tm = _row_tile(s)

    def body(dy_ref, dh2_ref, x1_ref, gf_ref, wo_ref, y_ref, att_ref, wg_ref, bg_ref, ga_ref, gs_ref,
             dx1_ref, dx1b_ref, datt_ref, dys_ref, dwg_ref, dgf_ref, dga_ref, dgs_ref, dbg_ref):
        i = pl.program_id(0)

        @pl.when(i == 0)
        def _():
            for r in (dwg_ref, dgf_ref, dga_ref, dgs_ref, dbg_ref):
                r[...] = jnp.zeros_like(r)

        dxn, dgf = _rms_bwd(x1_ref[...], gf_ref[...], dh2_ref[...])
        dx1 = dy_ref[...] + dxn
        dx1_ref[...] = dx1
        dx1b = dx1.astype(BF16)
        dx1b_ref[...] = dx1b
        dgf_ref[...] += dgf
        dma = _dot_nt(dx1b, wo_ref[0:ATTN_W, :])
        dms = _dot_nt(dx1b, wo_ref[ATTN_W:D_MODEL, :])
        datt, dga = _rms_bwd(_merge_heads(att_ref), ga_ref[...], dma)
        _split_heads(datt_ref, datt)
        dga_ref[...] += dga
        yv = y_ref[...]
        ge, sg = _ssm_glu(yv, wg_ref[...], bg_ref[...])
        dssm, dgs = _rms_bwd(ge * sg, gs_ref[...], dms)
        dgs_ref[...] += dgs
        dgl = dssm * ge * sg * (1.0 - sg)
        dglb = dgl.astype(BF16)
        dge = dssm * sg + _dot_nt(dglb, wg_ref[...])
        dbg_ref[...] += jnp.sum(dgl, axis=0, keepdims=True)
        dwg_ref[...] += _dot_tn(ge.astype(BF16), dglb)
        dys_ref[...] = dge * _gelu_grad(yv)

    row = lambda w: pl.BlockSpec((tm, w), lambda i: (i, 0))
    const = lambda a, b: pl.BlockSpec((a, b), lambda i: (0, 0))
    heads = pl.BlockSpec((HEADS, tm, HEAD_DIM), lambda i: (0, i, 0))
    return _pcall(body, name="mix_bwd", grid=(s // tm,),
                  in_specs=[row(D_MODEL), row(D_MODEL), row(D_MODEL), const(1, D_MODEL), const(D_MODEL, D_MODEL), row(SSM_W),
                            heads, const(SSM_W, SSM_W), const(1, SSM_W), const(1, ATTN_W), const(1, SSM_W)],
                  out_specs=[row(D_MODEL), row(D_MODEL), heads, row(SSM_W), const(SSM_W, SSM_W), const(1, D_MODEL),
                             const(1, ATTN_W), const(1, SSM_W), const(1, SSM_W)],
                  out_shape=[_sds((s, D_MODEL)), _sds((s, D_MODEL), BF16), _sds((HEADS, s, HEAD_DIM)), _sds((s, SSM_W)),
                             _sds((SSM_W, SSM_W)), _sds((1, D_MODEL)), _sds((1, ATTN_W)), _sds((1, SSM_W)), _sds((1, SSM_W))],
                  dims=("arbitrary",))(dy, dh2, x1, g_ffn, w_out, y, att, w_glu, b_glu, g_att, g_ssm)


def _ssm_bwd(dys, uf, ub, xr, xi, bbr, bbi, ar, ai, ccr, cci, dsk, hosted=None):
    s = dys.shape[0]
    tm = _row_tile(s)
    nb = s // tm
    nt = tm // SUBLANES

    def body(dy_ref, u_ref, ub_ref, xr_ref, xi_ref, xrp_ref, xip_ref, bbr_ref, bbi_ref, ar_ref, ai_ref, ccr_ref,
             cci_ref, dsk_ref, du_ref, dbbr_ref, dbbi_ref, dccr_ref, dcci_ref, dar_ref, dai_ref, dd_ref,
             gr_s, gi_s, cr_s, ci_s, accr_s, acci_s):
        i = pl.program_id(1)
        first_block = i == nb - 1

        @pl.when(i == 0)
        def _():
            for r in (cr_s, ci_s, accr_s, acci_s, dbbr_ref, dbbi_ref, dccr_ref, dcci_ref, dd_ref):
                r[...] = jnp.zeros_like(r)

        dy = dy_ref[...]
        dyb = dy.astype(BF16)
        gr_s[...] = _dot_nt(dyb, ccr_ref[0])
        gi_s[...] = -_dot_nt(dyb, cci_ref[0])
        consts = _scan_consts(ar_ref[0], -ai_ref[0], CHUNK_S, True)
        row = lax.broadcasted_iota(jnp.int32, (SUBLANES, CHUNK_S), 0)

        def tile(kk, carry):
            cr, ci, accr, acci = carry
            k = nt - 1 - kk
            sl = pl.ds(pl.multiple_of(k * SUBLANES, SUBLANES), SUBLANES)
            gr, gi = _scan_tile(gr_s[sl, :], gi_s[sl, :], cr, ci, consts, True)
            gr_s[sl, :] = gr
            gi_s[sl, :] = gi
            slp = pl.ds(pl.multiple_of(jnp.maximum(k - 1, 0) * SUBLANES, SUBLANES), SUBLANES)
            inner = k > 0
            pr_t = jnp.where(inner, xr_ref[slp, :], xrp_ref[...])
            pi_t = jnp.where(inner, xi_ref[slp, :], xip_ref[...])
            live = jnp.logical_or(inner, jnp.logical_not(first_block))
            top_r = jnp.where(live, pltpu.roll(pr_t, 1, 0), 0.0)
            top_i = jnp.where(live, pltpu.roll(pi_t, 1, 0), 0.0)
            xpr = jnp.where(row == 0, top_r, pltpu.roll(xr_ref[sl, :], 1, 0))
            xpi = jnp.where(row == 0, top_i, pltpu.roll(xi_ref[sl, :], 1, 0))
            accr = accr + gr * xpr + gi * xpi
            acci = acci + gi * xpr - gr * xpi
            return gr[0:1, :], gi[0:1, :], accr, acci

        zeros = jnp.zeros((SUBLANES, CHUNK_S), F32)
        cr, ci, accr, acci = lax.fori_loop(0, nt, tile, (cr_s[...], ci_s[...], zeros, zeros))
        cr_s[...] = cr
        ci_s[...] = ci
        accr_s[...] += accr
        acci_s[...] += acci
        grb = gr_s[...].astype(BF16)
        gib = gi_s[...].astype(BF16)
        u_b = ub_ref[...]
        du_ref[...] = _dot_nt(grb, bbr_ref[0]) + _dot_nt(gib, bbi_ref[0]) + dsk_ref[...] * dy
        dbbr_ref[0] += _dot_tn(u_b, grb)
        dbbi_ref[0] += _dot_tn(u_b, gib)
        dccr_ref[0] += _dot_tn(xr_ref[...].astype(BF16), dyb)
        dcci_ref[0] -= _dot_tn(xi_ref[...].astype(BF16), dyb)
        dd_ref[...] += jnp.sum(dy * u_ref[...], axis=0, keepdims=True)

        @pl.when(i == nb - 1)
        def _():
            dar_ref[0] = jnp.sum(accr_s[...], axis=0, keepdims=True)
            dai_ref[0] = jnp.sum(acci_s[...], axis=0, keepdims=True)

    tiles_per_block = tm // SUBLANES
    rb = lambda i: nb - 1 - i
    wspec = lambda a, b: pl.BlockSpec((1, a, b), lambda j, i: (j, 0, 0))
    xblk = pl.BlockSpec((tm, CHUNK_S), lambda j, i: (rb(i), j))
    xprev = pl.BlockSpec((SUBLANES, CHUNK_S), lambda j, i: (jnp.maximum(rb(i) * tiles_per_block - 1, 0), j))
    ublk = pl.BlockSpec((tm, CHUNK_U), lambda j, i: (rb(i), j))
    first = lambda: jnp.logical_and(pl.program_id(0) == 0, pl.program_id(1) == 0)
    last = lambda: jnp.logical_and(pl.program_id(0) == SSM_CHUNKS - 1, pl.program_id(1) == nb - 1)
    return _host_pcall(
        body, hosted, first, last, n_in=14, n_out=8, n_scratch=6, name="ssm_bwd", grid=(SSM_CHUNKS, nb),
        in_specs=[ublk, ublk, ublk, xblk, xblk, xprev, xprev,
                  wspec(CHUNK_U, CHUNK_S), wspec(CHUNK_U, CHUNK_S), wspec(1, CHUNK_S), wspec(1, CHUNK_S),
                  wspec(CHUNK_S, CHUNK_U), wspec(CHUNK_S, CHUNK_U), pl.BlockSpec((1, CHUNK_U), lambda j, i: (0, j))],
        out_specs=[ublk, wspec(CHUNK_U, CHUNK_S), wspec(CHUNK_U, CHUNK_S), wspec(CHUNK_S, CHUNK_U),
                   wspec(CHUNK_S, CHUNK_U), wspec(1, CHUNK_S), wspec(1, CHUNK_S),
                   pl.BlockSpec((1, CHUNK_U), lambda j, i: (0, j))],
        out_shape=[_sds((s, SSM_W)), _sds((SSM_CHUNKS, CHUNK_U, CHUNK_S)), _sds((SSM_CHUNKS, CHUNK_U, CHUNK_S)),
                   _sds((SSM_CHUNKS, CHUNK_S, CHUNK_U)), _sds((SSM_CHUNKS, CHUNK_S, CHUNK_U)),
                   _sds((SSM_CHUNKS, 1, CHUNK_S)), _sds((SSM_CHUNKS, 1, CHUNK_S)), _sds((1, SSM_W))],
        scratch_shapes=[pltpu.VMEM((tm, CHUNK_S), F32)] * 2 + [pltpu.VMEM((1, CHUNK_S), F32)] * 2
                       + [pltpu.VMEM((SUBLANES, CHUNK_S), F32)] * 2,
        dims=("parallel", "arbitrary"), operands=(dys, uf, ub, xr, xi, xr, xi, bbr, bbi, ar, ai, ccr, cci, dsk))


def _attn_probs(q, ks, cs, lse, scale, diagonal):
    p = jnp.exp(_dot_nt(q, ks) * scale - cs - lse)
    if diagonal:
        tq, tk = p.shape
        causal = lax.broadcasted_iota(jnp.int32, (tq, tk), 1) <= lax.broadcasted_iota(jnp.int32, (tq, tk), 0)
        p = jnp.where(causal, p, 0.0)
    return p


def _attn_bwd(qh, kh, vh, crow, lse, doh, hosted=None):
    _, s, _ = qh.shape
    tq = _row_tile(s)
    nq = s // tq
    scale = HEAD_DIM ** -0.5
    hp = HEADS_PER_STEP

    def body(q_ref, k_ref, v_ref, c_ref, lse_ref, do_ref, dq_ref, dk_ref, dv_ref, dc_ref, p_s, dp_s):
        i = pl.program_id(1)

        @pl.when(i == 0)
        def _():
            for r in (dk_ref, dv_ref, dc_ref):
                r[...] = jnp.zeros_like(r)

        dobs = [do_ref[hh].astype(BF16) for hh in range(hp)]

        def first(j, dls, diagonal):
            off = pl.multiple_of(j * tq, tq)
            out = []
            for hh in range(hp):
                p = _attn_probs(q_ref[hh], k_ref[hh, pl.ds(off, tq), :], c_ref[hh, :, pl.ds(off, tq)], lse_ref[hh],
                                scale, diagonal)
                dp = _dot_nt(dobs[hh], v_ref[hh, pl.ds(off, tq), :])
                p_s[hh, j] = p
                dp_s[hh, j] = dp
                out.append(dls[hh] + jnp.sum(p * dp, axis=-1, keepdims=True))
            return tuple(out)

        zero_col = jnp.zeros((tq, 1), F32)
        dls = lax.fori_loop(0, i, lambda j, c: first(j, c, False), (zero_col,) * hp)
        dls = first(i, dls, True)

        def second(j, dqs):
            rows = pl.ds(pl.multiple_of(j * tq, tq), tq)
            out = []
            for hh in range(hp):
                p = p_s[hh, j]
                ds = p * (dp_s[hh, j] - dls[hh])
                dsb = ds.astype(BF16)
                dv_ref[hh, rows, :] += _dot_tn(p.astype(BF16), dobs[hh])
                dk_ref[hh, rows, :] += _dot_tn(dsb, q_ref[hh]) * scale
                dc_ref[hh, :, rows] -= jnp.sum(ds, axis=0, keepdims=True)
                out.append(dqs[hh] + _dot(dsb, k_ref[hh, rows, :]))
            return tuple(out)

        dqs = lax.fori_loop(0, i + 1, second, (jnp.zeros((tq, HEAD_DIM), F32),) * hp)
        for hh in range(hp):
            dq_ref[hh] = dqs[hh] * scale

    blk = pl.BlockSpec((hp, tq, HEAD_DIM), lambda h, i: (h, i, 0))
    full = pl.BlockSpec((hp, s, HEAD_DIM), lambda h, i: (h, 0, 0))
    crow_spec = pl.BlockSpec((hp, 1, s), lambda h, i: (h, 0, 0))
    nh = HEADS // hp
    first = lambda: jnp.logical_and(pl.program_id(0) == 0, pl.program_id(1) == 0)
    last = lambda: jnp.logical_and(pl.program_id(0) == nh - 1, pl.program_id(1) == nq - 1)
    return _host_pcall(body, hosted, first, last, n_in=6, n_out=4, n_scratch=2, name="attn_bwd", grid=(nh, nq),
                       in_specs=[blk, full, full, crow_spec, pl.BlockSpec((hp, tq, 1), lambda h, i: (h, i, 0)), blk],
                       out_specs=[blk, full, full, crow_spec],
                       out_shape=[_sds((HEADS, s, HEAD_DIM))] * 3 + [_sds((HEADS, 1, s))],
                       scratch_shapes=[pltpu.VMEM((hp, nq, tq, tq), F32)] * 2,
                       dims=("parallel", "arbitrary"), operands=(qh, kh, vh, crow, lse, doh))


def _prep_bwd(z, dqn, dkn, dv, du, dc, gq, gk, bf, gg):
    s = z.shape[0]
    tm = _row_tile(s)
    nb = s // tm

    def body(z_ref, dqn_ref, dkn_ref, dv_ref, du_ref, dc_ref, gq_ref, gk_ref, bf_ref, gg_ref,
             dz_ref, dgq_ref, dgk_ref, dbf_ref, carry_ref):
        i = pl.program_id(0)

        @pl.when(i == 0)
        def _():
            for r in (dgq_ref, dgk_ref, dbf_ref, carry_ref):
                r[...] = jnp.zeros_like(r)

        gg_m = gg_ref[...]

        def head_norm_bwd(t, g, dn):
            r = lax.rsqrt(_dot_exact_r(t * t, gg_m) * (1.0 / HEAD_DIM) + EPS)
            w = dn * g
            mean_wt = _dot_exact_r(w * t, gg_m) * (1.0 / HEAD_DIM)
            return r * w - t * (r * r * r) * mean_wt, jnp.sum(dn * t * r, axis=0, keepdims=True)

        dq, dgq = head_norm_bwd(z_ref[:, 0:ATTN_W], gq_ref[...], _merge_heads(dqn_ref))
        dk, dgk = head_norm_bwd(z_ref[:, ATTN_W:2 * ATTN_W], gk_ref[...], _merge_heads(dkn_ref))
        dgq_ref[...] += dgq
        dgk_ref[...] += dgk
        row = lax.broadcasted_iota(jnp.int32, (tm, tm), 0)
        col = lax.broadcasted_iota(jnp.int32, (tm, tm), 1)
        triu = (col >= row).astype(BF16)
        dlf = _dot_exact_l(triu, dc_ref[...]) + carry_ref[...]
        carry_ref[...] = dlf[0:1, :]
        df = dlf * _sigmoid(-_forget_logits(z_ref, bf_ref))
        dbf_ref[...] += jnp.sum(df, axis=0, keepdims=True)
        dz_ref[:, 0:ATTN_W] = dq.astype(BF16)
        dz_ref[:, ATTN_W:2 * ATTN_W] = dk.astype(BF16)
        dz_ref[:, 2 * ATTN_W:3 * ATTN_W] = _merge_heads(dv_ref).astype(BF16)
        tail = jnp.concatenate([df[:, :HEADS], du_ref[...], jnp.zeros((tm, Z_COLS - IN_COLS), F32)], axis=-1)
        dz_ref[:, F_COL0:Z_COLS] = tail.astype(BF16)

    row_spec = lambda w: pl.BlockSpec((tm, w), lambda i: (nb - 1 - i, 0))
    const = lambda shape: pl.BlockSpec(shape, lambda i: (0, 0))
    return _pcall(body, name="prep_bwd", grid=(nb,),
                  in_specs=[row_spec(Z_COLS)] + [pl.BlockSpec((HEADS, tm, HEAD_DIM), lambda i: (0, nb - 1 - i, 0))] * 3
                           + [row_spec(ATTN_W), row_spec(LANES), const((1, ATTN_W)),
                              const((1, ATTN_W)), const((1, LANES)), const((ATTN_W, ATTN_W))],
                  out_specs=[row_spec(Z_COLS), const((1, ATTN_W)), const((1, ATTN_W)), const((1, LANES))],
                  out_shape=[_sds((s, Z_COLS), BF16), _sds((1, ATTN_W)), _sds((1, ATTN_W)), _sds((1, LANES))],
                  scratch_shapes=[pltpu.VMEM((1, LANES), F32)], dims=("arbitrary",))(z, dqn, dkn, dv, du, dc, gq, gk, bf, gg)


def _in_norm_bwd(x, g_mix, dh, dx1):
    s = x.shape[0]
    tm = _row_tile(s)

    def body(x_ref, g_ref, dh_ref, dx1_ref, dx_ref, dg_ref):
        i = pl.program_id(0)

        @pl.when(i == 0)
        def _():
            dg_ref[...] = jnp.zeros_like(dg_ref)

        dxn, dg = _rms_bwd(x_ref[...], g_ref[...], dh_ref[...])
        dx_ref[...] = dx1_ref[...] + dxn
        dg_ref[...] += dg

    row = pl.BlockSpec((tm, D_MODEL), lambda i: (i, 0))
    vec = pl.BlockSpec((1, D_MODEL), lambda i: (0, 0))
    return _pcall(body, name="in_norm_bwd", grid=(s // tm,), in_specs=[row, vec, row, row], out_specs=[row, vec],
                  out_shape=[_sds((s, D_MODEL)), _sds((1, D_MODEL))], dims=("arbitrary",))(x, g_mix, dh, dx1)


def _adamw_refs(w_ref, g_ref, m_ref, v_ref, d_ref, mo_ref, vo_ref):
    gv = g_ref[...]
    mn = ADAM_B1 * m_ref[...] + (1.0 - ADAM_B1) * gv
    vn = ADAM_B2 * v_ref[...] + (1.0 - ADAM_B2) * (gv * gv)
    m_hat = mn / (1.0 - ADAM_B1 ** ADAM_STEP)
    v_hat = vn / (1.0 - ADAM_B2 ** ADAM_STEP)
    d_ref[...] = -ADAM_LR * (m_hat / (jnp.sqrt(v_hat) + ADAM_EPS) + ADAM_WD * w_ref[...])
    mo_ref[...] = mn
    vo_ref[...] = vn


def _adamw_small(ws, gs, ms, vs):
    n = len(ws)

    def body(*refs):
        ins, outs = refs[:4 * n], refs[4 * n:]
        for i in range(n):
            _adamw_refs(ins[i], ins[n + i], ins[2 * n + i], ins[3 * n + i], *outs[3 * i:3 * i + 3])

    vm = pl.BlockSpec(memory_space=pltpu.VMEM)
    out_shape = [_sds(w.shape) for w in ws for _ in range(3)]
    return _pallas(body, name="adamw_small", in_specs=[vm] * (4 * n), out_specs=[vm] * (3 * n), out_shape=out_shape,
                   compiler_params=pltpu.CompilerParams(vmem_limit_bytes=VMEM_LIMIT))(*ws, *gs, *ms, *vs)


def _adamw(w, g, m, v, *, name):
    r, c = w.shape
    tr = r
    for cand in (256, 176, 128, 64):
        if r > cand and r % cand == 0:
            tr = cand
            break

    def body(w_ref, g_ref, m_ref, v_ref, d_ref, mo_ref, vo_ref):
        _adamw_refs(w_ref, g_ref, m_ref, v_ref, d_ref, mo_ref, vo_ref)

    spec = pl.BlockSpec((tr, c), lambda i: (i, 0))
    return _pcall(body, name=name, grid=(r // tr,), in_specs=[spec] * 4, out_specs=[spec] * 3,
                  out_shape=[_sds((r, c))] * 3, dims=("parallel",))(w, g, m, v)


def _prefetch_call(body, *, name, grid, in_specs, out_specs, out_shape, operands):
    grid_spec = pltpu.PrefetchScalarGridSpec(num_scalar_prefetch=1, grid=grid, in_specs=in_specs, out_specs=out_specs)
    params = pltpu.CompilerParams(dimension_semantics=("parallel",) * len(grid), vmem_limit_bytes=VMEM_LIMIT)
    return _pallas(body, name=name, grid_spec=grid_spec, out_shape=out_shape, compiler_params=params)(*operands)


def _half_rows_tile(hr):
    return hr if hr <= 256 else 176 if hr % 176 == 0 else 256


def _add_half(g, landed, place, *, name):
    def body(place_ref, g_ref, l_ref, o_ref):
        own = g_ref[0] if len(g_ref.shape) == 4 else g_ref[...]
        o_ref[...] = (own + l_ref[...]).astype(BF16)

    if g.ndim == 4:
        _, _, hr, c = g.shape
        tr = _half_rows_tile(hr)
        blk = (1, tr, c)
        return _prefetch_call(
            body, name=name, grid=(N_CHIPS, hr // tr),
            in_specs=[pl.BlockSpec((1,) + blk, lambda j, i, p: (j, p[1], i, 0)), pl.BlockSpec(blk, lambda j, i, p: (j, i, 0))],
            out_specs=pl.BlockSpec(blk, lambda j, i, p: (j, i, 0)), out_shape=_sds(landed.shape, BF16),
            operands=(place, g, landed))
    hr, c = landed.shape
    tr, tc = 256, _tile(c, 2176)
    nb = hr // tr
    return _prefetch_call(
        body, name=name, grid=(nb, c // tc),
        in_specs=[pl.BlockSpec((tr, tc), lambda i, j, p: (p[1] * nb + i, j)), pl.BlockSpec((tr, tc), lambda i, j, p: (i, j))],
        out_specs=pl.BlockSpec((tr, tc), lambda i, j, p: (i, j)), out_shape=_sds(landed.shape, BF16),
        operands=(place, g, landed))


def _sum_chips(chip_sum, lands, place, *, name, tc, window_stride=0):
    _, hr, c = lands.shape
    tr = _half_rows_tile(hr)
    nb = hr // tr
    ncb = c // tc

    def body(place_ref, own_ref, a_ref, b_ref, c_ref, o_ref):
        own = own_ref[0] if len(own_ref.shape) == 3 else own_ref[...]
        o_ref[...] = ((own.astype(F32) + a_ref[0].astype(F32)) + b_ref[0].astype(F32)) + c_ref[0].astype(F32)

    land = lambda k: pl.BlockSpec((1, tr, tc), lambda i, j, p: ((p[0] + k) % N_CHIPS, i, j))
    if chip_sum.ndim == 3:
        own_spec = land(0)
    else:
        stride = window_stride // tc
        own_spec = pl.BlockSpec((tr, tc), lambda i, j, p: (i, p[0] * stride + j))
    return _prefetch_call(
        body, name=name, grid=(nb, ncb), in_specs=[own_spec, land(1), land(2), land(3)],
        out_specs=pl.BlockSpec((tr, tc), lambda i, j, p: (p[1] * nb + i, j)), out_shape=_sds((2 * hr, c)),
        operands=(place, chip_sum, lands, lands, lands))


_HBM = pl.BlockSpec(memory_space=pltpu.HBM)


def _place():
    x, y, c = lax.axis_index("x"), lax.axis_index("y"), lax.axis_index("c")
    chips = [(1 - x, y), (x, 1 - y), (1 - x, 1 - y)]
    return x, y, c, chips


def _rcopy(src, dst, send_sem, recv_sem, to):
    return pltpu.make_async_remote_copy(src_ref=src, dst_ref=dst, send_sem=send_sem, recv_sem=recv_sem,
                                        device_id=to, device_id_type=MESH)


N_BIG = 5
UP_COLS = 2 * D_FF // N_CHIPS
IN_WINDOW = 640
IN_STRIDE = 512


class _Hosted:
    def __init__(self, operands, out_shapes, n_sems, start, finish, aliases=None, local_sems=0):
        self.operands, self.out_shapes, self.n_sems = list(operands), list(out_shapes), n_sems
        self.start, self.finish, self.aliases, self.local_sems = start, finish, dict(aliases or {}), local_sems

    def scratch(self):
        return ([pltpu.SemaphoreType.DMA((self.n_sems,)), pltpu.SemaphoreType.DMA((self.n_sems,))]
                + [pltpu.SemaphoreType.DMA] * self.local_sems)


def _both(a, b):
    na, nao, nas = len(a.operands), len(a.out_shapes), len(a.scratch())

    def start(ins, outs, sems):
        a.start(ins[:na], outs[:nao], sems[:nas])
        b.start(ins[na:], outs[nao:], sems[nas:])

    def finish(ins, outs, sems):
        a.finish(ins[:na], outs[:nao], sems[:nas])
        b.finish(ins[na:], outs[nao:], sems[nas:])

    both = _Hosted(a.operands + b.operands, a.out_shapes + b.out_shapes, 0, start, finish,
                   aliases={**a.aliases, **{na + i: nao + o for i, o in b.aliases.items()}})
    both.scratch = lambda: a.scratch() + b.scratch()
    return both


def _run_hosted(hosted, *, name):
    n_in, n_out = len(hosted.operands), len(hosted.out_shapes)

    def body(*refs):
        parts = (refs[:n_in], refs[n_in:n_in + n_out], refs[n_in + n_out:])
        hosted.start(*parts)
        hosted.finish(*parts)

    return _pallas(body, name=name, in_specs=[_HBM] * n_in, out_specs=[_HBM] * n_out, out_shape=hosted.out_shapes,
                   input_output_aliases=hosted.aliases, scratch_shapes=hosted.scratch())(*hosted.operands)


def _host_pcall(core_body, hosted, first, last, *, n_in, n_out, n_scratch, name, grid, in_specs, out_specs, out_shape,
                scratch_shapes, dims, operands):
    if hosted is None:
        outs = _pcall(core_body, name=name, grid=grid, in_specs=in_specs, out_specs=out_specs, out_shape=out_shape,
                      scratch_shapes=scratch_shapes, dims=dims)(*operands)
        return outs, []
    hi, ho = len(hosted.operands), len(hosted.out_shapes)

    def body(*refs):
        a, b = n_in, n_in + hi
        c, d = b + n_out, b + n_out + ho
        e = d + n_scratch
        parts = (refs[a:b], refs[c:d], refs[e:])

        @pl.when(first())
        def _():
            hosted.start(*parts)

        core_body(*refs[:a], *refs[b:c], *refs[d:e])

        @pl.when(last())
        def _():
            hosted.finish(*parts)

    params = pltpu.CompilerParams(dimension_semantics=("arbitrary",) * len(grid), vmem_limit_bytes=VMEM_LIMIT)
    outs = _pallas(body, name=name, grid=grid, in_specs=list(in_specs) + [_HBM] * hi, out_specs=list(out_specs) + [_HBM] * ho,
                   out_shape=list(out_shape) + hosted.out_shapes, scratch_shapes=list(scratch_shapes) + hosted.scratch(),
                   input_output_aliases={n_in + a: n_out + b for a, b in hosted.aliases.items()},
                   compiler_params=params)(*operands, *hosted.operands)
    return outs[:n_out], outs[n_out:]


def _gather_slot(src, out, chip, hc):
    hr, cols = src.shape[0] // 2, src.shape[1]
    if len(out.shape) == 2:
        return out.at[pl.ds(hc * hr, hr), pl.ds(pl.multiple_of(chip * cols, LANES), cols)]
    return out.at[chip, pl.ds(hc * hr, hr), :]


def _gathered_shape(shard, by_cols):
    if by_cols:
        return _sds((shard.shape[0], N_CHIPS * shard.shape[1]), shard.dtype)
    return _sds((N_CHIPS,) + shard.shape, shard.dtype)


def _plan_gather_ici(shards, by_cols, whole=(), own_cols=()):
    n = len(shards)

    def copies(ins, outs, sems):
        send_sems, recv_sems = sems[0], sems[1]
        x, y, c, chips = _place()
        me = 2 * x + y
        sends, waits = [], []
        for w in range(n + len(whole)):
            for k, (cx, cy) in enumerate(chips):
                sem = (send_sems.at[3 * w + k], recv_sems.at[3 * w + k])
                if w < n:
                    hr = ins[w].shape[0] // 2
                    sends.append(_rcopy(ins[w].at[pl.ds(c * hr, hr), :], _gather_slot(ins[w], outs[w], me, c), *sem, (cx, cy, c)))
                    landed = _gather_slot(ins[w], outs[w], 2 * cx + cy, c)
                else:
                    sends.append(_rcopy(ins[w], outs[w].at[me], *sem, (cx, cy, c)))
                    landed = outs[w].at[2 * cx + cy]
                waits.append(_rcopy(landed, landed, *sem, (cx, cy, c)))
        local = [pltpu.make_async_copy(
            ins[w], outs[w].at[:, pl.ds(pl.multiple_of(me * ins[w].shape[1], LANES), ins[w].shape[1])], sems[2 + i])
            for i, w in enumerate(own_cols)]
        return sends, waits, local

    def start(ins, outs, sems):
        sends, _, local = copies(ins, outs, sems)
        for cp in local + sends:
            cp.start()

    def finish(ins, outs, sems):
        sends, waits, local = copies(ins, outs, sems)
        for cp in waits:
            cp.wait_recv()
        for cp in sends:
            cp.wait_send()
        for cp in local:
            cp.wait()

    out_shapes = [_gathered_shape(s, bc) for s, bc in zip(shards, by_cols)] + [_sds((N_CHIPS,) + a.shape, a.dtype) for a in whole]
    return _Hosted(list(shards) + list(whole), out_shapes, 3 * (n + len(whole)), start, finish, local_sems=len(own_cols))


def _plan_gather_d2d(bufs, shard_shapes):
    n = len(bufs)

    def copies(ins, outs, sems):
        send_sems, recv_sems = sems
        x, y, c, chips = _place()
        sibling = (x, y, 1 - c)
        sends, waits = [], []
        for w in range(n):
            for k, (cx, cy) in enumerate(chips):
                sem = (send_sems.at[3 * w + k], recv_sems.at[3 * w + k])
                landed = _gather_slot(shard_shapes[w], outs[w], 2 * cx + cy, c)
                other = _gather_slot(shard_shapes[w], outs[w], 2 * cx + cy, 1 - c)
                sends.append(_rcopy(landed, landed, *sem, sibling))
                waits.append(_rcopy(other, other, *sem, sibling))
        return sends, waits

    def start(ins, outs, sems):
        for cp in copies(ins, outs, sems)[0]:
            cp.start()

    def finish(ins, outs, sems):
        sends, waits = copies(ins, outs, sems)
        for cp in waits:
            cp.wait_recv()
        for cp in sends:
            cp.wait_send()

    return _Hosted(bufs, [_sds(b.shape, b.dtype) for b in bufs], 3 * n, start, finish, aliases={w: w for w in range(n)})


def _plan_swap(grads):
    def copies(ins, outs, sems):
        send_sems, recv_sems = sems
        x, y, c, _ = _place()
        cps = []
        for w, g_ref in enumerate(ins):
            if len(g_ref.shape) == 4:
                theirs = g_ref.at[:, 1 - c]
            else:
                hr = g_ref.shape[0] // 2
                theirs = g_ref.at[pl.ds((1 - c) * hr, hr), :]
            cps.append(_rcopy(theirs, outs[w], send_sems.at[w], recv_sems.at[w], (x, y, 1 - c)))
        return cps

    def start(ins, outs, sems):
        for cp in copies(ins, outs, sems):
            cp.start()

    def finish(ins, outs, sems):
        for cp in copies(ins, outs, sems):
            cp.wait()

    out_shapes = [_sds((g.shape[0], g.shape[2], g.shape[3])) if g.ndim == 4 else _sds((g.shape[0] // 2, g.shape[1]))
                  for g in grads]
    return _Hosted(grads, out_shapes, len(grads), start, finish)


def _plan_scatter(chip_sums, windows):
    def copies(ins, outs, sems):
        send_sems, recv_sems = sems
        x, y, c, chips = _place()
        me = 2 * x + y
        sends, waits = [], []
        for w, s_ref in enumerate(ins):
            for k, (cx, cy) in enumerate(chips):
                tgt = 2 * cx + cy
                if windows[w] is not None:
                    stride, width = windows[w]
                    part = s_ref.at[:, pl.ds(pl.multiple_of(tgt * stride, LANES), width)]
                else:
                    part = s_ref.at[tgt]
                sem = (send_sems.at[3 * w + k], recv_sems.at[3 * w + k])
                sends.append(_rcopy(part, outs[w].at[me], *sem, (cx, cy, c)))
                slot = outs[w].at[tgt]
                waits.append(_rcopy(slot, slot, *sem, (cx, cy, c)))
        return sends, waits

    def start(ins, outs, sems):
        for cp in copies(ins, outs, sems)[0]:
            cp.start()

    def finish(ins, outs, sems):
        sends, waits = copies(ins, outs, sems)
        for cp in waits:
            cp.wait_recv()
        for cp in sends:
            cp.wait_send()

    out_shapes = [_sds((N_CHIPS, s.shape[0], win[1]), BF16) if win is not None else _sds(s.shape, BF16)
                  for s, win in zip(chip_sums, windows)]
    return _Hosted(chip_sums, out_shapes, 3 * len(chip_sums), start, finish)


def _plan_join(reds):
    def copies(ins, outs, sems):
        send_sems, recv_sems = sems
        x, y, c, _ = _place()
        sends, waits = [], []
        for w, out in enumerate(outs):
            hr = out.shape[0] // 2
            mine = out.at[pl.ds(c * hr, hr), :]
            theirs = out.at[pl.ds((1 - c) * hr, hr), :]
            sends.append(_rcopy(mine, mine, send_sems.at[w], recv_sems.at[w], (x, y, 1 - c)))
            waits.append(_rcopy(theirs, theirs, send_sems.at[w], recv_sems.at[w], (x, y, 1 - c)))
        return sends, waits

    def start(ins, outs, sems):
        for cp in copies(ins, outs, sems)[0]:
            cp.start()

    def finish(ins, outs, sems):
        sends, waits = copies(ins, outs, sems)
        for cp in waits:
            cp.wait_recv()
        for cp in sends:
            cp.wait_send()

    return _Hosted(reds, [_sds(r.shape) for r in reds], len(reds), start, finish, aliases={w: w for w in range(len(reds))})


def _allreduce_small(v):
    m_per = v.shape[0]

    def body(v_ref, out_ref, all_ref, send_sems, recv_sems, local_sem):
        x, y, c, chips = _place()
        me, sibling = (x, y, c), (x, y, 1 - c)

        def rows(px, py, pc):
            return all_ref.at[pl.ds((4 * px + 2 * py + pc) * m_per, m_per), :]

        def copy(k, block, to, src=None):
            return _rcopy(rows(*block) if src is None else src, rows(*block), send_sems.at[k], recv_sems.at[k], to)

        mine = pltpu.make_async_copy(v_ref, rows(*me), local_sem)
        mine.start()
        first = [copy(0, me, sibling, src=v_ref)]
        first += [copy(1 + k, me, (*chip, c), src=v_ref) for k, chip in enumerate(chips)]
        for cp in first:
            cp.start()
        passed = [copy(4 + k, (*chip, c), sibling) for k, chip in enumerate(chips)]
        for k, chip in enumerate(chips):
            copy(1 + k, (*chip, c), me).wait_recv()
            passed[k].start()
        copy(0, sibling, me).wait_recv()
        for k, chip in enumerate(chips):
            copy(4 + k, (*chip, 1 - c), me).wait_recv()
        for cp in first + passed:
            cp.wait_send()
        mine.wait()
        acc = all_ref[pl.ds(0, m_per), :]
        for d in range(1, 8):
            acc = acc + all_ref[pl.ds(d * m_per, m_per), :]
        out_ref[...] = acc

    vm = pl.BlockSpec(memory_space=pltpu.VMEM)
    return _pallas(body, name="allreduce_small", in_specs=[vm], out_specs=vm, out_shape=_sds((m_per, LANES)),
                          scratch_shapes=[pltpu.VMEM((8 * m_per, LANES), F32), pltpu.SemaphoreType.DMA((7,)),
                                          pltpu.SemaphoreType.DMA((7,)), pltpu.SemaphoreType.DMA],
                          compiler_params=pltpu.CompilerParams(vmem_limit_bytes=VMEM_LIMIT))(v)


def _block_diag(blocks):
    j, g, a, b = blocks.shape
    eye = jnp.eye(g, dtype=bool)[None, :, None, :, None]
    return jnp.where(eye, blocks[:, :, :, None, :], jnp.zeros((), blocks.dtype)).reshape(j, g * a, g * b)


def _diag_blocks(m, a, b):
    j = m.shape[0]
    g = m.shape[1] // a
    t = m.reshape(j, g, a, g, b)
    eye = jnp.eye(g, dtype=bool)[None, :, None, :, None]
    return jnp.sum(jnp.where(eye, t, 0.0), axis=3)


def _pack_rows(parts, rows, dtype):
    used = sum(p.shape[0] for p in parts)
    return jnp.concatenate([p.astype(dtype) for p in parts] + [jnp.zeros((rows - used, D_MODEL), dtype)], axis=0)


_SMALL = (("g_mix", (1024,)), ("b_f", (8,)), ("g_q", (64,)), ("g_k", (64,)), ("lambda_re", (32, 64)),
          ("lambda_im", (32, 64)), ("log_step", (32,)), ("b_re", (32, 64, 16)), ("b_im", (32, 64, 16)),
          ("c_re", (32, 16, 64)), ("c_im", (32, 16, 64)), ("d_skip", (32, 16)), ("b_glu", (512,)),
          ("g_attn_out", (512,)), ("g_ssm_out", (512,)), ("g_ffn", (1024,)), ("conv_b", (5632,)))


def _small_rows(shape):
    return -(-math.prod(shape) // LANES)


def _pack_small(arrs, extra=()):
    parts = []
    for a in list(arrs) + list(extra):
        flat = a.reshape(-1)
        rows = -(-flat.shape[0] // LANES)
        parts.append(jnp.pad(flat, (0, rows * LANES - flat.shape[0])).reshape(rows, LANES))
    total = sum(p.shape[0] for p in parts)
    pad = -total % SUBLANES
    if pad:
        parts.append(jnp.zeros((pad, LANES), F32))
    return jnp.concatenate(parts, axis=0)


def _unpack_small(buf, shapes):
    out, r = [], 0
    for shape in shapes:
        n = math.prod(shape)
        rows = -(-n // LANES)
        out.append(buf[r:r + rows].reshape(-1)[:n].reshape(shape))
        r += rows
    return out


def _halves(t):
    return t.reshape(N_CHIPS, 2, t.shape[0] // (2 * N_CHIPS), t.shape[1])


class _MeshComm:
    def __init__(self, args):
        x, y, self.core = lax.axis_index("x"), lax.axis_index("y"), lax.axis_index("c")
        self.chip = 2 * x + y
        self.place = jnp.stack([self.chip, self.core]).astype(jnp.int32)
        self.shards = {n: args[n].astype(BF16) for n in ("w_in", "w_glu", "w_out", "w_up", "w_down")}
        self.conv_w = args["conv_w"]

    def _own(self, stacked, mine):
        return lax.dynamic_update_slice(stacked, mine[None], (self.chip,) + (0,) * mine.ndim)

    def w_in(self):
        sh = self.shards["w_in"]
        (buf,) = _run_hosted(_plan_gather_ici([sh], [False]), name="gather_w_in")
        (buf,) = _run_hosted(_plan_gather_d2d([buf], [sh]), name="pass_w_in")
        whole = self._own(buf, sh).transpose(1, 0, 2).reshape(D_MODEL, IN_COLS)
        return jnp.pad(whole, ((0, 0), (0, Z_COLS - IN_COLS)))

    def gather_first(self):
        self.mid = [self.shards[n] for n in ("w_glu", "w_out", "w_up")]
        return _plan_gather_ici(self.mid, [False, False, True], whole=[self.conv_w], own_cols=[2])

    def gather_second(self, landed):
        self.g_cw = landed[3]
        return _both(_plan_gather_d2d(list(landed[:3]), self.mid), _plan_gather_ici([self.shards["w_down"]], [False]))

    def weights(self, gathered):
        g_glu, g_out, w_up_b = gathered[:3]
        own = self._own
        return (own(g_glu, self.mid[0]).reshape(SSM_W, SSM_W), own(g_out, self.mid[1]).reshape(D_MODEL, D_MODEL), w_up_b,
                own(self.g_cw, self.conv_w).transpose(1, 0, 2).reshape(3, 2 * D_FF))

    def gather_third(self, gathered):
        return _plan_gather_d2d([gathered[3]], [self.shards["w_down"]])

    def w_down(self, passed):
        return self._own(passed[0], self.shards["w_down"]).reshape(D_FF, D_MODEL)

    def swap(self, d_w_down, d_w_up, d_w_glu, d_w_out):
        self.early = [_halves(d_w_down), d_w_up, _halves(d_w_glu), _halves(d_w_out)]
        return _plan_swap(self.early)

    def scatter(self, landed):
        self.early_sums = [_add_half(g, l, self.place, name="add_" + n)
                           for g, l, n in zip(self.early, landed, ("w_down", "w_up", "w_glu", "w_out"))]
        return _plan_scatter(self.early_sums, [None, (UP_COLS, UP_COLS), None, None])

    def reduce(self, early_lands, d_w_in):
        d_in = d_w_in
        (landed,) = _run_hosted(_plan_swap([d_in]), name="swap_halves")
        sum_in = _add_half(d_in, landed, self.place, name="add_w_in")
        (land_in,) = _run_hosted(_plan_scatter([sum_in], [(IN_STRIDE, IN_WINDOW)]), name="scatter_chips")
        es, el = self.early_sums, early_lands
        todo = [(sum_in, land_in, "w_in", LANES, IN_STRIDE), (es[2], el[2], "w_glu", SSM_W, 0),
                (es[3], el[3], "w_out", D_MODEL, 0), (es[1], el[1], "w_up", UP_COLS, UP_COLS),
                (es[0], el[0], "w_down", D_MODEL, 0)]
        reds = _run_hosted(_plan_join([_sum_chips(s, l, self.place, name="sum_" + n, tc=tc, window_stride=st)
                                       for s, l, n, tc, st in todo]), name="join_halves")
        g_big = dict(zip(("w_in", "w_glu", "w_out", "w_up", "w_down"), reds))
        g_big["w_in"] = lax.dynamic_slice_in_dim(reds[0], 2 * self.chip, IN_COLS // N_CHIPS, axis=1)
        return g_big


def _local_step(x, tgt, p, comm):
    s = x.shape[0]
    row = lambda v: v.reshape(1, -1)
    g_mix, g_ffn = row(p["g_mix"]), row(p["g_ffn"])
    g_att, g_ssm, b_glu, conv_b = row(p["g_attn_out"]), row(p["g_ssm_out"]), row(p["b_glu"]), row(p["conv_b"])
    gq = row(jnp.tile(p["g_q"], HEADS))
    gk = row(jnp.tile(p["g_k"], HEADS))
    bf = row(jnp.pad(p["b_f"], (0, LANES - HEADS)))
    gg = jnp.kron(jnp.eye(HEADS, dtype=F32), jnp.ones((HEAD_DIM, HEAD_DIM), F32)).astype(BF16)
    dsk = row(p["d_skip"])

    rep = lambda a: jnp.repeat(a, SSM_GROUP, axis=0)
    lr, li = rep(p["lambda_re"]), rep(p["lambda_im"])
    ls = rep(jnp.broadcast_to(p["log_step"][:, None], (SSM_GROUPS, SSM_STATE)))
    bt_re = p["b_re"].transpose(0, 2, 1).reshape(_PARAM_SHAPE)
    bt_im = p["b_im"].transpose(0, 2, 1).reshape(_PARAM_SHAPE)
    a_re_rep, a_im_rep, bb_re, bb_im = _ssm_params(lr, li, ls, bt_re, bt_im)
    ar = a_re_rep[::SSM_GROUP].reshape(SSM_CHUNKS, 1, CHUNK_S)
    ai = a_im_rep[::SSM_GROUP].reshape(SSM_CHUNKS, 1, CHUNK_S)
    chunked = lambda t: t.reshape(SSM_CHUNKS, SSM_GROUPS // SSM_CHUNKS, SSM_GROUP, SSM_STATE)
    bbr = _block_diag(chunked(bb_re)).astype(BF16)
    bbi = _block_diag(chunked(bb_im)).astype(BF16)
    to_cc = lambda c: _block_diag(chunked(c).transpose(0, 1, 3, 2)).astype(BF16)
    ccr, cci = to_cc(p["c_re"]), to_cc(p["c_im"])

    w_in_r = comm.w_in()
    hb, z = _in_proj(x, g_mix, w_in_r)
    qh, kh, vh, ub, uf, c128 = _attn_prep(z, gq, gk, bf, gg)
    crow = c128[:, :HEADS].T.reshape(HEADS, 1, s)
    (oh, lse), landed = _attn_fwd(qh, kh, vh, crow, comm.gather_first())
    (xr, xi, y), gathered = _ssm_fwd(ub, uf, bbr, bbi, ar, ai, ccr, cci, dsk, comm.gather_second(landed))
    w_glu_b, w_out_b, w_up_b, conv_w_full = comm.weights(gathered)
    (x1, mixb, h2b), passed = _mix_out(y, oh, x, w_glu_b, b_glu, g_att, g_ssm, w_out_b, g_ffn, comm.gather_third(gathered))
    w_down_b = comm.w_down(passed)
    up = _mm(h2b, w_up_b, name="ffn_up", tm=1024, tn=1408, tk=1024)
    act = _conv_act(up, conv_w_full, conv_b)
    dy, dyb, loss_blk = _down_loss(act, w_down_b, x1, tgt)

    d_w_down = _mm(act, dyb, ta=True, name="d_w_down", tm=1408, tn=1024, tk=2048)
    dact = _mm(dyb, w_down_b, tb=True, name="d_act", tm=1024, tn=1408, tk=1024)
    dupb, dcw = _conv_act_bwd(up, dact, conv_w_full, conv_b)
    d_w_up = _mm(h2b, dupb, ta=True, b_parts=2, name="d_w_up", tm=1024, tn=1408, tk=2048)
    dh2 = _mm(dupb, w_up_b, tb=True, a_parts=2, name="d_h2", tm=1024, tn=1024, tk=1408)
    dx1, dx1b, doh, dys, d_w_glu, d_g_ffn, d_g_att, d_g_ssm, d_b_glu = _mix_bwd(
        dy, dh2, x1, g_ffn, w_out_b, y, oh, w_glu_b, b_glu, g_att, g_ssm)
    d_w_out = _mm(mixb, dx1b, ta=True, name="d_w_out", tm=1024, tn=1024, tk=2048)
    (du, dbbr, dbbi, dccr, dcci, dar, dai, dd), swapped = _ssm_bwd(dys, uf, ub, xr, xi, bbr, bbi, ar, ai, ccr, cci, dsk,
                                                                comm.swap(d_w_down, d_w_up, d_w_glu, d_w_out))
    (dqh, dkh, dvh, dcrow), early_lands = _attn_bwd(qh, kh, vh, crow, lse, doh, comm.scatter(swapped))
    dc128 = jnp.pad(dcrow.reshape(HEADS, s).T, ((0, 0), (0, LANES - HEADS)))
    dzb, d_gq, d_gk, d_bf = _prep_bwd(z, dqh, dkh, dvh, du, dc128, gq, gk, bf, gg)
    d_w_in_r = _mm(hb, dzb, ta=True, name="d_w_in", tm=512, tn=Z_COLS, tk=2048)
    dh = _mm(dzb, w_in_r, tb=True, name="d_h", tm=1024, tn=1024, tk=Z_COLS)
    dx, d_g_mix = _in_norm_bwd(x, g_mix, dh, dx1)

    unchunk = lambda t: t.reshape(_PARAM_SHAPE)
    dbb_re = unchunk(_diag_blocks(dbbr, SSM_GROUP, SSM_STATE))
    dbb_im = unchunk(_diag_blocks(dbbi, SSM_GROUP, SSM_STATE))
    first_row = (jnp.arange(_PARAM_SHAPE[0]) % SSM_GROUP == 0)[:, None]
    da_re = jnp.where(first_row, rep(dar.reshape(SSM_GROUPS, SSM_STATE)), 0.0)
    da_im = jnp.where(first_row, rep(dai.reshape(SSM_GROUPS, SSM_STATE)), 0.0)
    expand_t = (jnp.arange(SSM_GROUPS)[:, None] == (jnp.arange(_PARAM_SHAPE[0]) // SSM_GROUP)[None, :]).astype(BF16)
    d_lr, d_li, d_ls, d_bt_re, d_bt_im = _ssm_params_bwd(lr, li, ls, bt_re, bt_im, da_re, da_im, dbb_re, dbb_im, expand_t)
    from_bt = lambda t: t.reshape(SSM_GROUPS, SSM_GROUP, SSM_STATE).transpose(0, 2, 1)
    from_cc = lambda t: _diag_blocks(t, SSM_STATE, SSM_GROUP).transpose(0, 1, 3, 2).reshape(SSM_GROUPS, SSM_GROUP, SSM_STATE)

    small = {
        "g_mix": d_g_mix, "b_f": d_bf[0, :HEADS], "g_q": d_gq.reshape(HEADS, HEAD_DIM).sum(0),
        "g_k": d_gk.reshape(HEADS, HEAD_DIM).sum(0), "lambda_re": d_lr, "lambda_im": d_li, "log_step": d_ls,
        "b_re": from_bt(d_bt_re), "b_im": from_bt(d_bt_im), "c_re": from_cc(dccr), "c_im": from_cc(dcci),
        "d_skip": dd, "b_glu": d_b_glu, "g_attn_out": d_g_att, "g_ssm_out": d_g_ssm, "g_ffn": d_g_ffn,
        "conv_b": dcw[:, 3],
    }
    big = {"w_in": d_w_in_r, "w_glu": d_w_glu, "w_out": d_w_out, "w_up": d_w_up, "w_down": d_w_down}
    return loss_blk[0, 0], dx, big, small, dcw[:, 0:3].transpose(1, 0, 2).reshape(3, 2 * D_FF), early_lands


def kernel(x, g_mix, w_in, b_f, g_q, g_k, lambda_re, lambda_im, log_step, b_re, b_im, c_re, c_im, d_skip, w_glu, b_glu, g_attn_out, g_ssm_out, w_out, g_ffn, w_up, conv_w, conv_b, w_down, loss_target, m_g_mix, m_w_in, m_b_f, m_g_q, m_g_k, m_lambda_re, m_lambda_im, m_log_step, m_b_re, m_b_im, m_c_re, m_c_im, m_d_skip, m_w_glu, m_b_glu, m_g_attn_out, m_g_ssm_out, m_w_out, m_g_ffn, m_w_up, m_conv_w, m_conv_b, m_w_down, v_g_mix, v_w_in, v_b_f, v_g_q, v_g_k, v_lambda_re, v_lambda_im, v_log_step, v_b_re, v_b_im, v_c_re, v_c_im, v_d_skip, v_w_glu, v_b_glu, v_g_attn_out, v_g_ssm_out, v_w_out, v_g_ffn, v_w_up, v_conv_w, v_conv_b, v_w_down):
    args = dict(locals())
    order = ["g_mix", "w_in", "b_f", "g_q", "g_k", "lambda_re", "lambda_im", "log_step", "b_re", "b_im", "c_re", "c_im",
             "d_skip", "w_glu", "b_glu", "g_attn_out", "g_ssm_out", "w_out", "g_ffn", "w_up", "conv_w", "conv_b", "w_down"]
    comm = _MeshComm(args)
    chip = comm.chip
    loss_part, dx, big, small, d_conv_w, early_lands = _local_step(x[0], loss_target[0], args, comm)
    loss = lax.psum(loss_part, ("x", "y", "c"))

    g_big = comm.reduce(early_lands, big["w_in"])

    small_names = [n for n, _ in _SMALL]
    small_shapes = [sh for _, sh in _SMALL]
    gsum = _allreduce_small(_pack_small([small[n] for n in small_names], extra=[d_conv_w]))
    g_small = _unpack_small(gsum, small_shapes + [(3, 2 * D_FF)])
    g_conv_w = lax.dynamic_slice_in_dim(g_small[-1], chip * (2 * D_FF // N_CHIPS), 2 * D_FF // N_CHIPS, axis=1)
    g_small = dict(zip(small_names, g_small[:-1]))

    grad, delta, new_m, new_v = {}, {}, {}, {}
    for n in ("w_in", "w_glu", "w_out", "w_up", "w_down"):
        grad[n] = g_big[n]
        delta[n], new_m[n], new_v[n] = _adamw(args[n], g_big[n], args["m_" + n], args["v_" + n], name="adamw_" + n)
    grad["conv_w"] = g_conv_w
    delta["conv_w"], new_m["conv_w"], new_v["conv_w"] = _adamw(conv_w, g_conv_w, m_conv_w, v_conv_w, name="adamw_conv_w")
    stepped = _adamw_small([args[n] for n in small_names], [g_small[n] for n in small_names],
                           [args["m_" + n] for n in small_names], [args["v_" + n] for n in small_names])
    for i, n in enumerate(small_names):
        grad[n] = g_small[n]
        delta[n], new_m[n], new_v[n] = stepped[3 * i:3 * i + 3]

    return (loss, dx[None], *[grad[n] for n in order], *[delta[n] for n in order], *[new_m[n] for n in order],
            *[new_v[n] for n in order])
```

```python
import math

import jax
import jax.numpy as jnp
from jax import lax
from jax.experimental import pallas as pl
from jax.experimental.pallas import tpu as pltpu

F32 = jnp.float32
BF16 = jnp.bfloat16

D_MODEL = 1024
HEADS = 8
HEAD_DIM = 64
ATTN_W = 512
SSM_W = 512
SSM_GROUPS = 32
SSM_GROUP = 16
SSM_STATE = 64
N_STATE = SSM_GROUPS * SSM_STATE
D_FF = 2816
IN_COLS = 2056
Z_COLS = 2176
F_COL0 = 1536
U_COL0 = 1544
EPS = 1e-6
NEG_INF = -1e30
N_CHIPS = 4
LANES = 128
SUBLANES = 8
SSM_CHUNKS = 2
CHUNK_U = SSM_W // SSM_CHUNKS
CHUNK_S = N_STATE // SSM_CHUNKS
HEADS_PER_STEP = 4
STRIP = 128
N_STRIPS = D_FF // STRIP

ROWS_IN, ROWS_GLU, ROWS_OUT, ROWS_UP, ROWS_DOWN = 514, 64, 256, 1408, 704
OFF_GLU = ROWS_IN
OFF_OUT = OFF_GLU + ROWS_GLU
OFF_UP = OFF_OUT + ROWS_OUT
OFF_DOWN = OFF_UP + ROWS_UP
OFF_SPARE = OFF_DOWN + ROWS_DOWN
PACK_ROWS = 2976
HALF_ROWS = PACK_ROWS // 2
CONVW_ROWS = 9

ADAM_LR = 0.001
ADAM_B1 = 0.9
ADAM_B2 = 0.999
ADAM_EPS = 1e-08
ADAM_WD = 0.01
ADAM_STEP = 10

VMEM_LIMIT = 56 * 1024 * 1024
MESH = pl.DeviceIdType.MESH


def _pallas(body, **kw):
    return pl.pallas_call(body, **kw)


def _pcall(body, *, name, out_shape, in_specs, out_specs, grid=(), scratch_shapes=(), dims=None):
    params = pltpu.CompilerParams(dimension_semantics=dims, vmem_limit_bytes=VMEM_LIMIT)
    return _pallas(body, name=name, grid=grid, in_specs=in_specs, out_specs=out_specs,
                   out_shape=out_shape, scratch_shapes=scratch_shapes, compiler_params=params)


def _sds(shape, dtype=F32):
    return jax.ShapeDtypeStruct(shape, dtype)


def _dot(a, b):
    return jnp.dot(a, b, preferred_element_type=F32)


def _dot_nt(a, b):
    return lax.dot_general(a, b, (((1,), (1,)), ((), ())), preferred_element_type=F32)


def _dot_tn(a, b):
    return lax.dot_general(a, b, (((0,), (0,)), ((), ())), preferred_element_type=F32)


def _split3(x):
    hi = x.astype(BF16)
    r = x - hi.astype(F32)
    mid = r.astype(BF16)
    lo = (r - mid.astype(F32)).astype(BF16)
    return hi, mid, lo


def _dot_exact_r(x, m01):
    hi, mid, lo = _split3(x)
    return _dot(hi, m01) + _dot(mid, m01) + _dot(lo, m01)


def _dot_exact_l(m01, x):
    hi, mid, lo = _split3(x)
    return _dot(m01, hi) + _dot(m01, mid) + _dot(m01, lo)


def _sigmoid(x):
    return 1.0 / (1.0 + jnp.exp(-x))


def _rms(x, g):
    r = lax.rsqrt(jnp.mean(x * x, axis=-1, keepdims=True) + EPS)
    return x * r * g


def _rms_bwd(x, g, dy):
    r = lax.rsqrt(jnp.mean(x * x, axis=-1, keepdims=True) + EPS)
    w = dy * g
    dx = r * w - x * (r * r * r) * jnp.mean(w * x, axis=-1, keepdims=True)
    dg = jnp.sum(dy * x * r, axis=0, keepdims=True)
    return dx, dg


_GELU_K = math.sqrt(2.0 / math.pi)
_GELU_C = 0.044715


def _gelu(y):
    return y * (0.5 * (1.0 + jnp.tanh(_GELU_K * (y + _GELU_C * (y * y * y)))))


def _gelu_grad(y):
    t = jnp.tanh(_GELU_K * (y + _GELU_C * (y * y * y)))
    return 0.5 * (1.0 + t) + 0.5 * y * (1.0 - t * t) * (_GELU_K * (1.0 + 3.0 * _GELU_C * y * y))


def _tile(n, pref):
    if n <= pref:
        return n
    divs = [t for t in range(LANES, n + 1, LANES) if n % t == 0]
    below = [t for t in divs if t <= pref]
    if below and 2 * below[-1] >= pref:
        return below[-1]
    above = [t for t in divs if t > pref]
    return above[0] if above else n


def _row_tile(s):
    return min(256, s)


def _mm(a, b, *, name, tm, tn, tk, ta=False, tb=False, a_parts=1, b_parts=1):
    if a_parts > 1:
        m, kk = a.shape[1], a.shape[2] * a_parts
    elif ta:
        kk, m = a.shape
    else:
        m, kk = a.shape
    if b_parts > 1:
        n = b.shape[2] * b_parts
    else:
        n = b.shape[0] if tb else b.shape[1]
    tm, tn, tk = _tile(m, tm), _tile(n // b_parts, tn), _tile(kk // a_parts, tk)
    k_per, n_per = kk // a_parts // tk, n // b_parts // tn

    def body(a_ref, b_ref, o_ref):
        k = pl.program_id(2)
        if ta:
            part = _dot_tn(a_ref[...], b_ref[...])
        elif tb:
            part = _dot_nt(a_ref[...], b_ref[...])
        else:
            part = _dot(a_ref[...], b_ref[...])

        @pl.when(k == 0)
        def _():
            o_ref[...] = part

        @pl.when(k > 0)
        def _():
            o_ref[...] += part

    if a_parts > 1:
        a_spec = pl.BlockSpec((None, tm, tk), lambda i, j, k: (k // k_per, i, k % k_per))
    else:
        a_spec = pl.BlockSpec((tk, tm), lambda i, j, k: (k, i)) if ta else pl.BlockSpec((tm, tk), lambda i, j, k: (i, k))
    if b_parts > 1:
        b_spec = pl.BlockSpec((None, tk, tn), lambda i, j, k: (j // n_per, k, j % n_per))
    else:
        b_spec = pl.BlockSpec((tn, tk), lambda i, j, k: (j, k)) if tb else pl.BlockSpec((tk, tn), lambda i, j, k: (k, j))
    return _pcall(body, name=name, grid=(m // tm, n // tn, kk // tk), in_specs=[a_spec, b_spec],
                  out_specs=pl.BlockSpec((tm, tn), lambda i, j, k: (i, j)), out_shape=_sds((m, n)),
                  dims=("parallel", "parallel", "arbitrary"))(a, b)


def _in_proj(x, g_mix, w_in_r):
    s = x.shape[0]
    tm = _row_tile(s)

    def body(x_ref, g_ref, w_ref, h_ref, z_ref):
        h = _rms(x_ref[...], g_ref[...]).astype(BF16)
        h_ref[...] = h
        z_ref[...] = _dot(h, w_ref[...])

    return _pcall(body, name="in_proj", grid=(s // tm,),
                  in_specs=[pl.BlockSpec((tm, D_MODEL), lambda i: (i, 0)), pl.BlockSpec((1, D_MODEL), lambda i: (0, 0)),
                            pl.BlockSpec((D_MODEL, Z_COLS), lambda i: (0, 0))],
                  out_specs=[pl.BlockSpec((tm, D_MODEL), lambda i: (i, 0)), pl.BlockSpec((tm, Z_COLS), lambda i: (i, 0))],
                  out_shape=[_sds((s, D_MODEL), BF16), _sds((s, Z_COLS))], dims=("parallel",))(x, g_mix, w_in_r)


def _split_heads(ref, val):
    for h in range(HEADS):
        ref[h] = val[:, h * HEAD_DIM:(h + 1) * HEAD_DIM].astype(ref.dtype)


def _merge_heads(ref):
    return jnp.concatenate([ref[h].astype(F32) for h in range(HEADS)], axis=-1)


def _forget_logits(z_ref, bf_ref):
    fl = z_ref[:, F_COL0:F_COL0 + LANES] + bf_ref[...]
    return jnp.where(lax.broadcasted_iota(jnp.int32, fl.shape, 1) < HEADS, fl, 0.0)


def _attn_prep(z, gq, gk, bf, gg):
    s = z.shape[0]
    tm = _row_tile(s)

    def body(z_ref, gq_ref, gk_ref, bf_ref, gg_ref, qn_ref, kn_ref, vb_ref, ub_ref, uf_ref, c_ref, carry_ref):
        i = pl.program_id(0)

        @pl.when(i == 0)
        def _():
            carry_ref[...] = jnp.zeros_like(carry_ref)

        gg_m = gg_ref[...]

        def head_norm(t, g):
            ssq = _dot_exact_r(t * t, gg_m)
            return t * lax.rsqrt(ssq * (1.0 / HEAD_DIM) + EPS) * g

        _split_heads(qn_ref, head_norm(z_ref[:, 0:ATTN_W], gq_ref[...]))
        _split_heads(kn_ref, head_norm(z_ref[:, ATTN_W:2 * ATTN_W], gk_ref[...]))
        _split_heads(vb_ref, z_ref[:, 2 * ATTN_W:3 * ATTN_W])
        u = z_ref[:, U_COL0:U_COL0 + SSM_W]
        uf_ref[...] = u
        ub_ref[...] = u.astype(BF16)
        fl = _forget_logits(z_ref, bf_ref)
        lf = jnp.minimum(fl, 0.0) - jnp.log1p(jnp.exp(-jnp.abs(fl)))
        row = lax.broadcasted_iota(jnp.int32, (tm, tm), 0)
        col = lax.broadcasted_iota(jnp.int32, (tm, tm), 1)
        tri = (row >= col).astype(BF16)
        c = _dot_exact_l(tri, lf) + carry_ref[...]
        c_ref[...] = c
        carry_ref[...] = c[tm - 1:tm, :]

    row_spec = lambda w: pl.BlockSpec((tm, w), lambda i: (i, 0))
    const = lambda shape: pl.BlockSpec(shape, lambda i: (0, 0))
    heads = pl.BlockSpec((HEADS, tm, HEAD_DIM), lambda i: (0, i, 0))
    return _pcall(body, name="attn_prep", grid=(s // tm,),
                  in_specs=[row_spec(Z_COLS), const((1, ATTN_W)), const((1, ATTN_W)), const((1, LANES)), const((ATTN_W, ATTN_W))],
                  out_specs=[heads] * 3 + [row_spec(SSM_W), row_spec(SSM_W), row_spec(LANES)],
                  out_shape=[_sds((HEADS, s, HEAD_DIM), BF16)] * 3 + [_sds((s, SSM_W), BF16), _sds((s, SSM_W)), _sds((s, LANES))],
                  scratch_shapes=[pltpu.VMEM((1, LANES), F32)], dims=("arbitrary",))(z, gq, gk, bf, gg)


def _attn_fwd(qh, kh, vh, crow, hosted=None):
    _, s, _ = qh.shape
    tq = _row_tile(s)
    scale = HEAD_DIM ** -0.5

    hp = HEADS_PER_STEP
    nq = s // tq
    fold = lambda t, op: op(t[:, :tq // 2], t[:, tq // 2:])

    def body(q_ref, k_ref, v_ref, c_ref, o_ref, lse_ref, s_s):
        i = pl.program_id(1)

        def first(j, ms, diagonal):
            off = pl.multiple_of(j * tq, tq)
            out = []
            for hh in range(hp):
                sc = _dot_nt(q_ref[hh], k_ref[hh, pl.ds(off, tq), :]) * scale - c_ref[hh, :, pl.ds(off, tq)]
                if diagonal:
                    causal = lax.broadcasted_iota(jnp.int32, (tq, tq), 1) <= lax.broadcasted_iota(jnp.int32, (tq, tq), 0)
                    sc = jnp.where(causal, sc, NEG_INF)
                s_s[hh, j] = sc
                out.append(jnp.maximum(ms[hh], fold(sc, jnp.maximum)))
            return tuple(out)

        ms = lax.fori_loop(0, i, lambda j, c: first(j, c, False), (jnp.full((tq, tq // 2), NEG_INF, F32),) * hp)
        ms = [jnp.max(t, axis=-1, keepdims=True) for t in first(i, ms, True)]

        def second(j, carry):
            rows = pl.ds(pl.multiple_of(j * tq, tq), tq)
            out = []
            for hh in range(hp):
                ls, acc = carry[hh]
                p = jnp.exp(s_s[hh, j] - ms[hh])
                out.append((ls + fold(p, jnp.add), acc + _dot(p.astype(BF16), v_ref[hh, rows, :])))
            return tuple(out)

        zero = (jnp.zeros((tq, tq // 2), F32), jnp.zeros((tq, HEAD_DIM), F32))
        for hh, (ls, acc) in enumerate(lax.fori_loop(0, i + 1, second, (zero,) * hp)):
            l = jnp.sum(ls, axis=-1, keepdims=True)
            o_ref[hh] = acc / l
            lse_ref[hh] = ms[hh] + jnp.log(l)

    blk = pl.BlockSpec((hp, tq, HEAD_DIM), lambda h, i: (h, i, 0))
    full = pl.BlockSpec((hp, s, HEAD_DIM), lambda h, i: (h, 0, 0))
    nh = HEADS // hp
    first = lambda: jnp.logical_and(pl.program_id(0) == 0, pl.program_id(1) == 0)
    last = lambda: jnp.logical_and(pl.program_id(0) == nh - 1, pl.program_id(1) == nq - 1)
    return _host_pcall(body, hosted, first, last, n_in=4, n_out=2, n_scratch=1, name="attn_fwd", grid=(nh, nq),
                       in_specs=[blk, full, full, pl.BlockSpec((hp, 1, s), lambda h, i: (h, 0, 0))],
                       out_specs=[blk, pl.BlockSpec((hp, tq, 1), lambda h, i: (h, i, 0))],
                       out_shape=[_sds((HEADS, s, HEAD_DIM)), _sds((HEADS, s, 1))],
                       scratch_shapes=[pltpu.VMEM((hp, nq, tq, tq), F32)],
                       dims=("parallel", "parallel"), operands=(qh, kh, vh, crow))


def _ssm_param_fn(lr, li, ls, br, bi):
    step = jnp.exp(ls)
    er = jnp.exp(lr * step)
    ab_re = er * jnp.cos(li * step)
    ab_im = er * jnp.sin(li * step)
    num_re = ab_re - 1.0
    num_im = ab_im
    den = lr * lr + li * li
    f_re = (num_re * lr + num_im * li) / den
    f_im = (num_im * lr - num_re * li) / den
    bb_re = f_re * br - f_im * bi
    bb_im = f_re * bi + f_im * br
    return ab_re, ab_im, bb_re, bb_im


_PARAM_SHAPE = (SSM_GROUPS * SSM_GROUP, SSM_STATE)


def _ssm_params(lr, li, ls, br, bi):
    def body(lr_ref, li_ref, ls_ref, br_ref, bi_ref, ar_ref, ai_ref, bbr_ref, bbi_ref):
        ar, ai, bbr, bbi = _ssm_param_fn(lr_ref[...], li_ref[...], ls_ref[...], br_ref[...], bi_ref[...])
        ar_ref[...] = ar
        ai_ref[...] = ai
        bbr_ref[...] = bbr
        bbi_ref[...] = bbi

    spec = pl.BlockSpec(_PARAM_SHAPE, lambda: (0, 0))
    return _pcall(body, name="ssm_params", in_specs=[spec] * 5, out_specs=[spec] * 4,
                  out_shape=[_sds(_PARAM_SHAPE)] * 4)(lr, li, ls, br, bi)


def _ssm_params_bwd(lr, li, ls, br, bi, dar, dai, dbbr, dbbi, expand_t):
    def body(lr_ref, li_ref, ls_ref, br_ref, bi_ref, dar_ref, dai_ref, dbbr_ref, dbbi_ref, et_ref,
             dlr_ref, dli_ref, dls_ref, dbr_ref, dbi_ref):
        _, vjp = jax.vjp(_ssm_param_fn, lr_ref[...], li_ref[...], ls_ref[...], br_ref[...], bi_ref[...])
        dlr, dli, dls, dbr, dbi = vjp((dar_ref[...], dai_ref[...], dbbr_ref[...], dbbi_ref[...]))
        et = et_ref[...]
        dlr_ref[...] = _dot_exact_l(et, dlr)
        dli_ref[...] = _dot_exact_l(et, dli)
        dls_ref[...] = jnp.sum(_dot_exact_l(et, dls), axis=-1, keepdims=True)
        dbr_ref[...] = dbr
        dbi_ref[...] = dbi

    spec = pl.BlockSpec(_PARAM_SHAPE, lambda: (0, 0))
    gspec = pl.BlockSpec((SSM_GROUPS, SSM_STATE), lambda: (0, 0))
    return _pcall(body, name="ssm_params_bwd",
                  in_specs=[spec] * 9 + [pl.BlockSpec((SSM_GROUPS, _PARAM_SHAPE[0]), lambda: (0, 0))],
                  out_specs=[gspec, gspec, pl.BlockSpec((SSM_GROUPS, 1), lambda: (0, 0)), spec, spec],
                  out_shape=[_sds((SSM_GROUPS, SSM_STATE))] * 2 + [_sds((SSM_GROUPS, 1))] + [_sds(_PARAM_SHAPE)] * 2,
                  )(lr, li, ls, br, bi, dar, dai, dbbr, dbbi, expand_t)


def _cmul(ar, ai, br, bi):
    return ar * br - ai * bi, ar * bi + ai * br


def _scan_consts(ar, ai, width, reverse):
    row = lax.broadcasted_iota(jnp.int32, (SUBLANES, width), 0)
    pw = [(ar, ai)]
    for _ in range(SUBLANES - 1):
        pw.append(_cmul(pw[-1][0], pw[-1][1], ar, ai))
    steps = []
    for d in (1, 2, 4):
        keep = (row < SUBLANES - d) if reverse else (row >= d)
        steps.append((d, jnp.where(keep, pw[d - 1][0], 0.0), jnp.where(keep, pw[d - 1][1], 0.0)))
    pr = jnp.zeros((SUBLANES, width), F32)
    pi = jnp.zeros((SUBLANES, width), F32)
    for r in range(SUBLANES):
        e = (SUBLANES - r) if reverse else (r + 1)
        pr = jnp.where(row == r, pw[e - 1][0], pr)
        pi = jnp.where(row == r, pw[e - 1][1], pi)
    return steps, pr, pi


def _scan_tile(xr, xi, cr, ci, consts, reverse):
    steps, pr, pi = consts
    for d, mr, mi in steps:
        sh = (SUBLANES - d) if reverse else d
        sr = pltpu.roll(xr, sh, 0)
        si = pltpu.roll(xi, sh, 0)
        xr, xi = xr + mr * sr - mi * si, xi + mr * si + mi * sr
    return xr + pr * cr - pi * ci, xi + pr * ci + pi * cr


def _ssm_fwd(ub, uf, bbr, bbi, ar, ai, ccr, cci, dsk, hosted=None):
    s = ub.shape[0]
    tm = _row_tile(s)
    nt = tm // SUBLANES

    def body(ub_ref, u_ref, bbr_ref, bbi_ref, ar_ref, ai_ref, ccr_ref, cci_ref, dsk_ref,
             xr_ref, xi_ref, y_ref, cr_s, ci_s):
        i = pl.program_id(1)

        @pl.when(i == 0)
        def _():
            cr_s[...] = jnp.zeros_like(cr_s)
            ci_s[...] = jnp.zeros_like(ci_s)

        u_b = ub_ref[...]
        xr_ref[...] = _dot(u_b, bbr_ref[0])
        xi_ref[...] = _dot(u_b, bbi_ref[0])
        consts = _scan_consts(ar_ref[0], ai_ref[0], CHUNK_S, False)

        def tile(k, carry):
            cr, ci = carry
            sl = pl.ds(pl.multiple_of(k * SUBLANES, SUBLANES), SUBLANES)
            xr, xi = _scan_tile(xr_ref[sl, :], xi_ref[sl, :], cr, ci, consts, False)
            xr_ref[sl, :] = xr
            xi_ref[sl, :] = xi
            return xr[SUBLANES - 1:SUBLANES, :], xi[SUBLANES - 1:SUBLANES, :]

        cr, ci = lax.fori_loop(0, nt, tile, (cr_s[...], ci_s[...]))
        cr_s[...] = cr
        ci_s[...] = ci
        y_ref[...] = (_dot(xr_ref[...].astype(BF16), ccr_ref[0]) - _dot(xi_ref[...].astype(BF16), cci_ref[0])
                      + dsk_ref[...] * u_ref[...])

    wspec = lambda a, b: pl.BlockSpec((1, a, b), lambda j, i: (j, 0, 0))
    nb = s // tm
    first = lambda: jnp.logical_and(pl.program_id(0) == 0, pl.program_id(1) == 0)
    last = lambda: jnp.logical_and(pl.program_id(0) == SSM_CHUNKS - 1, pl.program_id(1) == nb - 1)
    return _host_pcall(
        body, hosted, first, last, n_in=9, n_out=3, n_scratch=2, name="ssm_fwd", grid=(SSM_CHUNKS, nb),
        in_specs=[pl.BlockSpec((tm, CHUNK_U), lambda j, i: (i, j)),
                  pl.BlockSpec((tm, CHUNK_U), lambda j, i: (i, j)),
                  wspec(CHUNK_U, CHUNK_S), wspec(CHUNK_U, CHUNK_S), wspec(1, CHUNK_S), wspec(1, CHUNK_S),
                  wspec(CHUNK_S, CHUNK_U), wspec(CHUNK_S, CHUNK_U),
                  pl.BlockSpec((1, CHUNK_U), lambda j, i: (0, j))],
        out_specs=[pl.BlockSpec((tm, CHUNK_S), lambda j, i: (i, j)), pl.BlockSpec((tm, CHUNK_S), lambda j, i: (i, j)),
                   pl.BlockSpec((tm, CHUNK_U), lambda j, i: (i, j))],
        out_shape=[_sds((s, N_STATE)), _sds((s, N_STATE)), _sds((s, SSM_W))],
        scratch_shapes=[pltpu.VMEM((1, CHUNK_S), F32)] * 2,
        dims=("parallel", "arbitrary"), operands=(ub, uf, bbr, bbi, ar, ai, ccr, cci, dsk))


def _ssm_glu(y, w_glu, b_glu):
    ge = _gelu(y)
    sg = _sigmoid(_dot(ge.astype(BF16), w_glu) + b_glu)
    return ge, sg


def _mix_out(y, att, x, w_glu, b_glu, g_att, g_ssm, w_out, g_ffn, hosted=None):
    s = x.shape[0]
    tm = _row_tile(s)

    def body(y_ref, att_ref, x_ref, wg_ref, bg_ref, ga_ref, gs_ref, wo_ref, gf_ref, x1_ref, mix_ref, h2_ref):
        ge, sg = _ssm_glu(y_ref[...], wg_ref[...], bg_ref[...])
        ms = _rms(ge * sg, gs_ref[...]).astype(BF16)
        ma = _rms(_merge_heads(att_ref), ga_ref[...]).astype(BF16)
        mix_ref[:, 0:ATTN_W] = ma
        mix_ref[:, ATTN_W:D_MODEL] = ms
        x1 = x_ref[...] + (_dot(ma, wo_ref[0:ATTN_W, :]) + _dot(ms, wo_ref[ATTN_W:D_MODEL, :]))
        x1_ref[...] = x1
        h2_ref[...] = _rms(x1, gf_ref[...]).astype(BF16)

    row = lambda w: pl.BlockSpec((tm, w), lambda i: (i, 0))
    const = lambda a, b: pl.BlockSpec((a, b), lambda i: (0, 0))
    nb = s // tm
    return _host_pcall(body, hosted, lambda: pl.program_id(0) == 0, lambda: pl.program_id(0) == nb - 1,
                       n_in=9, n_out=3, n_scratch=0, name="mix_out", grid=(nb,),
                       in_specs=[row(SSM_W), pl.BlockSpec((HEADS, tm, HEAD_DIM), lambda i: (0, i, 0)), row(D_MODEL),
                                 const(SSM_W, SSM_W), const(1, SSM_W),
                                 const(1, ATTN_W), const(1, SSM_W), const(D_MODEL, D_MODEL), const(1, D_MODEL)],
                       out_specs=[row(D_MODEL)] * 3,
                       out_shape=[_sds((s, D_MODEL)), _sds((s, D_MODEL), BF16), _sds((s, D_MODEL), BF16)],
                       scratch_shapes=[], dims=("parallel",), operands=(y, att, x, w_glu, b_glu, g_att, g_ssm, w_out, g_ffn))


CONV_CHUNK = 64


def _conv_rows(pad_ref, w, b, r0, n):
    y = b + pad_ref[pl.ds(r0 + SUBLANES - 2, n), :] * w[0:1, :]
    y = y + pad_ref[pl.ds(r0 + SUBLANES - 1, n), :] * w[1:2, :]
    return y + pad_ref[pl.ds(r0 + SUBLANES, n), :] * w[2:3, :]


def _fill_front_pad(pad_ref, strip_ref, s):
    pad_ref[0:SUBLANES, :] = jnp.zeros((SUBLANES, STRIP), F32)
    for r0 in range(0, s, CONV_CHUNK):
        pad_ref[pl.ds(SUBLANES + r0, CONV_CHUNK), :] = strip_ref[pl.ds(r0, CONV_CHUNK), :]


def _conv_act(up, conv_w, conv_b):
    s = up.shape[0]

    def body(ug_ref, uv_ref, wg_ref, wv_ref, bg_ref, bv_ref, act_ref, pg_ref, pv_ref):
        _fill_front_pad(pg_ref, ug_ref, s)
        _fill_front_pad(pv_ref, uv_ref, s)
        wg, wv, bg, bv = wg_ref[...], wv_ref[...], bg_ref[...], bv_ref[...]
        for r0 in range(0, s, CONV_CHUNK):
            hg = _conv_rows(pg_ref, wg, bg, r0, CONV_CHUNK)
            hv = _conv_rows(pv_ref, wv, bv, r0, CONV_CHUNK)
            act_ref[pl.ds(r0, CONV_CHUNK), :] = (hg * _sigmoid(hg) * hv).astype(BF16)

    strip = lambda off: pl.BlockSpec((s, STRIP), lambda j: (0, j + off))
    wsp = lambda off: pl.BlockSpec((3, STRIP), lambda j: (0, j + off))
    bsp = lambda off: pl.BlockSpec((1, STRIP), lambda j: (0, j + off))
    return _pcall(body, name="conv_act", grid=(N_STRIPS,),
                  in_specs=[strip(0), strip(N_STRIPS), wsp(0), wsp(N_STRIPS), bsp(0), bsp(N_STRIPS)],
                  out_specs=pl.BlockSpec((s, STRIP), lambda j: (0, j)), out_shape=_sds((s, D_FF), BF16),
                  scratch_shapes=[pltpu.VMEM((s + SUBLANES, STRIP), F32)] * 2,
                  dims=("parallel",))(up, up, conv_w, conv_w, conv_b, conv_b)


def _down_loss(act, w_down, x1, tgt):
    s = x1.shape[0]
    tm = _row_tile(s)

    def body(a_ref, w_ref, x1_ref, t_ref, dy_ref, dyb_ref, loss_ref):
        i = pl.program_id(0)

        @pl.when(i == 0)
        def _():
            loss_ref[...] = jnp.zeros_like(loss_ref)

        diff = x1_ref[...] + _dot(a_ref[...], w_ref[...]) - t_ref[...]
        dy = diff * (1.0 / D_MODEL)
        dy_ref[...] = dy
        dyb_ref[...] = dy.astype(BF16)
        loss_ref[...] += 0.5 * jnp.sum(diff * dy)

    row = lambda w: pl.BlockSpec((tm, w), lambda i: (i, 0))
    return _pcall(body, name="down_loss", grid=(s // tm,),
                  in_specs=[row(D_FF), pl.BlockSpec((D_FF, D_MODEL), lambda i: (0, 0)), row(D_MODEL), row(D_MODEL)],
                  out_specs=[row(D_MODEL), row(D_MODEL), pl.BlockSpec((SUBLANES, LANES), lambda i: (0, 0))],
                  out_shape=[_sds((s, D_MODEL)), _sds((s, D_MODEL), BF16), _sds((SUBLANES, LANES))],
                  dims=("arbitrary",))(act, w_down, x1, tgt)


def _conv_act_bwd(up, dact, conv_w, conv_b):
    s = up.shape[0]
    ch = CONV_CHUNK

    def body(ug_ref, uv_ref, da_ref, wg_ref, wv_ref, bg_ref, bv_ref, dup_ref, dcw_ref, pg_ref, pv_ref, dg_ref, dv_ref):
        _fill_front_pad(pg_ref, ug_ref, s)
        _fill_front_pad(pv_ref, uv_ref, s)
        zero = jnp.zeros((SUBLANES, STRIP), F32)
        dg_ref[pl.ds(s, SUBLANES), :] = zero
        dv_ref[pl.ds(s, SUBLANES), :] = zero
        wg, wv, bg, bv = wg_ref[...], wv_ref[...], bg_ref[...], bv_ref[...]
        tile_sum = lambda t: jnp.sum(t.reshape(ch // SUBLANES, SUBLANES, STRIP), axis=0)
        accs = [[zero] * 4, [zero] * 4]
        for r0 in range(0, s, ch):
            hg = _conv_rows(pg_ref, wg, bg, r0, ch)
            hv = _conv_rows(pv_ref, wv, bv, r0, ch)
            sg = _sigmoid(hg)
            da = da_ref[pl.ds(r0, ch), :]
            dhs = (da * hv * (sg * (1.0 + hg * (1.0 - sg))), da * (hg * sg))
            for half, (dh, d_ref, p_ref) in enumerate(zip(dhs, (dg_ref, dv_ref), (pg_ref, pv_ref))):
                d_ref[pl.ds(r0, ch), :] = dh
                for k in range(3):
                    accs[half][k] = accs[half][k] + tile_sum(dh * p_ref[pl.ds(r0 + SUBLANES - 2 + k, ch), :])
                accs[half][3] = accs[half][3] + tile_sum(dh)
        for half, (d_ref, w) in enumerate(((dg_ref, wg), (dv_ref, wv))):
            for r0 in range(0, s, ch):
                dup = (d_ref[pl.ds(r0, ch), :] * w[2:3, :] + d_ref[pl.ds(r0 + 1, ch), :] * w[1:2, :]
                       + d_ref[pl.ds(r0 + 2, ch), :] * w[0:1, :])
                dup_ref[half, pl.ds(r0, ch), :] = dup.astype(BF16)
            rid = lax.broadcasted_iota(jnp.int32, (SUBLANES, STRIP), 0)
            out = zero
            for k in range(4):
                out = jnp.where(rid == k, jnp.sum(accs[half][k], axis=0, keepdims=True), out)
            dcw_ref[half] = out

    strip = lambda off: pl.BlockSpec((s, STRIP), lambda j: (0, j + off))
    wsp = lambda off: pl.BlockSpec((3, STRIP), lambda j: (0, j + off))
    bsp = lambda off: pl.BlockSpec((1, STRIP), lambda j: (0, j + off))
    return _pcall(body, name="conv_act_bwd", grid=(N_STRIPS,),
                  in_specs=[strip(0), strip(N_STRIPS), strip(0), wsp(0), wsp(N_STRIPS), bsp(0), bsp(N_STRIPS)],
                  out_specs=[pl.BlockSpec((2, s, STRIP), lambda j: (0, 0, j)), pl.BlockSpec((2, SUBLANES, STRIP), lambda j: (0, 0, j))],
                  out_shape=[_sds((2, s, D_FF), BF16), _sds((2, SUBLANES, D_FF))],
                  scratch_shapes=[pltpu.VMEM((s + SUBLANES, STRIP), F32)] * 4,
                  dims=("parallel",))(up, up, dact, conv_w, conv_w, conv_b, conv_b)


def _mix_bwd(dy, dh2, x1, g_ffn, w_out, y, att, w_glu, b_glu, g_att, g_ssm):
    s = dy.shape[0]
    tm = _row_tile(s)

    def body(dy_ref, dh2_ref, x1_ref, gf_ref, wo_ref, y_ref, att_ref, wg_ref, bg_ref, ga_ref, gs_ref,
             dx1_ref, dx1b_ref, datt_ref, dys_ref, dwg_ref, dgf_ref, dga_ref, dgs_ref, dbg_ref):
        i = pl.program_id(0)

        @pl.when(i == 0)
        def _():
            for r in (dwg_ref, dgf_ref, dga_ref, dgs_ref, dbg_ref):
                r[...] = jnp.zeros_like(r)

        dxn, dgf = _rms_bwd(x1_ref[...], gf_ref[...], dh2_ref[...])
        dx1 = dy_ref[...] + dxn
        dx1_ref[...] = dx1
        dx1b = dx1.astype(BF16)
        dx1b_ref[...] = dx1b
        dgf_ref[...] += dgf
        dma = _dot_nt(dx1b, wo_ref[0:ATTN_W, :])
        dms = _dot_nt(dx1b, wo_ref[ATTN_W:D_MODEL, :])
        datt, dga = _rms_bwd(_merge_heads(att_ref), ga_ref[...], dma)
        _split_heads(datt_ref, datt)
        dga_ref[...] += dga
        yv = y_ref[...]
        ge, sg = _ssm_glu(yv, wg_ref[...], bg_ref[...])
        dssm, dgs = _rms_bwd(ge * sg, gs_ref[...], dms)
        dgs_ref[...] += dgs
        dgl = dssm * ge * sg * (1.0 - sg)
        dglb = dgl.astype(BF16)
        dge = dssm * sg + _dot_nt(dglb, wg_ref[...])
        dbg_ref[...] += jnp.sum(dgl, axis=0, keepdims=True)
        dwg_ref[...] += _dot_tn(ge.astype(BF16), dglb)
        dys_ref[...] = dge * _gelu_grad(yv)

    row = lambda w: pl.BlockSpec((tm, w), lambda i: (i, 0))
    const = lambda a, b: pl.BlockSpec((a, b), lambda i: (0, 0))
    heads = pl.BlockSpec((HEADS, tm, HEAD_DIM), lambda i: (0, i, 0))
    return _pcall(body, name="mix_bwd", grid=(s // tm,),
                  in_specs=[row(D_MODEL), row(D_MODEL), row(D_MODEL), const(1, D_MODEL), const(D_MODEL, D_MODEL), row(SSM_W),
                            heads, const(SSM_W, SSM_W), const(1, SSM_W), const(1, ATTN_W), const(1, SSM_W)],
                  out_specs=[row(D_MODEL), row(D_MODEL), heads, row(SSM_W), const(SSM_W, SSM_W), const(1, D_MODEL),
                             const(1, ATTN_W), const(1, SSM_W), const(1, SSM_W)],
                  out_shape=[_sds((s, D_MODEL)), _sds((s, D_MODEL), BF16), _sds((HEADS, s, HEAD_DIM)), _sds((s, SSM_W)),
                             _sds((SSM_W, SSM_W)), _sds((1, D_MODEL)), _sds((1, ATTN_W)), _sds((1, SSM_W)), _sds((1, SSM_W))],
                  dims=("arbitrary",))(dy, dh2, x1, g_ffn, w_out, y, att, w_glu, b_glu, g_att, g_ssm)


def _ssm_bwd(dys, uf, ub, xr, xi, bbr, bbi, ar, ai, ccr, cci, dsk, hosted=None):
    s = dys.shape[0]
    tm = _row_tile(s)
    nb = s // tm
    nt = tm // SUBLANES

    def body(dy_ref, u_ref, ub_ref, xr_ref, xi_ref, xrp_ref, xip_ref, bbr_ref, bbi_ref, ar_ref, ai_ref, ccr_ref,
             cci_ref, dsk_ref, du_ref, dbbr_ref, dbbi_ref, dccr_ref, dcci_ref, dar_ref, dai_ref, dd_ref,
             gr_s, gi_s, cr_s, ci_s, accr_s, acci_s):
        i = pl.program_id(1)
        first_block = i == nb - 1

        @pl.when(i == 0)
        def _():
            for r in (cr_s, ci_s, accr_s, acci_s, dbbr_ref, dbbi_ref, dccr_ref, dcci_ref, dd_ref):
                r[...] = jnp.zeros_like(r)

        dy = dy_ref[...]
        dyb = dy.astype(BF16)
        gr_s[...] = _dot_nt(dyb, ccr_ref[0])
        gi_s[...] = -_dot_nt(dyb, cci_ref[0])
        consts = _scan_consts(ar_ref[0], -ai_ref[0], CHUNK_S, True)
        row = lax.broadcasted_iota(jnp.int32, (SUBLANES, CHUNK_S), 0)

        def tile(kk, carry):
            cr, ci, accr, acci = carry
            k = nt - 1 - kk
            sl = pl.ds(pl.multiple_of(k * SUBLANES, SUBLANES), SUBLANES)
            gr, gi = _scan_tile(gr_s[sl, :], gi_s[sl, :], cr, ci, consts, True)
            gr_s[sl, :] = gr
            gi_s[sl, :] = gi
            slp = pl.ds(pl.multiple_of(jnp.maximum(k - 1, 0) * SUBLANES, SUBLANES), SUBLANES)
            inner = k > 0
            pr_t = jnp.where(inner, xr_ref[slp, :], xrp_ref[...])
            pi_t = jnp.where(inner, xi_ref[slp, :], xip_ref[...])
            live = jnp.logical_or(inner, jnp.logical_not(first_block))
            top_r = jnp.where(live, pltpu.roll(pr_t, 1, 0), 0.0)
            top_i = jnp.where(live, pltpu.roll(pi_t, 1, 0), 0.0)
            xpr = jnp.where(row == 0, top_r, pltpu.roll(xr_ref[sl, :], 1, 0))
            xpi = jnp.where(row == 0, top_i, pltpu.roll(xi_ref[sl, :], 1, 0))
            accr = accr + gr * xpr + gi * xpi
            acci = acci + gi * xpr - gr * xpi
            return gr[0:1, :], gi[0:1, :], accr, acci

        zeros = jnp.zeros((SUBLANES, CHUNK_S), F32)
        cr, ci, accr, acci = lax.fori_loop(0, nt, tile, (cr_s[...], ci_s[...], zeros, zeros))
        cr_s[...] = cr
        ci_s[...] = ci
        accr_s[...] += accr
        acci_s[...] += acci
        grb = gr_s[...].astype(BF16)
        gib = gi_s[...].astype(BF16)
        u_b = ub_ref[...]
        du_ref[...] = _dot_nt(grb, bbr_ref[0]) + _dot_nt(gib, bbi_ref[0]) + dsk_ref[...] * dy
        dbbr_ref[0] += _dot_tn(u_b, grb)
        dbbi_ref[0] += _dot_tn(u_b, gib)
        dccr_ref[0] += _dot_tn(xr_ref[...].astype(BF16), dyb)
        dcci_ref[0] -= _dot_tn(xi_ref[...].astype(BF16), dyb)
        dd_ref[...] += jnp.sum(dy * u_ref[...], axis=0, keepdims=True)

        @pl.when(i == nb - 1)
        def _():
            dar_ref[0] = jnp.sum(accr_s[...], axis=0, keepdims=True)
            dai_ref[0] = jnp.sum(acci_s[...], axis=0, keepdims=True)

    tiles_per_block = tm // SUBLANES
    rb = lambda i: nb - 1 - i
    wspec = lambda a, b: pl.BlockSpec((1, a, b), lambda j, i: (j, 0, 0))
    xblk = pl.BlockSpec((tm, CHUNK_S), lambda j, i: (rb(i), j))
    xprev = pl.BlockSpec((SUBLANES, CHUNK_S), lambda j, i: (jnp.maximum(rb(i) * tiles_per_block - 1, 0), j))
    ublk = pl.BlockSpec((tm, CHUNK_U), lambda j, i: (rb(i), j))
    first = lambda: jnp.logical_and(pl.program_id(0) == 0, pl.program_id(1) == 0)
    last = lambda: jnp.logical_and(pl.program_id(0) == SSM_CHUNKS - 1, pl.program_id(1) == nb - 1)
    return _host_pcall(
        body, hosted, first, last, n_in=14, n_out=8, n_scratch=6, name="ssm_bwd", grid=(SSM_CHUNKS, nb),
        in_specs=[ublk, ublk, ublk, xblk, xblk, xprev, xprev,
                  wspec(CHUNK_U, CHUNK_S), wspec(CHUNK_U, CHUNK_S), wspec(1, CHUNK_S), wspec(1, CHUNK_S),
                  wspec(CHUNK_S, CHUNK_U), wspec(CHUNK_S, CHUNK_U), pl.BlockSpec((1, CHUNK_U), lambda j, i: (0, j))],
        out_specs=[ublk, wspec(CHUNK_U, CHUNK_S), wspec(CHUNK_U, CHUNK_S), wspec(CHUNK_S, CHUNK_U),
                   wspec(CHUNK_S, CHUNK_U), wspec(1, CHUNK_S), wspec(1, CHUNK_S),
                   pl.BlockSpec((1, CHUNK_U), lambda j, i: (0, j))],
        out_shape=[_sds((s, SSM_W)), _sds((SSM_CHUNKS, CHUNK_U, CHUNK_S)), _sds((SSM_CHUNKS, CHUNK_U, CHUNK_S)),
                   _sds((SSM_CHUNKS, CHUNK_S, CHUNK_U)), _sds((SSM_CHUNKS, CHUNK_S, CHUNK_U)),
                   _sds((SSM_CHUNKS, 1, CHUNK_S)), _sds((SSM_CHUNKS, 1, CHUNK_S)), _sds((1, SSM_W))],
        scratch_shapes=[pltpu.VMEM((tm, CHUNK_S), F32)] * 2 + [pltpu.VMEM((1, CHUNK_S), F32)] * 2
                       + [pltpu.VMEM((SUBLANES, CHUNK_S), F32)] * 2,
        dims=("parallel", "arbitrary"), operands=(dys, uf, ub, xr, xi, xr, xi, bbr, bbi, ar, ai, ccr, cci, dsk))


def _attn_probs(q, ks, cs, lse, scale, diagonal):
    p = jnp.exp(_dot_nt(q, ks) * scale - cs - lse)
    if diagonal:
        tq, tk = p.shape
        causal = lax.broadcasted_iota(jnp.int32, (tq, tk), 1) <= lax.broadcasted_iota(jnp.int32, (tq, tk), 0)
        p = jnp.where(causal, p, 0.0)
    return p


def _attn_bwd(qh, kh, vh, crow, lse, doh, hosted=None):
    _, s, _ = qh.shape
    tq = _row_tile(s)
    nq = s // tq
    scale = HEAD_DIM ** -0.5
    hp = HEADS_PER_STEP

    def body(q_ref, k_ref, v_ref, c_ref, lse_ref, do_ref, dq_ref, dk_ref, dv_ref, dc_ref, p_s, dp_s):
        i = pl.program_id(1)

        @pl.when(i == 0)
        def _():
            for r in (dk_ref, dv_ref, dc_ref):
                r[...] = jnp.zeros_like(r)

        dobs = [do_ref[hh].astype(BF16) for hh in range(hp)]

        def first(j, dls, diagonal):
            off = pl.multiple_of(j * tq, tq)
            out = []
            for hh in range(hp):
                p = _attn_probs(q_ref[hh], k_ref[hh, pl.ds(off, tq), :], c_ref[hh, :, pl.ds(off, tq)], lse_ref[hh],
                                scale, diagonal)
                dp = _dot_nt(dobs[hh], v_ref[hh, pl.ds(off, tq), :])
                p_s[hh, j] = p
                dp_s[hh, j] = dp
                out.append(dls[hh] + jnp.sum(p * dp, axis=-1, keepdims=True))
            return tuple(out)

        zero_col = jnp.zeros((tq, 1), F32)
        dls = lax.fori_loop(0, i, lambda j, c: first(j, c, False), (zero_col,) * hp)
        dls = first(i, dls, True)

        def second(j, dqs):
            rows = pl.ds(pl.multiple_of(j * tq, tq), tq)
            out = []
            for hh in range(hp):
                p = p_s[hh, j]
                ds = p * (dp_s[hh, j] - dls[hh])
                dsb = ds.astype(BF16)
                dv_ref[hh, rows, :] += _dot_tn(p.astype(BF16), dobs[hh])
                dk_ref[hh, rows, :] += _dot_tn(dsb, q_ref[hh]) * scale
                dc_ref[hh, :, rows] -= jnp.sum(ds, axis=0, keepdims=True)
                out.append(dqs[hh] + _dot(dsb, k_ref[hh, rows, :]))
            return tuple(out)

        dqs = lax.fori_loop(0, i + 1, second, (jnp.zeros((tq, HEAD_DIM), F32),) * hp)
        for hh in range(hp):
            dq_ref[hh] = dqs[hh] * scale

    blk = pl.BlockSpec((hp, tq, HEAD_DIM), lambda h, i: (h, i, 0))
    full = pl.BlockSpec((hp, s, HEAD_DIM), lambda h, i: (h, 0, 0))
    crow_spec = pl.BlockSpec((hp, 1, s), lambda h, i: (h, 0, 0))
    nh = HEADS // hp
    first = lambda: jnp.logical_and(pl.program_id(0) == 0, pl.program_id(1) == 0)
    last = lambda: jnp.logical_and(pl.program_id(0) == nh - 1, pl.program_id(1) == nq - 1)
    return _host_pcall(body, hosted, first, last, n_in=6, n_out=4, n_scratch=2, name="attn_bwd", grid=(nh, nq),
                       in_specs=[blk, full, full, crow_spec, pl.BlockSpec((hp, tq, 1), lambda h, i: (h, i, 0)), blk],
                       out_specs=[blk, full, full, crow_spec],
                       out_shape=[_sds((HEADS, s, HEAD_DIM))] * 3 + [_sds((HEADS, 1, s))],
                       scratch_shapes=[pltpu.VMEM((hp, nq, tq, tq), F32)] * 2,
                       dims=("parallel", "arbitrary"), operands=(qh, kh, vh, crow, lse, doh))


def _prep_bwd(z, dqn, dkn, dv, du, dc, gq, gk, bf, gg):
    s = z.shape[0]
    tm = _row_tile(s)
    nb = s // tm

    def body(z_ref, dqn_ref, dkn_ref, dv_ref, du_ref, dc_ref, gq_ref, gk_ref, bf_ref, gg_ref,
             dz_ref, dgq_ref, dgk_ref, dbf_ref, carry_ref):
        i = pl.program_id(0)

        @pl.when(i == 0)
        def _():
            for r in (dgq_ref, dgk_ref, dbf_ref, carry_ref):
                r[...] = jnp.zeros_like(r)

        gg_m = gg_ref[...]

        def head_norm_bwd(t, g, dn):
            r = lax.rsqrt(_dot_exact_r(t * t, gg_m) * (1.0 / HEAD_DIM) + EPS)
            w = dn * g
            mean_wt = _dot_exact_r(w * t, gg_m) * (1.0 / HEAD_DIM)
            return r * w - t * (r * r * r) * mean_wt, jnp.sum(dn * t * r, axis=0, keepdims=True)

        dq, dgq = head_norm_bwd(z_ref[:, 0:ATTN_W], gq_ref[...], _merge_heads(dqn_ref))
        dk, dgk = head_norm_bwd(z_ref[:, ATTN_W:2 * ATTN_W], gk_ref[...], _merge_heads(dkn_ref))
        dgq_ref[...] += dgq
        dgk_ref[...] += dgk
        row = lax.broadcasted_iota(jnp.int32, (tm, tm), 0)
        col = lax.broadcasted_iota(jnp.int32, (tm, tm), 1)
        triu = (col >= row).astype(BF16)
        dlf = _dot_exact_l(triu, dc_ref[...]) + carry_ref[...]
        carry_ref[...] = dlf[0:1, :]
        df = dlf * _sigmoid(-_forget_logits(z_ref, bf_ref))
        dbf_ref[...] += jnp.sum(df, axis=0, keepdims=True)
        dz_ref[:, 0:ATTN_W] = dq.astype(BF16)
        dz_ref[:, ATTN_W:2 * ATTN_W] = dk.astype(BF16)
        dz_ref[:, 2 * ATTN_W:3 * ATTN_W] = _merge_heads(dv_ref).astype(BF16)
        tail = jnp.concatenate([df[:, :HEADS], du_ref[...], jnp.zeros((tm, Z_COLS - IN_COLS), F32)], axis=-1)
        dz_ref[:, F_COL0:Z_COLS] = tail.astype(BF16)

    row_spec = lambda w: pl.BlockSpec((tm, w), lambda i: (nb - 1 - i, 0))
    const = lambda shape: pl.BlockSpec(shape, lambda i: (0, 0))
    return _pcall(body, name="prep_bwd", grid=(nb,),
                  in_specs=[row_spec(Z_COLS)] + [pl.BlockSpec((HEADS, tm, HEAD_DIM), lambda i: (0, nb - 1 - i, 0))] * 3
                           + [row_spec(ATTN_W), row_spec(LANES), const((1, ATTN_W)),
                              const((1, ATTN_W)), const((1, LANES)), const((ATTN_W, ATTN_W))],
                  out_specs=[row_spec(Z_COLS), const((1, ATTN_W)), const((1, ATTN_W)), const((1, LANES))],
                  out_shape=[_sds((s, Z_COLS), BF16), _sds((1, ATTN_W)), _sds((1, ATTN_W)), _sds((1, LANES))],
                  scratch_shapes=[pltpu.VMEM((1, LANES), F32)], dims=("arbitrary",))(z, dqn, dkn, dv, du, dc, gq, gk, bf, gg)


def _in_norm_bwd(x, g_mix, dh, dx1):
    s = x.shape[0]
    tm = _row_tile(s)

    def body(x_ref, g_ref, dh_ref, dx1_ref, dx_ref, dg_ref):
        i = pl.program_id(0)

        @pl.when(i == 0)
        def _():
            dg_ref[...] = jnp.zeros_like(dg_ref)

        dxn, dg = _rms_bwd(x_ref[...], g_ref[...], dh_ref[...])
        dx_ref[...] = dx1_ref[...] + dxn
        dg_ref[...] += dg

    row = pl.BlockSpec((tm, D_MODEL), lambda i: (i, 0))
    vec = pl.BlockSpec((1, D_MODEL), lambda i: (0, 0))
    return _pcall(body, name="in_norm_bwd", grid=(s // tm,), in_specs=[row, vec, row, row], out_specs=[row, vec],
                  out_shape=[_sds((s, D_MODEL)), _sds((1, D_MODEL))], dims=("arbitrary",))(x, g_mix, dh, dx1)


def _adamw_refs(w_ref, g_ref, m_ref, v_ref, d_ref, mo_ref, vo_ref):
    gv = g_ref[...]
    mn = ADAM_B1 * m_ref[...] + (1.0 - ADAM_B1) * gv
    vn = ADAM_B2 * v_ref[...] + (1.0 - ADAM_B2) * (gv * gv)
    m_hat = mn / (1.0 - ADAM_B1 ** ADAM_STEP)
    v_hat = vn / (1.0 - ADAM_B2 ** ADAM_STEP)
    d_ref[...] = -ADAM_LR * (m_hat / (jnp.sqrt(v_hat) + ADAM_EPS) + ADAM_WD * w_ref[...])
    mo_ref[...] = mn
    vo_ref[...] = vn


def _adamw_small(ws, gs, ms, vs):
    n = len(ws)

    def body(*refs):
        ins, outs = refs[:4 * n], refs[4 * n:]
        for i in range(n):
            _adamw_refs(ins[i], ins[n + i], ins[2 * n + i], ins[3 * n + i], *outs[3 * i:3 * i + 3])

    vm = pl.BlockSpec(memory_space=pltpu.VMEM)
    out_shape = [_sds(w.shape) for w in ws for _ in range(3)]
    return _pallas(body, name="adamw_small", in_specs=[vm] * (4 * n), out_specs=[vm] * (3 * n), out_shape=out_shape,
                   compiler_params=pltpu.CompilerParams(vmem_limit_bytes=VMEM_LIMIT))(*ws, *gs, *ms, *vs)


def _adamw(w, g, m, v, *, name):
    r, c = w.shape
    tr = r
    for cand in (256, 176, 128, 64):
        if r > cand and r % cand == 0:
            tr = cand
            break

    def body(w_ref, g_ref, m_ref, v_ref, d_ref, mo_ref, vo_ref):
        _adamw_refs(w_ref, g_ref, m_ref, v_ref, d_ref, mo_ref, vo_ref)

    spec = pl.BlockSpec((tr, c), lambda i: (i, 0))
    return _pcall(body, name=name, grid=(r // tr,), in_specs=[spec] * 4, out_specs=[spec] * 3,
                  out_shape=[_sds((r, c))] * 3, dims=("parallel",))(w, g, m, v)


def _prefetch_call(body, *, name, grid, in_specs, out_specs, out_shape, operands):
    grid_spec = pltpu.PrefetchScalarGridSpec(num_scalar_prefetch=1, grid=grid, in_specs=in_specs, out_specs=out_specs)
    params = pltpu.CompilerParams(dimension_semantics=("parallel",) * len(grid), vmem_limit_bytes=VMEM_LIMIT)
    return _pallas(body, name=name, grid_spec=grid_spec, out_shape=out_shape, compiler_params=params)(*operands)


def _half_rows_tile(hr):
    return hr if hr <= 256 else 176 if hr % 176 == 0 else 256


def _add_half(g, landed, place, *, name):
    def body(place_ref, g_ref, l_ref, o_ref):
        own = g_ref[0] if len(g_ref.shape) == 4 else g_ref[...]
        o_ref[...] = (own + l_ref[...]).astype(BF16)

    if g.ndim == 4:
        _, _, hr, c = g.shape
        tr = _half_rows_tile(hr)
        blk = (1, tr, c)
        return _prefetch_call(
            body, name=name, grid=(N_CHIPS, hr // tr),
            in_specs=[pl.BlockSpec((1,) + blk, lambda j, i, p: (j, p[1], i, 0)), pl.BlockSpec(blk, lambda j, i, p: (j, i, 0))],
            out_specs=pl.BlockSpec(blk, lambda j, i, p: (j, i, 0)), out_shape=_sds(landed.shape, BF16),
            operands=(place, g, landed))
    hr, c = landed.shape
    tr, tc = 256, _tile(c, 2176)
    nb = hr // tr
    return _prefetch_call(
        body, name=name, grid=(nb, c // tc),
        in_specs=[pl.BlockSpec((tr, tc), lambda i, j, p: (p[1] * nb + i, j)), pl.BlockSpec((tr, tc), lambda i, j, p: (i, j))],
        out_specs=pl.BlockSpec((tr, tc), lambda i, j, p: (i, j)), out_shape=_sds(landed.shape, BF16),
        operands=(place, g, landed))


def _sum_chips(chip_sum, lands, place, *, name, tc, window_stride=0):
    _, hr, c = lands.shape
    tr = _half_rows_tile(hr)
    nb = hr // tr
    ncb = c // tc

    def body(place_ref, own_ref, a_ref, b_ref, c_ref, o_ref):
        own = own_ref[0] if len(own_ref.shape) == 3 else own_ref[...]
        o_ref[...] = ((own.astype(F32) + a_ref[0].astype(F32)) + b_ref[0].astype(F32)) + c_ref[0].astype(F32)

    land = lambda k: pl.BlockSpec((1, tr, tc), lambda i, j, p: ((p[0] + k) % N_CHIPS, i, j))
    if chip_sum.ndim == 3:
        own_spec = land(0)
    else:
        stride = window_stride // tc
        own_spec = pl.BlockSpec((tr, tc), lambda i, j, p: (i, p[0] * stride + j))
    return _prefetch_call(
        body, name=name, grid=(nb, ncb), in_specs=[own_spec, land(1), land(2), land(3)],
        out_specs=pl.BlockSpec((tr, tc), lambda i, j, p: (p[1] * nb + i, j)), out_shape=_sds((2 * hr, c)),
        operands=(place, chip_sum, lands, lands, lands))


_HBM = pl.BlockSpec(memory_space=pltpu.HBM)


def _place():
    x, y, c = lax.axis_index("x"), lax.axis_index("y"), lax.axis_index("c")
    chips = [(1 - x, y), (x, 1 - y), (1 - x, 1 - y)]
    return x, y, c, chips


def _rcopy(src, dst, send_sem, recv_sem, to):
    return pltpu.make_async_remote_copy(src_ref=src, dst_ref=dst, send_sem=send_sem, recv_sem=recv_sem,
                                        device_id=to, device_id_type=MESH)


N_BIG = 5
UP_COLS = 2 * D_FF // N_CHIPS
IN_WINDOW = 640
IN_STRIDE = 512


class _Hosted:
    def __init__(self, operands, out_shapes, n_sems, start, finish, aliases=None, local_sems=0):
        self.operands, self.out_shapes, self.n_sems = list(operands), list(out_shapes), n_sems
        self.start, self.finish, self.aliases, self.local_sems = start, finish, dict(aliases or {}), local_sems

    def scratch(self):
        return ([pltpu.SemaphoreType.DMA((self.n_sems,)), pltpu.SemaphoreType.DMA((self.n_sems,))]
                + [pltpu.SemaphoreType.DMA] * self.local_sems)


def _both(a, b):
    na, nao, nas = len(a.operands), len(a.out_shapes), len(a.scratch())

    def start(ins, outs, sems):
        a.start(ins[:na], outs[:nao], sems[:nas])
        b.start(ins[na:], outs[nao:], sems[nas:])

    def finish(ins, outs, sems):
        a.finish(ins[:na], outs[:nao], sems[:nas])
        b.finish(ins[na:], outs[nao:], sems[nas:])

    both = _Hosted(a.operands + b.operands, a.out_shapes + b.out_shapes, 0, start, finish,
                   aliases={**a.aliases, **{na + i: nao + o for i, o in b.aliases.items()}})
    both.scratch = lambda: a.scratch() + b.scratch()
    return both


def _run_hosted(hosted, *, name):
    n_in, n_out = len(hosted.operands), len(hosted.out_shapes)

    def body(*refs):
        parts = (refs[:n_in], refs[n_in:n_in + n_out], refs[n_in + n_out:])
        hosted.start(*parts)
        hosted.finish(*parts)

    return _pallas(body, name=name, in_specs=[_HBM] * n_in, out_specs=[_HBM] * n_out, out_shape=hosted.out_shapes,
                   input_output_aliases=hosted.aliases, scratch_shapes=hosted.scratch())(*hosted.operands)


def _host_pcall(core_body, hosted, first, last, *, n_in, n_out, n_scratch, name, grid, in_specs, out_specs, out_shape,
                scratch_shapes, dims, operands):
    if hosted is None:
        outs = _pcall(core_body, name=name, grid=grid, in_specs=in_specs, out_specs=out_specs, out_shape=out_shape,
                      scratch_shapes=scratch_shapes, dims=dims)(*operands)
        return outs, []
    hi, ho = len(hosted.operands), len(hosted.out_shapes)

    def body(*refs):
        a, b = n_in, n_in + hi
        c, d = b + n_out, b + n_out + ho
        e = d + n_scratch
        parts = (refs[a:b], refs[c:d], refs[e:])

        @pl.when(first())
        def _():
            hosted.start(*parts)

        core_body(*refs[:a], *refs[b:c], *refs[d:e])

        @pl.when(last())
        def _():
            hosted.finish(*parts)

    params = pltpu.CompilerParams(dimension_semantics=("arbitrary",) * len(grid), vmem_limit_bytes=VMEM_LIMIT)
    outs = _pallas(body, name=name, grid=grid, in_specs=list(in_specs) + [_HBM] * hi, out_specs=list(out_specs) + [_HBM] * ho,
                   out_shape=list(out_shape) + hosted.out_shapes, scratch_shapes=list(scratch_shapes) + hosted.scratch(),
                   input_output_aliases={n_in + a: n_out + b for a, b in hosted.aliases.items()},
                   compiler_params=params)(*operands, *hosted.operands)
    return outs[:n_out], outs[n_out:]


def _gather_slot(src, out, chip, hc):
    hr, cols = src.shape[0] // 2, src.shape[1]
    if len(out.shape) == 2:
        return out.at[pl.ds(hc * hr, hr), pl.ds(pl.multiple_of(chip * cols, LANES), cols)]
    return out.at[chip, pl.ds(hc * hr, hr), :]


def _gathered_shape(shard, by_cols):
    if by_cols:
        return _sds((shard.shape[0], N_CHIPS * shard.shape[1]), shard.dtype)
    return _sds((N_CHIPS,) + shard.shape, shard.dtype)


def _plan_gather_ici(shards, by_cols, whole=(), own_cols=()):
    n = len(shards)

    def copies(ins, outs, sems):
        send_sems, recv_sems = sems[0], sems[1]
        x, y, c, chips = _place()
        me = 2 * x + y
        sends, waits = [], []
        for w in range(n + len(whole)):
            for k, (cx, cy) in enumerate(chips):
                sem = (send_sems.at[3 * w + k], recv_sems.at[3 * w + k])
                if w < n:
                    hr = ins[w].shape[0] // 2
                    sends.append(_rcopy(ins[w].at[pl.ds(c * hr, hr), :], _gather_slot(ins[w], outs[w], me, c), *sem, (cx, cy, c)))
                    landed = _gather_slot(ins[w], outs[w], 2 * cx + cy, c)
                else:
                    sends.append(_rcopy(ins[w], outs[w].at[me], *sem, (cx, cy, c)))
                    landed = outs[w].at[2 * cx + cy]
                waits.append(_rcopy(landed, landed, *sem, (cx, cy, c)))
        local = [pltpu.make_async_copy(
            ins[w], outs[w].at[:, pl.ds(pl.multiple_of(me * ins[w].shape[1], LANES), ins[w].shape[1])], sems[2 + i])
            for i, w in enumerate(own_cols)]
        return sends, waits, local

    def start(ins, outs, sems):
        sends, _, local = copies(ins, outs, sems)
        for cp in local + sends:
            cp.start()

    def finish(ins, outs, sems):
        sends, waits, local = copies(ins, outs, sems)
        for cp in waits:
            cp.wait_recv()
        for cp in sends:
            cp.wait_send()
        for cp in local:
            cp.wait()

    out_shapes = [_gathered_shape(s, bc) for s, bc in zip(shards, by_cols)] + [_sds((N_CHIPS,) + a.shape, a.dtype) for a in whole]
    return _Hosted(list(shards) + list(whole), out_shapes, 3 * (n + len(whole)), start, finish, local_sems=len(own_cols))


def _plan_gather_d2d(bufs, shard_shapes):
    n = len(bufs)

    def copies(ins, outs, sems):
        send_sems, recv_sems = sems
        x, y, c, chips = _place()
        sibling = (x, y, 1 - c)
        sends, waits = [], []
        for w in range(n):
            for k, (cx, cy) in enumerate(chips):
                sem = (send_sems.at[3 * w + k], recv_sems.at[3 * w + k])
                landed = _gather_slot(shard_shapes[w], outs[w], 2 * cx + cy, c)
                other = _gather_slot(shard_shapes[w], outs[w], 2 * cx + cy, 1 - c)
                sends.append(_rcopy(landed, landed, *sem, sibling))
                waits.append(_rcopy(other, other, *sem, sibling))
        return sends, waits

    def start(ins, outs, sems):
        for cp in copies(ins, outs, sems)[0]:
            cp.start()

    def finish(ins, outs, sems):
        sends, waits = copies(ins, outs, sems)
        for cp in waits:
            cp.wait_recv()
        for cp in sends:
            cp.wait_send()

    return _Hosted(bufs, [_sds(b.shape, b.dtype) for b in bufs], 3 * n, start, finish, aliases={w: w for w in range(n)})


def _plan_swap(grads):
    def copies(ins, outs, sems):
        send_sems, recv_sems = sems
        x, y, c, _ = _place()
        cps = []
        for w, g_ref in enumerate(ins):
            if len(g_ref.shape) == 4:
                theirs = g_ref.at[:, 1 - c]
            else:
                hr = g_ref.shape[0] // 2
                theirs = g_ref.at[pl.ds((1 - c) * hr, hr), :]
            cps.append(_rcopy(theirs, outs[w], send_sems.at[w], recv_sems.at[w], (x, y, 1 - c)))
        return cps

    def start(ins, outs, sems):
        for cp in copies(ins, outs, sems):
            cp.start()

    def finish(ins, outs, sems):
        for cp in copies(ins, outs, sems):
            cp.wait()

    out_shapes = [_sds((g.shape[0], g.shape[2], g.shape[3])) if g.ndim == 4 else _sds((g.shape[0] // 2, g.shape[1]))
                  for g in grads]
    return _Hosted(grads, out_shapes, len(grads), start, finish)


def _plan_scatter(chip_sums, windows):
    def copies(ins, outs, sems):
        send_sems, recv_sems = sems
        x, y, c, chips = _place()
        me = 2 * x + y
        sends, waits = [], []
        for w, s_ref in enumerate(ins):
            for k, (cx, cy) in enumerate(chips):
                tgt = 2 * cx + cy
                if windows[w] is not None:
                    stride, width = windows[w]
                    part = s_ref.at[:, pl.ds(pl.multiple_of(tgt * stride, LANES), width)]
                else:
                    part = s_ref.at[tgt]
                sem = (send_sems.at[3 * w + k], recv_sems.at[3 * w + k])
                sends.append(_rcopy(part, outs[w].at[me], *sem, (cx, cy, c)))
                slot = outs[w].at[tgt]
                waits.append(_rcopy(slot, slot, *sem, (cx, cy, c)))
        return sends, waits

    def start(ins, outs, sems):
        for cp in copies(ins, outs, sems)[0]:
            cp.start()

    def finish(ins, outs, sems):
        sends, waits = copies(ins, outs, sems)
        for cp in waits:
            cp.wait_recv()
        for cp in sends:
            cp.wait_send()

    out_shapes = [_sds((N_CHIPS, s.shape[0], win[1]), BF16) if win is not None else _sds(s.shape, BF16)
                  for s, win in zip(chip_sums, windows)]
    return _Hosted(chip_sums, out_shapes, 3 * len(chip_sums), start, finish)


def _plan_join(reds):
    def copies(ins, outs, sems):
        send_sems, recv_sems = sems
        x, y, c, _ = _place()
        sends, waits = [], []
        for w, out in enumerate(outs):
            hr = out.shape[0] // 2
            mine = out.at[pl.ds(c * hr, hr), :]
            theirs = out.at[pl.ds((1 - c) * hr, hr), :]
            sends.append(_rcopy(mine, mine, send_sems.at[w], recv_sems.at[w], (x, y, 1 - c)))
            waits.append(_rcopy(theirs, theirs, send_sems.at[w], recv_sems.at[w], (x, y, 1 - c)))
        return sends, waits

    def start(ins, outs, sems):
        for cp in copies(ins, outs, sems)[0]:
            cp.start()

    def finish(ins, outs, sems):
        sends, waits = copies(ins, outs, sems)
        for cp in waits:
            cp.wait_recv()
        for cp in sends:
            cp.wait_send()

    return _Hosted(reds, [_sds(r.shape) for r in reds], len(reds), start, finish, aliases={w: w for w in range(len(reds))})


def _allreduce_small(v):
    m_per = v.shape[0]

    def body(v_ref, out_ref, all_ref, send_sems, recv_sems, local_sem):
        x, y, c, chips = _place()
        me, sibling = (x, y, c), (x, y, 1 - c)

        def rows(px, py, pc):
            return all_ref.at[pl.ds((4 * px + 2 * py + pc) * m_per, m_per), :]

        def copy(k, block, to, src=None):
            return _rcopy(rows(*block) if src is None else src, rows(*block), send_sems.at[k], recv_sems.at[k], to)

        mine = pltpu.make_async_copy(v_ref, rows(*me), local_sem)
        mine.start()
        first = [copy(0, me, sibling, src=v_ref)]
        first += [copy(1 + k, me, (*chip, c), src=v_ref) for k, chip in enumerate(chips)]
        for cp in first:
            cp.start()
        passed = [copy(4 + k, (*chip, c), sibling) for k, chip in enumerate(chips)]
        for k, chip in enumerate(chips):
            copy(1 + k, (*chip, c), me).wait_recv()
            passed[k].start()
        copy(0, sibling, me).wait_recv()
        for k, chip in enumerate(chips):
            copy(4 + k, (*chip, 1 - c), me).wait_recv()
        for cp in first + passed:
            cp.wait_send()
        mine.wait()
        acc = all_ref[pl.ds(0, m_per), :]
        for d in range(1, 8):
            acc = acc + all_ref[pl.ds(d * m_per, m_per), :]
        out_ref[...] = acc

    vm = pl.BlockSpec(memory_space=pltpu.VMEM)
    return _pallas(body, name="allreduce_small", in_specs=[vm], out_specs=vm, out_shape=_sds((m_per, LANES)),
                          scratch_shapes=[pltpu.VMEM((8 * m_per, LANES), F32), pltpu.SemaphoreType.DMA((7,)),
                                          pltpu.SemaphoreType.DMA((7,)), pltpu.SemaphoreType.DMA],
                          compiler_params=pltpu.CompilerParams(vmem_limit_bytes=VMEM_LIMIT))(v)


def _block_diag(blocks):
    j, g, a, b = blocks.shape
    eye = jnp.eye(g, dtype=bool)[None, :, None, :, None]
    return jnp.where(eye, blocks[:, :, :, None, :], jnp.zeros((), blocks.dtype)).reshape(j, g * a, g * b)


def _diag_blocks(m, a, b):
    j = m.shape[0]
    g = m.shape[1] // a
    t = m.reshape(j, g, a, g, b)
    eye = jnp.eye(g, dtype=bool)[None, :, None, :, None]
    return jnp.sum(jnp.where(eye, t, 0.0), axis=3)


def _pack_rows(parts, rows, dtype):
    used = sum(p.shape[0] for p in parts)
    return jnp.concatenate([p.astype(dtype) for p in parts] + [jnp.zeros((rows - used, D_MODEL), dtype)], axis=0)


_SMALL = (("g_mix", (1024,)), ("b_f", (8,)), ("g_q", (64,)), ("g_k", (64,)), ("lambda_re", (32, 64)),
          ("lambda_im", (32, 64)), ("log_step", (32,)), ("b_re", (32, 64, 16)), ("b_im", (32, 64, 16)),
          ("c_re", (32, 16, 64)), ("c_im", (32, 16, 64)), ("d_skip", (32, 16)), ("b_glu", (512,)),
          ("g_attn_out", (512,)), ("g_ssm_out", (512,)), ("g_ffn", (1024,)), ("conv_b", (5632,)))


def _small_rows(shape):
    return -(-math.prod(shape) // LANES)


def _pack_small(arrs, extra=()):
    parts = []
    for a in list(arrs) + list(extra):
        flat = a.reshape(-1)
        rows = -(-flat.shape[0] // LANES)
        parts.append(jnp.pad(flat, (0, rows * LANES - flat.shape[0])).reshape(rows, LANES))
    total = sum(p.shape[0] for p in parts)
    pad = -total % SUBLANES
    if pad:
        parts.append(jnp.zeros((pad, LANES), F32))
    return jnp.concatenate(parts, axis=0)


def _unpack_small(buf, shapes):
    out, r = [], 0
    for shape in shapes:
        n = math.prod(shape)
        rows = -(-n // LANES)
        out.append(buf[r:r + rows].reshape(-1)[:n].reshape(shape))
        r += rows
    return out


def _halves(t):
    return t.reshape(N_CHIPS, 2, t.shape[0] // (2 * N_CHIPS), t.shape[1])


class _MeshComm:
    def __init__(self, args):
        x, y, self.core = lax.axis_index("x"), lax.axis_index("y"), lax.axis_index("c")
        self.chip = 2 * x + y
        self.place = jnp.stack([self.chip, self.core]).astype(jnp.int32)
        self.shards = {n: args[n].astype(BF16) for n in ("w_in", "w_glu", "w_out", "w_up", "w_down")}
        self.conv_w = args["conv_w"]

    def _own(self, stacked, mine):
        return lax.dynamic_update_slice(stacked, mine[None], (self.chip,) + (0,) * mine.ndim)

    def w_in(self):
        sh = self.shards["w_in"]
        (buf,) = _run_hosted(_plan_gather_ici([sh], [False]), name="gather_w_in")
        (buf,) = _run_hosted(_plan_gather_d2d([buf], [sh]), name="pass_w_in")
        whole = self._own(buf, sh).transpose(1, 0, 2).reshape(D_MODEL, IN_COLS)
        return jnp.pad(whole, ((0, 0), (0, Z_COLS - IN_COLS)))

    def gather_first(self):
        self.mid = [self.shards[n] for n in ("w_glu", "w_out", "w_up")]
        return _plan_gather_ici(self.mid, [False, False, True], whole=[self.conv_w], own_cols=[2])

    def gather_second(self, landed):
        self.g_cw = landed[3]
        return _both(_plan_gather_d2d(list(landed[:3]), self.mid), _plan_gather_ici([self.shards["w_down"]], [False]))

    def weights(self, gathered):
        g_glu, g_out, w_up_b = gathered[:3]
        own = self._own
        return (own(g_glu, self.mid[0]).reshape(SSM_W, SSM_W), own(g_out, self.mid[1]).reshape(D_MODEL, D_MODEL), w_up_b,
                own(self.g_cw, self.conv_w).transpose(1, 0, 2).reshape(3, 2 * D_FF))

    def gather_third(self, gathered):
        return _plan_gather_d2d([gathered[3]], [self.shards["w_down"]])

    def w_down(self, passed):
        return self._own(passed[0], self.shards["w_down"]).reshape(D_FF, D_MODEL)

    def swap(self, d_w_down, d_w_up, d_w_glu, d_w_out):
        self.early = [_halves(d_w_down), d_w_up, _halves(d_w_glu), _halves(d_w_out)]
        return _plan_swap(self.early)

    def scatter(self, landed):
        self.early_sums = [_add_half(g, l, self.place, name="add_" + n)
                           for g, l, n in zip(self.early, landed, ("w_down", "w_up", "w_glu", "w_out"))]
        return _plan_scatter(self.early_sums, [None, (UP_COLS, UP_COLS), None, None])

    def reduce(self, early_lands, d_w_in):
        d_in = d_w_in
        (landed,) = _run_hosted(_plan_swap([d_in]), name="swap_halves")
        sum_in = _add_half(d_in, landed, self.place, name="add_w_in")
        (land_in,) = _run_hosted(_plan_scatter([sum_in], [(IN_STRIDE, IN_WINDOW)]), name="scatter_chips")
        es, el = self.early_sums, early_lands
        todo = [(sum_in, land_in, "w_in", LANES, IN_STRIDE), (es[2], el[2], "w_glu", SSM_W, 0),
                (es[3], el[3], "w_out", D_MODEL, 0), (es[1], el[1], "w_up", UP_COLS, UP_COLS),
                (es[0], el[0], "w_down", D_MODEL, 0)]
        reds = _run_hosted(_plan_join([_sum_chips(s, l, self.place, name="sum_" + n, tc=tc, window_stride=st)
                                       for s, l, n, tc, st in todo]), name="join_halves")
        g_big = dict(zip(("w_in", "w_glu", "w_out", "w_up", "w_down"), reds))
        g_big["w_in"] = lax.dynamic_slice_in_dim(reds[0], 2 * self.chip, IN_COLS // N_CHIPS, axis=1)
        return g_big


def _local_step(x, tgt, p, comm):
    s = x.shape[0]
    row = lambda v: v.reshape(1, -1)
    g_mix, g_ffn = row(p["g_mix"]), row(p["g_ffn"])
    g_att, g_ssm, b_glu, conv_b = row(p["g_attn_out"]), row(p["g_ssm_out"]), row(p["b_glu"]), row(p["conv_b"])
    gq = row(jnp.tile(p["g_q"], HEADS))
    gk = row(jnp.tile(p["g_k"], HEADS))
    bf = row(jnp.pad(p["b_f"], (0, LANES - HEADS)))
    gg = jnp.kron(jnp.eye(HEADS, dtype=F32), jnp.ones((HEAD_DIM, HEAD_DIM), F32)).astype(BF16)
    dsk = row(p["d_skip"])

    rep = lambda a: jnp.repeat(a, SSM_GROUP, axis=0)
    lr, li = rep(p["lambda_re"]), rep(p["lambda_im"])
    ls = rep(jnp.broadcast_to(p["log_step"][:, None], (SSM_GROUPS, SSM_STATE)))
    bt_re = p["b_re"].transpose(0, 2, 1).reshape(_PARAM_SHAPE)
    bt_im = p["b_im"].transpose(0, 2, 1).reshape(_PARAM_SHAPE)
    a_re_rep, a_im_rep, bb_re, bb_im = _ssm_params(lr, li, ls, bt_re, bt_im)
    ar = a_re_rep[::SSM_GROUP].reshape(SSM_CHUNKS, 1, CHUNK_S)
    ai = a_im_rep[::SSM_GROUP].reshape(SSM_CHUNKS, 1, CHUNK_S)
    chunked = lambda t: t.reshape(SSM_CHUNKS, SSM_GROUPS // SSM_CHUNKS, SSM_GROUP, SSM_STATE)
    bbr = _block_diag(chunked(bb_re)).astype(BF16)
    bbi = _block_diag(chunked(bb_im)).astype(BF16)
    to_cc = lambda c: _block_diag(chunked(c).transpose(0, 1, 3, 2)).astype(BF16)
    ccr, cci = to_cc(p["c_re"]), to_cc(p["c_im"])

    w_in_r = comm.w_in()
    hb, z = _in_proj(x, g_mix, w_in_r)
    qh, kh, vh, ub, uf, c128 = _attn_prep(z, gq, gk, bf, gg)
    crow = c128[:, :HEADS].T.reshape(HEADS, 1, s)
    (oh, lse), landed = _attn_fwd(qh, kh, vh, crow, comm.gather_first())
    (xr, xi, y), gathered = _ssm_fwd(ub, uf, bbr, bbi, ar, ai, ccr, cci, dsk, comm.gather_second(landed))
    w_glu_b, w_out_b, w_up_b, conv_w_full = comm.weights(gathered)
    (x1, mixb, h2b), passed = _mix_out(y, oh, x, w_glu_b, b_glu, g_att, g_ssm, w_out_b, g_ffn, comm.gather_third(gathered))
    w_down_b = comm.w_down(passed)
    up = _mm(h2b, w_up_b, name="ffn_up", tm=1024, tn=1408, tk=1024)
    act = _conv_act(up, conv_w_full, conv_b)
    dy, dyb, loss_blk = _down_loss(act, w_down_b, x1, tgt)

    d_w_down = _mm(act, dyb, ta=True, name="d_w_down", tm=1408, tn=1024, tk=2048)
    dact = _mm(dyb, w_down_b, tb=True, name="d_act", tm=1024, tn=1408, tk=1024)
    dupb, dcw = _conv_act_bwd(up, dact, conv_w_full, conv_b)
    d_w_up = _mm(h2b, dupb, ta=True, b_parts=2, name="d_w_up", tm=1024, tn=1408, tk=2048)
    dh2 = _mm(dupb, w_up_b, tb=True, a_parts=2, name="d_h2", tm=1024, tn=1024, tk=1408)
    dx1, dx1b, doh, dys, d_w_glu, d_g_ffn, d_g_att, d_g_ssm, d_b_glu = _mix_bwd(
        dy, dh2, x1, g_ffn, w_out_b, y, oh, w_glu_b, b_glu, g_att, g_ssm)
    d_w_out = _mm(mixb, dx1b, ta=True, name="d_w_out", tm=1024, tn=1024, tk=2048)
    (du, dbbr, dbbi, dccr, dcci, dar, dai, dd), swapped = _ssm_bwd(dys, uf, ub, xr, xi, bbr, bbi, ar, ai, ccr, cci, dsk,
                                                                comm.swap(d_w_down, d_w_up, d_w_glu, d_w_out))
    (dqh, dkh, dvh, dcrow), early_lands = _attn_bwd(qh, kh, vh, crow, lse, doh, comm.scatter(swapped))
    dc128 = jnp.pad(dcrow.reshape(HEADS, s).T, ((0, 0), (0, LANES - HEADS)))
    dzb, d_gq, d_gk, d_bf = _prep_bwd(z, dqh, dkh, dvh, du, dc128, gq, gk, bf, gg)
    d_w_in_r = _mm(hb, dzb, ta=True, name="d_w_in", tm=512, tn=Z_COLS, tk=2048)
    dh = _mm(dzb, w_in_r, tb=True, name="d_h", tm=1024, tn=1024, tk=Z_COLS)
    dx, d_g_mix = _in_norm_bwd(x, g_mix, dh, dx1)

    unchunk = lambda t: t.reshape(_PARAM_SHAPE)
    dbb_re = unchunk(_diag_blocks(dbbr, SSM_GROUP, SSM_STATE))
    dbb_im = unchunk(_diag_blocks(dbbi, SSM_GROUP, SSM_STATE))
    first_row = (jnp.arange(_PARAM_SHAPE[0]) % SSM_GROUP == 0)[:, None]
    da_re = jnp.where(first_row, rep(dar.reshape(SSM_GROUPS, SSM_STATE)), 0.0)
    da_im = jnp.where(first_row, rep(dai.reshape(SSM_GROUPS, SSM_STATE)), 0.0)
    expand_t = (jnp.arange(SSM_GROUPS)[:, None] == (jnp.arange(_PARAM_SHAPE[0]) // SSM_GROUP)[None, :]).astype(BF16)
    d_lr, d_li, d_ls, d_bt_re, d_bt_im = _ssm_params_bwd(lr, li, ls, bt_re, bt_im, da_re, da_im, dbb_re, dbb_im, expand_t)
    from_bt = lambda t: t.reshape(SSM_GROUPS, SSM_GROUP, SSM_STATE).transpose(0, 2, 1)
    from_cc = lambda t: _diag_blocks(t, SSM_STATE, SSM_GROUP).transpose(0, 1, 3, 2).reshape(SSM_GROUPS, SSM_GROUP, SSM_STATE)

    small = {
        "g_mix": d_g_mix, "b_f": d_bf[0, :HEADS], "g_q": d_gq.reshape(HEADS, HEAD_DIM).sum(0),
        "g_k": d_gk.reshape(HEADS, HEAD_DIM).sum(0), "lambda_re": d_lr, "lambda_im": d_li, "log_step": d_ls,
        "b_re": from_bt(d_bt_re), "b_im": from_bt(d_bt_im), "c_re": from_cc(dccr), "c_im": from_cc(dcci),
        "d_skip": dd, "b_glu": d_b_glu, "g_attn_out": d_g_att, "g_ssm_out": d_g_ssm, "g_ffn": d_g_ffn,
        "conv_b": dcw[:, 3],
    }
    big = {"w_in": d_w_in_r, "w_glu": d_w_glu, "w_out": d_w_out, "w_up": d_w_up, "w_down": d_w_down}
    return loss_blk[0, 0], dx, big, small, dcw[:, 0:3].transpose(1, 0, 2).reshape(3, 2 * D_FF), early_lands


def kernel(x, g_mix, w_in, b_f, g_q, g_k, lambda_re, lambda_im, log_step, b_re, b_im, c_re, c_im, d_skip, w_glu, b_glu, g_attn_out, g_ssm_out, w_out, g_ffn, w_up, conv_w, conv_b, w_down, loss_target, m_g_mix, m_w_in, m_b_f, m_g_q, m_g_k, m_lambda_re, m_lambda_im, m_log_step, m_b_re, m_b_im, m_c_re, m_c_im, m_d_skip, m_w_glu, m_b_glu, m_g_attn_out, m_g_ssm_out, m_w_out, m_g_ffn, m_w_up, m_conv_w, m_conv_b, m_w_down, v_g_mix, v_w_in, v_b_f, v_g_q, v_g_k, v_lambda_re, v_lambda_im, v_log_step, v_b_re, v_b_im, v_c_re, v_c_im, v_d_skip, v_w_glu, v_b_glu, v_g_attn_out, v_g_ssm_out, v_w_out, v_g_ffn, v_w_up, v_conv_w, v_conv_b, v_w_down):
    args = dict(locals())
    order = ["g_mix", "w_in", "b_f", "g_q", "g_k", "lambda_re", "lambda_im", "log_step", "b_re", "b_im", "c_re", "c_im",
             "d_skip", "w_glu", "b_glu", "g_attn_out", "g_ssm_out", "w_out", "g_ffn", "w_up", "conv_w", "conv_b", "w_down"]
    comm = _MeshComm(args)
    chip = comm.chip
    loss_part, dx, big, small, d_conv_w, early_lands = _local_step(x[0], loss_target[0], args, comm)
    loss = lax.psum(loss_part, ("x", "y", "c"))

    g_big = comm.reduce(early_lands, big["w_in"])

    small_names = [n for n, _ in _SMALL]
    small_shapes = [sh for _, sh in _SMALL]
    gsum = _allreduce_small(_pack_small([small[n] for n in small_names], extra=[d_conv_w]))
    g_small = _unpack_small(gsum, small_shapes + [(3, 2 * D_FF)])
    g_conv_w = lax.dynamic_slice_in_dim(g_small[-1], chip * (2 * D_FF // N_CHIPS), 2 * D_FF // N_CHIPS, axis=1)
    g_small = dict(zip(small_names, g_small[:-1]))

    grad, delta, new_m, new_v = {}, {}, {}, {}
    for n in ("w_in", "w_glu", "w_out", "w_up", "w_down"):
        grad[n] = g_big[n]
        delta[n], new_m[n], new_v[n] = _adamw(args[n], g_big[n], args["m_" + n], args["v_" + n], name="adamw_" + n)
    grad["conv_w"] = g_conv_w
    delta["conv_w"], new_m["conv_w"], new_v["conv_w"] = _adamw(conv_w, g_conv_w, m_conv_w, v_conv_w, name="adamw_conv_w")
    stepped = _adamw_small([args[n] for n in small_names], [g_small[n] for n in small_names],
                           [args["m_" + n] for n in small_names], [args["v_" + n] for n in small_names])
    for i, n in enumerate(small_names):
        grad[n] = g_small[n]
        delta[n], new_m[n], new_v[n] = stepped[3 * i:3 * i + 3]

    return (loss, dx[None], *[grad[n] for n in order], *[delta[n] for n in order], *[new_m[n] for n in order],
            *[new_v[n] for n in order])
```

```python
import math

import jax
import jax.numpy as jnp
from jax import lax
from jax.experimental import pallas as pl
from jax.experimental.pallas import tpu as pltpu

F32 = jnp.float32
BF16 = jnp.bfloat16

D_MODEL = 1024
HEADS = 8
HEAD_DIM = 64
ATTN_W = 512
SSM_W = 512
SSM_GROUPS = 32
SSM_GROUP = 16
SSM_STATE = 64
N_STATE = SSM_GROUPS * SSM_STATE
D_FF = 2816
IN_COLS = 2056
Z_COLS = 2176
F_COL0 = 1536
U_COL0 = 1544
EPS = 1e-6
NEG_INF = -1e30
N_CHIPS = 4
LANES = 128
SUBLANES = 8
SSM_CHUNKS = 2
CHUNK_U = SSM_W // SSM_CHUNKS
CHUNK_S = N_STATE // SSM_CHUNKS
HEADS_PER_STEP = 4
STRIP = 128
N_STRIPS = D_FF // STRIP

ROWS_IN, ROWS_GLU, ROWS_OUT, ROWS_UP, ROWS_DOWN = 514, 64, 256, 1408, 704
OFF_GLU = ROWS_IN
OFF_OUT = OFF_GLU + ROWS_GLU
OFF_UP = OFF_OUT + ROWS_OUT
OFF_DOWN = OFF_UP + ROWS_UP
OFF_SPARE = OFF_DOWN + ROWS_DOWN
PACK_ROWS = 2976
HALF_ROWS = PACK_ROWS // 2
CONVW_ROWS = 9

ADAM_LR = 0.001
ADAM_B1 = 0.9
ADAM_B2 = 0.999
ADAM_EPS = 1e-08
ADAM_WD = 0.01
ADAM_STEP = 10

VMEM_LIMIT = 56 * 1024 * 1024
MESH = pl.DeviceIdType.MESH


def _pallas(body, **kw):
    return pl.pallas_call(body, **kw)


def _pcall(body, *, name, out_shape, in_specs, out_specs, grid=(), scratch_shapes=(), dims=None):
    params = pltpu.CompilerParams(dimension_semantics=dims, vmem_limit_bytes=VMEM_LIMIT)
    return _pallas(body, name=name, grid=grid, in_specs=in_specs, out_specs=out_specs,
                   out_shape=out_shape, scratch_shapes=scratch_shapes, compiler_params=params)


def _sds(shape, dtype=F32):
    return jax.ShapeDtypeStruct(shape, dtype)


def _dot(a, b):
    return jnp.dot(a, b, preferred_element_type=F32)


def _dot_nt(a, b):
    return lax.dot_general(a, b, (((1,), (1,)), ((), ())), preferred_element_type=F32)


def _dot_tn(a, b):
    return lax.dot_general(a, b, (((0,), (0,)), ((), ())), preferred_element_type=F32)


def _split3(x):
    hi = x.astype(BF16)
    r = x - hi.astype(F32)
    mid = r.astype(BF16)
    lo = (r - mid.astype(F32)).astype(BF16)
    return hi, mid, lo


def _dot_exact_r(x, m01):
    hi, mid, lo = _split3(x)
    return _dot(hi, m01) + _dot(mid, m01) + _dot(lo, m01)


def _dot_exact_l(m01, x):
    hi, mid, lo = _split3(x)
    return _dot(m01, hi) + _dot(m01, mid) + _dot(m01, lo)


def _sigmoid(x):
    return 1.0 / (1.0 + jnp.exp(-x))


def _rms(x, g):
    r = lax.rsqrt(jnp.mean(x * x, axis=-1, keepdims=True) + EPS)
    return x * r * g


def _rms_bwd(x, g, dy):
    r = lax.rsqrt(jnp.mean(x * x, axis=-1, keepdims=True) + EPS)
    w = dy * g
    dx = r * w - x * (r * r * r) * jnp.mean(w * x, axis=-1, keepdims=True)
    dg = jnp.sum(dy * x * r, axis=0, keepdims=True)
    return dx, dg


_GELU_K = math.sqrt(2.0 / math.pi)
_GELU_C = 0.044715


def _gelu(y):
    return y * (0.5 * (1.0 + jnp.tanh(_GELU_K * (y + _GELU_C * (y * y * y)))))


def _gelu_grad(y):
    t = jnp.tanh(_GELU_K * (y + _GELU_C * (y * y * y)))
    return 0.5 * (1.0 + t) + 0.5 * y * (1.0 - t * t) * (_GELU_K * (1.0 + 3.0 * _GELU_C * y * y))


def _tile(n, pref):
    if n <= pref:
        return n
    divs = [t for t in range(LANES, n + 1, LANES) if n % t == 0]
    below = [t for t in divs if t <= pref]
    if below and 2 * below[-1] >= pref:
        return below[-1]
    above = [t for t in divs if t > pref]
    return above[0] if above else n


def _row_tile(s):
    return min(256, s)


def _mm(a, b, *, name, tm, tn, tk, ta=False, tb=False, a_parts=1, b_parts=1):
    if a_parts > 1:
        m, kk = a.shape[1], a.shape[2] * a_parts
    elif ta:
        kk, m = a.shape
    else:
        m, kk = a.shape
    if b_parts > 1:
        n = b.shape[2] * b_parts
    else:
        n = b.shape[0] if tb else b.shape[1]
    tm, tn, tk = _tile(m, tm), _tile(n // b_parts, tn), _tile(kk // a_parts, tk)
    k_per, n_per = kk // a_parts // tk, n // b_parts // tn

    def body(a_ref, b_ref, o_ref):
        k = pl.program_id(2)
        if ta:
            part = _dot_tn(a_ref[...], b_ref[...])
        elif tb:
            part = _dot_nt(a_ref[...], b_ref[...])
        else:
            part = _dot(a_ref[...], b_ref[...])

        @pl.when(k == 0)
        def _():
            o_ref[...] = part

        @pl.when(k > 0)
        def _():
            o_ref[...] += part

    if a_parts > 1:
        a_spec = pl.BlockSpec((None, tm, tk), lambda i, j, k: (k // k_per, i, k % k_per))
    else:
        a_spec = pl.BlockSpec((tk, tm), lambda i, j, k: (k, i)) if ta else pl.BlockSpec((tm, tk), lambda i, j, k: (i, k))
    if b_parts > 1:
        b_spec = pl.BlockSpec((None, tk, tn), lambda i, j, k: (j // n_per, k, j % n_per))
    else:
        b_spec = pl.BlockSpec((tn, tk), lambda i, j, k: (j, k)) if tb else pl.BlockSpec((tk, tn), lambda i, j, k: (k, j))
    return _pcall(body, name=name, grid=(m // tm, n // tn, kk // tk), in_specs=[a_spec, b_spec],
                  out_specs=pl.BlockSpec((tm, tn), lambda i, j, k: (i, j)), out_shape=_sds((m, n)),
                  dims=("parallel", "parallel", "arbitrary"))(a, b)


def _in_proj(x, g_mix, w_in_r):
    s = x.shape[0]
    tm = _row_tile(s)

    def body(x_ref, g_ref, w_ref, h_ref, z_ref):
        h = _rms(x_ref[...], g_ref[...]).astype(BF16)
        h_ref[...] = h
        z_ref[...] = _dot(h, w_ref[...])

    return _pcall(body, name="in_proj", grid=(s // tm,),
                  in_specs=[pl.BlockSpec((tm, D_MODEL), lambda i: (i, 0)), pl.BlockSpec((1, D_MODEL), lambda i: (0, 0)),
                            pl.BlockSpec((D_MODEL, Z_COLS), lambda i: (0, 0))],
                  out_specs=[pl.BlockSpec((tm, D_MODEL), lambda i: (i, 0)), pl.BlockSpec((tm, Z_COLS), lambda i: (i, 0))],
                  out_shape=[_sds((s, D_MODEL), BF16), _sds((s, Z_COLS))], dims=("parallel",))(x, g_mix, w_in_r)


def _split_heads(ref, val):
    for h in range(HEADS):
        ref[h] = val[:, h * HEAD_DIM:(h + 1) * HEAD_DIM].astype(ref.dtype)


def _merge_heads(ref):
    return jnp.concatenate([ref[h].astype(F32) for h in range(HEADS)], axis=-1)


def _forget_logits(z_ref, bf_ref):
    fl = z_ref[:, F_COL0:F_COL0 + LANES] + bf_ref[...]
    return jnp.where(lax.broadcasted_iota(jnp.int32, fl.shape, 1) < HEADS, fl, 0.0)


def _attn_prep(z, gq, gk, bf, gg):
    s = z.shape[0]
    tm = _row_tile(s)

    def body(z_ref, gq_ref, gk_ref, bf_ref, gg_ref, qn_ref, kn_ref, vb_ref, ub_ref, uf_ref, c_ref, carry_ref):
        i = pl.program_id(0)

        @pl.when(i == 0)
        def _():
            carry_ref[...] = jnp.zeros_like(carry_ref)

        gg_m = gg_ref[...]

        def head_norm(t, g):
            ssq = _dot_exact_r(t * t, gg_m)
            return t * lax.rsqrt(ssq * (1.0 / HEAD_DIM) + EPS) * g

        _split_heads(qn_ref, head_norm(z_ref[:, 0:ATTN_W], gq_ref[...]))
        _split_heads(kn_ref, head_norm(z_ref[:, ATTN_W:2 * ATTN_W], gk_ref[...]))
        _split_heads(vb_ref, z_ref[:, 2 * ATTN_W:3 * ATTN_W])
        u = z_ref[:, U_COL0:U_COL0 + SSM_W]
        uf_ref[...] = u
        ub_ref[...] = u.astype(BF16)
        fl = _forget_logits(z_ref, bf_ref)
        lf = jnp.minimum(fl, 0.0) - jnp.log1p(jnp.exp(-jnp.abs(fl)))
        row = lax.broadcasted_iota(jnp.int32, (tm, tm), 0)
        col = lax.broadcasted_iota(jnp.int32, (tm, tm), 1)
        tri = (row >= col).astype(BF16)
        c = _dot_exact_l(tri, lf) + carry_ref[...]
        c_ref[...] = c
        carry_ref[...] = c[tm - 1:tm, :]

    row_spec = lambda w: pl.BlockSpec((tm, w), lambda i: (i, 0))
    const = lambda shape: pl.BlockSpec(shape, lambda i: (0, 0))
    heads = pl.BlockSpec((HEADS, tm, HEAD_DIM), lambda i: (0, i, 0))
    return _pcall(body, name="attn_prep", grid=(s // tm,),
                  in_specs=[row_spec(Z_COLS), const((1, ATTN_W)), const((1, ATTN_W)), const((1, LANES)), const((ATTN_W, ATTN_W))],
                  out_specs=[heads] * 3 + [row_spec(SSM_W), row_spec(SSM_W), row_spec(LANES)],
                  out_shape=[_sds((HEADS, s, HEAD_DIM), BF16)] * 3 + [_sds((s, SSM_W), BF16), _sds((s, SSM_W)), _sds((s, LANES))],
                  scratch_shapes=[pltpu.VMEM((1, LANES), F32)], dims=("arbitrary",))(z, gq, gk, bf, gg)


def _attn_fwd(qh, kh, vh, crow, hosted=None):
    _, s, _ = qh.shape
    tq = _row_tile(s)
    scale = HEAD_DIM ** -0.5

    hp = HEADS_PER_STEP
    nq = s // tq
    fold = lambda t, op: op(t[:, :tq // 2], t[:, tq // 2:])

    def body(q_ref, k_ref, v_ref, c_ref, o_ref, lse_ref, s_s):
        i = pl.program_id(1)

        def first(j, ms, diagonal):
            off = pl.multiple_of(j * tq, tq)
            out = []
            for hh in range(hp):
                sc = _dot_nt(q_ref[hh], k_ref[hh, pl.ds(off, tq), :]) * scale - c_ref[hh, :, pl.ds(off, tq)]
                if diagonal:
                    causal = lax.broadcasted_iota(jnp.int32, (tq, tq), 1) <= lax.broadcasted_iota(jnp.int32, (tq, tq), 0)
                    sc = jnp.where(causal, sc, NEG_INF)
                s_s[hh, j] = sc
                out.append(jnp.maximum(ms[hh], fold(sc, jnp.maximum)))
            return tuple(out)

        ms = lax.fori_loop(0, i, lambda j, c: first(j, c, False), (jnp.full((tq, tq // 2), NEG_INF, F32),) * hp)
        ms = [jnp.max(t, axis=-1, keepdims=True) for t in first(i, ms, True)]

        def second(j, carry):
            rows = pl.ds(pl.multiple_of(j * tq, tq), tq)
            out = []
            for hh in range(hp):
                ls, acc = carry[hh]
                p = jnp.exp(s_s[hh, j] - ms[hh])
                out.append((ls + fold(p, jnp.add), acc + _dot(p.astype(BF16), v_ref[hh, rows, :])))
            return tuple(out)

        zero = (jnp.zeros((tq, tq // 2), F32), jnp.zeros((tq, HEAD_DIM), F32))
        for hh, (ls, acc) in enumerate(lax.fori_loop(0, i + 1, second, (zero,) * hp)):
            l = jnp.sum(ls, axis=-1, keepdims=True)
            o_ref[hh] = acc / l
            lse_ref[hh] = ms[hh] + jnp.log(l)

    blk = pl.BlockSpec((hp, tq, HEAD_DIM), lambda h, i: (h, i, 0))
    full = pl.BlockSpec((hp, s, HEAD_DIM), lambda h, i: (h, 0, 0))
    nh = HEADS // hp
    first = lambda: jnp.logical_and(pl.program_id(0) == 0, pl.program_id(1) == 0)
    last = lambda: jnp.logical_and(pl.program_id(0) == nh - 1, pl.program_id(1) == nq - 1)
    return _host_pcall(body, hosted, first, last, n_in=4, n_out=2, n_scratch=1, name="attn_fwd", grid=(nh, nq),
                       in_specs=[blk, full, full, pl.BlockSpec((hp, 1, s), lambda h, i: (h, 0, 0))],
                       out_specs=[blk, pl.BlockSpec((hp, tq, 1), lambda h, i: (h, i, 0))],
                       out_shape=[_sds((HEADS, s, HEAD_DIM)), _sds((HEADS, s, 1))],
                       scratch_shapes=[pltpu.VMEM((hp, nq, tq, tq), F32)],
                       dims=("parallel", "parallel"), operands=(qh, kh, vh, crow))


def _ssm_param_fn(lr, li, ls, br, bi):
    step = jnp.exp(ls)
    er = jnp.exp(lr * step)
    ab_re = er * jnp.cos(li * step)
    ab_im = er * jnp.sin(li * step)
    num_re = ab_re - 1.0
    num_im = ab_im
    den = lr * lr + li * li
    f_re = (num_re * lr + num_im * li) / den
    f_im = (num_im * lr - num_re * li) / den
    bb_re = f_re * br - f_im * bi
    bb_im = f_re * bi + f_im * br
    return ab_re, ab_im, bb_re, bb_im


_PARAM_SHAPE = (SSM_GROUPS * SSM_GROUP, SSM_STATE)


def _ssm_params(lr, li, ls, br, bi):
    def body(lr_ref, li_ref, ls_ref, br_ref, bi_ref, ar_ref, ai_ref, bbr_ref, bbi_ref):
        ar, ai, bbr, bbi = _ssm_param_fn(lr_ref[...], li_ref[...], ls_ref[...], br_ref[...], bi_ref[...])
        ar_ref[...] = ar
        ai_ref[...] = ai
        bbr_ref[...] = bbr
        bbi_ref[...] = bbi

    spec = pl.BlockSpec(_PARAM_SHAPE, lambda: (0, 0))
    return _pcall(body, name="ssm_params", in_specs=[spec] * 5, out_specs=[spec] * 4,
                  out_shape=[_sds(_PARAM_SHAPE)] * 4)(lr, li, ls, br, bi)


def _ssm_params_bwd(lr, li, ls, br, bi, dar, dai, dbbr, dbbi, expand_t):
    def body(lr_ref, li_ref, ls_ref, br_ref, bi_ref, dar_ref, dai_ref, dbbr_ref, dbbi_ref, et_ref,
             dlr_ref, dli_ref, dls_ref, dbr_ref, dbi_ref):
        _, vjp = jax.vjp(_ssm_param_fn, lr_ref[...], li_ref[...], ls_ref[...], br_ref[...], bi_ref[...])
        dlr, dli, dls, dbr, dbi = vjp((dar_ref[...], dai_ref[...], dbbr_ref[...], dbbi_ref[...]))
        et = et_ref[...]
        dlr_ref[...] = _dot_exact_l(et, dlr)
        dli_ref[...] = _dot_exact_l(et, dli)
        dls_ref[...] = jnp.sum(_dot_exact_l(et, dls), axis=-1, keepdims=True)
        dbr_ref[...] = dbr
        dbi_ref[...] = dbi

    spec = pl.BlockSpec(_PARAM_SHAPE, lambda: (0, 0))
    gspec = pl.BlockSpec((SSM_GROUPS, SSM_STATE), lambda: (0, 0))
    return _pcall(body, name="ssm_params_bwd",
                  in_specs=[spec] * 9 + [pl.BlockSpec((SSM_GROUPS, _PARAM_SHAPE[0]), lambda: (0, 0))],
                  out_specs=[gspec, gspec, pl.BlockSpec((SSM_GROUPS, 1), lambda: (0, 0)), spec, spec],
                  out_shape=[_sds((SSM_GROUPS, SSM_STATE))] * 2 + [_sds((SSM_GROUPS, 1))] + [_sds(_PARAM_SHAPE)] * 2,
                  )(lr, li, ls, br, bi, dar, dai, dbbr, dbbi, expand_t)


def _cmul(ar, ai, br, bi):
    return ar * br - ai * bi, ar * bi + ai * br


def _scan_consts(ar, ai, width, reverse):
    row = lax.broadcasted_iota(jnp.int32, (SUBLANES, width), 0)
    pw = [(ar, ai)]
    for _ in range(SUBLANES - 1):
        pw.append(_cmul(pw[-1][0], pw[-1][1], ar, ai))
    steps = []
    for d in (1, 2, 4):
        keep = (row < SUBLANES - d) if reverse else (row >= d)
        steps.append((d, jnp.where(keep, pw[d - 1][0], 0.0), jnp.where(keep, pw[d - 1][1], 0.0)))
    pr = jnp.zeros((SUBLANES, width), F32)
    pi = jnp.zeros((SUBLANES, width), F32)
    for r in range(SUBLANES):
        e = (SUBLANES - r) if reverse else (r + 1)
        pr = jnp.where(row == r, pw[e - 1][0], pr)
        pi = jnp.where(row == r, pw[e - 1][1], pi)
    return steps, pr, pi


def _scan_tile(xr, xi, cr, ci, consts, reverse):
    steps, pr, pi = consts
    for d, mr, mi in steps:
        sh = (SUBLANES - d) if reverse else d
        sr = pltpu.roll(xr, sh, 0)
        si = pltpu.roll(xi, sh, 0)
        xr, xi = xr + mr * sr - mi * si, xi + mr * si + mi * sr
    return xr + pr * cr - pi * ci, xi + pr * ci + pi * cr


def _ssm_fwd(ub, uf, bbr, bbi, ar, ai, ccr, cci, dsk, hosted=None):
    s = ub.shape[0]
    tm = _row_tile(s)
    nt = tm // SUBLANES

    def body(ub_ref, u_ref, bbr_ref, bbi_ref, ar_ref, ai_ref, ccr_ref, cci_ref, dsk_ref,
             xr_ref, xi_ref, y_ref, cr_s, ci_s):
        i = pl.program_id(1)

        @pl.when(i == 0)
        def _():
            cr_s[...] = jnp.zeros_like(cr_s)
            ci_s[...] = jnp.zeros_like(ci_s)

        u_b = ub_ref[...]
        xr_ref[...] = _dot(u_b, bbr_ref[0])
        xi_ref[...] = _dot(u_b, bbi_ref[0])
        consts = _scan_consts(ar_ref[0], ai_ref[0], CHUNK_S, False)

        def tile(k, carry):
            cr, ci = carry
            sl = pl.ds(pl.multiple_of(k * SUBLANES, SUBLANES), SUBLANES)
            xr, xi = _scan_tile(xr_ref[sl, :], xi_ref[sl, :], cr, ci, consts, False)
            xr_ref[sl, :] = xr
            xi_ref[sl, :] = xi
            return xr[SUBLANES - 1:SUBLANES, :], xi[SUBLANES - 1:SUBLANES, :]

        cr, ci = lax.fori_loop(0, nt, tile, (cr_s[...], ci_s[...]))
        cr_s[...] = cr
        ci_s[...] = ci
        y_ref[...] = (_dot(xr_ref[...].astype(BF16), ccr_ref[0]) - _dot(xi_ref[...].astype(BF16), cci_ref[0])
                      + dsk_ref[...] * u_ref[...])

    wspec = lambda a, b: pl.BlockSpec((1, a, b), lambda j, i: (j, 0, 0))
    nb = s // tm
    first = lambda: jnp.logical_and(pl.program_id(0) == 0, pl.program_id(1) == 0)
    last = lambda: jnp.logical_and(pl.program_id(0) == SSM_CHUNKS - 1, pl.program_id(1) == nb - 1)
    return _host_pcall(
        body, hosted, first, last, n_in=9, n_out=3, n_scratch=2, name="ssm_fwd", grid=(SSM_CHUNKS, nb),
        in_specs=[pl.BlockSpec((tm, CHUNK_U), lambda j, i: (i, j)),
                  pl.BlockSpec((tm, CHUNK_U), lambda j, i: (i, j)),
                  wspec(CHUNK_U, CHUNK_S), wspec(CHUNK_U, CHUNK_S), wspec(1, CHUNK_S), wspec(1, CHUNK_S),
                  wspec(CHUNK_S, CHUNK_U), wspec(CHUNK_S, CHUNK_U),
                  pl.BlockSpec((1, CHUNK_U), lambda j, i: (0, j))],
        out_specs=[pl.BlockSpec((tm, CHUNK_S), lambda j, i: (i, j)), pl.BlockSpec((tm, CHUNK_S), lambda j, i: (i, j)),
                   pl.BlockSpec((tm, CHUNK_U), lambda j, i: (i, j))],
        out_shape=[_sds((s, N_STATE)), _sds((s, N_STATE)), _sds((s, SSM_W))],
        scratch_shapes=[pltpu.VMEM((1, CHUNK_S), F32)] * 2,
        dims=("parallel", "arbitrary"), operands=(ub, uf, bbr, bbi, ar, ai, ccr, cci, dsk))


def _ssm_glu(y, w_glu, b_glu):
    ge = _gelu(y)
    sg = _sigmoid(_dot(ge.astype(BF16), w_glu) + b_glu)
    return ge, sg


def _mix_out(y, att, x, w_glu, b_glu, g_att, g_ssm, w_out, g_ffn, hosted=None):
    s = x.shape[0]
    tm = _row_tile(s)

    def body(y_ref, att_ref, x_ref, wg_ref, bg_ref, ga_ref, gs_ref, wo_ref, gf_ref, x1_ref, mix_ref, h2_ref):
        ge, sg = _ssm_glu(y_ref[...], wg_ref[...], bg_ref[...])
        ms = _rms(ge * sg, gs_ref[...]).astype(BF16)
        ma = _rms(_merge_heads(att_ref), ga_ref[...]).astype(BF16)
        mix_ref[:, 0:ATTN_W] = ma
        mix_ref[:, ATTN_W:D_MODEL] = ms
        x1 = x_ref[...] + (_dot(ma, wo_ref[0:ATTN_W, :]) + _dot(ms, wo_ref[ATTN_W:D_MODEL, :]))
        x1_ref[...] = x1
        h2_ref[...] = _rms(x1, gf_ref[...]).astype(BF16)

    row = lambda w: pl.BlockSpec((tm, w), lambda i: (i, 0))
    const = lambda a, b: pl.BlockSpec((a, b), lambda i: (0, 0))
    nb = s // tm
    return _host_pcall(body, hosted, lambda: pl.program_id(0) == 0, lambda: pl.program_id(0) == nb - 1,
                       n_in=9, n_out=3, n_scratch=0, name="mix_out", grid=(nb,),
                       in_specs=[row(SSM_W), pl.BlockSpec((HEADS, tm, HEAD_DIM), lambda i: (0, i, 0)), row(D_MODEL),
                                 const(SSM_W, SSM_W), const(1, SSM_W),
                                 const(1, ATTN_W), const(1, SSM_W), const(D_MODEL, D_MODEL), const(1, D_MODEL)],
                       out_specs=[row(D_MODEL)] * 3,
                       out_shape=[_sds((s, D_MODEL)), _sds((s, D_MODEL), BF16), _sds((s, D_MODEL), BF16)],
                       scratch_shapes=[], dims=("parallel",), operands=(y, att, x, w_glu, b_glu, g_att, g_ssm, w_out, g_ffn))


CONV_CHUNK = 64


def _conv_rows(pad_ref, w, b, r0, n):
    y = b + pad_ref[pl.ds(r0 + SUBLANES - 2, n), :] * w[0:1, :]
    y = y + pad_ref[pl.ds(r0 + SUBLANES - 1, n), :] * w[1:2, :]
    return y + pad_ref[pl.ds(r0 + SUBLANES, n), :] * w[2:3, :]


def _fill_front_pad(pad_ref, strip_ref, s):
    pad_ref[0:SUBLANES, :] = jnp.zeros((SUBLANES, STRIP), F32)
    for r0 in range(0, s, CONV_CHUNK):
        pad_ref[pl.ds(SUBLANES + r0, CONV_CHUNK), :] = strip_ref[pl.ds(r0, CONV_CHUNK), :]


def _conv_act(up, conv_w, conv_b):
    s = up.shape[0]

    def body(ug_ref, uv_ref, wg_ref, wv_ref, bg_ref, bv_ref, act_ref, pg_ref, pv_ref):
        _fill_front_pad(pg_ref, ug_ref, s)
        _fill_front_pad(pv_ref, uv_ref, s)
        wg, wv, bg, bv = wg_ref[...], wv_ref[...], bg_ref[...], bv_ref[...]
        for r0 in range(0, s, CONV_CHUNK):
            hg = _conv_rows(pg_ref, wg, bg, r0, CONV_CHUNK)
            hv = _conv_rows(pv_ref, wv, bv, r0, CONV_CHUNK)
            act_ref[pl.ds(r0, CONV_CHUNK), :] = (hg * _sigmoid(hg) * hv).astype(BF16)

    strip = lambda off: pl.BlockSpec((s, STRIP), lambda j: (0, j + off))
    wsp = lambda off: pl.BlockSpec((3, STRIP), lambda j: (0, j + off))
    bsp = lambda off: pl.BlockSpec((1, STRIP), lambda j: (0, j + off))
    return _pcall(body, name="conv_act", grid=(N_STRIPS,),
                  in_specs=[strip(0), strip(N_STRIPS), wsp(0), wsp(N_STRIPS), bsp(0), bsp(N_STRIPS)],
                  out_specs=pl.BlockSpec((s, STRIP), lambda j: (0, j)), out_shape=_sds((s, D_FF), BF16),
                  scratch_shapes=[pltpu.VMEM((s + SUBLANES, STRIP), F32)] * 2,
                  dims=("parallel",))(up, up, conv_w, conv_w, conv_b, conv_b)


def _down_loss(act, w_down, x1, tgt):
    s = x1.shape[0]
    tm = _row_tile(s)

    def body(a_ref, w_ref, x1_ref, t_ref, dy_ref, dyb_ref, loss_ref):
        i = pl.program_id(0)

        @pl.when(i == 0)
        def _():
            loss_ref[...] = jnp.zeros_like(loss_ref)

        diff = x1_ref[...] + _dot(a_ref[...], w_ref[...]) - t_ref[...]
        dy = diff * (1.0 / D_MODEL)
        dy_ref[...] = dy
        dyb_ref[...] = dy.astype(BF16)
        loss_ref[...] += 0.5 * jnp.sum(diff * dy)

    row = lambda w: pl.BlockSpec((tm, w), lambda i: (i, 0))
    return _pcall(body, name="down_loss", grid=(s // tm,),
                  in_specs=[row(D_FF), pl.BlockSpec((D_FF, D_MODEL), lambda i: (0, 0)), row(D_MODEL), row(D_MODEL)],
                  out_specs=[row(D_MODEL), row(D_MODEL), pl.BlockSpec((SUBLANES, LANES), lambda i: (0, 0))],
                  out_shape=[_sds((s, D_MODEL)), _sds((s, D_MODEL), BF16), _sds((SUBLANES, LANES))],
                  dims=("arbitrary",))(act, w_down, x1, tgt)


def _conv_act_bwd(up, dact, conv_w, conv_b):
    s = up.shape[0]
    ch = CONV_CHUNK

    def body(ug_ref, uv_ref, da_ref, wg_ref, wv_ref, bg_ref, bv_ref, dup_ref, dcw_ref, pg_ref, pv_ref, dg_ref, dv_ref):
        _fill_front_pad(pg_ref, ug_ref, s)
        _fill_front_pad(pv_ref, uv_ref, s)
        zero = jnp.zeros((SUBLANES, STRIP), F32)
        dg_ref[pl.ds(s, SUBLANES), :] = zero
        dv_ref[pl.ds(s, SUBLANES), :] = zero
        wg, wv, bg, bv = wg_ref[...], wv_ref[...], bg_ref[...], bv_ref[...]
        tile_sum = lambda t: jnp.sum(t.reshape(ch // SUBLANES, SUBLANES, STRIP), axis=0)
        accs = [[zero] * 4, [zero] * 4]
        for r0 in range(0, s, ch):
            hg = _conv_rows(pg_ref, wg, bg, r0, ch)
            hv = _conv_rows(pv_ref, wv, bv, r0, ch)
            sg = _sigmoid(hg)
            da = da_ref[pl.ds(r0, ch), :]
            dhs = (da * hv * (sg * (1.0 + hg * (1.0 - sg))), da * (hg * sg))
            for half, (dh, d_ref, p_ref) in enumerate(zip(dhs, (dg_ref, dv_ref), (pg_ref, pv_ref))):
                d_ref[pl.ds(r0, ch), :] = dh
                for k in range(3):
                    accs[half][k] = accs[half][k] + tile_sum(dh * p_ref[pl.ds(r0 + SUBLANES - 2 + k, ch), :])
                accs[half][3] = accs[half][3] + tile_sum(dh)
        for half, (d_ref, w) in enumerate(((dg_ref, wg), (dv_ref, wv))):
            for r0 in range(0, s, ch):
                dup = (d_ref[pl.ds(r0, ch), :] * w[2:3, :] + d_ref[pl.ds(r0 + 1, ch), :] * w[1:2, :]
                       + d_ref[pl.ds(r0 + 2, ch), :] * w[0:1, :])
                dup_ref[half, pl.ds(r0, ch), :] = dup.astype(BF16)
            rid = lax.broadcasted_iota(jnp.int32, (SUBLANES, STRIP), 0)
            out = zero
            for k in range(4):
                out = jnp.where(rid == k, jnp.sum(accs[half][k], axis=0, keepdims=True), out)
            dcw_ref[half] = out

    strip = lambda off: pl.BlockSpec((s, STRIP), lambda j: (0, j + off))
    wsp = lambda off: pl.BlockSpec((3, STRIP), lambda j: (0, j + off))
    bsp = lambda off: pl.BlockSpec((1, STRIP), lambda j: (0, j + off))
    return _pcall(body, name="conv_act_bwd", grid=(N_STRIPS,),
                  in_specs=[strip(0), strip(N_STRIPS), strip(0), wsp(0), wsp(N_STRIPS), bsp(0), bsp(N_STRIPS)],
                  out_specs=[pl.BlockSpec((2, s, STRIP), lambda j: (0, 0, j)), pl.BlockSpec((2, SUBLANES, STRIP), lambda j: (0, 0, j))],
                  out_shape=[_sds((2, s, D_FF), BF16), _sds((2, SUBLANES, D_FF))],
                  scratch_shapes=[pltpu.VMEM((s + SUBLANES, STRIP), F32)] * 4,
                  dims=("parallel",))(up, up, dact, conv_w, conv_w, conv_b, conv_b)


def _mix_bwd(dy, dh2, x1, g_ffn, w_out, y, att, w_glu, b_glu, g_att, g_ssm):
    s = dy.shape[0]
    tm = _row_tile(s)

    def body(dy_ref, dh2_ref, x1_ref, gf_ref, wo_ref, y_ref, att_ref, wg_ref, bg_ref, ga_ref, gs_ref,
             dx1_ref, dx1b_ref, datt_ref, dys_ref, dwg_ref, dgf_ref, dga_ref, dgs_ref, dbg_ref):
        i = pl.program_id(0)

        @pl.when(i == 0)
        def _():
            for r in (dwg_ref, dgf_ref, dga_ref, dgs_ref, dbg_ref):
                r[...] = jnp.zeros_like(r)

        dxn, dgf = _rms_bwd(x1_ref[...], gf_ref[...], dh2_ref[...])
        dx1 = dy_ref[...] + dxn
        dx1_ref[...] = dx1
        dx1b = dx1.astype(BF16)
        dx1b_ref[...] = dx1b
        dgf_ref[...] += dgf
        dma = _dot_nt(dx1b, wo_ref[0:ATTN_W, :])
        dms = _dot_nt(dx1b, wo_ref[ATTN_W:D_MODEL, :])
        datt, dga = _rms_bwd(_merge_heads(att_ref), ga_ref[...], dma)
        _split_heads(datt_ref, datt)
        dga_ref[...] += dga
        yv = y_ref[...]
        ge, sg = _ssm_glu(yv, wg_ref[...], bg_ref[...])
        dssm, dgs = _rms_bwd(ge * sg, gs_ref[...], dms)
        dgs_ref[...] += dgs
        dgl = dssm * ge * sg * (1.0 - sg)
        dglb = dgl.astype(BF16)
        dge = dssm * sg + _dot_nt(dglb, wg_ref[...])
        dbg_ref[...] += jnp.sum(dgl, axis=0, keepdims=True)
        dwg_ref[...] += _dot_tn(ge.astype(BF16), dglb)
        dys_ref[...] = dge * _gelu_grad(yv)

    row = lambda w: pl.BlockSpec((tm, w), lambda i: (i, 0))
    const = lambda a, b: pl.BlockSpec((a, b), lambda i: (0, 0))
    heads = pl.BlockSpec((HEADS, tm, HEAD_DIM), lambda i: (0, i, 0))
    return _pcall(body, name="mix_bwd", grid=(s // tm,),
                  in_specs=[row(D_MODEL), row(D_MODEL), row(D_MODEL), const(1, D_MODEL), const(D_MODEL, D_MODEL), row(SSM_W),
                            heads, const(SSM_W, SSM_W), const(1, SSM_W), const(1, ATTN_W), const(1, SSM_W)],
                  out_specs=[row(D_MODEL), row(D_MODEL), heads, row(SSM_W), const(SSM_W, SSM_W), const(1, D_MODEL),
                             const(1, ATTN_W), const(1, SSM_W), const(1, SSM_W)],
                  out_shape=[_sds((s, D_MODEL)), _sds((s, D_MODEL), BF16), _sds((HEADS, s, HEAD_DIM)), _sds((s, SSM_W)),
                             _sds((SSM_W, SSM_W)), _sds((1, D_MODEL)), _sds((1, ATTN_W)), _sds((1, SSM_W)), _sds((1, SSM_W))],
                  dims=("arbitrary",))(dy, dh2, x1, g_ffn, w_out, y, att, w_glu, b_glu, g_att, g_ssm)


def _ssm_bwd(dys, uf, ub, xr, xi, bbr, bbi, ar, ai, ccr, cci, dsk, hosted=None):
    s = dys.shape[0]
    tm = _row_tile(s)
    nb = s // tm
    nt = tm // SUBLANES

    def body(dy_ref, u_ref, ub_ref, xr_ref, xi_ref, xrp_ref, xip_ref, bbr_ref, bbi_ref, ar_ref, ai_ref, ccr_ref,
             cci_ref, dsk_ref, du_ref, dbbr_ref, dbbi_ref, dccr_ref, dcci_ref, dar_ref, dai_ref, dd_ref,
             gr_s, gi_s, cr_s, ci_s, accr_s, acci_s):
        i = pl.program_id(1)
        first_block = i == nb - 1

        @pl.when(i == 0)
        def _():
            for r in (cr_s, ci_s, accr_s, acci_s, dbbr_ref, dbbi_ref, dccr_ref, dcci_ref, dd_ref):
                r[...] = jnp.zeros_like(r)

        dy = dy_ref[...]
        dyb = dy.astype(BF16)
        gr_s[...] = _dot_nt(dyb, ccr_ref[0])
        gi_s[...] = -_dot_nt(dyb, cci_ref[0])
        consts = _scan_consts(ar_ref[0], -ai_ref[0], CHUNK_S, True)
        row = lax.broadcasted_iota(jnp.int32, (SUBLANES, CHUNK_S), 0)

        def tile(kk, carry):
            cr, ci, accr, acci = carry
            k = nt - 1 - kk
            sl = pl.ds(pl.multiple_of(k * SUBLANES, SUBLANES), SUBLANES)
            gr, gi = _scan_tile(gr_s[sl, :], gi_s[sl, :], cr, ci, consts, True)
            gr_s[sl, :] = gr
            gi_s[sl, :] = gi
            slp = pl.ds(pl.multiple_of(jnp.maximum(k - 1, 0) * SUBLANES, SUBLANES), SUBLANES)
            inner = k > 0
            pr_t = jnp.where(inner, xr_ref[slp, :], xrp_ref[...])
            pi_t = jnp.where(inner, xi_ref[slp, :], xip_ref[...])
            live = jnp.logical_or(inner, jnp.logical_not(first_block))
            top_r = jnp.where(live, pltpu.roll(pr_t, 1, 0), 0.0)
            top_i = jnp.where(live, pltpu.roll(pi_t, 1, 0), 0.0)
            xpr = jnp.where(row == 0, top_r, pltpu.roll(xr_ref[sl, :], 1, 0))
            xpi = jnp.where(row == 0, top_i, pltpu.roll(xi_ref[sl, :], 1, 0))
            accr = accr + gr * xpr + gi * xpi
            acci = acci + gi * xpr - gr * xpi
            return gr[0:1, :], gi[0:1, :], accr, acci

        zeros = jnp.zeros((SUBLANES, CHUNK_S), F32)
        cr, ci, accr, acci = lax.fori_loop(0, nt, tile, (cr_s[...], ci_s[...], zeros, zeros))
        cr_s[...] = cr
        ci_s[...] = ci
        accr_s[...] += accr
        acci_s[...] += acci
        grb = gr_s[...].astype(BF16)
        gib = gi_s[...].astype(BF16)
        u_b = ub_ref[...]
        du_ref[...] = _dot_nt(grb, bbr_ref[0]) + _dot_nt(gib, bbi_ref[0]) + dsk_ref[...] * dy
        dbbr_ref[0] += _dot_tn(u_b, grb)
        dbbi_ref[0] += _dot_tn(u_b, gib)
        dccr_ref[0] += _dot_tn(xr_ref[...].astype(BF16), dyb)
        dcci_ref[0] -= _dot_tn(xi_ref[...].astype(BF16), dyb)
        dd_ref[...] += jnp.sum(dy * u_ref[...], axis=0, keepdims=True)

        @pl.when(i == nb - 1)
        def _():
            dar_ref[0] = jnp.sum(accr_s[...], axis=0, keepdims=True)
            dai_ref[0] = jnp.sum(acci_s[...], axis=0, keepdims=True)

    tiles_per_block = tm // SUBLANES
    rb = lambda i: nb - 1 - i
    wspec = lambda a, b: pl.BlockSpec((1, a, b), lambda j, i: (j, 0, 0))
    xblk = pl.BlockSpec((tm, CHUNK_S), lambda j, i: (rb(i), j))
    xprev = pl.BlockSpec((SUBLANES, CHUNK_S), lambda j, i: (jnp.maximum(rb(i) * tiles_per_block - 1, 0), j))
    ublk = pl.BlockSpec((tm, CHUNK_U), lambda j, i: (rb(i), j))
    first = lambda: jnp.logical_and(pl.program_id(0) == 0, pl.program_id(1) == 0)
    last = lambda: jnp.logical_and(pl.program_id(0) == SSM_CHUNKS - 1, pl.program_id(1) == nb - 1)
    return _host_pcall(
        body, hosted, first, last, n_in=14, n_out=8, n_scratch=6, name="ssm_bwd", grid=(SSM_CHUNKS, nb),
        in_specs=[ublk, ublk, ublk, xblk, xblk, xprev, xprev,
                  wspec(CHUNK_U, CHUNK_S), wspec(CHUNK_U, CHUNK_S), wspec(1, CHUNK_S), wspec(1, CHUNK_S),
                  wspec(CHUNK_S, CHUNK_U), wspec(CHUNK_S, CHUNK_U), pl.BlockSpec((1, CHUNK_U), lambda j, i: (0, j))],
        out_specs=[ublk, wspec(CHUNK_U, CHUNK_S), wspec(CHUNK_U, CHUNK_S), wspec(CHUNK_S, CHUNK_U),
                   wspec(CHUNK_S, CHUNK_U), wspec(1, CHUNK_S), wspec(1, CHUNK_S),
                   pl.BlockSpec((1, CHUNK_U), lambda j, i: (0, j))],
        out_shape=[_sds((s, SSM_W)), _sds((SSM_CHUNKS, CHUNK_U, CHUNK_S)), _sds((SSM_CHUNKS, CHUNK_U, CHUNK_S)),
                   _sds((SSM_CHUNKS, CHUNK_S, CHUNK_U)), _sds((SSM_CHUNKS, CHUNK_S, CHUNK_U)),
                   _sds((SSM_CHUNKS, 1, CHUNK_S)), _sds((SSM_CHUNKS, 1, CHUNK_S)), _sds((1, SSM_W))],
        scratch_shapes=[pltpu.VMEM((tm, CHUNK_S), F32)] * 2 + [pltpu.VMEM((1, CHUNK_S), F32)] * 2
                       + [pltpu.VMEM((SUBLANES, CHUNK_S), F32)] * 2,
        dims=("parallel", "arbitrary"), operands=(dys, uf, ub, xr, xi, xr, xi, bbr, bbi, ar, ai, ccr, cci, dsk))


def _attn_probs(q, ks, cs, lse, scale, diagonal):
    p = jnp.exp(_dot_nt(q, ks) * scale - cs - lse)
    if diagonal:
        tq, tk = p.shape
        causal = lax.broadcasted_iota(jnp.int32, (tq, tk), 1) <= lax.broadcasted_iota(jnp.int32, (tq, tk), 0)
        p = jnp.where(causal, p, 0.0)
    return p


def _attn_bwd(qh, kh, vh, crow, lse, doh, hosted=None):
    _, s, _ = qh.shape
    tq = _row_tile(s)
    nq = s // tq
    scale = HEAD_DIM ** -0.5
    hp = HEADS_PER_STEP

    def body(q_ref, k_ref, v_ref, c_ref, lse_ref, do_ref, dq_ref, dk_ref, dv_ref, dc_ref, p_s, dp_s):
        i = pl.program_id(1)

        @pl.when(i == 0)
        def _():
            for r in (dk_ref, dv_ref, dc_ref):
                r[...] = jnp.zeros_like(r)

        dobs = [do_ref[hh].astype(BF16) for hh in range(hp)]

        def first(j, dls, diagonal):
            off = pl.multiple_of(j * tq, tq)
            out = []
            for hh in range(hp):
                p = _attn_probs(q_ref[hh], k_ref[hh, pl.ds(off, tq), :], c_ref[hh, :, pl.ds(off, tq)], lse_ref[hh],
                                scale, diagonal)
                dp = _dot_nt(dobs[hh], v_ref[hh, pl.ds(off, tq), :])
                p_s[hh, j] = p
                dp_s[hh, j] = dp
                out.append(dls[hh] + jnp.sum(p * dp, axis=-1, keepdims=True))
            return tuple(out)

        zero_col = jnp.zeros((tq, 1), F32)
        dls = lax.fori_loop(0, i, lambda j, c: first(j, c, False), (zero_col,) * hp)
        dls = first(i, dls, True)

        def second(j, dqs):
            rows = pl.ds(pl.multiple_of(j * tq, tq), tq)
            out = []
            for hh in range(hp):
                p = p_s[hh, j]
                ds = p * (dp_s[hh, j] - dls[hh])
                dsb = ds.astype(BF16)
                dv_ref[hh, rows, :] += _dot_tn(p.astype(BF16), dobs[hh])
                dk_ref[hh, rows, :] += _dot_tn(dsb, q_ref[hh]) * scale
                dc_ref[hh, :, rows] -= jnp.sum(ds, axis=0, keepdims=True)
                out.append(dqs[hh] + _dot(dsb, k_ref[hh, rows, :]))
            return tuple(out)

        dqs = lax.fori_loop(0, i + 1, second, (jnp.zeros((tq, HEAD_DIM), F32),) * hp)
        for hh in range(hp):
            dq_ref[hh] = dqs[hh] * scale

    blk = pl.BlockSpec((hp, tq, HEAD_DIM), lambda h, i: (h, i, 0))
    full = pl.BlockSpec((hp, s, HEAD_DIM), lambda h, i: (h, 0, 0))
    crow_spec = pl.BlockSpec((hp, 1, s), lambda h, i: (h, 0, 0))
    nh = HEADS // hp
    first = lambda: jnp.logical_and(pl.program_id(0) == 0, pl.program_id(1) == 0)
    last = lambda: jnp.logical_and(pl.program_id(0) == nh - 1, pl.program_id(1) == nq - 1)
    return _host_pcall(body, hosted, first, last, n_in=6, n_out=4, n_scratch=2, name="attn_bwd", grid=(nh, nq),
                       in_specs=[blk, full, full, crow_spec, pl.BlockSpec((hp, tq, 1), lambda h, i: (h, i, 0)), blk],
                       out_specs=[blk, full, full, crow_spec],
                       out_shape=[_sds((HEADS, s, HEAD_DIM))] * 3 + [_sds((HEADS, 1, s))],
                       scratch_shapes=[pltpu.VMEM((hp, nq, tq, tq), F32)] * 2,
                       dims=("parallel", "arbitrary"), operands=(qh, kh, vh, crow, lse, doh))


def _prep_bwd(z, dqn, dkn, dv, du, dc, gq, gk, bf, gg):
    s = z.shape[0]
    tm = _row_tile(s)
    nb = s // tm

    def body(z_ref, dqn_ref, dkn_ref, dv_ref, du_ref, dc_ref, gq_ref, gk_ref, bf_ref, gg_ref,
             dz_ref, dgq_ref, dgk_ref, dbf_ref, carry_ref):
        i = pl.program_id(0)

        @pl.when(i == 0)
        def _():
            for r in (dgq_ref, dgk_ref, dbf_ref, carry_ref):
                r[...] = jnp.zeros_like(r)

        gg_m = gg_ref[...]

        def head_norm_bwd(t, g, dn):
            r = lax.rsqrt(_dot_exact_r(t * t, gg_m) * (1.0 / HEAD_DIM) + EPS)
            w = dn * g
            mean_wt = _dot_exact_r(w * t, gg_m) * (1.0 / HEAD_DIM)
            return r * w - t * (r * r * r) * mean_wt, jnp.sum(dn * t * r, axis=0, keepdims=True)

        dq, dgq = head_norm_bwd(z_ref[:, 0:ATTN_W], gq_ref[...], _merge_heads(dqn_ref))
        dk, dgk = head_norm_bwd(z_ref[:, ATTN_W:2 * ATTN_W], gk_ref[...], _merge_heads(dkn_ref))
        dgq_ref[...] += dgq
        dgk_ref[...] += dgk
        row = lax.broadcasted_iota(jnp.int32, (tm, tm), 0)
        col = lax.broadcasted_iota(jnp.int32, (tm, tm), 1)
        triu = (col >= row).astype(BF16)
        dlf = _dot_exact_l(triu, dc_ref[...]) + carry_ref[...]
        carry_ref[...] = dlf[0:1, :]
        df = dlf * _sigmoid(-_forget_logits(z_ref, bf_ref))
        dbf_ref[...] += jnp.sum(df, axis=0, keepdims=True)
        dz_ref[:, 0:ATTN_W] = dq.astype(BF16)
        dz_ref[:, ATTN_W:2 * ATTN_W] = dk.astype(BF16)
        dz_ref[:, 2 * ATTN_W:3 * ATTN_W] = _merge_heads(dv_ref).astype(BF16)
        tail = jnp.concatenate([df[:, :HEADS], du_ref[...], jnp.zeros((tm, Z_COLS - IN_COLS), F32)], axis=-1)
        dz_ref[:, F_COL0:Z_COLS] = tail.astype(BF16)

    row_spec = lambda w: pl.BlockSpec((tm, w), lambda i: (nb - 1 - i, 0))
    const = lambda shape: pl.BlockSpec(shape, lambda i: (0, 0))
    return _pcall(body, name="prep_bwd", grid=(nb,),
                  in_specs=[row_spec(Z_COLS)] + [pl.BlockSpec((HEADS, tm, HEAD_DIM), lambda i: (0, nb - 1 - i, 0))] * 3
                           + [row_spec(ATTN_W), row_spec(LANES), const((1, ATTN_W)),
                              const((1, ATTN_W)), const((1, LANES)), const((ATTN_W, ATTN_W))],
                  out_specs=[row_spec(Z_COLS), const((1, ATTN_W)), const((1, ATTN_W)), const((1, LANES))],
                  out_shape=[_sds((s, Z_COLS), BF16), _sds((1, ATTN_W)), _sds((1, ATTN_W)), _sds((1, LANES))],
                  scratch_shapes=[pltpu.VMEM((1, LANES), F32)], dims=("arbitrary",))(z, dqn, dkn, dv, du, dc, gq, gk, bf, gg)


def _in_norm_bwd(x, g_mix, dh, dx1):
    s = x.shape[0]
    tm = _row_tile(s)

    def body(x_ref, g_ref, dh_ref, dx1_ref, dx_ref, dg_ref):
        i = pl.program_id(0)

        @pl.when(i == 0)
        def _():
            dg_ref[...] = jnp.zeros_like(dg_ref)

        dxn, dg = _rms_bwd(x_ref[...], g_ref[...], dh_ref[...])
        dx_ref[...] = dx1_ref[...] + dxn
        dg_ref[...] += dg

    row = pl.BlockSpec((tm, D_MODEL), lambda i: (i, 0))
    vec = pl.BlockSpec((1, D_MODEL), lambda i: (0, 0))
    return _pcall(body, name="in_norm_bwd", grid=(s // tm,), in_specs=[row, vec, row, row], out_specs=[row, vec],
                  out_shape=[_sds((s, D_MODEL)), _sds((1, D_MODEL))], dims=("arbitrary",))(x, g_mix, dh, dx1)


def _adamw_refs(w_ref, g_ref, m_ref, v_ref, d_ref, mo_ref, vo_ref):
    gv = g_ref[...]
    mn = ADAM_B1 * m_ref[...] + (1.0 - ADAM_B1) * gv
    vn = ADAM_B2 * v_ref[...] + (1.0 - ADAM_B2) * (gv * gv)
    m_hat = mn / (1.0 - ADAM_B1 ** ADAM_STEP)
    v_hat = vn / (1.0 - ADAM_B2 ** ADAM_STEP)
    d_ref[...] = -ADAM_LR * (m_hat / (jnp.sqrt(v_hat) + ADAM_EPS) + ADAM_WD * w_ref[...])
    mo_ref[...] = mn
    vo_ref[...] = vn


def _adamw_small(ws, gs, ms, vs):
    n = len(ws)

    def body(*refs):
        ins, outs = refs[:4 * n], refs[4 * n:]
        for i in range(n):
            _adamw_refs(ins[i], ins[n + i], ins[2 * n + i], ins[3 * n + i], *outs[3 * i:3 * i + 3])

    vm = pl.BlockSpec(memory_space=pltpu.VMEM)
    out_shape = [_sds(w.shape) for w in ws for _ in range(3)]
    return _pallas(body, name="adamw_small", in_specs=[vm] * (4 * n), out_specs=[vm] * (3 * n), out_shape=out_shape,
                   compiler_params=pltpu.CompilerParams(vmem_limit_bytes=VMEM_LIMIT))(*ws, *gs, *ms, *vs)


def _adamw(w, g, m, v, *, name):
    r, c = w.shape
    tr = r
    for cand in (256, 176, 128, 64):
        if r > cand and r % cand == 0:
            tr = cand
            break

    def body(w_ref, g_ref, m_ref, v_ref, d_ref, mo_ref, vo_ref):
        _adamw_refs(w_ref, g_ref, m_ref, v_ref, d_ref, mo_ref, vo_ref)

    spec = pl.BlockSpec((tr, c), lambda i: (i, 0))
    return _pcall(body, name=name, grid=(r // tr,), in_specs=[spec] * 4, out_specs=[spec] * 3,
                  out_shape=[_sds((r, c))] * 3, dims=("parallel",))(w, g, m, v)


def _prefetch_call(body, *, name, grid, in_specs, out_specs, out_shape, operands):
    grid_spec = pltpu.PrefetchScalarGridSpec(num_scalar_prefetch=1, grid=grid, in_specs=in_specs, out_specs=out_specs)
    params = pltpu.CompilerParams(dimension_semantics=("parallel",) * len(grid), vmem_limit_bytes=VMEM_LIMIT)
    return _pallas(body, name=name, grid_spec=grid_spec, out_shape=out_shape, compiler_params=params)(*operands)


def _half_rows_tile(hr):
    return hr if hr <= 256 else 176 if hr % 176 == 0 else 256


def _add_half(g, landed, place, *, name):
    def body(place_ref, g_ref, l_ref, o_ref):
        own = g_ref[0] if len(g_ref.shape) == 4 else g_ref[...]
        o_ref[...] = (own + l_ref[...]).astype(BF16)

    if g.ndim == 4:
        _, _, hr, c = g.shape
        tr = _half_rows_tile(hr)
        blk = (1, tr, c)
        return _prefetch_call(
            body, name=name, grid=(N_CHIPS, hr // tr),
            in_specs=[pl.BlockSpec((1,) + blk, lambda j, i, p: (j, p[1], i, 0)), pl.BlockSpec(blk, lambda j, i, p: (j, i, 0))],
            out_specs=pl.BlockSpec(blk, lambda j, i, p: (j, i, 0)), out_shape=_sds(landed.shape, BF16),
            operands=(place, g, landed))
    hr, c = landed.shape
    tr, tc = 256, _tile(c, 2176)
    nb = hr // tr
    return _prefetch_call(
        body, name=name, grid=(nb, c // tc),
        in_specs=[pl.BlockSpec((tr, tc), lambda i, j, p: (p[1] * nb + i, j)), pl.BlockSpec((tr, tc), lambda i, j, p: (i, j))],
        out_specs=pl.BlockSpec((tr, tc), lambda i, j, p: (i, j)), out_shape=_sds(landed.shape, BF16),
        operands=(place, g, landed))


def _sum_chips(chip_sum, lands, place, *, name, tc, window_stride=0):
    _, hr, c = lands.shape
    tr = _half_rows_tile(hr)
    nb = hr // tr
    ncb = c // tc

    def body(place_ref, own_ref, a_ref, b_ref, c_ref, o_ref):
        own = own_ref[0] if len(own_ref.shape) == 3 else own_ref[...]
        o_ref[...] = ((own.astype(F32) + a_ref[0].astype(F32)) + b_ref[0].astype(F32)) + c_ref[0].astype(F32)

    land = lambda k: pl.BlockSpec((1, tr, tc), lambda i, j, p: ((p[0] + k) % N_CHIPS, i, j))
    if chip_sum.ndim == 3:
        own_spec = land(0)
    else:
        stride = window_stride // tc
        own_spec = pl.BlockSpec((tr, tc), lambda i, j, p: (i, p[0] * stride + j))
    return _prefetch_call(
        body, name=name, grid=(nb, ncb), in_specs=[own_spec, land(1), land(2), land(3)],
        out_specs=pl.BlockSpec((tr, tc), lambda i, j, p: (p[1] * nb + i, j)), out_shape=_sds((2 * hr, c)),
        operands=(place, chip_sum, lands, lands, lands))


_HBM = pl.BlockSpec(memory_space=pltpu.HBM)


def _place():
    x, y, c = lax.axis_index("x"), lax.axis_index("y"), lax.axis_index("c")
    chips = [(1 - x, y), (x, 1 - y), (1 - x, 1 - y)]
    return x, y, c, chips


def _rcopy(src, dst, send_sem, recv_sem, to):
    return pltpu.make_async_remote_copy(src_ref=src, dst_ref=dst, send_sem=send_sem, recv_sem=recv_sem,
                                        device_id=to, device_id_type=MESH)


N_BIG = 5
UP_COLS = 2 * D_FF // N_CHIPS
IN_WINDOW = 640
IN_STRIDE = 512


class _Hosted:
    def __init__(self, operands, out_shapes, n_sems, start, finish, aliases=None, local_sems=0):
        self.operands, self.out_shapes, self.n_sems = list(operands), list(out_shapes), n_sems
        self.start, self.finish, self.aliases, self.local_sems = start, finish, dict(aliases or {}), local_sems

    def scratch(self):
        return ([pltpu.SemaphoreType.DMA((self.n_sems,)), pltpu.SemaphoreType.DMA((self.n_sems,))]
                + [pltpu.SemaphoreType.DMA] * self.local_sems)


def _both(a, b):
    na, nao, nas = len(a.operands), len(a.out_shapes), len(a.scratch())

    def start(ins, outs, sems):
        a.start(ins[:na], outs[:nao], sems[:nas])
        b.start(ins[na:], outs[nao:], sems[nas:])

    def finish(ins, outs, sems):
        a.finish(ins[:na], outs[:nao], sems[:nas])
        b.finish(ins[na:], outs[nao:], sems[nas:])

    both = _Hosted(a.operands + b.operands, a.out_shapes + b.out_shapes, 0, start, finish,
                   aliases={**a.aliases, **{na + i: nao + o for i, o in b.aliases.items()}})
    both.scratch = lambda: a.scratch() + b.scratch()
    return both


def _run_hosted(hosted, *, name):
    n_in, n_out = len(hosted.operands), len(hosted.out_shapes)

    def body(*refs):
        parts = (refs[:n_in], refs[n_in:n_in + n_out], refs[n_in + n_out:])
        hosted.start(*parts)
        hosted.finish(*parts)

    return _pallas(body, name=name, in_specs=[_HBM] * n_in, out_specs=[_HBM] * n_out, out_shape=hosted.out_shapes,
                   input_output_aliases=hosted.aliases, scratch_shapes=hosted.scratch())(*hosted.operands)


def _host_pcall(core_body, hosted, first, last, *, n_in, n_out, n_scratch, name, grid, in_specs, out_specs, out_shape,
                scratch_shapes, dims, operands):
    if hosted is None:
        outs = _pcall(core_body, name=name, grid=grid, in_specs=in_specs, out_specs=out_specs, out_shape=out_shape,
                      scratch_shapes=scratch_shapes, dims=dims)(*operands)
        return outs, []
    hi, ho = len(hosted.operands), len(hosted.out_shapes)

    def body(*refs):
        a, b = n_in, n_in + hi
        c, d = b + n_out, b + n_out + ho
        e = d + n_scratch
        parts = (refs[a:b], refs[c:d], refs[e:])

        @pl.when(first())
        def _():
            hosted.start(*parts)

        core_body(*refs[:a], *refs[b:c], *refs[d:e])

        @pl.when(last())
        def _():
            hosted.finish(*parts)

    params = pltpu.CompilerParams(dimension_semantics=("arbitrary",) * len(grid), vmem_limit_bytes=VMEM_LIMIT)
    outs = _pallas(body, name=name, grid=grid, in_specs=list(in_specs) + [_HBM] * hi, out_specs=list(out_specs) + [_HBM] * ho,
                   out_shape=list(out_shape) + hosted.out_shapes, scratch_shapes=list(scratch_shapes) + hosted.scratch(),
                   input_output_aliases={n_in + a: n_out + b for a, b in hosted.aliases.items()},
                   compiler_params=params)(*operands, *hosted.operands)
    return outs[:n_out], outs[n_out:]


def _gather_slot(src, out, chip, hc):
    hr, cols = src.shape[0] // 2, src.shape[1]
    if len(out.shape) == 2:
        return out.at[pl.ds(hc * hr, hr), pl.ds(pl.multiple_of(chip * cols, LANES), cols)]
    return out.at[chip, pl.ds(hc * hr, hr), :]


def _gathered_shape(shard, by_cols):
    if by_cols:
        return _sds((shard.shape[0], N_CHIPS * shard.shape[1]), shard.dtype)
    return _sds((N_CHIPS,) + shard.shape, shard.dtype)


def _plan_gather_ici(shards, by_cols, whole=(), own_cols=()):
    n = len(shards)

    def copies(ins, outs, sems):
        send_sems, recv_sems = sems[0], sems[1]
        x, y, c, chips = _place()
        me = 2 * x + y
        sends, waits = [], []
        for w in range(n + len(whole)):
            for k, (cx, cy) in enumerate(chips):
                sem = (send_sems.at[3 * w + k], recv_sems.at[3 * w + k])
                if w < n:
                    hr = ins[w].shape[0] // 2
                    sends.append(_rcopy(ins[w].at[pl.ds(c * hr, hr), :], _gather_slot(ins[w], outs[w], me, c), *sem, (cx, cy, c)))
                    landed = _gather_slot(ins[w], outs[w], 2 * cx + cy, c)
                else:
                    sends.append(_rcopy(ins[w], outs[w].at[me], *sem, (cx, cy, c)))
                    landed = outs[w].at[2 * cx + cy]
                waits.append(_rcopy(landed, landed, *sem, (cx, cy, c)))
        local = [pltpu.make_async_copy(
            ins[w], outs[w].at[:, pl.ds(pl.multiple_of(me * ins[w].shape[1], LANES), ins[w].shape[1])], sems[2 + i])
            for i, w in enumerate(own_cols)]
        return sends, waits, local

    def start(ins, outs, sems):
        sends, _, local = copies(ins, outs, sems)
        for cp in local + sends:
            cp.start()

    def finish(ins, outs, sems):
        sends, waits, local = copies(ins, outs, sems)
        for cp in waits:
            cp.wait_recv()
        for cp in sends:
            cp.wait_send()
        for cp in local:
            cp.wait()

    out_shapes = [_gathered_shape(s, bc) for s, bc in zip(shards, by_cols)] + [_sds((N_CHIPS,) + a.shape, a.dtype) for a in whole]
    return _Hosted(list(shards) + list(whole), out_shapes, 3 * (n + len(whole)), start, finish, local_sems=len(own_cols))


def _plan_gather_d2d(bufs, shard_shapes):
    n = len(bufs)

    def copies(ins, outs, sems):
        send_sems, recv_sems = sems
        x, y, c, chips = _place()
        sibling = (x, y, 1 - c)
        sends, waits = [], []
        for w in range(n):
            for k, (cx, cy) in enumerate(chips):
                sem = (send_sems.at[3 * w + k], recv_sems.at[3 * w + k])
                landed = _gather_slot(shard_shapes[w], outs[w], 2 * cx + cy, c)
                other = _gather_slot(shard_shapes[w], outs[w], 2 * cx + cy, 1 - c)
                sends.append(_rcopy(landed, landed, *sem, sibling))
                waits.append(_rcopy(other, other, *sem, sibling))
        return sends, waits

    def start(ins, outs, sems):
        for cp in copies(ins, outs, sems)[0]:
            cp.start()

    def finish(ins, outs, sems):
        sends, waits = copies(ins, outs, sems)
        for cp in waits:
            cp.wait_recv()
        for cp in sends:
            cp.wait_send()

    return _Hosted(bufs, [_sds(b.shape, b.dtype) for b in bufs], 3 * n, start, finish, aliases={w: w for w in range(n)})


def _plan_swap(grads):
    def copies(ins, outs, sems):
        send_sems, recv_sems = sems
        x, y, c, _ = _place()
        cps = []
        for w, g_ref in enumerate(ins):
            if len(g_ref.shape) == 4:
                theirs = g_ref.at[:, 1 - c]
            else:
                hr = g_ref.shape[0] // 2
                theirs = g_ref.at[pl.ds((1 - c) * hr, hr), :]
            cps.append(_rcopy(theirs, outs[w], send_sems.at[w], recv_sems.at[w], (x, y, 1 - c)))
        return cps

    def start(ins, outs, sems):
        for cp in copies(ins, outs, sems):
            cp.start()

    def finish(ins, outs, sems):
        for cp in copies(ins, outs, sems):
            cp.wait()

    out_shapes = [_sds((g.shape[0], g.shape[2], g.shape[3])) if g.ndim == 4 else _sds((g.shape[0] // 2, g.shape[1]))
                  for g in grads]
    return _Hosted(grads, out_shapes, len(grads), start, finish)


def _plan_scatter(chip_sums, windows):
    def copies(ins, outs, sems):
        send_sems, recv_sems = sems
        x, y, c, chips = _place()
        me = 2 * x + y
        sends, waits = [], []
        for w, s_ref in enumerate(ins):
            for k, (cx, cy) in enumerate(chips):
                tgt = 2 * cx + cy
                if windows[w] is not None:
                    stride, width = windows[w]
                    part = s_ref.at[:, pl.ds(pl.multiple_of(tgt * stride, LANES), width)]
                else:
                    part = s_ref.at[tgt]
                sem = (send_sems.at[3 * w + k], recv_sems.at[3 * w + k])
                sends.append(_rcopy(part, outs[w].at[me], *sem, (cx, cy, c)))
                slot = outs[w].at[tgt]
                waits.append(_rcopy(slot, slot, *sem, (cx, cy, c)))
        return sends, waits

    def start(ins, outs, sems):
        for cp in copies(ins, outs, sems)[0]:
            cp.start()

    def finish(ins, outs, sems):
        sends, waits = copies(ins, outs, sems)
        for cp in waits:
            cp.wait_recv()
        for cp in sends:
            cp.wait_send()

    out_shapes = [_sds((N_CHIPS, s.shape[0], win[1]), BF16) if win is not None else _sds(s.shape, BF16)
                  for s, win in zip(chip_sums, windows)]
    return _Hosted(chip_sums, out_shapes, 3 * len(chip_sums), start, finish)


def _plan_join(reds):
    def copies(ins, outs, sems):
        send_sems, recv_sems = sems
        x, y, c, _ = _place()
        sends, waits = [], []
        for w, out in enumerate(outs):
            hr = out.shape[0] // 2
            mine = out.at[pl.ds(c * hr, hr), :]
            theirs = out.at[pl.ds((1 - c) * hr, hr), :]
            sends.append(_rcopy(mine, mine, send_sems.at[w], recv_sems.at[w], (x, y, 1 - c)))
            waits.append(_rcopy(theirs, theirs, send_sems.at[w], recv_sems.at[w], (x, y, 1 - c)))
        return sends, waits

    def start(ins, outs, sems):
        for cp in copies(ins, outs, sems)[0]:
            cp.start()

    def finish(ins, outs, sems):
        sends, waits = copies(ins, outs, sems)
        for cp in waits:
            cp.wait_recv()
        for cp in sends:
            cp.wait_send()

    return _Hosted(reds, [_sds(r.shape) for r in reds], len(reds), start, finish, aliases={w: w for w in range(len(reds))})


def _allreduce_small(v):
    m_per = v.shape[0]

    def body(v_ref, out_ref, all_ref, send_sems, recv_sems, local_sem):
        x, y, c, chips = _place()
        me, sibling = (x, y, c), (x, y, 1 - c)

        def rows(px, py, pc):
            return all_ref.at[pl.ds((4 * px + 2 * py + pc) * m_per, m_per), :]

        def copy(k, block, to, src=None):
            return _rcopy(rows(*block) if src is None else src, rows(*block), send_sems.at[k], recv_sems.at[k], to)

        mine = pltpu.make_async_copy(v_ref, rows(*me), local_sem)
        mine.start()
        first = [copy(0, me, sibling, src=v_ref)]
        first += [copy(1 + k, me, (*chip, c), src=v_ref) for k, chip in enumerate(chips)]
        for cp in first:
            cp.start()
        passed = [copy(4 + k, (*chip, c), sibling) for k, chip in enumerate(chips)]
        for k, chip in enumerate(chips):
            copy(1 + k, (*chip, c), me).wait_recv()
            passed[k].start()
        copy(0, sibling, me).wait_recv()
        for k, chip in enumerate(chips):
            copy(4 + k, (*chip, 1 - c), me).wait_recv()
        for cp in first + passed:
            cp.wait_send()
        mine.wait()
        acc = all_ref[pl.ds(0, m_per), :]
        for d in range(1, 8):
            acc = acc + all_ref[pl.ds(d * m_per, m_per), :]
        out_ref[...] = acc

    vm = pl.BlockSpec(memory_space=pltpu.VMEM)
    return _pallas(body, name="allreduce_small", in_specs=[vm], out_specs=vm, out_shape=_sds((m_per, LANES)),
                          scratch_shapes=[pltpu.VMEM((8 * m_per, LANES), F32), pltpu.SemaphoreType.DMA((7,)),
                                          pltpu.SemaphoreType.DMA((7,)), pltpu.SemaphoreType.DMA],
                          compiler_params=pltpu.CompilerParams(vmem_limit_bytes=VMEM_LIMIT))(v)


def _block_diag(blocks):
    j, g, a, b = blocks.shape
    eye = jnp.eye(g, dtype=bool)[None, :, None, :, None]
    return jnp.where(eye, blocks[:, :, :, None, :], jnp.zeros((), blocks.dtype)).reshape(j, g * a, g * b)


def _diag_blocks(m, a, b):
    j = m.shape[0]
    g = m.shape[1] // a
    t = m.reshape(j, g, a, g, b)
    eye = jnp.eye(g, dtype=bool)[None, :, None, :, None]
    return jnp.sum(jnp.where(eye, t, 0.0), axis=3)


def _pack_rows(parts, rows, dtype):
    used = sum(p.shape[0] for p in parts)
    return jnp.concatenate([p.astype(dtype) for p in parts] + [jnp.zeros((rows - used, D_MODEL), dtype)], axis=0)


_SMALL = (("g_mix", (1024,)), ("b_f", (8,)), ("g_q", (64,)), ("g_k", (64,)), ("lambda_re", (32, 64)),
          ("lambda_im", (32, 64)), ("log_step", (32,)), ("b_re", (32, 64, 16)), ("b_im", (32, 64, 16)),
          ("c_re", (32, 16, 64)), ("c_im", (32, 16, 64)), ("d_skip", (32, 16)), ("b_glu", (512,)),
          ("g_attn_out", (512,)), ("g_ssm_out", (512,)), ("g_ffn", (1024,)), ("conv_b", (5632,)))


def _small_rows(shape):
    return -(-math.prod(shape) // LANES)


def _pack_small(arrs, extra=()):
    parts = []
    for a in list(arrs) + list(extra):
        flat = a.reshape(-1)
        rows = -(-flat.shape[0] // LANES)
        parts.append(jnp.pad(flat, (0, rows * LANES - flat.shape[0])).reshape(rows, LANES))
    total = sum(p.shape[0] for p in parts)
    pad = -total % SUBLANES
    if pad:
        parts.append(jnp.zeros((pad, LANES), F32))
    return jnp.concatenate(parts, axis=0)


def _unpack_small(buf, shapes):
    out, r = [], 0
    for shape in shapes:
        n = math.prod(shape)
        rows = -(-n // LANES)
        out.append(buf[r:r + rows].reshape(-1)[:n].reshape(shape))
        r += rows
    return out


def _halves(t):
    return t.reshape(N_CHIPS, 2, t.shape[0] // (2 * N_CHIPS), t.shape[1])


class _MeshComm:
    def __init__(self, args):
        x, y, self.core = lax.axis_index("x"), lax.axis_index("y"), lax.axis_index("c")
        self.chip = 2 * x + y
        self.place = jnp.stack([self.chip, self.core]).astype(jnp.int32)
        self.shards = {n: args[n].astype(BF16) for n in ("w_in", "w_glu", "w_out", "w_up", "w_down")}
        self.conv_w = args["conv_w"]

    def _own(self, stacked, mine):
        return lax.dynamic_update_slice(stacked, mine[None], (self.chip,) + (0,) * mine.ndim)

    def w_in(self):
        sh = self.shards["w_in"]
        (buf,) = _run_hosted(_plan_gather_ici([sh], [False]), name="gather_w_in")
        (buf,) = _run_hosted(_plan_gather_d2d([buf], [sh]), name="pass_w_in")
        whole = self._own(buf, sh).transpose(1, 0, 2).reshape(D_MODEL, IN_COLS)
        return jnp.pad(whole, ((0, 0), (0, Z_COLS - IN_COLS)))

    def gather_first(self):
        self.mid = [self.shards[n] for n in ("w_glu", "w_out", "w_down")]
        return _plan_gather_ici(self.mid, [False, False, False], whole=[self.conv_w])

    def gather_second(self, landed):
        self.g_cw = landed[3]
        return _both(_plan_gather_d2d(list(landed[:3]), self.mid),
                     _plan_gather_ici([self.shards["w_up"]], [True], own_cols=[0]))

    def weights(self, gathered):
        g_glu, g_out, g_down = gathered[:3]
        own = self._own
        return (own(g_glu, self.mid[0]).reshape(SSM_W, SSM_W), own(g_out, self.mid[1]).reshape(D_MODEL, D_MODEL),
                own(g_down, self.mid[2]).reshape(D_FF, D_MODEL),
                own(self.g_cw, self.conv_w).transpose(1, 0, 2).reshape(3, 2 * D_FF))

    def gather_third(self, gathered):
        return _plan_gather_d2d([gathered[3]], [self.shards["w_up"]])

    def w_up(self, passed):
        return passed[0]

    def swap(self, d_w_down, d_w_up, d_w_glu, d_w_out):
        self.early = [_halves(d_w_down), d_w_up, _halves(d_w_glu), _halves(d_w_out)]
        return _plan_swap(self.early)

    def scatter(self, landed):
        self.early_sums = [_add_half(g, l, self.place, name="add_" + n)
                           for g, l, n in zip(self.early, landed, ("w_down", "w_up", "w_glu", "w_out"))]
        return _plan_scatter(self.early_sums, [None, (UP_COLS, UP_COLS), None, None])

    def reduce(self, early_lands, d_w_in):
        d_in = d_w_in
        (landed,) = _run_hosted(_plan_swap([d_in]), name="swap_halves")
        sum_in = _add_half(d_in, landed, self.place, name="add_w_in")
        (land_in,) = _run_hosted(_plan_scatter([sum_in], [(IN_STRIDE, IN_WINDOW)]), name="scatter_chips")
        es, el = self.early_sums, early_lands
        todo = [(sum_in, land_in, "w_in", LANES, IN_STRIDE), (es[2], el[2], "w_glu", SSM_W, 0),
                (es[3], el[3], "w_out", D_MODEL, 0), (es[1], el[1], "w_up", UP_COLS, UP_COLS),
                (es[0], el[0], "w_down", D_MODEL, 0)]
        reds = _run_hosted(_plan_join([_sum_chips(s, l, self.place, name="sum_" + n, tc=tc, window_stride=st)
                                       for s, l, n, tc, st in todo]), name="join_halves")
        g_big = dict(zip(("w_in", "w_glu", "w_out", "w_up", "w_down"), reds))
        g_big["w_in"] = lax.dynamic_slice_in_dim(reds[0], 2 * self.chip, IN_COLS // N_CHIPS, axis=1)
        return g_big


def _local_step(x, tgt, p, comm):
    s = x.shape[0]
    row = lambda v: v.reshape(1, -1)
    g_mix, g_ffn = row(p["g_mix"]), row(p["g_ffn"])
    g_att, g_ssm, b_glu, conv_b = row(p["g_attn_out"]), row(p["g_ssm_out"]), row(p["b_glu"]), row(p["conv_b"])
    gq = row(jnp.tile(p["g_q"], HEADS))
    gk = row(jnp.tile(p["g_k"], HEADS))
    bf = row(jnp.pad(p["b_f"], (0, LANES - HEADS)))
    gg = jnp.kron(jnp.eye(HEADS, dtype=F32), jnp.ones((HEAD_DIM, HEAD_DIM), F32)).astype(BF16)
    dsk = row(p["d_skip"])

    rep = lambda a: jnp.repeat(a, SSM_GROUP, axis=0)
    lr, li = rep(p["lambda_re"]), rep(p["lambda_im"])
    ls = rep(jnp.broadcast_to(p["log_step"][:, None], (SSM_GROUPS, SSM_STATE)))
    bt_re = p["b_re"].transpose(0, 2, 1).reshape(_PARAM_SHAPE)
    bt_im = p["b_im"].transpose(0, 2, 1).reshape(_PARAM_SHAPE)
    a_re_rep, a_im_rep, bb_re, bb_im = _ssm_params(lr, li, ls, bt_re, bt_im)
    ar = a_re_rep[::SSM_GROUP].reshape(SSM_CHUNKS, 1, CHUNK_S)
    ai = a_im_rep[::SSM_GROUP].reshape(SSM_CHUNKS, 1, CHUNK_S)
    chunked = lambda t: t.reshape(SSM_CHUNKS, SSM_GROUPS // SSM_CHUNKS, SSM_GROUP, SSM_STATE)
    bbr = _block_diag(chunked(bb_re)).astype(BF16)
    bbi = _block_diag(chunked(bb_im)).astype(BF16)
    to_cc = lambda c: _block_diag(chunked(c).transpose(0, 1, 3, 2)).astype(BF16)
    ccr, cci = to_cc(p["c_re"]), to_cc(p["c_im"])

    w_in_r = comm.w_in()
    hb, z = _in_proj(x, g_mix, w_in_r)
    qh, kh, vh, ub, uf, c128 = _attn_prep(z, gq, gk, bf, gg)
    crow = c128[:, :HEADS].T.reshape(HEADS, 1, s)
    (oh, lse), landed = _attn_fwd(qh, kh, vh, crow, comm.gather_first())
    (xr, xi, y), gathered = _ssm_fwd(ub, uf, bbr, bbi, ar, ai, ccr, cci, dsk, comm.gather_second(landed))
    w_glu_b, w_out_b, w_down_b, conv_w_full = comm.weights(gathered)
    (x1, mixb, h2b), passed = _mix_out(y, oh, x, w_glu_b, b_glu, g_att, g_ssm, w_out_b, g_ffn, comm.gather_third(gathered))
    w_up_b = comm.w_up(passed)
    up = _mm(h2b, w_up_b, name="ffn_up", tm=1024, tn=1408, tk=1024)
    act = _conv_act(up, conv_w_full, conv_b)
    dy, dyb, loss_blk = _down_loss(act, w_down_b, x1, tgt)

    d_w_down = _mm(act, dyb, ta=True, name="d_w_down", tm=1408, tn=1024, tk=2048)
    dact = _mm(dyb, w_down_b, tb=True, name="d_act", tm=1024, tn=1408, tk=1024)
    dupb, dcw = _conv_act_bwd(up, dact, conv_w_full, conv_b)
    d_w_up = _mm(h2b, dupb, ta=True, b_parts=2, name="d_w_up", tm=1024, tn=1408, tk=2048)
    dh2 = _mm(dupb, w_up_b, tb=True, a_parts=2, name="d_h2", tm=1024, tn=1024, tk=1408)
    dx1, dx1b, doh, dys, d_w_glu, d_g_ffn, d_g_att, d_g_ssm, d_b_glu = _mix_bwd(
        dy, dh2, x1, g_ffn, w_out_b, y, oh, w_glu_b, b_glu, g_att, g_ssm)
    d_w_out = _mm(mixb, dx1b, ta=True, name="d_w_out", tm=1024, tn=1024, tk=2048)
    (du, dbbr, dbbi, dccr, dcci, dar, dai, dd), swapped = _ssm_bwd(dys, uf, ub, xr, xi, bbr, bbi, ar, ai, ccr, cci, dsk,
                                                                comm.swap(d_w_down, d_w_up, d_w_glu, d_w_out))
    (dqh, dkh, dvh, dcrow), early_lands = _attn_bwd(qh, kh, vh, crow, lse, doh, comm.scatter(swapped))
    dc128 = jnp.pad(dcrow.reshape(HEADS, s).T, ((0, 0), (0, LANES - HEADS)))
    dzb, d_gq, d_gk, d_bf = _prep_bwd(z, dqh, dkh, dvh, du, dc128, gq, gk, bf, gg)
    d_w_in_r = _mm(hb, dzb, ta=True, name="d_w_in", tm=512, tn=Z_COLS, tk=2048)
    dh = _mm(dzb, w_in_r, tb=True, name="d_h", tm=1024, tn=1024, tk=Z_COLS)
    dx, d_g_mix = _in_norm_bwd(x, g_mix, dh, dx1)

    unchunk = lambda t: t.reshape(_PARAM_SHAPE)
    dbb_re = unchunk(_diag_blocks(dbbr, SSM_GROUP, SSM_STATE))
    dbb_im = unchunk(_diag_blocks(dbbi, SSM_GROUP, SSM_STATE))
    first_row = (jnp.arange(_PARAM_SHAPE[0]) % SSM_GROUP == 0)[:, None]
    da_re = jnp.where(first_row, rep(dar.reshape(SSM_GROUPS, SSM_STATE)), 0.0)
    da_im = jnp.where(first_row, rep(dai.reshape(SSM_GROUPS, SSM_STATE)), 0.0)
    expand_t = (jnp.arange(SSM_GROUPS)[:, None] == (jnp.arange(_PARAM_SHAPE[0]) // SSM_GROUP)[None, :]).astype(BF16)
    d_lr, d_li, d_ls, d_bt_re, d_bt_im = _ssm_params_bwd(lr, li, ls, bt_re, bt_im, da_re, da_im, dbb_re, dbb_im, expand_t)
    from_bt = lambda t: t.reshape(SSM_GROUPS, SSM_GROUP, SSM_STATE).transpose(0, 2, 1)
    from_cc = lambda t: _diag_blocks(t, SSM_STATE, SSM_GROUP).transpose(0, 1, 3, 2).reshape(SSM_GROUPS, SSM_GROUP, SSM_STATE)

    small = {
        "g_mix": d_g_mix, "b_f": d_bf[0, :HEADS], "g_q": d_gq.reshape(HEADS, HEAD_DIM).sum(0),
        "g_k": d_gk.reshape(HEADS, HEAD_DIM).sum(0), "lambda_re": d_lr, "lambda_im": d_li, "log_step": d_ls,
        "b_re": from_bt(d_bt_re), "b_im": from_bt(d_bt_im), "c_re": from_cc(dccr), "c_im": from_cc(dcci),
        "d_skip": dd, "b_glu": d_b_glu, "g_attn_out": d_g_att, "g_ssm_out": d_g_ssm, "g_ffn": d_g_ffn,
        "conv_b": dcw[:, 3],
    }
    big = {"w_in": d_w_in_r, "w_glu": d_w_glu, "w_out": d_w_out, "w_up": d_w_up, "w_down": d_w_down}
    return loss_blk[0, 0], dx, big, small, dcw[:, 0:3].transpose(1, 0, 2).reshape(3, 2 * D_FF), early_lands


def kernel(x, g_mix, w_in, b_f, g_q, g_k, lambda_re, lambda_im, log_step, b_re, b_im, c_re, c_im, d_skip, w_glu, b_glu, g_attn_out, g_ssm_out, w_out, g_ffn, w_up, conv_w, conv_b, w_down, loss_target, m_g_mix, m_w_in, m_b_f, m_g_q, m_g_k, m_lambda_re, m_lambda_im, m_log_step, m_b_re, m_b_im, m_c_re, m_c_im, m_d_skip, m_w_glu, m_b_glu, m_g_attn_out, m_g_ssm_out, m_w_out, m_g_ffn, m_w_up, m_conv_w, m_conv_b, m_w_down, v_g_mix, v_w_in, v_b_f, v_g_q, v_g_k, v_lambda_re, v_lambda_im, v_log_step, v_b_re, v_b_im, v_c_re, v_c_im, v_d_skip, v_w_glu, v_b_glu, v_g_attn_out, v_g_ssm_out, v_w_out, v_g_ffn, v_w_up, v_conv_w, v_conv_b, v_w_down):
    args = dict(locals())
    order = ["g_mix", "w_in", "b_f", "g_q", "g_k", "lambda_re", "lambda_im", "log_step", "b_re", "b_im", "c_re", "c_im",
             "d_skip", "w_glu", "b_glu", "g_attn_out", "g_ssm_out", "w_out", "g_ffn", "w_up", "conv_w", "conv_b", "w_down"]
    comm = _MeshComm(args)
    chip = comm.chip
    loss_part, dx, big, small, d_conv_w, early_lands = _local_step(x[0], loss_target[0], args, comm)
    loss = lax.psum(loss_part, ("x", "y", "c"))

    g_big = comm.reduce(early_lands, big["w_in"])

    small_names = [n for n, _ in _SMALL]
    small_shapes = [sh for _, sh in _SMALL]
    gsum = _allreduce_small(_pack_small([small[n] for n in small_names], extra=[d_conv_w]))
    g_small = _unpack_small(gsum, small_shapes + [(3, 2 * D_FF)])
    g_conv_w = lax.dynamic_slice_in_dim(g_small[-1], chip * (2 * D_FF // N_CHIPS), 2 * D_FF // N_CHIPS, axis=1)
    g_small = dict(zip(small_names, g_small[:-1]))

    grad, delta, new_m, new_v = {}, {}, {}, {}
    for n in ("w_in", "w_glu", "w_out", "w_up", "w_down"):
        grad[n] = g_big[n]
        delta[n], new_m[n], new_v[n] = _adamw(args[n], g_big[n], args["m_" + n], args["v_" + n], name="adamw_" + n)
    grad["conv_w"] = g_conv_w
    delta["conv_w"], new_m["conv_w"], new_v["conv_w"] = _adamw(conv_w, g_conv_w, m_conv_w, v_conv_w, name="adamw_conv_w")
    stepped = _adamw_small([args[n] for n in small_names], [g_small[n] for n in small_names],
                           [args["m_" + n] for n in small_names], [args["v_" + n] for n in small_names])
    for i, n in enumerate(small_names):
        grad[n] = g_small[n]
        delta[n], new_m[n], new_v[n] = stepped[3 * i:3 * i + 3]

    return (loss, dx[None], *[grad[n] for n in order], *[delta[n] for n in order], *[new_m[n] for n in order],
            *[new_v[n] for n in order])
```

```python
import math

import jax
import jax.numpy as jnp
from jax import lax
from jax.experimental import pallas as pl
from jax.experimental.pallas import tpu as pltpu

F32 = jnp.float32
BF16 = jnp.bfloat16

D_MODEL = 1024
HEADS = 8
HEAD_DIM = 64
ATTN_W = 512
SSM_W = 512
SSM_GROUPS = 32
SSM_GROUP = 16
SSM_STATE = 64
N_STATE = SSM_GROUPS * SSM_STATE
D_FF = 2816
IN_COLS = 2056
Z_COLS = 2176
F_COL0 = 1536
U_COL0 = 1544
EPS = 1e-6
NEG_INF = -1e30
N_CHIPS = 4
LANES = 128
SUBLANES = 8
SSM_CHUNKS = 2
CHUNK_U = SSM_W // SSM_CHUNKS
CHUNK_S = N_STATE // SSM_CHUNKS
HEADS_PER_STEP = 4
STRIP = 128
N_STRIPS = D_FF // STRIP

ROWS_IN, ROWS_GLU, ROWS_OUT, ROWS_UP, ROWS_DOWN = 514, 64, 256, 1408, 704
OFF_GLU = ROWS_IN
OFF_OUT = OFF_GLU + ROWS_GLU
OFF_UP = OFF_OUT + ROWS_OUT
OFF_DOWN = OFF_UP + ROWS_UP
OFF_SPARE = OFF_DOWN + ROWS_DOWN
PACK_ROWS = 2976
HALF_ROWS = PACK_ROWS // 2
CONVW_ROWS = 9

ADAM_LR = 0.001
ADAM_B1 = 0.9
ADAM_B2 = 0.999
ADAM_EPS = 1e-08
ADAM_WD = 0.01
ADAM_STEP = 10

VMEM_LIMIT = 56 * 1024 * 1024
MESH = pl.DeviceIdType.MESH


def _pallas(body, **kw):
    return pl.pallas_call(body, **kw)


def _pcall(body, *, name, out_shape, in_specs, out_specs, grid=(), scratch_shapes=(), dims=None):
    params = pltpu.CompilerParams(dimension_semantics=dims, vmem_limit_bytes=VMEM_LIMIT)
    return _pallas(body, name=name, grid=grid, in_specs=in_specs, out_specs=out_specs,
                   out_shape=out_shape, scratch_shapes=scratch_shapes, compiler_params=params)


def _sds(shape, dtype=F32):
    return jax.ShapeDtypeStruct(shape, dtype)


def _dot(a, b):
    return jnp.dot(a, b, preferred_element_type=F32)


def _dot_nt(a, b):
    return lax.dot_general(a, b, (((1,), (1,)), ((), ())), preferred_element_type=F32)


def _dot_tn(a, b):
    return lax.dot_general(a, b, (((0,), (0,)), ((), ())), preferred_element_type=F32)


def _split3(x):
    hi = x.astype(BF16)
    r = x - hi.astype(F32)
    mid = r.astype(BF16)
    lo = (r - mid.astype(F32)).astype(BF16)
    return hi, mid, lo


def _dot_exact_r(x, m01):
    hi, mid, lo = _split3(x)
    return _dot(hi, m01) + _dot(mid, m01) + _dot(lo, m01)


def _dot_exact_l(m01, x):
    hi, mid, lo = _split3(x)
    return _dot(m01, hi) + _dot(m01, mid) + _dot(m01, lo)


def _sigmoid(x):
    return 1.0 / (1.0 + jnp.exp(-x))


def _rms(x, g):
    r = lax.rsqrt(jnp.mean(x * x, axis=-1, keepdims=True) + EPS)
    return x * r * g


def _rms_bwd(x, g, dy):
    r = lax.rsqrt(jnp.mean(x * x, axis=-1, keepdims=True) + EPS)
    w = dy * g
    dx = r * w - x * (r * r * r) * jnp.mean(w * x, axis=-1, keepdims=True)
    dg = jnp.sum(dy * x * r, axis=0, keepdims=True)
    return dx, dg


_GELU_K = math.sqrt(2.0 / math.pi)
_GELU_C = 0.044715


def _gelu(y):
    return y * (0.5 * (1.0 + jnp.tanh(_GELU_K * (y + _GELU_C * (y * y * y)))))


def _gelu_grad(y):
    t = jnp.tanh(_GELU_K * (y + _GELU_C * (y * y * y)))
    return 0.5 * (1.0 + t) + 0.5 * y * (1.0 - t * t) * (_GELU_K * (1.0 + 3.0 * _GELU_C * y * y))


def _tile(n, pref):
    if n <= pref:
        return n
    divs = [t for t in range(LANES, n + 1, LANES) if n % t == 0]
    below = [t for t in divs if t <= pref]
    if below and 2 * below[-1] >= pref:
        return below[-1]
    above = [t for t in divs if t > pref]
    return above[0] if above else n


def _row_tile(s):
    return min(256, s)


def _mm(a, b, *, name, tm, tn, tk, ta=False, tb=False, a_parts=1, b_parts=1):
    if a_parts > 1:
        m, kk = a.shape[1], a.shape[2] * a_parts
    elif ta:
        kk, m = a.shape
    else:
        m, kk = a.shape
    if b_parts > 1:
        n = b.shape[2] * b_parts
    else:
        n = b.shape[0] if tb else b.shape[1]
    tm, tn, tk = _tile(m, tm), _tile(n // b_parts, tn), _tile(kk // a_parts, tk)
    k_per, n_per = kk // a_parts // tk, n // b_parts // tn

    def body(a_ref, b_ref, o_ref):
        k = pl.program_id(2)
        if ta:
            part = _dot_tn(a_ref[...], b_ref[...])
        elif tb:
            part = _dot_nt(a_ref[...], b_ref[...])
        else:
            part = _dot(a_ref[...], b_ref[...])

        @pl.when(k == 0)
        def _():
            o_ref[...] = part

        @pl.when(k > 0)
        def _():
            o_ref[...] += part

    if a_parts > 1:
        a_spec = pl.BlockSpec((None, tm, tk), lambda i, j, k: (k // k_per, i, k % k_per))
    else:
        a_spec = pl.BlockSpec((tk, tm), lambda i, j, k: (k, i)) if ta else pl.BlockSpec((tm, tk), lambda i, j, k: (i, k))
    if b_parts > 1:
        b_spec = pl.BlockSpec((None, tk, tn), lambda i, j, k: (j // n_per, k, j % n_per))
    else:
        b_spec = pl.BlockSpec((tn, tk), lambda i, j, k: (j, k)) if tb else pl.BlockSpec((tk, tn), lambda i, j, k: (k, j))
    return _pcall(body, name=name, grid=(m // tm, n // tn, kk // tk), in_specs=[a_spec, b_spec],
                  out_specs=pl.BlockSpec((tm, tn), lambda i, j, k: (i, j)), out_shape=_sds((m, n)),
                  dims=("parallel", "parallel", "arbitrary"))(a, b)


def _in_proj(x, g_mix, w_in_r):
    s = x.shape[0]
    tm = _row_tile(s)

    def body(x_ref, g_ref, w_ref, h_ref, z_ref):
        h = _rms(x_ref[...], g_ref[...]).astype(BF16)
        h_ref[...] = h
        z_ref[...] = _dot(h, w_ref[...])

    return _pcall(body, name="in_proj", grid=(s // tm,),
                  in_specs=[pl.BlockSpec((tm, D_MODEL), lambda i: (i, 0)), pl.BlockSpec((1, D_MODEL), lambda i: (0, 0)),
                            pl.BlockSpec((D_MODEL, Z_COLS), lambda i: (0, 0))],
                  out_specs=[pl.BlockSpec((tm, D_MODEL), lambda i: (i, 0)), pl.BlockSpec((tm, Z_COLS), lambda i: (i, 0))],
                  out_shape=[_sds((s, D_MODEL), BF16), _sds((s, Z_COLS))], dims=("parallel",))(x, g_mix, w_in_r)


def _split_heads(ref, val):
    for h in range(HEADS):
        ref[h] = val[:, h * HEAD_DIM:(h + 1) * HEAD_DIM].astype(ref.dtype)


def _merge_heads(ref):
    return jnp.concatenate([ref[h].astype(F32) for h in range(HEADS)], axis=-1)


def _forget_logits(z_ref, bf_ref):
    fl = z_ref[:, F_COL0:F_COL0 + LANES] + bf_ref[...]
    return jnp.where(lax.broadcasted_iota(jnp.int32, fl.shape, 1) < HEADS, fl, 0.0)


def _attn_prep(z, gq, gk, bf, gg):
    s = z.shape[0]
    tm = _row_tile(s)

    def body(z_ref, gq_ref, gk_ref, bf_ref, gg_ref, qn_ref, kn_ref, vb_ref, ub_ref, uf_ref, c_ref, carry_ref):
        i = pl.program_id(0)

        @pl.when(i == 0)
        def _():
            carry_ref[...] = jnp.zeros_like(carry_ref)

        gg_m = gg_ref[...]

        def head_norm(t, g):
            ssq = _dot_exact_r(t * t, gg_m)
            return t * lax.rsqrt(ssq * (1.0 / HEAD_DIM) + EPS) * g

        _split_heads(qn_ref, head_norm(z_ref[:, 0:ATTN_W], gq_ref[...]))
        _split_heads(kn_ref, head_norm(z_ref[:, ATTN_W:2 * ATTN_W], gk_ref[...]))
        _split_heads(vb_ref, z_ref[:, 2 * ATTN_W:3 * ATTN_W])
        u = z_ref[:, U_COL0:U_COL0 + SSM_W]
        uf_ref[...] = u
        ub_ref[...] = u.astype(BF16)
        fl = _forget_logits(z_ref, bf_ref)
        lf = jnp.minimum(fl, 0.0) - jnp.log1p(jnp.exp(-jnp.abs(fl)))
        row = lax.broadcasted_iota(jnp.int32, (tm, tm), 0)
        col = lax.broadcasted_iota(jnp.int32, (tm, tm), 1)
        tri = (row >= col).astype(BF16)
        c = _dot_exact_l(tri, lf) + carry_ref[...]
        c_ref[...] = c
        carry_ref[...] = c[tm - 1:tm, :]

    row_spec = lambda w: pl.BlockSpec((tm, w), lambda i: (i, 0))
    const = lambda shape: pl.BlockSpec(shape, lambda i: (0, 0))
    heads = pl.BlockSpec((HEADS, tm, HEAD_DIM), lambda i: (0, i, 0))
    return _pcall(body, name="attn_prep", grid=(s // tm,),
                  in_specs=[row_spec(Z_COLS), const((1, ATTN_W)), const((1, ATTN_W)), const((1, LANES)), const((ATTN_W, ATTN_W))],
                  out_specs=[heads] * 3 + [row_spec(SSM_W), row_spec(SSM_W), row_spec(LANES)],
                  out_shape=[_sds((HEADS, s, HEAD_DIM), BF16)] * 3 + [_sds((s, SSM_W), BF16), _sds((s, SSM_W)), _sds((s, LANES))],
                  scratch_shapes=[pltpu.VMEM((1, LANES), F32)], dims=("arbitrary",))(z, gq, gk, bf, gg)


def _attn_fwd(qh, kh, vh, crow, hosted=None):
    _, s, _ = qh.shape
    tq = _row_tile(s)
    scale = HEAD_DIM ** -0.5

    hp = HEADS_PER_STEP
    nq = s // tq
    fold = lambda t, op: op(t[:, :tq // 2], t[:, tq // 2:])

    def body(q_ref, k_ref, v_ref, c_ref, o_ref, lse_ref, s_s):
        i = pl.program_id(1)

        def first(j, ms, diagonal):
            off = pl.multiple_of(j * tq, tq)
            out = []
            for hh in range(hp):
                sc = _dot_nt(q_ref[hh], k_ref[hh, pl.ds(off, tq), :]) * scale - c_ref[hh, :, pl.ds(off, tq)]
                if diagonal:
                    causal = lax.broadcasted_iota(jnp.int32, (tq, tq), 1) <= lax.broadcasted_iota(jnp.int32, (tq, tq), 0)
                    sc = jnp.where(causal, sc, NEG_INF)
                s_s[hh, j] = sc
                out.append(jnp.maximum(ms[hh], fold(sc, jnp.maximum)))
            return tuple(out)

        ms = lax.fori_loop(0, i, lambda j, c: first(j, c, False), (jnp.full((tq, tq // 2), NEG_INF, F32),) * hp)
        ms = [jnp.max(t, axis=-1, keepdims=True) for t in first(i, ms, True)]

        def second(j, carry):
            rows = pl.ds(pl.multiple_of(j * tq, tq), tq)
            out = []
            for hh in range(hp):
                ls, acc = carry[hh]
                p = jnp.exp(s_s[hh, j] - ms[hh])
                out.append((ls + fold(p, jnp.add), acc + _dot(p.astype(BF16), v_ref[hh, rows, :])))
            return tuple(out)

        zero = (jnp.zeros((tq, tq // 2), F32), jnp.zeros((tq, HEAD_DIM), F32))
        for hh, (ls, acc) in enumerate(lax.fori_loop(0, i + 1, second, (zero,) * hp)):
            l = jnp.sum(ls, axis=-1, keepdims=True)
            o_ref[hh] = acc / l
            lse_ref[hh] = ms[hh] + jnp.log(l)

    blk = pl.BlockSpec((hp, tq, HEAD_DIM), lambda h, i: (h, i, 0))
    full = pl.BlockSpec((hp, s, HEAD_DIM), lambda h, i: (h, 0, 0))
    nh = HEADS // hp
    first = lambda: jnp.logical_and(pl.program_id(0) == 0, pl.program_id(1) == 0)
    last = lambda: jnp.logical_and(pl.program_id(0) == nh - 1, pl.program_id(1) == nq - 1)
    return _host_pcall(body, hosted, first, last, n_in=4, n_out=2, n_scratch=1, name="attn_fwd", grid=(nh, nq),
                       in_specs=[blk, full, full, pl.BlockSpec((hp, 1, s), lambda h, i: (h, 0, 0))],
                       out_specs=[blk, pl.BlockSpec((hp, tq, 1), lambda h, i: (h, i, 0))],
                       out_shape=[_sds((HEADS, s, HEAD_DIM)), _sds((HEADS, s, 1))],
                       scratch_shapes=[pltpu.VMEM((hp, nq, tq, tq), F32)],
                       dims=("parallel", "parallel"), operands=(qh, kh, vh, crow))


def _ssm_param_fn(lr, li, ls, br, bi):
    step = jnp.exp(ls)
    er = jnp.exp(lr * step)
    ab_re = er * jnp.cos(li * step)
    ab_im = er * jnp.sin(li * step)
    num_re = ab_re - 1.0
    num_im = ab_im
    den = lr * lr + li * li
    f_re = (num_re * lr + num_im * li) / den
    f_im = (num_im * lr - num_re * li) / den
    bb_re = f_re * br - f_im * bi
    bb_im = f_re * bi + f_im * br
    return ab_re, ab_im, bb_re, bb_im


_PARAM_SHAPE = (SSM_GROUPS * SSM_GROUP, SSM_STATE)


def _ssm_params(lr, li, ls, br, bi):
    def body(lr_ref, li_ref, ls_ref, br_ref, bi_ref, ar_ref, ai_ref, bbr_ref, bbi_ref):
        ar, ai, bbr, bbi = _ssm_param_fn(lr_ref[...], li_ref[...], ls_ref[...], br_ref[...], bi_ref[...])
        ar_ref[...] = ar
        ai_ref[...] = ai
        bbr_ref[...] = bbr
        bbi_ref[...] = bbi

    spec = pl.BlockSpec(_PARAM_SHAPE, lambda: (0, 0))
    return _pcall(body, name="ssm_params", in_specs=[spec] * 5, out_specs=[spec] * 4,
                  out_shape=[_sds(_PARAM_SHAPE)] * 4)(lr, li, ls, br, bi)


def _ssm_params_bwd(lr, li, ls, br, bi, dar, dai, dbbr, dbbi, expand_t):
    def body(lr_ref, li_ref, ls_ref, br_ref, bi_ref, dar_ref, dai_ref, dbbr_ref, dbbi_ref, et_ref,
             dlr_ref, dli_ref, dls_ref, dbr_ref, dbi_ref):
        _, vjp = jax.vjp(_ssm_param_fn, lr_ref[...], li_ref[...], ls_ref[...], br_ref[...], bi_ref[...])
        dlr, dli, dls, dbr, dbi = vjp((dar_ref[...], dai_ref[...], dbbr_ref[...], dbbi_ref[...]))
        et = et_ref[...]
        dlr_ref[...] = _dot_exact_l(et, dlr)
        dli_ref[...] = _dot_exact_l(et, dli)
        dls_ref[...] = jnp.sum(_dot_exact_l(et, dls), axis=-1, keepdims=True)
        dbr_ref[...] = dbr
        dbi_ref[...] = dbi

    spec = pl.BlockSpec(_PARAM_SHAPE, lambda: (0, 0))
    gspec = pl.BlockSpec((SSM_GROUPS, SSM_STATE), lambda: (0, 0))
    return _pcall(body, name="ssm_params_bwd",
                  in_specs=[spec] * 9 + [pl.BlockSpec((SSM_GROUPS, _PARAM_SHAPE[0]), lambda: (0, 0))],
                  out_specs=[gspec, gspec, pl.BlockSpec((SSM_GROUPS, 1), lambda: (0, 0)), spec, spec],
                  out_shape=[_sds((SSM_GROUPS, SSM_STATE))] * 2 + [_sds((SSM_GROUPS, 1))] + [_sds(_PARAM_SHAPE)] * 2,
                  )(lr, li, ls, br, bi, dar, dai, dbbr, dbbi, expand_t)


def _cmul(ar, ai, br, bi):
    return ar * br - ai * bi, ar * bi + ai * br


def _scan_consts(ar, ai, width, reverse):
    row = lax.broadcasted_iota(jnp.int32, (SUBLANES, width), 0)
    pw = [(ar, ai)]
    for _ in range(SUBLANES - 1):
        pw.append(_cmul(pw[-1][0], pw[-1][1], ar, ai))
    steps = []
    for d in (1, 2, 4):
        keep = (row < SUBLANES - d) if reverse else (row >= d)
        steps.append((d, jnp.where(keep, pw[d - 1][0], 0.0), jnp.where(keep, pw[d - 1][1], 0.0)))
    pr = jnp.zeros((SUBLANES, width), F32)
    pi = jnp.zeros((SUBLANES, width), F32)
    for r in range(SUBLANES):
        e = (SUBLANES - r) if reverse else (r + 1)
        pr = jnp.where(row == r, pw[e - 1][0], pr)
        pi = jnp.where(row == r, pw[e - 1][1], pi)
    return steps, pr, pi


def _scan_tile(xr, xi, cr, ci, consts, reverse):
    steps, pr, pi = consts
    for d, mr, mi in steps:
        sh = (SUBLANES - d) if reverse else d
        sr = pltpu.roll(xr, sh, 0)
        si = pltpu.roll(xi, sh, 0)
        xr, xi = xr + mr * sr - mi * si, xi + mr * si + mi * sr
    return xr + pr * cr - pi * ci, xi + pr * ci + pi * cr


def _ssm_fwd(ub, uf, bbr, bbi, ar, ai, ccr, cci, dsk, hosted=None):
    s = ub.shape[0]
    tm = _row_tile(s)
    nt = tm // SUBLANES

    def body(ub_ref, u_ref, bbr_ref, bbi_ref, ar_ref, ai_ref, ccr_ref, cci_ref, dsk_ref,
             xr_ref, xi_ref, y_ref, cr_s, ci_s):
        i = pl.program_id(1)

        @pl.when(i == 0)
        def _():
            cr_s[...] = jnp.zeros_like(cr_s)
            ci_s[...] = jnp.zeros_like(ci_s)

        u_b = ub_ref[...]
        xr_ref[...] = _dot(u_b, bbr_ref[0])
        xi_ref[...] = _dot(u_b, bbi_ref[0])
        consts = _scan_consts(ar_ref[0], ai_ref[0], CHUNK_S, False)

        def tile(k, carry):
            cr, ci = carry
            sl = pl.ds(pl.multiple_of(k * SUBLANES, SUBLANES), SUBLANES)
            xr, xi = _scan_tile(xr_ref[sl, :], xi_ref[sl, :], cr, ci, consts, False)
            xr_ref[sl, :] = xr
            xi_ref[sl, :] = xi
            return xr[SUBLANES - 1:SUBLANES, :], xi[SUBLANES - 1:SUBLANES, :]

        cr, ci = lax.fori_loop(0, nt, tile, (cr_s[...], ci_s[...]))
        cr_s[...] = cr
        ci_s[...] = ci
        y_ref[...] = (_dot(xr_ref[...].astype(BF16), ccr_ref[0]) - _dot(xi_ref[...].astype(BF16), cci_ref[0])
                      + dsk_ref[...] * u_ref[...])

    wspec = lambda a, b: pl.BlockSpec((1, a, b), lambda j, i: (j, 0, 0))
    nb = s // tm
    first = lambda: jnp.logical_and(pl.program_id(0) == 0, pl.program_id(1) == 0)
    last = lambda: jnp.logical_and(pl.program_id(0) == SSM_CHUNKS - 1, pl.program_id(1) == nb - 1)
    return _host_pcall(
        body, hosted, first, last, n_in=9, n_out=3, n_scratch=2, name="ssm_fwd", grid=(SSM_CHUNKS, nb),
        in_specs=[pl.BlockSpec((tm, CHUNK_U), lambda j, i: (i, j)),
                  pl.BlockSpec((tm, CHUNK_U), lambda j, i: (i, j)),
                  wspec(CHUNK_U, CHUNK_S), wspec(CHUNK_U, CHUNK_S), wspec(1, CHUNK_S), wspec(1, CHUNK_S),
                  wspec(CHUNK_S, CHUNK_U), wspec(CHUNK_S, CHUNK_U),
                  pl.BlockSpec((1, CHUNK_U), lambda j, i: (0, j))],
        out_specs=[pl.BlockSpec((tm, CHUNK_S), lambda j, i: (i, j)), pl.BlockSpec((tm, CHUNK_S), lambda j, i: (i, j)),
                   pl.BlockSpec((tm, CHUNK_U), lambda j, i: (i, j))],
        out_shape=[_sds((s, N_STATE)), _sds((s, N_STATE)), _sds((s, SSM_W))],
        scratch_shapes=[pltpu.VMEM((1, CHUNK_S), F32)] * 2,
        dims=("parallel", "arbitrary"), operands=(ub, uf, bbr, bbi, ar, ai, ccr, cci, dsk))


def _ssm_glu(y, w_glu, b_glu):
    ge = _gelu(y)
    sg = _sigmoid(_dot(ge.astype(BF16), w_glu) + b_glu)
    return ge, sg


def _mix_out(y, att, x, w_glu, b_glu, g_att, g_ssm, w_out, g_ffn, hosted=None):
    s = x.shape[0]
    tm = _row_tile(s)

    def body(y_ref, att_ref, x_ref, wg_ref, bg_ref, ga_ref, gs_ref, wo_ref, gf_ref, x1_ref, mix_ref, h2_ref):
        ge, sg = _ssm_glu(y_ref[...], wg_ref[...], bg_ref[...])
        ms = _rms(ge * sg, gs_ref[...]).astype(BF16)
        ma = _rms(_merge_heads(att_ref), ga_ref[...]).astype(BF16)
        mix_ref[:, 0:ATTN_W] = ma
        mix_ref[:, ATTN_W:D_MODEL] = ms
        x1 = x_ref[...] + (_dot(ma, wo_ref[0:ATTN_W, :]) + _dot(ms, wo_ref[ATTN_W:D_MODEL, :]))
        x1_ref[...] = x1
        h2_ref[...] = _rms(x1, gf_ref[...]).astype(BF16)

    row = lambda w: pl.BlockSpec((tm, w), lambda i: (i, 0))
    const = lambda a, b: pl.BlockSpec((a, b), lambda i: (0, 0))
    nb = s // tm
    return _host_pcall(body, hosted, lambda: pl.program_id(0) == 0, lambda: pl.program_id(0) == nb - 1,
                       n_in=9, n_out=3, n_scratch=0, name="mix_out", grid=(nb,),
                       in_specs=[row(SSM_W), pl.BlockSpec((HEADS, tm, HEAD_DIM), lambda i: (0, i, 0)), row(D_MODEL),
                                 const(SSM_W, SSM_W), const(1, SSM_W),
                                 const(1, ATTN_W), const(1, SSM_W), const(D_MODEL, D_MODEL), const(1, D_MODEL)],
                       out_specs=[row(D_MODEL)] * 3,
                       out_shape=[_sds((s, D_MODEL)), _sds((s, D_MODEL), BF16), _sds((s, D_MODEL), BF16)],
                       scratch_shapes=[], dims=("parallel",), operands=(y, att, x, w_glu, b_glu, g_att, g_ssm, w_out, g_ffn))


CONV_CHUNK = 64


def _conv_rows(pad_ref, w, b, r0, n):
    y = b + pad_ref[pl.ds(r0 + SUBLANES - 2, n), :] * w[0:1, :]
    y = y + pad_ref[pl.ds(r0 + SUBLANES - 1, n), :] * w[1:2, :]
    return y + pad_ref[pl.ds(r0 + SUBLANES, n), :] * w[2:3, :]


def _fill_front_pad(pad_ref, strip_ref, s):
    pad_ref[0:SUBLANES, :] = jnp.zeros((SUBLANES, STRIP), F32)
    for r0 in range(0, s, CONV_CHUNK):
        pad_ref[pl.ds(SUBLANES + r0, CONV_CHUNK), :] = strip_ref[pl.ds(r0, CONV_CHUNK), :]


def _conv_act(up, conv_w, conv_b):
    s = up.shape[0]

    def body(ug_ref, uv_ref, wg_ref, wv_ref, bg_ref, bv_ref, act_ref, pg_ref, pv_ref):
        _fill_front_pad(pg_ref, ug_ref, s)
        _fill_front_pad(pv_ref, uv_ref, s)
        wg, wv, bg, bv = wg_ref[...], wv_ref[...], bg_ref[...], bv_ref[...]
        for r0 in range(0, s, CONV_CHUNK):
            hg = _conv_rows(pg_ref, wg, bg, r0, CONV_CHUNK)
            hv = _conv_rows(pv_ref, wv, bv, r0, CONV_CHUNK)
            act_ref[pl.ds(r0, CONV_CHUNK), :] = (hg * _sigmoid(hg) * hv).astype(BF16)

    strip = lambda off: pl.BlockSpec((s, STRIP), lambda j: (0, j + off))
    wsp = lambda off: pl.BlockSpec((3, STRIP), lambda j: (0, j + off))
    bsp = lambda off: pl.BlockSpec((1, STRIP), lambda j: (0, j + off))
    return _pcall(body, name="conv_act", grid=(N_STRIPS,),
                  in_specs=[strip(0), strip(N_STRIPS), wsp(0), wsp(N_STRIPS), bsp(0), bsp(N_STRIPS)],
                  out_specs=pl.BlockSpec((s, STRIP), lambda j: (0, j)), out_shape=_sds((s, D_FF), BF16),
                  scratch_shapes=[pltpu.VMEM((s + SUBLANES, STRIP), F32)] * 2,
                  dims=("parallel",))(up, up, conv_w, conv_w, conv_b, conv_b)


def _down_loss(act, w_down, x1, tgt):
    s = x1.shape[0]
    tm = _row_tile(s)

    def body(a_ref, w_ref, x1_ref, t_ref, dy_ref, dyb_ref, loss_ref):
        i = pl.program_id(0)

        @pl.when(i == 0)
        def _():
            loss_ref[...] = jnp.zeros_like(loss_ref)

        diff = x1_ref[...] + _dot(a_ref[...], w_ref[...]) - t_ref[...]
        dy = diff * (1.0 / D_MODEL)
        dy_ref[...] = dy
        dyb_ref[...] = dy.astype(BF16)
        loss_ref[...] += 0.5 * jnp.sum(diff * dy)

    row = lambda w: pl.BlockSpec((tm, w), lambda i: (i, 0))
    return _pcall(body, name="down_loss", grid=(s // tm,),
                  in_specs=[row(D_FF), pl.BlockSpec((D_FF, D_MODEL), lambda i: (0, 0)), row(D_MODEL), row(D_MODEL)],
                  out_specs=[row(D_MODEL), row(D_MODEL), pl.BlockSpec((SUBLANES, LANES), lambda i: (0, 0))],
                  out_shape=[_sds((s, D_MODEL)), _sds((s, D_MODEL), BF16), _sds((SUBLANES, LANES))],
                  dims=("arbitrary",))(act, w_down, x1, tgt)


def _conv_act_bwd(up, dact, conv_w, conv_b):
    s = up.shape[0]
    ch = CONV_CHUNK

    def body(ug_ref, uv_ref, da_ref, wg_ref, wv_ref, bg_ref, bv_ref, dup_ref, dcw_ref, pg_ref, pv_ref, dg_ref, dv_ref):
        _fill_front_pad(pg_ref, ug_ref, s)
        _fill_front_pad(pv_ref, uv_ref, s)
        zero = jnp.zeros((SUBLANES, STRIP), F32)
        dg_ref[pl.ds(s, SUBLANES), :] = zero
        dv_ref[pl.ds(s, SUBLANES), :] = zero
        wg, wv, bg, bv = wg_ref[...], wv_ref[...], bg_ref[...], bv_ref[...]
        tile_sum = lambda t: jnp.sum(t.reshape(ch // SUBLANES, SUBLANES, STRIP), axis=0)
        accs = [[zero] * 4, [zero] * 4]
        for r0 in range(0, s, ch):
            hg = _conv_rows(pg_ref, wg, bg, r0, ch)
            hv = _conv_rows(pv_ref, wv, bv, r0, ch)
            sg = _sigmoid(hg)
            da = da_ref[pl.ds(r0, ch), :]
            dhs = (da * hv * (sg * (1.0 + hg * (1.0 - sg))), da * (hg * sg))
            for half, (dh, d_ref, p_ref) in enumerate(zip(dhs, (dg_ref, dv_ref), (pg_ref, pv_ref))):
                d_ref[pl.ds(r0, ch), :] = dh
                for k in range(3):
                    accs[half][k] = accs[half][k] + tile_sum(dh * p_ref[pl.ds(r0 + SUBLANES - 2 + k, ch), :])
                accs[half][3] = accs[half][3] + tile_sum(dh)
        for half, (d_ref, w) in enumerate(((dg_ref, wg), (dv_ref, wv))):
            for r0 in range(0, s, ch):
                dup = (d_ref[pl.ds(r0, ch), :] * w[2:3, :] + d_ref[pl.ds(r0 + 1, ch), :] * w[1:2, :]
                       + d_ref[pl.ds(r0 + 2, ch), :] * w[0:1, :])
                dup_ref[half, pl.ds(r0, ch), :] = dup.astype(BF16)
            rid = lax.broadcasted_iota(jnp.int32, (SUBLANES, STRIP), 0)
            out = zero
            for k in range(4):
                out = jnp.where(rid == k, jnp.sum(accs[half][k], axis=0, keepdims=True), out)
            dcw_ref[half] = out

    strip = lambda off: pl.BlockSpec((s, STRIP), lambda j: (0, j + off))
    wsp = lambda off: pl.BlockSpec((3, STRIP), lambda j: (0, j + off))
    bsp = lambda off: pl.BlockSpec((1, STRIP), lambda j: (0, j + off))
    return _pcall(body, name="conv_act_bwd", grid=(N_STRIPS,),
                  in_specs=[strip(0), strip(N_STRIPS), strip(0), wsp(0), wsp(N_STRIPS), bsp(0), bsp(N_STRIPS)],
                  out_specs=[pl.BlockSpec((2, s, STRIP), lambda j: (0, 0, j)), pl.BlockSpec((2, SUBLANES, STRIP), lambda j: (0, 0, j))],
                  out_shape=[_sds((2, s, D_FF), BF16), _sds((2, SUBLANES, D_FF))],
                  scratch_shapes=[pltpu.VMEM((s + SUBLANES, STRIP), F32)] * 4,
                  dims=("parallel",))(up, up, dact, conv_w, conv_w, conv_b, conv_b)


def _mix_bwd(dy, dh2, x1, g_ffn, w_out, y, att, w_glu, b_glu, g_att, g_ssm):
    s = dy.shape[0]
    tm = _row_tile(s)

    def body(dy_ref, dh2_ref, x1_ref, gf_ref, wo_ref, y_ref, att_ref, wg_ref, bg_ref, ga_ref, gs_ref,
             dx1_ref, dx1b_ref, datt_ref, dys_ref, dwg_ref, dgf_ref, dga_ref, dgs_ref, dbg_ref):
        i = pl.program_id(0)

        @pl.when(i == 0)
        def _():
            for r in (dwg_ref, dgf_ref, dga_ref, dgs_ref, dbg_ref):
                r[...] = jnp.zeros_like(r)

        dxn, dgf = _rms_bwd(x1_ref[...], gf_ref[...], dh2_ref[...])
        dx1 = dy_ref[...] + dxn
        dx1_ref[...] = dx1
        dx1b = dx1.astype(BF16)
        dx1b_ref[...] = dx1b
        dgf_ref[...] += dgf
        dma = _dot_nt(dx1b, wo_ref[0:ATTN_W, :])
        dms = _dot_nt(dx1b, wo_ref[ATTN_W:D_MODEL, :])
        datt, dga = _rms_bwd(_merge_heads(att_ref), ga_ref[...], dma)
        _split_heads(datt_ref, datt)
        dga_ref[...] += dga
        yv = y_ref[...]
        ge, sg = _ssm_glu(yv, wg_ref[...], bg_ref[...])
        dssm, dgs = _rms_bwd(ge * sg, gs_ref[...], dms)
        dgs_ref[...] += dgs
        dgl = dssm * ge * sg * (1.0 - sg)
        dglb = dgl.astype(BF16)
        dge = dssm * sg + _dot_nt(dglb, wg_ref[...])
        dbg_ref[...] += jnp.sum(dgl, axis=0, keepdims=True)
        dwg_ref[...] += _dot_tn(ge.astype(BF16), dglb)
        dys_ref[...] = dge * _gelu_grad(yv)

    row = lambda w: pl.BlockSpec((tm, w), lambda i: (i, 0))
    const = lambda a, b: pl.BlockSpec((a, b), lambda i: (0, 0))
    heads = pl.BlockSpec((HEADS, tm, HEAD_DIM), lambda i: (0, i, 0))
    return _pcall(body, name="mix_bwd", grid=(s // tm,),
                  in_specs=[row(D_MODEL), row(D_MODEL), row(D_MODEL), const(1, D_MODEL), const(D_MODEL, D_MODEL), row(SSM_W),
                            heads, const(SSM_W, SSM_W), const(1, SSM_W), const(1, ATTN_W), const(1, SSM_W)],
                  out_specs=[row(D_MODEL), row(D_MODEL), heads, row(SSM_W), const(SSM_W, SSM_W), const(1, D_MODEL),
                             const(1, ATTN_W), const(1, SSM_W), const(1, SSM_W)],
                  out_shape=[_sds((s, D_MODEL)), _sds((s, D_MODEL), BF16), _sds((HEADS, s, HEAD_DIM)), _sds((s, SSM_W)),
                             _sds((SSM_W, SSM_W)), _sds((1, D_MODEL)), _sds((1, ATTN_W)), _sds((1, SSM_W)), _sds((1, SSM_W))],
                  dims=("arbitrary",))(dy, dh2, x1, g_ffn, w_out, y, att, w_glu, b_glu, g_att, g_ssm)


def _ssm_bwd(dys, uf, ub, xr, xi, bbr, bbi, ar, ai, ccr, cci, dsk, hosted=None):
    s = dys.shape[0]
    tm = _row_tile(s)
    nb = s // tm
    nt = tm // SUBLANES

    def body(dy_ref, u_ref, ub_ref, xr_ref, xi_ref, xrp_ref, xip_ref, bbr_ref, bbi_ref, ar_ref, ai_ref, ccr_ref,
             cci_ref, dsk_ref, du_ref, dbbr_ref, dbbi_ref, dccr_ref, dcci_ref, dar_ref, dai_ref, dd_ref,
             gr_s, gi_s, cr_s, ci_s, accr_s, acci_s):
        i = pl.program_id(1)
        first_block = i == nb - 1

        @pl.when(i == 0)
        def _():
            for r in (cr_s, ci_s, accr_s, acci_s, dbbr_ref, dbbi_ref, dccr_ref, dcci_ref, dd_ref):
                r[...] = jnp.zeros_like(r)

        dy = dy_ref[...]
        dyb = dy.astype(BF16)
        gr_s[...] = _dot_nt(dyb, ccr_ref[0])
        gi_s[...] = -_dot_nt(dyb, cci_ref[0])
        consts = _scan_consts(ar_ref[0], -ai_ref[0], CHUNK_S, True)
        row = lax.broadcasted_iota(jnp.int32, (SUBLANES, CHUNK_S), 0)

        def tile(kk, carry):
            cr, ci, accr, acci = carry
            k = nt - 1 - kk
            sl = pl.ds(pl.multiple_of(k * SUBLANES, SUBLANES), SUBLANES)
            gr, gi = _scan_tile(gr_s[sl, :], gi_s[sl, :], cr, ci, consts, True)
            gr_s[sl, :] = gr
            gi_s[sl, :] = gi
            slp = pl.ds(pl.multiple_of(jnp.maximum(k - 1, 0) * SUBLANES, SUBLANES), SUBLANES)
            inner = k > 0
            pr_t = jnp.where(inner, xr_ref[slp, :], xrp_ref[...])
            pi_t = jnp.where(inner, xi_ref[slp, :], xip_ref[...])
            live = jnp.logical_or(inner, jnp.logical_not(first_block))
            top_r = jnp.where(live, pltpu.roll(pr_t, 1, 0), 0.0)
            top_i = jnp.where(live, pltpu.roll(pi_t, 1, 0), 0.0)
            xpr = jnp.where(row == 0, top_r, pltpu.roll(xr_ref[sl, :], 1, 0))
            xpi = jnp.where(row == 0, top_i, pltpu.roll(xi_ref[sl, :], 1, 0))
            accr = accr + gr * xpr + gi * xpi
            acci = acci + gi * xpr - gr * xpi
            return gr[0:1, :], gi[0:1, :], accr, acci

        zeros = jnp.zeros((SUBLANES, CHUNK_S), F32)
        cr, ci, accr, acci = lax.fori_loop(0, nt, tile, (cr_s[...], ci_s[...], zeros, zeros))
        cr_s[...] = cr
        ci_s[...] = ci
        accr_s[...] += accr
        acci_s[...] += acci
        grb = gr_s[...].astype(BF16)
        gib = gi_s[...].astype(BF16)
        u_b = ub_ref[...]
        du_ref[...] = _dot_nt(grb, bbr_ref[0]) + _dot_nt(gib, bbi_ref[0]) + dsk_ref[...] * dy
        dbbr_ref[0] += _dot_tn(u_b, grb)
        dbbi_ref[0] += _dot_tn(u_b, gib)
        dccr_ref[0] += _dot_tn(xr_ref[...].astype(BF16), dyb)
        dcci_ref[0] -= _dot_tn(xi_ref[...].astype(BF16), dyb)
        dd_ref[...] += jnp.sum(dy * u_ref[...], axis=0, keepdims=True)

        @pl.when(i == nb - 1)
        def _():
            dar_ref[0] = jnp.sum(accr_s[...], axis=0, keepdims=True)
            dai_ref[0] = jnp.sum(acci_s[...], axis=0, keepdims=True)

    tiles_per_block = tm // SUBLANES
    rb = lambda i: nb - 1 - i
    wspec = lambda a, b: pl.BlockSpec((1, a, b), lambda j, i: (j, 0, 0))
    xblk = pl.BlockSpec((tm, CHUNK_S), lambda j, i: (rb(i), j))
    xprev = pl.BlockSpec((SUBLANES, CHUNK_S), lambda j, i: (jnp.maximum(rb(i) * tiles_per_block - 1, 0), j))
    ublk = pl.BlockSpec((tm, CHUNK_U), lambda j, i: (rb(i), j))
    first = lambda: jnp.logical_and(pl.program_id(0) == 0, pl.program_id(1) == 0)
    last = lambda: jnp.logical_and(pl.program_id(0) == SSM_CHUNKS - 1, pl.program_id(1) == nb - 1)
    return _host_pcall(
        body, hosted, first, last, n_in=14, n_out=8, n_scratch=6, name="ssm_bwd", grid=(SSM_CHUNKS, nb),
        in_specs=[ublk, ublk, ublk, xblk, xblk, xprev, xprev,
                  wspec(CHUNK_U, CHUNK_S), wspec(CHUNK_U, CHUNK_S), wspec(1, CHUNK_S), wspec(1, CHUNK_S),
                  wspec(CHUNK_S, CHUNK_U), wspec(CHUNK_S, CHUNK_U), pl.BlockSpec((1, CHUNK_U), lambda j, i: (0, j))],
        out_specs=[ublk, wspec(CHUNK_U, CHUNK_S), wspec(CHUNK_U, CHUNK_S), wspec(CHUNK_S, CHUNK_U),
                   wspec(CHUNK_S, CHUNK_U), wspec(1, CHUNK_S), wspec(1, CHUNK_S),
                   pl.BlockSpec((1, CHUNK_U), lambda j, i: (0, j))],
        out_shape=[_sds((s, SSM_W)), _sds((SSM_CHUNKS, CHUNK_U, CHUNK_S)), _sds((SSM_CHUNKS, CHUNK_U, CHUNK_S)),
                   _sds((SSM_CHUNKS, CHUNK_S, CHUNK_U)), _sds((SSM_CHUNKS, CHUNK_S, CHUNK_U)),
                   _sds((SSM_CHUNKS, 1, CHUNK_S)), _sds((SSM_CHUNKS, 1, CHUNK_S)), _sds((1, SSM_W))],
        scratch_shapes=[pltpu.VMEM((tm, CHUNK_S), F32)] * 2 + [pltpu.VMEM((1, CHUNK_S), F32)] * 2
                       + [pltpu.VMEM((SUBLANES, CHUNK_S), F32)] * 2,
        dims=("parallel", "arbitrary"), operands=(dys, uf, ub, xr, xi, xr, xi, bbr, bbi, ar, ai, ccr, cci, dsk))


def _attn_probs(q, ks, cs, lse, scale, diagonal):
    p = jnp.exp(_dot_nt(q, ks) * scale - cs - lse)
    if diagonal:
        tq, tk = p.shape
        causal = lax.broadcasted_iota(jnp.int32, (tq, tk), 1) <= lax.broadcasted_iota(jnp.int32, (tq, tk), 0)
        p = jnp.where(causal, p, 0.0)
    return p


def _attn_bwd(qh, kh, vh, crow, lse, doh, hosted=None):
    _, s, _ = qh.shape
    tq = _row_tile(s)
    nq = s // tq
    scale = HEAD_DIM ** -0.5
    hp = HEADS_PER_STEP

    def body(q_ref, k_ref, v_ref, c_ref, lse_ref, do_ref, dq_ref, dk_ref, dv_ref, dc_ref, p_s, dp_s):
        i = pl.program_id(1)

        @pl.when(i == 0)
        def _():
            for r in (dk_ref, dv_ref, dc_ref):
                r[...] = jnp.zeros_like(r)

        dobs = [do_ref[hh].astype(BF16) for hh in range(hp)]

        def first(j, dls, diagonal):
            off = pl.multiple_of(j * tq, tq)
            out = []
            for hh in range(hp):
                p = _attn_probs(q_ref[hh], k_ref[hh, pl.ds(off, tq), :], c_ref[hh, :, pl.ds(off, tq)], lse_ref[hh],
                                scale, diagonal)
                dp = _dot_nt(dobs[hh], v_ref[hh, pl.ds(off, tq), :])
                p_s[hh, j] = p
                dp_s[hh, j] = dp
                out.append(dls[hh] + jnp.sum(p * dp, axis=-1, keepdims=True))
            return tuple(out)

        zero_col = jnp.zeros((tq, 1), F32)
        dls = lax.fori_loop(0, i, lambda j, c: first(j, c, False), (zero_col,) * hp)
        dls = first(i, dls, True)

        def second(j, dqs):
            rows = pl.ds(pl.multiple_of(j * tq, tq), tq)
            out = []
            for hh in range(hp):
                p = p_s[hh, j]
                ds = p * (dp_s[hh, j] - dls[hh])
                dsb = ds.astype(BF16)
                dv_ref[hh, rows, :] += _dot_tn(p.astype(BF16), dobs[hh])
                dk_ref[hh, rows, :] += _dot_tn(dsb, q_ref[hh]) * scale
                dc_ref[hh, :, rows] -= jnp.sum(ds, axis=0, keepdims=True)
                out.append(dqs[hh] + _dot(dsb, k_ref[hh, rows, :]))
            return tuple(out)

        dqs = lax.fori_loop(0, i + 1, second, (jnp.zeros((tq, HEAD_DIM), F32),) * hp)
        for hh in range(hp):
            dq_ref[hh] = dqs[hh] * scale

    blk = pl.BlockSpec((hp, tq, HEAD_DIM), lambda h, i: (h, i, 0))
    full = pl.BlockSpec((hp, s, HEAD_DIM), lambda h, i: (h, 0, 0))
    crow_spec = pl.BlockSpec((hp, 1, s), lambda h, i: (h, 0, 0))
    nh = HEADS // hp
    first = lambda: jnp.logical_and(pl.program_id(0) == 0, pl.program_id(1) == 0)
    last = lambda: jnp.logical_and(pl.program_id(0) == nh - 1, pl.program_id(1) == nq - 1)
    return _host_pcall(body, hosted, first, last, n_in=6, n_out=4, n_scratch=2, name="attn_bwd", grid=(nh, nq),
                       in_specs=[blk, full, full, crow_spec, pl.BlockSpec((hp, tq, 1), lambda h, i: (h, i, 0)), blk],
                       out_specs=[blk, full, full, crow_spec],
                       out_shape=[_sds((HEADS, s, HEAD_DIM))] * 3 + [_sds((HEADS, 1, s))],
                       scratch_shapes=[pltpu.VMEM((hp, nq, tq, tq), F32)] * 2,
                       dims=("parallel", "arbitrary"), operands=(qh, kh, vh, crow, lse, doh))


def _prep_bwd(z, dqn, dkn, dv, du, dc, gq, gk, bf, gg):
    s = z.shape[0]
    tm = _row_tile(s)
    nb = s // tm

    def body(z_ref, dqn_ref, dkn_ref, dv_ref, du_ref, dc_ref, gq_ref, gk_ref, bf_ref, gg_ref,
             dz_ref, dgq_ref, dgk_ref, dbf_ref, carry_ref):
        i = pl.program_id(0)

        @pl.when(i == 0)
        def _():
            for r in (dgq_ref, dgk_ref, dbf_ref, carry_ref):
                r[...] = jnp.zeros_like(r)

        gg_m = gg_ref[...]

        def head_norm_bwd(t, g, dn):
            r = lax.rsqrt(_dot_exact_r(t * t, gg_m) * (1.0 / HEAD_DIM) + EPS)
            w = dn * g
            mean_wt = _dot_exact_r(w * t, gg_m) * (1.0 / HEAD_DIM)
            return r * w - t * (r * r * r) * mean_wt, jnp.sum(dn * t * r, axis=0, keepdims=True)

        dq, dgq = head_norm_bwd(z_ref[:, 0:ATTN_W], gq_ref[...], _merge_heads(dqn_ref))
        dk, dgk = head_norm_bwd(z_ref[:, ATTN_W:2 * ATTN_W], gk_ref[...], _merge_heads(dkn_ref))
        dgq_ref[...] += dgq
        dgk_ref[...] += dgk
        row = lax.broadcasted_iota(jnp.int32, (tm, tm), 0)
        col = lax.broadcasted_iota(jnp.int32, (tm, tm), 1)
        triu = (col >= row).astype(BF16)
        dlf = _dot_exact_l(triu, dc_ref[...]) + carry_ref[...]
        carry_ref[...] = dlf[0:1, :]
        df = dlf * _sigmoid(-_forget_logits(z_ref, bf_ref))
        dbf_ref[...] += jnp.sum(df, axis=0, keepdims=True)
        dz_ref[:, 0:ATTN_W] = dq.astype(BF16)
        dz_ref[:, ATTN_W:2 * ATTN_W] = dk.astype(BF16)
        dz_ref[:, 2 * ATTN_W:3 * ATTN_W] = _merge_heads(dv_ref).astype(BF16)
        tail = jnp.concatenate([df[:, :HEADS], du_ref[...], jnp.zeros((tm, Z_COLS - IN_COLS), F32)], axis=-1)
        dz_ref[:, F_COL0:Z_COLS] = tail.astype(BF16)

    row_spec = lambda w: pl.BlockSpec((tm, w), lambda i: (nb - 1 - i, 0))
    const = lambda shape: pl.BlockSpec(shape, lambda i: (0, 0))
    return _pcall(body, name="prep_bwd", grid=(nb,),
                  in_specs=[row_spec(Z_COLS)] + [pl.BlockSpec((HEADS, tm, HEAD_DIM), lambda i: (0, nb - 1 - i, 0))] * 3
                           + [row_spec(ATTN_W), row_spec(LANES), const((1, ATTN_W)),
                              const((1, ATTN_W)), const((1, LANES)), const((ATTN_W, ATTN_W))],
                  out_specs=[row_spec(Z_COLS), const((1, ATTN_W)), const((1, ATTN_W)), const((1, LANES))],
                  out_shape=[_sds((s, Z_COLS), BF16), _sds((1, ATTN_W)), _sds((1, ATTN_W)), _sds((1, LANES))],
                  scratch_shapes=[pltpu.VMEM((1, LANES), F32)], dims=("arbitrary",))(z, dqn, dkn, dv, du, dc, gq, gk, bf, gg)


def _in_norm_bwd(x, g_mix, dh, dx1):
    s = x.shape[0]
    tm = _row_tile(s)

    def body(x_ref, g_ref, dh_ref, dx1_ref, dx_ref, dg_ref):
        i = pl.program_id(0)

        @pl.when(i == 0)
        def _():
            dg_ref[...] = jnp.zeros_like(dg_ref)

        dxn, dg = _rms_bwd(x_ref[...], g_ref[...], dh_ref[...])
        dx_ref[...] = dx1_ref[...] + dxn
        dg_ref[...] += dg

    row = pl.BlockSpec((tm, D_MODEL), lambda i: (i, 0))
    vec = pl.BlockSpec((1, D_MODEL), lambda i: (0, 0))
    return _pcall(body, name="in_norm_bwd", grid=(s // tm,), in_specs=[row, vec, row, row], out_specs=[row, vec],
                  out_shape=[_sds((s, D_MODEL)), _sds((1, D_MODEL))], dims=("arbitrary",))(x, g_mix, dh, dx1)


def _adamw_refs(w_ref, g_ref, m_ref, v_ref, d_ref, mo_ref, vo_ref):
    gv = g_ref[...]
    mn = ADAM_B1 * m_ref[...] + (1.0 - ADAM_B1) * gv
    vn = ADAM_B2 * v_ref[...] + (1.0 - ADAM_B2) * (gv * gv)
    m_hat = mn / (1.0 - ADAM_B1 ** ADAM_STEP)
    v_hat = vn / (1.0 - ADAM_B2 ** ADAM_STEP)
    d_ref[...] = -ADAM_LR * (m_hat / (jnp.sqrt(v_hat) + ADAM_EPS) + ADAM_WD * w_ref[...])
    mo_ref[...] = mn
    vo_ref[...] = vn


def _adamw_small(ws, gs, ms, vs):
    n = len(ws)

    def body(*refs):
        ins, outs = refs[:4 * n], refs[4 * n:]
        for i in range(n):
            _adamw_refs(ins[i], ins[n + i], ins[2 * n + i], ins[3 * n + i], *outs[3 * i:3 * i + 3])

    vm = pl.BlockSpec(memory_space=pltpu.VMEM)
    out_shape = [_sds(w.shape) for w in ws for _ in range(3)]
    return _pallas(body, name="adamw_small", in_specs=[vm] * (4 * n), out_specs=[vm] * (3 * n), out_shape=out_shape,
                   compiler_params=pltpu.CompilerParams(vmem_limit_bytes=VMEM_LIMIT))(*ws, *gs, *ms, *vs)


def _adamw(w, g, m, v, *, name):
    r, c = w.shape
    tr = r
    for cand in (256, 176, 128, 64):
        if r > cand and r % cand == 0:
            tr = cand
            break

    def body(w_ref, g_ref, m_ref, v_ref, d_ref, mo_ref, vo_ref):
        _adamw_refs(w_ref, g_ref, m_ref, v_ref, d_ref, mo_ref, vo_ref)

    spec = pl.BlockSpec((tr, c), lambda i: (i, 0))
    return _pcall(body, name=name, grid=(r // tr,), in_specs=[spec] * 4, out_specs=[spec] * 3,
                  out_shape=[_sds((r, c))] * 3, dims=("parallel",))(w, g, m, v)


def _prefetch_call(body, *, name, grid, in_specs, out_specs, out_shape, operands):
    grid_spec = pltpu.PrefetchScalarGridSpec(num_scalar_prefetch=1, grid=grid, in_specs=in_specs, out_specs=out_specs)
    params = pltpu.CompilerParams(dimension_semantics=("parallel",) * len(grid), vmem_limit_bytes=VMEM_LIMIT)
    return _pallas(body, name=name, grid_spec=grid_spec, out_shape=out_shape, compiler_params=params)(*operands)


def _half_rows_tile(hr):
    return hr if hr <= 256 else 176 if hr % 176 == 0 else 256


def _add_half(g, landed, place, *, name):
    def body(place_ref, g_ref, l_ref, o_ref):
        own = g_ref[0] if len(g_ref.shape) == 4 else g_ref[...]
        o_ref[...] = (own + l_ref[...]).astype(BF16)

    if g.ndim == 4:
        _, _, hr, c = g.shape
        tr = _half_rows_tile(hr)
        blk = (1, tr, c)
        return _prefetch_call(
            body, name=name, grid=(N_CHIPS, hr // tr),
            in_specs=[pl.BlockSpec((1,) + blk, lambda j, i, p: (j, p[1], i, 0)), pl.BlockSpec(blk, lambda j, i, p: (j, i, 0))],
            out_specs=pl.BlockSpec(blk, lambda j, i, p: (j, i, 0)), out_shape=_sds(landed.shape, BF16),
            operands=(place, g, landed))
    hr, c = landed.shape
    tr, tc = 256, _tile(c, 2176)
    nb = hr // tr
    return _prefetch_call(
        body, name=name, grid=(nb, c // tc),
        in_specs=[pl.BlockSpec((tr, tc), lambda i, j, p: (p[1] * nb + i, j)), pl.BlockSpec((tr, tc), lambda i, j, p: (i, j))],
        out_specs=pl.BlockSpec((tr, tc), lambda i, j, p: (i, j)), out_shape=_sds(landed.shape, BF16),
        operands=(place, g, landed))


def _sum_chips(chip_sum, lands, place, *, name, tc, window_stride=0):
    _, hr, c = lands.shape
    tr = _half_rows_tile(hr)
    nb = hr // tr
    ncb = c // tc

    def body(place_ref, own_ref, a_ref, b_ref, c_ref, o_ref):
        own = own_ref[0] if len(own_ref.shape) == 3 else own_ref[...]
        o_ref[...] = ((own.astype(F32) + a_ref[0].astype(F32)) + b_ref[0].astype(F32)) + c_ref[0].astype(F32)

    land = lambda k: pl.BlockSpec((1, tr, tc), lambda i, j, p: ((p[0] + k) % N_CHIPS, i, j))
    if chip_sum.ndim == 3:
        own_spec = land(0)
    else:
        stride = window_stride // tc
        own_spec = pl.BlockSpec((tr, tc), lambda i, j, p: (i, p[0] * stride + j))
    return _prefetch_call(
        body, name=name, grid=(nb, ncb), in_specs=[own_spec, land(1), land(2), land(3)],
        out_specs=pl.BlockSpec((tr, tc), lambda i, j, p: (p[1] * nb + i, j)), out_shape=_sds((2 * hr, c)),
        operands=(place, chip_sum, lands, lands, lands))


_HBM = pl.BlockSpec(memory_space=pltpu.HBM)


def _place():
    x, y, c = lax.axis_index("x"), lax.axis_index("y"), lax.axis_index("c")
    chips = [(1 - x, y), (x, 1 - y), (1 - x, 1 - y)]
    return x, y, c, chips


def _rcopy(src, dst, send_sem, recv_sem, to):
    return pltpu.make_async_remote_copy(src_ref=src, dst_ref=dst, send_sem=send_sem, recv_sem=recv_sem,
                                        device_id=to, device_id_type=MESH)


N_BIG = 5
UP_COLS = 2 * D_FF // N_CHIPS
IN_WINDOW = 640
IN_STRIDE = 512


class _Hosted:
    def __init__(self, operands, out_shapes, n_sems, start, finish, aliases=None, local_sems=0):
        self.operands, self.out_shapes, self.n_sems = list(operands), list(out_shapes), n_sems
        self.start, self.finish, self.aliases, self.local_sems = start, finish, dict(aliases or {}), local_sems

    def scratch(self):
        return ([pltpu.SemaphoreType.DMA((self.n_sems,)), pltpu.SemaphoreType.DMA((self.n_sems,))]
                + [pltpu.SemaphoreType.DMA] * self.local_sems)


def _both(a, b):
    na, nao, nas = len(a.operands), len(a.out_shapes), len(a.scratch())

    def start(ins, outs, sems):
        a.start(ins[:na], outs[:nao], sems[:nas])
        b.start(ins[na:], outs[nao:], sems[nas:])

    def finish(ins, outs, sems):
        a.finish(ins[:na], outs[:nao], sems[:nas])
        b.finish(ins[na:], outs[nao:], sems[nas:])

    both = _Hosted(a.operands + b.operands, a.out_shapes + b.out_shapes, 0, start, finish,
                   aliases={**a.aliases, **{na + i: nao + o for i, o in b.aliases.items()}})
    both.scratch = lambda: a.scratch() + b.scratch()
    return both


def _run_hosted(hosted, *, name):
    n_in, n_out = len(hosted.operands), len(hosted.out_shapes)

    def body(*refs):
        parts = (refs[:n_in], refs[n_in:n_in + n_out], refs[n_in + n_out:])
        hosted.start(*parts)
        hosted.finish(*parts)

    return _pallas(body, name=name, in_specs=[_HBM] * n_in, out_specs=[_HBM] * n_out, out_shape=hosted.out_shapes,
                   input_output_aliases=hosted.aliases, scratch_shapes=hosted.scratch())(*hosted.operands)


def _host_pcall(core_body, hosted, first, last, *, n_in, n_out, n_scratch, name, grid, in_specs, out_specs, out_shape,
                scratch_shapes, dims, operands):
    if hosted is None:
        outs = _pcall(core_body, name=name, grid=grid, in_specs=in_specs, out_specs=out_specs, out_shape=out_shape,
                      scratch_shapes=scratch_shapes, dims=dims)(*operands)
        return outs, []
    hi, ho = len(hosted.operands), len(hosted.out_shapes)

    def body(*refs):
        a, b = n_in, n_in + hi
        c, d = b + n_out, b + n_out + ho
        e = d + n_scratch
        parts = (refs[a:b], refs[c:d], refs[e:])

        @pl.when(first())
        def _():
            hosted.start(*parts)

        core_body(*refs[:a], *refs[b:c], *refs[d:e])

        @pl.when(last())
        def _():
            hosted.finish(*parts)

    params = pltpu.CompilerParams(dimension_semantics=("arbitrary",) * len(grid), vmem_limit_bytes=VMEM_LIMIT)
    outs = _pallas(body, name=name, grid=grid, in_specs=list(in_specs) + [_HBM] * hi, out_specs=list(out_specs) + [_HBM] * ho,
                   out_shape=list(out_shape) + hosted.out_shapes, scratch_shapes=list(scratch_shapes) + hosted.scratch(),
                   input_output_aliases={n_in + a: n_out + b for a, b in hosted.aliases.items()},
                   compiler_params=params)(*operands, *hosted.operands)
    return outs[:n_out], outs[n_out:]


def _gather_slot(src, out, chip, hc):
    hr, cols = src.shape[0] // 2, src.shape[1]
    if len(out.shape) == 2:
        return out.at[pl.ds(hc * hr, hr), pl.ds(pl.multiple_of(chip * cols, LANES), cols)]
    return out.at[chip, pl.ds(hc * hr, hr), :]


def _gathered_shape(shard, by_cols):
    if by_cols:
        return _sds((shard.shape[0], N_CHIPS * shard.shape[1]), shard.dtype)
    return _sds((N_CHIPS,) + shard.shape, shard.dtype)


LOCAL_PARTS = 8


def _plan_gather_ici(shards, by_cols, whole=(), own_cols=()):
    n = len(shards)

    def copies(ins, outs, sems):
        send_sems, recv_sems = sems[0], sems[1]
        x, y, c, chips = _place()
        me = 2 * x + y
        sends, waits = [], []
        for w in range(n + len(whole)):
            for k, (cx, cy) in enumerate(chips):
                sem = (send_sems.at[3 * w + k], recv_sems.at[3 * w + k])
                if w < n:
                    hr = ins[w].shape[0] // 2
                    sends.append(_rcopy(ins[w].at[pl.ds(c * hr, hr), :], _gather_slot(ins[w], outs[w], me, c), *sem, (cx, cy, c)))
                    landed = _gather_slot(ins[w], outs[w], 2 * cx + cy, c)
                else:
                    sends.append(_rcopy(ins[w], outs[w].at[me], *sem, (cx, cy, c)))
                    landed = outs[w].at[2 * cx + cy]
                waits.append(_rcopy(landed, landed, *sem, (cx, cy, c)))
        local = []
        for i, w in enumerate(own_cols):
            rows, cols = ins[w].shape[0] // LOCAL_PARTS, ins[w].shape[1]
            for part in range(LOCAL_PARTS):
                band = pl.ds(part * rows, rows)
                local.append(pltpu.make_async_copy(
                    ins[w].at[band, :], outs[w].at[band, pl.ds(pl.multiple_of(me * cols, LANES), cols)],
                    sems[2 + i * LOCAL_PARTS + part]))
        return sends, waits, local

    def start(ins, outs, sems):
        sends, _, local = copies(ins, outs, sems)
        for cp in local + sends:
            cp.start()

    def finish(ins, outs, sems):
        sends, waits, local = copies(ins, outs, sems)
        for cp in waits:
            cp.wait_recv()
        for cp in sends:
            cp.wait_send()
        for cp in local:
            cp.wait()

    out_shapes = [_gathered_shape(s, bc) for s, bc in zip(shards, by_cols)] + [_sds((N_CHIPS,) + a.shape, a.dtype) for a in whole]
    return _Hosted(list(shards) + list(whole), out_shapes, 3 * (n + len(whole)), start, finish,
                   local_sems=LOCAL_PARTS * len(own_cols))


def _plan_gather_d2d(bufs, shard_shapes):
    n = len(bufs)

    def copies(ins, outs, sems):
        send_sems, recv_sems = sems
        x, y, c, chips = _place()
        sibling = (x, y, 1 - c)
        sends, waits = [], []
        for w in range(n):
            for k, (cx, cy) in enumerate(chips):
                sem = (send_sems.at[3 * w + k], recv_sems.at[3 * w + k])
                landed = _gather_slot(shard_shapes[w], outs[w], 2 * cx + cy, c)
                other = _gather_slot(shard_shapes[w], outs[w], 2 * cx + cy, 1 - c)
                sends.append(_rcopy(landed, landed, *sem, sibling))
                waits.append(_rcopy(other, other, *sem, sibling))
        return sends, waits

    def start(ins, outs, sems):
        for cp in copies(ins, outs, sems)[0]:
            cp.start()

    def finish(ins, outs, sems):
        sends, waits = copies(ins, outs, sems)
        for cp in waits:
            cp.wait_recv()
        for cp in sends:
            cp.wait_send()

    return _Hosted(bufs, [_sds(b.shape, b.dtype) for b in bufs], 3 * n, start, finish, aliases={w: w for w in range(n)})


def _plan_swap(grads):
    def copies(ins, outs, sems):
        send_sems, recv_sems = sems
        x, y, c, _ = _place()
        cps = []
        for w, g_ref in enumerate(ins):
            if len(g_ref.shape) == 4:
                theirs = g_ref.at[:, 1 - c]
            else:
                hr = g_ref.shape[0] // 2
                theirs = g_ref.at[pl.ds((1 - c) * hr, hr), :]
            cps.append(_rcopy(theirs, outs[w], send_sems.at[w], recv_sems.at[w], (x, y, 1 - c)))
        return cps

    def start(ins, outs, sems):
        for cp in copies(ins, outs, sems):
            cp.start()

    def finish(ins, outs, sems):
        for cp in copies(ins, outs, sems):
            cp.wait()

    out_shapes = [_sds((g.shape[0], g.shape[2], g.shape[3])) if g.ndim == 4 else _sds((g.shape[0] // 2, g.shape[1]))
                  for g in grads]
    return _Hosted(grads, out_shapes, len(grads), start, finish)


def _plan_scatter(chip_sums, windows):
    def copies(ins, outs, sems):
        send_sems, recv_sems = sems
        x, y, c, chips = _place()
        me = 2 * x + y
        sends, waits = [], []
        for w, s_ref in enumerate(ins):
            for k, (cx, cy) in enumerate(chips):
                tgt = 2 * cx + cy
                if windows[w] is not None:
                    stride, width = windows[w]
                    part = s_ref.at[:, pl.ds(pl.multiple_of(tgt * stride, LANES), width)]
                else:
                    part = s_ref.at[tgt]
                sem = (send_sems.at[3 * w + k], recv_sems.at[3 * w + k])
                sends.append(_rcopy(part, outs[w].at[me], *sem, (cx, cy, c)))
                slot = outs[w].at[tgt]
                waits.append(_rcopy(slot, slot, *sem, (cx, cy, c)))
        return sends, waits

    def start(ins, outs, sems):
        for cp in copies(ins, outs, sems)[0]:
            cp.start()

    def finish(ins, outs, sems):
        sends, waits = copies(ins, outs, sems)
        for cp in waits:
            cp.wait_recv()
        for cp in sends:
            cp.wait_send()

    out_shapes = [_sds((N_CHIPS, s.shape[0], win[1]), BF16) if win is not None else _sds(s.shape, BF16)
                  for s, win in zip(chip_sums, windows)]
    return _Hosted(chip_sums, out_shapes, 3 * len(chip_sums), start, finish)


def _plan_join(reds):
    def copies(ins, outs, sems):
        send_sems, recv_sems = sems
        x, y, c, _ = _place()
        sends, waits = [], []
        for w, out in enumerate(outs):
            hr = out.shape[0] // 2
            mine = out.at[pl.ds(c * hr, hr), :]
            theirs = out.at[pl.ds((1 - c) * hr, hr), :]
            sends.append(_rcopy(mine, mine, send_sems.at[w], recv_sems.at[w], (x, y, 1 - c)))
            waits.append(_rcopy(theirs, theirs, send_sems.at[w], recv_sems.at[w], (x, y, 1 - c)))
        return sends, waits

    def start(ins, outs, sems):
        for cp in copies(ins, outs, sems)[0]:
            cp.start()

    def finish(ins, outs, sems):
        sends, waits = copies(ins, outs, sems)
        for cp in waits:
            cp.wait_recv()
        for cp in sends:
            cp.wait_send()

    return _Hosted(reds, [_sds(r.shape) for r in reds], len(reds), start, finish, aliases={w: w for w in range(len(reds))})


def _allreduce_small(v):
    m_per = v.shape[0]

    def body(v_ref, out_ref, all_ref, send_sems, recv_sems, local_sem):
        x, y, c, chips = _place()
        me, sibling = (x, y, c), (x, y, 1 - c)

        def rows(px, py, pc):
            return all_ref.at[pl.ds((4 * px + 2 * py + pc) * m_per, m_per), :]

        def copy(k, block, to, src=None):
            return _rcopy(rows(*block) if src is None else src, rows(*block), send_sems.at[k], recv_sems.at[k], to)

        mine = pltpu.make_async_copy(v_ref, rows(*me), local_sem)
        mine.start()
        first = [copy(0, me, sibling, src=v_ref)]
        first += [copy(1 + k, me, (*chip, c), src=v_ref) for k, chip in enumerate(chips)]
        for cp in first:
            cp.start()
        passed = [copy(4 + k, (*chip, c), sibling) for k, chip in enumerate(chips)]
        for k, chip in enumerate(chips):
            copy(1 + k, (*chip, c), me).wait_recv()
            passed[k].start()
        copy(0, sibling, me).wait_recv()
        for k, chip in enumerate(chips):
            copy(4 + k, (*chip, 1 - c), me).wait_recv()
        for cp in first + passed:
            cp.wait_send()
        mine.wait()
        acc = all_ref[pl.ds(0, m_per), :]
        for d in range(1, 8):
            acc = acc + all_ref[pl.ds(d * m_per, m_per), :]
        out_ref[...] = acc

    vm = pl.BlockSpec(memory_space=pltpu.VMEM)
    return _pallas(body, name="allreduce_small", in_specs=[vm], out_specs=vm, out_shape=_sds((m_per, LANES)),
                          scratch_shapes=[pltpu.VMEM((8 * m_per, LANES), F32), pltpu.SemaphoreType.DMA((7,)),
                                          pltpu.SemaphoreType.DMA((7,)), pltpu.SemaphoreType.DMA],
                          compiler_params=pltpu.CompilerParams(vmem_limit_bytes=VMEM_LIMIT))(v)


def _block_diag(blocks):
    j, g, a, b = blocks.shape
    eye = jnp.eye(g, dtype=bool)[None, :, None, :, None]
    return jnp.where(eye, blocks[:, :, :, None, :], jnp.zeros((), blocks.dtype)).reshape(j, g * a, g * b)


def _diag_blocks(m, a, b):
    j = m.shape[0]
    g = m.shape[1] // a
    t = m.reshape(j, g, a, g, b)
    eye = jnp.eye(g, dtype=bool)[None, :, None, :, None]
    return jnp.sum(jnp.where(eye, t, 0.0), axis=3)


def _pack_rows(parts, rows, dtype):
    used = sum(p.shape[0] for p in parts)
    return jnp.concatenate([p.astype(dtype) for p in parts] + [jnp.zeros((rows - used, D_MODEL), dtype)], axis=0)


_SMALL = (("g_mix", (1024,)), ("b_f", (8,)), ("g_q", (64,)), ("g_k", (64,)), ("lambda_re", (32, 64)),
          ("lambda_im", (32, 64)), ("log_step", (32,)), ("b_re", (32, 64, 16)), ("b_im", (32, 64, 16)),
          ("c_re", (32, 16, 64)), ("c_im", (32, 16, 64)), ("d_skip", (32, 16)), ("b_glu", (512,)),
          ("g_attn_out", (512,)), ("g_ssm_out", (512,)), ("g_ffn", (1024,)), ("conv_b", (5632,)))


def _small_rows(shape):
    return -(-math.prod(shape) // LANES)


def _pack_small(arrs, extra=()):
    parts = []
    for a in list(arrs) + list(extra):
        flat = a.reshape(-1)
        rows = -(-flat.shape[0] // LANES)
        parts.append(jnp.pad(flat, (0, rows * LANES - flat.shape[0])).reshape(rows, LANES))
    total = sum(p.shape[0] for p in parts)
    pad = -total % SUBLANES
    if pad:
        parts.append(jnp.zeros((pad, LANES), F32))
    return jnp.concatenate(parts, axis=0)


def _unpack_small(buf, shapes):
    out, r = [], 0
    for shape in shapes:
        n = math.prod(shape)
        rows = -(-n // LANES)
        out.append(buf[r:r + rows].reshape(-1)[:n].reshape(shape))
        r += rows
    return out


def _halves(t):
    return t.reshape(N_CHIPS, 2, t.shape[0] // (2 * N_CHIPS), t.shape[1])


class _MeshComm:
    def __init__(self, args):
        x, y, self.core = lax.axis_index("x"), lax.axis_index("y"), lax.axis_index("c")
        self.chip = 2 * x + y
        self.place = jnp.stack([self.chip, self.core]).astype(jnp.int32)
        self.shards = {n: args[n].astype(BF16) for n in ("w_in", "w_glu", "w_out", "w_up", "w_down")}
        self.conv_w = args["conv_w"]

    def _own(self, stacked, mine):
        return lax.dynamic_update_slice(stacked, mine[None], (self.chip,) + (0,) * mine.ndim)

    def w_in(self):
        sh = self.shards["w_in"]
        (buf,) = _run_hosted(_plan_gather_ici([sh], [False]), name="gather_w_in")
        (buf,) = _run_hosted(_plan_gather_d2d([buf], [sh]), name="pass_w_in")
        whole = self._own(buf, sh).transpose(1, 0, 2).reshape(D_MODEL, IN_COLS)
        return jnp.pad(whole, ((0, 0), (0, Z_COLS - IN_COLS)))

    def gather_first(self):
        self.mid = [self.shards[n] for n in ("w_glu", "w_out", "w_down")]
        return _plan_gather_ici(self.mid, [False, False, False], whole=[self.conv_w])

    def gather_second(self, landed):
        self.g_cw = landed[3]
        return _both(_plan_gather_d2d(list(landed[:3]), self.mid),
                     _plan_gather_ici([self.shards["w_up"]], [True], own_cols=[0]))

    def weights(self, gathered):
        g_glu, g_out, g_down = gathered[:3]
        own = self._own
        return (own(g_glu, self.mid[0]).reshape(SSM_W, SSM_W), own(g_out, self.mid[1]).reshape(D_MODEL, D_MODEL),
                own(g_down, self.mid[2]).reshape(D_FF, D_MODEL),
                own(self.g_cw, self.conv_w).transpose(1, 0, 2).reshape(3, 2 * D_FF))

    def gather_third(self, gathered):
        return _plan_gather_d2d([gathered[3]], [self.shards["w_up"]])

    def w_up(self, passed):
        return passed[0]

    def swap(self, d_w_down, d_w_up, d_w_glu, d_w_out):
        self.early = [_halves(d_w_down), d_w_up, _halves(d_w_glu), _halves(d_w_out)]
        return _plan_swap(self.early)

    def scatter(self, landed):
        self.early_sums = [_add_half(g, l, self.place, name="add_" + n)
                           for g, l, n in zip(self.early, landed, ("w_down", "w_up", "w_glu", "w_out"))]
        return _plan_scatter(self.early_sums, [None, (UP_COLS, UP_COLS), None, None])

    def reduce(self, early_lands, d_w_in):
        d_in = d_w_in
        (landed,) = _run_hosted(_plan_swap([d_in]), name="swap_halves")
        sum_in = _add_half(d_in, landed, self.place, name="add_w_in")
        (land_in,) = _run_hosted(_plan_scatter([sum_in], [(IN_STRIDE, IN_WINDOW)]), name="scatter_chips")
        es, el = self.early_sums, early_lands
        todo = [(sum_in, land_in, "w_in", LANES, IN_STRIDE), (es[2], el[2], "w_glu", SSM_W, 0),
                (es[3], el[3], "w_out", D_MODEL, 0), (es[1], el[1], "w_up", UP_COLS, UP_COLS),
                (es[0], el[0], "w_down", D_MODEL, 0)]
        reds = _run_hosted(_plan_join([_sum_chips(s, l, self.place, name="sum_" + n, tc=tc, window_stride=st)
                                       for s, l, n, tc, st in todo]), name="join_halves")
        g_big = dict(zip(("w_in", "w_glu", "w_out", "w_up", "w_down"), reds))
        g_big["w_in"] = lax.dynamic_slice_in_dim(reds[0], 2 * self.chip, IN_COLS // N_CHIPS, axis=1)
        return g_big


def _local_step(x, tgt, p, comm):
    s = x.shape[0]
    row = lambda v: v.reshape(1, -1)
    g_mix, g_ffn = row(p["g_mix"]), row(p["g_ffn"])
    g_att, g_ssm, b_glu, conv_b = row(p["g_attn_out"]), row(p["g_ssm_out"]), row(p["b_glu"]), row(p["conv_b"])
    gq = row(jnp.tile(p["g_q"], HEADS))
    gk = row(jnp.tile(p["g_k"], HEADS))
    bf = row(jnp.pad(p["b_f"], (0, LANES - HEADS)))
    gg = jnp.kron(jnp.eye(HEADS, dtype=F32), jnp.ones((HEAD_DIM, HEAD_DIM), F32)).astype(BF16)
    dsk = row(p["d_skip"])

    rep = lambda a: jnp.repeat(a, SSM_GROUP, axis=0)
    lr, li = rep(p["lambda_re"]), rep(p["lambda_im"])
    ls = rep(jnp.broadcast_to(p["log_step"][:, None], (SSM_GROUPS, SSM_STATE)))
    bt_re = p["b_re"].transpose(0, 2, 1).reshape(_PARAM_SHAPE)
    bt_im = p["b_im"].transpose(0, 2, 1).reshape(_PARAM_SHAPE)
    a_re_rep, a_im_rep, bb_re, bb_im = _ssm_params(lr, li, ls, bt_re, bt_im)
    ar = a_re_rep[::SSM_GROUP].reshape(SSM_CHUNKS, 1, CHUNK_S)
    ai = a_im_rep[::SSM_GROUP].reshape(SSM_CHUNKS, 1, CHUNK_S)
    chunked = lambda t: t.reshape(SSM_CHUNKS, SSM_GROUPS // SSM_CHUNKS, SSM_GROUP, SSM_STATE)
    bbr = _block_diag(chunked(bb_re)).astype(BF16)
    bbi = _block_diag(chunked(bb_im)).astype(BF16)
    to_cc = lambda c: _block_diag(chunked(c).transpose(0, 1, 3, 2)).astype(BF16)
    ccr, cci = to_cc(p["c_re"]), to_cc(p["c_im"])

    w_in_r = comm.w_in()
    hb, z = _in_proj(x, g_mix, w_in_r)
    qh, kh, vh, ub, uf, c128 = _attn_prep(z, gq, gk, bf, gg)
    crow = c128[:, :HEADS].T.reshape(HEADS, 1, s)
    (oh, lse), landed = _attn_fwd(qh, kh, vh, crow, comm.gather_first())
    (xr, xi, y), gathered = _ssm_fwd(ub, uf, bbr, bbi, ar, ai, ccr, cci, dsk, comm.gather_second(landed))
    w_glu_b, w_out_b, w_down_b, conv_w_full = comm.weights(gathered)
    (x1, mixb, h2b), passed = _mix_out(y, oh, x, w_glu_b, b_glu, g_att, g_ssm, w_out_b, g_ffn, comm.gather_third(gathered))
    w_up_b = comm.w_up(passed)
    up = _mm(h2b, w_up_b, name="ffn_up", tm=1024, tn=1408, tk=1024)
    act = _conv_act(up, conv_w_full, conv_b)
    dy, dyb, loss_blk = _down_loss(act, w_down_b, x1, tgt)

    d_w_down = _mm(act, dyb, ta=True, name="d_w_down", tm=1408, tn=1024, tk=2048)
    dact = _mm(dyb, w_down_b, tb=True, name="d_act", tm=1024, tn=1408, tk=1024)
    dupb, dcw = _conv_act_bwd(up, dact, conv_w_full, conv_b)
    d_w_up = _mm(h2b, dupb, ta=True, b_parts=2, name="d_w_up", tm=1024, tn=1408, tk=2048)
    dh2 = _mm(dupb, w_up_b, tb=True, a_parts=2, name="d_h2", tm=1024, tn=1024, tk=1408)
    dx1, dx1b, doh, dys, d_w_glu, d_g_ffn, d_g_att, d_g_ssm, d_b_glu = _mix_bwd(
        dy, dh2, x1, g_ffn, w_out_b, y, oh, w_glu_b, b_glu, g_att, g_ssm)
    d_w_out = _mm(mixb, dx1b, ta=True, name="d_w_out", tm=1024, tn=1024, tk=2048)
    (du, dbbr, dbbi, dccr, dcci, dar, dai, dd), swapped = _ssm_bwd(dys, uf, ub, xr, xi, bbr, bbi, ar, ai, ccr, cci, dsk,
                                                                comm.swap(d_w_down, d_w_up, d_w_glu, d_w_out))
    (dqh, dkh, dvh, dcrow), early_lands = _attn_bwd(qh, kh, vh, crow, lse, doh, comm.scatter(swapped))
    dc128 = jnp.pad(dcrow.reshape(HEADS, s).T, ((0, 0), (0, LANES - HEADS)))
    dzb, d_gq, d_gk, d_bf = _prep_bwd(z, dqh, dkh, dvh, du, dc128, gq, gk, bf, gg)
    d_w_in_r = _mm(hb, dzb, ta=True, name="d_w_in", tm=512, tn=Z_COLS, tk=2048)
    dh = _mm(dzb, w_in_r, tb=True, name="d_h", tm=1024, tn=1024, tk=Z_COLS)
    dx, d_g_mix = _in_norm_bwd(x, g_mix, dh, dx1)

    unchunk = lambda t: t.reshape(_PARAM_SHAPE)
    dbb_re = unchunk(_diag_blocks(dbbr, SSM_GROUP, SSM_STATE))
    dbb_im = unchunk(_diag_blocks(dbbi, SSM_GROUP, SSM_STATE))
    first_row = (jnp.arange(_PARAM_SHAPE[0]) % SSM_GROUP == 0)[:, None]
    da_re = jnp.where(first_row, rep(dar.reshape(SSM_GROUPS, SSM_STATE)), 0.0)
    da_im = jnp.where(first_row, rep(dai.reshape(SSM_GROUPS, SSM_STATE)), 0.0)
    expand_t = (jnp.arange(SSM_GROUPS)[:, None] == (jnp.arange(_PARAM_SHAPE[0]) // SSM_GROUP)[None, :]).astype(BF16)
    d_lr, d_li, d_ls, d_bt_re, d_bt_im = _ssm_params_bwd(lr, li, ls, bt_re, bt_im, da_re, da_im, dbb_re, dbb_im, expand_t)
    from_bt = lambda t: t.reshape(SSM_GROUPS, SSM_GROUP, SSM_STATE).transpose(0, 2, 1)
    from_cc = lambda t: _diag_blocks(t, SSM_STATE, SSM_GROUP).transpose(0, 1, 3, 2).reshape(SSM_GROUPS, SSM_GROUP, SSM_STATE)

    small = {
        "g_mix": d_g_mix, "b_f": d_bf[0, :HEADS], "g_q": d_gq.reshape(HEADS, HEAD_DIM).sum(0),
        "g_k": d_gk.reshape(HEADS, HEAD_DIM).sum(0), "lambda_re": d_lr, "lambda_im": d_li, "log_step": d_ls,
        "b_re": from_bt(d_bt_re), "b_im": from_bt(d_bt_im), "c_re": from_cc(dccr), "c_im": from_cc(dcci),
        "d_skip": dd, "b_glu": d_b_glu, "g_attn_out": d_g_att, "g_ssm_out": d_g_ssm, "g_ffn": d_g_ffn,
        "conv_b": dcw[:, 3],
    }
    big = {"w_in": d_w_in_r, "w_glu": d_w_glu, "w_out": d_w_out, "w_up": d_w_up, "w_down": d_w_down}
    return loss_blk[0, 0], dx, big, small, dcw[:, 0:3].transpose(1, 0, 2).reshape(3, 2 * D_FF), early_lands


def kernel(x, g_mix, w_in, b_f, g_q, g_k, lambda_re, lambda_im, log_step, b_re, b_im, c_re, c_im, d_skip, w_glu, b_glu, g_attn_out, g_ssm_out, w_out, g_ffn, w_up, conv_w, conv_b, w_down, loss_target, m_g_mix, m_w_in, m_b_f, m_g_q, m_g_k, m_lambda_re, m_lambda_im, m_log_step, m_b_re, m_b_im, m_c_re, m_c_im, m_d_skip, m_w_glu, m_b_glu, m_g_attn_out, m_g_ssm_out, m_w_out, m_g_ffn, m_w_up, m_conv_w, m_conv_b, m_w_down, v_g_mix, v_w_in, v_b_f, v_g_q, v_g_k, v_lambda_re, v_lambda_im, v_log_step, v_b_re, v_b_im, v_c_re, v_c_im, v_d_skip, v_w_glu, v_b_glu, v_g_attn_out, v_g_ssm_out, v_w_out, v_g_ffn, v_w_up, v_conv_w, v_conv_b, v_w_down):
    args = dict(locals())
    order = ["g_mix", "w_in", "b_f", "g_q", "g_k", "lambda_re", "lambda_im", "log_step", "b_re", "b_im", "c_re", "c_im",
             "d_skip", "w_glu", "b_glu", "g_attn_out", "g_ssm_out", "w_out", "g_ffn", "w_up", "conv_w", "conv_b", "w_down"]
    comm = _MeshComm(args)
    chip = comm.chip
    loss_part, dx, big, small, d_conv_w, early_lands = _local_step(x[0], loss_target[0], args, comm)
    loss = lax.psum(loss_part, ("x", "y", "c"))

    g_big = comm.reduce(early_lands, big["w_in"])

    small_names = [n for n, _ in _SMALL]
    small_shapes = [sh for _, sh in _SMALL]
    gsum = _allreduce_small(_pack_small([small[n] for n in small_names], extra=[d_conv_w]))
    g_small = _unpack_small(gsum, small_shapes + [(3, 2 * D_FF)])
    g_conv_w = lax.dynamic_slice_in_dim(g_small[-1], chip * (2 * D_FF // N_CHIPS), 2 * D_FF // N_CHIPS, axis=1)
    g_small = dict(zip(small_names, g_small[:-1]))

    grad, delta, new_m, new_v = {}, {}, {}, {}
    for n in ("w_in", "w_glu", "w_out", "w_up", "w_down"):
        grad[n] = g_big[n]
        delta[n], new_m[n], new_v[n] = _adamw(args[n], g_big[n], args["m_" + n], args["v_" + n], name="adamw_" + n)
    grad["conv_w"] = g_conv_w
    delta["conv_w"], new_m["conv_w"], new_v["conv_w"] = _adamw(conv_w, g_conv_w, m_conv_w, v_conv_w, name="adamw_conv_w")
    stepped = _adamw_small([args[n] for n in small_names], [g_small[n] for n in small_names],
                           [args["m_" + n] for n in small_names], [args["v_" + n] for n in small_names])
    for i, n in enumerate(small_names):
        grad[n] = g_small[n]
        delta[n], new_m[n], new_v[n] = stepped[3 * i:3 * i + 3]

    return (loss, dx[None], *[grad[n] for n in order], *[delta[n] for n in order], *[new_m[n] for n in order],
            *[new_v[n] for n in order])
```

```python
import math

import jax
import jax.numpy as jnp
from jax import lax
from jax.experimental import pallas as pl
from jax.experimental.pallas import tpu as pltpu

F32 = jnp.float32
BF16 = jnp.bfloat16

D_MODEL = 1024
HEADS = 8
HEAD_DIM = 64
ATTN_W = 512
SSM_W = 512
SSM_GROUPS = 32
SSM_GROUP = 16
SSM_STATE = 64
N_STATE = SSM_GROUPS * SSM_STATE
D_FF = 2816
IN_COLS = 2056
Z_COLS = 2176
F_COL0 = 1536
U_COL0 = 1544
EPS = 1e-6
NEG_INF = -1e30
N_CHIPS = 4
LANES = 128
SUBLANES = 8
SSM_CHUNKS = 2
CHUNK_U = SSM_W // SSM_CHUNKS
CHUNK_S = N_STATE // SSM_CHUNKS
HEADS_PER_STEP = 4
STRIP = 128
N_STRIPS = D_FF // STRIP

ROWS_IN, ROWS_GLU, ROWS_OUT, ROWS_UP, ROWS_DOWN = 514, 64, 256, 1408, 704
OFF_GLU = ROWS_IN
OFF_OUT = OFF_GLU + ROWS_GLU
OFF_UP = OFF_OUT + ROWS_OUT
OFF_DOWN = OFF_UP + ROWS_UP
OFF_SPARE = OFF_DOWN + ROWS_DOWN
PACK_ROWS = 2976
HALF_ROWS = PACK_ROWS // 2
CONVW_ROWS = 9

ADAM_LR = 0.001
ADAM_B1 = 0.9
ADAM_B2 = 0.999
ADAM_EPS = 1e-08
ADAM_WD = 0.01
ADAM_STEP = 10

VMEM_LIMIT = 56 * 1024 * 1024
MESH = pl.DeviceIdType.MESH


def _pallas(body, **kw):
    return pl.pallas_call(body, **kw)


def _pcall(body, *, name, out_shape, in_specs, out_specs, grid=(), scratch_shapes=(), dims=None):
    params = pltpu.CompilerParams(dimension_semantics=dims, vmem_limit_bytes=VMEM_LIMIT)
    return _pallas(body, name=name, grid=grid, in_specs=in_specs, out_specs=out_specs,
                   out_shape=out_shape, scratch_shapes=scratch_shapes, compiler_params=params)


def _sds(shape, dtype=F32):
    return jax.ShapeDtypeStruct(shape, dtype)


def _dot(a, b):
    return jnp.dot(a, b, preferred_element_type=F32)


def _dot_nt(a, b):
    return lax.dot_general(a, b, (((1,), (1,)), ((), ())), preferred_element_type=F32)


def _dot_tn(a, b):
    return lax.dot_general(a, b, (((0,), (0,)), ((), ())), preferred_element_type=F32)


def _split3(x):
    hi = x.astype(BF16)
    r = x - hi.astype(F32)
    mid = r.astype(BF16)
    lo = (r - mid.astype(F32)).astype(BF16)
    return hi, mid, lo


def _dot_exact_r(x, m01):
    hi, mid, lo = _split3(x)
    return _dot(hi, m01) + _dot(mid, m01) + _dot(lo, m01)


def _dot_exact_l(m01, x):
    hi, mid, lo = _split3(x)
    return _dot(m01, hi) + _dot(m01, mid) + _dot(m01, lo)


def _sigmoid(x):
    return 1.0 / (1.0 + jnp.exp(-x))


def _rms(x, g):
    r = lax.rsqrt(jnp.mean(x * x, axis=-1, keepdims=True) + EPS)
    return x * r * g


def _rms_bwd(x, g, dy):
    r = lax.rsqrt(jnp.mean(x * x, axis=-1, keepdims=True) + EPS)
    w = dy * g
    dx = r * w - x * (r * r * r) * jnp.mean(w * x, axis=-1, keepdims=True)
    dg = jnp.sum(dy * x * r, axis=0, keepdims=True)
    return dx, dg


_GELU_K = math.sqrt(2.0 / math.pi)
_GELU_C = 0.044715


def _gelu(y):
    return y * (0.5 * (1.0 + jnp.tanh(_GELU_K * (y + _GELU_C * (y * y * y)))))


def _gelu_grad(y):
    t = jnp.tanh(_GELU_K * (y + _GELU_C * (y * y * y)))
    return 0.5 * (1.0 + t) + 0.5 * y * (1.0 - t * t) * (_GELU_K * (1.0 + 3.0 * _GELU_C * y * y))


def _tile(n, pref):
    if n <= pref:
        return n
    divs = [t for t in range(LANES, n + 1, LANES) if n % t == 0]
    below = [t for t in divs if t <= pref]
    if below and 2 * below[-1] >= pref:
        return below[-1]
    above = [t for t in divs if t > pref]
    return above[0] if above else n


def _row_tile(s):
    return min(256, s)


def _mm(a, b, *, name, tm, tn, tk, ta=False, tb=False, a_parts=1, b_parts=1):
    if a_parts > 1:
        m, kk = a.shape[1], a.shape[2] * a_parts
    elif ta:
        kk, m = a.shape
    else:
        m, kk = a.shape
    if b_parts > 1:
        n = b.shape[2] * b_parts
    else:
        n = b.shape[0] if tb else b.shape[1]
    tm, tn, tk = _tile(m, tm), _tile(n // b_parts, tn), _tile(kk // a_parts, tk)
    k_per, n_per = kk // a_parts // tk, n // b_parts // tn

    def body(a_ref, b_ref, o_ref):
        k = pl.program_id(2)
        if ta:
            part = _dot_tn(a_ref[...], b_ref[...])
        elif tb:
            part = _dot_nt(a_ref[...], b_ref[...])
        else:
            part = _dot(a_ref[...], b_ref[...])

        @pl.when(k == 0)
        def _():
            o_ref[...] = part

        @pl.when(k > 0)
        def _():
            o_ref[...] += part

    if a_parts > 1:
        a_spec = pl.BlockSpec((None, tm, tk), lambda i, j, k: (k // k_per, i, k % k_per))
    else:
        a_spec = pl.BlockSpec((tk, tm), lambda i, j, k: (k, i)) if ta else pl.BlockSpec((tm, tk), lambda i, j, k: (i, k))
    if b_parts > 1:
        b_spec = pl.BlockSpec((None, tk, tn), lambda i, j, k: (j // n_per, k, j % n_per))
    else:
        b_spec = pl.BlockSpec((tn, tk), lambda i, j, k: (j, k)) if tb else pl.BlockSpec((tk, tn), lambda i, j, k: (k, j))
    return _pcall(body, name=name, grid=(m // tm, n // tn, kk // tk), in_specs=[a_spec, b_spec],
                  out_specs=pl.BlockSpec((tm, tn), lambda i, j, k: (i, j)), out_shape=_sds((m, n)),
                  dims=("parallel", "parallel", "arbitrary"))(a, b)


def _in_proj(x, g_mix, w_in_r):
    s = x.shape[0]
    tm = _row_tile(s)

    def body(x_ref, g_ref, w_ref, h_ref, z_ref):
        h = _rms(x_ref[...], g_ref[...]).astype(BF16)
        h_ref[...] = h
        z_ref[...] = _dot(h, w_ref[...])

    return _pcall(body, name="in_proj", grid=(s // tm,),
                  in_specs=[pl.BlockSpec((tm, D_MODEL), lambda i: (i, 0)), pl.BlockSpec((1, D_MODEL), lambda i: (0, 0)),
                            pl.BlockSpec((D_MODEL, Z_COLS), lambda i: (0, 0))],
                  out_specs=[pl.BlockSpec((tm, D_MODEL), lambda i: (i, 0)), pl.BlockSpec((tm, Z_COLS), lambda i: (i, 0))],
                  out_shape=[_sds((s, D_MODEL), BF16), _sds((s, Z_COLS))], dims=("parallel",))(x, g_mix, w_in_r)


def _split_heads(ref, val):
    for h in range(HEADS):
        ref[h] = val[:, h * HEAD_DIM:(h + 1) * HEAD_DIM].astype(ref.dtype)


def _merge_heads(ref):
    return jnp.concatenate([ref[h].astype(F32) for h in range(HEADS)], axis=-1)


def _forget_logits(z_ref, bf_ref):
    fl = z_ref[:, F_COL0:F_COL0 + LANES] + bf_ref[...]
    return jnp.where(lax.broadcasted_iota(jnp.int32, fl.shape, 1) < HEADS, fl, 0.0)


def _attn_prep(z, gq, gk, bf, gg):
    s = z.shape[0]
    tm = _row_tile(s)

    def body(z_ref, gq_ref, gk_ref, bf_ref, gg_ref, qn_ref, kn_ref, vb_ref, ub_ref, uf_ref, c_ref, carry_ref):
        i = pl.program_id(0)

        @pl.when(i == 0)
        def _():
            carry_ref[...] = jnp.zeros_like(carry_ref)

        gg_m = gg_ref[...]

        def head_norm(t, g):
            ssq = _dot_exact_r(t * t, gg_m)
            return t * lax.rsqrt(ssq * (1.0 / HEAD_DIM) + EPS) * g

        _split_heads(qn_ref, head_norm(z_ref[:, 0:ATTN_W], gq_ref[...]))
        _split_heads(kn_ref, head_norm(z_ref[:, ATTN_W:2 * ATTN_W], gk_ref[...]))
        _split_heads(vb_ref, z_ref[:, 2 * ATTN_W:3 * ATTN_W])
        u = z_ref[:, U_COL0:U_COL0 + SSM_W]
        uf_ref[...] = u
        ub_ref[...] = u.astype(BF16)
        fl = _forget_logits(z_ref, bf_ref)
        lf = jnp.minimum(fl, 0.0) - jnp.log1p(jnp.exp(-jnp.abs(fl)))
        row = lax.broadcasted_iota(jnp.int32, (tm, tm), 0)
        col = lax.broadcasted_iota(jnp.int32, (tm, tm), 1)
        tri = (row >= col).astype(BF16)
        c = _dot_exact_l(tri, lf) + carry_ref[...]
        c_ref[...] = c
        carry_ref[...] = c[tm - 1:tm, :]

    row_spec = lambda w: pl.BlockSpec((tm, w), lambda i: (i, 0))
    const = lambda shape: pl.BlockSpec(shape, lambda i: (0, 0))
    heads = pl.BlockSpec((HEADS, tm, HEAD_DIM), lambda i: (0, i, 0))
    return _pcall(body, name="attn_prep", grid=(s // tm,),
                  in_specs=[row_spec(Z_COLS), const((1, ATTN_W)), const((1, ATTN_W)), const((1, LANES)), const((ATTN_W, ATTN_W))],
                  out_specs=[heads] * 3 + [row_spec(SSM_W), row_spec(SSM_W), row_spec(LANES)],
                  out_shape=[_sds((HEADS, s, HEAD_DIM), BF16)] * 3 + [_sds((s, SSM_W), BF16), _sds((s, SSM_W)), _sds((s, LANES))],
                  scratch_shapes=[pltpu.VMEM((1, LANES), F32)], dims=("arbitrary",))(z, gq, gk, bf, gg)


def _attn_fwd(qh, kh, vh, crow, hosted=None):
    _, s, _ = qh.shape
    tq = _row_tile(s)
    scale = HEAD_DIM ** -0.5

    hp = HEADS_PER_STEP
    nq = s // tq
    fold = lambda t, op: op(t[:, :tq // 2], t[:, tq // 2:])

    def body(q_ref, k_ref, v_ref, c_ref, o_ref, lse_ref, s_s):
        i = pl.program_id(1)

        def first(j, ms, diagonal):
            off = pl.multiple_of(j * tq, tq)
            out = []
            for hh in range(hp):
                sc = _dot_nt(q_ref[hh], k_ref[hh, pl.ds(off, tq), :]) * scale - c_ref[hh, :, pl.ds(off, tq)]
                if diagonal:
                    causal = lax.broadcasted_iota(jnp.int32, (tq, tq), 1) <= lax.broadcasted_iota(jnp.int32, (tq, tq), 0)
                    sc = jnp.where(causal, sc, NEG_INF)
                s_s[hh, j] = sc
                out.append(jnp.maximum(ms[hh], fold(sc, jnp.maximum)))
            return tuple(out)

        ms = lax.fori_loop(0, i, lambda j, c: first(j, c, False), (jnp.full((tq, tq // 2), NEG_INF, F32),) * hp)
        ms = [jnp.max(t, axis=-1, keepdims=True) for t in first(i, ms, True)]

        def second(j, carry):
            rows = pl.ds(pl.multiple_of(j * tq, tq), tq)
            out = []
            for hh in range(hp):
                ls, acc = carry[hh]
                p = jnp.exp(s_s[hh, j] - ms[hh])
                out.append((ls + fold(p, jnp.add), acc + _dot(p.astype(BF16), v_ref[hh, rows, :])))
            return tuple(out)

        zero = (jnp.zeros((tq, tq // 2), F32), jnp.zeros((tq, HEAD_DIM), F32))
        for hh, (ls, acc) in enumerate(lax.fori_loop(0, i + 1, second, (zero,) * hp)):
            l = jnp.sum(ls, axis=-1, keepdims=True)
            o_ref[hh] = acc / l
            lse_ref[hh] = ms[hh] + jnp.log(l)

    blk = pl.BlockSpec((hp, tq, HEAD_DIM), lambda h, i: (h, i, 0))
    full = pl.BlockSpec((hp, s, HEAD_DIM), lambda h, i: (h, 0, 0))
    nh = HEADS // hp
    first = lambda: jnp.logical_and(pl.program_id(0) == 0, pl.program_id(1) == 0)
    last = lambda: jnp.logical_and(pl.program_id(0) == nh - 1, pl.program_id(1) == nq - 1)
    return _host_pcall(body, hosted, first, last, n_in=4, n_out=2, n_scratch=1, name="attn_fwd", grid=(nh, nq),
                       in_specs=[blk, full, full, pl.BlockSpec((hp, 1, s), lambda h, i: (h, 0, 0))],
                       out_specs=[blk, pl.BlockSpec((hp, tq, 1), lambda h, i: (h, i, 0))],
                       out_shape=[_sds((HEADS, s, HEAD_DIM)), _sds((HEADS, s, 1))],
                       scratch_shapes=[pltpu.VMEM((hp, nq, tq, tq), F32)],
                       dims=("parallel", "parallel"), operands=(qh, kh, vh, crow))


def _ssm_param_fn(lr, li, ls, br, bi):
    step = jnp.exp(ls)
    er = jnp.exp(lr * step)
    ab_re = er * jnp.cos(li * step)
    ab_im = er * jnp.sin(li * step)
    num_re = ab_re - 1.0
    num_im = ab_im
    den = lr * lr + li * li
    f_re = (num_re * lr + num_im * li) / den
    f_im = (num_im * lr - num_re * li) / den
    bb_re = f_re * br - f_im * bi
    bb_im = f_re * bi + f_im * br
    return ab_re, ab_im, bb_re, bb_im


_PARAM_SHAPE = (SSM_GROUPS * SSM_GROUP, SSM_STATE)


def _ssm_params(lr, li, ls, br, bi):
    def body(lr_ref, li_ref, ls_ref, br_ref, bi_ref, ar_ref, ai_ref, bbr_ref, bbi_ref):
        ar, ai, bbr, bbi = _ssm_param_fn(lr_ref[...], li_ref[...], ls_ref[...], br_ref[...], bi_ref[...])
        ar_ref[...] = ar
        ai_ref[...] = ai
        bbr_ref[...] = bbr
        bbi_ref[...] = bbi

    spec = pl.BlockSpec(_PARAM_SHAPE, lambda: (0, 0))
    return _pcall(body, name="ssm_params", in_specs=[spec] * 5, out_specs=[spec] * 4,
                  out_shape=[_sds(_PARAM_SHAPE)] * 4)(lr, li, ls, br, bi)


def _ssm_params_bwd(lr, li, ls, br, bi, dar, dai, dbbr, dbbi, expand_t):
    def body(lr_ref, li_ref, ls_ref, br_ref, bi_ref, dar_ref, dai_ref, dbbr_ref, dbbi_ref, et_ref,
             dlr_ref, dli_ref, dls_ref, dbr_ref, dbi_ref):
        _, vjp = jax.vjp(_ssm_param_fn, lr_ref[...], li_ref[...], ls_ref[...], br_ref[...], bi_ref[...])
        dlr, dli, dls, dbr, dbi = vjp((dar_ref[...], dai_ref[...], dbbr_ref[...], dbbi_ref[...]))
        et = et_ref[...]
        dlr_ref[...] = _dot_exact_l(et, dlr)
        dli_ref[...] = _dot_exact_l(et, dli)
        dls_ref[...] = jnp.sum(_dot_exact_l(et, dls), axis=-1, keepdims=True)
        dbr_ref[...] = dbr
        dbi_ref[...] = dbi

    spec = pl.BlockSpec(_PARAM_SHAPE, lambda: (0, 0))
    gspec = pl.BlockSpec((SSM_GROUPS, SSM_STATE), lambda: (0, 0))
    return _pcall(body, name="ssm_params_bwd",
                  in_specs=[spec] * 9 + [pl.BlockSpec((SSM_GROUPS, _PARAM_SHAPE[0]), lambda: (0, 0))],
                  out_specs=[gspec, gspec, pl.BlockSpec((SSM_GROUPS, 1), lambda: (0, 0)), spec, spec],
                  out_shape=[_sds((SSM_GROUPS, SSM_STATE))] * 2 + [_sds((SSM_GROUPS, 1))] + [_sds(_PARAM_SHAPE)] * 2,
                  )(lr, li, ls, br, bi, dar, dai, dbbr, dbbi, expand_t)


def _cmul(ar, ai, br, bi):
    return ar * br - ai * bi, ar * bi + ai * br


def _scan_consts(ar, ai, width, reverse):
    row = lax.broadcasted_iota(jnp.int32, (SUBLANES, width), 0)
    pw = [(ar, ai)]
    for _ in range(SUBLANES - 1):
        pw.append(_cmul(pw[-1][0], pw[-1][1], ar, ai))
    steps = []
    for d in (1, 2, 4):
        keep = (row < SUBLANES - d) if reverse else (row >= d)
        steps.append((d, jnp.where(keep, pw[d - 1][0], 0.0), jnp.where(keep, pw[d - 1][1], 0.0)))
    pr = jnp.zeros((SUBLANES, width), F32)
    pi = jnp.zeros((SUBLANES, width), F32)
    for r in range(SUBLANES):
        e = (SUBLANES - r) if reverse else (r + 1)
        pr = jnp.where(row == r, pw[e - 1][0], pr)
        pi = jnp.where(row == r, pw[e - 1][1], pi)
    return steps, pr, pi


def _scan_tile(xr, xi, cr, ci, consts, reverse):
    steps, pr, pi = consts
    for d, mr, mi in steps:
        sh = (SUBLANES - d) if reverse else d
        sr = pltpu.roll(xr, sh, 0)
        si = pltpu.roll(xi, sh, 0)
        xr, xi = xr + mr * sr - mi * si, xi + mr * si + mi * sr
    return xr + pr * cr - pi * ci, xi + pr * ci + pi * cr


def _ssm_fwd(ub, uf, bbr, bbi, ar, ai, ccr, cci, dsk, hosted=None):
    s = ub.shape[0]
    tm = _row_tile(s)
    nt = tm // SUBLANES

    def body(ub_ref, u_ref, bbr_ref, bbi_ref, ar_ref, ai_ref, ccr_ref, cci_ref, dsk_ref,
             xr_ref, xi_ref, y_ref, cr_s, ci_s):
        i = pl.program_id(1)

        @pl.when(i == 0)
        def _():
            cr_s[...] = jnp.zeros_like(cr_s)
            ci_s[...] = jnp.zeros_like(ci_s)

        u_b = ub_ref[...]
        xr_ref[...] = _dot(u_b, bbr_ref[0])
        xi_ref[...] = _dot(u_b, bbi_ref[0])
        consts = _scan_consts(ar_ref[0], ai_ref[0], CHUNK_S, False)

        def tile(k, carry):
            cr, ci = carry
            sl = pl.ds(pl.multiple_of(k * SUBLANES, SUBLANES), SUBLANES)
            xr, xi = _scan_tile(xr_ref[sl, :], xi_ref[sl, :], cr, ci, consts, False)
            xr_ref[sl, :] = xr
            xi_ref[sl, :] = xi
            return xr[SUBLANES - 1:SUBLANES, :], xi[SUBLANES - 1:SUBLANES, :]

        cr, ci = lax.fori_loop(0, nt, tile, (cr_s[...], ci_s[...]))
        cr_s[...] = cr
        ci_s[...] = ci
        y_ref[...] = (_dot(xr_ref[...].astype(BF16), ccr_ref[0]) - _dot(xi_ref[...].astype(BF16), cci_ref[0])
                      + dsk_ref[...] * u_ref[...])

    wspec = lambda a, b: pl.BlockSpec((1, a, b), lambda j, i: (j, 0, 0))
    nb = s // tm
    first = lambda: jnp.logical_and(pl.program_id(0) == 0, pl.program_id(1) == 0)
    last = lambda: jnp.logical_and(pl.program_id(0) == SSM_CHUNKS - 1, pl.program_id(1) == nb - 1)
    return _host_pcall(
        body, hosted, first, last, n_in=9, n_out=3, n_scratch=2, name="ssm_fwd", grid=(SSM_CHUNKS, nb),
        in_specs=[pl.BlockSpec((tm, CHUNK_U), lambda j, i: (i, j)),
                  pl.BlockSpec((tm, CHUNK_U), lambda j, i: (i, j)),
                  wspec(CHUNK_U, CHUNK_S), wspec(CHUNK_U, CHUNK_S), wspec(1, CHUNK_S), wspec(1, CHUNK_S),
                  wspec(CHUNK_S, CHUNK_U), wspec(CHUNK_S, CHUNK_U),
                  pl.BlockSpec((1, CHUNK_U), lambda j, i: (0, j))],
        out_specs=[pl.BlockSpec((tm, CHUNK_S), lambda j, i: (i, j)), pl.BlockSpec((tm, CHUNK_S), lambda j, i: (i, j)),
                   pl.BlockSpec((tm, CHUNK_U), lambda j, i: (i, j))],
        out_shape=[_sds((s, N_STATE)), _sds((s, N_STATE)), _sds((s, SSM_W))],
        scratch_shapes=[pltpu.VMEM((1, CHUNK_S), F32)] * 2,
        dims=("parallel", "arbitrary"), operands=(ub, uf, bbr, bbi, ar, ai, ccr, cci, dsk))


def _ssm_glu(y, w_glu, b_glu):
    ge = _gelu(y)
    sg = _sigmoid(_dot(ge.astype(BF16), w_glu) + b_glu)
    return ge, sg


def _mix_out(y, att, x, w_glu, b_glu, g_att, g_ssm, w_out, g_ffn, hosted=None):
    s = x.shape[0]
    tm = _row_tile(s)

    def body(y_ref, att_ref, x_ref, wg_ref, bg_ref, ga_ref, gs_ref, wo_ref, gf_ref, x1_ref, mix_ref, h2_ref):
        ge, sg = _ssm_glu(y_ref[...], wg_ref[...], bg_ref[...])
        ms = _rms(ge * sg, gs_ref[...]).astype(BF16)
        ma = _rms(_merge_heads(att_ref), ga_ref[...]).astype(BF16)
        mix_ref[:, 0:ATTN_W] = ma
        mix_ref[:, ATTN_W:D_MODEL] = ms
        x1 = x_ref[...] + (_dot(ma, wo_ref[0:ATTN_W, :]) + _dot(ms, wo_ref[ATTN_W:D_MODEL, :]))
        x1_ref[...] = x1
        h2_ref[...] = _rms(x1, gf_ref[...]).astype(BF16)

    row = lambda w: pl.BlockSpec((tm, w), lambda i: (i, 0))
    const = lambda a, b: pl.BlockSpec((a, b), lambda i: (0, 0))
    nb = s // tm
    return _host_pcall(body, hosted, lambda: pl.program_id(0) == 0, lambda: pl.program_id(0) == nb - 1,
                       n_in=9, n_out=3, n_scratch=0, name="mix_out", grid=(nb,),
                       in_specs=[row(SSM_W), pl.BlockSpec((HEADS, tm, HEAD_DIM), lambda i: (0, i, 0)), row(D_MODEL),
                                 const(SSM_W, SSM_W), const(1, SSM_W),
                                 const(1, ATTN_W), const(1, SSM_W), const(D_MODEL, D_MODEL), const(1, D_MODEL)],
                       out_specs=[row(D_MODEL)] * 3,
                       out_shape=[_sds((s, D_MODEL)), _sds((s, D_MODEL), BF16), _sds((s, D_MODEL), BF16)],
                       scratch_shapes=[], dims=("parallel",), operands=(y, att, x, w_glu, b_glu, g_att, g_ssm, w_out, g_ffn))


CONV_CHUNK = 64


def _conv_rows(pad_ref, w, b, r0, n):
    y = b + pad_ref[pl.ds(r0 + SUBLANES - 2, n), :] * w[0:1, :]
    y = y + pad_ref[pl.ds(r0 + SUBLANES - 1, n), :] * w[1:2, :]
    return y + pad_ref[pl.ds(r0 + SUBLANES, n), :] * w[2:3, :]


def _fill_front_pad(pad_ref, strip_ref, s):
    pad_ref[0:SUBLANES, :] = jnp.zeros((SUBLANES, STRIP), F32)
    for r0 in range(0, s, CONV_CHUNK):
        pad_ref[pl.ds(SUBLANES + r0, CONV_CHUNK), :] = strip_ref[pl.ds(r0, CONV_CHUNK), :]


def _conv_act(up, conv_w, conv_b):
    s = up.shape[0]

    def body(ug_ref, uv_ref, wg_ref, wv_ref, bg_ref, bv_ref, act_ref, pg_ref, pv_ref):
        _fill_front_pad(pg_ref, ug_ref, s)
        _fill_front_pad(pv_ref, uv_ref, s)
        wg, wv, bg, bv = wg_ref[...], wv_ref[...], bg_ref[...], bv_ref[...]
        for r0 in range(0, s, CONV_CHUNK):
            hg = _conv_rows(pg_ref, wg, bg, r0, CONV_CHUNK)
            hv = _conv_rows(pv_ref, wv, bv, r0, CONV_CHUNK)
            act_ref[pl.ds(r0, CONV_CHUNK), :] = (hg * _sigmoid(hg) * hv).astype(BF16)

    strip = lambda off: pl.BlockSpec((s, STRIP), lambda j: (0, j + off))
    wsp = lambda off: pl.BlockSpec((3, STRIP), lambda j: (0, j + off))
    bsp = lambda off: pl.BlockSpec((1, STRIP), lambda j: (0, j + off))
    return _pcall(body, name="conv_act", grid=(N_STRIPS,),
                  in_specs=[strip(0), strip(N_STRIPS), wsp(0), wsp(N_STRIPS), bsp(0), bsp(N_STRIPS)],
                  out_specs=pl.BlockSpec((s, STRIP), lambda j: (0, j)), out_shape=_sds((s, D_FF), BF16),
                  scratch_shapes=[pltpu.VMEM((s + SUBLANES, STRIP), F32)] * 2,
                  dims=("parallel",))(up, up, conv_w, conv_w, conv_b, conv_b)


def _down_loss(act, w_down, x1, tgt):
    s = x1.shape[0]
    tm = _row_tile(s)

    def body(a_ref, w_ref, x1_ref, t_ref, dy_ref, dyb_ref, loss_ref):
        i = pl.program_id(0)

        @pl.when(i == 0)
        def _():
            loss_ref[...] = jnp.zeros_like(loss_ref)

        diff = x1_ref[...] + _dot(a_ref[...], w_ref[...]) - t_ref[...]
        dy = diff * (1.0 / D_MODEL)
        dy_ref[...] = dy
        dyb_ref[...] = dy.astype(BF16)
        loss_ref[...] += 0.5 * jnp.sum(diff * dy)

    row = lambda w: pl.BlockSpec((tm, w), lambda i: (i, 0))
    return _pcall(body, name="down_loss", grid=(s // tm,),
                  in_specs=[row(D_FF), pl.BlockSpec((D_FF, D_MODEL), lambda i: (0, 0)), row(D_MODEL), row(D_MODEL)],
                  out_specs=[row(D_MODEL), row(D_MODEL), pl.BlockSpec((SUBLANES, LANES), lambda i: (0, 0))],
                  out_shape=[_sds((s, D_MODEL)), _sds((s, D_MODEL), BF16), _sds((SUBLANES, LANES))],
                  dims=("arbitrary",))(act, w_down, x1, tgt)


def _conv_act_bwd(up, dact, conv_w, conv_b):
    s = up.shape[0]
    ch = CONV_CHUNK

    def body(ug_ref, uv_ref, da_ref, wg_ref, wv_ref, bg_ref, bv_ref, dup_ref, dcw_ref, pg_ref, pv_ref, dg_ref, dv_ref):
        _fill_front_pad(pg_ref, ug_ref, s)
        _fill_front_pad(pv_ref, uv_ref, s)
        zero = jnp.zeros((SUBLANES, STRIP), F32)
        dg_ref[pl.ds(s, SUBLANES), :] = zero
        dv_ref[pl.ds(s, SUBLANES), :] = zero
        wg, wv, bg, bv = wg_ref[...], wv_ref[...], bg_ref[...], bv_ref[...]
        tile_sum = lambda t: jnp.sum(t.reshape(ch // SUBLANES, SUBLANES, STRIP), axis=0)
        accs = [[zero] * 4, [zero] * 4]
        for r0 in range(0, s, ch):
            hg = _conv_rows(pg_ref, wg, bg, r0, ch)
            hv = _conv_rows(pv_ref, wv, bv, r0, ch)
            sg = _sigmoid(hg)
            da = da_ref[pl.ds(r0, ch), :]
            dhs = (da * hv * (sg * (1.0 + hg * (1.0 - sg))), da * (hg * sg))
            for half, (dh, d_ref, p_ref) in enumerate(zip(dhs, (dg_ref, dv_ref), (pg_ref, pv_ref))):
                d_ref[pl.ds(r0, ch), :] = dh
                for k in range(3):
                    accs[half][k] = accs[half][k] + tile_sum(dh * p_ref[pl.ds(r0 + SUBLANES - 2 + k, ch), :])
                accs[half][3] = accs[half][3] + tile_sum(dh)
        for half, (d_ref, w) in enumerate(((dg_ref, wg), (dv_ref, wv))):
            for r0 in range(0, s, ch):
                dup = (d_ref[pl.ds(r0, ch), :] * w[2:3, :] + d_ref[pl.ds(r0 + 1, ch), :] * w[1:2, :]
                       + d_ref[pl.ds(r0 + 2, ch), :] * w[0:1, :])
                dup_ref[half, pl.ds(r0, ch), :] = dup.astype(BF16)
            rid = lax.broadcasted_iota(jnp.int32, (SUBLANES, STRIP), 0)
            out = zero
            for k in range(4):
                out = jnp.where(rid == k, jnp.sum(accs[half][k], axis=0, keepdims=True), out)
            dcw_ref[half] = out

    strip = lambda off: pl.BlockSpec((s, STRIP), lambda j: (0, j + off))
    wsp = lambda off: pl.BlockSpec((3, STRIP), lambda j: (0, j + off))
    bsp = lambda off: pl.BlockSpec((1, STRIP), lambda j: (0, j + off))
    return _pcall(body, name="conv_act_bwd", grid=(N_STRIPS,),
                  in_specs=[strip(0), strip(N_STRIPS), strip(0), wsp(0), wsp(N_STRIPS), bsp(0), bsp(N_STRIPS)],
                  out_specs=[pl.BlockSpec((2, s, STRIP), lambda j: (0, 0, j)), pl.BlockSpec((2, SUBLANES, STRIP), lambda j: (0, 0, j))],
                  out_shape=[_sds((2, s, D_FF), BF16), _sds((2, SUBLANES, D_FF))],
                  scratch_shapes=[pltpu.VMEM((s + SUBLANES, STRIP), F32)] * 4,
                  dims=("parallel",))(up, up, dact, conv_w, conv_w, conv_b, conv_b)


def _mix_bwd(dy, dh2, x1, g_ffn, w_out, y, att, w_glu, b_glu, g_att, g_ssm):
    s = dy.shape[0]
    tm = _row_tile(s)

    def body(dy_ref, dh2_ref, x1_ref, gf_ref, wo_ref, y_ref, att_ref, wg_ref, bg_ref, ga_ref, gs_ref,
             dx1_ref, dx1b_ref, datt_ref, dys_ref, dwg_ref, dgf_ref, dga_ref, dgs_ref, dbg_ref):
        i = pl.program_id(0)

        @pl.when(i == 0)
        def _():
            for r in (dwg_ref, dgf_ref, dga_ref, dgs_ref, dbg_ref):
                r[...] = jnp.zeros_like(r)

        dxn, dgf = _rms_bwd(x1_ref[...], gf_ref[...], dh2_ref[...])
        dx1 = dy_ref[...] + dxn
        dx1_ref[...] = dx1
        dx1b = dx1.astype(BF16)
        dx1b_ref[...] = dx1b
        dgf_ref[...] += dgf
        dma = _dot_nt(dx1b, wo_ref[0:ATTN_W, :])
        dms = _dot_nt(dx1b, wo_ref[ATTN_W:D_MODEL, :])
        datt, dga = _rms_bwd(_merge_heads(att_ref), ga_ref[...], dma)
        _split_heads(datt_ref, datt)
        dga_ref[...] += dga
        yv = y_ref[...]
        ge, sg = _ssm_glu(yv, wg_ref[...], bg_ref[...])
        dssm, dgs = _rms_bwd(ge * sg, gs_ref[...], dms)
        dgs_ref[...] += dgs
        dgl = dssm * ge * sg * (1.0 - sg)
        dglb = dgl.astype(BF16)
        dge = dssm * sg + _dot_nt(dglb, wg_ref[...])
        dbg_ref[...] += jnp.sum(dgl, axis=0, keepdims=True)
        dwg_ref[...] += _dot_tn(ge.astype(BF16), dglb)
        dys_ref[...] = dge * _gelu_grad(yv)

    row = lambda w: pl.BlockSpec((tm, w), lambda i: (i, 0))
    const = lambda a, b: pl.BlockSpec((a, b), lambda i: (0, 0))
    heads = pl.BlockSpec((HEADS, tm, HEAD_DIM), lambda i: (0, i, 0))
    return _pcall(body, name="mix_bwd", grid=(s // tm,),
                  in_specs=[row(D_MODEL), row(D_MODEL), row(D_MODEL), const(1, D_MODEL), const(D_MODEL, D_MODEL), row(SSM_W),
                            heads, const(SSM_W, SSM_W), const(1, SSM_W), const(1, ATTN_W), const(1, SSM_W)],
                  out_specs=[row(D_MODEL), row(D_MODEL), heads, row(SSM_W), const(SSM_W, SSM_W), const(1, D_MODEL),
                             const(1, ATTN_W), const(1, SSM_W), const(1, SSM_W)],
                  out_shape=[_sds((s, D_MODEL)), _sds((s, D_MODEL), BF16), _sds((HEADS, s, HEAD_DIM)), _sds((s, SSM_W)),
                             _sds((SSM_W, SSM_W)), _sds((1, D_MODEL)), _sds((1, ATTN_W)), _sds((1, SSM_W)), _sds((1, SSM_W))],
                  dims=("arbitrary",))(dy, dh2, x1, g_ffn, w_out, y, att, w_glu, b_glu, g_att, g_ssm)


def _ssm_bwd(dys, uf, ub, xr, xi, bbr, bbi, ar, ai, ccr, cci, dsk, hosted=None):
    s = dys.shape[0]
    tm = _row_tile(s)
    nb = s // tm
    nt = tm // SUBLANES

    def body(dy_ref, u_ref, ub_ref, xr_ref, xi_ref, xrp_ref, xip_ref, bbr_ref, bbi_ref, ar_ref, ai_ref, ccr_ref,
             cci_ref, dsk_ref, du_ref, dbbr_ref, dbbi_ref, dccr_ref, dcci_ref, dar_ref, dai_ref, dd_ref,
             gr_s, gi_s, cr_s, ci_s, accr_s, acci_s):
        i = pl.program_id(1)
        first_block = i == nb - 1

        @pl.when(i == 0)
        def _():
            for r in (cr_s, ci_s, accr_s, acci_s, dbbr_ref, dbbi_ref, dccr_ref, dcci_ref, dd_ref):
                r[...] = jnp.zeros_like(r)

        dy = dy_ref[...]
        dyb = dy.astype(BF16)
        gr_s[...] = _dot_nt(dyb, ccr_ref[0])
        gi_s[...] = -_dot_nt(dyb, cci_ref[0])
        consts = _scan_consts(ar_ref[0], -ai_ref[0], CHUNK_S, True)
        row = lax.broadcasted_iota(jnp.int32, (SUBLANES, CHUNK_S), 0)

        def tile(kk, carry):
            cr, ci, accr, acci = carry
            k = nt - 1 - kk
            sl = pl.ds(pl.multiple_of(k * SUBLANES, SUBLANES), SUBLANES)
            gr, gi = _scan_tile(gr_s[sl, :], gi_s[sl, :], cr, ci, consts, True)
            gr_s[sl, :] = gr
            gi_s[sl, :] = gi
            slp = pl.ds(pl.multiple_of(jnp.maximum(k - 1, 0) * SUBLANES, SUBLANES), SUBLANES)
            inner = k > 0
            pr_t = jnp.where(inner, xr_ref[slp, :], xrp_ref[...])
            pi_t = jnp.where(inner, xi_ref[slp, :], xip_ref[...])
            live = jnp.logical_or(inner, jnp.logical_not(first_block))
            top_r = jnp.where(live, pltpu.roll(pr_t, 1, 0), 0.0)
            top_i = jnp.where(live, pltpu.roll(pi_t, 1, 0), 0.0)
            xpr = jnp.where(row == 0, top_r, pltpu.roll(xr_ref[sl, :], 1, 0))
            xpi = jnp.where(row == 0, top_i, pltpu.roll(xi_ref[sl, :], 1, 0))
            accr = accr + gr * xpr + gi * xpi
            acci = acci + gi * xpr - gr * xpi
            return gr[0:1, :], gi[0:1, :], accr, acci

        zeros = jnp.zeros((SUBLANES, CHUNK_S), F32)
        cr, ci, accr, acci = lax.fori_loop(0, nt, tile, (cr_s[...], ci_s[...], zeros, zeros))
        cr_s[...] = cr
        ci_s[...] = ci
        accr_s[...] += accr
        acci_s[...] += acci
        grb = gr_s[...].astype(BF16)
        gib = gi_s[...].astype(BF16)
        u_b = ub_ref[...]
        du_ref[...] = _dot_nt(grb, bbr_ref[0]) + _dot_nt(gib, bbi_ref[0]) + dsk_ref[...] * dy
        dbbr_ref[0] += _dot_tn(u_b, grb)
        dbbi_ref[0] += _dot_tn(u_b, gib)
        dccr_ref[0] += _dot_tn(xr_ref[...].astype(BF16), dyb)
        dcci_ref[0] -= _dot_tn(xi_ref[...].astype(BF16), dyb)
        dd_ref[...] += jnp.sum(dy * u_ref[...], axis=0, keepdims=True)

        @pl.when(i == nb - 1)
        def _():
            dar_ref[0] = jnp.sum(accr_s[...], axis=0, keepdims=True)
            dai_ref[0] = jnp.sum(acci_s[...], axis=0, keepdims=True)

    tiles_per_block = tm // SUBLANES
    rb = lambda i: nb - 1 - i
    wspec = lambda a, b: pl.BlockSpec((1, a, b), lambda j, i: (j, 0, 0))
    xblk = pl.BlockSpec((tm, CHUNK_S), lambda j, i: (rb(i), j))
    xprev = pl.BlockSpec((SUBLANES, CHUNK_S), lambda j, i: (jnp.maximum(rb(i) * tiles_per_block - 1, 0), j))
    ublk = pl.BlockSpec((tm, CHUNK_U), lambda j, i: (rb(i), j))
    first = lambda: jnp.logical_and(pl.program_id(0) == 0, pl.program_id(1) == 0)
    last = lambda: jnp.logical_and(pl.program_id(0) == SSM_CHUNKS - 1, pl.program_id(1) == nb - 1)
    return _host_pcall(
        body, hosted, first, last, n_in=14, n_out=8, n_scratch=6, name="ssm_bwd", grid=(SSM_CHUNKS, nb),
        in_specs=[ublk, ublk, ublk, xblk, xblk, xprev, xprev,
                  wspec(CHUNK_U, CHUNK_S), wspec(CHUNK_U, CHUNK_S), wspec(1, CHUNK_S), wspec(1, CHUNK_S),
                  wspec(CHUNK_S, CHUNK_U), wspec(CHUNK_S, CHUNK_U), pl.BlockSpec((1, CHUNK_U), lambda j, i: (0, j))],
        out_specs=[ublk, wspec(CHUNK_U, CHUNK_S), wspec(CHUNK_U, CHUNK_S), wspec(CHUNK_S, CHUNK_U),
                   wspec(CHUNK_S, CHUNK_U), wspec(1, CHUNK_S), wspec(1, CHUNK_S),
                   pl.BlockSpec((1, CHUNK_U), lambda j, i: (0, j))],
        out_shape=[_sds((s, SSM_W)), _sds((SSM_CHUNKS, CHUNK_U, CHUNK_S)), _sds((SSM_CHUNKS, CHUNK_U, CHUNK_S)),
                   _sds((SSM_CHUNKS, CHUNK_S, CHUNK_U)), _sds((SSM_CHUNKS, CHUNK_S, CHUNK_U)),
                   _sds((SSM_CHUNKS, 1, CHUNK_S)), _sds((SSM_CHUNKS, 1, CHUNK_S)), _sds((1, SSM_W))],
        scratch_shapes=[pltpu.VMEM((tm, CHUNK_S), F32)] * 2 + [pltpu.VMEM((1, CHUNK_S), F32)] * 2
                       + [pltpu.VMEM((SUBLANES, CHUNK_S), F32)] * 2,
        dims=("parallel", "arbitrary"), operands=(dys, uf, ub, xr, xi, xr, xi, bbr, bbi, ar, ai, ccr, cci, dsk))


def _attn_probs(q, ks, cs, lse, scale, diagonal):
    p = jnp.exp(_dot_nt(q, ks) * scale - cs - lse)
    if diagonal:
        tq, tk = p.shape
        causal = lax.broadcasted_iota(jnp.int32, (tq, tk), 1) <= lax.broadcasted_iota(jnp.int32, (tq, tk), 0)
        p = jnp.where(causal, p, 0.0)
    return p


def _attn_bwd(qh, kh, vh, crow, lse, doh, hosted=None):
    _, s, _ = qh.shape
    tq = _row_tile(s)
    nq = s // tq
    scale = HEAD_DIM ** -0.5
    hp = HEADS_PER_STEP

    def body(q_ref, k_ref, v_ref, c_ref, lse_ref, do_ref, dq_ref, dk_ref, dv_ref, dc_ref, p_s, dp_s):
        i = pl.program_id(1)

        @pl.when(i == 0)
        def _():
            for r in (dk_ref, dv_ref, dc_ref):
                r[...] = jnp.zeros_like(r)

        dobs = [do_ref[hh].astype(BF16) for hh in range(hp)]

        def first(j, dls, diagonal):
            off = pl.multiple_of(j * tq, tq)
            out = []
            for hh in range(hp):
                p = _attn_probs(q_ref[hh], k_ref[hh, pl.ds(off, tq), :], c_ref[hh, :, pl.ds(off, tq)], lse_ref[hh],
                                scale, diagonal)
                dp = _dot_nt(dobs[hh], v_ref[hh, pl.ds(off, tq), :])
                p_s[hh, j] = p
                dp_s[hh, j] = dp
                out.append(dls[hh] + jnp.sum(p * dp, axis=-1, keepdims=True))
            return tuple(out)

        zero_col = jnp.zeros((tq, 1), F32)
        dls = lax.fori_loop(0, i, lambda j, c: first(j, c, False), (zero_col,) * hp)
        dls = first(i, dls, True)

        def second(j, dqs):
            rows = pl.ds(pl.multiple_of(j * tq, tq), tq)
            out = []
            for hh in range(hp):
                p = p_s[hh, j]
                ds = p * (dp_s[hh, j] - dls[hh])
                dsb = ds.astype(BF16)
                dv_ref[hh, rows, :] += _dot_tn(p.astype(BF16), dobs[hh])
                dk_ref[hh, rows, :] += _dot_tn(dsb, q_ref[hh]) * scale
                dc_ref[hh, :, rows] -= jnp.sum(ds, axis=0, keepdims=True)
                out.append(dqs[hh] + _dot(dsb, k_ref[hh, rows, :]))
            return tuple(out)

        dqs = lax.fori_loop(0, i + 1, second, (jnp.zeros((tq, HEAD_DIM), F32),) * hp)
        for hh in range(hp):
            dq_ref[hh] = dqs[hh] * scale

    blk = pl.BlockSpec((hp, tq, HEAD_DIM), lambda h, i: (h, i, 0))
    full = pl.BlockSpec((hp, s, HEAD_DIM), lambda h, i: (h, 0, 0))
    crow_spec = pl.BlockSpec((hp, 1, s), lambda h, i: (h, 0, 0))
    nh = HEADS // hp
    first = lambda: jnp.logical_and(pl.program_id(0) == 0, pl.program_id(1) == 0)
    last = lambda: jnp.logical_and(pl.program_id(0) == nh - 1, pl.program_id(1) == nq - 1)
    return _host_pcall(body, hosted, first, last, n_in=6, n_out=4, n_scratch=2, name="attn_bwd", grid=(nh, nq),
                       in_specs=[blk, full, full, crow_spec, pl.BlockSpec((hp, tq, 1), lambda h, i: (h, i, 0)), blk],
                       out_specs=[blk, full, full, crow_spec],
                       out_shape=[_sds((HEADS, s, HEAD_DIM))] * 3 + [_sds((HEADS, 1, s))],
                       scratch_shapes=[pltpu.VMEM((hp, nq, tq, tq), F32)] * 2,
                       dims=("parallel", "arbitrary"), operands=(qh, kh, vh, crow, lse, doh))


def _prep_bwd(z, dqn, dkn, dv, du, dc, gq, gk, bf, gg):
    s = z.shape[0]
    tm = _row_tile(s)
    nb = s // tm

    def body(z_ref, dqn_ref, dkn_ref, dv_ref, du_ref, dc_ref, gq_ref, gk_ref, bf_ref, gg_ref,
             dz_ref, dgq_ref, dgk_ref, dbf_ref, carry_ref):
        i = pl.program_id(0)

        @pl.when(i == 0)
        def _():
            for r in (dgq_ref, dgk_ref, dbf_ref, carry_ref):
                r[...] = jnp.zeros_like(r)

        gg_m = gg_ref[...]

        def head_norm_bwd(t, g, dn):
            r = lax.rsqrt(_dot_exact_r(t * t, gg_m) * (1.0 / HEAD_DIM) + EPS)
            w = dn * g
            mean_wt = _dot_exact_r(w * t, gg_m) * (1.0 / HEAD_DIM)
            return r * w - t * (r * r * r) * mean_wt, jnp.sum(dn * t * r, axis=0, keepdims=True)

        dq, dgq = head_norm_bwd(z_ref[:, 0:ATTN_W], gq_ref[...], _merge_heads(dqn_ref))
        dk, dgk = head_norm_bwd(z_ref[:, ATTN_W:2 * ATTN_W], gk_ref[...], _merge_heads(dkn_ref))
        dgq_ref[...] += dgq
        dgk_ref[...] += dgk
        row = lax.broadcasted_iota(jnp.int32, (tm, tm), 0)
        col = lax.broadcasted_iota(jnp.int32, (tm, tm), 1)
        triu = (col >= row).astype(BF16)
        dlf = _dot_exact_l(triu, dc_ref[...]) + carry_ref[...]
        carry_ref[...] = dlf[0:1, :]
        df = dlf * _sigmoid(-_forget_logits(z_ref, bf_ref))
        dbf_ref[...] += jnp.sum(df, axis=0, keepdims=True)
        dz_ref[:, 0:ATTN_W] = dq.astype(BF16)
        dz_ref[:, ATTN_W:2 * ATTN_W] = dk.astype(BF16)
        dz_ref[:, 2 * ATTN_W:3 * ATTN_W] = _merge_heads(dv_ref).astype(BF16)
        tail = jnp.concatenate([df[:, :HEADS], du_ref[...], jnp.zeros((tm, Z_COLS - IN_COLS), F32)], axis=-1)
        dz_ref[:, F_COL0:Z_COLS] = tail.astype(BF16)

    row_spec = lambda w: pl.BlockSpec((tm, w), lambda i: (nb - 1 - i, 0))
    const = lambda shape: pl.BlockSpec(shape, lambda i: (0, 0))
    return _pcall(body, name="prep_bwd", grid=(nb,),
                  in_specs=[row_spec(Z_COLS)] + [pl.BlockSpec((HEADS, tm, HEAD_DIM), lambda i: (0, nb - 1 - i, 0))] * 3
                           + [row_spec(ATTN_W), row_spec(LANES), const((1, ATTN_W)),
                              const((1, ATTN_W)), const((1, LANES)), const((ATTN_W, ATTN_W))],
                  out_specs=[row_spec(Z_COLS), const((1, ATTN_W)), const((1, ATTN_W)), const((1, LANES))],
                  out_shape=[_sds((s, Z_COLS), BF16), _sds((1, ATTN_W)), _sds((1, ATTN_W)), _sds((1, LANES))],
                  scratch_shapes=[pltpu.VMEM((1, LANES), F32)], dims=("arbitrary",))(z, dqn, dkn, dv, du, dc, gq, gk, bf, gg)


def _in_norm_bwd(x, g_mix, dh, dx1):
    s = x.shape[0]
    tm = _row_tile(s)

    def body(x_ref, g_ref, dh_ref, dx1_ref, dx_ref, dg_ref):
        i = pl.program_id(0)

        @pl.when(i == 0)
        def _():
            dg_ref[...] = jnp.zeros_like(dg_ref)

        dxn, dg = _rms_bwd(x_ref[...], g_ref[...], dh_ref[...])
        dx_ref[...] = dx1_ref[...] + dxn
        dg_ref[...] += dg

    row = pl.BlockSpec((tm, D_MODEL), lambda i: (i, 0))
    vec = pl.BlockSpec((1, D_MODEL), lambda i: (0, 0))
    return _pcall(body, name="in_norm_bwd", grid=(s // tm,), in_specs=[row, vec, row, row], out_specs=[row, vec],
                  out_shape=[_sds((s, D_MODEL)), _sds((1, D_MODEL))], dims=("arbitrary",))(x, g_mix, dh, dx1)


def _adamw_refs(w_ref, g_ref, m_ref, v_ref, d_ref, mo_ref, vo_ref):
    gv = g_ref[...]
    mn = ADAM_B1 * m_ref[...] + (1.0 - ADAM_B1) * gv
    vn = ADAM_B2 * v_ref[...] + (1.0 - ADAM_B2) * (gv * gv)
    m_hat = mn / (1.0 - ADAM_B1 ** ADAM_STEP)
    v_hat = vn / (1.0 - ADAM_B2 ** ADAM_STEP)
    d_ref[...] = -ADAM_LR * (m_hat / (jnp.sqrt(v_hat) + ADAM_EPS) + ADAM_WD * w_ref[...])
    mo_ref[...] = mn
    vo_ref[...] = vn


def _adamw_small(ws, gs, ms, vs):
    n = len(ws)

    def body(*refs):
        ins, outs = refs[:4 * n], refs[4 * n:]
        for i in range(n):
            _adamw_refs(ins[i], ins[n + i], ins[2 * n + i], ins[3 * n + i], *outs[3 * i:3 * i + 3])

    vm = pl.BlockSpec(memory_space=pltpu.VMEM)
    out_shape = [_sds(w.shape) for w in ws for _ in range(3)]
    return _pallas(body, name="adamw_small", in_specs=[vm] * (4 * n), out_specs=[vm] * (3 * n), out_shape=out_shape,
                   compiler_params=pltpu.CompilerParams(vmem_limit_bytes=VMEM_LIMIT))(*ws, *gs, *ms, *vs)


def _adamw(w, g, m, v, *, name):
    r, c = w.shape
    tr = r
    for cand in (256, 176, 128, 64):
        if r > cand and r % cand == 0:
            tr = cand
            break

    def body(w_ref, g_ref, m_ref, v_ref, d_ref, mo_ref, vo_ref):
        _adamw_refs(w_ref, g_ref, m_ref, v_ref, d_ref, mo_ref, vo_ref)

    spec = pl.BlockSpec((tr, c), lambda i: (i, 0))
    return _pcall(body, name=name, grid=(r // tr,), in_specs=[spec] * 4, out_specs=[spec] * 3,
                  out_shape=[_sds((r, c))] * 3, dims=("parallel",))(w, g, m, v)


def _prefetch_call(body, *, name, grid, in_specs, out_specs, out_shape, operands):
    grid_spec = pltpu.PrefetchScalarGridSpec(num_scalar_prefetch=1, grid=grid, in_specs=in_specs, out_specs=out_specs)
    params = pltpu.CompilerParams(dimension_semantics=("parallel",) * len(grid), vmem_limit_bytes=VMEM_LIMIT)
    return _pallas(body, name=name, grid_spec=grid_spec, out_shape=out_shape, compiler_params=params)(*operands)


def _place_cols(buf, shard, place):
    rows, cols = shard.shape
    tr = 256

    def body(place_ref, s_ref, b_ref, o_ref):
        o_ref[...] = s_ref[...]

    grid_spec = pltpu.PrefetchScalarGridSpec(
        num_scalar_prefetch=1, grid=(rows // tr,),
        in_specs=[pl.BlockSpec((tr, cols), lambda i, p: (i, 0)), pl.BlockSpec(memory_space=pltpu.HBM)],
        out_specs=pl.BlockSpec((tr, cols), lambda i, p: (i, p[0])))
    return _pallas(body, name="place_own_cols", grid_spec=grid_spec, out_shape=_sds(buf.shape, buf.dtype),
                   input_output_aliases={2: 0},
                   compiler_params=pltpu.CompilerParams(dimension_semantics=("parallel",),
                                                        vmem_limit_bytes=VMEM_LIMIT))(place, shard, buf)


def _half_rows_tile(hr):
    return hr if hr <= 256 else 176 if hr % 176 == 0 else 256


def _add_half(g, landed, place, *, name):
    def body(place_ref, g_ref, l_ref, o_ref):
        own = g_ref[0] if len(g_ref.shape) == 4 else g_ref[...]
        o_ref[...] = (own + l_ref[...]).astype(BF16)

    if g.ndim == 4:
        _, _, hr, c = g.shape
        tr = _half_rows_tile(hr)
        blk = (1, tr, c)
        return _prefetch_call(
            body, name=name, grid=(N_CHIPS, hr // tr),
            in_specs=[pl.BlockSpec((1,) + blk, lambda j, i, p: (j, p[1], i, 0)), pl.BlockSpec(blk, lambda j, i, p: (j, i, 0))],
            out_specs=pl.BlockSpec(blk, lambda j, i, p: (j, i, 0)), out_shape=_sds(landed.shape, BF16),
            operands=(place, g, landed))
    hr, c = landed.shape
    tr, tc = 256, _tile(c, 2176)
    nb = hr // tr
    return _prefetch_call(
        body, name=name, grid=(nb, c // tc),
        in_specs=[pl.BlockSpec((tr, tc), lambda i, j, p: (p[1] * nb + i, j)), pl.BlockSpec((tr, tc), lambda i, j, p: (i, j))],
        out_specs=pl.BlockSpec((tr, tc), lambda i, j, p: (i, j)), out_shape=_sds(landed.shape, BF16),
        operands=(place, g, landed))


def _sum_chips(chip_sum, lands, place, *, name, tc, window_stride=0):
    _, hr, c = lands.shape
    tr = _half_rows_tile(hr)
    nb = hr // tr
    ncb = c // tc

    def body(place_ref, own_ref, a_ref, b_ref, c_ref, o_ref):
        own = own_ref[0] if len(own_ref.shape) == 3 else own_ref[...]
        o_ref[...] = ((own.astype(F32) + a_ref[0].astype(F32)) + b_ref[0].astype(F32)) + c_ref[0].astype(F32)

    land = lambda k: pl.BlockSpec((1, tr, tc), lambda i, j, p: ((p[0] + k) % N_CHIPS, i, j))
    if chip_sum.ndim == 3:
        own_spec = land(0)
    else:
        stride = window_stride // tc
        own_spec = pl.BlockSpec((tr, tc), lambda i, j, p: (i, p[0] * stride + j))
    return _prefetch_call(
        body, name=name, grid=(nb, ncb), in_specs=[own_spec, land(1), land(2), land(3)],
        out_specs=pl.BlockSpec((tr, tc), lambda i, j, p: (p[1] * nb + i, j)), out_shape=_sds((2 * hr, c)),
        operands=(place, chip_sum, lands, lands, lands))


_HBM = pl.BlockSpec(memory_space=pltpu.HBM)


def _place():
    x, y, c = lax.axis_index("x"), lax.axis_index("y"), lax.axis_index("c")
    chips = [(1 - x, y), (x, 1 - y), (1 - x, 1 - y)]
    return x, y, c, chips


def _rcopy(src, dst, send_sem, recv_sem, to):
    return pltpu.make_async_remote_copy(src_ref=src, dst_ref=dst, send_sem=send_sem, recv_sem=recv_sem,
                                        device_id=to, device_id_type=MESH)


N_BIG = 5
UP_COLS = 2 * D_FF // N_CHIPS
IN_WINDOW = 640
IN_STRIDE = 512


class _Hosted:
    def __init__(self, operands, out_shapes, n_sems, start, finish, aliases=None, local_sems=0):
        self.operands, self.out_shapes, self.n_sems = list(operands), list(out_shapes), n_sems
        self.start, self.finish, self.aliases, self.local_sems = start, finish, dict(aliases or {}), local_sems

    def scratch(self):
        return ([pltpu.SemaphoreType.DMA((self.n_sems,)), pltpu.SemaphoreType.DMA((self.n_sems,))]
                + [pltpu.SemaphoreType.DMA] * self.local_sems)


def _both(a, b):
    na, nao, nas = len(a.operands), len(a.out_shapes), len(a.scratch())

    def start(ins, outs, sems):
        a.start(ins[:na], outs[:nao], sems[:nas])
        b.start(ins[na:], outs[nao:], sems[nas:])

    def finish(ins, outs, sems):
        a.finish(ins[:na], outs[:nao], sems[:nas])
        b.finish(ins[na:], outs[nao:], sems[nas:])

    both = _Hosted(a.operands + b.operands, a.out_shapes + b.out_shapes, 0, start, finish,
                   aliases={**a.aliases, **{na + i: nao + o for i, o in b.aliases.items()}})
    both.scratch = lambda: a.scratch() + b.scratch()
    return both


def _run_hosted(hosted, *, name):
    n_in, n_out = len(hosted.operands), len(hosted.out_shapes)

    def body(*refs):
        parts = (refs[:n_in], refs[n_in:n_in + n_out], refs[n_in + n_out:])
        hosted.start(*parts)
        hosted.finish(*parts)

    return _pallas(body, name=name, in_specs=[_HBM] * n_in, out_specs=[_HBM] * n_out, out_shape=hosted.out_shapes,
                   input_output_aliases=hosted.aliases, scratch_shapes=hosted.scratch())(*hosted.operands)


def _host_pcall(core_body, hosted, first, last, *, n_in, n_out, n_scratch, name, grid, in_specs, out_specs, out_shape,
                scratch_shapes, dims, operands):
    if hosted is None:
        outs = _pcall(core_body, name=name, grid=grid, in_specs=in_specs, out_specs=out_specs, out_shape=out_shape,
                      scratch_shapes=scratch_shapes, dims=dims)(*operands)
        return outs, []
    hi, ho = len(hosted.operands), len(hosted.out_shapes)

    def body(*refs):
        a, b = n_in, n_in + hi
        c, d = b + n_out, b + n_out + ho
        e = d + n_scratch
        parts = (refs[a:b], refs[c:d], refs[e:])

        @pl.when(first())
        def _():
            hosted.start(*parts)

        core_body(*refs[:a], *refs[b:c], *refs[d:e])

        @pl.when(last())
        def _():
            hosted.finish(*parts)

    params = pltpu.CompilerParams(dimension_semantics=("arbitrary",) * len(grid), vmem_limit_bytes=VMEM_LIMIT)
    outs = _pallas(body, name=name, grid=grid, in_specs=list(in_specs) + [_HBM] * hi, out_specs=list(out_specs) + [_HBM] * ho,
                   out_shape=list(out_shape) + hosted.out_shapes, scratch_shapes=list(scratch_shapes) + hosted.scratch(),
                   input_output_aliases={n_in + a: n_out + b for a, b in hosted.aliases.items()},
                   compiler_params=params)(*operands, *hosted.operands)
    return outs[:n_out], outs[n_out:]


def _gather_slot(src, out, chip, hc):
    hr, cols = src.shape[0] // 2, src.shape[1]
    if len(out.shape) == 2:
        return out.at[pl.ds(hc * hr, hr), pl.ds(pl.multiple_of(chip * cols, LANES), cols)]
    return out.at[chip, pl.ds(hc * hr, hr), :]


def _gathered_shape(shard, by_cols):
    if by_cols:
        return _sds((shard.shape[0], N_CHIPS * shard.shape[1]), shard.dtype)
    return _sds((N_CHIPS,) + shard.shape, shard.dtype)


LOCAL_PARTS = 8


def _plan_gather_ici(shards, by_cols, whole=(), own_cols=()):
    n = len(shards)

    def copies(ins, outs, sems):
        send_sems, recv_sems = sems[0], sems[1]
        x, y, c, chips = _place()
        me = 2 * x + y
        sends, waits = [], []
        for w in range(n + len(whole)):
            for k, (cx, cy) in enumerate(chips):
                sem = (send_sems.at[3 * w + k], recv_sems.at[3 * w + k])
                if w < n:
                    hr = ins[w].shape[0] // 2
                    sends.append(_rcopy(ins[w].at[pl.ds(c * hr, hr), :], _gather_slot(ins[w], outs[w], me, c), *sem, (cx, cy, c)))
                    landed = _gather_slot(ins[w], outs[w], 2 * cx + cy, c)
                else:
                    sends.append(_rcopy(ins[w], outs[w].at[me], *sem, (cx, cy, c)))
                    landed = outs[w].at[2 * cx + cy]
                waits.append(_rcopy(landed, landed, *sem, (cx, cy, c)))
        local = []
        for i, w in enumerate(own_cols):
            rows, cols = ins[w].shape[0] // LOCAL_PARTS, ins[w].shape[1]
            for part in range(LOCAL_PARTS):
                band = pl.ds(part * rows, rows)
                local.append(pltpu.make_async_copy(
                    ins[w].at[band, :], outs[w].at[band, pl.ds(pl.multiple_of(me * cols, LANES), cols)],
                    sems[2 + i * LOCAL_PARTS + part]))
        return sends, waits, local

    def start(ins, outs, sems):
        sends, _, local = copies(ins, outs, sems)
        for cp in local + sends:
            cp.start()

    def finish(ins, outs, sems):
        sends, waits, local = copies(ins, outs, sems)
        for cp in waits:
            cp.wait_recv()
        for cp in sends:
            cp.wait_send()
        for cp in local:
            cp.wait()

    out_shapes = [_gathered_shape(s, bc) for s, bc in zip(shards, by_cols)] + [_sds((N_CHIPS,) + a.shape, a.dtype) for a in whole]
    return _Hosted(list(shards) + list(whole), out_shapes, 3 * (n + len(whole)), start, finish,
                   local_sems=LOCAL_PARTS * len(own_cols))


def _plan_gather_d2d(bufs, shard_shapes):
    n = len(bufs)

    def copies(ins, outs, sems):
        send_sems, recv_sems = sems
        x, y, c, chips = _place()
        sibling = (x, y, 1 - c)
        sends, waits = [], []
        for w in range(n):
            for k, (cx, cy) in enumerate(chips):
                sem = (send_sems.at[3 * w + k], recv_sems.at[3 * w + k])
                landed = _gather_slot(shard_shapes[w], outs[w], 2 * cx + cy, c)
                other = _gather_slot(shard_shapes[w], outs[w], 2 * cx + cy, 1 - c)
                sends.append(_rcopy(landed, landed, *sem, sibling))
                waits.append(_rcopy(other, other, *sem, sibling))
        return sends, waits

    def start(ins, outs, sems):
        for cp in copies(ins, outs, sems)[0]:
            cp.start()

    def finish(ins, outs, sems):
        sends, waits = copies(ins, outs, sems)
        for cp in waits:
            cp.wait_recv()
        for cp in sends:
            cp.wait_send()

    return _Hosted(bufs, [_sds(b.shape, b.dtype) for b in bufs], 3 * n, start, finish, aliases={w: w for w in range(n)})


def _plan_swap(grads):
    def copies(ins, outs, sems):
        send_sems, recv_sems = sems
        x, y, c, _ = _place()
        cps = []
        for w, g_ref in enumerate(ins):
            if len(g_ref.shape) == 4:
                theirs = g_ref.at[:, 1 - c]
            else:
                hr = g_ref.shape[0] // 2
                theirs = g_ref.at[pl.ds((1 - c) * hr, hr), :]
            cps.append(_rcopy(theirs, outs[w], send_sems.at[w], recv_sems.at[w], (x, y, 1 - c)))
        return cps

    def start(ins, outs, sems):
        for cp in copies(ins, outs, sems):
            cp.start()

    def finish(ins, outs, sems):
        for cp in copies(ins, outs, sems):
            cp.wait()

    out_shapes = [_sds((g.shape[0], g.shape[2], g.shape[3])) if g.ndim == 4 else _sds((g.shape[0] // 2, g.shape[1]))
                  for g in grads]
    return _Hosted(grads, out_shapes, len(grads), start, finish)


def _plan_scatter(chip_sums, windows):
    def copies(ins, outs, sems):
        send_sems, recv_sems = sems
        x, y, c, chips = _place()
        me = 2 * x + y
        sends, waits = [], []
        for w, s_ref in enumerate(ins):
            for k, (cx, cy) in enumerate(chips):
                tgt = 2 * cx + cy
                if windows[w] is not None:
                    stride, width = windows[w]
                    part = s_ref.at[:, pl.ds(pl.multiple_of(tgt * stride, LANES), width)]
                else:
                    part = s_ref.at[tgt]
                sem = (send_sems.at[3 * w + k], recv_sems.at[3 * w + k])
                sends.append(_rcopy(part, outs[w].at[me], *sem, (cx, cy, c)))
                slot = outs[w].at[tgt]
                waits.append(_rcopy(slot, slot, *sem, (cx, cy, c)))
        return sends, waits

    def start(ins, outs, sems):
        for cp in copies(ins, outs, sems)[0]:
            cp.start()

    def finish(ins, outs, sems):
        sends, waits = copies(ins, outs, sems)
        for cp in waits:
            cp.wait_recv()
        for cp in sends:
            cp.wait_send()

    out_shapes = [_sds((N_CHIPS, s.shape[0], win[1]), BF16) if win is not None else _sds(s.shape, BF16)
                  for s, win in zip(chip_sums, windows)]
    return _Hosted(chip_sums, out_shapes, 3 * len(chip_sums), start, finish)


def _plan_join(reds):
    def copies(ins, outs, sems):
        send_sems, recv_sems = sems
        x, y, c, _ = _place()
        sends, waits = [], []
        for w, out in enumerate(outs):
            hr = out.shape[0] // 2
            mine = out.at[pl.ds(c * hr, hr), :]
            theirs = out.at[pl.ds((1 - c) * hr, hr), :]
            sends.append(_rcopy(mine, mine, send_sems.at[w], recv_sems.at[w], (x, y, 1 - c)))
            waits.append(_rcopy(theirs, theirs, send_sems.at[w], recv_sems.at[w], (x, y, 1 - c)))
        return sends, waits

    def start(ins, outs, sems):
        for cp in copies(ins, outs, sems)[0]:
            cp.start()

    def finish(ins, outs, sems):
        sends, waits = copies(ins, outs, sems)
        for cp in waits:
            cp.wait_recv()
        for cp in sends:
            cp.wait_send()

    return _Hosted(reds, [_sds(r.shape) for r in reds], len(reds), start, finish, aliases={w: w for w in range(len(reds))})


def _allreduce_small(v):
    m_per = v.shape[0]

    def body(v_ref, out_ref, all_ref, send_sems, recv_sems, local_sem):
        x, y, c, chips = _place()
        me, sibling = (x, y, c), (x, y, 1 - c)

        def rows(px, py, pc):
            return all_ref.at[pl.ds((4 * px + 2 * py + pc) * m_per, m_per), :]

        def copy(k, block, to, src=None):
            return _rcopy(rows(*block) if src is None else src, rows(*block), send_sems.at[k], recv_sems.at[k], to)

        mine = pltpu.make_async_copy(v_ref, rows(*me), local_sem)
        mine.start()
        first = [copy(0, me, sibling, src=v_ref)]
        first += [copy(1 + k, me, (*chip, c), src=v_ref) for k, chip in enumerate(chips)]
        for cp in first:
            cp.start()
        passed = [copy(4 + k, (*chip, c), sibling) for k, chip in enumerate(chips)]
        for k, chip in enumerate(chips):
            copy(1 + k, (*chip, c), me).wait_recv()
            passed[k].start()
        copy(0, sibling, me).wait_recv()
        for k, chip in enumerate(chips):
            copy(4 + k, (*chip, 1 - c), me).wait_recv()
        for cp in first + passed:
            cp.wait_send()
        mine.wait()
        acc = all_ref[pl.ds(0, m_per), :]
        for d in range(1, 8):
            acc = acc + all_ref[pl.ds(d * m_per, m_per), :]
        out_ref[...] = acc

    vm = pl.BlockSpec(memory_space=pltpu.VMEM)
    return _pallas(body, name="allreduce_small", in_specs=[vm], out_specs=vm, out_shape=_sds((m_per, LANES)),
                          scratch_shapes=[pltpu.VMEM((8 * m_per, LANES), F32), pltpu.SemaphoreType.DMA((7,)),
                                          pltpu.SemaphoreType.DMA((7,)), pltpu.SemaphoreType.DMA],
                          compiler_params=pltpu.CompilerParams(vmem_limit_bytes=VMEM_LIMIT))(v)


def _block_diag(blocks):
    j, g, a, b = blocks.shape
    eye = jnp.eye(g, dtype=bool)[None, :, None, :, None]
    return jnp.where(eye, blocks[:, :, :, None, :], jnp.zeros((), blocks.dtype)).reshape(j, g * a, g * b)


def _diag_blocks(m, a, b):
    j = m.shape[0]
    g = m.shape[1] // a
    t = m.reshape(j, g, a, g, b)
    eye = jnp.eye(g, dtype=bool)[None, :, None, :, None]
    return jnp.sum(jnp.where(eye, t, 0.0), axis=3)


def _pack_rows(parts, rows, dtype):
    used = sum(p.shape[0] for p in parts)
    return jnp.concatenate([p.astype(dtype) for p in parts] + [jnp.zeros((rows - used, D_MODEL), dtype)], axis=0)


_SMALL = (("g_mix", (1024,)), ("b_f", (8,)), ("g_q", (64,)), ("g_k", (64,)), ("lambda_re", (32, 64)),
          ("lambda_im", (32, 64)), ("log_step", (32,)), ("b_re", (32, 64, 16)), ("b_im", (32, 64, 16)),
          ("c_re", (32, 16, 64)), ("c_im", (32, 16, 64)), ("d_skip", (32, 16)), ("b_glu", (512,)),
          ("g_attn_out", (512,)), ("g_ssm_out", (512,)), ("g_ffn", (1024,)), ("conv_b", (5632,)))


def _small_rows(shape):
    return -(-math.prod(shape) // LANES)


def _pack_small(arrs, extra=()):
    parts = []
    for a in list(arrs) + list(extra):
        flat = a.reshape(-1)
        rows = -(-flat.shape[0] // LANES)
        parts.append(jnp.pad(flat, (0, rows * LANES - flat.shape[0])).reshape(rows, LANES))
    total = sum(p.shape[0] for p in parts)
    pad = -total % SUBLANES
    if pad:
        parts.append(jnp.zeros((pad, LANES), F32))
    return jnp.concatenate(parts, axis=0)


def _unpack_small(buf, shapes):
    out, r = [], 0
    for shape in shapes:
        n = math.prod(shape)
        rows = -(-n // LANES)
        out.append(buf[r:r + rows].reshape(-1)[:n].reshape(shape))
        r += rows
    return out


def _halves(t):
    return t.reshape(N_CHIPS, 2, t.shape[0] // (2 * N_CHIPS), t.shape[1])


class _MeshComm:
    def __init__(self, args):
        x, y, self.core = lax.axis_index("x"), lax.axis_index("y"), lax.axis_index("c")
        self.chip = 2 * x + y
        self.place = jnp.stack([self.chip, self.core]).astype(jnp.int32)
        self.shards = {n: args[n].astype(BF16) for n in ("w_in", "w_glu", "w_out", "w_up", "w_down")}
        self.conv_w = args["conv_w"]

    def _own(self, stacked, mine):
        return lax.dynamic_update_slice(stacked, mine[None], (self.chip,) + (0,) * mine.ndim)

    def w_in(self):
        sh = self.shards["w_in"]
        (buf,) = _run_hosted(_plan_gather_ici([sh], [False]), name="gather_w_in")
        (buf,) = _run_hosted(_plan_gather_d2d([buf], [sh]), name="pass_w_in")
        whole = self._own(buf, sh).transpose(1, 0, 2).reshape(D_MODEL, IN_COLS)
        return jnp.pad(whole, ((0, 0), (0, Z_COLS - IN_COLS)))

    def gather_first(self):
        self.mid = [self.shards[n] for n in ("w_glu", "w_out", "w_down")]
        return _plan_gather_ici(self.mid, [False, False, False], whole=[self.conv_w])

    def gather_second(self, landed):
        self.g_cw = landed[3]
        return _both(_plan_gather_d2d(list(landed[:3]), self.mid),
                     _plan_gather_ici([self.shards["w_up"]], [True]))

    def weights(self, gathered):
        g_glu, g_out, g_down = gathered[:3]
        own = self._own
        return (own(g_glu, self.mid[0]).reshape(SSM_W, SSM_W), own(g_out, self.mid[1]).reshape(D_MODEL, D_MODEL),
                own(g_down, self.mid[2]).reshape(D_FF, D_MODEL),
                own(self.g_cw, self.conv_w).transpose(1, 0, 2).reshape(3, 2 * D_FF))

    def gather_third(self, gathered):
        return _plan_gather_d2d([gathered[3]], [self.shards["w_up"]])

    def w_up(self, passed):
        return _place_cols(passed[0], self.shards["w_up"], self.place)

    def swap(self, d_w_down, d_w_up, d_w_glu, d_w_out):
        self.early = [_halves(d_w_down), d_w_up, _halves(d_w_glu), _halves(d_w_out)]
        return _plan_swap(self.early)

    def scatter(self, landed):
        self.early_sums = [_add_half(g, l, self.place, name="add_" + n)
                           for g, l, n in zip(self.early, landed, ("w_down", "w_up", "w_glu", "w_out"))]
        return _plan_scatter(self.early_sums, [None, (UP_COLS, UP_COLS), None, None])

    def reduce(self, early_lands, d_w_in):
        d_in = d_w_in
        (landed,) = _run_hosted(_plan_swap([d_in]), name="swap_halves")
        sum_in = _add_half(d_in, landed, self.place, name="add_w_in")
        (land_in,) = _run_hosted(_plan_scatter([sum_in], [(IN_STRIDE, IN_WINDOW)]), name="scatter_chips")
        es, el = self.early_sums, early_lands
        todo = [(sum_in, land_in, "w_in", LANES, IN_STRIDE), (es[2], el[2], "w_glu", SSM_W, 0),
                (es[3], el[3], "w_out", D_MODEL, 0), (es[1], el[1], "w_up", UP_COLS, UP_COLS),
                (es[0], el[0], "w_down", D_MODEL, 0)]
        reds = _run_hosted(_plan_join([_sum_chips(s, l, self.place, name="sum_" + n, tc=tc, window_stride=st)
                                       for s, l, n, tc, st in todo]), name="join_halves")
        g_big = dict(zip(("w_in", "w_glu", "w_out", "w_up", "w_down"), reds))
        g_big["w_in"] = lax.dynamic_slice_in_dim(reds[0], 2 * self.chip, IN_COLS // N_CHIPS, axis=1)
        return g_big


def _local_step(x, tgt, p, comm):
    s = x.shape[0]
    row = lambda v: v.reshape(1, -1)
    g_mix, g_ffn = row(p["g_mix"]), row(p["g_ffn"])
    g_att, g_ssm, b_glu, conv_b = row(p["g_attn_out"]), row(p["g_ssm_out"]), row(p["b_glu"]), row(p["conv_b"])
    gq = row(jnp.tile(p["g_q"], HEADS))
    gk = row(jnp.tile(p["g_k"], HEADS))
    bf = row(jnp.pad(p["b_f"], (0, LANES - HEADS)))
    gg = jnp.kron(jnp.eye(HEADS, dtype=F32), jnp.ones((HEAD_DIM, HEAD_DIM), F32)).astype(BF16)
    dsk = row(p["d_skip"])

    rep = lambda a: jnp.repeat(a, SSM_GROUP, axis=0)
    lr, li = rep(p["lambda_re"]), rep(p["lambda_im"])
    ls = rep(jnp.broadcast_to(p["log_step"][:, None], (SSM_GROUPS, SSM_STATE)))
    bt_re = p["b_re"].transpose(0, 2, 1).reshape(_PARAM_SHAPE)
    bt_im = p["b_im"].transpose(0, 2, 1).reshape(_PARAM_SHAPE)
    a_re_rep, a_im_rep, bb_re, bb_im = _ssm_params(lr, li, ls, bt_re, bt_im)
    ar = a_re_rep[::SSM_GROUP].reshape(SSM_CHUNKS, 1, CHUNK_S)
    ai = a_im_rep[::SSM_GROUP].reshape(SSM_CHUNKS, 1, CHUNK_S)
    chunked = lambda t: t.reshape(SSM_CHUNKS, SSM_GROUPS // SSM_CHUNKS, SSM_GROUP, SSM_STATE)
    bbr = _block_diag(chunked(bb_re)).astype(BF16)
    bbi = _block_diag(chunked(bb_im)).astype(BF16)
    to_cc = lambda c: _block_diag(chunked(c).transpose(0, 1, 3, 2)).astype(BF16)
    ccr, cci = to_cc(p["c_re"]), to_cc(p["c_im"])

    w_in_r = comm.w_in()
    hb, z = _in_proj(x, g_mix, w_in_r)
    qh, kh, vh, ub, uf, c128 = _attn_prep(z, gq, gk, bf, gg)
    crow = c128[:, :HEADS].T.reshape(HEADS, 1, s)
    (oh, lse), landed = _attn_fwd(qh, kh, vh, crow, comm.gather_first())
    (xr, xi, y), gathered = _ssm_fwd(ub, uf, bbr, bbi, ar, ai, ccr, cci, dsk, comm.gather_second(landed))
    w_glu_b, w_out_b, w_down_b, conv_w_full = comm.weights(gathered)
    (x1, mixb, h2b), passed = _mix_out(y, oh, x, w_glu_b, b_glu, g_att, g_ssm, w_out_b, g_ffn, comm.gather_third(gathered))
    w_up_b = comm.w_up(passed)
    up = _mm(h2b, w_up_b, name="ffn_up", tm=1024, tn=1408, tk=1024)
    act = _conv_act(up, conv_w_full, conv_b)
    dy, dyb, loss_blk = _down_loss(act, w_down_b, x1, tgt)

    d_w_down = _mm(act, dyb, ta=True, name="d_w_down", tm=1408, tn=1024, tk=2048)
    dact = _mm(dyb, w_down_b, tb=True, name="d_act", tm=1024, tn=1408, tk=1024)
    dupb, dcw = _conv_act_bwd(up, dact, conv_w_full, conv_b)
    d_w_up = _mm(h2b, dupb, ta=True, b_parts=2, name="d_w_up", tm=1024, tn=1408, tk=2048)
    dh2 = _mm(dupb, w_up_b, tb=True, a_parts=2, name="d_h2", tm=1024, tn=1024, tk=1408)
    dx1, dx1b, doh, dys, d_w_glu, d_g_ffn, d_g_att, d_g_ssm, d_b_glu = _mix_bwd(
        dy, dh2, x1, g_ffn, w_out_b, y, oh, w_glu_b, b_glu, g_att, g_ssm)
    d_w_out = _mm(mixb, dx1b, ta=True, name="d_w_out", tm=1024, tn=1024, tk=2048)
    (du, dbbr, dbbi, dccr, dcci, dar, dai, dd), swapped = _ssm_bwd(dys, uf, ub, xr, xi, bbr, bbi, ar, ai, ccr, cci, dsk,
                                                                comm.swap(d_w_down, d_w_up, d_w_glu, d_w_out))
    (dqh, dkh, dvh, dcrow), early_lands = _attn_bwd(qh, kh, vh, crow, lse, doh, comm.scatter(swapped))
    dc128 = jnp.pad(dcrow.reshape(HEADS, s).T, ((0, 0), (0, LANES - HEADS)))
    dzb, d_gq, d_gk, d_bf = _prep_bwd(z, dqh, dkh, dvh, du, dc128, gq, gk, bf, gg)
    d_w_in_r = _mm(hb, dzb, ta=True, name="d_w_in", tm=512, tn=Z_COLS, tk=2048)
    dh = _mm(dzb, w_in_r, tb=True, name="d_h", tm=1024, tn=1024, tk=Z_COLS)
    dx, d_g_mix = _in_norm_bwd(x, g_mix, dh, dx1)

    unchunk = lambda t: t.reshape(_PARAM_SHAPE)
    dbb_re = unchunk(_diag_blocks(dbbr, SSM_GROUP, SSM_STATE))
    dbb_im = unchunk(_diag_blocks(dbbi, SSM_GROUP, SSM_STATE))
    first_row = (jnp.arange(_PARAM_SHAPE[0]) % SSM_GROUP == 0)[:, None]
    da_re = jnp.where(first_row, rep(dar.reshape(SSM_GROUPS, SSM_STATE)), 0.0)
    da_im = jnp.where(first_row, rep(dai.reshape(SSM_GROUPS, SSM_STATE)), 0.0)
    expand_t = (jnp.arange(SSM_GROUPS)[:, None] == (jnp.arange(_PARAM_SHAPE[0]) // SSM_GROUP)[None, :]).astype(BF16)
    d_lr, d_li, d_ls, d_bt_re, d_bt_im = _ssm_params_bwd(lr, li, ls, bt_re, bt_im, da_re, da_im, dbb_re, dbb_im, expand_t)
    from_bt = lambda t: t.reshape(SSM_GROUPS, SSM_GROUP, SSM_STATE).transpose(0, 2, 1)
    from_cc = lambda t: _diag_blocks(t, SSM_STATE, SSM_GROUP).transpose(0, 1, 3, 2).reshape(SSM_GROUPS, SSM_GROUP, SSM_STATE)

    small = {
        "g_mix": d_g_mix, "b_f": d_bf[0, :HEADS], "g_q": d_gq.reshape(HEADS, HEAD_DIM).sum(0),
        "g_k": d_gk.reshape(HEADS, HEAD_DIM).sum(0), "lambda_re": d_lr, "lambda_im": d_li, "log_step": d_ls,
        "b_re": from_bt(d_bt_re), "b_im": from_bt(d_bt_im), "c_re": from_cc(dccr), "c_im": from_cc(dcci),
        "d_skip": dd, "b_glu": d_b_glu, "g_attn_out": d_g_att, "g_ssm_out": d_g_ssm, "g_ffn": d_g_ffn,
        "conv_b": dcw[:, 3],
    }
    big = {"w_in": d_w_in_r, "w_glu": d_w_glu, "w_out": d_w_out, "w_up": d_w_up, "w_down": d_w_down}
    return loss_blk[0, 0], dx, big, small, dcw[:, 0:3].transpose(1, 0, 2).reshape(3, 2 * D_FF), early_lands


def kernel(x, g_mix, w_in, b_f, g_q, g_k, lambda_re, lambda_im, log_step, b_re, b_im, c_re, c_im, d_skip, w_glu, b_glu, g_attn_out, g_ssm_out, w_out, g_ffn, w_up, conv_w, conv_b, w_down, loss_target, m_g_mix, m_w_in, m_b_f, m_g_q, m_g_k, m_lambda_re, m_lambda_im, m_log_step, m_b_re, m_b_im, m_c_re, m_c_im, m_d_skip, m_w_glu, m_b_glu, m_g_attn_out, m_g_ssm_out, m_w_out, m_g_ffn, m_w_up, m_conv_w, m_conv_b, m_w_down, v_g_mix, v_w_in, v_b_f, v_g_q, v_g_k, v_lambda_re, v_lambda_im, v_log_step, v_b_re, v_b_im, v_c_re, v_c_im, v_d_skip, v_w_glu, v_b_glu, v_g_attn_out, v_g_ssm_out, v_w_out, v_g_ffn, v_w_up, v_conv_w, v_conv_b, v_w_down):
    args = dict(locals())
    order = ["g_mix", "w_in", "b_f", "g_q", "g_k", "lambda_re", "lambda_im", "log_step", "b_re", "b_im", "c_re", "c_im",
             "d_skip", "w_glu", "b_glu", "g_attn_out", "g_ssm_out", "w_out", "g_ffn", "w_up", "conv_w", "conv_b", "w_down"]
    comm = _MeshComm(args)
    chip = comm.chip
    loss_part, dx, big, small, d_conv_w, early_lands = _local_step(x[0], loss_target[0], args, comm)
    loss = lax.psum(loss_part, ("x", "y", "c"))

    g_big = comm.reduce(early_lands, big["w_in"])

    small_names = [n for n, _ in _SMALL]
    small_shapes = [sh for _, sh in _SMALL]
    gsum = _allreduce_small(_pack_small([small[n] for n in small_names], extra=[d_conv_w]))
    g_small = _unpack_small(gsum, small_shapes + [(3, 2 * D_FF)])
    g_conv_w = lax.dynamic_slice_in_dim(g_small[-1], chip * (2 * D_FF // N_CHIPS), 2 * D_FF // N_CHIPS, axis=1)
    g_small = dict(zip(small_names, g_small[:-1]))

    grad, delta, new_m, new_v = {}, {}, {}, {}
    for n in ("w_in", "w_glu", "w_out", "w_up", "w_down"):
        grad[n] = g_big[n]
        delta[n], new_m[n], new_v[n] = _adamw(args[n], g_big[n], args["m_" + n], args["v_" + n], name="adamw_" + n)
    grad["conv_w"] = g_conv_w
    delta["conv_w"], new_m["conv_w"], new_v["conv_w"] = _adamw(conv_w, g_conv_w, m_conv_w, v_conv_w, name="adamw_conv_w")
    stepped = _adamw_small([args[n] for n in small_names], [g_small[n] for n in small_names],
                           [args["m_" + n] for n in small_names], [args["v_" + n] for n in small_names])
    for i, n in enumerate(small_names):
        grad[n] = g_small[n]
        delta[n], new_m[n], new_v[n] = stepped[3 * i:3 * i + 3]

    return (loss, dx[None], *[grad[n] for n in order], *[delta[n] for n in order], *[new_m[n] for n in order],
            *[new_v[n] for n in order])
```

```python
import math

import jax
import jax.numpy as jnp
from jax import lax
from jax.experimental import pallas as pl
from jax.experimental.pallas import tpu as pltpu

F32 = jnp.float32
BF16 = jnp.bfloat16

D_MODEL = 1024
HEADS = 8
HEAD_DIM = 64
ATTN_W = 512
SSM_W = 512
SSM_GROUPS = 32
SSM_GROUP = 16
SSM_STATE = 64
N_STATE = SSM_GROUPS * SSM_STATE
D_FF = 2816
IN_COLS = 2056
Z_COLS = 2176
F_COL0 = 1536
U_COL0 = 1544
EPS = 1e-6
NEG_INF = -1e30
N_CHIPS = 4
LANES = 128
SUBLANES = 8
SSM_CHUNKS = 2
CHUNK_U = SSM_W // SSM_CHUNKS
CHUNK_S = N_STATE // SSM_CHUNKS
HEADS_PER_STEP = 4
STRIP = 128
N_STRIPS = D_FF // STRIP

ROWS_IN, ROWS_GLU, ROWS_OUT, ROWS_UP, ROWS_DOWN = 514, 64, 256, 1408, 704
OFF_GLU = ROWS_IN
OFF_OUT = OFF_GLU + ROWS_GLU
OFF_UP = OFF_OUT + ROWS_OUT
OFF_DOWN = OFF_UP + ROWS_UP
OFF_SPARE = OFF_DOWN + ROWS_DOWN
PACK_ROWS = 2976
HALF_ROWS = PACK_ROWS // 2
CONVW_ROWS = 9

ADAM_LR = 0.001
ADAM_B1 = 0.9
ADAM_B2 = 0.999
ADAM_EPS = 1e-08
ADAM_WD = 0.01
ADAM_STEP = 10

VMEM_LIMIT = 56 * 1024 * 1024
MESH = pl.DeviceIdType.MESH


def _pallas(body, **kw):
    return pl.pallas_call(body, **kw)


def _pcall(body, *, name, out_shape, in_specs, out_specs, grid=(), scratch_shapes=(), dims=None):
    params = pltpu.CompilerParams(dimension_semantics=dims, vmem_limit_bytes=VMEM_LIMIT)
    return _pallas(body, name=name, grid=grid, in_specs=in_specs, out_specs=out_specs,
                   out_shape=out_shape, scratch_shapes=scratch_shapes, compiler_params=params)


def _sds(shape, dtype=F32):
    return jax.ShapeDtypeStruct(shape, dtype)


def _dot(a, b):
    return jnp.dot(a, b, preferred_element_type=F32)


def _dot_nt(a, b):
    return lax.dot_general(a, b, (((1,), (1,)), ((), ())), preferred_element_type=F32)


def _dot_tn(a, b):
    return lax.dot_general(a, b, (((0,), (0,)), ((), ())), preferred_element_type=F32)


def _split3(x):
    hi = x.astype(BF16)
    r = x - hi.astype(F32)
    mid = r.astype(BF16)
    lo = (r - mid.astype(F32)).astype(BF16)
    return hi, mid, lo


def _dot_exact_r(x, m01):
    hi, mid, lo = _split3(x)
    return _dot(hi, m01) + _dot(mid, m01) + _dot(lo, m01)


def _dot_exact_l(m01, x):
    hi, mid, lo = _split3(x)
    return _dot(m01, hi) + _dot(m01, mid) + _dot(m01, lo)


def _sigmoid(x):
    return 1.0 / (1.0 + jnp.exp(-x))


def _rms(x, g):
    r = lax.rsqrt(jnp.mean(x * x, axis=-1, keepdims=True) + EPS)
    return x * r * g


def _rms_bwd(x, g, dy):
    r = lax.rsqrt(jnp.mean(x * x, axis=-1, keepdims=True) + EPS)
    w = dy * g
    dx = r * w - x * (r * r * r) * jnp.mean(w * x, axis=-1, keepdims=True)
    dg = jnp.sum(dy * x * r, axis=0, keepdims=True)
    return dx, dg


_GELU_K = math.sqrt(2.0 / math.pi)
_GELU_C = 0.044715


def _gelu(y):
    return y * (0.5 * (1.0 + jnp.tanh(_GELU_K * (y + _GELU_C * (y * y * y)))))


def _gelu_grad(y):
    t = jnp.tanh(_GELU_K * (y + _GELU_C * (y * y * y)))
    return 0.5 * (1.0 + t) + 0.5 * y * (1.0 - t * t) * (_GELU_K * (1.0 + 3.0 * _GELU_C * y * y))


def _tile(n, pref):
    if n <= pref:
        return n
    divs = [t for t in range(LANES, n + 1, LANES) if n % t == 0]
    below = [t for t in divs if t <= pref]
    if below and 2 * below[-1] >= pref:
        return below[-1]
    above = [t for t in divs if t > pref]
    return above[0] if above else n


def _row_tile(s):
    return min(256, s)


def _mm(a, b, *, name, tm, tn, tk, ta=False, tb=False, a_parts=1, b_parts=1):
    if a_parts > 1:
        m, kk = a.shape[1], a.shape[2] * a_parts
    elif ta:
        kk, m = a.shape
    else:
        m, kk = a.shape
    if b_parts > 1:
        n = b.shape[2] * b_parts
    else:
        n = b.shape[0] if tb else b.shape[1]
    tm, tn, tk = _tile(m, tm), _tile(n // b_parts, tn), _tile(kk // a_parts, tk)
    k_per, n_per = kk // a_parts // tk, n // b_parts // tn

    def body(a_ref, b_ref, o_ref):
        k = pl.program_id(2)
        if ta:
            part = _dot_tn(a_ref[...], b_ref[...])
        elif tb:
            part = _dot_nt(a_ref[...], b_ref[...])
        else:
            part = _dot(a_ref[...], b_ref[...])

        @pl.when(k == 0)
        def _():
            o_ref[...] = part

        @pl.when(k > 0)
        def _():
            o_ref[...] += part

    if a_parts > 1:
        a_spec = pl.BlockSpec((None, tm, tk), lambda i, j, k: (k // k_per, i, k % k_per))
    else:
        a_spec = pl.BlockSpec((tk, tm), lambda i, j, k: (k, i)) if ta else pl.BlockSpec((tm, tk), lambda i, j, k: (i, k))
    if b_parts > 1:
        b_spec = pl.BlockSpec((None, tk, tn), lambda i, j, k: (j // n_per, k, j % n_per))
    else:
        b_spec = pl.BlockSpec((tn, tk), lambda i, j, k: (j, k)) if tb else pl.BlockSpec((tk, tn), lambda i, j, k: (k, j))
    return _pcall(body, name=name, grid=(m // tm, n // tn, kk // tk), in_specs=[a_spec, b_spec],
                  out_specs=pl.BlockSpec((tm, tn), lambda i, j, k: (i, j)), out_shape=_sds((m, n)),
                  dims=("parallel", "parallel", "arbitrary"))(a, b)


def _in_proj(x, g_mix, w_in_r):
    s = x.shape[0]
    tm = _row_tile(s)

    def body(x_ref, g_ref, w_ref, h_ref, z_ref):
        h = _rms(x_ref[...], g_ref[...]).astype(BF16)
        h_ref[...] = h
        z_ref[...] = _dot(h, w_ref[...])

    return _pcall(body, name="in_proj", grid=(s // tm,),
                  in_specs=[pl.BlockSpec((tm, D_MODEL), lambda i: (i, 0)), pl.BlockSpec((1, D_MODEL), lambda i: (0, 0)),
                            pl.BlockSpec((D_MODEL, Z_COLS), lambda i: (0, 0))],
                  out_specs=[pl.BlockSpec((tm, D_MODEL), lambda i: (i, 0)), pl.BlockSpec((tm, Z_COLS), lambda i: (i, 0))],
                  out_shape=[_sds((s, D_MODEL), BF16), _sds((s, Z_COLS))], dims=("parallel",))(x, g_mix, w_in_r)


def _split_heads(ref, val):
    for h in range(HEADS):
        ref[h] = val[:, h * HEAD_DIM:(h + 1) * HEAD_DIM].astype(ref.dtype)


def _merge_heads(ref):
    return jnp.concatenate([ref[h].astype(F32) for h in range(HEADS)], axis=-1)


def _forget_logits(z_ref, bf_ref):
    fl = z_ref[:, F_COL0:F_COL0 + LANES] + bf_ref[...]
    return jnp.where(lax.broadcasted_iota(jnp.int32, fl.shape, 1) < HEADS, fl, 0.0)


def _attn_prep(z, gq, gk, bf, gg):
    s = z.shape[0]
    tm = _row_tile(s)

    def body(z_ref, gq_ref, gk_ref, bf_ref, gg_ref, qn_ref, kn_ref, vb_ref, ub_ref, uf_ref, c_ref, carry_ref):
        i = pl.program_id(0)

        @pl.when(i == 0)
        def _():
            carry_ref[...] = jnp.zeros_like(carry_ref)

        gg_m = gg_ref[...]

        def head_norm(t, g):
            ssq = _dot_exact_r(t * t, gg_m)
            return t * lax.rsqrt(ssq * (1.0 / HEAD_DIM) + EPS) * g

        _split_heads(qn_ref, head_norm(z_ref[:, 0:ATTN_W], gq_ref[...]))
        _split_heads(kn_ref, head_norm(z_ref[:, ATTN_W:2 * ATTN_W], gk_ref[...]))
        _split_heads(vb_ref, z_ref[:, 2 * ATTN_W:3 * ATTN_W])
        u = z_ref[:, U_COL0:U_COL0 + SSM_W]
        uf_ref[...] = u
        ub_ref[...] = u.astype(BF16)
        fl = _forget_logits(z_ref, bf_ref)
        lf = jnp.minimum(fl, 0.0) - jnp.log1p(jnp.exp(-jnp.abs(fl)))
        row = lax.broadcasted_iota(jnp.int32, (tm, tm), 0)
        col = lax.broadcasted_iota(jnp.int32, (tm, tm), 1)
        tri = (row >= col).astype(BF16)
        c = _dot_exact_l(tri, lf) + carry_ref[...]
        c_ref[...] = c
        carry_ref[...] = c[tm - 1:tm, :]

    row_spec = lambda w: pl.BlockSpec((tm, w), lambda i: (i, 0))
    const = lambda shape: pl.BlockSpec(shape, lambda i: (0, 0))
    heads = pl.BlockSpec((HEADS, tm, HEAD_DIM), lambda i: (0, i, 0))
    return _pcall(body, name="attn_prep", grid=(s // tm,),
                  in_specs=[row_spec(Z_COLS), const((1, ATTN_W)), const((1, ATTN_W)), const((1, LANES)), const((ATTN_W, ATTN_W))],
                  out_specs=[heads] * 3 + [row_spec(SSM_W), row_spec(SSM_W), row_spec(LANES)],
                  out_shape=[_sds((HEADS, s, HEAD_DIM), BF16)] * 3 + [_sds((s, SSM_W), BF16), _sds((s, SSM_W)), _sds((s, LANES))],
                  scratch_shapes=[pltpu.VMEM((1, LANES), F32)], dims=("arbitrary",))(z, gq, gk, bf, gg)


def _attn_fwd(qh, kh, vh, crow, hosted=None):
    _, s, _ = qh.shape
    tq = _row_tile(s)
    scale = HEAD_DIM ** -0.5

    hp = HEADS_PER_STEP
    nq = s // tq
    fold = lambda t, op: op(t[:, :tq // 2], t[:, tq // 2:])

    def body(q_ref, k_ref, v_ref, c_ref, o_ref, lse_ref, s_s):
        i = pl.program_id(1)

        def first(j, ms, diagonal):
            off = pl.multiple_of(j * tq, tq)
            out = []
            for hh in range(hp):
                sc = _dot_nt(q_ref[hh], k_ref[hh, pl.ds(off, tq), :]) * scale - c_ref[hh, :, pl.ds(off, tq)]
                if diagonal:
                    causal = lax.broadcasted_iota(jnp.int32, (tq, tq), 1) <= lax.broadcasted_iota(jnp.int32, (tq, tq), 0)
                    sc = jnp.where(causal, sc, NEG_INF)
                s_s[hh, j] = sc
                out.append(jnp.maximum(ms[hh], fold(sc, jnp.maximum)))
            return tuple(out)

        ms = lax.fori_loop(0, i, lambda j, c: first(j, c, False), (jnp.full((tq, tq // 2), NEG_INF, F32),) * hp)
        ms = [jnp.max(t, axis=-1, keepdims=True) for t in first(i, ms, True)]

        def second(j, carry):
            rows = pl.ds(pl.multiple_of(j * tq, tq), tq)
            out = []
            for hh in range(hp):
                ls, acc = carry[hh]
                p = jnp.exp(s_s[hh, j] - ms[hh])
                out.append((ls + fold(p, jnp.add), acc + _dot(p.astype(BF16), v_ref[hh, rows, :])))
            return tuple(out)

        zero = (jnp.zeros((tq, tq // 2), F32), jnp.zeros((tq, HEAD_DIM), F32))
        for hh, (ls, acc) in enumerate(lax.fori_loop(0, i + 1, second, (zero,) * hp)):
            l = jnp.sum(ls, axis=-1, keepdims=True)
            o_ref[hh] = acc / l
            lse_ref[hh] = ms[hh] + jnp.log(l)

    blk = pl.BlockSpec((hp, tq, HEAD_DIM), lambda h, i: (h, i, 0))
    full = pl.BlockSpec((hp, s, HEAD_DIM), lambda h, i: (h, 0, 0))
    nh = HEADS // hp
    first = lambda: jnp.logical_and(pl.program_id(0) == 0, pl.program_id(1) == 0)
    last = lambda: jnp.logical_and(pl.program_id(0) == nh - 1, pl.program_id(1) == nq - 1)
    return _host_pcall(body, hosted, first, last, n_in=4, n_out=2, n_scratch=1, name="attn_fwd", grid=(nh, nq),
                       in_specs=[blk, full, full, pl.BlockSpec((hp, 1, s), lambda h, i: (h, 0, 0))],
                       out_specs=[blk, pl.BlockSpec((hp, tq, 1), lambda h, i: (h, i, 0))],
                       out_shape=[_sds((HEADS, s, HEAD_DIM)), _sds((HEADS, s, 1))],
                       scratch_shapes=[pltpu.VMEM((hp, nq, tq, tq), F32)],
                       dims=("parallel", "parallel"), operands=(qh, kh, vh, crow))


def _ssm_param_fn(lr, li, ls, br, bi):
    step = jnp.exp(ls)
    er = jnp.exp(lr * step)
    ab_re = er * jnp.cos(li * step)
    ab_im = er * jnp.sin(li * step)
    num_re = ab_re - 1.0
    num_im = ab_im
    den = lr * lr + li * li
    f_re = (num_re * lr + num_im * li) / den
    f_im = (num_im * lr - num_re * li) / den
    bb_re = f_re * br - f_im * bi
    bb_im = f_re * bi + f_im * br
    return ab_re, ab_im, bb_re, bb_im


_PARAM_SHAPE = (SSM_GROUPS * SSM_GROUP, SSM_STATE)


def _ssm_params(lr, li, ls, br, bi):
    def body(lr_ref, li_ref, ls_ref, br_ref, bi_ref, ar_ref, ai_ref, bbr_ref, bbi_ref):
        ar, ai, bbr, bbi = _ssm_param_fn(lr_ref[...], li_ref[...], ls_ref[...], br_ref[...], bi_ref[...])
        ar_ref[...] = ar
        ai_ref[...] = ai
        bbr_ref[...] = bbr
        bbi_ref[...] = bbi

    spec = pl.BlockSpec(_PARAM_SHAPE, lambda: (0, 0))
    return _pcall(body, name="ssm_params", in_specs=[spec] * 5, out_specs=[spec] * 4,
                  out_shape=[_sds(_PARAM_SHAPE)] * 4)(lr, li, ls, br, bi)


def _ssm_params_bwd(lr, li, ls, br, bi, dar, dai, dbbr, dbbi, expand_t):
    def body(lr_ref, li_ref, ls_ref, br_ref, bi_ref, dar_ref, dai_ref, dbbr_ref, dbbi_ref, et_ref,
             dlr_ref, dli_ref, dls_ref, dbr_ref, dbi_ref):
        _, vjp = jax.vjp(_ssm_param_fn, lr_ref[...], li_ref[...], ls_ref[...], br_ref[...], bi_ref[...])
        dlr, dli, dls, dbr, dbi = vjp((dar_ref[...], dai_ref[...], dbbr_ref[...], dbbi_ref[...]))
        et = et_ref[...]
        dlr_ref[...] = _dot_exact_l(et, dlr)
        dli_ref[...] = _dot_exact_l(et, dli)
        dls_ref[...] = jnp.sum(_dot_exact_l(et, dls), axis=-1, keepdims=True)
        dbr_ref[...] = dbr
        dbi_ref[...] = dbi

    spec = pl.BlockSpec(_PARAM_SHAPE, lambda: (0, 0))
    gspec = pl.BlockSpec((SSM_GROUPS, SSM_STATE), lambda: (0, 0))
    return _pcall(body, name="ssm_params_bwd",
                  in_specs=[spec] * 9 + [pl.BlockSpec((SSM_GROUPS, _PARAM_SHAPE[0]), lambda: (0, 0))],
                  out_specs=[gspec, gspec, pl.BlockSpec((SSM_GROUPS, 1), lambda: (0, 0)), spec, spec],
                  out_shape=[_sds((SSM_GROUPS, SSM_STATE))] * 2 + [_sds((SSM_GROUPS, 1))] + [_sds(_PARAM_SHAPE)] * 2,
                  )(lr, li, ls, br, bi, dar, dai, dbbr, dbbi, expand_t)


def _cmul(ar, ai, br, bi):
    return ar * br - ai * bi, ar * bi + ai * br


def _scan_consts(ar, ai, width, reverse):
    row = lax.broadcasted_iota(jnp.int32, (SUBLANES, width), 0)
    pw = [(ar, ai)]
    for _ in range(SUBLANES - 1):
        pw.append(_cmul(pw[-1][0], pw[-1][1], ar, ai))
    steps = []
    for d in (1, 2, 4):
        keep = (row < SUBLANES - d) if reverse else (row >= d)
        steps.append((d, jnp.where(keep, pw[d - 1][0], 0.0), jnp.where(keep, pw[d - 1][1], 0.0)))
    pr = jnp.zeros((SUBLANES, width), F32)
    pi = jnp.zeros((SUBLANES, width), F32)
    for r in range(SUBLANES):
        e = (SUBLANES - r) if reverse else (r + 1)
        pr = jnp.where(row == r, pw[e - 1][0], pr)
        pi = jnp.where(row == r, pw[e - 1][1], pi)
    return steps, pr, pi


def _scan_tile(xr, xi, cr, ci, consts, reverse):
    steps, pr, pi = consts
    for d, mr, mi in steps:
        sh = (SUBLANES - d) if reverse else d
        sr = pltpu.roll(xr, sh, 0)
        si = pltpu.roll(xi, sh, 0)
        xr, xi = xr + mr * sr - mi * si, xi + mr * si + mi * sr
    return xr + pr * cr - pi * ci, xi + pr * ci + pi * cr


def _ssm_fwd(ub, uf, bbr, bbi, ar, ai, ccr, cci, dsk, hosted=None):
    s = ub.shape[0]
    tm = _row_tile(s)
    nt = tm // SUBLANES

    def body(ub_ref, u_ref, bbr_ref, bbi_ref, ar_ref, ai_ref, ccr_ref, cci_ref, dsk_ref,
             xr_ref, xi_ref, y_ref, cr_s, ci_s):
        i = pl.program_id(1)

        @pl.when(i == 0)
        def _():
            cr_s[...] = jnp.zeros_like(cr_s)
            ci_s[...] = jnp.zeros_like(ci_s)

        u_b = ub_ref[...]
        xr_ref[...] = _dot(u_b, bbr_ref[0])
        xi_ref[...] = _dot(u_b, bbi_ref[0])
        consts = _scan_consts(ar_ref[0], ai_ref[0], CHUNK_S, False)

        def tile(k, carry):
            cr, ci = carry
            sl = pl.ds(pl.multiple_of(k * SUBLANES, SUBLANES), SUBLANES)
            xr, xi = _scan_tile(xr_ref[sl, :], xi_ref[sl, :], cr, ci, consts, False)
            xr_ref[sl, :] = xr
            xi_ref[sl, :] = xi
            return xr[SUBLANES - 1:SUBLANES, :], xi[SUBLANES - 1:SUBLANES, :]

        cr, ci = lax.fori_loop(0, nt, tile, (cr_s[...], ci_s[...]))
        cr_s[...] = cr
        ci_s[...] = ci
        y_ref[...] = (_dot(xr_ref[...].astype(BF16), ccr_ref[0]) - _dot(xi_ref[...].astype(BF16), cci_ref[0])
                      + dsk_ref[...] * u_ref[...])

    wspec = lambda a, b: pl.BlockSpec((1, a, b), lambda j, i: (j, 0, 0))
    nb = s // tm
    first = lambda: jnp.logical_and(pl.program_id(0) == 0, pl.program_id(1) == 0)
    last = lambda: jnp.logical_and(pl.program_id(0) == SSM_CHUNKS - 1, pl.program_id(1) == nb - 1)
    return _host_pcall(
        body, hosted, first, last, n_in=9, n_out=3, n_scratch=2, name="ssm_fwd", grid=(SSM_CHUNKS, nb),
        in_specs=[pl.BlockSpec((tm, CHUNK_U), lambda j, i: (i, j)),
                  pl.BlockSpec((tm, CHUNK_U), lambda j, i: (i, j)),
                  wspec(CHUNK_U, CHUNK_S), wspec(CHUNK_U, CHUNK_S), wspec(1, CHUNK_S), wspec(1, CHUNK_S),
                  wspec(CHUNK_S, CHUNK_U), wspec(CHUNK_S, CHUNK_U),
                  pl.BlockSpec((1, CHUNK_U), lambda j, i: (0, j))],
        out_specs=[pl.BlockSpec((tm, CHUNK_S), lambda j, i: (i, j)), pl.BlockSpec((tm, CHUNK_S), lambda j, i: (i, j)),
                   pl.BlockSpec((tm, CHUNK_U), lambda j, i: (i, j))],
        out_shape=[_sds((s, N_STATE)), _sds((s, N_STATE)), _sds((s, SSM_W))],
        scratch_shapes=[pltpu.VMEM((1, CHUNK_S), F32)] * 2,
        dims=("parallel", "arbitrary"), operands=(ub, uf, bbr, bbi, ar, ai, ccr, cci, dsk))


def _ssm_glu(y, w_glu, b_glu):
    ge = _gelu(y)
    sg = _sigmoid(_dot(ge.astype(BF16), w_glu) + b_glu)
    return ge, sg


def _mix_out(y, att, x, w_glu, b_glu, g_att, g_ssm, w_out, g_ffn, hosted=None):
    s = x.shape[0]
    tm = _row_tile(s)

    def body(y_ref, att_ref, x_ref, wg_ref, bg_ref, ga_ref, gs_ref, wo_ref, gf_ref, x1_ref, mix_ref, h2_ref):
        ge, sg = _ssm_glu(y_ref[...], wg_ref[...], bg_ref[...])
        ms = _rms(ge * sg, gs_ref[...]).astype(BF16)
        ma = _rms(_merge_heads(att_ref), ga_ref[...]).astype(BF16)
        mix_ref[:, 0:ATTN_W] = ma
        mix_ref[:, ATTN_W:D_MODEL] = ms
        x1 = x_ref[...] + (_dot(ma, wo_ref[0:ATTN_W, :]) + _dot(ms, wo_ref[ATTN_W:D_MODEL, :]))
        x1_ref[...] = x1
        h2_ref[...] = _rms(x1, gf_ref[...]).astype(BF16)

    row = lambda w: pl.BlockSpec((tm, w), lambda i: (i, 0))
    const = lambda a, b: pl.BlockSpec((a, b), lambda i: (0, 0))
    nb = s // tm
    return _host_pcall(body, hosted, lambda: pl.program_id(0) == 0, lambda: pl.program_id(0) == nb - 1,
                       n_in=9, n_out=3, n_scratch=0, name="mix_out", grid=(nb,),
                       in_specs=[row(SSM_W), pl.BlockSpec((HEADS, tm, HEAD_DIM), lambda i: (0, i, 0)), row(D_MODEL),
                                 const(SSM_W, SSM_W), const(1, SSM_W),
                                 const(1, ATTN_W), const(1, SSM_W), const(D_MODEL, D_MODEL), const(1, D_MODEL)],
                       out_specs=[row(D_MODEL)] * 3,
                       out_shape=[_sds((s, D_MODEL)), _sds((s, D_MODEL), BF16), _sds((s, D_MODEL), BF16)],
                       scratch_shapes=[], dims=("parallel",), operands=(y, att, x, w_glu, b_glu, g_att, g_ssm, w_out, g_ffn))


CONV_CHUNK = 64


def _conv_rows(pad_ref, w, b, r0, n):
    y = b + pad_ref[pl.ds(r0 + SUBLANES - 2, n), :] * w[0:1, :]
    y = y + pad_ref[pl.ds(r0 + SUBLANES - 1, n), :] * w[1:2, :]
    return y + pad_ref[pl.ds(r0 + SUBLANES, n), :] * w[2:3, :]


def _fill_front_pad(pad_ref, strip_ref, s):
    pad_ref[0:SUBLANES, :] = jnp.zeros((SUBLANES, STRIP), F32)
    for r0 in range(0, s, CONV_CHUNK):
        pad_ref[pl.ds(SUBLANES + r0, CONV_CHUNK), :] = strip_ref[pl.ds(r0, CONV_CHUNK), :]


def _conv_act(up, conv_w, conv_b):
    s = up.shape[0]

    def body(ug_ref, uv_ref, wg_ref, wv_ref, bg_ref, bv_ref, act_ref, pg_ref, pv_ref):
        _fill_front_pad(pg_ref, ug_ref, s)
        _fill_front_pad(pv_ref, uv_ref, s)
        wg, wv, bg, bv = wg_ref[...], wv_ref[...], bg_ref[...], bv_ref[...]
        for r0 in range(0, s, CONV_CHUNK):
            hg = _conv_rows(pg_ref, wg, bg, r0, CONV_CHUNK)
            hv = _conv_rows(pv_ref, wv, bv, r0, CONV_CHUNK)
            act_ref[pl.ds(r0, CONV_CHUNK), :] = (hg * _sigmoid(hg) * hv).astype(BF16)

    strip = lambda off: pl.BlockSpec((s, STRIP), lambda j: (0, j + off))
    wsp = lambda off: pl.BlockSpec((3, STRIP), lambda j: (0, j + off))
    bsp = lambda off: pl.BlockSpec((1, STRIP), lambda j: (0, j + off))
    return _pcall(body, name="conv_act", grid=(N_STRIPS,),
                  in_specs=[strip(0), strip(N_STRIPS), wsp(0), wsp(N_STRIPS), bsp(0), bsp(N_STRIPS)],
                  out_specs=pl.BlockSpec((s, STRIP), lambda j: (0, j)), out_shape=_sds((s, D_FF), BF16),
                  scratch_shapes=[pltpu.VMEM((s + SUBLANES, STRIP), F32)] * 2,
                  dims=("parallel",))(up, up, conv_w, conv_w, conv_b, conv_b)


def _down_loss(act, w_down, x1, tgt):
    s = x1.shape[0]
    tm = _row_tile(s)

    def body(a_ref, w_ref, x1_ref, t_ref, dy_ref, dyb_ref, loss_ref):
        i = pl.program_id(0)

        @pl.when(i == 0)
        def _():
            loss_ref[...] = jnp.zeros_like(loss_ref)

        diff = x1_ref[...] + _dot(a_ref[...], w_ref[...]) - t_ref[...]
        dy = diff * (1.0 / D_MODEL)
        dy_ref[...] = dy
        dyb_ref[...] = dy.astype(BF16)
        loss_ref[...] += 0.5 * jnp.sum(diff * dy)

    row = lambda w: pl.BlockSpec((tm, w), lambda i: (i, 0))
    return _pcall(body, name="down_loss", grid=(s // tm,),
                  in_specs=[row(D_FF), pl.BlockSpec((D_FF, D_MODEL), lambda i: (0, 0)), row(D_MODEL), row(D_MODEL)],
                  out_specs=[row(D_MODEL), row(D_MODEL), pl.BlockSpec((SUBLANES, LANES), lambda i: (0, 0))],
                  out_shape=[_sds((s, D_MODEL)), _sds((s, D_MODEL), BF16), _sds((SUBLANES, LANES))],
                  dims=("arbitrary",))(act, w_down, x1, tgt)


def _conv_act_bwd(up, dact, conv_w, conv_b):
    s = up.shape[0]
    ch = CONV_CHUNK

    def body(ug_ref, uv_ref, da_ref, wg_ref, wv_ref, bg_ref, bv_ref, dup_ref, dcw_ref, pg_ref, pv_ref, dg_ref, dv_ref):
        _fill_front_pad(pg_ref, ug_ref, s)
        _fill_front_pad(pv_ref, uv_ref, s)
        zero = jnp.zeros((SUBLANES, STRIP), F32)
        dg_ref[pl.ds(s, SUBLANES), :] = zero
        dv_ref[pl.ds(s, SUBLANES), :] = zero
        wg, wv, bg, bv = wg_ref[...], wv_ref[...], bg_ref[...], bv_ref[...]
        tile_sum = lambda t: jnp.sum(t.reshape(ch // SUBLANES, SUBLANES, STRIP), axis=0)
        accs = [[zero] * 4, [zero] * 4]
        for r0 in range(0, s, ch):
            hg = _conv_rows(pg_ref, wg, bg, r0, ch)
            hv = _conv_rows(pv_ref, wv, bv, r0, ch)
            sg = _sigmoid(hg)
            da = da_ref[pl.ds(r0, ch), :]
            dhs = (da * hv * (sg * (1.0 + hg * (1.0 - sg))), da * (hg * sg))
            for half, (dh, d_ref, p_ref) in enumerate(zip(dhs, (dg_ref, dv_ref), (pg_ref, pv_ref))):
                d_ref[pl.ds(r0, ch), :] = dh
                for k in range(3):
                    accs[half][k] = accs[half][k] + tile_sum(dh * p_ref[pl.ds(r0 + SUBLANES - 2 + k, ch), :])
                accs[half][3] = accs[half][3] + tile_sum(dh)
        for half, (d_ref, w) in enumerate(((dg_ref, wg), (dv_ref, wv))):
            for r0 in range(0, s, ch):
                dup = (d_ref[pl.ds(r0, ch), :] * w[2:3, :] + d_ref[pl.ds(r0 + 1, ch), :] * w[1:2, :]
                       + d_ref[pl.ds(r0 + 2, ch), :] * w[0:1, :])
                dup_ref[half, pl.ds(r0, ch), :] = dup.astype(BF16)
            rid = lax.broadcasted_iota(jnp.int32, (SUBLANES, STRIP), 0)
            out = zero
            for k in range(4):
                out = jnp.where(rid == k, jnp.sum(accs[half][k], axis=0, keepdims=True), out)
            dcw_ref[half] = out

    strip = lambda off: pl.BlockSpec((s, STRIP), lambda j: (0, j + off))
    wsp = lambda off: pl.BlockSpec((3, STRIP), lambda j: (0, j + off))
    bsp = lambda off: pl.BlockSpec((1, STRIP), lambda j: (0, j + off))
    return _pcall(body, name="conv_act_bwd", grid=(N_STRIPS,),
                  in_specs=[strip(0), strip(N_STRIPS), strip(0), wsp(0), wsp(N_STRIPS), bsp(0), bsp(N_STRIPS)],
                  out_specs=[pl.BlockSpec((2, s, STRIP), lambda j: (0, 0, j)), pl.BlockSpec((2, SUBLANES, STRIP), lambda j: (0, 0, j))],
                  out_shape=[_sds((2, s, D_FF), BF16), _sds((2, SUBLANES, D_FF))],
                  scratch_shapes=[pltpu.VMEM((s + SUBLANES, STRIP), F32)] * 4,
                  dims=("parallel",))(up, up, dact, conv_w, conv_w, conv_b, conv_b)


def _mix_bwd(dy, dh2, x1, g_ffn, w_out, y, att, w_glu, b_glu, g_att, g_ssm):
    s = dy.shape[0]
    tm = _row_tile(s)

    def body(dy_ref, dh2_ref, x1_ref, gf_ref, wo_ref, y_ref, att_ref, wg_ref, bg_ref, ga_ref, gs_ref,
             dx1_ref, dx1b_ref, datt_ref, dys_ref, dwg_ref, dgf_ref, dga_ref, dgs_ref, dbg_ref):
        i = pl.program_id(0)

        @pl.when(i == 0)
        def _():
            for r in (dwg_ref, dgf_ref, dga_ref, dgs_ref, dbg_ref):
                r[...] = jnp.zeros_like(r)

        dxn, dgf = _rms_bwd(x1_ref[...], gf_ref[...], dh2_ref[...])
        dx1 = dy_ref[...] + dxn
        dx1_ref[...] = dx1
        dx1b = dx1.astype(BF16)
        dx1b_ref[...] = dx1b
        dgf_ref[...] += dgf
        dma = _dot_nt(dx1b, wo_ref[0:ATTN_W, :])
        dms = _dot_nt(dx1b, wo_ref[ATTN_W:D_MODEL, :])
        datt, dga = _rms_bwd(_merge_heads(att_ref), ga_ref[...], dma)
        _split_heads(datt_ref, datt)
        dga_ref[...] += dga
        yv = y_ref[...]
        ge, sg = _ssm_glu(yv, wg_ref[...], bg_ref[...])
        dssm, dgs = _rms_bwd(ge * sg, gs_ref[...], dms)
        dgs_ref[...] += dgs
        dgl = dssm * ge * sg * (1.0 - sg)
        dglb = dgl.astype(BF16)
        dge = dssm * sg + _dot_nt(dglb, wg_ref[...])
        dbg_ref[...] += jnp.sum(dgl, axis=0, keepdims=True)
        dwg_ref[...] += _dot_tn(ge.astype(BF16), dglb)
        dys_ref[...] = dge * _gelu_grad(yv)

    row = lambda w: pl.BlockSpec((tm, w), lambda i: (i, 0))
    const = lambda a, b: pl.BlockSpec((a, b), lambda i: (0, 0))
    heads = pl.BlockSpec((HEADS, tm, HEAD_DIM), lambda i: (0, i, 0))
    return _pcall(body, name="mix_bwd", grid=(s // tm,),
                  in_specs=[row(D_MODEL), row(D_MODEL), row(D_MODEL), const(1, D_MODEL), const(D_MODEL, D_MODEL), row(SSM_W),
                            heads, const(SSM_W, SSM_W), const(1, SSM_W), const(1, ATTN_W), const(1, SSM_W)],
                  out_specs=[row(D_MODEL), row(D_MODEL), heads, row(SSM_W), const(SSM_W, SSM_W), const(1, D_MODEL),
                             const(1, ATTN_W), const(1, SSM_W), const(1, SSM_W)],
                  out_shape=[_sds((s, D_MODEL)), _sds((s, D_MODEL), BF16), _sds((HEADS, s, HEAD_DIM)), _sds((s, SSM_W)),
                             _sds((SSM_W, SSM_W)), _sds((1, D_MODEL)), _sds((1, ATTN_W)), _sds((1, SSM_W)), _sds((1, SSM_W))],
                  dims=("arbitrary",))(dy, dh2, x1, g_ffn, w_out, y, att, w_glu, b_glu, g_att, g_ssm)


def _ssm_bwd(dys, uf, ub, xr, xi, bbr, bbi, ar, ai, ccr, cci, dsk, hosted=None):
    s = dys.shape[0]
    tm = _row_tile(s)
    nb = s // tm
    nt = tm // SUBLANES

    def body(dy_ref, u_ref, ub_ref, xr_ref, xi_ref, xrp_ref, xip_ref, bbr_ref, bbi_ref, ar_ref, ai_ref, ccr_ref,
             cci_ref, dsk_ref, du_ref, dbbr_ref, dbbi_ref, dccr_ref, dcci_ref, dar_ref, dai_ref, dd_ref,
             gr_s, gi_s, cr_s, ci_s, accr_s, acci_s):
        i = pl.program_id(1)
        first_block = i == nb - 1

        @pl.when(i == 0)
        def _():
            for r in (cr_s, ci_s, accr_s, acci_s, dbbr_ref, dbbi_ref, dccr_ref, dcci_ref, dd_ref):
                r[...] = jnp.zeros_like(r)

        dy = dy_ref[...]
        dyb = dy.astype(BF16)
        gr_s[...] = _dot_nt(dyb, ccr_ref[0])
        gi_s[...] = -_dot_nt(dyb, cci_ref[0])
        consts = _scan_consts(ar_ref[0], -ai_ref[0], CHUNK_S, True)
        row = lax.broadcasted_iota(jnp.int32, (SUBLANES, CHUNK_S), 0)

        def tile(kk, carry):
            cr, ci, accr, acci = carry
            k = nt - 1 - kk
            sl = pl.ds(pl.multiple_of(k * SUBLANES, SUBLANES), SUBLANES)
            gr, gi = _scan_tile(gr_s[sl, :], gi_s[sl, :], cr, ci, consts, True)
            gr_s[sl, :] = gr
            gi_s[sl, :] = gi
            slp = pl.ds(pl.multiple_of(jnp.maximum(k - 1, 0) * SUBLANES, SUBLANES), SUBLANES)
            inner = k > 0
            pr_t = jnp.where(inner, xr_ref[slp, :], xrp_ref[...])
            pi_t = jnp.where(inner, xi_ref[slp, :], xip_ref[...])
            live = jnp.logical_or(inner, jnp.logical_not(first_block))
            top_r = jnp.where(live, pltpu.roll(pr_t, 1, 0), 0.0)
            top_i = jnp.where(live, pltpu.roll(pi_t, 1, 0), 0.0)
            xpr = jnp.where(row == 0, top_r, pltpu.roll(xr_ref[sl, :], 1, 0))
            xpi = jnp.where(row == 0, top_i, pltpu.roll(xi_ref[sl, :], 1, 0))
            accr = accr + gr * xpr + gi * xpi
            acci = acci + gi * xpr - gr * xpi
            return gr[0:1, :], gi[0:1, :], accr, acci

        zeros = jnp.zeros((SUBLANES, CHUNK_S), F32)
        cr, ci, accr, acci = lax.fori_loop(0, nt, tile, (cr_s[...], ci_s[...], zeros, zeros))
        cr_s[...] = cr
        ci_s[...] = ci
        accr_s[...] += accr
        acci_s[...] += acci
        grb = gr_s[...].astype(BF16)
        gib = gi_s[...].astype(BF16)
        u_b = ub_ref[...]
        du_ref[...] = _dot_nt(grb, bbr_ref[0]) + _dot_nt(gib, bbi_ref[0]) + dsk_ref[...] * dy
        dbbr_ref[0] += _dot_tn(u_b, grb)
        dbbi_ref[0] += _dot_tn(u_b, gib)
        dccr_ref[0] += _dot_tn(xr_ref[...].astype(BF16), dyb)
        dcci_ref[0] -= _dot_tn(xi_ref[...].astype(BF16), dyb)
        dd_ref[...] += jnp.sum(dy * u_ref[...], axis=0, keepdims=True)

        @pl.when(i == nb - 1)
        def _():
            dar_ref[0] = jnp.sum(accr_s[...], axis=0, keepdims=True)
            dai_ref[0] = jnp.sum(acci_s[...], axis=0, keepdims=True)

    tiles_per_block = tm // SUBLANES
    rb = lambda i: nb - 1 - i
    wspec = lambda a, b: pl.BlockSpec((1, a, b), lambda j, i: (j, 0, 0))
    xblk = pl.BlockSpec((tm, CHUNK_S), lambda j, i: (rb(i), j))
    xprev = pl.BlockSpec((SUBLANES, CHUNK_S), lambda j, i: (jnp.maximum(rb(i) * tiles_per_block - 1, 0), j))
    ublk = pl.BlockSpec((tm, CHUNK_U), lambda j, i: (rb(i), j))
    first = lambda: jnp.logical_and(pl.program_id(0) == 0, pl.program_id(1) == 0)
    last = lambda: jnp.logical_and(pl.program_id(0) == SSM_CHUNKS - 1, pl.program_id(1) == nb - 1)
    return _host_pcall(
        body, hosted, first, last, n_in=14, n_out=8, n_scratch=6, name="ssm_bwd", grid=(SSM_CHUNKS, nb),
        in_specs=[ublk, ublk, ublk, xblk, xblk, xprev, xprev,
                  wspec(CHUNK_U, CHUNK_S), wspec(CHUNK_U, CHUNK_S), wspec(1, CHUNK_S), wspec(1, CHUNK_S),
                  wspec(CHUNK_S, CHUNK_U), wspec(CHUNK_S, CHUNK_U), pl.BlockSpec((1, CHUNK_U), lambda j, i: (0, j))],
        out_specs=[ublk, wspec(CHUNK_U, CHUNK_S), wspec(CHUNK_U, CHUNK_S), wspec(CHUNK_S, CHUNK_U),
                   wspec(CHUNK_S, CHUNK_U), wspec(1, CHUNK_S), wspec(1, CHUNK_S),
                   pl.BlockSpec((1, CHUNK_U), lambda j, i: (0, j))],
        out_shape=[_sds((s, SSM_W)), _sds((SSM_CHUNKS, CHUNK_U, CHUNK_S)), _sds((SSM_CHUNKS, CHUNK_U, CHUNK_S)),
                   _sds((SSM_CHUNKS, CHUNK_S, CHUNK_U)), _sds((SSM_CHUNKS, CHUNK_S, CHUNK_U)),
                   _sds((SSM_CHUNKS, 1, CHUNK_S)), _sds((SSM_CHUNKS, 1, CHUNK_S)), _sds((1, SSM_W))],
        scratch_shapes=[pltpu.VMEM((tm, CHUNK_S), F32)] * 2 + [pltpu.VMEM((1, CHUNK_S), F32)] * 2
                       + [pltpu.VMEM((SUBLANES, CHUNK_S), F32)] * 2,
        dims=("parallel", "arbitrary"), operands=(dys, uf, ub, xr, xi, xr, xi, bbr, bbi, ar, ai, ccr, cci, dsk))


def _attn_probs(q, ks, cs, lse, scale, diagonal):
    p = jnp.exp(_dot_nt(q, ks) * scale - cs - lse)
    if diagonal:
        tq, tk = p.shape
        causal = lax.broadcasted_iota(jnp.int32, (tq, tk), 1) <= lax.broadcasted_iota(jnp.int32, (tq, tk), 0)
        p = jnp.where(causal, p, 0.0)
    return p


def _attn_bwd(qh, kh, vh, crow, lse, doh, hosted=None):
    _, s, _ = qh.shape
    tq = _row_tile(s)
    nq = s // tq
    scale = HEAD_DIM ** -0.5
    hp = HEADS_PER_STEP

    def body(q_ref, k_ref, v_ref, c_ref, lse_ref, do_ref, dq_ref, dk_ref, dv_ref, dc_ref, p_s, dp_s):
        i = pl.program_id(1)

        @pl.when(i == 0)
        def _():
            for r in (dk_ref, dv_ref, dc_ref):
                r[...] = jnp.zeros_like(r)

        dobs = [do_ref[hh].astype(BF16) for hh in range(hp)]

        def first(j, dls, diagonal):
            off = pl.multiple_of(j * tq, tq)
            out = []
            for hh in range(hp):
                p = _attn_probs(q_ref[hh], k_ref[hh, pl.ds(off, tq), :], c_ref[hh, :, pl.ds(off, tq)], lse_ref[hh],
                                scale, diagonal)
                dp = _dot_nt(dobs[hh], v_ref[hh, pl.ds(off, tq), :])
                p_s[hh, j] = p
                dp_s[hh, j] = dp
                out.append(dls[hh] + jnp.sum(p * dp, axis=-1, keepdims=True))
            return tuple(out)

        zero_col = jnp.zeros((tq, 1), F32)
        dls = lax.fori_loop(0, i, lambda j, c: first(j, c, False), (zero_col,) * hp)
        dls = first(i, dls, True)

        def second(j, dqs):
            rows = pl.ds(pl.multiple_of(j * tq, tq), tq)
            out = []
            for hh in range(hp):
                p = p_s[hh, j]
                ds = p * (dp_s[hh, j] - dls[hh])
                dsb = ds.astype(BF16)
                dv_ref[hh, rows, :] += _dot_tn(p.astype(BF16), dobs[hh])
                dk_ref[hh, rows, :] += _dot_tn(dsb, q_ref[hh]) * scale
                dc_ref[hh, :, rows] -= jnp.sum(ds, axis=0, keepdims=True)
                out.append(dqs[hh] + _dot(dsb, k_ref[hh, rows, :]))
            return tuple(out)

        dqs = lax.fori_loop(0, i + 1, second, (jnp.zeros((tq, HEAD_DIM), F32),) * hp)
        for hh in range(hp):
            dq_ref[hh] = dqs[hh] * scale

    blk = pl.BlockSpec((hp, tq, HEAD_DIM), lambda h, i: (h, i, 0))
    full = pl.BlockSpec((hp, s, HEAD_DIM), lambda h, i: (h, 0, 0))
    crow_spec = pl.BlockSpec((hp, 1, s), lambda h, i: (h, 0, 0))
    nh = HEADS // hp
    first = lambda: jnp.logical_and(pl.program_id(0) == 0, pl.program_id(1) == 0)
    last = lambda: jnp.logical_and(pl.program_id(0) == nh - 1, pl.program_id(1) == nq - 1)
    return _host_pcall(body, hosted, first, last, n_in=6, n_out=4, n_scratch=2, name="attn_bwd", grid=(nh, nq),
                       in_specs=[blk, full, full, crow_spec, pl.BlockSpec((hp, tq, 1), lambda h, i: (h, i, 0)), blk],
                       out_specs=[blk, full, full, crow_spec],
                       out_shape=[_sds((HEADS, s, HEAD_DIM))] * 3 + [_sds((HEADS, 1, s))],
                       scratch_shapes=[pltpu.VMEM((hp, nq, tq, tq), F32)] * 2,
                       dims=("parallel", "arbitrary"), operands=(qh, kh, vh, crow, lse, doh))


def _prep_bwd(z, dqn, dkn, dv, du, dc, gq, gk, bf, gg):
    s = z.shape[0]
    tm = _row_tile(s)
    nb = s // tm

    def body(z_ref, dqn_ref, dkn_ref, dv_ref, du_ref, dc_ref, gq_ref, gk_ref, bf_ref, gg_ref,
             dz_ref, dgq_ref, dgk_ref, dbf_ref, carry_ref):
        i = pl.program_id(0)

        @pl.when(i == 0)
        def _():
            for r in (dgq_ref, dgk_ref, dbf_ref, carry_ref):
                r[...] = jnp.zeros_like(r)

        gg_m = gg_ref[...]

        def head_norm_bwd(t, g, dn):
            r = lax.rsqrt(_dot_exact_r(t * t, gg_m) * (1.0 / HEAD_DIM) + EPS)
            w = dn * g
            mean_wt = _dot_exact_r(w * t, gg_m) * (1.0 / HEAD_DIM)
            return r * w - t * (r * r * r) * mean_wt, jnp.sum(dn * t * r, axis=0, keepdims=True)

        dq, dgq = head_norm_bwd(z_ref[:, 0:ATTN_W], gq_ref[...], _merge_heads(dqn_ref))
        dk, dgk = head_norm_bwd(z_ref[:, ATTN_W:2 * ATTN_W], gk_ref[...], _merge_heads(dkn_ref))
        dgq_ref[...] += dgq
        dgk_ref[...] += dgk
        row = lax.broadcasted_iota(jnp.int32, (tm, tm), 0)
        col = lax.broadcasted_iota(jnp.int32, (tm, tm), 1)
        triu = (col >= row).astype(BF16)
        dlf = _dot_exact_l(triu, dc_ref[...]) + carry_ref[...]
        carry_ref[...] = dlf[0:1, :]
        df = dlf * _sigmoid(-_forget_logits(z_ref, bf_ref))
        dbf_ref[...] += jnp.sum(df, axis=0, keepdims=True)
        dz_ref[:, 0:ATTN_W] = dq.astype(BF16)
        dz_ref[:, ATTN_W:2 * ATTN_W] = dk.astype(BF16)
        dz_ref[:, 2 * ATTN_W:3 * ATTN_W] = _merge_heads(dv_ref).astype(BF16)
        tail = jnp.concatenate([df[:, :HEADS], du_ref[...], jnp.zeros((tm, Z_COLS - IN_COLS), F32)], axis=-1)
        dz_ref[:, F_COL0:Z_COLS] = tail.astype(BF16)

    row_spec = lambda w: pl.BlockSpec((tm, w), lambda i: (nb - 1 - i, 0))
    const = lambda shape: pl.BlockSpec(shape, lambda i: (0, 0))
    return _pcall(body, name="prep_bwd", grid=(nb,),
                  in_specs=[row_spec(Z_COLS)] + [pl.BlockSpec((HEADS, tm, HEAD_DIM), lambda i: (0, nb - 1 - i, 0))] * 3
                           + [row_spec(ATTN_W), row_spec(LANES), const((1, ATTN_W)),
                              const((1, ATTN_W)), const((1, LANES)), const((ATTN_W, ATTN_W))],
                  out_specs=[row_spec(Z_COLS), const((1, ATTN_W)), const((1, ATTN_W)), const((1, LANES))],
                  out_shape=[_sds((s, Z_COLS), BF16), _sds((1, ATTN_W)), _sds((1, ATTN_W)), _sds((1, LANES))],
                  scratch_shapes=[pltpu.VMEM((1, LANES), F32)], dims=("arbitrary",))(z, dqn, dkn, dv, du, dc, gq, gk, bf, gg)


def _in_norm_bwd(x, g_mix, dh, dx1):
    s = x.shape[0]
    tm = _row_tile(s)

    def body(x_ref, g_ref, dh_ref, dx1_ref, dx_ref, dg_ref):
        i = pl.program_id(0)

        @pl.when(i == 0)
        def _():
            dg_ref[...] = jnp.zeros_like(dg_ref)

        dxn, dg = _rms_bwd(x_ref[...], g_ref[...], dh_ref[...])
        dx_ref[...] = dx1_ref[...] + dxn
        dg_ref[...] += dg

    row = pl.BlockSpec((tm, D_MODEL), lambda i: (i, 0))
    vec = pl.BlockSpec((1, D_MODEL), lambda i: (0, 0))
    return _pcall(body, name="in_norm_bwd", grid=(s // tm,), in_specs=[row, vec, row, row], out_specs=[row, vec],
                  out_shape=[_sds((s, D_MODEL)), _sds((1, D_MODEL))], dims=("arbitrary",))(x, g_mix, dh, dx1)


def _adamw_refs(w_ref, g_ref, m_ref, v_ref, d_ref, mo_ref, vo_ref):
    gv = g_ref[...]
    mn = ADAM_B1 * m_ref[...] + (1.0 - ADAM_B1) * gv
    vn = ADAM_B2 * v_ref[...] + (1.0 - ADAM_B2) * (gv * gv)
    m_hat = mn / (1.0 - ADAM_B1 ** ADAM_STEP)
    v_hat = vn / (1.0 - ADAM_B2 ** ADAM_STEP)
    d_ref[...] = -ADAM_LR * (m_hat / (jnp.sqrt(v_hat) + ADAM_EPS) + ADAM_WD * w_ref[...])
    mo_ref[...] = mn
    vo_ref[...] = vn


def _adamw_small(ws, gs, ms, vs):
    n = len(ws)

    def body(*refs):
        ins, outs = refs[:4 * n], refs[4 * n:]
        for i in range(n):
            _adamw_refs(ins[i], ins[n + i], ins[2 * n + i], ins[3 * n + i], *outs[3 * i:3 * i + 3])

    vm = pl.BlockSpec(memory_space=pltpu.VMEM)
    out_shape = [_sds(w.shape) for w in ws for _ in range(3)]
    return _pallas(body, name="adamw_small", in_specs=[vm] * (4 * n), out_specs=[vm] * (3 * n), out_shape=out_shape,
                   compiler_params=pltpu.CompilerParams(vmem_limit_bytes=VMEM_LIMIT))(*ws, *gs, *ms, *vs)


def _adamw(w, g, m, v, *, name):
    r, c = w.shape
    tr = r
    for cand in (256, 176, 128, 64):
        if r > cand and r % cand == 0:
            tr = cand
            break

    def body(w_ref, g_ref, m_ref, v_ref, d_ref, mo_ref, vo_ref):
        _adamw_refs(w_ref, g_ref, m_ref, v_ref, d_ref, mo_ref, vo_ref)

    spec = pl.BlockSpec((tr, c), lambda i: (i, 0))
    return _pcall(body, name=name, grid=(r // tr,), in_specs=[spec] * 4, out_specs=[spec] * 3,
                  out_shape=[_sds((r, c))] * 3, dims=("parallel",))(w, g, m, v)


def _prefetch_call(body, *, name, grid, in_specs, out_specs, out_shape, operands):
    grid_spec = pltpu.PrefetchScalarGridSpec(num_scalar_prefetch=1, grid=grid, in_specs=in_specs, out_specs=out_specs)
    params = pltpu.CompilerParams(dimension_semantics=("parallel",) * len(grid), vmem_limit_bytes=VMEM_LIMIT)
    return _pallas(body, name=name, grid_spec=grid_spec, out_shape=out_shape, compiler_params=params)(*operands)


def _place_cols(buf, shard, place):
    rows, cols = shard.shape
    tr = 256

    def body(place_ref, s_ref, b_ref, o_ref):
        o_ref[...] = s_ref[...]

    grid_spec = pltpu.PrefetchScalarGridSpec(
        num_scalar_prefetch=1, grid=(rows // tr,),
        in_specs=[pl.BlockSpec((tr, cols), lambda i, p: (i, 0)), pl.BlockSpec(memory_space=pltpu.HBM)],
        out_specs=pl.BlockSpec((tr, cols), lambda i, p: (i, p[0])))
    return _pallas(body, name="place_own_cols", grid_spec=grid_spec, out_shape=_sds(buf.shape, buf.dtype),
                   input_output_aliases={2: 0},
                   compiler_params=pltpu.CompilerParams(dimension_semantics=("parallel",),
                                                        vmem_limit_bytes=VMEM_LIMIT))(place, shard, buf)


def _half_rows_tile(hr):
    return hr if hr <= 256 else 176 if hr % 176 == 0 else 256


def _add_half(g, landed, place, *, name):
    def body(place_ref, g_ref, l_ref, o_ref):
        own = g_ref[0] if len(g_ref.shape) == 4 else g_ref[...]
        o_ref[...] = (own + l_ref[...]).astype(BF16)

    if g.ndim == 4:
        _, _, hr, c = g.shape
        tr = _half_rows_tile(hr)
        blk = (1, tr, c)
        return _prefetch_call(
            body, name=name, grid=(N_CHIPS, hr // tr),
            in_specs=[pl.BlockSpec((1,) + blk, lambda j, i, p: (j, p[1], i, 0)), pl.BlockSpec(blk, lambda j, i, p: (j, i, 0))],
            out_specs=pl.BlockSpec(blk, lambda j, i, p: (j, i, 0)), out_shape=_sds(landed.shape, BF16),
            operands=(place, g, landed))
    hr, c = landed.shape
    tr, tc = 256, _tile(c, 2176)
    nb = hr // tr
    return _prefetch_call(
        body, name=name, grid=(nb, c // tc),
        in_specs=[pl.BlockSpec((tr, tc), lambda i, j, p: (p[1] * nb + i, j)), pl.BlockSpec((tr, tc), lambda i, j, p: (i, j))],
        out_specs=pl.BlockSpec((tr, tc), lambda i, j, p: (i, j)), out_shape=_sds(landed.shape, BF16),
        operands=(place, g, landed))


def _sum_chips(chip_sum, lands, place, *, name, tc, window_stride=0):
    _, hr, c = lands.shape
    tr = _half_rows_tile(hr)
    nb = hr // tr
    ncb = c // tc

    def body(place_ref, own_ref, a_ref, b_ref, c_ref, o_ref):
        own = own_ref[0] if len(own_ref.shape) == 3 else own_ref[...]
        o_ref[...] = ((own.astype(F32) + a_ref[0].astype(F32)) + b_ref[0].astype(F32)) + c_ref[0].astype(F32)

    land = lambda k: pl.BlockSpec((1, tr, tc), lambda i, j, p: ((p[0] + k) % N_CHIPS, i, j))
    if chip_sum.ndim == 3:
        own_spec = land(0)
    else:
        stride = window_stride // tc
        own_spec = pl.BlockSpec((tr, tc), lambda i, j, p: (i, p[0] * stride + j))
    return _prefetch_call(
        body, name=name, grid=(nb, ncb), in_specs=[own_spec, land(1), land(2), land(3)],
        out_specs=pl.BlockSpec((tr, tc), lambda i, j, p: (p[1] * nb + i, j)), out_shape=_sds((2 * hr, c)),
        operands=(place, chip_sum, lands, lands, lands))


_HBM = pl.BlockSpec(memory_space=pltpu.HBM)


def _place():
    x, y, c = lax.axis_index("x"), lax.axis_index("y"), lax.axis_index("c")
    chips = [(1 - x, y), (x, 1 - y), (1 - x, 1 - y)]
    return x, y, c, chips


def _rcopy(src, dst, send_sem, recv_sem, to):
    return pltpu.make_async_remote_copy(src_ref=src, dst_ref=dst, send_sem=send_sem, recv_sem=recv_sem,
                                        device_id=to, device_id_type=MESH)


N_BIG = 5
UP_COLS = 2 * D_FF // N_CHIPS
IN_WINDOW = 640
IN_STRIDE = 512


class _Hosted:
    def __init__(self, operands, out_shapes, n_sems, start, finish, aliases=None, local_sems=0):
        self.operands, self.out_shapes, self.n_sems = list(operands), list(out_shapes), n_sems
        self.start, self.finish, self.aliases, self.local_sems = start, finish, dict(aliases or {}), local_sems

    def scratch(self):
        return ([pltpu.SemaphoreType.DMA((self.n_sems,)), pltpu.SemaphoreType.DMA((self.n_sems,))]
                + [pltpu.SemaphoreType.DMA] * self.local_sems)


def _both(a, b):
    na, nao, nas = len(a.operands), len(a.out_shapes), len(a.scratch())

    def start(ins, outs, sems):
        a.start(ins[:na], outs[:nao], sems[:nas])
        b.start(ins[na:], outs[nao:], sems[nas:])

    def finish(ins, outs, sems):
        a.finish(ins[:na], outs[:nao], sems[:nas])
        b.finish(ins[na:], outs[nao:], sems[nas:])

    both = _Hosted(a.operands + b.operands, a.out_shapes + b.out_shapes, 0, start, finish,
                   aliases={**a.aliases, **{na + i: nao + o for i, o in b.aliases.items()}})
    both.scratch = lambda: a.scratch() + b.scratch()
    return both


def _run_hosted(hosted, *, name):
    n_in, n_out = len(hosted.operands), len(hosted.out_shapes)

    def body(*refs):
        parts = (refs[:n_in], refs[n_in:n_in + n_out], refs[n_in + n_out:])
        hosted.start(*parts)
        hosted.finish(*parts)

    return _pallas(body, name=name, in_specs=[_HBM] * n_in, out_specs=[_HBM] * n_out, out_shape=hosted.out_shapes,
                   input_output_aliases=hosted.aliases, scratch_shapes=hosted.scratch())(*hosted.operands)


def _host_pcall(core_body, hosted, first, last, *, n_in, n_out, n_scratch, name, grid, in_specs, out_specs, out_shape,
                scratch_shapes, dims, operands):
    if hosted is None:
        outs = _pcall(core_body, name=name, grid=grid, in_specs=in_specs, out_specs=out_specs, out_shape=out_shape,
                      scratch_shapes=scratch_shapes, dims=dims)(*operands)
        return outs, []
    hi, ho = len(hosted.operands), len(hosted.out_shapes)

    def body(*refs):
        a, b = n_in, n_in + hi
        c, d = b + n_out, b + n_out + ho
        e = d + n_scratch
        parts = (refs[a:b], refs[c:d], refs[e:])

        @pl.when(first())
        def _():
            hosted.start(*parts)

        core_body(*refs[:a], *refs[b:c], *refs[d:e])

        @pl.when(last())
        def _():
            hosted.finish(*parts)

    params = pltpu.CompilerParams(dimension_semantics=("arbitrary",) * len(grid), vmem_limit_bytes=VMEM_LIMIT)
    outs = _pallas(body, name=name, grid=grid, in_specs=list(in_specs) + [_HBM] * hi, out_specs=list(out_specs) + [_HBM] * ho,
                   out_shape=list(out_shape) + hosted.out_shapes, scratch_shapes=list(scratch_shapes) + hosted.scratch(),
                   input_output_aliases={n_in + a: n_out + b for a, b in hosted.aliases.items()},
                   compiler_params=params)(*operands, *hosted.operands)
    return outs[:n_out], outs[n_out:]


WHOLE_HALF = (0, 1, 1)


def _band_rows(src, hc, band):
    first, count, of = band
    hr = src.shape[0] // 2
    return pl.ds(hc * hr + first * (hr // of), count * (hr // of))


def _gather_slot(src, out, chip, hc, band=WHOLE_HALF):
    cols = src.shape[1]
    if len(out.shape) == 2:
        return out.at[_band_rows(src, hc, band), pl.ds(pl.multiple_of(chip * cols, LANES), cols)]
    return out.at[chip, _band_rows(src, hc, band), :]


def _gathered_shape(shard, by_cols):
    if by_cols:
        return _sds((shard.shape[0], N_CHIPS * shard.shape[1]), shard.dtype)
    return _sds((N_CHIPS,) + shard.shape, shard.dtype)


def _plan_gather_ici(shards, by_cols, whole=(), bands=None, into=None):
    n = len(shards)
    bands = bands or [WHOLE_HALF] * n
    into = into or [None] * n
    given = [w for w in range(n) if into[w] is not None]
    n_ops = n + len(whole)

    def copies(ins, outs, sems):
        send_sems, recv_sems = sems[0], sems[1]
        x, y, c, chips = _place()
        me = 2 * x + y
        sends, waits = [], []
        for w in range(n + len(whole)):
            for k, (cx, cy) in enumerate(chips):
                sem = (send_sems.at[3 * w + k], recv_sems.at[3 * w + k])
                if w < n:
                    sends.append(_rcopy(ins[w].at[_band_rows(ins[w], c, bands[w]), :],
                                        _gather_slot(ins[w], outs[w], me, c, bands[w]), *sem, (cx, cy, c)))
                    landed = _gather_slot(ins[w], outs[w], 2 * cx + cy, c, bands[w])
                else:
                    sends.append(_rcopy(ins[w], outs[w].at[me], *sem, (cx, cy, c)))
                    landed = outs[w].at[2 * cx + cy]
                waits.append(_rcopy(landed, landed, *sem, (cx, cy, c)))
        return sends, waits

    def start(ins, outs, sems):
        for cp in copies(ins, outs, sems)[0]:
            cp.start()

    def finish(ins, outs, sems):
        sends, waits = copies(ins, outs, sems)
        for cp in waits:
            cp.wait_recv()
        for cp in sends:
            cp.wait_send()

    out_shapes = [_gathered_shape(s, bc) for s, bc in zip(shards, by_cols)] + [_sds((N_CHIPS,) + a.shape, a.dtype) for a in whole]
    return _Hosted(list(shards) + list(whole) + [into[w] for w in given], out_shapes, 3 * n_ops, start, finish,
                   aliases={n_ops + i: w for i, w in enumerate(given)})


def _plan_gather_d2d(bufs, shard_shapes, bands=None):
    n = len(bufs)
    bands = bands or [WHOLE_HALF] * n

    def copies(ins, outs, sems):
        send_sems, recv_sems = sems
        x, y, c, chips = _place()
        sibling = (x, y, 1 - c)
        sends, waits = [], []
        for w in range(n):
            for k, (cx, cy) in enumerate(chips):
                sem = (send_sems.at[3 * w + k], recv_sems.at[3 * w + k])
                landed = _gather_slot(shard_shapes[w], outs[w], 2 * cx + cy, c, bands[w])
                other = _gather_slot(shard_shapes[w], outs[w], 2 * cx + cy, 1 - c, bands[w])
                sends.append(_rcopy(landed, landed, *sem, sibling))
                waits.append(_rcopy(other, other, *sem, sibling))
        return sends, waits

    def start(ins, outs, sems):
        for cp in copies(ins, outs, sems)[0]:
            cp.start()

    def finish(ins, outs, sems):
        sends, waits = copies(ins, outs, sems)
        for cp in waits:
            cp.wait_recv()
        for cp in sends:
            cp.wait_send()

    return _Hosted(bufs, [_sds(b.shape, b.dtype) for b in bufs], 3 * n, start, finish, aliases={w: w for w in range(n)})


def _plan_swap(grads):
    def copies(ins, outs, sems):
        send_sems, recv_sems = sems
        x, y, c, _ = _place()
        cps = []
        for w, g_ref in enumerate(ins):
            if len(g_ref.shape) == 4:
                theirs = g_ref.at[:, 1 - c]
            else:
                hr = g_ref.shape[0] // 2
                theirs = g_ref.at[pl.ds((1 - c) * hr, hr), :]
            cps.append(_rcopy(theirs, outs[w], send_sems.at[w], recv_sems.at[w], (x, y, 1 - c)))
        return cps

    def start(ins, outs, sems):
        for cp in copies(ins, outs, sems):
            cp.start()

    def finish(ins, outs, sems):
        for cp in copies(ins, outs, sems):
            cp.wait()

    out_shapes = [_sds((g.shape[0], g.shape[2], g.shape[3])) if g.ndim == 4 else _sds((g.shape[0] // 2, g.shape[1]))
                  for g in grads]
    return _Hosted(grads, out_shapes, len(grads), start, finish)


def _plan_scatter(chip_sums, windows):
    def copies(ins, outs, sems):
        send_sems, recv_sems = sems
        x, y, c, chips = _place()
        me = 2 * x + y
        sends, waits = [], []
        for w, s_ref in enumerate(ins):
            for k, (cx, cy) in enumerate(chips):
                tgt = 2 * cx + cy
                if windows[w] is not None:
                    stride, width = windows[w]
                    part = s_ref.at[:, pl.ds(pl.multiple_of(tgt * stride, LANES), width)]
                else:
                    part = s_ref.at[tgt]
                sem = (send_sems.at[3 * w + k], recv_sems.at[3 * w + k])
                sends.append(_rcopy(part, outs[w].at[me], *sem, (cx, cy, c)))
                slot = outs[w].at[tgt]
                waits.append(_rcopy(slot, slot, *sem, (cx, cy, c)))
        return sends, waits

    def start(ins, outs, sems):
        for cp in copies(ins, outs, sems)[0]:
            cp.start()

    def finish(ins, outs, sems):
        sends, waits = copies(ins, outs, sems)
        for cp in waits:
            cp.wait_recv()
        for cp in sends:
            cp.wait_send()

    out_shapes = [_sds((N_CHIPS, s.shape[0], win[1]), BF16) if win is not None else _sds(s.shape, BF16)
                  for s, win in zip(chip_sums, windows)]
    return _Hosted(chip_sums, out_shapes, 3 * len(chip_sums), start, finish)


def _plan_join(reds):
    def copies(ins, outs, sems):
        send_sems, recv_sems = sems
        x, y, c, _ = _place()
        sends, waits = [], []
        for w, out in enumerate(outs):
            hr = out.shape[0] // 2
            mine = out.at[pl.ds(c * hr, hr), :]
            theirs = out.at[pl.ds((1 - c) * hr, hr), :]
            sends.append(_rcopy(mine, mine, send_sems.at[w], recv_sems.at[w], (x, y, 1 - c)))
            waits.append(_rcopy(theirs, theirs, send_sems.at[w], recv_sems.at[w], (x, y, 1 - c)))
        return sends, waits

    def start(ins, outs, sems):
        for cp in copies(ins, outs, sems)[0]:
            cp.start()

    def finish(ins, outs, sems):
        sends, waits = copies(ins, outs, sems)
        for cp in waits:
            cp.wait_recv()
        for cp in sends:
            cp.wait_send()

    return _Hosted(reds, [_sds(r.shape) for r in reds], len(reds), start, finish, aliases={w: w for w in range(len(reds))})


def _allreduce_small(v):
    m_per = v.shape[0]

    def body(v_ref, out_ref, all_ref, send_sems, recv_sems, local_sem):
        x, y, c, chips = _place()
        me, sibling = (x, y, c), (x, y, 1 - c)

        def rows(px, py, pc):
            return all_ref.at[pl.ds((4 * px + 2 * py + pc) * m_per, m_per), :]

        def copy(k, block, to, src=None):
            return _rcopy(rows(*block) if src is None else src, rows(*block), send_sems.at[k], recv_sems.at[k], to)

        mine = pltpu.make_async_copy(v_ref, rows(*me), local_sem)
        mine.start()
        first = [copy(0, me, sibling, src=v_ref)]
        first += [copy(1 + k, me, (*chip, c), src=v_ref) for k, chip in enumerate(chips)]
        for cp in first:
            cp.start()
        passed = [copy(4 + k, (*chip, c), sibling) for k, chip in enumerate(chips)]
        for k, chip in enumerate(chips):
            copy(1 + k, (*chip, c), me).wait_recv()
            passed[k].start()
        copy(0, sibling, me).wait_recv()
        for k, chip in enumerate(chips):
            copy(4 + k, (*chip, 1 - c), me).wait_recv()
        for cp in first + passed:
            cp.wait_send()
        mine.wait()
        acc = all_ref[pl.ds(0, m_per), :]
        for d in range(1, 8):
            acc = acc + all_ref[pl.ds(d * m_per, m_per), :]
        out_ref[...] = acc

    vm = pl.BlockSpec(memory_space=pltpu.VMEM)
    return _pallas(body, name="allreduce_small", in_specs=[vm], out_specs=vm, out_shape=_sds((m_per, LANES)),
                          scratch_shapes=[pltpu.VMEM((8 * m_per, LANES), F32), pltpu.SemaphoreType.DMA((7,)),
                                          pltpu.SemaphoreType.DMA((7,)), pltpu.SemaphoreType.DMA],
                          compiler_params=pltpu.CompilerParams(vmem_limit_bytes=VMEM_LIMIT))(v)


def _block_diag(blocks):
    j, g, a, b = blocks.shape
    eye = jnp.eye(g, dtype=bool)[None, :, None, :, None]
    return jnp.where(eye, blocks[:, :, :, None, :], jnp.zeros((), blocks.dtype)).reshape(j, g * a, g * b)


def _diag_blocks(m, a, b):
    j = m.shape[0]
    g = m.shape[1] // a
    t = m.reshape(j, g, a, g, b)
    eye = jnp.eye(g, dtype=bool)[None, :, None, :, None]
    return jnp.sum(jnp.where(eye, t, 0.0), axis=3)


def _pack_rows(parts, rows, dtype):
    used = sum(p.shape[0] for p in parts)
    return jnp.concatenate([p.astype(dtype) for p in parts] + [jnp.zeros((rows - used, D_MODEL), dtype)], axis=0)


_SMALL = (("g_mix", (1024,)), ("b_f", (8,)), ("g_q", (64,)), ("g_k", (64,)), ("lambda_re", (32, 64)),
          ("lambda_im", (32, 64)), ("log_step", (32,)), ("b_re", (32, 64, 16)), ("b_im", (32, 64, 16)),
          ("c_re", (32, 16, 64)), ("c_im", (32, 16, 64)), ("d_skip", (32, 16)), ("b_glu", (512,)),
          ("g_attn_out", (512,)), ("g_ssm_out", (512,)), ("g_ffn", (1024,)), ("conv_b", (5632,)))


def _small_rows(shape):
    return -(-math.prod(shape) // LANES)


def _pack_small(arrs, extra=()):
    parts = []
    for a in list(arrs) + list(extra):
        flat = a.reshape(-1)
        rows = -(-flat.shape[0] // LANES)
        parts.append(jnp.pad(flat, (0, rows * LANES - flat.shape[0])).reshape(rows, LANES))
    total = sum(p.shape[0] for p in parts)
    pad = -total % SUBLANES
    if pad:
        parts.append(jnp.zeros((pad, LANES), F32))
    return jnp.concatenate(parts, axis=0)


def _unpack_small(buf, shapes):
    out, r = [], 0
    for shape in shapes:
        n = math.prod(shape)
        rows = -(-n // LANES)
        out.append(buf[r:r + rows].reshape(-1)[:n].reshape(shape))
        r += rows
    return out


def _halves(t):
    return t.reshape(N_CHIPS, 2, t.shape[0] // (2 * N_CHIPS), t.shape[1])


class _MeshComm:
    def __init__(self, args):
        x, y, self.core = lax.axis_index("x"), lax.axis_index("y"), lax.axis_index("c")
        self.chip = 2 * x + y
        self.place = jnp.stack([self.chip, self.core]).astype(jnp.int32)
        self.shards = {n: args[n].astype(BF16) for n in ("w_in", "w_glu", "w_out", "w_up", "w_down")}
        self.conv_w = args["conv_w"]

    def _own(self, stacked, mine):
        return lax.dynamic_update_slice(stacked, mine[None], (self.chip,) + (0,) * mine.ndim)

    def w_in(self):
        sh = self.shards["w_in"]
        (buf,) = _run_hosted(_plan_gather_ici([sh], [False]), name="gather_w_in")
        (buf,) = _run_hosted(_plan_gather_d2d([buf], [sh]), name="pass_w_in")
        whole = self._own(buf, sh).transpose(1, 0, 2).reshape(D_MODEL, IN_COLS)
        return jnp.pad(whole, ((0, 0), (0, Z_COLS - IN_COLS)))

    def gather_first(self):
        self.mid = [self.shards[n] for n in ("w_glu", "w_out", "w_down")]
        return _plan_gather_ici(self.mid + [self.shards["w_up"]], [False, False, False, True], whole=[self.conv_w],
                                bands=[WHOLE_HALF] * 3 + [(0, 1, 4)])

    def gather_second(self, landed):
        self.g_cw = landed[4]
        return _both(_plan_gather_d2d(list(landed[:3]), self.mid),
                     _plan_gather_ici([self.shards["w_up"]], [True], bands=[(1, 3, 4)], into=[landed[3]]))

    def weights(self, gathered):
        g_glu, g_out, g_down = gathered[:3]
        own = self._own
        return (own(g_glu, self.mid[0]).reshape(SSM_W, SSM_W), own(g_out, self.mid[1]).reshape(D_MODEL, D_MODEL),
                own(g_down, self.mid[2]).reshape(D_FF, D_MODEL),
                own(self.g_cw, self.conv_w).transpose(1, 0, 2).reshape(3, 2 * D_FF))

    def gather_third(self, gathered):
        return _plan_gather_d2d([gathered[3]], [self.shards["w_up"]])

    def w_up(self, passed):
        return _place_cols(passed[0], self.shards["w_up"], self.place)

    def swap(self, d_w_down, d_w_up, d_w_glu, d_w_out):
        self.early = [_halves(d_w_down), d_w_up, _halves(d_w_glu), _halves(d_w_out)]
        return _plan_swap(self.early)

    def scatter(self, landed):
        self.early_sums = [_add_half(g, l, self.place, name="add_" + n)
                           for g, l, n in zip(self.early, landed, ("w_down", "w_up", "w_glu", "w_out"))]
        return _plan_scatter(self.early_sums, [None, (UP_COLS, UP_COLS), None, None])

    def reduce(self, early_lands, d_w_in):
        d_in = d_w_in
        (landed,) = _run_hosted(_plan_swap([d_in]), name="swap_halves")
        sum_in = _add_half(d_in, landed, self.place, name="add_w_in")
        (land_in,) = _run_hosted(_plan_scatter([sum_in], [(IN_STRIDE, IN_WINDOW)]), name="scatter_chips")
        es, el = self.early_sums, early_lands
        todo = [(sum_in, land_in, "w_in", LANES, IN_STRIDE), (es[2], el[2], "w_glu", SSM_W, 0),
                (es[3], el[3], "w_out", D_MODEL, 0), (es[1], el[1], "w_up", UP_COLS, UP_COLS),
                (es[0], el[0], "w_down", D_MODEL, 0)]
        reds = _run_hosted(_plan_join([_sum_chips(s, l, self.place, name="sum_" + n, tc=tc, window_stride=st)
                                       for s, l, n, tc, st in todo]), name="join_halves")
        g_big = dict(zip(("w_in", "w_glu", "w_out", "w_up", "w_down"), reds))
        g_big["w_in"] = lax.dynamic_slice_in_dim(reds[0], 2 * self.chip, IN_COLS // N_CHIPS, axis=1)
        return g_big


def _local_step(x, tgt, p, comm):
    s = x.shape[0]
    row = lambda v: v.reshape(1, -1)
    g_mix, g_ffn = row(p["g_mix"]), row(p["g_ffn"])
    g_att, g_ssm, b_glu, conv_b = row(p["g_attn_out"]), row(p["g_ssm_out"]), row(p["b_glu"]), row(p["conv_b"])
    gq = row(jnp.tile(p["g_q"], HEADS))
    gk = row(jnp.tile(p["g_k"], HEADS))
    bf = row(jnp.pad(p["b_f"], (0, LANES - HEADS)))
    gg = jnp.kron(jnp.eye(HEADS, dtype=F32), jnp.ones((HEAD_DIM, HEAD_DIM), F32)).astype(BF16)
    dsk = row(p["d_skip"])

    rep = lambda a: jnp.repeat(a, SSM_GROUP, axis=0)
    lr, li = rep(p["lambda_re"]), rep(p["lambda_im"])
    ls = rep(jnp.broadcast_to(p["log_step"][:, None], (SSM_GROUPS, SSM_STATE)))
    bt_re = p["b_re"].transpose(0, 2, 1).reshape(_PARAM_SHAPE)
    bt_im = p["b_im"].transpose(0, 2, 1).reshape(_PARAM_SHAPE)
    a_re_rep, a_im_rep, bb_re, bb_im = _ssm_params(lr, li, ls, bt_re, bt_im)
    ar = a_re_rep[::SSM_GROUP].reshape(SSM_CHUNKS, 1, CHUNK_S)
    ai = a_im_rep[::SSM_GROUP].reshape(SSM_CHUNKS, 1, CHUNK_S)
    chunked = lambda t: t.reshape(SSM_CHUNKS, SSM_GROUPS // SSM_CHUNKS, SSM_GROUP, SSM_STATE)
    bbr = _block_diag(chunked(bb_re)).astype(BF16)
    bbi = _block_diag(chunked(bb_im)).astype(BF16)
    to_cc = lambda c: _block_diag(chunked(c).transpose(0, 1, 3, 2)).astype(BF16)
    ccr, cci = to_cc(p["c_re"]), to_cc(p["c_im"])

    w_in_r = comm.w_in()
    hb, z = _in_proj(x, g_mix, w_in_r)
    qh, kh, vh, ub, uf, c128 = _attn_prep(z, gq, gk, bf, gg)
    crow = c128[:, :HEADS].T.reshape(HEADS, 1, s)
    (oh, lse), landed = _attn_fwd(qh, kh, vh, crow, comm.gather_first())
    (xr, xi, y), gathered = _ssm_fwd(ub, uf, bbr, bbi, ar, ai, ccr, cci, dsk, comm.gather_second(landed))
    w_glu_b, w_out_b, w_down_b, conv_w_full = comm.weights(gathered)
    (x1, mixb, h2b), passed = _mix_out(y, oh, x, w_glu_b, b_glu, g_att, g_ssm, w_out_b, g_ffn, comm.gather_third(gathered))
    w_up_b = comm.w_up(passed)
    up = _mm(h2b, w_up_b, name="ffn_up", tm=1024, tn=1408, tk=1024)
    act = _conv_act(up, conv_w_full, conv_b)
    dy, dyb, loss_blk = _down_loss(act, w_down_b, x1, tgt)

    d_w_down = _mm(act, dyb, ta=True, name="d_w_down", tm=1408, tn=1024, tk=2048)
    dact = _mm(dyb, w_down_b, tb=True, name="d_act", tm=1024, tn=1408, tk=1024)
    dupb, dcw = _conv_act_bwd(up, dact, conv_w_full, conv_b)
    d_w_up = _mm(h2b, dupb, ta=True, b_parts=2, name="d_w_up", tm=1024, tn=1408, tk=2048)
    dh2 = _mm(dupb, w_up_b, tb=True, a_parts=2, name="d_h2", tm=1024, tn=1024, tk=1408)
    dx1, dx1b, doh, dys, d_w_glu, d_g_ffn, d_g_att, d_g_ssm, d_b_glu = _mix_bwd(
        dy, dh2, x1, g_ffn, w_out_b, y, oh, w_glu_b, b_glu, g_att, g_ssm)
    d_w_out = _mm(mixb, dx1b, ta=True, name="d_w_out", tm=1024, tn=1024, tk=2048)
    (du, dbbr, dbbi, dccr, dcci, dar, dai, dd), swapped = _ssm_bwd(dys, uf, ub, xr, xi, bbr, bbi, ar, ai, ccr, cci, dsk,
                                                                comm.swap(d_w_down, d_w_up, d_w_glu, d_w_out))
    (dqh, dkh, dvh, dcrow), early_lands = _attn_bwd(qh, kh, vh, crow, lse, doh, comm.scatter(swapped))
    dc128 = jnp.pad(dcrow.reshape(HEADS, s).T, ((0, 0), (0, LANES - HEADS)))
    dzb, d_gq, d_gk, d_bf = _prep_bwd(z, dqh, dkh, dvh, du, dc128, gq, gk, bf, gg)
    d_w_in_r = _mm(hb, dzb, ta=True, name="d_w_in", tm=512, tn=Z_COLS, tk=2048)
    dh = _mm(dzb, w_in_r, tb=True, name="d_h", tm=1024, tn=1024, tk=Z_COLS)
    dx, d_g_mix = _in_norm_bwd(x, g_mix, dh, dx1)

    unchunk = lambda t: t.reshape(_PARAM_SHAPE)
    dbb_re = unchunk(_diag_blocks(dbbr, SSM_GROUP, SSM_STATE))
    dbb_im = unchunk(_diag_blocks(dbbi, SSM_GROUP, SSM_STATE))
    first_row = (jnp.arange(_PARAM_SHAPE[0]) % SSM_GROUP == 0)[:, None]
    da_re = jnp.where(first_row, rep(dar.reshape(SSM_GROUPS, SSM_STATE)), 0.0)
    da_im = jnp.where(first_row, rep(dai.reshape(SSM_GROUPS, SSM_STATE)), 0.0)
    expand_t = (jnp.arange(SSM_GROUPS)[:, None] == (jnp.arange(_PARAM_SHAPE[0]) // SSM_GROUP)[None, :]).astype(BF16)
    d_lr, d_li, d_ls, d_bt_re, d_bt_im = _ssm_params_bwd(lr, li, ls, bt_re, bt_im, da_re, da_im, dbb_re, dbb_im, expand_t)
    from_bt = lambda t: t.reshape(SSM_GROUPS, SSM_GROUP, SSM_STATE).transpose(0, 2, 1)
    from_cc = lambda t: _diag_blocks(t, SSM_STATE, SSM_GROUP).transpose(0, 1, 3, 2).reshape(SSM_GROUPS, SSM_GROUP, SSM_STATE)

    small = {
        "g_mix": d_g_mix, "b_f": d_bf[0, :HEADS], "g_q": d_gq.reshape(HEADS, HEAD_DIM).sum(0),
        "g_k": d_gk.reshape(HEADS, HEAD_DIM).sum(0), "lambda_re": d_lr, "lambda_im": d_li, "log_step": d_ls,
        "b_re": from_bt(d_bt_re), "b_im": from_bt(d_bt_im), "c_re": from_cc(dccr), "c_im": from_cc(dcci),
        "d_skip": dd, "b_glu": d_b_glu, "g_attn_out": d_g_att, "g_ssm_out": d_g_ssm, "g_ffn": d_g_ffn,
        "conv_b": dcw[:, 3],
    }
    big = {"w_in": d_w_in_r, "w_glu": d_w_glu, "w_out": d_w_out, "w_up": d_w_up, "w_down": d_w_down}
    return loss_blk[0, 0], dx, big, small, dcw[:, 0:3].transpose(1, 0, 2).reshape(3, 2 * D_FF), early_lands


def kernel(x, g_mix, w_in, b_f, g_q, g_k, lambda_re, lambda_im, log_step, b_re, b_im, c_re, c_im, d_skip, w_glu, b_glu, g_attn_out, g_ssm_out, w_out, g_ffn, w_up, conv_w, conv_b, w_down, loss_target, m_g_mix, m_w_in, m_b_f, m_g_q, m_g_k, m_lambda_re, m_lambda_im, m_log_step, m_b_re, m_b_im, m_c_re, m_c_im, m_d_skip, m_w_glu, m_b_glu, m_g_attn_out, m_g_ssm_out, m_w_out, m_g_ffn, m_w_up, m_conv_w, m_conv_b, m_w_down, v_g_mix, v_w_in, v_b_f, v_g_q, v_g_k, v_lambda_re, v_lambda_im, v_log_step, v_b_re, v_b_im, v_c_re, v_c_im, v_d_skip, v_w_glu, v_b_glu, v_g_attn_out, v_g_ssm_out, v_w_out, v_g_ffn, v_w_up, v_conv_w, v_conv_b, v_w_down):
    args = dict(locals())
    order = ["g_mix", "w_in", "b_f", "g_q", "g_k", "lambda_re", "lambda_im", "log_step", "b_re", "b_im", "c_re", "c_im",
             "d_skip", "w_glu", "b_glu", "g_attn_out", "g_ssm_out", "w_out", "g_ffn", "w_up", "conv_w", "conv_b", "w_down"]
    comm = _MeshComm(args)
    chip = comm.chip
    loss_part, dx, big, small, d_conv_w, early_lands = _local_step(x[0], loss_target[0], args, comm)
    loss = lax.psum(loss_part, ("x", "y", "c"))

    g_big = comm.reduce(early_lands, big["w_in"])

    small_names = [n for n, _ in _SMALL]
    small_shapes = [sh for _, sh in _SMALL]
    gsum = _allreduce_small(_pack_small([small[n] for n in small_names], extra=[d_conv_w]))
    g_small = _unpack_small(gsum, small_shapes + [(3, 2 * D_FF)])
    g_conv_w = lax.dynamic_slice_in_dim(g_small[-1], chip * (2 * D_FF // N_CHIPS), 2 * D_FF // N_CHIPS, axis=1)
    g_small = dict(zip(small_names, g_small[:-1]))

    grad, delta, new_m, new_v = {}, {}, {}, {}
    for n in ("w_in", "w_glu", "w_out", "w_up", "w_down"):
        grad[n] = g_big[n]
        delta[n], new_m[n], new_v[n] = _adamw(args[n], g_big[n], args["m_" + n], args["v_" + n], name="adamw_" + n)
    grad["conv_w"] = g_conv_w
    delta["conv_w"], new_m["conv_w"], new_v["conv_w"] = _adamw(conv_w, g_conv_w, m_conv_w, v_conv_w, name="adamw_conv_w")
    stepped = _adamw_small([args[n] for n in small_names], [g_small[n] for n in small_names],
                           [args["m_" + n] for n in small_names], [args["v_" + n] for n in small_names])
    for i, n in enumerate(small_names):
        grad[n] = g_small[n]
        delta[n], new_m[n], new_v[n] = stepped[3 * i:3 * i + 3]

    return (loss, dx[None], *[grad[n] for n in order], *[delta[n] for n in order], *[new_m[n] for n in order],
            *[new_v[n] for n in order])
```

```python
import math

import jax
import jax.numpy as jnp
from jax import lax
from jax.experimental import pallas as pl
from jax.experimental.pallas import tpu as pltpu

F32 = jnp.float32
BF16 = jnp.bfloat16

D_MODEL = 1024
HEADS = 8
HEAD_DIM = 64
ATTN_W = 512
SSM_W = 512
SSM_GROUPS = 32
SSM_GROUP = 16
SSM_STATE = 64
N_STATE = SSM_GROUPS * SSM_STATE
D_FF = 2816
IN_COLS = 2056
Z_COLS = 2176
F_COL0 = 1536
U_COL0 = 1544
EPS = 1e-6
NEG_INF = -1e30
N_CHIPS = 4
LANES = 128
SUBLANES = 8
SSM_CHUNKS = 2
CHUNK_U = SSM_W // SSM_CHUNKS
CHUNK_S = N_STATE // SSM_CHUNKS
HEADS_PER_STEP = 4
STRIP = 128
N_STRIPS = D_FF // STRIP

ROWS_IN, ROWS_GLU, ROWS_OUT, ROWS_UP, ROWS_DOWN = 514, 64, 256, 1408, 704
OFF_GLU = ROWS_IN
OFF_OUT = OFF_GLU + ROWS_GLU
OFF_UP = OFF_OUT + ROWS_OUT
OFF_DOWN = OFF_UP + ROWS_UP
OFF_SPARE = OFF_DOWN + ROWS_DOWN
PACK_ROWS = 2976
HALF_ROWS = PACK_ROWS // 2
CONVW_ROWS = 9

ADAM_LR = 0.001
ADAM_B1 = 0.9
ADAM_B2 = 0.999
ADAM_EPS = 1e-08
ADAM_WD = 0.01
ADAM_STEP = 10

VMEM_LIMIT = 56 * 1024 * 1024
MESH = pl.DeviceIdType.MESH


def _pallas(body, **kw):
    return pl.pallas_call(body, **kw)


def _pcall(body, *, name, out_shape, in_specs, out_specs, grid=(), scratch_shapes=(), dims=None):
    params = pltpu.CompilerParams(dimension_semantics=dims, vmem_limit_bytes=VMEM_LIMIT)
    return _pallas(body, name=name, grid=grid, in_specs=in_specs, out_specs=out_specs,
                   out_shape=out_shape, scratch_shapes=scratch_shapes, compiler_params=params)


def _sds(shape, dtype=F32):
    return jax.ShapeDtypeStruct(shape, dtype)


def _dot(a, b):
    return jnp.dot(a, b, preferred_element_type=F32)


def _dot_nt(a, b):
    return lax.dot_general(a, b, (((1,), (1,)), ((), ())), preferred_element_type=F32)


def _dot_tn(a, b):
    return lax.dot_general(a, b, (((0,), (0,)), ((), ())), preferred_element_type=F32)


def _split3(x):
    hi = x.astype(BF16)
    r = x - hi.astype(F32)
    mid = r.astype(BF16)
    lo = (r - mid.astype(F32)).astype(BF16)
    return hi, mid, lo


def _dot_exact_r(x, m01):
    hi, mid, lo = _split3(x)
    return _dot(hi, m01) + _dot(mid, m01) + _dot(lo, m01)


def _dot_exact_l(m01, x):
    hi, mid, lo = _split3(x)
    return _dot(m01, hi) + _dot(m01, mid) + _dot(m01, lo)


def _sigmoid(x):
    return 1.0 / (1.0 + jnp.exp(-x))


def _rms(x, g):
    r = lax.rsqrt(jnp.mean(x * x, axis=-1, keepdims=True) + EPS)
    return x * r * g


def _rms_bwd(x, g, dy):
    r = lax.rsqrt(jnp.mean(x * x, axis=-1, keepdims=True) + EPS)
    w = dy * g
    dx = r * w - x * (r * r * r) * jnp.mean(w * x, axis=-1, keepdims=True)
    dg = jnp.sum(dy * x * r, axis=0, keepdims=True)
    return dx, dg


_GELU_K = math.sqrt(2.0 / math.pi)
_GELU_C = 0.044715


def _gelu(y):
    return y * (0.5 * (1.0 + jnp.tanh(_GELU_K * (y + _GELU_C * (y * y * y)))))


def _gelu_grad(y):
    t = jnp.tanh(_GELU_K * (y + _GELU_C * (y * y * y)))
    return 0.5 * (1.0 + t) + 0.5 * y * (1.0 - t * t) * (_GELU_K * (1.0 + 3.0 * _GELU_C * y * y))


def _tile(n, pref):
    if n <= pref:
        return n
    divs = [t for t in range(LANES, n + 1, LANES) if n % t == 0]
    below = [t for t in divs if t <= pref]
    if below and 2 * below[-1] >= pref:
        return below[-1]
    above = [t for t in divs if t > pref]
    return above[0] if above else n


def _row_tile(s):
    return min(256, s)


def _mm(a, b, *, name, tm, tn, tk, ta=False, tb=False, a_parts=1, b_parts=1, carry=False, hosted=None):
    if a_parts > 1:
        m, kk = a.shape[1], a.shape[2] * a_parts
    elif ta:
        kk, m = a.shape
    else:
        m, kk = a.shape
    if b_parts > 1:
        n = b.shape[2] * b_parts
    else:
        n = b.shape[0] if tb else b.shape[1]
    tm, tn, tk = _tile(m, tm), _tile(n // b_parts, tn), _tile(kk // a_parts, tk)
    k_per, n_per = kk // a_parts // tk, n // b_parts // tn

    def body(a_ref, b_ref, o_ref):
        k = pl.program_id(2)
        if ta:
            part = _dot_tn(a_ref[...], b_ref[...])
        elif tb:
            part = _dot_nt(a_ref[...], b_ref[...])
        else:
            part = _dot(a_ref[...], b_ref[...])

        @pl.when(k == 0)
        def _():
            o_ref[...] = part

        @pl.when(k > 0)
        def _():
            o_ref[...] += part

    if a_parts > 1:
        a_spec = pl.BlockSpec((None, tm, tk), lambda i, j, k: (k // k_per, i, k % k_per))
    else:
        a_spec = pl.BlockSpec((tk, tm), lambda i, j, k: (k, i)) if ta else pl.BlockSpec((tm, tk), lambda i, j, k: (i, k))
    if b_parts > 1:
        b_spec = pl.BlockSpec((None, tk, tn), lambda i, j, k: (j // n_per, k, j % n_per))
    else:
        b_spec = pl.BlockSpec((tn, tk), lambda i, j, k: (j, k)) if tb else pl.BlockSpec((tk, tn), lambda i, j, k: (k, j))
    grid = (m // tm, n // tn, kk // tk)
    at = lambda step: (lambda: jnp.logical_and(jnp.logical_and(pl.program_id(0) == step[0], pl.program_id(1) == step[1]),
                                               pl.program_id(2) == step[2]))
    (out,), carried = _host_pcall(body, hosted, at((0, 0, 0)), at(tuple(g - 1 for g in grid)), n_in=2, n_out=1, n_scratch=0,
                                  name=name, grid=grid, in_specs=[a_spec, b_spec],
                                  out_specs=[pl.BlockSpec((tm, tn), lambda i, j, k: (i, j))], out_shape=[_sds((m, n))],
                                  scratch_shapes=[], dims=("parallel", "parallel", "arbitrary"), operands=(a, b))
    return (out, carried) if carry else out


def _in_proj(x, g_mix, w_in_r):
    s = x.shape[0]
    tm = _row_tile(s)

    def body(x_ref, g_ref, w_ref, h_ref, z_ref):
        h = _rms(x_ref[...], g_ref[...]).astype(BF16)
        h_ref[...] = h
        z_ref[...] = _dot(h, w_ref[...])

    return _pcall(body, name="in_proj", grid=(s // tm,),
                  in_specs=[pl.BlockSpec((tm, D_MODEL), lambda i: (i, 0)), pl.BlockSpec((1, D_MODEL), lambda i: (0, 0)),
                            pl.BlockSpec((D_MODEL, Z_COLS), lambda i: (0, 0))],
                  out_specs=[pl.BlockSpec((tm, D_MODEL), lambda i: (i, 0)), pl.BlockSpec((tm, Z_COLS), lambda i: (i, 0))],
                  out_shape=[_sds((s, D_MODEL), BF16), _sds((s, Z_COLS))], dims=("parallel",))(x, g_mix, w_in_r)


def _split_heads(ref, val):
    for h in range(HEADS):
        ref[h] = val[:, h * HEAD_DIM:(h + 1) * HEAD_DIM].astype(ref.dtype)


def _merge_heads(ref):
    return jnp.concatenate([ref[h].astype(F32) for h in range(HEADS)], axis=-1)


def _forget_logits(z_ref, bf_ref):
    fl = z_ref[:, F_COL0:F_COL0 + LANES] + bf_ref[...]
    return jnp.where(lax.broadcasted_iota(jnp.int32, fl.shape, 1) < HEADS, fl, 0.0)


def _attn_prep(z, gq, gk, bf, gg):
    s = z.shape[0]
    tm = _row_tile(s)

    def body(z_ref, gq_ref, gk_ref, bf_ref, gg_ref, qn_ref, kn_ref, vb_ref, ub_ref, uf_ref, c_ref, carry_ref):
        i = pl.program_id(0)

        @pl.when(i == 0)
        def _():
            carry_ref[...] = jnp.zeros_like(carry_ref)

        gg_m = gg_ref[...]

        def head_norm(t, g):
            ssq = _dot_exact_r(t * t, gg_m)
            return t * lax.rsqrt(ssq * (1.0 / HEAD_DIM) + EPS) * g

        _split_heads(qn_ref, head_norm(z_ref[:, 0:ATTN_W], gq_ref[...]))
        _split_heads(kn_ref, head_norm(z_ref[:, ATTN_W:2 * ATTN_W], gk_ref[...]))
        _split_heads(vb_ref, z_ref[:, 2 * ATTN_W:3 * ATTN_W])
        u = z_ref[:, U_COL0:U_COL0 + SSM_W]
        uf_ref[...] = u
        ub_ref[...] = u.astype(BF16)
        fl = _forget_logits(z_ref, bf_ref)
        lf = jnp.minimum(fl, 0.0) - jnp.log1p(jnp.exp(-jnp.abs(fl)))
        row = lax.broadcasted_iota(jnp.int32, (tm, tm), 0)
        col = lax.broadcasted_iota(jnp.int32, (tm, tm), 1)
        tri = (row >= col).astype(BF16)
        c = _dot_exact_l(tri, lf) + carry_ref[...]
        c_ref[...] = c
        carry_ref[...] = c[tm - 1:tm, :]

    row_spec = lambda w: pl.BlockSpec((tm, w), lambda i: (i, 0))
    const = lambda shape: pl.BlockSpec(shape, lambda i: (0, 0))
    heads = pl.BlockSpec((HEADS, tm, HEAD_DIM), lambda i: (0, i, 0))
    return _pcall(body, name="attn_prep", grid=(s // tm,),
                  in_specs=[row_spec(Z_COLS), const((1, ATTN_W)), const((1, ATTN_W)), const((1, LANES)), const((ATTN_W, ATTN_W))],
                  out_specs=[heads] * 3 + [row_spec(SSM_W), row_spec(SSM_W), row_spec(LANES)],
                  out_shape=[_sds((HEADS, s, HEAD_DIM), BF16)] * 3 + [_sds((s, SSM_W), BF16), _sds((s, SSM_W)), _sds((s, LANES))],
                  scratch_shapes=[pltpu.VMEM((1, LANES), F32)], dims=("arbitrary",))(z, gq, gk, bf, gg)


def _attn_fwd(qh, kh, vh, crow, hosted=None):
    _, s, _ = qh.shape
    tq = _row_tile(s)
    scale = HEAD_DIM ** -0.5

    hp = HEADS_PER_STEP
    nq = s // tq
    fold = lambda t, op: op(t[:, :tq // 2], t[:, tq // 2:])

    def body(q_ref, k_ref, v_ref, c_ref, o_ref, lse_ref, s_s):
        i = pl.program_id(1)

        def first(j, ms, diagonal):
            off = pl.multiple_of(j * tq, tq)
            out = []
            for hh in range(hp):
                sc = _dot_nt(q_ref[hh], k_ref[hh, pl.ds(off, tq), :]) * scale - c_ref[hh, :, pl.ds(off, tq)]
                if diagonal:
                    causal = lax.broadcasted_iota(jnp.int32, (tq, tq), 1) <= lax.broadcasted_iota(jnp.int32, (tq, tq), 0)
                    sc = jnp.where(causal, sc, NEG_INF)
                s_s[hh, j] = sc
                out.append(jnp.maximum(ms[hh], fold(sc, jnp.maximum)))
            return tuple(out)

        ms = lax.fori_loop(0, i, lambda j, c: first(j, c, False), (jnp.full((tq, tq // 2), NEG_INF, F32),) * hp)
        ms = [jnp.max(t, axis=-1, keepdims=True) for t in first(i, ms, True)]

        def second(j, carry):
            rows = pl.ds(pl.multiple_of(j * tq, tq), tq)
            out = []
            for hh in range(hp):
                ls, acc = carry[hh]
                p = jnp.exp(s_s[hh, j] - ms[hh])
                out.append((ls + fold(p, jnp.add), acc + _dot(p.astype(BF16), v_ref[hh, rows, :])))
            return tuple(out)

        zero = (jnp.zeros((tq, tq // 2), F32), jnp.zeros((tq, HEAD_DIM), F32))
        for hh, (ls, acc) in enumerate(lax.fori_loop(0, i + 1, second, (zero,) * hp)):
            l = jnp.sum(ls, axis=-1, keepdims=True)
            o_ref[hh] = acc / l
            lse_ref[hh] = ms[hh] + jnp.log(l)

    blk = pl.BlockSpec((hp, tq, HEAD_DIM), lambda h, i: (h, i, 0))
    full = pl.BlockSpec((hp, s, HEAD_DIM), lambda h, i: (h, 0, 0))
    nh = HEADS // hp
    first = lambda: jnp.logical_and(pl.program_id(0) == 0, pl.program_id(1) == 0)
    last = lambda: jnp.logical_and(pl.program_id(0) == nh - 1, pl.program_id(1) == nq - 1)
    return _host_pcall(body, hosted, first, last, n_in=4, n_out=2, n_scratch=1, name="attn_fwd", grid=(nh, nq),
                       in_specs=[blk, full, full, pl.BlockSpec((hp, 1, s), lambda h, i: (h, 0, 0))],
                       out_specs=[blk, pl.BlockSpec((hp, tq, 1), lambda h, i: (h, i, 0))],
                       out_shape=[_sds((HEADS, s, HEAD_DIM)), _sds((HEADS, s, 1))],
                       scratch_shapes=[pltpu.VMEM((hp, nq, tq, tq), F32)],
                       dims=("parallel", "parallel"), operands=(qh, kh, vh, crow))


def _ssm_param_fn(lr, li, ls, br, bi):
    step = jnp.exp(ls)
    er = jnp.exp(lr * step)
    ab_re = er * jnp.cos(li * step)
    ab_im = er * jnp.sin(li * step)
    num_re = ab_re - 1.0
    num_im = ab_im
    den = lr * lr + li * li
    f_re = (num_re * lr + num_im * li) / den
    f_im = (num_im * lr - num_re * li) / den
    bb_re = f_re * br - f_im * bi
    bb_im = f_re * bi + f_im * br
    return ab_re, ab_im, bb_re, bb_im


_PARAM_SHAPE = (SSM_GROUPS * SSM_GROUP, SSM_STATE)


def _ssm_params(lr, li, ls, br, bi):
    def body(lr_ref, li_ref, ls_ref, br_ref, bi_ref, ar_ref, ai_ref, bbr_ref, bbi_ref):
        ar, ai, bbr, bbi = _ssm_param_fn(lr_ref[...], li_ref[...], ls_ref[...], br_ref[...], bi_ref[...])
        ar_ref[...] = ar
        ai_ref[...] = ai
        bbr_ref[...] = bbr
        bbi_ref[...] = bbi

    spec = pl.BlockSpec(_PARAM_SHAPE, lambda: (0, 0))
    return _pcall(body, name="ssm_params", in_specs=[spec] * 5, out_specs=[spec] * 4,
                  out_shape=[_sds(_PARAM_SHAPE)] * 4)(lr, li, ls, br, bi)


def _ssm_params_bwd(lr, li, ls, br, bi, dar, dai, dbbr, dbbi, expand_t):
    def body(lr_ref, li_ref, ls_ref, br_ref, bi_ref, dar_ref, dai_ref, dbbr_ref, dbbi_ref, et_ref,
             dlr_ref, dli_ref, dls_ref, dbr_ref, dbi_ref):
        _, vjp = jax.vjp(_ssm_param_fn, lr_ref[...], li_ref[...], ls_ref[...], br_ref[...], bi_ref[...])
        dlr, dli, dls, dbr, dbi = vjp((dar_ref[...], dai_ref[...], dbbr_ref[...], dbbi_ref[...]))
        et = et_ref[...]
        dlr_ref[...] = _dot_exact_l(et, dlr)
        dli_ref[...] = _dot_exact_l(et, dli)
        dls_ref[...] = jnp.sum(_dot_exact_l(et, dls), axis=-1, keepdims=True)
        dbr_ref[...] = dbr
        dbi_ref[...] = dbi

    spec = pl.BlockSpec(_PARAM_SHAPE, lambda: (0, 0))
    gspec = pl.BlockSpec((SSM_GROUPS, SSM_STATE), lambda: (0, 0))
    return _pcall(body, name="ssm_params_bwd",
                  in_specs=[spec] * 9 + [pl.BlockSpec((SSM_GROUPS, _PARAM_SHAPE[0]), lambda: (0, 0))],
                  out_specs=[gspec, gspec, pl.BlockSpec((SSM_GROUPS, 1), lambda: (0, 0)), spec, spec],
                  out_shape=[_sds((SSM_GROUPS, SSM_STATE))] * 2 + [_sds((SSM_GROUPS, 1))] + [_sds(_PARAM_SHAPE)] * 2,
                  )(lr, li, ls, br, bi, dar, dai, dbbr, dbbi, expand_t)


def _cmul(ar, ai, br, bi):
    return ar * br - ai * bi, ar * bi + ai * br


def _scan_consts(ar, ai, width, reverse):
    row = lax.broadcasted_iota(jnp.int32, (SUBLANES, width), 0)
    pw = [(ar, ai)]
    for _ in range(SUBLANES - 1):
        pw.append(_cmul(pw[-1][0], pw[-1][1], ar, ai))
    steps = []
    for d in (1, 2, 4):
        keep = (row < SUBLANES - d) if reverse else (row >= d)
        steps.append((d, jnp.where(keep, pw[d - 1][0], 0.0), jnp.where(keep, pw[d - 1][1], 0.0)))
    pr = jnp.zeros((SUBLANES, width), F32)
    pi = jnp.zeros((SUBLANES, width), F32)
    for r in range(SUBLANES):
        e = (SUBLANES - r) if reverse else (r + 1)
        pr = jnp.where(row == r, pw[e - 1][0], pr)
        pi = jnp.where(row == r, pw[e - 1][1], pi)
    return steps, pr, pi


def _scan_tile(xr, xi, cr, ci, consts, reverse):
    steps, pr, pi = consts
    for d, mr, mi in steps:
        sh = (SUBLANES - d) if reverse else d
        sr = pltpu.roll(xr, sh, 0)
        si = pltpu.roll(xi, sh, 0)
        xr, xi = xr + mr * sr - mi * si, xi + mr * si + mi * sr
    return xr + pr * cr - pi * ci, xi + pr * ci + pi * cr


def _ssm_fwd(ub, uf, bbr, bbi, ar, ai, ccr, cci, dsk, hosted=None):
    s = ub.shape[0]
    tm = _row_tile(s)
    nt = tm // SUBLANES

    def body(ub_ref, u_ref, bbr_ref, bbi_ref, ar_ref, ai_ref, ccr_ref, cci_ref, dsk_ref,
             xr_ref, xi_ref, y_ref, cr_s, ci_s):
        i = pl.program_id(1)

        @pl.when(i == 0)
        def _():
            cr_s[...] = jnp.zeros_like(cr_s)
            ci_s[...] = jnp.zeros_like(ci_s)

        u_b = ub_ref[...]
        xr_ref[...] = _dot(u_b, bbr_ref[0])
        xi_ref[...] = _dot(u_b, bbi_ref[0])
        consts = _scan_consts(ar_ref[0], ai_ref[0], CHUNK_S, False)

        def tile(k, carry):
            cr, ci = carry
            sl = pl.ds(pl.multiple_of(k * SUBLANES, SUBLANES), SUBLANES)
            xr, xi = _scan_tile(xr_ref[sl, :], xi_ref[sl, :], cr, ci, consts, False)
            xr_ref[sl, :] = xr
            xi_ref[sl, :] = xi
            return xr[SUBLANES - 1:SUBLANES, :], xi[SUBLANES - 1:SUBLANES, :]

        cr, ci = lax.fori_loop(0, nt, tile, (cr_s[...], ci_s[...]))
        cr_s[...] = cr
        ci_s[...] = ci
        y_ref[...] = (_dot(xr_ref[...].astype(BF16), ccr_ref[0]) - _dot(xi_ref[...].astype(BF16), cci_ref[0])
                      + dsk_ref[...] * u_ref[...])

    wspec = lambda a, b: pl.BlockSpec((1, a, b), lambda j, i: (j, 0, 0))
    nb = s // tm
    first = lambda: jnp.logical_and(pl.program_id(0) == 0, pl.program_id(1) == 0)
    last = lambda: jnp.logical_and(pl.program_id(0) == SSM_CHUNKS - 1, pl.program_id(1) == nb - 1)
    return _host_pcall(
        body, hosted, first, last, n_in=9, n_out=3, n_scratch=2, name="ssm_fwd", grid=(SSM_CHUNKS, nb),
        in_specs=[pl.BlockSpec((tm, CHUNK_U), lambda j, i: (i, j)),
                  pl.BlockSpec((tm, CHUNK_U), lambda j, i: (i, j)),
                  wspec(CHUNK_U, CHUNK_S), wspec(CHUNK_U, CHUNK_S), wspec(1, CHUNK_S), wspec(1, CHUNK_S),
                  wspec(CHUNK_S, CHUNK_U), wspec(CHUNK_S, CHUNK_U),
                  pl.BlockSpec((1, CHUNK_U), lambda j, i: (0, j))],
        out_specs=[pl.BlockSpec((tm, CHUNK_S), lambda j, i: (i, j)), pl.BlockSpec((tm, CHUNK_S), lambda j, i: (i, j)),
                   pl.BlockSpec((tm, CHUNK_U), lambda j, i: (i, j))],
        out_shape=[_sds((s, N_STATE)), _sds((s, N_STATE)), _sds((s, SSM_W))],
        scratch_shapes=[pltpu.VMEM((1, CHUNK_S), F32)] * 2,
        dims=("parallel", "arbitrary"), operands=(ub, uf, bbr, bbi, ar, ai, ccr, cci, dsk))


def _ssm_glu(y, w_glu, b_glu):
    ge = _gelu(y)
    sg = _sigmoid(_dot(ge.astype(BF16), w_glu) + b_glu)
    return ge, sg


def _mix_out(y, att, x, w_glu, b_glu, g_att, g_ssm, w_out, g_ffn, hosted=None):
    s = x.shape[0]
    tm = _row_tile(s)

    def body(y_ref, att_ref, x_ref, wg_ref, bg_ref, ga_ref, gs_ref, wo_ref, gf_ref, x1_ref, mix_ref, h2_ref):
        ge, sg = _ssm_glu(y_ref[...], wg_ref[...], bg_ref[...])
        ms = _rms(ge * sg, gs_ref[...]).astype(BF16)
        ma = _rms(_merge_heads(att_ref), ga_ref[...]).astype(BF16)
        mix_ref[:, 0:ATTN_W] = ma
        mix_ref[:, ATTN_W:D_MODEL] = ms
        x1 = x_ref[...] + (_dot(ma, wo_ref[0:ATTN_W, :]) + _dot(ms, wo_ref[ATTN_W:D_MODEL, :]))
        x1_ref[...] = x1
        h2_ref[...] = _rms(x1, gf_ref[...]).astype(BF16)

    row = lambda w: pl.BlockSpec((tm, w), lambda i: (i, 0))
    const = lambda a, b: pl.BlockSpec((a, b), lambda i: (0, 0))
    nb = s // tm
    return _host_pcall(body, hosted, lambda: pl.program_id(0) == 0, lambda: pl.program_id(0) == nb - 1,
                       n_in=9, n_out=3, n_scratch=0, name="mix_out", grid=(nb,),
                       in_specs=[row(SSM_W), pl.BlockSpec((HEADS, tm, HEAD_DIM), lambda i: (0, i, 0)), row(D_MODEL),
                                 const(SSM_W, SSM_W), const(1, SSM_W),
                                 const(1, ATTN_W), const(1, SSM_W), const(D_MODEL, D_MODEL), const(1, D_MODEL)],
                       out_specs=[row(D_MODEL)] * 3,
                       out_shape=[_sds((s, D_MODEL)), _sds((s, D_MODEL), BF16), _sds((s, D_MODEL), BF16)],
                       scratch_shapes=[], dims=("parallel",), operands=(y, att, x, w_glu, b_glu, g_att, g_ssm, w_out, g_ffn))


CONV_CHUNK = 64


def _conv_rows(pad_ref, w, b, r0, n):
    y = b + pad_ref[pl.ds(r0 + SUBLANES - 2, n), :] * w[0:1, :]
    y = y + pad_ref[pl.ds(r0 + SUBLANES - 1, n), :] * w[1:2, :]
    return y + pad_ref[pl.ds(r0 + SUBLANES, n), :] * w[2:3, :]


def _fill_front_pad(pad_ref, strip_ref, s):
    pad_ref[0:SUBLANES, :] = jnp.zeros((SUBLANES, STRIP), F32)
    for r0 in range(0, s, CONV_CHUNK):
        pad_ref[pl.ds(SUBLANES + r0, CONV_CHUNK), :] = strip_ref[pl.ds(r0, CONV_CHUNK), :]


def _conv_act(up, conv_w, conv_b):
    s = up.shape[0]

    def body(ug_ref, uv_ref, wg_ref, wv_ref, bg_ref, bv_ref, act_ref, pg_ref, pv_ref):
        _fill_front_pad(pg_ref, ug_ref, s)
        _fill_front_pad(pv_ref, uv_ref, s)
        wg, wv, bg, bv = wg_ref[...], wv_ref[...], bg_ref[...], bv_ref[...]
        for r0 in range(0, s, CONV_CHUNK):
            hg = _conv_rows(pg_ref, wg, bg, r0, CONV_CHUNK)
            hv = _conv_rows(pv_ref, wv, bv, r0, CONV_CHUNK)
            act_ref[pl.ds(r0, CONV_CHUNK), :] = (hg * _sigmoid(hg) * hv).astype(BF16)

    strip = lambda off: pl.BlockSpec((s, STRIP), lambda j: (0, j + off))
    wsp = lambda off: pl.BlockSpec((3, STRIP), lambda j: (0, j + off))
    bsp = lambda off: pl.BlockSpec((1, STRIP), lambda j: (0, j + off))
    return _pcall(body, name="conv_act", grid=(N_STRIPS,),
                  in_specs=[strip(0), strip(N_STRIPS), wsp(0), wsp(N_STRIPS), bsp(0), bsp(N_STRIPS)],
                  out_specs=pl.BlockSpec((s, STRIP), lambda j: (0, j)), out_shape=_sds((s, D_FF), BF16),
                  scratch_shapes=[pltpu.VMEM((s + SUBLANES, STRIP), F32)] * 2,
                  dims=("parallel",))(up, up, conv_w, conv_w, conv_b, conv_b)


def _down_loss(act, w_down, x1, tgt):
    s = x1.shape[0]
    tm = _row_tile(s)

    def body(a_ref, w_ref, x1_ref, t_ref, dy_ref, dyb_ref, loss_ref):
        i = pl.program_id(0)

        @pl.when(i == 0)
        def _():
            loss_ref[...] = jnp.zeros_like(loss_ref)

        diff = x1_ref[...] + _dot(a_ref[...], w_ref[...]) - t_ref[...]
        dy = diff * (1.0 / D_MODEL)
        dy_ref[...] = dy
        dyb_ref[...] = dy.astype(BF16)
        loss_ref[...] += 0.5 * jnp.sum(diff * dy)

    row = lambda w: pl.BlockSpec((tm, w), lambda i: (i, 0))
    return _pcall(body, name="down_loss", grid=(s // tm,),
                  in_specs=[row(D_FF), pl.BlockSpec((D_FF, D_MODEL), lambda i: (0, 0)), row(D_MODEL), row(D_MODEL)],
                  out_specs=[row(D_MODEL), row(D_MODEL), pl.BlockSpec((SUBLANES, LANES), lambda i: (0, 0))],
                  out_shape=[_sds((s, D_MODEL)), _sds((s, D_MODEL), BF16), _sds((SUBLANES, LANES))],
                  dims=("arbitrary",))(act, w_down, x1, tgt)


def _conv_act_bwd(up, dact, conv_w, conv_b):
    s = up.shape[0]
    ch = CONV_CHUNK

    def body(ug_ref, uv_ref, da_ref, wg_ref, wv_ref, bg_ref, bv_ref, dup_ref, dcw_ref, pg_ref, pv_ref, dg_ref, dv_ref):
        _fill_front_pad(pg_ref, ug_ref, s)
        _fill_front_pad(pv_ref, uv_ref, s)
        zero = jnp.zeros((SUBLANES, STRIP), F32)
        dg_ref[pl.ds(s, SUBLANES), :] = zero
        dv_ref[pl.ds(s, SUBLANES), :] = zero
        wg, wv, bg, bv = wg_ref[...], wv_ref[...], bg_ref[...], bv_ref[...]
        tile_sum = lambda t: jnp.sum(t.reshape(ch // SUBLANES, SUBLANES, STRIP), axis=0)
        accs = [[zero] * 4, [zero] * 4]
        for r0 in range(0, s, ch):
            hg = _conv_rows(pg_ref, wg, bg, r0, ch)
            hv = _conv_rows(pv_ref, wv, bv, r0, ch)
            sg = _sigmoid(hg)
            da = da_ref[pl.ds(r0, ch), :]
            dhs = (da * hv * (sg * (1.0 + hg * (1.0 - sg))), da * (hg * sg))
            for half, (dh, d_ref, p_ref) in enumerate(zip(dhs, (dg_ref, dv_ref), (pg_ref, pv_ref))):
                d_ref[pl.ds(r0, ch), :] = dh
                for k in range(3):
                    accs[half][k] = accs[half][k] + tile_sum(dh * p_ref[pl.ds(r0 + SUBLANES - 2 + k, ch), :])
                accs[half][3] = accs[half][3] + tile_sum(dh)
        for half, (d_ref, w) in enumerate(((dg_ref, wg), (dv_ref, wv))):
            for r0 in range(0, s, ch):
                dup = (d_ref[pl.ds(r0, ch), :] * w[2:3, :] + d_ref[pl.ds(r0 + 1, ch), :] * w[1:2, :]
                       + d_ref[pl.ds(r0 + 2, ch), :] * w[0:1, :])
                dup_ref[half, pl.ds(r0, ch), :] = dup.astype(BF16)
            rid = lax.broadcasted_iota(jnp.int32, (SUBLANES, STRIP), 0)
            out = zero
            for k in range(4):
                out = jnp.where(rid == k, jnp.sum(accs[half][k], axis=0, keepdims=True), out)
            dcw_ref[half] = out

    strip = lambda off: pl.BlockSpec((s, STRIP), lambda j: (0, j + off))
    wsp = lambda off: pl.BlockSpec((3, STRIP), lambda j: (0, j + off))
    bsp = lambda off: pl.BlockSpec((1, STRIP), lambda j: (0, j + off))
    return _pcall(body, name="conv_act_bwd", grid=(N_STRIPS,),
                  in_specs=[strip(0), strip(N_STRIPS), strip(0), wsp(0), wsp(N_STRIPS), bsp(0), bsp(N_STRIPS)],
                  out_specs=[pl.BlockSpec((2, s, STRIP), lambda j: (0, 0, j)), pl.BlockSpec((2, SUBLANES, STRIP), lambda j: (0, 0, j))],
                  out_shape=[_sds((2, s, D_FF), BF16), _sds((2, SUBLANES, D_FF))],
                  scratch_shapes=[pltpu.VMEM((s + SUBLANES, STRIP), F32)] * 4,
                  dims=("parallel",))(up, up, dact, conv_w, conv_w, conv_b, conv_b)


def _mix_bwd(dy, dh2, x1, g_ffn, w_out, y, att, w_glu, b_glu, g_att, g_ssm):
    s = dy.shape[0]
    tm = _row_tile(s)

    def body(dy_ref, dh2_ref, x1_ref, gf_ref, wo_ref, y_ref, att_ref, wg_ref, bg_ref, ga_ref, gs_ref,
             dx1_ref, dx1b_ref, datt_ref, dys_ref, dwg_ref, dgf_ref, dga_ref, dgs_ref, dbg_ref):
        i = pl.program_id(0)

        @pl.when(i == 0)
        def _():
            for r in (dwg_ref, dgf_ref, dga_ref, dgs_ref, dbg_ref):
                r[...] = jnp.zeros_like(r)

        dxn, dgf = _rms_bwd(x1_ref[...], gf_ref[...], dh2_ref[...])
        dx1 = dy_ref[...] + dxn
        dx1_ref[...] = dx1
        dx1b = dx1.astype(BF16)
        dx1b_ref[...] = dx1b
        dgf_ref[...] += dgf
        dma = _dot_nt(dx1b, wo_ref[0:ATTN_W, :])
        dms = _dot_nt(dx1b, wo_ref[ATTN_W:D_MODEL, :])
        datt, dga = _rms_bwd(_merge_heads(att_ref), ga_ref[...], dma)
        _split_heads(datt_ref, datt)
        dga_ref[...] += dga
        yv = y_ref[...]
        ge, sg = _ssm_glu(yv, wg_ref[...], bg_ref[...])
        dssm, dgs = _rms_bwd(ge * sg, gs_ref[...], dms)
        dgs_ref[...] += dgs
        dgl = dssm * ge * sg * (1.0 - sg)
        dglb = dgl.astype(BF16)
        dge = dssm * sg + _dot_nt(dglb, wg_ref[...])
        dbg_ref[...] += jnp.sum(dgl, axis=0, keepdims=True)
        dwg_ref[...] += _dot_tn(ge.astype(BF16), dglb)
        dys_ref[...] = dge * _gelu_grad(yv)

    row = lambda w: pl.BlockSpec((tm, w), lambda i: (i, 0))
    const = lambda a, b: pl.BlockSpec((a, b), lambda i: (0, 0))
    heads = pl.BlockSpec((HEADS, tm, HEAD_DIM), lambda i: (0, i, 0))
    return _pcall(body, name="mix_bwd", grid=(s // tm,),
                  in_specs=[row(D_MODEL), row(D_MODEL), row(D_MODEL), const(1, D_MODEL), const(D_MODEL, D_MODEL), row(SSM_W),
                            heads, const(SSM_W, SSM_W), const(1, SSM_W), const(1, ATTN_W), const(1, SSM_W)],
                  out_specs=[row(D_MODEL), row(D_MODEL), heads, row(SSM_W), const(SSM_W, SSM_W), const(1, D_MODEL),
                             const(1, ATTN_W), const(1, SSM_W), const(1, SSM_W)],
                  out_shape=[_sds((s, D_MODEL)), _sds((s, D_MODEL), BF16), _sds((HEADS, s, HEAD_DIM)), _sds((s, SSM_W)),
                             _sds((SSM_W, SSM_W)), _sds((1, D_MODEL)), _sds((1, ATTN_W)), _sds((1, SSM_W)), _sds((1, SSM_W))],
                  dims=("arbitrary",))(dy, dh2, x1, g_ffn, w_out, y, att, w_glu, b_glu, g_att, g_ssm)


def _ssm_bwd(dys, uf, ub, xr, xi, bbr, bbi, ar, ai, ccr, cci, dsk, hosted=None):
    s = dys.shape[0]
    tm = _row_tile(s)
    nb = s // tm
    nt = tm // SUBLANES

    def body(dy_ref, u_ref, ub_ref, xr_ref, xi_ref, xrp_ref, xip_ref, bbr_ref, bbi_ref, ar_ref, ai_ref, ccr_ref,
             cci_ref, dsk_ref, du_ref, dbbr_ref, dbbi_ref, dccr_ref, dcci_ref, dar_ref, dai_ref, dd_ref,
             gr_s, gi_s, cr_s, ci_s, accr_s, acci_s):
        i = pl.program_id(1)
        first_block = i == nb - 1

        @pl.when(i == 0)
        def _():
            for r in (cr_s, ci_s, accr_s, acci_s, dbbr_ref, dbbi_ref, dccr_ref, dcci_ref, dd_ref):
                r[...] = jnp.zeros_like(r)

        dy = dy_ref[...]
        dyb = dy.astype(BF16)
        gr_s[...] = _dot_nt(dyb, ccr_ref[0])
        gi_s[...] = -_dot_nt(dyb, cci_ref[0])
        consts = _scan_consts(ar_ref[0], -ai_ref[0], CHUNK_S, True)
        row = lax.broadcasted_iota(jnp.int32, (SUBLANES, CHUNK_S), 0)

        def tile(kk, carry):
            cr, ci, accr, acci = carry
            k = nt - 1 - kk
            sl = pl.ds(pl.multiple_of(k * SUBLANES, SUBLANES), SUBLANES)
            gr, gi = _scan_tile(gr_s[sl, :], gi_s[sl, :], cr, ci, consts, True)
            gr_s[sl, :] = gr
            gi_s[sl, :] = gi
            slp = pl.ds(pl.multiple_of(jnp.maximum(k - 1, 0) * SUBLANES, SUBLANES), SUBLANES)
            inner = k > 0
            pr_t = jnp.where(inner, xr_ref[slp, :], xrp_ref[...])
            pi_t = jnp.where(inner, xi_ref[slp, :], xip_ref[...])
            live = jnp.logical_or(inner, jnp.logical_not(first_block))
            top_r = jnp.where(live, pltpu.roll(pr_t, 1, 0), 0.0)
            top_i = jnp.where(live, pltpu.roll(pi_t, 1, 0), 0.0)
            xpr = jnp.where(row == 0, top_r, pltpu.roll(xr_ref[sl, :], 1, 0))
            xpi = jnp.where(row == 0, top_i, pltpu.roll(xi_ref[sl, :], 1, 0))
            accr = accr + gr * xpr + gi * xpi
            acci = acci + gi * xpr - gr * xpi
            return gr[0:1, :], gi[0:1, :], accr, acci

        zeros = jnp.zeros((SUBLANES, CHUNK_S), F32)
        cr, ci, accr, acci = lax.fori_loop(0, nt, tile, (cr_s[...], ci_s[...], zeros, zeros))
        cr_s[...] = cr
        ci_s[...] = ci
        accr_s[...] += accr
        acci_s[...] += acci
        grb = gr_s[...].astype(BF16)
        gib = gi_s[...].astype(BF16)
        u_b = ub_ref[...]
        du_ref[...] = _dot_nt(grb, bbr_ref[0]) + _dot_nt(gib, bbi_ref[0]) + dsk_ref[...] * dy
        dbbr_ref[0] += _dot_tn(u_b, grb)
        dbbi_ref[0] += _dot_tn(u_b, gib)
        dccr_ref[0] += _dot_tn(xr_ref[...].astype(BF16), dyb)
        dcci_ref[0] -= _dot_tn(xi_ref[...].astype(BF16), dyb)
        dd_ref[...] += jnp.sum(dy * u_ref[...], axis=0, keepdims=True)

        @pl.when(i == nb - 1)
        def _():
            dar_ref[0] = jnp.sum(accr_s[...], axis=0, keepdims=True)
            dai_ref[0] = jnp.sum(acci_s[...], axis=0, keepdims=True)

    tiles_per_block = tm // SUBLANES
    rb = lambda i: nb - 1 - i
    wspec = lambda a, b: pl.BlockSpec((1, a, b), lambda j, i: (j, 0, 0))
    xblk = pl.BlockSpec((tm, CHUNK_S), lambda j, i: (rb(i), j))
    xprev = pl.BlockSpec((SUBLANES, CHUNK_S), lambda j, i: (jnp.maximum(rb(i) * tiles_per_block - 1, 0), j))
    ublk = pl.BlockSpec((tm, CHUNK_U), lambda j, i: (rb(i), j))
    first = lambda: jnp.logical_and(pl.program_id(0) == 0, pl.program_id(1) == 0)
    last = lambda: jnp.logical_and(pl.program_id(0) == SSM_CHUNKS - 1, pl.program_id(1) == nb - 1)
    return _host_pcall(
        body, hosted, first, last, n_in=14, n_out=8, n_scratch=6, name="ssm_bwd", grid=(SSM_CHUNKS, nb),
        in_specs=[ublk, ublk, ublk, xblk, xblk, xprev, xprev,
                  wspec(CHUNK_U, CHUNK_S), wspec(CHUNK_U, CHUNK_S), wspec(1, CHUNK_S), wspec(1, CHUNK_S),
                  wspec(CHUNK_S, CHUNK_U), wspec(CHUNK_S, CHUNK_U), pl.BlockSpec((1, CHUNK_U), lambda j, i: (0, j))],
        out_specs=[ublk, wspec(CHUNK_U, CHUNK_S), wspec(CHUNK_U, CHUNK_S), wspec(CHUNK_S, CHUNK_U),
                   wspec(CHUNK_S, CHUNK_U), wspec(1, CHUNK_S), wspec(1, CHUNK_S),
                   pl.BlockSpec((1, CHUNK_U), lambda j, i: (0, j))],
        out_shape=[_sds((s, SSM_W)), _sds((SSM_CHUNKS, CHUNK_U, CHUNK_S)), _sds((SSM_CHUNKS, CHUNK_U, CHUNK_S)),
                   _sds((SSM_CHUNKS, CHUNK_S, CHUNK_U)), _sds((SSM_CHUNKS, CHUNK_S, CHUNK_U)),
                   _sds((SSM_CHUNKS, 1, CHUNK_S)), _sds((SSM_CHUNKS, 1, CHUNK_S)), _sds((1, SSM_W))],
        scratch_shapes=[pltpu.VMEM((tm, CHUNK_S), F32)] * 2 + [pltpu.VMEM((1, CHUNK_S), F32)] * 2
                       + [pltpu.VMEM((SUBLANES, CHUNK_S), F32)] * 2,
        dims=("parallel", "arbitrary"), operands=(dys, uf, ub, xr, xi, xr, xi, bbr, bbi, ar, ai, ccr, cci, dsk))


def _attn_probs(q, ks, cs, lse, scale, diagonal):
    p = jnp.exp(_dot_nt(q, ks) * scale - cs - lse)
    if diagonal:
        tq, tk = p.shape
        causal = lax.broadcasted_iota(jnp.int32, (tq, tk), 1) <= lax.broadcasted_iota(jnp.int32, (tq, tk), 0)
        p = jnp.where(causal, p, 0.0)
    return p


def _attn_bwd(qh, kh, vh, crow, lse, doh, hosted=None):
    _, s, _ = qh.shape
    tq = _row_tile(s)
    nq = s // tq
    scale = HEAD_DIM ** -0.5
    hp = HEADS_PER_STEP

    def body(q_ref, k_ref, v_ref, c_ref, lse_ref, do_ref, dq_ref, dk_ref, dv_ref, dc_ref, p_s, dp_s):
        i = pl.program_id(1)

        @pl.when(i == 0)
        def _():
            for r in (dk_ref, dv_ref, dc_ref):
                r[...] = jnp.zeros_like(r)

        dobs = [do_ref[hh].astype(BF16) for hh in range(hp)]

        def first(j, dls, diagonal):
            off = pl.multiple_of(j * tq, tq)
            out = []
            for hh in range(hp):
                p = _attn_probs(q_ref[hh], k_ref[hh, pl.ds(off, tq), :], c_ref[hh, :, pl.ds(off, tq)], lse_ref[hh],
                                scale, diagonal)
                dp = _dot_nt(dobs[hh], v_ref[hh, pl.ds(off, tq), :])
                p_s[hh, j] = p
                dp_s[hh, j] = dp
                out.append(dls[hh] + jnp.sum(p * dp, axis=-1, keepdims=True))
            return tuple(out)

        zero_col = jnp.zeros((tq, 1), F32)
        dls = lax.fori_loop(0, i, lambda j, c: first(j, c, False), (zero_col,) * hp)
        dls = first(i, dls, True)

        def second(j, dqs):
            rows = pl.ds(pl.multiple_of(j * tq, tq), tq)
            out = []
            for hh in range(hp):
                p = p_s[hh, j]
                ds = p * (dp_s[hh, j] - dls[hh])
                dsb = ds.astype(BF16)
                dv_ref[hh, rows, :] += _dot_tn(p.astype(BF16), dobs[hh])
                dk_ref[hh, rows, :] += _dot_tn(dsb, q_ref[hh]) * scale
                dc_ref[hh, :, rows] -= jnp.sum(ds, axis=0, keepdims=True)
                out.append(dqs[hh] + _dot(dsb, k_ref[hh, rows, :]))
            return tuple(out)

        dqs = lax.fori_loop(0, i + 1, second, (jnp.zeros((tq, HEAD_DIM), F32),) * hp)
        for hh in range(hp):
            dq_ref[hh] = dqs[hh] * scale

    blk = pl.BlockSpec((hp, tq, HEAD_DIM), lambda h, i: (h, i, 0))
    full = pl.BlockSpec((hp, s, HEAD_DIM), lambda h, i: (h, 0, 0))
    crow_spec = pl.BlockSpec((hp, 1, s), lambda h, i: (h, 0, 0))
    nh = HEADS // hp
    first = lambda: jnp.logical_and(pl.program_id(0) == 0, pl.program_id(1) == 0)
    last = lambda: jnp.logical_and(pl.program_id(0) == nh - 1, pl.program_id(1) == nq - 1)
    return _host_pcall(body, hosted, first, last, n_in=6, n_out=4, n_scratch=2, name="attn_bwd", grid=(nh, nq),
                       in_specs=[blk, full, full, crow_spec, pl.BlockSpec((hp, tq, 1), lambda h, i: (h, i, 0)), blk],
                       out_specs=[blk, full, full, crow_spec],
                       out_shape=[_sds((HEADS, s, HEAD_DIM))] * 3 + [_sds((HEADS, 1, s))],
                       scratch_shapes=[pltpu.VMEM((hp, nq, tq, tq), F32)] * 2,
                       dims=("parallel", "arbitrary"), operands=(qh, kh, vh, crow, lse, doh))


def _prep_bwd(z, dqn, dkn, dv, du, dc, gq, gk, bf, gg):
    s = z.shape[0]
    tm = _row_tile(s)
    nb = s // tm

    def body(z_ref, dqn_ref, dkn_ref, dv_ref, du_ref, dc_ref, gq_ref, gk_ref, bf_ref, gg_ref,
             dz_ref, dgq_ref, dgk_ref, dbf_ref, carry_ref):
        i = pl.program_id(0)

        @pl.when(i == 0)
        def _():
            for r in (dgq_ref, dgk_ref, dbf_ref, carry_ref):
                r[...] = jnp.zeros_like(r)

        gg_m = gg_ref[...]

        def head_norm_bwd(t, g, dn):
            r = lax.rsqrt(_dot_exact_r(t * t, gg_m) * (1.0 / HEAD_DIM) + EPS)
            w = dn * g
            mean_wt = _dot_exact_r(w * t, gg_m) * (1.0 / HEAD_DIM)
            return r * w - t * (r * r * r) * mean_wt, jnp.sum(dn * t * r, axis=0, keepdims=True)

        dq, dgq = head_norm_bwd(z_ref[:, 0:ATTN_W], gq_ref[...], _merge_heads(dqn_ref))
        dk, dgk = head_norm_bwd(z_ref[:, ATTN_W:2 * ATTN_W], gk_ref[...], _merge_heads(dkn_ref))
        dgq_ref[...] += dgq
        dgk_ref[...] += dgk
        row = lax.broadcasted_iota(jnp.int32, (tm, tm), 0)
        col = lax.broadcasted_iota(jnp.int32, (tm, tm), 1)
        triu = (col >= row).astype(BF16)
        dlf = _dot_exact_l(triu, dc_ref[...]) + carry_ref[...]
        carry_ref[...] = dlf[0:1, :]
        df = dlf * _sigmoid(-_forget_logits(z_ref, bf_ref))
        dbf_ref[...] += jnp.sum(df, axis=0, keepdims=True)
        dz_ref[:, 0:ATTN_W] = dq.astype(BF16)
        dz_ref[:, ATTN_W:2 * ATTN_W] = dk.astype(BF16)
        dz_ref[:, 2 * ATTN_W:3 * ATTN_W] = _merge_heads(dv_ref).astype(BF16)
        tail = jnp.concatenate([df[:, :HEADS], du_ref[...], jnp.zeros((tm, Z_COLS - IN_COLS), F32)], axis=-1)
        dz_ref[:, F_COL0:Z_COLS] = tail.astype(BF16)

    row_spec = lambda w: pl.BlockSpec((tm, w), lambda i: (nb - 1 - i, 0))
    const = lambda shape: pl.BlockSpec(shape, lambda i: (0, 0))
    return _pcall(body, name="prep_bwd", grid=(nb,),
                  in_specs=[row_spec(Z_COLS)] + [pl.BlockSpec((HEADS, tm, HEAD_DIM), lambda i: (0, nb - 1 - i, 0))] * 3
                           + [row_spec(ATTN_W), row_spec(LANES), const((1, ATTN_W)),
                              const((1, ATTN_W)), const((1, LANES)), const((ATTN_W, ATTN_W))],
                  out_specs=[row_spec(Z_COLS), const((1, ATTN_W)), const((1, ATTN_W)), const((1, LANES))],
                  out_shape=[_sds((s, Z_COLS), BF16), _sds((1, ATTN_W)), _sds((1, ATTN_W)), _sds((1, LANES))],
                  scratch_shapes=[pltpu.VMEM((1, LANES), F32)], dims=("arbitrary",))(z, dqn, dkn, dv, du, dc, gq, gk, bf, gg)


def _in_norm_bwd(x, g_mix, dh, dx1, hosted=None):
    s = x.shape[0]
    tm = _row_tile(s)

    def body(x_ref, g_ref, dh_ref, dx1_ref, dx_ref, dg_ref):
        i = pl.program_id(0)

        @pl.when(i == 0)
        def _():
            dg_ref[...] = jnp.zeros_like(dg_ref)

        dxn, dg = _rms_bwd(x_ref[...], g_ref[...], dh_ref[...])
        dx_ref[...] = dx1_ref[...] + dxn
        dg_ref[...] += dg

    row = pl.BlockSpec((tm, D_MODEL), lambda i: (i, 0))
    vec = pl.BlockSpec((1, D_MODEL), lambda i: (0, 0))
    nb = s // tm
    return _host_pcall(body, hosted, lambda: pl.program_id(0) == 0, lambda: pl.program_id(0) == nb - 1,
                       n_in=4, n_out=2, n_scratch=0, name="in_norm_bwd", grid=(nb,), in_specs=[row, vec, row, row],
                       out_specs=[row, vec], out_shape=[_sds((s, D_MODEL)), _sds((1, D_MODEL))], scratch_shapes=[],
                       dims=("arbitrary",), operands=(x, g_mix, dh, dx1))


def _adamw_refs(w_ref, g_ref, m_ref, v_ref, d_ref, mo_ref, vo_ref):
    gv = g_ref[...]
    mn = ADAM_B1 * m_ref[...] + (1.0 - ADAM_B1) * gv
    vn = ADAM_B2 * v_ref[...] + (1.0 - ADAM_B2) * (gv * gv)
    m_hat = mn / (1.0 - ADAM_B1 ** ADAM_STEP)
    v_hat = vn / (1.0 - ADAM_B2 ** ADAM_STEP)
    d_ref[...] = -ADAM_LR * (m_hat / (jnp.sqrt(v_hat) + ADAM_EPS) + ADAM_WD * w_ref[...])
    mo_ref[...] = mn
    vo_ref[...] = vn


def _adamw_small(ws, gs, ms, vs):
    n = len(ws)

    def body(*refs):
        ins, outs = refs[:4 * n], refs[4 * n:]
        for i in range(n):
            _adamw_refs(ins[i], ins[n + i], ins[2 * n + i], ins[3 * n + i], *outs[3 * i:3 * i + 3])

    vm = pl.BlockSpec(memory_space=pltpu.VMEM)
    out_shape = [_sds(w.shape) for w in ws for _ in range(3)]
    return _pallas(body, name="adamw_small", in_specs=[vm] * (4 * n), out_specs=[vm] * (3 * n), out_shape=out_shape,
                   compiler_params=pltpu.CompilerParams(vmem_limit_bytes=VMEM_LIMIT))(*ws, *gs, *ms, *vs)


def _adamw(w, g, m, v, *, name):
    r, c = w.shape
    tr = r
    for cand in (256, 176, 128, 64):
        if r > cand and r % cand == 0:
            tr = cand
            break

    def body(w_ref, g_ref, m_ref, v_ref, d_ref, mo_ref, vo_ref):
        _adamw_refs(w_ref, g_ref, m_ref, v_ref, d_ref, mo_ref, vo_ref)

    spec = pl.BlockSpec((tr, c), lambda i: (i, 0))
    return _pcall(body, name=name, grid=(r // tr,), in_specs=[spec] * 4, out_specs=[spec] * 3,
                  out_shape=[_sds((r, c))] * 3, dims=("parallel",))(w, g, m, v)


def _prefetch_call(body, *, name, grid, in_specs, out_specs, out_shape, operands):
    grid_spec = pltpu.PrefetchScalarGridSpec(num_scalar_prefetch=1, grid=grid, in_specs=in_specs, out_specs=out_specs)
    params = pltpu.CompilerParams(dimension_semantics=("parallel",) * len(grid), vmem_limit_bytes=VMEM_LIMIT)
    return _pallas(body, name=name, grid_spec=grid_spec, out_shape=out_shape, compiler_params=params)(*operands)


def _place_cols(buf, shard, place):
    rows, cols = shard.shape
    tr = 256

    def body(place_ref, s_ref, b_ref, o_ref):
        o_ref[...] = s_ref[...]

    grid_spec = pltpu.PrefetchScalarGridSpec(
        num_scalar_prefetch=1, grid=(rows // tr,),
        in_specs=[pl.BlockSpec((tr, cols), lambda i, p: (i, 0)), pl.BlockSpec(memory_space=pltpu.HBM)],
        out_specs=pl.BlockSpec((tr, cols), lambda i, p: (i, p[0])))
    return _pallas(body, name="place_own_cols", grid_spec=grid_spec, out_shape=_sds(buf.shape, buf.dtype),
                   input_output_aliases={2: 0},
                   compiler_params=pltpu.CompilerParams(dimension_semantics=("parallel",),
                                                        vmem_limit_bytes=VMEM_LIMIT))(place, shard, buf)


def _half_rows_tile(hr):
    return hr if hr <= 256 else 176 if hr % 176 == 0 else 256


def _add_half(g, landed, place, *, name):
    def body(place_ref, g_ref, l_ref, o_ref):
        own = g_ref[0] if len(g_ref.shape) == 4 else g_ref[...]
        o_ref[...] = (own + l_ref[...]).astype(BF16)

    if g.ndim == 4:
        _, _, hr, c = g.shape
        tr = _half_rows_tile(hr)
        blk = (1, tr, c)
        return _prefetch_call(
            body, name=name, grid=(N_CHIPS, hr // tr),
            in_specs=[pl.BlockSpec((1,) + blk, lambda j, i, p: (j, p[1], i, 0)), pl.BlockSpec(blk, lambda j, i, p: (j, i, 0))],
            out_specs=pl.BlockSpec(blk, lambda j, i, p: (j, i, 0)), out_shape=_sds(landed.shape, BF16),
            operands=(place, g, landed))
    hr, c = landed.shape
    tr, tc = 256, _tile(c, 2176)
    nb = hr // tr
    return _prefetch_call(
        body, name=name, grid=(nb, c // tc),
        in_specs=[pl.BlockSpec((tr, tc), lambda i, j, p: (p[1] * nb + i, j)), pl.BlockSpec((tr, tc), lambda i, j, p: (i, j))],
        out_specs=pl.BlockSpec((tr, tc), lambda i, j, p: (i, j)), out_shape=_sds(landed.shape, BF16),
        operands=(place, g, landed))


def _sum_chips(chip_sum, lands, place, *, name, tc, window_stride=0):
    _, hr, c = lands.shape
    tr = _half_rows_tile(hr)
    nb = hr // tr
    ncb = c // tc

    def body(place_ref, own_ref, a_ref, b_ref, c_ref, o_ref):
        own = own_ref[0] if len(own_ref.shape) == 3 else own_ref[...]
        o_ref[...] = ((own.astype(F32) + a_ref[0].astype(F32)) + b_ref[0].astype(F32)) + c_ref[0].astype(F32)

    land = lambda k: pl.BlockSpec((1, tr, tc), lambda i, j, p: ((p[0] + k) % N_CHIPS, i, j))
    if chip_sum.ndim == 3:
        own_spec = land(0)
    else:
        stride = window_stride // tc
        own_spec = pl.BlockSpec((tr, tc), lambda i, j, p: (i, p[0] * stride + j))
    return _prefetch_call(
        body, name=name, grid=(nb, ncb), in_specs=[own_spec, land(1), land(2), land(3)],
        out_specs=pl.BlockSpec((tr, tc), lambda i, j, p: (p[1] * nb + i, j)), out_shape=_sds((2 * hr, c)),
        operands=(place, chip_sum, lands, lands, lands))


_HBM = pl.BlockSpec(memory_space=pltpu.HBM)


def _place():
    x, y, c = lax.axis_index("x"), lax.axis_index("y"), lax.axis_index("c")
    chips = [(1 - x, y), (x, 1 - y), (1 - x, 1 - y)]
    return x, y, c, chips


def _rcopy(src, dst, send_sem, recv_sem, to):
    return pltpu.make_async_remote_copy(src_ref=src, dst_ref=dst, send_sem=send_sem, recv_sem=recv_sem,
                                        device_id=to, device_id_type=MESH)


N_BIG = 5
UP_COLS = 2 * D_FF // N_CHIPS
IN_WINDOW = 640
IN_STRIDE = 512


class _Hosted:
    def __init__(self, operands, out_shapes, n_sems, start, finish, aliases=None, local_sems=0):
        self.operands, self.out_shapes, self.n_sems = list(operands), list(out_shapes), n_sems
        self.start, self.finish, self.aliases, self.local_sems = start, finish, dict(aliases or {}), local_sems

    def scratch(self):
        return ([pltpu.SemaphoreType.DMA((self.n_sems,)), pltpu.SemaphoreType.DMA((self.n_sems,))]
                + [pltpu.SemaphoreType.DMA] * self.local_sems)


def _both(a, b):
    na, nao, nas = len(a.operands), len(a.out_shapes), len(a.scratch())

    def start(ins, outs, sems):
        a.start(ins[:na], outs[:nao], sems[:nas])
        b.start(ins[na:], outs[nao:], sems[nas:])

    def finish(ins, outs, sems):
        a.finish(ins[:na], outs[:nao], sems[:nas])
        b.finish(ins[na:], outs[nao:], sems[nas:])

    both = _Hosted(a.operands + b.operands, a.out_shapes + b.out_shapes, 0, start, finish,
                   aliases={**a.aliases, **{na + i: nao + o for i, o in b.aliases.items()}})
    both.scratch = lambda: a.scratch() + b.scratch()
    return both


def _run_hosted(hosted, *, name):
    n_in, n_out = len(hosted.operands), len(hosted.out_shapes)

    def body(*refs):
        parts = (refs[:n_in], refs[n_in:n_in + n_out], refs[n_in + n_out:])
        hosted.start(*parts)
        hosted.finish(*parts)

    return _pallas(body, name=name, in_specs=[_HBM] * n_in, out_specs=[_HBM] * n_out, out_shape=hosted.out_shapes,
                   input_output_aliases=hosted.aliases, scratch_shapes=hosted.scratch())(*hosted.operands)


def _host_pcall(core_body, hosted, first, last, *, n_in, n_out, n_scratch, name, grid, in_specs, out_specs, out_shape,
                scratch_shapes, dims, operands):
    if hosted is None:
        outs = _pcall(core_body, name=name, grid=grid, in_specs=in_specs, out_specs=out_specs, out_shape=out_shape,
                      scratch_shapes=scratch_shapes, dims=dims)(*operands)
        return outs, []
    hi, ho = len(hosted.operands), len(hosted.out_shapes)

    def body(*refs):
        a, b = n_in, n_in + hi
        c, d = b + n_out, b + n_out + ho
        e = d + n_scratch
        parts = (refs[a:b], refs[c:d], refs[e:])

        @pl.when(first())
        def _():
            hosted.start(*parts)

        core_body(*refs[:a], *refs[b:c], *refs[d:e])

        @pl.when(last())
        def _():
            hosted.finish(*parts)

    params = pltpu.CompilerParams(dimension_semantics=("arbitrary",) * len(grid), vmem_limit_bytes=VMEM_LIMIT)
    outs = _pallas(body, name=name, grid=grid, in_specs=list(in_specs) + [_HBM] * hi, out_specs=list(out_specs) + [_HBM] * ho,
                   out_shape=list(out_shape) + hosted.out_shapes, scratch_shapes=list(scratch_shapes) + hosted.scratch(),
                   input_output_aliases={n_in + a: n_out + b for a, b in hosted.aliases.items()},
                   compiler_params=params)(*operands, *hosted.operands)
    return outs[:n_out], outs[n_out:]


WHOLE_HALF = (0, 1, 1)


def _band_rows(src, hc, band):
    first, count, of = band
    hr = src.shape[0] // 2
    return pl.ds(hc * hr + first * (hr // of), count * (hr // of))


def _gather_slot(src, out, chip, hc, band=WHOLE_HALF):
    cols = src.shape[1]
    if len(out.shape) == 2:
        return out.at[_band_rows(src, hc, band), pl.ds(pl.multiple_of(chip * cols, LANES), cols)]
    return out.at[chip, _band_rows(src, hc, band), :]


def _gathered_shape(shard, by_cols):
    if by_cols:
        return _sds((shard.shape[0], N_CHIPS * shard.shape[1]), shard.dtype)
    return _sds((N_CHIPS,) + shard.shape, shard.dtype)


def _plan_gather_ici(shards, by_cols, whole=(), bands=None, into=None):
    n = len(shards)
    bands = bands or [WHOLE_HALF] * n
    into = into or [None] * n
    given = [w for w in range(n) if into[w] is not None]
    n_ops = n + len(whole)

    def copies(ins, outs, sems):
        send_sems, recv_sems = sems[0], sems[1]
        x, y, c, chips = _place()
        me = 2 * x + y
        sends, waits = [], []
        for w in range(n + len(whole)):
            for k, (cx, cy) in enumerate(chips):
                sem = (send_sems.at[3 * w + k], recv_sems.at[3 * w + k])
                if w < n:
                    sends.append(_rcopy(ins[w].at[_band_rows(ins[w], c, bands[w]), :],
                                        _gather_slot(ins[w], outs[w], me, c, bands[w]), *sem, (cx, cy, c)))
                    landed = _gather_slot(ins[w], outs[w], 2 * cx + cy, c, bands[w])
                else:
                    sends.append(_rcopy(ins[w], outs[w].at[me], *sem, (cx, cy, c)))
                    landed = outs[w].at[2 * cx + cy]
                waits.append(_rcopy(landed, landed, *sem, (cx, cy, c)))
        return sends, waits

    def start(ins, outs, sems):
        for cp in copies(ins, outs, sems)[0]:
            cp.start()

    def finish(ins, outs, sems):
        sends, waits = copies(ins, outs, sems)
        for cp in waits:
            cp.wait_recv()
        for cp in sends:
            cp.wait_send()

    out_shapes = [_gathered_shape(s, bc) for s, bc in zip(shards, by_cols)] + [_sds((N_CHIPS,) + a.shape, a.dtype) for a in whole]
    return _Hosted(list(shards) + list(whole) + [into[w] for w in given], out_shapes, 3 * n_ops, start, finish,
                   aliases={n_ops + i: w for i, w in enumerate(given)})


def _plan_gather_d2d(bufs, shard_shapes, bands=None):
    n = len(bufs)
    bands = bands or [WHOLE_HALF] * n

    def copies(ins, outs, sems):
        send_sems, recv_sems = sems
        x, y, c, chips = _place()
        sibling = (x, y, 1 - c)
        sends, waits = [], []
        for w in range(n):
            for k, (cx, cy) in enumerate(chips):
                sem = (send_sems.at[3 * w + k], recv_sems.at[3 * w + k])
                landed = _gather_slot(shard_shapes[w], outs[w], 2 * cx + cy, c, bands[w])
                other = _gather_slot(shard_shapes[w], outs[w], 2 * cx + cy, 1 - c, bands[w])
                sends.append(_rcopy(landed, landed, *sem, sibling))
                waits.append(_rcopy(other, other, *sem, sibling))
        return sends, waits

    def start(ins, outs, sems):
        for cp in copies(ins, outs, sems)[0]:
            cp.start()

    def finish(ins, outs, sems):
        sends, waits = copies(ins, outs, sems)
        for cp in waits:
            cp.wait_recv()
        for cp in sends:
            cp.wait_send()

    return _Hosted(bufs, [_sds(b.shape, b.dtype) for b in bufs], 3 * n, start, finish, aliases={w: w for w in range(n)})


def _plan_swap(grads):
    def copies(ins, outs, sems):
        send_sems, recv_sems = sems
        x, y, c, _ = _place()
        cps = []
        for w, g_ref in enumerate(ins):
            if len(g_ref.shape) == 4:
                theirs = g_ref.at[:, 1 - c]
            else:
                hr = g_ref.shape[0] // 2
                theirs = g_ref.at[pl.ds((1 - c) * hr, hr), :]
            cps.append(_rcopy(theirs, outs[w], send_sems.at[w], recv_sems.at[w], (x, y, 1 - c)))
        return cps

    def start(ins, outs, sems):
        for cp in copies(ins, outs, sems):
            cp.start()

    def finish(ins, outs, sems):
        for cp in copies(ins, outs, sems):
            cp.wait()

    out_shapes = [_sds((g.shape[0], g.shape[2], g.shape[3])) if g.ndim == 4 else _sds((g.shape[0] // 2, g.shape[1]))
                  for g in grads]
    return _Hosted(grads, out_shapes, len(grads), start, finish)


def _plan_scatter(chip_sums, windows):
    def copies(ins, outs, sems):
        send_sems, recv_sems = sems
        x, y, c, chips = _place()
        me = 2 * x + y
        sends, waits = [], []
        for w, s_ref in enumerate(ins):
            for k, (cx, cy) in enumerate(chips):
                tgt = 2 * cx + cy
                if windows[w] is not None:
                    stride, width = windows[w]
                    part = s_ref.at[:, pl.ds(pl.multiple_of(tgt * stride, LANES), width)]
                else:
                    part = s_ref.at[tgt]
                sem = (send_sems.at[3 * w + k], recv_sems.at[3 * w + k])
                sends.append(_rcopy(part, outs[w].at[me], *sem, (cx, cy, c)))
                slot = outs[w].at[tgt]
                waits.append(_rcopy(slot, slot, *sem, (cx, cy, c)))
        return sends, waits

    def start(ins, outs, sems):
        for cp in copies(ins, outs, sems)[0]:
            cp.start()

    def finish(ins, outs, sems):
        sends, waits = copies(ins, outs, sems)
        for cp in waits:
            cp.wait_recv()
        for cp in sends:
            cp.wait_send()

    out_shapes = [_sds((N_CHIPS, s.shape[0], win[1]), BF16) if win is not None else _sds(s.shape, BF16)
                  for s, win in zip(chip_sums, windows)]
    return _Hosted(chip_sums, out_shapes, 3 * len(chip_sums), start, finish)


def _plan_join(reds):
    def copies(ins, outs, sems):
        send_sems, recv_sems = sems
        x, y, c, _ = _place()
        sends, waits = [], []
        for w, out in enumerate(outs):
            hr = out.shape[0] // 2
            mine = out.at[pl.ds(c * hr, hr), :]
            theirs = out.at[pl.ds((1 - c) * hr, hr), :]
            sends.append(_rcopy(mine, mine, send_sems.at[w], recv_sems.at[w], (x, y, 1 - c)))
            waits.append(_rcopy(theirs, theirs, send_sems.at[w], recv_sems.at[w], (x, y, 1 - c)))
        return sends, waits

    def start(ins, outs, sems):
        for cp in copies(ins, outs, sems)[0]:
            cp.start()

    def finish(ins, outs, sems):
        sends, waits = copies(ins, outs, sems)
        for cp in waits:
            cp.wait_recv()
        for cp in sends:
            cp.wait_send()

    return _Hosted(reds, [_sds(r.shape) for r in reds], len(reds), start, finish, aliases={w: w for w in range(len(reds))})


def _allreduce_small(v):
    m_per = v.shape[0]

    def body(v_ref, out_ref, all_ref, send_sems, recv_sems, local_sem):
        x, y, c, chips = _place()
        me, sibling = (x, y, c), (x, y, 1 - c)

        def rows(px, py, pc):
            return all_ref.at[pl.ds((4 * px + 2 * py + pc) * m_per, m_per), :]

        def copy(k, block, to, src=None):
            return _rcopy(rows(*block) if src is None else src, rows(*block), send_sems.at[k], recv_sems.at[k], to)

        mine = pltpu.make_async_copy(v_ref, rows(*me), local_sem)
        mine.start()
        first = [copy(0, me, sibling, src=v_ref)]
        first += [copy(1 + k, me, (*chip, c), src=v_ref) for k, chip in enumerate(chips)]
        for cp in first:
            cp.start()
        passed = [copy(4 + k, (*chip, c), sibling) for k, chip in enumerate(chips)]
        for k, chip in enumerate(chips):
            copy(1 + k, (*chip, c), me).wait_recv()
            passed[k].start()
        copy(0, sibling, me).wait_recv()
        for k, chip in enumerate(chips):
            copy(4 + k, (*chip, 1 - c), me).wait_recv()
        for cp in first + passed:
            cp.wait_send()
        mine.wait()
        acc = all_ref[pl.ds(0, m_per), :]
        for d in range(1, 8):
            acc = acc + all_ref[pl.ds(d * m_per, m_per), :]
        out_ref[...] = acc

    vm = pl.BlockSpec(memory_space=pltpu.VMEM)
    return _pallas(body, name="allreduce_small", in_specs=[vm], out_specs=vm, out_shape=_sds((m_per, LANES)),
                          scratch_shapes=[pltpu.VMEM((8 * m_per, LANES), F32), pltpu.SemaphoreType.DMA((7,)),
                                          pltpu.SemaphoreType.DMA((7,)), pltpu.SemaphoreType.DMA],
                          compiler_params=pltpu.CompilerParams(vmem_limit_bytes=VMEM_LIMIT))(v)


def _block_diag(blocks):
    j, g, a, b = blocks.shape
    eye = jnp.eye(g, dtype=bool)[None, :, None, :, None]
    return jnp.where(eye, blocks[:, :, :, None, :], jnp.zeros((), blocks.dtype)).reshape(j, g * a, g * b)


def _diag_blocks(m, a, b):
    j = m.shape[0]
    g = m.shape[1] // a
    t = m.reshape(j, g, a, g, b)
    eye = jnp.eye(g, dtype=bool)[None, :, None, :, None]
    return jnp.sum(jnp.where(eye, t, 0.0), axis=3)


def _pack_rows(parts, rows, dtype):
    used = sum(p.shape[0] for p in parts)
    return jnp.concatenate([p.astype(dtype) for p in parts] + [jnp.zeros((rows - used, D_MODEL), dtype)], axis=0)


_SMALL = (("g_mix", (1024,)), ("b_f", (8,)), ("g_q", (64,)), ("g_k", (64,)), ("lambda_re", (32, 64)),
          ("lambda_im", (32, 64)), ("log_step", (32,)), ("b_re", (32, 64, 16)), ("b_im", (32, 64, 16)),
          ("c_re", (32, 16, 64)), ("c_im", (32, 16, 64)), ("d_skip", (32, 16)), ("b_glu", (512,)),
          ("g_attn_out", (512,)), ("g_ssm_out", (512,)), ("g_ffn", (1024,)), ("conv_b", (5632,)))


def _small_rows(shape):
    return -(-math.prod(shape) // LANES)


def _pack_small(arrs, extra=()):
    parts = []
    for a in list(arrs) + list(extra):
        flat = a.reshape(-1)
        rows = -(-flat.shape[0] // LANES)
        parts.append(jnp.pad(flat, (0, rows * LANES - flat.shape[0])).reshape(rows, LANES))
    total = sum(p.shape[0] for p in parts)
    pad = -total % SUBLANES
    if pad:
        parts.append(jnp.zeros((pad, LANES), F32))
    return jnp.concatenate(parts, axis=0)


def _unpack_small(buf, shapes):
    out, r = [], 0
    for shape in shapes:
        n = math.prod(shape)
        rows = -(-n // LANES)
        out.append(buf[r:r + rows].reshape(-1)[:n].reshape(shape))
        r += rows
    return out


def _halves(t):
    return t.reshape(N_CHIPS, 2, t.shape[0] // (2 * N_CHIPS), t.shape[1])


class _MeshComm:
    def __init__(self, args):
        x, y, self.core = lax.axis_index("x"), lax.axis_index("y"), lax.axis_index("c")
        self.chip = 2 * x + y
        self.place = jnp.stack([self.chip, self.core]).astype(jnp.int32)
        self.shards = {n: args[n].astype(BF16) for n in ("w_in", "w_glu", "w_out", "w_up", "w_down")}
        self.conv_w = args["conv_w"]

    def _own(self, stacked, mine):
        return lax.dynamic_update_slice(stacked, mine[None], (self.chip,) + (0,) * mine.ndim)

    def w_in(self):
        sh = self.shards["w_in"]
        (buf,) = _run_hosted(_plan_gather_ici([sh], [False]), name="gather_w_in")
        (buf,) = _run_hosted(_plan_gather_d2d([buf], [sh]), name="pass_w_in")
        whole = self._own(buf, sh).transpose(1, 0, 2).reshape(D_MODEL, IN_COLS)
        return jnp.pad(whole, ((0, 0), (0, Z_COLS - IN_COLS)))

    def gather_first(self):
        self.mid = [self.shards[n] for n in ("w_glu", "w_out", "w_down")]
        return _plan_gather_ici(self.mid + [self.shards["w_up"]], [False, False, False, True], whole=[self.conv_w],
                                bands=[WHOLE_HALF] * 3 + [(0, 1, 4)])

    def gather_second(self, landed):
        self.g_cw = landed[4]
        return _both(_plan_gather_d2d(list(landed[:3]), self.mid),
                     _plan_gather_ici([self.shards["w_up"]], [True], bands=[(1, 3, 4)], into=[landed[3]]))

    def weights(self, gathered):
        g_glu, g_out, g_down = gathered[:3]
        own = self._own
        return (own(g_glu, self.mid[0]).reshape(SSM_W, SSM_W), own(g_out, self.mid[1]).reshape(D_MODEL, D_MODEL),
                own(g_down, self.mid[2]).reshape(D_FF, D_MODEL),
                own(self.g_cw, self.conv_w).transpose(1, 0, 2).reshape(3, 2 * D_FF))

    def gather_third(self, gathered):
        return _plan_gather_d2d([gathered[3]], [self.shards["w_up"]])

    def w_up(self, passed):
        return _place_cols(passed[0], self.shards["w_up"], self.place)

    def swap(self, d_w_down, d_w_up, d_w_glu, d_w_out):
        self.early = [_halves(d_w_down), d_w_up, _halves(d_w_glu), _halves(d_w_out)]
        return _plan_swap(self.early)

    def scatter(self, landed):
        self.early_sums = [_add_half(g, l, self.place, name="add_" + n)
                           for g, l, n in zip(self.early, landed, ("w_down", "w_up", "w_glu", "w_out"))]
        return _plan_scatter(self.early_sums, [None, (UP_COLS, UP_COLS), None, None])

    def swap_in(self, d_w_in):
        self.d_in = d_w_in
        return _plan_swap([d_w_in])

    def scatter_in(self, landed):
        self.sum_in = _add_half(self.d_in, landed[0], self.place, name="add_w_in")
        return _plan_scatter([self.sum_in], [(IN_STRIDE, IN_WINDOW)])

    def reduce(self, lands):
        early_lands, (land_in,) = lands
        sum_in = self.sum_in
        es, el = self.early_sums, early_lands
        todo = [(sum_in, land_in, "w_in", LANES, IN_STRIDE), (es[2], el[2], "w_glu", SSM_W, 0),
                (es[3], el[3], "w_out", D_MODEL, 0), (es[1], el[1], "w_up", UP_COLS, UP_COLS),
                (es[0], el[0], "w_down", D_MODEL, 0)]
        reds = _run_hosted(_plan_join([_sum_chips(s, l, self.place, name="sum_" + n, tc=tc, window_stride=st)
                                       for s, l, n, tc, st in todo]), name="join_halves")
        g_big = dict(zip(("w_in", "w_glu", "w_out", "w_up", "w_down"), reds))
        g_big["w_in"] = lax.dynamic_slice_in_dim(reds[0], 2 * self.chip, IN_COLS // N_CHIPS, axis=1)
        return g_big


def _local_step(x, tgt, p, comm):
    s = x.shape[0]
    row = lambda v: v.reshape(1, -1)
    g_mix, g_ffn = row(p["g_mix"]), row(p["g_ffn"])
    g_att, g_ssm, b_glu, conv_b = row(p["g_attn_out"]), row(p["g_ssm_out"]), row(p["b_glu"]), row(p["conv_b"])
    gq = row(jnp.tile(p["g_q"], HEADS))
    gk = row(jnp.tile(p["g_k"], HEADS))
    bf = row(jnp.pad(p["b_f"], (0, LANES - HEADS)))
    gg = jnp.kron(jnp.eye(HEADS, dtype=F32), jnp.ones((HEAD_DIM, HEAD_DIM), F32)).astype(BF16)
    dsk = row(p["d_skip"])

    rep = lambda a: jnp.repeat(a, SSM_GROUP, axis=0)
    lr, li = rep(p["lambda_re"]), rep(p["lambda_im"])
    ls = rep(jnp.broadcast_to(p["log_step"][:, None], (SSM_GROUPS, SSM_STATE)))
    bt_re = p["b_re"].transpose(0, 2, 1).reshape(_PARAM_SHAPE)
    bt_im = p["b_im"].transpose(0, 2, 1).reshape(_PARAM_SHAPE)
    a_re_rep, a_im_rep, bb_re, bb_im = _ssm_params(lr, li, ls, bt_re, bt_im)
    ar = a_re_rep[::SSM_GROUP].reshape(SSM_CHUNKS, 1, CHUNK_S)
    ai = a_im_rep[::SSM_GROUP].reshape(SSM_CHUNKS, 1, CHUNK_S)
    chunked = lambda t: t.reshape(SSM_CHUNKS, SSM_GROUPS // SSM_CHUNKS, SSM_GROUP, SSM_STATE)
    bbr = _block_diag(chunked(bb_re)).astype(BF16)
    bbi = _block_diag(chunked(bb_im)).astype(BF16)
    to_cc = lambda c: _block_diag(chunked(c).transpose(0, 1, 3, 2)).astype(BF16)
    ccr, cci = to_cc(p["c_re"]), to_cc(p["c_im"])

    w_in_r = comm.w_in()
    hb, z = _in_proj(x, g_mix, w_in_r)
    qh, kh, vh, ub, uf, c128 = _attn_prep(z, gq, gk, bf, gg)
    crow = c128[:, :HEADS].T.reshape(HEADS, 1, s)
    (oh, lse), landed = _attn_fwd(qh, kh, vh, crow, comm.gather_first())
    (xr, xi, y), gathered = _ssm_fwd(ub, uf, bbr, bbi, ar, ai, ccr, cci, dsk, comm.gather_second(landed))
    w_glu_b, w_out_b, w_down_b, conv_w_full = comm.weights(gathered)
    (x1, mixb, h2b), passed = _mix_out(y, oh, x, w_glu_b, b_glu, g_att, g_ssm, w_out_b, g_ffn, comm.gather_third(gathered))
    w_up_b = comm.w_up(passed)
    up = _mm(h2b, w_up_b, name="ffn_up", tm=1024, tn=1408, tk=1024)
    act = _conv_act(up, conv_w_full, conv_b)
    dy, dyb, loss_blk = _down_loss(act, w_down_b, x1, tgt)

    d_w_down = _mm(act, dyb, ta=True, name="d_w_down", tm=1408, tn=1024, tk=2048)
    dact = _mm(dyb, w_down_b, tb=True, name="d_act", tm=1024, tn=1408, tk=1024)
    dupb, dcw = _conv_act_bwd(up, dact, conv_w_full, conv_b)
    d_w_up = _mm(h2b, dupb, ta=True, b_parts=2, name="d_w_up", tm=1024, tn=1408, tk=2048)
    dh2 = _mm(dupb, w_up_b, tb=True, a_parts=2, name="d_h2", tm=1024, tn=1024, tk=1408)
    dx1, dx1b, doh, dys, d_w_glu, d_g_ffn, d_g_att, d_g_ssm, d_b_glu = _mix_bwd(
        dy, dh2, x1, g_ffn, w_out_b, y, oh, w_glu_b, b_glu, g_att, g_ssm)
    d_w_out = _mm(mixb, dx1b, ta=True, name="d_w_out", tm=1024, tn=1024, tk=2048)
    (du, dbbr, dbbi, dccr, dcci, dar, dai, dd), swapped = _ssm_bwd(dys, uf, ub, xr, xi, bbr, bbi, ar, ai, ccr, cci, dsk,
                                                                comm.swap(d_w_down, d_w_up, d_w_glu, d_w_out))
    (dqh, dkh, dvh, dcrow), early_lands = _attn_bwd(qh, kh, vh, crow, lse, doh, comm.scatter(swapped))
    dc128 = jnp.pad(dcrow.reshape(HEADS, s).T, ((0, 0), (0, LANES - HEADS)))
    dzb, d_gq, d_gk, d_bf = _prep_bwd(z, dqh, dkh, dvh, du, dc128, gq, gk, bf, gg)
    d_w_in_r = _mm(hb, dzb, ta=True, name="d_w_in", tm=512, tn=Z_COLS, tk=2048)
    dh, swapped_in = _mm(dzb, w_in_r, tb=True, name="d_h", tm=1024, tn=1024, tk=Z_COLS, carry=True,
                         hosted=comm.swap_in(d_w_in_r))
    (dx, d_g_mix), land_in = _in_norm_bwd(x, g_mix, dh, dx1, comm.scatter_in(swapped_in))

    unchunk = lambda t: t.reshape(_PARAM_SHAPE)
    dbb_re = unchunk(_diag_blocks(dbbr, SSM_GROUP, SSM_STATE))
    dbb_im = unchunk(_diag_blocks(dbbi, SSM_GROUP, SSM_STATE))
    first_row = (jnp.arange(_PARAM_SHAPE[0]) % SSM_GROUP == 0)[:, None]
    da_re = jnp.where(first_row, rep(dar.reshape(SSM_GROUPS, SSM_STATE)), 0.0)
    da_im = jnp.where(first_row, rep(dai.reshape(SSM_GROUPS, SSM_STATE)), 0.0)
    expand_t = (jnp.arange(SSM_GROUPS)[:, None] == (jnp.arange(_PARAM_SHAPE[0]) // SSM_GROUP)[None, :]).astype(BF16)
    d_lr, d_li, d_ls, d_bt_re, d_bt_im = _ssm_params_bwd(lr, li, ls, bt_re, bt_im, da_re, da_im, dbb_re, dbb_im, expand_t)
    from_bt = lambda t: t.reshape(SSM_GROUPS, SSM_GROUP, SSM_STATE).transpose(0, 2, 1)
    from_cc = lambda t: _diag_blocks(t, SSM_STATE, SSM_GROUP).transpose(0, 1, 3, 2).reshape(SSM_GROUPS, SSM_GROUP, SSM_STATE)

    small = {
        "g_mix": d_g_mix, "b_f": d_bf[0, :HEADS], "g_q": d_gq.reshape(HEADS, HEAD_DIM).sum(0),
        "g_k": d_gk.reshape(HEADS, HEAD_DIM).sum(0), "lambda_re": d_lr, "lambda_im": d_li, "log_step": d_ls,
        "b_re": from_bt(d_bt_re), "b_im": from_bt(d_bt_im), "c_re": from_cc(dccr), "c_im": from_cc(dcci),
        "d_skip": dd, "b_glu": d_b_glu, "g_attn_out": d_g_att, "g_ssm_out": d_g_ssm, "g_ffn": d_g_ffn,
        "conv_b": dcw[:, 3],
    }
    big = {"w_in": d_w_in_r, "w_glu": d_w_glu, "w_out": d_w_out, "w_up": d_w_up, "w_down": d_w_down}
    return loss_blk[0, 0], dx, big, small, dcw[:, 0:3].transpose(1, 0, 2).reshape(3, 2 * D_FF), (early_lands, land_in)


def kernel(x, g_mix, w_in, b_f, g_q, g_k, lambda_re, lambda_im, log_step, b_re, b_im, c_re, c_im, d_skip, w_glu, b_glu, g_attn_out, g_ssm_out, w_out, g_ffn, w_up, conv_w, conv_b, w_down, loss_target, m_g_mix, m_w_in, m_b_f, m_g_q, m_g_k, m_lambda_re, m_lambda_im, m_log_step, m_b_re, m_b_im, m_c_re, m_c_im, m_d_skip, m_w_glu, m_b_glu, m_g_attn_out, m_g_ssm_out, m_w_out, m_g_ffn, m_w_up, m_conv_w, m_conv_b, m_w_down, v_g_mix, v_w_in, v_b_f, v_g_q, v_g_k, v_lambda_re, v_lambda_im, v_log_step, v_b_re, v_b_im, v_c_re, v_c_im, v_d_skip, v_w_glu, v_b_glu, v_g_attn_out, v_g_ssm_out, v_w_out, v_g_ffn, v_w_up, v_conv_w, v_conv_b, v_w_down):
    args = dict(locals())
    order = ["g_mix", "w_in", "b_f", "g_q", "g_k", "lambda_re", "lambda_im", "log_step", "b_re", "b_im", "c_re", "c_im",
             "d_skip", "w_glu", "b_glu", "g_attn_out", "g_ssm_out", "w_out", "g_ffn", "w_up", "conv_w", "conv_b", "w_down"]
    comm = _MeshComm(args)
    chip = comm.chip
    loss_part, dx, big, small, d_conv_w, lands = _local_step(x[0], loss_target[0], args, comm)
    loss = lax.psum(loss_part, ("x", "y", "c"))

    g_big = comm.reduce(lands)

    small_names = [n for n, _ in _SMALL]
    small_shapes = [sh for _, sh in _SMALL]
    gsum = _allreduce_small(_pack_small([small[n] for n in small_names], extra=[d_conv_w]))
    g_small = _unpack_small(gsum, small_shapes + [(3, 2 * D_FF)])
    g_conv_w = lax.dynamic_slice_in_dim(g_small[-1], chip * (2 * D_FF // N_CHIPS), 2 * D_FF // N_CHIPS, axis=1)
    g_small = dict(zip(small_names, g_small[:-1]))

    grad, delta, new_m, new_v = {}, {}, {}, {}
    for n in ("w_in", "w_glu", "w_out", "w_up", "w_down"):
        grad[n] = g_big[n]
        delta[n], new_m[n], new_v[n] = _adamw(args[n], g_big[n], args["m_" + n], args["v_" + n], name="adamw_" + n)
    grad["conv_w"] = g_conv_w
    delta["conv_w"], new_m["conv_w"], new_v["conv_w"] = _adamw(conv_w, g_conv_w, m_conv_w, v_conv_w, name="adamw_conv_w")
    stepped = _adamw_small([args[n] for n in small_names], [g_small[n] for n in small_names],
                           [args["m_" + n] for n in small_names], [args["v_" + n] for n in small_names])
    for i, n in enumerate(small_names):
        grad[n] = g_small[n]
        delta[n], new_m[n], new_v[n] = stepped[3 * i:3 * i + 3]

    return (loss, dx[None], *[grad[n] for n in order], *[delta[n] for n in order], *[new_m[n] for n in order],
            *[new_v[n] for n in order])
```

```python
import math

import jax
import jax.numpy as jnp
from jax import lax
from jax.experimental import pallas as pl
from jax.experimental.pallas import tpu as pltpu

F32 = jnp.float32
BF16 = jnp.bfloat16

D_MODEL = 1024
HEADS = 8
HEAD_DIM = 64
ATTN_W = 512
SSM_W = 512
SSM_GROUPS = 32
SSM_GROUP = 16
SSM_STATE = 64
N_STATE = SSM_GROUPS * SSM_STATE
D_FF = 2816
IN_COLS = 2056
Z_COLS = 2176
F_COL0 = 1536
U_COL0 = 1544
EPS = 1e-6
NEG_INF = -1e30
N_CHIPS = 4
LANES = 128
SUBLANES = 8
SSM_CHUNKS = 2
CHUNK_U = SSM_W // SSM_CHUNKS
CHUNK_S = N_STATE // SSM_CHUNKS
HEADS_PER_STEP = 4
STRIP = 128
N_STRIPS = D_FF // STRIP

ADAM_LR = 0.001
ADAM_B1 = 0.9
ADAM_B2 = 0.999
ADAM_EPS = 1e-08
ADAM_WD = 0.01
ADAM_STEP = 10

VMEM_LIMIT = 56 * 1024 * 1024
MESH = pl.DeviceIdType.MESH


def _pallas(body, **kw):
    return pl.pallas_call(body, **kw)


def _pcall(body, *, name, out_shape, in_specs, out_specs, grid=(), scratch_shapes=(), dims=None):
    params = pltpu.CompilerParams(dimension_semantics=dims, vmem_limit_bytes=VMEM_LIMIT)
    return _pallas(body, name=name, grid=grid, in_specs=in_specs, out_specs=out_specs,
                   out_shape=out_shape, scratch_shapes=scratch_shapes, compiler_params=params)


def _sds(shape, dtype=F32):
    return jax.ShapeDtypeStruct(shape, dtype)


def _dot(a, b):
    return jnp.dot(a, b, preferred_element_type=F32)


def _dot_nt(a, b):
    return lax.dot_general(a, b, (((1,), (1,)), ((), ())), preferred_element_type=F32)


def _dot_tn(a, b):
    return lax.dot_general(a, b, (((0,), (0,)), ((), ())), preferred_element_type=F32)


def _split3(x):
    hi = x.astype(BF16)
    r = x - hi.astype(F32)
    mid = r.astype(BF16)
    lo = (r - mid.astype(F32)).astype(BF16)
    return hi, mid, lo


def _dot_exact_r(x, m01):
    hi, mid, lo = _split3(x)
    return _dot(hi, m01) + _dot(mid, m01) + _dot(lo, m01)


def _dot_exact_l(m01, x):
    hi, mid, lo = _split3(x)
    return _dot(m01, hi) + _dot(m01, mid) + _dot(m01, lo)


def _sigmoid(x):
    return 1.0 / (1.0 + jnp.exp(-x))


def _rms(x, g):
    r = lax.rsqrt(jnp.mean(x * x, axis=-1, keepdims=True) + EPS)
    return x * r * g


def _rms_bwd(x, g, dy):
    r = lax.rsqrt(jnp.mean(x * x, axis=-1, keepdims=True) + EPS)
    w = dy * g
    dx = r * w - x * (r * r * r) * jnp.mean(w * x, axis=-1, keepdims=True)
    dg = jnp.sum(dy * x * r, axis=0, keepdims=True)
    return dx, dg


_GELU_K = math.sqrt(2.0 / math.pi)
_GELU_C = 0.044715


def _gelu(y):
    return y * (0.5 * (1.0 + jnp.tanh(_GELU_K * (y + _GELU_C * (y * y * y)))))


def _gelu_grad(y):
    t = jnp.tanh(_GELU_K * (y + _GELU_C * (y * y * y)))
    return 0.5 * (1.0 + t) + 0.5 * y * (1.0 - t * t) * (_GELU_K * (1.0 + 3.0 * _GELU_C * y * y))


def _tile(n, pref):
    if n <= pref:
        return n
    divs = [t for t in range(LANES, n + 1, LANES) if n % t == 0]
    below = [t for t in divs if t <= pref]
    if below and 2 * below[-1] >= pref:
        return below[-1]
    above = [t for t in divs if t > pref]
    return above[0] if above else n


def _row_tile(s):
    return min(256, s)


def _mm(a, b, *, name, tm, tn, tk, ta=False, tb=False, a_parts=1, b_parts=1, carry=False, hosted=None):
    if a_parts > 1:
        m, kk = a.shape[1], a.shape[2] * a_parts
    elif ta:
        kk, m = a.shape
    else:
        m, kk = a.shape
    if b_parts > 1:
        n = b.shape[2] * b_parts
    else:
        n = b.shape[0] if tb else b.shape[1]
    tm, tn, tk = _tile(m, tm), _tile(n // b_parts, tn), _tile(kk // a_parts, tk)
    k_per, n_per = kk // a_parts // tk, n // b_parts // tn

    def body(a_ref, b_ref, o_ref):
        k = pl.program_id(2)
        if ta:
            part = _dot_tn(a_ref[...], b_ref[...])
        elif tb:
            part = _dot_nt(a_ref[...], b_ref[...])
        else:
            part = _dot(a_ref[...], b_ref[...])

        @pl.when(k == 0)
        def _():
            o_ref[...] = part

        @pl.when(k > 0)
        def _():
            o_ref[...] += part

    if a_parts > 1:
        a_spec = pl.BlockSpec((None, tm, tk), lambda i, j, k: (k // k_per, i, k % k_per))
    else:
        a_spec = pl.BlockSpec((tk, tm), lambda i, j, k: (k, i)) if ta else pl.BlockSpec((tm, tk), lambda i, j, k: (i, k))
    if b_parts > 1:
        b_spec = pl.BlockSpec((None, tk, tn), lambda i, j, k: (j // n_per, k, j % n_per))
    else:
        b_spec = pl.BlockSpec((tn, tk), lambda i, j, k: (j, k)) if tb else pl.BlockSpec((tk, tn), lambda i, j, k: (k, j))
    grid = (m // tm, n // tn, kk // tk)
    at = lambda step: (lambda: jnp.logical_and(jnp.logical_and(pl.program_id(0) == step[0], pl.program_id(1) == step[1]),
                                               pl.program_id(2) == step[2]))
    (out,), carried = _host_pcall(body, hosted, at((0, 0, 0)), at(tuple(g - 1 for g in grid)), n_in=2, n_out=1, n_scratch=0,
                                  name=name, grid=grid, in_specs=[a_spec, b_spec],
                                  out_specs=[pl.BlockSpec((tm, tn), lambda i, j, k: (i, j))], out_shape=[_sds((m, n))],
                                  scratch_shapes=[], dims=("parallel", "parallel", "arbitrary"), operands=(a, b))
    return (out, carried) if carry else out


def _in_proj(x, g_mix, w_in_r):
    s = x.shape[0]
    tm = _row_tile(s)

    def body(x_ref, g_ref, w_ref, h_ref, z_ref):
        h = _rms(x_ref[...], g_ref[...]).astype(BF16)
        h_ref[...] = h
        z_ref[...] = _dot(h, w_ref[...])

    return _pcall(body, name="in_proj", grid=(s // tm,),
                  in_specs=[pl.BlockSpec((tm, D_MODEL), lambda i: (i, 0)), pl.BlockSpec((1, D_MODEL), lambda i: (0, 0)),
                            pl.BlockSpec((D_MODEL, Z_COLS), lambda i: (0, 0))],
                  out_specs=[pl.BlockSpec((tm, D_MODEL), lambda i: (i, 0)), pl.BlockSpec((tm, Z_COLS), lambda i: (i, 0))],
                  out_shape=[_sds((s, D_MODEL), BF16), _sds((s, Z_COLS))], dims=("parallel",))(x, g_mix, w_in_r)


def _split_heads(ref, val):
    for h in range(HEADS):
        ref[h] = val[:, h * HEAD_DIM:(h + 1) * HEAD_DIM].astype(ref.dtype)


def _merge_heads(ref):
    return jnp.concatenate([ref[h].astype(F32) for h in range(HEADS)], axis=-1)


def _forget_logits(z_ref, bf_ref):
    fl = z_ref[:, F_COL0:F_COL0 + LANES] + bf_ref[...]
    return jnp.where(lax.broadcasted_iota(jnp.int32, fl.shape, 1) < HEADS, fl, 0.0)


def _attn_prep(z, gq, gk, bf, gg):
    s = z.shape[0]
    tm = _row_tile(s)

    def body(z_ref, gq_ref, gk_ref, bf_ref, gg_ref, qn_ref, kn_ref, vb_ref, ub_ref, uf_ref, c_ref, carry_ref):
        i = pl.program_id(0)

        @pl.when(i == 0)
        def _():
            carry_ref[...] = jnp.zeros_like(carry_ref)

        gg_m = gg_ref[...]

        def head_norm(t, g):
            ssq = _dot_exact_r(t * t, gg_m)
            return t * lax.rsqrt(ssq * (1.0 / HEAD_DIM) + EPS) * g

        _split_heads(qn_ref, head_norm(z_ref[:, 0:ATTN_W], gq_ref[...]))
        _split_heads(kn_ref, head_norm(z_ref[:, ATTN_W:2 * ATTN_W], gk_ref[...]))
        _split_heads(vb_ref, z_ref[:, 2 * ATTN_W:3 * ATTN_W])
        u = z_ref[:, U_COL0:U_COL0 + SSM_W]
        uf_ref[...] = u
        ub_ref[...] = u.astype(BF16)
        fl = _forget_logits(z_ref, bf_ref)
        lf = jnp.minimum(fl, 0.0) - jnp.log1p(jnp.exp(-jnp.abs(fl)))
        row = lax.broadcasted_iota(jnp.int32, (tm, tm), 0)
        col = lax.broadcasted_iota(jnp.int32, (tm, tm), 1)
        tri = (row >= col).astype(BF16)
        c = _dot_exact_l(tri, lf) + carry_ref[...]
        c_ref[...] = c
        carry_ref[...] = c[tm - 1:tm, :]

    row_spec = lambda w: pl.BlockSpec((tm, w), lambda i: (i, 0))
    const = lambda shape: pl.BlockSpec(shape, lambda i: (0, 0))
    heads = pl.BlockSpec((HEADS, tm, HEAD_DIM), lambda i: (0, i, 0))
    return _pcall(body, name="attn_prep", grid=(s // tm,),
                  in_specs=[row_spec(Z_COLS), const((1, ATTN_W)), const((1, ATTN_W)), const((1, LANES)), const((ATTN_W, ATTN_W))],
                  out_specs=[heads] * 3 + [row_spec(SSM_W), row_spec(SSM_W), row_spec(LANES)],
                  out_shape=[_sds((HEADS, s, HEAD_DIM), BF16)] * 3 + [_sds((s, SSM_W), BF16), _sds((s, SSM_W)), _sds((s, LANES))],
                  scratch_shapes=[pltpu.VMEM((1, LANES), F32)], dims=("arbitrary",))(z, gq, gk, bf, gg)


def _attn_fwd(qh, kh, vh, crow, hosted=None):
    _, s, _ = qh.shape
    tq = _row_tile(s)
    scale = HEAD_DIM ** -0.5

    hp = HEADS_PER_STEP
    nq = s // tq
    fold = lambda t, op: op(t[:, :tq // 2], t[:, tq // 2:])

    def body(q_ref, k_ref, v_ref, c_ref, o_ref, lse_ref, s_s):
        i = pl.program_id(1)

        def first(j, ms, diagonal):
            off = pl.multiple_of(j * tq, tq)
            out = []
            for hh in range(hp):
                sc = _dot_nt(q_ref[hh], k_ref[hh, pl.ds(off, tq), :]) * scale - c_ref[hh, :, pl.ds(off, tq)]
                if diagonal:
                    causal = lax.broadcasted_iota(jnp.int32, (tq, tq), 1) <= lax.broadcasted_iota(jnp.int32, (tq, tq), 0)
                    sc = jnp.where(causal, sc, NEG_INF)
                s_s[hh, j] = sc
                out.append(jnp.maximum(ms[hh], fold(sc, jnp.maximum)))
            return tuple(out)

        ms = lax.fori_loop(0, i, lambda j, c: first(j, c, False), (jnp.full((tq, tq // 2), NEG_INF, F32),) * hp)
        ms = [jnp.max(t, axis=-1, keepdims=True) for t in first(i, ms, True)]

        def second(j, carry):
            rows = pl.ds(pl.multiple_of(j * tq, tq), tq)
            out = []
            for hh in range(hp):
                ls, acc = carry[hh]
                p = jnp.exp(s_s[hh, j] - ms[hh])
                out.append((ls + fold(p, jnp.add), acc + _dot(p.astype(BF16), v_ref[hh, rows, :])))
            return tuple(out)

        zero = (jnp.zeros((tq, tq // 2), F32), jnp.zeros((tq, HEAD_DIM), F32))
        for hh, (ls, acc) in enumerate(lax.fori_loop(0, i + 1, second, (zero,) * hp)):
            l = jnp.sum(ls, axis=-1, keepdims=True)
            o_ref[hh] = acc / l
            lse_ref[hh] = ms[hh] + jnp.log(l)

    blk = pl.BlockSpec((hp, tq, HEAD_DIM), lambda h, i: (h, i, 0))
    full = pl.BlockSpec((hp, s, HEAD_DIM), lambda h, i: (h, 0, 0))
    nh = HEADS // hp
    first = lambda: jnp.logical_and(pl.program_id(0) == 0, pl.program_id(1) == 0)
    last = lambda: jnp.logical_and(pl.program_id(0) == nh - 1, pl.program_id(1) == nq - 1)
    return _host_pcall(body, hosted, first, last, n_in=4, n_out=2, n_scratch=1, name="attn_fwd", grid=(nh, nq),
                       in_specs=[blk, full, full, pl.BlockSpec((hp, 1, s), lambda h, i: (h, 0, 0))],
                       out_specs=[blk, pl.BlockSpec((hp, tq, 1), lambda h, i: (h, i, 0))],
                       out_shape=[_sds((HEADS, s, HEAD_DIM)), _sds((HEADS, s, 1))],
                       scratch_shapes=[pltpu.VMEM((hp, nq, tq, tq), F32)],
                       dims=("parallel", "parallel"), operands=(qh, kh, vh, crow))


def _ssm_param_fn(lr, li, ls, br, bi):
    step = jnp.exp(ls)
    er = jnp.exp(lr * step)
    ab_re = er * jnp.cos(li * step)
    ab_im = er * jnp.sin(li * step)
    num_re = ab_re - 1.0
    num_im = ab_im
    den = lr * lr + li * li
    f_re = (num_re * lr + num_im * li) / den
    f_im = (num_im * lr - num_re * li) / den
    bb_re = f_re * br - f_im * bi
    bb_im = f_re * bi + f_im * br
    return ab_re, ab_im, bb_re, bb_im


_PARAM_SHAPE = (SSM_GROUPS * SSM_GROUP, SSM_STATE)


def _ssm_params(lr, li, ls, br, bi):
    def body(lr_ref, li_ref, ls_ref, br_ref, bi_ref, ar_ref, ai_ref, bbr_ref, bbi_ref):
        ar, ai, bbr, bbi = _ssm_param_fn(lr_ref[...], li_ref[...], ls_ref[...], br_ref[...], bi_ref[...])
        ar_ref[...] = ar
        ai_ref[...] = ai
        bbr_ref[...] = bbr
        bbi_ref[...] = bbi

    spec = pl.BlockSpec(_PARAM_SHAPE, lambda: (0, 0))
    return _pcall(body, name="ssm_params", in_specs=[spec] * 5, out_specs=[spec] * 4,
                  out_shape=[_sds(_PARAM_SHAPE)] * 4)(lr, li, ls, br, bi)


def _ssm_params_bwd(lr, li, ls, br, bi, dar, dai, dbbr, dbbi, expand_t):
    def body(lr_ref, li_ref, ls_ref, br_ref, bi_ref, dar_ref, dai_ref, dbbr_ref, dbbi_ref, et_ref,
             dlr_ref, dli_ref, dls_ref, dbr_ref, dbi_ref):
        _, vjp = jax.vjp(_ssm_param_fn, lr_ref[...], li_ref[...], ls_ref[...], br_ref[...], bi_ref[...])
        dlr, dli, dls, dbr, dbi = vjp((dar_ref[...], dai_ref[...], dbbr_ref[...], dbbi_ref[...]))
        et = et_ref[...]
        dlr_ref[...] = _dot_exact_l(et, dlr)
        dli_ref[...] = _dot_exact_l(et, dli)
        dls_ref[...] = jnp.sum(_dot_exact_l(et, dls), axis=-1, keepdims=True)
        dbr_ref[...] = dbr
        dbi_ref[...] = dbi

    spec = pl.BlockSpec(_PARAM_SHAPE, lambda: (0, 0))
    gspec = pl.BlockSpec((SSM_GROUPS, SSM_STATE), lambda: (0, 0))
    return _pcall(body, name="ssm_params_bwd",
                  in_specs=[spec] * 9 + [pl.BlockSpec((SSM_GROUPS, _PARAM_SHAPE[0]), lambda: (0, 0))],
                  out_specs=[gspec, gspec, pl.BlockSpec((SSM_GROUPS, 1), lambda: (0, 0)), spec, spec],
                  out_shape=[_sds((SSM_GROUPS, SSM_STATE))] * 2 + [_sds((SSM_GROUPS, 1))] + [_sds(_PARAM_SHAPE)] * 2,
                  )(lr, li, ls, br, bi, dar, dai, dbbr, dbbi, expand_t)


def _cmul(ar, ai, br, bi):
    return ar * br - ai * bi, ar * bi + ai * br


def _scan_consts(ar, ai, width, reverse):
    row = lax.broadcasted_iota(jnp.int32, (SUBLANES, width), 0)
    pw = [(ar, ai)]
    for _ in range(SUBLANES - 1):
        pw.append(_cmul(pw[-1][0], pw[-1][1], ar, ai))
    steps = []
    for d in (1, 2, 4):
        keep = (row < SUBLANES - d) if reverse else (row >= d)
        steps.append((d, jnp.where(keep, pw[d - 1][0], 0.0), jnp.where(keep, pw[d - 1][1], 0.0)))
    pr = jnp.zeros((SUBLANES, width), F32)
    pi = jnp.zeros((SUBLANES, width), F32)
    for r in range(SUBLANES):
        e = (SUBLANES - r) if reverse else (r + 1)
        pr = jnp.where(row == r, pw[e - 1][0], pr)
        pi = jnp.where(row == r, pw[e - 1][1], pi)
    return steps, pr, pi


def _scan_tile(xr, xi, cr, ci, consts, reverse):
    steps, pr, pi = consts
    for d, mr, mi in steps:
        sh = (SUBLANES - d) if reverse else d
        sr = pltpu.roll(xr, sh, 0)
        si = pltpu.roll(xi, sh, 0)
        xr, xi = xr + mr * sr - mi * si, xi + mr * si + mi * sr
    return xr + pr * cr - pi * ci, xi + pr * ci + pi * cr


def _ssm_fwd(ub, uf, bbr, bbi, ar, ai, ccr, cci, dsk, hosted=None):
    s = ub.shape[0]
    tm = _row_tile(s)
    nt = tm // SUBLANES

    def body(ub_ref, u_ref, bbr_ref, bbi_ref, ar_ref, ai_ref, ccr_ref, cci_ref, dsk_ref,
             xr_ref, xi_ref, y_ref, cr_s, ci_s):
        i = pl.program_id(1)

        @pl.when(i == 0)
        def _():
            cr_s[...] = jnp.zeros_like(cr_s)
            ci_s[...] = jnp.zeros_like(ci_s)

        u_b = ub_ref[...]
        xr_ref[...] = _dot(u_b, bbr_ref[0])
        xi_ref[...] = _dot(u_b, bbi_ref[0])
        consts = _scan_consts(ar_ref[0], ai_ref[0], CHUNK_S, False)

        def tile(k, carry):
            cr, ci = carry
            sl = pl.ds(pl.multiple_of(k * SUBLANES, SUBLANES), SUBLANES)
            xr, xi = _scan_tile(xr_ref[sl, :], xi_ref[sl, :], cr, ci, consts, False)
            xr_ref[sl, :] = xr
            xi_ref[sl, :] = xi
            return xr[SUBLANES - 1:SUBLANES, :], xi[SUBLANES - 1:SUBLANES, :]

        cr, ci = lax.fori_loop(0, nt, tile, (cr_s[...], ci_s[...]))
        cr_s[...] = cr
        ci_s[...] = ci
        y_ref[...] = (_dot(xr_ref[...].astype(BF16), ccr_ref[0]) - _dot(xi_ref[...].astype(BF16), cci_ref[0])
                      + dsk_ref[...] * u_ref[...])

    wspec = lambda a, b: pl.BlockSpec((1, a, b), lambda j, i: (j, 0, 0))
    nb = s // tm
    first = lambda: jnp.logical_and(pl.program_id(0) == 0, pl.program_id(1) == 0)
    last = lambda: jnp.logical_and(pl.program_id(0) == SSM_CHUNKS - 1, pl.program_id(1) == nb - 1)
    return _host_pcall(
        body, hosted, first, last, n_in=9, n_out=3, n_scratch=2, name="ssm_fwd", grid=(SSM_CHUNKS, nb),
        in_specs=[pl.BlockSpec((tm, CHUNK_U), lambda j, i: (i, j)),
                  pl.BlockSpec((tm, CHUNK_U), lambda j, i: (i, j)),
                  wspec(CHUNK_U, CHUNK_S), wspec(CHUNK_U, CHUNK_S), wspec(1, CHUNK_S), wspec(1, CHUNK_S),
                  wspec(CHUNK_S, CHUNK_U), wspec(CHUNK_S, CHUNK_U),
                  pl.BlockSpec((1, CHUNK_U), lambda j, i: (0, j))],
        out_specs=[pl.BlockSpec((tm, CHUNK_S), lambda j, i: (i, j)), pl.BlockSpec((tm, CHUNK_S), lambda j, i: (i, j)),
                   pl.BlockSpec((tm, CHUNK_U), lambda j, i: (i, j))],
        out_shape=[_sds((s, N_STATE)), _sds((s, N_STATE)), _sds((s, SSM_W))],
        scratch_shapes=[pltpu.VMEM((1, CHUNK_S), F32)] * 2,
        dims=("parallel", "arbitrary"), operands=(ub, uf, bbr, bbi, ar, ai, ccr, cci, dsk))


def _ssm_glu(y, w_glu, b_glu):
    ge = _gelu(y)
    sg = _sigmoid(_dot(ge.astype(BF16), w_glu) + b_glu)
    return ge, sg


def _mix_out(y, att, x, w_glu, b_glu, g_att, g_ssm, w_out, g_ffn, hosted=None):
    s = x.shape[0]
    tm = _row_tile(s)

    def body(y_ref, att_ref, x_ref, wg_ref, bg_ref, ga_ref, gs_ref, wo_ref, gf_ref, x1_ref, mix_ref, h2_ref):
        ge, sg = _ssm_glu(y_ref[...], wg_ref[...], bg_ref[...])
        ms = _rms(ge * sg, gs_ref[...]).astype(BF16)
        ma = _rms(_merge_heads(att_ref), ga_ref[...]).astype(BF16)
        mix_ref[:, 0:ATTN_W] = ma
        mix_ref[:, ATTN_W:D_MODEL] = ms
        x1 = x_ref[...] + (_dot(ma, wo_ref[0:ATTN_W, :]) + _dot(ms, wo_ref[ATTN_W:D_MODEL, :]))
        x1_ref[...] = x1
        h2_ref[...] = _rms(x1, gf_ref[...]).astype(BF16)

    row = lambda w: pl.BlockSpec((tm, w), lambda i: (i, 0))
    const = lambda a, b: pl.BlockSpec((a, b), lambda i: (0, 0))
    nb = s // tm
    return _host_pcall(body, hosted, lambda: pl.program_id(0) == 0, lambda: pl.program_id(0) == nb - 1,
                       n_in=9, n_out=3, n_scratch=0, name="mix_out", grid=(nb,),
                       in_specs=[row(SSM_W), pl.BlockSpec((HEADS, tm, HEAD_DIM), lambda i: (0, i, 0)), row(D_MODEL),
                                 const(SSM_W, SSM_W), const(1, SSM_W),
                                 const(1, ATTN_W), const(1, SSM_W), const(D_MODEL, D_MODEL), const(1, D_MODEL)],
                       out_specs=[row(D_MODEL)] * 3,
                       out_shape=[_sds((s, D_MODEL)), _sds((s, D_MODEL), BF16), _sds((s, D_MODEL), BF16)],
                       scratch_shapes=[], dims=("parallel",), operands=(y, att, x, w_glu, b_glu, g_att, g_ssm, w_out, g_ffn))


CONV_CHUNK = 64


def _conv_rows(pad_ref, w, b, r0, n):
    y = b + pad_ref[pl.ds(r0 + SUBLANES - 2, n), :] * w[0:1, :]
    y = y + pad_ref[pl.ds(r0 + SUBLANES - 1, n), :] * w[1:2, :]
    return y + pad_ref[pl.ds(r0 + SUBLANES, n), :] * w[2:3, :]


def _fill_front_pad(pad_ref, strip_ref, s):
    pad_ref[0:SUBLANES, :] = jnp.zeros((SUBLANES, STRIP), F32)
    for r0 in range(0, s, CONV_CHUNK):
        pad_ref[pl.ds(SUBLANES + r0, CONV_CHUNK), :] = strip_ref[pl.ds(r0, CONV_CHUNK), :]


def _conv_act(up, conv_w, conv_b):
    s = up.shape[0]

    def body(ug_ref, uv_ref, wg_ref, wv_ref, bg_ref, bv_ref, act_ref, pg_ref, pv_ref):
        _fill_front_pad(pg_ref, ug_ref, s)
        _fill_front_pad(pv_ref, uv_ref, s)
        wg, wv, bg, bv = wg_ref[...], wv_ref[...], bg_ref[...], bv_ref[...]
        for r0 in range(0, s, CONV_CHUNK):
            hg = _conv_rows(pg_ref, wg, bg, r0, CONV_CHUNK)
            hv = _conv_rows(pv_ref, wv, bv, r0, CONV_CHUNK)
            act_ref[pl.ds(r0, CONV_CHUNK), :] = (hg * _sigmoid(hg) * hv).astype(BF16)

    strip = lambda off: pl.BlockSpec((s, STRIP), lambda j: (0, j + off))
    wsp = lambda off: pl.BlockSpec((3, STRIP), lambda j: (0, j + off))
    bsp = lambda off: pl.BlockSpec((1, STRIP), lambda j: (0, j + off))
    return _pcall(body, name="conv_act", grid=(N_STRIPS,),
                  in_specs=[strip(0), strip(N_STRIPS), wsp(0), wsp(N_STRIPS), bsp(0), bsp(N_STRIPS)],
                  out_specs=pl.BlockSpec((s, STRIP), lambda j: (0, j)), out_shape=_sds((s, D_FF), BF16),
                  scratch_shapes=[pltpu.VMEM((s + SUBLANES, STRIP), F32)] * 2,
                  dims=("parallel",))(up, up, conv_w, conv_w, conv_b, conv_b)


def _down_loss(act, w_down, x1, tgt):
    s = x1.shape[0]
    tm = _row_tile(s)

    def body(a_ref, w_ref, x1_ref, t_ref, dy_ref, dyb_ref, loss_ref):
        i = pl.program_id(0)

        @pl.when(i == 0)
        def _():
            loss_ref[...] = jnp.zeros_like(loss_ref)

        diff = x1_ref[...] + _dot(a_ref[...], w_ref[...]) - t_ref[...]
        dy = diff * (1.0 / D_MODEL)
        dy_ref[...] = dy
        dyb_ref[...] = dy.astype(BF16)
        loss_ref[...] += 0.5 * jnp.sum(diff * dy)

    row = lambda w: pl.BlockSpec((tm, w), lambda i: (i, 0))
    return _pcall(body, name="down_loss", grid=(s // tm,),
                  in_specs=[row(D_FF), pl.BlockSpec((D_FF, D_MODEL), lambda i: (0, 0)), row(D_MODEL), row(D_MODEL)],
                  out_specs=[row(D_MODEL), row(D_MODEL), pl.BlockSpec((SUBLANES, LANES), lambda i: (0, 0))],
                  out_shape=[_sds((s, D_MODEL)), _sds((s, D_MODEL), BF16), _sds((SUBLANES, LANES))],
                  dims=("arbitrary",))(act, w_down, x1, tgt)


def _conv_act_bwd(up, dact, conv_w, conv_b):
    s = up.shape[0]
    ch = CONV_CHUNK

    def body(ug_ref, uv_ref, da_ref, wg_ref, wv_ref, bg_ref, bv_ref, dup_ref, dcw_ref, pg_ref, pv_ref, dg_ref, dv_ref):
        _fill_front_pad(pg_ref, ug_ref, s)
        _fill_front_pad(pv_ref, uv_ref, s)
        zero = jnp.zeros((SUBLANES, STRIP), F32)
        dg_ref[pl.ds(s, SUBLANES), :] = zero
        dv_ref[pl.ds(s, SUBLANES), :] = zero
        wg, wv, bg, bv = wg_ref[...], wv_ref[...], bg_ref[...], bv_ref[...]
        tile_sum = lambda t: jnp.sum(t.reshape(ch // SUBLANES, SUBLANES, STRIP), axis=0)
        accs = [[zero] * 4, [zero] * 4]
        for r0 in range(0, s, ch):
            hg = _conv_rows(pg_ref, wg, bg, r0, ch)
            hv = _conv_rows(pv_ref, wv, bv, r0, ch)
            sg = _sigmoid(hg)
            da = da_ref[pl.ds(r0, ch), :]
            dhs = (da * hv * (sg * (1.0 + hg * (1.0 - sg))), da * (hg * sg))
            for half, (dh, d_ref, p_ref) in enumerate(zip(dhs, (dg_ref, dv_ref), (pg_ref, pv_ref))):
                d_ref[pl.ds(r0, ch), :] = dh
                for k in range(3):
                    accs[half][k] = accs[half][k] + tile_sum(dh * p_ref[pl.ds(r0 + SUBLANES - 2 + k, ch), :])
                accs[half][3] = accs[half][3] + tile_sum(dh)
        for half, (d_ref, w) in enumerate(((dg_ref, wg), (dv_ref, wv))):
            for r0 in range(0, s, ch):
                dup = (d_ref[pl.ds(r0, ch), :] * w[2:3, :] + d_ref[pl.ds(r0 + 1, ch), :] * w[1:2, :]
                       + d_ref[pl.ds(r0 + 2, ch), :] * w[0:1, :])
                dup_ref[half, pl.ds(r0, ch), :] = dup.astype(BF16)
            rid = lax.broadcasted_iota(jnp.int32, (SUBLANES, STRIP), 0)
            out = zero
            for k in range(4):
                out = jnp.where(rid == k, jnp.sum(accs[half][k], axis=0, keepdims=True), out)
            dcw_ref[half] = out

    strip = lambda off: pl.BlockSpec((s, STRIP), lambda j: (0, j + off))
    wsp = lambda off: pl.BlockSpec((3, STRIP), lambda j: (0, j + off))
    bsp = lambda off: pl.BlockSpec((1, STRIP), lambda j: (0, j + off))
    return _pcall(body, name="conv_act_bwd", grid=(N_STRIPS,),
                  in_specs=[strip(0), strip(N_STRIPS), strip(0), wsp(0), wsp(N_STRIPS), bsp(0), bsp(N_STRIPS)],
                  out_specs=[pl.BlockSpec((2, s, STRIP), lambda j: (0, 0, j)), pl.BlockSpec((2, SUBLANES, STRIP), lambda j: (0, 0, j))],
                  out_shape=[_sds((2, s, D_FF), BF16), _sds((2, SUBLANES, D_FF))],
                  scratch_shapes=[pltpu.VMEM((s + SUBLANES, STRIP), F32)] * 4,
                  dims=("parallel",))(up, up, dact, conv_w, conv_w, conv_b, conv_b)


def _mix_bwd(dy, dh2, x1, g_ffn, w_out, y, att, w_glu, b_glu, g_att, g_ssm):
    s = dy.shape[0]
    tm = _row_tile(s)

    def body(dy_ref, dh2_ref, x1_ref, gf_ref, wo_ref, y_ref, att_ref, wg_ref, bg_ref, ga_ref, gs_ref,
             dx1_ref, dx1b_ref, datt_ref, dys_ref, dwg_ref, dgf_ref, dga_ref, dgs_ref, dbg_ref):
        i = pl.program_id(0)

        @pl.when(i == 0)
        def _():
            for r in (dwg_ref, dgf_ref, dga_ref, dgs_ref, dbg_ref):
                r[...] = jnp.zeros_like(r)

        dxn, dgf = _rms_bwd(x1_ref[...], gf_ref[...], dh2_ref[...])
        dx1 = dy_ref[...] + dxn
        dx1_ref[...] = dx1
        dx1b = dx1.astype(BF16)
        dx1b_ref[...] = dx1b
        dgf_ref[...] += dgf
        dma = _dot_nt(dx1b, wo_ref[0:ATTN_W, :])
        dms = _dot_nt(dx1b, wo_ref[ATTN_W:D_MODEL, :])
        datt, dga = _rms_bwd(_merge_heads(att_ref), ga_ref[...], dma)
        _split_heads(datt_ref, datt)
        dga_ref[...] += dga
        yv = y_ref[...]
        ge, sg = _ssm_glu(yv, wg_ref[...], bg_ref[...])
        dssm, dgs = _rms_bwd(ge * sg, gs_ref[...], dms)
        dgs_ref[...] += dgs
        dgl = dssm * ge * sg * (1.0 - sg)
        dglb = dgl.astype(BF16)
        dge = dssm * sg + _dot_nt(dglb, wg_ref[...])
        dbg_ref[...] += jnp.sum(dgl, axis=0, keepdims=True)
        dwg_ref[...] += _dot_tn(ge.astype(BF16), dglb)
        dys_ref[...] = dge * _gelu_grad(yv)

    row = lambda w: pl.BlockSpec((tm, w), lambda i: (i, 0))
    const = lambda a, b: pl.BlockSpec((a, b), lambda i: (0, 0))
    heads = pl.BlockSpec((HEADS, tm, HEAD_DIM), lambda i: (0, i, 0))
    return _pcall(body, name="mix_bwd", grid=(s // tm,),
                  in_specs=[row(D_MODEL), row(D_MODEL), row(D_MODEL), const(1, D_MODEL), const(D_MODEL, D_MODEL), row(SSM_W),
                            heads, const(SSM_W, SSM_W), const(1, SSM_W), const(1, ATTN_W), const(1, SSM_W)],
                  out_specs=[row(D_MODEL), row(D_MODEL), heads, row(SSM_W), const(SSM_W, SSM_W), const(1, D_MODEL),
                             const(1, ATTN_W), const(1, SSM_W), const(1, SSM_W)],
                  out_shape=[_sds((s, D_MODEL)), _sds((s, D_MODEL), BF16), _sds((HEADS, s, HEAD_DIM)), _sds((s, SSM_W)),
                             _sds((SSM_W, SSM_W)), _sds((1, D_MODEL)), _sds((1, ATTN_W)), _sds((1, SSM_W)), _sds((1, SSM_W))],
                  dims=("arbitrary",))(dy, dh2, x1, g_ffn, w_out, y, att, w_glu, b_glu, g_att, g_ssm)


def _ssm_bwd(dys, uf, ub, xr, xi, bbr, bbi, ar, ai, ccr, cci, dsk, hosted=None):
    s = dys.shape[0]
    tm = _row_tile(s)
    nb = s // tm
    nt = tm // SUBLANES

    def body(dy_ref, u_ref, ub_ref, xr_ref, xi_ref, xrp_ref, xip_ref, bbr_ref, bbi_ref, ar_ref, ai_ref, ccr_ref,
             cci_ref, dsk_ref, du_ref, dbbr_ref, dbbi_ref, dccr_ref, dcci_ref, dar_ref, dai_ref, dd_ref,
             gr_s, gi_s, cr_s, ci_s, accr_s, acci_s):
        i = pl.program_id(1)
        first_block = i == nb - 1

        @pl.when(i == 0)
        def _():
            for r in (cr_s, ci_s, accr_s, acci_s, dbbr_ref, dbbi_ref, dccr_ref, dcci_ref, dd_ref):
                r[...] = jnp.zeros_like(r)

        dy = dy_ref[...]
        dyb = dy.astype(BF16)
        gr_s[...] = _dot_nt(dyb, ccr_ref[0])
        gi_s[...] = -_dot_nt(dyb, cci_ref[0])
        consts = _scan_consts(ar_ref[0], -ai_ref[0], CHUNK_S, True)
        row = lax.broadcasted_iota(jnp.int32, (SUBLANES, CHUNK_S), 0)

        def tile(kk, carry):
            cr, ci, accr, acci = carry
            k = nt - 1 - kk
            sl = pl.ds(pl.multiple_of(k * SUBLANES, SUBLANES), SUBLANES)
            gr, gi = _scan_tile(gr_s[sl, :], gi_s[sl, :], cr, ci, consts, True)
            gr_s[sl, :] = gr
            gi_s[sl, :] = gi
            slp = pl.ds(pl.multiple_of(jnp.maximum(k - 1, 0) * SUBLANES, SUBLANES), SUBLANES)
            inner = k > 0
            pr_t = jnp.where(inner, xr_ref[slp, :], xrp_ref[...])
            pi_t = jnp.where(inner, xi_ref[slp, :], xip_ref[...])
            live = jnp.logical_or(inner, jnp.logical_not(first_block))
            top_r = jnp.where(live, pltpu.roll(pr_t, 1, 0), 0.0)
            top_i = jnp.where(live, pltpu.roll(pi_t, 1, 0), 0.0)
            xpr = jnp.where(row == 0, top_r, pltpu.roll(xr_ref[sl, :], 1, 0))
            xpi = jnp.where(row == 0, top_i, pltpu.roll(xi_ref[sl, :], 1, 0))
            accr = accr + gr * xpr + gi * xpi
            acci = acci + gi * xpr - gr * xpi
            return gr[0:1, :], gi[0:1, :], accr, acci

        zeros = jnp.zeros((SUBLANES, CHUNK_S), F32)
        cr, ci, accr, acci = lax.fori_loop(0, nt, tile, (cr_s[...], ci_s[...], zeros, zeros))
        cr_s[...] = cr
        ci_s[...] = ci
        accr_s[...] += accr
        acci_s[...] += acci
        grb = gr_s[...].astype(BF16)
        gib = gi_s[...].astype(BF16)
        u_b = ub_ref[...]
        du_ref[...] = _dot_nt(grb, bbr_ref[0]) + _dot_nt(gib, bbi_ref[0]) + dsk_ref[...] * dy
        dbbr_ref[0] += _dot_tn(u_b, grb)
        dbbi_ref[0] += _dot_tn(u_b, gib)
        dccr_ref[0] += _dot_tn(xr_ref[...].astype(BF16), dyb)
        dcci_ref[0] -= _dot_tn(xi_ref[...].astype(BF16), dyb)
        dd_ref[...] += jnp.sum(dy * u_ref[...], axis=0, keepdims=True)

        @pl.when(i == nb - 1)
        def _():
            dar_ref[0] = jnp.sum(accr_s[...], axis=0, keepdims=True)
            dai_ref[0] = jnp.sum(acci_s[...], axis=0, keepdims=True)

    tiles_per_block = tm // SUBLANES
    rb = lambda i: nb - 1 - i
    wspec = lambda a, b: pl.BlockSpec((1, a, b), lambda j, i: (j, 0, 0))
    xblk = pl.BlockSpec((tm, CHUNK_S), lambda j, i: (rb(i), j))
    xprev = pl.BlockSpec((SUBLANES, CHUNK_S), lambda j, i: (jnp.maximum(rb(i) * tiles_per_block - 1, 0), j))
    ublk = pl.BlockSpec((tm, CHUNK_U), lambda j, i: (rb(i), j))
    first = lambda: jnp.logical_and(pl.program_id(0) == 0, pl.program_id(1) == 0)
    last = lambda: jnp.logical_and(pl.program_id(0) == SSM_CHUNKS - 1, pl.program_id(1) == nb - 1)
    return _host_pcall(
        body, hosted, first, last, n_in=14, n_out=8, n_scratch=6, name="ssm_bwd", grid=(SSM_CHUNKS, nb),
        in_specs=[ublk, ublk, ublk, xblk, xblk, xprev, xprev,
                  wspec(CHUNK_U, CHUNK_S), wspec(CHUNK_U, CHUNK_S), wspec(1, CHUNK_S), wspec(1, CHUNK_S),
                  wspec(CHUNK_S, CHUNK_U), wspec(CHUNK_S, CHUNK_U), pl.BlockSpec((1, CHUNK_U), lambda j, i: (0, j))],
        out_specs=[ublk, wspec(CHUNK_U, CHUNK_S), wspec(CHUNK_U, CHUNK_S), wspec(CHUNK_S, CHUNK_U),
                   wspec(CHUNK_S, CHUNK_U), wspec(1, CHUNK_S), wspec(1, CHUNK_S),
                   pl.BlockSpec((1, CHUNK_U), lambda j, i: (0, j))],
        out_shape=[_sds((s, SSM_W)), _sds((SSM_CHUNKS, CHUNK_U, CHUNK_S)), _sds((SSM_CHUNKS, CHUNK_U, CHUNK_S)),
                   _sds((SSM_CHUNKS, CHUNK_S, CHUNK_U)), _sds((SSM_CHUNKS, CHUNK_S, CHUNK_U)),
                   _sds((SSM_CHUNKS, 1, CHUNK_S)), _sds((SSM_CHUNKS, 1, CHUNK_S)), _sds((1, SSM_W))],
        scratch_shapes=[pltpu.VMEM((tm, CHUNK_S), F32)] * 2 + [pltpu.VMEM((1, CHUNK_S), F32)] * 2
                       + [pltpu.VMEM((SUBLANES, CHUNK_S), F32)] * 2,
        dims=("parallel", "arbitrary"), operands=(dys, uf, ub, xr, xi, xr, xi, bbr, bbi, ar, ai, ccr, cci, dsk))


def _attn_probs(q, ks, cs, lse, scale, diagonal):
    p = jnp.exp(_dot_nt(q, ks) * scale - cs - lse)
    if diagonal:
        tq, tk = p.shape
        causal = lax.broadcasted_iota(jnp.int32, (tq, tk), 1) <= lax.broadcasted_iota(jnp.int32, (tq, tk), 0)
        p = jnp.where(causal, p, 0.0)
    return p


def _attn_bwd(qh, kh, vh, crow, lse, doh, hosted=None):
    _, s, _ = qh.shape
    tq = _row_tile(s)
    nq = s // tq
    scale = HEAD_DIM ** -0.5
    hp = HEADS_PER_STEP

    def body(q_ref, k_ref, v_ref, c_ref, lse_ref, do_ref, dq_ref, dk_ref, dv_ref, dc_ref, p_s, dp_s):
        i = pl.program_id(1)

        @pl.when(i == 0)
        def _():
            for r in (dk_ref, dv_ref, dc_ref):
                r[...] = jnp.zeros_like(r)

        dobs = [do_ref[hh].astype(BF16) for hh in range(hp)]

        def first(j, dls, diagonal):
            off = pl.multiple_of(j * tq, tq)
            out = []
            for hh in range(hp):
                p = _attn_probs(q_ref[hh], k_ref[hh, pl.ds(off, tq), :], c_ref[hh, :, pl.ds(off, tq)], lse_ref[hh],
                                scale, diagonal)
                dp = _dot_nt(dobs[hh], v_ref[hh, pl.ds(off, tq), :])
                p_s[hh, j] = p
                dp_s[hh, j] = dp
                out.append(dls[hh] + jnp.sum(p * dp, axis=-1, keepdims=True))
            return tuple(out)

        zero_col = jnp.zeros((tq, 1), F32)
        dls = lax.fori_loop(0, i, lambda j, c: first(j, c, False), (zero_col,) * hp)
        dls = first(i, dls, True)

        def second(j, dqs):
            rows = pl.ds(pl.multiple_of(j * tq, tq), tq)
            out = []
            for hh in range(hp):
                p = p_s[hh, j]
                ds = p * (dp_s[hh, j] - dls[hh])
                dsb = ds.astype(BF16)
                dv_ref[hh, rows, :] += _dot_tn(p.astype(BF16), dobs[hh])
                dk_ref[hh, rows, :] += _dot_tn(dsb, q_ref[hh]) * scale
                dc_ref[hh, :, rows] -= jnp.sum(ds, axis=0, keepdims=True)
                out.append(dqs[hh] + _dot(dsb, k_ref[hh, rows, :]))
            return tuple(out)

        dqs = lax.fori_loop(0, i + 1, second, (jnp.zeros((tq, HEAD_DIM), F32),) * hp)
        for hh in range(hp):
            dq_ref[hh] = dqs[hh] * scale

    blk = pl.BlockSpec((hp, tq, HEAD_DIM), lambda h, i: (h, i, 0))
    full = pl.BlockSpec((hp, s, HEAD_DIM), lambda h, i: (h, 0, 0))
    crow_spec = pl.BlockSpec((hp, 1, s), lambda h, i: (h, 0, 0))
    nh = HEADS // hp
    first = lambda: jnp.logical_and(pl.program_id(0) == 0, pl.program_id(1) == 0)
    last = lambda: jnp.logical_and(pl.program_id(0) == nh - 1, pl.program_id(1) == nq - 1)
    return _host_pcall(body, hosted, first, last, n_in=6, n_out=4, n_scratch=2, name="attn_bwd", grid=(nh, nq),
                       in_specs=[blk, full, full, crow_spec, pl.BlockSpec((hp, tq, 1), lambda h, i: (h, i, 0)), blk],
                       out_specs=[blk, full, full, crow_spec],
                       out_shape=[_sds((HEADS, s, HEAD_DIM))] * 3 + [_sds((HEADS, 1, s))],
                       scratch_shapes=[pltpu.VMEM((hp, nq, tq, tq), F32)] * 2,
                       dims=("parallel", "arbitrary"), operands=(qh, kh, vh, crow, lse, doh))


def _prep_bwd(z, dqn, dkn, dv, du, dc, gq, gk, bf, gg):
    s = z.shape[0]
    tm = _row_tile(s)
    nb = s // tm

    def body(z_ref, dqn_ref, dkn_ref, dv_ref, du_ref, dc_ref, gq_ref, gk_ref, bf_ref, gg_ref,
             dz_ref, dgq_ref, dgk_ref, dbf_ref, carry_ref):
        i = pl.program_id(0)

        @pl.when(i == 0)
        def _():
            for r in (dgq_ref, dgk_ref, dbf_ref, carry_ref):
                r[...] = jnp.zeros_like(r)

        gg_m = gg_ref[...]

        def head_norm_bwd(t, g, dn):
            r = lax.rsqrt(_dot_exact_r(t * t, gg_m) * (1.0 / HEAD_DIM) + EPS)
            w = dn * g
            mean_wt = _dot_exact_r(w * t, gg_m) * (1.0 / HEAD_DIM)
            return r * w - t * (r * r * r) * mean_wt, jnp.sum(dn * t * r, axis=0, keepdims=True)

        dq, dgq = head_norm_bwd(z_ref[:, 0:ATTN_W], gq_ref[...], _merge_heads(dqn_ref))
        dk, dgk = head_norm_bwd(z_ref[:, ATTN_W:2 * ATTN_W], gk_ref[...], _merge_heads(dkn_ref))
        dgq_ref[...] += dgq
        dgk_ref[...] += dgk
        row = lax.broadcasted_iota(jnp.int32, (tm, tm), 0)
        col = lax.broadcasted_iota(jnp.int32, (tm, tm), 1)
        triu = (col >= row).astype(BF16)
        dlf = _dot_exact_l(triu, dc_ref[...]) + carry_ref[...]
        carry_ref[...] = dlf[0:1, :]
        df = dlf * _sigmoid(-_forget_logits(z_ref, bf_ref))
        dbf_ref[...] += jnp.sum(df, axis=0, keepdims=True)
        dz_ref[:, 0:ATTN_W] = dq.astype(BF16)
        dz_ref[:, ATTN_W:2 * ATTN_W] = dk.astype(BF16)
        dz_ref[:, 2 * ATTN_W:3 * ATTN_W] = _merge_heads(dv_ref).astype(BF16)
        tail = jnp.concatenate([df[:, :HEADS], du_ref[...], jnp.zeros((tm, Z_COLS - IN_COLS), F32)], axis=-1)
        dz_ref[:, F_COL0:Z_COLS] = tail.astype(BF16)

    row_spec = lambda w: pl.BlockSpec((tm, w), lambda i: (nb - 1 - i, 0))
    const = lambda shape: pl.BlockSpec(shape, lambda i: (0, 0))
    return _pcall(body, name="prep_bwd", grid=(nb,),
                  in_specs=[row_spec(Z_COLS)] + [pl.BlockSpec((HEADS, tm, HEAD_DIM), lambda i: (0, nb - 1 - i, 0))] * 3
                           + [row_spec(ATTN_W), row_spec(LANES), const((1, ATTN_W)),
                              const((1, ATTN_W)), const((1, LANES)), const((ATTN_W, ATTN_W))],
                  out_specs=[row_spec(Z_COLS), const((1, ATTN_W)), const((1, ATTN_W)), const((1, LANES))],
                  out_shape=[_sds((s, Z_COLS), BF16), _sds((1, ATTN_W)), _sds((1, ATTN_W)), _sds((1, LANES))],
                  scratch_shapes=[pltpu.VMEM((1, LANES), F32)], dims=("arbitrary",))(z, dqn, dkn, dv, du, dc, gq, gk, bf, gg)


def _in_norm_bwd(x, g_mix, dh, dx1, hosted=None):
    s = x.shape[0]
    tm = _row_tile(s)

    def body(x_ref, g_ref, dh_ref, dx1_ref, dx_ref, dg_ref):
        i = pl.program_id(0)

        @pl.when(i == 0)
        def _():
            dg_ref[...] = jnp.zeros_like(dg_ref)

        dxn, dg = _rms_bwd(x_ref[...], g_ref[...], dh_ref[...])
        dx_ref[...] = dx1_ref[...] + dxn
        dg_ref[...] += dg

    row = pl.BlockSpec((tm, D_MODEL), lambda i: (i, 0))
    vec = pl.BlockSpec((1, D_MODEL), lambda i: (0, 0))
    nb = s // tm
    return _host_pcall(body, hosted, lambda: pl.program_id(0) == 0, lambda: pl.program_id(0) == nb - 1,
                       n_in=4, n_out=2, n_scratch=0, name="in_norm_bwd", grid=(nb,), in_specs=[row, vec, row, row],
                       out_specs=[row, vec], out_shape=[_sds((s, D_MODEL)), _sds((1, D_MODEL))], scratch_shapes=[],
                       dims=("arbitrary",), operands=(x, g_mix, dh, dx1))


def _adamw_refs(w_ref, g_ref, m_ref, v_ref, d_ref, mo_ref, vo_ref):
    gv = g_ref[...]
    mn = ADAM_B1 * m_ref[...] + (1.0 - ADAM_B1) * gv
    vn = ADAM_B2 * v_ref[...] + (1.0 - ADAM_B2) * (gv * gv)
    m_hat = mn / (1.0 - ADAM_B1 ** ADAM_STEP)
    v_hat = vn / (1.0 - ADAM_B2 ** ADAM_STEP)
    d_ref[...] = -ADAM_LR * (m_hat / (jnp.sqrt(v_hat) + ADAM_EPS) + ADAM_WD * w_ref[...])
    mo_ref[...] = mn
    vo_ref[...] = vn


def _adamw_small(ws, gs, ms, vs):
    n = len(ws)

    def body(*refs):
        ins, outs = refs[:4 * n], refs[4 * n:]
        for i in range(n):
            _adamw_refs(ins[i], ins[n + i], ins[2 * n + i], ins[3 * n + i], *outs[3 * i:3 * i + 3])

    vm = pl.BlockSpec(memory_space=pltpu.VMEM)
    out_shape = [_sds(w.shape) for w in ws for _ in range(3)]
    return _pallas(body, name="adamw_small", in_specs=[vm] * (4 * n), out_specs=[vm] * (3 * n), out_shape=out_shape,
                   compiler_params=pltpu.CompilerParams(vmem_limit_bytes=VMEM_LIMIT))(*ws, *gs, *ms, *vs)


def _adamw(w, g, m, v, *, name):
    r, c = w.shape
    tr = r
    for cand in (256, 176, 128, 64):
        if r > cand and r % cand == 0:
            tr = cand
            break

    def body(w_ref, g_ref, m_ref, v_ref, d_ref, mo_ref, vo_ref):
        _adamw_refs(w_ref, g_ref, m_ref, v_ref, d_ref, mo_ref, vo_ref)

    spec = pl.BlockSpec((tr, c), lambda i: (i, 0))
    return _pcall(body, name=name, grid=(r // tr,), in_specs=[spec] * 4, out_specs=[spec] * 3,
                  out_shape=[_sds((r, c))] * 3, dims=("parallel",))(w, g, m, v)


def _prefetch_call(body, *, name, grid, in_specs, out_specs, out_shape, operands):
    grid_spec = pltpu.PrefetchScalarGridSpec(num_scalar_prefetch=1, grid=grid, in_specs=in_specs, out_specs=out_specs)
    params = pltpu.CompilerParams(dimension_semantics=("parallel",) * len(grid), vmem_limit_bytes=VMEM_LIMIT)
    return _pallas(body, name=name, grid_spec=grid_spec, out_shape=out_shape, compiler_params=params)(*operands)


def _place_cols(buf, shard, place):
    rows, cols = shard.shape
    tr = 256

    def body(place_ref, s_ref, b_ref, o_ref):
        o_ref[...] = s_ref[...]

    grid_spec = pltpu.PrefetchScalarGridSpec(
        num_scalar_prefetch=1, grid=(rows // tr,),
        in_specs=[pl.BlockSpec((tr, cols), lambda i, p: (i, 0)), pl.BlockSpec(memory_space=pltpu.HBM)],
        out_specs=pl.BlockSpec((tr, cols), lambda i, p: (i, p[0])))
    return _pallas(body, name="place_own_cols", grid_spec=grid_spec, out_shape=_sds(buf.shape, buf.dtype),
                   input_output_aliases={2: 0},
                   compiler_params=pltpu.CompilerParams(dimension_semantics=("parallel",),
                                                        vmem_limit_bytes=VMEM_LIMIT))(place, shard, buf)


def _half_rows_tile(hr):
    return hr if hr <= 256 else 176 if hr % 176 == 0 else 256


def _add_half(g, landed, place, *, name):
    def body(place_ref, g_ref, l_ref, o_ref):
        own = g_ref[0] if len(g_ref.shape) == 4 else g_ref[...]
        o_ref[...] = (own + l_ref[...]).astype(BF16)

    if g.ndim == 4:
        _, _, hr, c = g.shape
        tr = _half_rows_tile(hr)
        blk = (1, tr, c)
        return _prefetch_call(
            body, name=name, grid=(N_CHIPS, hr // tr),
            in_specs=[pl.BlockSpec((1,) + blk, lambda j, i, p: (j, p[1], i, 0)), pl.BlockSpec(blk, lambda j, i, p: (j, i, 0))],
            out_specs=pl.BlockSpec(blk, lambda j, i, p: (j, i, 0)), out_shape=_sds(landed.shape, BF16),
            operands=(place, g, landed))
    hr, c = landed.shape
    tr, tc = 256, _tile(c, 2176)
    nb = hr // tr
    return _prefetch_call(
        body, name=name, grid=(nb, c // tc),
        in_specs=[pl.BlockSpec((tr, tc), lambda i, j, p: (p[1] * nb + i, j)), pl.BlockSpec((tr, tc), lambda i, j, p: (i, j))],
        out_specs=pl.BlockSpec((tr, tc), lambda i, j, p: (i, j)), out_shape=_sds(landed.shape, BF16),
        operands=(place, g, landed))


def _sum_chips(chip_sum, lands, place, *, name, tc, window_stride=0):
    _, hr, c = lands.shape
    tr = _half_rows_tile(hr)
    nb = hr // tr
    ncb = c // tc

    def body(place_ref, own_ref, a_ref, b_ref, c_ref, o_ref):
        own = own_ref[0] if len(own_ref.shape) == 3 else own_ref[...]
        o_ref[...] = ((own.astype(F32) + a_ref[0].astype(F32)) + b_ref[0].astype(F32)) + c_ref[0].astype(F32)

    land = lambda k: pl.BlockSpec((1, tr, tc), lambda i, j, p: ((p[0] + k) % N_CHIPS, i, j))
    if chip_sum.ndim == 3:
        own_spec = land(0)
    else:
        stride = window_stride // tc
        own_spec = pl.BlockSpec((tr, tc), lambda i, j, p: (i, p[0] * stride + j))
    return _prefetch_call(
        body, name=name, grid=(nb, ncb), in_specs=[own_spec, land(1), land(2), land(3)],
        out_specs=pl.BlockSpec((tr, tc), lambda i, j, p: (p[1] * nb + i, j)), out_shape=_sds((2 * hr, c)),
        operands=(place, chip_sum, lands, lands, lands))


_HBM = pl.BlockSpec(memory_space=pltpu.HBM)


def _place():
    x, y, c = lax.axis_index("x"), lax.axis_index("y"), lax.axis_index("c")
    chips = [(1 - x, y), (x, 1 - y), (1 - x, 1 - y)]
    return x, y, c, chips


def _rcopy(src, dst, send_sem, recv_sem, to):
    return pltpu.make_async_remote_copy(src_ref=src, dst_ref=dst, send_sem=send_sem, recv_sem=recv_sem,
                                        device_id=to, device_id_type=MESH)


UP_COLS = 2 * D_FF // N_CHIPS
IN_WINDOW = 640
IN_STRIDE = 512


class _Hosted:
    def __init__(self, operands, out_shapes, n_sems, start, finish, aliases=None, local_sems=0):
        self.operands, self.out_shapes, self.n_sems = list(operands), list(out_shapes), n_sems
        self.start, self.finish, self.aliases, self.local_sems = start, finish, dict(aliases or {}), local_sems

    def scratch(self):
        return ([pltpu.SemaphoreType.DMA((self.n_sems,)), pltpu.SemaphoreType.DMA((self.n_sems,))]
                + [pltpu.SemaphoreType.DMA] * self.local_sems)


def _both(a, b):
    na, nao, nas = len(a.operands), len(a.out_shapes), len(a.scratch())

    def start(ins, outs, sems):
        a.start(ins[:na], outs[:nao], sems[:nas])
        b.start(ins[na:], outs[nao:], sems[nas:])

    def finish(ins, outs, sems):
        a.finish(ins[:na], outs[:nao], sems[:nas])
        b.finish(ins[na:], outs[nao:], sems[nas:])

    both = _Hosted(a.operands + b.operands, a.out_shapes + b.out_shapes, 0, start, finish,
                   aliases={**a.aliases, **{na + i: nao + o for i, o in b.aliases.items()}})
    both.scratch = lambda: a.scratch() + b.scratch()
    return both


def _then(a, b):
    nas = len(a.scratch())

    def finish(ins, outs, sems):
        a.finish(ins, outs, sems[:nas])
        b.start(ins, outs, sems[nas:])
        b.finish(ins, outs, sems[nas:])

    chain = _Hosted(a.operands, a.out_shapes, 0, lambda ins, outs, sems: a.start(ins, outs, sems[:nas]), finish,
                    aliases=a.aliases)
    chain.scratch = lambda: a.scratch() + b.scratch()
    return chain


def _run_hosted(hosted, *, name):
    n_in, n_out = len(hosted.operands), len(hosted.out_shapes)

    def body(*refs):
        parts = (refs[:n_in], refs[n_in:n_in + n_out], refs[n_in + n_out:])
        hosted.start(*parts)
        hosted.finish(*parts)

    return _pallas(body, name=name, in_specs=[_HBM] * n_in, out_specs=[_HBM] * n_out, out_shape=hosted.out_shapes,
                   input_output_aliases=hosted.aliases, scratch_shapes=hosted.scratch())(*hosted.operands)


def _host_pcall(core_body, hosted, first, last, *, n_in, n_out, n_scratch, name, grid, in_specs, out_specs, out_shape,
                scratch_shapes, dims, operands):
    if hosted is None:
        outs = _pcall(core_body, name=name, grid=grid, in_specs=in_specs, out_specs=out_specs, out_shape=out_shape,
                      scratch_shapes=scratch_shapes, dims=dims)(*operands)
        return outs, []
    hi, ho = len(hosted.operands), len(hosted.out_shapes)

    def body(*refs):
        a, b = n_in, n_in + hi
        c, d = b + n_out, b + n_out + ho
        e = d + n_scratch
        parts = (refs[a:b], refs[c:d], refs[e:])

        @pl.when(first())
        def _():
            hosted.start(*parts)

        core_body(*refs[:a], *refs[b:c], *refs[d:e])

        @pl.when(last())
        def _():
            hosted.finish(*parts)

    params = pltpu.CompilerParams(dimension_semantics=("arbitrary",) * len(grid), vmem_limit_bytes=VMEM_LIMIT)
    outs = _pallas(body, name=name, grid=grid, in_specs=list(in_specs) + [_HBM] * hi, out_specs=list(out_specs) + [_HBM] * ho,
                   out_shape=list(out_shape) + hosted.out_shapes, scratch_shapes=list(scratch_shapes) + hosted.scratch(),
                   input_output_aliases={n_in + a: n_out + b for a, b in hosted.aliases.items()},
                   compiler_params=params)(*operands, *hosted.operands)
    return outs[:n_out], outs[n_out:]


WHOLE_HALF = (0, 1, 1)


def _band_rows(src, hc, band):
    first, count, of = band
    hr = src.shape[0] // 2
    return pl.ds(hc * hr + first * (hr // of), count * (hr // of))


def _gather_slot(src, out, chip, hc, band=WHOLE_HALF):
    cols = src.shape[1]
    if len(out.shape) == 2:
        return out.at[_band_rows(src, hc, band), pl.ds(pl.multiple_of(chip * cols, LANES), cols)]
    return out.at[chip, _band_rows(src, hc, band), :]


def _gathered_shape(shard, by_cols):
    if by_cols:
        return _sds((shard.shape[0], N_CHIPS * shard.shape[1]), shard.dtype)
    return _sds((N_CHIPS,) + shard.shape, shard.dtype)


def _plan_gather_ici(shards, by_cols, whole=(), bands=None, into=None):
    n = len(shards)
    bands = bands or [WHOLE_HALF] * n
    into = into or [None] * n
    given = [w for w in range(n) if into[w] is not None]
    n_ops = n + len(whole)

    def copies(ins, outs, sems):
        send_sems, recv_sems = sems[0], sems[1]
        x, y, c, chips = _place()
        me = 2 * x + y
        sends, waits = [], []
        for w in range(n + len(whole)):
            for k, (cx, cy) in enumerate(chips):
                sem = (send_sems.at[3 * w + k], recv_sems.at[3 * w + k])
                if w < n:
                    sends.append(_rcopy(ins[w].at[_band_rows(ins[w], c, bands[w]), :],
                                        _gather_slot(ins[w], outs[w], me, c, bands[w]), *sem, (cx, cy, c)))
                    landed = _gather_slot(ins[w], outs[w], 2 * cx + cy, c, bands[w])
                else:
                    sends.append(_rcopy(ins[w], outs[w].at[me], *sem, (cx, cy, c)))
                    landed = outs[w].at[2 * cx + cy]
                waits.append(_rcopy(landed, landed, *sem, (cx, cy, c)))
        return sends, waits

    def start(ins, outs, sems):
        for cp in copies(ins, outs, sems)[0]:
            cp.start()

    def finish(ins, outs, sems):
        sends, waits = copies(ins, outs, sems)
        for cp in waits:
            cp.wait_recv()
        for cp in sends:
            cp.wait_send()

    out_shapes = [_gathered_shape(s, bc) for s, bc in zip(shards, by_cols)] + [_sds((N_CHIPS,) + a.shape, a.dtype) for a in whole]
    return _Hosted(list(shards) + list(whole) + [into[w] for w in given], out_shapes, 3 * n_ops, start, finish,
                   aliases={n_ops + i: w for i, w in enumerate(given)})


def _plan_gather_d2d(bufs, shard_shapes, bands=None):
    n = len(bufs)
    bands = bands or [WHOLE_HALF] * n

    def copies(ins, outs, sems):
        send_sems, recv_sems = sems
        x, y, c, chips = _place()
        sibling = (x, y, 1 - c)
        sends, waits = [], []
        for w in range(n):
            for k, (cx, cy) in enumerate(chips):
                sem = (send_sems.at[3 * w + k], recv_sems.at[3 * w + k])
                landed = _gather_slot(shard_shapes[w], outs[w], 2 * cx + cy, c, bands[w])
                other = _gather_slot(shard_shapes[w], outs[w], 2 * cx + cy, 1 - c, bands[w])
                sends.append(_rcopy(landed, landed, *sem, sibling))
                waits.append(_rcopy(other, other, *sem, sibling))
        return sends, waits

    def start(ins, outs, sems):
        for cp in copies(ins, outs, sems)[0]:
            cp.start()

    def finish(ins, outs, sems):
        sends, waits = copies(ins, outs, sems)
        for cp in waits:
            cp.wait_recv()
        for cp in sends:
            cp.wait_send()

    return _Hosted(bufs, [_sds(b.shape, b.dtype) for b in bufs], 3 * n, start, finish, aliases={w: w for w in range(n)})


def _plan_swap(grads):
    def copies(ins, outs, sems):
        send_sems, recv_sems = sems
        x, y, c, _ = _place()
        cps = []
        for w, g_ref in enumerate(ins):
            if len(g_ref.shape) == 4:
                theirs = g_ref.at[:, 1 - c]
            else:
                hr = g_ref.shape[0] // 2
                theirs = g_ref.at[pl.ds((1 - c) * hr, hr), :]
            cps.append(_rcopy(theirs, outs[w], send_sems.at[w], recv_sems.at[w], (x, y, 1 - c)))
        return cps

    def start(ins, outs, sems):
        for cp in copies(ins, outs, sems):
            cp.start()

    def finish(ins, outs, sems):
        for cp in copies(ins, outs, sems):
            cp.wait()

    out_shapes = [_sds((g.shape[0], g.shape[2], g.shape[3])) if g.ndim == 4 else _sds((g.shape[0] // 2, g.shape[1]))
                  for g in grads]
    return _Hosted(grads, out_shapes, len(grads), start, finish)


def _plan_scatter(chip_sums, windows):
    def copies(ins, outs, sems):
        send_sems, recv_sems = sems
        x, y, c, chips = _place()
        me = 2 * x + y
        sends, waits = [], []
        for w, s_ref in enumerate(ins):
            for k, (cx, cy) in enumerate(chips):
                tgt = 2 * cx + cy
                if windows[w] is not None:
                    stride, width = windows[w]
                    part = s_ref.at[:, pl.ds(pl.multiple_of(tgt * stride, LANES), width)]
                else:
                    part = s_ref.at[tgt]
                sem = (send_sems.at[3 * w + k], recv_sems.at[3 * w + k])
                sends.append(_rcopy(part, outs[w].at[me], *sem, (cx, cy, c)))
                slot = outs[w].at[tgt]
                waits.append(_rcopy(slot, slot, *sem, (cx, cy, c)))
        return sends, waits

    def start(ins, outs, sems):
        for cp in copies(ins, outs, sems)[0]:
            cp.start()

    def finish(ins, outs, sems):
        sends, waits = copies(ins, outs, sems)
        for cp in waits:
            cp.wait_recv()
        for cp in sends:
            cp.wait_send()

    out_shapes = [_sds((N_CHIPS, s.shape[0], win[1]), BF16) if win is not None else _sds(s.shape, BF16)
                  for s, win in zip(chip_sums, windows)]
    return _Hosted(chip_sums, out_shapes, 3 * len(chip_sums), start, finish)


def _plan_join(reds):
    def copies(ins, outs, sems):
        send_sems, recv_sems = sems
        x, y, c, _ = _place()
        sends, waits = [], []
        for w, out in enumerate(outs):
            hr = out.shape[0] // 2
            mine = out.at[pl.ds(c * hr, hr), :]
            theirs = out.at[pl.ds((1 - c) * hr, hr), :]
            sends.append(_rcopy(mine, mine, send_sems.at[w], recv_sems.at[w], (x, y, 1 - c)))
            waits.append(_rcopy(theirs, theirs, send_sems.at[w], recv_sems.at[w], (x, y, 1 - c)))
        return sends, waits

    def start(ins, outs, sems):
        for cp in copies(ins, outs, sems)[0]:
            cp.start()

    def finish(ins, outs, sems):
        sends, waits = copies(ins, outs, sems)
        for cp in waits:
            cp.wait_recv()
        for cp in sends:
            cp.wait_send()

    return _Hosted(reds, [_sds(r.shape) for r in reds], len(reds), start, finish, aliases={w: w for w in range(len(reds))})


def _allreduce_small(v):
    m_per = v.shape[0]

    def body(v_ref, out_ref, all_ref, send_sems, recv_sems, local_sem):
        x, y, c, chips = _place()
        me, sibling = (x, y, c), (x, y, 1 - c)

        def rows(px, py, pc):
            return all_ref.at[pl.ds((4 * px + 2 * py + pc) * m_per, m_per), :]

        def copy(k, block, to, src=None):
            return _rcopy(rows(*block) if src is None else src, rows(*block), send_sems.at[k], recv_sems.at[k], to)

        mine = pltpu.make_async_copy(v_ref, rows(*me), local_sem)
        mine.start()
        first = [copy(0, me, sibling, src=v_ref)]
        first += [copy(1 + k, me, (*chip, c), src=v_ref) for k, chip in enumerate(chips)]
        for cp in first:
            cp.start()
        passed = [copy(4 + k, (*chip, c), sibling) for k, chip in enumerate(chips)]
        for k, chip in enumerate(chips):
            copy(1 + k, (*chip, c), me).wait_recv()
            passed[k].start()
        copy(0, sibling, me).wait_recv()
        for k, chip in enumerate(chips):
            copy(4 + k, (*chip, 1 - c), me).wait_recv()
        for cp in first + passed:
            cp.wait_send()
        mine.wait()
        acc = all_ref[pl.ds(0, m_per), :]
        for d in range(1, 8):
            acc = acc + all_ref[pl.ds(d * m_per, m_per), :]
        out_ref[...] = acc

    vm = pl.BlockSpec(memory_space=pltpu.VMEM)
    return _pallas(body, name="allreduce_small", in_specs=[vm], out_specs=vm, out_shape=_sds((m_per, LANES)),
                          scratch_shapes=[pltpu.VMEM((8 * m_per, LANES), F32), pltpu.SemaphoreType.DMA((7,)),
                                          pltpu.SemaphoreType.DMA((7,)), pltpu.SemaphoreType.DMA],
                          compiler_params=pltpu.CompilerParams(vmem_limit_bytes=VMEM_LIMIT))(v)


def _block_diag(blocks):
    j, g, a, b = blocks.shape
    eye = jnp.eye(g, dtype=bool)[None, :, None, :, None]
    return jnp.where(eye, blocks[:, :, :, None, :], jnp.zeros((), blocks.dtype)).reshape(j, g * a, g * b)


def _diag_blocks(m, a, b):
    j = m.shape[0]
    g = m.shape[1] // a
    t = m.reshape(j, g, a, g, b)
    eye = jnp.eye(g, dtype=bool)[None, :, None, :, None]
    return jnp.sum(jnp.where(eye, t, 0.0), axis=3)


_SMALL = (("g_mix", (1024,)), ("b_f", (8,)), ("g_q", (64,)), ("g_k", (64,)), ("lambda_re", (32, 64)),
          ("lambda_im", (32, 64)), ("log_step", (32,)), ("b_re", (32, 64, 16)), ("b_im", (32, 64, 16)),
          ("c_re", (32, 16, 64)), ("c_im", (32, 16, 64)), ("d_skip", (32, 16)), ("b_glu", (512,)),
          ("g_attn_out", (512,)), ("g_ssm_out", (512,)), ("g_ffn", (1024,)), ("conv_b", (5632,)))


def _pack_small(arrs, extra=()):
    parts = []
    for a in list(arrs) + list(extra):
        flat = a.reshape(-1)
        rows = -(-flat.shape[0] // LANES)
        parts.append(jnp.pad(flat, (0, rows * LANES - flat.shape[0])).reshape(rows, LANES))
    total = sum(p.shape[0] for p in parts)
    pad = -total % SUBLANES
    if pad:
        parts.append(jnp.zeros((pad, LANES), F32))
    return jnp.concatenate(parts, axis=0)


def _unpack_small(buf, shapes):
    out, r = [], 0
    for shape in shapes:
        n = math.prod(shape)
        rows = -(-n // LANES)
        out.append(buf[r:r + rows].reshape(-1)[:n].reshape(shape))
        r += rows
    return out


def _halves(t):
    return t.reshape(N_CHIPS, 2, t.shape[0] // (2 * N_CHIPS), t.shape[1])


class _MeshComm:
    def __init__(self, args):
        x, y, self.core = lax.axis_index("x"), lax.axis_index("y"), lax.axis_index("c")
        self.chip = 2 * x + y
        self.place = jnp.stack([self.chip, self.core]).astype(jnp.int32)
        self.shards = {n: args[n].astype(BF16) for n in ("w_in", "w_glu", "w_out", "w_up", "w_down")}
        self.conv_w = args["conv_w"]

    def _own(self, stacked, mine):
        return lax.dynamic_update_slice(stacked, mine[None], (self.chip,) + (0,) * mine.ndim)

    def w_in(self):
        sh = self.shards["w_in"]
        (buf,) = _run_hosted(_then(_plan_gather_ici([sh], [False]), _plan_gather_d2d([sh], [sh])), name="gather_w_in")
        whole = self._own(buf, sh).transpose(1, 0, 2).reshape(D_MODEL, IN_COLS)
        return jnp.pad(whole, ((0, 0), (0, Z_COLS - IN_COLS)))

    def gather_first(self):
        self.mid = [self.shards[n] for n in ("w_glu", "w_out", "w_down")]
        return _plan_gather_ici(self.mid + [self.shards["w_up"]], [False, False, False, True], whole=[self.conv_w],
                                bands=[WHOLE_HALF] * 3 + [(0, 1, 4)])

    def gather_second(self, landed):
        self.g_cw = landed[4]
        return _both(_plan_gather_d2d(list(landed[:3]), self.mid),
                     _plan_gather_ici([self.shards["w_up"]], [True], bands=[(1, 3, 4)], into=[landed[3]]))

    def weights(self, gathered):
        g_glu, g_out, g_down = gathered[:3]
        own = self._own
        return (own(g_glu, self.mid[0]).reshape(SSM_W, SSM_W), own(g_out, self.mid[1]).reshape(D_MODEL, D_MODEL),
                own(g_down, self.mid[2]).reshape(D_FF, D_MODEL),
                own(self.g_cw, self.conv_w).transpose(1, 0, 2).reshape(3, 2 * D_FF))

    def gather_third(self, gathered):
        return _plan_gather_d2d([gathered[3]], [self.shards["w_up"]])

    def w_up(self, passed):
        return _place_cols(passed[0], self.shards["w_up"], self.place)

    def swap(self, d_w_down, d_w_up, d_w_glu, d_w_out):
        self.early = [_halves(d_w_down), d_w_up, _halves(d_w_glu), _halves(d_w_out)]
        return _plan_swap(self.early)

    def scatter(self, landed):
        self.early_sums = [_add_half(g, l, self.place, name="add_" + n)
                           for g, l, n in zip(self.early, landed, ("w_down", "w_up", "w_glu", "w_out"))]
        return _plan_scatter(self.early_sums, [None, (UP_COLS, UP_COLS), None, None])

    def swap_in(self, d_w_in):
        self.d_in = d_w_in
        return _plan_swap([d_w_in])

    def scatter_in(self, landed):
        self.sum_in = _add_half(self.d_in, landed[0], self.place, name="add_w_in")
        return _plan_scatter([self.sum_in], [(IN_STRIDE, IN_WINDOW)])

    def reduce(self, lands):
        early_lands, (land_in,) = lands
        sum_in = self.sum_in
        es, el = self.early_sums, early_lands
        todo = [(sum_in, land_in, "w_in", LANES, IN_STRIDE), (es[2], el[2], "w_glu", SSM_W, 0),
                (es[3], el[3], "w_out", D_MODEL, 0), (es[1], el[1], "w_up", UP_COLS, UP_COLS),
                (es[0], el[0], "w_down", D_MODEL, 0)]
        reds = _run_hosted(_plan_join([_sum_chips(s, l, self.place, name="sum_" + n, tc=tc, window_stride=st)
                                       for s, l, n, tc, st in todo]), name="join_halves")
        g_big = dict(zip(("w_in", "w_glu", "w_out", "w_up", "w_down"), reds))
        g_big["w_in"] = lax.dynamic_slice_in_dim(reds[0], 2 * self.chip, IN_COLS // N_CHIPS, axis=1)
        return g_big


def _local_step(x, tgt, p, comm):
    s = x.shape[0]
    row = lambda v: v.reshape(1, -1)
    g_mix, g_ffn = row(p["g_mix"]), row(p["g_ffn"])
    g_att, g_ssm, b_glu, conv_b = row(p["g_attn_out"]), row(p["g_ssm_out"]), row(p["b_glu"]), row(p["conv_b"])
    gq = row(jnp.tile(p["g_q"], HEADS))
    gk = row(jnp.tile(p["g_k"], HEADS))
    bf = row(jnp.pad(p["b_f"], (0, LANES - HEADS)))
    gg = jnp.kron(jnp.eye(HEADS, dtype=F32), jnp.ones((HEAD_DIM, HEAD_DIM), F32)).astype(BF16)
    dsk = row(p["d_skip"])

    rep = lambda a: jnp.repeat(a, SSM_GROUP, axis=0)
    lr, li = rep(p["lambda_re"]), rep(p["lambda_im"])
    ls = rep(jnp.broadcast_to(p["log_step"][:, None], (SSM_GROUPS, SSM_STATE)))
    bt_re = p["b_re"].transpose(0, 2, 1).reshape(_PARAM_SHAPE)
    bt_im = p["b_im"].transpose(0, 2, 1).reshape(_PARAM_SHAPE)
    a_re_rep, a_im_rep, bb_re, bb_im = _ssm_params(lr, li, ls, bt_re, bt_im)
    ar = a_re_rep[::SSM_GROUP].reshape(SSM_CHUNKS, 1, CHUNK_S)
    ai = a_im_rep[::SSM_GROUP].reshape(SSM_CHUNKS, 1, CHUNK_S)
    chunked = lambda t: t.reshape(SSM_CHUNKS, SSM_GROUPS // SSM_CHUNKS, SSM_GROUP, SSM_STATE)
    bbr = _block_diag(chunked(bb_re)).astype(BF16)
    bbi = _block_diag(chunked(bb_im)).astype(BF16)
    to_cc = lambda c: _block_diag(chunked(c).transpose(0, 1, 3, 2)).astype(BF16)
    ccr, cci = to_cc(p["c_re"]), to_cc(p["c_im"])

    w_in_r = comm.w_in()
    hb, z = _in_proj(x, g_mix, w_in_r)
    qh, kh, vh, ub, uf, c128 = _attn_prep(z, gq, gk, bf, gg)
    crow = c128[:, :HEADS].T.reshape(HEADS, 1, s)
    (oh, lse), landed = _attn_fwd(qh, kh, vh, crow, comm.gather_first())
    (xr, xi, y), gathered = _ssm_fwd(ub, uf, bbr, bbi, ar, ai, ccr, cci, dsk, comm.gather_second(landed))
    w_glu_b, w_out_b, w_down_b, conv_w_full = comm.weights(gathered)
    (x1, mixb, h2b), passed = _mix_out(y, oh, x, w_glu_b, b_glu, g_att, g_ssm, w_out_b, g_ffn, comm.gather_third(gathered))
    w_up_b = comm.w_up(passed)
    up = _mm(h2b, w_up_b, name="ffn_up", tm=1024, tn=1408, tk=1024)
    act = _conv_act(up, conv_w_full, conv_b)
    dy, dyb, loss_blk = _down_loss(act, w_down_b, x1, tgt)

    d_w_down = _mm(act, dyb, ta=True, name="d_w_down", tm=1408, tn=1024, tk=2048)
    dact = _mm(dyb, w_down_b, tb=True, name="d_act", tm=1024, tn=1408, tk=1024)
    dupb, dcw = _conv_act_bwd(up, dact, conv_w_full, conv_b)
    d_w_up = _mm(h2b, dupb, ta=True, b_parts=2, name="d_w_up", tm=1024, tn=1408, tk=2048)
    dh2 = _mm(dupb, w_up_b, tb=True, a_parts=2, name="d_h2", tm=1024, tn=1024, tk=1408)
    dx1, dx1b, doh, dys, d_w_glu, d_g_ffn, d_g_att, d_g_ssm, d_b_glu = _mix_bwd(
        dy, dh2, x1, g_ffn, w_out_b, y, oh, w_glu_b, b_glu, g_att, g_ssm)
    d_w_out = _mm(mixb, dx1b, ta=True, name="d_w_out", tm=1024, tn=1024, tk=2048)
    (du, dbbr, dbbi, dccr, dcci, dar, dai, dd), swapped = _ssm_bwd(dys, uf, ub, xr, xi, bbr, bbi, ar, ai, ccr, cci, dsk,
                                                                comm.swap(d_w_down, d_w_up, d_w_glu, d_w_out))
    (dqh, dkh, dvh, dcrow), early_lands = _attn_bwd(qh, kh, vh, crow, lse, doh, comm.scatter(swapped))
    dc128 = jnp.pad(dcrow.reshape(HEADS, s).T, ((0, 0), (0, LANES - HEADS)))
    dzb, d_gq, d_gk, d_bf = _prep_bwd(z, dqh, dkh, dvh, du, dc128, gq, gk, bf, gg)
    d_w_in_r = _mm(hb, dzb, ta=True, name="d_w_in", tm=512, tn=Z_COLS, tk=2048)
    dh, swapped_in = _mm(dzb, w_in_r, tb=True, name="d_h", tm=1024, tn=1024, tk=Z_COLS, carry=True,
                         hosted=comm.swap_in(d_w_in_r))
    (dx, d_g_mix), land_in = _in_norm_bwd(x, g_mix, dh, dx1, comm.scatter_in(swapped_in))

    unchunk = lambda t: t.reshape(_PARAM_SHAPE)
    dbb_re = unchunk(_diag_blocks(dbbr, SSM_GROUP, SSM_STATE))
    dbb_im = unchunk(_diag_blocks(dbbi, SSM_GROUP, SSM_STATE))
    first_row = (jnp.arange(_PARAM_SHAPE[0]) % SSM_GROUP == 0)[:, None]
    da_re = jnp.where(first_row, rep(dar.reshape(SSM_GROUPS, SSM_STATE)), 0.0)
    da_im = jnp.where(first_row, rep(dai.reshape(SSM_GROUPS, SSM_STATE)), 0.0)
    expand_t = (jnp.arange(SSM_GROUPS)[:, None] == (jnp.arange(_PARAM_SHAPE[0]) // SSM_GROUP)[None, :]).astype(BF16)
    d_lr, d_li, d_ls, d_bt_re, d_bt_im = _ssm_params_bwd(lr, li, ls, bt_re, bt_im, da_re, da_im, dbb_re, dbb_im, expand_t)
    from_bt = lambda t: t.reshape(SSM_GROUPS, SSM_GROUP, SSM_STATE).transpose(0, 2, 1)
    from_cc = lambda t: _diag_blocks(t, SSM_STATE, SSM_GROUP).transpose(0, 1, 3, 2).reshape(SSM_GROUPS, SSM_GROUP, SSM_STATE)

    small = {
        "g_mix": d_g_mix, "b_f": d_bf[0, :HEADS], "g_q": d_gq.reshape(HEADS, HEAD_DIM).sum(0),
        "g_k": d_gk.reshape(HEADS, HEAD_DIM).sum(0), "lambda_re": d_lr, "lambda_im": d_li, "log_step": d_ls,
        "b_re": from_bt(d_bt_re), "b_im": from_bt(d_bt_im), "c_re": from_cc(dccr), "c_im": from_cc(dcci),
        "d_skip": dd, "b_glu": d_b_glu, "g_attn_out": d_g_att, "g_ssm_out": d_g_ssm, "g_ffn": d_g_ffn,
        "conv_b": dcw[:, 3],
    }
    big = {"w_in": d_w_in_r, "w_glu": d_w_glu, "w_out": d_w_out, "w_up": d_w_up, "w_down": d_w_down}
    return loss_blk[0, 0], dx, big, small, dcw[:, 0:3].transpose(1, 0, 2).reshape(3, 2 * D_FF), (early_lands, land_in)


def kernel(x, g_mix, w_in, b_f, g_q, g_k, lambda_re, lambda_im, log_step, b_re, b_im, c_re, c_im, d_skip, w_glu, b_glu, g_attn_out, g_ssm_out, w_out, g_ffn, w_up, conv_w, conv_b, w_down, loss_target, m_g_mix, m_w_in, m_b_f, m_g_q, m_g_k, m_lambda_re, m_lambda_im, m_log_step, m_b_re, m_b_im, m_c_re, m_c_im, m_d_skip, m_w_glu, m_b_glu, m_g_attn_out, m_g_ssm_out, m_w_out, m_g_ffn, m_w_up, m_conv_w, m_conv_b, m_w_down, v_g_mix, v_w_in, v_b_f, v_g_q, v_g_k, v_lambda_re, v_lambda_im, v_log_step, v_b_re, v_b_im, v_c_re, v_c_im, v_d_skip, v_w_glu, v_b_glu, v_g_attn_out, v_g_ssm_out, v_w_out, v_g_ffn, v_w_up, v_conv_w, v_conv_b, v_w_down):
    args = dict(locals())
    order = ["g_mix", "w_in", "b_f", "g_q", "g_k", "lambda_re", "lambda_im", "log_step", "b_re", "b_im", "c_re", "c_im",
             "d_skip", "w_glu", "b_glu", "g_attn_out", "g_ssm_out", "w_out", "g_ffn", "w_up", "conv_w", "conv_b", "w_down"]
    comm = _MeshComm(args)
    chip = comm.chip
    loss_part, dx, big, small, d_conv_w, lands = _local_step(x[0], loss_target[0], args, comm)
    loss = lax.psum(loss_part, ("x", "y", "c"))

    g_big = comm.reduce(lands)

    small_names = [n for n, _ in _SMALL]
    small_shapes = [sh for _, sh in _SMALL]
    gsum = _allreduce_small(_pack_small([small[n] for n in small_names], extra=[d_conv_w]))
    g_small = _unpack_small(gsum, small_shapes + [(3, 2 * D_FF)])
    g_conv_w = lax.dynamic_slice_in_dim(g_small[-1], chip * (2 * D_FF // N_CHIPS), 2 * D_FF // N_CHIPS, axis=1)
    g_small = dict(zip(small_names, g_small[:-1]))

    grad, delta, new_m, new_v = {}, {}, {}, {}
    for n in ("w_in", "w_glu", "w_out", "w_up", "w_down"):
        grad[n] = g_big[n]
        delta[n], new_m[n], new_v[n] = _adamw(args[n], g_big[n], args["m_" + n], args["v_" + n], name="adamw_" + n)
    grad["conv_w"] = g_conv_w
    delta["conv_w"], new_m["conv_w"], new_v["conv_w"] = _adamw(conv_w, g_conv_w, m_conv_w, v_conv_w, name="adamw_conv_w")
    stepped = _adamw_small([args[n] for n in small_names], [g_small[n] for n in small_names],
                           [args["m_" + n] for n in small_names], [args["v_" + n] for n in small_names])
    for i, n in enumerate(small_names):
        grad[n] = g_small[n]
        delta[n], new_m[n], new_v[n] = stepped[3 * i:3 * i + 3]

    return (loss, dx[None], *[grad[n] for n in order], *[delta[n] for n in order], *[new_m[n] for n in order],
            *[new_v[n] for n in order])
```

```python
import math

import jax
import jax.numpy as jnp
from jax import lax
from jax.experimental import pallas as pl
from jax.experimental.pallas import tpu as pltpu

F32 = jnp.float32
BF16 = jnp.bfloat16

D_MODEL = 1024
HEADS = 8
HEAD_DIM = 64
ATTN_W = 512
SSM_W = 512
SSM_GROUPS = 32
SSM_GROUP = 16
SSM_STATE = 64
N_STATE = SSM_GROUPS * SSM_STATE
D_FF = 2816
IN_COLS = 2056
Z_COLS = 2176
F_COL0 = 1536
U_COL0 = 1544
EPS = 1e-6
NEG_INF = -1e30
N_CHIPS = 4
LANES = 128
SUBLANES = 8
SSM_CHUNKS = 2
CHUNK_U = SSM_W // SSM_CHUNKS
CHUNK_S = N_STATE // SSM_CHUNKS
HEADS_PER_STEP = 4
STRIP = 128
N_STRIPS = D_FF // STRIP

ADAM_LR = 0.001
ADAM_B1 = 0.9
ADAM_B2 = 0.999
ADAM_EPS = 1e-08
ADAM_WD = 0.01
ADAM_STEP = 10

VMEM_LIMIT = 56 * 1024 * 1024
MESH = pl.DeviceIdType.MESH


def _pallas(body, **kw):
    return pl.pallas_call(body, **kw)


def _pcall(body, *, name, out_shape, in_specs, out_specs, grid=(), scratch_shapes=(), dims=None):
    params = pltpu.CompilerParams(dimension_semantics=dims, vmem_limit_bytes=VMEM_LIMIT)
    return _pallas(body, name=name, grid=grid, in_specs=in_specs, out_specs=out_specs,
                   out_shape=out_shape, scratch_shapes=scratch_shapes, compiler_params=params)


def _sds(shape, dtype=F32):
    return jax.ShapeDtypeStruct(shape, dtype)


def _dot(a, b):
    return jnp.dot(a, b, preferred_element_type=F32)


def _dot_nt(a, b):
    return lax.dot_general(a, b, (((1,), (1,)), ((), ())), preferred_element_type=F32)


def _dot_tn(a, b):
    return lax.dot_general(a, b, (((0,), (0,)), ((), ())), preferred_element_type=F32)


def _split3(x):
    hi = x.astype(BF16)
    r = x - hi.astype(F32)
    mid = r.astype(BF16)
    lo = (r - mid.astype(F32)).astype(BF16)
    return hi, mid, lo


def _dot_exact_r(x, m01):
    hi, mid, lo = _split3(x)
    return _dot(hi, m01) + _dot(mid, m01) + _dot(lo, m01)


def _dot_exact_l(m01, x):
    hi, mid, lo = _split3(x)
    return _dot(m01, hi) + _dot(m01, mid) + _dot(m01, lo)


def _sigmoid(x):
    return 1.0 / (1.0 + jnp.exp(-x))


def _rms(x, g):
    r = lax.rsqrt(jnp.mean(x * x, axis=-1, keepdims=True) + EPS)
    return x * r * g


def _rms_bwd(x, g, dy):
    r = lax.rsqrt(jnp.mean(x * x, axis=-1, keepdims=True) + EPS)
    w = dy * g
    dx = r * w - x * (r * r * r) * jnp.mean(w * x, axis=-1, keepdims=True)
    dg = jnp.sum(dy * x * r, axis=0, keepdims=True)
    return dx, dg


_GELU_K = math.sqrt(2.0 / math.pi)
_GELU_C = 0.044715


def _gelu(y):
    return y * (0.5 * (1.0 + jnp.tanh(_GELU_K * (y + _GELU_C * (y * y * y)))))


def _gelu_grad(y):
    t = jnp.tanh(_GELU_K * (y + _GELU_C * (y * y * y)))
    return 0.5 * (1.0 + t) + 0.5 * y * (1.0 - t * t) * (_GELU_K * (1.0 + 3.0 * _GELU_C * y * y))


def _tile(n, pref):
    if n <= pref:
        return n
    divs = [t for t in range(LANES, n + 1, LANES) if n % t == 0]
    below = [t for t in divs if t <= pref]
    if below and 2 * below[-1] >= pref:
        return below[-1]
    above = [t for t in divs if t > pref]
    return above[0] if above else n


def _row_tile(s):
    return min(256, s)


def _mm(a, b, *, name, tm, tn, tk, ta=False, tb=False, a_parts=1, b_parts=1, carry=False, hosted=None):
    if a_parts > 1:
        m, kk = a.shape[1], a.shape[2] * a_parts
    elif ta:
        kk, m = a.shape
    else:
        m, kk = a.shape
    if b_parts > 1:
        n = b.shape[2] * b_parts
    else:
        n = b.shape[0] if tb else b.shape[1]
    tm, tn, tk = _tile(m, tm), _tile(n // b_parts, tn), _tile(kk // a_parts, tk)
    k_per, n_per = kk // a_parts // tk, n // b_parts // tn

    def body(a_ref, b_ref, o_ref):
        k = pl.program_id(2)
        if ta:
            part = _dot_tn(a_ref[...], b_ref[...])
        elif tb:
            part = _dot_nt(a_ref[...], b_ref[...])
        else:
            part = _dot(a_ref[...], b_ref[...])

        @pl.when(k == 0)
        def _():
            o_ref[...] = part

        @pl.when(k > 0)
        def _():
            o_ref[...] += part

    if a_parts > 1:
        a_spec = pl.BlockSpec((None, tm, tk), lambda i, j, k: (k // k_per, i, k % k_per))
    else:
        a_spec = pl.BlockSpec((tk, tm), lambda i, j, k: (k, i)) if ta else pl.BlockSpec((tm, tk), lambda i, j, k: (i, k))
    if b_parts > 1:
        b_spec = pl.BlockSpec((None, tk, tn), lambda i, j, k: (j // n_per, k, j % n_per))
    else:
        b_spec = pl.BlockSpec((tn, tk), lambda i, j, k: (j, k)) if tb else pl.BlockSpec((tk, tn), lambda i, j, k: (k, j))
    grid = (m // tm, n // tn, kk // tk)
    at = lambda step: (lambda: jnp.logical_and(jnp.logical_and(pl.program_id(0) == step[0], pl.program_id(1) == step[1]),
                                               pl.program_id(2) == step[2]))
    (out,), carried = _host_pcall(body, hosted, at((0, 0, 0)), at(tuple(g - 1 for g in grid)), n_in=2, n_out=1, n_scratch=0,
                                  name=name, grid=grid, in_specs=[a_spec, b_spec],
                                  out_specs=[pl.BlockSpec((tm, tn), lambda i, j, k: (i, j))], out_shape=[_sds((m, n))],
                                  scratch_shapes=[], dims=("parallel", "parallel", "arbitrary"), operands=(a, b))
    return (out, carried) if carry else out


def _in_proj(x, g_mix, w_in_r):
    s = x.shape[0]
    tm = _row_tile(s)

    def body(x_ref, g_ref, w_ref, h_ref, z_ref):
        h = _rms(x_ref[...], g_ref[...]).astype(BF16)
        h_ref[...] = h
        z_ref[...] = _dot(h, w_ref[...])

    return _pcall(body, name="in_proj", grid=(s // tm,),
                  in_specs=[pl.BlockSpec((tm, D_MODEL), lambda i: (i, 0)), pl.BlockSpec((1, D_MODEL), lambda i: (0, 0)),
                            pl.BlockSpec((D_MODEL, Z_COLS), lambda i: (0, 0))],
                  out_specs=[pl.BlockSpec((tm, D_MODEL), lambda i: (i, 0)), pl.BlockSpec((tm, Z_COLS), lambda i: (i, 0))],
                  out_shape=[_sds((s, D_MODEL), BF16), _sds((s, Z_COLS))], dims=("parallel",))(x, g_mix, w_in_r)


def _split_heads(ref, val):
    for h in range(HEADS):
        ref[h] = val[:, h * HEAD_DIM:(h + 1) * HEAD_DIM].astype(ref.dtype)


def _merge_heads(ref):
    return jnp.concatenate([ref[h].astype(F32) for h in range(HEADS)], axis=-1)


def _forget_logits(z_ref, bf_ref):
    fl = z_ref[:, F_COL0:F_COL0 + LANES] + bf_ref[...]
    return jnp.where(lax.broadcasted_iota(jnp.int32, fl.shape, 1) < HEADS, fl, 0.0)


def _attn_prep(z, gq, gk, bf, gg):
    s = z.shape[0]
    tm = _row_tile(s)

    def body(z_ref, gq_ref, gk_ref, bf_ref, gg_ref, qn_ref, kn_ref, vb_ref, ub_ref, uf_ref, c_ref, carry_ref):
        i = pl.program_id(0)

        @pl.when(i == 0)
        def _():
            carry_ref[...] = jnp.zeros_like(carry_ref)

        gg_m = gg_ref[...]

        def head_norm(t, g):
            ssq = _dot_exact_r(t * t, gg_m)
            return t * lax.rsqrt(ssq * (1.0 / HEAD_DIM) + EPS) * g

        _split_heads(qn_ref, head_norm(z_ref[:, 0:ATTN_W], gq_ref[...]))
        _split_heads(kn_ref, head_norm(z_ref[:, ATTN_W:2 * ATTN_W], gk_ref[...]))
        _split_heads(vb_ref, z_ref[:, 2 * ATTN_W:3 * ATTN_W])
        u = z_ref[:, U_COL0:U_COL0 + SSM_W]
        uf_ref[...] = u
        ub_ref[...] = u.astype(BF16)
        fl = _forget_logits(z_ref, bf_ref)
        lf = jnp.minimum(fl, 0.0) - jnp.log1p(jnp.exp(-jnp.abs(fl)))
        row = lax.broadcasted_iota(jnp.int32, (tm, tm), 0)
        col = lax.broadcasted_iota(jnp.int32, (tm, tm), 1)
        tri = (row >= col).astype(BF16)
        c = _dot_exact_l(tri, lf) + carry_ref[...]
        c_ref[...] = c
        carry_ref[...] = c[tm - 1:tm, :]

    row_spec = lambda w: pl.BlockSpec((tm, w), lambda i: (i, 0))
    const = lambda shape: pl.BlockSpec(shape, lambda i: (0, 0))
    heads = pl.BlockSpec((HEADS, tm, HEAD_DIM), lambda i: (0, i, 0))
    return _pcall(body, name="attn_prep", grid=(s // tm,),
                  in_specs=[row_spec(Z_COLS), const((1, ATTN_W)), const((1, ATTN_W)), const((1, LANES)), const((ATTN_W, ATTN_W))],
                  out_specs=[heads] * 3 + [row_spec(SSM_W), row_spec(SSM_W), row_spec(LANES)],
                  out_shape=[_sds((HEADS, s, HEAD_DIM), BF16)] * 3 + [_sds((s, SSM_W), BF16), _sds((s, SSM_W)), _sds((s, LANES))],
                  scratch_shapes=[pltpu.VMEM((1, LANES), F32)], dims=("arbitrary",))(z, gq, gk, bf, gg)


def _attn_fwd(qh, kh, vh, crow, hosted=None):
    _, s, _ = qh.shape
    tq = _row_tile(s)
    scale = HEAD_DIM ** -0.5

    hp = HEADS_PER_STEP
    nq = s // tq
    fold = lambda t, op: op(t[:, :tq // 2], t[:, tq // 2:])

    def body(q_ref, k_ref, v_ref, c_ref, o_ref, lse_ref, s_s):
        i = pl.program_id(1)

        def first(j, ms, diagonal):
            off = pl.multiple_of(j * tq, tq)
            out = []
            for hh in range(hp):
                sc = _dot_nt(q_ref[hh], k_ref[hh, pl.ds(off, tq), :]) * scale - c_ref[hh, :, pl.ds(off, tq)]
                if diagonal:
                    causal = lax.broadcasted_iota(jnp.int32, (tq, tq), 1) <= lax.broadcasted_iota(jnp.int32, (tq, tq), 0)
                    sc = jnp.where(causal, sc, NEG_INF)
                s_s[hh, j] = sc
                out.append(jnp.maximum(ms[hh], fold(sc, jnp.maximum)))
            return tuple(out)

        ms = lax.fori_loop(0, i, lambda j, c: first(j, c, False), (jnp.full((tq, tq // 2), NEG_INF, F32),) * hp)
        ms = [jnp.max(t, axis=-1, keepdims=True) for t in first(i, ms, True)]

        def second(j, carry):
            rows = pl.ds(pl.multiple_of(j * tq, tq), tq)
            out = []
            for hh in range(hp):
                ls, acc = carry[hh]
                p = jnp.exp(s_s[hh, j] - ms[hh])
                out.append((ls + fold(p, jnp.add), acc + _dot(p.astype(BF16), v_ref[hh, rows, :])))
            return tuple(out)

        zero = (jnp.zeros((tq, tq // 2), F32), jnp.zeros((tq, HEAD_DIM), F32))
        for hh, (ls, acc) in enumerate(lax.fori_loop(0, i + 1, second, (zero,) * hp)):
            l = jnp.sum(ls, axis=-1, keepdims=True)
            o_ref[hh] = acc / l
            lse_ref[hh] = ms[hh] + jnp.log(l)

    blk = pl.BlockSpec((hp, tq, HEAD_DIM), lambda h, i: (h, i, 0))
    full = pl.BlockSpec((hp, s, HEAD_DIM), lambda h, i: (h, 0, 0))
    nh = HEADS // hp
    first = lambda: jnp.logical_and(pl.program_id(0) == 0, pl.program_id(1) == 0)
    last = lambda: jnp.logical_and(pl.program_id(0) == nh - 1, pl.program_id(1) == nq - 1)
    return _host_pcall(body, hosted, first, last, n_in=4, n_out=2, n_scratch=1, name="attn_fwd", grid=(nh, nq),
                       in_specs=[blk, full, full, pl.BlockSpec((hp, 1, s), lambda h, i: (h, 0, 0))],
                       out_specs=[blk, pl.BlockSpec((hp, tq, 1), lambda h, i: (h, i, 0))],
                       out_shape=[_sds((HEADS, s, HEAD_DIM)), _sds((HEADS, s, 1))],
                       scratch_shapes=[pltpu.VMEM((hp, nq, tq, tq), F32)],
                       dims=("parallel", "parallel"), operands=(qh, kh, vh, crow))


def _ssm_param_fn(lr, li, ls, br, bi):
    step = jnp.exp(ls)
    er = jnp.exp(lr * step)
    ab_re = er * jnp.cos(li * step)
    ab_im = er * jnp.sin(li * step)
    num_re = ab_re - 1.0
    num_im = ab_im
    den = lr * lr + li * li
    f_re = (num_re * lr + num_im * li) / den
    f_im = (num_im * lr - num_re * li) / den
    bb_re = f_re * br - f_im * bi
    bb_im = f_re * bi + f_im * br
    return ab_re, ab_im, bb_re, bb_im


_PARAM_SHAPE = (SSM_GROUPS * SSM_GROUP, SSM_STATE)


def _ssm_params(lr, li, ls, br, bi):
    def body(lr_ref, li_ref, ls_ref, br_ref, bi_ref, ar_ref, ai_ref, bbr_ref, bbi_ref):
        ar, ai, bbr, bbi = _ssm_param_fn(lr_ref[...], li_ref[...], ls_ref[...], br_ref[...], bi_ref[...])
        ar_ref[...] = ar
        ai_ref[...] = ai
        bbr_ref[...] = bbr
        bbi_ref[...] = bbi

    spec = pl.BlockSpec(_PARAM_SHAPE, lambda: (0, 0))
    return _pcall(body, name="ssm_params", in_specs=[spec] * 5, out_specs=[spec] * 4,
                  out_shape=[_sds(_PARAM_SHAPE)] * 4)(lr, li, ls, br, bi)


def _ssm_params_bwd(lr, li, ls, br, bi, dar, dai, dbbr, dbbi, expand_t):
    def body(lr_ref, li_ref, ls_ref, br_ref, bi_ref, dar_ref, dai_ref, dbbr_ref, dbbi_ref, et_ref,
             dlr_ref, dli_ref, dls_ref, dbr_ref, dbi_ref):
        _, vjp = jax.vjp(_ssm_param_fn, lr_ref[...], li_ref[...], ls_ref[...], br_ref[...], bi_ref[...])
        dlr, dli, dls, dbr, dbi = vjp((dar_ref[...], dai_ref[...], dbbr_ref[...], dbbi_ref[...]))
        et = et_ref[...]
        dlr_ref[...] = _dot_exact_l(et, dlr)
        dli_ref[...] = _dot_exact_l(et, dli)
        dls_ref[...] = jnp.sum(_dot_exact_l(et, dls), axis=-1, keepdims=True)
        dbr_ref[...] = dbr
        dbi_ref[...] = dbi

    spec = pl.BlockSpec(_PARAM_SHAPE, lambda: (0, 0))
    gspec = pl.BlockSpec((SSM_GROUPS, SSM_STATE), lambda: (0, 0))
    return _pcall(body, name="ssm_params_bwd",
                  in_specs=[spec] * 9 + [pl.BlockSpec((SSM_GROUPS, _PARAM_SHAPE[0]), lambda: (0, 0))],
                  out_specs=[gspec, gspec, pl.BlockSpec((SSM_GROUPS, 1), lambda: (0, 0)), spec, spec],
                  out_shape=[_sds((SSM_GROUPS, SSM_STATE))] * 2 + [_sds((SSM_GROUPS, 1))] + [_sds(_PARAM_SHAPE)] * 2,
                  )(lr, li, ls, br, bi, dar, dai, dbbr, dbbi, expand_t)


def _cmul(ar, ai, br, bi):
    return ar * br - ai * bi, ar * bi + ai * br


def _scan_consts(ar, ai, width, reverse):
    row = lax.broadcasted_iota(jnp.int32, (SUBLANES, width), 0)
    pw = [(ar, ai)]
    for _ in range(SUBLANES - 1):
        pw.append(_cmul(pw[-1][0], pw[-1][1], ar, ai))
    steps = []
    for d in (1, 2, 4):
        keep = (row < SUBLANES - d) if reverse else (row >= d)
        steps.append((d, jnp.where(keep, pw[d - 1][0], 0.0), jnp.where(keep, pw[d - 1][1], 0.0)))
    pr = jnp.zeros((SUBLANES, width), F32)
    pi = jnp.zeros((SUBLANES, width), F32)
    for r in range(SUBLANES):
        e = (SUBLANES - r) if reverse else (r + 1)
        pr = jnp.where(row == r, pw[e - 1][0], pr)
        pi = jnp.where(row == r, pw[e - 1][1], pi)
    return steps, pr, pi


def _scan_tile(xr, xi, cr, ci, consts, reverse):
    steps, pr, pi = consts
    for d, mr, mi in steps:
        sh = (SUBLANES - d) if reverse else d
        sr = pltpu.roll(xr, sh, 0)
        si = pltpu.roll(xi, sh, 0)
        xr, xi = xr + mr * sr - mi * si, xi + mr * si + mi * sr
    return xr + pr * cr - pi * ci, xi + pr * ci + pi * cr


def _ssm_fwd(ub, uf, bbr, bbi, ar, ai, ccr, cci, dsk, hosted=None):
    s = ub.shape[0]
    tm = _row_tile(s)
    nt = tm // SUBLANES

    def body(ub_ref, u_ref, bbr_ref, bbi_ref, ar_ref, ai_ref, ccr_ref, cci_ref, dsk_ref,
             xr_ref, xi_ref, y_ref, cr_s, ci_s):
        i = pl.program_id(1)

        @pl.when(i == 0)
        def _():
            cr_s[...] = jnp.zeros_like(cr_s)
            ci_s[...] = jnp.zeros_like(ci_s)

        u_b = ub_ref[...]
        xr_ref[...] = _dot(u_b, bbr_ref[0])
        xi_ref[...] = _dot(u_b, bbi_ref[0])
        consts = _scan_consts(ar_ref[0], ai_ref[0], CHUNK_S, False)

        def tile(k, carry):
            cr, ci = carry
            sl = pl.ds(pl.multiple_of(k * SUBLANES, SUBLANES), SUBLANES)
            xr, xi = _scan_tile(xr_ref[sl, :], xi_ref[sl, :], cr, ci, consts, False)
            xr_ref[sl, :] = xr
            xi_ref[sl, :] = xi
            return xr[SUBLANES - 1:SUBLANES, :], xi[SUBLANES - 1:SUBLANES, :]

        cr, ci = lax.fori_loop(0, nt, tile, (cr_s[...], ci_s[...]))
        cr_s[...] = cr
        ci_s[...] = ci
        y_ref[...] = (_dot(xr_ref[...].astype(BF16), ccr_ref[0]) - _dot(xi_ref[...].astype(BF16), cci_ref[0])
                      + dsk_ref[...] * u_ref[...])

    wspec = lambda a, b: pl.BlockSpec((1, a, b), lambda j, i: (j, 0, 0))
    nb = s // tm
    first = lambda: jnp.logical_and(pl.program_id(0) == 0, pl.program_id(1) == 0)
    last = lambda: jnp.logical_and(pl.program_id(0) == SSM_CHUNKS - 1, pl.program_id(1) == nb - 1)
    return _host_pcall(
        body, hosted, first, last, n_in=9, n_out=3, n_scratch=2, name="ssm_fwd", grid=(SSM_CHUNKS, nb),
        in_specs=[pl.BlockSpec((tm, CHUNK_U), lambda j, i: (i, j)),
                  pl.BlockSpec((tm, CHUNK_U), lambda j, i: (i, j)),
                  wspec(CHUNK_U, CHUNK_S), wspec(CHUNK_U, CHUNK_S), wspec(1, CHUNK_S), wspec(1, CHUNK_S),
                  wspec(CHUNK_S, CHUNK_U), wspec(CHUNK_S, CHUNK_U),
                  pl.BlockSpec((1, CHUNK_U), lambda j, i: (0, j))],
        out_specs=[pl.BlockSpec((tm, CHUNK_S), lambda j, i: (i, j)), pl.BlockSpec((tm, CHUNK_S), lambda j, i: (i, j)),
                   pl.BlockSpec((tm, CHUNK_U), lambda j, i: (i, j))],
        out_shape=[_sds((s, N_STATE)), _sds((s, N_STATE)), _sds((s, SSM_W))],
        scratch_shapes=[pltpu.VMEM((1, CHUNK_S), F32)] * 2,
        dims=("parallel", "arbitrary"), operands=(ub, uf, bbr, bbi, ar, ai, ccr, cci, dsk))


def _ssm_glu(y, w_glu, b_glu):
    ge = _gelu(y)
    sg = _sigmoid(_dot(ge.astype(BF16), w_glu) + b_glu)
    return ge, sg


def _mix_out(y, att, x, w_glu, b_glu, g_att, g_ssm, w_out, g_ffn, hosted=None):
    s = x.shape[0]
    tm = _row_tile(s)

    def body(y_ref, att_ref, x_ref, wg_ref, bg_ref, ga_ref, gs_ref, wo_ref, gf_ref, x1_ref, mix_ref, h2_ref):
        ge, sg = _ssm_glu(y_ref[...], wg_ref[...], bg_ref[...])
        ms = _rms(ge * sg, gs_ref[...]).astype(BF16)
        ma = _rms(_merge_heads(att_ref), ga_ref[...]).astype(BF16)
        mix_ref[:, 0:ATTN_W] = ma
        mix_ref[:, ATTN_W:D_MODEL] = ms
        x1 = x_ref[...] + (_dot(ma, wo_ref[0:ATTN_W, :]) + _dot(ms, wo_ref[ATTN_W:D_MODEL, :]))
        x1_ref[...] = x1
        h2_ref[...] = _rms(x1, gf_ref[...]).astype(BF16)

    row = lambda w: pl.BlockSpec((tm, w), lambda i: (i, 0))
    const = lambda a, b: pl.BlockSpec((a, b), lambda i: (0, 0))
    nb = s // tm
    return _host_pcall(body, hosted, lambda: pl.program_id(0) == 0, lambda: pl.program_id(0) == nb - 1,
                       n_in=9, n_out=3, n_scratch=0, name="mix_out", grid=(nb,),
                       in_specs=[row(SSM_W), pl.BlockSpec((HEADS, tm, HEAD_DIM), lambda i: (0, i, 0)), row(D_MODEL),
                                 const(SSM_W, SSM_W), const(1, SSM_W),
                                 const(1, ATTN_W), const(1, SSM_W), const(D_MODEL, D_MODEL), const(1, D_MODEL)],
                       out_specs=[row(D_MODEL)] * 3,
                       out_shape=[_sds((s, D_MODEL)), _sds((s, D_MODEL), BF16), _sds((s, D_MODEL), BF16)],
                       scratch_shapes=[], dims=("parallel",), operands=(y, att, x, w_glu, b_glu, g_att, g_ssm, w_out, g_ffn))


CONV_CHUNK = 64


def _conv_rows(pad_ref, w, b, r0, n):
    y = b + pad_ref[pl.ds(r0 + SUBLANES - 2, n), :] * w[0:1, :]
    y = y + pad_ref[pl.ds(r0 + SUBLANES - 1, n), :] * w[1:2, :]
    return y + pad_ref[pl.ds(r0 + SUBLANES, n), :] * w[2:3, :]


def _fill_front_pad(pad_ref, strip_ref, s):
    pad_ref[0:SUBLANES, :] = jnp.zeros((SUBLANES, STRIP), F32)
    for r0 in range(0, s, CONV_CHUNK):
        pad_ref[pl.ds(SUBLANES + r0, CONV_CHUNK), :] = strip_ref[pl.ds(r0, CONV_CHUNK), :]


def _conv_act(up, conv_w, conv_b):
    s = up.shape[0]

    def body(ug_ref, uv_ref, wg_ref, wv_ref, bg_ref, bv_ref, act_ref, pg_ref, pv_ref):
        _fill_front_pad(pg_ref, ug_ref, s)
        _fill_front_pad(pv_ref, uv_ref, s)
        wg, wv, bg, bv = wg_ref[...], wv_ref[...], bg_ref[...], bv_ref[...]
        for r0 in range(0, s, CONV_CHUNK):
            hg = _conv_rows(pg_ref, wg, bg, r0, CONV_CHUNK)
            hv = _conv_rows(pv_ref, wv, bv, r0, CONV_CHUNK)
            act_ref[pl.ds(r0, CONV_CHUNK), :] = (hg * _sigmoid(hg) * hv).astype(BF16)

    strip = lambda off: pl.BlockSpec((s, STRIP), lambda j: (0, j + off))
    wsp = lambda off: pl.BlockSpec((3, STRIP), lambda j: (0, j + off))
    bsp = lambda off: pl.BlockSpec((1, STRIP), lambda j: (0, j + off))
    return _pcall(body, name="conv_act", grid=(N_STRIPS,),
                  in_specs=[strip(0), strip(N_STRIPS), wsp(0), wsp(N_STRIPS), bsp(0), bsp(N_STRIPS)],
                  out_specs=pl.BlockSpec((s, STRIP), lambda j: (0, j)), out_shape=_sds((s, D_FF), BF16),
                  scratch_shapes=[pltpu.VMEM((s + SUBLANES, STRIP), F32)] * 2,
                  dims=("parallel",))(up, up, conv_w, conv_w, conv_b, conv_b)


def _down_loss(act, w_down, x1, tgt):
    s = x1.shape[0]
    tm = _row_tile(s)

    def body(a_ref, w_ref, x1_ref, t_ref, dy_ref, dyb_ref, loss_ref):
        i = pl.program_id(0)

        @pl.when(i == 0)
        def _():
            loss_ref[...] = jnp.zeros_like(loss_ref)

        diff = x1_ref[...] + _dot(a_ref[...], w_ref[...]) - t_ref[...]
        dy = diff * (1.0 / D_MODEL)
        dy_ref[...] = dy
        dyb_ref[...] = dy.astype(BF16)
        loss_ref[...] += 0.5 * jnp.sum(diff * dy)

    row = lambda w: pl.BlockSpec((tm, w), lambda i: (i, 0))
    return _pcall(body, name="down_loss", grid=(s // tm,),
                  in_specs=[row(D_FF), pl.BlockSpec((D_FF, D_MODEL), lambda i: (0, 0)), row(D_MODEL), row(D_MODEL)],
                  out_specs=[row(D_MODEL), row(D_MODEL), pl.BlockSpec((SUBLANES, LANES), lambda i: (0, 0))],
                  out_shape=[_sds((s, D_MODEL)), _sds((s, D_MODEL), BF16), _sds((SUBLANES, LANES))],
                  dims=("arbitrary",))(act, w_down, x1, tgt)


def _conv_act_bwd(up, dact, conv_w, conv_b):
    s = up.shape[0]
    ch = CONV_CHUNK

    def body(ug_ref, uv_ref, da_ref, wg_ref, wv_ref, bg_ref, bv_ref, dup_ref, dcw_ref, pg_ref, pv_ref, dg_ref, dv_ref):
        _fill_front_pad(pg_ref, ug_ref, s)
        _fill_front_pad(pv_ref, uv_ref, s)
        zero = jnp.zeros((SUBLANES, STRIP), F32)
        dg_ref[pl.ds(s, SUBLANES), :] = zero
        dv_ref[pl.ds(s, SUBLANES), :] = zero
        wg, wv, bg, bv = wg_ref[...], wv_ref[...], bg_ref[...], bv_ref[...]
        tile_sum = lambda t: jnp.sum(t.reshape(ch // SUBLANES, SUBLANES, STRIP), axis=0)
        accs = [[zero] * 4, [zero] * 4]
        for r0 in range(0, s, ch):
            hg = _conv_rows(pg_ref, wg, bg, r0, ch)
            hv = _conv_rows(pv_ref, wv, bv, r0, ch)
            sg = _sigmoid(hg)
            da = da_ref[pl.ds(r0, ch), :]
            dhs = (da * hv * (sg * (1.0 + hg * (1.0 - sg))), da * (hg * sg))
            for half, (dh, d_ref, p_ref) in enumerate(zip(dhs, (dg_ref, dv_ref), (pg_ref, pv_ref))):
                d_ref[pl.ds(r0, ch), :] = dh
                for k in range(3):
                    accs[half][k] = accs[half][k] + tile_sum(dh * p_ref[pl.ds(r0 + SUBLANES - 2 + k, ch), :])
                accs[half][3] = accs[half][3] + tile_sum(dh)
        for half, (d_ref, w) in enumerate(((dg_ref, wg), (dv_ref, wv))):
            for r0 in range(0, s, ch):
                dup = (d_ref[pl.ds(r0, ch), :] * w[2:3, :] + d_ref[pl.ds(r0 + 1, ch), :] * w[1:2, :]
                       + d_ref[pl.ds(r0 + 2, ch), :] * w[0:1, :])
                dup_ref[half, pl.ds(r0, ch), :] = dup.astype(BF16)
            rid = lax.broadcasted_iota(jnp.int32, (SUBLANES, STRIP), 0)
            out = zero
            for k in range(4):
                out = jnp.where(rid == k, jnp.sum(accs[half][k], axis=0, keepdims=True), out)
            dcw_ref[half] = out

    strip = lambda off: pl.BlockSpec((s, STRIP), lambda j: (0, j + off))
    wsp = lambda off: pl.BlockSpec((3, STRIP), lambda j: (0, j + off))
    bsp = lambda off: pl.BlockSpec((1, STRIP), lambda j: (0, j + off))
    return _pcall(body, name="conv_act_bwd", grid=(N_STRIPS,),
                  in_specs=[strip(0), strip(N_STRIPS), strip(0), wsp(0), wsp(N_STRIPS), bsp(0), bsp(N_STRIPS)],
                  out_specs=[pl.BlockSpec((2, s, STRIP), lambda j: (0, 0, j)), pl.BlockSpec((2, SUBLANES, STRIP), lambda j: (0, 0, j))],
                  out_shape=[_sds((2, s, D_FF), BF16), _sds((2, SUBLANES, D_FF))],
                  scratch_shapes=[pltpu.VMEM((s + SUBLANES, STRIP), F32)] * 4,
                  dims=("parallel",))(up, up, dact, conv_w, conv_w, conv_b, conv_b)


def _mix_bwd(dy, dh2, x1, g_ffn, w_out, y, att, w_glu, b_glu, g_att, g_ssm):
    s = dy.shape[0]
    tm = _row_tile(s)

    def body(dy_ref, dh2_ref, x1_ref, gf_ref, wo_ref, y_ref, att_ref, wg_ref, bg_ref, ga_ref, gs_ref,
             dx1_ref, dx1b_ref, datt_ref, dys_ref, dwg_ref, dgf_ref, dga_ref, dgs_ref, dbg_ref):
        i = pl.program_id(0)

        @pl.when(i == 0)
        def _():
            for r in (dwg_ref, dgf_ref, dga_ref, dgs_ref, dbg_ref):
                r[...] = jnp.zeros_like(r)

        dxn, dgf = _rms_bwd(x1_ref[...], gf_ref[...], dh2_ref[...])
        dx1 = dy_ref[...] + dxn
        dx1_ref[...] = dx1
        dx1b = dx1.astype(BF16)
        dx1b_ref[...] = dx1b
        dgf_ref[...] += dgf
        dma = _dot_nt(dx1b, wo_ref[0:ATTN_W, :])
        dms = _dot_nt(dx1b, wo_ref[ATTN_W:D_MODEL, :])
        datt, dga = _rms_bwd(_merge_heads(att_ref), ga_ref[...], dma)
        _split_heads(datt_ref, datt)
        dga_ref[...] += dga
        yv = y_ref[...]
        ge, sg = _ssm_glu(yv, wg_ref[...], bg_ref[...])
        dssm, dgs = _rms_bwd(ge * sg, gs_ref[...], dms)
        dgs_ref[...] += dgs
        dgl = dssm * ge * sg * (1.0 - sg)
        dglb = dgl.astype(BF16)
        dge = dssm * sg + _dot_nt(dglb, wg_ref[...])
        dbg_ref[...] += jnp.sum(dgl, axis=0, keepdims=True)
        dwg_ref[...] += _dot_tn(ge.astype(BF16), dglb)
        dys_ref[...] = dge * _gelu_grad(yv)

    row = lambda w: pl.BlockSpec((tm, w), lambda i: (i, 0))
    const = lambda a, b: pl.BlockSpec((a, b), lambda i: (0, 0))
    heads = pl.BlockSpec((HEADS, tm, HEAD_DIM), lambda i: (0, i, 0))
    return _pcall(body, name="mix_bwd", grid=(s // tm,),
                  in_specs=[row(D_MODEL), row(D_MODEL), row(D_MODEL), const(1, D_MODEL), const(D_MODEL, D_MODEL), row(SSM_W),
                            heads, const(SSM_W, SSM_W), const(1, SSM_W), const(1, ATTN_W), const(1, SSM_W)],
                  out_specs=[row(D_MODEL), row(D_MODEL), heads, row(SSM_W), const(SSM_W, SSM_W), const(1, D_MODEL),
                             const(1, ATTN_W), const(1, SSM_W), const(1, SSM_W)],
                  out_shape=[_sds((s, D_MODEL)), _sds((s, D_MODEL), BF16), _sds((HEADS, s, HEAD_DIM)), _sds((s, SSM_W)),
                             _sds((SSM_W, SSM_W)), _sds((1, D_MODEL)), _sds((1, ATTN_W)), _sds((1, SSM_W)), _sds((1, SSM_W))],
                  dims=("arbitrary",))(dy, dh2, x1, g_ffn, w_out, y, att, w_glu, b_glu, g_att, g_ssm)


def _ssm_bwd(dys, uf, ub, xr, xi, bbr, bbi, ar, ai, ccr, cci, dsk, hosted=None):
    s = dys.shape[0]
    tm = _row_tile(s)
    nb = s // tm
    nt = tm // SUBLANES

    def body(dy_ref, u_ref, ub_ref, xr_ref, xi_ref, xrp_ref, xip_ref, bbr_ref, bbi_ref, ar_ref, ai_ref, ccr_ref,
             cci_ref, dsk_ref, du_ref, dbbr_ref, dbbi_ref, dccr_ref, dcci_ref, dar_ref, dai_ref, dd_ref,
             gr_s, gi_s, cr_s, ci_s, accr_s, acci_s):
        i = pl.program_id(1)
        first_block = i == nb - 1

        @pl.when(i == 0)
        def _():
            for r in (cr_s, ci_s, accr_s, acci_s, dbbr_ref, dbbi_ref, dccr_ref, dcci_ref, dd_ref):
                r[...] = jnp.zeros_like(r)

        dy = dy_ref[...]
        dyb = dy.astype(BF16)
        gr_s[...] = _dot_nt(dyb, ccr_ref[0])
        gi_s[...] = -_dot_nt(dyb, cci_ref[0])
        consts = _scan_consts(ar_ref[0], -ai_ref[0], CHUNK_S, True)
        row = lax.broadcasted_iota(jnp.int32, (SUBLANES, CHUNK_S), 0)

        def tile(kk, carry):
            cr, ci, accr, acci = carry
            k = nt - 1 - kk
            sl = pl.ds(pl.multiple_of(k * SUBLANES, SUBLANES), SUBLANES)
            gr, gi = _scan_tile(gr_s[sl, :], gi_s[sl, :], cr, ci, consts, True)
            gr_s[sl, :] = gr
            gi_s[sl, :] = gi
            slp = pl.ds(pl.multiple_of(jnp.maximum(k - 1, 0) * SUBLANES, SUBLANES), SUBLANES)
            inner = k > 0
            pr_t = jnp.where(inner, xr_ref[slp, :], xrp_ref[...])
            pi_t = jnp.where(inner, xi_ref[slp, :], xip_ref[...])
            live = jnp.logical_or(inner, jnp.logical_not(first_block))
            top_r = jnp.where(live, pltpu.roll(pr_t, 1, 0), 0.0)
            top_i = jnp.where(live, pltpu.roll(pi_t, 1, 0), 0.0)
            xpr = jnp.where(row == 0, top_r, pltpu.roll(xr_ref[sl, :], 1, 0))
            xpi = jnp.where(row == 0, top_i, pltpu.roll(xi_ref[sl, :], 1, 0))
            accr = accr + gr * xpr + gi * xpi
            acci = acci + gi * xpr - gr * xpi
            return gr[0:1, :], gi[0:1, :], accr, acci

        zeros = jnp.zeros((SUBLANES, CHUNK_S), F32)
        cr, ci, accr, acci = lax.fori_loop(0, nt, tile, (cr_s[...], ci_s[...], zeros, zeros))
        cr_s[...] = cr
        ci_s[...] = ci
        accr_s[...] += accr
        acci_s[...] += acci
        grb = gr_s[...].astype(BF16)
        gib = gi_s[...].astype(BF16)
        u_b = ub_ref[...]
        du_ref[...] = _dot_nt(grb, bbr_ref[0]) + _dot_nt(gib, bbi_ref[0]) + dsk_ref[...] * dy
        dbbr_ref[0] += _dot_tn(u_b, grb)
        dbbi_ref[0] += _dot_tn(u_b, gib)
        dccr_ref[0] += _dot_tn(xr_ref[...].astype(BF16), dyb)
        dcci_ref[0] -= _dot_tn(xi_ref[...].astype(BF16), dyb)
        dd_ref[...] += jnp.sum(dy * u_ref[...], axis=0, keepdims=True)

        @pl.when(i == nb - 1)
        def _():
            dar_ref[0] = jnp.sum(accr_s[...], axis=0, keepdims=True)
            dai_ref[0] = jnp.sum(acci_s[...], axis=0, keepdims=True)

    tiles_per_block = tm // SUBLANES
    rb = lambda i: nb - 1 - i
    wspec = lambda a, b: pl.BlockSpec((1, a, b), lambda j, i: (j, 0, 0))
    xblk = pl.BlockSpec((tm, CHUNK_S), lambda j, i: (rb(i), j))
    xprev = pl.BlockSpec((SUBLANES, CHUNK_S), lambda j, i: (jnp.maximum(rb(i) * tiles_per_block - 1, 0), j))
    ublk = pl.BlockSpec((tm, CHUNK_U), lambda j, i: (rb(i), j))
    first = lambda: jnp.logical_and(pl.program_id(0) == 0, pl.program_id(1) == 0)
    last = lambda: jnp.logical_and(pl.program_id(0) == SSM_CHUNKS - 1, pl.program_id(1) == nb - 1)
    return _host_pcall(
        body, hosted, first, last, n_in=14, n_out=8, n_scratch=6, name="ssm_bwd", grid=(SSM_CHUNKS, nb),
        in_specs=[ublk, ublk, ublk, xblk, xblk, xprev, xprev,
                  wspec(CHUNK_U, CHUNK_S), wspec(CHUNK_U, CHUNK_S), wspec(1, CHUNK_S), wspec(1, CHUNK_S),
                  wspec(CHUNK_S, CHUNK_U), wspec(CHUNK_S, CHUNK_U), pl.BlockSpec((1, CHUNK_U), lambda j, i: (0, j))],
        out_specs=[ublk, wspec(CHUNK_U, CHUNK_S), wspec(CHUNK_U, CHUNK_S), wspec(CHUNK_S, CHUNK_U),
                   wspec(CHUNK_S, CHUNK_U), wspec(1, CHUNK_S), wspec(1, CHUNK_S),
                   pl.BlockSpec((1, CHUNK_U), lambda j, i: (0, j))],
        out_shape=[_sds((s, SSM_W)), _sds((SSM_CHUNKS, CHUNK_U, CHUNK_S)), _sds((SSM_CHUNKS, CHUNK_U, CHUNK_S)),
                   _sds((SSM_CHUNKS, CHUNK_S, CHUNK_U)), _sds((SSM_CHUNKS, CHUNK_S, CHUNK_U)),
                   _sds((SSM_CHUNKS, 1, CHUNK_S)), _sds((SSM_CHUNKS, 1, CHUNK_S)), _sds((1, SSM_W))],
        scratch_shapes=[pltpu.VMEM((tm, CHUNK_S), F32)] * 2 + [pltpu.VMEM((1, CHUNK_S), F32)] * 2
                       + [pltpu.VMEM((SUBLANES, CHUNK_S), F32)] * 2,
        dims=("parallel", "arbitrary"), operands=(dys, uf, ub, xr, xi, xr, xi, bbr, bbi, ar, ai, ccr, cci, dsk))


def _attn_probs(q, ks, cs, lse, scale, diagonal):
    p = jnp.exp(_dot_nt(q, ks) * scale - cs - lse)
    if diagonal:
        tq, tk = p.shape
        causal = lax.broadcasted_iota(jnp.int32, (tq, tk), 1) <= lax.broadcasted_iota(jnp.int32, (tq, tk), 0)
        p = jnp.where(causal, p, 0.0)
    return p


def _attn_bwd(qh, kh, vh, crow, lse, doh, hosted=None):
    _, s, _ = qh.shape
    tq = _row_tile(s)
    nq = s // tq
    scale = HEAD_DIM ** -0.5
    hp = HEADS_PER_STEP

    def body(q_ref, k_ref, v_ref, c_ref, lse_ref, do_ref, dq_ref, dk_ref, dv_ref, dc_ref, p_s, dp_s):
        i = pl.program_id(1)

        @pl.when(i == 0)
        def _():
            for r in (dk_ref, dv_ref, dc_ref):
                r[...] = jnp.zeros_like(r)

        dobs = [do_ref[hh].astype(BF16) for hh in range(hp)]

        def first(j, dls, diagonal):
            off = pl.multiple_of(j * tq, tq)
            out = []
            for hh in range(hp):
                p = _attn_probs(q_ref[hh], k_ref[hh, pl.ds(off, tq), :], c_ref[hh, :, pl.ds(off, tq)], lse_ref[hh],
                                scale, diagonal)
                dp = _dot_nt(dobs[hh], v_ref[hh, pl.ds(off, tq), :])
                p_s[hh, j] = p
                dp_s[hh, j] = dp
                out.append(dls[hh] + jnp.sum(p * dp, axis=-1, keepdims=True))
            return tuple(out)

        zero_col = jnp.zeros((tq, 1), F32)
        dls = lax.fori_loop(0, i, lambda j, c: first(j, c, False), (zero_col,) * hp)
        dls = first(i, dls, True)

        def second(j, dqs):
            rows = pl.ds(pl.multiple_of(j * tq, tq), tq)
            out = []
            for hh in range(hp):
                p = p_s[hh, j]
                ds = p * (dp_s[hh, j] - dls[hh])
                dsb = ds.astype(BF16)
                dv_ref[hh, rows, :] += _dot_tn(p.astype(BF16), dobs[hh])
                dk_ref[hh, rows, :] += _dot_tn(dsb, q_ref[hh]) * scale
                dc_ref[hh, :, rows] -= jnp.sum(ds, axis=0, keepdims=True)
                out.append(dqs[hh] + _dot(dsb, k_ref[hh, rows, :]))
            return tuple(out)

        dqs = lax.fori_loop(0, i + 1, second, (jnp.zeros((tq, HEAD_DIM), F32),) * hp)
        for hh in range(hp):
            dq_ref[hh] = dqs[hh] * scale

    blk = pl.BlockSpec((hp, tq, HEAD_DIM), lambda h, i: (h, i, 0))
    full = pl.BlockSpec((hp, s, HEAD_DIM), lambda h, i: (h, 0, 0))
    crow_spec = pl.BlockSpec((hp, 1, s), lambda h, i: (h, 0, 0))
    nh = HEADS // hp
    first = lambda: jnp.logical_and(pl.program_id(0) == 0, pl.program_id(1) == 0)
    last = lambda: jnp.logical_and(pl.program_id(0) == nh - 1, pl.program_id(1) == nq - 1)
    return _host_pcall(body, hosted, first, last, n_in=6, n_out=4, n_scratch=2, name="attn_bwd", grid=(nh, nq),
                       in_specs=[blk, full, full, crow_spec, pl.BlockSpec((hp, tq, 1), lambda h, i: (h, i, 0)), blk],
                       out_specs=[blk, full, full, crow_spec],
                       out_shape=[_sds((HEADS, s, HEAD_DIM))] * 3 + [_sds((HEADS, 1, s))],
                       scratch_shapes=[pltpu.VMEM((hp, nq, tq, tq), F32)] * 2,
                       dims=("parallel", "arbitrary"), operands=(qh, kh, vh, crow, lse, doh))


def _prep_bwd(z, dqn, dkn, dv, du, dc, gq, gk, bf, gg):
    s = z.shape[0]
    tm = _row_tile(s)
    nb = s // tm

    def body(z_ref, dqn_ref, dkn_ref, dv_ref, du_ref, dc_ref, gq_ref, gk_ref, bf_ref, gg_ref,
             dz_ref, dgq_ref, dgk_ref, dbf_ref, carry_ref):
        i = pl.program_id(0)

        @pl.when(i == 0)
        def _():
            for r in (dgq_ref, dgk_ref, dbf_ref, carry_ref):
                r[...] = jnp.zeros_like(r)

        gg_m = gg_ref[...]

        def head_norm_bwd(t, g, dn):
            r = lax.rsqrt(_dot_exact_r(t * t, gg_m) * (1.0 / HEAD_DIM) + EPS)
            w = dn * g
            mean_wt = _dot_exact_r(w * t, gg_m) * (1.0 / HEAD_DIM)
            return r * w - t * (r * r * r) * mean_wt, jnp.sum(dn * t * r, axis=0, keepdims=True)

        dq, dgq = head_norm_bwd(z_ref[:, 0:ATTN_W], gq_ref[...], _merge_heads(dqn_ref))
        dk, dgk = head_norm_bwd(z_ref[:, ATTN_W:2 * ATTN_W], gk_ref[...], _merge_heads(dkn_ref))
        dgq_ref[...] += dgq
        dgk_ref[...] += dgk
        row = lax.broadcasted_iota(jnp.int32, (tm, tm), 0)
        col = lax.broadcasted_iota(jnp.int32, (tm, tm), 1)
        triu = (col >= row).astype(BF16)
        dlf = _dot_exact_l(triu, dc_ref[...]) + carry_ref[...]
        carry_ref[...] = dlf[0:1, :]
        df = dlf * _sigmoid(-_forget_logits(z_ref, bf_ref))
        dbf_ref[...] += jnp.sum(df, axis=0, keepdims=True)
        dz_ref[:, 0:ATTN_W] = dq.astype(BF16)
        dz_ref[:, ATTN_W:2 * ATTN_W] = dk.astype(BF16)
        dz_ref[:, 2 * ATTN_W:3 * ATTN_W] = _merge_heads(dv_ref).astype(BF16)
        tail = jnp.concatenate([df[:, :HEADS], du_ref[...], jnp.zeros((tm, Z_COLS - IN_COLS), F32)], axis=-1)
        dz_ref[:, F_COL0:Z_COLS] = tail.astype(BF16)

    row_spec = lambda w: pl.BlockSpec((tm, w), lambda i: (nb - 1 - i, 0))
    const = lambda shape: pl.BlockSpec(shape, lambda i: (0, 0))
    return _pcall(body, name="prep_bwd", grid=(nb,),
                  in_specs=[row_spec(Z_COLS)] + [pl.BlockSpec((HEADS, tm, HEAD_DIM), lambda i: (0, nb - 1 - i, 0))] * 3
                           + [row_spec(ATTN_W), row_spec(LANES), const((1, ATTN_W)),
                              const((1, ATTN_W)), const((1, LANES)), const((ATTN_W, ATTN_W))],
                  out_specs=[row_spec(Z_COLS), const((1, ATTN_W)), const((1, ATTN_W)), const((1, LANES))],
                  out_shape=[_sds((s, Z_COLS), BF16), _sds((1, ATTN_W)), _sds((1, ATTN_W)), _sds((1, LANES))],
                  scratch_shapes=[pltpu.VMEM((1, LANES), F32)], dims=("arbitrary",))(z, dqn, dkn, dv, du, dc, gq, gk, bf, gg)


def _in_norm_bwd(x, g_mix, dh, dx1, hosted=None):
    s = x.shape[0]
    tm = _row_tile(s)

    def body(x_ref, g_ref, dh_ref, dx1_ref, dx_ref, dg_ref):
        i = pl.program_id(0)

        @pl.when(i == 0)
        def _():
            dg_ref[...] = jnp.zeros_like(dg_ref)

        dxn, dg = _rms_bwd(x_ref[...], g_ref[...], dh_ref[...])
        dx_ref[...] = dx1_ref[...] + dxn
        dg_ref[...] += dg

    row = pl.BlockSpec((tm, D_MODEL), lambda i: (i, 0))
    vec = pl.BlockSpec((1, D_MODEL), lambda i: (0, 0))
    nb = s // tm
    return _host_pcall(body, hosted, lambda: pl.program_id(0) == 0, lambda: pl.program_id(0) == nb - 1,
                       n_in=4, n_out=2, n_scratch=0, name="in_norm_bwd", grid=(nb,), in_specs=[row, vec, row, row],
                       out_specs=[row, vec], out_shape=[_sds((s, D_MODEL)), _sds((1, D_MODEL))], scratch_shapes=[],
                       dims=("arbitrary",), operands=(x, g_mix, dh, dx1))


def _adamw_refs(w_ref, g_ref, m_ref, v_ref, d_ref, mo_ref, vo_ref):
    gv = g_ref[...]
    mn = ADAM_B1 * m_ref[...] + (1.0 - ADAM_B1) * gv
    vn = ADAM_B2 * v_ref[...] + (1.0 - ADAM_B2) * (gv * gv)
    m_hat = mn / (1.0 - ADAM_B1 ** ADAM_STEP)
    v_hat = vn / (1.0 - ADAM_B2 ** ADAM_STEP)
    d_ref[...] = -ADAM_LR * (m_hat / (jnp.sqrt(v_hat) + ADAM_EPS) + ADAM_WD * w_ref[...])
    mo_ref[...] = mn
    vo_ref[...] = vn


def _adamw_small(ws, gs, ms, vs):
    n = len(ws)

    def body(*refs):
        ins, outs = refs[:4 * n], refs[4 * n:]
        for i in range(n):
            _adamw_refs(ins[i], ins[n + i], ins[2 * n + i], ins[3 * n + i], *outs[3 * i:3 * i + 3])

    vm = pl.BlockSpec(memory_space=pltpu.VMEM)
    out_shape = [_sds(w.shape) for w in ws for _ in range(3)]
    return _pallas(body, name="adamw_small", in_specs=[vm] * (4 * n), out_specs=[vm] * (3 * n), out_shape=out_shape,
                   compiler_params=pltpu.CompilerParams(vmem_limit_bytes=VMEM_LIMIT))(*ws, *gs, *ms, *vs)


def _adamw(w, g, m, v, *, name):
    r, c = w.shape
    tr = r
    for cand in (256, 176, 128, 64):
        if r > cand and r % cand == 0:
            tr = cand
            break

    def body(w_ref, g_ref, m_ref, v_ref, d_ref, mo_ref, vo_ref):
        _adamw_refs(w_ref, g_ref, m_ref, v_ref, d_ref, mo_ref, vo_ref)

    spec = pl.BlockSpec((tr, c), lambda i: (i, 0))
    return _pcall(body, name=name, grid=(r // tr,), in_specs=[spec] * 4, out_specs=[spec] * 3,
                  out_shape=[_sds((r, c))] * 3, dims=("parallel",))(w, g, m, v)


def _prefetch_call(body, *, name, grid, in_specs, out_specs, out_shape, operands):
    grid_spec = pltpu.PrefetchScalarGridSpec(num_scalar_prefetch=1, grid=grid, in_specs=in_specs, out_specs=out_specs)
    params = pltpu.CompilerParams(dimension_semantics=("parallel",) * len(grid), vmem_limit_bytes=VMEM_LIMIT)
    return _pallas(body, name=name, grid_spec=grid_spec, out_shape=out_shape, compiler_params=params)(*operands)


def _place_cols(buf, shard, place):
    rows, cols = shard.shape
    tr = 256

    def body(place_ref, s_ref, b_ref, o_ref):
        o_ref[...] = s_ref[...]

    grid_spec = pltpu.PrefetchScalarGridSpec(
        num_scalar_prefetch=1, grid=(rows // tr,),
        in_specs=[pl.BlockSpec((tr, cols), lambda i, p: (i, 0)), pl.BlockSpec(memory_space=pltpu.HBM)],
        out_specs=pl.BlockSpec((tr, cols), lambda i, p: (i, p[0])))
    return _pallas(body, name="place_own_cols", grid_spec=grid_spec, out_shape=_sds(buf.shape, buf.dtype),
                   input_output_aliases={2: 0},
                   compiler_params=pltpu.CompilerParams(dimension_semantics=("parallel",),
                                                        vmem_limit_bytes=VMEM_LIMIT))(place, shard, buf)


def _half_rows_tile(hr):
    return hr if hr <= 256 else 176 if hr % 176 == 0 else 256


def _add_half(g, landed, place, *, name):
    def body(place_ref, g_ref, l_ref, o_ref):
        own = g_ref[0] if len(g_ref.shape) == 4 else g_ref[...]
        o_ref[...] = (own + l_ref[...]).astype(BF16)

    if g.ndim == 4:
        _, _, hr, c = g.shape
        tr = _half_rows_tile(hr)
        blk = (1, tr, c)
        return _prefetch_call(
            body, name=name, grid=(N_CHIPS, hr // tr),
            in_specs=[pl.BlockSpec((1,) + blk, lambda j, i, p: (j, p[1], i, 0)), pl.BlockSpec(blk, lambda j, i, p: (j, i, 0))],
            out_specs=pl.BlockSpec(blk, lambda j, i, p: (j, i, 0)), out_shape=_sds(landed.shape, BF16),
            operands=(place, g, landed))
    hr, c = landed.shape
    tr, tc = 256, _tile(c, 2176)
    nb = hr // tr
    return _prefetch_call(
        body, name=name, grid=(nb, c // tc),
        in_specs=[pl.BlockSpec((tr, tc), lambda i, j, p: (p[1] * nb + i, j)), pl.BlockSpec((tr, tc), lambda i, j, p: (i, j))],
        out_specs=pl.BlockSpec((tr, tc), lambda i, j, p: (i, j)), out_shape=_sds(landed.shape, BF16),
        operands=(place, g, landed))


def _sum_chips(chip_sum, lands, place, *, name, tc, window_stride=0):
    _, hr, c = lands.shape
    tr = _half_rows_tile(hr)
    nb = hr // tr
    ncb = c // tc

    def body(place_ref, own_ref, a_ref, b_ref, c_ref, o_ref):
        own = own_ref[0] if len(own_ref.shape) == 3 else own_ref[...]
        o_ref[...] = ((own.astype(F32) + a_ref[0].astype(F32)) + b_ref[0].astype(F32)) + c_ref[0].astype(F32)

    land = lambda k: pl.BlockSpec((1, tr, tc), lambda i, j, p: ((p[0] + k) % N_CHIPS, i, j))
    if chip_sum.ndim == 3:
        own_spec = land(0)
    else:
        stride = window_stride // tc
        own_spec = pl.BlockSpec((tr, tc), lambda i, j, p: (i, p[0] * stride + j))
    return _prefetch_call(
        body, name=name, grid=(nb, ncb), in_specs=[own_spec, land(1), land(2), land(3)],
        out_specs=pl.BlockSpec((tr, tc), lambda i, j, p: (p[1] * nb + i, j)), out_shape=_sds((2 * hr, c)),
        operands=(place, chip_sum, lands, lands, lands))


_HBM = pl.BlockSpec(memory_space=pltpu.HBM)


def _place():
    x, y, c = lax.axis_index("x"), lax.axis_index("y"), lax.axis_index("c")
    chips = [(1 - x, y), (x, 1 - y), (1 - x, 1 - y)]
    return x, y, c, chips


def _rcopy(src, dst, send_sem, recv_sem, to):
    return pltpu.make_async_remote_copy(src_ref=src, dst_ref=dst, send_sem=send_sem, recv_sem=recv_sem,
                                        device_id=to, device_id_type=MESH)


UP_COLS = 2 * D_FF // N_CHIPS
IN_WINDOW = 640
IN_STRIDE = 512


class _Hosted:
    def __init__(self, operands, out_shapes, n_sems, start, finish, aliases=None, local_sems=0):
        self.operands, self.out_shapes, self.n_sems = list(operands), list(out_shapes), n_sems
        self.start, self.finish, self.aliases, self.local_sems = start, finish, dict(aliases or {}), local_sems

    def scratch(self):
        return ([pltpu.SemaphoreType.DMA((self.n_sems,)), pltpu.SemaphoreType.DMA((self.n_sems,))]
                + [pltpu.SemaphoreType.DMA] * self.local_sems)


def _both(a, b):
    na, nao, nas = len(a.operands), len(a.out_shapes), len(a.scratch())

    def start(ins, outs, sems):
        a.start(ins[:na], outs[:nao], sems[:nas])
        b.start(ins[na:], outs[nao:], sems[nas:])

    def finish(ins, outs, sems):
        a.finish(ins[:na], outs[:nao], sems[:nas])
        b.finish(ins[na:], outs[nao:], sems[nas:])

    both = _Hosted(a.operands + b.operands, a.out_shapes + b.out_shapes, 0, start, finish,
                   aliases={**a.aliases, **{na + i: nao + o for i, o in b.aliases.items()}})
    both.scratch = lambda: a.scratch() + b.scratch()
    return both


def _then(a, b):
    nas = len(a.scratch())

    def finish(ins, outs, sems):
        a.finish(ins, outs, sems[:nas])
        b.start(ins, outs, sems[nas:])
        b.finish(ins, outs, sems[nas:])

    chain = _Hosted(a.operands, a.out_shapes, 0, lambda ins, outs, sems: a.start(ins, outs, sems[:nas]), finish,
                    aliases=a.aliases)
    chain.scratch = lambda: a.scratch() + b.scratch()
    return chain


def _run_hosted(hosted, *, name):
    n_in, n_out = len(hosted.operands), len(hosted.out_shapes)

    def body(*refs):
        parts = (refs[:n_in], refs[n_in:n_in + n_out], refs[n_in + n_out:])
        hosted.start(*parts)
        hosted.finish(*parts)

    return _pallas(body, name=name, in_specs=[_HBM] * n_in, out_specs=[_HBM] * n_out, out_shape=hosted.out_shapes,
                   input_output_aliases=hosted.aliases, scratch_shapes=hosted.scratch())(*hosted.operands)


def _host_pcall(core_body, hosted, first, last, *, n_in, n_out, n_scratch, name, grid, in_specs, out_specs, out_shape,
                scratch_shapes, dims, operands):
    if hosted is None:
        outs = _pcall(core_body, name=name, grid=grid, in_specs=in_specs, out_specs=out_specs, out_shape=out_shape,
                      scratch_shapes=scratch_shapes, dims=dims)(*operands)
        return outs, []
    hi, ho = len(hosted.operands), len(hosted.out_shapes)

    def body(*refs):
        a, b = n_in, n_in + hi
        c, d = b + n_out, b + n_out + ho
        e = d + n_scratch
        parts = (refs[a:b], refs[c:d], refs[e:])

        @pl.when(first())
        def _():
            hosted.start(*parts)

        core_body(*refs[:a], *refs[b:c], *refs[d:e])

        @pl.when(last())
        def _():
            hosted.finish(*parts)

    params = pltpu.CompilerParams(dimension_semantics=("arbitrary",) * len(grid), vmem_limit_bytes=VMEM_LIMIT)
    outs = _pallas(body, name=name, grid=grid, in_specs=list(in_specs) + [_HBM] * hi, out_specs=list(out_specs) + [_HBM] * ho,
                   out_shape=list(out_shape) + hosted.out_shapes, scratch_shapes=list(scratch_shapes) + hosted.scratch(),
                   input_output_aliases={n_in + a: n_out + b for a, b in hosted.aliases.items()},
                   compiler_params=params)(*operands, *hosted.operands)
    return outs[:n_out], outs[n_out:]


WHOLE_HALF = (0, 1, 1)


def _band_rows(src, hc, band):
    first, count, of = band
    hr = src.shape[0] // 2
    return pl.ds(hc * hr + first * (hr // of), count * (hr // of))


def _gather_slot(src, out, chip, hc, band=WHOLE_HALF):
    cols = src.shape[1]
    if len(out.shape) == 2:
        return out.at[_band_rows(src, hc, band), pl.ds(pl.multiple_of(chip * cols, LANES), cols)]
    return out.at[chip, _band_rows(src, hc, band), :]


def _gathered_shape(shard, by_cols):
    if by_cols:
        return _sds((shard.shape[0], N_CHIPS * shard.shape[1]), shard.dtype)
    return _sds((N_CHIPS,) + shard.shape, shard.dtype)


def _plan_gather_ici(shards, by_cols, whole=(), bands=None, into=None):
    n = len(shards)
    bands = bands or [WHOLE_HALF] * n
    into = into or [None] * n
    given = [w for w in range(n) if into[w] is not None]
    n_ops = n + len(whole)

    def copies(ins, outs, sems):
        send_sems, recv_sems = sems[0], sems[1]
        x, y, c, chips = _place()
        me = 2 * x + y
        sends, waits = [], []
        for w in range(n + len(whole)):
            for k, (cx, cy) in enumerate(chips):
                sem = (send_sems.at[3 * w + k], recv_sems.at[3 * w + k])
                if w < n:
                    sends.append(_rcopy(ins[w].at[_band_rows(ins[w], c, bands[w]), :],
                                        _gather_slot(ins[w], outs[w], me, c, bands[w]), *sem, (cx, cy, c)))
                    landed = _gather_slot(ins[w], outs[w], 2 * cx + cy, c, bands[w])
                else:
                    sends.append(_rcopy(ins[w], outs[w].at[me], *sem, (cx, cy, c)))
                    landed = outs[w].at[2 * cx + cy]
                waits.append(_rcopy(landed, landed, *sem, (cx, cy, c)))
        return sends, waits

    def start(ins, outs, sems):
        for cp in copies(ins, outs, sems)[0]:
            cp.start()

    def finish(ins, outs, sems):
        sends, waits = copies(ins, outs, sems)
        for cp in waits:
            cp.wait_recv()
        for cp in sends:
            cp.wait_send()

    out_shapes = [_gathered_shape(s, bc) for s, bc in zip(shards, by_cols)] + [_sds((N_CHIPS,) + a.shape, a.dtype) for a in whole]
    return _Hosted(list(shards) + list(whole) + [into[w] for w in given], out_shapes, 3 * n_ops, start, finish,
                   aliases={n_ops + i: w for i, w in enumerate(given)})


def _plan_gather_d2d(bufs, shard_shapes, bands=None):
    n = len(bufs)
    bands = bands or [WHOLE_HALF] * n

    def copies(ins, outs, sems):
        send_sems, recv_sems = sems
        x, y, c, chips = _place()
        sibling = (x, y, 1 - c)
        sends, waits = [], []
        for w in range(n):
            for k, (cx, cy) in enumerate(chips):
                sem = (send_sems.at[3 * w + k], recv_sems.at[3 * w + k])
                landed = _gather_slot(shard_shapes[w], outs[w], 2 * cx + cy, c, bands[w])
                other = _gather_slot(shard_shapes[w], outs[w], 2 * cx + cy, 1 - c, bands[w])
                sends.append(_rcopy(landed, landed, *sem, sibling))
                waits.append(_rcopy(other, other, *sem, sibling))
        return sends, waits

    def start(ins, outs, sems):
        for cp in copies(ins, outs, sems)[0]:
            cp.start()

    def finish(ins, outs, sems):
        sends, waits = copies(ins, outs, sems)
        for cp in waits:
            cp.wait_recv()
        for cp in sends:
            cp.wait_send()

    return _Hosted(bufs, [_sds(b.shape, b.dtype) for b in bufs], 3 * n, start, finish, aliases={w: w for w in range(n)})


def _plan_swap(grads):
    def copies(ins, outs, sems):
        send_sems, recv_sems = sems
        x, y, c, _ = _place()
        cps = []
        for w, g_ref in enumerate(ins):
            if len(g_ref.shape) == 4:
                theirs = g_ref.at[:, 1 - c]
            else:
                hr = g_ref.shape[0] // 2
                theirs = g_ref.at[pl.ds((1 - c) * hr, hr), :]
            cps.append(_rcopy(theirs, outs[w], send_sems.at[w], recv_sems.at[w], (x, y, 1 - c)))
        return cps

    def start(ins, outs, sems):
        for cp in copies(ins, outs, sems):
            cp.start()

    def finish(ins, outs, sems):
        for cp in copies(ins, outs, sems):
            cp.wait()

    out_shapes = [_sds((g.shape[0], g.shape[2], g.shape[3])) if g.ndim == 4 else _sds((g.shape[0] // 2, g.shape[1]))
                  for g in grads]
    return _Hosted(grads, out_shapes, len(grads), start, finish)


def _plan_scatter(chip_sums, windows):
    def copies(ins, outs, sems):
        send_sems, recv_sems = sems
        x, y, c, chips = _place()
        me = 2 * x + y
        sends, waits = [], []
        for w, s_ref in enumerate(ins):
            for k, (cx, cy) in enumerate(chips):
                tgt = 2 * cx + cy
                if windows[w] is not None:
                    stride, width = windows[w]
                    part = s_ref.at[:, pl.ds(pl.multiple_of(tgt * stride, LANES), width)]
                else:
                    part = s_ref.at[tgt]
                sem = (send_sems.at[3 * w + k], recv_sems.at[3 * w + k])
                sends.append(_rcopy(part, outs[w].at[me], *sem, (cx, cy, c)))
                slot = outs[w].at[tgt]
                waits.append(_rcopy(slot, slot, *sem, (cx, cy, c)))
        return sends, waits

    def start(ins, outs, sems):
        for cp in copies(ins, outs, sems)[0]:
            cp.start()

    def finish(ins, outs, sems):
        sends, waits = copies(ins, outs, sems)
        for cp in waits:
            cp.wait_recv()
        for cp in sends:
            cp.wait_send()

    out_shapes = [_sds((N_CHIPS, s.shape[0], win[1]), BF16) if win is not None else _sds(s.shape, BF16)
                  for s, win in zip(chip_sums, windows)]
    return _Hosted(chip_sums, out_shapes, 3 * len(chip_sums), start, finish)


def _plan_join(reds):
    def copies(ins, outs, sems):
        send_sems, recv_sems = sems
        x, y, c, _ = _place()
        sends, waits = [], []
        for w, out in enumerate(outs):
            hr = out.shape[0] // 2
            mine = out.at[pl.ds(c * hr, hr), :]
            theirs = out.at[pl.ds((1 - c) * hr, hr), :]
            sends.append(_rcopy(mine, mine, send_sems.at[w], recv_sems.at[w], (x, y, 1 - c)))
            waits.append(_rcopy(theirs, theirs, send_sems.at[w], recv_sems.at[w], (x, y, 1 - c)))
        return sends, waits

    def start(ins, outs, sems):
        for cp in copies(ins, outs, sems)[0]:
            cp.start()

    def finish(ins, outs, sems):
        sends, waits = copies(ins, outs, sems)
        for cp in waits:
            cp.wait_recv()
        for cp in sends:
            cp.wait_send()

    return _Hosted(reds, [_sds(r.shape) for r in reds], len(reds), start, finish, aliases={w: w for w in range(len(reds))})


def _allreduce_small(v):
    m_per = v.shape[0]

    def body(v_ref, out_ref, all_ref, send_sems, recv_sems, local_sem):
        x, y, c, chips = _place()
        me, sibling = (x, y, c), (x, y, 1 - c)

        def rows(px, py, pc):
            return all_ref.at[pl.ds((4 * px + 2 * py + pc) * m_per, m_per), :]

        def copy(k, block, to, src=None):
            return _rcopy(rows(*block) if src is None else src, rows(*block), send_sems.at[k], recv_sems.at[k], to)

        mine = pltpu.make_async_copy(v_ref, rows(*me), local_sem)
        mine.start()
        first = [copy(0, me, sibling, src=v_ref)]
        first += [copy(1 + k, me, (*chip, c), src=v_ref) for k, chip in enumerate(chips)]
        for cp in first:
            cp.start()
        passed = [copy(4 + k, (*chip, c), sibling) for k, chip in enumerate(chips)]
        for k, chip in enumerate(chips):
            copy(1 + k, (*chip, c), me).wait_recv()
            passed[k].start()
        copy(0, sibling, me).wait_recv()
        for k, chip in enumerate(chips):
            copy(4 + k, (*chip, 1 - c), me).wait_recv()
        for cp in first + passed:
            cp.wait_send()
        mine.wait()
        acc = all_ref[pl.ds(0, m_per), :]
        for d in range(1, 8):
            acc = acc + all_ref[pl.ds(d * m_per, m_per), :]
        out_ref[...] = acc

    vm = pl.BlockSpec(memory_space=pltpu.VMEM)
    return _pallas(body, name="allreduce_small", in_specs=[vm], out_specs=vm, out_shape=_sds((m_per, LANES)),
                          scratch_shapes=[pltpu.VMEM((8 * m_per, LANES), F32), pltpu.SemaphoreType.DMA((7,)),
                                          pltpu.SemaphoreType.DMA((7,)), pltpu.SemaphoreType.DMA],
                          compiler_params=pltpu.CompilerParams(vmem_limit_bytes=VMEM_LIMIT))(v)


def _block_diag(blocks):
    j, g, a, b = blocks.shape
    eye = jnp.eye(g, dtype=bool)[None, :, None, :, None]
    return jnp.where(eye, blocks[:, :, :, None, :], jnp.zeros((), blocks.dtype)).reshape(j, g * a, g * b)


def _diag_blocks(m, a, b):
    j = m.shape[0]
    g = m.shape[1] // a
    t = m.reshape(j, g, a, g, b)
    eye = jnp.eye(g, dtype=bool)[None, :, None, :, None]
    return jnp.sum(jnp.where(eye, t, 0.0), axis=3)


_SMALL = (("g_mix", (1024,)), ("b_f", (8,)), ("g_q", (64,)), ("g_k", (64,)), ("lambda_re", (32, 64)),
          ("lambda_im", (32, 64)), ("log_step", (32,)), ("b_re", (32, 64, 16)), ("b_im", (32, 64, 16)),
          ("c_re", (32, 16, 64)), ("c_im", (32, 16, 64)), ("d_skip", (32, 16)), ("b_glu", (512,)),
          ("g_attn_out", (512,)), ("g_ssm_out", (512,)), ("g_ffn", (1024,)), ("conv_b", (5632,)))


def _packed_rows(n):
    tile = SUBLANES * LANES
    return -(-n // tile) * SUBLANES


def _pack_small(arrs):
    parts = []
    for a in arrs:
        flat = a.reshape(-1)
        rows = _packed_rows(flat.shape[0])
        parts.append(jnp.pad(flat, (0, rows * LANES - flat.shape[0])).reshape(rows, LANES))
    return jnp.concatenate(parts, axis=0)


def _unpack_small(buf, shapes):
    out, r = [], 0
    for shape in shapes:
        n = math.prod(shape)
        out.append(buf[r:r + _packed_rows(n)].reshape(-1)[:n].reshape(shape))
        r += _packed_rows(n)
    return out


def _halves(t):
    return t.reshape(N_CHIPS, 2, t.shape[0] // (2 * N_CHIPS), t.shape[1])


class _MeshComm:
    def __init__(self, args):
        x, y, self.core = lax.axis_index("x"), lax.axis_index("y"), lax.axis_index("c")
        self.chip = 2 * x + y
        self.place = jnp.stack([self.chip, self.core]).astype(jnp.int32)
        self.shards = {n: args[n].astype(BF16) for n in ("w_in", "w_glu", "w_out", "w_up", "w_down")}
        self.conv_w = args["conv_w"]

    def _own(self, stacked, mine):
        return lax.dynamic_update_slice(stacked, mine[None], (self.chip,) + (0,) * mine.ndim)

    def w_in(self):
        sh = self.shards["w_in"]
        (buf,) = _run_hosted(_then(_plan_gather_ici([sh], [False]), _plan_gather_d2d([sh], [sh])), name="gather_w_in")
        whole = self._own(buf, sh).transpose(1, 0, 2).reshape(D_MODEL, IN_COLS)
        return jnp.pad(whole, ((0, 0), (0, Z_COLS - IN_COLS)))

    def gather_first(self):
        self.mid = [self.shards[n] for n in ("w_glu", "w_out", "w_down")]
        return _plan_gather_ici(self.mid + [self.shards["w_up"]], [False, False, False, True], whole=[self.conv_w],
                                bands=[WHOLE_HALF] * 3 + [(0, 1, 4)])

    def gather_second(self, landed):
        self.g_cw = landed[4]
        return _both(_plan_gather_d2d(list(landed[:3]), self.mid),
                     _plan_gather_ici([self.shards["w_up"]], [True], bands=[(1, 3, 4)], into=[landed[3]]))

    def weights(self, gathered):
        g_glu, g_out, g_down = gathered[:3]
        own = self._own
        return (own(g_glu, self.mid[0]).reshape(SSM_W, SSM_W), own(g_out, self.mid[1]).reshape(D_MODEL, D_MODEL),
                own(g_down, self.mid[2]).reshape(D_FF, D_MODEL),
                own(self.g_cw, self.conv_w).transpose(1, 0, 2).reshape(3, 2 * D_FF))

    def gather_third(self, gathered):
        return _plan_gather_d2d([gathered[3]], [self.shards["w_up"]])

    def w_up(self, passed):
        return _place_cols(passed[0], self.shards["w_up"], self.place)

    def swap(self, d_w_down, d_w_up, d_w_glu, d_w_out):
        self.early = [_halves(d_w_down), d_w_up, _halves(d_w_glu), _halves(d_w_out)]
        return _plan_swap(self.early)

    def scatter(self, landed):
        self.early_sums = [_add_half(g, l, self.place, name="add_" + n)
                           for g, l, n in zip(self.early, landed, ("w_down", "w_up", "w_glu", "w_out"))]
        return _plan_scatter(self.early_sums, [None, (UP_COLS, UP_COLS), None, None])

    def swap_in(self, d_w_in):
        self.d_in = d_w_in
        return _plan_swap([d_w_in])

    def scatter_in(self, landed):
        self.sum_in = _add_half(self.d_in, landed[0], self.place, name="add_w_in")
        return _plan_scatter([self.sum_in], [(IN_STRIDE, IN_WINDOW)])

    def reduce(self, lands):
        early_lands, (land_in,) = lands
        sum_in = self.sum_in
        es, el = self.early_sums, early_lands
        todo = [(sum_in, land_in, "w_in", LANES, IN_STRIDE), (es[2], el[2], "w_glu", SSM_W, 0),
                (es[3], el[3], "w_out", D_MODEL, 0), (es[1], el[1], "w_up", UP_COLS, UP_COLS),
                (es[0], el[0], "w_down", D_MODEL, 0)]
        reds = _run_hosted(_plan_join([_sum_chips(s, l, self.place, name="sum_" + n, tc=tc, window_stride=st)
                                       for s, l, n, tc, st in todo]), name="join_halves")
        g_big = dict(zip(("w_in", "w_glu", "w_out", "w_up", "w_down"), reds))
        g_big["w_in"] = lax.dynamic_slice_in_dim(reds[0], 2 * self.chip, IN_COLS // N_CHIPS, axis=1)
        return g_big


def _local_step(x, tgt, p, comm):
    s = x.shape[0]
    row = lambda v: v.reshape(1, -1)
    g_mix, g_ffn = row(p["g_mix"]), row(p["g_ffn"])
    g_att, g_ssm, b_glu, conv_b = row(p["g_attn_out"]), row(p["g_ssm_out"]), row(p["b_glu"]), row(p["conv_b"])
    gq = row(jnp.tile(p["g_q"], HEADS))
    gk = row(jnp.tile(p["g_k"], HEADS))
    bf = row(jnp.pad(p["b_f"], (0, LANES - HEADS)))
    gg = jnp.kron(jnp.eye(HEADS, dtype=F32), jnp.ones((HEAD_DIM, HEAD_DIM), F32)).astype(BF16)
    dsk = row(p["d_skip"])

    rep = lambda a: jnp.repeat(a, SSM_GROUP, axis=0)
    lr, li = rep(p["lambda_re"]), rep(p["lambda_im"])
    ls = rep(jnp.broadcast_to(p["log_step"][:, None], (SSM_GROUPS, SSM_STATE)))
    bt_re = p["b_re"].transpose(0, 2, 1).reshape(_PARAM_SHAPE)
    bt_im = p["b_im"].transpose(0, 2, 1).reshape(_PARAM_SHAPE)
    a_re_rep, a_im_rep, bb_re, bb_im = _ssm_params(lr, li, ls, bt_re, bt_im)
    ar = a_re_rep[::SSM_GROUP].reshape(SSM_CHUNKS, 1, CHUNK_S)
    ai = a_im_rep[::SSM_GROUP].reshape(SSM_CHUNKS, 1, CHUNK_S)
    chunked = lambda t: t.reshape(SSM_CHUNKS, SSM_GROUPS // SSM_CHUNKS, SSM_GROUP, SSM_STATE)
    bbr = _block_diag(chunked(bb_re)).astype(BF16)
    bbi = _block_diag(chunked(bb_im)).astype(BF16)
    to_cc = lambda c: _block_diag(chunked(c).transpose(0, 1, 3, 2)).astype(BF16)
    ccr, cci = to_cc(p["c_re"]), to_cc(p["c_im"])

    w_in_r = comm.w_in()
    hb, z = _in_proj(x, g_mix, w_in_r)
    qh, kh, vh, ub, uf, c128 = _attn_prep(z, gq, gk, bf, gg)
    crow = c128[:, :HEADS].T.reshape(HEADS, 1, s)
    (oh, lse), landed = _attn_fwd(qh, kh, vh, crow, comm.gather_first())
    (xr, xi, y), gathered = _ssm_fwd(ub, uf, bbr, bbi, ar, ai, ccr, cci, dsk, comm.gather_second(landed))
    w_glu_b, w_out_b, w_down_b, conv_w_full = comm.weights(gathered)
    (x1, mixb, h2b), passed = _mix_out(y, oh, x, w_glu_b, b_glu, g_att, g_ssm, w_out_b, g_ffn, comm.gather_third(gathered))
    w_up_b = comm.w_up(passed)
    up = _mm(h2b, w_up_b, name="ffn_up", tm=1024, tn=1408, tk=1024)
    act = _conv_act(up, conv_w_full, conv_b)
    dy, dyb, loss_blk = _down_loss(act, w_down_b, x1, tgt)

    d_w_down = _mm(act, dyb, ta=True, name="d_w_down", tm=1408, tn=1024, tk=2048)
    dact = _mm(dyb, w_down_b, tb=True, name="d_act", tm=1024, tn=1408, tk=1024)
    dupb, dcw = _conv_act_bwd(up, dact, conv_w_full, conv_b)
    d_w_up = _mm(h2b, dupb, ta=True, b_parts=2, name="d_w_up", tm=1024, tn=1408, tk=2048)
    dh2 = _mm(dupb, w_up_b, tb=True, a_parts=2, name="d_h2", tm=1024, tn=1024, tk=1408)
    dx1, dx1b, doh, dys, d_w_glu, d_g_ffn, d_g_att, d_g_ssm, d_b_glu = _mix_bwd(
        dy, dh2, x1, g_ffn, w_out_b, y, oh, w_glu_b, b_glu, g_att, g_ssm)
    d_w_out = _mm(mixb, dx1b, ta=True, name="d_w_out", tm=1024, tn=1024, tk=2048)
    (du, dbbr, dbbi, dccr, dcci, dar, dai, dd), swapped = _ssm_bwd(dys, uf, ub, xr, xi, bbr, bbi, ar, ai, ccr, cci, dsk,
                                                                comm.swap(d_w_down, d_w_up, d_w_glu, d_w_out))
    (dqh, dkh, dvh, dcrow), early_lands = _attn_bwd(qh, kh, vh, crow, lse, doh, comm.scatter(swapped))
    dc128 = jnp.pad(dcrow.reshape(HEADS, s).T, ((0, 0), (0, LANES - HEADS)))
    dzb, d_gq, d_gk, d_bf = _prep_bwd(z, dqh, dkh, dvh, du, dc128, gq, gk, bf, gg)
    d_w_in_r = _mm(hb, dzb, ta=True, name="d_w_in", tm=512, tn=Z_COLS, tk=2048)
    dh, swapped_in = _mm(dzb, w_in_r, tb=True, name="d_h", tm=1024, tn=1024, tk=Z_COLS, carry=True,
                         hosted=comm.swap_in(d_w_in_r))
    (dx, d_g_mix), land_in = _in_norm_bwd(x, g_mix, dh, dx1, comm.scatter_in(swapped_in))

    unchunk = lambda t: t.reshape(_PARAM_SHAPE)
    dbb_re = unchunk(_diag_blocks(dbbr, SSM_GROUP, SSM_STATE))
    dbb_im = unchunk(_diag_blocks(dbbi, SSM_GROUP, SSM_STATE))
    first_row = (jnp.arange(_PARAM_SHAPE[0]) % SSM_GROUP == 0)[:, None]
    da_re = jnp.where(first_row, rep(dar.reshape(SSM_GROUPS, SSM_STATE)), 0.0)
    da_im = jnp.where(first_row, rep(dai.reshape(SSM_GROUPS, SSM_STATE)), 0.0)
    expand_t = (jnp.arange(SSM_GROUPS)[:, None] == (jnp.arange(_PARAM_SHAPE[0]) // SSM_GROUP)[None, :]).astype(BF16)
    d_lr, d_li, d_ls, d_bt_re, d_bt_im = _ssm_params_bwd(lr, li, ls, bt_re, bt_im, da_re, da_im, dbb_re, dbb_im, expand_t)
    from_bt = lambda t: t.reshape(SSM_GROUPS, SSM_GROUP, SSM_STATE).transpose(0, 2, 1)
    from_cc = lambda t: _diag_blocks(t, SSM_STATE, SSM_GROUP).transpose(0, 1, 3, 2).reshape(SSM_GROUPS, SSM_GROUP, SSM_STATE)

    small = {
        "g_mix": d_g_mix, "b_f": d_bf[0, :HEADS], "g_q": d_gq.reshape(HEADS, HEAD_DIM).sum(0),
        "g_k": d_gk.reshape(HEADS, HEAD_DIM).sum(0), "lambda_re": d_lr, "lambda_im": d_li, "log_step": d_ls,
        "b_re": from_bt(d_bt_re), "b_im": from_bt(d_bt_im), "c_re": from_cc(dccr), "c_im": from_cc(dcci),
        "d_skip": dd, "b_glu": d_b_glu, "g_attn_out": d_g_att, "g_ssm_out": d_g_ssm, "g_ffn": d_g_ffn,
        "conv_b": dcw[:, 3],
    }
    big = {"w_in": d_w_in_r, "w_glu": d_w_glu, "w_out": d_w_out, "w_up": d_w_up, "w_down": d_w_down}
    return loss_blk[0, 0], dx, big, small, dcw[:, 0:3].transpose(1, 0, 2).reshape(3, 2 * D_FF), (early_lands, land_in)


def kernel(x, g_mix, w_in, b_f, g_q, g_k, lambda_re, lambda_im, log_step, b_re, b_im, c_re, c_im, d_skip, w_glu, b_glu, g_attn_out, g_ssm_out, w_out, g_ffn, w_up, conv_w, conv_b, w_down, loss_target, m_g_mix, m_w_in, m_b_f, m_g_q, m_g_k, m_lambda_re, m_lambda_im, m_log_step, m_b_re, m_b_im, m_c_re, m_c_im, m_d_skip, m_w_glu, m_b_glu, m_g_attn_out, m_g_ssm_out, m_w_out, m_g_ffn, m_w_up, m_conv_w, m_conv_b, m_w_down, v_g_mix, v_w_in, v_b_f, v_g_q, v_g_k, v_lambda_re, v_lambda_im, v_log_step, v_b_re, v_b_im, v_c_re, v_c_im, v_d_skip, v_w_glu, v_b_glu, v_g_attn_out, v_g_ssm_out, v_w_out, v_g_ffn, v_w_up, v_conv_w, v_conv_b, v_w_down):
    args = dict(locals())
    order = ["g_mix", "w_in", "b_f", "g_q", "g_k", "lambda_re", "lambda_im", "log_step", "b_re", "b_im", "c_re", "c_im",
             "d_skip", "w_glu", "b_glu", "g_attn_out", "g_ssm_out", "w_out", "g_ffn", "w_up", "conv_w", "conv_b", "w_down"]
    comm = _MeshComm(args)
    chip = comm.chip
    loss_part, dx, big, small, d_conv_w, lands = _local_step(x[0], loss_target[0], args, comm)

    g_big = comm.reduce(lands)

    small_names = [n for n, _ in _SMALL]
    small_shapes = [sh for _, sh in _SMALL]
    gsum = _allreduce_small(_pack_small([small[n] for n in small_names] + [d_conv_w, loss_part]))
    g_small = _unpack_small(gsum, small_shapes + [(3, 2 * D_FF), ()])
    loss = g_small[-1]
    g_conv_w = lax.dynamic_slice_in_dim(g_small[-2], chip * (2 * D_FF // N_CHIPS), 2 * D_FF // N_CHIPS, axis=1)
    g_small = dict(zip(small_names, g_small[:-2]))

    grad, delta, new_m, new_v = {}, {}, {}, {}
    for n in ("w_in", "w_glu", "w_out", "w_up", "w_down"):
        grad[n] = g_big[n]
        delta[n], new_m[n], new_v[n] = _adamw(args[n], g_big[n], args["m_" + n], args["v_" + n], name="adamw_" + n)
    grad["conv_w"] = g_conv_w
    delta["conv_w"], new_m["conv_w"], new_v["conv_w"] = _adamw(conv_w, g_conv_w, m_conv_w, v_conv_w, name="adamw_conv_w")
    stepped = _adamw_small([args[n] for n in small_names], [g_small[n] for n in small_names],
                           [args["m_" + n] for n in small_names], [args["v_" + n] for n in small_names])
    for i, n in enumerate(small_names):
        grad[n] = g_small[n]
        delta[n], new_m[n], new_v[n] = stepped[3 * i:3 * i + 3]

    return (loss, dx[None], *[grad[n] for n in order], *[delta[n] for n in order], *[new_m[n] for n in order],
            *[new_v[n] for n in order])
```

```python
import math

import jax
import jax.numpy as jnp
from jax import lax
from jax.experimental import pallas as pl
from jax.experimental.pallas import tpu as pltpu

F32 = jnp.float32
BF16 = jnp.bfloat16

D_MODEL = 1024
HEADS = 8
HEAD_DIM = 64
ATTN_W = 512
SSM_W = 512
SSM_GROUPS = 32
SSM_GROUP = 16
SSM_STATE = 64
N_STATE = SSM_GROUPS * SSM_STATE
D_FF = 2816
IN_COLS = 2056
Z_COLS = 2176
F_COL0 = 1536
U_COL0 = 1544
EPS = 1e-6
NEG_INF = -1e30
N_CHIPS = 4
LANES = 128
SUBLANES = 8
SSM_CHUNKS = 2
CHUNK_U = SSM_W // SSM_CHUNKS
CHUNK_S = N_STATE // SSM_CHUNKS
HEADS_PER_STEP = 4
STRIP = 128
N_STRIPS = D_FF // STRIP

ADAM_LR = 0.001
ADAM_B1 = 0.9
ADAM_B2 = 0.999
ADAM_EPS = 1e-08
ADAM_WD = 0.01
ADAM_STEP = 10

VMEM_LIMIT = 56 * 1024 * 1024
MESH = pl.DeviceIdType.MESH


def _pallas(body, **kw):
    return pl.pallas_call(body, **kw)


def _pcall(body, *, name, out_shape, in_specs, out_specs, grid=(), scratch_shapes=(), dims=None):
    params = pltpu.CompilerParams(dimension_semantics=dims, vmem_limit_bytes=VMEM_LIMIT)
    return _pallas(body, name=name, grid=grid, in_specs=in_specs, out_specs=out_specs,
                   out_shape=out_shape, scratch_shapes=scratch_shapes, compiler_params=params)


def _sds(shape, dtype=F32):
    return jax.ShapeDtypeStruct(shape, dtype)


def _dot(a, b):
    return jnp.dot(a, b, preferred_element_type=F32)


def _dot_nt(a, b):
    return lax.dot_general(a, b, (((1,), (1,)), ((), ())), preferred_element_type=F32)


def _dot_tn(a, b):
    return lax.dot_general(a, b, (((0,), (0,)), ((), ())), preferred_element_type=F32)


def _split3(x):
    hi = x.astype(BF16)
    r = x - hi.astype(F32)
    mid = r.astype(BF16)
    lo = (r - mid.astype(F32)).astype(BF16)
    return hi, mid, lo


def _dot_exact_r(x, m01):
    hi, mid, lo = _split3(x)
    return _dot(hi, m01) + _dot(mid, m01) + _dot(lo, m01)


def _dot_exact_l(m01, x):
    hi, mid, lo = _split3(x)
    return _dot(m01, hi) + _dot(m01, mid) + _dot(m01, lo)


def _sigmoid(x):
    return 1.0 / (1.0 + jnp.exp(-x))


def _rms(x, g):
    r = lax.rsqrt(jnp.mean(x * x, axis=-1, keepdims=True) + EPS)
    return x * r * g


def _rms_bwd(x, g, dy):
    r = lax.rsqrt(jnp.mean(x * x, axis=-1, keepdims=True) + EPS)
    w = dy * g
    dx = r * w - x * (r * r * r) * jnp.mean(w * x, axis=-1, keepdims=True)
    dg = jnp.sum(dy * x * r, axis=0, keepdims=True)
    return dx, dg


_GELU_K = math.sqrt(2.0 / math.pi)
_GELU_C = 0.044715


def _gelu(y):
    return y * (0.5 * (1.0 + jnp.tanh(_GELU_K * (y + _GELU_C * (y * y * y)))))


def _gelu_grad(y):
    t = jnp.tanh(_GELU_K * (y + _GELU_C * (y * y * y)))
    return 0.5 * (1.0 + t) + 0.5 * y * (1.0 - t * t) * (_GELU_K * (1.0 + 3.0 * _GELU_C * y * y))


def _tile(n, pref):
    if n <= pref:
        return n
    divs = [t for t in range(LANES, n + 1, LANES) if n % t == 0]
    below = [t for t in divs if t <= pref]
    if below and 2 * below[-1] >= pref:
        return below[-1]
    above = [t for t in divs if t > pref]
    return above[0] if above else n


def _row_tile(s):
    return min(256, s)


def _mm(a, b, *, name, tm, tn, tk, ta=False, tb=False, a_parts=1, b_parts=1, carry=False, hosted=None):
    if a_parts > 1:
        m, kk = a.shape[1], a.shape[2] * a_parts
    elif ta:
        kk, m = a.shape
    else:
        m, kk = a.shape
    if b_parts > 1:
        n = b.shape[2] * b_parts
    else:
        n = b.shape[0] if tb else b.shape[1]
    tm, tn, tk = _tile(m, tm), _tile(n // b_parts, tn), _tile(kk // a_parts, tk)
    k_per, n_per = kk // a_parts // tk, n // b_parts // tn

    def body(a_ref, b_ref, o_ref):
        k = pl.program_id(2)
        if ta:
            part = _dot_tn(a_ref[...], b_ref[...])
        elif tb:
            part = _dot_nt(a_ref[...], b_ref[...])
        else:
            part = _dot(a_ref[...], b_ref[...])

        @pl.when(k == 0)
        def _():
            o_ref[...] = part

        @pl.when(k > 0)
        def _():
            o_ref[...] += part

    if a_parts > 1:
        a_spec = pl.BlockSpec((None, tm, tk), lambda i, j, k: (k // k_per, i, k % k_per))
    else:
        a_spec = pl.BlockSpec((tk, tm), lambda i, j, k: (k, i)) if ta else pl.BlockSpec((tm, tk), lambda i, j, k: (i, k))
    if b_parts > 1:
        b_spec = pl.BlockSpec((None, tk, tn), lambda i, j, k: (j // n_per, k, j % n_per))
    else:
        b_spec = pl.BlockSpec((tn, tk), lambda i, j, k: (j, k)) if tb else pl.BlockSpec((tk, tn), lambda i, j, k: (k, j))
    grid = (m // tm, n // tn, kk // tk)
    at = lambda step: (lambda: jnp.logical_and(jnp.logical_and(pl.program_id(0) == step[0], pl.program_id(1) == step[1]),
                                               pl.program_id(2) == step[2]))
    (out,), carried = _host_pcall(body, hosted, at((0, 0, 0)), at(tuple(g - 1 for g in grid)), n_in=2, n_out=1, n_scratch=0,
                                  name=name, grid=grid, in_specs=[a_spec, b_spec],
                                  out_specs=[pl.BlockSpec((tm, tn), lambda i, j, k: (i, j))], out_shape=[_sds((m, n))],
                                  scratch_shapes=[], dims=("parallel", "parallel", "arbitrary"), operands=(a, b))
    return (out, carried) if carry else out


def _in_proj(x, g_mix, w_in_r):
    s = x.shape[0]
    tm = _row_tile(s)

    def body(x_ref, g_ref, w_ref, h_ref, z_ref):
        h = _rms(x_ref[...], g_ref[...]).astype(BF16)
        h_ref[...] = h
        z_ref[...] = _dot(h, w_ref[...])

    return _pcall(body, name="in_proj", grid=(s // tm,),
                  in_specs=[pl.BlockSpec((tm, D_MODEL), lambda i: (i, 0)), pl.BlockSpec((1, D_MODEL), lambda i: (0, 0)),
                            pl.BlockSpec((D_MODEL, Z_COLS), lambda i: (0, 0))],
                  out_specs=[pl.BlockSpec((tm, D_MODEL), lambda i: (i, 0)), pl.BlockSpec((tm, Z_COLS), lambda i: (i, 0))],
                  out_shape=[_sds((s, D_MODEL), BF16), _sds((s, Z_COLS))], dims=("parallel",))(x, g_mix, w_in_r)


def _split_heads(ref, val):
    for h in range(HEADS):
        ref[h] = val[:, h * HEAD_DIM:(h + 1) * HEAD_DIM].astype(ref.dtype)


def _merge_heads(ref):
    return jnp.concatenate([ref[h].astype(F32) for h in range(HEADS)], axis=-1)


def _forget_logits(z_ref, bf_ref):
    fl = z_ref[:, F_COL0:F_COL0 + LANES] + bf_ref[...]
    return jnp.where(lax.broadcasted_iota(jnp.int32, fl.shape, 1) < HEADS, fl, 0.0)


def _attn_prep(z, gq, gk, bf, gg):
    s = z.shape[0]
    tm = _row_tile(s)

    def body(z_ref, gq_ref, gk_ref, bf_ref, gg_ref, qn_ref, kn_ref, vb_ref, ub_ref, uf_ref, c_ref, carry_ref):
        i = pl.program_id(0)

        @pl.when(i == 0)
        def _():
            carry_ref[...] = jnp.zeros_like(carry_ref)

        gg_m = gg_ref[...]

        def head_norm(t, g):
            ssq = _dot_exact_r(t * t, gg_m)
            return t * lax.rsqrt(ssq * (1.0 / HEAD_DIM) + EPS) * g

        _split_heads(qn_ref, head_norm(z_ref[:, 0:ATTN_W], gq_ref[...]))
        _split_heads(kn_ref, head_norm(z_ref[:, ATTN_W:2 * ATTN_W], gk_ref[...]))
        _split_heads(vb_ref, z_ref[:, 2 * ATTN_W:3 * ATTN_W])
        u = z_ref[:, U_COL0:U_COL0 + SSM_W]
        uf_ref[...] = u
        ub_ref[...] = u.astype(BF16)
        fl = _forget_logits(z_ref, bf_ref)
        lf = jnp.minimum(fl, 0.0) - jnp.log1p(jnp.exp(-jnp.abs(fl)))
        row = lax.broadcasted_iota(jnp.int32, (tm, tm), 0)
        col = lax.broadcasted_iota(jnp.int32, (tm, tm), 1)
        tri = (row >= col).astype(BF16)
        c = _dot_exact_l(tri, lf) + carry_ref[...]
        c_ref[...] = c
        carry_ref[...] = c[tm - 1:tm, :]

    row_spec = lambda w: pl.BlockSpec((tm, w), lambda i: (i, 0))
    const = lambda shape: pl.BlockSpec(shape, lambda i: (0, 0))
    heads = pl.BlockSpec((HEADS, tm, HEAD_DIM), lambda i: (0, i, 0))
    return _pcall(body, name="attn_prep", grid=(s // tm,),
                  in_specs=[row_spec(Z_COLS), const((1, ATTN_W)), const((1, ATTN_W)), const((1, LANES)), const((ATTN_W, ATTN_W))],
                  out_specs=[heads] * 3 + [row_spec(SSM_W), row_spec(SSM_W), row_spec(LANES)],
                  out_shape=[_sds((HEADS, s, HEAD_DIM), BF16)] * 3 + [_sds((s, SSM_W), BF16), _sds((s, SSM_W)), _sds((s, LANES))],
                  scratch_shapes=[pltpu.VMEM((1, LANES), F32)], dims=("arbitrary",))(z, gq, gk, bf, gg)


def _attn_fwd(qh, kh, vh, crow, hosted=None):
    _, s, _ = qh.shape
    tq = _row_tile(s)
    scale = HEAD_DIM ** -0.5

    hp = HEADS
    nq = s // tq
    fold = lambda t, op: op(t[:, :tq // 2], t[:, tq // 2:])

    def body(q_ref, k_ref, v_ref, c_ref, o_ref, lse_ref, s_s):
        i = pl.program_id(1)

        def first(j, ms, diagonal):
            off = pl.multiple_of(j * tq, tq)
            out = []
            for hh in range(hp):
                sc = _dot_nt(q_ref[hh], k_ref[hh, pl.ds(off, tq), :]) * scale - c_ref[hh, :, pl.ds(off, tq)]
                if diagonal:
                    causal = lax.broadcasted_iota(jnp.int32, (tq, tq), 1) <= lax.broadcasted_iota(jnp.int32, (tq, tq), 0)
                    sc = jnp.where(causal, sc, NEG_INF)
                s_s[hh, j] = sc
                out.append(jnp.maximum(ms[hh], fold(sc, jnp.maximum)))
            return tuple(out)

        ms = lax.fori_loop(0, i, lambda j, c: first(j, c, False), (jnp.full((tq, tq // 2), NEG_INF, F32),) * hp)
        ms = [jnp.max(t, axis=-1, keepdims=True) for t in first(i, ms, True)]

        def second(j, carry):
            rows = pl.ds(pl.multiple_of(j * tq, tq), tq)
            out = []
            for hh in range(hp):
                ls, acc = carry[hh]
                p = jnp.exp(s_s[hh, j] - ms[hh])
                out.append((ls + fold(p, jnp.add), acc + _dot(p.astype(BF16), v_ref[hh, rows, :])))
            return tuple(out)

        zero = (jnp.zeros((tq, tq // 2), F32), jnp.zeros((tq, HEAD_DIM), F32))
        for hh, (ls, acc) in enumerate(lax.fori_loop(0, i + 1, second, (zero,) * hp)):
            l = jnp.sum(ls, axis=-1, keepdims=True)
            o_ref[hh] = acc / l
            lse_ref[hh] = ms[hh] + jnp.log(l)

    blk = pl.BlockSpec((hp, tq, HEAD_DIM), lambda h, i: (h, i, 0))
    full = pl.BlockSpec((hp, s, HEAD_DIM), lambda h, i: (h, 0, 0))
    nh = HEADS // hp
    first = lambda: jnp.logical_and(pl.program_id(0) == 0, pl.program_id(1) == 0)
    last = lambda: jnp.logical_and(pl.program_id(0) == nh - 1, pl.program_id(1) == nq - 1)
    return _host_pcall(body, hosted, first, last, n_in=4, n_out=2, n_scratch=1, name="attn_fwd", grid=(nh, nq),
                       in_specs=[blk, full, full, pl.BlockSpec((hp, 1, s), lambda h, i: (h, 0, 0))],
                       out_specs=[blk, pl.BlockSpec((hp, tq, 1), lambda h, i: (h, i, 0))],
                       out_shape=[_sds((HEADS, s, HEAD_DIM)), _sds((HEADS, s, 1))],
                       scratch_shapes=[pltpu.VMEM((hp, nq, tq, tq), F32)],
                       dims=("parallel", "parallel"), operands=(qh, kh, vh, crow))


def _ssm_param_fn(lr, li, ls, br, bi):
    step = jnp.exp(ls)
    er = jnp.exp(lr * step)
    ab_re = er * jnp.cos(li * step)
    ab_im = er * jnp.sin(li * step)
    num_re = ab_re - 1.0
    num_im = ab_im
    den = lr * lr + li * li
    f_re = (num_re * lr + num_im * li) / den
    f_im = (num_im * lr - num_re * li) / den
    bb_re = f_re * br - f_im * bi
    bb_im = f_re * bi + f_im * br
    return ab_re, ab_im, bb_re, bb_im


_PARAM_SHAPE = (SSM_GROUPS * SSM_GROUP, SSM_STATE)


def _ssm_params(lr, li, ls, br, bi):
    def body(lr_ref, li_ref, ls_ref, br_ref, bi_ref, ar_ref, ai_ref, bbr_ref, bbi_ref):
        ar, ai, bbr, bbi = _ssm_param_fn(lr_ref[...], li_ref[...], ls_ref[...], br_ref[...], bi_ref[...])
        ar_ref[...] = ar
        ai_ref[...] = ai
        bbr_ref[...] = bbr
        bbi_ref[...] = bbi

    spec = pl.BlockSpec(_PARAM_SHAPE, lambda: (0, 0))
    return _pcall(body, name="ssm_params", in_specs=[spec] * 5, out_specs=[spec] * 4,
                  out_shape=[_sds(_PARAM_SHAPE)] * 4)(lr, li, ls, br, bi)


def _ssm_params_bwd(lr, li, ls, br, bi, dar, dai, dbbr, dbbi, expand_t):
    def body(lr_ref, li_ref, ls_ref, br_ref, bi_ref, dar_ref, dai_ref, dbbr_ref, dbbi_ref, et_ref,
             dlr_ref, dli_ref, dls_ref, dbr_ref, dbi_ref):
        _, vjp = jax.vjp(_ssm_param_fn, lr_ref[...], li_ref[...], ls_ref[...], br_ref[...], bi_ref[...])
        dlr, dli, dls, dbr, dbi = vjp((dar_ref[...], dai_ref[...], dbbr_ref[...], dbbi_ref[...]))
        et = et_ref[...]
        dlr_ref[...] = _dot_exact_l(et, dlr)
        dli_ref[...] = _dot_exact_l(et, dli)
        dls_ref[...] = jnp.sum(_dot_exact_l(et, dls), axis=-1, keepdims=True)
        dbr_ref[...] = dbr
        dbi_ref[...] = dbi

    spec = pl.BlockSpec(_PARAM_SHAPE, lambda: (0, 0))
    gspec = pl.BlockSpec((SSM_GROUPS, SSM_STATE), lambda: (0, 0))
    return _pcall(body, name="ssm_params_bwd",
                  in_specs=[spec] * 9 + [pl.BlockSpec((SSM_GROUPS, _PARAM_SHAPE[0]), lambda: (0, 0))],
                  out_specs=[gspec, gspec, pl.BlockSpec((SSM_GROUPS, 1), lambda: (0, 0)), spec, spec],
                  out_shape=[_sds((SSM_GROUPS, SSM_STATE))] * 2 + [_sds((SSM_GROUPS, 1))] + [_sds(_PARAM_SHAPE)] * 2,
                  )(lr, li, ls, br, bi, dar, dai, dbbr, dbbi, expand_t)


def _cmul(ar, ai, br, bi):
    return ar * br - ai * bi, ar * bi + ai * br


def _scan_consts(ar, ai, width, reverse):
    row = lax.broadcasted_iota(jnp.int32, (SUBLANES, width), 0)
    pw = [(ar, ai)]
    for _ in range(SUBLANES - 1):
        pw.append(_cmul(pw[-1][0], pw[-1][1], ar, ai))
    steps = []
    for d in (1, 2, 4):
        keep = (row < SUBLANES - d) if reverse else (row >= d)
        steps.append((d, jnp.where(keep, pw[d - 1][0], 0.0), jnp.where(keep, pw[d - 1][1], 0.0)))
    pr = jnp.zeros((SUBLANES, width), F32)
    pi = jnp.zeros((SUBLANES, width), F32)
    for r in range(SUBLANES):
        e = (SUBLANES - r) if reverse else (r + 1)
        pr = jnp.where(row == r, pw[e - 1][0], pr)
        pi = jnp.where(row == r, pw[e - 1][1], pi)
    return steps, pr, pi


def _scan_tile(xr, xi, cr, ci, consts, reverse):
    steps, pr, pi = consts
    for d, mr, mi in steps:
        sh = (SUBLANES - d) if reverse else d
        sr = pltpu.roll(xr, sh, 0)
        si = pltpu.roll(xi, sh, 0)
        xr, xi = xr + mr * sr - mi * si, xi + mr * si + mi * sr
    return xr + pr * cr - pi * ci, xi + pr * ci + pi * cr


def _ssm_fwd(ub, uf, bbr, bbi, ar, ai, ccr, cci, dsk, hosted=None):
    s = ub.shape[0]
    tm = _row_tile(s)
    nt = tm // SUBLANES

    def body(ub_ref, u_ref, bbr_ref, bbi_ref, ar_ref, ai_ref, ccr_ref, cci_ref, dsk_ref,
             xr_ref, xi_ref, y_ref, cr_s, ci_s):
        i = pl.program_id(1)

        @pl.when(i == 0)
        def _():
            cr_s[...] = jnp.zeros_like(cr_s)
            ci_s[...] = jnp.zeros_like(ci_s)

        u_b = ub_ref[...]
        xr_ref[...] = _dot(u_b, bbr_ref[0])
        xi_ref[...] = _dot(u_b, bbi_ref[0])
        consts = _scan_consts(ar_ref[0], ai_ref[0], CHUNK_S, False)

        def tile(k, carry):
            cr, ci = carry
            sl = pl.ds(pl.multiple_of(k * SUBLANES, SUBLANES), SUBLANES)
            xr, xi = _scan_tile(xr_ref[sl, :], xi_ref[sl, :], cr, ci, consts, False)
            xr_ref[sl, :] = xr
            xi_ref[sl, :] = xi
            return xr[SUBLANES - 1:SUBLANES, :], xi[SUBLANES - 1:SUBLANES, :]

        cr, ci = lax.fori_loop(0, nt, tile, (cr_s[...], ci_s[...]))
        cr_s[...] = cr
        ci_s[...] = ci
        y_ref[...] = (_dot(xr_ref[...].astype(BF16), ccr_ref[0]) - _dot(xi_ref[...].astype(BF16), cci_ref[0])
                      + dsk_ref[...] * u_ref[...])

    wspec = lambda a, b: pl.BlockSpec((1, a, b), lambda j, i: (j, 0, 0))
    nb = s // tm
    first = lambda: jnp.logical_and(pl.program_id(0) == 0, pl.program_id(1) == 0)
    last = lambda: jnp.logical_and(pl.program_id(0) == SSM_CHUNKS - 1, pl.program_id(1) == nb - 1)
    return _host_pcall(
        body, hosted, first, last, n_in=9, n_out=3, n_scratch=2, name="ssm_fwd", grid=(SSM_CHUNKS, nb),
        in_specs=[pl.BlockSpec((tm, CHUNK_U), lambda j, i: (i, j)),
                  pl.BlockSpec((tm, CHUNK_U), lambda j, i: (i, j)),
                  wspec(CHUNK_U, CHUNK_S), wspec(CHUNK_U, CHUNK_S), wspec(1, CHUNK_S), wspec(1, CHUNK_S),
                  wspec(CHUNK_S, CHUNK_U), wspec(CHUNK_S, CHUNK_U),
                  pl.BlockSpec((1, CHUNK_U), lambda j, i: (0, j))],
        out_specs=[pl.BlockSpec((tm, CHUNK_S), lambda j, i: (i, j)), pl.BlockSpec((tm, CHUNK_S), lambda j, i: (i, j)),
                   pl.BlockSpec((tm, CHUNK_U), lambda j, i: (i, j))],
        out_shape=[_sds((s, N_STATE)), _sds((s, N_STATE)), _sds((s, SSM_W))],
        scratch_shapes=[pltpu.VMEM((1, CHUNK_S), F32)] * 2,
        dims=("parallel", "arbitrary"), operands=(ub, uf, bbr, bbi, ar, ai, ccr, cci, dsk))


def _ssm_glu(y, w_glu, b_glu):
    ge = _gelu(y)
    sg = _sigmoid(_dot(ge.astype(BF16), w_glu) + b_glu)
    return ge, sg


def _mix_out(y, att, x, w_glu, b_glu, g_att, g_ssm, w_out, g_ffn, hosted=None):
    s = x.shape[0]
    tm = _row_tile(s)

    def body(y_ref, att_ref, x_ref, wg_ref, bg_ref, ga_ref, gs_ref, wo_ref, gf_ref, x1_ref, mix_ref, h2_ref):
        ge, sg = _ssm_glu(y_ref[...], wg_ref[...], bg_ref[...])
        ms = _rms(ge * sg, gs_ref[...]).astype(BF16)
        ma = _rms(_merge_heads(att_ref), ga_ref[...]).astype(BF16)
        mix_ref[:, 0:ATTN_W] = ma
        mix_ref[:, ATTN_W:D_MODEL] = ms
        x1 = x_ref[...] + (_dot(ma, wo_ref[0:ATTN_W, :]) + _dot(ms, wo_ref[ATTN_W:D_MODEL, :]))
        x1_ref[...] = x1
        h2_ref[...] = _rms(x1, gf_ref[...]).astype(BF16)

    row = lambda w: pl.BlockSpec((tm, w), lambda i: (i, 0))
    const = lambda a, b: pl.BlockSpec((a, b), lambda i: (0, 0))
    nb = s // tm
    return _host_pcall(body, hosted, lambda: pl.program_id(0) == 0, lambda: pl.program_id(0) == nb - 1,
                       n_in=9, n_out=3, n_scratch=0, name="mix_out", grid=(nb,),
                       in_specs=[row(SSM_W), pl.BlockSpec((HEADS, tm, HEAD_DIM), lambda i: (0, i, 0)), row(D_MODEL),
                                 const(SSM_W, SSM_W), const(1, SSM_W),
                                 const(1, ATTN_W), const(1, SSM_W), const(D_MODEL, D_MODEL), const(1, D_MODEL)],
                       out_specs=[row(D_MODEL)] * 3,
                       out_shape=[_sds((s, D_MODEL)), _sds((s, D_MODEL), BF16), _sds((s, D_MODEL), BF16)],
                       scratch_shapes=[], dims=("parallel",), operands=(y, att, x, w_glu, b_glu, g_att, g_ssm, w_out, g_ffn))


CONV_CHUNK = 64


def _conv_rows(pad_ref, w, b, r0, n):
    y = b + pad_ref[pl.ds(r0 + SUBLANES - 2, n), :] * w[0:1, :]
    y = y + pad_ref[pl.ds(r0 + SUBLANES - 1, n), :] * w[1:2, :]
    return y + pad_ref[pl.ds(r0 + SUBLANES, n), :] * w[2:3, :]


def _fill_front_pad(pad_ref, strip_ref, s):
    pad_ref[0:SUBLANES, :] = jnp.zeros((SUBLANES, STRIP), F32)
    for r0 in range(0, s, CONV_CHUNK):
        pad_ref[pl.ds(SUBLANES + r0, CONV_CHUNK), :] = strip_ref[pl.ds(r0, CONV_CHUNK), :]


def _conv_act(up, conv_w, conv_b):
    s = up.shape[0]

    def body(ug_ref, uv_ref, wg_ref, wv_ref, bg_ref, bv_ref, act_ref, pg_ref, pv_ref):
        _fill_front_pad(pg_ref, ug_ref, s)
        _fill_front_pad(pv_ref, uv_ref, s)
        wg, wv, bg, bv = wg_ref[...], wv_ref[...], bg_ref[...], bv_ref[...]
        for r0 in range(0, s, CONV_CHUNK):
            hg = _conv_rows(pg_ref, wg, bg, r0, CONV_CHUNK)
            hv = _conv_rows(pv_ref, wv, bv, r0, CONV_CHUNK)
            act_ref[pl.ds(r0, CONV_CHUNK), :] = (hg * _sigmoid(hg) * hv).astype(BF16)

    strip = lambda off: pl.BlockSpec((s, STRIP), lambda j: (0, j + off))
    wsp = lambda off: pl.BlockSpec((3, STRIP), lambda j: (0, j + off))
    bsp = lambda off: pl.BlockSpec((1, STRIP), lambda j: (0, j + off))
    return _pcall(body, name="conv_act", grid=(N_STRIPS,),
                  in_specs=[strip(0), strip(N_STRIPS), wsp(0), wsp(N_STRIPS), bsp(0), bsp(N_STRIPS)],
                  out_specs=pl.BlockSpec((s, STRIP), lambda j: (0, j)), out_shape=_sds((s, D_FF), BF16),
                  scratch_shapes=[pltpu.VMEM((s + SUBLANES, STRIP), F32)] * 2,
                  dims=("parallel",))(up, up, conv_w, conv_w, conv_b, conv_b)


def _down_loss(act, w_down, x1, tgt):
    s = x1.shape[0]
    tm = _row_tile(s)

    def body(a_ref, w_ref, x1_ref, t_ref, dy_ref, dyb_ref, loss_ref):
        i = pl.program_id(0)

        @pl.when(i == 0)
        def _():
            loss_ref[...] = jnp.zeros_like(loss_ref)

        diff = x1_ref[...] + _dot(a_ref[...], w_ref[...]) - t_ref[...]
        dy = diff * (1.0 / D_MODEL)
        dy_ref[...] = dy
        dyb_ref[...] = dy.astype(BF16)
        loss_ref[...] += 0.5 * jnp.sum(diff * dy)

    row = lambda w: pl.BlockSpec((tm, w), lambda i: (i, 0))
    return _pcall(body, name="down_loss", grid=(s // tm,),
                  in_specs=[row(D_FF), pl.BlockSpec((D_FF, D_MODEL), lambda i: (0, 0)), row(D_MODEL), row(D_MODEL)],
                  out_specs=[row(D_MODEL), row(D_MODEL), pl.BlockSpec((SUBLANES, LANES), lambda i: (0, 0))],
                  out_shape=[_sds((s, D_MODEL)), _sds((s, D_MODEL), BF16), _sds((SUBLANES, LANES))],
                  dims=("arbitrary",))(act, w_down, x1, tgt)


def _conv_act_bwd(up, dact, conv_w, conv_b):
    s = up.shape[0]
    ch = CONV_CHUNK

    def body(ug_ref, uv_ref, da_ref, wg_ref, wv_ref, bg_ref, bv_ref, dup_ref, dcw_ref, pg_ref, pv_ref, dg_ref, dv_ref):
        _fill_front_pad(pg_ref, ug_ref, s)
        _fill_front_pad(pv_ref, uv_ref, s)
        zero = jnp.zeros((SUBLANES, STRIP), F32)
        dg_ref[pl.ds(s, SUBLANES), :] = zero
        dv_ref[pl.ds(s, SUBLANES), :] = zero
        wg, wv, bg, bv = wg_ref[...], wv_ref[...], bg_ref[...], bv_ref[...]
        tile_sum = lambda t: jnp.sum(t.reshape(ch // SUBLANES, SUBLANES, STRIP), axis=0)
        accs = [[zero] * 4, [zero] * 4]
        for r0 in range(0, s, ch):
            hg = _conv_rows(pg_ref, wg, bg, r0, ch)
            hv = _conv_rows(pv_ref, wv, bv, r0, ch)
            sg = _sigmoid(hg)
            da = da_ref[pl.ds(r0, ch), :]
            dhs = (da * hv * (sg * (1.0 + hg * (1.0 - sg))), da * (hg * sg))
            for half, (dh, d_ref, p_ref) in enumerate(zip(dhs, (dg_ref, dv_ref), (pg_ref, pv_ref))):
                d_ref[pl.ds(r0, ch), :] = dh
                for k in range(3):
                    accs[half][k] = accs[half][k] + tile_sum(dh * p_ref[pl.ds(r0 + SUBLANES - 2 + k, ch), :])
                accs[half][3] = accs[half][3] + tile_sum(dh)
        for half, (d_ref, w) in enumerate(((dg_ref, wg), (dv_ref, wv))):
            for r0 in range(0, s, ch):
                dup = (d_ref[pl.ds(r0, ch), :] * w[2:3, :] + d_ref[pl.ds(r0 + 1, ch), :] * w[1:2, :]
                       + d_ref[pl.ds(r0 + 2, ch), :] * w[0:1, :])
                dup_ref[half, pl.ds(r0, ch), :] = dup.astype(BF16)
            rid = lax.broadcasted_iota(jnp.int32, (SUBLANES, STRIP), 0)
            out = zero
            for k in range(4):
                out = jnp.where(rid == k, jnp.sum(accs[half][k], axis=0, keepdims=True), out)
            dcw_ref[half] = out

    strip = lambda off: pl.BlockSpec((s, STRIP), lambda j: (0, j + off))
    wsp = lambda off: pl.BlockSpec((3, STRIP), lambda j: (0, j + off))
    bsp = lambda off: pl.BlockSpec((1, STRIP), lambda j: (0, j + off))
    return _pcall(body, name="conv_act_bwd", grid=(N_STRIPS,),
                  in_specs=[strip(0), strip(N_STRIPS), strip(0), wsp(0), wsp(N_STRIPS), bsp(0), bsp(N_STRIPS)],
                  out_specs=[pl.BlockSpec((2, s, STRIP), lambda j: (0, 0, j)), pl.BlockSpec((2, SUBLANES, STRIP), lambda j: (0, 0, j))],
                  out_shape=[_sds((2, s, D_FF), BF16), _sds((2, SUBLANES, D_FF))],
                  scratch_shapes=[pltpu.VMEM((s + SUBLANES, STRIP), F32)] * 4,
                  dims=("parallel",))(up, up, dact, conv_w, conv_w, conv_b, conv_b)


def _mix_bwd(dy, dh2, x1, g_ffn, w_out, y, att, w_glu, b_glu, g_att, g_ssm):
    s = dy.shape[0]
    tm = _row_tile(s)

    def body(dy_ref, dh2_ref, x1_ref, gf_ref, wo_ref, y_ref, att_ref, wg_ref, bg_ref, ga_ref, gs_ref,
             dx1_ref, dx1b_ref, datt_ref, dys_ref, dwg_ref, dgf_ref, dga_ref, dgs_ref, dbg_ref):
        i = pl.program_id(0)

        @pl.when(i == 0)
        def _():
            for r in (dwg_ref, dgf_ref, dga_ref, dgs_ref, dbg_ref):
                r[...] = jnp.zeros_like(r)

        dxn, dgf = _rms_bwd(x1_ref[...], gf_ref[...], dh2_ref[...])
        dx1 = dy_ref[...] + dxn
        dx1_ref[...] = dx1
        dx1b = dx1.astype(BF16)
        dx1b_ref[...] = dx1b
        dgf_ref[...] += dgf
        dma = _dot_nt(dx1b, wo_ref[0:ATTN_W, :])
        dms = _dot_nt(dx1b, wo_ref[ATTN_W:D_MODEL, :])
        datt, dga = _rms_bwd(_merge_heads(att_ref), ga_ref[...], dma)
        _split_heads(datt_ref, datt)
        dga_ref[...] += dga
        yv = y_ref[...]
        ge, sg = _ssm_glu(yv, wg_ref[...], bg_ref[...])
        dssm, dgs = _rms_bwd(ge * sg, gs_ref[...], dms)
        dgs_ref[...] += dgs
        dgl = dssm * ge * sg * (1.0 - sg)
        dglb = dgl.astype(BF16)
        dge = dssm * sg + _dot_nt(dglb, wg_ref[...])
        dbg_ref[...] += jnp.sum(dgl, axis=0, keepdims=True)
        dwg_ref[...] += _dot_tn(ge.astype(BF16), dglb)
        dys_ref[...] = dge * _gelu_grad(yv)

    row = lambda w: pl.BlockSpec((tm, w), lambda i: (i, 0))
    const = lambda a, b: pl.BlockSpec((a, b), lambda i: (0, 0))
    heads = pl.BlockSpec((HEADS, tm, HEAD_DIM), lambda i: (0, i, 0))
    return _pcall(body, name="mix_bwd", grid=(s // tm,),
                  in_specs=[row(D_MODEL), row(D_MODEL), row(D_MODEL), const(1, D_MODEL), const(D_MODEL, D_MODEL), row(SSM_W),
                            heads, const(SSM_W, SSM_W), const(1, SSM_W), const(1, ATTN_W), const(1, SSM_W)],
                  out_specs=[row(D_MODEL), row(D_MODEL), heads, row(SSM_W), const(SSM_W, SSM_W), const(1, D_MODEL),
                             const(1, ATTN_W), const(1, SSM_W), const(1, SSM_W)],
                  out_shape=[_sds((s, D_MODEL)), _sds((s, D_MODEL), BF16), _sds((HEADS, s, HEAD_DIM)), _sds((s, SSM_W)),
                             _sds((SSM_W, SSM_W)), _sds((1, D_MODEL)), _sds((1, ATTN_W)), _sds((1, SSM_W)), _sds((1, SSM_W))],
                  dims=("arbitrary",))(dy, dh2, x1, g_ffn, w_out, y, att, w_glu, b_glu, g_att, g_ssm)


def _ssm_bwd(dys, uf, ub, xr, xi, bbr, bbi, ar, ai, ccr, cci, dsk, hosted=None):
    s = dys.shape[0]
    tm = _row_tile(s)
    nb = s // tm
    nt = tm // SUBLANES

    def body(dy_ref, u_ref, ub_ref, xr_ref, xi_ref, xrp_ref, xip_ref, bbr_ref, bbi_ref, ar_ref, ai_ref, ccr_ref,
             cci_ref, dsk_ref, du_ref, dbbr_ref, dbbi_ref, dccr_ref, dcci_ref, dar_ref, dai_ref, dd_ref,
             gr_s, gi_s, cr_s, ci_s, accr_s, acci_s):
        i = pl.program_id(1)
        first_block = i == nb - 1

        @pl.when(i == 0)
        def _():
            for r in (cr_s, ci_s, accr_s, acci_s, dbbr_ref, dbbi_ref, dccr_ref, dcci_ref, dd_ref):
                r[...] = jnp.zeros_like(r)

        dy = dy_ref[...]
        dyb = dy.astype(BF16)
        gr_s[...] = _dot_nt(dyb, ccr_ref[0])
        gi_s[...] = -_dot_nt(dyb, cci_ref[0])
        consts = _scan_consts(ar_ref[0], -ai_ref[0], CHUNK_S, True)
        row = lax.broadcasted_iota(jnp.int32, (SUBLANES, CHUNK_S), 0)

        def tile(kk, carry):
            cr, ci, accr, acci = carry
            k = nt - 1 - kk
            sl = pl.ds(pl.multiple_of(k * SUBLANES, SUBLANES), SUBLANES)
            gr, gi = _scan_tile(gr_s[sl, :], gi_s[sl, :], cr, ci, consts, True)
            gr_s[sl, :] = gr
            gi_s[sl, :] = gi
            slp = pl.ds(pl.multiple_of(jnp.maximum(k - 1, 0) * SUBLANES, SUBLANES), SUBLANES)
            inner = k > 0
            pr_t = jnp.where(inner, xr_ref[slp, :], xrp_ref[...])
            pi_t = jnp.where(inner, xi_ref[slp, :], xip_ref[...])
            live = jnp.logical_or(inner, jnp.logical_not(first_block))
            top_r = jnp.where(live, pltpu.roll(pr_t, 1, 0), 0.0)
            top_i = jnp.where(live, pltpu.roll(pi_t, 1, 0), 0.0)
            xpr = jnp.where(row == 0, top_r, pltpu.roll(xr_ref[sl, :], 1, 0))
            xpi = jnp.where(row == 0, top_i, pltpu.roll(xi_ref[sl, :], 1, 0))
            accr = accr + gr * xpr + gi * xpi
            acci = acci + gi * xpr - gr * xpi
            return gr[0:1, :], gi[0:1, :], accr, acci

        zeros = jnp.zeros((SUBLANES, CHUNK_S), F32)
        cr, ci, accr, acci = lax.fori_loop(0, nt, tile, (cr_s[...], ci_s[...], zeros, zeros))
        cr_s[...] = cr
        ci_s[...] = ci
        accr_s[...] += accr
        acci_s[...] += acci
        grb = gr_s[...].astype(BF16)
        gib = gi_s[...].astype(BF16)
        u_b = ub_ref[...]
        du_ref[...] = _dot_nt(grb, bbr_ref[0]) + _dot_nt(gib, bbi_ref[0]) + dsk_ref[...] * dy
        dbbr_ref[0] += _dot_tn(u_b, grb)
        dbbi_ref[0] += _dot_tn(u_b, gib)
        dccr_ref[0] += _dot_tn(xr_ref[...].astype(BF16), dyb)
        dcci_ref[0] -= _dot_tn(xi_ref[...].astype(BF16), dyb)
        dd_ref[...] += jnp.sum(dy * u_ref[...], axis=0, keepdims=True)

        @pl.when(i == nb - 1)
        def _():
            dar_ref[0] = jnp.sum(accr_s[...], axis=0, keepdims=True)
            dai_ref[0] = jnp.sum(acci_s[...], axis=0, keepdims=True)

    tiles_per_block = tm // SUBLANES
    rb = lambda i: nb - 1 - i
    wspec = lambda a, b: pl.BlockSpec((1, a, b), lambda j, i: (j, 0, 0))
    xblk = pl.BlockSpec((tm, CHUNK_S), lambda j, i: (rb(i), j))
    xprev = pl.BlockSpec((SUBLANES, CHUNK_S), lambda j, i: (jnp.maximum(rb(i) * tiles_per_block - 1, 0), j))
    ublk = pl.BlockSpec((tm, CHUNK_U), lambda j, i: (rb(i), j))
    first = lambda: jnp.logical_and(pl.program_id(0) == 0, pl.program_id(1) == 0)
    last = lambda: jnp.logical_and(pl.program_id(0) == SSM_CHUNKS - 1, pl.program_id(1) == nb - 1)
    return _host_pcall(
        body, hosted, first, last, n_in=14, n_out=8, n_scratch=6, name="ssm_bwd", grid=(SSM_CHUNKS, nb),
        in_specs=[ublk, ublk, ublk, xblk, xblk, xprev, xprev,
                  wspec(CHUNK_U, CHUNK_S), wspec(CHUNK_U, CHUNK_S), wspec(1, CHUNK_S), wspec(1, CHUNK_S),
                  wspec(CHUNK_S, CHUNK_U), wspec(CHUNK_S, CHUNK_U), pl.BlockSpec((1, CHUNK_U), lambda j, i: (0, j))],
        out_specs=[ublk, wspec(CHUNK_U, CHUNK_S), wspec(CHUNK_U, CHUNK_S), wspec(CHUNK_S, CHUNK_U),
                   wspec(CHUNK_S, CHUNK_U), wspec(1, CHUNK_S), wspec(1, CHUNK_S),
                   pl.BlockSpec((1, CHUNK_U), lambda j, i: (0, j))],
        out_shape=[_sds((s, SSM_W)), _sds((SSM_CHUNKS, CHUNK_U, CHUNK_S)), _sds((SSM_CHUNKS, CHUNK_U, CHUNK_S)),
                   _sds((SSM_CHUNKS, CHUNK_S, CHUNK_U)), _sds((SSM_CHUNKS, CHUNK_S, CHUNK_U)),
                   _sds((SSM_CHUNKS, 1, CHUNK_S)), _sds((SSM_CHUNKS, 1, CHUNK_S)), _sds((1, SSM_W))],
        scratch_shapes=[pltpu.VMEM((tm, CHUNK_S), F32)] * 2 + [pltpu.VMEM((1, CHUNK_S), F32)] * 2
                       + [pltpu.VMEM((SUBLANES, CHUNK_S), F32)] * 2,
        dims=("parallel", "arbitrary"), operands=(dys, uf, ub, xr, xi, xr, xi, bbr, bbi, ar, ai, ccr, cci, dsk))


def _attn_probs(q, ks, cs, lse, scale, diagonal):
    p = jnp.exp(_dot_nt(q, ks) * scale - cs - lse)
    if diagonal:
        tq, tk = p.shape
        causal = lax.broadcasted_iota(jnp.int32, (tq, tk), 1) <= lax.broadcasted_iota(jnp.int32, (tq, tk), 0)
        p = jnp.where(causal, p, 0.0)
    return p


def _attn_bwd(qh, kh, vh, crow, lse, doh, hosted=None):
    _, s, _ = qh.shape
    tq = _row_tile(s)
    nq = s // tq
    scale = HEAD_DIM ** -0.5
    hp = HEADS_PER_STEP

    def body(q_ref, k_ref, v_ref, c_ref, lse_ref, do_ref, dq_ref, dk_ref, dv_ref, dc_ref, p_s, dp_s):
        i = pl.program_id(1)

        @pl.when(i == 0)
        def _():
            for r in (dk_ref, dv_ref, dc_ref):
                r[...] = jnp.zeros_like(r)

        dobs = [do_ref[hh].astype(BF16) for hh in range(hp)]

        def first(j, dls, diagonal):
            off = pl.multiple_of(j * tq, tq)
            out = []
            for hh in range(hp):
                p = _attn_probs(q_ref[hh], k_ref[hh, pl.ds(off, tq), :], c_ref[hh, :, pl.ds(off, tq)], lse_ref[hh],
                                scale, diagonal)
                dp = _dot_nt(dobs[hh], v_ref[hh, pl.ds(off, tq), :])
                p_s[hh, j] = p
                dp_s[hh, j] = dp
                out.append(dls[hh] + jnp.sum(p * dp, axis=-1, keepdims=True))
            return tuple(out)

        zero_col = jnp.zeros((tq, 1), F32)
        dls = lax.fori_loop(0, i, lambda j, c: first(j, c, False), (zero_col,) * hp)
        dls = first(i, dls, True)

        def second(j, dqs):
            rows = pl.ds(pl.multiple_of(j * tq, tq), tq)
            out = []
            for hh in range(hp):
                p = p_s[hh, j]
                ds = p * (dp_s[hh, j] - dls[hh])
                dsb = ds.astype(BF16)
                dv_ref[hh, rows, :] += _dot_tn(p.astype(BF16), dobs[hh])
                dk_ref[hh, rows, :] += _dot_tn(dsb, q_ref[hh]) * scale
                dc_ref[hh, :, rows] -= jnp.sum(ds, axis=0, keepdims=True)
                out.append(dqs[hh] + _dot(dsb, k_ref[hh, rows, :]))
            return tuple(out)

        dqs = lax.fori_loop(0, i + 1, second, (jnp.zeros((tq, HEAD_DIM), F32),) * hp)
        for hh in range(hp):
            dq_ref[hh] = dqs[hh] * scale

    blk = pl.BlockSpec((hp, tq, HEAD_DIM), lambda h, i: (h, i, 0))
    full = pl.BlockSpec((hp, s, HEAD_DIM), lambda h, i: (h, 0, 0))
    crow_spec = pl.BlockSpec((hp, 1, s), lambda h, i: (h, 0, 0))
    nh = HEADS // hp
    first = lambda: jnp.logical_and(pl.program_id(0) == 0, pl.program_id(1) == 0)
    last = lambda: jnp.logical_and(pl.program_id(0) == nh - 1, pl.program_id(1) == nq - 1)
    return _host_pcall(body, hosted, first, last, n_in=6, n_out=4, n_scratch=2, name="attn_bwd", grid=(nh, nq),
                       in_specs=[blk, full, full, crow_spec, pl.BlockSpec((hp, tq, 1), lambda h, i: (h, i, 0)), blk],
                       out_specs=[blk, full, full, crow_spec],
                       out_shape=[_sds((HEADS, s, HEAD_DIM))] * 3 + [_sds((HEADS, 1, s))],
                       scratch_shapes=[pltpu.VMEM((hp, nq, tq, tq), F32)] * 2,
                       dims=("parallel", "arbitrary"), operands=(qh, kh, vh, crow, lse, doh))


def _prep_bwd(z, dqn, dkn, dv, du, dc, gq, gk, bf, gg):
    s = z.shape[0]
    tm = _row_tile(s)
    nb = s // tm

    def body(z_ref, dqn_ref, dkn_ref, dv_ref, du_ref, dc_ref, gq_ref, gk_ref, bf_ref, gg_ref,
             dz_ref, dgq_ref, dgk_ref, dbf_ref, carry_ref):
        i = pl.program_id(0)

        @pl.when(i == 0)
        def _():
            for r in (dgq_ref, dgk_ref, dbf_ref, carry_ref):
                r[...] = jnp.zeros_like(r)

        gg_m = gg_ref[...]

        def head_norm_bwd(t, g, dn):
            r = lax.rsqrt(_dot_exact_r(t * t, gg_m) * (1.0 / HEAD_DIM) + EPS)
            w = dn * g
            mean_wt = _dot_exact_r(w * t, gg_m) * (1.0 / HEAD_DIM)
            return r * w - t * (r * r * r) * mean_wt, jnp.sum(dn * t * r, axis=0, keepdims=True)

        dq, dgq = head_norm_bwd(z_ref[:, 0:ATTN_W], gq_ref[...], _merge_heads(dqn_ref))
        dk, dgk = head_norm_bwd(z_ref[:, ATTN_W:2 * ATTN_W], gk_ref[...], _merge_heads(dkn_ref))
        dgq_ref[...] += dgq
        dgk_ref[...] += dgk
        row = lax.broadcasted_iota(jnp.int32, (tm, tm), 0)
        col = lax.broadcasted_iota(jnp.int32, (tm, tm), 1)
        triu = (col >= row).astype(BF16)
        dlf = _dot_exact_l(triu, dc_ref[...]) + carry_ref[...]
        carry_ref[...] = dlf[0:1, :]
        df = dlf * _sigmoid(-_forget_logits(z_ref, bf_ref))
        dbf_ref[...] += jnp.sum(df, axis=0, keepdims=True)
        dz_ref[:, 0:ATTN_W] = dq.astype(BF16)
        dz_ref[:, ATTN_W:2 * ATTN_W] = dk.astype(BF16)
        dz_ref[:, 2 * ATTN_W:3 * ATTN_W] = _merge_heads(dv_ref).astype(BF16)
        tail = jnp.concatenate([df[:, :HEADS], du_ref[...], jnp.zeros((tm, Z_COLS - IN_COLS), F32)], axis=-1)
        dz_ref[:, F_COL0:Z_COLS] = tail.astype(BF16)

    row_spec = lambda w: pl.BlockSpec((tm, w), lambda i: (nb - 1 - i, 0))
    const = lambda shape: pl.BlockSpec(shape, lambda i: (0, 0))
    return _pcall(body, name="prep_bwd", grid=(nb,),
                  in_specs=[row_spec(Z_COLS)] + [pl.BlockSpec((HEADS, tm, HEAD_DIM), lambda i: (0, nb - 1 - i, 0))] * 3
                           + [row_spec(ATTN_W), row_spec(LANES), const((1, ATTN_W)),
                              const((1, ATTN_W)), const((1, LANES)), const((ATTN_W, ATTN_W))],
                  out_specs=[row_spec(Z_COLS), const((1, ATTN_W)), const((1, ATTN_W)), const((1, LANES))],
                  out_shape=[_sds((s, Z_COLS), BF16), _sds((1, ATTN_W)), _sds((1, ATTN_W)), _sds((1, LANES))],
                  scratch_shapes=[pltpu.VMEM((1, LANES), F32)], dims=("arbitrary",))(z, dqn, dkn, dv, du, dc, gq, gk, bf, gg)


def _in_norm_bwd(x, g_mix, dh, dx1, hosted=None):
    s = x.shape[0]
    tm = _row_tile(s)

    def body(x_ref, g_ref, dh_ref, dx1_ref, dx_ref, dg_ref):
        i = pl.program_id(0)

        @pl.when(i == 0)
        def _():
            dg_ref[...] = jnp.zeros_like(dg_ref)

        dxn, dg = _rms_bwd(x_ref[...], g_ref[...], dh_ref[...])
        dx_ref[...] = dx1_ref[...] + dxn
        dg_ref[...] += dg

    row = pl.BlockSpec((tm, D_MODEL), lambda i: (i, 0))
    vec = pl.BlockSpec((1, D_MODEL), lambda i: (0, 0))
    nb = s // tm
    return _host_pcall(body, hosted, lambda: pl.program_id(0) == 0, lambda: pl.program_id(0) == nb - 1,
                       n_in=4, n_out=2, n_scratch=0, name="in_norm_bwd", grid=(nb,), in_specs=[row, vec, row, row],
                       out_specs=[row, vec], out_shape=[_sds((s, D_MODEL)), _sds((1, D_MODEL))], scratch_shapes=[],
                       dims=("arbitrary",), operands=(x, g_mix, dh, dx1))


def _adamw_refs(w_ref, g_ref, m_ref, v_ref, d_ref, mo_ref, vo_ref):
    gv = g_ref[...]
    mn = ADAM_B1 * m_ref[...] + (1.0 - ADAM_B1) * gv
    vn = ADAM_B2 * v_ref[...] + (1.0 - ADAM_B2) * (gv * gv)
    m_hat = mn / (1.0 - ADAM_B1 ** ADAM_STEP)
    v_hat = vn / (1.0 - ADAM_B2 ** ADAM_STEP)
    d_ref[...] = -ADAM_LR * (m_hat / (jnp.sqrt(v_hat) + ADAM_EPS) + ADAM_WD * w_ref[...])
    mo_ref[...] = mn
    vo_ref[...] = vn


def _adamw_small(ws, gs, ms, vs):
    n = len(ws)

    def body(*refs):
        ins, outs = refs[:4 * n], refs[4 * n:]
        for i in range(n):
            _adamw_refs(ins[i], ins[n + i], ins[2 * n + i], ins[3 * n + i], *outs[3 * i:3 * i + 3])

    vm = pl.BlockSpec(memory_space=pltpu.VMEM)
    out_shape = [_sds(w.shape) for w in ws for _ in range(3)]
    return _pallas(body, name="adamw_small", in_specs=[vm] * (4 * n), out_specs=[vm] * (3 * n), out_shape=out_shape,
                   compiler_params=pltpu.CompilerParams(vmem_limit_bytes=VMEM_LIMIT))(*ws, *gs, *ms, *vs)


def _adamw(w, g, m, v, *, name):
    r, c = w.shape
    tr = r
    for cand in (256, 176, 128, 64):
        if r > cand and r % cand == 0:
            tr = cand
            break

    def body(w_ref, g_ref, m_ref, v_ref, d_ref, mo_ref, vo_ref):
        _adamw_refs(w_ref, g_ref, m_ref, v_ref, d_ref, mo_ref, vo_ref)

    spec = pl.BlockSpec((tr, c), lambda i: (i, 0))
    return _pcall(body, name=name, grid=(r // tr,), in_specs=[spec] * 4, out_specs=[spec] * 3,
                  out_shape=[_sds((r, c))] * 3, dims=("parallel",))(w, g, m, v)


def _prefetch_call(body, *, name, grid, in_specs, out_specs, out_shape, operands):
    grid_spec = pltpu.PrefetchScalarGridSpec(num_scalar_prefetch=1, grid=grid, in_specs=in_specs, out_specs=out_specs)
    params = pltpu.CompilerParams(dimension_semantics=("parallel",) * len(grid), vmem_limit_bytes=VMEM_LIMIT)
    return _pallas(body, name=name, grid_spec=grid_spec, out_shape=out_shape, compiler_params=params)(*operands)


def _place_cols(buf, shard, place):
    rows, cols = shard.shape
    tr = 256

    def body(place_ref, s_ref, b_ref, o_ref):
        o_ref[...] = s_ref[...]

    grid_spec = pltpu.PrefetchScalarGridSpec(
        num_scalar_prefetch=1, grid=(rows // tr,),
        in_specs=[pl.BlockSpec((tr, cols), lambda i, p: (i, 0)), pl.BlockSpec(memory_space=pltpu.HBM)],
        out_specs=pl.BlockSpec((tr, cols), lambda i, p: (i, p[0])))
    return _pallas(body, name="place_own_cols", grid_spec=grid_spec, out_shape=_sds(buf.shape, buf.dtype),
                   input_output_aliases={2: 0},
                   compiler_params=pltpu.CompilerParams(dimension_semantics=("parallel",),
                                                        vmem_limit_bytes=VMEM_LIMIT))(place, shard, buf)


def _half_rows_tile(hr):
    return hr if hr <= 256 else 176 if hr % 176 == 0 else 256


def _add_half(g, landed, place, *, name):
    def body(place_ref, g_ref, l_ref, o_ref):
        own = g_ref[0] if len(g_ref.shape) == 4 else g_ref[...]
        o_ref[...] = (own + l_ref[...]).astype(BF16)

    if g.ndim == 4:
        _, _, hr, c = g.shape
        tr = _half_rows_tile(hr)
        blk = (1, tr, c)
        return _prefetch_call(
            body, name=name, grid=(N_CHIPS, hr // tr),
            in_specs=[pl.BlockSpec((1,) + blk, lambda j, i, p: (j, p[1], i, 0)), pl.BlockSpec(blk, lambda j, i, p: (j, i, 0))],
            out_specs=pl.BlockSpec(blk, lambda j, i, p: (j, i, 0)), out_shape=_sds(landed.shape, BF16),
            operands=(place, g, landed))
    hr, c = landed.shape
    tr, tc = 256, _tile(c, 2176)
    nb = hr // tr
    return _prefetch_call(
        body, name=name, grid=(nb, c // tc),
        in_specs=[pl.BlockSpec((tr, tc), lambda i, j, p: (p[1] * nb + i, j)), pl.BlockSpec((tr, tc), lambda i, j, p: (i, j))],
        out_specs=pl.BlockSpec((tr, tc), lambda i, j, p: (i, j)), out_shape=_sds(landed.shape, BF16),
        operands=(place, g, landed))


def _sum_chips(chip_sum, lands, place, *, name, tc, window_stride=0):
    _, hr, c = lands.shape
    tr = _half_rows_tile(hr)
    nb = hr // tr
    ncb = c // tc

    def body(place_ref, own_ref, a_ref, b_ref, c_ref, o_ref):
        own = own_ref[0] if len(own_ref.shape) == 3 else own_ref[...]
        o_ref[...] = ((own.astype(F32) + a_ref[0].astype(F32)) + b_ref[0].astype(F32)) + c_ref[0].astype(F32)

    land = lambda k: pl.BlockSpec((1, tr, tc), lambda i, j, p: ((p[0] + k) % N_CHIPS, i, j))
    if chip_sum.ndim == 3:
        own_spec = land(0)
    else:
        stride = window_stride // tc
        own_spec = pl.BlockSpec((tr, tc), lambda i, j, p: (i, p[0] * stride + j))
    return _prefetch_call(
        body, name=name, grid=(nb, ncb), in_specs=[own_spec, land(1), land(2), land(3)],
        out_specs=pl.BlockSpec((tr, tc), lambda i, j, p: (p[1] * nb + i, j)), out_shape=_sds((2 * hr, c)),
        operands=(place, chip_sum, lands, lands, lands))


_HBM = pl.BlockSpec(memory_space=pltpu.HBM)


def _place():
    x, y, c = lax.axis_index("x"), lax.axis_index("y"), lax.axis_index("c")
    chips = [(1 - x, y), (x, 1 - y), (1 - x, 1 - y)]
    return x, y, c, chips


def _rcopy(src, dst, send_sem, recv_sem, to):
    return pltpu.make_async_remote_copy(src_ref=src, dst_ref=dst, send_sem=send_sem, recv_sem=recv_sem,
                                        device_id=to, device_id_type=MESH)


UP_COLS = 2 * D_FF // N_CHIPS
IN_WINDOW = 640
IN_STRIDE = 512


class _Hosted:
    def __init__(self, operands, out_shapes, n_sems, start, finish, aliases=None, local_sems=0):
        self.operands, self.out_shapes, self.n_sems = list(operands), list(out_shapes), n_sems
        self.start, self.finish, self.aliases, self.local_sems = start, finish, dict(aliases or {}), local_sems

    def scratch(self):
        return ([pltpu.SemaphoreType.DMA((self.n_sems,)), pltpu.SemaphoreType.DMA((self.n_sems,))]
                + [pltpu.SemaphoreType.DMA] * self.local_sems)


def _both(a, b):
    na, nao, nas = len(a.operands), len(a.out_shapes), len(a.scratch())

    def start(ins, outs, sems):
        a.start(ins[:na], outs[:nao], sems[:nas])
        b.start(ins[na:], outs[nao:], sems[nas:])

    def finish(ins, outs, sems):
        a.finish(ins[:na], outs[:nao], sems[:nas])
        b.finish(ins[na:], outs[nao:], sems[nas:])

    both = _Hosted(a.operands + b.operands, a.out_shapes + b.out_shapes, 0, start, finish,
                   aliases={**a.aliases, **{na + i: nao + o for i, o in b.aliases.items()}})
    both.scratch = lambda: a.scratch() + b.scratch()
    return both


def _then(a, b):
    nas = len(a.scratch())

    def finish(ins, outs, sems):
        a.finish(ins, outs, sems[:nas])
        b.start(ins, outs, sems[nas:])
        b.finish(ins, outs, sems[nas:])

    chain = _Hosted(a.operands, a.out_shapes, 0, lambda ins, outs, sems: a.start(ins, outs, sems[:nas]), finish,
                    aliases=a.aliases)
    chain.scratch = lambda: a.scratch() + b.scratch()
    return chain


def _run_hosted(hosted, *, name):
    n_in, n_out = len(hosted.operands), len(hosted.out_shapes)

    def body(*refs):
        parts = (refs[:n_in], refs[n_in:n_in + n_out], refs[n_in + n_out:])
        hosted.start(*parts)
        hosted.finish(*parts)

    return _pallas(body, name=name, in_specs=[_HBM] * n_in, out_specs=[_HBM] * n_out, out_shape=hosted.out_shapes,
                   input_output_aliases=hosted.aliases, scratch_shapes=hosted.scratch())(*hosted.operands)


def _host_pcall(core_body, hosted, first, last, *, n_in, n_out, n_scratch, name, grid, in_specs, out_specs, out_shape,
                scratch_shapes, dims, operands):
    if hosted is None:
        outs = _pcall(core_body, name=name, grid=grid, in_specs=in_specs, out_specs=out_specs, out_shape=out_shape,
                      scratch_shapes=scratch_shapes, dims=dims)(*operands)
        return outs, []
    hi, ho = len(hosted.operands), len(hosted.out_shapes)

    def body(*refs):
        a, b = n_in, n_in + hi
        c, d = b + n_out, b + n_out + ho
        e = d + n_scratch
        parts = (refs[a:b], refs[c:d], refs[e:])

        @pl.when(first())
        def _():
            hosted.start(*parts)

        core_body(*refs[:a], *refs[b:c], *refs[d:e])

        @pl.when(last())
        def _():
            hosted.finish(*parts)

    params = pltpu.CompilerParams(dimension_semantics=("arbitrary",) * len(grid), vmem_limit_bytes=VMEM_LIMIT)
    outs = _pallas(body, name=name, grid=grid, in_specs=list(in_specs) + [_HBM] * hi, out_specs=list(out_specs) + [_HBM] * ho,
                   out_shape=list(out_shape) + hosted.out_shapes, scratch_shapes=list(scratch_shapes) + hosted.scratch(),
                   input_output_aliases={n_in + a: n_out + b for a, b in hosted.aliases.items()},
                   compiler_params=params)(*operands, *hosted.operands)
    return outs[:n_out], outs[n_out:]


WHOLE_HALF = (0, 1, 1)


def _band_rows(src, hc, band):
    first, count, of = band
    hr = src.shape[0] // 2
    return pl.ds(hc * hr + first * (hr // of), count * (hr // of))


def _gather_slot(src, out, chip, hc, band=WHOLE_HALF):
    cols = src.shape[1]
    if len(out.shape) == 2:
        return out.at[_band_rows(src, hc, band), pl.ds(pl.multiple_of(chip * cols, LANES), cols)]
    return out.at[chip, _band_rows(src, hc, band), :]


def _gathered_shape(shard, by_cols):
    if by_cols:
        return _sds((shard.shape[0], N_CHIPS * shard.shape[1]), shard.dtype)
    return _sds((N_CHIPS,) + shard.shape, shard.dtype)


def _plan_gather_ici(shards, by_cols, whole=(), bands=None, into=None):
    n = len(shards)
    bands = bands or [WHOLE_HALF] * n
    into = into or [None] * n
    given = [w for w in range(n) if into[w] is not None]
    n_ops = n + len(whole)

    def copies(ins, outs, sems):
        send_sems, recv_sems = sems[0], sems[1]
        x, y, c, chips = _place()
        me = 2 * x + y
        sends, waits = [], []
        for w in range(n + len(whole)):
            for k, (cx, cy) in enumerate(chips):
                sem = (send_sems.at[3 * w + k], recv_sems.at[3 * w + k])
                if w < n:
                    sends.append(_rcopy(ins[w].at[_band_rows(ins[w], c, bands[w]), :],
                                        _gather_slot(ins[w], outs[w], me, c, bands[w]), *sem, (cx, cy, c)))
                    landed = _gather_slot(ins[w], outs[w], 2 * cx + cy, c, bands[w])
                else:
                    sends.append(_rcopy(ins[w], outs[w].at[me], *sem, (cx, cy, c)))
                    landed = outs[w].at[2 * cx + cy]
                waits.append(_rcopy(landed, landed, *sem, (cx, cy, c)))
        return sends, waits

    def start(ins, outs, sems):
        for cp in copies(ins, outs, sems)[0]:
            cp.start()

    def finish(ins, outs, sems):
        sends, waits = copies(ins, outs, sems)
        for cp in waits:
            cp.wait_recv()
        for cp in sends:
            cp.wait_send()

    out_shapes = [_gathered_shape(s, bc) for s, bc in zip(shards, by_cols)] + [_sds((N_CHIPS,) + a.shape, a.dtype) for a in whole]
    return _Hosted(list(shards) + list(whole) + [into[w] for w in given], out_shapes, 3 * n_ops, start, finish,
                   aliases={n_ops + i: w for i, w in enumerate(given)})


def _plan_gather_d2d(bufs, shard_shapes, bands=None):
    n = len(bufs)
    bands = bands or [WHOLE_HALF] * n

    def copies(ins, outs, sems):
        send_sems, recv_sems = sems
        x, y, c, chips = _place()
        sibling = (x, y, 1 - c)
        sends, waits = [], []
        for w in range(n):
            for k, (cx, cy) in enumerate(chips):
                sem = (send_sems.at[3 * w + k], recv_sems.at[3 * w + k])
                landed = _gather_slot(shard_shapes[w], outs[w], 2 * cx + cy, c, bands[w])
                other = _gather_slot(shard_shapes[w], outs[w], 2 * cx + cy, 1 - c, bands[w])
                sends.append(_rcopy(landed, landed, *sem, sibling))
                waits.append(_rcopy(other, other, *sem, sibling))
        return sends, waits

    def start(ins, outs, sems):
        for cp in copies(ins, outs, sems)[0]:
            cp.start()

    def finish(ins, outs, sems):
        sends, waits = copies(ins, outs, sems)
        for cp in waits:
            cp.wait_recv()
        for cp in sends:
            cp.wait_send()

    return _Hosted(bufs, [_sds(b.shape, b.dtype) for b in bufs], 3 * n, start, finish, aliases={w: w for w in range(n)})


def _plan_swap(grads):
    def copies(ins, outs, sems):
        send_sems, recv_sems = sems
        x, y, c, _ = _place()
        cps = []
        for w, g_ref in enumerate(ins):
            if len(g_ref.shape) == 4:
                theirs = g_ref.at[:, 1 - c]
            else:
                hr = g_ref.shape[0] // 2
                theirs = g_ref.at[pl.ds((1 - c) * hr, hr), :]
            cps.append(_rcopy(theirs, outs[w], send_sems.at[w], recv_sems.at[w], (x, y, 1 - c)))
        return cps

    def start(ins, outs, sems):
        for cp in copies(ins, outs, sems):
            cp.start()

    def finish(ins, outs, sems):
        for cp in copies(ins, outs, sems):
            cp.wait()

    out_shapes = [_sds((g.shape[0], g.shape[2], g.shape[3])) if g.ndim == 4 else _sds((g.shape[0] // 2, g.shape[1]))
                  for g in grads]
    return _Hosted(grads, out_shapes, len(grads), start, finish)


def _plan_scatter(chip_sums, windows):
    def copies(ins, outs, sems):
        send_sems, recv_sems = sems
        x, y, c, chips = _place()
        me = 2 * x + y
        sends, waits = [], []
        for w, s_ref in enumerate(ins):
            for k, (cx, cy) in enumerate(chips):
                tgt = 2 * cx + cy
                if windows[w] is not None:
                    stride, width = windows[w]
                    part = s_ref.at[:, pl.ds(pl.multiple_of(tgt * stride, LANES), width)]
                else:
                    part = s_ref.at[tgt]
                sem = (send_sems.at[3 * w + k], recv_sems.at[3 * w + k])
                sends.append(_rcopy(part, outs[w].at[me], *sem, (cx, cy, c)))
                slot = outs[w].at[tgt]
                waits.append(_rcopy(slot, slot, *sem, (cx, cy, c)))
        return sends, waits

    def start(ins, outs, sems):
        for cp in copies(ins, outs, sems)[0]:
            cp.start()

    def finish(ins, outs, sems):
        sends, waits = copies(ins, outs, sems)
        for cp in waits:
            cp.wait_recv()
        for cp in sends:
            cp.wait_send()

    out_shapes = [_sds((N_CHIPS, s.shape[0], win[1]), BF16) if win is not None else _sds(s.shape, BF16)
                  for s, win in zip(chip_sums, windows)]
    return _Hosted(chip_sums, out_shapes, 3 * len(chip_sums), start, finish)


def _plan_join(reds):
    def copies(ins, outs, sems):
        send_sems, recv_sems = sems
        x, y, c, _ = _place()
        sends, waits = [], []
        for w, out in enumerate(outs):
            hr = out.shape[0] // 2
            mine = out.at[pl.ds(c * hr, hr), :]
            theirs = out.at[pl.ds((1 - c) * hr, hr), :]
            sends.append(_rcopy(mine, mine, send_sems.at[w], recv_sems.at[w], (x, y, 1 - c)))
            waits.append(_rcopy(theirs, theirs, send_sems.at[w], recv_sems.at[w], (x, y, 1 - c)))
        return sends, waits

    def start(ins, outs, sems):
        for cp in copies(ins, outs, sems)[0]:
            cp.start()

    def finish(ins, outs, sems):
        sends, waits = copies(ins, outs, sems)
        for cp in waits:
            cp.wait_recv()
        for cp in sends:
            cp.wait_send()

    return _Hosted(reds, [_sds(r.shape) for r in reds], len(reds), start, finish, aliases={w: w for w in range(len(reds))})


def _allreduce_small(v):
    m_per = v.shape[0]

    def body(v_ref, out_ref, all_ref, send_sems, recv_sems, local_sem):
        x, y, c, chips = _place()
        me, sibling = (x, y, c), (x, y, 1 - c)

        def rows(px, py, pc):
            return all_ref.at[pl.ds((4 * px + 2 * py + pc) * m_per, m_per), :]

        def copy(k, block, to, src=None):
            return _rcopy(rows(*block) if src is None else src, rows(*block), send_sems.at[k], recv_sems.at[k], to)

        mine = pltpu.make_async_copy(v_ref, rows(*me), local_sem)
        mine.start()
        first = [copy(0, me, sibling, src=v_ref)]
        first += [copy(1 + k, me, (*chip, c), src=v_ref) for k, chip in enumerate(chips)]
        for cp in first:
            cp.start()
        passed = [copy(4 + k, (*chip, c), sibling) for k, chip in enumerate(chips)]
        for k, chip in enumerate(chips):
            copy(1 + k, (*chip, c), me).wait_recv()
            passed[k].start()
        copy(0, sibling, me).wait_recv()
        for k, chip in enumerate(chips):
            copy(4 + k, (*chip, 1 - c), me).wait_recv()
        for cp in first + passed:
            cp.wait_send()
        mine.wait()
        acc = all_ref[pl.ds(0, m_per), :]
        for d in range(1, 8):
            acc = acc + all_ref[pl.ds(d * m_per, m_per), :]
        out_ref[...] = acc

    vm = pl.BlockSpec(memory_space=pltpu.VMEM)
    return _pallas(body, name="allreduce_small", in_specs=[vm], out_specs=vm, out_shape=_sds((m_per, LANES)),
                          scratch_shapes=[pltpu.VMEM((8 * m_per, LANES), F32), pltpu.SemaphoreType.DMA((7,)),
                                          pltpu.SemaphoreType.DMA((7,)), pltpu.SemaphoreType.DMA],
                          compiler_params=pltpu.CompilerParams(vmem_limit_bytes=VMEM_LIMIT))(v)


def _block_diag(blocks):
    j, g, a, b = blocks.shape
    eye = jnp.eye(g, dtype=bool)[None, :, None, :, None]
    return jnp.where(eye, blocks[:, :, :, None, :], jnp.zeros((), blocks.dtype)).reshape(j, g * a, g * b)


def _diag_blocks(m, a, b):
    j = m.shape[0]
    g = m.shape[1] // a
    t = m.reshape(j, g, a, g, b)
    eye = jnp.eye(g, dtype=bool)[None, :, None, :, None]
    return jnp.sum(jnp.where(eye, t, 0.0), axis=3)


_SMALL = (("g_mix", (1024,)), ("b_f", (8,)), ("g_q", (64,)), ("g_k", (64,)), ("lambda_re", (32, 64)),
          ("lambda_im", (32, 64)), ("log_step", (32,)), ("b_re", (32, 64, 16)), ("b_im", (32, 64, 16)),
          ("c_re", (32, 16, 64)), ("c_im", (32, 16, 64)), ("d_skip", (32, 16)), ("b_glu", (512,)),
          ("g_attn_out", (512,)), ("g_ssm_out", (512,)), ("g_ffn", (1024,)), ("conv_b", (5632,)))


def _packed_rows(n):
    tile = SUBLANES * LANES
    return -(-n // tile) * SUBLANES


def _pack_small(arrs):
    parts = []
    for a in arrs:
        flat = a.reshape(-1)
        rows = _packed_rows(flat.shape[0])
        parts.append(jnp.pad(flat, (0, rows * LANES - flat.shape[0])).reshape(rows, LANES))
    return jnp.concatenate(parts, axis=0)


def _unpack_small(buf, shapes):
    out, r = [], 0
    for shape in shapes:
        n = math.prod(shape)
        out.append(buf[r:r + _packed_rows(n)].reshape(-1)[:n].reshape(shape))
        r += _packed_rows(n)
    return out


def _halves(t):
    return t.reshape(N_CHIPS, 2, t.shape[0] // (2 * N_CHIPS), t.shape[1])


class _MeshComm:
    def __init__(self, args):
        x, y, self.core = lax.axis_index("x"), lax.axis_index("y"), lax.axis_index("c")
        self.chip = 2 * x + y
        self.place = jnp.stack([self.chip, self.core]).astype(jnp.int32)
        self.shards = {n: args[n].astype(BF16) for n in ("w_in", "w_glu", "w_out", "w_up", "w_down")}
        self.conv_w = args["conv_w"]

    def _own(self, stacked, mine):
        return lax.dynamic_update_slice(stacked, mine[None], (self.chip,) + (0,) * mine.ndim)

    def w_in(self):
        sh = self.shards["w_in"]
        (buf,) = _run_hosted(_then(_plan_gather_ici([sh], [False]), _plan_gather_d2d([sh], [sh])), name="gather_w_in")
        whole = self._own(buf, sh).transpose(1, 0, 2).reshape(D_MODEL, IN_COLS)
        return jnp.pad(whole, ((0, 0), (0, Z_COLS - IN_COLS)))

    def gather_first(self):
        self.mid = [self.shards[n] for n in ("w_glu", "w_out", "w_down")]
        return _plan_gather_ici(self.mid + [self.shards["w_up"]], [False, False, False, True], whole=[self.conv_w],
                                bands=[WHOLE_HALF] * 3 + [(0, 1, 4)])

    def gather_second(self, landed):
        self.g_cw = landed[4]
        return _both(_plan_gather_d2d(list(landed[:3]), self.mid),
                     _plan_gather_ici([self.shards["w_up"]], [True], bands=[(1, 3, 4)], into=[landed[3]]))

    def weights(self, gathered):
        g_glu, g_out, g_down = gathered[:3]
        own = self._own
        return (own(g_glu, self.mid[0]).reshape(SSM_W, SSM_W), own(g_out, self.mid[1]).reshape(D_MODEL, D_MODEL),
                own(g_down, self.mid[2]).reshape(D_FF, D_MODEL),
                own(self.g_cw, self.conv_w).transpose(1, 0, 2).reshape(3, 2 * D_FF))

    def gather_third(self, gathered):
        return _plan_gather_d2d([gathered[3]], [self.shards["w_up"]])

    def w_up(self, passed):
        return _place_cols(passed[0], self.shards["w_up"], self.place)

    def swap(self, d_w_down, d_w_up, d_w_glu, d_w_out):
        self.early = [_halves(d_w_down), d_w_up, _halves(d_w_glu), _halves(d_w_out)]
        return _plan_swap(self.early)

    def scatter(self, landed):
        self.early_sums = [_add_half(g, l, self.place, name="add_" + n)
                           for g, l, n in zip(self.early, landed, ("w_down", "w_up", "w_glu", "w_out"))]
        return _plan_scatter(self.early_sums, [None, (UP_COLS, UP_COLS), None, None])

    def swap_in(self, d_w_in):
        self.d_in = d_w_in
        return _plan_swap([d_w_in])

    def scatter_in(self, landed):
        self.sum_in = _add_half(self.d_in, landed[0], self.place, name="add_w_in")
        return _plan_scatter([self.sum_in], [(IN_STRIDE, IN_WINDOW)])

    def reduce(self, lands):
        early_lands, (land_in,) = lands
        sum_in = self.sum_in
        es, el = self.early_sums, early_lands
        todo = [(sum_in, land_in, "w_in", LANES, IN_STRIDE), (es[2], el[2], "w_glu", SSM_W, 0),
                (es[3], el[3], "w_out", D_MODEL, 0), (es[1], el[1], "w_up", UP_COLS, UP_COLS),
                (es[0], el[0], "w_down", D_MODEL, 0)]
        reds = _run_hosted(_plan_join([_sum_chips(s, l, self.place, name="sum_" + n, tc=tc, window_stride=st)
                                       for s, l, n, tc, st in todo]), name="join_halves")
        g_big = dict(zip(("w_in", "w_glu", "w_out", "w_up", "w_down"), reds))
        g_big["w_in"] = lax.dynamic_slice_in_dim(reds[0], 2 * self.chip, IN_COLS // N_CHIPS, axis=1)
        return g_big


def _local_step(x, tgt, p, comm):
    s = x.shape[0]
    row = lambda v: v.reshape(1, -1)
    g_mix, g_ffn = row(p["g_mix"]), row(p["g_ffn"])
    g_att, g_ssm, b_glu, conv_b = row(p["g_attn_out"]), row(p["g_ssm_out"]), row(p["b_glu"]), row(p["conv_b"])
    gq = row(jnp.tile(p["g_q"], HEADS))
    gk = row(jnp.tile(p["g_k"], HEADS))
    bf = row(jnp.pad(p["b_f"], (0, LANES - HEADS)))
    gg = jnp.kron(jnp.eye(HEADS, dtype=F32), jnp.ones((HEAD_DIM, HEAD_DIM), F32)).astype(BF16)
    dsk = row(p["d_skip"])

    rep = lambda a: jnp.repeat(a, SSM_GROUP, axis=0)
    lr, li = rep(p["lambda_re"]), rep(p["lambda_im"])
    ls = rep(jnp.broadcast_to(p["log_step"][:, None], (SSM_GROUPS, SSM_STATE)))
    bt_re = p["b_re"].transpose(0, 2, 1).reshape(_PARAM_SHAPE)
    bt_im = p["b_im"].transpose(0, 2, 1).reshape(_PARAM_SHAPE)
    a_re_rep, a_im_rep, bb_re, bb_im = _ssm_params(lr, li, ls, bt_re, bt_im)
    ar = a_re_rep[::SSM_GROUP].reshape(SSM_CHUNKS, 1, CHUNK_S)
    ai = a_im_rep[::SSM_GROUP].reshape(SSM_CHUNKS, 1, CHUNK_S)
    chunked = lambda t: t.reshape(SSM_CHUNKS, SSM_GROUPS // SSM_CHUNKS, SSM_GROUP, SSM_STATE)
    bbr = _block_diag(chunked(bb_re)).astype(BF16)
    bbi = _block_diag(chunked(bb_im)).astype(BF16)
    to_cc = lambda c: _block_diag(chunked(c).transpose(0, 1, 3, 2)).astype(BF16)
    ccr, cci = to_cc(p["c_re"]), to_cc(p["c_im"])

    w_in_r = comm.w_in()
    hb, z = _in_proj(x, g_mix, w_in_r)
    qh, kh, vh, ub, uf, c128 = _attn_prep(z, gq, gk, bf, gg)
    crow = c128[:, :HEADS].T.reshape(HEADS, 1, s)
    (oh, lse), landed = _attn_fwd(qh, kh, vh, crow, comm.gather_first())
    (xr, xi, y), gathered = _ssm_fwd(ub, uf, bbr, bbi, ar, ai, ccr, cci, dsk, comm.gather_second(landed))
    w_glu_b, w_out_b, w_down_b, conv_w_full = comm.weights(gathered)
    (x1, mixb, h2b), passed = _mix_out(y, oh, x, w_glu_b, b_glu, g_att, g_ssm, w_out_b, g_ffn, comm.gather_third(gathered))
    w_up_b = comm.w_up(passed)
    up = _mm(h2b, w_up_b, name="ffn_up", tm=1024, tn=1408, tk=1024)
    act = _conv_act(up, conv_w_full, conv_b)
    dy, dyb, loss_blk = _down_loss(act, w_down_b, x1, tgt)

    d_w_down = _mm(act, dyb, ta=True, name="d_w_down", tm=1408, tn=1024, tk=2048)
    dact = _mm(dyb, w_down_b, tb=True, name="d_act", tm=1024, tn=1408, tk=1024)
    dupb, dcw = _conv_act_bwd(up, dact, conv_w_full, conv_b)
    d_w_up = _mm(h2b, dupb, ta=True, b_parts=2, name="d_w_up", tm=1024, tn=1408, tk=2048)
    dh2 = _mm(dupb, w_up_b, tb=True, a_parts=2, name="d_h2", tm=1024, tn=1024, tk=1408)
    dx1, dx1b, doh, dys, d_w_glu, d_g_ffn, d_g_att, d_g_ssm, d_b_glu = _mix_bwd(
        dy, dh2, x1, g_ffn, w_out_b, y, oh, w_glu_b, b_glu, g_att, g_ssm)
    d_w_out = _mm(mixb, dx1b, ta=True, name="d_w_out", tm=1024, tn=1024, tk=2048)
    (du, dbbr, dbbi, dccr, dcci, dar, dai, dd), swapped = _ssm_bwd(dys, uf, ub, xr, xi, bbr, bbi, ar, ai, ccr, cci, dsk,
                                                                comm.swap(d_w_down, d_w_up, d_w_glu, d_w_out))
    (dqh, dkh, dvh, dcrow), early_lands = _attn_bwd(qh, kh, vh, crow, lse, doh, comm.scatter(swapped))
    dc128 = jnp.pad(dcrow.reshape(HEADS, s).T, ((0, 0), (0, LANES - HEADS)))
    dzb, d_gq, d_gk, d_bf = _prep_bwd(z, dqh, dkh, dvh, du, dc128, gq, gk, bf, gg)
    d_w_in_r = _mm(hb, dzb, ta=True, name="d_w_in", tm=512, tn=Z_COLS, tk=2048)
    dh, swapped_in = _mm(dzb, w_in_r, tb=True, name="d_h", tm=1024, tn=1024, tk=Z_COLS, carry=True,
                         hosted=comm.swap_in(d_w_in_r))
    (dx, d_g_mix), land_in = _in_norm_bwd(x, g_mix, dh, dx1, comm.scatter_in(swapped_in))

    unchunk = lambda t: t.reshape(_PARAM_SHAPE)
    dbb_re = unchunk(_diag_blocks(dbbr, SSM_GROUP, SSM_STATE))
    dbb_im = unchunk(_diag_blocks(dbbi, SSM_GROUP, SSM_STATE))
    first_row = (jnp.arange(_PARAM_SHAPE[0]) % SSM_GROUP == 0)[:, None]
    da_re = jnp.where(first_row, rep(dar.reshape(SSM_GROUPS, SSM_STATE)), 0.0)
    da_im = jnp.where(first_row, rep(dai.reshape(SSM_GROUPS, SSM_STATE)), 0.0)
    expand_t = (jnp.arange(SSM_GROUPS)[:, None] == (jnp.arange(_PARAM_SHAPE[0]) // SSM_GROUP)[None, :]).astype(BF16)
    d_lr, d_li, d_ls, d_bt_re, d_bt_im = _ssm_params_bwd(lr, li, ls, bt_re, bt_im, da_re, da_im, dbb_re, dbb_im, expand_t)
    from_bt = lambda t: t.reshape(SSM_GROUPS, SSM_GROUP, SSM_STATE).transpose(0, 2, 1)
    from_cc = lambda t: _diag_blocks(t, SSM_STATE, SSM_GROUP).transpose(0, 1, 3, 2).reshape(SSM_GROUPS, SSM_GROUP, SSM_STATE)

    small = {
        "g_mix": d_g_mix, "b_f": d_bf[0, :HEADS], "g_q": d_gq.reshape(HEADS, HEAD_DIM).sum(0),
        "g_k": d_gk.reshape(HEADS, HEAD_DIM).sum(0), "lambda_re": d_lr, "lambda_im": d_li, "log_step": d_ls,
        "b_re": from_bt(d_bt_re), "b_im": from_bt(d_bt_im), "c_re": from_cc(dccr), "c_im": from_cc(dcci),
        "d_skip": dd, "b_glu": d_b_glu, "g_attn_out": d_g_att, "g_ssm_out": d_g_ssm, "g_ffn": d_g_ffn,
        "conv_b": dcw[:, 3],
    }
    big = {"w_in": d_w_in_r, "w_glu": d_w_glu, "w_out": d_w_out, "w_up": d_w_up, "w_down": d_w_down}
    return loss_blk[0, 0], dx, big, small, dcw[:, 0:3].transpose(1, 0, 2).reshape(3, 2 * D_FF), (early_lands, land_in)


def kernel(x, g_mix, w_in, b_f, g_q, g_k, lambda_re, lambda_im, log_step, b_re, b_im, c_re, c_im, d_skip, w_glu, b_glu, g_attn_out, g_ssm_out, w_out, g_ffn, w_up, conv_w, conv_b, w_down, loss_target, m_g_mix, m_w_in, m_b_f, m_g_q, m_g_k, m_lambda_re, m_lambda_im, m_log_step, m_b_re, m_b_im, m_c_re, m_c_im, m_d_skip, m_w_glu, m_b_glu, m_g_attn_out, m_g_ssm_out, m_w_out, m_g_ffn, m_w_up, m_conv_w, m_conv_b, m_w_down, v_g_mix, v_w_in, v_b_f, v_g_q, v_g_k, v_lambda_re, v_lambda_im, v_log_step, v_b_re, v_b_im, v_c_re, v_c_im, v_d_skip, v_w_glu, v_b_glu, v_g_attn_out, v_g_ssm_out, v_w_out, v_g_ffn, v_w_up, v_conv_w, v_conv_b, v_w_down):
    args = dict(locals())
    order = ["g_mix", "w_in", "b_f", "g_q", "g_k", "lambda_re", "lambda_im", "log_step", "b_re", "b_im", "c_re", "c_im",
             "d_skip", "w_glu", "b_glu", "g_attn_out", "g_ssm_out", "w_out", "g_ffn", "w_up", "conv_w", "conv_b", "w_down"]
    comm = _MeshComm(args)
    chip = comm.chip
    loss_part, dx, big, small, d_conv_w, lands = _local_step(x[0], loss_target[0], args, comm)

    g_big = comm.reduce(lands)

    small_names = [n for n, _ in _SMALL]
    small_shapes = [sh for _, sh in _SMALL]
    gsum = _allreduce_small(_pack_small([small[n] for n in small_names] + [d_conv_w, loss_part]))
    g_small = _unpack_small(gsum, small_shapes + [(3, 2 * D_FF), ()])
    loss = g_small[-1]
    g_conv_w = lax.dynamic_slice_in_dim(g_small[-2], chip * (2 * D_FF // N_CHIPS), 2 * D_FF // N_CHIPS, axis=1)
    g_small = dict(zip(small_names, g_small[:-2]))

    grad, delta, new_m, new_v = {}, {}, {}, {}
    for n in ("w_in", "w_glu", "w_out", "w_up", "w_down"):
        grad[n] = g_big[n]
        delta[n], new_m[n], new_v[n] = _adamw(args[n], g_big[n], args["m_" + n], args["v_" + n], name="adamw_" + n)
    grad["conv_w"] = g_conv_w
    delta["conv_w"], new_m["conv_w"], new_v["conv_w"] = _adamw(conv_w, g_conv_w, m_conv_w, v_conv_w, name="adamw_conv_w")
    stepped = _adamw_small([args[n] for n in small_names], [g_small[n] for n in small_names],
                           [args["m_" + n] for n in small_names], [args["v_" + n] for n in small_names])
    for i, n in enumerate(small_names):
        grad[n] = g_small[n]
        delta[n], new_m[n], new_v[n] = stepped[3 * i:3 * i + 3]

    return (loss, dx[None], *[grad[n] for n in order], *[delta[n] for n in order], *[new_m[n] for n in order],
            *[new_v[n] for n in order])
```

```python
import math

import jax
import jax.numpy as jnp
from jax import lax
from jax.experimental import pallas as pl
from jax.experimental.pallas import tpu as pltpu

F32 = jnp.float32
BF16 = jnp.bfloat16

D_MODEL = 1024
HEADS = 8
HEAD_DIM = 64
ATTN_W = 512
SSM_W = 512
SSM_GROUPS = 32
SSM_GROUP = 16
SSM_STATE = 64
N_STATE = SSM_GROUPS * SSM_STATE
D_FF = 2816
IN_COLS = 2056
Z_COLS = 2176
F_COL0 = 1536
U_COL0 = 1544
EPS = 1e-6
NEG_INF = -1e30
N_CHIPS = 4
LANES = 128
SUBLANES = 8
SSM_CHUNKS = 2
CHUNK_U = SSM_W // SSM_CHUNKS
CHUNK_S = N_STATE // SSM_CHUNKS
HEADS_PER_STEP = 4
STRIP = 128
N_STRIPS = D_FF // STRIP

ADAM_LR = 0.001
ADAM_B1 = 0.9
ADAM_B2 = 0.999
ADAM_EPS = 1e-08
ADAM_WD = 0.01
ADAM_STEP = 10

VMEM_LIMIT = 56 * 1024 * 1024
MESH = pl.DeviceIdType.MESH


def _pallas(body, **kw):
    return pl.pallas_call(body, **kw)


def _pcall(body, *, name, out_shape, in_specs, out_specs, grid=(), scratch_shapes=(), dims=None):
    params = pltpu.CompilerParams(dimension_semantics=dims, vmem_limit_bytes=VMEM_LIMIT)
    return _pallas(body, name=name, grid=grid, in_specs=in_specs, out_specs=out_specs,
                   out_shape=out_shape, scratch_shapes=scratch_shapes, compiler_params=params)


def _sds(shape, dtype=F32):
    return jax.ShapeDtypeStruct(shape, dtype)


def _dot(a, b):
    return jnp.dot(a, b, preferred_element_type=F32)


def _dot_nt(a, b):
    return lax.dot_general(a, b, (((1,), (1,)), ((), ())), preferred_element_type=F32)


def _dot_tn(a, b):
    return lax.dot_general(a, b, (((0,), (0,)), ((), ())), preferred_element_type=F32)


def _split3(x):
    hi = x.astype(BF16)
    r = x - hi.astype(F32)
    mid = r.astype(BF16)
    lo = (r - mid.astype(F32)).astype(BF16)
    return hi, mid, lo


def _dot_exact_r(x, m01):
    hi, mid, lo = _split3(x)
    return _dot(hi, m01) + _dot(mid, m01) + _dot(lo, m01)


def _dot_exact_l(m01, x):
    hi, mid, lo = _split3(x)
    return _dot(m01, hi) + _dot(m01, mid) + _dot(m01, lo)


def _sigmoid(x):
    return 1.0 / (1.0 + jnp.exp(-x))


def _rms(x, g):
    r = lax.rsqrt(jnp.mean(x * x, axis=-1, keepdims=True) + EPS)
    return x * r * g


def _rms_bwd(x, g, dy):
    r = lax.rsqrt(jnp.mean(x * x, axis=-1, keepdims=True) + EPS)
    w = dy * g
    dx = r * w - x * (r * r * r) * jnp.mean(w * x, axis=-1, keepdims=True)
    dg = jnp.sum(dy * x * r, axis=0, keepdims=True)
    return dx, dg


_GELU_K = math.sqrt(2.0 / math.pi)
_GELU_C = 0.044715


def _gelu(y):
    return y * (0.5 * (1.0 + jnp.tanh(_GELU_K * (y + _GELU_C * (y * y * y)))))


def _gelu_grad(y):
    t = jnp.tanh(_GELU_K * (y + _GELU_C * (y * y * y)))
    return 0.5 * (1.0 + t) + 0.5 * y * (1.0 - t * t) * (_GELU_K * (1.0 + 3.0 * _GELU_C * y * y))


def _tile(n, pref):
    if n <= pref:
        return n
    divs = [t for t in range(LANES, n + 1, LANES) if n % t == 0]
    below = [t for t in divs if t <= pref]
    if below and 2 * below[-1] >= pref:
        return below[-1]
    above = [t for t in divs if t > pref]
    return above[0] if above else n


def _row_tile(s):
    return min(256, s)


def _mm(a, b, *, name, tm, tn, tk, ta=False, tb=False, a_parts=1, b_parts=1, carry=False, hosted=None):
    if a_parts > 1:
        m, kk = a.shape[1], a.shape[2] * a_parts
    elif ta:
        kk, m = a.shape
    else:
        m, kk = a.shape
    if b_parts > 1:
        n = b.shape[2] * b_parts
    else:
        n = b.shape[0] if tb else b.shape[1]
    tm, tn, tk = _tile(m, tm), _tile(n // b_parts, tn), _tile(kk // a_parts, tk)
    k_per, n_per = kk // a_parts // tk, n // b_parts // tn

    def body(a_ref, b_ref, o_ref):
        k = pl.program_id(2)
        if ta:
            part = _dot_tn(a_ref[...], b_ref[...])
        elif tb:
            part = _dot_nt(a_ref[...], b_ref[...])
        else:
            part = _dot(a_ref[...], b_ref[...])

        @pl.when(k == 0)
        def _():
            o_ref[...] = part

        @pl.when(k > 0)
        def _():
            o_ref[...] += part

    if a_parts > 1:
        a_spec = pl.BlockSpec((None, tm, tk), lambda i, j, k: (k // k_per, i, k % k_per))
    else:
        a_spec = pl.BlockSpec((tk, tm), lambda i, j, k: (k, i)) if ta else pl.BlockSpec((tm, tk), lambda i, j, k: (i, k))
    if b_parts > 1:
        b_spec = pl.BlockSpec((None, tk, tn), lambda i, j, k: (j // n_per, k, j % n_per))
    else:
        b_spec = pl.BlockSpec((tn, tk), lambda i, j, k: (j, k)) if tb else pl.BlockSpec((tk, tn), lambda i, j, k: (k, j))
    grid = (m // tm, n // tn, kk // tk)
    at = lambda step: (lambda: jnp.logical_and(jnp.logical_and(pl.program_id(0) == step[0], pl.program_id(1) == step[1]),
                                               pl.program_id(2) == step[2]))
    (out,), carried = _host_pcall(body, hosted, at((0, 0, 0)), at(tuple(g - 1 for g in grid)), n_in=2, n_out=1, n_scratch=0,
                                  name=name, grid=grid, in_specs=[a_spec, b_spec],
                                  out_specs=[pl.BlockSpec((tm, tn), lambda i, j, k: (i, j))], out_shape=[_sds((m, n))],
                                  scratch_shapes=[], dims=("parallel", "parallel", "arbitrary"), operands=(a, b))
    return (out, carried) if carry else out


def _in_proj(x, g_mix, w_in_r):
    s = x.shape[0]
    tm = _row_tile(s)

    def body(x_ref, g_ref, w_ref, h_ref, z_ref):
        h = _rms(x_ref[...], g_ref[...]).astype(BF16)
        h_ref[...] = h
        z_ref[...] = _dot(h, w_ref[...])

    return _pcall(body, name="in_proj", grid=(s // tm,),
                  in_specs=[pl.BlockSpec((tm, D_MODEL), lambda i: (i, 0)), pl.BlockSpec((1, D_MODEL), lambda i: (0, 0)),
                            pl.BlockSpec((D_MODEL, Z_COLS), lambda i: (0, 0))],
                  out_specs=[pl.BlockSpec((tm, D_MODEL), lambda i: (i, 0)), pl.BlockSpec((tm, Z_COLS), lambda i: (i, 0))],
                  out_shape=[_sds((s, D_MODEL), BF16), _sds((s, Z_COLS))], dims=("parallel",))(x, g_mix, w_in_r)


def _split_heads(ref, val):
    for h in range(HEADS):
        ref[h] = val[:, h * HEAD_DIM:(h + 1) * HEAD_DIM].astype(ref.dtype)


def _merge_heads(ref):
    return jnp.concatenate([ref[h].astype(F32) for h in range(HEADS)], axis=-1)


def _forget_logits(z_ref, bf_ref):
    fl = z_ref[:, F_COL0:F_COL0 + LANES] + bf_ref[...]
    return jnp.where(lax.broadcasted_iota(jnp.int32, fl.shape, 1) < HEADS, fl, 0.0)


def _attn_prep(z, gq, gk, bf, gg):
    s = z.shape[0]
    tm = _row_tile(s)

    def body(z_ref, gq_ref, gk_ref, bf_ref, gg_ref, qn_ref, kn_ref, vb_ref, ub_ref, uf_ref, c_ref, carry_ref):
        i = pl.program_id(0)

        @pl.when(i == 0)
        def _():
            carry_ref[...] = jnp.zeros_like(carry_ref)

        gg_m = gg_ref[...]

        def head_norm(t, g):
            ssq = _dot_exact_r(t * t, gg_m)
            return t * lax.rsqrt(ssq * (1.0 / HEAD_DIM) + EPS) * g

        _split_heads(qn_ref, head_norm(z_ref[:, 0:ATTN_W], gq_ref[...]))
        _split_heads(kn_ref, head_norm(z_ref[:, ATTN_W:2 * ATTN_W], gk_ref[...]))
        _split_heads(vb_ref, z_ref[:, 2 * ATTN_W:3 * ATTN_W])
        u = z_ref[:, U_COL0:U_COL0 + SSM_W]
        uf_ref[...] = u
        ub_ref[...] = u.astype(BF16)
        fl = _forget_logits(z_ref, bf_ref)
        lf = jnp.minimum(fl, 0.0) - jnp.log1p(jnp.exp(-jnp.abs(fl)))
        row = lax.broadcasted_iota(jnp.int32, (tm, tm), 0)
        col = lax.broadcasted_iota(jnp.int32, (tm, tm), 1)
        tri = (row >= col).astype(BF16)
        c = _dot_exact_l(tri, lf) + carry_ref[...]
        c_ref[...] = c
        carry_ref[...] = c[tm - 1:tm, :]

    row_spec = lambda w: pl.BlockSpec((tm, w), lambda i: (i, 0))
    const = lambda shape: pl.BlockSpec(shape, lambda i: (0, 0))
    heads = pl.BlockSpec((HEADS, tm, HEAD_DIM), lambda i: (0, i, 0))
    return _pcall(body, name="attn_prep", grid=(s // tm,),
                  in_specs=[row_spec(Z_COLS), const((1, ATTN_W)), const((1, ATTN_W)), const((1, LANES)), const((ATTN_W, ATTN_W))],
                  out_specs=[heads] * 3 + [row_spec(SSM_W), row_spec(SSM_W), row_spec(LANES)],
                  out_shape=[_sds((HEADS, s, HEAD_DIM), BF16)] * 3 + [_sds((s, SSM_W), BF16), _sds((s, SSM_W)), _sds((s, LANES))],
                  scratch_shapes=[pltpu.VMEM((1, LANES), F32)], dims=("arbitrary",))(z, gq, gk, bf, gg)


def _attn_fwd(qh, kh, vh, crow, hosted=None):
    _, s, _ = qh.shape
    tq = _row_tile(s)
    scale = HEAD_DIM ** -0.5

    hp = HEADS
    nq = s // tq
    fold = lambda t, op: op(t[:, :tq // 2], t[:, tq // 2:])

    def body(q_ref, k_ref, v_ref, c_ref, o_ref, lse_ref, s_s):
        i = pl.program_id(1)

        def first(j, ms, diagonal):
            off = pl.multiple_of(j * tq, tq)
            out = []
            for hh in range(hp):
                sc = _dot_nt(q_ref[hh], k_ref[hh, pl.ds(off, tq), :]) * scale - c_ref[hh, :, pl.ds(off, tq)]
                if diagonal:
                    causal = lax.broadcasted_iota(jnp.int32, (tq, tq), 1) <= lax.broadcasted_iota(jnp.int32, (tq, tq), 0)
                    sc = jnp.where(causal, sc, NEG_INF)
                s_s[hh, j] = sc
                out.append(jnp.maximum(ms[hh], fold(sc, jnp.maximum)))
            return tuple(out)

        ms = lax.fori_loop(0, i, lambda j, c: first(j, c, False), (jnp.full((tq, tq // 2), NEG_INF, F32),) * hp)
        ms = [jnp.max(t, axis=-1, keepdims=True) for t in first(i, ms, True)]

        def second(j, carry):
            rows = pl.ds(pl.multiple_of(j * tq, tq), tq)
            out = []
            for hh in range(hp):
                ls, acc = carry[hh]
                p = jnp.exp(s_s[hh, j] - ms[hh])
                out.append((ls + fold(p, jnp.add), acc + _dot(p.astype(BF16), v_ref[hh, rows, :])))
            return tuple(out)

        zero = (jnp.zeros((tq, tq // 2), F32), jnp.zeros((tq, HEAD_DIM), F32))
        for hh, (ls, acc) in enumerate(lax.fori_loop(0, i + 1, second, (zero,) * hp)):
            l = jnp.sum(ls, axis=-1, keepdims=True)
            o_ref[hh] = acc / l
            lse_ref[hh] = ms[hh] + jnp.log(l)

    blk = pl.BlockSpec((hp, tq, HEAD_DIM), lambda h, i: (h, i, 0))
    full = pl.BlockSpec((hp, s, HEAD_DIM), lambda h, i: (h, 0, 0))
    nh = HEADS // hp
    first = lambda: jnp.logical_and(pl.program_id(0) == 0, pl.program_id(1) == 0)
    last = lambda: jnp.logical_and(pl.program_id(0) == nh - 1, pl.program_id(1) == nq - 1)
    return _host_pcall(body, hosted, first, last, n_in=4, n_out=2, n_scratch=1, name="attn_fwd", grid=(nh, nq),
                       in_specs=[blk, full, full, pl.BlockSpec((hp, 1, s), lambda h, i: (h, 0, 0))],
                       out_specs=[blk, pl.BlockSpec((hp, tq, 1), lambda h, i: (h, i, 0))],
                       out_shape=[_sds((HEADS, s, HEAD_DIM)), _sds((HEADS, s, 1))],
                       scratch_shapes=[pltpu.VMEM((hp, nq, tq, tq), F32)],
                       dims=("parallel", "parallel"), operands=(qh, kh, vh, crow))


def _ssm_param_fn(lr, li, ls, br, bi):
    step = jnp.exp(ls)
    er = jnp.exp(lr * step)
    ab_re = er * jnp.cos(li * step)
    ab_im = er * jnp.sin(li * step)
    num_re = ab_re - 1.0
    num_im = ab_im
    den = lr * lr + li * li
    f_re = (num_re * lr + num_im * li) / den
    f_im = (num_im * lr - num_re * li) / den
    bb_re = f_re * br - f_im * bi
    bb_im = f_re * bi + f_im * br
    return ab_re, ab_im, bb_re, bb_im


_PARAM_SHAPE = (SSM_GROUPS * SSM_GROUP, SSM_STATE)


def _ssm_params(lr, li, ls, br, bi):
    def body(lr_ref, li_ref, ls_ref, br_ref, bi_ref, ar_ref, ai_ref, bbr_ref, bbi_ref):
        ar, ai, bbr, bbi = _ssm_param_fn(lr_ref[...], li_ref[...], ls_ref[...], br_ref[...], bi_ref[...])
        ar_ref[...] = ar
        ai_ref[...] = ai
        bbr_ref[...] = bbr
        bbi_ref[...] = bbi

    spec = pl.BlockSpec(_PARAM_SHAPE, lambda: (0, 0))
    return _pcall(body, name="ssm_params", in_specs=[spec] * 5, out_specs=[spec] * 4,
                  out_shape=[_sds(_PARAM_SHAPE)] * 4)(lr, li, ls, br, bi)


def _ssm_params_bwd(lr, li, ls, br, bi, dar, dai, dbbr, dbbi, expand_t):
    def body(lr_ref, li_ref, ls_ref, br_ref, bi_ref, dar_ref, dai_ref, dbbr_ref, dbbi_ref, et_ref,
             dlr_ref, dli_ref, dls_ref, dbr_ref, dbi_ref):
        _, vjp = jax.vjp(_ssm_param_fn, lr_ref[...], li_ref[...], ls_ref[...], br_ref[...], bi_ref[...])
        dlr, dli, dls, dbr, dbi = vjp((dar_ref[...], dai_ref[...], dbbr_ref[...], dbbi_ref[...]))
        et = et_ref[...]
        dlr_ref[...] = _dot_exact_l(et, dlr)
        dli_ref[...] = _dot_exact_l(et, dli)
        dls_ref[...] = jnp.sum(_dot_exact_l(et, dls), axis=-1, keepdims=True)
        dbr_ref[...] = dbr
        dbi_ref[...] = dbi

    spec = pl.BlockSpec(_PARAM_SHAPE, lambda: (0, 0))
    gspec = pl.BlockSpec((SSM_GROUPS, SSM_STATE), lambda: (0, 0))
    return _pcall(body, name="ssm_params_bwd",
                  in_specs=[spec] * 9 + [pl.BlockSpec((SSM_GROUPS, _PARAM_SHAPE[0]), lambda: (0, 0))],
                  out_specs=[gspec, gspec, pl.BlockSpec((SSM_GROUPS, 1), lambda: (0, 0)), spec, spec],
                  out_shape=[_sds((SSM_GROUPS, SSM_STATE))] * 2 + [_sds((SSM_GROUPS, 1))] + [_sds(_PARAM_SHAPE)] * 2,
                  )(lr, li, ls, br, bi, dar, dai, dbbr, dbbi, expand_t)


def _cmul(ar, ai, br, bi):
    return ar * br - ai * bi, ar * bi + ai * br


def _scan_consts(ar, ai, width, reverse):
    row = lax.broadcasted_iota(jnp.int32, (SUBLANES, width), 0)
    pw = [(ar, ai)]
    for _ in range(SUBLANES - 1):
        pw.append(_cmul(pw[-1][0], pw[-1][1], ar, ai))
    steps = []
    for d in (1, 2, 4):
        keep = (row < SUBLANES - d) if reverse else (row >= d)
        steps.append((d, jnp.where(keep, pw[d - 1][0], 0.0), jnp.where(keep, pw[d - 1][1], 0.0)))
    pr = jnp.zeros((SUBLANES, width), F32)
    pi = jnp.zeros((SUBLANES, width), F32)
    for r in range(SUBLANES):
        e = (SUBLANES - r) if reverse else (r + 1)
        pr = jnp.where(row == r, pw[e - 1][0], pr)
        pi = jnp.where(row == r, pw[e - 1][1], pi)
    return steps, pr, pi


def _scan_tile(xr, xi, cr, ci, consts, reverse):
    steps, pr, pi = consts
    for d, mr, mi in steps:
        sh = (SUBLANES - d) if reverse else d
        sr = pltpu.roll(xr, sh, 0)
        si = pltpu.roll(xi, sh, 0)
        xr, xi = xr + mr * sr - mi * si, xi + mr * si + mi * sr
    return xr + pr * cr - pi * ci, xi + pr * ci + pi * cr


def _ssm_fwd(ub, uf, bbr, bbi, ar, ai, ccr, cci, dsk, hosted=None):
    s = ub.shape[0]
    tm = _row_tile(s)
    nt = tm // SUBLANES

    def body(ub_ref, u_ref, bbr_ref, bbi_ref, ar_ref, ai_ref, ccr_ref, cci_ref, dsk_ref,
             xr_ref, xi_ref, y_ref, cr_s, ci_s):
        i = pl.program_id(1)

        @pl.when(i == 0)
        def _():
            cr_s[...] = jnp.zeros_like(cr_s)
            ci_s[...] = jnp.zeros_like(ci_s)

        u_b = ub_ref[...]
        xr_ref[...] = _dot(u_b, bbr_ref[0])
        xi_ref[...] = _dot(u_b, bbi_ref[0])
        consts = _scan_consts(ar_ref[0], ai_ref[0], CHUNK_S, False)

        def tile(k, carry):
            cr, ci = carry
            sl = pl.ds(pl.multiple_of(k * SUBLANES, SUBLANES), SUBLANES)
            xr, xi = _scan_tile(xr_ref[sl, :], xi_ref[sl, :], cr, ci, consts, False)
            xr_ref[sl, :] = xr
            xi_ref[sl, :] = xi
            return xr[SUBLANES - 1:SUBLANES, :], xi[SUBLANES - 1:SUBLANES, :]

        cr, ci = lax.fori_loop(0, nt, tile, (cr_s[...], ci_s[...]))
        cr_s[...] = cr
        ci_s[...] = ci
        y_ref[...] = (_dot(xr_ref[...].astype(BF16), ccr_ref[0]) - _dot(xi_ref[...].astype(BF16), cci_ref[0])
                      + dsk_ref[...] * u_ref[...])

    wspec = lambda a, b: pl.BlockSpec((1, a, b), lambda j, i: (j, 0, 0))
    nb = s // tm
    first = lambda: jnp.logical_and(pl.program_id(0) == 0, pl.program_id(1) == 0)
    last = lambda: jnp.logical_and(pl.program_id(0) == SSM_CHUNKS - 1, pl.program_id(1) == nb - 1)
    return _host_pcall(
        body, hosted, first, last, n_in=9, n_out=3, n_scratch=2, name="ssm_fwd", grid=(SSM_CHUNKS, nb),
        in_specs=[pl.BlockSpec((tm, CHUNK_U), lambda j, i: (i, j)),
                  pl.BlockSpec((tm, CHUNK_U), lambda j, i: (i, j)),
                  wspec(CHUNK_U, CHUNK_S), wspec(CHUNK_U, CHUNK_S), wspec(1, CHUNK_S), wspec(1, CHUNK_S),
                  wspec(CHUNK_S, CHUNK_U), wspec(CHUNK_S, CHUNK_U),
                  pl.BlockSpec((1, CHUNK_U), lambda j, i: (0, j))],
        out_specs=[pl.BlockSpec((tm, CHUNK_S), lambda j, i: (i, j)), pl.BlockSpec((tm, CHUNK_S), lambda j, i: (i, j)),
                   pl.BlockSpec((tm, CHUNK_U), lambda j, i: (i, j))],
        out_shape=[_sds((s, N_STATE)), _sds((s, N_STATE)), _sds((s, SSM_W))],
        scratch_shapes=[pltpu.VMEM((1, CHUNK_S), F32)] * 2,
        dims=("parallel", "arbitrary"), operands=(ub, uf, bbr, bbi, ar, ai, ccr, cci, dsk))


def _ssm_glu(y, w_glu, b_glu):
    ge = _gelu(y)
    sg = _sigmoid(_dot(ge.astype(BF16), w_glu) + b_glu)
    return ge, sg


def _mix_out(y, att, x, w_glu, b_glu, g_att, g_ssm, w_out, g_ffn, hosted=None):
    s = x.shape[0]
    tm = _row_tile(s)

    def body(y_ref, att_ref, x_ref, wg_ref, bg_ref, ga_ref, gs_ref, wo_ref, gf_ref, x1_ref, mix_ref, h2_ref):
        ge, sg = _ssm_glu(y_ref[...], wg_ref[...], bg_ref[...])
        ms = _rms(ge * sg, gs_ref[...]).astype(BF16)
        ma = _rms(_merge_heads(att_ref), ga_ref[...]).astype(BF16)
        mix_ref[:, 0:ATTN_W] = ma
        mix_ref[:, ATTN_W:D_MODEL] = ms
        x1 = x_ref[...] + (_dot(ma, wo_ref[0:ATTN_W, :]) + _dot(ms, wo_ref[ATTN_W:D_MODEL, :]))
        x1_ref[...] = x1
        h2_ref[...] = _rms(x1, gf_ref[...]).astype(BF16)

    row = lambda w: pl.BlockSpec((tm, w), lambda i: (i, 0))
    const = lambda a, b: pl.BlockSpec((a, b), lambda i: (0, 0))
    nb = s // tm
    return _host_pcall(body, hosted, lambda: pl.program_id(0) == 0, lambda: pl.program_id(0) == nb - 1,
                       n_in=9, n_out=3, n_scratch=0, name="mix_out", grid=(nb,),
                       in_specs=[row(SSM_W), pl.BlockSpec((HEADS, tm, HEAD_DIM), lambda i: (0, i, 0)), row(D_MODEL),
                                 const(SSM_W, SSM_W), const(1, SSM_W),
                                 const(1, ATTN_W), const(1, SSM_W), const(D_MODEL, D_MODEL), const(1, D_MODEL)],
                       out_specs=[row(D_MODEL)] * 3,
                       out_shape=[_sds((s, D_MODEL)), _sds((s, D_MODEL), BF16), _sds((s, D_MODEL), BF16)],
                       scratch_shapes=[], dims=("parallel",), operands=(y, att, x, w_glu, b_glu, g_att, g_ssm, w_out, g_ffn))


CONV_CHUNK = 64


def _conv_rows(pad_ref, w, b, r0, n):
    y = b + pad_ref[pl.ds(r0 + SUBLANES - 2, n), :] * w[0:1, :]
    y = y + pad_ref[pl.ds(r0 + SUBLANES - 1, n), :] * w[1:2, :]
    return y + pad_ref[pl.ds(r0 + SUBLANES, n), :] * w[2:3, :]


def _fill_front_pad(pad_ref, strip_ref, s):
    pad_ref[0:SUBLANES, :] = jnp.zeros((SUBLANES, STRIP), F32)
    for r0 in range(0, s, CONV_CHUNK):
        pad_ref[pl.ds(SUBLANES + r0, CONV_CHUNK), :] = strip_ref[pl.ds(r0, CONV_CHUNK), :]


def _conv_act(up, conv_w, conv_b):
    s = up.shape[0]

    def body(ug_ref, uv_ref, wg_ref, wv_ref, bg_ref, bv_ref, act_ref, pg_ref, pv_ref):
        _fill_front_pad(pg_ref, ug_ref, s)
        _fill_front_pad(pv_ref, uv_ref, s)
        wg, wv, bg, bv = wg_ref[...], wv_ref[...], bg_ref[...], bv_ref[...]
        for r0 in range(0, s, CONV_CHUNK):
            hg = _conv_rows(pg_ref, wg, bg, r0, CONV_CHUNK)
            hv = _conv_rows(pv_ref, wv, bv, r0, CONV_CHUNK)
            act_ref[pl.ds(r0, CONV_CHUNK), :] = (hg * _sigmoid(hg) * hv).astype(BF16)

    strip = lambda off: pl.BlockSpec((s, STRIP), lambda j: (0, j + off))
    wsp = lambda off: pl.BlockSpec((3, STRIP), lambda j: (0, j + off))
    bsp = lambda off: pl.BlockSpec((1, STRIP), lambda j: (0, j + off))
    return _pcall(body, name="conv_act", grid=(N_STRIPS,),
                  in_specs=[strip(0), strip(N_STRIPS), wsp(0), wsp(N_STRIPS), bsp(0), bsp(N_STRIPS)],
                  out_specs=pl.BlockSpec((s, STRIP), lambda j: (0, j)), out_shape=_sds((s, D_FF), BF16),
                  scratch_shapes=[pltpu.VMEM((s + SUBLANES, STRIP), F32)] * 2,
                  dims=("parallel",))(up, up, conv_w, conv_w, conv_b, conv_b)


def _down_loss(act, w_down, x1, tgt):
    s = x1.shape[0]
    tm = _row_tile(s)

    def body(a_ref, w_ref, x1_ref, t_ref, dy_ref, dyb_ref, loss_ref):
        i = pl.program_id(0)

        @pl.when(i == 0)
        def _():
            loss_ref[...] = jnp.zeros_like(loss_ref)

        diff = x1_ref[...] + _dot(a_ref[...], w_ref[...]) - t_ref[...]
        dy = diff * (1.0 / D_MODEL)
        dy_ref[...] = dy
        dyb_ref[...] = dy.astype(BF16)
        loss_ref[...] += 0.5 * jnp.sum(diff * dy)

    row = lambda w: pl.BlockSpec((tm, w), lambda i: (i, 0))
    return _pcall(body, name="down_loss", grid=(s // tm,),
                  in_specs=[row(D_FF), pl.BlockSpec((D_FF, D_MODEL), lambda i: (0, 0)), row(D_MODEL), row(D_MODEL)],
                  out_specs=[row(D_MODEL), row(D_MODEL), pl.BlockSpec((SUBLANES, LANES), lambda i: (0, 0))],
                  out_shape=[_sds((s, D_MODEL)), _sds((s, D_MODEL), BF16), _sds((SUBLANES, LANES))],
                  dims=("arbitrary",))(act, w_down, x1, tgt)


def _conv_act_bwd(up, dact, conv_w, conv_b):
    s = up.shape[0]
    ch = CONV_CHUNK

    def body(ug_ref, uv_ref, da_ref, wg_ref, wv_ref, bg_ref, bv_ref, dup_ref, dcw_ref, pg_ref, pv_ref, dg_ref, dv_ref):
        _fill_front_pad(pg_ref, ug_ref, s)
        _fill_front_pad(pv_ref, uv_ref, s)
        zero = jnp.zeros((SUBLANES, STRIP), F32)
        dg_ref[pl.ds(s, SUBLANES), :] = zero
        dv_ref[pl.ds(s, SUBLANES), :] = zero
        wg, wv, bg, bv = wg_ref[...], wv_ref[...], bg_ref[...], bv_ref[...]
        tile_sum = lambda t: jnp.sum(t.reshape(ch // SUBLANES, SUBLANES, STRIP), axis=0)
        accs = [[zero] * 4, [zero] * 4]
        for r0 in range(0, s, ch):
            hg = _conv_rows(pg_ref, wg, bg, r0, ch)
            hv = _conv_rows(pv_ref, wv, bv, r0, ch)
            sg = _sigmoid(hg)
            da = da_ref[pl.ds(r0, ch), :]
            dhs = (da * hv * (sg * (1.0 + hg * (1.0 - sg))), da * (hg * sg))
            for half, (dh, d_ref, p_ref) in enumerate(zip(dhs, (dg_ref, dv_ref), (pg_ref, pv_ref))):
                d_ref[pl.ds(r0, ch), :] = dh
                for k in range(3):
                    accs[half][k] = accs[half][k] + tile_sum(dh * p_ref[pl.ds(r0 + SUBLANES - 2 + k, ch), :])
                accs[half][3] = accs[half][3] + tile_sum(dh)
        for half, (d_ref, w) in enumerate(((dg_ref, wg), (dv_ref, wv))):
            for r0 in range(0, s, ch):
                dup = (d_ref[pl.ds(r0, ch), :] * w[2:3, :] + d_ref[pl.ds(r0 + 1, ch), :] * w[1:2, :]
                       + d_ref[pl.ds(r0 + 2, ch), :] * w[0:1, :])
                dup_ref[half, pl.ds(r0, ch), :] = dup.astype(BF16)
            rid = lax.broadcasted_iota(jnp.int32, (SUBLANES, STRIP), 0)
            out = zero
            for k in range(4):
                out = jnp.where(rid == k, jnp.sum(accs[half][k], axis=0, keepdims=True), out)
            dcw_ref[half] = out

    strip = lambda off: pl.BlockSpec((s, STRIP), lambda j: (0, j + off))
    wsp = lambda off: pl.BlockSpec((3, STRIP), lambda j: (0, j + off))
    bsp = lambda off: pl.BlockSpec((1, STRIP), lambda j: (0, j + off))
    return _pcall(body, name="conv_act_bwd", grid=(N_STRIPS,),
                  in_specs=[strip(0), strip(N_STRIPS), strip(0), wsp(0), wsp(N_STRIPS), bsp(0), bsp(N_STRIPS)],
                  out_specs=[pl.BlockSpec((2, s, STRIP), lambda j: (0, 0, j)), pl.BlockSpec((2, SUBLANES, STRIP), lambda j: (0, 0, j))],
                  out_shape=[_sds((2, s, D_FF), BF16), _sds((2, SUBLANES, D_FF))],
                  scratch_shapes=[pltpu.VMEM((s + SUBLANES, STRIP), F32)] * 4,
                  dims=("parallel",))(up, up, dact, conv_w, conv_w, conv_b, conv_b)


def _mix_bwd(dy, dh2, x1, g_ffn, w_out, y, att, w_glu, b_glu, g_att, g_ssm):
    s = dy.shape[0]
    tm = _row_tile(s)

    def body(dy_ref, dh2_ref, x1_ref, gf_ref, wo_ref, y_ref, att_ref, wg_ref, bg_ref, ga_ref, gs_ref,
             dx1_ref, dx1b_ref, datt_ref, dys_ref, dwg_ref, dgf_ref, dga_ref, dgs_ref, dbg_ref):
        i = pl.program_id(0)

        @pl.when(i == 0)
        def _():
            for r in (dwg_ref, dgf_ref, dga_ref, dgs_ref, dbg_ref):
                r[...] = jnp.zeros_like(r)

        dxn, dgf = _rms_bwd(x1_ref[...], gf_ref[...], dh2_ref[...])
        dx1 = dy_ref[...] + dxn
        dx1_ref[...] = dx1
        dx1b = dx1.astype(BF16)
        dx1b_ref[...] = dx1b
        dgf_ref[...] += dgf
        dma = _dot_nt(dx1b, wo_ref[0:ATTN_W, :])
        dms = _dot_nt(dx1b, wo_ref[ATTN_W:D_MODEL, :])
        datt, dga = _rms_bwd(_merge_heads(att_ref), ga_ref[...], dma)
        _split_heads(datt_ref, datt)
        dga_ref[...] += dga
        yv = y_ref[...]
        ge, sg = _ssm_glu(yv, wg_ref[...], bg_ref[...])
        dssm, dgs = _rms_bwd(ge * sg, gs_ref[...], dms)
        dgs_ref[...] += dgs
        dgl = dssm * ge * sg * (1.0 - sg)
        dglb = dgl.astype(BF16)
        dge = dssm * sg + _dot_nt(dglb, wg_ref[...])
        dbg_ref[...] += jnp.sum(dgl, axis=0, keepdims=True)
        dwg_ref[...] += _dot_tn(ge.astype(BF16), dglb)
        dys_ref[...] = dge * _gelu_grad(yv)

    row = lambda w: pl.BlockSpec((tm, w), lambda i: (i, 0))
    const = lambda a, b: pl.BlockSpec((a, b), lambda i: (0, 0))
    heads = pl.BlockSpec((HEADS, tm, HEAD_DIM), lambda i: (0, i, 0))
    return _pcall(body, name="mix_bwd", grid=(s // tm,),
                  in_specs=[row(D_MODEL), row(D_MODEL), row(D_MODEL), const(1, D_MODEL), const(D_MODEL, D_MODEL), row(SSM_W),
                            heads, const(SSM_W, SSM_W), const(1, SSM_W), const(1, ATTN_W), const(1, SSM_W)],
                  out_specs=[row(D_MODEL), row(D_MODEL), heads, row(SSM_W), const(SSM_W, SSM_W), const(1, D_MODEL),
                             const(1, ATTN_W), const(1, SSM_W), const(1, SSM_W)],
                  out_shape=[_sds((s, D_MODEL)), _sds((s, D_MODEL), BF16), _sds((HEADS, s, HEAD_DIM)), _sds((s, SSM_W)),
                             _sds((SSM_W, SSM_W)), _sds((1, D_MODEL)), _sds((1, ATTN_W)), _sds((1, SSM_W)), _sds((1, SSM_W))],
                  dims=("arbitrary",))(dy, dh2, x1, g_ffn, w_out, y, att, w_glu, b_glu, g_att, g_ssm)


def _ssm_bwd(dys, uf, ub, xr, xi, bbr, bbi, ar, ai, ccr, cci, dsk, hosted=None):
    s = dys.shape[0]
    tm = _row_tile(s)
    nb = s // tm
    nt = tm // SUBLANES

    def body(dy_ref, u_ref, ub_ref, xr_ref, xi_ref, xrp_ref, xip_ref, bbr_ref, bbi_ref, ar_ref, ai_ref, ccr_ref,
             cci_ref, dsk_ref, du_ref, dbbr_ref, dbbi_ref, dccr_ref, dcci_ref, dar_ref, dai_ref, dd_ref,
             gr_s, gi_s, cr_s, ci_s, accr_s, acci_s):
        i = pl.program_id(1)
        first_block = i == nb - 1

        @pl.when(i == 0)
        def _():
            for r in (cr_s, ci_s, accr_s, acci_s, dbbr_ref, dbbi_ref, dccr_ref, dcci_ref, dd_ref):
                r[...] = jnp.zeros_like(r)

        dy = dy_ref[...]
        dyb = dy.astype(BF16)
        gr_s[...] = _dot_nt(dyb, ccr_ref[0])
        gi_s[...] = -_dot_nt(dyb, cci_ref[0])
        consts = _scan_consts(ar_ref[0], -ai_ref[0], CHUNK_S, True)
        row = lax.broadcasted_iota(jnp.int32, (SUBLANES, CHUNK_S), 0)

        def tile(kk, carry):
            cr, ci, accr, acci = carry
            k = nt - 1 - kk
            sl = pl.ds(pl.multiple_of(k * SUBLANES, SUBLANES), SUBLANES)
            gr, gi = _scan_tile(gr_s[sl, :], gi_s[sl, :], cr, ci, consts, True)
            gr_s[sl, :] = gr
            gi_s[sl, :] = gi
            slp = pl.ds(pl.multiple_of(jnp.maximum(k - 1, 0) * SUBLANES, SUBLANES), SUBLANES)
            inner = k > 0
            pr_t = jnp.where(inner, xr_ref[slp, :], xrp_ref[...])
            pi_t = jnp.where(inner, xi_ref[slp, :], xip_ref[...])
            live = jnp.logical_or(inner, jnp.logical_not(first_block))
            top_r = jnp.where(live, pltpu.roll(pr_t, 1, 0), 0.0)
            top_i = jnp.where(live, pltpu.roll(pi_t, 1, 0), 0.0)
            xpr = jnp.where(row == 0, top_r, pltpu.roll(xr_ref[sl, :], 1, 0))
            xpi = jnp.where(row == 0, top_i, pltpu.roll(xi_ref[sl, :], 1, 0))
            accr = accr + gr * xpr + gi * xpi
            acci = acci + gi * xpr - gr * xpi
            return gr[0:1, :], gi[0:1, :], accr, acci

        zeros = jnp.zeros((SUBLANES, CHUNK_S), F32)
        cr, ci, accr, acci = lax.fori_loop(0, nt, tile, (cr_s[...], ci_s[...], zeros, zeros))
        cr_s[...] = cr
        ci_s[...] = ci
        accr_s[...] += accr
        acci_s[...] += acci
        grb = gr_s[...].astype(BF16)
        gib = gi_s[...].astype(BF16)
        u_b = ub_ref[...]
        du_ref[...] = _dot_nt(grb, bbr_ref[0]) + _dot_nt(gib, bbi_ref[0]) + dsk_ref[...] * dy
        dbbr_ref[0] += _dot_tn(u_b, grb)
        dbbi_ref[0] += _dot_tn(u_b, gib)
        dccr_ref[0] += _dot_tn(xr_ref[...].astype(BF16), dyb)
        dcci_ref[0] -= _dot_tn(xi_ref[...].astype(BF16), dyb)
        dd_ref[...] += jnp.sum(dy * u_ref[...], axis=0, keepdims=True)

        @pl.when(i == nb - 1)
        def _():
            dar_ref[0] = jnp.sum(accr_s[...], axis=0, keepdims=True)
            dai_ref[0] = jnp.sum(acci_s[...], axis=0, keepdims=True)

    tiles_per_block = tm // SUBLANES
    rb = lambda i: nb - 1 - i
    wspec = lambda a, b: pl.BlockSpec((1, a, b), lambda j, i: (j, 0, 0))
    xblk = pl.BlockSpec((tm, CHUNK_S), lambda j, i: (rb(i), j))
    xprev = pl.BlockSpec((SUBLANES, CHUNK_S), lambda j, i: (jnp.maximum(rb(i) * tiles_per_block - 1, 0), j))
    ublk = pl.BlockSpec((tm, CHUNK_U), lambda j, i: (rb(i), j))
    first = lambda: jnp.logical_and(pl.program_id(0) == 0, pl.program_id(1) == 0)
    last = lambda: jnp.logical_and(pl.program_id(0) == SSM_CHUNKS - 1, pl.program_id(1) == nb - 1)
    return _host_pcall(
        body, hosted, first, last, n_in=14, n_out=8, n_scratch=6, name="ssm_bwd", grid=(SSM_CHUNKS, nb),
        in_specs=[ublk, ublk, ublk, xblk, xblk, xprev, xprev,
                  wspec(CHUNK_U, CHUNK_S), wspec(CHUNK_U, CHUNK_S), wspec(1, CHUNK_S), wspec(1, CHUNK_S),
                  wspec(CHUNK_S, CHUNK_U), wspec(CHUNK_S, CHUNK_U), pl.BlockSpec((1, CHUNK_U), lambda j, i: (0, j))],
        out_specs=[ublk, wspec(CHUNK_U, CHUNK_S), wspec(CHUNK_U, CHUNK_S), wspec(CHUNK_S, CHUNK_U),
                   wspec(CHUNK_S, CHUNK_U), wspec(1, CHUNK_S), wspec(1, CHUNK_S),
                   pl.BlockSpec((1, CHUNK_U), lambda j, i: (0, j))],
        out_shape=[_sds((s, SSM_W)), _sds((SSM_CHUNKS, CHUNK_U, CHUNK_S)), _sds((SSM_CHUNKS, CHUNK_U, CHUNK_S)),
                   _sds((SSM_CHUNKS, CHUNK_S, CHUNK_U)), _sds((SSM_CHUNKS, CHUNK_S, CHUNK_U)),
                   _sds((SSM_CHUNKS, 1, CHUNK_S)), _sds((SSM_CHUNKS, 1, CHUNK_S)), _sds((1, SSM_W))],
        scratch_shapes=[pltpu.VMEM((tm, CHUNK_S), F32)] * 2 + [pltpu.VMEM((1, CHUNK_S), F32)] * 2
                       + [pltpu.VMEM((SUBLANES, CHUNK_S), F32)] * 2,
        dims=("parallel", "arbitrary"), operands=(dys, uf, ub, xr, xi, xr, xi, bbr, bbi, ar, ai, ccr, cci, dsk))


def _attn_probs(q, ks, cs, lse, scale, diagonal):
    p = jnp.exp(_dot_nt(q, ks) * scale - cs - lse)
    if diagonal:
        tq, tk = p.shape
        causal = lax.broadcasted_iota(jnp.int32, (tq, tk), 1) <= lax.broadcasted_iota(jnp.int32, (tq, tk), 0)
        p = jnp.where(causal, p, 0.0)
    return p


def _attn_bwd(qh, kh, vh, crow, lse, doh, hosted=None):
    _, s, _ = qh.shape
    tq = _row_tile(s)
    nq = s // tq
    scale = HEAD_DIM ** -0.5
    hp = HEADS_PER_STEP

    def body(q_ref, k_ref, v_ref, c_ref, lse_ref, do_ref, dq_ref, dk_ref, dv_ref, dc_ref, p_s, dp_s):
        i = pl.program_id(1)

        @pl.when(i == 0)
        def _():
            for r in (dk_ref, dv_ref, dc_ref):
                r[...] = jnp.zeros_like(r)

        dobs = [do_ref[hh].astype(BF16) for hh in range(hp)]

        def first(j, dls, diagonal):
            off = pl.multiple_of(j * tq, tq)
            out = []
            for hh in range(hp):
                p = _attn_probs(q_ref[hh], k_ref[hh, pl.ds(off, tq), :], c_ref[hh, :, pl.ds(off, tq)], lse_ref[hh],
                                scale, diagonal)
                dp = _dot_nt(dobs[hh], v_ref[hh, pl.ds(off, tq), :])
                p_s[hh, j] = p
                dp_s[hh, j] = dp
                out.append(dls[hh] + jnp.sum(p * dp, axis=-1, keepdims=True))
            return tuple(out)

        zero_col = jnp.zeros((tq, 1), F32)
        dls = lax.fori_loop(0, i, lambda j, c: first(j, c, False), (zero_col,) * hp)
        dls = first(i, dls, True)

        def second(j, dqs):
            rows = pl.ds(pl.multiple_of(j * tq, tq), tq)
            out = []
            for hh in range(hp):
                p = p_s[hh, j]
                ds = p * (dp_s[hh, j] - dls[hh])
                dsb = ds.astype(BF16)
                dv_ref[hh, rows, :] += _dot_tn(p.astype(BF16), dobs[hh])
                dk_ref[hh, rows, :] += _dot_tn(dsb, q_ref[hh]) * scale
                dc_ref[hh, :, rows] -= jnp.sum(ds, axis=0, keepdims=True)
                out.append(dqs[hh] + _dot(dsb, k_ref[hh, rows, :]))
            return tuple(out)

        dqs = lax.fori_loop(0, i + 1, second, (jnp.zeros((tq, HEAD_DIM), F32),) * hp)
        for hh in range(hp):
            dq_ref[hh] = dqs[hh] * scale

    blk = pl.BlockSpec((hp, tq, HEAD_DIM), lambda h, i: (h, i, 0))
    full = pl.BlockSpec((hp, s, HEAD_DIM), lambda h, i: (h, 0, 0))
    crow_spec = pl.BlockSpec((hp, 1, s), lambda h, i: (h, 0, 0))
    nh = HEADS // hp
    first = lambda: jnp.logical_and(pl.program_id(0) == 0, pl.program_id(1) == 0)
    last = lambda: jnp.logical_and(pl.program_id(0) == nh - 1, pl.program_id(1) == nq - 1)
    return _host_pcall(body, hosted, first, last, n_in=6, n_out=4, n_scratch=2, name="attn_bwd", grid=(nh, nq),
                       in_specs=[blk, full, full, crow_spec, pl.BlockSpec((hp, tq, 1), lambda h, i: (h, i, 0)), blk],
                       out_specs=[blk, full, full, crow_spec],
                       out_shape=[_sds((HEADS, s, HEAD_DIM))] * 3 + [_sds((HEADS, 1, s))],
                       scratch_shapes=[pltpu.VMEM((hp, nq, tq, tq), F32)] * 2,
                       dims=("parallel", "arbitrary"), operands=(qh, kh, vh, crow, lse, doh))


def _prep_bwd(z, dqn, dkn, dv, du, dc, gq, gk, bf, gg, hosted=None):
    s = z.shape[0]
    tm = _row_tile(s)
    nb = s // tm

    def body(z_ref, dqn_ref, dkn_ref, dv_ref, du_ref, dc_ref, gq_ref, gk_ref, bf_ref, gg_ref,
             dz_ref, dgq_ref, dgk_ref, dbf_ref, carry_ref):
        i = pl.program_id(0)

        @pl.when(i == 0)
        def _():
            for r in (dgq_ref, dgk_ref, dbf_ref, carry_ref):
                r[...] = jnp.zeros_like(r)

        gg_m = gg_ref[...]

        def head_norm_bwd(t, g, dn):
            r = lax.rsqrt(_dot_exact_r(t * t, gg_m) * (1.0 / HEAD_DIM) + EPS)
            w = dn * g
            mean_wt = _dot_exact_r(w * t, gg_m) * (1.0 / HEAD_DIM)
            return r * w - t * (r * r * r) * mean_wt, jnp.sum(dn * t * r, axis=0, keepdims=True)

        dq, dgq = head_norm_bwd(z_ref[:, 0:ATTN_W], gq_ref[...], _merge_heads(dqn_ref))
        dk, dgk = head_norm_bwd(z_ref[:, ATTN_W:2 * ATTN_W], gk_ref[...], _merge_heads(dkn_ref))
        dgq_ref[...] += dgq
        dgk_ref[...] += dgk
        row = lax.broadcasted_iota(jnp.int32, (tm, tm), 0)
        col = lax.broadcasted_iota(jnp.int32, (tm, tm), 1)
        triu = (col >= row).astype(BF16)
        dlf = _dot_exact_l(triu, dc_ref[...]) + carry_ref[...]
        carry_ref[...] = dlf[0:1, :]
        df = dlf * _sigmoid(-_forget_logits(z_ref, bf_ref))
        dbf_ref[...] += jnp.sum(df, axis=0, keepdims=True)
        dz_ref[:, 0:ATTN_W] = dq.astype(BF16)
        dz_ref[:, ATTN_W:2 * ATTN_W] = dk.astype(BF16)
        dz_ref[:, 2 * ATTN_W:3 * ATTN_W] = _merge_heads(dv_ref).astype(BF16)
        tail = jnp.concatenate([df[:, :HEADS], du_ref[...], jnp.zeros((tm, Z_COLS - IN_COLS), F32)], axis=-1)
        dz_ref[:, F_COL0:Z_COLS] = tail.astype(BF16)

    row_spec = lambda w: pl.BlockSpec((tm, w), lambda i: (nb - 1 - i, 0))
    const = lambda shape: pl.BlockSpec(shape, lambda i: (0, 0))
    return _host_pcall(
        body, hosted, lambda: pl.program_id(0) == 0, lambda: pl.program_id(0) == nb - 1, n_in=10, n_out=4, n_scratch=1,
        name="prep_bwd", grid=(nb,),
        in_specs=[row_spec(Z_COLS)] + [pl.BlockSpec((HEADS, tm, HEAD_DIM), lambda i: (0, nb - 1 - i, 0))] * 3
                 + [row_spec(ATTN_W), row_spec(LANES), const((1, ATTN_W)),
                    const((1, ATTN_W)), const((1, LANES)), const((ATTN_W, ATTN_W))],
        out_specs=[row_spec(Z_COLS), const((1, ATTN_W)), const((1, ATTN_W)), const((1, LANES))],
        out_shape=[_sds((s, Z_COLS), BF16), _sds((1, ATTN_W)), _sds((1, ATTN_W)), _sds((1, LANES))],
        scratch_shapes=[pltpu.VMEM((1, LANES), F32)], dims=("arbitrary",),
        operands=(z, dqn, dkn, dv, du, dc, gq, gk, bf, gg))


def _in_norm_bwd(x, g_mix, dh, dx1, hosted=None):
    s = x.shape[0]
    tm = _row_tile(s)

    def body(x_ref, g_ref, dh_ref, dx1_ref, dx_ref, dg_ref):
        i = pl.program_id(0)

        @pl.when(i == 0)
        def _():
            dg_ref[...] = jnp.zeros_like(dg_ref)

        dxn, dg = _rms_bwd(x_ref[...], g_ref[...], dh_ref[...])
        dx_ref[...] = dx1_ref[...] + dxn
        dg_ref[...] += dg

    row = pl.BlockSpec((tm, D_MODEL), lambda i: (i, 0))
    vec = pl.BlockSpec((1, D_MODEL), lambda i: (0, 0))
    nb = s // tm
    return _host_pcall(body, hosted, lambda: pl.program_id(0) == 0, lambda: pl.program_id(0) == nb - 1,
                       n_in=4, n_out=2, n_scratch=0, name="in_norm_bwd", grid=(nb,), in_specs=[row, vec, row, row],
                       out_specs=[row, vec], out_shape=[_sds((s, D_MODEL)), _sds((1, D_MODEL))], scratch_shapes=[],
                       dims=("arbitrary",), operands=(x, g_mix, dh, dx1))


def _adamw_refs(w_ref, g_ref, m_ref, v_ref, d_ref, mo_ref, vo_ref):
    gv = g_ref[...]
    mn = ADAM_B1 * m_ref[...] + (1.0 - ADAM_B1) * gv
    vn = ADAM_B2 * v_ref[...] + (1.0 - ADAM_B2) * (gv * gv)
    m_hat = mn / (1.0 - ADAM_B1 ** ADAM_STEP)
    v_hat = vn / (1.0 - ADAM_B2 ** ADAM_STEP)
    d_ref[...] = -ADAM_LR * (m_hat / (jnp.sqrt(v_hat) + ADAM_EPS) + ADAM_WD * w_ref[...])
    mo_ref[...] = mn
    vo_ref[...] = vn


def _adamw_small(ws, gs, ms, vs):
    n = len(ws)

    def body(*refs):
        ins, outs = refs[:4 * n], refs[4 * n:]
        for i in range(n):
            _adamw_refs(ins[i], ins[n + i], ins[2 * n + i], ins[3 * n + i], *outs[3 * i:3 * i + 3])

    vm = pl.BlockSpec(memory_space=pltpu.VMEM)
    out_shape = [_sds(w.shape) for w in ws for _ in range(3)]
    return _pallas(body, name="adamw_small", in_specs=[vm] * (4 * n), out_specs=[vm] * (3 * n), out_shape=out_shape,
                   compiler_params=pltpu.CompilerParams(vmem_limit_bytes=VMEM_LIMIT))(*ws, *gs, *ms, *vs)


def _adamw(w, g, m, v, *, name):
    r, c = w.shape
    tr = r
    for cand in (256, 176, 128, 64):
        if r > cand and r % cand == 0:
            tr = cand
            break

    def body(w_ref, g_ref, m_ref, v_ref, d_ref, mo_ref, vo_ref):
        _adamw_refs(w_ref, g_ref, m_ref, v_ref, d_ref, mo_ref, vo_ref)

    spec = pl.BlockSpec((tr, c), lambda i: (i, 0))
    return _pcall(body, name=name, grid=(r // tr,), in_specs=[spec] * 4, out_specs=[spec] * 3,
                  out_shape=[_sds((r, c))] * 3, dims=("parallel",))(w, g, m, v)


def _prefetch_call(body, *, name, grid, in_specs, out_specs, out_shape, operands):
    grid_spec = pltpu.PrefetchScalarGridSpec(num_scalar_prefetch=1, grid=grid, in_specs=in_specs, out_specs=out_specs)
    params = pltpu.CompilerParams(dimension_semantics=("parallel",) * len(grid), vmem_limit_bytes=VMEM_LIMIT)
    return _pallas(body, name=name, grid_spec=grid_spec, out_shape=out_shape, compiler_params=params)(*operands)


def _place_cols(buf, shard, place):
    rows, cols = shard.shape
    tr = 256

    def body(place_ref, s_ref, b_ref, o_ref):
        o_ref[...] = s_ref[...]

    grid_spec = pltpu.PrefetchScalarGridSpec(
        num_scalar_prefetch=1, grid=(rows // tr,),
        in_specs=[pl.BlockSpec((tr, cols), lambda i, p: (i, 0)), pl.BlockSpec(memory_space=pltpu.HBM)],
        out_specs=pl.BlockSpec((tr, cols), lambda i, p: (i, p[0])))
    return _pallas(body, name="place_own_cols", grid_spec=grid_spec, out_shape=_sds(buf.shape, buf.dtype),
                   input_output_aliases={2: 0},
                   compiler_params=pltpu.CompilerParams(dimension_semantics=("parallel",),
                                                        vmem_limit_bytes=VMEM_LIMIT))(place, shard, buf)


def _half_rows_tile(hr):
    return hr if hr <= 256 else 176 if hr % 176 == 0 else 256


def _add_half(g, landed, place, *, name):
    def body(place_ref, g_ref, l_ref, o_ref):
        own = g_ref[0] if len(g_ref.shape) == 4 else g_ref[...]
        o_ref[...] = (own + l_ref[...]).astype(BF16)

    if g.ndim == 4:
        _, _, hr, c = g.shape
        tr = _half_rows_tile(hr)
        blk = (1, tr, c)
        return _prefetch_call(
            body, name=name, grid=(N_CHIPS, hr // tr),
            in_specs=[pl.BlockSpec((1,) + blk, lambda j, i, p: (j, p[1], i, 0)), pl.BlockSpec(blk, lambda j, i, p: (j, i, 0))],
            out_specs=pl.BlockSpec(blk, lambda j, i, p: (j, i, 0)), out_shape=_sds(landed.shape, BF16),
            operands=(place, g, landed))
    hr, c = landed.shape
    tr, tc = 256, _tile(c, 2176)
    nb = hr // tr
    return _prefetch_call(
        body, name=name, grid=(nb, c // tc),
        in_specs=[pl.BlockSpec((tr, tc), lambda i, j, p: (p[1] * nb + i, j)), pl.BlockSpec((tr, tc), lambda i, j, p: (i, j))],
        out_specs=pl.BlockSpec((tr, tc), lambda i, j, p: (i, j)), out_shape=_sds(landed.shape, BF16),
        operands=(place, g, landed))


def _sum_chips(chip_sum, lands, place, *, name, tc, window_stride=0):
    _, hr, c = lands.shape
    tr = _half_rows_tile(hr)
    nb = hr // tr
    ncb = c // tc

    def body(place_ref, own_ref, a_ref, b_ref, c_ref, o_ref):
        own = own_ref[0] if len(own_ref.shape) == 3 else own_ref[...]
        o_ref[...] = ((own.astype(F32) + a_ref[0].astype(F32)) + b_ref[0].astype(F32)) + c_ref[0].astype(F32)

    land = lambda k: pl.BlockSpec((1, tr, tc), lambda i, j, p: ((p[0] + k) % N_CHIPS, i, j))
    if chip_sum.ndim == 3:
        own_spec = land(0)
    else:
        stride = window_stride // tc
        own_spec = pl.BlockSpec((tr, tc), lambda i, j, p: (i, p[0] * stride + j))
    return _prefetch_call(
        body, name=name, grid=(nb, ncb), in_specs=[own_spec, land(1), land(2), land(3)],
        out_specs=pl.BlockSpec((tr, tc), lambda i, j, p: (p[1] * nb + i, j)), out_shape=_sds((2 * hr, c)),
        operands=(place, chip_sum, lands, lands, lands))


_HBM = pl.BlockSpec(memory_space=pltpu.HBM)


def _place():
    x, y, c = lax.axis_index("x"), lax.axis_index("y"), lax.axis_index("c")
    chips = [(1 - x, y), (x, 1 - y), (1 - x, 1 - y)]
    return x, y, c, chips


def _rcopy(src, dst, send_sem, recv_sem, to):
    return pltpu.make_async_remote_copy(src_ref=src, dst_ref=dst, send_sem=send_sem, recv_sem=recv_sem,
                                        device_id=to, device_id_type=MESH)


UP_COLS = 2 * D_FF // N_CHIPS
IN_WINDOW = 640
IN_STRIDE = 512


class _Hosted:
    def __init__(self, operands, out_shapes, n_sems, start, finish, aliases=None, local_sems=0):
        self.operands, self.out_shapes, self.n_sems = list(operands), list(out_shapes), n_sems
        self.start, self.finish, self.aliases, self.local_sems = start, finish, dict(aliases or {}), local_sems

    def scratch(self):
        return ([pltpu.SemaphoreType.DMA((self.n_sems,)), pltpu.SemaphoreType.DMA((self.n_sems,))]
                + [pltpu.SemaphoreType.DMA] * self.local_sems)


def _both(a, b):
    na, nao, nas = len(a.operands), len(a.out_shapes), len(a.scratch())

    def start(ins, outs, sems):
        a.start(ins[:na], outs[:nao], sems[:nas])
        b.start(ins[na:], outs[nao:], sems[nas:])

    def finish(ins, outs, sems):
        a.finish(ins[:na], outs[:nao], sems[:nas])
        b.finish(ins[na:], outs[nao:], sems[nas:])

    both = _Hosted(a.operands + b.operands, a.out_shapes + b.out_shapes, 0, start, finish,
                   aliases={**a.aliases, **{na + i: nao + o for i, o in b.aliases.items()}})
    both.scratch = lambda: a.scratch() + b.scratch()
    return both


def _then(a, b):
    nas = len(a.scratch())

    def finish(ins, outs, sems):
        a.finish(ins, outs, sems[:nas])
        b.start(ins, outs, sems[nas:])
        b.finish(ins, outs, sems[nas:])

    chain = _Hosted(a.operands, a.out_shapes, 0, lambda ins, outs, sems: a.start(ins, outs, sems[:nas]), finish,
                    aliases=a.aliases)
    chain.scratch = lambda: a.scratch() + b.scratch()
    return chain


def _run_hosted(hosted, *, name):
    n_in, n_out = len(hosted.operands), len(hosted.out_shapes)

    def body(*refs):
        parts = (refs[:n_in], refs[n_in:n_in + n_out], refs[n_in + n_out:])
        hosted.start(*parts)
        hosted.finish(*parts)

    return _pallas(body, name=name, in_specs=[_HBM] * n_in, out_specs=[_HBM] * n_out, out_shape=hosted.out_shapes,
                   input_output_aliases=hosted.aliases, scratch_shapes=hosted.scratch())(*hosted.operands)


def _host_pcall(core_body, hosted, first, last, *, n_in, n_out, n_scratch, name, grid, in_specs, out_specs, out_shape,
                scratch_shapes, dims, operands):
    if hosted is None:
        outs = _pcall(core_body, name=name, grid=grid, in_specs=in_specs, out_specs=out_specs, out_shape=out_shape,
                      scratch_shapes=scratch_shapes, dims=dims)(*operands)
        return outs, []
    hi, ho = len(hosted.operands), len(hosted.out_shapes)

    def body(*refs):
        a, b = n_in, n_in + hi
        c, d = b + n_out, b + n_out + ho
        e = d + n_scratch
        parts = (refs[a:b], refs[c:d], refs[e:])

        @pl.when(first())
        def _():
            hosted.start(*parts)

        core_body(*refs[:a], *refs[b:c], *refs[d:e])

        @pl.when(last())
        def _():
            hosted.finish(*parts)

    params = pltpu.CompilerParams(dimension_semantics=("arbitrary",) * len(grid), vmem_limit_bytes=VMEM_LIMIT)
    outs = _pallas(body, name=name, grid=grid, in_specs=list(in_specs) + [_HBM] * hi, out_specs=list(out_specs) + [_HBM] * ho,
                   out_shape=list(out_shape) + hosted.out_shapes, scratch_shapes=list(scratch_shapes) + hosted.scratch(),
                   input_output_aliases={n_in + a: n_out + b for a, b in hosted.aliases.items()},
                   compiler_params=params)(*operands, *hosted.operands)
    return outs[:n_out], outs[n_out:]


WHOLE_HALF = (0, 1, 1)


def _band_rows(src, hc, band):
    first, count, of = band
    hr = src.shape[0] // 2
    return pl.ds(hc * hr + first * (hr // of), count * (hr // of))


def _gather_slot(src, out, chip, hc, band=WHOLE_HALF):
    cols = src.shape[1]
    if len(out.shape) == 2:
        return out.at[_band_rows(src, hc, band), pl.ds(pl.multiple_of(chip * cols, LANES), cols)]
    return out.at[chip, _band_rows(src, hc, band), :]


def _gathered_shape(shard, by_cols):
    if by_cols:
        return _sds((shard.shape[0], N_CHIPS * shard.shape[1]), shard.dtype)
    return _sds((N_CHIPS,) + shard.shape, shard.dtype)


def _plan_gather_ici(shards, by_cols, whole=(), bands=None, into=None):
    n = len(shards)
    bands = bands or [WHOLE_HALF] * n
    into = into or [None] * n
    given = [w for w in range(n) if into[w] is not None]
    n_ops = n + len(whole)

    def copies(ins, outs, sems):
        send_sems, recv_sems = sems[0], sems[1]
        x, y, c, chips = _place()
        me = 2 * x + y
        sends, waits = [], []
        for w in range(n + len(whole)):
            for k, (cx, cy) in enumerate(chips):
                sem = (send_sems.at[3 * w + k], recv_sems.at[3 * w + k])
                if w < n:
                    sends.append(_rcopy(ins[w].at[_band_rows(ins[w], c, bands[w]), :],
                                        _gather_slot(ins[w], outs[w], me, c, bands[w]), *sem, (cx, cy, c)))
                    landed = _gather_slot(ins[w], outs[w], 2 * cx + cy, c, bands[w])
                else:
                    sends.append(_rcopy(ins[w], outs[w].at[me], *sem, (cx, cy, c)))
                    landed = outs[w].at[2 * cx + cy]
                waits.append(_rcopy(landed, landed, *sem, (cx, cy, c)))
        return sends, waits

    def start(ins, outs, sems):
        for cp in copies(ins, outs, sems)[0]:
            cp.start()

    def finish(ins, outs, sems):
        sends, waits = copies(ins, outs, sems)
        for cp in waits:
            cp.wait_recv()
        for cp in sends:
            cp.wait_send()

    out_shapes = [_gathered_shape(s, bc) for s, bc in zip(shards, by_cols)] + [_sds((N_CHIPS,) + a.shape, a.dtype) for a in whole]
    return _Hosted(list(shards) + list(whole) + [into[w] for w in given], out_shapes, 3 * n_ops, start, finish,
                   aliases={n_ops + i: w for i, w in enumerate(given)})


def _plan_gather_d2d(bufs, shard_shapes, bands=None):
    n = len(bufs)
    bands = bands or [WHOLE_HALF] * n

    def copies(ins, outs, sems):
        send_sems, recv_sems = sems
        x, y, c, chips = _place()
        sibling = (x, y, 1 - c)
        sends, waits = [], []
        for w in range(n):
            for k, (cx, cy) in enumerate(chips):
                sem = (send_sems.at[3 * w + k], recv_sems.at[3 * w + k])
                landed = _gather_slot(shard_shapes[w], outs[w], 2 * cx + cy, c, bands[w])
                other = _gather_slot(shard_shapes[w], outs[w], 2 * cx + cy, 1 - c, bands[w])
                sends.append(_rcopy(landed, landed, *sem, sibling))
                waits.append(_rcopy(other, other, *sem, sibling))
        return sends, waits

    def start(ins, outs, sems):
        for cp in copies(ins, outs, sems)[0]:
            cp.start()

    def finish(ins, outs, sems):
        sends, waits = copies(ins, outs, sems)
        for cp in waits:
            cp.wait_recv()
        for cp in sends:
            cp.wait_send()

    return _Hosted(bufs, [_sds(b.shape, b.dtype) for b in bufs], 3 * n, start, finish, aliases={w: w for w in range(n)})


def _plan_allgather_first(block):
    def copies(ins, outs, sems):
        send_sems, recv_sems, local_sem = sems
        x, y, c, chips = _place()
        me = 4 * x + 2 * y + c
        peers = [(x, y, 1 - c)] + [(cx, cy, c) for cx, cy in chips]
        sends = [_rcopy(ins[0], outs[0].at[me], send_sems.at[k], recv_sems.at[k], p) for k, p in enumerate(peers)]
        waits = [_rcopy(outs[0].at[4 * px + 2 * py + pc], outs[0].at[4 * px + 2 * py + pc], send_sems.at[k],
                        recv_sems.at[k], (px, py, pc)) for k, (px, py, pc) in enumerate(peers)]
        return sends, waits, pltpu.make_async_copy(ins[0], outs[0].at[me], local_sem)

    def start(ins, outs, sems):
        sends, _, own = copies(ins, outs, sems)
        for cp in [own] + sends:
            cp.start()

    def finish(ins, outs, sems):
        sends, waits, own = copies(ins, outs, sems)
        for cp in waits:
            cp.wait_recv()
        for cp in sends:
            cp.wait_send()
        own.wait()

    return _Hosted([block], [_sds((8,) + block.shape)], 4, start, finish, local_sems=1)


def _plan_allgather_second(gathered):
    def copies(ins, outs, sems):
        send_sems, recv_sems = sems
        x, y, c, chips = _place()
        sends, waits = [], []
        for k, (cx, cy) in enumerate(chips):
            landed = outs[0].at[4 * cx + 2 * cy + c]
            other = outs[0].at[4 * cx + 2 * cy + 1 - c]
            sends.append(_rcopy(landed, landed, send_sems.at[k], recv_sems.at[k], (x, y, 1 - c)))
            waits.append(_rcopy(other, other, send_sems.at[k], recv_sems.at[k], (x, y, 1 - c)))
        return sends, waits

    def start(ins, outs, sems):
        for cp in copies(ins, outs, sems)[0]:
            cp.start()

    def finish(ins, outs, sems):
        sends, waits = copies(ins, outs, sems)
        for cp in waits:
            cp.wait_recv()
        for cp in sends:
            cp.wait_send()

    return _Hosted([gathered], [_sds(gathered.shape)], 3, start, finish, aliases={0: 0})


def _sum_devices(gathered):
    _, rows, lanes = gathered.shape
    tr = rows // 2 if rows % 16 == 0 else rows

    def body(g_ref, o_ref):
        acc = g_ref[0]
        for d in range(1, 8):
            acc = acc + g_ref[d]
        o_ref[...] = acc

    return _pcall(body, name="sum_devices", grid=(rows // tr,), in_specs=[pl.BlockSpec((8, tr, lanes), lambda i: (0, i, 0))],
                  out_specs=pl.BlockSpec((tr, lanes), lambda i: (i, 0)), out_shape=_sds((rows, lanes)), dims=("parallel",))(gathered)


def _plan_swap(grads):
    def copies(ins, outs, sems):
        send_sems, recv_sems = sems
        x, y, c, _ = _place()
        cps = []
        for w, g_ref in enumerate(ins):
            if len(g_ref.shape) == 4:
                theirs = g_ref.at[:, 1 - c]
            else:
                hr = g_ref.shape[0] // 2
                theirs = g_ref.at[pl.ds((1 - c) * hr, hr), :]
            cps.append(_rcopy(theirs, outs[w], send_sems.at[w], recv_sems.at[w], (x, y, 1 - c)))
        return cps

    def start(ins, outs, sems):
        for cp in copies(ins, outs, sems):
            cp.start()

    def finish(ins, outs, sems):
        for cp in copies(ins, outs, sems):
            cp.wait()

    out_shapes = [_sds((g.shape[0], g.shape[2], g.shape[3])) if g.ndim == 4 else _sds((g.shape[0] // 2, g.shape[1]))
                  for g in grads]
    return _Hosted(grads, out_shapes, len(grads), start, finish)


def _plan_scatter(chip_sums, windows):
    def copies(ins, outs, sems):
        send_sems, recv_sems = sems
        x, y, c, chips = _place()
        me = 2 * x + y
        sends, waits = [], []
        for w, s_ref in enumerate(ins):
            for k, (cx, cy) in enumerate(chips):
                tgt = 2 * cx + cy
                if windows[w] is not None:
                    stride, width = windows[w]
                    part = s_ref.at[:, pl.ds(pl.multiple_of(tgt * stride, LANES), width)]
                else:
                    part = s_ref.at[tgt]
                sem = (send_sems.at[3 * w + k], recv_sems.at[3 * w + k])
                sends.append(_rcopy(part, outs[w].at[me], *sem, (cx, cy, c)))
                slot = outs[w].at[tgt]
                waits.append(_rcopy(slot, slot, *sem, (cx, cy, c)))
        return sends, waits

    def start(ins, outs, sems):
        for cp in copies(ins, outs, sems)[0]:
            cp.start()

    def finish(ins, outs, sems):
        sends, waits = copies(ins, outs, sems)
        for cp in waits:
            cp.wait_recv()
        for cp in sends:
            cp.wait_send()

    out_shapes = [_sds((N_CHIPS, s.shape[0], win[1]), BF16) if win is not None else _sds(s.shape, BF16)
                  for s, win in zip(chip_sums, windows)]
    return _Hosted(chip_sums, out_shapes, 3 * len(chip_sums), start, finish)


def _plan_join(reds):
    def copies(ins, outs, sems):
        send_sems, recv_sems = sems
        x, y, c, _ = _place()
        sends, waits = [], []
        for w, out in enumerate(outs):
            hr = out.shape[0] // 2
            mine = out.at[pl.ds(c * hr, hr), :]
            theirs = out.at[pl.ds((1 - c) * hr, hr), :]
            sends.append(_rcopy(mine, mine, send_sems.at[w], recv_sems.at[w], (x, y, 1 - c)))
            waits.append(_rcopy(theirs, theirs, send_sems.at[w], recv_sems.at[w], (x, y, 1 - c)))
        return sends, waits

    def start(ins, outs, sems):
        for cp in copies(ins, outs, sems)[0]:
            cp.start()

    def finish(ins, outs, sems):
        sends, waits = copies(ins, outs, sems)
        for cp in waits:
            cp.wait_recv()
        for cp in sends:
            cp.wait_send()

    return _Hosted(reds, [_sds(r.shape) for r in reds], len(reds), start, finish, aliases={w: w for w in range(len(reds))})


def _allreduce_small(v):
    m_per = v.shape[0]

    def body(v_ref, out_ref, all_ref, send_sems, recv_sems, local_sem):
        x, y, c, chips = _place()
        me, sibling = (x, y, c), (x, y, 1 - c)

        def rows(px, py, pc):
            return all_ref.at[pl.ds((4 * px + 2 * py + pc) * m_per, m_per), :]

        def copy(k, block, to, src=None):
            return _rcopy(rows(*block) if src is None else src, rows(*block), send_sems.at[k], recv_sems.at[k], to)

        mine = pltpu.make_async_copy(v_ref, rows(*me), local_sem)
        mine.start()
        first = [copy(0, me, sibling, src=v_ref)]
        first += [copy(1 + k, me, (*chip, c), src=v_ref) for k, chip in enumerate(chips)]
        for cp in first:
            cp.start()
        passed = [copy(4 + k, (*chip, c), sibling) for k, chip in enumerate(chips)]
        for k, chip in enumerate(chips):
            copy(1 + k, (*chip, c), me).wait_recv()
            passed[k].start()
        copy(0, sibling, me).wait_recv()
        for k, chip in enumerate(chips):
            copy(4 + k, (*chip, 1 - c), me).wait_recv()
        for cp in first + passed:
            cp.wait_send()
        mine.wait()
        acc = all_ref[pl.ds(0, m_per), :]
        for d in range(1, 8):
            acc = acc + all_ref[pl.ds(d * m_per, m_per), :]
        out_ref[...] = acc

    vm = pl.BlockSpec(memory_space=pltpu.VMEM)
    return _pallas(body, name="allreduce_small", in_specs=[vm], out_specs=vm, out_shape=_sds((m_per, LANES)),
                          scratch_shapes=[pltpu.VMEM((8 * m_per, LANES), F32), pltpu.SemaphoreType.DMA((7,)),
                                          pltpu.SemaphoreType.DMA((7,)), pltpu.SemaphoreType.DMA],
                          compiler_params=pltpu.CompilerParams(vmem_limit_bytes=VMEM_LIMIT))(v)


def _block_diag(blocks):
    j, g, a, b = blocks.shape
    eye = jnp.eye(g, dtype=bool)[None, :, None, :, None]
    return jnp.where(eye, blocks[:, :, :, None, :], jnp.zeros((), blocks.dtype)).reshape(j, g * a, g * b)


def _diag_blocks(m, a, b):
    j = m.shape[0]
    g = m.shape[1] // a
    t = m.reshape(j, g, a, g, b)
    eye = jnp.eye(g, dtype=bool)[None, :, None, :, None]
    return jnp.sum(jnp.where(eye, t, 0.0), axis=3)


_SMALL = (("g_mix", (1024,)), ("b_f", (8,)), ("g_q", (64,)), ("g_k", (64,)), ("lambda_re", (32, 64)),
          ("lambda_im", (32, 64)), ("log_step", (32,)), ("b_re", (32, 64, 16)), ("b_im", (32, 64, 16)),
          ("c_re", (32, 16, 64)), ("c_im", (32, 16, 64)), ("d_skip", (32, 16)), ("b_glu", (512,)),
          ("g_attn_out", (512,)), ("g_ssm_out", (512,)), ("g_ffn", (1024,)), ("conv_b", (5632,)))


_LATE_SMALL = ("g_mix", "b_f", "g_q", "g_k")
_EARLY_SMALL = tuple(n for n, _ in _SMALL if n not in _LATE_SMALL)


def _packed_rows(n):
    tile = SUBLANES * LANES
    return -(-n // tile) * SUBLANES


def _pack_small(arrs):
    parts = []
    for a in arrs:
        flat = a.reshape(-1)
        rows = _packed_rows(flat.shape[0])
        parts.append(jnp.pad(flat, (0, rows * LANES - flat.shape[0])).reshape(rows, LANES))
    return jnp.concatenate(parts, axis=0)


def _unpack_small(buf, shapes):
    out, r = [], 0
    for shape in shapes:
        n = math.prod(shape)
        out.append(buf[r:r + _packed_rows(n)].reshape(-1)[:n].reshape(shape))
        r += _packed_rows(n)
    return out


def _halves(t):
    return t.reshape(N_CHIPS, 2, t.shape[0] // (2 * N_CHIPS), t.shape[1])


class _MeshComm:
    def __init__(self, args):
        x, y, self.core = lax.axis_index("x"), lax.axis_index("y"), lax.axis_index("c")
        self.chip = 2 * x + y
        self.place = jnp.stack([self.chip, self.core]).astype(jnp.int32)
        self.shards = {n: args[n].astype(BF16) for n in ("w_in", "w_glu", "w_out", "w_up", "w_down")}
        self.conv_w = args["conv_w"]

    def _own(self, stacked, mine):
        return lax.dynamic_update_slice(stacked, mine[None], (self.chip,) + (0,) * mine.ndim)

    def w_in(self):
        sh = self.shards["w_in"]
        (buf,) = _run_hosted(_then(_plan_gather_ici([sh], [False]), _plan_gather_d2d([sh], [sh])), name="gather_w_in")
        whole = self._own(buf, sh).transpose(1, 0, 2).reshape(D_MODEL, IN_COLS)
        return jnp.pad(whole, ((0, 0), (0, Z_COLS - IN_COLS)))

    def gather_first(self):
        self.mid = [self.shards[n] for n in ("w_glu", "w_out", "w_down")]
        return _plan_gather_ici(self.mid + [self.shards["w_up"]], [False, False, False, True], whole=[self.conv_w],
                                bands=[WHOLE_HALF] * 3 + [(0, 1, 4)])

    def gather_second(self, landed):
        self.g_cw = landed[4]
        return _both(_plan_gather_d2d(list(landed[:3]), self.mid),
                     _plan_gather_ici([self.shards["w_up"]], [True], bands=[(1, 3, 4)], into=[landed[3]]))

    def weights(self, gathered):
        g_glu, g_out, g_down = gathered[:3]
        own = self._own
        return (own(g_glu, self.mid[0]).reshape(SSM_W, SSM_W), own(g_out, self.mid[1]).reshape(D_MODEL, D_MODEL),
                own(g_down, self.mid[2]).reshape(D_FF, D_MODEL),
                own(self.g_cw, self.conv_w).transpose(1, 0, 2).reshape(3, 2 * D_FF))

    def gather_third(self, gathered):
        return _plan_gather_d2d([gathered[3]], [self.shards["w_up"]])

    def w_up(self, passed):
        return _place_cols(passed[0], self.shards["w_up"], self.place)

    def swap(self, d_w_down, d_w_up, d_w_glu, d_w_out):
        self.early = [_halves(d_w_down), d_w_up, _halves(d_w_glu), _halves(d_w_out)]
        return _plan_swap(self.early)

    def scatter(self, landed):
        self.early_sums = [_add_half(g, l, self.place, name="add_" + n)
                           for g, l, n in zip(self.early, landed, ("w_down", "w_up", "w_glu", "w_out"))]
        return _plan_scatter(self.early_sums, [None, (UP_COLS, UP_COLS), None, None])

    def swap_in(self, d_w_in):
        self.d_in = d_w_in
        return _plan_swap([d_w_in])

    def scatter_in(self, landed):
        self.sum_in = _add_half(self.d_in, landed[0], self.place, name="add_w_in")
        return _plan_scatter([self.sum_in], [(IN_STRIDE, IN_WINDOW)])

    def small_first(self, block):
        return _plan_allgather_first(block)

    def small_second(self, landed):
        return _plan_allgather_second(landed[0])

    def reduce(self, lands):
        early_lands, (land_in,) = lands
        sum_in = self.sum_in
        es, el = self.early_sums, early_lands
        todo = [(sum_in, land_in, "w_in", LANES, IN_STRIDE), (es[2], el[2], "w_glu", SSM_W, 0),
                (es[3], el[3], "w_out", D_MODEL, 0), (es[1], el[1], "w_up", UP_COLS, UP_COLS),
                (es[0], el[0], "w_down", D_MODEL, 0)]
        reds = _run_hosted(_plan_join([_sum_chips(s, l, self.place, name="sum_" + n, tc=tc, window_stride=st)
                                       for s, l, n, tc, st in todo]), name="join_halves")
        g_big = dict(zip(("w_in", "w_glu", "w_out", "w_up", "w_down"), reds))
        g_big["w_in"] = lax.dynamic_slice_in_dim(reds[0], 2 * self.chip, IN_COLS // N_CHIPS, axis=1)
        return g_big


def _local_step(x, tgt, p, comm):
    s = x.shape[0]
    row = lambda v: v.reshape(1, -1)
    g_mix, g_ffn = row(p["g_mix"]), row(p["g_ffn"])
    g_att, g_ssm, b_glu, conv_b = row(p["g_attn_out"]), row(p["g_ssm_out"]), row(p["b_glu"]), row(p["conv_b"])
    gq = row(jnp.tile(p["g_q"], HEADS))
    gk = row(jnp.tile(p["g_k"], HEADS))
    bf = row(jnp.pad(p["b_f"], (0, LANES - HEADS)))
    gg = jnp.kron(jnp.eye(HEADS, dtype=F32), jnp.ones((HEAD_DIM, HEAD_DIM), F32)).astype(BF16)
    dsk = row(p["d_skip"])

    rep = lambda a: jnp.repeat(a, SSM_GROUP, axis=0)
    lr, li = rep(p["lambda_re"]), rep(p["lambda_im"])
    ls = rep(jnp.broadcast_to(p["log_step"][:, None], (SSM_GROUPS, SSM_STATE)))
    bt_re = p["b_re"].transpose(0, 2, 1).reshape(_PARAM_SHAPE)
    bt_im = p["b_im"].transpose(0, 2, 1).reshape(_PARAM_SHAPE)
    a_re_rep, a_im_rep, bb_re, bb_im = _ssm_params(lr, li, ls, bt_re, bt_im)
    ar = a_re_rep[::SSM_GROUP].reshape(SSM_CHUNKS, 1, CHUNK_S)
    ai = a_im_rep[::SSM_GROUP].reshape(SSM_CHUNKS, 1, CHUNK_S)
    chunked = lambda t: t.reshape(SSM_CHUNKS, SSM_GROUPS // SSM_CHUNKS, SSM_GROUP, SSM_STATE)
    bbr = _block_diag(chunked(bb_re)).astype(BF16)
    bbi = _block_diag(chunked(bb_im)).astype(BF16)
    to_cc = lambda c: _block_diag(chunked(c).transpose(0, 1, 3, 2)).astype(BF16)
    ccr, cci = to_cc(p["c_re"]), to_cc(p["c_im"])

    w_in_r = comm.w_in()
    hb, z = _in_proj(x, g_mix, w_in_r)
    qh, kh, vh, ub, uf, c128 = _attn_prep(z, gq, gk, bf, gg)
    crow = c128[:, :HEADS].T.reshape(HEADS, 1, s)
    (oh, lse), landed = _attn_fwd(qh, kh, vh, crow, comm.gather_first())
    (xr, xi, y), gathered = _ssm_fwd(ub, uf, bbr, bbi, ar, ai, ccr, cci, dsk, comm.gather_second(landed))
    w_glu_b, w_out_b, w_down_b, conv_w_full = comm.weights(gathered)
    (x1, mixb, h2b), passed = _mix_out(y, oh, x, w_glu_b, b_glu, g_att, g_ssm, w_out_b, g_ffn, comm.gather_third(gathered))
    w_up_b = comm.w_up(passed)
    up = _mm(h2b, w_up_b, name="ffn_up", tm=1024, tn=1408, tk=1024)
    act = _conv_act(up, conv_w_full, conv_b)
    dy, dyb, loss_blk = _down_loss(act, w_down_b, x1, tgt)

    d_w_down = _mm(act, dyb, ta=True, name="d_w_down", tm=1408, tn=1024, tk=2048)
    dact = _mm(dyb, w_down_b, tb=True, name="d_act", tm=1024, tn=1408, tk=1024)
    dupb, dcw = _conv_act_bwd(up, dact, conv_w_full, conv_b)
    d_w_up = _mm(h2b, dupb, ta=True, b_parts=2, name="d_w_up", tm=1024, tn=1408, tk=2048)
    dh2 = _mm(dupb, w_up_b, tb=True, a_parts=2, name="d_h2", tm=1024, tn=1024, tk=1408)
    dx1, dx1b, doh, dys, d_w_glu, d_g_ffn, d_g_att, d_g_ssm, d_b_glu = _mix_bwd(
        dy, dh2, x1, g_ffn, w_out_b, y, oh, w_glu_b, b_glu, g_att, g_ssm)
    d_w_out = _mm(mixb, dx1b, ta=True, name="d_w_out", tm=1024, tn=1024, tk=2048)
    (du, dbbr, dbbi, dccr, dcci, dar, dai, dd), swapped = _ssm_bwd(dys, uf, ub, xr, xi, bbr, bbi, ar, ai, ccr, cci, dsk,
                                                                comm.swap(d_w_down, d_w_up, d_w_glu, d_w_out))
    unchunk = lambda t: t.reshape(_PARAM_SHAPE)
    dbb_re = unchunk(_diag_blocks(dbbr, SSM_GROUP, SSM_STATE))
    dbb_im = unchunk(_diag_blocks(dbbi, SSM_GROUP, SSM_STATE))
    first_row = (jnp.arange(_PARAM_SHAPE[0]) % SSM_GROUP == 0)[:, None]
    da_re = jnp.where(first_row, rep(dar.reshape(SSM_GROUPS, SSM_STATE)), 0.0)
    da_im = jnp.where(first_row, rep(dai.reshape(SSM_GROUPS, SSM_STATE)), 0.0)
    expand_t = (jnp.arange(SSM_GROUPS)[:, None] == (jnp.arange(_PARAM_SHAPE[0]) // SSM_GROUP)[None, :]).astype(BF16)
    d_lr, d_li, d_ls, d_bt_re, d_bt_im = _ssm_params_bwd(lr, li, ls, bt_re, bt_im, da_re, da_im, dbb_re, dbb_im, expand_t)
    from_bt = lambda t: t.reshape(SSM_GROUPS, SSM_GROUP, SSM_STATE).transpose(0, 2, 1)
    from_cc = lambda t: _diag_blocks(t, SSM_STATE, SSM_GROUP).transpose(0, 1, 3, 2).reshape(SSM_GROUPS, SSM_GROUP, SSM_STATE)

    small = {
        "lambda_re": d_lr, "lambda_im": d_li, "log_step": d_ls,
        "b_re": from_bt(d_bt_re), "b_im": from_bt(d_bt_im), "c_re": from_cc(dccr), "c_im": from_cc(dcci),
        "d_skip": dd, "b_glu": d_b_glu, "g_attn_out": d_g_att, "g_ssm_out": d_g_ssm, "g_ffn": d_g_ffn,
        "conv_b": dcw[:, 3],
    }
    d_conv_w = dcw[:, 0:3].transpose(1, 0, 2).reshape(3, 2 * D_FF)
    early_small = _pack_small([small[n] for n in _EARLY_SMALL] + [d_conv_w])

    (dqh, dkh, dvh, dcrow), early_lands = _attn_bwd(qh, kh, vh, crow, lse, doh, comm.scatter(swapped))
    dc128 = jnp.pad(dcrow.reshape(HEADS, s).T, ((0, 0), (0, LANES - HEADS)))
    (dzb, d_gq, d_gk, d_bf), small_landed = _prep_bwd(z, dqh, dkh, dvh, du, dc128, gq, gk, bf, gg,
                                                      comm.small_first(early_small))
    d_w_in_r, small_gathered = _mm(hb, dzb, ta=True, name="d_w_in", tm=512, tn=Z_COLS, tk=2048, carry=True,
                                   hosted=comm.small_second(small_landed))
    dh, swapped_in = _mm(dzb, w_in_r, tb=True, name="d_h", tm=1024, tn=1024, tk=Z_COLS, carry=True,
                         hosted=comm.swap_in(d_w_in_r))
    (dx, d_g_mix), land_in = _in_norm_bwd(x, g_mix, dh, dx1, comm.scatter_in(swapped_in))
    small.update({"g_mix": d_g_mix, "b_f": d_bf[0, :HEADS], "g_q": d_gq.reshape(HEADS, HEAD_DIM).sum(0),
                  "g_k": d_gk.reshape(HEADS, HEAD_DIM).sum(0)})
    big = {"w_in": d_w_in_r, "w_glu": d_w_glu, "w_out": d_w_out, "w_up": d_w_up, "w_down": d_w_down}
    return loss_blk[0, 0], dx, big, small, d_conv_w, (early_lands, land_in, small_gathered)


def kernel(x, g_mix, w_in, b_f, g_q, g_k, lambda_re, lambda_im, log_step, b_re, b_im, c_re, c_im, d_skip, w_glu, b_glu, g_attn_out, g_ssm_out, w_out, g_ffn, w_up, conv_w, conv_b, w_down, loss_target, m_g_mix, m_w_in, m_b_f, m_g_q, m_g_k, m_lambda_re, m_lambda_im, m_log_step, m_b_re, m_b_im, m_c_re, m_c_im, m_d_skip, m_w_glu, m_b_glu, m_g_attn_out, m_g_ssm_out, m_w_out, m_g_ffn, m_w_up, m_conv_w, m_conv_b, m_w_down, v_g_mix, v_w_in, v_b_f, v_g_q, v_g_k, v_lambda_re, v_lambda_im, v_log_step, v_b_re, v_b_im, v_c_re, v_c_im, v_d_skip, v_w_glu, v_b_glu, v_g_attn_out, v_g_ssm_out, v_w_out, v_g_ffn, v_w_up, v_conv_w, v_conv_b, v_w_down):
    args = dict(locals())
    order = ["g_mix", "w_in", "b_f", "g_q", "g_k", "lambda_re", "lambda_im", "log_step", "b_re", "b_im", "c_re", "c_im",
             "d_skip", "w_glu", "b_glu", "g_attn_out", "g_ssm_out", "w_out", "g_ffn", "w_up", "conv_w", "conv_b", "w_down"]
    comm = _MeshComm(args)
    chip = comm.chip
    loss_part, dx, big, small, d_conv_w, lands = _local_step(x[0], loss_target[0], args, comm)

    g_big = comm.reduce(lands[:2])

    shapes = dict(_SMALL)
    small_names = [n for n, _ in _SMALL]
    early = _unpack_small(_sum_devices(lands[2][0]), [shapes[n] for n in _EARLY_SMALL] + [(3, 2 * D_FF)])
    late = _unpack_small(_allreduce_small(_pack_small([small[n] for n in _LATE_SMALL] + [loss_part])),
                         [shapes[n] for n in _LATE_SMALL] + [()])
    loss = late[-1]
    g_conv_w = lax.dynamic_slice_in_dim(early[-1], chip * (2 * D_FF // N_CHIPS), 2 * D_FF // N_CHIPS, axis=1)
    g_small = {**dict(zip(_EARLY_SMALL, early[:-1])), **dict(zip(_LATE_SMALL, late[:-1]))}

    grad, delta, new_m, new_v = {}, {}, {}, {}
    for n in ("w_in", "w_glu", "w_out", "w_up", "w_down"):
        grad[n] = g_big[n]
        delta[n], new_m[n], new_v[n] = _adamw(args[n], g_big[n], args["m_" + n], args["v_" + n], name="adamw_" + n)
    grad["conv_w"] = g_conv_w
    delta["conv_w"], new_m["conv_w"], new_v["conv_w"] = _adamw(conv_w, g_conv_w, m_conv_w, v_conv_w, name="adamw_conv_w")
    stepped = _adamw_small([args[n] for n in small_names], [g_small[n] for n in small_names],
                           [args["m_" + n] for n in small_names], [args["v_" + n] for n in small_names])
    for i, n in enumerate(small_names):
        grad[n] = g_small[n]
        delta[n], new_m[n], new_v[n] = stepped[3 * i:3 * i + 3]

    return (loss, dx[None], *[grad[n] for n in order], *[delta[n] for n in order], *[new_m[n] for n in order],
            *[new_v[n] for n in order])
```

```python
import math

import jax
import jax.numpy as jnp
from jax import lax
from jax.experimental import pallas as pl
from jax.experimental.pallas import tpu as pltpu

F32 = jnp.float32
BF16 = jnp.bfloat16

D_MODEL = 1024
HEADS = 8
HEAD_DIM = 64
ATTN_W = 512
SSM_W = 512
SSM_GROUPS = 32
SSM_GROUP = 16
SSM_STATE = 64
N_STATE = SSM_GROUPS * SSM_STATE
D_FF = 2816
IN_COLS = 2056
Z_COLS = 2176
F_COL0 = 1536
U_COL0 = 1544
EPS = 1e-6
NEG_INF = -1e30
N_CHIPS = 4
LANES = 128
SUBLANES = 8
SSM_CHUNKS = 2
CHUNK_U = SSM_W // SSM_CHUNKS
CHUNK_S = N_STATE // SSM_CHUNKS
HEADS_PER_STEP = 4
STRIP = 128
N_STRIPS = D_FF // STRIP

ADAM_LR = 0.001
ADAM_B1 = 0.9
ADAM_B2 = 0.999
ADAM_EPS = 1e-08
ADAM_WD = 0.01
ADAM_STEP = 10

VMEM_LIMIT = 56 * 1024 * 1024
MESH = pl.DeviceIdType.MESH


def _pallas(body, **kw):
    return pl.pallas_call(body, **kw)


def _pcall(body, *, name, out_shape, in_specs, out_specs, grid=(), scratch_shapes=(), dims=None):
    params = pltpu.CompilerParams(dimension_semantics=dims, vmem_limit_bytes=VMEM_LIMIT)
    return _pallas(body, name=name, grid=grid, in_specs=in_specs, out_specs=out_specs,
                   out_shape=out_shape, scratch_shapes=scratch_shapes, compiler_params=params)


def _sds(shape, dtype=F32):
    return jax.ShapeDtypeStruct(shape, dtype)


def _dot(a, b):
    return jnp.dot(a, b, preferred_element_type=F32)


def _dot_nt(a, b):
    return lax.dot_general(a, b, (((1,), (1,)), ((), ())), preferred_element_type=F32)


def _dot_tn(a, b):
    return lax.dot_general(a, b, (((0,), (0,)), ((), ())), preferred_element_type=F32)


def _split3(x):
    hi = x.astype(BF16)
    r = x - hi.astype(F32)
    mid = r.astype(BF16)
    lo = (r - mid.astype(F32)).astype(BF16)
    return hi, mid, lo


def _dot_exact_r(x, m01):
    hi, mid, lo = _split3(x)
    return _dot(hi, m01) + _dot(mid, m01) + _dot(lo, m01)


def _dot_exact_l(m01, x):
    hi, mid, lo = _split3(x)
    return _dot(m01, hi) + _dot(m01, mid) + _dot(m01, lo)


def _sigmoid(x):
    return 1.0 / (1.0 + jnp.exp(-x))


def _rms(x, g):
    r = lax.rsqrt(jnp.mean(x * x, axis=-1, keepdims=True) + EPS)
    return x * r * g


def _rms_bwd(x, g, dy):
    r = lax.rsqrt(jnp.mean(x * x, axis=-1, keepdims=True) + EPS)
    w = dy * g
    dx = r * w - x * (r * r * r) * jnp.mean(w * x, axis=-1, keepdims=True)
    dg = jnp.sum(dy * x * r, axis=0, keepdims=True)
    return dx, dg


_GELU_K = math.sqrt(2.0 / math.pi)
_GELU_C = 0.044715


def _gelu(y):
    return y * (0.5 * (1.0 + jnp.tanh(_GELU_K * (y + _GELU_C * (y * y * y)))))


def _gelu_grad(y):
    t = jnp.tanh(_GELU_K * (y + _GELU_C * (y * y * y)))
    return 0.5 * (1.0 + t) + 0.5 * y * (1.0 - t * t) * (_GELU_K * (1.0 + 3.0 * _GELU_C * y * y))


def _tile(n, pref):
    if n <= pref:
        return n
    divs = [t for t in range(LANES, n + 1, LANES) if n % t == 0]
    below = [t for t in divs if t <= pref]
    if below and 2 * below[-1] >= pref:
        return below[-1]
    above = [t for t in divs if t > pref]
    return above[0] if above else n


def _row_tile(s):
    return min(256, s)


def _mm(a, b, *, name, tm, tn, tk, ta=False, tb=False, a_parts=1, b_parts=1, carry=False, hosted=None):
    if a_parts > 1:
        m, kk = a.shape[1], a.shape[2] * a_parts
    elif ta:
        kk, m = a.shape
    else:
        m, kk = a.shape
    if b_parts > 1:
        n = b.shape[2] * b_parts
    else:
        n = b.shape[0] if tb else b.shape[1]
    tm, tn, tk = _tile(m, tm), _tile(n // b_parts, tn), _tile(kk // a_parts, tk)
    k_per, n_per = kk // a_parts // tk, n // b_parts // tn

    def body(a_ref, b_ref, o_ref):
        k = pl.program_id(2)
        if ta:
            part = _dot_tn(a_ref[...], b_ref[...])
        elif tb:
            part = _dot_nt(a_ref[...], b_ref[...])
        else:
            part = _dot(a_ref[...], b_ref[...])

        @pl.when(k == 0)
        def _():
            o_ref[...] = part

        @pl.when(k > 0)
        def _():
            o_ref[...] += part

    if a_parts > 1:
        a_spec = pl.BlockSpec((None, tm, tk), lambda i, j, k: (k // k_per, i, k % k_per))
    else:
        a_spec = pl.BlockSpec((tk, tm), lambda i, j, k: (k, i)) if ta else pl.BlockSpec((tm, tk), lambda i, j, k: (i, k))
    if b_parts > 1:
        b_spec = pl.BlockSpec((None, tk, tn), lambda i, j, k: (j // n_per, k, j % n_per))
    else:
        b_spec = pl.BlockSpec((tn, tk), lambda i, j, k: (j, k)) if tb else pl.BlockSpec((tk, tn), lambda i, j, k: (k, j))
    grid = (m // tm, n // tn, kk // tk)
    at = lambda step: (lambda: jnp.logical_and(jnp.logical_and(pl.program_id(0) == step[0], pl.program_id(1) == step[1]),
                                               pl.program_id(2) == step[2]))
    (out,), carried = _host_pcall(body, hosted, at((0, 0, 0)), at(tuple(g - 1 for g in grid)), n_in=2, n_out=1, n_scratch=0,
                                  name=name, grid=grid, in_specs=[a_spec, b_spec],
                                  out_specs=[pl.BlockSpec((tm, tn), lambda i, j, k: (i, j))], out_shape=[_sds((m, n))],
                                  scratch_shapes=[], dims=("parallel", "parallel", "arbitrary"), operands=(a, b))
    return (out, carried) if carry else out


def _in_proj(x, g_mix, w_in_r):
    s = x.shape[0]
    tm = _row_tile(s)

    def body(x_ref, g_ref, w_ref, h_ref, z_ref):
        h = _rms(x_ref[...], g_ref[...]).astype(BF16)
        h_ref[...] = h
        z_ref[...] = _dot(h, w_ref[...])

    return _pcall(body, name="in_proj", grid=(s // tm,),
                  in_specs=[pl.BlockSpec((tm, D_MODEL), lambda i: (i, 0)), pl.BlockSpec((1, D_MODEL), lambda i: (0, 0)),
                            pl.BlockSpec((D_MODEL, Z_COLS), lambda i: (0, 0))],
                  out_specs=[pl.BlockSpec((tm, D_MODEL), lambda i: (i, 0)), pl.BlockSpec((tm, Z_COLS), lambda i: (i, 0))],
                  out_shape=[_sds((s, D_MODEL), BF16), _sds((s, Z_COLS))], dims=("parallel",))(x, g_mix, w_in_r)


def _split_heads(ref, val):
    for h in range(HEADS):
        ref[h] = val[:, h * HEAD_DIM:(h + 1) * HEAD_DIM].astype(ref.dtype)


def _merge_heads(ref):
    return jnp.concatenate([ref[h].astype(F32) for h in range(HEADS)], axis=-1)


def _forget_logits(z_ref, bf_ref):
    fl = z_ref[:, F_COL0:F_COL0 + LANES] + bf_ref[...]
    return jnp.where(lax.broadcasted_iota(jnp.int32, fl.shape, 1) < HEADS, fl, 0.0)


def _attn_prep(z, gq, gk, bf, gg):
    s = z.shape[0]
    tm = _row_tile(s)

    def body(z_ref, gq_ref, gk_ref, bf_ref, gg_ref, qn_ref, kn_ref, vb_ref, ub_ref, uf_ref, c_ref, carry_ref):
        i = pl.program_id(0)

        @pl.when(i == 0)
        def _():
            carry_ref[...] = jnp.zeros_like(carry_ref)

        gg_m = gg_ref[...]

        def head_norm(t, g):
            ssq = _dot_exact_r(t * t, gg_m)
            return t * lax.rsqrt(ssq * (1.0 / HEAD_DIM) + EPS) * g

        _split_heads(qn_ref, head_norm(z_ref[:, 0:ATTN_W], gq_ref[...]))
        _split_heads(kn_ref, head_norm(z_ref[:, ATTN_W:2 * ATTN_W], gk_ref[...]))
        _split_heads(vb_ref, z_ref[:, 2 * ATTN_W:3 * ATTN_W])
        u = z_ref[:, U_COL0:U_COL0 + SSM_W]
        uf_ref[...] = u
        ub_ref[...] = u.astype(BF16)
        fl = _forget_logits(z_ref, bf_ref)
        lf = jnp.minimum(fl, 0.0) - jnp.log1p(jnp.exp(-jnp.abs(fl)))
        row = lax.broadcasted_iota(jnp.int32, (tm, tm), 0)
        col = lax.broadcasted_iota(jnp.int32, (tm, tm), 1)
        tri = (row >= col).astype(BF16)
        c = _dot_exact_l(tri, lf) + carry_ref[...]
        c_ref[...] = c
        carry_ref[...] = c[tm - 1:tm, :]

    row_spec = lambda w: pl.BlockSpec((tm, w), lambda i: (i, 0))
    const = lambda shape: pl.BlockSpec(shape, lambda i: (0, 0))
    heads = pl.BlockSpec((HEADS, tm, HEAD_DIM), lambda i: (0, i, 0))
    return _pcall(body, name="attn_prep", grid=(s // tm,),
                  in_specs=[row_spec(Z_COLS), const((1, ATTN_W)), const((1, ATTN_W)), const((1, LANES)), const((ATTN_W, ATTN_W))],
                  out_specs=[heads] * 3 + [row_spec(SSM_W), row_spec(SSM_W), row_spec(LANES)],
                  out_shape=[_sds((HEADS, s, HEAD_DIM), BF16)] * 3 + [_sds((s, SSM_W), BF16), _sds((s, SSM_W)), _sds((s, LANES))],
                  scratch_shapes=[pltpu.VMEM((1, LANES), F32)], dims=("arbitrary",))(z, gq, gk, bf, gg)


def _attn_fwd(qh, kh, vh, crow, hosted=None):
    _, s, _ = qh.shape
    tq = _row_tile(s)
    scale = HEAD_DIM ** -0.5

    hp = HEADS
    nq = s // tq
    fold = lambda t, op: op(t[:, :tq // 2], t[:, tq // 2:])

    def body(q_ref, k_ref, v_ref, c_ref, o_ref, lse_ref, s_s):
        i = pl.program_id(1)

        def first(j, ms, diagonal):
            off = pl.multiple_of(j * tq, tq)
            out = []
            for hh in range(hp):
                sc = _dot_nt(q_ref[hh], k_ref[hh, pl.ds(off, tq), :]) * scale - c_ref[hh, :, pl.ds(off, tq)]
                if diagonal:
                    causal = lax.broadcasted_iota(jnp.int32, (tq, tq), 1) <= lax.broadcasted_iota(jnp.int32, (tq, tq), 0)
                    sc = jnp.where(causal, sc, NEG_INF)
                s_s[hh, j] = sc
                out.append(jnp.maximum(ms[hh], fold(sc, jnp.maximum)))
            return tuple(out)

        ms = lax.fori_loop(0, i, lambda j, c: first(j, c, False), (jnp.full((tq, tq // 2), NEG_INF, F32),) * hp)
        ms = [jnp.max(t, axis=-1, keepdims=True) for t in first(i, ms, True)]

        def second(j, carry):
            rows = pl.ds(pl.multiple_of(j * tq, tq), tq)
            out = []
            for hh in range(hp):
                ls, acc = carry[hh]
                p = jnp.exp(s_s[hh, j] - ms[hh])
                out.append((ls + fold(p, jnp.add), acc + _dot(p.astype(BF16), v_ref[hh, rows, :])))
            return tuple(out)

        zero = (jnp.zeros((tq, tq // 2), F32), jnp.zeros((tq, HEAD_DIM), F32))
        for hh, (ls, acc) in enumerate(lax.fori_loop(0, i + 1, second, (zero,) * hp)):
            l = jnp.sum(ls, axis=-1, keepdims=True)
            o_ref[hh] = acc / l
            lse_ref[hh] = ms[hh] + jnp.log(l)

    blk = pl.BlockSpec((hp, tq, HEAD_DIM), lambda h, i: (h, i, 0))
    full = pl.BlockSpec((hp, s, HEAD_DIM), lambda h, i: (h, 0, 0))
    nh = HEADS // hp
    first = lambda: jnp.logical_and(pl.program_id(0) == 0, pl.program_id(1) == 0)
    last = lambda: jnp.logical_and(pl.program_id(0) == nh - 1, pl.program_id(1) == nq - 1)
    return _host_pcall(body, hosted, first, last, n_in=4, n_out=2, n_scratch=1, name="attn_fwd", grid=(nh, nq),
                       in_specs=[blk, full, full, pl.BlockSpec((hp, 1, s), lambda h, i: (h, 0, 0))],
                       out_specs=[blk, pl.BlockSpec((hp, tq, 1), lambda h, i: (h, i, 0))],
                       out_shape=[_sds((HEADS, s, HEAD_DIM)), _sds((HEADS, s, 1))],
                       scratch_shapes=[pltpu.VMEM((hp, nq, tq, tq), F32)],
                       dims=("parallel", "parallel"), operands=(qh, kh, vh, crow))


def _ssm_param_fn(lr, li, ls, br, bi):
    step = jnp.exp(ls)
    er = jnp.exp(lr * step)
    ab_re = er * jnp.cos(li * step)
    ab_im = er * jnp.sin(li * step)
    num_re = ab_re - 1.0
    num_im = ab_im
    den = lr * lr + li * li
    f_re = (num_re * lr + num_im * li) / den
    f_im = (num_im * lr - num_re * li) / den
    bb_re = f_re * br - f_im * bi
    bb_im = f_re * bi + f_im * br
    return ab_re, ab_im, bb_re, bb_im


_PARAM_SHAPE = (SSM_GROUPS * SSM_GROUP, SSM_STATE)


def _ssm_params(lr, li, ls, br, bi):
    def body(lr_ref, li_ref, ls_ref, br_ref, bi_ref, ar_ref, ai_ref, bbr_ref, bbi_ref):
        ar, ai, bbr, bbi = _ssm_param_fn(lr_ref[...], li_ref[...], ls_ref[...], br_ref[...], bi_ref[...])
        ar_ref[...] = ar
        ai_ref[...] = ai
        bbr_ref[...] = bbr
        bbi_ref[...] = bbi

    spec = pl.BlockSpec(_PARAM_SHAPE, lambda: (0, 0))
    return _pcall(body, name="ssm_params", in_specs=[spec] * 5, out_specs=[spec] * 4,
                  out_shape=[_sds(_PARAM_SHAPE)] * 4)(lr, li, ls, br, bi)


def _ssm_params_bwd(lr, li, ls, br, bi, dar, dai, dbbr, dbbi, expand_t):
    def body(lr_ref, li_ref, ls_ref, br_ref, bi_ref, dar_ref, dai_ref, dbbr_ref, dbbi_ref, et_ref,
             dlr_ref, dli_ref, dls_ref, dbr_ref, dbi_ref):
        _, vjp = jax.vjp(_ssm_param_fn, lr_ref[...], li_ref[...], ls_ref[...], br_ref[...], bi_ref[...])
        dlr, dli, dls, dbr, dbi = vjp((dar_ref[...], dai_ref[...], dbbr_ref[...], dbbi_ref[...]))
        et = et_ref[...]
        dlr_ref[...] = _dot_exact_l(et, dlr)
        dli_ref[...] = _dot_exact_l(et, dli)
        dls_ref[...] = jnp.sum(_dot_exact_l(et, dls), axis=-1, keepdims=True)
        dbr_ref[...] = dbr
        dbi_ref[...] = dbi

    spec = pl.BlockSpec(_PARAM_SHAPE, lambda: (0, 0))
    gspec = pl.BlockSpec((SSM_GROUPS, SSM_STATE), lambda: (0, 0))
    return _pcall(body, name="ssm_params_bwd",
                  in_specs=[spec] * 9 + [pl.BlockSpec((SSM_GROUPS, _PARAM_SHAPE[0]), lambda: (0, 0))],
                  out_specs=[gspec, gspec, pl.BlockSpec((SSM_GROUPS, 1), lambda: (0, 0)), spec, spec],
                  out_shape=[_sds((SSM_GROUPS, SSM_STATE))] * 2 + [_sds((SSM_GROUPS, 1))] + [_sds(_PARAM_SHAPE)] * 2,
                  )(lr, li, ls, br, bi, dar, dai, dbbr, dbbi, expand_t)


def _cmul(ar, ai, br, bi):
    return ar * br - ai * bi, ar * bi + ai * br


def _scan_consts(ar, ai, width, reverse):
    row = lax.broadcasted_iota(jnp.int32, (SUBLANES, width), 0)
    pw = [(ar, ai)]
    for _ in range(SUBLANES - 1):
        pw.append(_cmul(pw[-1][0], pw[-1][1], ar, ai))
    steps = []
    for d in (1, 2, 4):
        keep = (row < SUBLANES - d) if reverse else (row >= d)
        steps.append((d, jnp.where(keep, pw[d - 1][0], 0.0), jnp.where(keep, pw[d - 1][1], 0.0)))
    pr = jnp.zeros((SUBLANES, width), F32)
    pi = jnp.zeros((SUBLANES, width), F32)
    for r in range(SUBLANES):
        e = (SUBLANES - r) if reverse else (r + 1)
        pr = jnp.where(row == r, pw[e - 1][0], pr)
        pi = jnp.where(row == r, pw[e - 1][1], pi)
    return steps, pr, pi


def _scan_tile(xr, xi, cr, ci, consts, reverse):
    steps, pr, pi = consts
    for d, mr, mi in steps:
        sh = (SUBLANES - d) if reverse else d
        sr = pltpu.roll(xr, sh, 0)
        si = pltpu.roll(xi, sh, 0)
        xr, xi = xr + mr * sr - mi * si, xi + mr * si + mi * sr
    return xr + pr * cr - pi * ci, xi + pr * ci + pi * cr


def _ssm_fwd(ub, uf, bbr, bbi, ar, ai, ccr, cci, dsk, hosted=None):
    s = ub.shape[0]
    tm = _row_tile(s)
    nt = tm // SUBLANES

    def body(ub_ref, u_ref, bbr_ref, bbi_ref, ar_ref, ai_ref, ccr_ref, cci_ref, dsk_ref,
             xr_ref, xi_ref, y_ref, cr_s, ci_s):
        i = pl.program_id(1)

        @pl.when(i == 0)
        def _():
            cr_s[...] = jnp.zeros_like(cr_s)
            ci_s[...] = jnp.zeros_like(ci_s)

        u_b = ub_ref[...]
        xr_ref[...] = _dot(u_b, bbr_ref[0])
        xi_ref[...] = _dot(u_b, bbi_ref[0])
        consts = _scan_consts(ar_ref[0], ai_ref[0], CHUNK_S, False)

        def tile(k, carry):
            cr, ci = carry
            sl = pl.ds(pl.multiple_of(k * SUBLANES, SUBLANES), SUBLANES)
            xr, xi = _scan_tile(xr_ref[sl, :], xi_ref[sl, :], cr, ci, consts, False)
            xr_ref[sl, :] = xr
            xi_ref[sl, :] = xi
            return xr[SUBLANES - 1:SUBLANES, :], xi[SUBLANES - 1:SUBLANES, :]

        cr, ci = lax.fori_loop(0, nt, tile, (cr_s[...], ci_s[...]))
        cr_s[...] = cr
        ci_s[...] = ci
        y_ref[...] = (_dot(xr_ref[...].astype(BF16), ccr_ref[0]) - _dot(xi_ref[...].astype(BF16), cci_ref[0])
                      + dsk_ref[...] * u_ref[...])

    wspec = lambda a, b: pl.BlockSpec((1, a, b), lambda j, i: (j, 0, 0))
    nb = s // tm
    first = lambda: jnp.logical_and(pl.program_id(0) == 0, pl.program_id(1) == 0)
    last = lambda: jnp.logical_and(pl.program_id(0) == SSM_CHUNKS - 1, pl.program_id(1) == nb - 1)
    return _host_pcall(
        body, hosted, first, last, n_in=9, n_out=3, n_scratch=2, name="ssm_fwd", grid=(SSM_CHUNKS, nb),
        in_specs=[pl.BlockSpec((tm, CHUNK_U), lambda j, i: (i, j)),
                  pl.BlockSpec((tm, CHUNK_U), lambda j, i: (i, j)),
                  wspec(CHUNK_U, CHUNK_S), wspec(CHUNK_U, CHUNK_S), wspec(1, CHUNK_S), wspec(1, CHUNK_S),
                  wspec(CHUNK_S, CHUNK_U), wspec(CHUNK_S, CHUNK_U),
                  pl.BlockSpec((1, CHUNK_U), lambda j, i: (0, j))],
        out_specs=[pl.BlockSpec((tm, CHUNK_S), lambda j, i: (i, j)), pl.BlockSpec((tm, CHUNK_S), lambda j, i: (i, j)),
                   pl.BlockSpec((tm, CHUNK_U), lambda j, i: (i, j))],
        out_shape=[_sds((s, N_STATE)), _sds((s, N_STATE)), _sds((s, SSM_W))],
        scratch_shapes=[pltpu.VMEM((1, CHUNK_S), F32)] * 2,
        dims=("parallel", "arbitrary"), operands=(ub, uf, bbr, bbi, ar, ai, ccr, cci, dsk))


def _ssm_glu(y, w_glu, b_glu):
    ge = _gelu(y)
    sg = _sigmoid(_dot(ge.astype(BF16), w_glu) + b_glu)
    return ge, sg


def _mix_out(y, att, x, w_glu, b_glu, g_att, g_ssm, w_out, g_ffn, hosted=None):
    s = x.shape[0]
    tm = _row_tile(s)

    def body(y_ref, att_ref, x_ref, wg_ref, bg_ref, ga_ref, gs_ref, wo_ref, gf_ref, x1_ref, mix_ref, h2_ref):
        ge, sg = _ssm_glu(y_ref[...], wg_ref[...], bg_ref[...])
        ms = _rms(ge * sg, gs_ref[...]).astype(BF16)
        ma = _rms(_merge_heads(att_ref), ga_ref[...]).astype(BF16)
        mix_ref[:, 0:ATTN_W] = ma
        mix_ref[:, ATTN_W:D_MODEL] = ms
        x1 = x_ref[...] + (_dot(ma, wo_ref[0:ATTN_W, :]) + _dot(ms, wo_ref[ATTN_W:D_MODEL, :]))
        x1_ref[...] = x1
        h2_ref[...] = _rms(x1, gf_ref[...]).astype(BF16)

    row = lambda w: pl.BlockSpec((tm, w), lambda i: (i, 0))
    const = lambda a, b: pl.BlockSpec((a, b), lambda i: (0, 0))
    nb = s // tm
    return _host_pcall(body, hosted, lambda: pl.program_id(0) == 0, lambda: pl.program_id(0) == nb - 1,
                       n_in=9, n_out=3, n_scratch=0, name="mix_out", grid=(nb,),
                       in_specs=[row(SSM_W), pl.BlockSpec((HEADS, tm, HEAD_DIM), lambda i: (0, i, 0)), row(D_MODEL),
                                 const(SSM_W, SSM_W), const(1, SSM_W),
                                 const(1, ATTN_W), const(1, SSM_W), const(D_MODEL, D_MODEL), const(1, D_MODEL)],
                       out_specs=[row(D_MODEL)] * 3,
                       out_shape=[_sds((s, D_MODEL)), _sds((s, D_MODEL), BF16), _sds((s, D_MODEL), BF16)],
                       scratch_shapes=[], dims=("parallel",), operands=(y, att, x, w_glu, b_glu, g_att, g_ssm, w_out, g_ffn))


CONV_CHUNK = 64


def _conv_rows(pad_ref, w, b, r0, n):
    y = b + pad_ref[pl.ds(r0 + SUBLANES - 2, n), :] * w[0:1, :]
    y = y + pad_ref[pl.ds(r0 + SUBLANES - 1, n), :] * w[1:2, :]
    return y + pad_ref[pl.ds(r0 + SUBLANES, n), :] * w[2:3, :]


def _fill_front_pad(pad_ref, strip_ref, s):
    pad_ref[0:SUBLANES, :] = jnp.zeros((SUBLANES, STRIP), F32)
    for r0 in range(0, s, CONV_CHUNK):
        pad_ref[pl.ds(SUBLANES + r0, CONV_CHUNK), :] = strip_ref[pl.ds(r0, CONV_CHUNK), :]


def _conv_act(up, conv_w, conv_b):
    s = up.shape[0]

    def body(ug_ref, uv_ref, wg_ref, wv_ref, bg_ref, bv_ref, act_ref, pg_ref, pv_ref):
        _fill_front_pad(pg_ref, ug_ref, s)
        _fill_front_pad(pv_ref, uv_ref, s)
        wg, wv, bg, bv = wg_ref[...], wv_ref[...], bg_ref[...], bv_ref[...]
        for r0 in range(0, s, CONV_CHUNK):
            hg = _conv_rows(pg_ref, wg, bg, r0, CONV_CHUNK)
            hv = _conv_rows(pv_ref, wv, bv, r0, CONV_CHUNK)
            act_ref[pl.ds(r0, CONV_CHUNK), :] = (hg * _sigmoid(hg) * hv).astype(BF16)

    strip = lambda off: pl.BlockSpec((s, STRIP), lambda j: (0, j + off))
    wsp = lambda off: pl.BlockSpec((3, STRIP), lambda j: (0, j + off))
    bsp = lambda off: pl.BlockSpec((1, STRIP), lambda j: (0, j + off))
    return _pcall(body, name="conv_act", grid=(N_STRIPS,),
                  in_specs=[strip(0), strip(N_STRIPS), wsp(0), wsp(N_STRIPS), bsp(0), bsp(N_STRIPS)],
                  out_specs=pl.BlockSpec((s, STRIP), lambda j: (0, j)), out_shape=_sds((s, D_FF), BF16),
                  scratch_shapes=[pltpu.VMEM((s + SUBLANES, STRIP), F32)] * 2,
                  dims=("parallel",))(up, up, conv_w, conv_w, conv_b, conv_b)


def _down_loss(act, w_down, x1, tgt):
    s = x1.shape[0]
    tm = _row_tile(s)

    def body(a_ref, w_ref, x1_ref, t_ref, dy_ref, dyb_ref, loss_ref):
        i = pl.program_id(0)

        @pl.when(i == 0)
        def _():
            loss_ref[...] = jnp.zeros_like(loss_ref)

        diff = x1_ref[...] + _dot(a_ref[...], w_ref[...]) - t_ref[...]
        dy = diff * (1.0 / D_MODEL)
        dy_ref[...] = dy
        dyb_ref[...] = dy.astype(BF16)
        loss_ref[...] += 0.5 * jnp.sum(diff * dy)

    row = lambda w: pl.BlockSpec((tm, w), lambda i: (i, 0))
    return _pcall(body, name="down_loss", grid=(s // tm,),
                  in_specs=[row(D_FF), pl.BlockSpec((D_FF, D_MODEL), lambda i: (0, 0)), row(D_MODEL), row(D_MODEL)],
                  out_specs=[row(D_MODEL), row(D_MODEL), pl.BlockSpec((SUBLANES, LANES), lambda i: (0, 0))],
                  out_shape=[_sds((s, D_MODEL)), _sds((s, D_MODEL), BF16), _sds((SUBLANES, LANES))],
                  dims=("arbitrary",))(act, w_down, x1, tgt)


def _conv_act_bwd(up, dact, conv_w, conv_b):
    s = up.shape[0]
    ch = CONV_CHUNK

    def body(ug_ref, uv_ref, da_ref, wg_ref, wv_ref, bg_ref, bv_ref, dup_ref, dcw_ref, pg_ref, pv_ref, dg_ref, dv_ref):
        _fill_front_pad(pg_ref, ug_ref, s)
        _fill_front_pad(pv_ref, uv_ref, s)
        zero = jnp.zeros((SUBLANES, STRIP), F32)
        dg_ref[pl.ds(s, SUBLANES), :] = zero
        dv_ref[pl.ds(s, SUBLANES), :] = zero
        wg, wv, bg, bv = wg_ref[...], wv_ref[...], bg_ref[...], bv_ref[...]
        tile_sum = lambda t: jnp.sum(t.reshape(ch // SUBLANES, SUBLANES, STRIP), axis=0)
        accs = [[zero] * 4, [zero] * 4]
        for r0 in range(0, s, ch):
            hg = _conv_rows(pg_ref, wg, bg, r0, ch)
            hv = _conv_rows(pv_ref, wv, bv, r0, ch)
            sg = _sigmoid(hg)
            da = da_ref[pl.ds(r0, ch), :]
            dhs = (da * hv * (sg * (1.0 + hg * (1.0 - sg))), da * (hg * sg))
            for half, (dh, d_ref, p_ref) in enumerate(zip(dhs, (dg_ref, dv_ref), (pg_ref, pv_ref))):
                d_ref[pl.ds(r0, ch), :] = dh
                for k in range(3):
                    accs[half][k] = accs[half][k] + tile_sum(dh * p_ref[pl.ds(r0 + SUBLANES - 2 + k, ch), :])
                accs[half][3] = accs[half][3] + tile_sum(dh)
        for half, (d_ref, w) in enumerate(((dg_ref, wg), (dv_ref, wv))):
            for r0 in range(0, s, ch):
                dup = (d_ref[pl.ds(r0, ch), :] * w[2:3, :] + d_ref[pl.ds(r0 + 1, ch), :] * w[1:2, :]
                       + d_ref[pl.ds(r0 + 2, ch), :] * w[0:1, :])
                dup_ref[half, pl.ds(r0, ch), :] = dup.astype(BF16)
            rid = lax.broadcasted_iota(jnp.int32, (SUBLANES, STRIP), 0)
            out = zero
            for k in range(4):
                out = jnp.where(rid == k, jnp.sum(accs[half][k], axis=0, keepdims=True), out)
            dcw_ref[half] = out

    strip = lambda off: pl.BlockSpec((s, STRIP), lambda j: (0, j + off))
    wsp = lambda off: pl.BlockSpec((3, STRIP), lambda j: (0, j + off))
    bsp = lambda off: pl.BlockSpec((1, STRIP), lambda j: (0, j + off))
    return _pcall(body, name="conv_act_bwd", grid=(N_STRIPS,),
                  in_specs=[strip(0), strip(N_STRIPS), strip(0), wsp(0), wsp(N_STRIPS), bsp(0), bsp(N_STRIPS)],
                  out_specs=[pl.BlockSpec((2, s, STRIP), lambda j: (0, 0, j)), pl.BlockSpec((2, SUBLANES, STRIP), lambda j: (0, 0, j))],
                  out_shape=[_sds((2, s, D_FF), BF16), _sds((2, SUBLANES, D_FF))],
                  scratch_shapes=[pltpu.VMEM((s + SUBLANES, STRIP), F32)] * 4,
                  dims=("parallel",))(up, up, dact, conv_w, conv_w, conv_b, conv_b)


def _mix_bwd(dy, dh2, x1, g_ffn, w_out, y, att, w_glu, b_glu, g_att, g_ssm):
    s = dy.shape[0]
    tm = _row_tile(s)

    def body(dy_ref, dh2_ref, x1_ref, gf_ref, wo_ref, y_ref, att_ref, wg_ref, bg_ref, ga_ref, gs_ref,
             dx1_ref, dx1b_ref, datt_ref, dys_ref, dwg_ref, dgf_ref, dga_ref, dgs_ref, dbg_ref):
        i = pl.program_id(0)

        @pl.when(i == 0)
        def _():
            for r in (dwg_ref, dgf_ref, dga_ref, dgs_ref, dbg_ref):
                r[...] = jnp.zeros_like(r)

        dxn, dgf = _rms_bwd(x1_ref[...], gf_ref[...], dh2_ref[...])
        dx1 = dy_ref[...] + dxn
        dx1_ref[...] = dx1
        dx1b = dx1.astype(BF16)
        dx1b_ref[...] = dx1b
        dgf_ref[...] += dgf
        dma = _dot_nt(dx1b, wo_ref[0:ATTN_W, :])
        dms = _dot_nt(dx1b, wo_ref[ATTN_W:D_MODEL, :])
        datt, dga = _rms_bwd(_merge_heads(att_ref), ga_ref[...], dma)
        _split_heads(datt_ref, datt)
        dga_ref[...] += dga
        yv = y_ref[...]
        ge, sg = _ssm_glu(yv, wg_ref[...], bg_ref[...])
        dssm, dgs = _rms_bwd(ge * sg, gs_ref[...], dms)
        dgs_ref[...] += dgs
        dgl = dssm * ge * sg * (1.0 - sg)
        dglb = dgl.astype(BF16)
        dge = dssm * sg + _dot_nt(dglb, wg_ref[...])
        dbg_ref[...] += jnp.sum(dgl, axis=0, keepdims=True)
        dwg_ref[...] += _dot_tn(ge.astype(BF16), dglb)
        dys_ref[...] = dge * _gelu_grad(yv)

    row = lambda w: pl.BlockSpec((tm, w), lambda i: (i, 0))
    const = lambda a, b: pl.BlockSpec((a, b), lambda i: (0, 0))
    heads = pl.BlockSpec((HEADS, tm, HEAD_DIM), lambda i: (0, i, 0))
    return _pcall(body, name="mix_bwd", grid=(s // tm,),
                  in_specs=[row(D_MODEL), row(D_MODEL), row(D_MODEL), const(1, D_MODEL), const(D_MODEL, D_MODEL), row(SSM_W),
                            heads, const(SSM_W, SSM_W), const(1, SSM_W), const(1, ATTN_W), const(1, SSM_W)],
                  out_specs=[row(D_MODEL), row(D_MODEL), heads, row(SSM_W), const(SSM_W, SSM_W), const(1, D_MODEL),
                             const(1, ATTN_W), const(1, SSM_W), const(1, SSM_W)],
                  out_shape=[_sds((s, D_MODEL)), _sds((s, D_MODEL), BF16), _sds((HEADS, s, HEAD_DIM)), _sds((s, SSM_W)),
                             _sds((SSM_W, SSM_W)), _sds((1, D_MODEL)), _sds((1, ATTN_W)), _sds((1, SSM_W)), _sds((1, SSM_W))],
                  dims=("arbitrary",))(dy, dh2, x1, g_ffn, w_out, y, att, w_glu, b_glu, g_att, g_ssm)


def _ssm_bwd(dys, uf, ub, xr, xi, bbr, bbi, ar, ai, ccr, cci, dsk, hosted=None):
    s = dys.shape[0]
    tm = _row_tile(s)
    nb = s // tm
    nt = tm // SUBLANES

    def body(dy_ref, u_ref, ub_ref, xr_ref, xi_ref, xrp_ref, xip_ref, bbr_ref, bbi_ref, ar_ref, ai_ref, ccr_ref,
             cci_ref, dsk_ref, du_ref, dbbr_ref, dbbi_ref, dccr_ref, dcci_ref, dar_ref, dai_ref, dd_ref,
             gr_s, gi_s, cr_s, ci_s, accr_s, acci_s):
        i = pl.program_id(1)
        first_block = i == nb - 1

        @pl.when(i == 0)
        def _():
            for r in (cr_s, ci_s, accr_s, acci_s, dbbr_ref, dbbi_ref, dccr_ref, dcci_ref, dd_ref):
                r[...] = jnp.zeros_like(r)

        dy = dy_ref[...]
        dyb = dy.astype(BF16)
        gr_s[...] = _dot_nt(dyb, ccr_ref[0])
        gi_s[...] = -_dot_nt(dyb, cci_ref[0])
        consts = _scan_consts(ar_ref[0], -ai_ref[0], CHUNK_S, True)
        row = lax.broadcasted_iota(jnp.int32, (SUBLANES, CHUNK_S), 0)

        def tile(kk, carry):
            cr, ci, accr, acci = carry
            k = nt - 1 - kk
            sl = pl.ds(pl.multiple_of(k * SUBLANES, SUBLANES), SUBLANES)
            gr, gi = _scan_tile(gr_s[sl, :], gi_s[sl, :], cr, ci, consts, True)
            gr_s[sl, :] = gr
            gi_s[sl, :] = gi
            slp = pl.ds(pl.multiple_of(jnp.maximum(k - 1, 0) * SUBLANES, SUBLANES), SUBLANES)
            inner = k > 0
            pr_t = jnp.where(inner, xr_ref[slp, :], xrp_ref[...])
            pi_t = jnp.where(inner, xi_ref[slp, :], xip_ref[...])
            live = jnp.logical_or(inner, jnp.logical_not(first_block))
            top_r = jnp.where(live, pltpu.roll(pr_t, 1, 0), 0.0)
            top_i = jnp.where(live, pltpu.roll(pi_t, 1, 0), 0.0)
            xpr = jnp.where(row == 0, top_r, pltpu.roll(xr_ref[sl, :], 1, 0))
            xpi = jnp.where(row == 0, top_i, pltpu.roll(xi_ref[sl, :], 1, 0))
            accr = accr + gr * xpr + gi * xpi
            acci = acci + gi * xpr - gr * xpi
            return gr[0:1, :], gi[0:1, :], accr, acci

        zeros = jnp.zeros((SUBLANES, CHUNK_S), F32)
        cr, ci, accr, acci = lax.fori_loop(0, nt, tile, (cr_s[...], ci_s[...], zeros, zeros))
        cr_s[...] = cr
        ci_s[...] = ci
        accr_s[...] += accr
        acci_s[...] += acci
        grb = gr_s[...].astype(BF16)
        gib = gi_s[...].astype(BF16)
        u_b = ub_ref[...]
        du_ref[...] = _dot_nt(grb, bbr_ref[0]) + _dot_nt(gib, bbi_ref[0]) + dsk_ref[...] * dy
        dbbr_ref[0] += _dot_tn(u_b, grb)
        dbbi_ref[0] += _dot_tn(u_b, gib)
        dccr_ref[0] += _dot_tn(xr_ref[...].astype(BF16), dyb)
        dcci_ref[0] -= _dot_tn(xi_ref[...].astype(BF16), dyb)
        dd_ref[...] += jnp.sum(dy * u_ref[...], axis=0, keepdims=True)

        @pl.when(i == nb - 1)
        def _():
            dar_ref[0] = jnp.sum(accr_s[...], axis=0, keepdims=True)
            dai_ref[0] = jnp.sum(acci_s[...], axis=0, keepdims=True)

    tiles_per_block = tm // SUBLANES
    rb = lambda i: nb - 1 - i
    wspec = lambda a, b: pl.BlockSpec((1, a, b), lambda j, i: (j, 0, 0))
    xblk = pl.BlockSpec((tm, CHUNK_S), lambda j, i: (rb(i), j))
    xprev = pl.BlockSpec((SUBLANES, CHUNK_S), lambda j, i: (jnp.maximum(rb(i) * tiles_per_block - 1, 0), j))
    ublk = pl.BlockSpec((tm, CHUNK_U), lambda j, i: (rb(i), j))
    first = lambda: jnp.logical_and(pl.program_id(0) == 0, pl.program_id(1) == 0)
    last = lambda: jnp.logical_and(pl.program_id(0) == SSM_CHUNKS - 1, pl.program_id(1) == nb - 1)
    return _host_pcall(
        body, hosted, first, last, n_in=14, n_out=8, n_scratch=6, name="ssm_bwd", grid=(SSM_CHUNKS, nb),
        in_specs=[ublk, ublk, ublk, xblk, xblk, xprev, xprev,
                  wspec(CHUNK_U, CHUNK_S), wspec(CHUNK_U, CHUNK_S), wspec(1, CHUNK_S), wspec(1, CHUNK_S),
                  wspec(CHUNK_S, CHUNK_U), wspec(CHUNK_S, CHUNK_U), pl.BlockSpec((1, CHUNK_U), lambda j, i: (0, j))],
        out_specs=[ublk, wspec(CHUNK_U, CHUNK_S), wspec(CHUNK_U, CHUNK_S), wspec(CHUNK_S, CHUNK_U),
                   wspec(CHUNK_S, CHUNK_U), wspec(1, CHUNK_S), wspec(1, CHUNK_S),
                   pl.BlockSpec((1, CHUNK_U), lambda j, i: (0, j))],
        out_shape=[_sds((s, SSM_W)), _sds((SSM_CHUNKS, CHUNK_U, CHUNK_S)), _sds((SSM_CHUNKS, CHUNK_U, CHUNK_S)),
                   _sds((SSM_CHUNKS, CHUNK_S, CHUNK_U)), _sds((SSM_CHUNKS, CHUNK_S, CHUNK_U)),
                   _sds((SSM_CHUNKS, 1, CHUNK_S)), _sds((SSM_CHUNKS, 1, CHUNK_S)), _sds((1, SSM_W))],
        scratch_shapes=[pltpu.VMEM((tm, CHUNK_S), F32)] * 2 + [pltpu.VMEM((1, CHUNK_S), F32)] * 2
                       + [pltpu.VMEM((SUBLANES, CHUNK_S), F32)] * 2,
        dims=("parallel", "arbitrary"), operands=(dys, uf, ub, xr, xi, xr, xi, bbr, bbi, ar, ai, ccr, cci, dsk))


def _attn_probs(q, ks, cs, lse, scale, diagonal):
    p = jnp.exp(_dot_nt(q, ks) * scale - cs - lse)
    if diagonal:
        tq, tk = p.shape
        causal = lax.broadcasted_iota(jnp.int32, (tq, tk), 1) <= lax.broadcasted_iota(jnp.int32, (tq, tk), 0)
        p = jnp.where(causal, p, 0.0)
    return p


def _attn_bwd(qh, kh, vh, crow, lse, doh, hosted=None):
    _, s, _ = qh.shape
    tq = _row_tile(s)
    nq = s // tq
    scale = HEAD_DIM ** -0.5
    hp = HEADS_PER_STEP

    def body(q_ref, k_ref, v_ref, c_ref, lse_ref, do_ref, dq_ref, dk_ref, dv_ref, dc_ref, p_s, dp_s):
        i = pl.program_id(1)

        @pl.when(i == 0)
        def _():
            for r in (dk_ref, dv_ref, dc_ref):
                r[...] = jnp.zeros_like(r)

        dobs = [do_ref[hh].astype(BF16) for hh in range(hp)]

        def first(j, dls, diagonal):
            off = pl.multiple_of(j * tq, tq)
            out = []
            for hh in range(hp):
                p = _attn_probs(q_ref[hh], k_ref[hh, pl.ds(off, tq), :], c_ref[hh, :, pl.ds(off, tq)], lse_ref[hh],
                                scale, diagonal)
                dp = _dot_nt(dobs[hh], v_ref[hh, pl.ds(off, tq), :])
                p_s[hh, j] = p
                dp_s[hh, j] = dp
                out.append(dls[hh] + jnp.sum(p * dp, axis=-1, keepdims=True))
            return tuple(out)

        zero_col = jnp.zeros((tq, 1), F32)
        dls = lax.fori_loop(0, i, lambda j, c: first(j, c, False), (zero_col,) * hp)
        dls = first(i, dls, True)

        def second(j, dqs):
            rows = pl.ds(pl.multiple_of(j * tq, tq), tq)
            out = []
            for hh in range(hp):
                p = p_s[hh, j]
                ds = p * (dp_s[hh, j] - dls[hh])
                dsb = ds.astype(BF16)
                dv_ref[hh, rows, :] += _dot_tn(p.astype(BF16), dobs[hh])
                dk_ref[hh, rows, :] += _dot_tn(dsb, q_ref[hh]) * scale
                dc_ref[hh, :, rows] -= jnp.sum(ds, axis=0, keepdims=True)
                out.append(dqs[hh] + _dot(dsb, k_ref[hh, rows, :]))
            return tuple(out)

        dqs = lax.fori_loop(0, i + 1, second, (jnp.zeros((tq, HEAD_DIM), F32),) * hp)
        for hh in range(hp):
            dq_ref[hh] = dqs[hh] * scale

    blk = pl.BlockSpec((hp, tq, HEAD_DIM), lambda h, i: (h, i, 0))
    full = pl.BlockSpec((hp, s, HEAD_DIM), lambda h, i: (h, 0, 0))
    crow_spec = pl.BlockSpec((hp, 1, s), lambda h, i: (h, 0, 0))
    nh = HEADS // hp
    first = lambda: jnp.logical_and(pl.program_id(0) == 0, pl.program_id(1) == 0)
    last = lambda: jnp.logical_and(pl.program_id(0) == nh - 1, pl.program_id(1) == nq - 1)
    return _host_pcall(body, hosted, first, last, n_in=6, n_out=4, n_scratch=2, name="attn_bwd", grid=(nh, nq),
                       in_specs=[blk, full, full, crow_spec, pl.BlockSpec((hp, tq, 1), lambda h, i: (h, i, 0)), blk],
                       out_specs=[blk, full, full, crow_spec],
                       out_shape=[_sds((HEADS, s, HEAD_DIM))] * 3 + [_sds((HEADS, 1, s))],
                       scratch_shapes=[pltpu.VMEM((hp, nq, tq, tq), F32)] * 2,
                       dims=("parallel", "arbitrary"), operands=(qh, kh, vh, crow, lse, doh))


def _prep_bwd(z, dqn, dkn, dv, du, dc, gq, gk, bf, gg, hosted=None):
    s = z.shape[0]
    tm = _row_tile(s)
    nb = s // tm

    def body(z_ref, dqn_ref, dkn_ref, dv_ref, du_ref, dc_ref, gq_ref, gk_ref, bf_ref, gg_ref,
             dz_ref, dgq_ref, dgk_ref, dbf_ref, carry_ref):
        i = pl.program_id(0)

        @pl.when(i == 0)
        def _():
            for r in (dgq_ref, dgk_ref, dbf_ref, carry_ref):
                r[...] = jnp.zeros_like(r)

        gg_m = gg_ref[...]

        def head_norm_bwd(t, g, dn):
            r = lax.rsqrt(_dot_exact_r(t * t, gg_m) * (1.0 / HEAD_DIM) + EPS)
            w = dn * g
            mean_wt = _dot_exact_r(w * t, gg_m) * (1.0 / HEAD_DIM)
            return r * w - t * (r * r * r) * mean_wt, jnp.sum(dn * t * r, axis=0, keepdims=True)

        dq, dgq = head_norm_bwd(z_ref[:, 0:ATTN_W], gq_ref[...], _merge_heads(dqn_ref))
        dk, dgk = head_norm_bwd(z_ref[:, ATTN_W:2 * ATTN_W], gk_ref[...], _merge_heads(dkn_ref))
        dgq_ref[...] += dgq
        dgk_ref[...] += dgk
        row = lax.broadcasted_iota(jnp.int32, (tm, tm), 0)
        col = lax.broadcasted_iota(jnp.int32, (tm, tm), 1)
        triu = (col >= row).astype(BF16)
        dlf = _dot_exact_l(triu, dc_ref[...]) + carry_ref[...]
        carry_ref[...] = dlf[0:1, :]
        df = dlf * _sigmoid(-_forget_logits(z_ref, bf_ref))
        dbf_ref[...] += jnp.sum(df, axis=0, keepdims=True)
        dz_ref[:, 0:ATTN_W] = dq.astype(BF16)
        dz_ref[:, ATTN_W:2 * ATTN_W] = dk.astype(BF16)
        dz_ref[:, 2 * ATTN_W:3 * ATTN_W] = _merge_heads(dv_ref).astype(BF16)
        tail = jnp.concatenate([df[:, :HEADS], du_ref[...], jnp.zeros((tm, Z_COLS - IN_COLS), F32)], axis=-1)
        dz_ref[:, F_COL0:Z_COLS] = tail.astype(BF16)

    row_spec = lambda w: pl.BlockSpec((tm, w), lambda i: (nb - 1 - i, 0))
    const = lambda shape: pl.BlockSpec(shape, lambda i: (0, 0))
    return _host_pcall(
        body, hosted, lambda: pl.program_id(0) == 0, lambda: pl.program_id(0) == nb - 1, n_in=10, n_out=4, n_scratch=1,
        name="prep_bwd", grid=(nb,),
        in_specs=[row_spec(Z_COLS)] + [pl.BlockSpec((HEADS, tm, HEAD_DIM), lambda i: (0, nb - 1 - i, 0))] * 3
                 + [row_spec(ATTN_W), row_spec(LANES), const((1, ATTN_W)),
                    const((1, ATTN_W)), const((1, LANES)), const((ATTN_W, ATTN_W))],
        out_specs=[row_spec(Z_COLS), const((1, ATTN_W)), const((1, ATTN_W)), const((1, LANES))],
        out_shape=[_sds((s, Z_COLS), BF16), _sds((1, ATTN_W)), _sds((1, ATTN_W)), _sds((1, LANES))],
        scratch_shapes=[pltpu.VMEM((1, LANES), F32)], dims=("arbitrary",),
        operands=(z, dqn, dkn, dv, du, dc, gq, gk, bf, gg))


def _in_norm_bwd(x, g_mix, dh, dx1, hosted=None):
    s = x.shape[0]
    tm = _row_tile(s)

    def body(x_ref, g_ref, dh_ref, dx1_ref, dx_ref, dg_ref):
        i = pl.program_id(0)

        @pl.when(i == 0)
        def _():
            dg_ref[...] = jnp.zeros_like(dg_ref)

        dxn, dg = _rms_bwd(x_ref[...], g_ref[...], dh_ref[...])
        dx_ref[...] = dx1_ref[...] + dxn
        dg_ref[...] += dg

    row = pl.BlockSpec((tm, D_MODEL), lambda i: (i, 0))
    vec = pl.BlockSpec((1, D_MODEL), lambda i: (0, 0))
    nb = s // tm
    return _host_pcall(body, hosted, lambda: pl.program_id(0) == 0, lambda: pl.program_id(0) == nb - 1,
                       n_in=4, n_out=2, n_scratch=0, name="in_norm_bwd", grid=(nb,), in_specs=[row, vec, row, row],
                       out_specs=[row, vec], out_shape=[_sds((s, D_MODEL)), _sds((1, D_MODEL))], scratch_shapes=[],
                       dims=("arbitrary",), operands=(x, g_mix, dh, dx1))


def _adamw_refs(w_ref, g_ref, m_ref, v_ref, d_ref, mo_ref, vo_ref):
    gv = g_ref[...]
    mn = ADAM_B1 * m_ref[...] + (1.0 - ADAM_B1) * gv
    vn = ADAM_B2 * v_ref[...] + (1.0 - ADAM_B2) * (gv * gv)
    m_hat = mn / (1.0 - ADAM_B1 ** ADAM_STEP)
    v_hat = vn / (1.0 - ADAM_B2 ** ADAM_STEP)
    d_ref[...] = -ADAM_LR * (m_hat / (jnp.sqrt(v_hat) + ADAM_EPS) + ADAM_WD * w_ref[...])
    mo_ref[...] = mn
    vo_ref[...] = vn


def _adamw_small(ws, gs, ms, vs):
    n = len(ws)

    def body(*refs):
        ins, outs = refs[:4 * n], refs[4 * n:]
        for i in range(n):
            _adamw_refs(ins[i], ins[n + i], ins[2 * n + i], ins[3 * n + i], *outs[3 * i:3 * i + 3])

    vm = pl.BlockSpec(memory_space=pltpu.VMEM)
    out_shape = [_sds(w.shape) for w in ws for _ in range(3)]
    return _pallas(body, name="adamw_small", in_specs=[vm] * (4 * n), out_specs=[vm] * (3 * n), out_shape=out_shape,
                   compiler_params=pltpu.CompilerParams(vmem_limit_bytes=VMEM_LIMIT))(*ws, *gs, *ms, *vs)


def _adamw(w, g, m, v, *, name):
    r, c = w.shape
    tr = r
    for cand in (256, 176, 128, 64):
        if r > cand and r % cand == 0:
            tr = cand
            break

    def body(w_ref, g_ref, m_ref, v_ref, d_ref, mo_ref, vo_ref):
        _adamw_refs(w_ref, g_ref, m_ref, v_ref, d_ref, mo_ref, vo_ref)

    spec = pl.BlockSpec((tr, c), lambda i: (i, 0))
    return _pcall(body, name=name, grid=(r // tr,), in_specs=[spec] * 4, out_specs=[spec] * 3,
                  out_shape=[_sds((r, c))] * 3, dims=("parallel",))(w, g, m, v)


def _prefetch_call(body, *, name, grid, in_specs, out_specs, out_shape, operands):
    grid_spec = pltpu.PrefetchScalarGridSpec(num_scalar_prefetch=1, grid=grid, in_specs=in_specs, out_specs=out_specs)
    params = pltpu.CompilerParams(dimension_semantics=("parallel",) * len(grid), vmem_limit_bytes=VMEM_LIMIT)
    return _pallas(body, name=name, grid_spec=grid_spec, out_shape=out_shape, compiler_params=params)(*operands)


def _place_cols(buf, shard, place):
    rows, cols = shard.shape
    tr = 256

    def body(place_ref, s_ref, b_ref, o_ref):
        o_ref[...] = s_ref[...]

    grid_spec = pltpu.PrefetchScalarGridSpec(
        num_scalar_prefetch=1, grid=(rows // tr,),
        in_specs=[pl.BlockSpec((tr, cols), lambda i, p: (i, 0)), pl.BlockSpec(memory_space=pltpu.HBM)],
        out_specs=pl.BlockSpec((tr, cols), lambda i, p: (i, p[0])))
    return _pallas(body, name="place_own_cols", grid_spec=grid_spec, out_shape=_sds(buf.shape, buf.dtype),
                   input_output_aliases={2: 0},
                   compiler_params=pltpu.CompilerParams(dimension_semantics=("parallel",),
                                                        vmem_limit_bytes=VMEM_LIMIT))(place, shard, buf)


def _half_rows_tile(hr):
    return hr if hr <= 256 else 176 if hr % 176 == 0 else 256


def _add_half(g, landed, place, *, name):
    def body(place_ref, g_ref, l_ref, o_ref):
        own = g_ref[0] if len(g_ref.shape) == 4 else g_ref[...]
        o_ref[...] = (own + l_ref[...]).astype(BF16)

    if g.ndim == 4:
        _, _, hr, c = g.shape
        tr = _half_rows_tile(hr)
        blk = (1, tr, c)
        return _prefetch_call(
            body, name=name, grid=(N_CHIPS, hr // tr),
            in_specs=[pl.BlockSpec((1,) + blk, lambda j, i, p: (j, p[1], i, 0)), pl.BlockSpec(blk, lambda j, i, p: (j, i, 0))],
            out_specs=pl.BlockSpec(blk, lambda j, i, p: (j, i, 0)), out_shape=_sds(landed.shape, BF16),
            operands=(place, g, landed))
    hr, c = landed.shape
    tr, tc = 256, _tile(c, 2176)
    nb = hr // tr
    return _prefetch_call(
        body, name=name, grid=(nb, c // tc),
        in_specs=[pl.BlockSpec((tr, tc), lambda i, j, p: (p[1] * nb + i, j)), pl.BlockSpec((tr, tc), lambda i, j, p: (i, j))],
        out_specs=pl.BlockSpec((tr, tc), lambda i, j, p: (i, j)), out_shape=_sds(landed.shape, BF16),
        operands=(place, g, landed))


def _sum_chips(chip_sum, lands, place, *, name, tc, window_stride=0):
    _, hr, c = lands.shape
    tr = _half_rows_tile(hr)
    nb = hr // tr
    ncb = c // tc

    def body(place_ref, own_ref, a_ref, b_ref, c_ref, o_ref):
        own = own_ref[0] if len(own_ref.shape) == 3 else own_ref[...]
        o_ref[...] = ((own.astype(F32) + a_ref[0].astype(F32)) + b_ref[0].astype(F32)) + c_ref[0].astype(F32)

    land = lambda k: pl.BlockSpec((1, tr, tc), lambda i, j, p: ((p[0] + k) % N_CHIPS, i, j))
    if chip_sum.ndim == 3:
        own_spec = land(0)
    else:
        stride = window_stride // tc
        own_spec = pl.BlockSpec((tr, tc), lambda i, j, p: (i, p[0] * stride + j))
    return _prefetch_call(
        body, name=name, grid=(nb, ncb), in_specs=[own_spec, land(1), land(2), land(3)],
        out_specs=pl.BlockSpec((tr, tc), lambda i, j, p: (p[1] * nb + i, j)), out_shape=_sds((2 * hr, c)),
        operands=(place, chip_sum, lands, lands, lands))


_HBM = pl.BlockSpec(memory_space=pltpu.HBM)


def _place():
    x, y, c = lax.axis_index("x"), lax.axis_index("y"), lax.axis_index("c")
    chips = [(1 - x, y), (x, 1 - y), (1 - x, 1 - y)]
    return x, y, c, chips


def _rcopy(src, dst, send_sem, recv_sem, to):
    return pltpu.make_async_remote_copy(src_ref=src, dst_ref=dst, send_sem=send_sem, recv_sem=recv_sem,
                                        device_id=to, device_id_type=MESH)


UP_COLS = 2 * D_FF // N_CHIPS
IN_WINDOW = 640
IN_STRIDE = 512


class _Hosted:
    def __init__(self, operands, out_shapes, n_sems, start, finish, aliases=None, local_sems=0):
        self.operands, self.out_shapes, self.n_sems = list(operands), list(out_shapes), n_sems
        self.start, self.finish, self.aliases, self.local_sems = start, finish, dict(aliases or {}), local_sems

    def scratch(self):
        return ([pltpu.SemaphoreType.DMA((self.n_sems,)), pltpu.SemaphoreType.DMA((self.n_sems,))]
                + [pltpu.SemaphoreType.DMA] * self.local_sems)


def _both(a, b):
    na, nao, nas = len(a.operands), len(a.out_shapes), len(a.scratch())

    def start(ins, outs, sems):
        a.start(ins[:na], outs[:nao], sems[:nas])
        b.start(ins[na:], outs[nao:], sems[nas:])

    def finish(ins, outs, sems):
        a.finish(ins[:na], outs[:nao], sems[:nas])
        b.finish(ins[na:], outs[nao:], sems[nas:])

    both = _Hosted(a.operands + b.operands, a.out_shapes + b.out_shapes, 0, start, finish,
                   aliases={**a.aliases, **{na + i: nao + o for i, o in b.aliases.items()}})
    both.scratch = lambda: a.scratch() + b.scratch()
    return both


def _then(a, b):
    nas = len(a.scratch())

    def finish(ins, outs, sems):
        a.finish(ins, outs, sems[:nas])
        b.start(ins, outs, sems[nas:])
        b.finish(ins, outs, sems[nas:])

    chain = _Hosted(a.operands, a.out_shapes, 0, lambda ins, outs, sems: a.start(ins, outs, sems[:nas]), finish,
                    aliases=a.aliases)
    chain.scratch = lambda: a.scratch() + b.scratch()
    return chain


def _run_hosted(hosted, *, name):
    n_in, n_out = len(hosted.operands), len(hosted.out_shapes)

    def body(*refs):
        parts = (refs[:n_in], refs[n_in:n_in + n_out], refs[n_in + n_out:])
        hosted.start(*parts)
        hosted.finish(*parts)

    return _pallas(body, name=name, in_specs=[_HBM] * n_in, out_specs=[_HBM] * n_out, out_shape=hosted.out_shapes,
                   input_output_aliases=hosted.aliases, scratch_shapes=hosted.scratch())(*hosted.operands)


def _host_pcall(core_body, hosted, first, last, *, n_in, n_out, n_scratch, name, grid, in_specs, out_specs, out_shape,
                scratch_shapes, dims, operands):
    if hosted is None:
        outs = _pcall(core_body, name=name, grid=grid, in_specs=in_specs, out_specs=out_specs, out_shape=out_shape,
                      scratch_shapes=scratch_shapes, dims=dims)(*operands)
        return outs, []
    hi, ho = len(hosted.operands), len(hosted.out_shapes)

    def body(*refs):
        a, b = n_in, n_in + hi
        c, d = b + n_out, b + n_out + ho
        e = d + n_scratch
        parts = (refs[a:b], refs[c:d], refs[e:])

        @pl.when(first())
        def _():
            hosted.start(*parts)

        core_body(*refs[:a], *refs[b:c], *refs[d:e])

        @pl.when(last())
        def _():
            hosted.finish(*parts)

    params = pltpu.CompilerParams(dimension_semantics=("arbitrary",) * len(grid), vmem_limit_bytes=VMEM_LIMIT)
    outs = _pallas(body, name=name, grid=grid, in_specs=list(in_specs) + [_HBM] * hi, out_specs=list(out_specs) + [_HBM] * ho,
                   out_shape=list(out_shape) + hosted.out_shapes, scratch_shapes=list(scratch_shapes) + hosted.scratch(),
                   input_output_aliases={n_in + a: n_out + b for a, b in hosted.aliases.items()},
                   compiler_params=params)(*operands, *hosted.operands)
    return outs[:n_out], outs[n_out:]


WHOLE_HALF = (0, 1, 1)


def _band_rows(src, hc, band):
    first, count, of = band
    hr = src.shape[0] // 2
    return pl.ds(hc * hr + first * (hr // of), count * (hr // of))


def _gather_slot(src, out, chip, hc, band=WHOLE_HALF):
    cols = src.shape[1]
    if len(out.shape) == 2:
        return out.at[_band_rows(src, hc, band), pl.ds(pl.multiple_of(chip * cols, LANES), cols)]
    return out.at[chip, _band_rows(src, hc, band), :]


def _gathered_shape(shard, by_cols):
    if by_cols:
        return _sds((shard.shape[0], N_CHIPS * shard.shape[1]), shard.dtype)
    return _sds((N_CHIPS,) + shard.shape, shard.dtype)


def _plan_gather_ici(shards, by_cols, whole=(), bands=None, into=None):
    n = len(shards)
    bands = bands or [WHOLE_HALF] * n
    into = into or [None] * n
    given = [w for w in range(n) if into[w] is not None]
    n_ops = n + len(whole)

    def copies(ins, outs, sems):
        send_sems, recv_sems = sems[0], sems[1]
        x, y, c, chips = _place()
        me = 2 * x + y
        sends, waits = [], []
        for w in range(n + len(whole)):
            for k, (cx, cy) in enumerate(chips):
                sem = (send_sems.at[3 * w + k], recv_sems.at[3 * w + k])
                if w < n:
                    sends.append(_rcopy(ins[w].at[_band_rows(ins[w], c, bands[w]), :],
                                        _gather_slot(ins[w], outs[w], me, c, bands[w]), *sem, (cx, cy, c)))
                    landed = _gather_slot(ins[w], outs[w], 2 * cx + cy, c, bands[w])
                else:
                    sends.append(_rcopy(ins[w], outs[w].at[me], *sem, (cx, cy, c)))
                    landed = outs[w].at[2 * cx + cy]
                waits.append(_rcopy(landed, landed, *sem, (cx, cy, c)))
        return sends, waits

    def start(ins, outs, sems):
        for cp in copies(ins, outs, sems)[0]:
            cp.start()

    def finish(ins, outs, sems):
        sends, waits = copies(ins, outs, sems)
        for cp in waits:
            cp.wait_recv()
        for cp in sends:
            cp.wait_send()

    out_shapes = [_gathered_shape(s, bc) for s, bc in zip(shards, by_cols)] + [_sds((N_CHIPS,) + a.shape, a.dtype) for a in whole]
    return _Hosted(list(shards) + list(whole) + [into[w] for w in given], out_shapes, 3 * n_ops, start, finish,
                   aliases={n_ops + i: w for i, w in enumerate(given)})


def _plan_gather_d2d(bufs, shard_shapes, bands=None):
    n = len(bufs)
    bands = bands or [WHOLE_HALF] * n

    def copies(ins, outs, sems):
        send_sems, recv_sems = sems
        x, y, c, chips = _place()
        sibling = (x, y, 1 - c)
        sends, waits = [], []
        for w in range(n):
            for k, (cx, cy) in enumerate(chips):
                sem = (send_sems.at[3 * w + k], recv_sems.at[3 * w + k])
                landed = _gather_slot(shard_shapes[w], outs[w], 2 * cx + cy, c, bands[w])
                other = _gather_slot(shard_shapes[w], outs[w], 2 * cx + cy, 1 - c, bands[w])
                sends.append(_rcopy(landed, landed, *sem, sibling))
                waits.append(_rcopy(other, other, *sem, sibling))
        return sends, waits

    def start(ins, outs, sems):
        for cp in copies(ins, outs, sems)[0]:
            cp.start()

    def finish(ins, outs, sems):
        sends, waits = copies(ins, outs, sems)
        for cp in waits:
            cp.wait_recv()
        for cp in sends:
            cp.wait_send()

    return _Hosted(bufs, [_sds(b.shape, b.dtype) for b in bufs], 3 * n, start, finish, aliases={w: w for w in range(n)})


def _plan_allgather_first(block):
    def copies(ins, outs, sems):
        send_sems, recv_sems = sems
        x, y, c, chips = _place()
        me = 4 * x + 2 * y + c
        peers = [(x, y, 1 - c)] + [(cx, cy, c) for cx, cy in chips]
        sends = [_rcopy(ins[0], outs[0].at[me], send_sems.at[k], recv_sems.at[k], p) for k, p in enumerate(peers)]
        waits = [_rcopy(outs[0].at[4 * px + 2 * py + pc], outs[0].at[4 * px + 2 * py + pc], send_sems.at[k],
                        recv_sems.at[k], (px, py, pc)) for k, (px, py, pc) in enumerate(peers)]
        return sends, waits

    def start(ins, outs, sems):
        for cp in copies(ins, outs, sems)[0]:
            cp.start()

    def finish(ins, outs, sems):
        sends, waits = copies(ins, outs, sems)
        for cp in waits:
            cp.wait_recv()
        for cp in sends:
            cp.wait_send()

    return _Hosted([block], [_sds((8,) + block.shape)], 4, start, finish)


def _plan_allgather_second(gathered):
    def copies(ins, outs, sems):
        send_sems, recv_sems = sems
        x, y, c, chips = _place()
        sends, waits = [], []
        for k, (cx, cy) in enumerate(chips):
            landed = outs[0].at[4 * cx + 2 * cy + c]
            other = outs[0].at[4 * cx + 2 * cy + 1 - c]
            sends.append(_rcopy(landed, landed, send_sems.at[k], recv_sems.at[k], (x, y, 1 - c)))
            waits.append(_rcopy(other, other, send_sems.at[k], recv_sems.at[k], (x, y, 1 - c)))
        return sends, waits

    def start(ins, outs, sems):
        for cp in copies(ins, outs, sems)[0]:
            cp.start()

    def finish(ins, outs, sems):
        sends, waits = copies(ins, outs, sems)
        for cp in waits:
            cp.wait_recv()
        for cp in sends:
            cp.wait_send()

    return _Hosted([gathered], [_sds(gathered.shape)], 3, start, finish, aliases={0: 0})


def _place_block(gathered, block, device):
    rows, lanes = block.shape

    def body(dev_ref, b_ref, g_ref, o_ref):
        o_ref[0] = b_ref[...]

    grid_spec = pltpu.PrefetchScalarGridSpec(
        num_scalar_prefetch=1, grid=(1,),
        in_specs=[pl.BlockSpec((rows, lanes), lambda i, d: (0, 0)), pl.BlockSpec(memory_space=pltpu.HBM)],
        out_specs=pl.BlockSpec((1, rows, lanes), lambda i, d: (d[0], 0, 0)))
    return _pallas(body, name="place_own_block", grid_spec=grid_spec, out_shape=_sds(gathered.shape),
                   input_output_aliases={2: 0},
                   compiler_params=pltpu.CompilerParams(dimension_semantics=("arbitrary",),
                                                        vmem_limit_bytes=VMEM_LIMIT))(device, block, gathered)


def _sum_devices(gathered):
    _, rows, lanes = gathered.shape
    tr = rows // 2 if rows % 16 == 0 else rows

    def body(g_ref, o_ref):
        acc = g_ref[0]
        for d in range(1, 8):
            acc = acc + g_ref[d]
        o_ref[...] = acc

    return _pcall(body, name="sum_devices", grid=(rows // tr,), in_specs=[pl.BlockSpec((8, tr, lanes), lambda i: (0, i, 0))],
                  out_specs=pl.BlockSpec((tr, lanes), lambda i: (i, 0)), out_shape=_sds((rows, lanes)), dims=("parallel",))(gathered)


def _plan_swap(grads):
    def copies(ins, outs, sems):
        send_sems, recv_sems = sems
        x, y, c, _ = _place()
        cps = []
        for w, g_ref in enumerate(ins):
            if len(g_ref.shape) == 4:
                theirs = g_ref.at[:, 1 - c]
            else:
                hr = g_ref.shape[0] // 2
                theirs = g_ref.at[pl.ds((1 - c) * hr, hr), :]
            cps.append(_rcopy(theirs, outs[w], send_sems.at[w], recv_sems.at[w], (x, y, 1 - c)))
        return cps

    def start(ins, outs, sems):
        for cp in copies(ins, outs, sems):
            cp.start()

    def finish(ins, outs, sems):
        for cp in copies(ins, outs, sems):
            cp.wait()

    out_shapes = [_sds((g.shape[0], g.shape[2], g.shape[3])) if g.ndim == 4 else _sds((g.shape[0] // 2, g.shape[1]))
                  for g in grads]
    return _Hosted(grads, out_shapes, len(grads), start, finish)


def _plan_scatter(chip_sums, windows):
    def copies(ins, outs, sems):
        send_sems, recv_sems = sems
        x, y, c, chips = _place()
        me = 2 * x + y
        sends, waits = [], []
        for w, s_ref in enumerate(ins):
            for k, (cx, cy) in enumerate(chips):
                tgt = 2 * cx + cy
                if windows[w] is not None:
                    stride, width = windows[w]
                    part = s_ref.at[:, pl.ds(pl.multiple_of(tgt * stride, LANES), width)]
                else:
                    part = s_ref.at[tgt]
                sem = (send_sems.at[3 * w + k], recv_sems.at[3 * w + k])
                sends.append(_rcopy(part, outs[w].at[me], *sem, (cx, cy, c)))
                slot = outs[w].at[tgt]
                waits.append(_rcopy(slot, slot, *sem, (cx, cy, c)))
        return sends, waits

    def start(ins, outs, sems):
        for cp in copies(ins, outs, sems)[0]:
            cp.start()

    def finish(ins, outs, sems):
        sends, waits = copies(ins, outs, sems)
        for cp in waits:
            cp.wait_recv()
        for cp in sends:
            cp.wait_send()

    out_shapes = [_sds((N_CHIPS, s.shape[0], win[1]), BF16) if win is not None else _sds(s.shape, BF16)
                  for s, win in zip(chip_sums, windows)]
    return _Hosted(chip_sums, out_shapes, 3 * len(chip_sums), start, finish)


def _plan_join(reds):
    def copies(ins, outs, sems):
        send_sems, recv_sems = sems
        x, y, c, _ = _place()
        sends, waits = [], []
        for w, out in enumerate(outs):
            hr = out.shape[0] // 2
            mine = out.at[pl.ds(c * hr, hr), :]
            theirs = out.at[pl.ds((1 - c) * hr, hr), :]
            sends.append(_rcopy(mine, mine, send_sems.at[w], recv_sems.at[w], (x, y, 1 - c)))
            waits.append(_rcopy(theirs, theirs, send_sems.at[w], recv_sems.at[w], (x, y, 1 - c)))
        return sends, waits

    def start(ins, outs, sems):
        for cp in copies(ins, outs, sems)[0]:
            cp.start()

    def finish(ins, outs, sems):
        sends, waits = copies(ins, outs, sems)
        for cp in waits:
            cp.wait_recv()
        for cp in sends:
            cp.wait_send()

    return _Hosted(reds, [_sds(r.shape) for r in reds], len(reds), start, finish, aliases={w: w for w in range(len(reds))})


def _allreduce_small(v):
    m_per = v.shape[0]

    def body(v_ref, out_ref, all_ref, send_sems, recv_sems, local_sem):
        x, y, c, chips = _place()
        me, sibling = (x, y, c), (x, y, 1 - c)

        def rows(px, py, pc):
            return all_ref.at[pl.ds((4 * px + 2 * py + pc) * m_per, m_per), :]

        def copy(k, block, to, src=None):
            return _rcopy(rows(*block) if src is None else src, rows(*block), send_sems.at[k], recv_sems.at[k], to)

        mine = pltpu.make_async_copy(v_ref, rows(*me), local_sem)
        mine.start()
        first = [copy(0, me, sibling, src=v_ref)]
        first += [copy(1 + k, me, (*chip, c), src=v_ref) for k, chip in enumerate(chips)]
        for cp in first:
            cp.start()
        passed = [copy(4 + k, (*chip, c), sibling) for k, chip in enumerate(chips)]
        for k, chip in enumerate(chips):
            copy(1 + k, (*chip, c), me).wait_recv()
            passed[k].start()
        copy(0, sibling, me).wait_recv()
        for k, chip in enumerate(chips):
            copy(4 + k, (*chip, 1 - c), me).wait_recv()
        for cp in first + passed:
            cp.wait_send()
        mine.wait()
        acc = all_ref[pl.ds(0, m_per), :]
        for d in range(1, 8):
            acc = acc + all_ref[pl.ds(d * m_per, m_per), :]
        out_ref[...] = acc

    vm = pl.BlockSpec(memory_space=pltpu.VMEM)
    return _pallas(body, name="allreduce_small", in_specs=[vm], out_specs=vm, out_shape=_sds((m_per, LANES)),
                          scratch_shapes=[pltpu.VMEM((8 * m_per, LANES), F32), pltpu.SemaphoreType.DMA((7,)),
                                          pltpu.SemaphoreType.DMA((7,)), pltpu.SemaphoreType.DMA],
                          compiler_params=pltpu.CompilerParams(vmem_limit_bytes=VMEM_LIMIT))(v)


def _block_diag(blocks):
    j, g, a, b = blocks.shape
    eye = jnp.eye(g, dtype=bool)[None, :, None, :, None]
    return jnp.where(eye, blocks[:, :, :, None, :], jnp.zeros((), blocks.dtype)).reshape(j, g * a, g * b)


def _diag_blocks(m, a, b):
    j = m.shape[0]
    g = m.shape[1] // a
    t = m.reshape(j, g, a, g, b)
    eye = jnp.eye(g, dtype=bool)[None, :, None, :, None]
    return jnp.sum(jnp.where(eye, t, 0.0), axis=3)


_SMALL = (("g_mix", (1024,)), ("b_f", (8,)), ("g_q", (64,)), ("g_k", (64,)), ("lambda_re", (32, 64)),
          ("lambda_im", (32, 64)), ("log_step", (32,)), ("b_re", (32, 64, 16)), ("b_im", (32, 64, 16)),
          ("c_re", (32, 16, 64)), ("c_im", (32, 16, 64)), ("d_skip", (32, 16)), ("b_glu", (512,)),
          ("g_attn_out", (512,)), ("g_ssm_out", (512,)), ("g_ffn", (1024,)), ("conv_b", (5632,)))


_LATE_SMALL = ("g_mix", "b_f", "g_q", "g_k")
_EARLY_SMALL = tuple(n for n, _ in _SMALL if n not in _LATE_SMALL)


def _packed_rows(n):
    tile = SUBLANES * LANES
    return -(-n // tile) * SUBLANES


def _pack_small(arrs):
    parts = []
    for a in arrs:
        flat = a.reshape(-1)
        rows = _packed_rows(flat.shape[0])
        parts.append(jnp.pad(flat, (0, rows * LANES - flat.shape[0])).reshape(rows, LANES))
    return jnp.concatenate(parts, axis=0)


def _unpack_small(buf, shapes):
    out, r = [], 0
    for shape in shapes:
        n = math.prod(shape)
        out.append(buf[r:r + _packed_rows(n)].reshape(-1)[:n].reshape(shape))
        r += _packed_rows(n)
    return out


def _halves(t):
    return t.reshape(N_CHIPS, 2, t.shape[0] // (2 * N_CHIPS), t.shape[1])


class _MeshComm:
    def __init__(self, args):
        x, y, self.core = lax.axis_index("x"), lax.axis_index("y"), lax.axis_index("c")
        self.chip = 2 * x + y
        self.place = jnp.stack([self.chip, self.core]).astype(jnp.int32)
        self.shards = {n: args[n].astype(BF16) for n in ("w_in", "w_glu", "w_out", "w_up", "w_down")}
        self.conv_w = args["conv_w"]

    def _own(self, stacked, mine):
        return lax.dynamic_update_slice(stacked, mine[None], (self.chip,) + (0,) * mine.ndim)

    def w_in(self):
        sh = self.shards["w_in"]
        (buf,) = _run_hosted(_then(_plan_gather_ici([sh], [False]), _plan_gather_d2d([sh], [sh])), name="gather_w_in")
        whole = self._own(buf, sh).transpose(1, 0, 2).reshape(D_MODEL, IN_COLS)
        return jnp.pad(whole, ((0, 0), (0, Z_COLS - IN_COLS)))

    def gather_first(self):
        self.mid = [self.shards[n] for n in ("w_glu", "w_out", "w_down")]
        return _plan_gather_ici(self.mid + [self.shards["w_up"]], [False, False, False, True], whole=[self.conv_w],
                                bands=[WHOLE_HALF] * 3 + [(0, 1, 4)])

    def gather_second(self, landed):
        self.g_cw = landed[4]
        return _both(_plan_gather_d2d(list(landed[:3]), self.mid),
                     _plan_gather_ici([self.shards["w_up"]], [True], bands=[(1, 3, 4)], into=[landed[3]]))

    def weights(self, gathered):
        g_glu, g_out, g_down = gathered[:3]
        own = self._own
        return (own(g_glu, self.mid[0]).reshape(SSM_W, SSM_W), own(g_out, self.mid[1]).reshape(D_MODEL, D_MODEL),
                own(g_down, self.mid[2]).reshape(D_FF, D_MODEL),
                own(self.g_cw, self.conv_w).transpose(1, 0, 2).reshape(3, 2 * D_FF))

    def gather_third(self, gathered):
        return _plan_gather_d2d([gathered[3]], [self.shards["w_up"]])

    def w_up(self, passed):
        return _place_cols(passed[0], self.shards["w_up"], self.place)

    def swap(self, d_w_down, d_w_up, d_w_glu, d_w_out):
        self.early = [_halves(d_w_down), d_w_up, _halves(d_w_glu), _halves(d_w_out)]
        return _plan_swap(self.early)

    def scatter(self, landed):
        self.early_sums = [_add_half(g, l, self.place, name="add_" + n)
                           for g, l, n in zip(self.early, landed, ("w_down", "w_up", "w_glu", "w_out"))]
        return _plan_scatter(self.early_sums, [None, (UP_COLS, UP_COLS), None, None])

    def swap_in(self, d_w_in):
        self.d_in = d_w_in
        return _plan_swap([d_w_in])

    def scatter_in(self, landed):
        self.sum_in = _add_half(self.d_in, landed[0], self.place, name="add_w_in")
        return _plan_scatter([self.sum_in], [(IN_STRIDE, IN_WINDOW)])

    def small_first(self, block):
        return _plan_allgather_first(block)

    def small_second(self, landed):
        return _plan_allgather_second(landed[0])

    def reduce(self, lands):
        early_lands, (land_in,) = lands
        sum_in = self.sum_in
        es, el = self.early_sums, early_lands
        todo = [(sum_in, land_in, "w_in", LANES, IN_STRIDE), (es[2], el[2], "w_glu", SSM_W, 0),
                (es[3], el[3], "w_out", D_MODEL, 0), (es[1], el[1], "w_up", UP_COLS, UP_COLS),
                (es[0], el[0], "w_down", D_MODEL, 0)]
        reds = _run_hosted(_plan_join([_sum_chips(s, l, self.place, name="sum_" + n, tc=tc, window_stride=st)
                                       for s, l, n, tc, st in todo]), name="join_halves")
        g_big = dict(zip(("w_in", "w_glu", "w_out", "w_up", "w_down"), reds))
        g_big["w_in"] = lax.dynamic_slice_in_dim(reds[0], 2 * self.chip, IN_COLS // N_CHIPS, axis=1)
        return g_big


def _local_step(x, tgt, p, comm):
    s = x.shape[0]
    row = lambda v: v.reshape(1, -1)
    g_mix, g_ffn = row(p["g_mix"]), row(p["g_ffn"])
    g_att, g_ssm, b_glu, conv_b = row(p["g_attn_out"]), row(p["g_ssm_out"]), row(p["b_glu"]), row(p["conv_b"])
    gq = row(jnp.tile(p["g_q"], HEADS))
    gk = row(jnp.tile(p["g_k"], HEADS))
    bf = row(jnp.pad(p["b_f"], (0, LANES - HEADS)))
    gg = jnp.kron(jnp.eye(HEADS, dtype=F32), jnp.ones((HEAD_DIM, HEAD_DIM), F32)).astype(BF16)
    dsk = row(p["d_skip"])

    rep = lambda a: jnp.repeat(a, SSM_GROUP, axis=0)
    lr, li = rep(p["lambda_re"]), rep(p["lambda_im"])
    ls = rep(jnp.broadcast_to(p["log_step"][:, None], (SSM_GROUPS, SSM_STATE)))
    bt_re = p["b_re"].transpose(0, 2, 1).reshape(_PARAM_SHAPE)
    bt_im = p["b_im"].transpose(0, 2, 1).reshape(_PARAM_SHAPE)
    a_re_rep, a_im_rep, bb_re, bb_im = _ssm_params(lr, li, ls, bt_re, bt_im)
    ar = a_re_rep[::SSM_GROUP].reshape(SSM_CHUNKS, 1, CHUNK_S)
    ai = a_im_rep[::SSM_GROUP].reshape(SSM_CHUNKS, 1, CHUNK_S)
    chunked = lambda t: t.reshape(SSM_CHUNKS, SSM_GROUPS // SSM_CHUNKS, SSM_GROUP, SSM_STATE)
    bbr = _block_diag(chunked(bb_re)).astype(BF16)
    bbi = _block_diag(chunked(bb_im)).astype(BF16)
    to_cc = lambda c: _block_diag(chunked(c).transpose(0, 1, 3, 2)).astype(BF16)
    ccr, cci = to_cc(p["c_re"]), to_cc(p["c_im"])

    w_in_r = comm.w_in()
    hb, z = _in_proj(x, g_mix, w_in_r)
    qh, kh, vh, ub, uf, c128 = _attn_prep(z, gq, gk, bf, gg)
    crow = c128[:, :HEADS].T.reshape(HEADS, 1, s)
    (oh, lse), landed = _attn_fwd(qh, kh, vh, crow, comm.gather_first())
    (xr, xi, y), gathered = _ssm_fwd(ub, uf, bbr, bbi, ar, ai, ccr, cci, dsk, comm.gather_second(landed))
    w_glu_b, w_out_b, w_down_b, conv_w_full = comm.weights(gathered)
    (x1, mixb, h2b), passed = _mix_out(y, oh, x, w_glu_b, b_glu, g_att, g_ssm, w_out_b, g_ffn, comm.gather_third(gathered))
    w_up_b = comm.w_up(passed)
    up = _mm(h2b, w_up_b, name="ffn_up", tm=1024, tn=1408, tk=1024)
    act = _conv_act(up, conv_w_full, conv_b)
    dy, dyb, loss_blk = _down_loss(act, w_down_b, x1, tgt)

    d_w_down = _mm(act, dyb, ta=True, name="d_w_down", tm=1408, tn=1024, tk=2048)
    dact = _mm(dyb, w_down_b, tb=True, name="d_act", tm=1024, tn=1408, tk=1024)
    dupb, dcw = _conv_act_bwd(up, dact, conv_w_full, conv_b)
    d_w_up = _mm(h2b, dupb, ta=True, b_parts=2, name="d_w_up", tm=1024, tn=1408, tk=2048)
    dh2 = _mm(dupb, w_up_b, tb=True, a_parts=2, name="d_h2", tm=1024, tn=1024, tk=1408)
    dx1, dx1b, doh, dys, d_w_glu, d_g_ffn, d_g_att, d_g_ssm, d_b_glu = _mix_bwd(
        dy, dh2, x1, g_ffn, w_out_b, y, oh, w_glu_b, b_glu, g_att, g_ssm)
    d_w_out = _mm(mixb, dx1b, ta=True, name="d_w_out", tm=1024, tn=1024, tk=2048)
    (du, dbbr, dbbi, dccr, dcci, dar, dai, dd), swapped = _ssm_bwd(dys, uf, ub, xr, xi, bbr, bbi, ar, ai, ccr, cci, dsk,
                                                                comm.swap(d_w_down, d_w_up, d_w_glu, d_w_out))
    unchunk = lambda t: t.reshape(_PARAM_SHAPE)
    dbb_re = unchunk(_diag_blocks(dbbr, SSM_GROUP, SSM_STATE))
    dbb_im = unchunk(_diag_blocks(dbbi, SSM_GROUP, SSM_STATE))
    first_row = (jnp.arange(_PARAM_SHAPE[0]) % SSM_GROUP == 0)[:, None]
    da_re = jnp.where(first_row, rep(dar.reshape(SSM_GROUPS, SSM_STATE)), 0.0)
    da_im = jnp.where(first_row, rep(dai.reshape(SSM_GROUPS, SSM_STATE)), 0.0)
    expand_t = (jnp.arange(SSM_GROUPS)[:, None] == (jnp.arange(_PARAM_SHAPE[0]) // SSM_GROUP)[None, :]).astype(BF16)
    d_lr, d_li, d_ls, d_bt_re, d_bt_im = _ssm_params_bwd(lr, li, ls, bt_re, bt_im, da_re, da_im, dbb_re, dbb_im, expand_t)
    from_bt = lambda t: t.reshape(SSM_GROUPS, SSM_GROUP, SSM_STATE).transpose(0, 2, 1)
    from_cc = lambda t: _diag_blocks(t, SSM_STATE, SSM_GROUP).transpose(0, 1, 3, 2).reshape(SSM_GROUPS, SSM_GROUP, SSM_STATE)

    small = {
        "lambda_re": d_lr, "lambda_im": d_li, "log_step": d_ls,
        "b_re": from_bt(d_bt_re), "b_im": from_bt(d_bt_im), "c_re": from_cc(dccr), "c_im": from_cc(dcci),
        "d_skip": dd, "b_glu": d_b_glu, "g_attn_out": d_g_att, "g_ssm_out": d_g_ssm, "g_ffn": d_g_ffn,
        "conv_b": dcw[:, 3],
    }
    d_conv_w = dcw[:, 0:3].transpose(1, 0, 2).reshape(3, 2 * D_FF)
    early_small = _pack_small([small[n] for n in _EARLY_SMALL] + [d_conv_w])

    (dqh, dkh, dvh, dcrow), early_lands = _attn_bwd(qh, kh, vh, crow, lse, doh, comm.scatter(swapped))
    dc128 = jnp.pad(dcrow.reshape(HEADS, s).T, ((0, 0), (0, LANES - HEADS)))
    (dzb, d_gq, d_gk, d_bf), small_landed = _prep_bwd(z, dqh, dkh, dvh, du, dc128, gq, gk, bf, gg,
                                                      comm.small_first(early_small))
    d_w_in_r, small_gathered = _mm(hb, dzb, ta=True, name="d_w_in", tm=512, tn=Z_COLS, tk=2048, carry=True,
                                   hosted=comm.small_second(small_landed))
    dh, swapped_in = _mm(dzb, w_in_r, tb=True, name="d_h", tm=1024, tn=1024, tk=Z_COLS, carry=True,
                         hosted=comm.swap_in(d_w_in_r))
    (dx, d_g_mix), land_in = _in_norm_bwd(x, g_mix, dh, dx1, comm.scatter_in(swapped_in))
    small.update({"g_mix": d_g_mix, "b_f": d_bf[0, :HEADS], "g_q": d_gq.reshape(HEADS, HEAD_DIM).sum(0),
                  "g_k": d_gk.reshape(HEADS, HEAD_DIM).sum(0)})
    big = {"w_in": d_w_in_r, "w_glu": d_w_glu, "w_out": d_w_out, "w_up": d_w_up, "w_down": d_w_down}
    return loss_blk[0, 0], dx, big, small, d_conv_w, (early_lands, land_in, small_gathered, early_small)


def kernel(x, g_mix, w_in, b_f, g_q, g_k, lambda_re, lambda_im, log_step, b_re, b_im, c_re, c_im, d_skip, w_glu, b_glu, g_attn_out, g_ssm_out, w_out, g_ffn, w_up, conv_w, conv_b, w_down, loss_target, m_g_mix, m_w_in, m_b_f, m_g_q, m_g_k, m_lambda_re, m_lambda_im, m_log_step, m_b_re, m_b_im, m_c_re, m_c_im, m_d_skip, m_w_glu, m_b_glu, m_g_attn_out, m_g_ssm_out, m_w_out, m_g_ffn, m_w_up, m_conv_w, m_conv_b, m_w_down, v_g_mix, v_w_in, v_b_f, v_g_q, v_g_k, v_lambda_re, v_lambda_im, v_log_step, v_b_re, v_b_im, v_c_re, v_c_im, v_d_skip, v_w_glu, v_b_glu, v_g_attn_out, v_g_ssm_out, v_w_out, v_g_ffn, v_w_up, v_conv_w, v_conv_b, v_w_down):
    args = dict(locals())
    order = ["g_mix", "w_in", "b_f", "g_q", "g_k", "lambda_re", "lambda_im", "log_step", "b_re", "b_im", "c_re", "c_im",
             "d_skip", "w_glu", "b_glu", "g_attn_out", "g_ssm_out", "w_out", "g_ffn", "w_up", "conv_w", "conv_b", "w_down"]
    comm = _MeshComm(args)
    chip = comm.chip
    loss_part, dx, big, small, d_conv_w, lands = _local_step(x[0], loss_target[0], args, comm)

    g_big = comm.reduce(lands[:2])

    shapes = dict(_SMALL)
    small_names = [n for n, _ in _SMALL]
    device = (2 * chip + comm.core).reshape(1).astype(jnp.int32)
    early = _unpack_small(_sum_devices(_place_block(lands[2][0], lands[3], device)),
                          [shapes[n] for n in _EARLY_SMALL] + [(3, 2 * D_FF)])
    late = _unpack_small(_allreduce_small(_pack_small([small[n] for n in _LATE_SMALL] + [loss_part])),
                         [shapes[n] for n in _LATE_SMALL] + [()])
    loss = late[-1]
    g_conv_w = lax.dynamic_slice_in_dim(early[-1], chip * (2 * D_FF // N_CHIPS), 2 * D_FF // N_CHIPS, axis=1)
    g_small = {**dict(zip(_EARLY_SMALL, early[:-1])), **dict(zip(_LATE_SMALL, late[:-1]))}

    grad, delta, new_m, new_v = {}, {}, {}, {}
    for n in ("w_in", "w_glu", "w_out", "w_up", "w_down"):
        grad[n] = g_big[n]
        delta[n], new_m[n], new_v[n] = _adamw(args[n], g_big[n], args["m_" + n], args["v_" + n], name="adamw_" + n)
    grad["conv_w"] = g_conv_w
    delta["conv_w"], new_m["conv_w"], new_v["conv_w"] = _adamw(conv_w, g_conv_w, m_conv_w, v_conv_w, name="adamw_conv_w")
    stepped = _adamw_small([args[n] for n in small_names], [g_small[n] for n in small_names],
                           [args["m_" + n] for n in small_names], [args["v_" + n] for n in small_names])
    for i, n in enumerate(small_names):
        grad[n] = g_small[n]
        delta[n], new_m[n], new_v[n] = stepped[3 * i:3 * i + 3]

    return (loss, dx[None], *[grad[n] for n in order], *[delta[n] for n in order], *[new_m[n] for n in order],
            *[new_v[n] for n in order])
```

```python
import math

import jax
import jax.numpy as jnp
from jax import lax
from jax.experimental import pallas as pl
from jax.experimental.pallas import tpu as pltpu

F32 = jnp.float32
BF16 = jnp.bfloat16

D_MODEL = 1024
HEADS = 8
HEAD_DIM = 64
ATTN_W = 512
SSM_W = 512
SSM_GROUPS = 32
SSM_GROUP = 16
SSM_STATE = 64
N_STATE = SSM_GROUPS * SSM_STATE
D_FF = 2816
IN_COLS = 2056
Z_COLS = 2176
F_COL0 = 1536
U_COL0 = 1544
EPS = 1e-6
NEG_INF = -1e30
N_CHIPS = 4
LANES = 128
SUBLANES = 8
SSM_CHUNKS = 2
CHUNK_U = SSM_W // SSM_CHUNKS
CHUNK_S = N_STATE // SSM_CHUNKS
HEADS_PER_STEP = 4
STRIP = 128
N_STRIPS = D_FF // STRIP

ADAM_LR = 0.001
ADAM_B1 = 0.9
ADAM_B2 = 0.999
ADAM_EPS = 1e-08
ADAM_WD = 0.01
ADAM_STEP = 10

VMEM_LIMIT = 56 * 1024 * 1024
MESH = pl.DeviceIdType.MESH


def _pallas(body, **kw):
    return pl.pallas_call(body, **kw)


def _pcall(body, *, name, out_shape, in_specs, out_specs, grid=(), scratch_shapes=(), dims=None):
    params = pltpu.CompilerParams(dimension_semantics=dims, vmem_limit_bytes=VMEM_LIMIT)
    return _pallas(body, name=name, grid=grid, in_specs=in_specs, out_specs=out_specs,
                   out_shape=out_shape, scratch_shapes=scratch_shapes, compiler_params=params)


def _sds(shape, dtype=F32):
    return jax.ShapeDtypeStruct(shape, dtype)


def _dot(a, b):
    return jnp.dot(a, b, preferred_element_type=F32)


def _dot_nt(a, b):
    return lax.dot_general(a, b, (((1,), (1,)), ((), ())), preferred_element_type=F32)


def _dot_tn(a, b):
    return lax.dot_general(a, b, (((0,), (0,)), ((), ())), preferred_element_type=F32)


def _split3(x):
    hi = x.astype(BF16)
    r = x - hi.astype(F32)
    mid = r.astype(BF16)
    lo = (r - mid.astype(F32)).astype(BF16)
    return hi, mid, lo


def _dot_exact_r(x, m01):
    hi, mid, lo = _split3(x)
    return _dot(hi, m01) + _dot(mid, m01) + _dot(lo, m01)


def _dot_exact_l(m01, x):
    hi, mid, lo = _split3(x)
    return _dot(m01, hi) + _dot(m01, mid) + _dot(m01, lo)


def _sigmoid(x):
    return 1.0 / (1.0 + jnp.exp(-x))


def _rms(x, g):
    r = lax.rsqrt(jnp.mean(x * x, axis=-1, keepdims=True) + EPS)
    return x * r * g


def _rms_bwd(x, g, dy):
    r = lax.rsqrt(jnp.mean(x * x, axis=-1, keepdims=True) + EPS)
    w = dy * g
    dx = r * w - x * (r * r * r) * jnp.mean(w * x, axis=-1, keepdims=True)
    dg = jnp.sum(dy * x * r, axis=0, keepdims=True)
    return dx, dg


_GELU_K = math.sqrt(2.0 / math.pi)
_GELU_C = 0.044715


def _gelu(y):
    return y * (0.5 * (1.0 + jnp.tanh(_GELU_K * (y + _GELU_C * (y * y * y)))))


def _gelu_grad(y):
    t = jnp.tanh(_GELU_K * (y + _GELU_C * (y * y * y)))
    return 0.5 * (1.0 + t) + 0.5 * y * (1.0 - t * t) * (_GELU_K * (1.0 + 3.0 * _GELU_C * y * y))


def _tile(n, pref):
    if n <= pref:
        return n
    divs = [t for t in range(LANES, n + 1, LANES) if n % t == 0]
    below = [t for t in divs if t <= pref]
    if below and 2 * below[-1] >= pref:
        return below[-1]
    above = [t for t in divs if t > pref]
    return above[0] if above else n


def _row_tile(s):
    return min(256, s)


def _mm(a, b, *, name, tm, tn, tk, ta=False, tb=False, a_parts=1, b_parts=1, carry=False, hosted=None):
    if a_parts > 1:
        m, kk = a.shape[1], a.shape[2] * a_parts
    elif ta:
        kk, m = a.shape
    else:
        m, kk = a.shape
    if b_parts > 1:
        n = b.shape[2] * b_parts
    else:
        n = b.shape[0] if tb else b.shape[1]
    tm, tn, tk = _tile(m, tm), _tile(n // b_parts, tn), _tile(kk // a_parts, tk)
    k_per, n_per = kk // a_parts // tk, n // b_parts // tn

    def body(a_ref, b_ref, o_ref):
        k = pl.program_id(2)
        if ta:
            part = _dot_tn(a_ref[...], b_ref[...])
        elif tb:
            part = _dot_nt(a_ref[...], b_ref[...])
        else:
            part = _dot(a_ref[...], b_ref[...])

        @pl.when(k == 0)
        def _():
            o_ref[...] = part

        @pl.when(k > 0)
        def _():
            o_ref[...] += part

    if a_parts > 1:
        a_spec = pl.BlockSpec((None, tm, tk), lambda i, j, k: (k // k_per, i, k % k_per))
    else:
        a_spec = pl.BlockSpec((tk, tm), lambda i, j, k: (k, i)) if ta else pl.BlockSpec((tm, tk), lambda i, j, k: (i, k))
    if b_parts > 1:
        b_spec = pl.BlockSpec((None, tk, tn), lambda i, j, k: (j // n_per, k, j % n_per))
    else:
        b_spec = pl.BlockSpec((tn, tk), lambda i, j, k: (j, k)) if tb else pl.BlockSpec((tk, tn), lambda i, j, k: (k, j))
    grid = (m // tm, n // tn, kk // tk)
    at = lambda step: (lambda: jnp.logical_and(jnp.logical_and(pl.program_id(0) == step[0], pl.program_id(1) == step[1]),
                                               pl.program_id(2) == step[2]))
    (out,), carried = _host_pcall(body, hosted, at((0, 0, 0)), at(tuple(g - 1 for g in grid)), n_in=2, n_out=1, n_scratch=0,
                                  name=name, grid=grid, in_specs=[a_spec, b_spec],
                                  out_specs=[pl.BlockSpec((tm, tn), lambda i, j, k: (i, j))], out_shape=[_sds((m, n))],
                                  scratch_shapes=[], dims=("parallel", "parallel", "arbitrary"), operands=(a, b))
    return (out, carried) if carry else out


def _in_proj(x, g_mix, w_in_r):
    s = x.shape[0]
    tm = _row_tile(s)

    def body(x_ref, g_ref, w_ref, h_ref, z_ref):
        h = _rms(x_ref[...], g_ref[...]).astype(BF16)
        h_ref[...] = h
        z_ref[...] = _dot(h, w_ref[...])

    return _pcall(body, name="in_proj", grid=(s // tm,),
                  in_specs=[pl.BlockSpec((tm, D_MODEL), lambda i: (i, 0)), pl.BlockSpec((1, D_MODEL), lambda i: (0, 0)),
                            pl.BlockSpec((D_MODEL, Z_COLS), lambda i: (0, 0))],
                  out_specs=[pl.BlockSpec((tm, D_MODEL), lambda i: (i, 0)), pl.BlockSpec((tm, Z_COLS), lambda i: (i, 0))],
                  out_shape=[_sds((s, D_MODEL), BF16), _sds((s, Z_COLS))], dims=("parallel",))(x, g_mix, w_in_r)


def _split_heads(ref, val):
    for h in range(HEADS):
        ref[h] = val[:, h * HEAD_DIM:(h + 1) * HEAD_DIM].astype(ref.dtype)


def _merge_heads(ref):
    return jnp.concatenate([ref[h].astype(F32) for h in range(HEADS)], axis=-1)


def _forget_logits(z_ref, bf_ref):
    fl = z_ref[:, F_COL0:F_COL0 + LANES] + bf_ref[...]
    return jnp.where(lax.broadcasted_iota(jnp.int32, fl.shape, 1) < HEADS, fl, 0.0)


def _attn_prep(z, gq, gk, bf, gg):
    s = z.shape[0]
    tm = _row_tile(s)

    def body(z_ref, gq_ref, gk_ref, bf_ref, gg_ref, qn_ref, kn_ref, vb_ref, ub_ref, uf_ref, c_ref, carry_ref):
        i = pl.program_id(0)

        @pl.when(i == 0)
        def _():
            carry_ref[...] = jnp.zeros_like(carry_ref)

        gg_m = gg_ref[...]

        def head_norm(t, g):
            ssq = _dot_exact_r(t * t, gg_m)
            return t * lax.rsqrt(ssq * (1.0 / HEAD_DIM) + EPS) * g

        _split_heads(qn_ref, head_norm(z_ref[:, 0:ATTN_W], gq_ref[...]))
        _split_heads(kn_ref, head_norm(z_ref[:, ATTN_W:2 * ATTN_W], gk_ref[...]))
        _split_heads(vb_ref, z_ref[:, 2 * ATTN_W:3 * ATTN_W])
        u = z_ref[:, U_COL0:U_COL0 + SSM_W]
        uf_ref[...] = u
        ub_ref[...] = u.astype(BF16)
        fl = _forget_logits(z_ref, bf_ref)
        lf = jnp.minimum(fl, 0.0) - jnp.log1p(jnp.exp(-jnp.abs(fl)))
        row = lax.broadcasted_iota(jnp.int32, (tm, tm), 0)
        col = lax.broadcasted_iota(jnp.int32, (tm, tm), 1)
        tri = (row >= col).astype(BF16)
        c = _dot_exact_l(tri, lf) + carry_ref[...]
        c_ref[...] = c
        carry_ref[...] = c[tm - 1:tm, :]

    row_spec = lambda w: pl.BlockSpec((tm, w), lambda i: (i, 0))
    const = lambda shape: pl.BlockSpec(shape, lambda i: (0, 0))
    heads = pl.BlockSpec((HEADS, tm, HEAD_DIM), lambda i: (0, i, 0))
    return _pcall(body, name="attn_prep", grid=(s // tm,),
                  in_specs=[row_spec(Z_COLS), const((1, ATTN_W)), const((1, ATTN_W)), const((1, LANES)), const((ATTN_W, ATTN_W))],
                  out_specs=[heads] * 3 + [row_spec(SSM_W), row_spec(SSM_W), row_spec(LANES)],
                  out_shape=[_sds((HEADS, s, HEAD_DIM), BF16)] * 3 + [_sds((s, SSM_W), BF16), _sds((s, SSM_W)), _sds((s, LANES))],
                  scratch_shapes=[pltpu.VMEM((1, LANES), F32)], dims=("arbitrary",))(z, gq, gk, bf, gg)


def _attn_fwd(qh, kh, vh, crow, hosted=None):
    _, s, _ = qh.shape
    tq = _row_tile(s)
    scale = HEAD_DIM ** -0.5

    hp = HEADS
    nq = s // tq
    fold = lambda t, op: op(t[:, :tq // 2], t[:, tq // 2:])

    def body(q_ref, k_ref, v_ref, c_ref, o_ref, lse_ref, s_s):
        i = pl.program_id(1)

        def first(j, ms, diagonal):
            off = pl.multiple_of(j * tq, tq)
            out = []
            for hh in range(hp):
                sc = _dot_nt(q_ref[hh], k_ref[hh, pl.ds(off, tq), :]) * scale - c_ref[hh, :, pl.ds(off, tq)]
                if diagonal:
                    causal = lax.broadcasted_iota(jnp.int32, (tq, tq), 1) <= lax.broadcasted_iota(jnp.int32, (tq, tq), 0)
                    sc = jnp.where(causal, sc, NEG_INF)
                s_s[hh, j] = sc
                out.append(jnp.maximum(ms[hh], fold(sc, jnp.maximum)))
            return tuple(out)

        ms = lax.fori_loop(0, i, lambda j, c: first(j, c, False), (jnp.full((tq, tq // 2), NEG_INF, F32),) * hp)
        ms = [jnp.max(t, axis=-1, keepdims=True) for t in first(i, ms, True)]

        def second(j, carry):
            rows = pl.ds(pl.multiple_of(j * tq, tq), tq)
            out = []
            for hh in range(hp):
                ls, acc = carry[hh]
                p = jnp.exp(s_s[hh, j] - ms[hh])
                out.append((ls + fold(p, jnp.add), acc + _dot(p.astype(BF16), v_ref[hh, rows, :])))
            return tuple(out)

        zero = (jnp.zeros((tq, tq // 2), F32), jnp.zeros((tq, HEAD_DIM), F32))
        for hh, (ls, acc) in enumerate(lax.fori_loop(0, i + 1, second, (zero,) * hp)):
            l = jnp.sum(ls, axis=-1, keepdims=True)
            o_ref[hh] = acc / l
            lse_ref[hh] = ms[hh] + jnp.log(l)

    blk = pl.BlockSpec((hp, tq, HEAD_DIM), lambda h, i: (h, i, 0))
    full = pl.BlockSpec((hp, s, HEAD_DIM), lambda h, i: (h, 0, 0))
    nh = HEADS // hp
    first = lambda: jnp.logical_and(pl.program_id(0) == 0, pl.program_id(1) == 0)
    last = lambda: jnp.logical_and(pl.program_id(0) == nh - 1, pl.program_id(1) == nq - 1)
    return _host_pcall(body, hosted, first, last, n_in=4, n_out=2, n_scratch=1, name="attn_fwd", grid=(nh, nq),
                       in_specs=[blk, full, full, pl.BlockSpec((hp, 1, s), lambda h, i: (h, 0, 0))],
                       out_specs=[blk, pl.BlockSpec((hp, tq, 1), lambda h, i: (h, i, 0))],
                       out_shape=[_sds((HEADS, s, HEAD_DIM)), _sds((HEADS, s, 1))],
                       scratch_shapes=[pltpu.VMEM((hp, nq, tq, tq), F32)],
                       dims=("parallel", "parallel"), operands=(qh, kh, vh, crow))


def _ssm_param_fn(lr, li, ls, br, bi):
    step = jnp.exp(ls)
    er = jnp.exp(lr * step)
    ab_re = er * jnp.cos(li * step)
    ab_im = er * jnp.sin(li * step)
    num_re = ab_re - 1.0
    num_im = ab_im
    den = lr * lr + li * li
    f_re = (num_re * lr + num_im * li) / den
    f_im = (num_im * lr - num_re * li) / den
    bb_re = f_re * br - f_im * bi
    bb_im = f_re * bi + f_im * br
    return ab_re, ab_im, bb_re, bb_im


_PARAM_SHAPE = (SSM_GROUPS * SSM_GROUP, SSM_STATE)


def _ssm_params(lr, li, ls, br, bi):
    def body(lr_ref, li_ref, ls_ref, br_ref, bi_ref, ar_ref, ai_ref, bbr_ref, bbi_ref):
        ar, ai, bbr, bbi = _ssm_param_fn(lr_ref[...], li_ref[...], ls_ref[...], br_ref[...], bi_ref[...])
        ar_ref[...] = ar
        ai_ref[...] = ai
        bbr_ref[...] = bbr
        bbi_ref[...] = bbi

    spec = pl.BlockSpec(_PARAM_SHAPE, lambda: (0, 0))
    return _pcall(body, name="ssm_params", in_specs=[spec] * 5, out_specs=[spec] * 4,
                  out_shape=[_sds(_PARAM_SHAPE)] * 4)(lr, li, ls, br, bi)


def _ssm_params_bwd(lr, li, ls, br, bi, dar, dai, dbbr, dbbi, expand_t):
    def body(lr_ref, li_ref, ls_ref, br_ref, bi_ref, dar_ref, dai_ref, dbbr_ref, dbbi_ref, et_ref,
             dlr_ref, dli_ref, dls_ref, dbr_ref, dbi_ref):
        _, vjp = jax.vjp(_ssm_param_fn, lr_ref[...], li_ref[...], ls_ref[...], br_ref[...], bi_ref[...])
        dlr, dli, dls, dbr, dbi = vjp((dar_ref[...], dai_ref[...], dbbr_ref[...], dbbi_ref[...]))
        et = et_ref[...]
        dlr_ref[...] = _dot_exact_l(et, dlr)
        dli_ref[...] = _dot_exact_l(et, dli)
        dls_ref[...] = jnp.sum(_dot_exact_l(et, dls), axis=-1, keepdims=True)
        dbr_ref[...] = dbr
        dbi_ref[...] = dbi

    spec = pl.BlockSpec(_PARAM_SHAPE, lambda: (0, 0))
    gspec = pl.BlockSpec((SSM_GROUPS, SSM_STATE), lambda: (0, 0))
    return _pcall(body, name="ssm_params_bwd",
                  in_specs=[spec] * 9 + [pl.BlockSpec((SSM_GROUPS, _PARAM_SHAPE[0]), lambda: (0, 0))],
                  out_specs=[gspec, gspec, pl.BlockSpec((SSM_GROUPS, 1), lambda: (0, 0)), spec, spec],
                  out_shape=[_sds((SSM_GROUPS, SSM_STATE))] * 2 + [_sds((SSM_GROUPS, 1))] + [_sds(_PARAM_SHAPE)] * 2,
                  )(lr, li, ls, br, bi, dar, dai, dbbr, dbbi, expand_t)


def _cmul(ar, ai, br, bi):
    return ar * br - ai * bi, ar * bi + ai * br


def _scan_consts(ar, ai, width, reverse):
    row = lax.broadcasted_iota(jnp.int32, (SUBLANES, width), 0)
    pw = [(ar, ai)]
    for _ in range(SUBLANES - 1):
        pw.append(_cmul(pw[-1][0], pw[-1][1], ar, ai))
    steps = []
    for d in (1, 2, 4):
        keep = (row < SUBLANES - d) if reverse else (row >= d)
        steps.append((d, jnp.where(keep, pw[d - 1][0], 0.0), jnp.where(keep, pw[d - 1][1], 0.0)))
    pr = jnp.zeros((SUBLANES, width), F32)
    pi = jnp.zeros((SUBLANES, width), F32)
    for r in range(SUBLANES):
        e = (SUBLANES - r) if reverse else (r + 1)
        pr = jnp.where(row == r, pw[e - 1][0], pr)
        pi = jnp.where(row == r, pw[e - 1][1], pi)
    return steps, pr, pi


def _scan_tile(xr, xi, cr, ci, consts, reverse):
    steps, pr, pi = consts
    for d, mr, mi in steps:
        sh = (SUBLANES - d) if reverse else d
        sr = pltpu.roll(xr, sh, 0)
        si = pltpu.roll(xi, sh, 0)
        xr, xi = xr + mr * sr - mi * si, xi + mr * si + mi * sr
    return xr + pr * cr - pi * ci, xi + pr * ci + pi * cr


def _ssm_fwd(ub, uf, bbr, bbi, ar, ai, ccr, cci, dsk, hosted=None):
    s = ub.shape[0]
    tm = _row_tile(s)
    nt = tm // SUBLANES

    def body(ub_ref, u_ref, bbr_ref, bbi_ref, ar_ref, ai_ref, ccr_ref, cci_ref, dsk_ref,
             xr_ref, xi_ref, y_ref, cr_s, ci_s):
        i = pl.program_id(1)

        @pl.when(i == 0)
        def _():
            cr_s[...] = jnp.zeros_like(cr_s)
            ci_s[...] = jnp.zeros_like(ci_s)

        u_b = ub_ref[...]
        xr_ref[...] = _dot(u_b, bbr_ref[0])
        xi_ref[...] = _dot(u_b, bbi_ref[0])
        consts = _scan_consts(ar_ref[0], ai_ref[0], CHUNK_S, False)

        def tile(k, carry):
            cr, ci = carry
            sl = pl.ds(pl.multiple_of(k * SUBLANES, SUBLANES), SUBLANES)
            xr, xi = _scan_tile(xr_ref[sl, :], xi_ref[sl, :], cr, ci, consts, False)
            xr_ref[sl, :] = xr
            xi_ref[sl, :] = xi
            return xr[SUBLANES - 1:SUBLANES, :], xi[SUBLANES - 1:SUBLANES, :]

        cr, ci = lax.fori_loop(0, nt, tile, (cr_s[...], ci_s[...]))
        cr_s[...] = cr
        ci_s[...] = ci
        y_ref[...] = (_dot(xr_ref[...].astype(BF16), ccr_ref[0]) - _dot(xi_ref[...].astype(BF16), cci_ref[0])
                      + dsk_ref[...] * u_ref[...])

    wspec = lambda a, b: pl.BlockSpec((1, a, b), lambda j, i: (j, 0, 0))
    nb = s // tm
    first = lambda: jnp.logical_and(pl.program_id(0) == 0, pl.program_id(1) == 0)
    last = lambda: jnp.logical_and(pl.program_id(0) == SSM_CHUNKS - 1, pl.program_id(1) == nb - 1)
    return _host_pcall(
        body, hosted, first, last, n_in=9, n_out=3, n_scratch=2, name="ssm_fwd", grid=(SSM_CHUNKS, nb),
        in_specs=[pl.BlockSpec((tm, CHUNK_U), lambda j, i: (i, j)),
                  pl.BlockSpec((tm, CHUNK_U), lambda j, i: (i, j)),
                  wspec(CHUNK_U, CHUNK_S), wspec(CHUNK_U, CHUNK_S), wspec(1, CHUNK_S), wspec(1, CHUNK_S),
                  wspec(CHUNK_S, CHUNK_U), wspec(CHUNK_S, CHUNK_U),
                  pl.BlockSpec((1, CHUNK_U), lambda j, i: (0, j))],
        out_specs=[pl.BlockSpec((tm, CHUNK_S), lambda j, i: (i, j)), pl.BlockSpec((tm, CHUNK_S), lambda j, i: (i, j)),
                   pl.BlockSpec((tm, CHUNK_U), lambda j, i: (i, j))],
        out_shape=[_sds((s, N_STATE)), _sds((s, N_STATE)), _sds((s, SSM_W))],
        scratch_shapes=[pltpu.VMEM((1, CHUNK_S), F32)] * 2,
        dims=("parallel", "arbitrary"), operands=(ub, uf, bbr, bbi, ar, ai, ccr, cci, dsk))


def _ssm_glu(y, w_glu, b_glu):
    ge = _gelu(y)
    sg = _sigmoid(_dot(ge.astype(BF16), w_glu) + b_glu)
    return ge, sg


def _mix_out(y, att, x, w_glu, b_glu, g_att, g_ssm, w_out, g_ffn, hosted=None):
    s = x.shape[0]
    tm = _row_tile(s)

    def body(y_ref, att_ref, x_ref, wg_ref, bg_ref, ga_ref, gs_ref, wo_ref, gf_ref, x1_ref, mix_ref, h2_ref):
        ge, sg = _ssm_glu(y_ref[...], wg_ref[...], bg_ref[...])
        ms = _rms(ge * sg, gs_ref[...]).astype(BF16)
        ma = _rms(_merge_heads(att_ref), ga_ref[...]).astype(BF16)
        mix_ref[:, 0:ATTN_W] = ma
        mix_ref[:, ATTN_W:D_MODEL] = ms
        x1 = x_ref[...] + (_dot(ma, wo_ref[0:ATTN_W, :]) + _dot(ms, wo_ref[ATTN_W:D_MODEL, :]))
        x1_ref[...] = x1
        h2_ref[...] = _rms(x1, gf_ref[...]).astype(BF16)

    row = lambda w: pl.BlockSpec((tm, w), lambda i: (i, 0))
    const = lambda a, b: pl.BlockSpec((a, b), lambda i: (0, 0))
    nb = s // tm
    return _host_pcall(body, hosted, lambda: pl.program_id(0) == 0, lambda: pl.program_id(0) == nb - 1,
                       n_in=9, n_out=3, n_scratch=0, name="mix_out", grid=(nb,),
                       in_specs=[row(SSM_W), pl.BlockSpec((HEADS, tm, HEAD_DIM), lambda i: (0, i, 0)), row(D_MODEL),
                                 const(SSM_W, SSM_W), const(1, SSM_W),
                                 const(1, ATTN_W), const(1, SSM_W), const(D_MODEL, D_MODEL), const(1, D_MODEL)],
                       out_specs=[row(D_MODEL)] * 3,
                       out_shape=[_sds((s, D_MODEL)), _sds((s, D_MODEL), BF16), _sds((s, D_MODEL), BF16)],
                       scratch_shapes=[], dims=("parallel",), operands=(y, att, x, w_glu, b_glu, g_att, g_ssm, w_out, g_ffn))


CONV_CHUNK = 64


def _conv_rows(pad_ref, w, b, r0, n):
    y = b + pad_ref[pl.ds(r0 + SUBLANES - 2, n), :] * w[0:1, :]
    y = y + pad_ref[pl.ds(r0 + SUBLANES - 1, n), :] * w[1:2, :]
    return y + pad_ref[pl.ds(r0 + SUBLANES, n), :] * w[2:3, :]


def _fill_front_pad(pad_ref, strip_ref, s):
    pad_ref[0:SUBLANES, :] = jnp.zeros((SUBLANES, STRIP), F32)
    for r0 in range(0, s, CONV_CHUNK):
        pad_ref[pl.ds(SUBLANES + r0, CONV_CHUNK), :] = strip_ref[pl.ds(r0, CONV_CHUNK), :]


def _conv_act(up, conv_w, conv_b):
    s = up.shape[0]

    def body(ug_ref, uv_ref, wg_ref, wv_ref, bg_ref, bv_ref, act_ref, pg_ref, pv_ref):
        _fill_front_pad(pg_ref, ug_ref, s)
        _fill_front_pad(pv_ref, uv_ref, s)
        wg, wv, bg, bv = wg_ref[...], wv_ref[...], bg_ref[...], bv_ref[...]
        for r0 in range(0, s, CONV_CHUNK):
            hg = _conv_rows(pg_ref, wg, bg, r0, CONV_CHUNK)
            hv = _conv_rows(pv_ref, wv, bv, r0, CONV_CHUNK)
            act_ref[pl.ds(r0, CONV_CHUNK), :] = (hg * _sigmoid(hg) * hv).astype(BF16)

    strip = lambda off: pl.BlockSpec((s, STRIP), lambda j: (0, j + off))
    wsp = lambda off: pl.BlockSpec((3, STRIP), lambda j: (0, j + off))
    bsp = lambda off: pl.BlockSpec((1, STRIP), lambda j: (0, j + off))
    return _pcall(body, name="conv_act", grid=(N_STRIPS,),
                  in_specs=[strip(0), strip(N_STRIPS), wsp(0), wsp(N_STRIPS), bsp(0), bsp(N_STRIPS)],
                  out_specs=pl.BlockSpec((s, STRIP), lambda j: (0, j)), out_shape=_sds((s, D_FF), BF16),
                  scratch_shapes=[pltpu.VMEM((s + SUBLANES, STRIP), F32)] * 2,
                  dims=("parallel",))(up, up, conv_w, conv_w, conv_b, conv_b)


def _down_loss(act, w_down, x1, tgt):
    s = x1.shape[0]
    tm = _row_tile(s)

    def body(a_ref, w_ref, x1_ref, t_ref, dy_ref, dyb_ref, loss_ref):
        i = pl.program_id(0)

        @pl.when(i == 0)
        def _():
            loss_ref[...] = jnp.zeros_like(loss_ref)

        diff = x1_ref[...] + _dot(a_ref[...], w_ref[...]) - t_ref[...]
        dy = diff * (1.0 / D_MODEL)
        dy_ref[...] = dy
        dyb_ref[...] = dy.astype(BF16)
        loss_ref[...] += 0.5 * jnp.sum(diff * dy)

    row = lambda w: pl.BlockSpec((tm, w), lambda i: (i, 0))
    return _pcall(body, name="down_loss", grid=(s // tm,),
                  in_specs=[row(D_FF), pl.BlockSpec((D_FF, D_MODEL), lambda i: (0, 0)), row(D_MODEL), row(D_MODEL)],
                  out_specs=[row(D_MODEL), row(D_MODEL), pl.BlockSpec((SUBLANES, LANES), lambda i: (0, 0))],
                  out_shape=[_sds((s, D_MODEL)), _sds((s, D_MODEL), BF16), _sds((SUBLANES, LANES))],
                  dims=("arbitrary",))(act, w_down, x1, tgt)


def _conv_act_bwd(up, dact, conv_w, conv_b):
    s = up.shape[0]
    ch = CONV_CHUNK

    def body(ug_ref, uv_ref, da_ref, wg_ref, wv_ref, bg_ref, bv_ref, dup_ref, dcw_ref, pg_ref, pv_ref, dg_ref, dv_ref):
        _fill_front_pad(pg_ref, ug_ref, s)
        _fill_front_pad(pv_ref, uv_ref, s)
        zero = jnp.zeros((SUBLANES, STRIP), F32)
        dg_ref[pl.ds(s, SUBLANES), :] = zero
        dv_ref[pl.ds(s, SUBLANES), :] = zero
        wg, wv, bg, bv = wg_ref[...], wv_ref[...], bg_ref[...], bv_ref[...]
        tile_sum = lambda t: jnp.sum(t.reshape(ch // SUBLANES, SUBLANES, STRIP), axis=0)
        accs = [[zero] * 4, [zero] * 4]
        for r0 in range(0, s, ch):
            hg = _conv_rows(pg_ref, wg, bg, r0, ch)
            hv = _conv_rows(pv_ref, wv, bv, r0, ch)
            sg = _sigmoid(hg)
            da = da_ref[pl.ds(r0, ch), :]
            dhs = (da * hv * (sg * (1.0 + hg * (1.0 - sg))), da * (hg * sg))
            for half, (dh, d_ref, p_ref) in enumerate(zip(dhs, (dg_ref, dv_ref), (pg_ref, pv_ref))):
                d_ref[pl.ds(r0, ch), :] = dh
                for k in range(3):
                    accs[half][k] = accs[half][k] + tile_sum(dh * p_ref[pl.ds(r0 + SUBLANES - 2 + k, ch), :])
                accs[half][3] = accs[half][3] + tile_sum(dh)
        for half, (d_ref, w) in enumerate(((dg_ref, wg), (dv_ref, wv))):
            for r0 in range(0, s, ch):
                dup = (d_ref[pl.ds(r0, ch), :] * w[2:3, :] + d_ref[pl.ds(r0 + 1, ch), :] * w[1:2, :]
                       + d_ref[pl.ds(r0 + 2, ch), :] * w[0:1, :])
                dup_ref[half, pl.ds(r0, ch), :] = dup.astype(BF16)
            rid = lax.broadcasted_iota(jnp.int32, (SUBLANES, STRIP), 0)
            out = zero
            for k in range(4):
                out = jnp.where(rid == k, jnp.sum(accs[half][k], axis=0, keepdims=True), out)
            dcw_ref[half] = out

    strip = lambda off: pl.BlockSpec((s, STRIP), lambda j: (0, j + off))
    wsp = lambda off: pl.BlockSpec((3, STRIP), lambda j: (0, j + off))
    bsp = lambda off: pl.BlockSpec((1, STRIP), lambda j: (0, j + off))
    return _pcall(body, name="conv_act_bwd", grid=(N_STRIPS,),
                  in_specs=[strip(0), strip(N_STRIPS), strip(0), wsp(0), wsp(N_STRIPS), bsp(0), bsp(N_STRIPS)],
                  out_specs=[pl.BlockSpec((2, s, STRIP), lambda j: (0, 0, j)), pl.BlockSpec((2, SUBLANES, STRIP), lambda j: (0, 0, j))],
                  out_shape=[_sds((2, s, D_FF), BF16), _sds((2, SUBLANES, D_FF))],
                  scratch_shapes=[pltpu.VMEM((s + SUBLANES, STRIP), F32)] * 4,
                  dims=("parallel",))(up, up, dact, conv_w, conv_w, conv_b, conv_b)


def _mix_bwd(dy, dh2, x1, g_ffn, w_out, y, att, w_glu, b_glu, g_att, g_ssm, hosted=None):
    s = dy.shape[0]
    tm = _row_tile(s)

    def body(dy_ref, dh2_ref, x1_ref, gf_ref, wo_ref, y_ref, att_ref, wg_ref, bg_ref, ga_ref, gs_ref,
             dx1_ref, dx1b_ref, datt_ref, dys_ref, dwg_ref, dgf_ref, dga_ref, dgs_ref, dbg_ref):
        i = pl.program_id(0)

        @pl.when(i == 0)
        def _():
            for r in (dwg_ref, dgf_ref, dga_ref, dgs_ref, dbg_ref):
                r[...] = jnp.zeros_like(r)

        dxn, dgf = _rms_bwd(x1_ref[...], gf_ref[...], dh2_ref[...])
        dx1 = dy_ref[...] + dxn
        dx1_ref[...] = dx1
        dx1b = dx1.astype(BF16)
        dx1b_ref[...] = dx1b
        dgf_ref[...] += dgf
        dma = _dot_nt(dx1b, wo_ref[0:ATTN_W, :])
        dms = _dot_nt(dx1b, wo_ref[ATTN_W:D_MODEL, :])
        datt, dga = _rms_bwd(_merge_heads(att_ref), ga_ref[...], dma)
        _split_heads(datt_ref, datt)
        dga_ref[...] += dga
        yv = y_ref[...]
        ge, sg = _ssm_glu(yv, wg_ref[...], bg_ref[...])
        dssm, dgs = _rms_bwd(ge * sg, gs_ref[...], dms)
        dgs_ref[...] += dgs
        dgl = dssm * ge * sg * (1.0 - sg)
        dglb = dgl.astype(BF16)
        dge = dssm * sg + _dot_nt(dglb, wg_ref[...])
        dbg_ref[...] += jnp.sum(dgl, axis=0, keepdims=True)
        dwg_ref[...] += _dot_tn(ge.astype(BF16), dglb)
        dys_ref[...] = dge * _gelu_grad(yv)

    row = lambda w: pl.BlockSpec((tm, w), lambda i: (i, 0))
    const = lambda a, b: pl.BlockSpec((a, b), lambda i: (0, 0))
    heads = pl.BlockSpec((HEADS, tm, HEAD_DIM), lambda i: (0, i, 0))
    nb = s // tm
    return _host_pcall(
        body, hosted, lambda: pl.program_id(0) == 0, lambda: pl.program_id(0) == nb - 1, n_in=11, n_out=9, n_scratch=0,
        name="mix_bwd", grid=(nb,),
        in_specs=[row(D_MODEL), row(D_MODEL), row(D_MODEL), const(1, D_MODEL), const(D_MODEL, D_MODEL), row(SSM_W),
                  heads, const(SSM_W, SSM_W), const(1, SSM_W), const(1, ATTN_W), const(1, SSM_W)],
        out_specs=[row(D_MODEL), row(D_MODEL), heads, row(SSM_W), const(SSM_W, SSM_W), const(1, D_MODEL),
                   const(1, ATTN_W), const(1, SSM_W), const(1, SSM_W)],
        out_shape=[_sds((s, D_MODEL)), _sds((s, D_MODEL), BF16), _sds((HEADS, s, HEAD_DIM)), _sds((s, SSM_W)),
                   _sds((SSM_W, SSM_W)), _sds((1, D_MODEL)), _sds((1, ATTN_W)), _sds((1, SSM_W)), _sds((1, SSM_W))],
        scratch_shapes=[], dims=("arbitrary",), operands=(dy, dh2, x1, g_ffn, w_out, y, att, w_glu, b_glu, g_att, g_ssm))


def _ssm_bwd(dys, uf, ub, xr, xi, bbr, bbi, ar, ai, ccr, cci, dsk, hosted=None):
    s = dys.shape[0]
    tm = _row_tile(s)
    nb = s // tm
    nt = tm // SUBLANES

    def body(dy_ref, u_ref, ub_ref, xr_ref, xi_ref, xrp_ref, xip_ref, bbr_ref, bbi_ref, ar_ref, ai_ref, ccr_ref,
             cci_ref, dsk_ref, du_ref, dbbr_ref, dbbi_ref, dccr_ref, dcci_ref, dar_ref, dai_ref, dd_ref,
             gr_s, gi_s, cr_s, ci_s, accr_s, acci_s):
        i = pl.program_id(1)
        first_block = i == nb - 1

        @pl.when(i == 0)
        def _():
            for r in (cr_s, ci_s, accr_s, acci_s, dbbr_ref, dbbi_ref, dccr_ref, dcci_ref, dd_ref):
                r[...] = jnp.zeros_like(r)

        dy = dy_ref[...]
        dyb = dy.astype(BF16)
        gr_s[...] = _dot_nt(dyb, ccr_ref[0])
        gi_s[...] = -_dot_nt(dyb, cci_ref[0])
        consts = _scan_consts(ar_ref[0], -ai_ref[0], CHUNK_S, True)
        row = lax.broadcasted_iota(jnp.int32, (SUBLANES, CHUNK_S), 0)

        def tile(kk, carry):
            cr, ci, accr, acci = carry
            k = nt - 1 - kk
            sl = pl.ds(pl.multiple_of(k * SUBLANES, SUBLANES), SUBLANES)
            gr, gi = _scan_tile(gr_s[sl, :], gi_s[sl, :], cr, ci, consts, True)
            gr_s[sl, :] = gr
            gi_s[sl, :] = gi
            slp = pl.ds(pl.multiple_of(jnp.maximum(k - 1, 0) * SUBLANES, SUBLANES), SUBLANES)
            inner = k > 0
            pr_t = jnp.where(inner, xr_ref[slp, :], xrp_ref[...])
            pi_t = jnp.where(inner, xi_ref[slp, :], xip_ref[...])
            live = jnp.logical_or(inner, jnp.logical_not(first_block))
            top_r = jnp.where(live, pltpu.roll(pr_t, 1, 0), 0.0)
            top_i = jnp.where(live, pltpu.roll(pi_t, 1, 0), 0.0)
            xpr = jnp.where(row == 0, top_r, pltpu.roll(xr_ref[sl, :], 1, 0))
            xpi = jnp.where(row == 0, top_i, pltpu.roll(xi_ref[sl, :], 1, 0))
            accr = accr + gr * xpr + gi * xpi
            acci = acci + gi * xpr - gr * xpi
            return gr[0:1, :], gi[0:1, :], accr, acci

        zeros = jnp.zeros((SUBLANES, CHUNK_S), F32)
        cr, ci, accr, acci = lax.fori_loop(0, nt, tile, (cr_s[...], ci_s[...], zeros, zeros))
        cr_s[...] = cr
        ci_s[...] = ci
        accr_s[...] += accr
        acci_s[...] += acci
        grb = gr_s[...].astype(BF16)
        gib = gi_s[...].astype(BF16)
        u_b = ub_ref[...]
        du_ref[...] = _dot_nt(grb, bbr_ref[0]) + _dot_nt(gib, bbi_ref[0]) + dsk_ref[...] * dy
        dbbr_ref[0] += _dot_tn(u_b, grb)
        dbbi_ref[0] += _dot_tn(u_b, gib)
        dccr_ref[0] += _dot_tn(xr_ref[...].astype(BF16), dyb)
        dcci_ref[0] -= _dot_tn(xi_ref[...].astype(BF16), dyb)
        dd_ref[...] += jnp.sum(dy * u_ref[...], axis=0, keepdims=True)

        @pl.when(i == nb - 1)
        def _():
            dar_ref[0] = jnp.sum(accr_s[...], axis=0, keepdims=True)
            dai_ref[0] = jnp.sum(acci_s[...], axis=0, keepdims=True)

    tiles_per_block = tm // SUBLANES
    rb = lambda i: nb - 1 - i
    wspec = lambda a, b: pl.BlockSpec((1, a, b), lambda j, i: (j, 0, 0))
    xblk = pl.BlockSpec((tm, CHUNK_S), lambda j, i: (rb(i), j))
    xprev = pl.BlockSpec((SUBLANES, CHUNK_S), lambda j, i: (jnp.maximum(rb(i) * tiles_per_block - 1, 0), j))
    ublk = pl.BlockSpec((tm, CHUNK_U), lambda j, i: (rb(i), j))
    first = lambda: jnp.logical_and(pl.program_id(0) == 0, pl.program_id(1) == 0)
    last = lambda: jnp.logical_and(pl.program_id(0) == SSM_CHUNKS - 1, pl.program_id(1) == nb - 1)
    return _host_pcall(
        body, hosted, first, last, n_in=14, n_out=8, n_scratch=6, name="ssm_bwd", grid=(SSM_CHUNKS, nb),
        in_specs=[ublk, ublk, ublk, xblk, xblk, xprev, xprev,
                  wspec(CHUNK_U, CHUNK_S), wspec(CHUNK_U, CHUNK_S), wspec(1, CHUNK_S), wspec(1, CHUNK_S),
                  wspec(CHUNK_S, CHUNK_U), wspec(CHUNK_S, CHUNK_U), pl.BlockSpec((1, CHUNK_U), lambda j, i: (0, j))],
        out_specs=[ublk, wspec(CHUNK_U, CHUNK_S), wspec(CHUNK_U, CHUNK_S), wspec(CHUNK_S, CHUNK_U),
                   wspec(CHUNK_S, CHUNK_U), wspec(1, CHUNK_S), wspec(1, CHUNK_S),
                   pl.BlockSpec((1, CHUNK_U), lambda j, i: (0, j))],
        out_shape=[_sds((s, SSM_W)), _sds((SSM_CHUNKS, CHUNK_U, CHUNK_S)), _sds((SSM_CHUNKS, CHUNK_U, CHUNK_S)),
                   _sds((SSM_CHUNKS, CHUNK_S, CHUNK_U)), _sds((SSM_CHUNKS, CHUNK_S, CHUNK_U)),
                   _sds((SSM_CHUNKS, 1, CHUNK_S)), _sds((SSM_CHUNKS, 1, CHUNK_S)), _sds((1, SSM_W))],
        scratch_shapes=[pltpu.VMEM((tm, CHUNK_S), F32)] * 2 + [pltpu.VMEM((1, CHUNK_S), F32)] * 2
                       + [pltpu.VMEM((SUBLANES, CHUNK_S), F32)] * 2,
        dims=("parallel", "arbitrary"), operands=(dys, uf, ub, xr, xi, xr, xi, bbr, bbi, ar, ai, ccr, cci, dsk))


def _attn_probs(q, ks, cs, lse, scale, diagonal):
    p = jnp.exp(_dot_nt(q, ks) * scale - cs - lse)
    if diagonal:
        tq, tk = p.shape
        causal = lax.broadcasted_iota(jnp.int32, (tq, tk), 1) <= lax.broadcasted_iota(jnp.int32, (tq, tk), 0)
        p = jnp.where(causal, p, 0.0)
    return p


def _attn_bwd(qh, kh, vh, crow, lse, doh, hosted=None):
    _, s, _ = qh.shape
    tq = _row_tile(s)
    nq = s // tq
    scale = HEAD_DIM ** -0.5
    hp = HEADS_PER_STEP

    def body(q_ref, k_ref, v_ref, c_ref, lse_ref, do_ref, dq_ref, dk_ref, dv_ref, dc_ref, p_s, dp_s):
        i = pl.program_id(1)

        @pl.when(i == 0)
        def _():
            for r in (dk_ref, dv_ref, dc_ref):
                r[...] = jnp.zeros_like(r)

        dobs = [do_ref[hh].astype(BF16) for hh in range(hp)]

        def first(j, dls, diagonal):
            off = pl.multiple_of(j * tq, tq)
            out = []
            for hh in range(hp):
                p = _attn_probs(q_ref[hh], k_ref[hh, pl.ds(off, tq), :], c_ref[hh, :, pl.ds(off, tq)], lse_ref[hh],
                                scale, diagonal)
                dp = _dot_nt(dobs[hh], v_ref[hh, pl.ds(off, tq), :])
                p_s[hh, j] = p
                dp_s[hh, j] = dp
                out.append(dls[hh] + jnp.sum(p * dp, axis=-1, keepdims=True))
            return tuple(out)

        zero_col = jnp.zeros((tq, 1), F32)
        dls = lax.fori_loop(0, i, lambda j, c: first(j, c, False), (zero_col,) * hp)
        dls = first(i, dls, True)

        def second(j, dqs):
            rows = pl.ds(pl.multiple_of(j * tq, tq), tq)
            out = []
            for hh in range(hp):
                p = p_s[hh, j]
                ds = p * (dp_s[hh, j] - dls[hh])
                dsb = ds.astype(BF16)
                dv_ref[hh, rows, :] += _dot_tn(p.astype(BF16), dobs[hh])
                dk_ref[hh, rows, :] += _dot_tn(dsb, q_ref[hh]) * scale
                dc_ref[hh, :, rows] -= jnp.sum(ds, axis=0, keepdims=True)
                out.append(dqs[hh] + _dot(dsb, k_ref[hh, rows, :]))
            return tuple(out)

        dqs = lax.fori_loop(0, i + 1, second, (jnp.zeros((tq, HEAD_DIM), F32),) * hp)
        for hh in range(hp):
            dq_ref[hh] = dqs[hh] * scale

    blk = pl.BlockSpec((hp, tq, HEAD_DIM), lambda h, i: (h, i, 0))
    full = pl.BlockSpec((hp, s, HEAD_DIM), lambda h, i: (h, 0, 0))
    crow_spec = pl.BlockSpec((hp, 1, s), lambda h, i: (h, 0, 0))
    nh = HEADS // hp
    first = lambda: jnp.logical_and(pl.program_id(0) == 0, pl.program_id(1) == 0)
    last = lambda: jnp.logical_and(pl.program_id(0) == nh - 1, pl.program_id(1) == nq - 1)
    return _host_pcall(body, hosted, first, last, n_in=6, n_out=4, n_scratch=2, name="attn_bwd", grid=(nh, nq),
                       in_specs=[blk, full, full, crow_spec, pl.BlockSpec((hp, tq, 1), lambda h, i: (h, i, 0)), blk],
                       out_specs=[blk, full, full, crow_spec],
                       out_shape=[_sds((HEADS, s, HEAD_DIM))] * 3 + [_sds((HEADS, 1, s))],
                       scratch_shapes=[pltpu.VMEM((hp, nq, tq, tq), F32)] * 2,
                       dims=("parallel", "arbitrary"), operands=(qh, kh, vh, crow, lse, doh))


def _prep_bwd(z, dqn, dkn, dv, du, dc, gq, gk, bf, gg, hosted=None):
    s = z.shape[0]
    tm = _row_tile(s)
    nb = s // tm

    def body(z_ref, dqn_ref, dkn_ref, dv_ref, du_ref, dc_ref, gq_ref, gk_ref, bf_ref, gg_ref,
             dz_ref, dgq_ref, dgk_ref, dbf_ref, carry_ref):
        i = pl.program_id(0)

        @pl.when(i == 0)
        def _():
            for r in (dgq_ref, dgk_ref, dbf_ref, carry_ref):
                r[...] = jnp.zeros_like(r)

        gg_m = gg_ref[...]

        def head_norm_bwd(t, g, dn):
            r = lax.rsqrt(_dot_exact_r(t * t, gg_m) * (1.0 / HEAD_DIM) + EPS)
            w = dn * g
            mean_wt = _dot_exact_r(w * t, gg_m) * (1.0 / HEAD_DIM)
            return r * w - t * (r * r * r) * mean_wt, jnp.sum(dn * t * r, axis=0, keepdims=True)

        dq, dgq = head_norm_bwd(z_ref[:, 0:ATTN_W], gq_ref[...], _merge_heads(dqn_ref))
        dk, dgk = head_norm_bwd(z_ref[:, ATTN_W:2 * ATTN_W], gk_ref[...], _merge_heads(dkn_ref))
        dgq_ref[...] += dgq
        dgk_ref[...] += dgk
        row = lax.broadcasted_iota(jnp.int32, (tm, tm), 0)
        col = lax.broadcasted_iota(jnp.int32, (tm, tm), 1)
        triu = (col >= row).astype(BF16)
        dlf = _dot_exact_l(triu, dc_ref[...]) + carry_ref[...]
        carry_ref[...] = dlf[0:1, :]
        df = dlf * _sigmoid(-_forget_logits(z_ref, bf_ref))
        dbf_ref[...] += jnp.sum(df, axis=0, keepdims=True)
        dz_ref[:, 0:ATTN_W] = dq.astype(BF16)
        dz_ref[:, ATTN_W:2 * ATTN_W] = dk.astype(BF16)
        dz_ref[:, 2 * ATTN_W:3 * ATTN_W] = _merge_heads(dv_ref).astype(BF16)
        tail = jnp.concatenate([df[:, :HEADS], du_ref[...], jnp.zeros((tm, Z_COLS - IN_COLS), F32)], axis=-1)
        dz_ref[:, F_COL0:Z_COLS] = tail.astype(BF16)

    row_spec = lambda w: pl.BlockSpec((tm, w), lambda i: (nb - 1 - i, 0))
    const = lambda shape: pl.BlockSpec(shape, lambda i: (0, 0))
    return _host_pcall(
        body, hosted, lambda: pl.program_id(0) == 0, lambda: pl.program_id(0) == nb - 1, n_in=10, n_out=4, n_scratch=1,
        name="prep_bwd", grid=(nb,),
        in_specs=[row_spec(Z_COLS)] + [pl.BlockSpec((HEADS, tm, HEAD_DIM), lambda i: (0, nb - 1 - i, 0))] * 3
                 + [row_spec(ATTN_W), row_spec(LANES), const((1, ATTN_W)),
                    const((1, ATTN_W)), const((1, LANES)), const((ATTN_W, ATTN_W))],
        out_specs=[row_spec(Z_COLS), const((1, ATTN_W)), const((1, ATTN_W)), const((1, LANES))],
        out_shape=[_sds((s, Z_COLS), BF16), _sds((1, ATTN_W)), _sds((1, ATTN_W)), _sds((1, LANES))],
        scratch_shapes=[pltpu.VMEM((1, LANES), F32)], dims=("arbitrary",),
        operands=(z, dqn, dkn, dv, du, dc, gq, gk, bf, gg))


def _in_norm_bwd(x, g_mix, dh, dx1, hosted=None):
    s = x.shape[0]
    tm = _row_tile(s)

    def body(x_ref, g_ref, dh_ref, dx1_ref, dx_ref, dg_ref):
        i = pl.program_id(0)

        @pl.when(i == 0)
        def _():
            dg_ref[...] = jnp.zeros_like(dg_ref)

        dxn, dg = _rms_bwd(x_ref[...], g_ref[...], dh_ref[...])
        dx_ref[...] = dx1_ref[...] + dxn
        dg_ref[...] += dg

    row = pl.BlockSpec((tm, D_MODEL), lambda i: (i, 0))
    vec = pl.BlockSpec((1, D_MODEL), lambda i: (0, 0))
    nb = s // tm
    return _host_pcall(body, hosted, lambda: pl.program_id(0) == 0, lambda: pl.program_id(0) == nb - 1,
                       n_in=4, n_out=2, n_scratch=0, name="in_norm_bwd", grid=(nb,), in_specs=[row, vec, row, row],
                       out_specs=[row, vec], out_shape=[_sds((s, D_MODEL)), _sds((1, D_MODEL))], scratch_shapes=[],
                       dims=("arbitrary",), operands=(x, g_mix, dh, dx1))


def _adamw_refs(w_ref, g_ref, m_ref, v_ref, d_ref, mo_ref, vo_ref):
    gv = g_ref[...]
    mn = ADAM_B1 * m_ref[...] + (1.0 - ADAM_B1) * gv
    vn = ADAM_B2 * v_ref[...] + (1.0 - ADAM_B2) * (gv * gv)
    m_hat = mn / (1.0 - ADAM_B1 ** ADAM_STEP)
    v_hat = vn / (1.0 - ADAM_B2 ** ADAM_STEP)
    d_ref[...] = -ADAM_LR * (m_hat / (jnp.sqrt(v_hat) + ADAM_EPS) + ADAM_WD * w_ref[...])
    mo_ref[...] = mn
    vo_ref[...] = vn


def _adamw_small(ws, gs, ms, vs):
    n = len(ws)

    def body(*refs):
        ins, outs = refs[:4 * n], refs[4 * n:]
        for i in range(n):
            _adamw_refs(ins[i], ins[n + i], ins[2 * n + i], ins[3 * n + i], *outs[3 * i:3 * i + 3])

    vm = pl.BlockSpec(memory_space=pltpu.VMEM)
    out_shape = [_sds(w.shape) for w in ws for _ in range(3)]
    return _pallas(body, name="adamw_small", in_specs=[vm] * (4 * n), out_specs=[vm] * (3 * n), out_shape=out_shape,
                   compiler_params=pltpu.CompilerParams(vmem_limit_bytes=VMEM_LIMIT))(*ws, *gs, *ms, *vs)


def _adamw(w, g, m, v, *, name):
    r, c = w.shape
    tr = r
    for cand in (256, 176, 128, 64):
        if r > cand and r % cand == 0:
            tr = cand
            break

    def body(w_ref, g_ref, m_ref, v_ref, d_ref, mo_ref, vo_ref):
        _adamw_refs(w_ref, g_ref, m_ref, v_ref, d_ref, mo_ref, vo_ref)

    spec = pl.BlockSpec((tr, c), lambda i: (i, 0))
    return _pcall(body, name=name, grid=(r // tr,), in_specs=[spec] * 4, out_specs=[spec] * 3,
                  out_shape=[_sds((r, c))] * 3, dims=("parallel",))(w, g, m, v)


def _prefetch_call(body, *, name, grid, in_specs, out_specs, out_shape, operands):
    grid_spec = pltpu.PrefetchScalarGridSpec(num_scalar_prefetch=1, grid=grid, in_specs=in_specs, out_specs=out_specs)
    params = pltpu.CompilerParams(dimension_semantics=("parallel",) * len(grid), vmem_limit_bytes=VMEM_LIMIT)
    return _pallas(body, name=name, grid_spec=grid_spec, out_shape=out_shape, compiler_params=params)(*operands)


def _place_cols(buf, shard, place):
    rows, cols = shard.shape
    tr = 256

    def body(place_ref, s_ref, b_ref, o_ref):
        o_ref[...] = s_ref[...]

    grid_spec = pltpu.PrefetchScalarGridSpec(
        num_scalar_prefetch=1, grid=(rows // tr,),
        in_specs=[pl.BlockSpec((tr, cols), lambda i, p: (i, 0)), pl.BlockSpec(memory_space=pltpu.HBM)],
        out_specs=pl.BlockSpec((tr, cols), lambda i, p: (i, p[0])))
    return _pallas(body, name="place_own_cols", grid_spec=grid_spec, out_shape=_sds(buf.shape, buf.dtype),
                   input_output_aliases={2: 0},
                   compiler_params=pltpu.CompilerParams(dimension_semantics=("parallel",),
                                                        vmem_limit_bytes=VMEM_LIMIT))(place, shard, buf)


def _half_rows_tile(hr):
    return hr if hr <= 256 else 176 if hr % 176 == 0 else 256


def _add_half(g, landed, place, *, name):
    def body(place_ref, g_ref, l_ref, o_ref):
        own = g_ref[0] if len(g_ref.shape) == 4 else g_ref[...]
        o_ref[...] = (own + l_ref[...]).astype(BF16)

    if g.ndim == 4:
        _, _, hr, c = g.shape
        tr = _half_rows_tile(hr)
        blk = (1, tr, c)
        return _prefetch_call(
            body, name=name, grid=(N_CHIPS, hr // tr),
            in_specs=[pl.BlockSpec((1,) + blk, lambda j, i, p: (j, p[1], i, 0)), pl.BlockSpec(blk, lambda j, i, p: (j, i, 0))],
            out_specs=pl.BlockSpec(blk, lambda j, i, p: (j, i, 0)), out_shape=_sds(landed.shape, BF16),
            operands=(place, g, landed))
    hr, c = landed.shape
    tr, tc = 256, _tile(c, 2176)
    nb = hr // tr
    return _prefetch_call(
        body, name=name, grid=(nb, c // tc),
        in_specs=[pl.BlockSpec((tr, tc), lambda i, j, p: (p[1] * nb + i, j)), pl.BlockSpec((tr, tc), lambda i, j, p: (i, j))],
        out_specs=pl.BlockSpec((tr, tc), lambda i, j, p: (i, j)), out_shape=_sds(landed.shape, BF16),
        operands=(place, g, landed))


def _sum_chips(chip_sum, lands, place, *, name, tc, window_stride=0):
    _, hr, c = lands.shape
    tr = _half_rows_tile(hr)
    nb = hr // tr
    ncb = c // tc

    def body(place_ref, own_ref, a_ref, b_ref, c_ref, o_ref):
        own = own_ref[0] if len(own_ref.shape) == 3 else own_ref[...]
        o_ref[...] = ((own.astype(F32) + a_ref[0].astype(F32)) + b_ref[0].astype(F32)) + c_ref[0].astype(F32)

    land = lambda k: pl.BlockSpec((1, tr, tc), lambda i, j, p: ((p[0] + k) % N_CHIPS, i, j))
    if chip_sum.ndim == 3:
        own_spec = land(0)
    else:
        stride = window_stride // tc
        own_spec = pl.BlockSpec((tr, tc), lambda i, j, p: (i, p[0] * stride + j))
    return _prefetch_call(
        body, name=name, grid=(nb, ncb), in_specs=[own_spec, land(1), land(2), land(3)],
        out_specs=pl.BlockSpec((tr, tc), lambda i, j, p: (p[1] * nb + i, j)), out_shape=_sds((2 * hr, c)),
        operands=(place, chip_sum, lands, lands, lands))


_HBM = pl.BlockSpec(memory_space=pltpu.HBM)


def _place():
    x, y, c = lax.axis_index("x"), lax.axis_index("y"), lax.axis_index("c")
    chips = [(1 - x, y), (x, 1 - y), (1 - x, 1 - y)]
    return x, y, c, chips


def _rcopy(src, dst, send_sem, recv_sem, to):
    return pltpu.make_async_remote_copy(src_ref=src, dst_ref=dst, send_sem=send_sem, recv_sem=recv_sem,
                                        device_id=to, device_id_type=MESH)


UP_COLS = 2 * D_FF // N_CHIPS
IN_WINDOW = 640
IN_STRIDE = 512


class _Hosted:
    def __init__(self, operands, out_shapes, n_sems, start, finish, aliases=None, local_sems=0):
        self.operands, self.out_shapes, self.n_sems = list(operands), list(out_shapes), n_sems
        self.start, self.finish, self.aliases, self.local_sems = start, finish, dict(aliases or {}), local_sems

    def scratch(self):
        return ([pltpu.SemaphoreType.DMA((self.n_sems,)), pltpu.SemaphoreType.DMA((self.n_sems,))]
                + [pltpu.SemaphoreType.DMA] * self.local_sems)


def _both(a, b):
    na, nao, nas = len(a.operands), len(a.out_shapes), len(a.scratch())

    def start(ins, outs, sems):
        a.start(ins[:na], outs[:nao], sems[:nas])
        b.start(ins[na:], outs[nao:], sems[nas:])

    def finish(ins, outs, sems):
        a.finish(ins[:na], outs[:nao], sems[:nas])
        b.finish(ins[na:], outs[nao:], sems[nas:])

    both = _Hosted(a.operands + b.operands, a.out_shapes + b.out_shapes, 0, start, finish,
                   aliases={**a.aliases, **{na + i: nao + o for i, o in b.aliases.items()}})
    both.scratch = lambda: a.scratch() + b.scratch()
    return both


def _then(a, b):
    nas = len(a.scratch())

    def finish(ins, outs, sems):
        a.finish(ins, outs, sems[:nas])
        b.start(ins, outs, sems[nas:])
        b.finish(ins, outs, sems[nas:])

    chain = _Hosted(a.operands, a.out_shapes, 0, lambda ins, outs, sems: a.start(ins, outs, sems[:nas]), finish,
                    aliases=a.aliases)
    chain.scratch = lambda: a.scratch() + b.scratch()
    return chain


def _run_hosted(hosted, *, name):
    n_in, n_out = len(hosted.operands), len(hosted.out_shapes)

    def body(*refs):
        parts = (refs[:n_in], refs[n_in:n_in + n_out], refs[n_in + n_out:])
        hosted.start(*parts)
        hosted.finish(*parts)

    return _pallas(body, name=name, in_specs=[_HBM] * n_in, out_specs=[_HBM] * n_out, out_shape=hosted.out_shapes,
                   input_output_aliases=hosted.aliases, scratch_shapes=hosted.scratch())(*hosted.operands)


def _host_pcall(core_body, hosted, first, last, *, n_in, n_out, n_scratch, name, grid, in_specs, out_specs, out_shape,
                scratch_shapes, dims, operands):
    if hosted is None:
        outs = _pcall(core_body, name=name, grid=grid, in_specs=in_specs, out_specs=out_specs, out_shape=out_shape,
                      scratch_shapes=scratch_shapes, dims=dims)(*operands)
        return outs, []
    hi, ho = len(hosted.operands), len(hosted.out_shapes)

    def body(*refs):
        a, b = n_in, n_in + hi
        c, d = b + n_out, b + n_out + ho
        e = d + n_scratch
        parts = (refs[a:b], refs[c:d], refs[e:])

        @pl.when(first())
        def _():
            hosted.start(*parts)

        core_body(*refs[:a], *refs[b:c], *refs[d:e])

        @pl.when(last())
        def _():
            hosted.finish(*parts)

    params = pltpu.CompilerParams(dimension_semantics=("arbitrary",) * len(grid), vmem_limit_bytes=VMEM_LIMIT)
    outs = _pallas(body, name=name, grid=grid, in_specs=list(in_specs) + [_HBM] * hi, out_specs=list(out_specs) + [_HBM] * ho,
                   out_shape=list(out_shape) + hosted.out_shapes, scratch_shapes=list(scratch_shapes) + hosted.scratch(),
                   input_output_aliases={n_in + a: n_out + b for a, b in hosted.aliases.items()},
                   compiler_params=params)(*operands, *hosted.operands)
    return outs[:n_out], outs[n_out:]


WHOLE_HALF = (0, 1, 1)


def _band_rows(src, hc, band):
    first, count, of = band
    hr = src.shape[0] // 2
    return pl.ds(hc * hr + first * (hr // of), count * (hr // of))


def _gather_slot(src, out, chip, hc, band=WHOLE_HALF):
    cols = src.shape[1]
    if len(out.shape) == 2:
        return out.at[_band_rows(src, hc, band), pl.ds(pl.multiple_of(chip * cols, LANES), cols)]
    return out.at[chip, _band_rows(src, hc, band), :]


def _gathered_shape(shard, by_cols):
    if by_cols:
        return _sds((shard.shape[0], N_CHIPS * shard.shape[1]), shard.dtype)
    return _sds((N_CHIPS,) + shard.shape, shard.dtype)


def _plan_gather_ici(shards, by_cols, whole=(), bands=None, into=None):
    n = len(shards)
    bands = bands or [WHOLE_HALF] * n
    into = into or [None] * n
    given = [w for w in range(n) if into[w] is not None]
    n_ops = n + len(whole)

    def copies(ins, outs, sems):
        send_sems, recv_sems = sems[0], sems[1]
        x, y, c, chips = _place()
        me = 2 * x + y
        sends, waits = [], []
        for w in range(n + len(whole)):
            for k, (cx, cy) in enumerate(chips):
                sem = (send_sems.at[3 * w + k], recv_sems.at[3 * w + k])
                if w < n:
                    sends.append(_rcopy(ins[w].at[_band_rows(ins[w], c, bands[w]), :],
                                        _gather_slot(ins[w], outs[w], me, c, bands[w]), *sem, (cx, cy, c)))
                    landed = _gather_slot(ins[w], outs[w], 2 * cx + cy, c, bands[w])
                else:
                    sends.append(_rcopy(ins[w], outs[w].at[me], *sem, (cx, cy, c)))
                    landed = outs[w].at[2 * cx + cy]
                waits.append(_rcopy(landed, landed, *sem, (cx, cy, c)))
        return sends, waits

    def start(ins, outs, sems):
        for cp in copies(ins, outs, sems)[0]:
            cp.start()

    def finish(ins, outs, sems):
        sends, waits = copies(ins, outs, sems)
        for cp in waits:
            cp.wait_recv()
        for cp in sends:
            cp.wait_send()

    out_shapes = [_gathered_shape(s, bc) for s, bc in zip(shards, by_cols)] + [_sds((N_CHIPS,) + a.shape, a.dtype) for a in whole]
    return _Hosted(list(shards) + list(whole) + [into[w] for w in given], out_shapes, 3 * n_ops, start, finish,
                   aliases={n_ops + i: w for i, w in enumerate(given)})


def _plan_gather_d2d(bufs, shard_shapes, bands=None):
    n = len(bufs)
    bands = bands or [WHOLE_HALF] * n

    def copies(ins, outs, sems):
        send_sems, recv_sems = sems
        x, y, c, chips = _place()
        sibling = (x, y, 1 - c)
        sends, waits = [], []
        for w in range(n):
            for k, (cx, cy) in enumerate(chips):
                sem = (send_sems.at[3 * w + k], recv_sems.at[3 * w + k])
                landed = _gather_slot(shard_shapes[w], outs[w], 2 * cx + cy, c, bands[w])
                other = _gather_slot(shard_shapes[w], outs[w], 2 * cx + cy, 1 - c, bands[w])
                sends.append(_rcopy(landed, landed, *sem, sibling))
                waits.append(_rcopy(other, other, *sem, sibling))
        return sends, waits

    def start(ins, outs, sems):
        for cp in copies(ins, outs, sems)[0]:
            cp.start()

    def finish(ins, outs, sems):
        sends, waits = copies(ins, outs, sems)
        for cp in waits:
            cp.wait_recv()
        for cp in sends:
            cp.wait_send()

    return _Hosted(bufs, [_sds(b.shape, b.dtype) for b in bufs], 3 * n, start, finish, aliases={w: w for w in range(n)})


def _plan_allgather_first(block):
    def copies(ins, outs, sems):
        send_sems, recv_sems = sems
        x, y, c, chips = _place()
        me = 4 * x + 2 * y + c
        peers = [(x, y, 1 - c)] + [(cx, cy, c) for cx, cy in chips]
        sends = [_rcopy(ins[0], outs[0].at[me], send_sems.at[k], recv_sems.at[k], p) for k, p in enumerate(peers)]
        waits = [_rcopy(outs[0].at[4 * px + 2 * py + pc], outs[0].at[4 * px + 2 * py + pc], send_sems.at[k],
                        recv_sems.at[k], (px, py, pc)) for k, (px, py, pc) in enumerate(peers)]
        return sends, waits

    def start(ins, outs, sems):
        for cp in copies(ins, outs, sems)[0]:
            cp.start()

    def finish(ins, outs, sems):
        sends, waits = copies(ins, outs, sems)
        for cp in waits:
            cp.wait_recv()
        for cp in sends:
            cp.wait_send()

    return _Hosted([block], [_sds((8,) + block.shape)], 4, start, finish)


def _plan_allgather_second(gathered):
    def copies(ins, outs, sems):
        send_sems, recv_sems = sems
        x, y, c, chips = _place()
        sends, waits = [], []
        for k, (cx, cy) in enumerate(chips):
            landed = outs[0].at[4 * cx + 2 * cy + c]
            other = outs[0].at[4 * cx + 2 * cy + 1 - c]
            sends.append(_rcopy(landed, landed, send_sems.at[k], recv_sems.at[k], (x, y, 1 - c)))
            waits.append(_rcopy(other, other, send_sems.at[k], recv_sems.at[k], (x, y, 1 - c)))
        return sends, waits

    def start(ins, outs, sems):
        for cp in copies(ins, outs, sems)[0]:
            cp.start()

    def finish(ins, outs, sems):
        sends, waits = copies(ins, outs, sems)
        for cp in waits:
            cp.wait_recv()
        for cp in sends:
            cp.wait_send()

    return _Hosted([gathered], [_sds(gathered.shape)], 3, start, finish, aliases={0: 0})


def _place_block(gathered, block, device):
    rows, lanes = block.shape

    def body(dev_ref, b_ref, g_ref, o_ref):
        o_ref[0] = b_ref[...]

    grid_spec = pltpu.PrefetchScalarGridSpec(
        num_scalar_prefetch=1, grid=(1,),
        in_specs=[pl.BlockSpec((rows, lanes), lambda i, d: (0, 0)), pl.BlockSpec(memory_space=pltpu.HBM)],
        out_specs=pl.BlockSpec((1, rows, lanes), lambda i, d: (d[0], 0, 0)))
    return _pallas(body, name="place_own_block", grid_spec=grid_spec, out_shape=_sds(gathered.shape),
                   input_output_aliases={2: 0},
                   compiler_params=pltpu.CompilerParams(dimension_semantics=("arbitrary",),
                                                        vmem_limit_bytes=VMEM_LIMIT))(device, block, gathered)


def _sum_devices(gathered):
    _, rows, lanes = gathered.shape
    tr = rows // 2 if rows % 16 == 0 else rows

    def body(g_ref, o_ref):
        acc = g_ref[0]
        for d in range(1, 8):
            acc = acc + g_ref[d]
        o_ref[...] = acc

    return _pcall(body, name="sum_devices", grid=(rows // tr,), in_specs=[pl.BlockSpec((8, tr, lanes), lambda i: (0, i, 0))],
                  out_specs=pl.BlockSpec((tr, lanes), lambda i: (i, 0)), out_shape=_sds((rows, lanes)), dims=("parallel",))(gathered)


def _plan_swap(grads):
    def copies(ins, outs, sems):
        send_sems, recv_sems = sems
        x, y, c, _ = _place()
        cps = []
        for w, g_ref in enumerate(ins):
            if len(g_ref.shape) == 4:
                theirs = g_ref.at[:, 1 - c]
            else:
                hr = g_ref.shape[0] // 2
                theirs = g_ref.at[pl.ds((1 - c) * hr, hr), :]
            cps.append(_rcopy(theirs, outs[w], send_sems.at[w], recv_sems.at[w], (x, y, 1 - c)))
        return cps

    def start(ins, outs, sems):
        for cp in copies(ins, outs, sems):
            cp.start()

    def finish(ins, outs, sems):
        for cp in copies(ins, outs, sems):
            cp.wait()

    out_shapes = [_sds((g.shape[0], g.shape[2], g.shape[3])) if g.ndim == 4 else _sds((g.shape[0] // 2, g.shape[1]))
                  for g in grads]
    return _Hosted(grads, out_shapes, len(grads), start, finish)


def _plan_scatter(chip_sums, windows):
    def copies(ins, outs, sems):
        send_sems, recv_sems = sems
        x, y, c, chips = _place()
        me = 2 * x + y
        sends, waits = [], []
        for w, s_ref in enumerate(ins):
            for k, (cx, cy) in enumerate(chips):
                tgt = 2 * cx + cy
                if windows[w] is not None:
                    stride, width = windows[w]
                    part = s_ref.at[:, pl.ds(pl.multiple_of(tgt * stride, LANES), width)]
                else:
                    part = s_ref.at[tgt]
                sem = (send_sems.at[3 * w + k], recv_sems.at[3 * w + k])
                sends.append(_rcopy(part, outs[w].at[me], *sem, (cx, cy, c)))
                slot = outs[w].at[tgt]
                waits.append(_rcopy(slot, slot, *sem, (cx, cy, c)))
        return sends, waits

    def start(ins, outs, sems):
        for cp in copies(ins, outs, sems)[0]:
            cp.start()

    def finish(ins, outs, sems):
        sends, waits = copies(ins, outs, sems)
        for cp in waits:
            cp.wait_recv()
        for cp in sends:
            cp.wait_send()

    out_shapes = [_sds((N_CHIPS, s.shape[0], win[1]), BF16) if win is not None else _sds(s.shape, BF16)
                  for s, win in zip(chip_sums, windows)]
    return _Hosted(chip_sums, out_shapes, 3 * len(chip_sums), start, finish)


def _plan_join(reds):
    def copies(ins, outs, sems):
        send_sems, recv_sems = sems
        x, y, c, _ = _place()
        sends, waits = [], []
        for w, out in enumerate(outs):
            hr = out.shape[0] // 2
            mine = out.at[pl.ds(c * hr, hr), :]
            theirs = out.at[pl.ds((1 - c) * hr, hr), :]
            sends.append(_rcopy(mine, mine, send_sems.at[w], recv_sems.at[w], (x, y, 1 - c)))
            waits.append(_rcopy(theirs, theirs, send_sems.at[w], recv_sems.at[w], (x, y, 1 - c)))
        return sends, waits

    def start(ins, outs, sems):
        for cp in copies(ins, outs, sems)[0]:
            cp.start()

    def finish(ins, outs, sems):
        sends, waits = copies(ins, outs, sems)
        for cp in waits:
            cp.wait_recv()
        for cp in sends:
            cp.wait_send()

    return _Hosted(reds, [_sds(r.shape) for r in reds], len(reds), start, finish, aliases={w: w for w in range(len(reds))})


def _allreduce_small(v):
    m_per = v.shape[0]

    def body(v_ref, out_ref, all_ref, send_sems, recv_sems, local_sem):
        x, y, c, chips = _place()
        me, sibling = (x, y, c), (x, y, 1 - c)

        def rows(px, py, pc):
            return all_ref.at[pl.ds((4 * px + 2 * py + pc) * m_per, m_per), :]

        def copy(k, block, to, src=None):
            return _rcopy(rows(*block) if src is None else src, rows(*block), send_sems.at[k], recv_sems.at[k], to)

        mine = pltpu.make_async_copy(v_ref, rows(*me), local_sem)
        mine.start()
        first = [copy(0, me, sibling, src=v_ref)]
        first += [copy(1 + k, me, (*chip, c), src=v_ref) for k, chip in enumerate(chips)]
        for cp in first:
            cp.start()
        passed = [copy(4 + k, (*chip, c), sibling) for k, chip in enumerate(chips)]
        for k, chip in enumerate(chips):
            copy(1 + k, (*chip, c), me).wait_recv()
            passed[k].start()
        copy(0, sibling, me).wait_recv()
        for k, chip in enumerate(chips):
            copy(4 + k, (*chip, 1 - c), me).wait_recv()
        for cp in first + passed:
            cp.wait_send()
        mine.wait()
        acc = all_ref[pl.ds(0, m_per), :]
        for d in range(1, 8):
            acc = acc + all_ref[pl.ds(d * m_per, m_per), :]
        out_ref[...] = acc

    vm = pl.BlockSpec(memory_space=pltpu.VMEM)
    return _pallas(body, name="allreduce_small", in_specs=[vm], out_specs=vm, out_shape=_sds((m_per, LANES)),
                          scratch_shapes=[pltpu.VMEM((8 * m_per, LANES), F32), pltpu.SemaphoreType.DMA((7,)),
                                          pltpu.SemaphoreType.DMA((7,)), pltpu.SemaphoreType.DMA],
                          compiler_params=pltpu.CompilerParams(vmem_limit_bytes=VMEM_LIMIT))(v)


def _block_diag(blocks):
    j, g, a, b = blocks.shape
    eye = jnp.eye(g, dtype=bool)[None, :, None, :, None]
    return jnp.where(eye, blocks[:, :, :, None, :], jnp.zeros((), blocks.dtype)).reshape(j, g * a, g * b)


def _diag_blocks(m, a, b):
    j = m.shape[0]
    g = m.shape[1] // a
    t = m.reshape(j, g, a, g, b)
    eye = jnp.eye(g, dtype=bool)[None, :, None, :, None]
    return jnp.sum(jnp.where(eye, t, 0.0), axis=3)


_SMALL = (("g_mix", (1024,)), ("b_f", (8,)), ("g_q", (64,)), ("g_k", (64,)), ("lambda_re", (32, 64)),
          ("lambda_im", (32, 64)), ("log_step", (32,)), ("b_re", (32, 64, 16)), ("b_im", (32, 64, 16)),
          ("c_re", (32, 16, 64)), ("c_im", (32, 16, 64)), ("d_skip", (32, 16)), ("b_glu", (512,)),
          ("g_attn_out", (512,)), ("g_ssm_out", (512,)), ("g_ffn", (1024,)), ("conv_b", (5632,)))


_LATE_SMALL = ("g_mix", "b_f", "g_q", "g_k")
_EARLY_SMALL = tuple(n for n, _ in _SMALL if n not in _LATE_SMALL)


def _packed_rows(n):
    tile = SUBLANES * LANES
    return -(-n // tile) * SUBLANES


def _pack_small(arrs):
    parts = []
    for a in arrs:
        flat = a.reshape(-1)
        rows = _packed_rows(flat.shape[0])
        parts.append(jnp.pad(flat, (0, rows * LANES - flat.shape[0])).reshape(rows, LANES))
    return jnp.concatenate(parts, axis=0)


def _unpack_small(buf, shapes):
    out, r = [], 0
    for shape in shapes:
        n = math.prod(shape)
        out.append(buf[r:r + _packed_rows(n)].reshape(-1)[:n].reshape(shape))
        r += _packed_rows(n)
    return out


def _halves(t):
    return t.reshape(N_CHIPS, 2, t.shape[0] // (2 * N_CHIPS), t.shape[1])


class _MeshComm:
    def __init__(self, args):
        x, y, self.core = lax.axis_index("x"), lax.axis_index("y"), lax.axis_index("c")
        self.chip = 2 * x + y
        self.place = jnp.stack([self.chip, self.core]).astype(jnp.int32)
        self.shards = {n: args[n].astype(BF16) for n in ("w_in", "w_glu", "w_out", "w_up", "w_down")}
        self.conv_w = args["conv_w"]

    def _own(self, stacked, mine):
        return lax.dynamic_update_slice(stacked, mine[None], (self.chip,) + (0,) * mine.ndim)

    def w_in(self):
        sh = self.shards["w_in"]
        (buf,) = _run_hosted(_then(_plan_gather_ici([sh], [False]), _plan_gather_d2d([sh], [sh])), name="gather_w_in")
        whole = self._own(buf, sh).transpose(1, 0, 2).reshape(D_MODEL, IN_COLS)
        return jnp.pad(whole, ((0, 0), (0, Z_COLS - IN_COLS)))

    def gather_first(self):
        self.mid = [self.shards[n] for n in ("w_glu", "w_out", "w_down")]
        return _plan_gather_ici(self.mid + [self.shards["w_up"]], [False, False, False, True], whole=[self.conv_w],
                                bands=[WHOLE_HALF] * 3 + [(0, 1, 4)])

    def gather_second(self, landed):
        self.g_cw = landed[4]
        return _both(_plan_gather_d2d(list(landed[:3]), self.mid),
                     _plan_gather_ici([self.shards["w_up"]], [True], bands=[(1, 3, 4)], into=[landed[3]]))

    def weights(self, gathered):
        g_glu, g_out, g_down = gathered[:3]
        own = self._own
        return (own(g_glu, self.mid[0]).reshape(SSM_W, SSM_W), own(g_out, self.mid[1]).reshape(D_MODEL, D_MODEL),
                own(g_down, self.mid[2]).reshape(D_FF, D_MODEL),
                own(self.g_cw, self.conv_w).transpose(1, 0, 2).reshape(3, 2 * D_FF))

    def gather_third(self, gathered):
        return _plan_gather_d2d([gathered[3]], [self.shards["w_up"]])

    def w_up(self, passed):
        return _place_cols(passed[0], self.shards["w_up"], self.place)

    def swap_down(self, d_w_down):
        self.d_down = _halves(d_w_down)
        return _plan_swap([self.d_down])

    def swap(self, landed_down, d_w_up, d_w_glu, d_w_out):
        self.sum_down = _add_half(self.d_down, landed_down[0], self.place, name="add_w_down")
        self.early = [d_w_up, _halves(d_w_glu), _halves(d_w_out)]
        return _both(_plan_scatter([self.sum_down], [None]), _plan_swap(self.early))

    def scatter(self, landed, small_block):
        self.land_down = landed[0]
        self.early_sums = [_add_half(g, l, self.place, name="add_" + n)
                           for g, l, n in zip(self.early, landed[1:], ("w_up", "w_glu", "w_out"))]
        return _both(_plan_scatter(self.early_sums, [(UP_COLS, UP_COLS), None, None]), _plan_allgather_first(small_block))

    def swap_in(self, d_w_in):
        self.d_in = d_w_in
        return _plan_swap([d_w_in])

    def scatter_in(self, landed):
        self.sum_in = _add_half(self.d_in, landed[0], self.place, name="add_w_in")
        return _plan_scatter([self.sum_in], [(IN_STRIDE, IN_WINDOW)])

    def small_second(self, landed_small):
        return _plan_allgather_second(landed_small[0])

    def reduce(self, lands):
        early_lands, (land_in,) = lands
        sum_in = self.sum_in
        es, el = self.early_sums, early_lands
        todo = [(sum_in, land_in, "w_in", LANES, IN_STRIDE), (es[1], el[1], "w_glu", SSM_W, 0),
                (es[2], el[2], "w_out", D_MODEL, 0), (es[0], el[0], "w_up", UP_COLS, UP_COLS),
                (self.sum_down, self.land_down, "w_down", D_MODEL, 0)]
        reds = _run_hosted(_plan_join([_sum_chips(s, l, self.place, name="sum_" + n, tc=tc, window_stride=st)
                                       for s, l, n, tc, st in todo]), name="join_halves")
        g_big = dict(zip(("w_in", "w_glu", "w_out", "w_up", "w_down"), reds))
        g_big["w_in"] = lax.dynamic_slice_in_dim(reds[0], 2 * self.chip, IN_COLS // N_CHIPS, axis=1)
        return g_big


def _local_step(x, tgt, p, comm):
    s = x.shape[0]
    row = lambda v: v.reshape(1, -1)
    g_mix, g_ffn = row(p["g_mix"]), row(p["g_ffn"])
    g_att, g_ssm, b_glu, conv_b = row(p["g_attn_out"]), row(p["g_ssm_out"]), row(p["b_glu"]), row(p["conv_b"])
    gq = row(jnp.tile(p["g_q"], HEADS))
    gk = row(jnp.tile(p["g_k"], HEADS))
    bf = row(jnp.pad(p["b_f"], (0, LANES - HEADS)))
    gg = jnp.kron(jnp.eye(HEADS, dtype=F32), jnp.ones((HEAD_DIM, HEAD_DIM), F32)).astype(BF16)
    dsk = row(p["d_skip"])

    rep = lambda a: jnp.repeat(a, SSM_GROUP, axis=0)
    lr, li = rep(p["lambda_re"]), rep(p["lambda_im"])
    ls = rep(jnp.broadcast_to(p["log_step"][:, None], (SSM_GROUPS, SSM_STATE)))
    bt_re = p["b_re"].transpose(0, 2, 1).reshape(_PARAM_SHAPE)
    bt_im = p["b_im"].transpose(0, 2, 1).reshape(_PARAM_SHAPE)
    a_re_rep, a_im_rep, bb_re, bb_im = _ssm_params(lr, li, ls, bt_re, bt_im)
    ar = a_re_rep[::SSM_GROUP].reshape(SSM_CHUNKS, 1, CHUNK_S)
    ai = a_im_rep[::SSM_GROUP].reshape(SSM_CHUNKS, 1, CHUNK_S)
    chunked = lambda t: t.reshape(SSM_CHUNKS, SSM_GROUPS // SSM_CHUNKS, SSM_GROUP, SSM_STATE)
    bbr = _block_diag(chunked(bb_re)).astype(BF16)
    bbi = _block_diag(chunked(bb_im)).astype(BF16)
    to_cc = lambda c: _block_diag(chunked(c).transpose(0, 1, 3, 2)).astype(BF16)
    ccr, cci = to_cc(p["c_re"]), to_cc(p["c_im"])

    w_in_r = comm.w_in()
    hb, z = _in_proj(x, g_mix, w_in_r)
    qh, kh, vh, ub, uf, c128 = _attn_prep(z, gq, gk, bf, gg)
    crow = c128[:, :HEADS].T.reshape(HEADS, 1, s)
    (oh, lse), landed = _attn_fwd(qh, kh, vh, crow, comm.gather_first())
    (xr, xi, y), gathered = _ssm_fwd(ub, uf, bbr, bbi, ar, ai, ccr, cci, dsk, comm.gather_second(landed))
    w_glu_b, w_out_b, w_down_b, conv_w_full = comm.weights(gathered)
    (x1, mixb, h2b), passed = _mix_out(y, oh, x, w_glu_b, b_glu, g_att, g_ssm, w_out_b, g_ffn, comm.gather_third(gathered))
    w_up_b = comm.w_up(passed)
    up = _mm(h2b, w_up_b, name="ffn_up", tm=1024, tn=1408, tk=1024)
    act = _conv_act(up, conv_w_full, conv_b)
    dy, dyb, loss_blk = _down_loss(act, w_down_b, x1, tgt)

    d_w_down = _mm(act, dyb, ta=True, name="d_w_down", tm=1408, tn=1024, tk=2048)
    dact = _mm(dyb, w_down_b, tb=True, name="d_act", tm=1024, tn=1408, tk=1024)
    dupb, dcw = _conv_act_bwd(up, dact, conv_w_full, conv_b)
    d_w_up = _mm(h2b, dupb, ta=True, b_parts=2, name="d_w_up", tm=1024, tn=1408, tk=2048)
    dh2 = _mm(dupb, w_up_b, tb=True, a_parts=2, name="d_h2", tm=1024, tn=1024, tk=1408)
    (dx1, dx1b, doh, dys, d_w_glu, d_g_ffn, d_g_att, d_g_ssm, d_b_glu), landed_down = _mix_bwd(
        dy, dh2, x1, g_ffn, w_out_b, y, oh, w_glu_b, b_glu, g_att, g_ssm, comm.swap_down(d_w_down))
    d_w_out = _mm(mixb, dx1b, ta=True, name="d_w_out", tm=1024, tn=1024, tk=2048)
    (du, dbbr, dbbi, dccr, dcci, dar, dai, dd), swapped = _ssm_bwd(dys, uf, ub, xr, xi, bbr, bbi, ar, ai, ccr, cci, dsk,
                                                                comm.swap(landed_down, d_w_up, d_w_glu, d_w_out))
    unchunk = lambda t: t.reshape(_PARAM_SHAPE)
    dbb_re = unchunk(_diag_blocks(dbbr, SSM_GROUP, SSM_STATE))
    dbb_im = unchunk(_diag_blocks(dbbi, SSM_GROUP, SSM_STATE))
    first_row = (jnp.arange(_PARAM_SHAPE[0]) % SSM_GROUP == 0)[:, None]
    da_re = jnp.where(first_row, rep(dar.reshape(SSM_GROUPS, SSM_STATE)), 0.0)
    da_im = jnp.where(first_row, rep(dai.reshape(SSM_GROUPS, SSM_STATE)), 0.0)
    expand_t = (jnp.arange(SSM_GROUPS)[:, None] == (jnp.arange(_PARAM_SHAPE[0]) // SSM_GROUP)[None, :]).astype(BF16)
    d_lr, d_li, d_ls, d_bt_re, d_bt_im = _ssm_params_bwd(lr, li, ls, bt_re, bt_im, da_re, da_im, dbb_re, dbb_im, expand_t)
    from_bt = lambda t: t.reshape(SSM_GROUPS, SSM_GROUP, SSM_STATE).transpose(0, 2, 1)
    from_cc = lambda t: _diag_blocks(t, SSM_STATE, SSM_GROUP).transpose(0, 1, 3, 2).reshape(SSM_GROUPS, SSM_GROUP, SSM_STATE)

    small = {
        "lambda_re": d_lr, "lambda_im": d_li, "log_step": d_ls,
        "b_re": from_bt(d_bt_re), "b_im": from_bt(d_bt_im), "c_re": from_cc(dccr), "c_im": from_cc(dcci),
        "d_skip": dd, "b_glu": d_b_glu, "g_attn_out": d_g_att, "g_ssm_out": d_g_ssm, "g_ffn": d_g_ffn,
        "conv_b": dcw[:, 3],
    }
    d_conv_w = dcw[:, 0:3].transpose(1, 0, 2).reshape(3, 2 * D_FF)
    early_small = _pack_small([small[n] for n in _EARLY_SMALL] + [d_conv_w])

    (dqh, dkh, dvh, dcrow), landed = _attn_bwd(qh, kh, vh, crow, lse, doh, comm.scatter(swapped, early_small))
    early_lands, small_landed = landed[:3], landed[3:]
    dc128 = jnp.pad(dcrow.reshape(HEADS, s).T, ((0, 0), (0, LANES - HEADS)))
    (dzb, d_gq, d_gk, d_bf), small_gathered = _prep_bwd(z, dqh, dkh, dvh, du, dc128, gq, gk, bf, gg,
                                                        comm.small_second(small_landed))
    d_w_in_r = _mm(hb, dzb, ta=True, name="d_w_in", tm=512, tn=Z_COLS, tk=2048)
    dh, swapped_in = _mm(dzb, w_in_r, tb=True, name="d_h", tm=1024, tn=1024, tk=Z_COLS, carry=True,
                         hosted=comm.swap_in(d_w_in_r))
    (dx, d_g_mix), land_in = _in_norm_bwd(x, g_mix, dh, dx1, comm.scatter_in(swapped_in))
    small.update({"g_mix": d_g_mix, "b_f": d_bf[0, :HEADS], "g_q": d_gq.reshape(HEADS, HEAD_DIM).sum(0),
                  "g_k": d_gk.reshape(HEADS, HEAD_DIM).sum(0)})
    big = {"w_in": d_w_in_r, "w_glu": d_w_glu, "w_out": d_w_out, "w_up": d_w_up, "w_down": d_w_down}
    return loss_blk[0, 0], dx, big, small, d_conv_w, (early_lands, land_in, small_gathered, early_small)


def kernel(x, g_mix, w_in, b_f, g_q, g_k, lambda_re, lambda_im, log_step, b_re, b_im, c_re, c_im, d_skip, w_glu, b_glu, g_attn_out, g_ssm_out, w_out, g_ffn, w_up, conv_w, conv_b, w_down, loss_target, m_g_mix, m_w_in, m_b_f, m_g_q, m_g_k, m_lambda_re, m_lambda_im, m_log_step, m_b_re, m_b_im, m_c_re, m_c_im, m_d_skip, m_w_glu, m_b_glu, m_g_attn_out, m_g_ssm_out, m_w_out, m_g_ffn, m_w_up, m_conv_w, m_conv_b, m_w_down, v_g_mix, v_w_in, v_b_f, v_g_q, v_g_k, v_lambda_re, v_lambda_im, v_log_step, v_b_re, v_b_im, v_c_re, v_c_im, v_d_skip, v_w_glu, v_b_glu, v_g_attn_out, v_g_ssm_out, v_w_out, v_g_ffn, v_w_up, v_conv_w, v_conv_b, v_w_down):
    args = dict(locals())
    order = ["g_mix", "w_in", "b_f", "g_q", "g_k", "lambda_re", "lambda_im", "log_step", "b_re", "b_im", "c_re", "c_im",
             "d_skip", "w_glu", "b_glu", "g_attn_out", "g_ssm_out", "w_out", "g_ffn", "w_up", "conv_w", "conv_b", "w_down"]
    comm = _MeshComm(args)
    chip = comm.chip
    loss_part, dx, big, small, d_conv_w, lands = _local_step(x[0], loss_target[0], args, comm)

    g_big = comm.reduce(lands[:2])

    shapes = dict(_SMALL)
    small_names = [n for n, _ in _SMALL]
    device = (2 * chip + comm.core).reshape(1).astype(jnp.int32)
    early = _unpack_small(_sum_devices(_place_block(lands[2][0], lands[3], device)),
                          [shapes[n] for n in _EARLY_SMALL] + [(3, 2 * D_FF)])
    late = _unpack_small(_allreduce_small(_pack_small([small[n] for n in _LATE_SMALL] + [loss_part])),
                         [shapes[n] for n in _LATE_SMALL] + [()])
    loss = late[-1]
    g_conv_w = lax.dynamic_slice_in_dim(early[-1], chip * (2 * D_FF // N_CHIPS), 2 * D_FF // N_CHIPS, axis=1)
    g_small = {**dict(zip(_EARLY_SMALL, early[:-1])), **dict(zip(_LATE_SMALL, late[:-1]))}

    grad, delta, new_m, new_v = {}, {}, {}, {}
    for n in ("w_in", "w_glu", "w_out", "w_up", "w_down"):
        grad[n] = g_big[n]
        delta[n], new_m[n], new_v[n] = _adamw(args[n], g_big[n], args["m_" + n], args["v_" + n], name="adamw_" + n)
    grad["conv_w"] = g_conv_w
    delta["conv_w"], new_m["conv_w"], new_v["conv_w"] = _adamw(conv_w, g_conv_w, m_conv_w, v_conv_w, name="adamw_conv_w")
    stepped = _adamw_small([args[n] for n in small_names], [g_small[n] for n in small_names],
                           [args["m_" + n] for n in small_names], [args["v_" + n] for n in small_names])
    for i, n in enumerate(small_names):
        grad[n] = g_small[n]
        delta[n], new_m[n], new_v[n] = stepped[3 * i:3 * i + 3]

    return (loss, dx[None], *[grad[n] for n in order], *[delta[n] for n in order], *[new_m[n] for n in order],
            *[new_v[n] for n in order])
```

```python
import math

import jax
import jax.numpy as jnp
from jax import lax
from jax.experimental import pallas as pl
from jax.experimental.pallas import tpu as pltpu

F32 = jnp.float32
BF16 = jnp.bfloat16

D_MODEL = 1024
HEADS = 8
HEAD_DIM = 64
ATTN_W = 512
SSM_W = 512
SSM_GROUPS = 32
SSM_GROUP = 16
SSM_STATE = 64
N_STATE = SSM_GROUPS * SSM_STATE
D_FF = 2816
IN_COLS = 2056
Z_COLS = 2176
F_COL0 = 1536
U_COL0 = 1544
EPS = 1e-6
NEG_INF = -1e30
N_CHIPS = 4
LANES = 128
SUBLANES = 8
SSM_CHUNKS = 2
CHUNK_U = SSM_W // SSM_CHUNKS
CHUNK_S = N_STATE // SSM_CHUNKS
HEADS_PER_STEP = 4
STRIP = 128
N_STRIPS = D_FF // STRIP

ADAM_LR = 0.001
ADAM_B1 = 0.9
ADAM_B2 = 0.999
ADAM_EPS = 1e-08
ADAM_WD = 0.01
ADAM_STEP = 10

VMEM_LIMIT = 56 * 1024 * 1024
MESH = pl.DeviceIdType.MESH


def _pallas(body, **kw):
    return pl.pallas_call(body, **kw)


def _pcall(body, *, name, out_shape, in_specs, out_specs, grid=(), scratch_shapes=(), dims=None):
    params = pltpu.CompilerParams(dimension_semantics=dims, vmem_limit_bytes=VMEM_LIMIT)
    return _pallas(body, name=name, grid=grid, in_specs=in_specs, out_specs=out_specs,
                   out_shape=out_shape, scratch_shapes=scratch_shapes, compiler_params=params)


def _sds(shape, dtype=F32):
    return jax.ShapeDtypeStruct(shape, dtype)


def _dot(a, b):
    return jnp.dot(a, b, preferred_element_type=F32)


def _dot_nt(a, b):
    return lax.dot_general(a, b, (((1,), (1,)), ((), ())), preferred_element_type=F32)


def _dot_tn(a, b):
    return lax.dot_general(a, b, (((0,), (0,)), ((), ())), preferred_element_type=F32)


def _split3(x):
    hi = x.astype(BF16)
    r = x - hi.astype(F32)
    mid = r.astype(BF16)
    lo = (r - mid.astype(F32)).astype(BF16)
    return hi, mid, lo


def _dot_exact_r(x, m01):
    hi, mid, lo = _split3(x)
    return _dot(hi, m01) + _dot(mid, m01) + _dot(lo, m01)


def _dot_exact_l(m01, x):
    hi, mid, lo = _split3(x)
    return _dot(m01, hi) + _dot(m01, mid) + _dot(m01, lo)


def _sigmoid(x):
    return 1.0 / (1.0 + jnp.exp(-x))


def _rms(x, g):
    r = lax.rsqrt(jnp.mean(x * x, axis=-1, keepdims=True) + EPS)
    return x * r * g


def _rms_bwd(x, g, dy):
    r = lax.rsqrt(jnp.mean(x * x, axis=-1, keepdims=True) + EPS)
    w = dy * g
    dx = r * w - x * (r * r * r) * jnp.mean(w * x, axis=-1, keepdims=True)
    dg = jnp.sum(dy * x * r, axis=0, keepdims=True)
    return dx, dg


_GELU_K = math.sqrt(2.0 / math.pi)
_GELU_C = 0.044715


def _gelu(y):
    return y * (0.5 * (1.0 + jnp.tanh(_GELU_K * (y + _GELU_C * (y * y * y)))))


def _gelu_grad(y):
    t = jnp.tanh(_GELU_K * (y + _GELU_C * (y * y * y)))
    return 0.5 * (1.0 + t) + 0.5 * y * (1.0 - t * t) * (_GELU_K * (1.0 + 3.0 * _GELU_C * y * y))


def _tile(n, pref):
    if n <= pref:
        return n
    divs = [t for t in range(LANES, n + 1, LANES) if n % t == 0]
    below = [t for t in divs if t <= pref]
    if below and 2 * below[-1] >= pref:
        return below[-1]
    above = [t for t in divs if t > pref]
    return above[0] if above else n


def _row_tile(s):
    return min(256, s)


def _mm(a, b, *, name, tm, tn, tk, ta=False, tb=False, a_parts=1, b_parts=1, carry=False, hosted=None):
    if a_parts > 1:
        m, kk = a.shape[1], a.shape[2] * a_parts
    elif ta:
        kk, m = a.shape
    else:
        m, kk = a.shape
    if b_parts > 1:
        n = b.shape[2] * b_parts
    else:
        n = b.shape[0] if tb else b.shape[1]
    tm, tn, tk = _tile(m, tm), _tile(n // b_parts, tn), _tile(kk // a_parts, tk)
    k_per, n_per = kk // a_parts // tk, n // b_parts // tn

    def body(a_ref, b_ref, o_ref):
        k = pl.program_id(2)
        if ta:
            part = _dot_tn(a_ref[...], b_ref[...])
        elif tb:
            part = _dot_nt(a_ref[...], b_ref[...])
        else:
            part = _dot(a_ref[...], b_ref[...])

        @pl.when(k == 0)
        def _():
            o_ref[...] = part

        @pl.when(k > 0)
        def _():
            o_ref[...] += part

    if a_parts > 1:
        a_spec = pl.BlockSpec((None, tm, tk), lambda i, j, k: (k // k_per, i, k % k_per))
    else:
        a_spec = pl.BlockSpec((tk, tm), lambda i, j, k: (k, i)) if ta else pl.BlockSpec((tm, tk), lambda i, j, k: (i, k))
    if b_parts > 1:
        b_spec = pl.BlockSpec((None, tk, tn), lambda i, j, k: (j // n_per, k, j % n_per))
    else:
        b_spec = pl.BlockSpec((tn, tk), lambda i, j, k: (j, k)) if tb else pl.BlockSpec((tk, tn), lambda i, j, k: (k, j))
    grid = (m // tm, n // tn, kk // tk)
    at = lambda step: (lambda: jnp.logical_and(jnp.logical_and(pl.program_id(0) == step[0], pl.program_id(1) == step[1]),
                                               pl.program_id(2) == step[2]))
    (out,), carried = _host_pcall(body, hosted, at((0, 0, 0)), at(tuple(g - 1 for g in grid)), n_in=2, n_out=1, n_scratch=0,
                                  name=name, grid=grid, in_specs=[a_spec, b_spec],
                                  out_specs=[pl.BlockSpec((tm, tn), lambda i, j, k: (i, j))], out_shape=[_sds((m, n))],
                                  scratch_shapes=[], dims=("parallel", "parallel", "arbitrary"), operands=(a, b))
    return (out, carried) if carry else out


def _in_proj(x, g_mix, w_in_r):
    s = x.shape[0]
    tm = _row_tile(s)

    def body(x_ref, g_ref, w_ref, h_ref, z_ref):
        h = _rms(x_ref[...], g_ref[...]).astype(BF16)
        h_ref[...] = h
        z_ref[...] = _dot(h, w_ref[...])

    return _pcall(body, name="in_proj", grid=(s // tm,),
                  in_specs=[pl.BlockSpec((tm, D_MODEL), lambda i: (i, 0)), pl.BlockSpec((1, D_MODEL), lambda i: (0, 0)),
                            pl.BlockSpec((D_MODEL, Z_COLS), lambda i: (0, 0))],
                  out_specs=[pl.BlockSpec((tm, D_MODEL), lambda i: (i, 0)), pl.BlockSpec((tm, Z_COLS), lambda i: (i, 0))],
                  out_shape=[_sds((s, D_MODEL), BF16), _sds((s, Z_COLS))], dims=("parallel",))(x, g_mix, w_in_r)


def _split_heads(ref, val):
    for h in range(HEADS):
        ref[h] = val[:, h * HEAD_DIM:(h + 1) * HEAD_DIM].astype(ref.dtype)


def _merge_heads(ref):
    return jnp.concatenate([ref[h].astype(F32) for h in range(HEADS)], axis=-1)


def _forget_logits(z_ref, bf_ref):
    fl = z_ref[:, F_COL0:F_COL0 + LANES] + bf_ref[...]
    return jnp.where(lax.broadcasted_iota(jnp.int32, fl.shape, 1) < HEADS, fl, 0.0)


def _attn_prep(z, gq, gk, bf, gg):
    s = z.shape[0]
    tm = _row_tile(s)

    def body(z_ref, gq_ref, gk_ref, bf_ref, gg_ref, qn_ref, kn_ref, vb_ref, ub_ref, uf_ref, c_ref, carry_ref):
        i = pl.program_id(0)

        @pl.when(i == 0)
        def _():
            carry_ref[...] = jnp.zeros_like(carry_ref)

        gg_m = gg_ref[...]

        def head_norm(t, g):
            ssq = _dot_exact_r(t * t, gg_m)
            return t * lax.rsqrt(ssq * (1.0 / HEAD_DIM) + EPS) * g

        _split_heads(qn_ref, head_norm(z_ref[:, 0:ATTN_W], gq_ref[...]))
        _split_heads(kn_ref, head_norm(z_ref[:, ATTN_W:2 * ATTN_W], gk_ref[...]))
        _split_heads(vb_ref, z_ref[:, 2 * ATTN_W:3 * ATTN_W])
        u = z_ref[:, U_COL0:U_COL0 + SSM_W]
        uf_ref[...] = u
        ub_ref[...] = u.astype(BF16)
        fl = _forget_logits(z_ref, bf_ref)
        lf = jnp.minimum(fl, 0.0) - jnp.log1p(jnp.exp(-jnp.abs(fl)))
        row = lax.broadcasted_iota(jnp.int32, (tm, tm), 0)
        col = lax.broadcasted_iota(jnp.int32, (tm, tm), 1)
        tri = (row >= col).astype(BF16)
        c = _dot_exact_l(tri, lf) + carry_ref[...]
        c_ref[...] = c
        carry_ref[...] = c[tm - 1:tm, :]

    row_spec = lambda w: pl.BlockSpec((tm, w), lambda i: (i, 0))
    const = lambda shape: pl.BlockSpec(shape, lambda i: (0, 0))
    heads = pl.BlockSpec((HEADS, tm, HEAD_DIM), lambda i: (0, i, 0))
    return _pcall(body, name="attn_prep", grid=(s // tm,),
                  in_specs=[row_spec(Z_COLS), const((1, ATTN_W)), const((1, ATTN_W)), const((1, LANES)), const((ATTN_W, ATTN_W))],
                  out_specs=[heads] * 3 + [row_spec(SSM_W), row_spec(SSM_W), row_spec(LANES)],
                  out_shape=[_sds((HEADS, s, HEAD_DIM), BF16)] * 3 + [_sds((s, SSM_W), BF16), _sds((s, SSM_W)), _sds((s, LANES))],
                  scratch_shapes=[pltpu.VMEM((1, LANES), F32)], dims=("arbitrary",))(z, gq, gk, bf, gg)


def _attn_fwd(qh, kh, vh, crow, hosted=None):
    _, s, _ = qh.shape
    tq = _row_tile(s)
    scale = HEAD_DIM ** -0.5

    hp = HEADS
    nq = s // tq
    fold = lambda t, op: op(t[:, :tq // 2], t[:, tq // 2:])

    def body(q_ref, k_ref, v_ref, c_ref, o_ref, lse_ref, s_s):
        i = pl.program_id(1)

        def first(j, ms, diagonal):
            off = pl.multiple_of(j * tq, tq)
            out = []
            for hh in range(hp):
                sc = _dot_nt(q_ref[hh], k_ref[hh, pl.ds(off, tq), :]) * scale - c_ref[hh, :, pl.ds(off, tq)]
                if diagonal:
                    causal = lax.broadcasted_iota(jnp.int32, (tq, tq), 1) <= lax.broadcasted_iota(jnp.int32, (tq, tq), 0)
                    sc = jnp.where(causal, sc, NEG_INF)
                s_s[hh, j] = sc
                out.append(jnp.maximum(ms[hh], fold(sc, jnp.maximum)))
            return tuple(out)

        ms = lax.fori_loop(0, i, lambda j, c: first(j, c, False), (jnp.full((tq, tq // 2), NEG_INF, F32),) * hp)
        ms = [jnp.max(t, axis=-1, keepdims=True) for t in first(i, ms, True)]

        def second(j, carry):
            rows = pl.ds(pl.multiple_of(j * tq, tq), tq)
            out = []
            for hh in range(hp):
                ls, acc = carry[hh]
                p = jnp.exp(s_s[hh, j] - ms[hh])
                out.append((ls + fold(p, jnp.add), acc + _dot(p.astype(BF16), v_ref[hh, rows, :])))
            return tuple(out)

        zero = (jnp.zeros((tq, tq // 2), F32), jnp.zeros((tq, HEAD_DIM), F32))
        for hh, (ls, acc) in enumerate(lax.fori_loop(0, i + 1, second, (zero,) * hp)):
            l = jnp.sum(ls, axis=-1, keepdims=True)
            o_ref[hh] = acc / l
            lse_ref[hh] = ms[hh] + jnp.log(l)

    blk = pl.BlockSpec((hp, tq, HEAD_DIM), lambda h, i: (h, i, 0))
    full = pl.BlockSpec((hp, s, HEAD_DIM), lambda h, i: (h, 0, 0))
    nh = HEADS // hp
    first = lambda: jnp.logical_and(pl.program_id(0) == 0, pl.program_id(1) == 0)
    last = lambda: jnp.logical_and(pl.program_id(0) == nh - 1, pl.program_id(1) == nq - 1)
    return _host_pcall(body, hosted, first, last, n_in=4, n_out=2, n_scratch=1, name="attn_fwd", grid=(nh, nq),
                       in_specs=[blk, full, full, pl.BlockSpec((hp, 1, s), lambda h, i: (h, 0, 0))],
                       out_specs=[blk, pl.BlockSpec((hp, tq, 1), lambda h, i: (h, i, 0))],
                       out_shape=[_sds((HEADS, s, HEAD_DIM)), _sds((HEADS, s, 1))],
                       scratch_shapes=[pltpu.VMEM((hp, nq, tq, tq), F32)],
                       dims=("parallel", "parallel"), operands=(qh, kh, vh, crow))


def _ssm_param_fn(lr, li, ls, br, bi):
    step = jnp.exp(ls)
    er = jnp.exp(lr * step)
    ab_re = er * jnp.cos(li * step)
    ab_im = er * jnp.sin(li * step)
    num_re = ab_re - 1.0
    num_im = ab_im
    den = lr * lr + li * li
    f_re = (num_re * lr + num_im * li) / den
    f_im = (num_im * lr - num_re * li) / den
    bb_re = f_re * br - f_im * bi
    bb_im = f_re * bi + f_im * br
    return ab_re, ab_im, bb_re, bb_im


_PARAM_SHAPE = (SSM_GROUPS * SSM_GROUP, SSM_STATE)


def _ssm_params(lr, li, ls, br, bi):
    def body(lr_ref, li_ref, ls_ref, br_ref, bi_ref, ar_ref, ai_ref, bbr_ref, bbi_ref):
        ar, ai, bbr, bbi = _ssm_param_fn(lr_ref[...], li_ref[...], ls_ref[...], br_ref[...], bi_ref[...])
        ar_ref[...] = ar
        ai_ref[...] = ai
        bbr_ref[...] = bbr
        bbi_ref[...] = bbi

    spec = pl.BlockSpec(_PARAM_SHAPE, lambda: (0, 0))
    return _pcall(body, name="ssm_params", in_specs=[spec] * 5, out_specs=[spec] * 4,
                  out_shape=[_sds(_PARAM_SHAPE)] * 4)(lr, li, ls, br, bi)


def _ssm_params_bwd(lr, li, ls, br, bi, dar, dai, dbbr, dbbi, expand_t):
    def body(lr_ref, li_ref, ls_ref, br_ref, bi_ref, dar_ref, dai_ref, dbbr_ref, dbbi_ref, et_ref,
             dlr_ref, dli_ref, dls_ref, dbr_ref, dbi_ref):
        _, vjp = jax.vjp(_ssm_param_fn, lr_ref[...], li_ref[...], ls_ref[...], br_ref[...], bi_ref[...])
        dlr, dli, dls, dbr, dbi = vjp((dar_ref[...], dai_ref[...], dbbr_ref[...], dbbi_ref[...]))
        et = et_ref[...]
        dlr_ref[...] = _dot_exact_l(et, dlr)
        dli_ref[...] = _dot_exact_l(et, dli)
        dls_ref[...] = jnp.sum(_dot_exact_l(et, dls), axis=-1, keepdims=True)
        dbr_ref[...] = dbr
        dbi_ref[...] = dbi

    spec = pl.BlockSpec(_PARAM_SHAPE, lambda: (0, 0))
    gspec = pl.BlockSpec((SSM_GROUPS, SSM_STATE), lambda: (0, 0))
    return _pcall(body, name="ssm_params_bwd",
                  in_specs=[spec] * 9 + [pl.BlockSpec((SSM_GROUPS, _PARAM_SHAPE[0]), lambda: (0, 0))],
                  out_specs=[gspec, gspec, pl.BlockSpec((SSM_GROUPS, 1), lambda: (0, 0)), spec, spec],
                  out_shape=[_sds((SSM_GROUPS, SSM_STATE))] * 2 + [_sds((SSM_GROUPS, 1))] + [_sds(_PARAM_SHAPE)] * 2,
                  )(lr, li, ls, br, bi, dar, dai, dbbr, dbbi, expand_t)


def _cmul(ar, ai, br, bi):
    return ar * br - ai * bi, ar * bi + ai * br


def _scan_consts(ar, ai, width, reverse):
    row = lax.broadcasted_iota(jnp.int32, (SUBLANES, width), 0)
    pw = [(ar, ai)]
    for _ in range(SUBLANES - 1):
        pw.append(_cmul(pw[-1][0], pw[-1][1], ar, ai))
    steps = []
    for d in (1, 2, 4):
        keep = (row < SUBLANES - d) if reverse else (row >= d)
        steps.append((d, jnp.where(keep, pw[d - 1][0], 0.0), jnp.where(keep, pw[d - 1][1], 0.0)))
    pr = jnp.zeros((SUBLANES, width), F32)
    pi = jnp.zeros((SUBLANES, width), F32)
    for r in range(SUBLANES):
        e = (SUBLANES - r) if reverse else (r + 1)
        pr = jnp.where(row == r, pw[e - 1][0], pr)
        pi = jnp.where(row == r, pw[e - 1][1], pi)
    return steps, pr, pi


def _scan_tile(xr, xi, cr, ci, consts, reverse):
    steps, pr, pi = consts
    for d, mr, mi in steps:
        sh = (SUBLANES - d) if reverse else d
        sr = pltpu.roll(xr, sh, 0)
        si = pltpu.roll(xi, sh, 0)
        xr, xi = xr + mr * sr - mi * si, xi + mr * si + mi * sr
    return xr + pr * cr - pi * ci, xi + pr * ci + pi * cr


def _ssm_fwd(ub, uf, bbr, bbi, ar, ai, ccr, cci, dsk, hosted=None):
    s = ub.shape[0]
    tm = _row_tile(s)
    nt = tm // SUBLANES

    def body(ub_ref, u_ref, bbr_ref, bbi_ref, ar_ref, ai_ref, ccr_ref, cci_ref, dsk_ref,
             xr_ref, xi_ref, y_ref, cr_s, ci_s):
        i = pl.program_id(1)

        @pl.when(i == 0)
        def _():
            cr_s[...] = jnp.zeros_like(cr_s)
            ci_s[...] = jnp.zeros_like(ci_s)

        u_b = ub_ref[...]
        xr_ref[...] = _dot(u_b, bbr_ref[0])
        xi_ref[...] = _dot(u_b, bbi_ref[0])
        consts = _scan_consts(ar_ref[0], ai_ref[0], CHUNK_S, False)

        def tile(k, carry):
            cr, ci = carry
            sl = pl.ds(pl.multiple_of(k * SUBLANES, SUBLANES), SUBLANES)
            xr, xi = _scan_tile(xr_ref[sl, :], xi_ref[sl, :], cr, ci, consts, False)
            xr_ref[sl, :] = xr
            xi_ref[sl, :] = xi
            return xr[SUBLANES - 1:SUBLANES, :], xi[SUBLANES - 1:SUBLANES, :]

        cr, ci = lax.fori_loop(0, nt, tile, (cr_s[...], ci_s[...]))
        cr_s[...] = cr
        ci_s[...] = ci
        y_ref[...] = (_dot(xr_ref[...].astype(BF16), ccr_ref[0]) - _dot(xi_ref[...].astype(BF16), cci_ref[0])
                      + dsk_ref[...] * u_ref[...])

    wspec = lambda a, b: pl.BlockSpec((1, a, b), lambda j, i: (j, 0, 0))
    nb = s // tm
    first = lambda: jnp.logical_and(pl.program_id(0) == 0, pl.program_id(1) == 0)
    last = lambda: jnp.logical_and(pl.program_id(0) == SSM_CHUNKS - 1, pl.program_id(1) == nb - 1)
    return _host_pcall(
        body, hosted, first, last, n_in=9, n_out=3, n_scratch=2, name="ssm_fwd", grid=(SSM_CHUNKS, nb),
        in_specs=[pl.BlockSpec((tm, CHUNK_U), lambda j, i: (i, j)),
                  pl.BlockSpec((tm, CHUNK_U), lambda j, i: (i, j)),
                  wspec(CHUNK_U, CHUNK_S), wspec(CHUNK_U, CHUNK_S), wspec(1, CHUNK_S), wspec(1, CHUNK_S),
                  wspec(CHUNK_S, CHUNK_U), wspec(CHUNK_S, CHUNK_U),
                  pl.BlockSpec((1, CHUNK_U), lambda j, i: (0, j))],
        out_specs=[pl.BlockSpec((tm, CHUNK_S), lambda j, i: (i, j)), pl.BlockSpec((tm, CHUNK_S), lambda j, i: (i, j)),
                   pl.BlockSpec((tm, CHUNK_U), lambda j, i: (i, j))],
        out_shape=[_sds((s, N_STATE)), _sds((s, N_STATE)), _sds((s, SSM_W))],
        scratch_shapes=[pltpu.VMEM((1, CHUNK_S), F32)] * 2,
        dims=("parallel", "arbitrary"), operands=(ub, uf, bbr, bbi, ar, ai, ccr, cci, dsk))


def _ssm_glu(y, w_glu, b_glu):
    ge = _gelu(y)
    sg = _sigmoid(_dot(ge.astype(BF16), w_glu) + b_glu)
    return ge, sg


def _mix_out(y, att, x, w_glu, b_glu, g_att, g_ssm, w_out, g_ffn, hosted=None):
    s = x.shape[0]
    tm = _row_tile(s)

    def body(y_ref, att_ref, x_ref, wg_ref, bg_ref, ga_ref, gs_ref, wo_ref, gf_ref, x1_ref, mix_ref, h2_ref):
        ge, sg = _ssm_glu(y_ref[...], wg_ref[...], bg_ref[...])
        ms = _rms(ge * sg, gs_ref[...]).astype(BF16)
        ma = _rms(_merge_heads(att_ref), ga_ref[...]).astype(BF16)
        mix_ref[:, 0:ATTN_W] = ma
        mix_ref[:, ATTN_W:D_MODEL] = ms
        x1 = x_ref[...] + (_dot(ma, wo_ref[0:ATTN_W, :]) + _dot(ms, wo_ref[ATTN_W:D_MODEL, :]))
        x1_ref[...] = x1
        h2_ref[...] = _rms(x1, gf_ref[...]).astype(BF16)

    row = lambda w: pl.BlockSpec((tm, w), lambda i: (i, 0))
    const = lambda a, b: pl.BlockSpec((a, b), lambda i: (0, 0))
    nb = s // tm
    return _host_pcall(body, hosted, lambda: pl.program_id(0) == 0, lambda: pl.program_id(0) == nb - 1,
                       n_in=9, n_out=3, n_scratch=0, name="mix_out", grid=(nb,),
                       in_specs=[row(SSM_W), pl.BlockSpec((HEADS, tm, HEAD_DIM), lambda i: (0, i, 0)), row(D_MODEL),
                                 const(SSM_W, SSM_W), const(1, SSM_W),
                                 const(1, ATTN_W), const(1, SSM_W), const(D_MODEL, D_MODEL), const(1, D_MODEL)],
                       out_specs=[row(D_MODEL)] * 3,
                       out_shape=[_sds((s, D_MODEL)), _sds((s, D_MODEL), BF16), _sds((s, D_MODEL), BF16)],
                       scratch_shapes=[], dims=("parallel",), operands=(y, att, x, w_glu, b_glu, g_att, g_ssm, w_out, g_ffn))


CONV_CHUNK = 64


def _conv_rows(pad_ref, w, b, r0, n):
    y = b + pad_ref[pl.ds(r0 + SUBLANES - 2, n), :] * w[0:1, :]
    y = y + pad_ref[pl.ds(r0 + SUBLANES - 1, n), :] * w[1:2, :]
    return y + pad_ref[pl.ds(r0 + SUBLANES, n), :] * w[2:3, :]


def _fill_front_pad(pad_ref, strip_ref, s):
    pad_ref[0:SUBLANES, :] = jnp.zeros((SUBLANES, STRIP), F32)
    for r0 in range(0, s, CONV_CHUNK):
        pad_ref[pl.ds(SUBLANES + r0, CONV_CHUNK), :] = strip_ref[pl.ds(r0, CONV_CHUNK), :]


def _conv_act(up, conv_w, conv_b):
    s = up.shape[0]

    def body(ug_ref, uv_ref, wg_ref, wv_ref, bg_ref, bv_ref, act_ref, pg_ref, pv_ref):
        _fill_front_pad(pg_ref, ug_ref, s)
        _fill_front_pad(pv_ref, uv_ref, s)
        wg, wv, bg, bv = wg_ref[...], wv_ref[...], bg_ref[...], bv_ref[...]
        for r0 in range(0, s, CONV_CHUNK):
            hg = _conv_rows(pg_ref, wg, bg, r0, CONV_CHUNK)
            hv = _conv_rows(pv_ref, wv, bv, r0, CONV_CHUNK)
            act_ref[pl.ds(r0, CONV_CHUNK), :] = (hg * _sigmoid(hg) * hv).astype(BF16)

    strip = lambda off: pl.BlockSpec((s, STRIP), lambda j: (0, j + off))
    wsp = lambda off: pl.BlockSpec((3, STRIP), lambda j: (0, j + off))
    bsp = lambda off: pl.BlockSpec((1, STRIP), lambda j: (0, j + off))
    return _pcall(body, name="conv_act", grid=(N_STRIPS,),
                  in_specs=[strip(0), strip(N_STRIPS), wsp(0), wsp(N_STRIPS), bsp(0), bsp(N_STRIPS)],
                  out_specs=pl.BlockSpec((s, STRIP), lambda j: (0, j)), out_shape=_sds((s, D_FF), BF16),
                  scratch_shapes=[pltpu.VMEM((s + SUBLANES, STRIP), F32)] * 2,
                  dims=("parallel",))(up, up, conv_w, conv_w, conv_b, conv_b)


def _down_loss(act, w_down, x1, tgt):
    s = x1.shape[0]
    tm = _row_tile(s)

    def body(a_ref, w_ref, x1_ref, t_ref, dy_ref, dyb_ref, loss_ref):
        i = pl.program_id(0)

        @pl.when(i == 0)
        def _():
            loss_ref[...] = jnp.zeros_like(loss_ref)

        diff = x1_ref[...] + _dot(a_ref[...], w_ref[...]) - t_ref[...]
        dy = diff * (1.0 / D_MODEL)
        dy_ref[...] = dy
        dyb_ref[...] = dy.astype(BF16)
        loss_ref[...] += 0.5 * jnp.sum(diff * dy)

    row = lambda w: pl.BlockSpec((tm, w), lambda i: (i, 0))
    return _pcall(body, name="down_loss", grid=(s // tm,),
                  in_specs=[row(D_FF), pl.BlockSpec((D_FF, D_MODEL), lambda i: (0, 0)), row(D_MODEL), row(D_MODEL)],
                  out_specs=[row(D_MODEL), row(D_MODEL), pl.BlockSpec((SUBLANES, LANES), lambda i: (0, 0))],
                  out_shape=[_sds((s, D_MODEL)), _sds((s, D_MODEL), BF16), _sds((SUBLANES, LANES))],
                  dims=("arbitrary",))(act, w_down, x1, tgt)


def _conv_act_bwd(up, dact, conv_w, conv_b):
    s = up.shape[0]
    ch = CONV_CHUNK

    def body(ug_ref, uv_ref, da_ref, wg_ref, wv_ref, bg_ref, bv_ref, dup_ref, dcw_ref, pg_ref, pv_ref, dg_ref, dv_ref):
        _fill_front_pad(pg_ref, ug_ref, s)
        _fill_front_pad(pv_ref, uv_ref, s)
        zero = jnp.zeros((SUBLANES, STRIP), F32)
        dg_ref[pl.ds(s, SUBLANES), :] = zero
        dv_ref[pl.ds(s, SUBLANES), :] = zero
        wg, wv, bg, bv = wg_ref[...], wv_ref[...], bg_ref[...], bv_ref[...]
        tile_sum = lambda t: jnp.sum(t.reshape(ch // SUBLANES, SUBLANES, STRIP), axis=0)
        accs = [[zero] * 4, [zero] * 4]
        for r0 in range(0, s, ch):
            hg = _conv_rows(pg_ref, wg, bg, r0, ch)
            hv = _conv_rows(pv_ref, wv, bv, r0, ch)
            sg = _sigmoid(hg)
            da = da_ref[pl.ds(r0, ch), :]
            dhs = (da * hv * (sg * (1.0 + hg * (1.0 - sg))), da * (hg * sg))
            for half, (dh, d_ref, p_ref) in enumerate(zip(dhs, (dg_ref, dv_ref), (pg_ref, pv_ref))):
                d_ref[pl.ds(r0, ch), :] = dh
                for k in range(3):
                    accs[half][k] = accs[half][k] + tile_sum(dh * p_ref[pl.ds(r0 + SUBLANES - 2 + k, ch), :])
                accs[half][3] = accs[half][3] + tile_sum(dh)
        for half, (d_ref, w) in enumerate(((dg_ref, wg), (dv_ref, wv))):
            for r0 in range(0, s, ch):
                dup = (d_ref[pl.ds(r0, ch), :] * w[2:3, :] + d_ref[pl.ds(r0 + 1, ch), :] * w[1:2, :]
                       + d_ref[pl.ds(r0 + 2, ch), :] * w[0:1, :])
                dup_ref[half, pl.ds(r0, ch), :] = dup.astype(BF16)
            rid = lax.broadcasted_iota(jnp.int32, (SUBLANES, STRIP), 0)
            out = zero
            for k in range(4):
                out = jnp.where(rid == k, jnp.sum(accs[half][k], axis=0, keepdims=True), out)
            dcw_ref[half] = out

    strip = lambda off: pl.BlockSpec((s, STRIP), lambda j: (0, j + off))
    wsp = lambda off: pl.BlockSpec((3, STRIP), lambda j: (0, j + off))
    bsp = lambda off: pl.BlockSpec((1, STRIP), lambda j: (0, j + off))
    return _pcall(body, name="conv_act_bwd", grid=(N_STRIPS,),
                  in_specs=[strip(0), strip(N_STRIPS), strip(0), wsp(0), wsp(N_STRIPS), bsp(0), bsp(N_STRIPS)],
                  out_specs=[pl.BlockSpec((2, s, STRIP), lambda j: (0, 0, j)), pl.BlockSpec((2, SUBLANES, STRIP), lambda j: (0, 0, j))],
                  out_shape=[_sds((2, s, D_FF), BF16), _sds((2, SUBLANES, D_FF))],
                  scratch_shapes=[pltpu.VMEM((s + SUBLANES, STRIP), F32)] * 4,
                  dims=("parallel",))(up, up, dact, conv_w, conv_w, conv_b, conv_b)


def _mix_bwd(dy, dh2, x1, g_ffn, w_out, y, att, w_glu, b_glu, g_att, g_ssm, hosted=None):
    s = dy.shape[0]
    tm = _row_tile(s)

    def body(dy_ref, dh2_ref, x1_ref, gf_ref, wo_ref, y_ref, att_ref, wg_ref, bg_ref, ga_ref, gs_ref,
             dx1_ref, dx1b_ref, datt_ref, dys_ref, dwg_ref, dgf_ref, dga_ref, dgs_ref, dbg_ref):
        i = pl.program_id(0)

        @pl.when(i == 0)
        def _():
            for r in (dwg_ref, dgf_ref, dga_ref, dgs_ref, dbg_ref):
                r[...] = jnp.zeros_like(r)

        dxn, dgf = _rms_bwd(x1_ref[...], gf_ref[...], dh2_ref[...])
        dx1 = dy_ref[...] + dxn
        dx1_ref[...] = dx1
        dx1b = dx1.astype(BF16)
        dx1b_ref[...] = dx1b
        dgf_ref[...] += dgf
        dma = _dot_nt(dx1b, wo_ref[0:ATTN_W, :])
        dms = _dot_nt(dx1b, wo_ref[ATTN_W:D_MODEL, :])
        datt, dga = _rms_bwd(_merge_heads(att_ref), ga_ref[...], dma)
        _split_heads(datt_ref, datt)
        dga_ref[...] += dga
        yv = y_ref[...]
        ge, sg = _ssm_glu(yv, wg_ref[...], bg_ref[...])
        dssm, dgs = _rms_bwd(ge * sg, gs_ref[...], dms)
        dgs_ref[...] += dgs
        dgl = dssm * ge * sg * (1.0 - sg)
        dglb = dgl.astype(BF16)
        dge = dssm * sg + _dot_nt(dglb, wg_ref[...])
        dbg_ref[...] += jnp.sum(dgl, axis=0, keepdims=True)
        dwg_ref[...] += _dot_tn(ge.astype(BF16), dglb)
        dys_ref[...] = dge * _gelu_grad(yv)

    row = lambda w: pl.BlockSpec((tm, w), lambda i: (i, 0))
    const = lambda a, b: pl.BlockSpec((a, b), lambda i: (0, 0))
    heads = pl.BlockSpec((HEADS, tm, HEAD_DIM), lambda i: (0, i, 0))
    nb = s // tm
    return _host_pcall(
        body, hosted, lambda: pl.program_id(0) == 0, lambda: pl.program_id(0) == nb - 1, n_in=11, n_out=9, n_scratch=0,
        name="mix_bwd", grid=(nb,),
        in_specs=[row(D_MODEL), row(D_MODEL), row(D_MODEL), const(1, D_MODEL), const(D_MODEL, D_MODEL), row(SSM_W),
                  heads, const(SSM_W, SSM_W), const(1, SSM_W), const(1, ATTN_W), const(1, SSM_W)],
        out_specs=[row(D_MODEL), row(D_MODEL), heads, row(SSM_W), const(SSM_W, SSM_W), const(1, D_MODEL),
                   const(1, ATTN_W), const(1, SSM_W), const(1, SSM_W)],
        out_shape=[_sds((s, D_MODEL)), _sds((s, D_MODEL), BF16), _sds((HEADS, s, HEAD_DIM)), _sds((s, SSM_W)),
                   _sds((SSM_W, SSM_W)), _sds((1, D_MODEL)), _sds((1, ATTN_W)), _sds((1, SSM_W)), _sds((1, SSM_W))],
        scratch_shapes=[], dims=("arbitrary",), operands=(dy, dh2, x1, g_ffn, w_out, y, att, w_glu, b_glu, g_att, g_ssm))


def _ssm_bwd(dys, uf, ub, xr, xi, bbr, bbi, ar, ai, ccr, cci, dsk, hosted=None):
    s = dys.shape[0]
    tm = _row_tile(s)
    nb = s // tm
    nt = tm // SUBLANES

    def body(dy_ref, u_ref, ub_ref, xr_ref, xi_ref, xrp_ref, xip_ref, bbr_ref, bbi_ref, ar_ref, ai_ref, ccr_ref,
             cci_ref, dsk_ref, du_ref, dbbr_ref, dbbi_ref, dccr_ref, dcci_ref, dar_ref, dai_ref, dd_ref,
             gr_s, gi_s, cr_s, ci_s, accr_s, acci_s):
        i = pl.program_id(1)
        first_block = i == nb - 1

        @pl.when(i == 0)
        def _():
            for r in (cr_s, ci_s, accr_s, acci_s, dbbr_ref, dbbi_ref, dccr_ref, dcci_ref, dd_ref):
                r[...] = jnp.zeros_like(r)

        dy = dy_ref[...]
        dyb = dy.astype(BF16)
        gr_s[...] = _dot_nt(dyb, ccr_ref[0])
        gi_s[...] = -_dot_nt(dyb, cci_ref[0])
        consts = _scan_consts(ar_ref[0], -ai_ref[0], CHUNK_S, True)
        row = lax.broadcasted_iota(jnp.int32, (SUBLANES, CHUNK_S), 0)

        def tile(kk, carry):
            cr, ci, accr, acci = carry
            k = nt - 1 - kk
            sl = pl.ds(pl.multiple_of(k * SUBLANES, SUBLANES), SUBLANES)
            gr, gi = _scan_tile(gr_s[sl, :], gi_s[sl, :], cr, ci, consts, True)
            gr_s[sl, :] = gr
            gi_s[sl, :] = gi
            slp = pl.ds(pl.multiple_of(jnp.maximum(k - 1, 0) * SUBLANES, SUBLANES), SUBLANES)
            inner = k > 0
            pr_t = jnp.where(inner, xr_ref[slp, :], xrp_ref[...])
            pi_t = jnp.where(inner, xi_ref[slp, :], xip_ref[...])
            live = jnp.logical_or(inner, jnp.logical_not(first_block))
            top_r = jnp.where(live, pltpu.roll(pr_t, 1, 0), 0.0)
            top_i = jnp.where(live, pltpu.roll(pi_t, 1, 0), 0.0)
            xpr = jnp.where(row == 0, top_r, pltpu.roll(xr_ref[sl, :], 1, 0))
            xpi = jnp.where(row == 0, top_i, pltpu.roll(xi_ref[sl, :], 1, 0))
            accr = accr + gr * xpr + gi * xpi
            acci = acci + gi * xpr - gr * xpi
            return gr[0:1, :], gi[0:1, :], accr, acci

        zeros = jnp.zeros((SUBLANES, CHUNK_S), F32)
        cr, ci, accr, acci = lax.fori_loop(0, nt, tile, (cr_s[...], ci_s[...], zeros, zeros))
        cr_s[...] = cr
        ci_s[...] = ci
        accr_s[...] += accr
        acci_s[...] += acci
        grb = gr_s[...].astype(BF16)
        gib = gi_s[...].astype(BF16)
        u_b = ub_ref[...]
        du_ref[...] = _dot_nt(grb, bbr_ref[0]) + _dot_nt(gib, bbi_ref[0]) + dsk_ref[...] * dy
        dbbr_ref[0] += _dot_tn(u_b, grb)
        dbbi_ref[0] += _dot_tn(u_b, gib)
        dccr_ref[0] += _dot_tn(xr_ref[...].astype(BF16), dyb)
        dcci_ref[0] -= _dot_tn(xi_ref[...].astype(BF16), dyb)
        dd_ref[...] += jnp.sum(dy * u_ref[...], axis=0, keepdims=True)

        @pl.when(i == nb - 1)
        def _():
            dar_ref[0] = jnp.sum(accr_s[...], axis=0, keepdims=True)
            dai_ref[0] = jnp.sum(acci_s[...], axis=0, keepdims=True)

    tiles_per_block = tm // SUBLANES
    rb = lambda i: nb - 1 - i
    wspec = lambda a, b: pl.BlockSpec((1, a, b), lambda j, i: (j, 0, 0))
    xblk = pl.BlockSpec((tm, CHUNK_S), lambda j, i: (rb(i), j))
    xprev = pl.BlockSpec((SUBLANES, CHUNK_S), lambda j, i: (jnp.maximum(rb(i) * tiles_per_block - 1, 0), j))
    ublk = pl.BlockSpec((tm, CHUNK_U), lambda j, i: (rb(i), j))
    first = lambda: jnp.logical_and(pl.program_id(0) == 0, pl.program_id(1) == 0)
    last = lambda: jnp.logical_and(pl.program_id(0) == SSM_CHUNKS - 1, pl.program_id(1) == nb - 1)
    return _host_pcall(
        body, hosted, first, last, n_in=14, n_out=8, n_scratch=6, name="ssm_bwd", grid=(SSM_CHUNKS, nb),
        in_specs=[ublk, ublk, ublk, xblk, xblk, xprev, xprev,
                  wspec(CHUNK_U, CHUNK_S), wspec(CHUNK_U, CHUNK_S), wspec(1, CHUNK_S), wspec(1, CHUNK_S),
                  wspec(CHUNK_S, CHUNK_U), wspec(CHUNK_S, CHUNK_U), pl.BlockSpec((1, CHUNK_U), lambda j, i: (0, j))],
        out_specs=[ublk, wspec(CHUNK_U, CHUNK_S), wspec(CHUNK_U, CHUNK_S), wspec(CHUNK_S, CHUNK_U),
                   wspec(CHUNK_S, CHUNK_U), wspec(1, CHUNK_S), wspec(1, CHUNK_S),
                   pl.BlockSpec((1, CHUNK_U), lambda j, i: (0, j))],
        out_shape=[_sds((s, SSM_W)), _sds((SSM_CHUNKS, CHUNK_U, CHUNK_S)), _sds((SSM_CHUNKS, CHUNK_U, CHUNK_S)),
                   _sds((SSM_CHUNKS, CHUNK_S, CHUNK_U)), _sds((SSM_CHUNKS, CHUNK_S, CHUNK_U)),
                   _sds((SSM_CHUNKS, 1, CHUNK_S)), _sds((SSM_CHUNKS, 1, CHUNK_S)), _sds((1, SSM_W))],
        scratch_shapes=[pltpu.VMEM((tm, CHUNK_S), F32)] * 2 + [pltpu.VMEM((1, CHUNK_S), F32)] * 2
                       + [pltpu.VMEM((SUBLANES, CHUNK_S), F32)] * 2,
        dims=("parallel", "arbitrary"), operands=(dys, uf, ub, xr, xi, xr, xi, bbr, bbi, ar, ai, ccr, cci, dsk))


def _attn_probs(q, ks, cs, lse, scale, diagonal):
    p = jnp.exp(_dot_nt(q, ks) * scale - cs - lse)
    if diagonal:
        tq, tk = p.shape
        causal = lax.broadcasted_iota(jnp.int32, (tq, tk), 1) <= lax.broadcasted_iota(jnp.int32, (tq, tk), 0)
        p = jnp.where(causal, p, 0.0)
    return p


def _attn_bwd(qh, kh, vh, crow, lse, doh, hosted=None):
    _, s, _ = qh.shape
    tq = _row_tile(s)
    nq = s // tq
    scale = HEAD_DIM ** -0.5
    hp = HEADS_PER_STEP

    def body(q_ref, k_ref, v_ref, c_ref, lse_ref, do_ref, dq_ref, dk_ref, dv_ref, dc_ref, p_s, dp_s):
        i = pl.program_id(1)

        @pl.when(i == 0)
        def _():
            for r in (dk_ref, dv_ref, dc_ref):
                r[...] = jnp.zeros_like(r)

        dobs = [do_ref[hh].astype(BF16) for hh in range(hp)]

        def first(j, dls, diagonal):
            off = pl.multiple_of(j * tq, tq)
            out = []
            for hh in range(hp):
                p = _attn_probs(q_ref[hh], k_ref[hh, pl.ds(off, tq), :], c_ref[hh, :, pl.ds(off, tq)], lse_ref[hh],
                                scale, diagonal)
                dp = _dot_nt(dobs[hh], v_ref[hh, pl.ds(off, tq), :])
                p_s[hh, j] = p
                dp_s[hh, j] = dp
                out.append(dls[hh] + jnp.sum(p * dp, axis=-1, keepdims=True))
            return tuple(out)

        zero_col = jnp.zeros((tq, 1), F32)
        dls = lax.fori_loop(0, i, lambda j, c: first(j, c, False), (zero_col,) * hp)
        dls = first(i, dls, True)

        def second(j, dqs):
            rows = pl.ds(pl.multiple_of(j * tq, tq), tq)
            out = []
            for hh in range(hp):
                p = p_s[hh, j]
                ds = p * (dp_s[hh, j] - dls[hh])
                dsb = ds.astype(BF16)
                dv_ref[hh, rows, :] += _dot_tn(p.astype(BF16), dobs[hh])
                dk_ref[hh, rows, :] += _dot_tn(dsb, q_ref[hh]) * scale
                dc_ref[hh, :, rows] -= jnp.sum(ds, axis=0, keepdims=True)
                out.append(dqs[hh] + _dot(dsb, k_ref[hh, rows, :]))
            return tuple(out)

        dqs = lax.fori_loop(0, i + 1, second, (jnp.zeros((tq, HEAD_DIM), F32),) * hp)
        for hh in range(hp):
            dq_ref[hh] = dqs[hh] * scale

    blk = pl.BlockSpec((hp, tq, HEAD_DIM), lambda h, i: (h, i, 0))
    full = pl.BlockSpec((hp, s, HEAD_DIM), lambda h, i: (h, 0, 0))
    crow_spec = pl.BlockSpec((hp, 1, s), lambda h, i: (h, 0, 0))
    nh = HEADS // hp
    first = lambda: jnp.logical_and(pl.program_id(0) == 0, pl.program_id(1) == 0)
    last = lambda: jnp.logical_and(pl.program_id(0) == nh - 1, pl.program_id(1) == nq - 1)
    return _host_pcall(body, hosted, first, last, n_in=6, n_out=4, n_scratch=2, name="attn_bwd", grid=(nh, nq),
                       in_specs=[blk, full, full, crow_spec, pl.BlockSpec((hp, tq, 1), lambda h, i: (h, i, 0)), blk],
                       out_specs=[blk, full, full, crow_spec],
                       out_shape=[_sds((HEADS, s, HEAD_DIM))] * 3 + [_sds((HEADS, 1, s))],
                       scratch_shapes=[pltpu.VMEM((hp, nq, tq, tq), F32)] * 2,
                       dims=("parallel", "arbitrary"), operands=(qh, kh, vh, crow, lse, doh))


def _prep_bwd(z, dqn, dkn, dv, du, dc, gq, gk, bf, gg, hosted=None):
    s = z.shape[0]
    tm = _row_tile(s)
    nb = s // tm

    def body(z_ref, dqn_ref, dkn_ref, dv_ref, du_ref, dc_ref, gq_ref, gk_ref, bf_ref, gg_ref,
             dz_ref, dgq_ref, dgk_ref, dbf_ref, carry_ref):
        i = pl.program_id(0)

        @pl.when(i == 0)
        def _():
            for r in (dgq_ref, dgk_ref, dbf_ref, carry_ref):
                r[...] = jnp.zeros_like(r)

        gg_m = gg_ref[...]

        def head_norm_bwd(t, g, dn):
            r = lax.rsqrt(_dot_exact_r(t * t, gg_m) * (1.0 / HEAD_DIM) + EPS)
            w = dn * g
            mean_wt = _dot_exact_r(w * t, gg_m) * (1.0 / HEAD_DIM)
            return r * w - t * (r * r * r) * mean_wt, jnp.sum(dn * t * r, axis=0, keepdims=True)

        dq, dgq = head_norm_bwd(z_ref[:, 0:ATTN_W], gq_ref[...], _merge_heads(dqn_ref))
        dk, dgk = head_norm_bwd(z_ref[:, ATTN_W:2 * ATTN_W], gk_ref[...], _merge_heads(dkn_ref))
        dgq_ref[...] += dgq
        dgk_ref[...] += dgk
        row = lax.broadcasted_iota(jnp.int32, (tm, tm), 0)
        col = lax.broadcasted_iota(jnp.int32, (tm, tm), 1)
        triu = (col >= row).astype(BF16)
        dlf = _dot_exact_l(triu, dc_ref[...]) + carry_ref[...]
        carry_ref[...] = dlf[0:1, :]
        df = dlf * _sigmoid(-_forget_logits(z_ref, bf_ref))
        dbf_ref[...] += jnp.sum(df, axis=0, keepdims=True)
        dz_ref[:, 0:ATTN_W] = dq.astype(BF16)
        dz_ref[:, ATTN_W:2 * ATTN_W] = dk.astype(BF16)
        dz_ref[:, 2 * ATTN_W:3 * ATTN_W] = _merge_heads(dv_ref).astype(BF16)
        tail = jnp.concatenate([df[:, :HEADS], du_ref[...], jnp.zeros((tm, Z_COLS - IN_COLS), F32)], axis=-1)
        dz_ref[:, F_COL0:Z_COLS] = tail.astype(BF16)

    row_spec = lambda w: pl.BlockSpec((tm, w), lambda i: (nb - 1 - i, 0))
    const = lambda shape: pl.BlockSpec(shape, lambda i: (0, 0))
    return _host_pcall(
        body, hosted, lambda: pl.program_id(0) == 0, lambda: pl.program_id(0) == nb - 1, n_in=10, n_out=4, n_scratch=1,
        name="prep_bwd", grid=(nb,),
        in_specs=[row_spec(Z_COLS)] + [pl.BlockSpec((HEADS, tm, HEAD_DIM), lambda i: (0, nb - 1 - i, 0))] * 3
                 + [row_spec(ATTN_W), row_spec(LANES), const((1, ATTN_W)),
                    const((1, ATTN_W)), const((1, LANES)), const((ATTN_W, ATTN_W))],
        out_specs=[row_spec(Z_COLS), const((1, ATTN_W)), const((1, ATTN_W)), const((1, LANES))],
        out_shape=[_sds((s, Z_COLS), BF16), _sds((1, ATTN_W)), _sds((1, ATTN_W)), _sds((1, LANES))],
        scratch_shapes=[pltpu.VMEM((1, LANES), F32)], dims=("arbitrary",),
        operands=(z, dqn, dkn, dv, du, dc, gq, gk, bf, gg))


def _in_norm_bwd(x, g_mix, dh, dx1, hosted=None):
    s = x.shape[0]
    tm = _row_tile(s)

    def body(x_ref, g_ref, dh_ref, dx1_ref, dx_ref, dg_ref):
        i = pl.program_id(0)

        @pl.when(i == 0)
        def _():
            dg_ref[...] = jnp.zeros_like(dg_ref)

        dxn, dg = _rms_bwd(x_ref[...], g_ref[...], dh_ref[...])
        dx_ref[...] = dx1_ref[...] + dxn
        dg_ref[...] += dg

    row = pl.BlockSpec((tm, D_MODEL), lambda i: (i, 0))
    vec = pl.BlockSpec((1, D_MODEL), lambda i: (0, 0))
    nb = s // tm
    return _host_pcall(body, hosted, lambda: pl.program_id(0) == 0, lambda: pl.program_id(0) == nb - 1,
                       n_in=4, n_out=2, n_scratch=0, name="in_norm_bwd", grid=(nb,), in_specs=[row, vec, row, row],
                       out_specs=[row, vec], out_shape=[_sds((s, D_MODEL)), _sds((1, D_MODEL))], scratch_shapes=[],
                       dims=("arbitrary",), operands=(x, g_mix, dh, dx1))


def _adamw_refs(w_ref, g_ref, m_ref, v_ref, d_ref, mo_ref, vo_ref):
    gv = g_ref[...]
    mn = ADAM_B1 * m_ref[...] + (1.0 - ADAM_B1) * gv
    vn = ADAM_B2 * v_ref[...] + (1.0 - ADAM_B2) * (gv * gv)
    m_hat = mn / (1.0 - ADAM_B1 ** ADAM_STEP)
    v_hat = vn / (1.0 - ADAM_B2 ** ADAM_STEP)
    d_ref[...] = -ADAM_LR * (m_hat / (jnp.sqrt(v_hat) + ADAM_EPS) + ADAM_WD * w_ref[...])
    mo_ref[...] = mn
    vo_ref[...] = vn


def _adamw_small(ws, gs, ms, vs):
    n = len(ws)

    def body(*refs):
        ins, outs = refs[:4 * n], refs[4 * n:]
        for i in range(n):
            _adamw_refs(ins[i], ins[n + i], ins[2 * n + i], ins[3 * n + i], *outs[3 * i:3 * i + 3])

    vm = pl.BlockSpec(memory_space=pltpu.VMEM)
    out_shape = [_sds(w.shape) for w in ws for _ in range(3)]
    return _pallas(body, name="adamw_small", in_specs=[vm] * (4 * n), out_specs=[vm] * (3 * n), out_shape=out_shape,
                   compiler_params=pltpu.CompilerParams(vmem_limit_bytes=VMEM_LIMIT))(*ws, *gs, *ms, *vs)


def _adamw(w, g, m, v, *, name):
    r, c = w.shape
    tr = r
    for cand in (256, 176, 128, 64):
        if r > cand and r % cand == 0:
            tr = cand
            break

    def body(w_ref, g_ref, m_ref, v_ref, d_ref, mo_ref, vo_ref):
        _adamw_refs(w_ref, g_ref, m_ref, v_ref, d_ref, mo_ref, vo_ref)

    spec = pl.BlockSpec((tr, c), lambda i: (i, 0))
    return _pcall(body, name=name, grid=(r // tr,), in_specs=[spec] * 4, out_specs=[spec] * 3,
                  out_shape=[_sds((r, c))] * 3, dims=("parallel",))(w, g, m, v)


def _prefetch_call(body, *, name, grid, in_specs, out_specs, out_shape, operands):
    grid_spec = pltpu.PrefetchScalarGridSpec(num_scalar_prefetch=1, grid=grid, in_specs=in_specs, out_specs=out_specs)
    params = pltpu.CompilerParams(dimension_semantics=("parallel",) * len(grid), vmem_limit_bytes=VMEM_LIMIT)
    return _pallas(body, name=name, grid_spec=grid_spec, out_shape=out_shape, compiler_params=params)(*operands)


def _place_cols(buf, shard, place):
    rows, cols = shard.shape
    tr = 256

    def body(place_ref, s_ref, b_ref, o_ref):
        o_ref[...] = s_ref[...]

    grid_spec = pltpu.PrefetchScalarGridSpec(
        num_scalar_prefetch=1, grid=(rows // tr,),
        in_specs=[pl.BlockSpec((tr, cols), lambda i, p: (i, 0)), pl.BlockSpec(memory_space=pltpu.HBM)],
        out_specs=pl.BlockSpec((tr, cols), lambda i, p: (i, p[0])))
    return _pallas(body, name="place_own_cols", grid_spec=grid_spec, out_shape=_sds(buf.shape, buf.dtype),
                   input_output_aliases={2: 0},
                   compiler_params=pltpu.CompilerParams(dimension_semantics=("parallel",),
                                                        vmem_limit_bytes=VMEM_LIMIT))(place, shard, buf)


def _half_rows_tile(hr):
    return hr if hr <= 256 else 176 if hr % 176 == 0 else 256


def _add_half(g, landed, place, *, name):
    def body(place_ref, g_ref, l_ref, o_ref):
        own = g_ref[0] if len(g_ref.shape) == 4 else g_ref[...]
        o_ref[...] = (own + l_ref[...]).astype(BF16)

    if g.ndim == 4:
        _, _, hr, c = g.shape
        tr = _half_rows_tile(hr)
        blk = (1, tr, c)
        return _prefetch_call(
            body, name=name, grid=(N_CHIPS, hr // tr),
            in_specs=[pl.BlockSpec((1,) + blk, lambda j, i, p: (j, p[1], i, 0)), pl.BlockSpec(blk, lambda j, i, p: (j, i, 0))],
            out_specs=pl.BlockSpec(blk, lambda j, i, p: (j, i, 0)), out_shape=_sds(landed.shape, BF16),
            operands=(place, g, landed))
    hr, c = landed.shape
    tr, tc = 256, _tile(c, 2176)
    nb = hr // tr
    return _prefetch_call(
        body, name=name, grid=(nb, c // tc),
        in_specs=[pl.BlockSpec((tr, tc), lambda i, j, p: (p[1] * nb + i, j)), pl.BlockSpec((tr, tc), lambda i, j, p: (i, j))],
        out_specs=pl.BlockSpec((tr, tc), lambda i, j, p: (i, j)), out_shape=_sds(landed.shape, BF16),
        operands=(place, g, landed))


def _sum_chips(chip_sum, lands, place, *, name, tc, window_stride=0):
    _, hr, c = lands.shape
    tr = _half_rows_tile(hr)
    nb = hr // tr
    ncb = c // tc

    def body(place_ref, own_ref, a_ref, b_ref, c_ref, o_ref):
        own = own_ref[0] if len(own_ref.shape) == 3 else own_ref[...]
        o_ref[...] = ((own.astype(F32) + a_ref[0].astype(F32)) + b_ref[0].astype(F32)) + c_ref[0].astype(F32)

    land = lambda k: pl.BlockSpec((1, tr, tc), lambda i, j, p: ((p[0] + k) % N_CHIPS, i, j))
    if chip_sum.ndim == 3:
        own_spec = land(0)
    else:
        stride = window_stride // tc
        own_spec = pl.BlockSpec((tr, tc), lambda i, j, p: (i, p[0] * stride + j))
    return _prefetch_call(
        body, name=name, grid=(nb, ncb), in_specs=[own_spec, land(1), land(2), land(3)],
        out_specs=pl.BlockSpec((tr, tc), lambda i, j, p: (p[1] * nb + i, j)), out_shape=_sds((2 * hr, c)),
        operands=(place, chip_sum, lands, lands, lands))


_HBM = pl.BlockSpec(memory_space=pltpu.HBM)


def _place():
    x, y, c = lax.axis_index("x"), lax.axis_index("y"), lax.axis_index("c")
    chips = [(1 - x, y), (x, 1 - y), (1 - x, 1 - y)]
    return x, y, c, chips


def _rcopy(src, dst, send_sem, recv_sem, to):
    return pltpu.make_async_remote_copy(src_ref=src, dst_ref=dst, send_sem=send_sem, recv_sem=recv_sem,
                                        device_id=to, device_id_type=MESH)


UP_COLS = 2 * D_FF // N_CHIPS
IN_WINDOW = 640
IN_STRIDE = 512


class _Hosted:
    def __init__(self, operands, out_shapes, n_sems, start, finish, aliases=None, local_sems=0):
        self.operands, self.out_shapes, self.n_sems = list(operands), list(out_shapes), n_sems
        self.start, self.finish, self.aliases, self.local_sems = start, finish, dict(aliases or {}), local_sems

    def scratch(self):
        return ([pltpu.SemaphoreType.DMA((self.n_sems,)), pltpu.SemaphoreType.DMA((self.n_sems,))]
                + [pltpu.SemaphoreType.DMA] * self.local_sems)


def _both(a, b):
    na, nao, nas = len(a.operands), len(a.out_shapes), len(a.scratch())

    def start(ins, outs, sems):
        a.start(ins[:na], outs[:nao], sems[:nas])
        b.start(ins[na:], outs[nao:], sems[nas:])

    def finish(ins, outs, sems):
        a.finish(ins[:na], outs[:nao], sems[:nas])
        b.finish(ins[na:], outs[nao:], sems[nas:])

    both = _Hosted(a.operands + b.operands, a.out_shapes + b.out_shapes, 0, start, finish,
                   aliases={**a.aliases, **{na + i: nao + o for i, o in b.aliases.items()}})
    both.scratch = lambda: a.scratch() + b.scratch()
    return both


def _then(a, b):
    nas = len(a.scratch())

    def finish(ins, outs, sems):
        a.finish(ins, outs, sems[:nas])
        b.start(ins, outs, sems[nas:])
        b.finish(ins, outs, sems[nas:])

    chain = _Hosted(a.operands, a.out_shapes, 0, lambda ins, outs, sems: a.start(ins, outs, sems[:nas]), finish,
                    aliases=a.aliases)
    chain.scratch = lambda: a.scratch() + b.scratch()
    return chain


def _run_hosted(hosted, *, name):
    n_in, n_out = len(hosted.operands), len(hosted.out_shapes)

    def body(*refs):
        parts = (refs[:n_in], refs[n_in:n_in + n_out], refs[n_in + n_out:])
        hosted.start(*parts)
        hosted.finish(*parts)

    return _pallas(body, name=name, in_specs=[_HBM] * n_in, out_specs=[_HBM] * n_out, out_shape=hosted.out_shapes,
                   input_output_aliases=hosted.aliases, scratch_shapes=hosted.scratch())(*hosted.operands)


def _host_pcall(core_body, hosted, first, last, *, n_in, n_out, n_scratch, name, grid, in_specs, out_specs, out_shape,
                scratch_shapes, dims, operands):
    if hosted is None:
        outs = _pcall(core_body, name=name, grid=grid, in_specs=in_specs, out_specs=out_specs, out_shape=out_shape,
                      scratch_shapes=scratch_shapes, dims=dims)(*operands)
        return outs, []
    hi, ho = len(hosted.operands), len(hosted.out_shapes)

    def body(*refs):
        a, b = n_in, n_in + hi
        c, d = b + n_out, b + n_out + ho
        e = d + n_scratch
        parts = (refs[a:b], refs[c:d], refs[e:])

        @pl.when(first())
        def _():
            hosted.start(*parts)

        core_body(*refs[:a], *refs[b:c], *refs[d:e])

        @pl.when(last())
        def _():
            hosted.finish(*parts)

    params = pltpu.CompilerParams(dimension_semantics=("arbitrary",) * len(grid), vmem_limit_bytes=VMEM_LIMIT)
    outs = _pallas(body, name=name, grid=grid, in_specs=list(in_specs) + [_HBM] * hi, out_specs=list(out_specs) + [_HBM] * ho,
                   out_shape=list(out_shape) + hosted.out_shapes, scratch_shapes=list(scratch_shapes) + hosted.scratch(),
                   input_output_aliases={n_in + a: n_out + b for a, b in hosted.aliases.items()},
                   compiler_params=params)(*operands, *hosted.operands)
    return outs[:n_out], outs[n_out:]


WHOLE_HALF = (0, 1, 1)


def _band_rows(src, hc, band):
    first, count, of = band
    hr = src.shape[0] // 2
    return pl.ds(hc * hr + first * (hr // of), count * (hr // of))


def _gather_slot(src, out, chip, hc, band=WHOLE_HALF):
    cols = src.shape[1]
    if len(out.shape) == 2:
        return out.at[_band_rows(src, hc, band), pl.ds(pl.multiple_of(chip * cols, LANES), cols)]
    return out.at[chip, _band_rows(src, hc, band), :]


def _gathered_shape(shard, by_cols):
    if by_cols:
        return _sds((shard.shape[0], N_CHIPS * shard.shape[1]), shard.dtype)
    return _sds((N_CHIPS,) + shard.shape, shard.dtype)


def _plan_gather_ici(shards, by_cols, whole=(), bands=None, into=None):
    n = len(shards)
    bands = bands or [WHOLE_HALF] * n
    into = into or [None] * n
    given = [w for w in range(n) if into[w] is not None]
    n_ops = n + len(whole)

    def copies(ins, outs, sems):
        send_sems, recv_sems = sems[0], sems[1]
        x, y, c, chips = _place()
        me = 2 * x + y
        sends, waits = [], []
        for w in range(n + len(whole)):
            for k, (cx, cy) in enumerate(chips):
                sem = (send_sems.at[3 * w + k], recv_sems.at[3 * w + k])
                if w < n:
                    sends.append(_rcopy(ins[w].at[_band_rows(ins[w], c, bands[w]), :],
                                        _gather_slot(ins[w], outs[w], me, c, bands[w]), *sem, (cx, cy, c)))
                    landed = _gather_slot(ins[w], outs[w], 2 * cx + cy, c, bands[w])
                else:
                    sends.append(_rcopy(ins[w], outs[w].at[me], *sem, (cx, cy, c)))
                    landed = outs[w].at[2 * cx + cy]
                waits.append(_rcopy(landed, landed, *sem, (cx, cy, c)))
        return sends, waits

    def start(ins, outs, sems):
        for cp in copies(ins, outs, sems)[0]:
            cp.start()

    def finish(ins, outs, sems):
        sends, waits = copies(ins, outs, sems)
        for cp in waits:
            cp.wait_recv()
        for cp in sends:
            cp.wait_send()

    out_shapes = [_gathered_shape(s, bc) for s, bc in zip(shards, by_cols)] + [_sds((N_CHIPS,) + a.shape, a.dtype) for a in whole]
    return _Hosted(list(shards) + list(whole) + [into[w] for w in given], out_shapes, 3 * n_ops, start, finish,
                   aliases={n_ops + i: w for i, w in enumerate(given)})


def _plan_gather_d2d(bufs, shard_shapes, bands=None):
    n = len(bufs)
    bands = bands or [WHOLE_HALF] * n

    def copies(ins, outs, sems):
        send_sems, recv_sems = sems
        x, y, c, chips = _place()
        sibling = (x, y, 1 - c)
        sends, waits = [], []
        for w in range(n):
            for k, (cx, cy) in enumerate(chips):
                sem = (send_sems.at[3 * w + k], recv_sems.at[3 * w + k])
                landed = _gather_slot(shard_shapes[w], outs[w], 2 * cx + cy, c, bands[w])
                other = _gather_slot(shard_shapes[w], outs[w], 2 * cx + cy, 1 - c, bands[w])
                sends.append(_rcopy(landed, landed, *sem, sibling))
                waits.append(_rcopy(other, other, *sem, sibling))
        return sends, waits

    def start(ins, outs, sems):
        for cp in copies(ins, outs, sems)[0]:
            cp.start()

    def finish(ins, outs, sems):
        sends, waits = copies(ins, outs, sems)
        for cp in waits:
            cp.wait_recv()
        for cp in sends:
            cp.wait_send()

    return _Hosted(bufs, [_sds(b.shape, b.dtype) for b in bufs], 3 * n, start, finish, aliases={w: w for w in range(n)})


def _plan_allgather_first(block):
    def copies(ins, outs, sems):
        send_sems, recv_sems = sems
        x, y, c, chips = _place()
        me = 4 * x + 2 * y + c
        peers = [(x, y, 1 - c)] + [(cx, cy, c) for cx, cy in chips]
        sends = [_rcopy(ins[0], outs[0].at[me], send_sems.at[k], recv_sems.at[k], p) for k, p in enumerate(peers)]
        waits = [_rcopy(outs[0].at[4 * px + 2 * py + pc], outs[0].at[4 * px + 2 * py + pc], send_sems.at[k],
                        recv_sems.at[k], (px, py, pc)) for k, (px, py, pc) in enumerate(peers)]
        return sends, waits

    def start(ins, outs, sems):
        for cp in copies(ins, outs, sems)[0]:
            cp.start()

    def finish(ins, outs, sems):
        sends, waits = copies(ins, outs, sems)
        for cp in waits:
            cp.wait_recv()
        for cp in sends:
            cp.wait_send()

    return _Hosted([block], [_sds((8,) + block.shape)], 4, start, finish)


def _plan_allgather_second(gathered):
    def copies(ins, outs, sems):
        send_sems, recv_sems = sems
        x, y, c, chips = _place()
        sends, waits = [], []
        for k, (cx, cy) in enumerate(chips):
            landed = outs[0].at[4 * cx + 2 * cy + c]
            other = outs[0].at[4 * cx + 2 * cy + 1 - c]
            sends.append(_rcopy(landed, landed, send_sems.at[k], recv_sems.at[k], (x, y, 1 - c)))
            waits.append(_rcopy(other, other, send_sems.at[k], recv_sems.at[k], (x, y, 1 - c)))
        return sends, waits

    def start(ins, outs, sems):
        for cp in copies(ins, outs, sems)[0]:
            cp.start()

    def finish(ins, outs, sems):
        sends, waits = copies(ins, outs, sems)
        for cp in waits:
            cp.wait_recv()
        for cp in sends:
            cp.wait_send()

    return _Hosted([gathered], [_sds(gathered.shape)], 3, start, finish, aliases={0: 0})


def _place_block(gathered, block, device):
    rows, lanes = block.shape

    def body(dev_ref, b_ref, g_ref, o_ref):
        o_ref[0] = b_ref[...]

    grid_spec = pltpu.PrefetchScalarGridSpec(
        num_scalar_prefetch=1, grid=(1,),
        in_specs=[pl.BlockSpec((rows, lanes), lambda i, d: (0, 0)), pl.BlockSpec(memory_space=pltpu.HBM)],
        out_specs=pl.BlockSpec((1, rows, lanes), lambda i, d: (d[0], 0, 0)))
    return _pallas(body, name="place_own_block", grid_spec=grid_spec, out_shape=_sds(gathered.shape),
                   input_output_aliases={2: 0},
                   compiler_params=pltpu.CompilerParams(dimension_semantics=("arbitrary",),
                                                        vmem_limit_bytes=VMEM_LIMIT))(device, block, gathered)


def _sum_devices(gathered):
    _, rows, lanes = gathered.shape
    tr = rows // 2 if rows % 16 == 0 else rows

    def body(g_ref, o_ref):
        acc = g_ref[0]
        for d in range(1, 8):
            acc = acc + g_ref[d]
        o_ref[...] = acc

    return _pcall(body, name="sum_devices", grid=(rows // tr,), in_specs=[pl.BlockSpec((8, tr, lanes), lambda i: (0, i, 0))],
                  out_specs=pl.BlockSpec((tr, lanes), lambda i: (i, 0)), out_shape=_sds((rows, lanes)), dims=("parallel",))(gathered)


def _plan_swap(grads):
    def copies(ins, outs, sems):
        send_sems, recv_sems = sems
        x, y, c, _ = _place()
        cps = []
        for w, g_ref in enumerate(ins):
            if len(g_ref.shape) == 4:
                theirs = g_ref.at[:, 1 - c]
            else:
                hr = g_ref.shape[0] // 2
                theirs = g_ref.at[pl.ds((1 - c) * hr, hr), :]
            cps.append(_rcopy(theirs, outs[w], send_sems.at[w], recv_sems.at[w], (x, y, 1 - c)))
        return cps

    def start(ins, outs, sems):
        for cp in copies(ins, outs, sems):
            cp.start()

    def finish(ins, outs, sems):
        for cp in copies(ins, outs, sems):
            cp.wait()

    out_shapes = [_sds((g.shape[0], g.shape[2], g.shape[3])) if g.ndim == 4 else _sds((g.shape[0] // 2, g.shape[1]))
                  for g in grads]
    return _Hosted(grads, out_shapes, len(grads), start, finish)


def _plan_scatter(chip_sums, windows):
    def copies(ins, outs, sems):
        send_sems, recv_sems = sems
        x, y, c, chips = _place()
        me = 2 * x + y
        sends, waits = [], []
        for w, s_ref in enumerate(ins):
            for k, (cx, cy) in enumerate(chips):
                tgt = 2 * cx + cy
                if windows[w] is not None:
                    stride, width = windows[w]
                    part = s_ref.at[:, pl.ds(pl.multiple_of(tgt * stride, LANES), width)]
                else:
                    part = s_ref.at[tgt]
                sem = (send_sems.at[3 * w + k], recv_sems.at[3 * w + k])
                sends.append(_rcopy(part, outs[w].at[me], *sem, (cx, cy, c)))
                slot = outs[w].at[tgt]
                waits.append(_rcopy(slot, slot, *sem, (cx, cy, c)))
        return sends, waits

    def start(ins, outs, sems):
        for cp in copies(ins, outs, sems)[0]:
            cp.start()

    def finish(ins, outs, sems):
        sends, waits = copies(ins, outs, sems)
        for cp in waits:
            cp.wait_recv()
        for cp in sends:
            cp.wait_send()

    out_shapes = [_sds((N_CHIPS, s.shape[0], win[1]), BF16) if win is not None else _sds(s.shape, BF16)
                  for s, win in zip(chip_sums, windows)]
    return _Hosted(chip_sums, out_shapes, 3 * len(chip_sums), start, finish)


def _plan_join(reds):
    def copies(ins, outs, sems):
        send_sems, recv_sems = sems
        x, y, c, _ = _place()
        sends, waits = [], []
        for w, out in enumerate(outs):
            hr = out.shape[0] // 2
            mine = out.at[pl.ds(c * hr, hr), :]
            theirs = out.at[pl.ds((1 - c) * hr, hr), :]
            sends.append(_rcopy(mine, mine, send_sems.at[w], recv_sems.at[w], (x, y, 1 - c)))
            waits.append(_rcopy(theirs, theirs, send_sems.at[w], recv_sems.at[w], (x, y, 1 - c)))
        return sends, waits

    def start(ins, outs, sems):
        for cp in copies(ins, outs, sems)[0]:
            cp.start()

    def finish(ins, outs, sems):
        sends, waits = copies(ins, outs, sems)
        for cp in waits:
            cp.wait_recv()
        for cp in sends:
            cp.wait_send()

    return _Hosted(reds, [_sds(r.shape) for r in reds], len(reds), start, finish, aliases={w: w for w in range(len(reds))})


def _allreduce_small(v):
    m_per = v.shape[0]

    def body(v_ref, out_ref, all_ref, send_sems, recv_sems, local_sem):
        x, y, c, chips = _place()
        me, sibling = (x, y, c), (x, y, 1 - c)

        def rows(px, py, pc):
            return all_ref.at[pl.ds((4 * px + 2 * py + pc) * m_per, m_per), :]

        def copy(k, block, to, src=None):
            return _rcopy(rows(*block) if src is None else src, rows(*block), send_sems.at[k], recv_sems.at[k], to)

        mine = pltpu.make_async_copy(v_ref, rows(*me), local_sem)
        mine.start()
        first = [copy(0, me, sibling, src=v_ref)]
        first += [copy(1 + k, me, (*chip, c), src=v_ref) for k, chip in enumerate(chips)]
        for cp in first:
            cp.start()
        passed = [copy(4 + k, (*chip, c), sibling) for k, chip in enumerate(chips)]
        for k, chip in enumerate(chips):
            copy(1 + k, (*chip, c), me).wait_recv()
            passed[k].start()
        copy(0, sibling, me).wait_recv()
        for k, chip in enumerate(chips):
            copy(4 + k, (*chip, 1 - c), me).wait_recv()
        for cp in first + passed:
            cp.wait_send()
        mine.wait()
        acc = all_ref[pl.ds(0, m_per), :]
        for d in range(1, 8):
            acc = acc + all_ref[pl.ds(d * m_per, m_per), :]
        out_ref[...] = acc

    vm = pl.BlockSpec(memory_space=pltpu.VMEM)
    return _pallas(body, name="allreduce_small", in_specs=[vm], out_specs=vm, out_shape=_sds((m_per, LANES)),
                          scratch_shapes=[pltpu.VMEM((8 * m_per, LANES), F32), pltpu.SemaphoreType.DMA((7,)),
                                          pltpu.SemaphoreType.DMA((7,)), pltpu.SemaphoreType.DMA],
                          compiler_params=pltpu.CompilerParams(vmem_limit_bytes=VMEM_LIMIT))(v)


def _block_diag(blocks):
    j, g, a, b = blocks.shape
    eye = jnp.eye(g, dtype=bool)[None, :, None, :, None]
    return jnp.where(eye, blocks[:, :, :, None, :], jnp.zeros((), blocks.dtype)).reshape(j, g * a, g * b)


def _diag_blocks(m, a, b):
    j = m.shape[0]
    g = m.shape[1] // a
    t = m.reshape(j, g, a, g, b)
    eye = jnp.eye(g, dtype=bool)[None, :, None, :, None]
    return jnp.sum(jnp.where(eye, t, 0.0), axis=3)


_SMALL = (("g_mix", (1024,)), ("b_f", (8,)), ("g_q", (64,)), ("g_k", (64,)), ("lambda_re", (32, 64)),
          ("lambda_im", (32, 64)), ("log_step", (32,)), ("b_re", (32, 64, 16)), ("b_im", (32, 64, 16)),
          ("c_re", (32, 16, 64)), ("c_im", (32, 16, 64)), ("d_skip", (32, 16)), ("b_glu", (512,)),
          ("g_attn_out", (512,)), ("g_ssm_out", (512,)), ("g_ffn", (1024,)), ("conv_b", (5632,)))


_LATE_SMALL = ("g_mix", "b_f", "g_q", "g_k")
_EARLY_SMALL = tuple(n for n, _ in _SMALL if n not in _LATE_SMALL)


def _packed_rows(n):
    tile = SUBLANES * LANES
    return -(-n // tile) * SUBLANES


def _pack_small(arrs):
    parts = []
    for a in arrs:
        flat = a.reshape(-1)
        rows = _packed_rows(flat.shape[0])
        parts.append(jnp.pad(flat, (0, rows * LANES - flat.shape[0])).reshape(rows, LANES))
    return jnp.concatenate(parts, axis=0)


def _unpack_small(buf, shapes):
    out, r = [], 0
    for shape in shapes:
        n = math.prod(shape)
        out.append(buf[r:r + _packed_rows(n)].reshape(-1)[:n].reshape(shape))
        r += _packed_rows(n)
    return out


def _halves(t):
    return t.reshape(N_CHIPS, 2, t.shape[0] // (2 * N_CHIPS), t.shape[1])


class _MeshComm:
    def __init__(self, args):
        x, y, self.core = lax.axis_index("x"), lax.axis_index("y"), lax.axis_index("c")
        self.chip = 2 * x + y
        self.place = jnp.stack([self.chip, self.core]).astype(jnp.int32)
        self.shards = {n: args[n].astype(BF16) for n in ("w_in", "w_glu", "w_out", "w_up", "w_down")}
        self.conv_w = args["conv_w"]

    def _own(self, stacked, mine):
        return lax.dynamic_update_slice(stacked, mine[None], (self.chip,) + (0,) * mine.ndim)

    def w_in(self):
        sh = self.shards["w_in"]
        (buf,) = _run_hosted(_then(_plan_gather_ici([sh], [False]), _plan_gather_d2d([sh], [sh])), name="gather_w_in")
        whole = self._own(buf, sh).transpose(1, 0, 2).reshape(D_MODEL, IN_COLS)
        return jnp.pad(whole, ((0, 0), (0, Z_COLS - IN_COLS)))

    def gather_first(self):
        self.mid = [self.shards[n] for n in ("w_glu", "w_out", "w_down")]
        return _plan_gather_ici(self.mid + [self.shards["w_up"]], [False, False, False, True], whole=[self.conv_w],
                                bands=[WHOLE_HALF] * 3 + [(0, 1, 4)])

    def gather_second(self, landed):
        self.g_cw = landed[4]
        return _both(_plan_gather_d2d(list(landed[:3]), self.mid),
                     _plan_gather_ici([self.shards["w_up"]], [True], bands=[(1, 3, 4)], into=[landed[3]]))

    def weights(self, gathered):
        g_glu, g_out, g_down = gathered[:3]
        own = self._own
        return (own(g_glu, self.mid[0]).reshape(SSM_W, SSM_W), own(g_out, self.mid[1]).reshape(D_MODEL, D_MODEL),
                own(g_down, self.mid[2]).reshape(D_FF, D_MODEL),
                own(self.g_cw, self.conv_w).transpose(1, 0, 2).reshape(3, 2 * D_FF))

    def gather_third(self, gathered):
        return _plan_gather_d2d([gathered[3]], [self.shards["w_up"]])

    def w_up(self, passed):
        return _place_cols(passed[0], self.shards["w_up"], self.place)

    def swap_down(self, d_w_down):
        self.d_down = _halves(d_w_down)
        return _plan_swap([self.d_down])

    def swap(self, landed_down, d_w_up, d_w_glu, d_w_out):
        self.sum_down = _add_half(self.d_down, landed_down[0], self.place, name="add_w_down")
        self.early = [d_w_up, _halves(d_w_glu), _halves(d_w_out)]
        return _both(_plan_scatter([self.sum_down], [None]), _plan_swap(self.early))

    def scatter(self, landed, small_block):
        self.land_down = landed[0]
        self.early_sums = [_add_half(g, l, self.place, name="add_" + n)
                           for g, l, n in zip(self.early, landed[1:], ("w_up", "w_glu", "w_out"))]
        return _both(_plan_scatter(self.early_sums, [(UP_COLS, UP_COLS), None, None]), _plan_allgather_first(small_block))

    def swap_in(self, d_w_in):
        self.d_in = d_w_in
        return _plan_swap([d_w_in])

    def scatter_in(self, landed):
        self.sum_in = _add_half(self.d_in, landed[0], self.place, name="add_w_in")
        return _plan_scatter([self.sum_in], [(IN_STRIDE, IN_WINDOW)])

    def small_second(self, landed_small):
        return _plan_allgather_second(landed_small[0])

    def reduce(self, lands):
        early_lands, (land_in,) = lands
        sum_in = self.sum_in
        es, el = self.early_sums, early_lands
        todo = [(sum_in, land_in, "w_in", LANES, IN_STRIDE), (es[1], el[1], "w_glu", SSM_W, 0),
                (es[2], el[2], "w_out", D_MODEL, 0), (es[0], el[0], "w_up", UP_COLS, UP_COLS),
                (self.sum_down, self.land_down, "w_down", D_MODEL, 0)]
        reds = _run_hosted(_plan_join([_sum_chips(s, l, self.place, name="sum_" + n, tc=tc, window_stride=st)
                                       for s, l, n, tc, st in todo]), name="join_halves")
        g_big = dict(zip(("w_in", "w_glu", "w_out", "w_up", "w_down"), reds))
        g_big["w_in"] = lax.dynamic_slice_in_dim(reds[0], 2 * self.chip, IN_COLS // N_CHIPS, axis=1)
        return g_big


def _local_step(x, tgt, p, comm):
    s = x.shape[0]
    row = lambda v: v.reshape(1, -1)
    g_mix, g_ffn = row(p["g_mix"]), row(p["g_ffn"])
    g_att, g_ssm, b_glu, conv_b = row(p["g_attn_out"]), row(p["g_ssm_out"]), row(p["b_glu"]), row(p["conv_b"])
    gq = row(jnp.tile(p["g_q"], HEADS))
    gk = row(jnp.tile(p["g_k"], HEADS))
    bf = row(jnp.pad(p["b_f"], (0, LANES - HEADS)))
    gg = jnp.kron(jnp.eye(HEADS, dtype=F32), jnp.ones((HEAD_DIM, HEAD_DIM), F32)).astype(BF16)
    dsk = row(p["d_skip"])

    rep = lambda a: jnp.repeat(a, SSM_GROUP, axis=0)
    lr, li = rep(p["lambda_re"]), rep(p["lambda_im"])
    ls = rep(jnp.broadcast_to(p["log_step"][:, None], (SSM_GROUPS, SSM_STATE)))
    bt_re = p["b_re"].transpose(0, 2, 1).reshape(_PARAM_SHAPE)
    bt_im = p["b_im"].transpose(0, 2, 1).reshape(_PARAM_SHAPE)
    a_re_rep, a_im_rep, bb_re, bb_im = _ssm_params(lr, li, ls, bt_re, bt_im)
    ar = a_re_rep[::SSM_GROUP].reshape(SSM_CHUNKS, 1, CHUNK_S)
    ai = a_im_rep[::SSM_GROUP].reshape(SSM_CHUNKS, 1, CHUNK_S)
    chunked = lambda t: t.reshape(SSM_CHUNKS, SSM_GROUPS // SSM_CHUNKS, SSM_GROUP, SSM_STATE)
    bbr = _block_diag(chunked(bb_re)).astype(BF16)
    bbi = _block_diag(chunked(bb_im)).astype(BF16)
    to_cc = lambda c: _block_diag(chunked(c).transpose(0, 1, 3, 2)).astype(BF16)
    ccr, cci = to_cc(p["c_re"]), to_cc(p["c_im"])

    w_in_r = comm.w_in()
    hb, z = _in_proj(x, g_mix, w_in_r)
    qh, kh, vh, ub, uf, c128 = _attn_prep(z, gq, gk, bf, gg)
    crow = c128[:, :HEADS].T.reshape(HEADS, 1, s)
    (oh, lse), landed = _attn_fwd(qh, kh, vh, crow, comm.gather_first())
    (xr, xi, y), gathered = _ssm_fwd(ub, uf, bbr, bbi, ar, ai, ccr, cci, dsk, comm.gather_second(landed))
    w_glu_b, w_out_b, w_down_b, conv_w_full = comm.weights(gathered)
    (x1, mixb, h2b), passed = _mix_out(y, oh, x, w_glu_b, b_glu, g_att, g_ssm, w_out_b, g_ffn, comm.gather_third(gathered))
    w_up_b = comm.w_up(passed)
    up = _mm(h2b, w_up_b, name="ffn_up", tm=1024, tn=1408, tk=1024)
    act = _conv_act(up, conv_w_full, conv_b)
    dy, dyb, loss_blk = _down_loss(act, w_down_b, x1, tgt)

    d_w_down = _mm(act, dyb, ta=True, name="d_w_down", tm=1408, tn=1024, tk=2048)
    dact = _mm(dyb, w_down_b, tb=True, name="d_act", tm=1024, tn=1408, tk=1024)
    dupb, dcw = _conv_act_bwd(up, dact, conv_w_full, conv_b)
    d_w_up = _mm(h2b, dupb, ta=True, b_parts=2, name="d_w_up", tm=1024, tn=1408, tk=2048)
    dh2, landed_down = _mm(dupb, w_up_b, tb=True, a_parts=2, name="d_h2", tm=1024, tn=1024, tk=1408, carry=True,
                           hosted=comm.swap_down(d_w_down))
    dx1, dx1b, doh, dys, d_w_glu, d_g_ffn, d_g_att, d_g_ssm, d_b_glu = _mix_bwd(
        dy, dh2, x1, g_ffn, w_out_b, y, oh, w_glu_b, b_glu, g_att, g_ssm)[0]
    d_w_out = _mm(mixb, dx1b, ta=True, name="d_w_out", tm=1024, tn=1024, tk=2048)
    (du, dbbr, dbbi, dccr, dcci, dar, dai, dd), swapped = _ssm_bwd(dys, uf, ub, xr, xi, bbr, bbi, ar, ai, ccr, cci, dsk,
                                                                comm.swap(landed_down, d_w_up, d_w_glu, d_w_out))
    unchunk = lambda t: t.reshape(_PARAM_SHAPE)
    dbb_re = unchunk(_diag_blocks(dbbr, SSM_GROUP, SSM_STATE))
    dbb_im = unchunk(_diag_blocks(dbbi, SSM_GROUP, SSM_STATE))
    first_row = (jnp.arange(_PARAM_SHAPE[0]) % SSM_GROUP == 0)[:, None]
    da_re = jnp.where(first_row, rep(dar.reshape(SSM_GROUPS, SSM_STATE)), 0.0)
    da_im = jnp.where(first_row, rep(dai.reshape(SSM_GROUPS, SSM_STATE)), 0.0)
    expand_t = (jnp.arange(SSM_GROUPS)[:, None] == (jnp.arange(_PARAM_SHAPE[0]) // SSM_GROUP)[None, :]).astype(BF16)
    d_lr, d_li, d_ls, d_bt_re, d_bt_im = _ssm_params_bwd(lr, li, ls, bt_re, bt_im, da_re, da_im, dbb_re, dbb_im, expand_t)
    from_bt = lambda t: t.reshape(SSM_GROUPS, SSM_GROUP, SSM_STATE).transpose(0, 2, 1)
    from_cc = lambda t: _diag_blocks(t, SSM_STATE, SSM_GROUP).transpose(0, 1, 3, 2).reshape(SSM_GROUPS, SSM_GROUP, SSM_STATE)

    small = {
        "lambda_re": d_lr, "lambda_im": d_li, "log_step": d_ls,
        "b_re": from_bt(d_bt_re), "b_im": from_bt(d_bt_im), "c_re": from_cc(dccr), "c_im": from_cc(dcci),
        "d_skip": dd, "b_glu": d_b_glu, "g_attn_out": d_g_att, "g_ssm_out": d_g_ssm, "g_ffn": d_g_ffn,
        "conv_b": dcw[:, 3],
    }
    d_conv_w = dcw[:, 0:3].transpose(1, 0, 2).reshape(3, 2 * D_FF)
    early_small = _pack_small([small[n] for n in _EARLY_SMALL] + [d_conv_w])

    (dqh, dkh, dvh, dcrow), landed = _attn_bwd(qh, kh, vh, crow, lse, doh, comm.scatter(swapped, early_small))
    early_lands, small_landed = landed[:3], landed[3:]
    dc128 = jnp.pad(dcrow.reshape(HEADS, s).T, ((0, 0), (0, LANES - HEADS)))
    (dzb, d_gq, d_gk, d_bf), small_gathered = _prep_bwd(z, dqh, dkh, dvh, du, dc128, gq, gk, bf, gg,
                                                        comm.small_second(small_landed))
    d_w_in_r = _mm(hb, dzb, ta=True, name="d_w_in", tm=512, tn=Z_COLS, tk=2048)
    dh, swapped_in = _mm(dzb, w_in_r, tb=True, name="d_h", tm=1024, tn=1024, tk=Z_COLS, carry=True,
                         hosted=comm.swap_in(d_w_in_r))
    (dx, d_g_mix), land_in = _in_norm_bwd(x, g_mix, dh, dx1, comm.scatter_in(swapped_in))
    small.update({"g_mix": d_g_mix, "b_f": d_bf[0, :HEADS], "g_q": d_gq.reshape(HEADS, HEAD_DIM).sum(0),
                  "g_k": d_gk.reshape(HEADS, HEAD_DIM).sum(0)})
    big = {"w_in": d_w_in_r, "w_glu": d_w_glu, "w_out": d_w_out, "w_up": d_w_up, "w_down": d_w_down}
    return loss_blk[0, 0], dx, big, small, d_conv_w, (early_lands, land_in, small_gathered, early_small)


def kernel(x, g_mix, w_in, b_f, g_q, g_k, lambda_re, lambda_im, log_step, b_re, b_im, c_re, c_im, d_skip, w_glu, b_glu, g_attn_out, g_ssm_out, w_out, g_ffn, w_up, conv_w, conv_b, w_down, loss_target, m_g_mix, m_w_in, m_b_f, m_g_q, m_g_k, m_lambda_re, m_lambda_im, m_log_step, m_b_re, m_b_im, m_c_re, m_c_im, m_d_skip, m_w_glu, m_b_glu, m_g_attn_out, m_g_ssm_out, m_w_out, m_g_ffn, m_w_up, m_conv_w, m_conv_b, m_w_down, v_g_mix, v_w_in, v_b_f, v_g_q, v_g_k, v_lambda_re, v_lambda_im, v_log_step, v_b_re, v_b_im, v_c_re, v_c_im, v_d_skip, v_w_glu, v_b_glu, v_g_attn_out, v_g_ssm_out, v_w_out, v_g_ffn, v_w_up, v_conv_w, v_conv_b, v_w_down):
    args = dict(locals())
    order = ["g_mix", "w_in", "b_f", "g_q", "g_k", "lambda_re", "lambda_im", "log_step", "b_re", "b_im", "c_re", "c_im",
             "d_skip", "w_glu", "b_glu", "g_attn_out", "g_ssm_out", "w_out", "g_ffn", "w_up", "conv_w", "conv_b", "w_down"]
    comm = _MeshComm(args)
    chip = comm.chip
    loss_part, dx, big, small, d_conv_w, lands = _local_step(x[0], loss_target[0], args, comm)

    g_big = comm.reduce(lands[:2])

    shapes = dict(_SMALL)
    small_names = [n for n, _ in _SMALL]
    device = (2 * chip + comm.core).reshape(1).astype(jnp.int32)
    early = _unpack_small(_sum_devices(_place_block(lands[2][0], lands[3], device)),
                          [shapes[n] for n in _EARLY_SMALL] + [(3, 2 * D_FF)])
    late = _unpack_small(_allreduce_small(_pack_small([small[n] for n in _LATE_SMALL] + [loss_part])),
                         [shapes[n] for n in _LATE_SMALL] + [()])
    loss = late[-1]
    g_conv_w = lax.dynamic_slice_in_dim(early[-1], chip * (2 * D_FF // N_CHIPS), 2 * D_FF // N_CHIPS, axis=1)
    g_small = {**dict(zip(_EARLY_SMALL, early[:-1])), **dict(zip(_LATE_SMALL, late[:-1]))}

    grad, delta, new_m, new_v = {}, {}, {}, {}
    for n in ("w_in", "w_glu", "w_out", "w_up", "w_down"):
        grad[n] = g_big[n]
        delta[n], new_m[n], new_v[n] = _adamw(args[n], g_big[n], args["m_" + n], args["v_" + n], name="adamw_" + n)
    grad["conv_w"] = g_conv_w
    delta["conv_w"], new_m["conv_w"], new_v["conv_w"] = _adamw(conv_w, g_conv_w, m_conv_w, v_conv_w, name="adamw_conv_w")
    stepped = _adamw_small([args[n] for n in small_names], [g_small[n] for n in small_names],
                           [args["m_" + n] for n in small_names], [args["v_" + n] for n in small_names])
    for i, n in enumerate(small_names):
        grad[n] = g_small[n]
        delta[n], new_m[n], new_v[n] = stepped[3 * i:3 * i + 3]

    return (loss, dx[None], *[grad[n] for n in order], *[delta[n] for n in order], *[new_m[n] for n in order],
            *[new_v[n] for n in order])
```

```python
import math

import jax
import jax.numpy as jnp
from jax import lax
from jax.experimental import pallas as pl
from jax.experimental.pallas import tpu as pltpu

F32 = jnp.float32
BF16 = jnp.bfloat16

D_MODEL = 1024
HEADS = 8
HEAD_DIM = 64
ATTN_W = 512
SSM_W = 512
SSM_GROUPS = 32
SSM_GROUP = 16
SSM_STATE = 64
N_STATE = SSM_GROUPS * SSM_STATE
D_FF = 2816
IN_COLS = 2056
Z_COLS = 2176
F_COL0 = 1536
U_COL0 = 1544
EPS = 1e-6
NEG_INF = -1e30
N_CHIPS = 4
LANES = 128
SUBLANES = 8
SSM_CHUNKS = 2
SSM_ROWS = 512
CHUNK_U = SSM_W // SSM_CHUNKS
CHUNK_S = N_STATE // SSM_CHUNKS
HEADS_PER_STEP = 4
STRIP = 128
N_STRIPS = D_FF // STRIP

ADAM_LR = 0.001
ADAM_B1 = 0.9
ADAM_B2 = 0.999
ADAM_EPS = 1e-08
ADAM_WD = 0.01
ADAM_STEP = 10

VMEM_LIMIT = 56 * 1024 * 1024
MESH = pl.DeviceIdType.MESH


def _pallas(body, **kw):
    return pl.pallas_call(body, **kw)


def _pcall(body, *, name, out_shape, in_specs, out_specs, grid=(), scratch_shapes=(), dims=None):
    params = pltpu.CompilerParams(dimension_semantics=dims, vmem_limit_bytes=VMEM_LIMIT)
    return _pallas(body, name=name, grid=grid, in_specs=in_specs, out_specs=out_specs,
                   out_shape=out_shape, scratch_shapes=scratch_shapes, compiler_params=params)


def _sds(shape, dtype=F32):
    return jax.ShapeDtypeStruct(shape, dtype)


def _dot(a, b):
    return jnp.dot(a, b, preferred_element_type=F32)


def _dot_nt(a, b):
    return lax.dot_general(a, b, (((1,), (1,)), ((), ())), preferred_element_type=F32)


def _dot_tn(a, b):
    return lax.dot_general(a, b, (((0,), (0,)), ((), ())), preferred_element_type=F32)


def _split3(x):
    hi = x.astype(BF16)
    r = x - hi.astype(F32)
    mid = r.astype(BF16)
    lo = (r - mid.astype(F32)).astype(BF16)
    return hi, mid, lo


def _dot_exact_r(x, m01):
    hi, mid, lo = _split3(x)
    return _dot(hi, m01) + _dot(mid, m01) + _dot(lo, m01)


def _dot_exact_l(m01, x):
    hi, mid, lo = _split3(x)
    return _dot(m01, hi) + _dot(m01, mid) + _dot(m01, lo)


def _sigmoid(x):
    return 1.0 / (1.0 + jnp.exp(-x))


def _rms(x, g):
    r = lax.rsqrt(jnp.mean(x * x, axis=-1, keepdims=True) + EPS)
    return x * r * g


def _rms_bwd(x, g, dy):
    r = lax.rsqrt(jnp.mean(x * x, axis=-1, keepdims=True) + EPS)
    w = dy * g
    dx = r * w - x * (r * r * r) * jnp.mean(w * x, axis=-1, keepdims=True)
    dg = jnp.sum(dy * x * r, axis=0, keepdims=True)
    return dx, dg


_GELU_K = math.sqrt(2.0 / math.pi)
_GELU_C = 0.044715


def _gelu(y):
    return y * (0.5 * (1.0 + jnp.tanh(_GELU_K * (y + _GELU_C * (y * y * y)))))


def _gelu_grad(y):
    t = jnp.tanh(_GELU_K * (y + _GELU_C * (y * y * y)))
    return 0.5 * (1.0 + t) + 0.5 * y * (1.0 - t * t) * (_GELU_K * (1.0 + 3.0 * _GELU_C * y * y))


def _tile(n, pref):
    if n <= pref:
        return n
    divs = [t for t in range(LANES, n + 1, LANES) if n % t == 0]
    below = [t for t in divs if t <= pref]
    if below and 2 * below[-1] >= pref:
        return below[-1]
    above = [t for t in divs if t > pref]
    return above[0] if above else n


def _row_tile(s):
    return min(256, s)


def _mm(a, b, *, name, tm, tn, tk, ta=False, tb=False, a_parts=1, b_parts=1, carry=False, hosted=None):
    if a_parts > 1:
        m, kk = a.shape[1], a.shape[2] * a_parts
    elif ta:
        kk, m = a.shape
    else:
        m, kk = a.shape
    if b_parts > 1:
        n = b.shape[2] * b_parts
    else:
        n = b.shape[0] if tb else b.shape[1]
    tm, tn, tk = _tile(m, tm), _tile(n // b_parts, tn), _tile(kk // a_parts, tk)
    k_per, n_per = kk // a_parts // tk, n // b_parts // tn

    def body(a_ref, b_ref, o_ref):
        k = pl.program_id(2)
        if ta:
            part = _dot_tn(a_ref[...], b_ref[...])
        elif tb:
            part = _dot_nt(a_ref[...], b_ref[...])
        else:
            part = _dot(a_ref[...], b_ref[...])

        @pl.when(k == 0)
        def _():
            o_ref[...] = part

        @pl.when(k > 0)
        def _():
            o_ref[...] += part

    if a_parts > 1:
        a_spec = pl.BlockSpec((None, tm, tk), lambda i, j, k: (k // k_per, i, k % k_per))
    else:
        a_spec = pl.BlockSpec((tk, tm), lambda i, j, k: (k, i)) if ta else pl.BlockSpec((tm, tk), lambda i, j, k: (i, k))
    if b_parts > 1:
        b_spec = pl.BlockSpec((None, tk, tn), lambda i, j, k: (j // n_per, k, j % n_per))
    else:
        b_spec = pl.BlockSpec((tn, tk), lambda i, j, k: (j, k)) if tb else pl.BlockSpec((tk, tn), lambda i, j, k: (k, j))
    grid = (m // tm, n // tn, kk // tk)
    at = lambda step: (lambda: jnp.logical_and(jnp.logical_and(pl.program_id(0) == step[0], pl.program_id(1) == step[1]),
                                               pl.program_id(2) == step[2]))
    (out,), carried = _host_pcall(body, hosted, at((0, 0, 0)), at(tuple(g - 1 for g in grid)), n_in=2, n_out=1, n_scratch=0,
                                  name=name, grid=grid, in_specs=[a_spec, b_spec],
                                  out_specs=[pl.BlockSpec((tm, tn), lambda i, j, k: (i, j))], out_shape=[_sds((m, n))],
                                  scratch_shapes=[], dims=("parallel", "parallel", "arbitrary"), operands=(a, b))
    return (out, carried) if carry else out


def _in_proj(x, g_mix, w_in_r):
    s = x.shape[0]
    tm = _row_tile(s)

    def body(x_ref, g_ref, w_ref, h_ref, z_ref):
        h = _rms(x_ref[...], g_ref[...]).astype(BF16)
        h_ref[...] = h
        z_ref[...] = _dot(h, w_ref[...])

    return _pcall(body, name="in_proj", grid=(s // tm,),
                  in_specs=[pl.BlockSpec((tm, D_MODEL), lambda i: (i, 0)), pl.BlockSpec((1, D_MODEL), lambda i: (0, 0)),
                            pl.BlockSpec((D_MODEL, Z_COLS), lambda i: (0, 0))],
                  out_specs=[pl.BlockSpec((tm, D_MODEL), lambda i: (i, 0)), pl.BlockSpec((tm, Z_COLS), lambda i: (i, 0))],
                  out_shape=[_sds((s, D_MODEL), BF16), _sds((s, Z_COLS))], dims=("parallel",))(x, g_mix, w_in_r)


def _split_heads(ref, val):
    for h in range(HEADS):
        ref[h] = val[:, h * HEAD_DIM:(h + 1) * HEAD_DIM].astype(ref.dtype)


def _merge_heads(ref):
    return jnp.concatenate([ref[h].astype(F32) for h in range(HEADS)], axis=-1)


def _forget_logits(z_ref, bf_ref):
    fl = z_ref[:, F_COL0:F_COL0 + LANES] + bf_ref[...]
    return jnp.where(lax.broadcasted_iota(jnp.int32, fl.shape, 1) < HEADS, fl, 0.0)


def _attn_prep(z, gq, gk, bf, gg):
    s = z.shape[0]
    tm = _row_tile(s)

    def body(z_ref, gq_ref, gk_ref, bf_ref, gg_ref, qn_ref, kn_ref, vb_ref, ub_ref, uf_ref, c_ref, carry_ref):
        i = pl.program_id(0)

        @pl.when(i == 0)
        def _():
            carry_ref[...] = jnp.zeros_like(carry_ref)

        gg_m = gg_ref[...]

        def head_norm(t, g):
            ssq = _dot_exact_r(t * t, gg_m)
            return t * lax.rsqrt(ssq * (1.0 / HEAD_DIM) + EPS) * g

        _split_heads(qn_ref, head_norm(z_ref[:, 0:ATTN_W], gq_ref[...]))
        _split_heads(kn_ref, head_norm(z_ref[:, ATTN_W:2 * ATTN_W], gk_ref[...]))
        _split_heads(vb_ref, z_ref[:, 2 * ATTN_W:3 * ATTN_W])
        u = z_ref[:, U_COL0:U_COL0 + SSM_W]
        uf_ref[...] = u
        ub_ref[...] = u.astype(BF16)
        fl = _forget_logits(z_ref, bf_ref)
        lf = jnp.minimum(fl, 0.0) - jnp.log1p(jnp.exp(-jnp.abs(fl)))
        row = lax.broadcasted_iota(jnp.int32, (tm, tm), 0)
        col = lax.broadcasted_iota(jnp.int32, (tm, tm), 1)
        tri = (row >= col).astype(BF16)
        c = _dot_exact_l(tri, lf) + carry_ref[...]
        c_ref[...] = c
        carry_ref[...] = c[tm - 1:tm, :]

    row_spec = lambda w: pl.BlockSpec((tm, w), lambda i: (i, 0))
    const = lambda shape: pl.BlockSpec(shape, lambda i: (0, 0))
    heads = pl.BlockSpec((HEADS, tm, HEAD_DIM), lambda i: (0, i, 0))
    return _pcall(body, name="attn_prep", grid=(s // tm,),
                  in_specs=[row_spec(Z_COLS), const((1, ATTN_W)), const((1, ATTN_W)), const((1, LANES)), const((ATTN_W, ATTN_W))],
                  out_specs=[heads] * 3 + [row_spec(SSM_W), row_spec(SSM_W), row_spec(LANES)],
                  out_shape=[_sds((HEADS, s, HEAD_DIM), BF16)] * 3 + [_sds((s, SSM_W), BF16), _sds((s, SSM_W)), _sds((s, LANES))],
                  scratch_shapes=[pltpu.VMEM((1, LANES), F32)], dims=("arbitrary",))(z, gq, gk, bf, gg)


def _attn_fwd(qh, kh, vh, crow, hosted=None):
    _, s, _ = qh.shape
    tq = _row_tile(s)
    scale = HEAD_DIM ** -0.5

    hp = HEADS
    nq = s // tq
    fold = lambda t, op: op(t[:, :tq // 2], t[:, tq // 2:])

    def body(q_ref, k_ref, v_ref, c_ref, o_ref, lse_ref, s_s):
        i = pl.program_id(1)

        def first(j, ms, diagonal):
            off = pl.multiple_of(j * tq, tq)
            out = []
            for hh in range(hp):
                sc = _dot_nt(q_ref[hh], k_ref[hh, pl.ds(off, tq), :]) * scale - c_ref[hh, :, pl.ds(off, tq)]
                if diagonal:
                    causal = lax.broadcasted_iota(jnp.int32, (tq, tq), 1) <= lax.broadcasted_iota(jnp.int32, (tq, tq), 0)
                    sc = jnp.where(causal, sc, NEG_INF)
                s_s[hh, j] = sc
                out.append(jnp.maximum(ms[hh], fold(sc, jnp.maximum)))
            return tuple(out)

        ms = lax.fori_loop(0, i, lambda j, c: first(j, c, False), (jnp.full((tq, tq // 2), NEG_INF, F32),) * hp)
        ms = [jnp.max(t, axis=-1, keepdims=True) for t in first(i, ms, True)]

        def second(j, carry):
            rows = pl.ds(pl.multiple_of(j * tq, tq), tq)
            out = []
            for hh in range(hp):
                ls, acc = carry[hh]
                p = jnp.exp(s_s[hh, j] - ms[hh])
                out.append((ls + fold(p, jnp.add), acc + _dot(p.astype(BF16), v_ref[hh, rows, :])))
            return tuple(out)

        zero = (jnp.zeros((tq, tq // 2), F32), jnp.zeros((tq, HEAD_DIM), F32))
        for hh, (ls, acc) in enumerate(lax.fori_loop(0, i + 1, second, (zero,) * hp)):
            l = jnp.sum(ls, axis=-1, keepdims=True)
            o_ref[hh] = acc / l
            lse_ref[hh] = ms[hh] + jnp.log(l)

    blk = pl.BlockSpec((hp, tq, HEAD_DIM), lambda h, i: (h, i, 0))
    full = pl.BlockSpec((hp, s, HEAD_DIM), lambda h, i: (h, 0, 0))
    nh = HEADS // hp
    first = lambda: jnp.logical_and(pl.program_id(0) == 0, pl.program_id(1) == 0)
    last = lambda: jnp.logical_and(pl.program_id(0) == nh - 1, pl.program_id(1) == nq - 1)
    return _host_pcall(body, hosted, first, last, n_in=4, n_out=2, n_scratch=1, name="attn_fwd", grid=(nh, nq),
                       in_specs=[blk, full, full, pl.BlockSpec((hp, 1, s), lambda h, i: (h, 0, 0))],
                       out_specs=[blk, pl.BlockSpec((hp, tq, 1), lambda h, i: (h, i, 0))],
                       out_shape=[_sds((HEADS, s, HEAD_DIM)), _sds((HEADS, s, 1))],
                       scratch_shapes=[pltpu.VMEM((hp, nq, tq, tq), F32)],
                       dims=("parallel", "parallel"), operands=(qh, kh, vh, crow))


def _ssm_param_fn(lr, li, ls, br, bi):
    step = jnp.exp(ls)
    er = jnp.exp(lr * step)
    ab_re = er * jnp.cos(li * step)
    ab_im = er * jnp.sin(li * step)
    num_re = ab_re - 1.0
    num_im = ab_im
    den = lr * lr + li * li
    f_re = (num_re * lr + num_im * li) / den
    f_im = (num_im * lr - num_re * li) / den
    bb_re = f_re * br - f_im * bi
    bb_im = f_re * bi + f_im * br
    return ab_re, ab_im, bb_re, bb_im


_PARAM_SHAPE = (SSM_GROUPS * SSM_GROUP, SSM_STATE)


def _ssm_params(lr, li, ls, br, bi):
    def body(lr_ref, li_ref, ls_ref, br_ref, bi_ref, ar_ref, ai_ref, bbr_ref, bbi_ref):
        ar, ai, bbr, bbi = _ssm_param_fn(lr_ref[...], li_ref[...], ls_ref[...], br_ref[...], bi_ref[...])
        ar_ref[...] = ar
        ai_ref[...] = ai
        bbr_ref[...] = bbr
        bbi_ref[...] = bbi

    spec = pl.BlockSpec(_PARAM_SHAPE, lambda: (0, 0))
    return _pcall(body, name="ssm_params", in_specs=[spec] * 5, out_specs=[spec] * 4,
                  out_shape=[_sds(_PARAM_SHAPE)] * 4)(lr, li, ls, br, bi)


def _ssm_params_bwd(lr, li, ls, br, bi, dar, dai, dbbr, dbbi, expand_t):
    def body(lr_ref, li_ref, ls_ref, br_ref, bi_ref, dar_ref, dai_ref, dbbr_ref, dbbi_ref, et_ref,
             dlr_ref, dli_ref, dls_ref, dbr_ref, dbi_ref):
        _, vjp = jax.vjp(_ssm_param_fn, lr_ref[...], li_ref[...], ls_ref[...], br_ref[...], bi_ref[...])
        dlr, dli, dls, dbr, dbi = vjp((dar_ref[...], dai_ref[...], dbbr_ref[...], dbbi_ref[...]))
        et = et_ref[...]
        dlr_ref[...] = _dot_exact_l(et, dlr)
        dli_ref[...] = _dot_exact_l(et, dli)
        dls_ref[...] = jnp.sum(_dot_exact_l(et, dls), axis=-1, keepdims=True)
        dbr_ref[...] = dbr
        dbi_ref[...] = dbi

    spec = pl.BlockSpec(_PARAM_SHAPE, lambda: (0, 0))
    gspec = pl.BlockSpec((SSM_GROUPS, SSM_STATE), lambda: (0, 0))
    return _pcall(body, name="ssm_params_bwd",
                  in_specs=[spec] * 9 + [pl.BlockSpec((SSM_GROUPS, _PARAM_SHAPE[0]), lambda: (0, 0))],
                  out_specs=[gspec, gspec, pl.BlockSpec((SSM_GROUPS, 1), lambda: (0, 0)), spec, spec],
                  out_shape=[_sds((SSM_GROUPS, SSM_STATE))] * 2 + [_sds((SSM_GROUPS, 1))] + [_sds(_PARAM_SHAPE)] * 2,
                  )(lr, li, ls, br, bi, dar, dai, dbbr, dbbi, expand_t)


def _cmul(ar, ai, br, bi):
    return ar * br - ai * bi, ar * bi + ai * br


def _scan_consts(ar, ai, width, reverse):
    row = lax.broadcasted_iota(jnp.int32, (SUBLANES, width), 0)
    pw = [(ar, ai)]
    for _ in range(SUBLANES - 1):
        pw.append(_cmul(pw[-1][0], pw[-1][1], ar, ai))
    steps = []
    for d in (1, 2, 4):
        keep = (row < SUBLANES - d) if reverse else (row >= d)
        steps.append((d, jnp.where(keep, pw[d - 1][0], 0.0), jnp.where(keep, pw[d - 1][1], 0.0)))
    pr = jnp.zeros((SUBLANES, width), F32)
    pi = jnp.zeros((SUBLANES, width), F32)
    for r in range(SUBLANES):
        e = (SUBLANES - r) if reverse else (r + 1)
        pr = jnp.where(row == r, pw[e - 1][0], pr)
        pi = jnp.where(row == r, pw[e - 1][1], pi)
    return steps, pr, pi


def _scan_tile(xr, xi, cr, ci, consts, reverse):
    steps, pr, pi = consts
    for d, mr, mi in steps:
        sh = (SUBLANES - d) if reverse else d
        sr = pltpu.roll(xr, sh, 0)
        si = pltpu.roll(xi, sh, 0)
        xr, xi = xr + mr * sr - mi * si, xi + mr * si + mi * sr
    return xr + pr * cr - pi * ci, xi + pr * ci + pi * cr


def _ssm_fwd(ub, uf, bbr, bbi, ar, ai, ccr, cci, dsk, hosted=None):
    s = ub.shape[0]
    tm = min(SSM_ROWS, s)
    nt = tm // SUBLANES

    def body(ub_ref, u_ref, bbr_ref, bbi_ref, ar_ref, ai_ref, ccr_ref, cci_ref, dsk_ref,
             xr_ref, xi_ref, y_ref, cr_s, ci_s):
        i = pl.program_id(1)

        @pl.when(i == 0)
        def _():
            cr_s[...] = jnp.zeros_like(cr_s)
            ci_s[...] = jnp.zeros_like(ci_s)

        u_b = ub_ref[...]
        xr_ref[...] = _dot(u_b, bbr_ref[0])
        xi_ref[...] = _dot(u_b, bbi_ref[0])
        consts = _scan_consts(ar_ref[0], ai_ref[0], CHUNK_S, False)

        def tile(k, carry):
            cr, ci = carry
            sl = pl.ds(pl.multiple_of(k * SUBLANES, SUBLANES), SUBLANES)
            xr, xi = _scan_tile(xr_ref[sl, :], xi_ref[sl, :], cr, ci, consts, False)
            xr_ref[sl, :] = xr
            xi_ref[sl, :] = xi
            return xr[SUBLANES - 1:SUBLANES, :], xi[SUBLANES - 1:SUBLANES, :]

        cr, ci = lax.fori_loop(0, nt, tile, (cr_s[...], ci_s[...]))
        cr_s[...] = cr
        ci_s[...] = ci
        y_ref[...] = (_dot(xr_ref[...].astype(BF16), ccr_ref[0]) - _dot(xi_ref[...].astype(BF16), cci_ref[0])
                      + dsk_ref[...] * u_ref[...])

    wspec = lambda a, b: pl.BlockSpec((1, a, b), lambda j, i: (j, 0, 0))
    nb = s // tm
    first = lambda: jnp.logical_and(pl.program_id(0) == 0, pl.program_id(1) == 0)
    last = lambda: jnp.logical_and(pl.program_id(0) == SSM_CHUNKS - 1, pl.program_id(1) == nb - 1)
    return _host_pcall(
        body, hosted, first, last, n_in=9, n_out=3, n_scratch=2, name="ssm_fwd", grid=(SSM_CHUNKS, nb),
        in_specs=[pl.BlockSpec((tm, CHUNK_U), lambda j, i: (i, j)),
                  pl.BlockSpec((tm, CHUNK_U), lambda j, i: (i, j)),
                  wspec(CHUNK_U, CHUNK_S), wspec(CHUNK_U, CHUNK_S), wspec(1, CHUNK_S), wspec(1, CHUNK_S),
                  wspec(CHUNK_S, CHUNK_U), wspec(CHUNK_S, CHUNK_U),
                  pl.BlockSpec((1, CHUNK_U), lambda j, i: (0, j))],
        out_specs=[pl.BlockSpec((tm, CHUNK_S), lambda j, i: (i, j)), pl.BlockSpec((tm, CHUNK_S), lambda j, i: (i, j)),
                   pl.BlockSpec((tm, CHUNK_U), lambda j, i: (i, j))],
        out_shape=[_sds((s, N_STATE)), _sds((s, N_STATE)), _sds((s, SSM_W))],
        scratch_shapes=[pltpu.VMEM((1, CHUNK_S), F32)] * 2,
        dims=("parallel", "arbitrary"), operands=(ub, uf, bbr, bbi, ar, ai, ccr, cci, dsk))


def _ssm_glu(y, w_glu, b_glu):
    ge = _gelu(y)
    sg = _sigmoid(_dot(ge.astype(BF16), w_glu) + b_glu)
    return ge, sg


def _mix_out(y, att, x, w_glu, b_glu, g_att, g_ssm, w_out, g_ffn, hosted=None):
    s = x.shape[0]
    tm = _row_tile(s)

    def body(y_ref, att_ref, x_ref, wg_ref, bg_ref, ga_ref, gs_ref, wo_ref, gf_ref, x1_ref, mix_ref, h2_ref):
        ge, sg = _ssm_glu(y_ref[...], wg_ref[...], bg_ref[...])
        ms = _rms(ge * sg, gs_ref[...]).astype(BF16)
        ma = _rms(_merge_heads(att_ref), ga_ref[...]).astype(BF16)
        mix_ref[:, 0:ATTN_W] = ma
        mix_ref[:, ATTN_W:D_MODEL] = ms
        x1 = x_ref[...] + (_dot(ma, wo_ref[0:ATTN_W, :]) + _dot(ms, wo_ref[ATTN_W:D_MODEL, :]))
        x1_ref[...] = x1
        h2_ref[...] = _rms(x1, gf_ref[...]).astype(BF16)

    row = lambda w: pl.BlockSpec((tm, w), lambda i: (i, 0))
    const = lambda a, b: pl.BlockSpec((a, b), lambda i: (0, 0))
    nb = s // tm
    return _host_pcall(body, hosted, lambda: pl.program_id(0) == 0, lambda: pl.program_id(0) == nb - 1,
                       n_in=9, n_out=3, n_scratch=0, name="mix_out", grid=(nb,),
                       in_specs=[row(SSM_W), pl.BlockSpec((HEADS, tm, HEAD_DIM), lambda i: (0, i, 0)), row(D_MODEL),
                                 const(SSM_W, SSM_W), const(1, SSM_W),
                                 const(1, ATTN_W), const(1, SSM_W), const(D_MODEL, D_MODEL), const(1, D_MODEL)],
                       out_specs=[row(D_MODEL)] * 3,
                       out_shape=[_sds((s, D_MODEL)), _sds((s, D_MODEL), BF16), _sds((s, D_MODEL), BF16)],
                       scratch_shapes=[], dims=("parallel",), operands=(y, att, x, w_glu, b_glu, g_att, g_ssm, w_out, g_ffn))


CONV_CHUNK = 64


def _conv_rows(pad_ref, w, b, r0, n):
    y = b + pad_ref[pl.ds(r0 + SUBLANES - 2, n), :] * w[0:1, :]
    y = y + pad_ref[pl.ds(r0 + SUBLANES - 1, n), :] * w[1:2, :]
    return y + pad_ref[pl.ds(r0 + SUBLANES, n), :] * w[2:3, :]


def _fill_front_pad(pad_ref, strip_ref, s):
    pad_ref[0:SUBLANES, :] = jnp.zeros((SUBLANES, STRIP), F32)
    for r0 in range(0, s, CONV_CHUNK):
        pad_ref[pl.ds(SUBLANES + r0, CONV_CHUNK), :] = strip_ref[pl.ds(r0, CONV_CHUNK), :]


def _conv_act(up, conv_w, conv_b):
    s = up.shape[0]

    def body(ug_ref, uv_ref, wg_ref, wv_ref, bg_ref, bv_ref, act_ref, pg_ref, pv_ref):
        _fill_front_pad(pg_ref, ug_ref, s)
        _fill_front_pad(pv_ref, uv_ref, s)
        wg, wv, bg, bv = wg_ref[...], wv_ref[...], bg_ref[...], bv_ref[...]
        for r0 in range(0, s, CONV_CHUNK):
            hg = _conv_rows(pg_ref, wg, bg, r0, CONV_CHUNK)
            hv = _conv_rows(pv_ref, wv, bv, r0, CONV_CHUNK)
            act_ref[pl.ds(r0, CONV_CHUNK), :] = (hg * _sigmoid(hg) * hv).astype(BF16)

    strip = lambda off: pl.BlockSpec((s, STRIP), lambda j: (0, j + off))
    wsp = lambda off: pl.BlockSpec((3, STRIP), lambda j: (0, j + off))
    bsp = lambda off: pl.BlockSpec((1, STRIP), lambda j: (0, j + off))
    return _pcall(body, name="conv_act", grid=(N_STRIPS,),
                  in_specs=[strip(0), strip(N_STRIPS), wsp(0), wsp(N_STRIPS), bsp(0), bsp(N_STRIPS)],
                  out_specs=pl.BlockSpec((s, STRIP), lambda j: (0, j)), out_shape=_sds((s, D_FF), BF16),
                  scratch_shapes=[pltpu.VMEM((s + SUBLANES, STRIP), F32)] * 2,
                  dims=("parallel",))(up, up, conv_w, conv_w, conv_b, conv_b)


def _down_loss(act, w_down, x1, tgt):
    s = x1.shape[0]
    tm = _row_tile(s)

    def body(a_ref, w_ref, x1_ref, t_ref, dy_ref, dyb_ref, loss_ref):
        i = pl.program_id(0)

        @pl.when(i == 0)
        def _():
            loss_ref[...] = jnp.zeros_like(loss_ref)

        diff = x1_ref[...] + _dot(a_ref[...], w_ref[...]) - t_ref[...]
        dy = diff * (1.0 / D_MODEL)
        dy_ref[...] = dy
        dyb_ref[...] = dy.astype(BF16)
        loss_ref[...] += 0.5 * jnp.sum(diff * dy)

    row = lambda w: pl.BlockSpec((tm, w), lambda i: (i, 0))
    return _pcall(body, name="down_loss", grid=(s // tm,),
                  in_specs=[row(D_FF), pl.BlockSpec((D_FF, D_MODEL), lambda i: (0, 0)), row(D_MODEL), row(D_MODEL)],
                  out_specs=[row(D_MODEL), row(D_MODEL), pl.BlockSpec((SUBLANES, LANES), lambda i: (0, 0))],
                  out_shape=[_sds((s, D_MODEL)), _sds((s, D_MODEL), BF16), _sds((SUBLANES, LANES))],
                  dims=("arbitrary",))(act, w_down, x1, tgt)


def _conv_act_bwd(up, dact, conv_w, conv_b):
    s = up.shape[0]
    ch = CONV_CHUNK

    def body(ug_ref, uv_ref, da_ref, wg_ref, wv_ref, bg_ref, bv_ref, dup_ref, dcw_ref, pg_ref, pv_ref, dg_ref, dv_ref):
        _fill_front_pad(pg_ref, ug_ref, s)
        _fill_front_pad(pv_ref, uv_ref, s)
        zero = jnp.zeros((SUBLANES, STRIP), F32)
        dg_ref[pl.ds(s, SUBLANES), :] = zero
        dv_ref[pl.ds(s, SUBLANES), :] = zero
        wg, wv, bg, bv = wg_ref[...], wv_ref[...], bg_ref[...], bv_ref[...]
        tile_sum = lambda t: jnp.sum(t.reshape(ch // SUBLANES, SUBLANES, STRIP), axis=0)
        accs = [[zero] * 4, [zero] * 4]
        for r0 in range(0, s, ch):
            hg = _conv_rows(pg_ref, wg, bg, r0, ch)
            hv = _conv_rows(pv_ref, wv, bv, r0, ch)
            sg = _sigmoid(hg)
            da = da_ref[pl.ds(r0, ch), :]
            dhs = (da * hv * (sg * (1.0 + hg * (1.0 - sg))), da * (hg * sg))
            for half, (dh, d_ref, p_ref) in enumerate(zip(dhs, (dg_ref, dv_ref), (pg_ref, pv_ref))):
                d_ref[pl.ds(r0, ch), :] = dh
                for k in range(3):
                    accs[half][k] = accs[half][k] + tile_sum(dh * p_ref[pl.ds(r0 + SUBLANES - 2 + k, ch), :])
                accs[half][3] = accs[half][3] + tile_sum(dh)
        for half, (d_ref, w) in enumerate(((dg_ref, wg), (dv_ref, wv))):
            for r0 in range(0, s, ch):
                dup = (d_ref[pl.ds(r0, ch), :] * w[2:3, :] + d_ref[pl.ds(r0 + 1, ch), :] * w[1:2, :]
                       + d_ref[pl.ds(r0 + 2, ch), :] * w[0:1, :])
                dup_ref[half, pl.ds(r0, ch), :] = dup.astype(BF16)
            rid = lax.broadcasted_iota(jnp.int32, (SUBLANES, STRIP), 0)
            out = zero
            for k in range(4):
                out = jnp.where(rid == k, jnp.sum(accs[half][k], axis=0, keepdims=True), out)
            dcw_ref[half] = out

    strip = lambda off: pl.BlockSpec((s, STRIP), lambda j: (0, j + off))
    wsp = lambda off: pl.BlockSpec((3, STRIP), lambda j: (0, j + off))
    bsp = lambda off: pl.BlockSpec((1, STRIP), lambda j: (0, j + off))
    return _pcall(body, name="conv_act_bwd", grid=(N_STRIPS,),
                  in_specs=[strip(0), strip(N_STRIPS), strip(0), wsp(0), wsp(N_STRIPS), bsp(0), bsp(N_STRIPS)],
                  out_specs=[pl.BlockSpec((2, s, STRIP), lambda j: (0, 0, j)), pl.BlockSpec((2, SUBLANES, STRIP), lambda j: (0, 0, j))],
                  out_shape=[_sds((2, s, D_FF), BF16), _sds((2, SUBLANES, D_FF))],
                  scratch_shapes=[pltpu.VMEM((s + SUBLANES, STRIP), F32)] * 4,
                  dims=("parallel",))(up, up, dact, conv_w, conv_w, conv_b, conv_b)


def _mix_bwd(dy, dh2, x1, g_ffn, w_out, y, att, w_glu, b_glu, g_att, g_ssm, hosted=None):
    s = dy.shape[0]
    tm = _row_tile(s)

    def body(dy_ref, dh2_ref, x1_ref, gf_ref, wo_ref, y_ref, att_ref, wg_ref, bg_ref, ga_ref, gs_ref,
             dx1_ref, dx1b_ref, datt_ref, dys_ref, dwg_ref, dgf_ref, dga_ref, dgs_ref, dbg_ref):
        i = pl.program_id(0)

        @pl.when(i == 0)
        def _():
            for r in (dwg_ref, dgf_ref, dga_ref, dgs_ref, dbg_ref):
                r[...] = jnp.zeros_like(r)

        dxn, dgf = _rms_bwd(x1_ref[...], gf_ref[...], dh2_ref[...])
        dx1 = dy_ref[...] + dxn
        dx1_ref[...] = dx1
        dx1b = dx1.astype(BF16)
        dx1b_ref[...] = dx1b
        dgf_ref[...] += dgf
        dma = _dot_nt(dx1b, wo_ref[0:ATTN_W, :])
        dms = _dot_nt(dx1b, wo_ref[ATTN_W:D_MODEL, :])
        datt, dga = _rms_bwd(_merge_heads(att_ref), ga_ref[...], dma)
        _split_heads(datt_ref, datt)
        dga_ref[...] += dga
        yv = y_ref[...]
        ge, sg = _ssm_glu(yv, wg_ref[...], bg_ref[...])
        dssm, dgs = _rms_bwd(ge * sg, gs_ref[...], dms)
        dgs_ref[...] += dgs
        dgl = dssm * ge * sg * (1.0 - sg)
        dglb = dgl.astype(BF16)
        dge = dssm * sg + _dot_nt(dglb, wg_ref[...])
        dbg_ref[...] += jnp.sum(dgl, axis=0, keepdims=True)
        dwg_ref[...] += _dot_tn(ge.astype(BF16), dglb)
        dys_ref[...] = dge * _gelu_grad(yv)

    row = lambda w: pl.BlockSpec((tm, w), lambda i: (i, 0))
    const = lambda a, b: pl.BlockSpec((a, b), lambda i: (0, 0))
    heads = pl.BlockSpec((HEADS, tm, HEAD_DIM), lambda i: (0, i, 0))
    nb = s // tm
    return _host_pcall(
        body, hosted, lambda: pl.program_id(0) == 0, lambda: pl.program_id(0) == nb - 1, n_in=11, n_out=9, n_scratch=0,
        name="mix_bwd", grid=(nb,),
        in_specs=[row(D_MODEL), row(D_MODEL), row(D_MODEL), const(1, D_MODEL), const(D_MODEL, D_MODEL), row(SSM_W),
                  heads, const(SSM_W, SSM_W), const(1, SSM_W), const(1, ATTN_W), const(1, SSM_W)],
        out_specs=[row(D_MODEL), row(D_MODEL), heads, row(SSM_W), const(SSM_W, SSM_W), const(1, D_MODEL),
                   const(1, ATTN_W), const(1, SSM_W), const(1, SSM_W)],
        out_shape=[_sds((s, D_MODEL)), _sds((s, D_MODEL), BF16), _sds((HEADS, s, HEAD_DIM)), _sds((s, SSM_W)),
                   _sds((SSM_W, SSM_W)), _sds((1, D_MODEL)), _sds((1, ATTN_W)), _sds((1, SSM_W)), _sds((1, SSM_W))],
        scratch_shapes=[], dims=("arbitrary",), operands=(dy, dh2, x1, g_ffn, w_out, y, att, w_glu, b_glu, g_att, g_ssm))


def _ssm_bwd(dys, uf, ub, xr, xi, bbr, bbi, ar, ai, ccr, cci, dsk, hosted=None):
    s = dys.shape[0]
    tm = min(SSM_ROWS, s)
    nb = s // tm
    nt = tm // SUBLANES

    def body(dy_ref, u_ref, ub_ref, xr_ref, xi_ref, xrp_ref, xip_ref, bbr_ref, bbi_ref, ar_ref, ai_ref, ccr_ref,
             cci_ref, dsk_ref, du_ref, dbbr_ref, dbbi_ref, dccr_ref, dcci_ref, dar_ref, dai_ref, dd_ref,
             gr_s, gi_s, cr_s, ci_s, accr_s, acci_s):
        i = pl.program_id(1)
        first_block = i == nb - 1

        @pl.when(i == 0)
        def _():
            for r in (cr_s, ci_s, accr_s, acci_s, dbbr_ref, dbbi_ref, dccr_ref, dcci_ref, dd_ref):
                r[...] = jnp.zeros_like(r)

        dy = dy_ref[...]
        dyb = dy.astype(BF16)
        gr_s[...] = _dot_nt(dyb, ccr_ref[0])
        gi_s[...] = -_dot_nt(dyb, cci_ref[0])
        consts = _scan_consts(ar_ref[0], -ai_ref[0], CHUNK_S, True)
        row = lax.broadcasted_iota(jnp.int32, (SUBLANES, CHUNK_S), 0)

        def tile(kk, carry):
            cr, ci, accr, acci = carry
            k = nt - 1 - kk
            sl = pl.ds(pl.multiple_of(k * SUBLANES, SUBLANES), SUBLANES)
            gr, gi = _scan_tile(gr_s[sl, :], gi_s[sl, :], cr, ci, consts, True)
            gr_s[sl, :] = gr
            gi_s[sl, :] = gi
            slp = pl.ds(pl.multiple_of(jnp.maximum(k - 1, 0) * SUBLANES, SUBLANES), SUBLANES)
            inner = k > 0
            pr_t = jnp.where(inner, xr_ref[slp, :], xrp_ref[...])
            pi_t = jnp.where(inner, xi_ref[slp, :], xip_ref[...])
            live = jnp.logical_or(inner, jnp.logical_not(first_block))
            top_r = jnp.where(live, pltpu.roll(pr_t, 1, 0), 0.0)
            top_i = jnp.where(live, pltpu.roll(pi_t, 1, 0), 0.0)
            xpr = jnp.where(row == 0, top_r, pltpu.roll(xr_ref[sl, :], 1, 0))
            xpi = jnp.where(row == 0, top_i, pltpu.roll(xi_ref[sl, :], 1, 0))
            accr = accr + gr * xpr + gi * xpi
            acci = acci + gi * xpr - gr * xpi
            return gr[0:1, :], gi[0:1, :], accr, acci

        zeros = jnp.zeros((SUBLANES, CHUNK_S), F32)
        cr, ci, accr, acci = lax.fori_loop(0, nt, tile, (cr_s[...], ci_s[...], zeros, zeros))
        cr_s[...] = cr
        ci_s[...] = ci
        accr_s[...] += accr
        acci_s[...] += acci
        grb = gr_s[...].astype(BF16)
        gib = gi_s[...].astype(BF16)
        u_b = ub_ref[...]
        du_ref[...] = _dot_nt(grb, bbr_ref[0]) + _dot_nt(gib, bbi_ref[0]) + dsk_ref[...] * dy
        dbbr_ref[0] += _dot_tn(u_b, grb)
        dbbi_ref[0] += _dot_tn(u_b, gib)
        dccr_ref[0] += _dot_tn(xr_ref[...].astype(BF16), dyb)
        dcci_ref[0] -= _dot_tn(xi_ref[...].astype(BF16), dyb)
        dd_ref[...] += jnp.sum(dy * u_ref[...], axis=0, keepdims=True)

        @pl.when(i == nb - 1)
        def _():
            dar_ref[0] = jnp.sum(accr_s[...], axis=0, keepdims=True)
            dai_ref[0] = jnp.sum(acci_s[...], axis=0, keepdims=True)

    tiles_per_block = tm // SUBLANES
    rb = lambda i: nb - 1 - i
    wspec = lambda a, b: pl.BlockSpec((1, a, b), lambda j, i: (j, 0, 0))
    xblk = pl.BlockSpec((tm, CHUNK_S), lambda j, i: (rb(i), j))
    xprev = pl.BlockSpec((SUBLANES, CHUNK_S), lambda j, i: (jnp.maximum(rb(i) * tiles_per_block - 1, 0), j))
    ublk = pl.BlockSpec((tm, CHUNK_U), lambda j, i: (rb(i), j))
    first = lambda: jnp.logical_and(pl.program_id(0) == 0, pl.program_id(1) == 0)
    last = lambda: jnp.logical_and(pl.program_id(0) == SSM_CHUNKS - 1, pl.program_id(1) == nb - 1)
    return _host_pcall(
        body, hosted, first, last, n_in=14, n_out=8, n_scratch=6, name="ssm_bwd", grid=(SSM_CHUNKS, nb),
        in_specs=[ublk, ublk, ublk, xblk, xblk, xprev, xprev,
                  wspec(CHUNK_U, CHUNK_S), wspec(CHUNK_U, CHUNK_S), wspec(1, CHUNK_S), wspec(1, CHUNK_S),
                  wspec(CHUNK_S, CHUNK_U), wspec(CHUNK_S, CHUNK_U), pl.BlockSpec((1, CHUNK_U), lambda j, i: (0, j))],
        out_specs=[ublk, wspec(CHUNK_U, CHUNK_S), wspec(CHUNK_U, CHUNK_S), wspec(CHUNK_S, CHUNK_U),
                   wspec(CHUNK_S, CHUNK_U), wspec(1, CHUNK_S), wspec(1, CHUNK_S),
                   pl.BlockSpec((1, CHUNK_U), lambda j, i: (0, j))],
        out_shape=[_sds((s, SSM_W)), _sds((SSM_CHUNKS, CHUNK_U, CHUNK_S)), _sds((SSM_CHUNKS, CHUNK_U, CHUNK_S)),
                   _sds((SSM_CHUNKS, CHUNK_S, CHUNK_U)), _sds((SSM_CHUNKS, CHUNK_S, CHUNK_U)),
                   _sds((SSM_CHUNKS, 1, CHUNK_S)), _sds((SSM_CHUNKS, 1, CHUNK_S)), _sds((1, SSM_W))],
        scratch_shapes=[pltpu.VMEM((tm, CHUNK_S), F32)] * 2 + [pltpu.VMEM((1, CHUNK_S), F32)] * 2
                       + [pltpu.VMEM((SUBLANES, CHUNK_S), F32)] * 2,
        dims=("parallel", "arbitrary"), operands=(dys, uf, ub, xr, xi, xr, xi, bbr, bbi, ar, ai, ccr, cci, dsk))


def _attn_probs(q, ks, cs, lse, scale, diagonal):
    p = jnp.exp(_dot_nt(q, ks) * scale - cs - lse)
    if diagonal:
        tq, tk = p.shape
        causal = lax.broadcasted_iota(jnp.int32, (tq, tk), 1) <= lax.broadcasted_iota(jnp.int32, (tq, tk), 0)
        p = jnp.where(causal, p, 0.0)
    return p


def _attn_bwd(qh, kh, vh, crow, lse, doh, hosted=None):
    _, s, _ = qh.shape
    tq = _row_tile(s)
    nq = s // tq
    scale = HEAD_DIM ** -0.5
    hp = HEADS_PER_STEP

    def body(q_ref, k_ref, v_ref, c_ref, lse_ref, do_ref, dq_ref, dk_ref, dv_ref, dc_ref, p_s, dp_s):
        i = pl.program_id(1)

        @pl.when(i == 0)
        def _():
            for r in (dk_ref, dv_ref, dc_ref):
                r[...] = jnp.zeros_like(r)

        dobs = [do_ref[hh].astype(BF16) for hh in range(hp)]

        def first(j, dls, diagonal):
            off = pl.multiple_of(j * tq, tq)
            out = []
            for hh in range(hp):
                p = _attn_probs(q_ref[hh], k_ref[hh, pl.ds(off, tq), :], c_ref[hh, :, pl.ds(off, tq)], lse_ref[hh],
                                scale, diagonal)
                dp = _dot_nt(dobs[hh], v_ref[hh, pl.ds(off, tq), :])
                p_s[hh, j] = p
                dp_s[hh, j] = dp
                out.append(dls[hh] + jnp.sum(p * dp, axis=-1, keepdims=True))
            return tuple(out)

        zero_col = jnp.zeros((tq, 1), F32)
        dls = lax.fori_loop(0, i, lambda j, c: first(j, c, False), (zero_col,) * hp)
        dls = first(i, dls, True)

        def second(j, dqs):
            rows = pl.ds(pl.multiple_of(j * tq, tq), tq)
            out = []
            for hh in range(hp):
                p = p_s[hh, j]
                ds = p * (dp_s[hh, j] - dls[hh])
                dsb = ds.astype(BF16)
                dv_ref[hh, rows, :] += _dot_tn(p.astype(BF16), dobs[hh])
                dk_ref[hh, rows, :] += _dot_tn(dsb, q_ref[hh]) * scale
                dc_ref[hh, :, rows] -= jnp.sum(ds, axis=0, keepdims=True)
                out.append(dqs[hh] + _dot(dsb, k_ref[hh, rows, :]))
            return tuple(out)

        dqs = lax.fori_loop(0, i + 1, second, (jnp.zeros((tq, HEAD_DIM), F32),) * hp)
        for hh in range(hp):
            dq_ref[hh] = dqs[hh] * scale

    blk = pl.BlockSpec((hp, tq, HEAD_DIM), lambda h, i: (h, i, 0))
    full = pl.BlockSpec((hp, s, HEAD_DIM), lambda h, i: (h, 0, 0))
    crow_spec = pl.BlockSpec((hp, 1, s), lambda h, i: (h, 0, 0))
    nh = HEADS // hp
    first = lambda: jnp.logical_and(pl.program_id(0) == 0, pl.program_id(1) == 0)
    last = lambda: jnp.logical_and(pl.program_id(0) == nh - 1, pl.program_id(1) == nq - 1)
    return _host_pcall(body, hosted, first, last, n_in=6, n_out=4, n_scratch=2, name="attn_bwd", grid=(nh, nq),
                       in_specs=[blk, full, full, crow_spec, pl.BlockSpec((hp, tq, 1), lambda h, i: (h, i, 0)), blk],
                       out_specs=[blk, full, full, crow_spec],
                       out_shape=[_sds((HEADS, s, HEAD_DIM))] * 3 + [_sds((HEADS, 1, s))],
                       scratch_shapes=[pltpu.VMEM((hp, nq, tq, tq), F32)] * 2,
                       dims=("parallel", "arbitrary"), operands=(qh, kh, vh, crow, lse, doh))


def _prep_bwd(z, dqn, dkn, dv, du, dc, gq, gk, bf, gg, hosted=None):
    s = z.shape[0]
    tm = _row_tile(s)
    nb = s // tm

    def body(z_ref, dqn_ref, dkn_ref, dv_ref, du_ref, dc_ref, gq_ref, gk_ref, bf_ref, gg_ref,
             dz_ref, dgq_ref, dgk_ref, dbf_ref, carry_ref):
        i = pl.program_id(0)

        @pl.when(i == 0)
        def _():
            for r in (dgq_ref, dgk_ref, dbf_ref, carry_ref):
                r[...] = jnp.zeros_like(r)

        gg_m = gg_ref[...]

        def head_norm_bwd(t, g, dn):
            r = lax.rsqrt(_dot_exact_r(t * t, gg_m) * (1.0 / HEAD_DIM) + EPS)
            w = dn * g
            mean_wt = _dot_exact_r(w * t, gg_m) * (1.0 / HEAD_DIM)
            return r * w - t * (r * r * r) * mean_wt, jnp.sum(dn * t * r, axis=0, keepdims=True)

        dq, dgq = head_norm_bwd(z_ref[:, 0:ATTN_W], gq_ref[...], _merge_heads(dqn_ref))
        dk, dgk = head_norm_bwd(z_ref[:, ATTN_W:2 * ATTN_W], gk_ref[...], _merge_heads(dkn_ref))
        dgq_ref[...] += dgq
        dgk_ref[...] += dgk
        row = lax.broadcasted_iota(jnp.int32, (tm, tm), 0)
        col = lax.broadcasted_iota(jnp.int32, (tm, tm), 1)
        triu = (col >= row).astype(BF16)
        dlf = _dot_exact_l(triu, dc_ref[...]) + carry_ref[...]
        carry_ref[...] = dlf[0:1, :]
        df = dlf * _sigmoid(-_forget_logits(z_ref, bf_ref))
        dbf_ref[...] += jnp.sum(df, axis=0, keepdims=True)
        dz_ref[:, 0:ATTN_W] = dq.astype(BF16)
        dz_ref[:, ATTN_W:2 * ATTN_W] = dk.astype(BF16)
        dz_ref[:, 2 * ATTN_W:3 * ATTN_W] = _merge_heads(dv_ref).astype(BF16)
        tail = jnp.concatenate([df[:, :HEADS], du_ref[...], jnp.zeros((tm, Z_COLS - IN_COLS), F32)], axis=-1)
        dz_ref[:, F_COL0:Z_COLS] = tail.astype(BF16)

    row_spec = lambda w: pl.BlockSpec((tm, w), lambda i: (nb - 1 - i, 0))
    const = lambda shape: pl.BlockSpec(shape, lambda i: (0, 0))
    return _host_pcall(
        body, hosted, lambda: pl.program_id(0) == 0, lambda: pl.program_id(0) == nb - 1, n_in=10, n_out=4, n_scratch=1,
        name="prep_bwd", grid=(nb,),
        in_specs=[row_spec(Z_COLS)] + [pl.BlockSpec((HEADS, tm, HEAD_DIM), lambda i: (0, nb - 1 - i, 0))] * 3
                 + [row_spec(ATTN_W), row_spec(LANES), const((1, ATTN_W)),
                    const((1, ATTN_W)), const((1, LANES)), const((ATTN_W, ATTN_W))],
        out_specs=[row_spec(Z_COLS), const((1, ATTN_W)), const((1, ATTN_W)), const((1, LANES))],
        out_shape=[_sds((s, Z_COLS), BF16), _sds((1, ATTN_W)), _sds((1, ATTN_W)), _sds((1, LANES))],
        scratch_shapes=[pltpu.VMEM((1, LANES), F32)], dims=("arbitrary",),
        operands=(z, dqn, dkn, dv, du, dc, gq, gk, bf, gg))


def _in_norm_bwd(x, g_mix, dh, dx1, hosted=None):
    s = x.shape[0]
    tm = _row_tile(s)

    def body(x_ref, g_ref, dh_ref, dx1_ref, dx_ref, dg_ref):
        i = pl.program_id(0)

        @pl.when(i == 0)
        def _():
            dg_ref[...] = jnp.zeros_like(dg_ref)

        dxn, dg = _rms_bwd(x_ref[...], g_ref[...], dh_ref[...])
        dx_ref[...] = dx1_ref[...] + dxn
        dg_ref[...] += dg

    row = pl.BlockSpec((tm, D_MODEL), lambda i: (i, 0))
    vec = pl.BlockSpec((1, D_MODEL), lambda i: (0, 0))
    nb = s // tm
    return _host_pcall(body, hosted, lambda: pl.program_id(0) == 0, lambda: pl.program_id(0) == nb - 1,
                       n_in=4, n_out=2, n_scratch=0, name="in_norm_bwd", grid=(nb,), in_specs=[row, vec, row, row],
                       out_specs=[row, vec], out_shape=[_sds((s, D_MODEL)), _sds((1, D_MODEL))], scratch_shapes=[],
                       dims=("arbitrary",), operands=(x, g_mix, dh, dx1))


def _adamw_refs(w_ref, g_ref, m_ref, v_ref, d_ref, mo_ref, vo_ref):
    gv = g_ref[...]
    mn = ADAM_B1 * m_ref[...] + (1.0 - ADAM_B1) * gv
    vn = ADAM_B2 * v_ref[...] + (1.0 - ADAM_B2) * (gv * gv)
    m_hat = mn / (1.0 - ADAM_B1 ** ADAM_STEP)
    v_hat = vn / (1.0 - ADAM_B2 ** ADAM_STEP)
    d_ref[...] = -ADAM_LR * (m_hat / (jnp.sqrt(v_hat) + ADAM_EPS) + ADAM_WD * w_ref[...])
    mo_ref[...] = mn
    vo_ref[...] = vn


def _adamw_small(ws, gs, ms, vs):
    n = len(ws)

    def body(*refs):
        ins, outs = refs[:4 * n], refs[4 * n:]
        for i in range(n):
            _adamw_refs(ins[i], ins[n + i], ins[2 * n + i], ins[3 * n + i], *outs[3 * i:3 * i + 3])

    vm = pl.BlockSpec(memory_space=pltpu.VMEM)
    out_shape = [_sds(w.shape) for w in ws for _ in range(3)]
    return _pallas(body, name="adamw_small", in_specs=[vm] * (4 * n), out_specs=[vm] * (3 * n), out_shape=out_shape,
                   compiler_params=pltpu.CompilerParams(vmem_limit_bytes=VMEM_LIMIT))(*ws, *gs, *ms, *vs)


def _adamw(w, g, m, v, *, name):
    r, c = w.shape
    tr = r
    for cand in (256, 176, 128, 64):
        if r > cand and r % cand == 0:
            tr = cand
            break

    def body(w_ref, g_ref, m_ref, v_ref, d_ref, mo_ref, vo_ref):
        _adamw_refs(w_ref, g_ref, m_ref, v_ref, d_ref, mo_ref, vo_ref)

    spec = pl.BlockSpec((tr, c), lambda i: (i, 0))
    return _pcall(body, name=name, grid=(r // tr,), in_specs=[spec] * 4, out_specs=[spec] * 3,
                  out_shape=[_sds((r, c))] * 3, dims=("parallel",))(w, g, m, v)


def _prefetch_call(body, *, name, grid, in_specs, out_specs, out_shape, operands):
    grid_spec = pltpu.PrefetchScalarGridSpec(num_scalar_prefetch=1, grid=grid, in_specs=in_specs, out_specs=out_specs)
    params = pltpu.CompilerParams(dimension_semantics=("parallel",) * len(grid), vmem_limit_bytes=VMEM_LIMIT)
    return _pallas(body, name=name, grid_spec=grid_spec, out_shape=out_shape, compiler_params=params)(*operands)


def _place_cols(buf, shard, place):
    rows, cols = shard.shape
    tr = 256

    def body(place_ref, s_ref, b_ref, o_ref):
        o_ref[...] = s_ref[...]

    grid_spec = pltpu.PrefetchScalarGridSpec(
        num_scalar_prefetch=1, grid=(rows // tr,),
        in_specs=[pl.BlockSpec((tr, cols), lambda i, p: (i, 0)), pl.BlockSpec(memory_space=pltpu.HBM)],
        out_specs=pl.BlockSpec((tr, cols), lambda i, p: (i, p[0])))
    return _pallas(body, name="place_own_cols", grid_spec=grid_spec, out_shape=_sds(buf.shape, buf.dtype),
                   input_output_aliases={2: 0},
                   compiler_params=pltpu.CompilerParams(dimension_semantics=("parallel",),
                                                        vmem_limit_bytes=VMEM_LIMIT))(place, shard, buf)


def _half_rows_tile(hr):
    return hr if hr <= 256 else 176 if hr % 176 == 0 else 256


def _add_half(g, landed, place, *, name):
    def body(place_ref, g_ref, l_ref, o_ref):
        own = g_ref[0] if len(g_ref.shape) == 4 else g_ref[...]
        o_ref[...] = (own + l_ref[...]).astype(BF16)

    if g.ndim == 4:
        _, _, hr, c = g.shape
        tr = _half_rows_tile(hr)
        blk = (1, tr, c)
        return _prefetch_call(
            body, name=name, grid=(N_CHIPS, hr // tr),
            in_specs=[pl.BlockSpec((1,) + blk, lambda j, i, p: (j, p[1], i, 0)), pl.BlockSpec(blk, lambda j, i, p: (j, i, 0))],
            out_specs=pl.BlockSpec(blk, lambda j, i, p: (j, i, 0)), out_shape=_sds(landed.shape, BF16),
            operands=(place, g, landed))
    hr, c = landed.shape
    tr, tc = 256, _tile(c, 2176)
    nb = hr // tr
    return _prefetch_call(
        body, name=name, grid=(nb, c // tc),
        in_specs=[pl.BlockSpec((tr, tc), lambda i, j, p: (p[1] * nb + i, j)), pl.BlockSpec((tr, tc), lambda i, j, p: (i, j))],
        out_specs=pl.BlockSpec((tr, tc), lambda i, j, p: (i, j)), out_shape=_sds(landed.shape, BF16),
        operands=(place, g, landed))


def _sum_chips(chip_sum, lands, place, *, name, tc, window_stride=0):
    _, hr, c = lands.shape
    tr = _half_rows_tile(hr)
    nb = hr // tr
    ncb = c // tc

    def body(place_ref, own_ref, a_ref, b_ref, c_ref, o_ref):
        own = own_ref[0] if len(own_ref.shape) == 3 else own_ref[...]
        o_ref[...] = ((own.astype(F32) + a_ref[0].astype(F32)) + b_ref[0].astype(F32)) + c_ref[0].astype(F32)

    land = lambda k: pl.BlockSpec((1, tr, tc), lambda i, j, p: ((p[0] + k) % N_CHIPS, i, j))
    if chip_sum.ndim == 3:
        own_spec = land(0)
    else:
        stride = window_stride // tc
        own_spec = pl.BlockSpec((tr, tc), lambda i, j, p: (i, p[0] * stride + j))
    return _prefetch_call(
        body, name=name, grid=(nb, ncb), in_specs=[own_spec, land(1), land(2), land(3)],
        out_specs=pl.BlockSpec((tr, tc), lambda i, j, p: (p[1] * nb + i, j)), out_shape=_sds((2 * hr, c)),
        operands=(place, chip_sum, lands, lands, lands))


_HBM = pl.BlockSpec(memory_space=pltpu.HBM)


def _place():
    x, y, c = lax.axis_index("x"), lax.axis_index("y"), lax.axis_index("c")
    chips = [(1 - x, y), (x, 1 - y), (1 - x, 1 - y)]
    return x, y, c, chips


def _rcopy(src, dst, send_sem, recv_sem, to):
    return pltpu.make_async_remote_copy(src_ref=src, dst_ref=dst, send_sem=send_sem, recv_sem=recv_sem,
                                        device_id=to, device_id_type=MESH)


UP_COLS = 2 * D_FF // N_CHIPS
IN_WINDOW = 640
IN_STRIDE = 512


class _Hosted:
    def __init__(self, operands, out_shapes, n_sems, start, finish, aliases=None, local_sems=0):
        self.operands, self.out_shapes, self.n_sems = list(operands), list(out_shapes), n_sems
        self.start, self.finish, self.aliases, self.local_sems = start, finish, dict(aliases or {}), local_sems

    def scratch(self):
        return ([pltpu.SemaphoreType.DMA((self.n_sems,)), pltpu.SemaphoreType.DMA((self.n_sems,))]
                + [pltpu.SemaphoreType.DMA] * self.local_sems)


def _both(a, b):
    na, nao, nas = len(a.operands), len(a.out_shapes), len(a.scratch())

    def start(ins, outs, sems):
        a.start(ins[:na], outs[:nao], sems[:nas])
        b.start(ins[na:], outs[nao:], sems[nas:])

    def finish(ins, outs, sems):
        a.finish(ins[:na], outs[:nao], sems[:nas])
        b.finish(ins[na:], outs[nao:], sems[nas:])

    both = _Hosted(a.operands + b.operands, a.out_shapes + b.out_shapes, 0, start, finish,
                   aliases={**a.aliases, **{na + i: nao + o for i, o in b.aliases.items()}})
    both.scratch = lambda: a.scratch() + b.scratch()
    return both


def _then(a, b):
    nas = len(a.scratch())

    def finish(ins, outs, sems):
        a.finish(ins, outs, sems[:nas])
        b.start(ins, outs, sems[nas:])
        b.finish(ins, outs, sems[nas:])

    chain = _Hosted(a.operands, a.out_shapes, 0, lambda ins, outs, sems: a.start(ins, outs, sems[:nas]), finish,
                    aliases=a.aliases)
    chain.scratch = lambda: a.scratch() + b.scratch()
    return chain


def _run_hosted(hosted, *, name):
    n_in, n_out = len(hosted.operands), len(hosted.out_shapes)

    def body(*refs):
        parts = (refs[:n_in], refs[n_in:n_in + n_out], refs[n_in + n_out:])
        hosted.start(*parts)
        hosted.finish(*parts)

    return _pallas(body, name=name, in_specs=[_HBM] * n_in, out_specs=[_HBM] * n_out, out_shape=hosted.out_shapes,
                   input_output_aliases=hosted.aliases, scratch_shapes=hosted.scratch())(*hosted.operands)


def _host_pcall(core_body, hosted, first, last, *, n_in, n_out, n_scratch, name, grid, in_specs, out_specs, out_shape,
                scratch_shapes, dims, operands):
    if hosted is None:
        outs = _pcall(core_body, name=name, grid=grid, in_specs=in_specs, out_specs=out_specs, out_shape=out_shape,
                      scratch_shapes=scratch_shapes, dims=dims)(*operands)
        return outs, []
    hi, ho = len(hosted.operands), len(hosted.out_shapes)

    def body(*refs):
        a, b = n_in, n_in + hi
        c, d = b + n_out, b + n_out + ho
        e = d + n_scratch
        parts = (refs[a:b], refs[c:d], refs[e:])

        @pl.when(first())
        def _():
            hosted.start(*parts)

        core_body(*refs[:a], *refs[b:c], *refs[d:e])

        @pl.when(last())
        def _():
            hosted.finish(*parts)

    params = pltpu.CompilerParams(dimension_semantics=("arbitrary",) * len(grid), vmem_limit_bytes=VMEM_LIMIT)
    outs = _pallas(body, name=name, grid=grid, in_specs=list(in_specs) + [_HBM] * hi, out_specs=list(out_specs) + [_HBM] * ho,
                   out_shape=list(out_shape) + hosted.out_shapes, scratch_shapes=list(scratch_shapes) + hosted.scratch(),
                   input_output_aliases={n_in + a: n_out + b for a, b in hosted.aliases.items()},
                   compiler_params=params)(*operands, *hosted.operands)
    return outs[:n_out], outs[n_out:]


WHOLE_HALF = (0, 1, 1)


def _band_rows(src, hc, band):
    first, count, of = band
    hr = src.shape[0] // 2
    return pl.ds(hc * hr + first * (hr // of), count * (hr // of))


def _gather_slot(src, out, chip, hc, band=WHOLE_HALF):
    cols = src.shape[1]
    if len(out.shape) == 2:
        return out.at[_band_rows(src, hc, band), pl.ds(pl.multiple_of(chip * cols, LANES), cols)]
    return out.at[chip, _band_rows(src, hc, band), :]


def _gathered_shape(shard, by_cols):
    if by_cols:
        return _sds((shard.shape[0], N_CHIPS * shard.shape[1]), shard.dtype)
    return _sds((N_CHIPS,) + shard.shape, shard.dtype)


def _plan_gather_ici(shards, by_cols, whole=(), bands=None, into=None):
    n = len(shards)
    bands = bands or [WHOLE_HALF] * n
    into = into or [None] * n
    given = [w for w in range(n) if into[w] is not None]
    n_ops = n + len(whole)

    def copies(ins, outs, sems):
        send_sems, recv_sems = sems[0], sems[1]
        x, y, c, chips = _place()
        me = 2 * x + y
        sends, waits = [], []
        for w in range(n + len(whole)):
            for k, (cx, cy) in enumerate(chips):
                sem = (send_sems.at[3 * w + k], recv_sems.at[3 * w + k])
                if w < n:
                    sends.append(_rcopy(ins[w].at[_band_rows(ins[w], c, bands[w]), :],
                                        _gather_slot(ins[w], outs[w], me, c, bands[w]), *sem, (cx, cy, c)))
                    landed = _gather_slot(ins[w], outs[w], 2 * cx + cy, c, bands[w])
                else:
                    sends.append(_rcopy(ins[w], outs[w].at[me], *sem, (cx, cy, c)))
                    landed = outs[w].at[2 * cx + cy]
                waits.append(_rcopy(landed, landed, *sem, (cx, cy, c)))
        return sends, waits

    def start(ins, outs, sems):
        for cp in copies(ins, outs, sems)[0]:
            cp.start()

    def finish(ins, outs, sems):
        sends, waits = copies(ins, outs, sems)
        for cp in waits:
            cp.wait_recv()
        for cp in sends:
            cp.wait_send()

    out_shapes = [_gathered_shape(s, bc) for s, bc in zip(shards, by_cols)] + [_sds((N_CHIPS,) + a.shape, a.dtype) for a in whole]
    return _Hosted(list(shards) + list(whole) + [into[w] for w in given], out_shapes, 3 * n_ops, start, finish,
                   aliases={n_ops + i: w for i, w in enumerate(given)})


def _plan_gather_d2d(bufs, shard_shapes, bands=None):
    n = len(bufs)
    bands = bands or [WHOLE_HALF] * n

    def copies(ins, outs, sems):
        send_sems, recv_sems = sems
        x, y, c, chips = _place()
        sibling = (x, y, 1 - c)
        sends, waits = [], []
        for w in range(n):
            for k, (cx, cy) in enumerate(chips):
                sem = (send_sems.at[3 * w + k], recv_sems.at[3 * w + k])
                landed = _gather_slot(shard_shapes[w], outs[w], 2 * cx + cy, c, bands[w])
                other = _gather_slot(shard_shapes[w], outs[w], 2 * cx + cy, 1 - c, bands[w])
                sends.append(_rcopy(landed, landed, *sem, sibling))
                waits.append(_rcopy(other, other, *sem, sibling))
        return sends, waits

    def start(ins, outs, sems):
        for cp in copies(ins, outs, sems)[0]:
            cp.start()

    def finish(ins, outs, sems):
        sends, waits = copies(ins, outs, sems)
        for cp in waits:
            cp.wait_recv()
        for cp in sends:
            cp.wait_send()

    return _Hosted(bufs, [_sds(b.shape, b.dtype) for b in bufs], 3 * n, start, finish, aliases={w: w for w in range(n)})


def _plan_allgather_first(block):
    def copies(ins, outs, sems):
        send_sems, recv_sems = sems
        x, y, c, chips = _place()
        me = 4 * x + 2 * y + c
        peers = [(x, y, 1 - c)] + [(cx, cy, c) for cx, cy in chips]
        sends = [_rcopy(ins[0], outs[0].at[me], send_sems.at[k], recv_sems.at[k], p) for k, p in enumerate(peers)]
        waits = [_rcopy(outs[0].at[4 * px + 2 * py + pc], outs[0].at[4 * px + 2 * py + pc], send_sems.at[k],
                        recv_sems.at[k], (px, py, pc)) for k, (px, py, pc) in enumerate(peers)]
        return sends, waits

    def start(ins, outs, sems):
        for cp in copies(ins, outs, sems)[0]:
            cp.start()

    def finish(ins, outs, sems):
        sends, waits = copies(ins, outs, sems)
        for cp in waits:
            cp.wait_recv()
        for cp in sends:
            cp.wait_send()

    return _Hosted([block], [_sds((8,) + block.shape)], 4, start, finish)


def _plan_allgather_second(gathered):
    def copies(ins, outs, sems):
        send_sems, recv_sems = sems
        x, y, c, chips = _place()
        sends, waits = [], []
        for k, (cx, cy) in enumerate(chips):
            landed = outs[0].at[4 * cx + 2 * cy + c]
            other = outs[0].at[4 * cx + 2 * cy + 1 - c]
            sends.append(_rcopy(landed, landed, send_sems.at[k], recv_sems.at[k], (x, y, 1 - c)))
            waits.append(_rcopy(other, other, send_sems.at[k], recv_sems.at[k], (x, y, 1 - c)))
        return sends, waits

    def start(ins, outs, sems):
        for cp in copies(ins, outs, sems)[0]:
            cp.start()

    def finish(ins, outs, sems):
        sends, waits = copies(ins, outs, sems)
        for cp in waits:
            cp.wait_recv()
        for cp in sends:
            cp.wait_send()

    return _Hosted([gathered], [_sds(gathered.shape)], 3, start, finish, aliases={0: 0})


def _place_block(gathered, block, device):
    rows, lanes = block.shape

    def body(dev_ref, b_ref, g_ref, o_ref):
        o_ref[0] = b_ref[...]

    grid_spec = pltpu.PrefetchScalarGridSpec(
        num_scalar_prefetch=1, grid=(1,),
        in_specs=[pl.BlockSpec((rows, lanes), lambda i, d: (0, 0)), pl.BlockSpec(memory_space=pltpu.HBM)],
        out_specs=pl.BlockSpec((1, rows, lanes), lambda i, d: (d[0], 0, 0)))
    return _pallas(body, name="place_own_block", grid_spec=grid_spec, out_shape=_sds(gathered.shape),
                   input_output_aliases={2: 0},
                   compiler_params=pltpu.CompilerParams(dimension_semantics=("arbitrary",),
                                                        vmem_limit_bytes=VMEM_LIMIT))(device, block, gathered)


def _sum_devices(gathered):
    _, rows, lanes = gathered.shape
    tr = rows // 2 if rows % 16 == 0 else rows

    def body(g_ref, o_ref):
        acc = g_ref[0]
        for d in range(1, 8):
            acc = acc + g_ref[d]
        o_ref[...] = acc

    return _pcall(body, name="sum_devices", grid=(rows // tr,), in_specs=[pl.BlockSpec((8, tr, lanes), lambda i: (0, i, 0))],
                  out_specs=pl.BlockSpec((tr, lanes), lambda i: (i, 0)), out_shape=_sds((rows, lanes)), dims=("parallel",))(gathered)


def _plan_swap(grads):
    def copies(ins, outs, sems):
        send_sems, recv_sems = sems
        x, y, c, _ = _place()
        cps = []
        for w, g_ref in enumerate(ins):
            if len(g_ref.shape) == 4:
                theirs = g_ref.at[:, 1 - c]
            else:
                hr = g_ref.shape[0] // 2
                theirs = g_ref.at[pl.ds((1 - c) * hr, hr), :]
            cps.append(_rcopy(theirs, outs[w], send_sems.at[w], recv_sems.at[w], (x, y, 1 - c)))
        return cps

    def start(ins, outs, sems):
        for cp in copies(ins, outs, sems):
            cp.start()

    def finish(ins, outs, sems):
        for cp in copies(ins, outs, sems):
            cp.wait()

    out_shapes = [_sds((g.shape[0], g.shape[2], g.shape[3])) if g.ndim == 4 else _sds((g.shape[0] // 2, g.shape[1]))
                  for g in grads]
    return _Hosted(grads, out_shapes, len(grads), start, finish)


def _plan_scatter(chip_sums, windows):
    def copies(ins, outs, sems):
        send_sems, recv_sems = sems
        x, y, c, chips = _place()
        me = 2 * x + y
        sends, waits = [], []
        for w, s_ref in enumerate(ins):
            for k, (cx, cy) in enumerate(chips):
                tgt = 2 * cx + cy
                if windows[w] is not None:
                    stride, width = windows[w]
                    part = s_ref.at[:, pl.ds(pl.multiple_of(tgt * stride, LANES), width)]
                else:
                    part = s_ref.at[tgt]
                sem = (send_sems.at[3 * w + k], recv_sems.at[3 * w + k])
                sends.append(_rcopy(part, outs[w].at[me], *sem, (cx, cy, c)))
                slot = outs[w].at[tgt]
                waits.append(_rcopy(slot, slot, *sem, (cx, cy, c)))
        return sends, waits

    def start(ins, outs, sems):
        for cp in copies(ins, outs, sems)[0]:
            cp.start()

    def finish(ins, outs, sems):
        sends, waits = copies(ins, outs, sems)
        for cp in waits:
            cp.wait_recv()
        for cp in sends:
            cp.wait_send()

    out_shapes = [_sds((N_CHIPS, s.shape[0], win[1]), BF16) if win is not None else _sds(s.shape, BF16)
                  for s, win in zip(chip_sums, windows)]
    return _Hosted(chip_sums, out_shapes, 3 * len(chip_sums), start, finish)


def _plan_join(reds):
    def copies(ins, outs, sems):
        send_sems, recv_sems = sems
        x, y, c, _ = _place()
        sends, waits = [], []
        for w, out in enumerate(outs):
            hr = out.shape[0] // 2
            mine = out.at[pl.ds(c * hr, hr), :]
            theirs = out.at[pl.ds((1 - c) * hr, hr), :]
            sends.append(_rcopy(mine, mine, send_sems.at[w], recv_sems.at[w], (x, y, 1 - c)))
            waits.append(_rcopy(theirs, theirs, send_sems.at[w], recv_sems.at[w], (x, y, 1 - c)))
        return sends, waits

    def start(ins, outs, sems):
        for cp in copies(ins, outs, sems)[0]:
            cp.start()

    def finish(ins, outs, sems):
        sends, waits = copies(ins, outs, sems)
        for cp in waits:
            cp.wait_recv()
        for cp in sends:
            cp.wait_send()

    return _Hosted(reds, [_sds(r.shape) for r in reds], len(reds), start, finish, aliases={w: w for w in range(len(reds))})


def _allreduce_small(v):
    m_per = v.shape[0]

    def body(v_ref, out_ref, all_ref, send_sems, recv_sems, local_sem):
        x, y, c, chips = _place()
        me, sibling = (x, y, c), (x, y, 1 - c)

        def rows(px, py, pc):
            return all_ref.at[pl.ds((4 * px + 2 * py + pc) * m_per, m_per), :]

        def copy(k, block, to, src=None):
            return _rcopy(rows(*block) if src is None else src, rows(*block), send_sems.at[k], recv_sems.at[k], to)

        mine = pltpu.make_async_copy(v_ref, rows(*me), local_sem)
        mine.start()
        first = [copy(0, me, sibling, src=v_ref)]
        first += [copy(1 + k, me, (*chip, c), src=v_ref) for k, chip in enumerate(chips)]
        for cp in first:
            cp.start()
        passed = [copy(4 + k, (*chip, c), sibling) for k, chip in enumerate(chips)]
        for k, chip in enumerate(chips):
            copy(1 + k, (*chip, c), me).wait_recv()
            passed[k].start()
        copy(0, sibling, me).wait_recv()
        for k, chip in enumerate(chips):
            copy(4 + k, (*chip, 1 - c), me).wait_recv()
        for cp in first + passed:
            cp.wait_send()
        mine.wait()
        acc = all_ref[pl.ds(0, m_per), :]
        for d in range(1, 8):
            acc = acc + all_ref[pl.ds(d * m_per, m_per), :]
        out_ref[...] = acc

    vm = pl.BlockSpec(memory_space=pltpu.VMEM)
    return _pallas(body, name="allreduce_small", in_specs=[vm], out_specs=vm, out_shape=_sds((m_per, LANES)),
                          scratch_shapes=[pltpu.VMEM((8 * m_per, LANES), F32), pltpu.SemaphoreType.DMA((7,)),
                                          pltpu.SemaphoreType.DMA((7,)), pltpu.SemaphoreType.DMA],
                          compiler_params=pltpu.CompilerParams(vmem_limit_bytes=VMEM_LIMIT))(v)


def _block_diag(blocks):
    j, g, a, b = blocks.shape
    eye = jnp.eye(g, dtype=bool)[None, :, None, :, None]
    return jnp.where(eye, blocks[:, :, :, None, :], jnp.zeros((), blocks.dtype)).reshape(j, g * a, g * b)


def _diag_blocks(m, a, b):
    j = m.shape[0]
    g = m.shape[1] // a
    t = m.reshape(j, g, a, g, b)
    eye = jnp.eye(g, dtype=bool)[None, :, None, :, None]
    return jnp.sum(jnp.where(eye, t, 0.0), axis=3)


_SMALL = (("g_mix", (1024,)), ("b_f", (8,)), ("g_q", (64,)), ("g_k", (64,)), ("lambda_re", (32, 64)),
          ("lambda_im", (32, 64)), ("log_step", (32,)), ("b_re", (32, 64, 16)), ("b_im", (32, 64, 16)),
          ("c_re", (32, 16, 64)), ("c_im", (32, 16, 64)), ("d_skip", (32, 16)), ("b_glu", (512,)),
          ("g_attn_out", (512,)), ("g_ssm_out", (512,)), ("g_ffn", (1024,)), ("conv_b", (5632,)))


_LATE_SMALL = ("g_mix", "b_f", "g_q", "g_k")
_EARLY_SMALL = tuple(n for n, _ in _SMALL if n not in _LATE_SMALL)


def _packed_rows(n):
    tile = SUBLANES * LANES
    return -(-n // tile) * SUBLANES


def _pack_small(arrs):
    parts = []
    for a in arrs:
        flat = a.reshape(-1)
        rows = _packed_rows(flat.shape[0])
        parts.append(jnp.pad(flat, (0, rows * LANES - flat.shape[0])).reshape(rows, LANES))
    return jnp.concatenate(parts, axis=0)


def _unpack_small(buf, shapes):
    out, r = [], 0
    for shape in shapes:
        n = math.prod(shape)
        out.append(buf[r:r + _packed_rows(n)].reshape(-1)[:n].reshape(shape))
        r += _packed_rows(n)
    return out


def _halves(t):
    return t.reshape(N_CHIPS, 2, t.shape[0] // (2 * N_CHIPS), t.shape[1])


class _MeshComm:
    def __init__(self, args):
        x, y, self.core = lax.axis_index("x"), lax.axis_index("y"), lax.axis_index("c")
        self.chip = 2 * x + y
        self.place = jnp.stack([self.chip, self.core]).astype(jnp.int32)
        self.shards = {n: args[n].astype(BF16) for n in ("w_in", "w_glu", "w_out", "w_up", "w_down")}
        self.conv_w = args["conv_w"]

    def _own(self, stacked, mine):
        return lax.dynamic_update_slice(stacked, mine[None], (self.chip,) + (0,) * mine.ndim)

    def w_in(self):
        sh = self.shards["w_in"]
        (buf,) = _run_hosted(_then(_plan_gather_ici([sh], [False]), _plan_gather_d2d([sh], [sh])), name="gather_w_in")
        whole = self._own(buf, sh).transpose(1, 0, 2).reshape(D_MODEL, IN_COLS)
        return jnp.pad(whole, ((0, 0), (0, Z_COLS - IN_COLS)))

    def gather_first(self):
        self.mid = [self.shards[n] for n in ("w_glu", "w_out", "w_down")]
        return _plan_gather_ici(self.mid + [self.shards["w_up"]], [False, False, False, True], whole=[self.conv_w],
                                bands=[WHOLE_HALF] * 3 + [(0, 1, 4)])

    def gather_second(self, landed):
        self.g_cw = landed[4]
        return _both(_plan_gather_d2d(list(landed[:3]), self.mid),
                     _plan_gather_ici([self.shards["w_up"]], [True], bands=[(1, 3, 4)], into=[landed[3]]))

    def weights(self, gathered):
        g_glu, g_out, g_down = gathered[:3]
        own = self._own
        return (own(g_glu, self.mid[0]).reshape(SSM_W, SSM_W), own(g_out, self.mid[1]).reshape(D_MODEL, D_MODEL),
                own(g_down, self.mid[2]).reshape(D_FF, D_MODEL),
                own(self.g_cw, self.conv_w).transpose(1, 0, 2).reshape(3, 2 * D_FF))

    def gather_third(self, gathered):
        return _plan_gather_d2d([gathered[3]], [self.shards["w_up"]])

    def w_up(self, passed):
        return _place_cols(passed[0], self.shards["w_up"], self.place)

    def swap_down(self, d_w_down):
        self.d_down = _halves(d_w_down)
        return _plan_swap([self.d_down])

    def swap(self, landed_down, d_w_up, d_w_glu, d_w_out):
        self.sum_down = _add_half(self.d_down, landed_down[0], self.place, name="add_w_down")
        self.early = [d_w_up, _halves(d_w_glu), _halves(d_w_out)]
        return _both(_plan_scatter([self.sum_down], [None]), _plan_swap(self.early))

    def scatter(self, landed, small_block):
        self.land_down = landed[0]
        self.early_sums = [_add_half(g, l, self.place, name="add_" + n)
                           for g, l, n in zip(self.early, landed[1:], ("w_up", "w_glu", "w_out"))]
        return _both(_plan_scatter(self.early_sums, [(UP_COLS, UP_COLS), None, None]), _plan_allgather_first(small_block))

    def swap_in(self, d_w_in):
        self.d_in = d_w_in
        return _plan_swap([d_w_in])

    def scatter_in(self, landed):
        self.sum_in = _add_half(self.d_in, landed[0], self.place, name="add_w_in")
        return _plan_scatter([self.sum_in], [(IN_STRIDE, IN_WINDOW)])

    def small_second(self, landed_small):
        return _plan_allgather_second(landed_small[0])

    def reduce(self, lands):
        early_lands, (land_in,) = lands
        sum_in = self.sum_in
        es, el = self.early_sums, early_lands
        todo = [(sum_in, land_in, "w_in", LANES, IN_STRIDE), (es[1], el[1], "w_glu", SSM_W, 0),
                (es[2], el[2], "w_out", D_MODEL, 0), (es[0], el[0], "w_up", UP_COLS, UP_COLS),
                (self.sum_down, self.land_down, "w_down", D_MODEL, 0)]
        reds = _run_hosted(_plan_join([_sum_chips(s, l, self.place, name="sum_" + n, tc=tc, window_stride=st)
                                       for s, l, n, tc, st in todo]), name="join_halves")
        g_big = dict(zip(("w_in", "w_glu", "w_out", "w_up", "w_down"), reds))
        g_big["w_in"] = lax.dynamic_slice_in_dim(reds[0], 2 * self.chip, IN_COLS // N_CHIPS, axis=1)
        return g_big


def _local_step(x, tgt, p, comm):
    s = x.shape[0]
    row = lambda v: v.reshape(1, -1)
    g_mix, g_ffn = row(p["g_mix"]), row(p["g_ffn"])
    g_att, g_ssm, b_glu, conv_b = row(p["g_attn_out"]), row(p["g_ssm_out"]), row(p["b_glu"]), row(p["conv_b"])
    gq = row(jnp.tile(p["g_q"], HEADS))
    gk = row(jnp.tile(p["g_k"], HEADS))
    bf = row(jnp.pad(p["b_f"], (0, LANES - HEADS)))
    gg = jnp.kron(jnp.eye(HEADS, dtype=F32), jnp.ones((HEAD_DIM, HEAD_DIM), F32)).astype(BF16)
    dsk = row(p["d_skip"])

    rep = lambda a: jnp.repeat(a, SSM_GROUP, axis=0)
    lr, li = rep(p["lambda_re"]), rep(p["lambda_im"])
    ls = rep(jnp.broadcast_to(p["log_step"][:, None], (SSM_GROUPS, SSM_STATE)))
    bt_re = p["b_re"].transpose(0, 2, 1).reshape(_PARAM_SHAPE)
    bt_im = p["b_im"].transpose(0, 2, 1).reshape(_PARAM_SHAPE)
    a_re_rep, a_im_rep, bb_re, bb_im = _ssm_params(lr, li, ls, bt_re, bt_im)
    ar = a_re_rep[::SSM_GROUP].reshape(SSM_CHUNKS, 1, CHUNK_S)
    ai = a_im_rep[::SSM_GROUP].reshape(SSM_CHUNKS, 1, CHUNK_S)
    chunked = lambda t: t.reshape(SSM_CHUNKS, SSM_GROUPS // SSM_CHUNKS, SSM_GROUP, SSM_STATE)
    bbr = _block_diag(chunked(bb_re)).astype(BF16)
    bbi = _block_diag(chunked(bb_im)).astype(BF16)
    to_cc = lambda c: _block_diag(chunked(c).transpose(0, 1, 3, 2)).astype(BF16)
    ccr, cci = to_cc(p["c_re"]), to_cc(p["c_im"])

    w_in_r = comm.w_in()
    hb, z = _in_proj(x, g_mix, w_in_r)
    qh, kh, vh, ub, uf, c128 = _attn_prep(z, gq, gk, bf, gg)
    crow = c128[:, :HEADS].T.reshape(HEADS, 1, s)
    (oh, lse), landed = _attn_fwd(qh, kh, vh, crow, comm.gather_first())
    (xr, xi, y), gathered = _ssm_fwd(ub, uf, bbr, bbi, ar, ai, ccr, cci, dsk, comm.gather_second(landed))
    w_glu_b, w_out_b, w_down_b, conv_w_full = comm.weights(gathered)
    (x1, mixb, h2b), passed = _mix_out(y, oh, x, w_glu_b, b_glu, g_att, g_ssm, w_out_b, g_ffn, comm.gather_third(gathered))
    w_up_b = comm.w_up(passed)
    up = _mm(h2b, w_up_b, name="ffn_up", tm=1024, tn=1408, tk=1024)
    act = _conv_act(up, conv_w_full, conv_b)
    dy, dyb, loss_blk = _down_loss(act, w_down_b, x1, tgt)

    d_w_down = _mm(act, dyb, ta=True, name="d_w_down", tm=1408, tn=1024, tk=2048)
    dact = _mm(dyb, w_down_b, tb=True, name="d_act", tm=1024, tn=1408, tk=1024)
    dupb, dcw = _conv_act_bwd(up, dact, conv_w_full, conv_b)
    d_w_up = _mm(h2b, dupb, ta=True, b_parts=2, name="d_w_up", tm=1024, tn=1408, tk=2048)
    dh2 = _mm(dupb, w_up_b, tb=True, a_parts=2, name="d_h2", tm=1024, tn=1024, tk=1408)
    (dx1, dx1b, doh, dys, d_w_glu, d_g_ffn, d_g_att, d_g_ssm, d_b_glu), landed_down = _mix_bwd(
        dy, dh2, x1, g_ffn, w_out_b, y, oh, w_glu_b, b_glu, g_att, g_ssm, comm.swap_down(d_w_down))
    d_w_out = _mm(mixb, dx1b, ta=True, name="d_w_out", tm=1024, tn=1024, tk=2048)
    (du, dbbr, dbbi, dccr, dcci, dar, dai, dd), swapped = _ssm_bwd(dys, uf, ub, xr, xi, bbr, bbi, ar, ai, ccr, cci, dsk,
                                                                comm.swap(landed_down, d_w_up, d_w_glu, d_w_out))
    unchunk = lambda t: t.reshape(_PARAM_SHAPE)
    dbb_re = unchunk(_diag_blocks(dbbr, SSM_GROUP, SSM_STATE))
    dbb_im = unchunk(_diag_blocks(dbbi, SSM_GROUP, SSM_STATE))
    first_row = (jnp.arange(_PARAM_SHAPE[0]) % SSM_GROUP == 0)[:, None]
    da_re = jnp.where(first_row, rep(dar.reshape(SSM_GROUPS, SSM_STATE)), 0.0)
    da_im = jnp.where(first_row, rep(dai.reshape(SSM_GROUPS, SSM_STATE)), 0.0)
    expand_t = (jnp.arange(SSM_GROUPS)[:, None] == (jnp.arange(_PARAM_SHAPE[0]) // SSM_GROUP)[None, :]).astype(BF16)
    d_lr, d_li, d_ls, d_bt_re, d_bt_im = _ssm_params_bwd(lr, li, ls, bt_re, bt_im, da_re, da_im, dbb_re, dbb_im, expand_t)
    from_bt = lambda t: t.reshape(SSM_GROUPS, SSM_GROUP, SSM_STATE).transpose(0, 2, 1)
    from_cc = lambda t: _diag_blocks(t, SSM_STATE, SSM_GROUP).transpose(0, 1, 3, 2).reshape(SSM_GROUPS, SSM_GROUP, SSM_STATE)

    small = {
        "lambda_re": d_lr, "lambda_im": d_li, "log_step": d_ls,
        "b_re": from_bt(d_bt_re), "b_im": from_bt(d_bt_im), "c_re": from_cc(dccr), "c_im": from_cc(dcci),
        "d_skip": dd, "b_glu": d_b_glu, "g_attn_out": d_g_att, "g_ssm_out": d_g_ssm, "g_ffn": d_g_ffn,
        "conv_b": dcw[:, 3],
    }
    d_conv_w = dcw[:, 0:3].transpose(1, 0, 2).reshape(3, 2 * D_FF)
    early_small = _pack_small([small[n] for n in _EARLY_SMALL] + [d_conv_w])

    (dqh, dkh, dvh, dcrow), landed = _attn_bwd(qh, kh, vh, crow, lse, doh, comm.scatter(swapped, early_small))
    early_lands, small_landed = landed[:3], landed[3:]
    dc128 = jnp.pad(dcrow.reshape(HEADS, s).T, ((0, 0), (0, LANES - HEADS)))
    (dzb, d_gq, d_gk, d_bf), small_gathered = _prep_bwd(z, dqh, dkh, dvh, du, dc128, gq, gk, bf, gg,
                                                        comm.small_second(small_landed))
    d_w_in_r = _mm(hb, dzb, ta=True, name="d_w_in", tm=512, tn=Z_COLS, tk=2048)
    dh, swapped_in = _mm(dzb, w_in_r, tb=True, name="d_h", tm=1024, tn=1024, tk=Z_COLS, carry=True,
                         hosted=comm.swap_in(d_w_in_r))
    (dx, d_g_mix), land_in = _in_norm_bwd(x, g_mix, dh, dx1, comm.scatter_in(swapped_in))
    small.update({"g_mix": d_g_mix, "b_f": d_bf[0, :HEADS], "g_q": d_gq.reshape(HEADS, HEAD_DIM).sum(0),
                  "g_k": d_gk.reshape(HEADS, HEAD_DIM).sum(0)})
    big = {"w_in": d_w_in_r, "w_glu": d_w_glu, "w_out": d_w_out, "w_up": d_w_up, "w_down": d_w_down}
    return loss_blk[0, 0], dx, big, small, d_conv_w, (early_lands, land_in, small_gathered, early_small)


def kernel(x, g_mix, w_in, b_f, g_q, g_k, lambda_re, lambda_im, log_step, b_re, b_im, c_re, c_im, d_skip, w_glu, b_glu, g_attn_out, g_ssm_out, w_out, g_ffn, w_up, conv_w, conv_b, w_down, loss_target, m_g_mix, m_w_in, m_b_f, m_g_q, m_g_k, m_lambda_re, m_lambda_im, m_log_step, m_b_re, m_b_im, m_c_re, m_c_im, m_d_skip, m_w_glu, m_b_glu, m_g_attn_out, m_g_ssm_out, m_w_out, m_g_ffn, m_w_up, m_conv_w, m_conv_b, m_w_down, v_g_mix, v_w_in, v_b_f, v_g_q, v_g_k, v_lambda_re, v_lambda_im, v_log_step, v_b_re, v_b_im, v_c_re, v_c_im, v_d_skip, v_w_glu, v_b_glu, v_g_attn_out, v_g_ssm_out, v_w_out, v_g_ffn, v_w_up, v_conv_w, v_conv_b, v_w_down):
    args = dict(locals())
    order = ["g_mix", "w_in", "b_f", "g_q", "g_k", "lambda_re", "lambda_im", "log_step", "b_re", "b_im", "c_re", "c_im",
             "d_skip", "w_glu", "b_glu", "g_attn_out", "g_ssm_out", "w_out", "g_ffn", "w_up", "conv_w", "conv_b", "w_down"]
    comm = _MeshComm(args)
    chip = comm.chip
    loss_part, dx, big, small, d_conv_w, lands = _local_step(x[0], loss_target[0], args, comm)

    g_big = comm.reduce(lands[:2])

    shapes = dict(_SMALL)
    small_names = [n for n, _ in _SMALL]
    device = (2 * chip + comm.core).reshape(1).astype(jnp.int32)
    early = _unpack_small(_sum_devices(_place_block(lands[2][0], lands[3], device)),
                          [shapes[n] for n in _EARLY_SMALL] + [(3, 2 * D_FF)])
    late = _unpack_small(_allreduce_small(_pack_small([small[n] for n in _LATE_SMALL] + [loss_part])),
                         [shapes[n] for n in _LATE_SMALL] + [()])
    loss = late[-1]
    g_conv_w = lax.dynamic_slice_in_dim(early[-1], chip * (2 * D_FF // N_CHIPS), 2 * D_FF // N_CHIPS, axis=1)
    g_small = {**dict(zip(_EARLY_SMALL, early[:-1])), **dict(zip(_LATE_SMALL, late[:-1]))}

    grad, delta, new_m, new_v = {}, {}, {}, {}
    for n in ("w_in", "w_glu", "w_out", "w_up", "w_down"):
        grad[n] = g_big[n]
        delta[n], new_m[n], new_v[n] = _adamw(args[n], g_big[n], args["m_" + n], args["v_" + n], name="adamw_" + n)
    grad["conv_w"] = g_conv_w
    delta["conv_w"], new_m["conv_w"], new_v["conv_w"] = _adamw(conv_w, g_conv_w, m_conv_w, v_conv_w, name="adamw_conv_w")
    stepped = _adamw_small([args[n] for n in small_names], [g_small[n] for n in small_names],
                           [args["m_" + n] for n in small_names], [args["v_" + n] for n in small_names])
    for i, n in enumerate(small_names):
        grad[n] = g_small[n]
        delta[n], new_m[n], new_v[n] = stepped[3 * i:3 * i + 3]

    return (loss, dx[None], *[grad[n] for n in order], *[delta[n] for n in order], *[new_m[n] for n in order],
            *[new_v[n] for n in order])
```

```python
import math

import jax
import jax.numpy as jnp
from jax import lax
from jax.experimental import pallas as pl
from jax.experimental.pallas import tpu as pltpu

F32 = jnp.float32
BF16 = jnp.bfloat16

D_MODEL = 1024
HEADS = 8
HEAD_DIM = 64
ATTN_W = 512
SSM_W = 512
SSM_GROUPS = 32
SSM_GROUP = 16
SSM_STATE = 64
N_STATE = SSM_GROUPS * SSM_STATE
D_FF = 2816
IN_COLS = 2056
Z_COLS = 2176
F_COL0 = 1536
U_COL0 = 1544
EPS = 1e-6
NEG_INF = -1e30
N_CHIPS = 4
LANES = 128
SUBLANES = 8
SSM_CHUNKS = 2
SSM_ROWS = 512
CHUNK_U = SSM_W // SSM_CHUNKS
CHUNK_S = N_STATE // SSM_CHUNKS
HEADS_PER_STEP = 4
STRIP = 128
N_STRIPS = D_FF // STRIP

ADAM_LR = 0.001
ADAM_B1 = 0.9
ADAM_B2 = 0.999
ADAM_EPS = 1e-08
ADAM_WD = 0.01
ADAM_STEP = 10

VMEM_LIMIT = 56 * 1024 * 1024
MESH = pl.DeviceIdType.MESH


def _pallas(body, **kw):
    return pl.pallas_call(body, **kw)


def _pcall(body, *, name, out_shape, in_specs, out_specs, grid=(), scratch_shapes=(), dims=None):
    params = pltpu.CompilerParams(dimension_semantics=dims, vmem_limit_bytes=VMEM_LIMIT)
    return _pallas(body, name=name, grid=grid, in_specs=in_specs, out_specs=out_specs,
                   out_shape=out_shape, scratch_shapes=scratch_shapes, compiler_params=params)


def _sds(shape, dtype=F32):
    return jax.ShapeDtypeStruct(shape, dtype)


def _dot(a, b):
    return jnp.dot(a, b, preferred_element_type=F32)


def _dot_nt(a, b):
    return lax.dot_general(a, b, (((1,), (1,)), ((), ())), preferred_element_type=F32)


def _dot_tn(a, b):
    return lax.dot_general(a, b, (((0,), (0,)), ((), ())), preferred_element_type=F32)


def _split3(x):
    hi = x.astype(BF16)
    r = x - hi.astype(F32)
    mid = r.astype(BF16)
    lo = (r - mid.astype(F32)).astype(BF16)
    return hi, mid, lo


def _dot_exact_r(x, m01):
    hi, mid, lo = _split3(x)
    return _dot(hi, m01) + _dot(mid, m01) + _dot(lo, m01)


def _dot_exact_l(m01, x):
    hi, mid, lo = _split3(x)
    return _dot(m01, hi) + _dot(m01, mid) + _dot(m01, lo)


def _sigmoid(x):
    return 1.0 / (1.0 + jnp.exp(-x))


def _rms(x, g):
    r = lax.rsqrt(jnp.mean(x * x, axis=-1, keepdims=True) + EPS)
    return x * r * g


def _rms_bwd(x, g, dy):
    r = lax.rsqrt(jnp.mean(x * x, axis=-1, keepdims=True) + EPS)
    w = dy * g
    dx = r * w - x * (r * r * r) * jnp.mean(w * x, axis=-1, keepdims=True)
    dg = jnp.sum(dy * x * r, axis=0, keepdims=True)
    return dx, dg


_GELU_K = math.sqrt(2.0 / math.pi)
_GELU_C = 0.044715


def _gelu(y):
    return y * (0.5 * (1.0 + jnp.tanh(_GELU_K * (y + _GELU_C * (y * y * y)))))


def _gelu_grad(y):
    t = jnp.tanh(_GELU_K * (y + _GELU_C * (y * y * y)))
    return 0.5 * (1.0 + t) + 0.5 * y * (1.0 - t * t) * (_GELU_K * (1.0 + 3.0 * _GELU_C * y * y))


def _tile(n, pref):
    if n <= pref:
        return n
    divs = [t for t in range(LANES, n + 1, LANES) if n % t == 0]
    below = [t for t in divs if t <= pref]
    if below and 2 * below[-1] >= pref:
        return below[-1]
    above = [t for t in divs if t > pref]
    return above[0] if above else n


def _row_tile(s):
    return min(256, s)


def _wide_tile(s):
    return min(512, s)


def _mm(a, b, *, name, tm, tn, tk, ta=False, tb=False, a_parts=1, b_parts=1, carry=False, hosted=None):
    if a_parts > 1:
        m, kk = a.shape[1], a.shape[2] * a_parts
    elif ta:
        kk, m = a.shape
    else:
        m, kk = a.shape
    if b_parts > 1:
        n = b.shape[2] * b_parts
    else:
        n = b.shape[0] if tb else b.shape[1]
    tm, tn, tk = _tile(m, tm), _tile(n // b_parts, tn), _tile(kk // a_parts, tk)
    k_per, n_per = kk // a_parts // tk, n // b_parts // tn

    def body(a_ref, b_ref, o_ref):
        k = pl.program_id(2)
        if ta:
            part = _dot_tn(a_ref[...], b_ref[...])
        elif tb:
            part = _dot_nt(a_ref[...], b_ref[...])
        else:
            part = _dot(a_ref[...], b_ref[...])

        @pl.when(k == 0)
        def _():
            o_ref[...] = part

        @pl.when(k > 0)
        def _():
            o_ref[...] += part

    if a_parts > 1:
        a_spec = pl.BlockSpec((None, tm, tk), lambda i, j, k: (k // k_per, i, k % k_per))
    else:
        a_spec = pl.BlockSpec((tk, tm), lambda i, j, k: (k, i)) if ta else pl.BlockSpec((tm, tk), lambda i, j, k: (i, k))
    if b_parts > 1:
        b_spec = pl.BlockSpec((None, tk, tn), lambda i, j, k: (j // n_per, k, j % n_per))
    else:
        b_spec = pl.BlockSpec((tn, tk), lambda i, j, k: (j, k)) if tb else pl.BlockSpec((tk, tn), lambda i, j, k: (k, j))
    grid = (m // tm, n // tn, kk // tk)
    at = lambda step: (lambda: jnp.logical_and(jnp.logical_and(pl.program_id(0) == step[0], pl.program_id(1) == step[1]),
                                               pl.program_id(2) == step[2]))
    (out,), carried = _host_pcall(body, hosted, at((0, 0, 0)), at(tuple(g - 1 for g in grid)), n_in=2, n_out=1, n_scratch=0,
                                  name=name, grid=grid, in_specs=[a_spec, b_spec],
                                  out_specs=[pl.BlockSpec((tm, tn), lambda i, j, k: (i, j))], out_shape=[_sds((m, n))],
                                  scratch_shapes=[], dims=("parallel", "parallel", "arbitrary"), operands=(a, b))
    return (out, carried) if carry else out


def _in_proj(x, g_mix, w_in_r):
    s = x.shape[0]
    tm = _wide_tile(s)

    def body(x_ref, g_ref, w_ref, h_ref, z_ref):
        h = _rms(x_ref[...], g_ref[...]).astype(BF16)
        h_ref[...] = h
        z_ref[...] = _dot(h, w_ref[...])

    return _pcall(body, name="in_proj", grid=(s // tm,),
                  in_specs=[pl.BlockSpec((tm, D_MODEL), lambda i: (i, 0)), pl.BlockSpec((1, D_MODEL), lambda i: (0, 0)),
                            pl.BlockSpec((D_MODEL, Z_COLS), lambda i: (0, 0))],
                  out_specs=[pl.BlockSpec((tm, D_MODEL), lambda i: (i, 0)), pl.BlockSpec((tm, Z_COLS), lambda i: (i, 0))],
                  out_shape=[_sds((s, D_MODEL), BF16), _sds((s, Z_COLS))], dims=("parallel",))(x, g_mix, w_in_r)


def _split_heads(ref, val):
    for h in range(HEADS):
        ref[h] = val[:, h * HEAD_DIM:(h + 1) * HEAD_DIM].astype(ref.dtype)


def _merge_heads(ref):
    return jnp.concatenate([ref[h].astype(F32) for h in range(HEADS)], axis=-1)


def _forget_logits(z_ref, bf_ref):
    fl = z_ref[:, F_COL0:F_COL0 + LANES] + bf_ref[...]
    return jnp.where(lax.broadcasted_iota(jnp.int32, fl.shape, 1) < HEADS, fl, 0.0)


def _attn_prep(z, gq, gk, bf, gg):
    s = z.shape[0]
    tm = _row_tile(s)

    def body(z_ref, gq_ref, gk_ref, bf_ref, gg_ref, qn_ref, kn_ref, vb_ref, ub_ref, uf_ref, c_ref, carry_ref):
        i = pl.program_id(0)

        @pl.when(i == 0)
        def _():
            carry_ref[...] = jnp.zeros_like(carry_ref)

        gg_m = gg_ref[...]

        def head_norm(t, g):
            ssq = _dot_exact_r(t * t, gg_m)
            return t * lax.rsqrt(ssq * (1.0 / HEAD_DIM) + EPS) * g

        _split_heads(qn_ref, head_norm(z_ref[:, 0:ATTN_W], gq_ref[...]))
        _split_heads(kn_ref, head_norm(z_ref[:, ATTN_W:2 * ATTN_W], gk_ref[...]))
        _split_heads(vb_ref, z_ref[:, 2 * ATTN_W:3 * ATTN_W])
        u = z_ref[:, U_COL0:U_COL0 + SSM_W]
        uf_ref[...] = u
        ub_ref[...] = u.astype(BF16)
        fl = _forget_logits(z_ref, bf_ref)
        lf = jnp.minimum(fl, 0.0) - jnp.log1p(jnp.exp(-jnp.abs(fl)))
        row = lax.broadcasted_iota(jnp.int32, (tm, tm), 0)
        col = lax.broadcasted_iota(jnp.int32, (tm, tm), 1)
        tri = (row >= col).astype(BF16)
        c = _dot_exact_l(tri, lf) + carry_ref[...]
        c_ref[...] = c
        carry_ref[...] = c[tm - 1:tm, :]

    row_spec = lambda w: pl.BlockSpec((tm, w), lambda i: (i, 0))
    const = lambda shape: pl.BlockSpec(shape, lambda i: (0, 0))
    heads = pl.BlockSpec((HEADS, tm, HEAD_DIM), lambda i: (0, i, 0))
    return _pcall(body, name="attn_prep", grid=(s // tm,),
                  in_specs=[row_spec(Z_COLS), const((1, ATTN_W)), const((1, ATTN_W)), const((1, LANES)), const((ATTN_W, ATTN_W))],
                  out_specs=[heads] * 3 + [row_spec(SSM_W), row_spec(SSM_W), row_spec(LANES)],
                  out_shape=[_sds((HEADS, s, HEAD_DIM), BF16)] * 3 + [_sds((s, SSM_W), BF16), _sds((s, SSM_W)), _sds((s, LANES))],
                  scratch_shapes=[pltpu.VMEM((1, LANES), F32)], dims=("arbitrary",))(z, gq, gk, bf, gg)


def _attn_fwd(qh, kh, vh, crow, hosted=None):
    _, s, _ = qh.shape
    tq = _row_tile(s)
    scale = HEAD_DIM ** -0.5

    hp = HEADS
    nq = s // tq
    fold = lambda t, op: op(t[:, :tq // 2], t[:, tq // 2:])

    def body(q_ref, k_ref, v_ref, c_ref, o_ref, lse_ref, s_s):
        i = pl.program_id(1)

        def first(j, ms, diagonal):
            off = pl.multiple_of(j * tq, tq)
            out = []
            for hh in range(hp):
                sc = _dot_nt(q_ref[hh], k_ref[hh, pl.ds(off, tq), :]) * scale - c_ref[hh, :, pl.ds(off, tq)]
                if diagonal:
                    causal = lax.broadcasted_iota(jnp.int32, (tq, tq), 1) <= lax.broadcasted_iota(jnp.int32, (tq, tq), 0)
                    sc = jnp.where(causal, sc, NEG_INF)
                s_s[hh, j] = sc
                out.append(jnp.maximum(ms[hh], fold(sc, jnp.maximum)))
            return tuple(out)

        ms = lax.fori_loop(0, i, lambda j, c: first(j, c, False), (jnp.full((tq, tq // 2), NEG_INF, F32),) * hp)
        ms = [jnp.max(t, axis=-1, keepdims=True) for t in first(i, ms, True)]

        def second(j, carry):
            rows = pl.ds(pl.multiple_of(j * tq, tq), tq)
            out = []
            for hh in range(hp):
                ls, acc = carry[hh]
                p = jnp.exp(s_s[hh, j] - ms[hh])
                out.append((ls + fold(p, jnp.add), acc + _dot(p.astype(BF16), v_ref[hh, rows, :])))
            return tuple(out)

        zero = (jnp.zeros((tq, tq // 2), F32), jnp.zeros((tq, HEAD_DIM), F32))
        for hh, (ls, acc) in enumerate(lax.fori_loop(0, i + 1, second, (zero,) * hp)):
            l = jnp.sum(ls, axis=-1, keepdims=True)
            o_ref[hh] = acc / l
            lse_ref[hh] = ms[hh] + jnp.log(l)

    blk = pl.BlockSpec((hp, tq, HEAD_DIM), lambda h, i: (h, i, 0))
    full = pl.BlockSpec((hp, s, HEAD_DIM), lambda h, i: (h, 0, 0))
    nh = HEADS // hp
    first = lambda: jnp.logical_and(pl.program_id(0) == 0, pl.program_id(1) == 0)
    last = lambda: jnp.logical_and(pl.program_id(0) == nh - 1, pl.program_id(1) == nq - 1)
    return _host_pcall(body, hosted, first, last, n_in=4, n_out=2, n_scratch=1, name="attn_fwd", grid=(nh, nq),
                       in_specs=[blk, full, full, pl.BlockSpec((hp, 1, s), lambda h, i: (h, 0, 0))],
                       out_specs=[blk, pl.BlockSpec((hp, tq, 1), lambda h, i: (h, i, 0))],
                       out_shape=[_sds((HEADS, s, HEAD_DIM)), _sds((HEADS, s, 1))],
                       scratch_shapes=[pltpu.VMEM((hp, nq, tq, tq), F32)],
                       dims=("parallel", "parallel"), operands=(qh, kh, vh, crow))


def _ssm_param_fn(lr, li, ls, br, bi):
    step = jnp.exp(ls)
    er = jnp.exp(lr * step)
    ab_re = er * jnp.cos(li * step)
    ab_im = er * jnp.sin(li * step)
    num_re = ab_re - 1.0
    num_im = ab_im
    den = lr * lr + li * li
    f_re = (num_re * lr + num_im * li) / den
    f_im = (num_im * lr - num_re * li) / den
    bb_re = f_re * br - f_im * bi
    bb_im = f_re * bi + f_im * br
    return ab_re, ab_im, bb_re, bb_im


_PARAM_SHAPE = (SSM_GROUPS * SSM_GROUP, SSM_STATE)


def _ssm_params(lr, li, ls, br, bi):
    def body(lr_ref, li_ref, ls_ref, br_ref, bi_ref, ar_ref, ai_ref, bbr_ref, bbi_ref):
        ar, ai, bbr, bbi = _ssm_param_fn(lr_ref[...], li_ref[...], ls_ref[...], br_ref[...], bi_ref[...])
        ar_ref[...] = ar
        ai_ref[...] = ai
        bbr_ref[...] = bbr
        bbi_ref[...] = bbi

    spec = pl.BlockSpec(_PARAM_SHAPE, lambda: (0, 0))
    return _pcall(body, name="ssm_params", in_specs=[spec] * 5, out_specs=[spec] * 4,
                  out_shape=[_sds(_PARAM_SHAPE)] * 4)(lr, li, ls, br, bi)


def _ssm_params_bwd(lr, li, ls, br, bi, dar, dai, dbbr, dbbi, expand_t):
    def body(lr_ref, li_ref, ls_ref, br_ref, bi_ref, dar_ref, dai_ref, dbbr_ref, dbbi_ref, et_ref,
             dlr_ref, dli_ref, dls_ref, dbr_ref, dbi_ref):
        _, vjp = jax.vjp(_ssm_param_fn, lr_ref[...], li_ref[...], ls_ref[...], br_ref[...], bi_ref[...])
        dlr, dli, dls, dbr, dbi = vjp((dar_ref[...], dai_ref[...], dbbr_ref[...], dbbi_ref[...]))
        et = et_ref[...]
        dlr_ref[...] = _dot_exact_l(et, dlr)
        dli_ref[...] = _dot_exact_l(et, dli)
        dls_ref[...] = jnp.sum(_dot_exact_l(et, dls), axis=-1, keepdims=True)
        dbr_ref[...] = dbr
        dbi_ref[...] = dbi

    spec = pl.BlockSpec(_PARAM_SHAPE, lambda: (0, 0))
    gspec = pl.BlockSpec((SSM_GROUPS, SSM_STATE), lambda: (0, 0))
    return _pcall(body, name="ssm_params_bwd",
                  in_specs=[spec] * 9 + [pl.BlockSpec((SSM_GROUPS, _PARAM_SHAPE[0]), lambda: (0, 0))],
                  out_specs=[gspec, gspec, pl.BlockSpec((SSM_GROUPS, 1), lambda: (0, 0)), spec, spec],
                  out_shape=[_sds((SSM_GROUPS, SSM_STATE))] * 2 + [_sds((SSM_GROUPS, 1))] + [_sds(_PARAM_SHAPE)] * 2,
                  )(lr, li, ls, br, bi, dar, dai, dbbr, dbbi, expand_t)


def _cmul(ar, ai, br, bi):
    return ar * br - ai * bi, ar * bi + ai * br


def _scan_consts(ar, ai, width, reverse):
    row = lax.broadcasted_iota(jnp.int32, (SUBLANES, width), 0)
    pw = [(ar, ai)]
    for _ in range(SUBLANES - 1):
        pw.append(_cmul(pw[-1][0], pw[-1][1], ar, ai))
    steps = []
    for d in (1, 2, 4):
        keep = (row < SUBLANES - d) if reverse else (row >= d)
        steps.append((d, jnp.where(keep, pw[d - 1][0], 0.0), jnp.where(keep, pw[d - 1][1], 0.0)))
    pr = jnp.zeros((SUBLANES, width), F32)
    pi = jnp.zeros((SUBLANES, width), F32)
    for r in range(SUBLANES):
        e = (SUBLANES - r) if reverse else (r + 1)
        pr = jnp.where(row == r, pw[e - 1][0], pr)
        pi = jnp.where(row == r, pw[e - 1][1], pi)
    return steps, pr, pi


def _scan_tile(xr, xi, cr, ci, consts, reverse):
    steps, pr, pi = consts
    for d, mr, mi in steps:
        sh = (SUBLANES - d) if reverse else d
        sr = pltpu.roll(xr, sh, 0)
        si = pltpu.roll(xi, sh, 0)
        xr, xi = xr + mr * sr - mi * si, xi + mr * si + mi * sr
    return xr + pr * cr - pi * ci, xi + pr * ci + pi * cr


def _ssm_fwd(ub, uf, bbr, bbi, ar, ai, ccr, cci, dsk, hosted=None):
    s = ub.shape[0]
    tm = min(SSM_ROWS, s)
    nt = tm // SUBLANES

    def body(ub_ref, u_ref, bbr_ref, bbi_ref, ar_ref, ai_ref, ccr_ref, cci_ref, dsk_ref,
             xr_ref, xi_ref, y_ref, cr_s, ci_s):
        i = pl.program_id(1)

        @pl.when(i == 0)
        def _():
            cr_s[...] = jnp.zeros_like(cr_s)
            ci_s[...] = jnp.zeros_like(ci_s)

        u_b = ub_ref[...]
        xr_ref[...] = _dot(u_b, bbr_ref[0])
        xi_ref[...] = _dot(u_b, bbi_ref[0])
        consts = _scan_consts(ar_ref[0], ai_ref[0], CHUNK_S, False)

        def tile(k, carry):
            cr, ci = carry
            sl = pl.ds(pl.multiple_of(k * SUBLANES, SUBLANES), SUBLANES)
            xr, xi = _scan_tile(xr_ref[sl, :], xi_ref[sl, :], cr, ci, consts, False)
            xr_ref[sl, :] = xr
            xi_ref[sl, :] = xi
            return xr[SUBLANES - 1:SUBLANES, :], xi[SUBLANES - 1:SUBLANES, :]

        cr, ci = lax.fori_loop(0, nt, tile, (cr_s[...], ci_s[...]))
        cr_s[...] = cr
        ci_s[...] = ci
        y_ref[...] = (_dot(xr_ref[...].astype(BF16), ccr_ref[0]) - _dot(xi_ref[...].astype(BF16), cci_ref[0])
                      + dsk_ref[...] * u_ref[...])

    wspec = lambda a, b: pl.BlockSpec((1, a, b), lambda j, i: (j, 0, 0))
    nb = s // tm
    first = lambda: jnp.logical_and(pl.program_id(0) == 0, pl.program_id(1) == 0)
    last = lambda: jnp.logical_and(pl.program_id(0) == SSM_CHUNKS - 1, pl.program_id(1) == nb - 1)
    return _host_pcall(
        body, hosted, first, last, n_in=9, n_out=3, n_scratch=2, name="ssm_fwd", grid=(SSM_CHUNKS, nb),
        in_specs=[pl.BlockSpec((tm, CHUNK_U), lambda j, i: (i, j)),
                  pl.BlockSpec((tm, CHUNK_U), lambda j, i: (i, j)),
                  wspec(CHUNK_U, CHUNK_S), wspec(CHUNK_U, CHUNK_S), wspec(1, CHUNK_S), wspec(1, CHUNK_S),
                  wspec(CHUNK_S, CHUNK_U), wspec(CHUNK_S, CHUNK_U),
                  pl.BlockSpec((1, CHUNK_U), lambda j, i: (0, j))],
        out_specs=[pl.BlockSpec((tm, CHUNK_S), lambda j, i: (i, j)), pl.BlockSpec((tm, CHUNK_S), lambda j, i: (i, j)),
                   pl.BlockSpec((tm, CHUNK_U), lambda j, i: (i, j))],
        out_shape=[_sds((s, N_STATE)), _sds((s, N_STATE)), _sds((s, SSM_W))],
        scratch_shapes=[pltpu.VMEM((1, CHUNK_S), F32)] * 2,
        dims=("parallel", "arbitrary"), operands=(ub, uf, bbr, bbi, ar, ai, ccr, cci, dsk))


def _ssm_glu(y, w_glu, b_glu):
    ge = _gelu(y)
    sg = _sigmoid(_dot(ge.astype(BF16), w_glu) + b_glu)
    return ge, sg


def _mix_out(y, att, x, w_glu, b_glu, g_att, g_ssm, w_out, g_ffn, hosted=None):
    s = x.shape[0]
    tm = _wide_tile(s)

    def body(y_ref, att_ref, x_ref, wg_ref, bg_ref, ga_ref, gs_ref, wo_ref, gf_ref, x1_ref, mix_ref, h2_ref):
        ge, sg = _ssm_glu(y_ref[...], wg_ref[...], bg_ref[...])
        ms = _rms(ge * sg, gs_ref[...]).astype(BF16)
        ma = _rms(_merge_heads(att_ref), ga_ref[...]).astype(BF16)
        mix_ref[:, 0:ATTN_W] = ma
        mix_ref[:, ATTN_W:D_MODEL] = ms
        x1 = x_ref[...] + (_dot(ma, wo_ref[0:ATTN_W, :]) + _dot(ms, wo_ref[ATTN_W:D_MODEL, :]))
        x1_ref[...] = x1
        h2_ref[...] = _rms(x1, gf_ref[...]).astype(BF16)

    row = lambda w: pl.BlockSpec((tm, w), lambda i: (i, 0))
    const = lambda a, b: pl.BlockSpec((a, b), lambda i: (0, 0))
    nb = s // tm
    return _host_pcall(body, hosted, lambda: pl.program_id(0) == 0, lambda: pl.program_id(0) == nb - 1,
                       n_in=9, n_out=3, n_scratch=0, name="mix_out", grid=(nb,),
                       in_specs=[row(SSM_W), pl.BlockSpec((HEADS, tm, HEAD_DIM), lambda i: (0, i, 0)), row(D_MODEL),
                                 const(SSM_W, SSM_W), const(1, SSM_W),
                                 const(1, ATTN_W), const(1, SSM_W), const(D_MODEL, D_MODEL), const(1, D_MODEL)],
                       out_specs=[row(D_MODEL)] * 3,
                       out_shape=[_sds((s, D_MODEL)), _sds((s, D_MODEL), BF16), _sds((s, D_MODEL), BF16)],
                       scratch_shapes=[], dims=("parallel",), operands=(y, att, x, w_glu, b_glu, g_att, g_ssm, w_out, g_ffn))


CONV_CHUNK = 64


def _conv_rows(pad_ref, w, b, r0, n):
    y = b + pad_ref[pl.ds(r0 + SUBLANES - 2, n), :] * w[0:1, :]
    y = y + pad_ref[pl.ds(r0 + SUBLANES - 1, n), :] * w[1:2, :]
    return y + pad_ref[pl.ds(r0 + SUBLANES, n), :] * w[2:3, :]


def _fill_front_pad(pad_ref, strip_ref, s):
    pad_ref[0:SUBLANES, :] = jnp.zeros((SUBLANES, STRIP), F32)
    for r0 in range(0, s, CONV_CHUNK):
        pad_ref[pl.ds(SUBLANES + r0, CONV_CHUNK), :] = strip_ref[pl.ds(r0, CONV_CHUNK), :]


def _conv_act(up, conv_w, conv_b):
    s = up.shape[0]

    def body(ug_ref, uv_ref, wg_ref, wv_ref, bg_ref, bv_ref, act_ref, pg_ref, pv_ref):
        _fill_front_pad(pg_ref, ug_ref, s)
        _fill_front_pad(pv_ref, uv_ref, s)
        wg, wv, bg, bv = wg_ref[...], wv_ref[...], bg_ref[...], bv_ref[...]
        for r0 in range(0, s, CONV_CHUNK):
            hg = _conv_rows(pg_ref, wg, bg, r0, CONV_CHUNK)
            hv = _conv_rows(pv_ref, wv, bv, r0, CONV_CHUNK)
            act_ref[pl.ds(r0, CONV_CHUNK), :] = (hg * _sigmoid(hg) * hv).astype(BF16)

    strip = lambda off: pl.BlockSpec((s, STRIP), lambda j: (0, j + off))
    wsp = lambda off: pl.BlockSpec((3, STRIP), lambda j: (0, j + off))
    bsp = lambda off: pl.BlockSpec((1, STRIP), lambda j: (0, j + off))
    return _pcall(body, name="conv_act", grid=(N_STRIPS,),
                  in_specs=[strip(0), strip(N_STRIPS), wsp(0), wsp(N_STRIPS), bsp(0), bsp(N_STRIPS)],
                  out_specs=pl.BlockSpec((s, STRIP), lambda j: (0, j)), out_shape=_sds((s, D_FF), BF16),
                  scratch_shapes=[pltpu.VMEM((s + SUBLANES, STRIP), F32)] * 2,
                  dims=("parallel",))(up, up, conv_w, conv_w, conv_b, conv_b)


def _down_loss(act, w_down, x1, tgt):
    s = x1.shape[0]
    tm = _wide_tile(s)

    def body(a_ref, w_ref, x1_ref, t_ref, dy_ref, dyb_ref, loss_ref):
        i = pl.program_id(0)

        @pl.when(i == 0)
        def _():
            loss_ref[...] = jnp.zeros_like(loss_ref)

        diff = x1_ref[...] + _dot(a_ref[...], w_ref[...]) - t_ref[...]
        dy = diff * (1.0 / D_MODEL)
        dy_ref[...] = dy
        dyb_ref[...] = dy.astype(BF16)
        loss_ref[...] += 0.5 * jnp.sum(diff * dy)

    row = lambda w: pl.BlockSpec((tm, w), lambda i: (i, 0))
    return _pcall(body, name="down_loss", grid=(s // tm,),
                  in_specs=[row(D_FF), pl.BlockSpec((D_FF, D_MODEL), lambda i: (0, 0)), row(D_MODEL), row(D_MODEL)],
                  out_specs=[row(D_MODEL), row(D_MODEL), pl.BlockSpec((SUBLANES, LANES), lambda i: (0, 0))],
                  out_shape=[_sds((s, D_MODEL)), _sds((s, D_MODEL), BF16), _sds((SUBLANES, LANES))],
                  dims=("arbitrary",))(act, w_down, x1, tgt)


def _conv_act_bwd(up, dact, conv_w, conv_b):
    s = up.shape[0]
    ch = CONV_CHUNK

    def body(ug_ref, uv_ref, da_ref, wg_ref, wv_ref, bg_ref, bv_ref, dup_ref, dcw_ref, pg_ref, pv_ref, dg_ref, dv_ref):
        _fill_front_pad(pg_ref, ug_ref, s)
        _fill_front_pad(pv_ref, uv_ref, s)
        zero = jnp.zeros((SUBLANES, STRIP), F32)
        dg_ref[pl.ds(s, SUBLANES), :] = zero
        dv_ref[pl.ds(s, SUBLANES), :] = zero
        wg, wv, bg, bv = wg_ref[...], wv_ref[...], bg_ref[...], bv_ref[...]
        tile_sum = lambda t: jnp.sum(t.reshape(ch // SUBLANES, SUBLANES, STRIP), axis=0)
        accs = [[zero] * 4, [zero] * 4]
        for r0 in range(0, s, ch):
            hg = _conv_rows(pg_ref, wg, bg, r0, ch)
            hv = _conv_rows(pv_ref, wv, bv, r0, ch)
            sg = _sigmoid(hg)
            da = da_ref[pl.ds(r0, ch), :]
            dhs = (da * hv * (sg * (1.0 + hg * (1.0 - sg))), da * (hg * sg))
            for half, (dh, d_ref, p_ref) in enumerate(zip(dhs, (dg_ref, dv_ref), (pg_ref, pv_ref))):
                d_ref[pl.ds(r0, ch), :] = dh
                for k in range(3):
                    accs[half][k] = accs[half][k] + tile_sum(dh * p_ref[pl.ds(r0 + SUBLANES - 2 + k, ch), :])
                accs[half][3] = accs[half][3] + tile_sum(dh)
        for half, (d_ref, w) in enumerate(((dg_ref, wg), (dv_ref, wv))):
            for r0 in range(0, s, ch):
                dup = (d_ref[pl.ds(r0, ch), :] * w[2:3, :] + d_ref[pl.ds(r0 + 1, ch), :] * w[1:2, :]
                       + d_ref[pl.ds(r0 + 2, ch), :] * w[0:1, :])
                dup_ref[half, pl.ds(r0, ch), :] = dup.astype(BF16)
            rid = lax.broadcasted_iota(jnp.int32, (SUBLANES, STRIP), 0)
            out = zero
            for k in range(4):
                out = jnp.where(rid == k, jnp.sum(accs[half][k], axis=0, keepdims=True), out)
            dcw_ref[half] = out

    strip = lambda off: pl.BlockSpec((s, STRIP), lambda j: (0, j + off))
    wsp = lambda off: pl.BlockSpec((3, STRIP), lambda j: (0, j + off))
    bsp = lambda off: pl.BlockSpec((1, STRIP), lambda j: (0, j + off))
    return _pcall(body, name="conv_act_bwd", grid=(N_STRIPS,),
                  in_specs=[strip(0), strip(N_STRIPS), strip(0), wsp(0), wsp(N_STRIPS), bsp(0), bsp(N_STRIPS)],
                  out_specs=[pl.BlockSpec((2, s, STRIP), lambda j: (0, 0, j)), pl.BlockSpec((2, SUBLANES, STRIP), lambda j: (0, 0, j))],
                  out_shape=[_sds((2, s, D_FF), BF16), _sds((2, SUBLANES, D_FF))],
                  scratch_shapes=[pltpu.VMEM((s + SUBLANES, STRIP), F32)] * 4,
                  dims=("parallel",))(up, up, dact, conv_w, conv_w, conv_b, conv_b)


def _mix_bwd(dy, dh2, x1, g_ffn, w_out, y, att, w_glu, b_glu, g_att, g_ssm, hosted=None):
    s = dy.shape[0]
    tm = _wide_tile(s)

    def body(dy_ref, dh2_ref, x1_ref, gf_ref, wo_ref, y_ref, att_ref, wg_ref, bg_ref, ga_ref, gs_ref,
             dx1_ref, dx1b_ref, datt_ref, dys_ref, dwg_ref, dgf_ref, dga_ref, dgs_ref, dbg_ref):
        i = pl.program_id(0)

        @pl.when(i == 0)
        def _():
            for r in (dwg_ref, dgf_ref, dga_ref, dgs_ref, dbg_ref):
                r[...] = jnp.zeros_like(r)

        dxn, dgf = _rms_bwd(x1_ref[...], gf_ref[...], dh2_ref[...])
        dx1 = dy_ref[...] + dxn
        dx1_ref[...] = dx1
        dx1b = dx1.astype(BF16)
        dx1b_ref[...] = dx1b
        dgf_ref[...] += dgf
        dma = _dot_nt(dx1b, wo_ref[0:ATTN_W, :])
        dms = _dot_nt(dx1b, wo_ref[ATTN_W:D_MODEL, :])
        datt, dga = _rms_bwd(_merge_heads(att_ref), ga_ref[...], dma)
        _split_heads(datt_ref, datt)
        dga_ref[...] += dga
        yv = y_ref[...]
        ge, sg = _ssm_glu(yv, wg_ref[...], bg_ref[...])
        dssm, dgs = _rms_bwd(ge * sg, gs_ref[...], dms)
        dgs_ref[...] += dgs
        dgl = dssm * ge * sg * (1.0 - sg)
        dglb = dgl.astype(BF16)
        dge = dssm * sg + _dot_nt(dglb, wg_ref[...])
        dbg_ref[...] += jnp.sum(dgl, axis=0, keepdims=True)
        dwg_ref[...] += _dot_tn(ge.astype(BF16), dglb)
        dys_ref[...] = dge * _gelu_grad(yv)

    row = lambda w: pl.BlockSpec((tm, w), lambda i: (i, 0))
    const = lambda a, b: pl.BlockSpec((a, b), lambda i: (0, 0))
    heads = pl.BlockSpec((HEADS, tm, HEAD_DIM), lambda i: (0, i, 0))
    nb = s // tm
    return _host_pcall(
        body, hosted, lambda: pl.program_id(0) == 0, lambda: pl.program_id(0) == nb - 1, n_in=11, n_out=9, n_scratch=0,
        name="mix_bwd", grid=(nb,),
        in_specs=[row(D_MODEL), row(D_MODEL), row(D_MODEL), const(1, D_MODEL), const(D_MODEL, D_MODEL), row(SSM_W),
                  heads, const(SSM_W, SSM_W), const(1, SSM_W), const(1, ATTN_W), const(1, SSM_W)],
        out_specs=[row(D_MODEL), row(D_MODEL), heads, row(SSM_W), const(SSM_W, SSM_W), const(1, D_MODEL),
                   const(1, ATTN_W), const(1, SSM_W), const(1, SSM_W)],
        out_shape=[_sds((s, D_MODEL)), _sds((s, D_MODEL), BF16), _sds((HEADS, s, HEAD_DIM)), _sds((s, SSM_W)),
                   _sds((SSM_W, SSM_W)), _sds((1, D_MODEL)), _sds((1, ATTN_W)), _sds((1, SSM_W)), _sds((1, SSM_W))],
        scratch_shapes=[], dims=("arbitrary",), operands=(dy, dh2, x1, g_ffn, w_out, y, att, w_glu, b_glu, g_att, g_ssm))


def _ssm_bwd(dys, uf, ub, xr, xi, bbr, bbi, ar, ai, ccr, cci, dsk, hosted=None):
    s = dys.shape[0]
    tm = min(SSM_ROWS, s)
    nb = s // tm
    nt = tm // SUBLANES

    def body(dy_ref, u_ref, ub_ref, xr_ref, xi_ref, xrp_ref, xip_ref, bbr_ref, bbi_ref, ar_ref, ai_ref, ccr_ref,
             cci_ref, dsk_ref, du_ref, dbbr_ref, dbbi_ref, dccr_ref, dcci_ref, dar_ref, dai_ref, dd_ref,
             gr_s, gi_s, cr_s, ci_s, accr_s, acci_s):
        i = pl.program_id(1)
        first_block = i == nb - 1

        @pl.when(i == 0)
        def _():
            for r in (cr_s, ci_s, accr_s, acci_s, dbbr_ref, dbbi_ref, dccr_ref, dcci_ref, dd_ref):
                r[...] = jnp.zeros_like(r)

        dy = dy_ref[...]
        dyb = dy.astype(BF16)
        gr_s[...] = _dot_nt(dyb, ccr_ref[0])
        gi_s[...] = -_dot_nt(dyb, cci_ref[0])
        consts = _scan_consts(ar_ref[0], -ai_ref[0], CHUNK_S, True)
        row = lax.broadcasted_iota(jnp.int32, (SUBLANES, CHUNK_S), 0)

        def tile(kk, carry):
            cr, ci, accr, acci = carry
            k = nt - 1 - kk
            sl = pl.ds(pl.multiple_of(k * SUBLANES, SUBLANES), SUBLANES)
            gr, gi = _scan_tile(gr_s[sl, :], gi_s[sl, :], cr, ci, consts, True)
            gr_s[sl, :] = gr
            gi_s[sl, :] = gi
            slp = pl.ds(pl.multiple_of(jnp.maximum(k - 1, 0) * SUBLANES, SUBLANES), SUBLANES)
            inner = k > 0
            pr_t = jnp.where(inner, xr_ref[slp, :], xrp_ref[...])
            pi_t = jnp.where(inner, xi_ref[slp, :], xip_ref[...])
            live = jnp.logical_or(inner, jnp.logical_not(first_block))
            top_r = jnp.where(live, pltpu.roll(pr_t, 1, 0), 0.0)
            top_i = jnp.where(live, pltpu.roll(pi_t, 1, 0), 0.0)
            xpr = jnp.where(row == 0, top_r, pltpu.roll(xr_ref[sl, :], 1, 0))
            xpi = jnp.where(row == 0, top_i, pltpu.roll(xi_ref[sl, :], 1, 0))
            accr = accr + gr * xpr + gi * xpi
            acci = acci + gi * xpr - gr * xpi
            return gr[0:1, :], gi[0:1, :], accr, acci

        zeros = jnp.zeros((SUBLANES, CHUNK_S), F32)
        cr, ci, accr, acci = lax.fori_loop(0, nt, tile, (cr_s[...], ci_s[...], zeros, zeros))
        cr_s[...] = cr
        ci_s[...] = ci
        accr_s[...] += accr
        acci_s[...] += acci
        grb = gr_s[...].astype(BF16)
        gib = gi_s[...].astype(BF16)
        u_b = ub_ref[...]
        du_ref[...] = _dot_nt(grb, bbr_ref[0]) + _dot_nt(gib, bbi_ref[0]) + dsk_ref[...] * dy
        dbbr_ref[0] += _dot_tn(u_b, grb)
        dbbi_ref[0] += _dot_tn(u_b, gib)
        dccr_ref[0] += _dot_tn(xr_ref[...].astype(BF16), dyb)
        dcci_ref[0] -= _dot_tn(xi_ref[...].astype(BF16), dyb)
        dd_ref[...] += jnp.sum(dy * u_ref[...], axis=0, keepdims=True)

        @pl.when(i == nb - 1)
        def _():
            dar_ref[0] = jnp.sum(accr_s[...], axis=0, keepdims=True)
            dai_ref[0] = jnp.sum(acci_s[...], axis=0, keepdims=True)

    tiles_per_block = tm // SUBLANES
    rb = lambda i: nb - 1 - i
    wspec = lambda a, b: pl.BlockSpec((1, a, b), lambda j, i: (j, 0, 0))
    xblk = pl.BlockSpec((tm, CHUNK_S), lambda j, i: (rb(i), j))
    xprev = pl.BlockSpec((SUBLANES, CHUNK_S), lambda j, i: (jnp.maximum(rb(i) * tiles_per_block - 1, 0), j))
    ublk = pl.BlockSpec((tm, CHUNK_U), lambda j, i: (rb(i), j))
    first = lambda: jnp.logical_and(pl.program_id(0) == 0, pl.program_id(1) == 0)
    last = lambda: jnp.logical_and(pl.program_id(0) == SSM_CHUNKS - 1, pl.program_id(1) == nb - 1)
    return _host_pcall(
        body, hosted, first, last, n_in=14, n_out=8, n_scratch=6, name="ssm_bwd", grid=(SSM_CHUNKS, nb),
        in_specs=[ublk, ublk, ublk, xblk, xblk, xprev, xprev,
                  wspec(CHUNK_U, CHUNK_S), wspec(CHUNK_U, CHUNK_S), wspec(1, CHUNK_S), wspec(1, CHUNK_S),
                  wspec(CHUNK_S, CHUNK_U), wspec(CHUNK_S, CHUNK_U), pl.BlockSpec((1, CHUNK_U), lambda j, i: (0, j))],
        out_specs=[ublk, wspec(CHUNK_U, CHUNK_S), wspec(CHUNK_U, CHUNK_S), wspec(CHUNK_S, CHUNK_U),
                   wspec(CHUNK_S, CHUNK_U), wspec(1, CHUNK_S), wspec(1, CHUNK_S),
                   pl.BlockSpec((1, CHUNK_U), lambda j, i: (0, j))],
        out_shape=[_sds((s, SSM_W)), _sds((SSM_CHUNKS, CHUNK_U, CHUNK_S)), _sds((SSM_CHUNKS, CHUNK_U, CHUNK_S)),
                   _sds((SSM_CHUNKS, CHUNK_S, CHUNK_U)), _sds((SSM_CHUNKS, CHUNK_S, CHUNK_U)),
                   _sds((SSM_CHUNKS, 1, CHUNK_S)), _sds((SSM_CHUNKS, 1, CHUNK_S)), _sds((1, SSM_W))],
        scratch_shapes=[pltpu.VMEM((tm, CHUNK_S), F32)] * 2 + [pltpu.VMEM((1, CHUNK_S), F32)] * 2
                       + [pltpu.VMEM((SUBLANES, CHUNK_S), F32)] * 2,
        dims=("parallel", "arbitrary"), operands=(dys, uf, ub, xr, xi, xr, xi, bbr, bbi, ar, ai, ccr, cci, dsk))


def _attn_probs(q, ks, cs, lse, scale, diagonal):
    p = jnp.exp(_dot_nt(q, ks) * scale - cs - lse)
    if diagonal:
        tq, tk = p.shape
        causal = lax.broadcasted_iota(jnp.int32, (tq, tk), 1) <= lax.broadcasted_iota(jnp.int32, (tq, tk), 0)
        p = jnp.where(causal, p, 0.0)
    return p


def _attn_bwd(qh, kh, vh, crow, lse, doh, hosted=None):
    _, s, _ = qh.shape
    tq = _row_tile(s)
    nq = s // tq
    scale = HEAD_DIM ** -0.5
    hp = HEADS_PER_STEP

    def body(q_ref, k_ref, v_ref, c_ref, lse_ref, do_ref, dq_ref, dk_ref, dv_ref, dc_ref, p_s, dp_s):
        i = pl.program_id(1)

        @pl.when(i == 0)
        def _():
            for r in (dk_ref, dv_ref, dc_ref):
                r[...] = jnp.zeros_like(r)

        dobs = [do_ref[hh].astype(BF16) for hh in range(hp)]

        def first(j, dls, diagonal):
            off = pl.multiple_of(j * tq, tq)
            out = []
            for hh in range(hp):
                p = _attn_probs(q_ref[hh], k_ref[hh, pl.ds(off, tq), :], c_ref[hh, :, pl.ds(off, tq)], lse_ref[hh],
                                scale, diagonal)
                dp = _dot_nt(dobs[hh], v_ref[hh, pl.ds(off, tq), :])
                p_s[hh, j] = p
                dp_s[hh, j] = dp
                out.append(dls[hh] + jnp.sum(p * dp, axis=-1, keepdims=True))
            return tuple(out)

        zero_col = jnp.zeros((tq, 1), F32)
        dls = lax.fori_loop(0, i, lambda j, c: first(j, c, False), (zero_col,) * hp)
        dls = first(i, dls, True)

        def second(j, dqs):
            rows = pl.ds(pl.multiple_of(j * tq, tq), tq)
            out = []
            for hh in range(hp):
                p = p_s[hh, j]
                ds = p * (dp_s[hh, j] - dls[hh])
                dsb = ds.astype(BF16)
                dv_ref[hh, rows, :] += _dot_tn(p.astype(BF16), dobs[hh])
                dk_ref[hh, rows, :] += _dot_tn(dsb, q_ref[hh]) * scale
                dc_ref[hh, :, rows] -= jnp.sum(ds, axis=0, keepdims=True)
                out.append(dqs[hh] + _dot(dsb, k_ref[hh, rows, :]))
            return tuple(out)

        dqs = lax.fori_loop(0, i + 1, second, (jnp.zeros((tq, HEAD_DIM), F32),) * hp)
        for hh in range(hp):
            dq_ref[hh] = dqs[hh] * scale

    blk = pl.BlockSpec((hp, tq, HEAD_DIM), lambda h, i: (h, i, 0))
    full = pl.BlockSpec((hp, s, HEAD_DIM), lambda h, i: (h, 0, 0))
    crow_spec = pl.BlockSpec((hp, 1, s), lambda h, i: (h, 0, 0))
    nh = HEADS // hp
    first = lambda: jnp.logical_and(pl.program_id(0) == 0, pl.program_id(1) == 0)
    last = lambda: jnp.logical_and(pl.program_id(0) == nh - 1, pl.program_id(1) == nq - 1)
    return _host_pcall(body, hosted, first, last, n_in=6, n_out=4, n_scratch=2, name="attn_bwd", grid=(nh, nq),
                       in_specs=[blk, full, full, crow_spec, pl.BlockSpec((hp, tq, 1), lambda h, i: (h, i, 0)), blk],
                       out_specs=[blk, full, full, crow_spec],
                       out_shape=[_sds((HEADS, s, HEAD_DIM))] * 3 + [_sds((HEADS, 1, s))],
                       scratch_shapes=[pltpu.VMEM((hp, nq, tq, tq), F32)] * 2,
                       dims=("parallel", "arbitrary"), operands=(qh, kh, vh, crow, lse, doh))


def _prep_bwd(z, dqn, dkn, dv, du, dc, gq, gk, bf, gg, hosted=None):
    s = z.shape[0]
    tm = _row_tile(s)
    nb = s // tm

    def body(z_ref, dqn_ref, dkn_ref, dv_ref, du_ref, dc_ref, gq_ref, gk_ref, bf_ref, gg_ref,
             dz_ref, dgq_ref, dgk_ref, dbf_ref, carry_ref):
        i = pl.program_id(0)

        @pl.when(i == 0)
        def _():
            for r in (dgq_ref, dgk_ref, dbf_ref, carry_ref):
                r[...] = jnp.zeros_like(r)

        gg_m = gg_ref[...]

        def head_norm_bwd(t, g, dn):
            r = lax.rsqrt(_dot_exact_r(t * t, gg_m) * (1.0 / HEAD_DIM) + EPS)
            w = dn * g
            mean_wt = _dot_exact_r(w * t, gg_m) * (1.0 / HEAD_DIM)
            return r * w - t * (r * r * r) * mean_wt, jnp.sum(dn * t * r, axis=0, keepdims=True)

        dq, dgq = head_norm_bwd(z_ref[:, 0:ATTN_W], gq_ref[...], _merge_heads(dqn_ref))
        dk, dgk = head_norm_bwd(z_ref[:, ATTN_W:2 * ATTN_W], gk_ref[...], _merge_heads(dkn_ref))
        dgq_ref[...] += dgq
        dgk_ref[...] += dgk
        row = lax.broadcasted_iota(jnp.int32, (tm, tm), 0)
        col = lax.broadcasted_iota(jnp.int32, (tm, tm), 1)
        triu = (col >= row).astype(BF16)
        dlf = _dot_exact_l(triu, dc_ref[...]) + carry_ref[...]
        carry_ref[...] = dlf[0:1, :]
        df = dlf * _sigmoid(-_forget_logits(z_ref, bf_ref))
        dbf_ref[...] += jnp.sum(df, axis=0, keepdims=True)
        dz_ref[:, 0:ATTN_W] = dq.astype(BF16)
        dz_ref[:, ATTN_W:2 * ATTN_W] = dk.astype(BF16)
        dz_ref[:, 2 * ATTN_W:3 * ATTN_W] = _merge_heads(dv_ref).astype(BF16)
        tail = jnp.concatenate([df[:, :HEADS], du_ref[...], jnp.zeros((tm, Z_COLS - IN_COLS), F32)], axis=-1)
        dz_ref[:, F_COL0:Z_COLS] = tail.astype(BF16)

    row_spec = lambda w: pl.BlockSpec((tm, w), lambda i: (nb - 1 - i, 0))
    const = lambda shape: pl.BlockSpec(shape, lambda i: (0, 0))
    return _host_pcall(
        body, hosted, lambda: pl.program_id(0) == 0, lambda: pl.program_id(0) == nb - 1, n_in=10, n_out=4, n_scratch=1,
        name="prep_bwd", grid=(nb,),
        in_specs=[row_spec(Z_COLS)] + [pl.BlockSpec((HEADS, tm, HEAD_DIM), lambda i: (0, nb - 1 - i, 0))] * 3
                 + [row_spec(ATTN_W), row_spec(LANES), const((1, ATTN_W)),
                    const((1, ATTN_W)), const((1, LANES)), const((ATTN_W, ATTN_W))],
        out_specs=[row_spec(Z_COLS), const((1, ATTN_W)), const((1, ATTN_W)), const((1, LANES))],
        out_shape=[_sds((s, Z_COLS), BF16), _sds((1, ATTN_W)), _sds((1, ATTN_W)), _sds((1, LANES))],
        scratch_shapes=[pltpu.VMEM((1, LANES), F32)], dims=("arbitrary",),
        operands=(z, dqn, dkn, dv, du, dc, gq, gk, bf, gg))


def _in_norm_bwd(x, g_mix, dh, dx1, hosted=None):
    s = x.shape[0]
    tm = _wide_tile(s)

    def body(x_ref, g_ref, dh_ref, dx1_ref, dx_ref, dg_ref):
        i = pl.program_id(0)

        @pl.when(i == 0)
        def _():
            dg_ref[...] = jnp.zeros_like(dg_ref)

        dxn, dg = _rms_bwd(x_ref[...], g_ref[...], dh_ref[...])
        dx_ref[...] = dx1_ref[...] + dxn
        dg_ref[...] += dg

    row = pl.BlockSpec((tm, D_MODEL), lambda i: (i, 0))
    vec = pl.BlockSpec((1, D_MODEL), lambda i: (0, 0))
    nb = s // tm
    return _host_pcall(body, hosted, lambda: pl.program_id(0) == 0, lambda: pl.program_id(0) == nb - 1,
                       n_in=4, n_out=2, n_scratch=0, name="in_norm_bwd", grid=(nb,), in_specs=[row, vec, row, row],
                       out_specs=[row, vec], out_shape=[_sds((s, D_MODEL)), _sds((1, D_MODEL))], scratch_shapes=[],
                       dims=("arbitrary",), operands=(x, g_mix, dh, dx1))


def _adamw_refs(w_ref, g_ref, m_ref, v_ref, d_ref, mo_ref, vo_ref):
    gv = g_ref[...]
    mn = ADAM_B1 * m_ref[...] + (1.0 - ADAM_B1) * gv
    vn = ADAM_B2 * v_ref[...] + (1.0 - ADAM_B2) * (gv * gv)
    m_hat = mn / (1.0 - ADAM_B1 ** ADAM_STEP)
    v_hat = vn / (1.0 - ADAM_B2 ** ADAM_STEP)
    d_ref[...] = -ADAM_LR * (m_hat / (jnp.sqrt(v_hat) + ADAM_EPS) + ADAM_WD * w_ref[...])
    mo_ref[...] = mn
    vo_ref[...] = vn


def _adamw_small(ws, gs, ms, vs):
    n = len(ws)

    def body(*refs):
        ins, outs = refs[:4 * n], refs[4 * n:]
        for i in range(n):
            _adamw_refs(ins[i], ins[n + i], ins[2 * n + i], ins[3 * n + i], *outs[3 * i:3 * i + 3])

    vm = pl.BlockSpec(memory_space=pltpu.VMEM)
    out_shape = [_sds(w.shape) for w in ws for _ in range(3)]
    return _pallas(body, name="adamw_small", in_specs=[vm] * (4 * n), out_specs=[vm] * (3 * n), out_shape=out_shape,
                   compiler_params=pltpu.CompilerParams(vmem_limit_bytes=VMEM_LIMIT))(*ws, *gs, *ms, *vs)


def _adamw(w, g, m, v, *, name):
    r, c = w.shape
    tr = r
    for cand in (256, 176, 128, 64):
        if r > cand and r % cand == 0:
            tr = cand
            break

    def body(w_ref, g_ref, m_ref, v_ref, d_ref, mo_ref, vo_ref):
        _adamw_refs(w_ref, g_ref, m_ref, v_ref, d_ref, mo_ref, vo_ref)

    spec = pl.BlockSpec((tr, c), lambda i: (i, 0))
    return _pcall(body, name=name, grid=(r // tr,), in_specs=[spec] * 4, out_specs=[spec] * 3,
                  out_shape=[_sds((r, c))] * 3, dims=("parallel",))(w, g, m, v)


def _prefetch_call(body, *, name, grid, in_specs, out_specs, out_shape, operands):
    grid_spec = pltpu.PrefetchScalarGridSpec(num_scalar_prefetch=1, grid=grid, in_specs=in_specs, out_specs=out_specs)
    params = pltpu.CompilerParams(dimension_semantics=("parallel",) * len(grid), vmem_limit_bytes=VMEM_LIMIT)
    return _pallas(body, name=name, grid_spec=grid_spec, out_shape=out_shape, compiler_params=params)(*operands)


def _place_cols(buf, shard, place):
    rows, cols = shard.shape
    tr = 256

    def body(place_ref, s_ref, b_ref, o_ref):
        o_ref[...] = s_ref[...]

    grid_spec = pltpu.PrefetchScalarGridSpec(
        num_scalar_prefetch=1, grid=(rows // tr,),
        in_specs=[pl.BlockSpec((tr, cols), lambda i, p: (i, 0)), pl.BlockSpec(memory_space=pltpu.HBM)],
        out_specs=pl.BlockSpec((tr, cols), lambda i, p: (i, p[0])))
    return _pallas(body, name="place_own_cols", grid_spec=grid_spec, out_shape=_sds(buf.shape, buf.dtype),
                   input_output_aliases={2: 0},
                   compiler_params=pltpu.CompilerParams(dimension_semantics=("parallel",),
                                                        vmem_limit_bytes=VMEM_LIMIT))(place, shard, buf)


def _half_rows_tile(hr):
    return hr if hr <= 256 else 176 if hr % 176 == 0 else 256


def _add_half(g, landed, place, *, name):
    def body(place_ref, g_ref, l_ref, o_ref):
        own = g_ref[0] if len(g_ref.shape) == 4 else g_ref[...]
        o_ref[...] = (own + l_ref[...]).astype(BF16)

    if g.ndim == 4:
        _, _, hr, c = g.shape
        tr = _half_rows_tile(hr)
        blk = (1, tr, c)
        return _prefetch_call(
            body, name=name, grid=(N_CHIPS, hr // tr),
            in_specs=[pl.BlockSpec((1,) + blk, lambda j, i, p: (j, p[1], i, 0)), pl.BlockSpec(blk, lambda j, i, p: (j, i, 0))],
            out_specs=pl.BlockSpec(blk, lambda j, i, p: (j, i, 0)), out_shape=_sds(landed.shape, BF16),
            operands=(place, g, landed))
    hr, c = landed.shape
    tr, tc = 256, _tile(c, 2176)
    nb = hr // tr
    return _prefetch_call(
        body, name=name, grid=(nb, c // tc),
        in_specs=[pl.BlockSpec((tr, tc), lambda i, j, p: (p[1] * nb + i, j)), pl.BlockSpec((tr, tc), lambda i, j, p: (i, j))],
        out_specs=pl.BlockSpec((tr, tc), lambda i, j, p: (i, j)), out_shape=_sds(landed.shape, BF16),
        operands=(place, g, landed))


def _sum_chips(chip_sum, lands, place, *, name, tc, window_stride=0):
    _, hr, c = lands.shape
    tr = _half_rows_tile(hr)
    nb = hr // tr
    ncb = c // tc

    def body(place_ref, own_ref, a_ref, b_ref, c_ref, o_ref):
        own = own_ref[0] if len(own_ref.shape) == 3 else own_ref[...]
        o_ref[...] = ((own.astype(F32) + a_ref[0].astype(F32)) + b_ref[0].astype(F32)) + c_ref[0].astype(F32)

    land = lambda k: pl.BlockSpec((1, tr, tc), lambda i, j, p: ((p[0] + k) % N_CHIPS, i, j))
    if chip_sum.ndim == 3:
        own_spec = land(0)
    else:
        stride = window_stride // tc
        own_spec = pl.BlockSpec((tr, tc), lambda i, j, p: (i, p[0] * stride + j))
    return _prefetch_call(
        body, name=name, grid=(nb, ncb), in_specs=[own_spec, land(1), land(2), land(3)],
        out_specs=pl.BlockSpec((tr, tc), lambda i, j, p: (p[1] * nb + i, j)), out_shape=_sds((2 * hr, c)),
        operands=(place, chip_sum, lands, lands, lands))


_HBM = pl.BlockSpec(memory_space=pltpu.HBM)


def _place():
    x, y, c = lax.axis_index("x"), lax.axis_index("y"), lax.axis_index("c")
    chips = [(1 - x, y), (x, 1 - y), (1 - x, 1 - y)]
    return x, y, c, chips


def _rcopy(src, dst, send_sem, recv_sem, to):
    return pltpu.make_async_remote_copy(src_ref=src, dst_ref=dst, send_sem=send_sem, recv_sem=recv_sem,
                                        device_id=to, device_id_type=MESH)


UP_COLS = 2 * D_FF // N_CHIPS
IN_WINDOW = 640
IN_STRIDE = 512


class _Hosted:
    def __init__(self, operands, out_shapes, n_sems, start, finish, aliases=None, local_sems=0):
        self.operands, self.out_shapes, self.n_sems = list(operands), list(out_shapes), n_sems
        self.start, self.finish, self.aliases, self.local_sems = start, finish, dict(aliases or {}), local_sems

    def scratch(self):
        return ([pltpu.SemaphoreType.DMA((self.n_sems,)), pltpu.SemaphoreType.DMA((self.n_sems,))]
                + [pltpu.SemaphoreType.DMA] * self.local_sems)


def _both(a, b):
    na, nao, nas = len(a.operands), len(a.out_shapes), len(a.scratch())

    def start(ins, outs, sems):
        a.start(ins[:na], outs[:nao], sems[:nas])
        b.start(ins[na:], outs[nao:], sems[nas:])

    def finish(ins, outs, sems):
        a.finish(ins[:na], outs[:nao], sems[:nas])
        b.finish(ins[na:], outs[nao:], sems[nas:])

    both = _Hosted(a.operands + b.operands, a.out_shapes + b.out_shapes, 0, start, finish,
                   aliases={**a.aliases, **{na + i: nao + o for i, o in b.aliases.items()}})
    both.scratch = lambda: a.scratch() + b.scratch()
    return both


def _then(a, b):
    nas = len(a.scratch())

    def finish(ins, outs, sems):
        a.finish(ins, outs, sems[:nas])
        b.start(ins, outs, sems[nas:])
        b.finish(ins, outs, sems[nas:])

    chain = _Hosted(a.operands, a.out_shapes, 0, lambda ins, outs, sems: a.start(ins, outs, sems[:nas]), finish,
                    aliases=a.aliases)
    chain.scratch = lambda: a.scratch() + b.scratch()
    return chain


def _run_hosted(hosted, *, name):
    n_in, n_out = len(hosted.operands), len(hosted.out_shapes)

    def body(*refs):
        parts = (refs[:n_in], refs[n_in:n_in + n_out], refs[n_in + n_out:])
        hosted.start(*parts)
        hosted.finish(*parts)

    return _pallas(body, name=name, in_specs=[_HBM] * n_in, out_specs=[_HBM] * n_out, out_shape=hosted.out_shapes,
                   input_output_aliases=hosted.aliases, scratch_shapes=hosted.scratch())(*hosted.operands)


def _host_pcall(core_body, hosted, first, last, *, n_in, n_out, n_scratch, name, grid, in_specs, out_specs, out_shape,
                scratch_shapes, dims, operands):
    if hosted is None:
        outs = _pcall(core_body, name=name, grid=grid, in_specs=in_specs, out_specs=out_specs, out_shape=out_shape,
                      scratch_shapes=scratch_shapes, dims=dims)(*operands)
        return outs, []
    hi, ho = len(hosted.operands), len(hosted.out_shapes)

    def body(*refs):
        a, b = n_in, n_in + hi
        c, d = b + n_out, b + n_out + ho
        e = d + n_scratch
        parts = (refs[a:b], refs[c:d], refs[e:])

        @pl.when(first())
        def _():
            hosted.start(*parts)

        core_body(*refs[:a], *refs[b:c], *refs[d:e])

        @pl.when(last())
        def _():
            hosted.finish(*parts)

    params = pltpu.CompilerParams(dimension_semantics=("arbitrary",) * len(grid), vmem_limit_bytes=VMEM_LIMIT)
    outs = _pallas(body, name=name, grid=grid, in_specs=list(in_specs) + [_HBM] * hi, out_specs=list(out_specs) + [_HBM] * ho,
                   out_shape=list(out_shape) + hosted.out_shapes, scratch_shapes=list(scratch_shapes) + hosted.scratch(),
                   input_output_aliases={n_in + a: n_out + b for a, b in hosted.aliases.items()},
                   compiler_params=params)(*operands, *hosted.operands)
    return outs[:n_out], outs[n_out:]


WHOLE_HALF = (0, 1, 1)


def _band_rows(src, hc, band):
    first, count, of = band
    hr = src.shape[0] // 2
    return pl.ds(hc * hr + first * (hr // of), count * (hr // of))


def _gather_slot(src, out, chip, hc, band=WHOLE_HALF):
    cols = src.shape[1]
    if len(out.shape) == 2:
        return out.at[_band_rows(src, hc, band), pl.ds(pl.multiple_of(chip * cols, LANES), cols)]
    return out.at[chip, _band_rows(src, hc, band), :]


def _gathered_shape(shard, by_cols):
    if by_cols:
        return _sds((shard.shape[0], N_CHIPS * shard.shape[1]), shard.dtype)
    return _sds((N_CHIPS,) + shard.shape, shard.dtype)


def _plan_gather_ici(shards, by_cols, whole=(), bands=None, into=None):
    n = len(shards)
    bands = bands or [WHOLE_HALF] * n
    into = into or [None] * n
    given = [w for w in range(n) if into[w] is not None]
    n_ops = n + len(whole)

    def copies(ins, outs, sems):
        send_sems, recv_sems = sems[0], sems[1]
        x, y, c, chips = _place()
        me = 2 * x + y
        sends, waits = [], []
        for w in range(n + len(whole)):
            for k, (cx, cy) in enumerate(chips):
                sem = (send_sems.at[3 * w + k], recv_sems.at[3 * w + k])
                if w < n:
                    sends.append(_rcopy(ins[w].at[_band_rows(ins[w], c, bands[w]), :],
                                        _gather_slot(ins[w], outs[w], me, c, bands[w]), *sem, (cx, cy, c)))
                    landed = _gather_slot(ins[w], outs[w], 2 * cx + cy, c, bands[w])
                else:
                    sends.append(_rcopy(ins[w], outs[w].at[me], *sem, (cx, cy, c)))
                    landed = outs[w].at[2 * cx + cy]
                waits.append(_rcopy(landed, landed, *sem, (cx, cy, c)))
        return sends, waits

    def start(ins, outs, sems):
        for cp in copies(ins, outs, sems)[0]:
            cp.start()

    def finish(ins, outs, sems):
        sends, waits = copies(ins, outs, sems)
        for cp in waits:
            cp.wait_recv()
        for cp in sends:
            cp.wait_send()

    out_shapes = [_gathered_shape(s, bc) for s, bc in zip(shards, by_cols)] + [_sds((N_CHIPS,) + a.shape, a.dtype) for a in whole]
    return _Hosted(list(shards) + list(whole) + [into[w] for w in given], out_shapes, 3 * n_ops, start, finish,
                   aliases={n_ops + i: w for i, w in enumerate(given)})


def _plan_gather_d2d(bufs, shard_shapes, bands=None):
    n = len(bufs)
    bands = bands or [WHOLE_HALF] * n

    def copies(ins, outs, sems):
        send_sems, recv_sems = sems
        x, y, c, chips = _place()
        sibling = (x, y, 1 - c)
        sends, waits = [], []
        for w in range(n):
            for k, (cx, cy) in enumerate(chips):
                sem = (send_sems.at[3 * w + k], recv_sems.at[3 * w + k])
                landed = _gather_slot(shard_shapes[w], outs[w], 2 * cx + cy, c, bands[w])
                other = _gather_slot(shard_shapes[w], outs[w], 2 * cx + cy, 1 - c, bands[w])
                sends.append(_rcopy(landed, landed, *sem, sibling))
                waits.append(_rcopy(other, other, *sem, sibling))
        return sends, waits

    def start(ins, outs, sems):
        for cp in copies(ins, outs, sems)[0]:
            cp.start()

    def finish(ins, outs, sems):
        sends, waits = copies(ins, outs, sems)
        for cp in waits:
            cp.wait_recv()
        for cp in sends:
            cp.wait_send()

    return _Hosted(bufs, [_sds(b.shape, b.dtype) for b in bufs], 3 * n, start, finish, aliases={w: w for w in range(n)})


def _plan_allgather_first(block):
    def copies(ins, outs, sems):
        send_sems, recv_sems = sems
        x, y, c, chips = _place()
        me = 4 * x + 2 * y + c
        peers = [(x, y, 1 - c)] + [(cx, cy, c) for cx, cy in chips]
        sends = [_rcopy(ins[0], outs[0].at[me], send_sems.at[k], recv_sems.at[k], p) for k, p in enumerate(peers)]
        waits = [_rcopy(outs[0].at[4 * px + 2 * py + pc], outs[0].at[4 * px + 2 * py + pc], send_sems.at[k],
                        recv_sems.at[k], (px, py, pc)) for k, (px, py, pc) in enumerate(peers)]
        return sends, waits

    def start(ins, outs, sems):
        for cp in copies(ins, outs, sems)[0]:
            cp.start()

    def finish(ins, outs, sems):
        sends, waits = copies(ins, outs, sems)
        for cp in waits:
            cp.wait_recv()
        for cp in sends:
            cp.wait_send()

    return _Hosted([block], [_sds((8,) + block.shape)], 4, start, finish)


def _plan_allgather_second(gathered):
    def copies(ins, outs, sems):
        send_sems, recv_sems = sems
        x, y, c, chips = _place()
        sends, waits = [], []
        for k, (cx, cy) in enumerate(chips):
            landed = outs[0].at[4 * cx + 2 * cy + c]
            other = outs[0].at[4 * cx + 2 * cy + 1 - c]
            sends.append(_rcopy(landed, landed, send_sems.at[k], recv_sems.at[k], (x, y, 1 - c)))
            waits.append(_rcopy(other, other, send_sems.at[k], recv_sems.at[k], (x, y, 1 - c)))
        return sends, waits

    def start(ins, outs, sems):
        for cp in copies(ins, outs, sems)[0]:
            cp.start()

    def finish(ins, outs, sems):
        sends, waits = copies(ins, outs, sems)
        for cp in waits:
            cp.wait_recv()
        for cp in sends:
            cp.wait_send()

    return _Hosted([gathered], [_sds(gathered.shape)], 3, start, finish, aliases={0: 0})


def _place_block(gathered, block, device):
    rows, lanes = block.shape

    def body(dev_ref, b_ref, g_ref, o_ref):
        o_ref[0] = b_ref[...]

    grid_spec = pltpu.PrefetchScalarGridSpec(
        num_scalar_prefetch=1, grid=(1,),
        in_specs=[pl.BlockSpec((rows, lanes), lambda i, d: (0, 0)), pl.BlockSpec(memory_space=pltpu.HBM)],
        out_specs=pl.BlockSpec((1, rows, lanes), lambda i, d: (d[0], 0, 0)))
    return _pallas(body, name="place_own_block", grid_spec=grid_spec, out_shape=_sds(gathered.shape),
                   input_output_aliases={2: 0},
                   compiler_params=pltpu.CompilerParams(dimension_semantics=("arbitrary",),
                                                        vmem_limit_bytes=VMEM_LIMIT))(device, block, gathered)


def _sum_devices(gathered):
    _, rows, lanes = gathered.shape
    tr = rows // 2 if rows % 16 == 0 else rows

    def body(g_ref, o_ref):
        acc = g_ref[0]
        for d in range(1, 8):
            acc = acc + g_ref[d]
        o_ref[...] = acc

    return _pcall(body, name="sum_devices", grid=(rows // tr,), in_specs=[pl.BlockSpec((8, tr, lanes), lambda i: (0, i, 0))],
                  out_specs=pl.BlockSpec((tr, lanes), lambda i: (i, 0)), out_shape=_sds((rows, lanes)), dims=("parallel",))(gathered)


def _plan_swap(grads):
    def copies(ins, outs, sems):
        send_sems, recv_sems = sems
        x, y, c, _ = _place()
        cps = []
        for w, g_ref in enumerate(ins):
            if len(g_ref.shape) == 4:
                theirs = g_ref.at[:, 1 - c]
            else:
                hr = g_ref.shape[0] // 2
                theirs = g_ref.at[pl.ds((1 - c) * hr, hr), :]
            cps.append(_rcopy(theirs, outs[w], send_sems.at[w], recv_sems.at[w], (x, y, 1 - c)))
        return cps

    def start(ins, outs, sems):
        for cp in copies(ins, outs, sems):
            cp.start()

    def finish(ins, outs, sems):
        for cp in copies(ins, outs, sems):
            cp.wait()

    out_shapes = [_sds((g.shape[0], g.shape[2], g.shape[3])) if g.ndim == 4 else _sds((g.shape[0] // 2, g.shape[1]))
                  for g in grads]
    return _Hosted(grads, out_shapes, len(grads), start, finish)


def _plan_scatter(chip_sums, windows):
    def copies(ins, outs, sems):
        send_sems, recv_sems = sems
        x, y, c, chips = _place()
        me = 2 * x + y
        sends, waits = [], []
        for w, s_ref in enumerate(ins):
            for k, (cx, cy) in enumerate(chips):
                tgt = 2 * cx + cy
                if windows[w] is not None:
                    stride, width = windows[w]
                    part = s_ref.at[:, pl.ds(pl.multiple_of(tgt * stride, LANES), width)]
                else:
                    part = s_ref.at[tgt]
                sem = (send_sems.at[3 * w + k], recv_sems.at[3 * w + k])
                sends.append(_rcopy(part, outs[w].at[me], *sem, (cx, cy, c)))
                slot = outs[w].at[tgt]
                waits.append(_rcopy(slot, slot, *sem, (cx, cy, c)))
        return sends, waits

    def start(ins, outs, sems):
        for cp in copies(ins, outs, sems)[0]:
            cp.start()

    def finish(ins, outs, sems):
        sends, waits = copies(ins, outs, sems)
        for cp in waits:
            cp.wait_recv()
        for cp in sends:
            cp.wait_send()

    out_shapes = [_sds((N_CHIPS, s.shape[0], win[1]), BF16) if win is not None else _sds(s.shape, BF16)
                  for s, win in zip(chip_sums, windows)]
    return _Hosted(chip_sums, out_shapes, 3 * len(chip_sums), start, finish)


def _plan_join(reds):
    def copies(ins, outs, sems):
        send_sems, recv_sems = sems
        x, y, c, _ = _place()
        sends, waits = [], []
        for w, out in enumerate(outs):
            hr = out.shape[0] // 2
            mine = out.at[pl.ds(c * hr, hr), :]
            theirs = out.at[pl.ds((1 - c) * hr, hr), :]
            sends.append(_rcopy(mine, mine, send_sems.at[w], recv_sems.at[w], (x, y, 1 - c)))
            waits.append(_rcopy(theirs, theirs, send_sems.at[w], recv_sems.at[w], (x, y, 1 - c)))
        return sends, waits

    def start(ins, outs, sems):
        for cp in copies(ins, outs, sems)[0]:
            cp.start()

    def finish(ins, outs, sems):
        sends, waits = copies(ins, outs, sems)
        for cp in waits:
            cp.wait_recv()
        for cp in sends:
            cp.wait_send()

    return _Hosted(reds, [_sds(r.shape) for r in reds], len(reds), start, finish, aliases={w: w for w in range(len(reds))})


def _allreduce_small(v):
    m_per = v.shape[0]

    def body(v_ref, out_ref, all_ref, send_sems, recv_sems, local_sem):
        x, y, c, chips = _place()
        me, sibling = (x, y, c), (x, y, 1 - c)

        def rows(px, py, pc):
            return all_ref.at[pl.ds((4 * px + 2 * py + pc) * m_per, m_per), :]

        def copy(k, block, to, src=None):
            return _rcopy(rows(*block) if src is None else src, rows(*block), send_sems.at[k], recv_sems.at[k], to)

        mine = pltpu.make_async_copy(v_ref, rows(*me), local_sem)
        mine.start()
        first = [copy(0, me, sibling, src=v_ref)]
        first += [copy(1 + k, me, (*chip, c), src=v_ref) for k, chip in enumerate(chips)]
        for cp in first:
            cp.start()
        passed = [copy(4 + k, (*chip, c), sibling) for k, chip in enumerate(chips)]
        for k, chip in enumerate(chips):
            copy(1 + k, (*chip, c), me).wait_recv()
            passed[k].start()
        copy(0, sibling, me).wait_recv()
        for k, chip in enumerate(chips):
            copy(4 + k, (*chip, 1 - c), me).wait_recv()
        for cp in first + passed:
            cp.wait_send()
        mine.wait()
        acc = all_ref[pl.ds(0, m_per), :]
        for d in range(1, 8):
            acc = acc + all_ref[pl.ds(d * m_per, m_per), :]
        out_ref[...] = acc

    vm = pl.BlockSpec(memory_space=pltpu.VMEM)
    return _pallas(body, name="allreduce_small", in_specs=[vm], out_specs=vm, out_shape=_sds((m_per, LANES)),
                          scratch_shapes=[pltpu.VMEM((8 * m_per, LANES), F32), pltpu.SemaphoreType.DMA((7,)),
                                          pltpu.SemaphoreType.DMA((7,)), pltpu.SemaphoreType.DMA],
                          compiler_params=pltpu.CompilerParams(vmem_limit_bytes=VMEM_LIMIT))(v)


def _block_diag(blocks):
    j, g, a, b = blocks.shape
    eye = jnp.eye(g, dtype=bool)[None, :, None, :, None]
    return jnp.where(eye, blocks[:, :, :, None, :], jnp.zeros((), blocks.dtype)).reshape(j, g * a, g * b)


def _diag_blocks(m, a, b):
    j = m.shape[0]
    g = m.shape[1] // a
    t = m.reshape(j, g, a, g, b)
    eye = jnp.eye(g, dtype=bool)[None, :, None, :, None]
    return jnp.sum(jnp.where(eye, t, 0.0), axis=3)


_SMALL = (("g_mix", (1024,)), ("b_f", (8,)), ("g_q", (64,)), ("g_k", (64,)), ("lambda_re", (32, 64)),
          ("lambda_im", (32, 64)), ("log_step", (32,)), ("b_re", (32, 64, 16)), ("b_im", (32, 64, 16)),
          ("c_re", (32, 16, 64)), ("c_im", (32, 16, 64)), ("d_skip", (32, 16)), ("b_glu", (512,)),
          ("g_attn_out", (512,)), ("g_ssm_out", (512,)), ("g_ffn", (1024,)), ("conv_b", (5632,)))


_LATE_SMALL = ("g_mix", "b_f", "g_q", "g_k")
_EARLY_SMALL = tuple(n for n, _ in _SMALL if n not in _LATE_SMALL)


def _packed_rows(n):
    tile = SUBLANES * LANES
    return -(-n // tile) * SUBLANES


def _pack_small(arrs):
    parts = []
    for a in arrs:
        flat = a.reshape(-1)
        rows = _packed_rows(flat.shape[0])
        parts.append(jnp.pad(flat, (0, rows * LANES - flat.shape[0])).reshape(rows, LANES))
    return jnp.concatenate(parts, axis=0)


def _unpack_small(buf, shapes):
    out, r = [], 0
    for shape in shapes:
        n = math.prod(shape)
        out.append(buf[r:r + _packed_rows(n)].reshape(-1)[:n].reshape(shape))
        r += _packed_rows(n)
    return out


def _halves(t):
    return t.reshape(N_CHIPS, 2, t.shape[0] // (2 * N_CHIPS), t.shape[1])


class _MeshComm:
    def __init__(self, args):
        x, y, self.core = lax.axis_index("x"), lax.axis_index("y"), lax.axis_index("c")
        self.chip = 2 * x + y
        self.place = jnp.stack([self.chip, self.core]).astype(jnp.int32)
        self.shards = {n: args[n].astype(BF16) for n in ("w_in", "w_glu", "w_out", "w_up", "w_down")}
        self.conv_w = args["conv_w"]

    def _own(self, stacked, mine):
        return lax.dynamic_update_slice(stacked, mine[None], (self.chip,) + (0,) * mine.ndim)

    def w_in(self):
        sh = self.shards["w_in"]
        (buf,) = _run_hosted(_then(_plan_gather_ici([sh], [False]), _plan_gather_d2d([sh], [sh])), name="gather_w_in")
        whole = self._own(buf, sh).transpose(1, 0, 2).reshape(D_MODEL, IN_COLS)
        return jnp.pad(whole, ((0, 0), (0, Z_COLS - IN_COLS)))

    def gather_first(self):
        self.mid = [self.shards[n] for n in ("w_glu", "w_out", "w_down")]
        return _plan_gather_ici(self.mid + [self.shards["w_up"]], [False, False, False, True], whole=[self.conv_w],
                                bands=[WHOLE_HALF] * 3 + [(0, 1, 4)])

    def gather_second(self, landed):
        self.g_cw = landed[4]
        return _both(_plan_gather_d2d(list(landed[:3]), self.mid),
                     _plan_gather_ici([self.shards["w_up"]], [True], bands=[(1, 3, 4)], into=[landed[3]]))

    def weights(self, gathered):
        g_glu, g_out, g_down = gathered[:3]
        own = self._own
        return (own(g_glu, self.mid[0]).reshape(SSM_W, SSM_W), own(g_out, self.mid[1]).reshape(D_MODEL, D_MODEL),
                own(g_down, self.mid[2]).reshape(D_FF, D_MODEL),
                own(self.g_cw, self.conv_w).transpose(1, 0, 2).reshape(3, 2 * D_FF))

    def gather_third(self, gathered):
        return _plan_gather_d2d([gathered[3]], [self.shards["w_up"]])

    def w_up(self, passed):
        return _place_cols(passed[0], self.shards["w_up"], self.place)

    def swap_down(self, d_w_down):
        self.d_down = _halves(d_w_down)
        return _plan_swap([self.d_down])

    def swap(self, landed_down, d_w_up, d_w_glu, d_w_out):
        self.sum_down = _add_half(self.d_down, landed_down[0], self.place, name="add_w_down")
        self.early = [d_w_up, _halves(d_w_glu), _halves(d_w_out)]
        return _both(_plan_scatter([self.sum_down], [None]), _plan_swap(self.early))

    def scatter(self, landed, small_block):
        self.land_down = landed[0]
        self.early_sums = [_add_half(g, l, self.place, name="add_" + n)
                           for g, l, n in zip(self.early, landed[1:], ("w_up", "w_glu", "w_out"))]
        return _both(_plan_scatter(self.early_sums, [(UP_COLS, UP_COLS), None, None]), _plan_allgather_first(small_block))

    def swap_in(self, d_w_in):
        self.d_in = d_w_in
        return _plan_swap([d_w_in])

    def scatter_in(self, landed):
        self.sum_in = _add_half(self.d_in, landed[0], self.place, name="add_w_in")
        return _plan_scatter([self.sum_in], [(IN_STRIDE, IN_WINDOW)])

    def small_second(self, landed_small):
        return _plan_allgather_second(landed_small[0])

    def reduce(self, lands):
        early_lands, (land_in,) = lands
        sum_in = self.sum_in
        es, el = self.early_sums, early_lands
        todo = [(sum_in, land_in, "w_in", LANES, IN_STRIDE), (es[1], el[1], "w_glu", SSM_W, 0),
                (es[2], el[2], "w_out", D_MODEL, 0), (es[0], el[0], "w_up", UP_COLS, UP_COLS),
                (self.sum_down, self.land_down, "w_down", D_MODEL, 0)]
        reds = _run_hosted(_plan_join([_sum_chips(s, l, self.place, name="sum_" + n, tc=tc, window_stride=st)
                                       for s, l, n, tc, st in todo]), name="join_halves")
        g_big = dict(zip(("w_in", "w_glu", "w_out", "w_up", "w_down"), reds))
        g_big["w_in"] = lax.dynamic_slice_in_dim(reds[0], 2 * self.chip, IN_COLS // N_CHIPS, axis=1)
        return g_big


def _local_step(x, tgt, p, comm):
    s = x.shape[0]
    row = lambda v: v.reshape(1, -1)
    g_mix, g_ffn = row(p["g_mix"]), row(p["g_ffn"])
    g_att, g_ssm, b_glu, conv_b = row(p["g_attn_out"]), row(p["g_ssm_out"]), row(p["b_glu"]), row(p["conv_b"])
    gq = row(jnp.tile(p["g_q"], HEADS))
    gk = row(jnp.tile(p["g_k"], HEADS))
    bf = row(jnp.pad(p["b_f"], (0, LANES - HEADS)))
    gg = jnp.kron(jnp.eye(HEADS, dtype=F32), jnp.ones((HEAD_DIM, HEAD_DIM), F32)).astype(BF16)
    dsk = row(p["d_skip"])

    rep = lambda a: jnp.repeat(a, SSM_GROUP, axis=0)
    lr, li = rep(p["lambda_re"]), rep(p["lambda_im"])
    ls = rep(jnp.broadcast_to(p["log_step"][:, None], (SSM_GROUPS, SSM_STATE)))
    bt_re = p["b_re"].transpose(0, 2, 1).reshape(_PARAM_SHAPE)
    bt_im = p["b_im"].transpose(0, 2, 1).reshape(_PARAM_SHAPE)
    a_re_rep, a_im_rep, bb_re, bb_im = _ssm_params(lr, li, ls, bt_re, bt_im)
    ar = a_re_rep[::SSM_GROUP].reshape(SSM_CHUNKS, 1, CHUNK_S)
    ai = a_im_rep[::SSM_GROUP].reshape(SSM_CHUNKS, 1, CHUNK_S)
    chunked = lambda t: t.reshape(SSM_CHUNKS, SSM_GROUPS // SSM_CHUNKS, SSM_GROUP, SSM_STATE)
    bbr = _block_diag(chunked(bb_re)).astype(BF16)
    bbi = _block_diag(chunked(bb_im)).astype(BF16)
    to_cc = lambda c: _block_diag(chunked(c).transpose(0, 1, 3, 2)).astype(BF16)
    ccr, cci = to_cc(p["c_re"]), to_cc(p["c_im"])

    w_in_r = comm.w_in()
    hb, z = _in_proj(x, g_mix, w_in_r)
    qh, kh, vh, ub, uf, c128 = _attn_prep(z, gq, gk, bf, gg)
    crow = c128[:, :HEADS].T.reshape(HEADS, 1, s)
    (oh, lse), landed = _attn_fwd(qh, kh, vh, crow, comm.gather_first())
    (xr, xi, y), gathered = _ssm_fwd(ub, uf, bbr, bbi, ar, ai, ccr, cci, dsk, comm.gather_second(landed))
    w_glu_b, w_out_b, w_down_b, conv_w_full = comm.weights(gathered)
    (x1, mixb, h2b), passed = _mix_out(y, oh, x, w_glu_b, b_glu, g_att, g_ssm, w_out_b, g_ffn, comm.gather_third(gathered))
    w_up_b = comm.w_up(passed)
    up = _mm(h2b, w_up_b, name="ffn_up", tm=1024, tn=1408, tk=1024)
    act = _conv_act(up, conv_w_full, conv_b)
    dy, dyb, loss_blk = _down_loss(act, w_down_b, x1, tgt)

    d_w_down = _mm(act, dyb, ta=True, name="d_w_down", tm=1408, tn=1024, tk=2048)
    dact = _mm(dyb, w_down_b, tb=True, name="d_act", tm=1024, tn=1408, tk=1024)
    dupb, dcw = _conv_act_bwd(up, dact, conv_w_full, conv_b)
    d_w_up = _mm(h2b, dupb, ta=True, b_parts=2, name="d_w_up", tm=1024, tn=1408, tk=2048)
    dh2 = _mm(dupb, w_up_b, tb=True, a_parts=2, name="d_h2", tm=1024, tn=1024, tk=1408)
    (dx1, dx1b, doh, dys, d_w_glu, d_g_ffn, d_g_att, d_g_ssm, d_b_glu), landed_down = _mix_bwd(
        dy, dh2, x1, g_ffn, w_out_b, y, oh, w_glu_b, b_glu, g_att, g_ssm, comm.swap_down(d_w_down))
    d_w_out = _mm(mixb, dx1b, ta=True, name="d_w_out", tm=1024, tn=1024, tk=2048)
    (du, dbbr, dbbi, dccr, dcci, dar, dai, dd), swapped = _ssm_bwd(dys, uf, ub, xr, xi, bbr, bbi, ar, ai, ccr, cci, dsk,
                                                                comm.swap(landed_down, d_w_up, d_w_glu, d_w_out))
    unchunk = lambda t: t.reshape(_PARAM_SHAPE)
    dbb_re = unchunk(_diag_blocks(dbbr, SSM_GROUP, SSM_STATE))
    dbb_im = unchunk(_diag_blocks(dbbi, SSM_GROUP, SSM_STATE))
    first_row = (jnp.arange(_PARAM_SHAPE[0]) % SSM_GROUP == 0)[:, None]
    da_re = jnp.where(first_row, rep(dar.reshape(SSM_GROUPS, SSM_STATE)), 0.0)
    da_im = jnp.where(first_row, rep(dai.reshape(SSM_GROUPS, SSM_STATE)), 0.0)
    expand_t = (jnp.arange(SSM_GROUPS)[:, None] == (jnp.arange(_PARAM_SHAPE[0]) // SSM_GROUP)[None, :]).astype(BF16)
    d_lr, d_li, d_ls, d_bt_re, d_bt_im = _ssm_params_bwd(lr, li, ls, bt_re, bt_im, da_re, da_im, dbb_re, dbb_im, expand_t)
    from_bt = lambda t: t.reshape(SSM_GROUPS, SSM_GROUP, SSM_STATE).transpose(0, 2, 1)
    from_cc = lambda t: _diag_blocks(t, SSM_STATE, SSM_GROUP).transpose(0, 1, 3, 2).reshape(SSM_GROUPS, SSM_GROUP, SSM_STATE)

    small = {
        "lambda_re": d_lr, "lambda_im": d_li, "log_step": d_ls,
        "b_re": from_bt(d_bt_re), "b_im": from_bt(d_bt_im), "c_re": from_cc(dccr), "c_im": from_cc(dcci),
        "d_skip": dd, "b_glu": d_b_glu, "g_attn_out": d_g_att, "g_ssm_out": d_g_ssm, "g_ffn": d_g_ffn,
        "conv_b": dcw[:, 3],
    }
    d_conv_w = dcw[:, 0:3].transpose(1, 0, 2).reshape(3, 2 * D_FF)
    early_small = _pack_small([small[n] for n in _EARLY_SMALL] + [d_conv_w])

    (dqh, dkh, dvh, dcrow), landed = _attn_bwd(qh, kh, vh, crow, lse, doh, comm.scatter(swapped, early_small))
    early_lands, small_landed = landed[:3], landed[3:]
    dc128 = jnp.pad(dcrow.reshape(HEADS, s).T, ((0, 0), (0, LANES - HEADS)))
    (dzb, d_gq, d_gk, d_bf), small_gathered = _prep_bwd(z, dqh, dkh, dvh, du, dc128, gq, gk, bf, gg,
                                                        comm.small_second(small_landed))
    d_w_in_r = _mm(hb, dzb, ta=True, name="d_w_in", tm=512, tn=Z_COLS, tk=2048)
    dh, swapped_in = _mm(dzb, w_in_r, tb=True, name="d_h", tm=1024, tn=1024, tk=Z_COLS, carry=True,
                         hosted=comm.swap_in(d_w_in_r))
    (dx, d_g_mix), land_in = _in_norm_bwd(x, g_mix, dh, dx1, comm.scatter_in(swapped_in))
    small.update({"g_mix": d_g_mix, "b_f": d_bf[0, :HEADS], "g_q": d_gq.reshape(HEADS, HEAD_DIM).sum(0),
                  "g_k": d_gk.reshape(HEADS, HEAD_DIM).sum(0)})
    big = {"w_in": d_w_in_r, "w_glu": d_w_glu, "w_out": d_w_out, "w_up": d_w_up, "w_down": d_w_down}
    return loss_blk[0, 0], dx, big, small, d_conv_w, (early_lands, land_in, small_gathered, early_small)


def kernel(x, g_mix, w_in, b_f, g_q, g_k, lambda_re, lambda_im, log_step, b_re, b_im, c_re, c_im, d_skip, w_glu, b_glu, g_attn_out, g_ssm_out, w_out, g_ffn, w_up, conv_w, conv_b, w_down, loss_target, m_g_mix, m_w_in, m_b_f, m_g_q, m_g_k, m_lambda_re, m_lambda_im, m_log_step, m_b_re, m_b_im, m_c_re, m_c_im, m_d_skip, m_w_glu, m_b_glu, m_g_attn_out, m_g_ssm_out, m_w_out, m_g_ffn, m_w_up, m_conv_w, m_conv_b, m_w_down, v_g_mix, v_w_in, v_b_f, v_g_q, v_g_k, v_lambda_re, v_lambda_im, v_log_step, v_b_re, v_b_im, v_c_re, v_c_im, v_d_skip, v_w_glu, v_b_glu, v_g_attn_out, v_g_ssm_out, v_w_out, v_g_ffn, v_w_up, v_conv_w, v_conv_b, v_w_down):
    args = dict(locals())
    order = ["g_mix", "w_in", "b_f", "g_q", "g_k", "lambda_re", "lambda_im", "log_step", "b_re", "b_im", "c_re", "c_im",
             "d_skip", "w_glu", "b_glu", "g_attn_out", "g_ssm_out", "w_out", "g_ffn", "w_up", "conv_w", "conv_b", "w_down"]
    comm = _MeshComm(args)
    chip = comm.chip
    loss_part, dx, big, small, d_conv_w, lands = _local_step(x[0], loss_target[0], args, comm)

    g_big = comm.reduce(lands[:2])

    shapes = dict(_SMALL)
    small_names = [n for n, _ in _SMALL]
    device = (2 * chip + comm.core).reshape(1).astype(jnp.int32)
    early = _unpack_small(_sum_devices(_place_block(lands[2][0], lands[3], device)),
                          [shapes[n] for n in _EARLY_SMALL] + [(3, 2 * D_FF)])
    late = _unpack_small(_allreduce_small(_pack_small([small[n] for n in _LATE_SMALL] + [loss_part])),
                         [shapes[n] for n in _LATE_SMALL] + [()])
    loss = late[-1]
    g_conv_w = lax.dynamic_slice_in_dim(early[-1], chip * (2 * D_FF // N_CHIPS), 2 * D_FF // N_CHIPS, axis=1)
    g_small = {**dict(zip(_EARLY_SMALL, early[:-1])), **dict(zip(_LATE_SMALL, late[:-1]))}

    grad, delta, new_m, new_v = {}, {}, {}, {}
    for n in ("w_in", "w_glu", "w_out", "w_up", "w_down"):
        grad[n] = g_big[n]
        delta[n], new_m[n], new_v[n] = _adamw(args[n], g_big[n], args["m_" + n], args["v_" + n], name="adamw_" + n)
    grad["conv_w"] = g_conv_w
    delta["conv_w"], new_m["conv_w"], new_v["conv_w"] = _adamw(conv_w, g_conv_w, m_conv_w, v_conv_w, name="adamw_conv_w")
    stepped = _adamw_small([args[n] for n in small_names], [g_small[n] for n in small_names],
                           [args["m_" + n] for n in small_names], [args["v_" + n] for n in small_names])
    for i, n in enumerate(small_names):
        grad[n] = g_small[n]
        delta[n], new_m[n], new_v[n] = stepped[3 * i:3 * i + 3]

    return (loss, dx[None], *[grad[n] for n in order], *[delta[n] for n in order], *[new_m[n] for n in order],
            *[new_v[n] for n in order])
```

```python
import math

import jax
import jax.numpy as jnp
from jax import lax
from jax.experimental import pallas as pl
from jax.experimental.pallas import tpu as pltpu

F32 = jnp.float32
BF16 = jnp.bfloat16

D_MODEL = 1024
HEADS = 8
HEAD_DIM = 64
ATTN_W = 512
SSM_W = 512
SSM_GROUPS = 32
SSM_GROUP = 16
SSM_STATE = 64
N_STATE = SSM_GROUPS * SSM_STATE
D_FF = 2816
IN_COLS = 2056
Z_COLS = 2176
F_COL0 = 1536
U_COL0 = 1544
EPS = 1e-6
NEG_INF = -1e30
N_CHIPS = 4
LANES = 128
SUBLANES = 8
SSM_CHUNKS = 2
SSM_ROWS = 512
CHUNK_U = SSM_W // SSM_CHUNKS
CHUNK_S = N_STATE // SSM_CHUNKS
HEADS_PER_STEP = 4
STRIP = 128
N_STRIPS = D_FF // STRIP

ADAM_LR = 0.001
ADAM_B1 = 0.9
ADAM_B2 = 0.999
ADAM_EPS = 1e-08
ADAM_WD = 0.01
ADAM_STEP = 10

VMEM_LIMIT = 56 * 1024 * 1024
MESH = pl.DeviceIdType.MESH


def _pallas(body, **kw):
    return pl.pallas_call(body, **kw)


def _pcall(body, *, name, out_shape, in_specs, out_specs, grid=(), scratch_shapes=(), dims=None):
    params = pltpu.CompilerParams(dimension_semantics=dims, vmem_limit_bytes=VMEM_LIMIT)
    return _pallas(body, name=name, grid=grid, in_specs=in_specs, out_specs=out_specs,
                   out_shape=out_shape, scratch_shapes=scratch_shapes, compiler_params=params)


def _sds(shape, dtype=F32):
    return jax.ShapeDtypeStruct(shape, dtype)


def _dot(a, b):
    return jnp.dot(a, b, preferred_element_type=F32)


def _dot_nt(a, b):
    return lax.dot_general(a, b, (((1,), (1,)), ((), ())), preferred_element_type=F32)


def _dot_tn(a, b):
    return lax.dot_general(a, b, (((0,), (0,)), ((), ())), preferred_element_type=F32)


def _split3(x):
    hi = x.astype(BF16)
    r = x - hi.astype(F32)
    mid = r.astype(BF16)
    lo = (r - mid.astype(F32)).astype(BF16)
    return hi, mid, lo


def _dot_hi_lo(x, m01):
    hi = x.astype(BF16)
    lo = (x - hi.astype(F32)).astype(BF16)
    return _dot(hi, m01) + _dot(lo, m01)


def _dot_exact_l(m01, x):
    hi, mid, lo = _split3(x)
    return _dot(m01, hi) + _dot(m01, mid) + _dot(m01, lo)


def _sigmoid(x):
    return 1.0 / (1.0 + jnp.exp(-x))


def _rms(x, g):
    r = lax.rsqrt(jnp.mean(x * x, axis=-1, keepdims=True) + EPS)
    return x * r * g


def _rms_bwd(x, g, dy):
    r = lax.rsqrt(jnp.mean(x * x, axis=-1, keepdims=True) + EPS)
    w = dy * g
    dx = r * w - x * (r * r * r) * jnp.mean(w * x, axis=-1, keepdims=True)
    dg = jnp.sum(dy * x * r, axis=0, keepdims=True)
    return dx, dg


_GELU_K = math.sqrt(2.0 / math.pi)
_GELU_C = 0.044715


def _gelu(y):
    return y * (0.5 * (1.0 + jnp.tanh(_GELU_K * (y + _GELU_C * (y * y * y)))))


def _gelu_grad(y):
    t = jnp.tanh(_GELU_K * (y + _GELU_C * (y * y * y)))
    return 0.5 * (1.0 + t) + 0.5 * y * (1.0 - t * t) * (_GELU_K * (1.0 + 3.0 * _GELU_C * y * y))


def _tile(n, pref):
    if n <= pref:
        return n
    divs = [t for t in range(LANES, n + 1, LANES) if n % t == 0]
    below = [t for t in divs if t <= pref]
    if below and 2 * below[-1] >= pref:
        return below[-1]
    above = [t for t in divs if t > pref]
    return above[0] if above else n


def _row_tile(s):
    return min(256, s)


def _wide_tile(s):
    return min(512, s)


def _mm(a, b, *, name, tm, tn, tk, ta=False, tb=False, a_parts=1, b_parts=1, carry=False, hosted=None):
    if a_parts > 1:
        m, kk = a.shape[1], a.shape[2] * a_parts
    elif ta:
        kk, m = a.shape
    else:
        m, kk = a.shape
    if b_parts > 1:
        n = b.shape[2] * b_parts
    else:
        n = b.shape[0] if tb else b.shape[1]
    tm, tn, tk = _tile(m, tm), _tile(n // b_parts, tn), _tile(kk // a_parts, tk)
    k_per, n_per = kk // a_parts // tk, n // b_parts // tn

    def body(a_ref, b_ref, o_ref):
        k = pl.program_id(2)
        if ta:
            part = _dot_tn(a_ref[...], b_ref[...])
        elif tb:
            part = _dot_nt(a_ref[...], b_ref[...])
        else:
            part = _dot(a_ref[...], b_ref[...])

        @pl.when(k == 0)
        def _():
            o_ref[...] = part

        @pl.when(k > 0)
        def _():
            o_ref[...] += part

    if a_parts > 1:
        a_spec = pl.BlockSpec((None, tm, tk), lambda i, j, k: (k // k_per, i, k % k_per))
    else:
        a_spec = pl.BlockSpec((tk, tm), lambda i, j, k: (k, i)) if ta else pl.BlockSpec((tm, tk), lambda i, j, k: (i, k))
    if b_parts > 1:
        b_spec = pl.BlockSpec((None, tk, tn), lambda i, j, k: (j // n_per, k, j % n_per))
    else:
        b_spec = pl.BlockSpec((tn, tk), lambda i, j, k: (j, k)) if tb else pl.BlockSpec((tk, tn), lambda i, j, k: (k, j))
    grid = (m // tm, n // tn, kk // tk)
    at = lambda step: (lambda: jnp.logical_and(jnp.logical_and(pl.program_id(0) == step[0], pl.program_id(1) == step[1]),
                                               pl.program_id(2) == step[2]))
    (out,), carried = _host_pcall(body, hosted, at((0, 0, 0)), at(tuple(g - 1 for g in grid)), n_in=2, n_out=1, n_scratch=0,
                                  name=name, grid=grid, in_specs=[a_spec, b_spec],
                                  out_specs=[pl.BlockSpec((tm, tn), lambda i, j, k: (i, j))], out_shape=[_sds((m, n))],
                                  scratch_shapes=[], dims=("parallel", "parallel", "arbitrary"), operands=(a, b))
    return (out, carried) if carry else out


def _in_proj(x, g_mix, w_in_r):
    s = x.shape[0]
    tm = _wide_tile(s)

    def body(x_ref, g_ref, w_ref, h_ref, z_ref):
        h = _rms(x_ref[...], g_ref[...]).astype(BF16)
        h_ref[...] = h
        z_ref[...] = _dot(h, w_ref[...])

    return _pcall(body, name="in_proj", grid=(s // tm,),
                  in_specs=[pl.BlockSpec((tm, D_MODEL), lambda i: (i, 0)), pl.BlockSpec((1, D_MODEL), lambda i: (0, 0)),
                            pl.BlockSpec((D_MODEL, Z_COLS), lambda i: (0, 0))],
                  out_specs=[pl.BlockSpec((tm, D_MODEL), lambda i: (i, 0)), pl.BlockSpec((tm, Z_COLS), lambda i: (i, 0))],
                  out_shape=[_sds((s, D_MODEL), BF16), _sds((s, Z_COLS))], dims=("parallel",))(x, g_mix, w_in_r)


def _split_heads(ref, val):
    for h in range(HEADS):
        ref[h] = val[:, h * HEAD_DIM:(h + 1) * HEAD_DIM].astype(ref.dtype)


def _merge_heads(ref):
    return jnp.concatenate([ref[h].astype(F32) for h in range(HEADS)], axis=-1)


def _forget_logits(z_ref, bf_ref):
    fl = z_ref[:, F_COL0:F_COL0 + LANES] + bf_ref[...]
    return jnp.where(lax.broadcasted_iota(jnp.int32, fl.shape, 1) < HEADS, fl, 0.0)


def _attn_prep(z, gq, gk, bf, gg):
    s = z.shape[0]
    tm = _row_tile(s)

    def body(z_ref, gq_ref, gk_ref, bf_ref, gg_ref, qn_ref, kn_ref, vb_ref, ub_ref, uf_ref, c_ref, carry_ref):
        i = pl.program_id(0)

        @pl.when(i == 0)
        def _():
            carry_ref[...] = jnp.zeros_like(carry_ref)

        gg_m = gg_ref[...]

        def head_norm(t, g):
            ssq = _dot_hi_lo(t * t, gg_m)
            return t * lax.rsqrt(ssq * (1.0 / HEAD_DIM) + EPS) * g

        _split_heads(qn_ref, head_norm(z_ref[:, 0:ATTN_W], gq_ref[...]))
        _split_heads(kn_ref, head_norm(z_ref[:, ATTN_W:2 * ATTN_W], gk_ref[...]))
        _split_heads(vb_ref, z_ref[:, 2 * ATTN_W:3 * ATTN_W])
        u = z_ref[:, U_COL0:U_COL0 + SSM_W]
        uf_ref[...] = u
        ub_ref[...] = u.astype(BF16)
        fl = _forget_logits(z_ref, bf_ref)
        lf = jnp.minimum(fl, 0.0) - jnp.log1p(jnp.exp(-jnp.abs(fl)))
        row = lax.broadcasted_iota(jnp.int32, (tm, tm), 0)
        col = lax.broadcasted_iota(jnp.int32, (tm, tm), 1)
        tri = (row >= col).astype(BF16)
        c = _dot_exact_l(tri, lf) + carry_ref[...]
        c_ref[...] = c
        carry_ref[...] = c[tm - 1:tm, :]

    row_spec = lambda w: pl.BlockSpec((tm, w), lambda i: (i, 0))
    const = lambda shape: pl.BlockSpec(shape, lambda i: (0, 0))
    heads = pl.BlockSpec((HEADS, tm, HEAD_DIM), lambda i: (0, i, 0))
    return _pcall(body, name="attn_prep", grid=(s // tm,),
                  in_specs=[row_spec(Z_COLS), const((1, ATTN_W)), const((1, ATTN_W)), const((1, LANES)), const((ATTN_W, ATTN_W))],
                  out_specs=[heads] * 3 + [row_spec(SSM_W), row_spec(SSM_W), row_spec(LANES)],
                  out_shape=[_sds((HEADS, s, HEAD_DIM), BF16)] * 3 + [_sds((s, SSM_W), BF16), _sds((s, SSM_W)), _sds((s, LANES))],
                  scratch_shapes=[pltpu.VMEM((1, LANES), F32)], dims=("arbitrary",))(z, gq, gk, bf, gg)


def _attn_fwd(qh, kh, vh, crow, hosted=None):
    _, s, _ = qh.shape
    tq = _row_tile(s)
    scale = HEAD_DIM ** -0.5

    hp = HEADS
    nq = s // tq
    fold = lambda t, op: op(t[:, :tq // 2], t[:, tq // 2:])

    def body(q_ref, k_ref, v_ref, c_ref, o_ref, lse_ref, s_s):
        i = pl.program_id(1)

        def first(j, ms, diagonal):
            off = pl.multiple_of(j * tq, tq)
            out = []
            for hh in range(hp):
                sc = _dot_nt(q_ref[hh], k_ref[hh, pl.ds(off, tq), :]) * scale - c_ref[hh, :, pl.ds(off, tq)]
                if diagonal:
                    causal = lax.broadcasted_iota(jnp.int32, (tq, tq), 1) <= lax.broadcasted_iota(jnp.int32, (tq, tq), 0)
                    sc = jnp.where(causal, sc, NEG_INF)
                s_s[hh, j] = sc
                out.append(jnp.maximum(ms[hh], fold(sc, jnp.maximum)))
            return tuple(out)

        ms = lax.fori_loop(0, i, lambda j, c: first(j, c, False), (jnp.full((tq, tq // 2), NEG_INF, F32),) * hp)
        ms = [jnp.max(t, axis=-1, keepdims=True) for t in first(i, ms, True)]

        def second(j, carry):
            rows = pl.ds(pl.multiple_of(j * tq, tq), tq)
            out = []
            for hh in range(hp):
                ls, acc = carry[hh]
                p = jnp.exp(s_s[hh, j] - ms[hh])
                out.append((ls + fold(p, jnp.add), acc + _dot(p.astype(BF16), v_ref[hh, rows, :])))
            return tuple(out)

        zero = (jnp.zeros((tq, tq // 2), F32), jnp.zeros((tq, HEAD_DIM), F32))
        for hh, (ls, acc) in enumerate(lax.fori_loop(0, i + 1, second, (zero,) * hp)):
            l = jnp.sum(ls, axis=-1, keepdims=True)
            o_ref[hh] = acc / l
            lse_ref[hh] = ms[hh] + jnp.log(l)

    blk = pl.BlockSpec((hp, tq, HEAD_DIM), lambda h, i: (h, i, 0))
    full = pl.BlockSpec((hp, s, HEAD_DIM), lambda h, i: (h, 0, 0))
    nh = HEADS // hp
    first = lambda: jnp.logical_and(pl.program_id(0) == 0, pl.program_id(1) == 0)
    last = lambda: jnp.logical_and(pl.program_id(0) == nh - 1, pl.program_id(1) == nq - 1)
    return _host_pcall(body, hosted, first, last, n_in=4, n_out=2, n_scratch=1, name="attn_fwd", grid=(nh, nq),
                       in_specs=[blk, full, full, pl.BlockSpec((hp, 1, s), lambda h, i: (h, 0, 0))],
                       out_specs=[blk, pl.BlockSpec((hp, tq, 1), lambda h, i: (h, i, 0))],
                       out_shape=[_sds((HEADS, s, HEAD_DIM)), _sds((HEADS, s, 1))],
                       scratch_shapes=[pltpu.VMEM((hp, nq, tq, tq), F32)],
                       dims=("parallel", "parallel"), operands=(qh, kh, vh, crow))


def _ssm_param_fn(lr, li, ls, br, bi):
    step = jnp.exp(ls)
    er = jnp.exp(lr * step)
    ab_re = er * jnp.cos(li * step)
    ab_im = er * jnp.sin(li * step)
    num_re = ab_re - 1.0
    num_im = ab_im
    den = lr * lr + li * li
    f_re = (num_re * lr + num_im * li) / den
    f_im = (num_im * lr - num_re * li) / den
    bb_re = f_re * br - f_im * bi
    bb_im = f_re * bi + f_im * br
    return ab_re, ab_im, bb_re, bb_im


_PARAM_SHAPE = (SSM_GROUPS * SSM_GROUP, SSM_STATE)


def _ssm_params(lr, li, ls, br, bi):
    def body(lr_ref, li_ref, ls_ref, br_ref, bi_ref, ar_ref, ai_ref, bbr_ref, bbi_ref):
        ar, ai, bbr, bbi = _ssm_param_fn(lr_ref[...], li_ref[...], ls_ref[...], br_ref[...], bi_ref[...])
        ar_ref[...] = ar
        ai_ref[...] = ai
        bbr_ref[...] = bbr
        bbi_ref[...] = bbi

    spec = pl.BlockSpec(_PARAM_SHAPE, lambda: (0, 0))
    return _pcall(body, name="ssm_params", in_specs=[spec] * 5, out_specs=[spec] * 4,
                  out_shape=[_sds(_PARAM_SHAPE)] * 4)(lr, li, ls, br, bi)


def _ssm_params_bwd(lr, li, ls, br, bi, dar, dai, dbbr, dbbi, expand_t):
    def body(lr_ref, li_ref, ls_ref, br_ref, bi_ref, dar_ref, dai_ref, dbbr_ref, dbbi_ref, et_ref,
             dlr_ref, dli_ref, dls_ref, dbr_ref, dbi_ref):
        _, vjp = jax.vjp(_ssm_param_fn, lr_ref[...], li_ref[...], ls_ref[...], br_ref[...], bi_ref[...])
        dlr, dli, dls, dbr, dbi = vjp((dar_ref[...], dai_ref[...], dbbr_ref[...], dbbi_ref[...]))
        et = et_ref[...]
        dlr_ref[...] = _dot_exact_l(et, dlr)
        dli_ref[...] = _dot_exact_l(et, dli)
        dls_ref[...] = jnp.sum(_dot_exact_l(et, dls), axis=-1, keepdims=True)
        dbr_ref[...] = dbr
        dbi_ref[...] = dbi

    spec = pl.BlockSpec(_PARAM_SHAPE, lambda: (0, 0))
    gspec = pl.BlockSpec((SSM_GROUPS, SSM_STATE), lambda: (0, 0))
    return _pcall(body, name="ssm_params_bwd",
                  in_specs=[spec] * 9 + [pl.BlockSpec((SSM_GROUPS, _PARAM_SHAPE[0]), lambda: (0, 0))],
                  out_specs=[gspec, gspec, pl.BlockSpec((SSM_GROUPS, 1), lambda: (0, 0)), spec, spec],
                  out_shape=[_sds((SSM_GROUPS, SSM_STATE))] * 2 + [_sds((SSM_GROUPS, 1))] + [_sds(_PARAM_SHAPE)] * 2,
                  )(lr, li, ls, br, bi, dar, dai, dbbr, dbbi, expand_t)


def _cmul(ar, ai, br, bi):
    return ar * br - ai * bi, ar * bi + ai * br


def _scan_consts(ar, ai, width, reverse):
    row = lax.broadcasted_iota(jnp.int32, (SUBLANES, width), 0)
    pw = [(ar, ai)]
    for _ in range(SUBLANES - 1):
        pw.append(_cmul(pw[-1][0], pw[-1][1], ar, ai))
    steps = []
    for d in (1, 2, 4):
        keep = (row < SUBLANES - d) if reverse else (row >= d)
        steps.append((d, jnp.where(keep, pw[d - 1][0], 0.0), jnp.where(keep, pw[d - 1][1], 0.0)))
    pr = jnp.zeros((SUBLANES, width), F32)
    pi = jnp.zeros((SUBLANES, width), F32)
    for r in range(SUBLANES):
        e = (SUBLANES - r) if reverse else (r + 1)
        pr = jnp.where(row == r, pw[e - 1][0], pr)
        pi = jnp.where(row == r, pw[e - 1][1], pi)
    return steps, pr, pi


def _scan_tile(xr, xi, cr, ci, consts, reverse):
    steps, pr, pi = consts
    for d, mr, mi in steps:
        sh = (SUBLANES - d) if reverse else d
        sr = pltpu.roll(xr, sh, 0)
        si = pltpu.roll(xi, sh, 0)
        xr, xi = xr + mr * sr - mi * si, xi + mr * si + mi * sr
    return xr + pr * cr - pi * ci, xi + pr * ci + pi * cr


def _ssm_fwd(ub, uf, bbr, bbi, ar, ai, ccr, cci, dsk, hosted=None):
    s = ub.shape[0]
    tm = min(SSM_ROWS, s)
    nt = tm // SUBLANES

    def body(ub_ref, u_ref, bbr_ref, bbi_ref, ar_ref, ai_ref, ccr_ref, cci_ref, dsk_ref,
             xr_ref, xi_ref, y_ref, cr_s, ci_s):
        i = pl.program_id(1)

        @pl.when(i == 0)
        def _():
            cr_s[...] = jnp.zeros_like(cr_s)
            ci_s[...] = jnp.zeros_like(ci_s)

        u_b = ub_ref[...]
        xr_ref[...] = _dot(u_b, bbr_ref[0])
        xi_ref[...] = _dot(u_b, bbi_ref[0])
        consts = _scan_consts(ar_ref[0], ai_ref[0], CHUNK_S, False)

        def tile(k, carry):
            cr, ci = carry
            sl = pl.ds(pl.multiple_of(k * SUBLANES, SUBLANES), SUBLANES)
            xr, xi = _scan_tile(xr_ref[sl, :], xi_ref[sl, :], cr, ci, consts, False)
            xr_ref[sl, :] = xr
            xi_ref[sl, :] = xi
            return xr[SUBLANES - 1:SUBLANES, :], xi[SUBLANES - 1:SUBLANES, :]

        cr, ci = lax.fori_loop(0, nt, tile, (cr_s[...], ci_s[...]))
        cr_s[...] = cr
        ci_s[...] = ci
        y_ref[...] = (_dot(xr_ref[...].astype(BF16), ccr_ref[0]) - _dot(xi_ref[...].astype(BF16), cci_ref[0])
                      + dsk_ref[...] * u_ref[...])

    wspec = lambda a, b: pl.BlockSpec((1, a, b), lambda j, i: (j, 0, 0))
    nb = s // tm
    first = lambda: jnp.logical_and(pl.program_id(0) == 0, pl.program_id(1) == 0)
    last = lambda: jnp.logical_and(pl.program_id(0) == SSM_CHUNKS - 1, pl.program_id(1) == nb - 1)
    return _host_pcall(
        body, hosted, first, last, n_in=9, n_out=3, n_scratch=2, name="ssm_fwd", grid=(SSM_CHUNKS, nb),
        in_specs=[pl.BlockSpec((tm, CHUNK_U), lambda j, i: (i, j)),
                  pl.BlockSpec((tm, CHUNK_U), lambda j, i: (i, j)),
                  wspec(CHUNK_U, CHUNK_S), wspec(CHUNK_U, CHUNK_S), wspec(1, CHUNK_S), wspec(1, CHUNK_S),
                  wspec(CHUNK_S, CHUNK_U), wspec(CHUNK_S, CHUNK_U),
                  pl.BlockSpec((1, CHUNK_U), lambda j, i: (0, j))],
        out_specs=[pl.BlockSpec((tm, CHUNK_S), lambda j, i: (i, j)), pl.BlockSpec((tm, CHUNK_S), lambda j, i: (i, j)),
                   pl.BlockSpec((tm, CHUNK_U), lambda j, i: (i, j))],
        out_shape=[_sds((s, N_STATE)), _sds((s, N_STATE)), _sds((s, SSM_W))],
        scratch_shapes=[pltpu.VMEM((1, CHUNK_S), F32)] * 2,
        dims=("parallel", "arbitrary"), operands=(ub, uf, bbr, bbi, ar, ai, ccr, cci, dsk))


def _ssm_glu(y, w_glu, b_glu):
    ge = _gelu(y)
    sg = _sigmoid(_dot(ge.astype(BF16), w_glu) + b_glu)
    return ge, sg


def _mix_out(y, att, x, w_glu, b_glu, g_att, g_ssm, w_out, g_ffn, hosted=None):
    s = x.shape[0]
    tm = _wide_tile(s)

    def body(y_ref, att_ref, x_ref, wg_ref, bg_ref, ga_ref, gs_ref, wo_ref, gf_ref, x1_ref, mix_ref, h2_ref):
        ge, sg = _ssm_glu(y_ref[...], wg_ref[...], bg_ref[...])
        ms = _rms(ge * sg, gs_ref[...]).astype(BF16)
        ma = _rms(_merge_heads(att_ref), ga_ref[...]).astype(BF16)
        mix_ref[:, 0:ATTN_W] = ma
        mix_ref[:, ATTN_W:D_MODEL] = ms
        x1 = x_ref[...] + (_dot(ma, wo_ref[0:ATTN_W, :]) + _dot(ms, wo_ref[ATTN_W:D_MODEL, :]))
        x1_ref[...] = x1
        h2_ref[...] = _rms(x1, gf_ref[...]).astype(BF16)

    row = lambda w: pl.BlockSpec((tm, w), lambda i: (i, 0))
    const = lambda a, b: pl.BlockSpec((a, b), lambda i: (0, 0))
    nb = s // tm
    return _host_pcall(body, hosted, lambda: pl.program_id(0) == 0, lambda: pl.program_id(0) == nb - 1,
                       n_in=9, n_out=3, n_scratch=0, name="mix_out", grid=(nb,),
                       in_specs=[row(SSM_W), pl.BlockSpec((HEADS, tm, HEAD_DIM), lambda i: (0, i, 0)), row(D_MODEL),
                                 const(SSM_W, SSM_W), const(1, SSM_W),
                                 const(1, ATTN_W), const(1, SSM_W), const(D_MODEL, D_MODEL), const(1, D_MODEL)],
                       out_specs=[row(D_MODEL)] * 3,
                       out_shape=[_sds((s, D_MODEL)), _sds((s, D_MODEL), BF16), _sds((s, D_MODEL), BF16)],
                       scratch_shapes=[], dims=("parallel",), operands=(y, att, x, w_glu, b_glu, g_att, g_ssm, w_out, g_ffn))


CONV_CHUNK = 64


def _conv_rows(pad_ref, w, b, r0, n):
    y = b + pad_ref[pl.ds(r0 + SUBLANES - 2, n), :] * w[0:1, :]
    y = y + pad_ref[pl.ds(r0 + SUBLANES - 1, n), :] * w[1:2, :]
    return y + pad_ref[pl.ds(r0 + SUBLANES, n), :] * w[2:3, :]


def _fill_front_pad(pad_ref, strip_ref, s):
    pad_ref[0:SUBLANES, :] = jnp.zeros((SUBLANES, STRIP), F32)
    for r0 in range(0, s, CONV_CHUNK):
        pad_ref[pl.ds(SUBLANES + r0, CONV_CHUNK), :] = strip_ref[pl.ds(r0, CONV_CHUNK), :]


def _conv_act(up, conv_w, conv_b):
    s = up.shape[0]

    def body(ug_ref, uv_ref, wg_ref, wv_ref, bg_ref, bv_ref, act_ref, pg_ref, pv_ref):
        _fill_front_pad(pg_ref, ug_ref, s)
        _fill_front_pad(pv_ref, uv_ref, s)
        wg, wv, bg, bv = wg_ref[...], wv_ref[...], bg_ref[...], bv_ref[...]
        for r0 in range(0, s, CONV_CHUNK):
            hg = _conv_rows(pg_ref, wg, bg, r0, CONV_CHUNK)
            hv = _conv_rows(pv_ref, wv, bv, r0, CONV_CHUNK)
            act_ref[pl.ds(r0, CONV_CHUNK), :] = (hg * _sigmoid(hg) * hv).astype(BF16)

    strip = lambda off: pl.BlockSpec((s, STRIP), lambda j: (0, j + off))
    wsp = lambda off: pl.BlockSpec((3, STRIP), lambda j: (0, j + off))
    bsp = lambda off: pl.BlockSpec((1, STRIP), lambda j: (0, j + off))
    return _pcall(body, name="conv_act", grid=(N_STRIPS,),
                  in_specs=[strip(0), strip(N_STRIPS), wsp(0), wsp(N_STRIPS), bsp(0), bsp(N_STRIPS)],
                  out_specs=pl.BlockSpec((s, STRIP), lambda j: (0, j)), out_shape=_sds((s, D_FF), BF16),
                  scratch_shapes=[pltpu.VMEM((s + SUBLANES, STRIP), F32)] * 2,
                  dims=("parallel",))(up, up, conv_w, conv_w, conv_b, conv_b)


def _down_loss(act, w_down, x1, tgt):
    s = x1.shape[0]
    tm = _wide_tile(s)

    def body(a_ref, w_ref, x1_ref, t_ref, dy_ref, dyb_ref, loss_ref):
        i = pl.program_id(0)

        @pl.when(i == 0)
        def _():
            loss_ref[...] = jnp.zeros_like(loss_ref)

        diff = x1_ref[...] + _dot(a_ref[...], w_ref[...]) - t_ref[...]
        dy = diff * (1.0 / D_MODEL)
        dy_ref[...] = dy
        dyb_ref[...] = dy.astype(BF16)
        loss_ref[...] += 0.5 * jnp.sum(diff * dy)

    row = lambda w: pl.BlockSpec((tm, w), lambda i: (i, 0))
    return _pcall(body, name="down_loss", grid=(s // tm,),
                  in_specs=[row(D_FF), pl.BlockSpec((D_FF, D_MODEL), lambda i: (0, 0)), row(D_MODEL), row(D_MODEL)],
                  out_specs=[row(D_MODEL), row(D_MODEL), pl.BlockSpec((SUBLANES, LANES), lambda i: (0, 0))],
                  out_shape=[_sds((s, D_MODEL)), _sds((s, D_MODEL), BF16), _sds((SUBLANES, LANES))],
                  dims=("arbitrary",))(act, w_down, x1, tgt)


def _conv_act_bwd(up, dact, conv_w, conv_b):
    s = up.shape[0]
    ch = CONV_CHUNK

    def body(ug_ref, uv_ref, da_ref, wg_ref, wv_ref, bg_ref, bv_ref, dup_ref, dcw_ref, pg_ref, pv_ref, dg_ref, dv_ref):
        _fill_front_pad(pg_ref, ug_ref, s)
        _fill_front_pad(pv_ref, uv_ref, s)
        zero = jnp.zeros((SUBLANES, STRIP), F32)
        dg_ref[pl.ds(s, SUBLANES), :] = zero
        dv_ref[pl.ds(s, SUBLANES), :] = zero
        wg, wv, bg, bv = wg_ref[...], wv_ref[...], bg_ref[...], bv_ref[...]
        tile_sum = lambda t: jnp.sum(t.reshape(ch // SUBLANES, SUBLANES, STRIP), axis=0)
        accs = [[zero] * 4, [zero] * 4]
        for r0 in range(0, s, ch):
            hg = _conv_rows(pg_ref, wg, bg, r0, ch)
            hv = _conv_rows(pv_ref, wv, bv, r0, ch)
            sg = _sigmoid(hg)
            da = da_ref[pl.ds(r0, ch), :]
            dhs = (da * hv * (sg * (1.0 + hg * (1.0 - sg))), da * (hg * sg))
            for half, (dh, d_ref, p_ref) in enumerate(zip(dhs, (dg_ref, dv_ref), (pg_ref, pv_ref))):
                d_ref[pl.ds(r0, ch), :] = dh
                for k in range(3):
                    accs[half][k] = accs[half][k] + tile_sum(dh * p_ref[pl.ds(r0 + SUBLANES - 2 + k, ch), :])
                accs[half][3] = accs[half][3] + tile_sum(dh)
        for half, (d_ref, w) in enumerate(((dg_ref, wg), (dv_ref, wv))):
            for r0 in range(0, s, ch):
                dup = (d_ref[pl.ds(r0, ch), :] * w[2:3, :] + d_ref[pl.ds(r0 + 1, ch), :] * w[1:2, :]
                       + d_ref[pl.ds(r0 + 2, ch), :] * w[0:1, :])
                dup_ref[half, pl.ds(r0, ch), :] = dup.astype(BF16)
            rid = lax.broadcasted_iota(jnp.int32, (SUBLANES, STRIP), 0)
            out = zero
            for k in range(4):
                out = jnp.where(rid == k, jnp.sum(accs[half][k], axis=0, keepdims=True), out)
            dcw_ref[half] = out

    strip = lambda off: pl.BlockSpec((s, STRIP), lambda j: (0, j + off))
    wsp = lambda off: pl.BlockSpec((3, STRIP), lambda j: (0, j + off))
    bsp = lambda off: pl.BlockSpec((1, STRIP), lambda j: (0, j + off))
    return _pcall(body, name="conv_act_bwd", grid=(N_STRIPS,),
                  in_specs=[strip(0), strip(N_STRIPS), strip(0), wsp(0), wsp(N_STRIPS), bsp(0), bsp(N_STRIPS)],
                  out_specs=[pl.BlockSpec((2, s, STRIP), lambda j: (0, 0, j)), pl.BlockSpec((2, SUBLANES, STRIP), lambda j: (0, 0, j))],
                  out_shape=[_sds((2, s, D_FF), BF16), _sds((2, SUBLANES, D_FF))],
                  scratch_shapes=[pltpu.VMEM((s + SUBLANES, STRIP), F32)] * 4,
                  dims=("parallel",))(up, up, dact, conv_w, conv_w, conv_b, conv_b)


def _mix_bwd(dy, dh2, x1, g_ffn, w_out, y, att, w_glu, b_glu, g_att, g_ssm, hosted=None):
    s = dy.shape[0]
    tm = _wide_tile(s)

    def body(dy_ref, dh2_ref, x1_ref, gf_ref, wo_ref, y_ref, att_ref, wg_ref, bg_ref, ga_ref, gs_ref,
             dx1_ref, dx1b_ref, datt_ref, dys_ref, dwg_ref, dgf_ref, dga_ref, dgs_ref, dbg_ref):
        i = pl.program_id(0)

        @pl.when(i == 0)
        def _():
            for r in (dwg_ref, dgf_ref, dga_ref, dgs_ref, dbg_ref):
                r[...] = jnp.zeros_like(r)

        dxn, dgf = _rms_bwd(x1_ref[...], gf_ref[...], dh2_ref[...])
        dx1 = dy_ref[...] + dxn
        dx1_ref[...] = dx1
        dx1b = dx1.astype(BF16)
        dx1b_ref[...] = dx1b
        dgf_ref[...] += dgf
        dma = _dot_nt(dx1b, wo_ref[0:ATTN_W, :])
        dms = _dot_nt(dx1b, wo_ref[ATTN_W:D_MODEL, :])
        datt, dga = _rms_bwd(_merge_heads(att_ref), ga_ref[...], dma)
        _split_heads(datt_ref, datt)
        dga_ref[...] += dga
        yv = y_ref[...]
        ge, sg = _ssm_glu(yv, wg_ref[...], bg_ref[...])
        dssm, dgs = _rms_bwd(ge * sg, gs_ref[...], dms)
        dgs_ref[...] += dgs
        dgl = dssm * ge * sg * (1.0 - sg)
        dglb = dgl.astype(BF16)
        dge = dssm * sg + _dot_nt(dglb, wg_ref[...])
        dbg_ref[...] += jnp.sum(dgl, axis=0, keepdims=True)
        dwg_ref[...] += _dot_tn(ge.astype(BF16), dglb)
        dys_ref[...] = dge * _gelu_grad(yv)

    row = lambda w: pl.BlockSpec((tm, w), lambda i: (i, 0))
    const = lambda a, b: pl.BlockSpec((a, b), lambda i: (0, 0))
    heads = pl.BlockSpec((HEADS, tm, HEAD_DIM), lambda i: (0, i, 0))
    nb = s // tm
    return _host_pcall(
        body, hosted, lambda: pl.program_id(0) == 0, lambda: pl.program_id(0) == nb - 1, n_in=11, n_out=9, n_scratch=0,
        name="mix_bwd", grid=(nb,),
        in_specs=[row(D_MODEL), row(D_MODEL), row(D_MODEL), const(1, D_MODEL), const(D_MODEL, D_MODEL), row(SSM_W),
                  heads, const(SSM_W, SSM_W), const(1, SSM_W), const(1, ATTN_W), const(1, SSM_W)],
        out_specs=[row(D_MODEL), row(D_MODEL), heads, row(SSM_W), const(SSM_W, SSM_W), const(1, D_MODEL),
                   const(1, ATTN_W), const(1, SSM_W), const(1, SSM_W)],
        out_shape=[_sds((s, D_MODEL)), _sds((s, D_MODEL), BF16), _sds((HEADS, s, HEAD_DIM)), _sds((s, SSM_W)),
                   _sds((SSM_W, SSM_W)), _sds((1, D_MODEL)), _sds((1, ATTN_W)), _sds((1, SSM_W)), _sds((1, SSM_W))],
        scratch_shapes=[], dims=("arbitrary",), operands=(dy, dh2, x1, g_ffn, w_out, y, att, w_glu, b_glu, g_att, g_ssm))


def _ssm_bwd(dys, uf, ub, xr, xi, bbr, bbi, ar, ai, ccr, cci, dsk, hosted=None):
    s = dys.shape[0]
    tm = min(SSM_ROWS, s)
    nb = s // tm
    nt = tm // SUBLANES

    def body(dy_ref, u_ref, ub_ref, xr_ref, xi_ref, xrp_ref, xip_ref, bbr_ref, bbi_ref, ar_ref, ai_ref, ccr_ref,
             cci_ref, dsk_ref, du_ref, dbbr_ref, dbbi_ref, dccr_ref, dcci_ref, dar_ref, dai_ref, dd_ref,
             gr_s, gi_s, cr_s, ci_s, accr_s, acci_s):
        i = pl.program_id(1)
        first_block = i == nb - 1

        @pl.when(i == 0)
        def _():
            for r in (cr_s, ci_s, accr_s, acci_s, dbbr_ref, dbbi_ref, dccr_ref, dcci_ref, dd_ref):
                r[...] = jnp.zeros_like(r)

        dy = dy_ref[...]
        dyb = dy.astype(BF16)
        gr_s[...] = _dot_nt(dyb, ccr_ref[0])
        gi_s[...] = -_dot_nt(dyb, cci_ref[0])
        consts = _scan_consts(ar_ref[0], -ai_ref[0], CHUNK_S, True)
        row = lax.broadcasted_iota(jnp.int32, (SUBLANES, CHUNK_S), 0)

        def tile(kk, carry):
            cr, ci, accr, acci = carry
            k = nt - 1 - kk
            sl = pl.ds(pl.multiple_of(k * SUBLANES, SUBLANES), SUBLANES)
            gr, gi = _scan_tile(gr_s[sl, :], gi_s[sl, :], cr, ci, consts, True)
            gr_s[sl, :] = gr
            gi_s[sl, :] = gi
            slp = pl.ds(pl.multiple_of(jnp.maximum(k - 1, 0) * SUBLANES, SUBLANES), SUBLANES)
            inner = k > 0
            pr_t = jnp.where(inner, xr_ref[slp, :], xrp_ref[...])
            pi_t = jnp.where(inner, xi_ref[slp, :], xip_ref[...])
            live = jnp.logical_or(inner, jnp.logical_not(first_block))
            top_r = jnp.where(live, pltpu.roll(pr_t, 1, 0), 0.0)
            top_i = jnp.where(live, pltpu.roll(pi_t, 1, 0), 0.0)
            xpr = jnp.where(row == 0, top_r, pltpu.roll(xr_ref[sl, :], 1, 0))
            xpi = jnp.where(row == 0, top_i, pltpu.roll(xi_ref[sl, :], 1, 0))
            accr = accr + gr * xpr + gi * xpi
            acci = acci + gi * xpr - gr * xpi
            return gr[0:1, :], gi[0:1, :], accr, acci

        zeros = jnp.zeros((SUBLANES, CHUNK_S), F32)
        cr, ci, accr, acci = lax.fori_loop(0, nt, tile, (cr_s[...], ci_s[...], zeros, zeros))
        cr_s[...] = cr
        ci_s[...] = ci
        accr_s[...] += accr
        acci_s[...] += acci
        grb = gr_s[...].astype(BF16)
        gib = gi_s[...].astype(BF16)
        u_b = ub_ref[...]
        du_ref[...] = _dot_nt(grb, bbr_ref[0]) + _dot_nt(gib, bbi_ref[0]) + dsk_ref[...] * dy
        dbbr_ref[0] += _dot_tn(u_b, grb)
        dbbi_ref[0] += _dot_tn(u_b, gib)
        dccr_ref[0] += _dot_tn(xr_ref[...].astype(BF16), dyb)
        dcci_ref[0] -= _dot_tn(xi_ref[...].astype(BF16), dyb)
        dd_ref[...] += jnp.sum(dy * u_ref[...], axis=0, keepdims=True)

        @pl.when(i == nb - 1)
        def _():
            dar_ref[0] = jnp.sum(accr_s[...], axis=0, keepdims=True)
            dai_ref[0] = jnp.sum(acci_s[...], axis=0, keepdims=True)

    tiles_per_block = tm // SUBLANES
    rb = lambda i: nb - 1 - i
    wspec = lambda a, b: pl.BlockSpec((1, a, b), lambda j, i: (j, 0, 0))
    xblk = pl.BlockSpec((tm, CHUNK_S), lambda j, i: (rb(i), j))
    xprev = pl.BlockSpec((SUBLANES, CHUNK_S), lambda j, i: (jnp.maximum(rb(i) * tiles_per_block - 1, 0), j))
    ublk = pl.BlockSpec((tm, CHUNK_U), lambda j, i: (rb(i), j))
    first = lambda: jnp.logical_and(pl.program_id(0) == 0, pl.program_id(1) == 0)
    last = lambda: jnp.logical_and(pl.program_id(0) == SSM_CHUNKS - 1, pl.program_id(1) == nb - 1)
    return _host_pcall(
        body, hosted, first, last, n_in=14, n_out=8, n_scratch=6, name="ssm_bwd", grid=(SSM_CHUNKS, nb),
        in_specs=[ublk, ublk, ublk, xblk, xblk, xprev, xprev,
                  wspec(CHUNK_U, CHUNK_S), wspec(CHUNK_U, CHUNK_S), wspec(1, CHUNK_S), wspec(1, CHUNK_S),
                  wspec(CHUNK_S, CHUNK_U), wspec(CHUNK_S, CHUNK_U), pl.BlockSpec((1, CHUNK_U), lambda j, i: (0, j))],
        out_specs=[ublk, wspec(CHUNK_U, CHUNK_S), wspec(CHUNK_U, CHUNK_S), wspec(CHUNK_S, CHUNK_U),
                   wspec(CHUNK_S, CHUNK_U), wspec(1, CHUNK_S), wspec(1, CHUNK_S),
                   pl.BlockSpec((1, CHUNK_U), lambda j, i: (0, j))],
        out_shape=[_sds((s, SSM_W)), _sds((SSM_CHUNKS, CHUNK_U, CHUNK_S)), _sds((SSM_CHUNKS, CHUNK_U, CHUNK_S)),
                   _sds((SSM_CHUNKS, CHUNK_S, CHUNK_U)), _sds((SSM_CHUNKS, CHUNK_S, CHUNK_U)),
                   _sds((SSM_CHUNKS, 1, CHUNK_S)), _sds((SSM_CHUNKS, 1, CHUNK_S)), _sds((1, SSM_W))],
        scratch_shapes=[pltpu.VMEM((tm, CHUNK_S), F32)] * 2 + [pltpu.VMEM((1, CHUNK_S), F32)] * 2
                       + [pltpu.VMEM((SUBLANES, CHUNK_S), F32)] * 2,
        dims=("parallel", "arbitrary"), operands=(dys, uf, ub, xr, xi, xr, xi, bbr, bbi, ar, ai, ccr, cci, dsk))


def _attn_probs(q, ks, cs, lse, scale, diagonal):
    p = jnp.exp(_dot_nt(q, ks) * scale - cs - lse)
    if diagonal:
        tq, tk = p.shape
        causal = lax.broadcasted_iota(jnp.int32, (tq, tk), 1) <= lax.broadcasted_iota(jnp.int32, (tq, tk), 0)
        p = jnp.where(causal, p, 0.0)
    return p


def _attn_bwd(qh, kh, vh, crow, lse, doh, hosted=None):
    _, s, _ = qh.shape
    tq = _row_tile(s)
    nq = s // tq
    scale = HEAD_DIM ** -0.5
    hp = HEADS_PER_STEP

    def body(q_ref, k_ref, v_ref, c_ref, lse_ref, do_ref, dq_ref, dk_ref, dv_ref, dc_ref, p_s, dp_s):
        i = pl.program_id(1)

        @pl.when(i == 0)
        def _():
            for r in (dk_ref, dv_ref, dc_ref):
                r[...] = jnp.zeros_like(r)

        dobs = [do_ref[hh].astype(BF16) for hh in range(hp)]

        def first(j, dls, diagonal):
            off = pl.multiple_of(j * tq, tq)
            out = []
            for hh in range(hp):
                p = _attn_probs(q_ref[hh], k_ref[hh, pl.ds(off, tq), :], c_ref[hh, :, pl.ds(off, tq)], lse_ref[hh],
                                scale, diagonal)
                dp = _dot_nt(dobs[hh], v_ref[hh, pl.ds(off, tq), :])
                p_s[hh, j] = p
                dp_s[hh, j] = dp
                out.append(dls[hh] + jnp.sum(p * dp, axis=-1, keepdims=True))
            return tuple(out)

        zero_col = jnp.zeros((tq, 1), F32)
        dls = lax.fori_loop(0, i, lambda j, c: first(j, c, False), (zero_col,) * hp)
        dls = first(i, dls, True)

        def second(j, dqs):
            rows = pl.ds(pl.multiple_of(j * tq, tq), tq)
            out = []
            for hh in range(hp):
                p = p_s[hh, j]
                ds = p * (dp_s[hh, j] - dls[hh])
                dsb = ds.astype(BF16)
                dv_ref[hh, rows, :] += _dot_tn(p.astype(BF16), dobs[hh])
                dk_ref[hh, rows, :] += _dot_tn(dsb, q_ref[hh]) * scale
                dc_ref[hh, :, rows] -= jnp.sum(ds, axis=0, keepdims=True)
                out.append(dqs[hh] + _dot(dsb, k_ref[hh, rows, :]))
            return tuple(out)

        dqs = lax.fori_loop(0, i + 1, second, (jnp.zeros((tq, HEAD_DIM), F32),) * hp)
        for hh in range(hp):
            dq_ref[hh] = dqs[hh] * scale

    blk = pl.BlockSpec((hp, tq, HEAD_DIM), lambda h, i: (h, i, 0))
    full = pl.BlockSpec((hp, s, HEAD_DIM), lambda h, i: (h, 0, 0))
    crow_spec = pl.BlockSpec((hp, 1, s), lambda h, i: (h, 0, 0))
    nh = HEADS // hp
    first = lambda: jnp.logical_and(pl.program_id(0) == 0, pl.program_id(1) == 0)
    last = lambda: jnp.logical_and(pl.program_id(0) == nh - 1, pl.program_id(1) == nq - 1)
    return _host_pcall(body, hosted, first, last, n_in=6, n_out=4, n_scratch=2, name="attn_bwd", grid=(nh, nq),
                       in_specs=[blk, full, full, crow_spec, pl.BlockSpec((hp, tq, 1), lambda h, i: (h, i, 0)), blk],
                       out_specs=[blk, full, full, crow_spec],
                       out_shape=[_sds((HEADS, s, HEAD_DIM))] * 3 + [_sds((HEADS, 1, s))],
                       scratch_shapes=[pltpu.VMEM((hp, nq, tq, tq), F32)] * 2,
                       dims=("parallel", "arbitrary"), operands=(qh, kh, vh, crow, lse, doh))


def _prep_bwd(z, dqn, dkn, dv, du, dc, gq, gk, bf, gg, hosted=None):
    s = z.shape[0]
    tm = _row_tile(s)
    nb = s // tm

    def body(z_ref, dqn_ref, dkn_ref, dv_ref, du_ref, dc_ref, gq_ref, gk_ref, bf_ref, gg_ref,
             dz_ref, dgq_ref, dgk_ref, dbf_ref, carry_ref):
        i = pl.program_id(0)

        @pl.when(i == 0)
        def _():
            for r in (dgq_ref, dgk_ref, dbf_ref, carry_ref):
                r[...] = jnp.zeros_like(r)

        gg_m = gg_ref[...]

        def head_norm_bwd(t, g, dn):
            r = lax.rsqrt(_dot_hi_lo(t * t, gg_m) * (1.0 / HEAD_DIM) + EPS)
            w = dn * g
            mean_wt = _dot_hi_lo(w * t, gg_m) * (1.0 / HEAD_DIM)
            return r * w - t * (r * r * r) * mean_wt, jnp.sum(dn * t * r, axis=0, keepdims=True)

        dq, dgq = head_norm_bwd(z_ref[:, 0:ATTN_W], gq_ref[...], _merge_heads(dqn_ref))
        dk, dgk = head_norm_bwd(z_ref[:, ATTN_W:2 * ATTN_W], gk_ref[...], _merge_heads(dkn_ref))
        dgq_ref[...] += dgq
        dgk_ref[...] += dgk
        row = lax.broadcasted_iota(jnp.int32, (tm, tm), 0)
        col = lax.broadcasted_iota(jnp.int32, (tm, tm), 1)
        triu = (col >= row).astype(BF16)
        dlf = _dot_exact_l(triu, dc_ref[...]) + carry_ref[...]
        carry_ref[...] = dlf[0:1, :]
        df = dlf * _sigmoid(-_forget_logits(z_ref, bf_ref))
        dbf_ref[...] += jnp.sum(df, axis=0, keepdims=True)
        dz_ref[:, 0:ATTN_W] = dq.astype(BF16)
        dz_ref[:, ATTN_W:2 * ATTN_W] = dk.astype(BF16)
        dz_ref[:, 2 * ATTN_W:3 * ATTN_W] = _merge_heads(dv_ref).astype(BF16)
        tail = jnp.concatenate([df[:, :HEADS], du_ref[...], jnp.zeros((tm, Z_COLS - IN_COLS), F32)], axis=-1)
        dz_ref[:, F_COL0:Z_COLS] = tail.astype(BF16)

    row_spec = lambda w: pl.BlockSpec((tm, w), lambda i: (nb - 1 - i, 0))
    const = lambda shape: pl.BlockSpec(shape, lambda i: (0, 0))
    return _host_pcall(
        body, hosted, lambda: pl.program_id(0) == 0, lambda: pl.program_id(0) == nb - 1, n_in=10, n_out=4, n_scratch=1,
        name="prep_bwd", grid=(nb,),
        in_specs=[row_spec(Z_COLS)] + [pl.BlockSpec((HEADS, tm, HEAD_DIM), lambda i: (0, nb - 1 - i, 0))] * 3
                 + [row_spec(ATTN_W), row_spec(LANES), const((1, ATTN_W)),
                    const((1, ATTN_W)), const((1, LANES)), const((ATTN_W, ATTN_W))],
        out_specs=[row_spec(Z_COLS), const((1, ATTN_W)), const((1, ATTN_W)), const((1, LANES))],
        out_shape=[_sds((s, Z_COLS), BF16), _sds((1, ATTN_W)), _sds((1, ATTN_W)), _sds((1, LANES))],
        scratch_shapes=[pltpu.VMEM((1, LANES), F32)], dims=("arbitrary",),
        operands=(z, dqn, dkn, dv, du, dc, gq, gk, bf, gg))


def _in_norm_bwd(x, g_mix, dh, dx1, hosted=None):
    s = x.shape[0]
    tm = _wide_tile(s)

    def body(x_ref, g_ref, dh_ref, dx1_ref, dx_ref, dg_ref):
        i = pl.program_id(0)

        @pl.when(i == 0)
        def _():
            dg_ref[...] = jnp.zeros_like(dg_ref)

        dxn, dg = _rms_bwd(x_ref[...], g_ref[...], dh_ref[...])
        dx_ref[...] = dx1_ref[...] + dxn
        dg_ref[...] += dg

    row = pl.BlockSpec((tm, D_MODEL), lambda i: (i, 0))
    vec = pl.BlockSpec((1, D_MODEL), lambda i: (0, 0))
    nb = s // tm
    return _host_pcall(body, hosted, lambda: pl.program_id(0) == 0, lambda: pl.program_id(0) == nb - 1,
                       n_in=4, n_out=2, n_scratch=0, name="in_norm_bwd", grid=(nb,), in_specs=[row, vec, row, row],
                       out_specs=[row, vec], out_shape=[_sds((s, D_MODEL)), _sds((1, D_MODEL))], scratch_shapes=[],
                       dims=("arbitrary",), operands=(x, g_mix, dh, dx1))


def _adamw_refs(w_ref, g_ref, m_ref, v_ref, d_ref, mo_ref, vo_ref):
    gv = g_ref[...]
    mn = ADAM_B1 * m_ref[...] + (1.0 - ADAM_B1) * gv
    vn = ADAM_B2 * v_ref[...] + (1.0 - ADAM_B2) * (gv * gv)
    m_hat = mn / (1.0 - ADAM_B1 ** ADAM_STEP)
    v_hat = vn / (1.0 - ADAM_B2 ** ADAM_STEP)
    d_ref[...] = -ADAM_LR * (m_hat / (jnp.sqrt(v_hat) + ADAM_EPS) + ADAM_WD * w_ref[...])
    mo_ref[...] = mn
    vo_ref[...] = vn


def _adamw_small(ws, gs, ms, vs):
    n = len(ws)

    def body(*refs):
        ins, outs = refs[:4 * n], refs[4 * n:]
        for i in range(n):
            _adamw_refs(ins[i], ins[n + i], ins[2 * n + i], ins[3 * n + i], *outs[3 * i:3 * i + 3])

    vm = pl.BlockSpec(memory_space=pltpu.VMEM)
    out_shape = [_sds(w.shape) for w in ws for _ in range(3)]
    return _pallas(body, name="adamw_small", in_specs=[vm] * (4 * n), out_specs=[vm] * (3 * n), out_shape=out_shape,
                   compiler_params=pltpu.CompilerParams(vmem_limit_bytes=VMEM_LIMIT))(*ws, *gs, *ms, *vs)


def _adamw(w, g, m, v, *, name):
    r, c = w.shape
    tr = r
    for cand in (256, 176, 128, 64):
        if r > cand and r % cand == 0:
            tr = cand
            break

    def body(w_ref, g_ref, m_ref, v_ref, d_ref, mo_ref, vo_ref):
        _adamw_refs(w_ref, g_ref, m_ref, v_ref, d_ref, mo_ref, vo_ref)

    spec = pl.BlockSpec((tr, c), lambda i: (i, 0))
    return _pcall(body, name=name, grid=(r // tr,), in_specs=[spec] * 4, out_specs=[spec] * 3,
                  out_shape=[_sds((r, c))] * 3, dims=("parallel",))(w, g, m, v)


def _prefetch_call(body, *, name, grid, in_specs, out_specs, out_shape, operands):
    grid_spec = pltpu.PrefetchScalarGridSpec(num_scalar_prefetch=1, grid=grid, in_specs=in_specs, out_specs=out_specs)
    params = pltpu.CompilerParams(dimension_semantics=("parallel",) * len(grid), vmem_limit_bytes=VMEM_LIMIT)
    return _pallas(body, name=name, grid_spec=grid_spec, out_shape=out_shape, compiler_params=params)(*operands)


def _place_cols(buf, shard, place):
    rows, cols = shard.shape
    tr = 256

    def body(place_ref, s_ref, b_ref, o_ref):
        o_ref[...] = s_ref[...]

    grid_spec = pltpu.PrefetchScalarGridSpec(
        num_scalar_prefetch=1, grid=(rows // tr,),
        in_specs=[pl.BlockSpec((tr, cols), lambda i, p: (i, 0)), pl.BlockSpec(memory_space=pltpu.HBM)],
        out_specs=pl.BlockSpec((tr, cols), lambda i, p: (i, p[0])))
    return _pallas(body, name="place_own_cols", grid_spec=grid_spec, out_shape=_sds(buf.shape, buf.dtype),
                   input_output_aliases={2: 0},
                   compiler_params=pltpu.CompilerParams(dimension_semantics=("parallel",),
                                                        vmem_limit_bytes=VMEM_LIMIT))(place, shard, buf)


def _half_rows_tile(hr):
    return hr if hr <= 256 else 176 if hr % 176 == 0 else 256


def _add_half(g, landed, place, *, name):
    def body(place_ref, g_ref, l_ref, o_ref):
        own = g_ref[0] if len(g_ref.shape) == 4 else g_ref[...]
        o_ref[...] = (own + l_ref[...]).astype(BF16)

    if g.ndim == 4:
        _, _, hr, c = g.shape
        tr = _half_rows_tile(hr)
        blk = (1, tr, c)
        return _prefetch_call(
            body, name=name, grid=(N_CHIPS, hr // tr),
            in_specs=[pl.BlockSpec((1,) + blk, lambda j, i, p: (j, p[1], i, 0)), pl.BlockSpec(blk, lambda j, i, p: (j, i, 0))],
            out_specs=pl.BlockSpec(blk, lambda j, i, p: (j, i, 0)), out_shape=_sds(landed.shape, BF16),
            operands=(place, g, landed))
    hr, c = landed.shape
    tr, tc = 256, _tile(c, 2176)
    nb = hr // tr
    return _prefetch_call(
        body, name=name, grid=(nb, c // tc),
        in_specs=[pl.BlockSpec((tr, tc), lambda i, j, p: (p[1] * nb + i, j)), pl.BlockSpec((tr, tc), lambda i, j, p: (i, j))],
        out_specs=pl.BlockSpec((tr, tc), lambda i, j, p: (i, j)), out_shape=_sds(landed.shape, BF16),
        operands=(place, g, landed))


def _sum_chips(chip_sum, lands, place, *, name, tc, window_stride=0):
    _, hr, c = lands.shape
    tr = _half_rows_tile(hr)
    nb = hr // tr
    ncb = c // tc

    def body(place_ref, own_ref, a_ref, b_ref, c_ref, o_ref):
        own = own_ref[0] if len(own_ref.shape) == 3 else own_ref[...]
        o_ref[...] = ((own.astype(F32) + a_ref[0].astype(F32)) + b_ref[0].astype(F32)) + c_ref[0].astype(F32)

    land = lambda k: pl.BlockSpec((1, tr, tc), lambda i, j, p: ((p[0] + k) % N_CHIPS, i, j))
    if chip_sum.ndim == 3:
        own_spec = land(0)
    else:
        stride = window_stride // tc
        own_spec = pl.BlockSpec((tr, tc), lambda i, j, p: (i, p[0] * stride + j))
    return _prefetch_call(
        body, name=name, grid=(nb, ncb), in_specs=[own_spec, land(1), land(2), land(3)],
        out_specs=pl.BlockSpec((tr, tc), lambda i, j, p: (p[1] * nb + i, j)), out_shape=_sds((2 * hr, c)),
        operands=(place, chip_sum, lands, lands, lands))


_HBM = pl.BlockSpec(memory_space=pltpu.HBM)


def _place():
    x, y, c = lax.axis_index("x"), lax.axis_index("y"), lax.axis_index("c")
    chips = [(1 - x, y), (x, 1 - y), (1 - x, 1 - y)]
    return x, y, c, chips


def _rcopy(src, dst, send_sem, recv_sem, to):
    return pltpu.make_async_remote_copy(src_ref=src, dst_ref=dst, send_sem=send_sem, recv_sem=recv_sem,
                                        device_id=to, device_id_type=MESH)


UP_COLS = 2 * D_FF // N_CHIPS
IN_WINDOW = 640
IN_STRIDE = 512


class _Hosted:
    def __init__(self, operands, out_shapes, n_sems, start, finish, aliases=None, local_sems=0):
        self.operands, self.out_shapes, self.n_sems = list(operands), list(out_shapes), n_sems
        self.start, self.finish, self.aliases, self.local_sems = start, finish, dict(aliases or {}), local_sems

    def scratch(self):
        return ([pltpu.SemaphoreType.DMA((self.n_sems,)), pltpu.SemaphoreType.DMA((self.n_sems,))]
                + [pltpu.SemaphoreType.DMA] * self.local_sems)


def _both(a, b):
    na, nao, nas = len(a.operands), len(a.out_shapes), len(a.scratch())

    def start(ins, outs, sems):
        a.start(ins[:na], outs[:nao], sems[:nas])
        b.start(ins[na:], outs[nao:], sems[nas:])

    def finish(ins, outs, sems):
        a.finish(ins[:na], outs[:nao], sems[:nas])
        b.finish(ins[na:], outs[nao:], sems[nas:])

    both = _Hosted(a.operands + b.operands, a.out_shapes + b.out_shapes, 0, start, finish,
                   aliases={**a.aliases, **{na + i: nao + o for i, o in b.aliases.items()}})
    both.scratch = lambda: a.scratch() + b.scratch()
    return both


def _then(a, b):
    nas = len(a.scratch())

    def finish(ins, outs, sems):
        a.finish(ins, outs, sems[:nas])
        b.start(ins, outs, sems[nas:])
        b.finish(ins, outs, sems[nas:])

    chain = _Hosted(a.operands, a.out_shapes, 0, lambda ins, outs, sems: a.start(ins, outs, sems[:nas]), finish,
                    aliases=a.aliases)
    chain.scratch = lambda: a.scratch() + b.scratch()
    return chain


def _run_hosted(hosted, *, name):
    n_in, n_out = len(hosted.operands), len(hosted.out_shapes)

    def body(*refs):
        parts = (refs[:n_in], refs[n_in:n_in + n_out], refs[n_in + n_out:])
        hosted.start(*parts)
        hosted.finish(*parts)

    return _pallas(body, name=name, in_specs=[_HBM] * n_in, out_specs=[_HBM] * n_out, out_shape=hosted.out_shapes,
                   input_output_aliases=hosted.aliases, scratch_shapes=hosted.scratch())(*hosted.operands)


def _host_pcall(core_body, hosted, first, last, *, n_in, n_out, n_scratch, name, grid, in_specs, out_specs, out_shape,
                scratch_shapes, dims, operands):
    if hosted is None:
        outs = _pcall(core_body, name=name, grid=grid, in_specs=in_specs, out_specs=out_specs, out_shape=out_shape,
                      scratch_shapes=scratch_shapes, dims=dims)(*operands)
        return outs, []
    hi, ho = len(hosted.operands), len(hosted.out_shapes)

    def body(*refs):
        a, b = n_in, n_in + hi
        c, d = b + n_out, b + n_out + ho
        e = d + n_scratch
        parts = (refs[a:b], refs[c:d], refs[e:])

        @pl.when(first())
        def _():
            hosted.start(*parts)

        core_body(*refs[:a], *refs[b:c], *refs[d:e])

        @pl.when(last())
        def _():
            hosted.finish(*parts)

    params = pltpu.CompilerParams(dimension_semantics=("arbitrary",) * len(grid), vmem_limit_bytes=VMEM_LIMIT)
    outs = _pallas(body, name=name, grid=grid, in_specs=list(in_specs) + [_HBM] * hi, out_specs=list(out_specs) + [_HBM] * ho,
                   out_shape=list(out_shape) + hosted.out_shapes, scratch_shapes=list(scratch_shapes) + hosted.scratch(),
                   input_output_aliases={n_in + a: n_out + b for a, b in hosted.aliases.items()},
                   compiler_params=params)(*operands, *hosted.operands)
    return outs[:n_out], outs[n_out:]


WHOLE_HALF = (0, 1, 1)


def _band_rows(src, hc, band):
    first, count, of = band
    hr = src.shape[0] // 2
    return pl.ds(hc * hr + first * (hr // of), count * (hr // of))


def _gather_slot(src, out, chip, hc, band=WHOLE_HALF):
    cols = src.shape[1]
    if len(out.shape) == 2:
        return out.at[_band_rows(src, hc, band), pl.ds(pl.multiple_of(chip * cols, LANES), cols)]
    return out.at[chip, _band_rows(src, hc, band), :]


def _gathered_shape(shard, by_cols):
    if by_cols:
        return _sds((shard.shape[0], N_CHIPS * shard.shape[1]), shard.dtype)
    return _sds((N_CHIPS,) + shard.shape, shard.dtype)


def _plan_gather_ici(shards, by_cols, whole=(), bands=None, into=None):
    n = len(shards)
    bands = bands or [WHOLE_HALF] * n
    into = into or [None] * n
    given = [w for w in range(n) if into[w] is not None]
    n_ops = n + len(whole)

    def copies(ins, outs, sems):
        send_sems, recv_sems = sems[0], sems[1]
        x, y, c, chips = _place()
        me = 2 * x + y
        sends, waits = [], []
        for w in range(n + len(whole)):
            for k, (cx, cy) in enumerate(chips):
                sem = (send_sems.at[3 * w + k], recv_sems.at[3 * w + k])
                if w < n:
                    sends.append(_rcopy(ins[w].at[_band_rows(ins[w], c, bands[w]), :],
                                        _gather_slot(ins[w], outs[w], me, c, bands[w]), *sem, (cx, cy, c)))
                    landed = _gather_slot(ins[w], outs[w], 2 * cx + cy, c, bands[w])
                else:
                    sends.append(_rcopy(ins[w], outs[w].at[me], *sem, (cx, cy, c)))
                    landed = outs[w].at[2 * cx + cy]
                waits.append(_rcopy(landed, landed, *sem, (cx, cy, c)))
        return sends, waits

    def start(ins, outs, sems):
        for cp in copies(ins, outs, sems)[0]:
            cp.start()

    def finish(ins, outs, sems):
        sends, waits = copies(ins, outs, sems)
        for cp in waits:
            cp.wait_recv()
        for cp in sends:
            cp.wait_send()

    out_shapes = [_gathered_shape(s, bc) for s, bc in zip(shards, by_cols)] + [_sds((N_CHIPS,) + a.shape, a.dtype) for a in whole]
    return _Hosted(list(shards) + list(whole) + [into[w] for w in given], out_shapes, 3 * n_ops, start, finish,
                   aliases={n_ops + i: w for i, w in enumerate(given)})


def _plan_gather_d2d(bufs, shard_shapes, bands=None):
    n = len(bufs)
    bands = bands or [WHOLE_HALF] * n

    def copies(ins, outs, sems):
        send_sems, recv_sems = sems
        x, y, c, chips = _place()
        sibling = (x, y, 1 - c)
        sends, waits = [], []
        for w in range(n):
            for k, (cx, cy) in enumerate(chips):
                sem = (send_sems.at[3 * w + k], recv_sems.at[3 * w + k])
                landed = _gather_slot(shard_shapes[w], outs[w], 2 * cx + cy, c, bands[w])
                other = _gather_slot(shard_shapes[w], outs[w], 2 * cx + cy, 1 - c, bands[w])
                sends.append(_rcopy(landed, landed, *sem, sibling))
                waits.append(_rcopy(other, other, *sem, sibling))
        return sends, waits

    def start(ins, outs, sems):
        for cp in copies(ins, outs, sems)[0]:
            cp.start()

    def finish(ins, outs, sems):
        sends, waits = copies(ins, outs, sems)
        for cp in waits:
            cp.wait_recv()
        for cp in sends:
            cp.wait_send()

    return _Hosted(bufs, [_sds(b.shape, b.dtype) for b in bufs], 3 * n, start, finish, aliases={w: w for w in range(n)})


def _plan_allgather_first(block):
    def copies(ins, outs, sems):
        send_sems, recv_sems = sems
        x, y, c, chips = _place()
        me = 4 * x + 2 * y + c
        peers = [(x, y, 1 - c)] + [(cx, cy, c) for cx, cy in chips]
        sends = [_rcopy(ins[0], outs[0].at[me], send_sems.at[k], recv_sems.at[k], p) for k, p in enumerate(peers)]
        waits = [_rcopy(outs[0].at[4 * px + 2 * py + pc], outs[0].at[4 * px + 2 * py + pc], send_sems.at[k],
                        recv_sems.at[k], (px, py, pc)) for k, (px, py, pc) in enumerate(peers)]
        return sends, waits

    def start(ins, outs, sems):
        for cp in copies(ins, outs, sems)[0]:
            cp.start()

    def finish(ins, outs, sems):
        sends, waits = copies(ins, outs, sems)
        for cp in waits:
            cp.wait_recv()
        for cp in sends:
            cp.wait_send()

    return _Hosted([block], [_sds((8,) + block.shape)], 4, start, finish)


def _plan_allgather_second(gathered):
    def copies(ins, outs, sems):
        send_sems, recv_sems = sems
        x, y, c, chips = _place()
        sends, waits = [], []
        for k, (cx, cy) in enumerate(chips):
            landed = outs[0].at[4 * cx + 2 * cy + c]
            other = outs[0].at[4 * cx + 2 * cy + 1 - c]
            sends.append(_rcopy(landed, landed, send_sems.at[k], recv_sems.at[k], (x, y, 1 - c)))
            waits.append(_rcopy(other, other, send_sems.at[k], recv_sems.at[k], (x, y, 1 - c)))
        return sends, waits

    def start(ins, outs, sems):
        for cp in copies(ins, outs, sems)[0]:
            cp.start()

    def finish(ins, outs, sems):
        sends, waits = copies(ins, outs, sems)
        for cp in waits:
            cp.wait_recv()
        for cp in sends:
            cp.wait_send()

    return _Hosted([gathered], [_sds(gathered.shape)], 3, start, finish, aliases={0: 0})


def _place_block(gathered, block, device):
    rows, lanes = block.shape

    def body(dev_ref, b_ref, g_ref, o_ref):
        o_ref[0] = b_ref[...]

    grid_spec = pltpu.PrefetchScalarGridSpec(
        num_scalar_prefetch=1, grid=(1,),
        in_specs=[pl.BlockSpec((rows, lanes), lambda i, d: (0, 0)), pl.BlockSpec(memory_space=pltpu.HBM)],
        out_specs=pl.BlockSpec((1, rows, lanes), lambda i, d: (d[0], 0, 0)))
    return _pallas(body, name="place_own_block", grid_spec=grid_spec, out_shape=_sds(gathered.shape),
                   input_output_aliases={2: 0},
                   compiler_params=pltpu.CompilerParams(dimension_semantics=("arbitrary",),
                                                        vmem_limit_bytes=VMEM_LIMIT))(device, block, gathered)


def _sum_devices(gathered):
    _, rows, lanes = gathered.shape
    tr = rows // 2 if rows % 16 == 0 else rows

    def body(g_ref, o_ref):
        acc = g_ref[0]
        for d in range(1, 8):
            acc = acc + g_ref[d]
        o_ref[...] = acc

    return _pcall(body, name="sum_devices", grid=(rows // tr,), in_specs=[pl.BlockSpec((8, tr, lanes), lambda i: (0, i, 0))],
                  out_specs=pl.BlockSpec((tr, lanes), lambda i: (i, 0)), out_shape=_sds((rows, lanes)), dims=("parallel",))(gathered)


def _plan_swap(grads):
    def copies(ins, outs, sems):
        send_sems, recv_sems = sems
        x, y, c, _ = _place()
        cps = []
        for w, g_ref in enumerate(ins):
            if len(g_ref.shape) == 4:
                theirs = g_ref.at[:, 1 - c]
            else:
                hr = g_ref.shape[0] // 2
                theirs = g_ref.at[pl.ds((1 - c) * hr, hr), :]
            cps.append(_rcopy(theirs, outs[w], send_sems.at[w], recv_sems.at[w], (x, y, 1 - c)))
        return cps

    def start(ins, outs, sems):
        for cp in copies(ins, outs, sems):
            cp.start()

    def finish(ins, outs, sems):
        for cp in copies(ins, outs, sems):
            cp.wait()

    out_shapes = [_sds((g.shape[0], g.shape[2], g.shape[3])) if g.ndim == 4 else _sds((g.shape[0] // 2, g.shape[1]))
                  for g in grads]
    return _Hosted(grads, out_shapes, len(grads), start, finish)


def _plan_scatter(chip_sums, windows):
    def copies(ins, outs, sems):
        send_sems, recv_sems = sems
        x, y, c, chips = _place()
        me = 2 * x + y
        sends, waits = [], []
        for w, s_ref in enumerate(ins):
            for k, (cx, cy) in enumerate(chips):
                tgt = 2 * cx + cy
                if windows[w] is not None:
                    stride, width = windows[w]
                    part = s_ref.at[:, pl.ds(pl.multiple_of(tgt * stride, LANES), width)]
                else:
                    part = s_ref.at[tgt]
                sem = (send_sems.at[3 * w + k], recv_sems.at[3 * w + k])
                sends.append(_rcopy(part, outs[w].at[me], *sem, (cx, cy, c)))
                slot = outs[w].at[tgt]
                waits.append(_rcopy(slot, slot, *sem, (cx, cy, c)))
        return sends, waits

    def start(ins, outs, sems):
        for cp in copies(ins, outs, sems)[0]:
            cp.start()

    def finish(ins, outs, sems):
        sends, waits = copies(ins, outs, sems)
        for cp in waits:
            cp.wait_recv()
        for cp in sends:
            cp.wait_send()

    out_shapes = [_sds((N_CHIPS, s.shape[0], win[1]), BF16) if win is not None else _sds(s.shape, BF16)
                  for s, win in zip(chip_sums, windows)]
    return _Hosted(chip_sums, out_shapes, 3 * len(chip_sums), start, finish)


def _plan_join(reds):
    def copies(ins, outs, sems):
        send_sems, recv_sems = sems
        x, y, c, _ = _place()
        sends, waits = [], []
        for w, out in enumerate(outs):
            hr = out.shape[0] // 2
            mine = out.at[pl.ds(c * hr, hr), :]
            theirs = out.at[pl.ds((1 - c) * hr, hr), :]
            sends.append(_rcopy(mine, mine, send_sems.at[w], recv_sems.at[w], (x, y, 1 - c)))
            waits.append(_rcopy(theirs, theirs, send_sems.at[w], recv_sems.at[w], (x, y, 1 - c)))
        return sends, waits

    def start(ins, outs, sems):
        for cp in copies(ins, outs, sems)[0]:
            cp.start()

    def finish(ins, outs, sems):
        sends, waits = copies(ins, outs, sems)
        for cp in waits:
            cp.wait_recv()
        for cp in sends:
            cp.wait_send()

    return _Hosted(reds, [_sds(r.shape) for r in reds], len(reds), start, finish, aliases={w: w for w in range(len(reds))})


def _allreduce_small(v):
    m_per = v.shape[0]

    def body(v_ref, out_ref, all_ref, send_sems, recv_sems, local_sem):
        x, y, c, chips = _place()
        me, sibling = (x, y, c), (x, y, 1 - c)

        def rows(px, py, pc):
            return all_ref.at[pl.ds((4 * px + 2 * py + pc) * m_per, m_per), :]

        def copy(k, block, to, src=None):
            return _rcopy(rows(*block) if src is None else src, rows(*block), send_sems.at[k], recv_sems.at[k], to)

        mine = pltpu.make_async_copy(v_ref, rows(*me), local_sem)
        mine.start()
        first = [copy(0, me, sibling, src=v_ref)]
        first += [copy(1 + k, me, (*chip, c), src=v_ref) for k, chip in enumerate(chips)]
        for cp in first:
            cp.start()
        passed = [copy(4 + k, (*chip, c), sibling) for k, chip in enumerate(chips)]
        for k, chip in enumerate(chips):
            copy(1 + k, (*chip, c), me).wait_recv()
            passed[k].start()
        copy(0, sibling, me).wait_recv()
        for k, chip in enumerate(chips):
            copy(4 + k, (*chip, 1 - c), me).wait_recv()
        for cp in first + passed:
            cp.wait_send()
        mine.wait()
        acc = all_ref[pl.ds(0, m_per), :]
        for d in range(1, 8):
            acc = acc + all_ref[pl.ds(d * m_per, m_per), :]
        out_ref[...] = acc

    vm = pl.BlockSpec(memory_space=pltpu.VMEM)
    return _pallas(body, name="allreduce_small", in_specs=[vm], out_specs=vm, out_shape=_sds((m_per, LANES)),
                          scratch_shapes=[pltpu.VMEM((8 * m_per, LANES), F32), pltpu.SemaphoreType.DMA((7,)),
                                          pltpu.SemaphoreType.DMA((7,)), pltpu.SemaphoreType.DMA],
                          compiler_params=pltpu.CompilerParams(vmem_limit_bytes=VMEM_LIMIT))(v)


def _block_diag(blocks):
    j, g, a, b = blocks.shape
    eye = jnp.eye(g, dtype=bool)[None, :, None, :, None]
    return jnp.where(eye, blocks[:, :, :, None, :], jnp.zeros((), blocks.dtype)).reshape(j, g * a, g * b)


def _diag_blocks(m, a, b):
    j = m.shape[0]
    g = m.shape[1] // a
    t = m.reshape(j, g, a, g, b)
    eye = jnp.eye(g, dtype=bool)[None, :, None, :, None]
    return jnp.sum(jnp.where(eye, t, 0.0), axis=3)


_SMALL = (("g_mix", (1024,)), ("b_f", (8,)), ("g_q", (64,)), ("g_k", (64,)), ("lambda_re", (32, 64)),
          ("lambda_im", (32, 64)), ("log_step", (32,)), ("b_re", (32, 64, 16)), ("b_im", (32, 64, 16)),
          ("c_re", (32, 16, 64)), ("c_im", (32, 16, 64)), ("d_skip", (32, 16)), ("b_glu", (512,)),
          ("g_attn_out", (512,)), ("g_ssm_out", (512,)), ("g_ffn", (1024,)), ("conv_b", (5632,)))


_LATE_SMALL = ("g_mix", "b_f", "g_q", "g_k")
_EARLY_SMALL = tuple(n for n, _ in _SMALL if n not in _LATE_SMALL)


def _packed_rows(n):
    tile = SUBLANES * LANES
    return -(-n // tile) * SUBLANES


def _pack_small(arrs):
    parts = []
    for a in arrs:
        flat = a.reshape(-1)
        rows = _packed_rows(flat.shape[0])
        parts.append(jnp.pad(flat, (0, rows * LANES - flat.shape[0])).reshape(rows, LANES))
    return jnp.concatenate(parts, axis=0)


def _unpack_small(buf, shapes):
    out, r = [], 0
    for shape in shapes:
        n = math.prod(shape)
        out.append(buf[r:r + _packed_rows(n)].reshape(-1)[:n].reshape(shape))
        r += _packed_rows(n)
    return out


def _halves(t):
    return t.reshape(N_CHIPS, 2, t.shape[0] // (2 * N_CHIPS), t.shape[1])


class _MeshComm:
    def __init__(self, args):
        x, y, self.core = lax.axis_index("x"), lax.axis_index("y"), lax.axis_index("c")
        self.chip = 2 * x + y
        self.place = jnp.stack([self.chip, self.core]).astype(jnp.int32)
        self.shards = {n: args[n].astype(BF16) for n in ("w_in", "w_glu", "w_out", "w_up", "w_down")}
        self.conv_w = args["conv_w"]

    def _own(self, stacked, mine):
        return lax.dynamic_update_slice(stacked, mine[None], (self.chip,) + (0,) * mine.ndim)

    def w_in(self):
        sh = self.shards["w_in"]
        (buf,) = _run_hosted(_then(_plan_gather_ici([sh], [False]), _plan_gather_d2d([sh], [sh])), name="gather_w_in")
        whole = self._own(buf, sh).transpose(1, 0, 2).reshape(D_MODEL, IN_COLS)
        return jnp.pad(whole, ((0, 0), (0, Z_COLS - IN_COLS)))

    def gather_first(self):
        self.mid = [self.shards[n] for n in ("w_glu", "w_out", "w_down")]
        return _plan_gather_ici(self.mid + [self.shards["w_up"]], [False, False, False, True], whole=[self.conv_w],
                                bands=[WHOLE_HALF] * 3 + [(0, 1, 4)])

    def gather_second(self, landed):
        self.g_cw = landed[4]
        return _both(_plan_gather_d2d(list(landed[:3]), self.mid),
                     _plan_gather_ici([self.shards["w_up"]], [True], bands=[(1, 3, 4)], into=[landed[3]]))

    def weights(self, gathered):
        g_glu, g_out, g_down = gathered[:3]
        own = self._own
        return (own(g_glu, self.mid[0]).reshape(SSM_W, SSM_W), own(g_out, self.mid[1]).reshape(D_MODEL, D_MODEL),
                own(g_down, self.mid[2]).reshape(D_FF, D_MODEL),
                own(self.g_cw, self.conv_w).transpose(1, 0, 2).reshape(3, 2 * D_FF))

    def gather_third(self, gathered):
        return _plan_gather_d2d([gathered[3]], [self.shards["w_up"]])

    def w_up(self, passed):
        return _place_cols(passed[0], self.shards["w_up"], self.place)

    def swap_down(self, d_w_down):
        self.d_down = _halves(d_w_down)
        return _plan_swap([self.d_down])

    def swap(self, landed_down, d_w_up, d_w_glu, d_w_out):
        self.sum_down = _add_half(self.d_down, landed_down[0], self.place, name="add_w_down")
        self.early = [d_w_up, _halves(d_w_glu), _halves(d_w_out)]
        return _both(_plan_scatter([self.sum_down], [None]), _plan_swap(self.early))

    def scatter(self, landed, small_block):
        self.land_down = landed[0]
        self.early_sums = [_add_half(g, l, self.place, name="add_" + n)
                           for g, l, n in zip(self.early, landed[1:], ("w_up", "w_glu", "w_out"))]
        return _both(_plan_scatter(self.early_sums, [(UP_COLS, UP_COLS), None, None]), _plan_allgather_first(small_block))

    def swap_in(self, d_w_in):
        self.d_in = d_w_in
        return _plan_swap([d_w_in])

    def scatter_in(self, landed):
        self.sum_in = _add_half(self.d_in, landed[0], self.place, name="add_w_in")
        return _plan_scatter([self.sum_in], [(IN_STRIDE, IN_WINDOW)])

    def small_second(self, landed_small):
        return _plan_allgather_second(landed_small[0])

    def reduce(self, lands):
        early_lands, (land_in,) = lands
        sum_in = self.sum_in
        es, el = self.early_sums, early_lands
        todo = [(sum_in, land_in, "w_in", LANES, IN_STRIDE), (es[1], el[1], "w_glu", SSM_W, 0),
                (es[2], el[2], "w_out", D_MODEL, 0), (es[0], el[0], "w_up", UP_COLS, UP_COLS),
                (self.sum_down, self.land_down, "w_down", D_MODEL, 0)]
        reds = _run_hosted(_plan_join([_sum_chips(s, l, self.place, name="sum_" + n, tc=tc, window_stride=st)
                                       for s, l, n, tc, st in todo]), name="join_halves")
        g_big = dict(zip(("w_in", "w_glu", "w_out", "w_up", "w_down"), reds))
        g_big["w_in"] = lax.dynamic_slice_in_dim(reds[0], 2 * self.chip, IN_COLS // N_CHIPS, axis=1)
        return g_big


def _local_step(x, tgt, p, comm):
    s = x.shape[0]
    row = lambda v: v.reshape(1, -1)
    g_mix, g_ffn = row(p["g_mix"]), row(p["g_ffn"])
    g_att, g_ssm, b_glu, conv_b = row(p["g_attn_out"]), row(p["g_ssm_out"]), row(p["b_glu"]), row(p["conv_b"])
    gq = row(jnp.tile(p["g_q"], HEADS))
    gk = row(jnp.tile(p["g_k"], HEADS))
    bf = row(jnp.pad(p["b_f"], (0, LANES - HEADS)))
    gg = jnp.kron(jnp.eye(HEADS, dtype=F32), jnp.ones((HEAD_DIM, HEAD_DIM), F32)).astype(BF16)
    dsk = row(p["d_skip"])

    rep = lambda a: jnp.repeat(a, SSM_GROUP, axis=0)
    lr, li = rep(p["lambda_re"]), rep(p["lambda_im"])
    ls = rep(jnp.broadcast_to(p["log_step"][:, None], (SSM_GROUPS, SSM_STATE)))
    bt_re = p["b_re"].transpose(0, 2, 1).reshape(_PARAM_SHAPE)
    bt_im = p["b_im"].transpose(0, 2, 1).reshape(_PARAM_SHAPE)
    a_re_rep, a_im_rep, bb_re, bb_im = _ssm_params(lr, li, ls, bt_re, bt_im)
    ar = a_re_rep[::SSM_GROUP].reshape(SSM_CHUNKS, 1, CHUNK_S)
    ai = a_im_rep[::SSM_GROUP].reshape(SSM_CHUNKS, 1, CHUNK_S)
    chunked = lambda t: t.reshape(SSM_CHUNKS, SSM_GROUPS // SSM_CHUNKS, SSM_GROUP, SSM_STATE)
    bbr = _block_diag(chunked(bb_re)).astype(BF16)
    bbi = _block_diag(chunked(bb_im)).astype(BF16)
    to_cc = lambda c: _block_diag(chunked(c).transpose(0, 1, 3, 2)).astype(BF16)
    ccr, cci = to_cc(p["c_re"]), to_cc(p["c_im"])

    w_in_r = comm.w_in()
    hb, z = _in_proj(x, g_mix, w_in_r)
    qh, kh, vh, ub, uf, c128 = _attn_prep(z, gq, gk, bf, gg)
    crow = c128[:, :HEADS].T.reshape(HEADS, 1, s)
    (oh, lse), landed = _attn_fwd(qh, kh, vh, crow, comm.gather_first())
    (xr, xi, y), gathered = _ssm_fwd(ub, uf, bbr, bbi, ar, ai, ccr, cci, dsk, comm.gather_second(landed))
    w_glu_b, w_out_b, w_down_b, conv_w_full = comm.weights(gathered)
    (x1, mixb, h2b), passed = _mix_out(y, oh, x, w_glu_b, b_glu, g_att, g_ssm, w_out_b, g_ffn, comm.gather_third(gathered))
    w_up_b = comm.w_up(passed)
    up = _mm(h2b, w_up_b, name="ffn_up", tm=1024, tn=1408, tk=1024)
    act = _conv_act(up, conv_w_full, conv_b)
    dy, dyb, loss_blk = _down_loss(act, w_down_b, x1, tgt)

    d_w_down = _mm(act, dyb, ta=True, name="d_w_down", tm=1408, tn=1024, tk=2048)
    dact = _mm(dyb, w_down_b, tb=True, name="d_act", tm=1024, tn=1408, tk=1024)
    dupb, dcw = _conv_act_bwd(up, dact, conv_w_full, conv_b)
    d_w_up = _mm(h2b, dupb, ta=True, b_parts=2, name="d_w_up", tm=1024, tn=1408, tk=2048)
    dh2 = _mm(dupb, w_up_b, tb=True, a_parts=2, name="d_h2", tm=1024, tn=1024, tk=1408)
    (dx1, dx1b, doh, dys, d_w_glu, d_g_ffn, d_g_att, d_g_ssm, d_b_glu), landed_down = _mix_bwd(
        dy, dh2, x1, g_ffn, w_out_b, y, oh, w_glu_b, b_glu, g_att, g_ssm, comm.swap_down(d_w_down))
    d_w_out = _mm(mixb, dx1b, ta=True, name="d_w_out", tm=1024, tn=1024, tk=2048)
    (du, dbbr, dbbi, dccr, dcci, dar, dai, dd), swapped = _ssm_bwd(dys, uf, ub, xr, xi, bbr, bbi, ar, ai, ccr, cci, dsk,
                                                                comm.swap(landed_down, d_w_up, d_w_glu, d_w_out))
    unchunk = lambda t: t.reshape(_PARAM_SHAPE)
    dbb_re = unchunk(_diag_blocks(dbbr, SSM_GROUP, SSM_STATE))
    dbb_im = unchunk(_diag_blocks(dbbi, SSM_GROUP, SSM_STATE))
    first_row = (jnp.arange(_PARAM_SHAPE[0]) % SSM_GROUP == 0)[:, None]
    da_re = jnp.where(first_row, rep(dar.reshape(SSM_GROUPS, SSM_STATE)), 0.0)
    da_im = jnp.where(first_row, rep(dai.reshape(SSM_GROUPS, SSM_STATE)), 0.0)
    expand_t = (jnp.arange(SSM_GROUPS)[:, None] == (jnp.arange(_PARAM_SHAPE[0]) // SSM_GROUP)[None, :]).astype(BF16)
    d_lr, d_li, d_ls, d_bt_re, d_bt_im = _ssm_params_bwd(lr, li, ls, bt_re, bt_im, da_re, da_im, dbb_re, dbb_im, expand_t)
    from_bt = lambda t: t.reshape(SSM_GROUPS, SSM_GROUP, SSM_STATE).transpose(0, 2, 1)
    from_cc = lambda t: _diag_blocks(t, SSM_STATE, SSM_GROUP).transpose(0, 1, 3, 2).reshape(SSM_GROUPS, SSM_GROUP, SSM_STATE)

    small = {
        "lambda_re": d_lr, "lambda_im": d_li, "log_step": d_ls,
        "b_re": from_bt(d_bt_re), "b_im": from_bt(d_bt_im), "c_re": from_cc(dccr), "c_im": from_cc(dcci),
        "d_skip": dd, "b_glu": d_b_glu, "g_attn_out": d_g_att, "g_ssm_out": d_g_ssm, "g_ffn": d_g_ffn,
        "conv_b": dcw[:, 3],
    }
    d_conv_w = dcw[:, 0:3].transpose(1, 0, 2).reshape(3, 2 * D_FF)
    early_small = _pack_small([small[n] for n in _EARLY_SMALL] + [d_conv_w])

    (dqh, dkh, dvh, dcrow), landed = _attn_bwd(qh, kh, vh, crow, lse, doh, comm.scatter(swapped, early_small))
    early_lands, small_landed = landed[:3], landed[3:]
    dc128 = jnp.pad(dcrow.reshape(HEADS, s).T, ((0, 0), (0, LANES - HEADS)))
    (dzb, d_gq, d_gk, d_bf), small_gathered = _prep_bwd(z, dqh, dkh, dvh, du, dc128, gq, gk, bf, gg,
                                                        comm.small_second(small_landed))
    d_w_in_r = _mm(hb, dzb, ta=True, name="d_w_in", tm=512, tn=Z_COLS, tk=2048)
    dh, swapped_in = _mm(dzb, w_in_r, tb=True, name="d_h", tm=1024, tn=1024, tk=Z_COLS, carry=True,
                         hosted=comm.swap_in(d_w_in_r))
    (dx, d_g_mix), land_in = _in_norm_bwd(x, g_mix, dh, dx1, comm.scatter_in(swapped_in))
    small.update({"g_mix": d_g_mix, "b_f": d_bf[0, :HEADS], "g_q": d_gq.reshape(HEADS, HEAD_DIM).sum(0),
                  "g_k": d_gk.reshape(HEADS, HEAD_DIM).sum(0)})
    big = {"w_in": d_w_in_r, "w_glu": d_w_glu, "w_out": d_w_out, "w_up": d_w_up, "w_down": d_w_down}
    return loss_blk[0, 0], dx, big, small, d_conv_w, (early_lands, land_in, small_gathered, early_small)


def kernel(x, g_mix, w_in, b_f, g_q, g_k, lambda_re, lambda_im, log_step, b_re, b_im, c_re, c_im, d_skip, w_glu, b_glu, g_attn_out, g_ssm_out, w_out, g_ffn, w_up, conv_w, conv_b, w_down, loss_target, m_g_mix, m_w_in, m_b_f, m_g_q, m_g_k, m_lambda_re, m_lambda_im, m_log_step, m_b_re, m_b_im, m_c_re, m_c_im, m_d_skip, m_w_glu, m_b_glu, m_g_attn_out, m_g_ssm_out, m_w_out, m_g_ffn, m_w_up, m_conv_w, m_conv_b, m_w_down, v_g_mix, v_w_in, v_b_f, v_g_q, v_g_k, v_lambda_re, v_lambda_im, v_log_step, v_b_re, v_b_im, v_c_re, v_c_im, v_d_skip, v_w_glu, v_b_glu, v_g_attn_out, v_g_ssm_out, v_w_out, v_g_ffn, v_w_up, v_conv_w, v_conv_b, v_w_down):
    args = dict(locals())
    order = ["g_mix", "w_in", "b_f", "g_q", "g_k", "lambda_re", "lambda_im", "log_step", "b_re", "b_im", "c_re", "c_im",
             "d_skip", "w_glu", "b_glu", "g_attn_out", "g_ssm_out", "w_out", "g_ffn", "w_up", "conv_w", "conv_b", "w_down"]
    comm = _MeshComm(args)
    chip = comm.chip
    loss_part, dx, big, small, d_conv_w, lands = _local_step(x[0], loss_target[0], args, comm)

    g_big = comm.reduce(lands[:2])

    shapes = dict(_SMALL)
    small_names = [n for n, _ in _SMALL]
    device = (2 * chip + comm.core).reshape(1).astype(jnp.int32)
    early = _unpack_small(_sum_devices(_place_block(lands[2][0], lands[3], device)),
                          [shapes[n] for n in _EARLY_SMALL] + [(3, 2 * D_FF)])
    late = _unpack_small(_allreduce_small(_pack_small([small[n] for n in _LATE_SMALL] + [loss_part])),
                         [shapes[n] for n in _LATE_SMALL] + [()])
    loss = late[-1]
    g_conv_w = lax.dynamic_slice_in_dim(early[-1], chip * (2 * D_FF // N_CHIPS), 2 * D_FF // N_CHIPS, axis=1)
    g_small = {**dict(zip(_EARLY_SMALL, early[:-1])), **dict(zip(_LATE_SMALL, late[:-1]))}

    grad, delta, new_m, new_v = {}, {}, {}, {}
    for n in ("w_in", "w_glu", "w_out", "w_up", "w_down"):
        grad[n] = g_big[n]
        delta[n], new_m[n], new_v[n] = _adamw(args[n], g_big[n], args["m_" + n], args["v_" + n], name="adamw_" + n)
    grad["conv_w"] = g_conv_w
    delta["conv_w"], new_m["conv_w"], new_v["conv_w"] = _adamw(conv_w, g_conv_w, m_conv_w, v_conv_w, name="adamw_conv_w")
    stepped = _adamw_small([args[n] for n in small_names], [g_small[n] for n in small_names],
                           [args["m_" + n] for n in small_names], [args["v_" + n] for n in small_names])
    for i, n in enumerate(small_names):
        grad[n] = g_small[n]
        delta[n], new_m[n], new_v[n] = stepped[3 * i:3 * i + 3]

    return (loss, dx[None], *[grad[n] for n in order], *[delta[n] for n in order], *[new_m[n] for n in order],
            *[new_v[n] for n in order])
```

```python
import math

import jax
import jax.numpy as jnp
from jax import lax
from jax.experimental import pallas as pl
from jax.experimental.pallas import tpu as pltpu

F32 = jnp.float32
BF16 = jnp.bfloat16

D_MODEL = 1024
HEADS = 8
HEAD_DIM = 64
ATTN_W = 512
SSM_W = 512
SSM_GROUPS = 32
SSM_GROUP = 16
SSM_STATE = 64
N_STATE = SSM_GROUPS * SSM_STATE
D_FF = 2816
IN_COLS = 2056
Z_COLS = 2176
F_COL0 = 1536
U_COL0 = 1544
EPS = 1e-6
NEG_INF = -1e30
N_CHIPS = 4
LANES = 128
SUBLANES = 8
SSM_CHUNKS = 2
SSM_ROWS = 1024
CHUNK_U = SSM_W // SSM_CHUNKS
CHUNK_S = N_STATE // SSM_CHUNKS
HEADS_PER_STEP = 4
STRIP = 128
N_STRIPS = D_FF // STRIP

ADAM_LR = 0.001
ADAM_B1 = 0.9
ADAM_B2 = 0.999
ADAM_EPS = 1e-08
ADAM_WD = 0.01
ADAM_STEP = 10

VMEM_LIMIT = 56 * 1024 * 1024
MESH = pl.DeviceIdType.MESH


def _pallas(body, **kw):
    return pl.pallas_call(body, **kw)


def _pcall(body, *, name, out_shape, in_specs, out_specs, grid=(), scratch_shapes=(), dims=None):
    params = pltpu.CompilerParams(dimension_semantics=dims, vmem_limit_bytes=VMEM_LIMIT)
    return _pallas(body, name=name, grid=grid, in_specs=in_specs, out_specs=out_specs,
                   out_shape=out_shape, scratch_shapes=scratch_shapes, compiler_params=params)


def _sds(shape, dtype=F32):
    return jax.ShapeDtypeStruct(shape, dtype)


def _dot(a, b):
    return jnp.dot(a, b, preferred_element_type=F32)


def _dot_nt(a, b):
    return lax.dot_general(a, b, (((1,), (1,)), ((), ())), preferred_element_type=F32)


def _dot_tn(a, b):
    return lax.dot_general(a, b, (((0,), (0,)), ((), ())), preferred_element_type=F32)


def _split3(x):
    hi = x.astype(BF16)
    r = x - hi.astype(F32)
    mid = r.astype(BF16)
    lo = (r - mid.astype(F32)).astype(BF16)
    return hi, mid, lo


def _dot_hi_lo(x, m01):
    hi = x.astype(BF16)
    lo = (x - hi.astype(F32)).astype(BF16)
    return _dot(hi, m01) + _dot(lo, m01)


def _dot_exact_l(m01, x):
    hi, mid, lo = _split3(x)
    return _dot(m01, hi) + _dot(m01, mid) + _dot(m01, lo)


def _sigmoid(x):
    return 1.0 / (1.0 + jnp.exp(-x))


def _rms(x, g):
    r = lax.rsqrt(jnp.mean(x * x, axis=-1, keepdims=True) + EPS)
    return x * r * g


def _rms_bwd(x, g, dy):
    r = lax.rsqrt(jnp.mean(x * x, axis=-1, keepdims=True) + EPS)
    w = dy * g
    dx = r * w - x * (r * r * r) * jnp.mean(w * x, axis=-1, keepdims=True)
    dg = jnp.sum(dy * x * r, axis=0, keepdims=True)
    return dx, dg


_GELU_K = math.sqrt(2.0 / math.pi)
_GELU_C = 0.044715


def _gelu(y):
    return y * (0.5 * (1.0 + jnp.tanh(_GELU_K * (y + _GELU_C * (y * y * y)))))


def _gelu_grad(y):
    t = jnp.tanh(_GELU_K * (y + _GELU_C * (y * y * y)))
    return 0.5 * (1.0 + t) + 0.5 * y * (1.0 - t * t) * (_GELU_K * (1.0 + 3.0 * _GELU_C * y * y))


def _tile(n, pref):
    if n <= pref:
        return n
    divs = [t for t in range(LANES, n + 1, LANES) if n % t == 0]
    below = [t for t in divs if t <= pref]
    if below and 2 * below[-1] >= pref:
        return below[-1]
    above = [t for t in divs if t > pref]
    return above[0] if above else n


def _row_tile(s):
    return min(256, s)


def _wide_tile(s):
    return min(512, s)


def _mm(a, b, *, name, tm, tn, tk, ta=False, tb=False, a_parts=1, b_parts=1, carry=False, hosted=None):
    if a_parts > 1:
        m, kk = a.shape[1], a.shape[2] * a_parts
    elif ta:
        kk, m = a.shape
    else:
        m, kk = a.shape
    if b_parts > 1:
        n = b.shape[2] * b_parts
    else:
        n = b.shape[0] if tb else b.shape[1]
    tm, tn, tk = _tile(m, tm), _tile(n // b_parts, tn), _tile(kk // a_parts, tk)
    k_per, n_per = kk // a_parts // tk, n // b_parts // tn

    def body(a_ref, b_ref, o_ref):
        k = pl.program_id(2)
        if ta:
            part = _dot_tn(a_ref[...], b_ref[...])
        elif tb:
            part = _dot_nt(a_ref[...], b_ref[...])
        else:
            part = _dot(a_ref[...], b_ref[...])

        @pl.when(k == 0)
        def _():
            o_ref[...] = part

        @pl.when(k > 0)
        def _():
            o_ref[...] += part

    if a_parts > 1:
        a_spec = pl.BlockSpec((None, tm, tk), lambda i, j, k: (k // k_per, i, k % k_per))
    else:
        a_spec = pl.BlockSpec((tk, tm), lambda i, j, k: (k, i)) if ta else pl.BlockSpec((tm, tk), lambda i, j, k: (i, k))
    if b_parts > 1:
        b_spec = pl.BlockSpec((None, tk, tn), lambda i, j, k: (j // n_per, k, j % n_per))
    else:
        b_spec = pl.BlockSpec((tn, tk), lambda i, j, k: (j, k)) if tb else pl.BlockSpec((tk, tn), lambda i, j, k: (k, j))
    grid = (m // tm, n // tn, kk // tk)
    at = lambda step: (lambda: jnp.logical_and(jnp.logical_and(pl.program_id(0) == step[0], pl.program_id(1) == step[1]),
                                               pl.program_id(2) == step[2]))
    (out,), carried = _host_pcall(body, hosted, at((0, 0, 0)), at(tuple(g - 1 for g in grid)), n_in=2, n_out=1, n_scratch=0,
                                  name=name, grid=grid, in_specs=[a_spec, b_spec],
                                  out_specs=[pl.BlockSpec((tm, tn), lambda i, j, k: (i, j))], out_shape=[_sds((m, n))],
                                  scratch_shapes=[], dims=("parallel", "parallel", "arbitrary"), operands=(a, b))
    return (out, carried) if carry else out


def _in_proj(x, g_mix, w_in_r):
    s = x.shape[0]
    tm = _wide_tile(s)

    def body(x_ref, g_ref, w_ref, h_ref, z_ref):
        h = _rms(x_ref[...], g_ref[...]).astype(BF16)
        h_ref[...] = h
        z_ref[...] = _dot(h, w_ref[...])

    return _pcall(body, name="in_proj", grid=(s // tm,),
                  in_specs=[pl.BlockSpec((tm, D_MODEL), lambda i: (i, 0)), pl.BlockSpec((1, D_MODEL), lambda i: (0, 0)),
                            pl.BlockSpec((D_MODEL, Z_COLS), lambda i: (0, 0))],
                  out_specs=[pl.BlockSpec((tm, D_MODEL), lambda i: (i, 0)), pl.BlockSpec((tm, Z_COLS), lambda i: (i, 0))],
                  out_shape=[_sds((s, D_MODEL), BF16), _sds((s, Z_COLS))], dims=("parallel",))(x, g_mix, w_in_r)


def _split_heads(ref, val):
    for h in range(HEADS):
        ref[h] = val[:, h * HEAD_DIM:(h + 1) * HEAD_DIM].astype(ref.dtype)


def _merge_heads(ref):
    return jnp.concatenate([ref[h].astype(F32) for h in range(HEADS)], axis=-1)


def _forget_logits(z_ref, bf_ref):
    fl = z_ref[:, F_COL0:F_COL0 + LANES] + bf_ref[...]
    return jnp.where(lax.broadcasted_iota(jnp.int32, fl.shape, 1) < HEADS, fl, 0.0)


def _attn_prep(z, gq, gk, bf, gg):
    s = z.shape[0]
    tm = _row_tile(s)

    def body(z_ref, gq_ref, gk_ref, bf_ref, gg_ref, qn_ref, kn_ref, vb_ref, ub_ref, uf_ref, c_ref, carry_ref):
        i = pl.program_id(0)

        @pl.when(i == 0)
        def _():
            carry_ref[...] = jnp.zeros_like(carry_ref)

        gg_m = gg_ref[...]

        def head_norm(t, g):
            ssq = _dot_hi_lo(t * t, gg_m)
            return t * lax.rsqrt(ssq * (1.0 / HEAD_DIM) + EPS) * g

        _split_heads(qn_ref, head_norm(z_ref[:, 0:ATTN_W], gq_ref[...]))
        _split_heads(kn_ref, head_norm(z_ref[:, ATTN_W:2 * ATTN_W], gk_ref[...]))
        _split_heads(vb_ref, z_ref[:, 2 * ATTN_W:3 * ATTN_W])
        u = z_ref[:, U_COL0:U_COL0 + SSM_W]
        uf_ref[...] = u
        ub_ref[...] = u.astype(BF16)
        fl = _forget_logits(z_ref, bf_ref)
        lf = jnp.minimum(fl, 0.0) - jnp.log1p(jnp.exp(-jnp.abs(fl)))
        row = lax.broadcasted_iota(jnp.int32, (tm, tm), 0)
        col = lax.broadcasted_iota(jnp.int32, (tm, tm), 1)
        tri = (row >= col).astype(BF16)
        c = _dot_exact_l(tri, lf) + carry_ref[...]
        c_ref[...] = c
        carry_ref[...] = c[tm - 1:tm, :]

    row_spec = lambda w: pl.BlockSpec((tm, w), lambda i: (i, 0))
    const = lambda shape: pl.BlockSpec(shape, lambda i: (0, 0))
    heads = pl.BlockSpec((HEADS, tm, HEAD_DIM), lambda i: (0, i, 0))
    return _pcall(body, name="attn_prep", grid=(s // tm,),
                  in_specs=[row_spec(Z_COLS), const((1, ATTN_W)), const((1, ATTN_W)), const((1, LANES)), const((ATTN_W, ATTN_W))],
                  out_specs=[heads] * 3 + [row_spec(SSM_W), row_spec(SSM_W), row_spec(LANES)],
                  out_shape=[_sds((HEADS, s, HEAD_DIM), BF16)] * 3 + [_sds((s, SSM_W), BF16), _sds((s, SSM_W)), _sds((s, LANES))],
                  scratch_shapes=[pltpu.VMEM((1, LANES), F32)], dims=("arbitrary",))(z, gq, gk, bf, gg)


def _attn_fwd(qh, kh, vh, crow, hosted=None):
    _, s, _ = qh.shape
    tq = _row_tile(s)
    scale = HEAD_DIM ** -0.5

    hp = HEADS
    nq = s // tq
    fold = lambda t, op: op(t[:, :tq // 2], t[:, tq // 2:])

    def body(q_ref, k_ref, v_ref, c_ref, o_ref, lse_ref, s_s):
        i = pl.program_id(1)

        def first(j, ms, diagonal):
            off = pl.multiple_of(j * tq, tq)
            out = []
            for hh in range(hp):
                sc = _dot_nt(q_ref[hh], k_ref[hh, pl.ds(off, tq), :]) * scale - c_ref[hh, :, pl.ds(off, tq)]
                if diagonal:
                    causal = lax.broadcasted_iota(jnp.int32, (tq, tq), 1) <= lax.broadcasted_iota(jnp.int32, (tq, tq), 0)
                    sc = jnp.where(causal, sc, NEG_INF)
                s_s[hh, j] = sc
                out.append(jnp.maximum(ms[hh], fold(sc, jnp.maximum)))
            return tuple(out)

        ms = lax.fori_loop(0, i, lambda j, c: first(j, c, False), (jnp.full((tq, tq // 2), NEG_INF, F32),) * hp)
        ms = [jnp.max(t, axis=-1, keepdims=True) for t in first(i, ms, True)]

        def second(j, carry):
            rows = pl.ds(pl.multiple_of(j * tq, tq), tq)
            out = []
            for hh in range(hp):
                ls, acc = carry[hh]
                p = jnp.exp(s_s[hh, j] - ms[hh])
                out.append((ls + fold(p, jnp.add), acc + _dot(p.astype(BF16), v_ref[hh, rows, :])))
            return tuple(out)

        zero = (jnp.zeros((tq, tq // 2), F32), jnp.zeros((tq, HEAD_DIM), F32))
        for hh, (ls, acc) in enumerate(lax.fori_loop(0, i + 1, second, (zero,) * hp)):
            l = jnp.sum(ls, axis=-1, keepdims=True)
            o_ref[hh] = acc / l
            lse_ref[hh] = ms[hh] + jnp.log(l)

    blk = pl.BlockSpec((hp, tq, HEAD_DIM), lambda h, i: (h, i, 0))
    full = pl.BlockSpec((hp, s, HEAD_DIM), lambda h, i: (h, 0, 0))
    nh = HEADS // hp
    first = lambda: jnp.logical_and(pl.program_id(0) == 0, pl.program_id(1) == 0)
    last = lambda: jnp.logical_and(pl.program_id(0) == nh - 1, pl.program_id(1) == nq - 1)
    return _host_pcall(body, hosted, first, last, n_in=4, n_out=2, n_scratch=1, name="attn_fwd", grid=(nh, nq),
                       in_specs=[blk, full, full, pl.BlockSpec((hp, 1, s), lambda h, i: (h, 0, 0))],
                       out_specs=[blk, pl.BlockSpec((hp, tq, 1), lambda h, i: (h, i, 0))],
                       out_shape=[_sds((HEADS, s, HEAD_DIM)), _sds((HEADS, s, 1))],
                       scratch_shapes=[pltpu.VMEM((hp, nq, tq, tq), F32)],
                       dims=("parallel", "parallel"), operands=(qh, kh, vh, crow))


def _ssm_param_fn(lr, li, ls, br, bi):
    step = jnp.exp(ls)
    er = jnp.exp(lr * step)
    ab_re = er * jnp.cos(li * step)
    ab_im = er * jnp.sin(li * step)
    num_re = ab_re - 1.0
    num_im = ab_im
    den = lr * lr + li * li
    f_re = (num_re * lr + num_im * li) / den
    f_im = (num_im * lr - num_re * li) / den
    bb_re = f_re * br - f_im * bi
    bb_im = f_re * bi + f_im * br
    return ab_re, ab_im, bb_re, bb_im


_PARAM_SHAPE = (SSM_GROUPS * SSM_GROUP, SSM_STATE)


def _ssm_params(lr, li, ls, br, bi):
    def body(lr_ref, li_ref, ls_ref, br_ref, bi_ref, ar_ref, ai_ref, bbr_ref, bbi_ref):
        ar, ai, bbr, bbi = _ssm_param_fn(lr_ref[...], li_ref[...], ls_ref[...], br_ref[...], bi_ref[...])
        ar_ref[...] = ar
        ai_ref[...] = ai
        bbr_ref[...] = bbr
        bbi_ref[...] = bbi

    spec = pl.BlockSpec(_PARAM_SHAPE, lambda: (0, 0))
    return _pcall(body, name="ssm_params", in_specs=[spec] * 5, out_specs=[spec] * 4,
                  out_shape=[_sds(_PARAM_SHAPE)] * 4)(lr, li, ls, br, bi)


def _ssm_params_bwd(lr, li, ls, br, bi, dar, dai, dbbr, dbbi, expand_t):
    def body(lr_ref, li_ref, ls_ref, br_ref, bi_ref, dar_ref, dai_ref, dbbr_ref, dbbi_ref, et_ref,
             dlr_ref, dli_ref, dls_ref, dbr_ref, dbi_ref):
        _, vjp = jax.vjp(_ssm_param_fn, lr_ref[...], li_ref[...], ls_ref[...], br_ref[...], bi_ref[...])
        dlr, dli, dls, dbr, dbi = vjp((dar_ref[...], dai_ref[...], dbbr_ref[...], dbbi_ref[...]))
        et = et_ref[...]
        dlr_ref[...] = _dot_exact_l(et, dlr)
        dli_ref[...] = _dot_exact_l(et, dli)
        dls_ref[...] = jnp.sum(_dot_exact_l(et, dls), axis=-1, keepdims=True)
        dbr_ref[...] = dbr
        dbi_ref[...] = dbi

    spec = pl.BlockSpec(_PARAM_SHAPE, lambda: (0, 0))
    gspec = pl.BlockSpec((SSM_GROUPS, SSM_STATE), lambda: (0, 0))
    return _pcall(body, name="ssm_params_bwd",
                  in_specs=[spec] * 9 + [pl.BlockSpec((SSM_GROUPS, _PARAM_SHAPE[0]), lambda: (0, 0))],
                  out_specs=[gspec, gspec, pl.BlockSpec((SSM_GROUPS, 1), lambda: (0, 0)), spec, spec],
                  out_shape=[_sds((SSM_GROUPS, SSM_STATE))] * 2 + [_sds((SSM_GROUPS, 1))] + [_sds(_PARAM_SHAPE)] * 2,
                  )(lr, li, ls, br, bi, dar, dai, dbbr, dbbi, expand_t)


def _cmul(ar, ai, br, bi):
    return ar * br - ai * bi, ar * bi + ai * br


def _scan_consts(ar, ai, width, reverse):
    row = lax.broadcasted_iota(jnp.int32, (SUBLANES, width), 0)
    pw = [(ar, ai)]
    for _ in range(SUBLANES - 1):
        pw.append(_cmul(pw[-1][0], pw[-1][1], ar, ai))
    steps = []
    for d in (1, 2, 4):
        keep = (row < SUBLANES - d) if reverse else (row >= d)
        steps.append((d, jnp.where(keep, pw[d - 1][0], 0.0), jnp.where(keep, pw[d - 1][1], 0.0)))
    pr = jnp.zeros((SUBLANES, width), F32)
    pi = jnp.zeros((SUBLANES, width), F32)
    for r in range(SUBLANES):
        e = (SUBLANES - r) if reverse else (r + 1)
        pr = jnp.where(row == r, pw[e - 1][0], pr)
        pi = jnp.where(row == r, pw[e - 1][1], pi)
    return steps, pr, pi


def _scan_tile(xr, xi, cr, ci, consts, reverse):
    steps, pr, pi = consts
    for d, mr, mi in steps:
        sh = (SUBLANES - d) if reverse else d
        sr = pltpu.roll(xr, sh, 0)
        si = pltpu.roll(xi, sh, 0)
        xr, xi = xr + mr * sr - mi * si, xi + mr * si + mi * sr
    return xr + pr * cr - pi * ci, xi + pr * ci + pi * cr


def _ssm_fwd(ub, uf, bbr, bbi, ar, ai, ccr, cci, dsk, hosted=None):
    s = ub.shape[0]
    tm = min(SSM_ROWS, s)
    nt = tm // SUBLANES

    def body(ub_ref, u_ref, bbr_ref, bbi_ref, ar_ref, ai_ref, ccr_ref, cci_ref, dsk_ref,
             xr_ref, xi_ref, y_ref, cr_s, ci_s):
        i = pl.program_id(1)

        @pl.when(i == 0)
        def _():
            cr_s[...] = jnp.zeros_like(cr_s)
            ci_s[...] = jnp.zeros_like(ci_s)

        u_b = ub_ref[...]
        xr_ref[...] = _dot(u_b, bbr_ref[0])
        xi_ref[...] = _dot(u_b, bbi_ref[0])
        consts = _scan_consts(ar_ref[0], ai_ref[0], CHUNK_S, False)

        def tile(k, carry):
            cr, ci = carry
            sl = pl.ds(pl.multiple_of(k * SUBLANES, SUBLANES), SUBLANES)
            xr, xi = _scan_tile(xr_ref[sl, :], xi_ref[sl, :], cr, ci, consts, False)
            xr_ref[sl, :] = xr
            xi_ref[sl, :] = xi
            return xr[SUBLANES - 1:SUBLANES, :], xi[SUBLANES - 1:SUBLANES, :]

        cr, ci = lax.fori_loop(0, nt, tile, (cr_s[...], ci_s[...]))
        cr_s[...] = cr
        ci_s[...] = ci
        y_ref[...] = (_dot(xr_ref[...].astype(BF16), ccr_ref[0]) - _dot(xi_ref[...].astype(BF16), cci_ref[0])
                      + dsk_ref[...] * u_ref[...])

    wspec = lambda a, b: pl.BlockSpec((1, a, b), lambda j, i: (j, 0, 0))
    nb = s // tm
    first = lambda: jnp.logical_and(pl.program_id(0) == 0, pl.program_id(1) == 0)
    last = lambda: jnp.logical_and(pl.program_id(0) == SSM_CHUNKS - 1, pl.program_id(1) == nb - 1)
    return _host_pcall(
        body, hosted, first, last, n_in=9, n_out=3, n_scratch=2, name="ssm_fwd", grid=(SSM_CHUNKS, nb),
        in_specs=[pl.BlockSpec((tm, CHUNK_U), lambda j, i: (i, j)),
                  pl.BlockSpec((tm, CHUNK_U), lambda j, i: (i, j)),
                  wspec(CHUNK_U, CHUNK_S), wspec(CHUNK_U, CHUNK_S), wspec(1, CHUNK_S), wspec(1, CHUNK_S),
                  wspec(CHUNK_S, CHUNK_U), wspec(CHUNK_S, CHUNK_U),
                  pl.BlockSpec((1, CHUNK_U), lambda j, i: (0, j))],
        out_specs=[pl.BlockSpec((tm, CHUNK_S), lambda j, i: (i, j)), pl.BlockSpec((tm, CHUNK_S), lambda j, i: (i, j)),
                   pl.BlockSpec((tm, CHUNK_U), lambda j, i: (i, j))],
        out_shape=[_sds((s, N_STATE)), _sds((s, N_STATE)), _sds((s, SSM_W))],
        scratch_shapes=[pltpu.VMEM((1, CHUNK_S), F32)] * 2,
        dims=("parallel", "arbitrary"), operands=(ub, uf, bbr, bbi, ar, ai, ccr, cci, dsk))


def _ssm_glu(y, w_glu, b_glu):
    ge = _gelu(y)
    sg = _sigmoid(_dot(ge.astype(BF16), w_glu) + b_glu)
    return ge, sg


def _mix_out(y, att, x, w_glu, b_glu, g_att, g_ssm, w_out, g_ffn, hosted=None):
    s = x.shape[0]
    tm = _wide_tile(s)

    def body(y_ref, att_ref, x_ref, wg_ref, bg_ref, ga_ref, gs_ref, wo_ref, gf_ref, x1_ref, mix_ref, h2_ref):
        ge, sg = _ssm_glu(y_ref[...], wg_ref[...], bg_ref[...])
        ms = _rms(ge * sg, gs_ref[...]).astype(BF16)
        ma = _rms(_merge_heads(att_ref), ga_ref[...]).astype(BF16)
        mix_ref[:, 0:ATTN_W] = ma
        mix_ref[:, ATTN_W:D_MODEL] = ms
        x1 = x_ref[...] + (_dot(ma, wo_ref[0:ATTN_W, :]) + _dot(ms, wo_ref[ATTN_W:D_MODEL, :]))
        x1_ref[...] = x1
        h2_ref[...] = _rms(x1, gf_ref[...]).astype(BF16)

    row = lambda w: pl.BlockSpec((tm, w), lambda i: (i, 0))
    const = lambda a, b: pl.BlockSpec((a, b), lambda i: (0, 0))
    nb = s // tm
    return _host_pcall(body, hosted, lambda: pl.program_id(0) == 0, lambda: pl.program_id(0) == nb - 1,
                       n_in=9, n_out=3, n_scratch=0, name="mix_out", grid=(nb,),
                       in_specs=[row(SSM_W), pl.BlockSpec((HEADS, tm, HEAD_DIM), lambda i: (0, i, 0)), row(D_MODEL),
                                 const(SSM_W, SSM_W), const(1, SSM_W),
                                 const(1, ATTN_W), const(1, SSM_W), const(D_MODEL, D_MODEL), const(1, D_MODEL)],
                       out_specs=[row(D_MODEL)] * 3,
                       out_shape=[_sds((s, D_MODEL)), _sds((s, D_MODEL), BF16), _sds((s, D_MODEL), BF16)],
                       scratch_shapes=[], dims=("parallel",), operands=(y, att, x, w_glu, b_glu, g_att, g_ssm, w_out, g_ffn))


CONV_CHUNK = 64


def _conv_rows(pad_ref, w, b, r0, n):
    y = b + pad_ref[pl.ds(r0 + SUBLANES - 2, n), :] * w[0:1, :]
    y = y + pad_ref[pl.ds(r0 + SUBLANES - 1, n), :] * w[1:2, :]
    return y + pad_ref[pl.ds(r0 + SUBLANES, n), :] * w[2:3, :]


def _fill_front_pad(pad_ref, strip_ref, s):
    pad_ref[0:SUBLANES, :] = jnp.zeros((SUBLANES, STRIP), F32)
    for r0 in range(0, s, CONV_CHUNK):
        pad_ref[pl.ds(SUBLANES + r0, CONV_CHUNK), :] = strip_ref[pl.ds(r0, CONV_CHUNK), :]


def _conv_act(up, conv_w, conv_b):
    s = up.shape[0]

    def body(ug_ref, uv_ref, wg_ref, wv_ref, bg_ref, bv_ref, act_ref, pg_ref, pv_ref):
        _fill_front_pad(pg_ref, ug_ref, s)
        _fill_front_pad(pv_ref, uv_ref, s)
        wg, wv, bg, bv = wg_ref[...], wv_ref[...], bg_ref[...], bv_ref[...]
        for r0 in range(0, s, CONV_CHUNK):
            hg = _conv_rows(pg_ref, wg, bg, r0, CONV_CHUNK)
            hv = _conv_rows(pv_ref, wv, bv, r0, CONV_CHUNK)
            act_ref[pl.ds(r0, CONV_CHUNK), :] = (hg * _sigmoid(hg) * hv).astype(BF16)

    strip = lambda off: pl.BlockSpec((s, STRIP), lambda j: (0, j + off))
    wsp = lambda off: pl.BlockSpec((3, STRIP), lambda j: (0, j + off))
    bsp = lambda off: pl.BlockSpec((1, STRIP), lambda j: (0, j + off))
    return _pcall(body, name="conv_act", grid=(N_STRIPS,),
                  in_specs=[strip(0), strip(N_STRIPS), wsp(0), wsp(N_STRIPS), bsp(0), bsp(N_STRIPS)],
                  out_specs=pl.BlockSpec((s, STRIP), lambda j: (0, j)), out_shape=_sds((s, D_FF), BF16),
                  scratch_shapes=[pltpu.VMEM((s + SUBLANES, STRIP), F32)] * 2,
                  dims=("parallel",))(up, up, conv_w, conv_w, conv_b, conv_b)


def _down_loss(act, w_down, x1, tgt):
    s = x1.shape[0]
    tm = _wide_tile(s)

    def body(a_ref, w_ref, x1_ref, t_ref, dy_ref, dyb_ref, loss_ref):
        i = pl.program_id(0)

        @pl.when(i == 0)
        def _():
            loss_ref[...] = jnp.zeros_like(loss_ref)

        diff = x1_ref[...] + _dot(a_ref[...], w_ref[...]) - t_ref[...]
        dy = diff * (1.0 / D_MODEL)
        dy_ref[...] = dy
        dyb_ref[...] = dy.astype(BF16)
        loss_ref[...] += 0.5 * jnp.sum(diff * dy)

    row = lambda w: pl.BlockSpec((tm, w), lambda i: (i, 0))
    return _pcall(body, name="down_loss", grid=(s // tm,),
                  in_specs=[row(D_FF), pl.BlockSpec((D_FF, D_MODEL), lambda i: (0, 0)), row(D_MODEL), row(D_MODEL)],
                  out_specs=[row(D_MODEL), row(D_MODEL), pl.BlockSpec((SUBLANES, LANES), lambda i: (0, 0))],
                  out_shape=[_sds((s, D_MODEL)), _sds((s, D_MODEL), BF16), _sds((SUBLANES, LANES))],
                  dims=("arbitrary",))(act, w_down, x1, tgt)


def _conv_act_bwd(up, dact, conv_w, conv_b):
    s = up.shape[0]
    ch = CONV_CHUNK

    def body(ug_ref, uv_ref, da_ref, wg_ref, wv_ref, bg_ref, bv_ref, dup_ref, dcw_ref, pg_ref, pv_ref, dg_ref, dv_ref):
        _fill_front_pad(pg_ref, ug_ref, s)
        _fill_front_pad(pv_ref, uv_ref, s)
        zero = jnp.zeros((SUBLANES, STRIP), F32)
        dg_ref[pl.ds(s, SUBLANES), :] = zero
        dv_ref[pl.ds(s, SUBLANES), :] = zero
        wg, wv, bg, bv = wg_ref[...], wv_ref[...], bg_ref[...], bv_ref[...]
        tile_sum = lambda t: jnp.sum(t.reshape(ch // SUBLANES, SUBLANES, STRIP), axis=0)
        accs = [[zero] * 4, [zero] * 4]
        for r0 in range(0, s, ch):
            hg = _conv_rows(pg_ref, wg, bg, r0, ch)
            hv = _conv_rows(pv_ref, wv, bv, r0, ch)
            sg = _sigmoid(hg)
            da = da_ref[pl.ds(r0, ch), :]
            dhs = (da * hv * (sg * (1.0 + hg * (1.0 - sg))), da * (hg * sg))
            for half, (dh, d_ref, p_ref) in enumerate(zip(dhs, (dg_ref, dv_ref), (pg_ref, pv_ref))):
                d_ref[pl.ds(r0, ch), :] = dh
                for k in range(3):
                    accs[half][k] = accs[half][k] + tile_sum(dh * p_ref[pl.ds(r0 + SUBLANES - 2 + k, ch), :])
                accs[half][3] = accs[half][3] + tile_sum(dh)
        for half, (d_ref, w) in enumerate(((dg_ref, wg), (dv_ref, wv))):
            for r0 in range(0, s, ch):
                dup = (d_ref[pl.ds(r0, ch), :] * w[2:3, :] + d_ref[pl.ds(r0 + 1, ch), :] * w[1:2, :]
                       + d_ref[pl.ds(r0 + 2, ch), :] * w[0:1, :])
                dup_ref[half, pl.ds(r0, ch), :] = dup.astype(BF16)
            rid = lax.broadcasted_iota(jnp.int32, (SUBLANES, STRIP), 0)
            out = zero
            for k in range(4):
                out = jnp.where(rid == k, jnp.sum(accs[half][k], axis=0, keepdims=True), out)
            dcw_ref[half] = out

    strip = lambda off: pl.BlockSpec((s, STRIP), lambda j: (0, j + off))
    wsp = lambda off: pl.BlockSpec((3, STRIP), lambda j: (0, j + off))
    bsp = lambda off: pl.BlockSpec((1, STRIP), lambda j: (0, j + off))
    return _pcall(body, name="conv_act_bwd", grid=(N_STRIPS,),
                  in_specs=[strip(0), strip(N_STRIPS), strip(0), wsp(0), wsp(N_STRIPS), bsp(0), bsp(N_STRIPS)],
                  out_specs=[pl.BlockSpec((2, s, STRIP), lambda j: (0, 0, j)), pl.BlockSpec((2, SUBLANES, STRIP), lambda j: (0, 0, j))],
                  out_shape=[_sds((2, s, D_FF), BF16), _sds((2, SUBLANES, D_FF))],
                  scratch_shapes=[pltpu.VMEM((s + SUBLANES, STRIP), F32)] * 4,
                  dims=("parallel",))(up, up, dact, conv_w, conv_w, conv_b, conv_b)


def _mix_bwd(dy, dh2, x1, g_ffn, w_out, y, att, w_glu, b_glu, g_att, g_ssm, hosted=None):
    s = dy.shape[0]
    tm = _wide_tile(s)

    def body(dy_ref, dh2_ref, x1_ref, gf_ref, wo_ref, y_ref, att_ref, wg_ref, bg_ref, ga_ref, gs_ref,
             dx1_ref, dx1b_ref, datt_ref, dys_ref, dwg_ref, dgf_ref, dga_ref, dgs_ref, dbg_ref):
        i = pl.program_id(0)

        @pl.when(i == 0)
        def _():
            for r in (dwg_ref, dgf_ref, dga_ref, dgs_ref, dbg_ref):
                r[...] = jnp.zeros_like(r)

        dxn, dgf = _rms_bwd(x1_ref[...], gf_ref[...], dh2_ref[...])
        dx1 = dy_ref[...] + dxn
        dx1_ref[...] = dx1
        dx1b = dx1.astype(BF16)
        dx1b_ref[...] = dx1b
        dgf_ref[...] += dgf
        dma = _dot_nt(dx1b, wo_ref[0:ATTN_W, :])
        dms = _dot_nt(dx1b, wo_ref[ATTN_W:D_MODEL, :])
        datt, dga = _rms_bwd(_merge_heads(att_ref), ga_ref[...], dma)
        _split_heads(datt_ref, datt)
        dga_ref[...] += dga
        yv = y_ref[...]
        ge, sg = _ssm_glu(yv, wg_ref[...], bg_ref[...])
        dssm, dgs = _rms_bwd(ge * sg, gs_ref[...], dms)
        dgs_ref[...] += dgs
        dgl = dssm * ge * sg * (1.0 - sg)
        dglb = dgl.astype(BF16)
        dge = dssm * sg + _dot_nt(dglb, wg_ref[...])
        dbg_ref[...] += jnp.sum(dgl, axis=0, keepdims=True)
        dwg_ref[...] += _dot_tn(ge.astype(BF16), dglb)
        dys_ref[...] = dge * _gelu_grad(yv)

    row = lambda w: pl.BlockSpec((tm, w), lambda i: (i, 0))
    const = lambda a, b: pl.BlockSpec((a, b), lambda i: (0, 0))
    heads = pl.BlockSpec((HEADS, tm, HEAD_DIM), lambda i: (0, i, 0))
    nb = s // tm
    return _host_pcall(
        body, hosted, lambda: pl.program_id(0) == 0, lambda: pl.program_id(0) == nb - 1, n_in=11, n_out=9, n_scratch=0,
        name="mix_bwd", grid=(nb,),
        in_specs=[row(D_MODEL), row(D_MODEL), row(D_MODEL), const(1, D_MODEL), const(D_MODEL, D_MODEL), row(SSM_W),
                  heads, const(SSM_W, SSM_W), const(1, SSM_W), const(1, ATTN_W), const(1, SSM_W)],
        out_specs=[row(D_MODEL), row(D_MODEL), heads, row(SSM_W), const(SSM_W, SSM_W), const(1, D_MODEL),
                   const(1, ATTN_W), const(1, SSM_W), const(1, SSM_W)],
        out_shape=[_sds((s, D_MODEL)), _sds((s, D_MODEL), BF16), _sds((HEADS, s, HEAD_DIM)), _sds((s, SSM_W)),
                   _sds((SSM_W, SSM_W)), _sds((1, D_MODEL)), _sds((1, ATTN_W)), _sds((1, SSM_W)), _sds((1, SSM_W))],
        scratch_shapes=[], dims=("arbitrary",), operands=(dy, dh2, x1, g_ffn, w_out, y, att, w_glu, b_glu, g_att, g_ssm))


def _ssm_bwd(dys, uf, ub, xr, xi, bbr, bbi, ar, ai, ccr, cci, dsk, hosted=None):
    s = dys.shape[0]
    tm = min(SSM_ROWS, s)
    nb = s // tm
    nt = tm // SUBLANES

    def body(dy_ref, u_ref, ub_ref, xr_ref, xi_ref, xrp_ref, xip_ref, bbr_ref, bbi_ref, ar_ref, ai_ref, ccr_ref,
             cci_ref, dsk_ref, du_ref, dbbr_ref, dbbi_ref, dccr_ref, dcci_ref, dar_ref, dai_ref, dd_ref,
             gr_s, gi_s, cr_s, ci_s, accr_s, acci_s):
        i = pl.program_id(1)
        first_block = i == nb - 1

        @pl.when(i == 0)
        def _():
            for r in (cr_s, ci_s, accr_s, acci_s, dbbr_ref, dbbi_ref, dccr_ref, dcci_ref, dd_ref):
                r[...] = jnp.zeros_like(r)

        dy = dy_ref[...]
        dyb = dy.astype(BF16)
        gr_s[...] = _dot_nt(dyb, ccr_ref[0])
        gi_s[...] = -_dot_nt(dyb, cci_ref[0])
        consts = _scan_consts(ar_ref[0], -ai_ref[0], CHUNK_S, True)
        row = lax.broadcasted_iota(jnp.int32, (SUBLANES, CHUNK_S), 0)

        def tile(kk, carry):
            cr, ci, accr, acci = carry
            k = nt - 1 - kk
            sl = pl.ds(pl.multiple_of(k * SUBLANES, SUBLANES), SUBLANES)
            gr, gi = _scan_tile(gr_s[sl, :], gi_s[sl, :], cr, ci, consts, True)
            gr_s[sl, :] = gr
            gi_s[sl, :] = gi
            slp = pl.ds(pl.multiple_of(jnp.maximum(k - 1, 0) * SUBLANES, SUBLANES), SUBLANES)
            inner = k > 0
            pr_t = jnp.where(inner, xr_ref[slp, :], xrp_ref[...])
            pi_t = jnp.where(inner, xi_ref[slp, :], xip_ref[...])
            live = jnp.logical_or(inner, jnp.logical_not(first_block))
            top_r = jnp.where(live, pltpu.roll(pr_t, 1, 0), 0.0)
            top_i = jnp.where(live, pltpu.roll(pi_t, 1, 0), 0.0)
            xpr = jnp.where(row == 0, top_r, pltpu.roll(xr_ref[sl, :], 1, 0))
            xpi = jnp.where(row == 0, top_i, pltpu.roll(xi_ref[sl, :], 1, 0))
            accr = accr + gr * xpr + gi * xpi
            acci = acci + gi * xpr - gr * xpi
            return gr[0:1, :], gi[0:1, :], accr, acci

        zeros = jnp.zeros((SUBLANES, CHUNK_S), F32)
        cr, ci, accr, acci = lax.fori_loop(0, nt, tile, (cr_s[...], ci_s[...], zeros, zeros))
        cr_s[...] = cr
        ci_s[...] = ci
        accr_s[...] += accr
        acci_s[...] += acci
        grb = gr_s[...].astype(BF16)
        gib = gi_s[...].astype(BF16)
        u_b = ub_ref[...]
        du_ref[...] = _dot_nt(grb, bbr_ref[0]) + _dot_nt(gib, bbi_ref[0]) + dsk_ref[...] * dy
        dbbr_ref[0] += _dot_tn(u_b, grb)
        dbbi_ref[0] += _dot_tn(u_b, gib)
        dccr_ref[0] += _dot_tn(xr_ref[...].astype(BF16), dyb)
        dcci_ref[0] -= _dot_tn(xi_ref[...].astype(BF16), dyb)
        dd_ref[...] += jnp.sum(dy * u_ref[...], axis=0, keepdims=True)

        @pl.when(i == nb - 1)
        def _():
            dar_ref[0] = jnp.sum(accr_s[...], axis=0, keepdims=True)
            dai_ref[0] = jnp.sum(acci_s[...], axis=0, keepdims=True)

    tiles_per_block = tm // SUBLANES
    rb = lambda i: nb - 1 - i
    wspec = lambda a, b: pl.BlockSpec((1, a, b), lambda j, i: (j, 0, 0))
    xblk = pl.BlockSpec((tm, CHUNK_S), lambda j, i: (rb(i), j))
    xprev = pl.BlockSpec((SUBLANES, CHUNK_S), lambda j, i: (jnp.maximum(rb(i) * tiles_per_block - 1, 0), j))
    ublk = pl.BlockSpec((tm, CHUNK_U), lambda j, i: (rb(i), j))
    first = lambda: jnp.logical_and(pl.program_id(0) == 0, pl.program_id(1) == 0)
    last = lambda: jnp.logical_and(pl.program_id(0) == SSM_CHUNKS - 1, pl.program_id(1) == nb - 1)
    return _host_pcall(
        body, hosted, first, last, n_in=14, n_out=8, n_scratch=6, name="ssm_bwd", grid=(SSM_CHUNKS, nb),
        in_specs=[ublk, ublk, ublk, xblk, xblk, xprev, xprev,
                  wspec(CHUNK_U, CHUNK_S), wspec(CHUNK_U, CHUNK_S), wspec(1, CHUNK_S), wspec(1, CHUNK_S),
                  wspec(CHUNK_S, CHUNK_U), wspec(CHUNK_S, CHUNK_U), pl.BlockSpec((1, CHUNK_U), lambda j, i: (0, j))],
        out_specs=[ublk, wspec(CHUNK_U, CHUNK_S), wspec(CHUNK_U, CHUNK_S), wspec(CHUNK_S, CHUNK_U),
                   wspec(CHUNK_S, CHUNK_U), wspec(1, CHUNK_S), wspec(1, CHUNK_S),
                   pl.BlockSpec((1, CHUNK_U), lambda j, i: (0, j))],
        out_shape=[_sds((s, SSM_W)), _sds((SSM_CHUNKS, CHUNK_U, CHUNK_S)), _sds((SSM_CHUNKS, CHUNK_U, CHUNK_S)),
                   _sds((SSM_CHUNKS, CHUNK_S, CHUNK_U)), _sds((SSM_CHUNKS, CHUNK_S, CHUNK_U)),
                   _sds((SSM_CHUNKS, 1, CHUNK_S)), _sds((SSM_CHUNKS, 1, CHUNK_S)), _sds((1, SSM_W))],
        scratch_shapes=[pltpu.VMEM((tm, CHUNK_S), F32)] * 2 + [pltpu.VMEM((1, CHUNK_S), F32)] * 2
                       + [pltpu.VMEM((SUBLANES, CHUNK_S), F32)] * 2,
        dims=("parallel", "arbitrary"), operands=(dys, uf, ub, xr, xi, xr, xi, bbr, bbi, ar, ai, ccr, cci, dsk))


def _attn_probs(q, ks, cs, lse, scale, diagonal):
    p = jnp.exp(_dot_nt(q, ks) * scale - cs - lse)
    if diagonal:
        tq, tk = p.shape
        causal = lax.broadcasted_iota(jnp.int32, (tq, tk), 1) <= lax.broadcasted_iota(jnp.int32, (tq, tk), 0)
        p = jnp.where(causal, p, 0.0)
    return p


def _attn_bwd(qh, kh, vh, crow, lse, doh, hosted=None):
    _, s, _ = qh.shape
    tq = _row_tile(s)
    nq = s // tq
    scale = HEAD_DIM ** -0.5
    hp = HEADS_PER_STEP

    def body(q_ref, k_ref, v_ref, c_ref, lse_ref, do_ref, dq_ref, dk_ref, dv_ref, dc_ref, p_s, dp_s):
        i = pl.program_id(1)

        @pl.when(i == 0)
        def _():
            for r in (dk_ref, dv_ref, dc_ref):
                r[...] = jnp.zeros_like(r)

        dobs = [do_ref[hh].astype(BF16) for hh in range(hp)]

        def first(j, dls, diagonal):
            off = pl.multiple_of(j * tq, tq)
            out = []
            for hh in range(hp):
                p = _attn_probs(q_ref[hh], k_ref[hh, pl.ds(off, tq), :], c_ref[hh, :, pl.ds(off, tq)], lse_ref[hh],
                                scale, diagonal)
                dp = _dot_nt(dobs[hh], v_ref[hh, pl.ds(off, tq), :])
                p_s[hh, j] = p
                dp_s[hh, j] = dp
                out.append(dls[hh] + jnp.sum(p * dp, axis=-1, keepdims=True))
            return tuple(out)

        zero_col = jnp.zeros((tq, 1), F32)
        dls = lax.fori_loop(0, i, lambda j, c: first(j, c, False), (zero_col,) * hp)
        dls = first(i, dls, True)

        def second(j, dqs):
            rows = pl.ds(pl.multiple_of(j * tq, tq), tq)
            out = []
            for hh in range(hp):
                p = p_s[hh, j]
                ds = p * (dp_s[hh, j] - dls[hh])
                dsb = ds.astype(BF16)
                dv_ref[hh, rows, :] += _dot_tn(p.astype(BF16), dobs[hh])
                dk_ref[hh, rows, :] += _dot_tn(dsb, q_ref[hh]) * scale
                dc_ref[hh, :, rows] -= jnp.sum(ds, axis=0, keepdims=True)
                out.append(dqs[hh] + _dot(dsb, k_ref[hh, rows, :]))
            return tuple(out)

        dqs = lax.fori_loop(0, i + 1, second, (jnp.zeros((tq, HEAD_DIM), F32),) * hp)
        for hh in range(hp):
            dq_ref[hh] = dqs[hh] * scale

    blk = pl.BlockSpec((hp, tq, HEAD_DIM), lambda h, i: (h, i, 0))
    full = pl.BlockSpec((hp, s, HEAD_DIM), lambda h, i: (h, 0, 0))
    crow_spec = pl.BlockSpec((hp, 1, s), lambda h, i: (h, 0, 0))
    nh = HEADS // hp
    first = lambda: jnp.logical_and(pl.program_id(0) == 0, pl.program_id(1) == 0)
    last = lambda: jnp.logical_and(pl.program_id(0) == nh - 1, pl.program_id(1) == nq - 1)
    return _host_pcall(body, hosted, first, last, n_in=6, n_out=4, n_scratch=2, name="attn_bwd", grid=(nh, nq),
                       in_specs=[blk, full, full, crow_spec, pl.BlockSpec((hp, tq, 1), lambda h, i: (h, i, 0)), blk],
                       out_specs=[blk, full, full, crow_spec],
                       out_shape=[_sds((HEADS, s, HEAD_DIM))] * 3 + [_sds((HEADS, 1, s))],
                       scratch_shapes=[pltpu.VMEM((hp, nq, tq, tq), F32)] * 2,
                       dims=("parallel", "arbitrary"), operands=(qh, kh, vh, crow, lse, doh))


def _prep_bwd(z, dqn, dkn, dv, du, dc, gq, gk, bf, gg, hosted=None):
    s = z.shape[0]
    tm = _row_tile(s)
    nb = s // tm

    def body(z_ref, dqn_ref, dkn_ref, dv_ref, du_ref, dc_ref, gq_ref, gk_ref, bf_ref, gg_ref,
             dz_ref, dgq_ref, dgk_ref, dbf_ref, carry_ref):
        i = pl.program_id(0)

        @pl.when(i == 0)
        def _():
            for r in (dgq_ref, dgk_ref, dbf_ref, carry_ref):
                r[...] = jnp.zeros_like(r)

        gg_m = gg_ref[...]

        def head_norm_bwd(t, g, dn):
            r = lax.rsqrt(_dot_hi_lo(t * t, gg_m) * (1.0 / HEAD_DIM) + EPS)
            w = dn * g
            mean_wt = _dot_hi_lo(w * t, gg_m) * (1.0 / HEAD_DIM)
            return r * w - t * (r * r * r) * mean_wt, jnp.sum(dn * t * r, axis=0, keepdims=True)

        dq, dgq = head_norm_bwd(z_ref[:, 0:ATTN_W], gq_ref[...], _merge_heads(dqn_ref))
        dk, dgk = head_norm_bwd(z_ref[:, ATTN_W:2 * ATTN_W], gk_ref[...], _merge_heads(dkn_ref))
        dgq_ref[...] += dgq
        dgk_ref[...] += dgk
        row = lax.broadcasted_iota(jnp.int32, (tm, tm), 0)
        col = lax.broadcasted_iota(jnp.int32, (tm, tm), 1)
        triu = (col >= row).astype(BF16)
        dlf = _dot_exact_l(triu, dc_ref[...]) + carry_ref[...]
        carry_ref[...] = dlf[0:1, :]
        df = dlf * _sigmoid(-_forget_logits(z_ref, bf_ref))
        dbf_ref[...] += jnp.sum(df, axis=0, keepdims=True)
        dz_ref[:, 0:ATTN_W] = dq.astype(BF16)
        dz_ref[:, ATTN_W:2 * ATTN_W] = dk.astype(BF16)
        dz_ref[:, 2 * ATTN_W:3 * ATTN_W] = _merge_heads(dv_ref).astype(BF16)
        tail = jnp.concatenate([df[:, :HEADS], du_ref[...], jnp.zeros((tm, Z_COLS - IN_COLS), F32)], axis=-1)
        dz_ref[:, F_COL0:Z_COLS] = tail.astype(BF16)

    row_spec = lambda w: pl.BlockSpec((tm, w), lambda i: (nb - 1 - i, 0))
    const = lambda shape: pl.BlockSpec(shape, lambda i: (0, 0))
    return _host_pcall(
        body, hosted, lambda: pl.program_id(0) == 0, lambda: pl.program_id(0) == nb - 1, n_in=10, n_out=4, n_scratch=1,
        name="prep_bwd", grid=(nb,),
        in_specs=[row_spec(Z_COLS)] + [pl.BlockSpec((HEADS, tm, HEAD_DIM), lambda i: (0, nb - 1 - i, 0))] * 3
                 + [row_spec(ATTN_W), row_spec(LANES), const((1, ATTN_W)),
                    const((1, ATTN_W)), const((1, LANES)), const((ATTN_W, ATTN_W))],
        out_specs=[row_spec(Z_COLS), const((1, ATTN_W)), const((1, ATTN_W)), const((1, LANES))],
        out_shape=[_sds((s, Z_COLS), BF16), _sds((1, ATTN_W)), _sds((1, ATTN_W)), _sds((1, LANES))],
        scratch_shapes=[pltpu.VMEM((1, LANES), F32)], dims=("arbitrary",),
        operands=(z, dqn, dkn, dv, du, dc, gq, gk, bf, gg))


def _in_norm_bwd(x, g_mix, dh, dx1, hosted=None):
    s = x.shape[0]
    tm = _wide_tile(s)

    def body(x_ref, g_ref, dh_ref, dx1_ref, dx_ref, dg_ref):
        i = pl.program_id(0)

        @pl.when(i == 0)
        def _():
            dg_ref[...] = jnp.zeros_like(dg_ref)

        dxn, dg = _rms_bwd(x_ref[...], g_ref[...], dh_ref[...])
        dx_ref[...] = dx1_ref[...] + dxn
        dg_ref[...] += dg

    row = pl.BlockSpec((tm, D_MODEL), lambda i: (i, 0))
    vec = pl.BlockSpec((1, D_MODEL), lambda i: (0, 0))
    nb = s // tm
    return _host_pcall(body, hosted, lambda: pl.program_id(0) == 0, lambda: pl.program_id(0) == nb - 1,
                       n_in=4, n_out=2, n_scratch=0, name="in_norm_bwd", grid=(nb,), in_specs=[row, vec, row, row],
                       out_specs=[row, vec], out_shape=[_sds((s, D_MODEL)), _sds((1, D_MODEL))], scratch_shapes=[],
                       dims=("arbitrary",), operands=(x, g_mix, dh, dx1))


def _adamw_refs(w_ref, g_ref, m_ref, v_ref, d_ref, mo_ref, vo_ref):
    gv = g_ref[...]
    mn = ADAM_B1 * m_ref[...] + (1.0 - ADAM_B1) * gv
    vn = ADAM_B2 * v_ref[...] + (1.0 - ADAM_B2) * (gv * gv)
    m_hat = mn / (1.0 - ADAM_B1 ** ADAM_STEP)
    v_hat = vn / (1.0 - ADAM_B2 ** ADAM_STEP)
    d_ref[...] = -ADAM_LR * (m_hat / (jnp.sqrt(v_hat) + ADAM_EPS) + ADAM_WD * w_ref[...])
    mo_ref[...] = mn
    vo_ref[...] = vn


def _adamw_small(ws, gs, ms, vs):
    n = len(ws)

    def body(*refs):
        ins, outs = refs[:4 * n], refs[4 * n:]
        for i in range(n):
            _adamw_refs(ins[i], ins[n + i], ins[2 * n + i], ins[3 * n + i], *outs[3 * i:3 * i + 3])

    vm = pl.BlockSpec(memory_space=pltpu.VMEM)
    out_shape = [_sds(w.shape) for w in ws for _ in range(3)]
    return _pallas(body, name="adamw_small", in_specs=[vm] * (4 * n), out_specs=[vm] * (3 * n), out_shape=out_shape,
                   compiler_params=pltpu.CompilerParams(vmem_limit_bytes=VMEM_LIMIT))(*ws, *gs, *ms, *vs)


def _adamw(w, g, m, v, *, name):
    r, c = w.shape
    tr = r
    for cand in (256, 176, 128, 64):
        if r > cand and r % cand == 0:
            tr = cand
            break

    def body(w_ref, g_ref, m_ref, v_ref, d_ref, mo_ref, vo_ref):
        _adamw_refs(w_ref, g_ref, m_ref, v_ref, d_ref, mo_ref, vo_ref)

    spec = pl.BlockSpec((tr, c), lambda i: (i, 0))
    return _pcall(body, name=name, grid=(r // tr,), in_specs=[spec] * 4, out_specs=[spec] * 3,
                  out_shape=[_sds((r, c))] * 3, dims=("parallel",))(w, g, m, v)


def _prefetch_call(body, *, name, grid, in_specs, out_specs, out_shape, operands):
    grid_spec = pltpu.PrefetchScalarGridSpec(num_scalar_prefetch=1, grid=grid, in_specs=in_specs, out_specs=out_specs)
    params = pltpu.CompilerParams(dimension_semantics=("parallel",) * len(grid), vmem_limit_bytes=VMEM_LIMIT)
    return _pallas(body, name=name, grid_spec=grid_spec, out_shape=out_shape, compiler_params=params)(*operands)


def _place_cols(buf, shard, place):
    rows, cols = shard.shape
    tr = 256

    def body(place_ref, s_ref, b_ref, o_ref):
        o_ref[...] = s_ref[...]

    grid_spec = pltpu.PrefetchScalarGridSpec(
        num_scalar_prefetch=1, grid=(rows // tr,),
        in_specs=[pl.BlockSpec((tr, cols), lambda i, p: (i, 0)), pl.BlockSpec(memory_space=pltpu.HBM)],
        out_specs=pl.BlockSpec((tr, cols), lambda i, p: (i, p[0])))
    return _pallas(body, name="place_own_cols", grid_spec=grid_spec, out_shape=_sds(buf.shape, buf.dtype),
                   input_output_aliases={2: 0},
                   compiler_params=pltpu.CompilerParams(dimension_semantics=("parallel",),
                                                        vmem_limit_bytes=VMEM_LIMIT))(place, shard, buf)


def _half_rows_tile(hr):
    return hr if hr <= 256 else 176 if hr % 176 == 0 else 256


def _add_half(g, landed, place, *, name):
    def body(place_ref, g_ref, l_ref, o_ref):
        own = g_ref[0] if len(g_ref.shape) == 4 else g_ref[...]
        o_ref[...] = (own + l_ref[...]).astype(BF16)

    if g.ndim == 4:
        _, _, hr, c = g.shape
        tr = _half_rows_tile(hr)
        blk = (1, tr, c)
        return _prefetch_call(
            body, name=name, grid=(N_CHIPS, hr // tr),
            in_specs=[pl.BlockSpec((1,) + blk, lambda j, i, p: (j, p[1], i, 0)), pl.BlockSpec(blk, lambda j, i, p: (j, i, 0))],
            out_specs=pl.BlockSpec(blk, lambda j, i, p: (j, i, 0)), out_shape=_sds(landed.shape, BF16),
            operands=(place, g, landed))
    hr, c = landed.shape
    tr, tc = 256, _tile(c, 2176)
    nb = hr // tr
    return _prefetch_call(
        body, name=name, grid=(nb, c // tc),
        in_specs=[pl.BlockSpec((tr, tc), lambda i, j, p: (p[1] * nb + i, j)), pl.BlockSpec((tr, tc), lambda i, j, p: (i, j))],
        out_specs=pl.BlockSpec((tr, tc), lambda i, j, p: (i, j)), out_shape=_sds(landed.shape, BF16),
        operands=(place, g, landed))


def _sum_chips(chip_sum, lands, place, *, name, tc, window_stride=0):
    _, hr, c = lands.shape
    tr = _half_rows_tile(hr)
    nb = hr // tr
    ncb = c // tc

    def body(place_ref, own_ref, a_ref, b_ref, c_ref, o_ref):
        own = own_ref[0] if len(own_ref.shape) == 3 else own_ref[...]
        o_ref[...] = ((own.astype(F32) + a_ref[0].astype(F32)) + b_ref[0].astype(F32)) + c_ref[0].astype(F32)

    land = lambda k: pl.BlockSpec((1, tr, tc), lambda i, j, p: ((p[0] + k) % N_CHIPS, i, j))
    if chip_sum.ndim == 3:
        own_spec = land(0)
    else:
        stride = window_stride // tc
        own_spec = pl.BlockSpec((tr, tc), lambda i, j, p: (i, p[0] * stride + j))
    return _prefetch_call(
        body, name=name, grid=(nb, ncb), in_specs=[own_spec, land(1), land(2), land(3)],
        out_specs=pl.BlockSpec((tr, tc), lambda i, j, p: (p[1] * nb + i, j)), out_shape=_sds((2 * hr, c)),
        operands=(place, chip_sum, lands, lands, lands))


_HBM = pl.BlockSpec(memory_space=pltpu.HBM)


def _place():
    x, y, c = lax.axis_index("x"), lax.axis_index("y"), lax.axis_index("c")
    chips = [(1 - x, y), (x, 1 - y), (1 - x, 1 - y)]
    return x, y, c, chips


def _rcopy(src, dst, send_sem, recv_sem, to):
    return pltpu.make_async_remote_copy(src_ref=src, dst_ref=dst, send_sem=send_sem, recv_sem=recv_sem,
                                        device_id=to, device_id_type=MESH)


UP_COLS = 2 * D_FF // N_CHIPS
IN_WINDOW = 640
IN_STRIDE = 512


class _Hosted:
    def __init__(self, operands, out_shapes, n_sems, start, finish, aliases=None, local_sems=0):
        self.operands, self.out_shapes, self.n_sems = list(operands), list(out_shapes), n_sems
        self.start, self.finish, self.aliases, self.local_sems = start, finish, dict(aliases or {}), local_sems

    def scratch(self):
        return ([pltpu.SemaphoreType.DMA((self.n_sems,)), pltpu.SemaphoreType.DMA((self.n_sems,))]
                + [pltpu.SemaphoreType.DMA] * self.local_sems)


def _both(a, b):
    na, nao, nas = len(a.operands), len(a.out_shapes), len(a.scratch())

    def start(ins, outs, sems):
        a.start(ins[:na], outs[:nao], sems[:nas])
        b.start(ins[na:], outs[nao:], sems[nas:])

    def finish(ins, outs, sems):
        a.finish(ins[:na], outs[:nao], sems[:nas])
        b.finish(ins[na:], outs[nao:], sems[nas:])

    both = _Hosted(a.operands + b.operands, a.out_shapes + b.out_shapes, 0, start, finish,
                   aliases={**a.aliases, **{na + i: nao + o for i, o in b.aliases.items()}})
    both.scratch = lambda: a.scratch() + b.scratch()
    return both


def _then(a, b):
    nas = len(a.scratch())

    def finish(ins, outs, sems):
        a.finish(ins, outs, sems[:nas])
        b.start(ins, outs, sems[nas:])
        b.finish(ins, outs, sems[nas:])

    chain = _Hosted(a.operands, a.out_shapes, 0, lambda ins, outs, sems: a.start(ins, outs, sems[:nas]), finish,
                    aliases=a.aliases)
    chain.scratch = lambda: a.scratch() + b.scratch()
    return chain


def _run_hosted(hosted, *, name):
    n_in, n_out = len(hosted.operands), len(hosted.out_shapes)

    def body(*refs):
        parts = (refs[:n_in], refs[n_in:n_in + n_out], refs[n_in + n_out:])
        hosted.start(*parts)
        hosted.finish(*parts)

    return _pallas(body, name=name, in_specs=[_HBM] * n_in, out_specs=[_HBM] * n_out, out_shape=hosted.out_shapes,
                   input_output_aliases=hosted.aliases, scratch_shapes=hosted.scratch())(*hosted.operands)


def _host_pcall(core_body, hosted, first, last, *, n_in, n_out, n_scratch, name, grid, in_specs, out_specs, out_shape,
                scratch_shapes, dims, operands):
    if hosted is None:
        outs = _pcall(core_body, name=name, grid=grid, in_specs=in_specs, out_specs=out_specs, out_shape=out_shape,
                      scratch_shapes=scratch_shapes, dims=dims)(*operands)
        return outs, []
    hi, ho = len(hosted.operands), len(hosted.out_shapes)

    def body(*refs):
        a, b = n_in, n_in + hi
        c, d = b + n_out, b + n_out + ho
        e = d + n_scratch
        parts = (refs[a:b], refs[c:d], refs[e:])

        @pl.when(first())
        def _():
            hosted.start(*parts)

        core_body(*refs[:a], *refs[b:c], *refs[d:e])

        @pl.when(last())
        def _():
            hosted.finish(*parts)

    params = pltpu.CompilerParams(dimension_semantics=("arbitrary",) * len(grid), vmem_limit_bytes=VMEM_LIMIT)
    outs = _pallas(body, name=name, grid=grid, in_specs=list(in_specs) + [_HBM] * hi, out_specs=list(out_specs) + [_HBM] * ho,
                   out_shape=list(out_shape) + hosted.out_shapes, scratch_shapes=list(scratch_shapes) + hosted.scratch(),
                   input_output_aliases={n_in + a: n_out + b for a, b in hosted.aliases.items()},
                   compiler_params=params)(*operands, *hosted.operands)
    return outs[:n_out], outs[n_out:]


WHOLE_HALF = (0, 1, 1)


def _band_rows(src, hc, band):
    first, count, of = band
    hr = src.shape[0] // 2
    return pl.ds(hc * hr + first * (hr // of), count * (hr // of))


def _gather_slot(src, out, chip, hc, band=WHOLE_HALF):
    cols = src.shape[1]
    if len(out.shape) == 2:
        return out.at[_band_rows(src, hc, band), pl.ds(pl.multiple_of(chip * cols, LANES), cols)]
    return out.at[chip, _band_rows(src, hc, band), :]


def _gathered_shape(shard, by_cols):
    if by_cols:
        return _sds((shard.shape[0], N_CHIPS * shard.shape[1]), shard.dtype)
    return _sds((N_CHIPS,) + shard.shape, shard.dtype)


def _plan_gather_ici(shards, by_cols, whole=(), bands=None, into=None):
    n = len(shards)
    bands = bands or [WHOLE_HALF] * n
    into = into or [None] * n
    given = [w for w in range(n) if into[w] is not None]
    n_ops = n + len(whole)

    def copies(ins, outs, sems):
        send_sems, recv_sems = sems[0], sems[1]
        x, y, c, chips = _place()
        me = 2 * x + y
        sends, waits = [], []
        for w in range(n + len(whole)):
            for k, (cx, cy) in enumerate(chips):
                sem = (send_sems.at[3 * w + k], recv_sems.at[3 * w + k])
                if w < n:
                    sends.append(_rcopy(ins[w].at[_band_rows(ins[w], c, bands[w]), :],
                                        _gather_slot(ins[w], outs[w], me, c, bands[w]), *sem, (cx, cy, c)))
                    landed = _gather_slot(ins[w], outs[w], 2 * cx + cy, c, bands[w])
                else:
                    sends.append(_rcopy(ins[w], outs[w].at[me], *sem, (cx, cy, c)))
                    landed = outs[w].at[2 * cx + cy]
                waits.append(_rcopy(landed, landed, *sem, (cx, cy, c)))
        return sends, waits

    def start(ins, outs, sems):
        for cp in copies(ins, outs, sems)[0]:
            cp.start()

    def finish(ins, outs, sems):
        sends, waits = copies(ins, outs, sems)
        for cp in waits:
            cp.wait_recv()
        for cp in sends:
            cp.wait_send()

    out_shapes = [_gathered_shape(s, bc) for s, bc in zip(shards, by_cols)] + [_sds((N_CHIPS,) + a.shape, a.dtype) for a in whole]
    return _Hosted(list(shards) + list(whole) + [into[w] for w in given], out_shapes, 3 * n_ops, start, finish,
                   aliases={n_ops + i: w for i, w in enumerate(given)})


def _plan_gather_d2d(bufs, shard_shapes, bands=None):
    n = len(bufs)
    bands = bands or [WHOLE_HALF] * n

    def copies(ins, outs, sems):
        send_sems, recv_sems = sems
        x, y, c, chips = _place()
        sibling = (x, y, 1 - c)
        sends, waits = [], []
        for w in range(n):
            for k, (cx, cy) in enumerate(chips):
                sem = (send_sems.at[3 * w + k], recv_sems.at[3 * w + k])
                landed = _gather_slot(shard_shapes[w], outs[w], 2 * cx + cy, c, bands[w])
                other = _gather_slot(shard_shapes[w], outs[w], 2 * cx + cy, 1 - c, bands[w])
                sends.append(_rcopy(landed, landed, *sem, sibling))
                waits.append(_rcopy(other, other, *sem, sibling))
        return sends, waits

    def start(ins, outs, sems):
        for cp in copies(ins, outs, sems)[0]:
            cp.start()

    def finish(ins, outs, sems):
        sends, waits = copies(ins, outs, sems)
        for cp in waits:
            cp.wait_recv()
        for cp in sends:
            cp.wait_send()

    return _Hosted(bufs, [_sds(b.shape, b.dtype) for b in bufs], 3 * n, start, finish, aliases={w: w for w in range(n)})


def _plan_allgather_first(block):
    def copies(ins, outs, sems):
        send_sems, recv_sems = sems
        x, y, c, chips = _place()
        me = 4 * x + 2 * y + c
        peers = [(x, y, 1 - c)] + [(cx, cy, c) for cx, cy in chips]
        sends = [_rcopy(ins[0], outs[0].at[me], send_sems.at[k], recv_sems.at[k], p) for k, p in enumerate(peers)]
        waits = [_rcopy(outs[0].at[4 * px + 2 * py + pc], outs[0].at[4 * px + 2 * py + pc], send_sems.at[k],
                        recv_sems.at[k], (px, py, pc)) for k, (px, py, pc) in enumerate(peers)]
        return sends, waits

    def start(ins, outs, sems):
        for cp in copies(ins, outs, sems)[0]:
            cp.start()

    def finish(ins, outs, sems):
        sends, waits = copies(ins, outs, sems)
        for cp in waits:
            cp.wait_recv()
        for cp in sends:
            cp.wait_send()

    return _Hosted([block], [_sds((8,) + block.shape)], 4, start, finish)


def _plan_allgather_second(gathered):
    def copies(ins, outs, sems):
        send_sems, recv_sems = sems
        x, y, c, chips = _place()
        sends, waits = [], []
        for k, (cx, cy) in enumerate(chips):
            landed = outs[0].at[4 * cx + 2 * cy + c]
            other = outs[0].at[4 * cx + 2 * cy + 1 - c]
            sends.append(_rcopy(landed, landed, send_sems.at[k], recv_sems.at[k], (x, y, 1 - c)))
            waits.append(_rcopy(other, other, send_sems.at[k], recv_sems.at[k], (x, y, 1 - c)))
        return sends, waits

    def start(ins, outs, sems):
        for cp in copies(ins, outs, sems)[0]:
            cp.start()

    def finish(ins, outs, sems):
        sends, waits = copies(ins, outs, sems)
        for cp in waits:
            cp.wait_recv()
        for cp in sends:
            cp.wait_send()

    return _Hosted([gathered], [_sds(gathered.shape)], 3, start, finish, aliases={0: 0})


def _place_block(gathered, block, device):
    rows, lanes = block.shape

    def body(dev_ref, b_ref, g_ref, o_ref):
        o_ref[0] = b_ref[...]

    grid_spec = pltpu.PrefetchScalarGridSpec(
        num_scalar_prefetch=1, grid=(1,),
        in_specs=[pl.BlockSpec((rows, lanes), lambda i, d: (0, 0)), pl.BlockSpec(memory_space=pltpu.HBM)],
        out_specs=pl.BlockSpec((1, rows, lanes), lambda i, d: (d[0], 0, 0)))
    return _pallas(body, name="place_own_block", grid_spec=grid_spec, out_shape=_sds(gathered.shape),
                   input_output_aliases={2: 0},
                   compiler_params=pltpu.CompilerParams(dimension_semantics=("arbitrary",),
                                                        vmem_limit_bytes=VMEM_LIMIT))(device, block, gathered)


def _sum_devices(gathered):
    _, rows, lanes = gathered.shape
    tr = rows // 2 if rows % 16 == 0 else rows

    def body(g_ref, o_ref):
        acc = g_ref[0]
        for d in range(1, 8):
            acc = acc + g_ref[d]
        o_ref[...] = acc

    return _pcall(body, name="sum_devices", grid=(rows // tr,), in_specs=[pl.BlockSpec((8, tr, lanes), lambda i: (0, i, 0))],
                  out_specs=pl.BlockSpec((tr, lanes), lambda i: (i, 0)), out_shape=_sds((rows, lanes)), dims=("parallel",))(gathered)


def _plan_swap(grads):
    def copies(ins, outs, sems):
        send_sems, recv_sems = sems
        x, y, c, _ = _place()
        cps = []
        for w, g_ref in enumerate(ins):
            if len(g_ref.shape) == 4:
                theirs = g_ref.at[:, 1 - c]
            else:
                hr = g_ref.shape[0] // 2
                theirs = g_ref.at[pl.ds((1 - c) * hr, hr), :]
            cps.append(_rcopy(theirs, outs[w], send_sems.at[w], recv_sems.at[w], (x, y, 1 - c)))
        return cps

    def start(ins, outs, sems):
        for cp in copies(ins, outs, sems):
            cp.start()

    def finish(ins, outs, sems):
        for cp in copies(ins, outs, sems):
            cp.wait()

    out_shapes = [_sds((g.shape[0], g.shape[2], g.shape[3])) if g.ndim == 4 else _sds((g.shape[0] // 2, g.shape[1]))
                  for g in grads]
    return _Hosted(grads, out_shapes, len(grads), start, finish)


def _plan_scatter(chip_sums, windows):
    def copies(ins, outs, sems):
        send_sems, recv_sems = sems
        x, y, c, chips = _place()
        me = 2 * x + y
        sends, waits = [], []
        for w, s_ref in enumerate(ins):
            for k, (cx, cy) in enumerate(chips):
                tgt = 2 * cx + cy
                if windows[w] is not None:
                    stride, width = windows[w]
                    part = s_ref.at[:, pl.ds(pl.multiple_of(tgt * stride, LANES), width)]
                else:
                    part = s_ref.at[tgt]
                sem = (send_sems.at[3 * w + k], recv_sems.at[3 * w + k])
                sends.append(_rcopy(part, outs[w].at[me], *sem, (cx, cy, c)))
                slot = outs[w].at[tgt]
                waits.append(_rcopy(slot, slot, *sem, (cx, cy, c)))
        return sends, waits

    def start(ins, outs, sems):
        for cp in copies(ins, outs, sems)[0]:
            cp.start()

    def finish(ins, outs, sems):
        sends, waits = copies(ins, outs, sems)
        for cp in waits:
            cp.wait_recv()
        for cp in sends:
            cp.wait_send()

    out_shapes = [_sds((N_CHIPS, s.shape[0], win[1]), BF16) if win is not None else _sds(s.shape, BF16)
                  for s, win in zip(chip_sums, windows)]
    return _Hosted(chip_sums, out_shapes, 3 * len(chip_sums), start, finish)


def _plan_join(reds):
    def copies(ins, outs, sems):
        send_sems, recv_sems = sems
        x, y, c, _ = _place()
        sends, waits = [], []
        for w, out in enumerate(outs):
            hr = out.shape[0] // 2
            mine = out.at[pl.ds(c * hr, hr), :]
            theirs = out.at[pl.ds((1 - c) * hr, hr), :]
            sends.append(_rcopy(mine, mine, send_sems.at[w], recv_sems.at[w], (x, y, 1 - c)))
            waits.append(_rcopy(theirs, theirs, send_sems.at[w], recv_sems.at[w], (x, y, 1 - c)))
        return sends, waits

    def start(ins, outs, sems):
        for cp in copies(ins, outs, sems)[0]:
            cp.start()

    def finish(ins, outs, sems):
        sends, waits = copies(ins, outs, sems)
        for cp in waits:
            cp.wait_recv()
        for cp in sends:
            cp.wait_send()

    return _Hosted(reds, [_sds(r.shape) for r in reds], len(reds), start, finish, aliases={w: w for w in range(len(reds))})


def _allreduce_small(v):
    m_per = v.shape[0]

    def body(v_ref, out_ref, all_ref, send_sems, recv_sems, local_sem):
        x, y, c, chips = _place()
        me, sibling = (x, y, c), (x, y, 1 - c)

        def rows(px, py, pc):
            return all_ref.at[pl.ds((4 * px + 2 * py + pc) * m_per, m_per), :]

        def copy(k, block, to, src=None):
            return _rcopy(rows(*block) if src is None else src, rows(*block), send_sems.at[k], recv_sems.at[k], to)

        mine = pltpu.make_async_copy(v_ref, rows(*me), local_sem)
        mine.start()
        first = [copy(0, me, sibling, src=v_ref)]
        first += [copy(1 + k, me, (*chip, c), src=v_ref) for k, chip in enumerate(chips)]
        for cp in first:
            cp.start()
        passed = [copy(4 + k, (*chip, c), sibling) for k, chip in enumerate(chips)]
        for k, chip in enumerate(chips):
            copy(1 + k, (*chip, c), me).wait_recv()
            passed[k].start()
        copy(0, sibling, me).wait_recv()
        for k, chip in enumerate(chips):
            copy(4 + k, (*chip, 1 - c), me).wait_recv()
        for cp in first + passed:
            cp.wait_send()
        mine.wait()
        acc = all_ref[pl.ds(0, m_per), :]
        for d in range(1, 8):
            acc = acc + all_ref[pl.ds(d * m_per, m_per), :]
        out_ref[...] = acc

    vm = pl.BlockSpec(memory_space=pltpu.VMEM)
    return _pallas(body, name="allreduce_small", in_specs=[vm], out_specs=vm, out_shape=_sds((m_per, LANES)),
                          scratch_shapes=[pltpu.VMEM((8 * m_per, LANES), F32), pltpu.SemaphoreType.DMA((7,)),
                                          pltpu.SemaphoreType.DMA((7,)), pltpu.SemaphoreType.DMA],
                          compiler_params=pltpu.CompilerParams(vmem_limit_bytes=VMEM_LIMIT))(v)


def _block_diag(blocks):
    j, g, a, b = blocks.shape
    eye = jnp.eye(g, dtype=bool)[None, :, None, :, None]
    return jnp.where(eye, blocks[:, :, :, None, :], jnp.zeros((), blocks.dtype)).reshape(j, g * a, g * b)


def _diag_blocks(m, a, b):
    j = m.shape[0]
    g = m.shape[1] // a
    t = m.reshape(j, g, a, g, b)
    eye = jnp.eye(g, dtype=bool)[None, :, None, :, None]
    return jnp.sum(jnp.where(eye, t, 0.0), axis=3)


_SMALL = (("g_mix", (1024,)), ("b_f", (8,)), ("g_q", (64,)), ("g_k", (64,)), ("lambda_re", (32, 64)),
          ("lambda_im", (32, 64)), ("log_step", (32,)), ("b_re", (32, 64, 16)), ("b_im", (32, 64, 16)),
          ("c_re", (32, 16, 64)), ("c_im", (32, 16, 64)), ("d_skip", (32, 16)), ("b_glu", (512,)),
          ("g_attn_out", (512,)), ("g_ssm_out", (512,)), ("g_ffn", (1024,)), ("conv_b", (5632,)))


_LATE_SMALL = ("g_mix", "b_f", "g_q", "g_k")
_EARLY_SMALL = tuple(n for n, _ in _SMALL if n not in _LATE_SMALL)


def _packed_rows(n):
    tile = SUBLANES * LANES
    return -(-n // tile) * SUBLANES


def _pack_small(arrs):
    parts = []
    for a in arrs:
        flat = a.reshape(-1)
        rows = _packed_rows(flat.shape[0])
        parts.append(jnp.pad(flat, (0, rows * LANES - flat.shape[0])).reshape(rows, LANES))
    return jnp.concatenate(parts, axis=0)


def _unpack_small(buf, shapes):
    out, r = [], 0
    for shape in shapes:
        n = math.prod(shape)
        out.append(buf[r:r + _packed_rows(n)].reshape(-1)[:n].reshape(shape))
        r += _packed_rows(n)
    return out


def _halves(t):
    return t.reshape(N_CHIPS, 2, t.shape[0] // (2 * N_CHIPS), t.shape[1])


class _MeshComm:
    def __init__(self, args):
        x, y, self.core = lax.axis_index("x"), lax.axis_index("y"), lax.axis_index("c")
        self.chip = 2 * x + y
        self.place = jnp.stack([self.chip, self.core]).astype(jnp.int32)
        self.shards = {n: args[n].astype(BF16) for n in ("w_in", "w_glu", "w_out", "w_up", "w_down")}
        self.conv_w = args["conv_w"]

    def _own(self, stacked, mine):
        return lax.dynamic_update_slice(stacked, mine[None], (self.chip,) + (0,) * mine.ndim)

    def w_in(self):
        sh = self.shards["w_in"]
        (buf,) = _run_hosted(_then(_plan_gather_ici([sh], [False]), _plan_gather_d2d([sh], [sh])), name="gather_w_in")
        whole = self._own(buf, sh).transpose(1, 0, 2).reshape(D_MODEL, IN_COLS)
        return jnp.pad(whole, ((0, 0), (0, Z_COLS - IN_COLS)))

    def gather_first(self):
        self.mid = [self.shards[n] for n in ("w_glu", "w_out", "w_down")]
        return _plan_gather_ici(self.mid + [self.shards["w_up"]], [False, False, False, True], whole=[self.conv_w],
                                bands=[WHOLE_HALF] * 3 + [(0, 1, 4)])

    def gather_second(self, landed):
        self.g_cw = landed[4]
        return _both(_plan_gather_d2d(list(landed[:3]), self.mid),
                     _plan_gather_ici([self.shards["w_up"]], [True], bands=[(1, 3, 4)], into=[landed[3]]))

    def weights(self, gathered):
        g_glu, g_out, g_down = gathered[:3]
        own = self._own
        return (own(g_glu, self.mid[0]).reshape(SSM_W, SSM_W), own(g_out, self.mid[1]).reshape(D_MODEL, D_MODEL),
                own(g_down, self.mid[2]).reshape(D_FF, D_MODEL),
                own(self.g_cw, self.conv_w).transpose(1, 0, 2).reshape(3, 2 * D_FF))

    def gather_third(self, gathered):
        return _plan_gather_d2d([gathered[3]], [self.shards["w_up"]])

    def w_up(self, passed):
        return _place_cols(passed[0], self.shards["w_up"], self.place)

    def swap_down(self, d_w_down):
        self.d_down = _halves(d_w_down)
        return _plan_swap([self.d_down])

    def swap(self, landed_down, d_w_up, d_w_glu, d_w_out):
        self.sum_down = _add_half(self.d_down, landed_down[0], self.place, name="add_w_down")
        self.early = [d_w_up, _halves(d_w_glu), _halves(d_w_out)]
        return _both(_plan_scatter([self.sum_down], [None]), _plan_swap(self.early))

    def scatter(self, landed, small_block):
        self.land_down = landed[0]
        self.early_sums = [_add_half(g, l, self.place, name="add_" + n)
                           for g, l, n in zip(self.early, landed[1:], ("w_up", "w_glu", "w_out"))]
        return _both(_plan_scatter(self.early_sums, [(UP_COLS, UP_COLS), None, None]), _plan_allgather_first(small_block))

    def swap_in(self, d_w_in):
        self.d_in = d_w_in
        return _plan_swap([d_w_in])

    def scatter_in(self, landed):
        self.sum_in = _add_half(self.d_in, landed[0], self.place, name="add_w_in")
        return _plan_scatter([self.sum_in], [(IN_STRIDE, IN_WINDOW)])

    def small_second(self, landed_small):
        return _plan_allgather_second(landed_small[0])

    def reduce(self, lands):
        early_lands, (land_in,) = lands
        sum_in = self.sum_in
        es, el = self.early_sums, early_lands
        todo = [(sum_in, land_in, "w_in", LANES, IN_STRIDE), (es[1], el[1], "w_glu", SSM_W, 0),
                (es[2], el[2], "w_out", D_MODEL, 0), (es[0], el[0], "w_up", UP_COLS, UP_COLS),
                (self.sum_down, self.land_down, "w_down", D_MODEL, 0)]
        reds = _run_hosted(_plan_join([_sum_chips(s, l, self.place, name="sum_" + n, tc=tc, window_stride=st)
                                       for s, l, n, tc, st in todo]), name="join_halves")
        g_big = dict(zip(("w_in", "w_glu", "w_out", "w_up", "w_down"), reds))
        g_big["w_in"] = lax.dynamic_slice_in_dim(reds[0], 2 * self.chip, IN_COLS // N_CHIPS, axis=1)
        return g_big


def _local_step(x, tgt, p, comm):
    s = x.shape[0]
    row = lambda v: v.reshape(1, -1)
    g_mix, g_ffn = row(p["g_mix"]), row(p["g_ffn"])
    g_att, g_ssm, b_glu, conv_b = row(p["g_attn_out"]), row(p["g_ssm_out"]), row(p["b_glu"]), row(p["conv_b"])
    gq = row(jnp.tile(p["g_q"], HEADS))
    gk = row(jnp.tile(p["g_k"], HEADS))
    bf = row(jnp.pad(p["b_f"], (0, LANES - HEADS)))
    gg = jnp.kron(jnp.eye(HEADS, dtype=F32), jnp.ones((HEAD_DIM, HEAD_DIM), F32)).astype(BF16)
    dsk = row(p["d_skip"])

    rep = lambda a: jnp.repeat(a, SSM_GROUP, axis=0)
    lr, li = rep(p["lambda_re"]), rep(p["lambda_im"])
    ls = rep(jnp.broadcast_to(p["log_step"][:, None], (SSM_GROUPS, SSM_STATE)))
    bt_re = p["b_re"].transpose(0, 2, 1).reshape(_PARAM_SHAPE)
    bt_im = p["b_im"].transpose(0, 2, 1).reshape(_PARAM_SHAPE)
    a_re_rep, a_im_rep, bb_re, bb_im = _ssm_params(lr, li, ls, bt_re, bt_im)
    ar = a_re_rep[::SSM_GROUP].reshape(SSM_CHUNKS, 1, CHUNK_S)
    ai = a_im_rep[::SSM_GROUP].reshape(SSM_CHUNKS, 1, CHUNK_S)
    chunked = lambda t: t.reshape(SSM_CHUNKS, SSM_GROUPS // SSM_CHUNKS, SSM_GROUP, SSM_STATE)
    bbr = _block_diag(chunked(bb_re)).astype(BF16)
    bbi = _block_diag(chunked(bb_im)).astype(BF16)
    to_cc = lambda c: _block_diag(chunked(c).transpose(0, 1, 3, 2)).astype(BF16)
    ccr, cci = to_cc(p["c_re"]), to_cc(p["c_im"])

    w_in_r = comm.w_in()
    hb, z = _in_proj(x, g_mix, w_in_r)
    qh, kh, vh, ub, uf, c128 = _attn_prep(z, gq, gk, bf, gg)
    crow = c128[:, :HEADS].T.reshape(HEADS, 1, s)
    (oh, lse), landed = _attn_fwd(qh, kh, vh, crow, comm.gather_first())
    (xr, xi, y), gathered = _ssm_fwd(ub, uf, bbr, bbi, ar, ai, ccr, cci, dsk, comm.gather_second(landed))
    w_glu_b, w_out_b, w_down_b, conv_w_full = comm.weights(gathered)
    (x1, mixb, h2b), passed = _mix_out(y, oh, x, w_glu_b, b_glu, g_att, g_ssm, w_out_b, g_ffn, comm.gather_third(gathered))
    w_up_b = comm.w_up(passed)
    up = _mm(h2b, w_up_b, name="ffn_up", tm=1024, tn=1408, tk=1024)
    act = _conv_act(up, conv_w_full, conv_b)
    dy, dyb, loss_blk = _down_loss(act, w_down_b, x1, tgt)

    d_w_down = _mm(act, dyb, ta=True, name="d_w_down", tm=1408, tn=1024, tk=2048)
    dact = _mm(dyb, w_down_b, tb=True, name="d_act", tm=1024, tn=1408, tk=1024)
    dupb, dcw = _conv_act_bwd(up, dact, conv_w_full, conv_b)
    d_w_up = _mm(h2b, dupb, ta=True, b_parts=2, name="d_w_up", tm=1024, tn=1408, tk=2048)
    dh2 = _mm(dupb, w_up_b, tb=True, a_parts=2, name="d_h2", tm=1024, tn=1024, tk=1408)
    (dx1, dx1b, doh, dys, d_w_glu, d_g_ffn, d_g_att, d_g_ssm, d_b_glu), landed_down = _mix_bwd(
        dy, dh2, x1, g_ffn, w_out_b, y, oh, w_glu_b, b_glu, g_att, g_ssm, comm.swap_down(d_w_down))
    d_w_out = _mm(mixb, dx1b, ta=True, name="d_w_out", tm=1024, tn=1024, tk=2048)
    (du, dbbr, dbbi, dccr, dcci, dar, dai, dd), swapped = _ssm_bwd(dys, uf, ub, xr, xi, bbr, bbi, ar, ai, ccr, cci, dsk,
                                                                comm.swap(landed_down, d_w_up, d_w_glu, d_w_out))
    unchunk = lambda t: t.reshape(_PARAM_SHAPE)
    dbb_re = unchunk(_diag_blocks(dbbr, SSM_GROUP, SSM_STATE))
    dbb_im = unchunk(_diag_blocks(dbbi, SSM_GROUP, SSM_STATE))
    first_row = (jnp.arange(_PARAM_SHAPE[0]) % SSM_GROUP == 0)[:, None]
    da_re = jnp.where(first_row, rep(dar.reshape(SSM_GROUPS, SSM_STATE)), 0.0)
    da_im = jnp.where(first_row, rep(dai.reshape(SSM_GROUPS, SSM_STATE)), 0.0)
    expand_t = (jnp.arange(SSM_GROUPS)[:, None] == (jnp.arange(_PARAM_SHAPE[0]) // SSM_GROUP)[None, :]).astype(BF16)
    d_lr, d_li, d_ls, d_bt_re, d_bt_im = _ssm_params_bwd(lr, li, ls, bt_re, bt_im, da_re, da_im, dbb_re, dbb_im, expand_t)
    from_bt = lambda t: t.reshape(SSM_GROUPS, SSM_GROUP, SSM_STATE).transpose(0, 2, 1)
    from_cc = lambda t: _diag_blocks(t, SSM_STATE, SSM_GROUP).transpose(0, 1, 3, 2).reshape(SSM_GROUPS, SSM_GROUP, SSM_STATE)

    small = {
        "lambda_re": d_lr, "lambda_im": d_li, "log_step": d_ls,
        "b_re": from_bt(d_bt_re), "b_im": from_bt(d_bt_im), "c_re": from_cc(dccr), "c_im": from_cc(dcci),
        "d_skip": dd, "b_glu": d_b_glu, "g_attn_out": d_g_att, "g_ssm_out": d_g_ssm, "g_ffn": d_g_ffn,
        "conv_b": dcw[:, 3],
    }
    d_conv_w = dcw[:, 0:3].transpose(1, 0, 2).reshape(3, 2 * D_FF)
    early_small = _pack_small([small[n] for n in _EARLY_SMALL] + [d_conv_w])

    (dqh, dkh, dvh, dcrow), landed = _attn_bwd(qh, kh, vh, crow, lse, doh, comm.scatter(swapped, early_small))
    early_lands, small_landed = landed[:3], landed[3:]
    dc128 = jnp.pad(dcrow.reshape(HEADS, s).T, ((0, 0), (0, LANES - HEADS)))
    (dzb, d_gq, d_gk, d_bf), small_gathered = _prep_bwd(z, dqh, dkh, dvh, du, dc128, gq, gk, bf, gg,
                                                        comm.small_second(small_landed))
    d_w_in_r = _mm(hb, dzb, ta=True, name="d_w_in", tm=512, tn=Z_COLS, tk=2048)
    dh, swapped_in = _mm(dzb, w_in_r, tb=True, name="d_h", tm=1024, tn=1024, tk=Z_COLS, carry=True,
                         hosted=comm.swap_in(d_w_in_r))
    (dx, d_g_mix), land_in = _in_norm_bwd(x, g_mix, dh, dx1, comm.scatter_in(swapped_in))
    small.update({"g_mix": d_g_mix, "b_f": d_bf[0, :HEADS], "g_q": d_gq.reshape(HEADS, HEAD_DIM).sum(0),
                  "g_k": d_gk.reshape(HEADS, HEAD_DIM).sum(0)})
    big = {"w_in": d_w_in_r, "w_glu": d_w_glu, "w_out": d_w_out, "w_up": d_w_up, "w_down": d_w_down}
    return loss_blk[0, 0], dx, big, small, d_conv_w, (early_lands, land_in, small_gathered, early_small)


def kernel(x, g_mix, w_in, b_f, g_q, g_k, lambda_re, lambda_im, log_step, b_re, b_im, c_re, c_im, d_skip, w_glu, b_glu, g_attn_out, g_ssm_out, w_out, g_ffn, w_up, conv_w, conv_b, w_down, loss_target, m_g_mix, m_w_in, m_b_f, m_g_q, m_g_k, m_lambda_re, m_lambda_im, m_log_step, m_b_re, m_b_im, m_c_re, m_c_im, m_d_skip, m_w_glu, m_b_glu, m_g_attn_out, m_g_ssm_out, m_w_out, m_g_ffn, m_w_up, m_conv_w, m_conv_b, m_w_down, v_g_mix, v_w_in, v_b_f, v_g_q, v_g_k, v_lambda_re, v_lambda_im, v_log_step, v_b_re, v_b_im, v_c_re, v_c_im, v_d_skip, v_w_glu, v_b_glu, v_g_attn_out, v_g_ssm_out, v_w_out, v_g_ffn, v_w_up, v_conv_w, v_conv_b, v_w_down):
    args = dict(locals())
    order = ["g_mix", "w_in", "b_f", "g_q", "g_k", "lambda_re", "lambda_im", "log_step", "b_re", "b_im", "c_re", "c_im",
             "d_skip", "w_glu", "b_glu", "g_attn_out", "g_ssm_out", "w_out", "g_ffn", "w_up", "conv_w", "conv_b", "w_down"]
    comm = _MeshComm(args)
    chip = comm.chip
    loss_part, dx, big, small, d_conv_w, lands = _local_step(x[0], loss_target[0], args, comm)

    g_big = comm.reduce(lands[:2])

    shapes = dict(_SMALL)
    small_names = [n for n, _ in _SMALL]
    device = (2 * chip + comm.core).reshape(1).astype(jnp.int32)
    early = _unpack_small(_sum_devices(_place_block(lands[2][0], lands[3], device)),
                          [shapes[n] for n in _EARLY_SMALL] + [(3, 2 * D_FF)])
    late = _unpack_small(_allreduce_small(_pack_small([small[n] for n in _LATE_SMALL] + [loss_part])),
                         [shapes[n] for n in _LATE_SMALL] + [()])
    loss = late[-1]
    g_conv_w = lax.dynamic_slice_in_dim(early[-1], chip * (2 * D_FF // N_CHIPS), 2 * D_FF // N_CHIPS, axis=1)
    g_small = {**dict(zip(_EARLY_SMALL, early[:-1])), **dict(zip(_LATE_SMALL, late[:-1]))}

    grad, delta, new_m, new_v = {}, {}, {}, {}
    for n in ("w_in", "w_glu", "w_out", "w_up", "w_down"):
        grad[n] = g_big[n]
        delta[n], new_m[n], new_v[n] = _adamw(args[n], g_big[n], args["m_" + n], args["v_" + n], name="adamw_" + n)
    grad["conv_w"] = g_conv_w
    delta["conv_w"], new_m["conv_w"], new_v["conv_w"] = _adamw(conv_w, g_conv_w, m_conv_w, v_conv_w, name="adamw_conv_w")
    stepped = _adamw_small([args[n] for n in small_names], [g_small[n] for n in small_names],
                           [args["m_" + n] for n in small_names], [args["v_" + n] for n in small_names])
    for i, n in enumerate(small_names):
        grad[n] = g_small[n]
        delta[n], new_m[n], new_v[n] = stepped[3 * i:3 * i + 3]

    return (loss, dx[None], *[grad[n] for n in order], *[delta[n] for n in order], *[new_m[n] for n in order],
            *[new_v[n] for n in order])
```

```python
import math

import jax
import jax.numpy as jnp
from jax import lax
from jax.experimental import pallas as pl
from jax.experimental.pallas import tpu as pltpu

F32 = jnp.float32
BF16 = jnp.bfloat16

D_MODEL = 1024
HEADS = 8
HEAD_DIM = 64
ATTN_W = 512
SSM_W = 512
SSM_GROUPS = 32
SSM_GROUP = 16
SSM_STATE = 64
N_STATE = SSM_GROUPS * SSM_STATE
D_FF = 2816
IN_COLS = 2056
Z_COLS = 2176
F_COL0 = 1536
U_COL0 = 1544
EPS = 1e-6
NEG_INF = -1e30
N_CHIPS = 4
LANES = 128
SUBLANES = 8
SSM_CHUNKS = 2
SSM_ROWS = 512
CHUNK_U = SSM_W // SSM_CHUNKS
CHUNK_S = N_STATE // SSM_CHUNKS
HEADS_PER_STEP = 4
STRIP = 128
N_STRIPS = D_FF // STRIP

ADAM_LR = 0.001
ADAM_B1 = 0.9
ADAM_B2 = 0.999
ADAM_EPS = 1e-08
ADAM_WD = 0.01
ADAM_STEP = 10

VMEM_LIMIT = 56 * 1024 * 1024
MESH = pl.DeviceIdType.MESH


def _pallas(body, **kw):
    return pl.pallas_call(body, **kw)


def _pcall(body, *, name, out_shape, in_specs, out_specs, grid=(), scratch_shapes=(), dims=None):
    params = pltpu.CompilerParams(dimension_semantics=dims, vmem_limit_bytes=VMEM_LIMIT)
    return _pallas(body, name=name, grid=grid, in_specs=in_specs, out_specs=out_specs,
                   out_shape=out_shape, scratch_shapes=scratch_shapes, compiler_params=params)


def _sds(shape, dtype=F32):
    return jax.ShapeDtypeStruct(shape, dtype)


def _dot(a, b):
    return jnp.dot(a, b, preferred_element_type=F32)


def _dot_nt(a, b):
    return lax.dot_general(a, b, (((1,), (1,)), ((), ())), preferred_element_type=F32)


def _dot_tn(a, b):
    return lax.dot_general(a, b, (((0,), (0,)), ((), ())), preferred_element_type=F32)


def _split3(x):
    hi = x.astype(BF16)
    r = x - hi.astype(F32)
    mid = r.astype(BF16)
    lo = (r - mid.astype(F32)).astype(BF16)
    return hi, mid, lo


def _dot_hi_lo(x, m01):
    hi = x.astype(BF16)
    lo = (x - hi.astype(F32)).astype(BF16)
    return _dot(hi, m01) + _dot(lo, m01)


def _dot_exact_l(m01, x):
    hi, mid, lo = _split3(x)
    return _dot(m01, hi) + _dot(m01, mid) + _dot(m01, lo)


def _sigmoid(x):
    return 1.0 / (1.0 + jnp.exp(-x))


def _rms(x, g):
    r = lax.rsqrt(jnp.mean(x * x, axis=-1, keepdims=True) + EPS)
    return x * r * g


def _rms_bwd(x, g, dy):
    r = lax.rsqrt(jnp.mean(x * x, axis=-1, keepdims=True) + EPS)
    w = dy * g
    dx = r * w - x * (r * r * r) * jnp.mean(w * x, axis=-1, keepdims=True)
    dg = jnp.sum(dy * x * r, axis=0, keepdims=True)
    return dx, dg


_GELU_K = math.sqrt(2.0 / math.pi)
_GELU_C = 0.044715


def _gelu(y):
    return y * (0.5 * (1.0 + jnp.tanh(_GELU_K * (y + _GELU_C * (y * y * y)))))


def _gelu_grad(y):
    t = jnp.tanh(_GELU_K * (y + _GELU_C * (y * y * y)))
    return 0.5 * (1.0 + t) + 0.5 * y * (1.0 - t * t) * (_GELU_K * (1.0 + 3.0 * _GELU_C * y * y))


def _tile(n, pref):
    if n <= pref:
        return n
    divs = [t for t in range(LANES, n + 1, LANES) if n % t == 0]
    below = [t for t in divs if t <= pref]
    if below and 2 * below[-1] >= pref:
        return below[-1]
    above = [t for t in divs if t > pref]
    return above[0] if above else n


def _row_tile(s):
    return min(256, s)


def _wide_tile(s):
    return min(512, s)


def _mm(a, b, *, name, tm, tn, tk, ta=False, tb=False, a_parts=1, b_parts=1, carry=False, hosted=None):
    if a_parts > 1:
        m, kk = a.shape[1], a.shape[2] * a_parts
    elif ta:
        kk, m = a.shape
    else:
        m, kk = a.shape
    if b_parts > 1:
        n = b.shape[2] * b_parts
    else:
        n = b.shape[0] if tb else b.shape[1]
    tm, tn, tk = _tile(m, tm), _tile(n // b_parts, tn), _tile(kk // a_parts, tk)
    k_per, n_per = kk // a_parts // tk, n // b_parts // tn

    def body(a_ref, b_ref, o_ref):
        k = pl.program_id(2)
        if ta:
            part = _dot_tn(a_ref[...], b_ref[...])
        elif tb:
            part = _dot_nt(a_ref[...], b_ref[...])
        else:
            part = _dot(a_ref[...], b_ref[...])

        @pl.when(k == 0)
        def _():
            o_ref[...] = part

        @pl.when(k > 0)
        def _():
            o_ref[...] += part

    if a_parts > 1:
        a_spec = pl.BlockSpec((None, tm, tk), lambda i, j, k: (k // k_per, i, k % k_per))
    else:
        a_spec = pl.BlockSpec((tk, tm), lambda i, j, k: (k, i)) if ta else pl.BlockSpec((tm, tk), lambda i, j, k: (i, k))
    if b_parts > 1:
        b_spec = pl.BlockSpec((None, tk, tn), lambda i, j, k: (j // n_per, k, j % n_per))
    else:
        b_spec = pl.BlockSpec((tn, tk), lambda i, j, k: (j, k)) if tb else pl.BlockSpec((tk, tn), lambda i, j, k: (k, j))
    grid = (m // tm, n // tn, kk // tk)
    at = lambda step: (lambda: jnp.logical_and(jnp.logical_and(pl.program_id(0) == step[0], pl.program_id(1) == step[1]),
                                               pl.program_id(2) == step[2]))
    (out,), carried = _host_pcall(body, hosted, at((0, 0, 0)), at(tuple(g - 1 for g in grid)), n_in=2, n_out=1, n_scratch=0,
                                  name=name, grid=grid, in_specs=[a_spec, b_spec],
                                  out_specs=[pl.BlockSpec((tm, tn), lambda i, j, k: (i, j))], out_shape=[_sds((m, n))],
                                  scratch_shapes=[], dims=("parallel", "parallel", "arbitrary"), operands=(a, b))
    return (out, carried) if carry else out


def _in_norm(x, g_mix, hosted=None):
    s = x.shape[0]
    tm = _wide_tile(s)
    nb = s // tm

    def body(x_ref, g_ref, h_ref):
        h_ref[...] = _rms(x_ref[...], g_ref[...]).astype(BF16)

    row = pl.BlockSpec((tm, D_MODEL), lambda i: (i, 0))
    return _host_pcall(body, hosted, lambda: pl.program_id(0) == 0, lambda: pl.program_id(0) == nb - 1,
                       n_in=2, n_out=1, n_scratch=0, name="in_norm", grid=(nb,),
                       in_specs=[row, pl.BlockSpec((1, D_MODEL), lambda i: (0, 0))], out_specs=[row],
                       out_shape=[_sds((s, D_MODEL), BF16)], scratch_shapes=[], dims=("parallel",), operands=(x, g_mix))


def _split_heads(ref, val):
    for h in range(HEADS):
        ref[h] = val[:, h * HEAD_DIM:(h + 1) * HEAD_DIM].astype(ref.dtype)


def _merge_heads(ref):
    return jnp.concatenate([ref[h].astype(F32) for h in range(HEADS)], axis=-1)


def _forget_logits(z_ref, bf_ref):
    fl = z_ref[:, F_COL0:F_COL0 + LANES] + bf_ref[...]
    return jnp.where(lax.broadcasted_iota(jnp.int32, fl.shape, 1) < HEADS, fl, 0.0)


def _attn_prep(z, gq, gk, bf, gg):
    s = z.shape[0]
    tm = _row_tile(s)

    def body(z_ref, gq_ref, gk_ref, bf_ref, gg_ref, qn_ref, kn_ref, vb_ref, ub_ref, uf_ref, c_ref, carry_ref):
        i = pl.program_id(0)

        @pl.when(i == 0)
        def _():
            carry_ref[...] = jnp.zeros_like(carry_ref)

        gg_m = gg_ref[...]

        def head_norm(t, g):
            ssq = _dot_hi_lo(t * t, gg_m)
            return t * lax.rsqrt(ssq * (1.0 / HEAD_DIM) + EPS) * g

        _split_heads(qn_ref, head_norm(z_ref[:, 0:ATTN_W], gq_ref[...]))
        _split_heads(kn_ref, head_norm(z_ref[:, ATTN_W:2 * ATTN_W], gk_ref[...]))
        _split_heads(vb_ref, z_ref[:, 2 * ATTN_W:3 * ATTN_W])
        u = z_ref[:, U_COL0:U_COL0 + SSM_W]
        uf_ref[...] = u
        ub_ref[...] = u.astype(BF16)
        fl = _forget_logits(z_ref, bf_ref)
        lf = jnp.minimum(fl, 0.0) - jnp.log1p(jnp.exp(-jnp.abs(fl)))
        row = lax.broadcasted_iota(jnp.int32, (tm, tm), 0)
        col = lax.broadcasted_iota(jnp.int32, (tm, tm), 1)
        tri = (row >= col).astype(BF16)
        c = _dot_exact_l(tri, lf) + carry_ref[...]
        c_ref[...] = c
        carry_ref[...] = c[tm - 1:tm, :]

    row_spec = lambda w: pl.BlockSpec((tm, w), lambda i: (i, 0))
    const = lambda shape: pl.BlockSpec(shape, lambda i: (0, 0))
    heads = pl.BlockSpec((HEADS, tm, HEAD_DIM), lambda i: (0, i, 0))
    return _pcall(body, name="attn_prep", grid=(s // tm,),
                  in_specs=[row_spec(Z_COLS), const((1, ATTN_W)), const((1, ATTN_W)), const((1, LANES)), const((ATTN_W, ATTN_W))],
                  out_specs=[heads] * 3 + [row_spec(SSM_W), row_spec(SSM_W), row_spec(LANES)],
                  out_shape=[_sds((HEADS, s, HEAD_DIM), BF16)] * 3 + [_sds((s, SSM_W), BF16), _sds((s, SSM_W)), _sds((s, LANES))],
                  scratch_shapes=[pltpu.VMEM((1, LANES), F32)], dims=("arbitrary",))(z, gq, gk, bf, gg)


def _attn_fwd(qh, kh, vh, crow, hosted=None):
    _, s, _ = qh.shape
    tq = _row_tile(s)
    scale = HEAD_DIM ** -0.5

    hp = HEADS
    nq = s // tq
    fold = lambda t, op: op(t[:, :tq // 2], t[:, tq // 2:])

    def body(q_ref, k_ref, v_ref, c_ref, o_ref, lse_ref, s_s):
        i = pl.program_id(1)

        def first(j, ms, diagonal):
            off = pl.multiple_of(j * tq, tq)
            out = []
            for hh in range(hp):
                sc = _dot_nt(q_ref[hh], k_ref[hh, pl.ds(off, tq), :]) * scale - c_ref[hh, :, pl.ds(off, tq)]
                if diagonal:
                    causal = lax.broadcasted_iota(jnp.int32, (tq, tq), 1) <= lax.broadcasted_iota(jnp.int32, (tq, tq), 0)
                    sc = jnp.where(causal, sc, NEG_INF)
                s_s[hh, j] = sc
                out.append(jnp.maximum(ms[hh], fold(sc, jnp.maximum)))
            return tuple(out)

        ms = lax.fori_loop(0, i, lambda j, c: first(j, c, False), (jnp.full((tq, tq // 2), NEG_INF, F32),) * hp)
        ms = [jnp.max(t, axis=-1, keepdims=True) for t in first(i, ms, True)]

        def second(j, carry):
            rows = pl.ds(pl.multiple_of(j * tq, tq), tq)
            out = []
            for hh in range(hp):
                ls, acc = carry[hh]
                p = jnp.exp(s_s[hh, j] - ms[hh])
                out.append((ls + fold(p, jnp.add), acc + _dot(p.astype(BF16), v_ref[hh, rows, :])))
            return tuple(out)

        zero = (jnp.zeros((tq, tq // 2), F32), jnp.zeros((tq, HEAD_DIM), F32))
        for hh, (ls, acc) in enumerate(lax.fori_loop(0, i + 1, second, (zero,) * hp)):
            l = jnp.sum(ls, axis=-1, keepdims=True)
            o_ref[hh] = acc / l
            lse_ref[hh] = ms[hh] + jnp.log(l)

    blk = pl.BlockSpec((hp, tq, HEAD_DIM), lambda h, i: (h, i, 0))
    full = pl.BlockSpec((hp, s, HEAD_DIM), lambda h, i: (h, 0, 0))
    nh = HEADS // hp
    first = lambda: jnp.logical_and(pl.program_id(0) == 0, pl.program_id(1) == 0)
    last = lambda: jnp.logical_and(pl.program_id(0) == nh - 1, pl.program_id(1) == nq - 1)
    return _host_pcall(body, hosted, first, last, n_in=4, n_out=2, n_scratch=1, name="attn_fwd", grid=(nh, nq),
                       in_specs=[blk, full, full, pl.BlockSpec((hp, 1, s), lambda h, i: (h, 0, 0))],
                       out_specs=[blk, pl.BlockSpec((hp, tq, 1), lambda h, i: (h, i, 0))],
                       out_shape=[_sds((HEADS, s, HEAD_DIM)), _sds((HEADS, s, 1))],
                       scratch_shapes=[pltpu.VMEM((hp, nq, tq, tq), F32)],
                       dims=("parallel", "parallel"), operands=(qh, kh, vh, crow))


def _ssm_param_fn(lr, li, ls, br, bi):
    step = jnp.exp(ls)
    er = jnp.exp(lr * step)
    ab_re = er * jnp.cos(li * step)
    ab_im = er * jnp.sin(li * step)
    num_re = ab_re - 1.0
    num_im = ab_im
    den = lr * lr + li * li
    f_re = (num_re * lr + num_im * li) / den
    f_im = (num_im * lr - num_re * li) / den
    bb_re = f_re * br - f_im * bi
    bb_im = f_re * bi + f_im * br
    return ab_re, ab_im, bb_re, bb_im


_PARAM_SHAPE = (SSM_GROUPS * SSM_GROUP, SSM_STATE)


def _ssm_params(lr, li, ls, br, bi):
    def body(lr_ref, li_ref, ls_ref, br_ref, bi_ref, ar_ref, ai_ref, bbr_ref, bbi_ref):
        ar, ai, bbr, bbi = _ssm_param_fn(lr_ref[...], li_ref[...], ls_ref[...], br_ref[...], bi_ref[...])
        ar_ref[...] = ar
        ai_ref[...] = ai
        bbr_ref[...] = bbr
        bbi_ref[...] = bbi

    spec = pl.BlockSpec(_PARAM_SHAPE, lambda: (0, 0))
    return _pcall(body, name="ssm_params", in_specs=[spec] * 5, out_specs=[spec] * 4,
                  out_shape=[_sds(_PARAM_SHAPE)] * 4)(lr, li, ls, br, bi)


def _ssm_params_bwd(lr, li, ls, br, bi, dar, dai, dbbr, dbbi, expand_t):
    def body(lr_ref, li_ref, ls_ref, br_ref, bi_ref, dar_ref, dai_ref, dbbr_ref, dbbi_ref, et_ref,
             dlr_ref, dli_ref, dls_ref, dbr_ref, dbi_ref):
        _, vjp = jax.vjp(_ssm_param_fn, lr_ref[...], li_ref[...], ls_ref[...], br_ref[...], bi_ref[...])
        dlr, dli, dls, dbr, dbi = vjp((dar_ref[...], dai_ref[...], dbbr_ref[...], dbbi_ref[...]))
        et = et_ref[...]
        dlr_ref[...] = _dot_exact_l(et, dlr)
        dli_ref[...] = _dot_exact_l(et, dli)
        dls_ref[...] = jnp.sum(_dot_exact_l(et, dls), axis=-1, keepdims=True)
        dbr_ref[...] = dbr
        dbi_ref[...] = dbi

    spec = pl.BlockSpec(_PARAM_SHAPE, lambda: (0, 0))
    gspec = pl.BlockSpec((SSM_GROUPS, SSM_STATE), lambda: (0, 0))
    return _pcall(body, name="ssm_params_bwd",
                  in_specs=[spec] * 9 + [pl.BlockSpec((SSM_GROUPS, _PARAM_SHAPE[0]), lambda: (0, 0))],
                  out_specs=[gspec, gspec, pl.BlockSpec((SSM_GROUPS, 1), lambda: (0, 0)), spec, spec],
                  out_shape=[_sds((SSM_GROUPS, SSM_STATE))] * 2 + [_sds((SSM_GROUPS, 1))] + [_sds(_PARAM_SHAPE)] * 2,
                  )(lr, li, ls, br, bi, dar, dai, dbbr, dbbi, expand_t)


def _cmul(ar, ai, br, bi):
    return ar * br - ai * bi, ar * bi + ai * br


def _scan_consts(ar, ai, width, reverse):
    row = lax.broadcasted_iota(jnp.int32, (SUBLANES, width), 0)
    pw = [(ar, ai)]
    for _ in range(SUBLANES - 1):
        pw.append(_cmul(pw[-1][0], pw[-1][1], ar, ai))
    steps = []
    for d in (1, 2, 4):
        keep = (row < SUBLANES - d) if reverse else (row >= d)
        steps.append((d, jnp.where(keep, pw[d - 1][0], 0.0), jnp.where(keep, pw[d - 1][1], 0.0)))
    pr = jnp.zeros((SUBLANES, width), F32)
    pi = jnp.zeros((SUBLANES, width), F32)
    for r in range(SUBLANES):
        e = (SUBLANES - r) if reverse else (r + 1)
        pr = jnp.where(row == r, pw[e - 1][0], pr)
        pi = jnp.where(row == r, pw[e - 1][1], pi)
    return steps, pr, pi


def _scan_tile(xr, xi, cr, ci, consts, reverse):
    steps, pr, pi = consts
    for d, mr, mi in steps:
        sh = (SUBLANES - d) if reverse else d
        sr = pltpu.roll(xr, sh, 0)
        si = pltpu.roll(xi, sh, 0)
        xr, xi = xr + mr * sr - mi * si, xi + mr * si + mi * sr
    return xr + pr * cr - pi * ci, xi + pr * ci + pi * cr


def _ssm_fwd(ub, uf, bbr, bbi, ar, ai, ccr, cci, dsk, hosted=None):
    s = ub.shape[0]
    tm = min(SSM_ROWS, s)
    nt = tm // SUBLANES

    def body(ub_ref, u_ref, bbr_ref, bbi_ref, ar_ref, ai_ref, ccr_ref, cci_ref, dsk_ref,
             xr_ref, xi_ref, y_ref, cr_s, ci_s):
        i = pl.program_id(1)

        @pl.when(i == 0)
        def _():
            cr_s[...] = jnp.zeros_like(cr_s)
            ci_s[...] = jnp.zeros_like(ci_s)

        u_b = ub_ref[...]
        xr_ref[...] = _dot(u_b, bbr_ref[0])
        xi_ref[...] = _dot(u_b, bbi_ref[0])
        consts = _scan_consts(ar_ref[0], ai_ref[0], CHUNK_S, False)

        def tile(k, carry):
            cr, ci = carry
            sl = pl.ds(pl.multiple_of(k * SUBLANES, SUBLANES), SUBLANES)
            xr, xi = _scan_tile(xr_ref[sl, :], xi_ref[sl, :], cr, ci, consts, False)
            xr_ref[sl, :] = xr
            xi_ref[sl, :] = xi
            return xr[SUBLANES - 1:SUBLANES, :], xi[SUBLANES - 1:SUBLANES, :]

        cr, ci = lax.fori_loop(0, nt, tile, (cr_s[...], ci_s[...]))
        cr_s[...] = cr
        ci_s[...] = ci
        y_ref[...] = (_dot(xr_ref[...].astype(BF16), ccr_ref[0]) - _dot(xi_ref[...].astype(BF16), cci_ref[0])
                      + dsk_ref[...] * u_ref[...])

    wspec = lambda a, b: pl.BlockSpec((1, a, b), lambda j, i: (j, 0, 0))
    nb = s // tm
    first = lambda: jnp.logical_and(pl.program_id(0) == 0, pl.program_id(1) == 0)
    last = lambda: jnp.logical_and(pl.program_id(0) == SSM_CHUNKS - 1, pl.program_id(1) == nb - 1)
    return _host_pcall(
        body, hosted, first, last, n_in=9, n_out=3, n_scratch=2, name="ssm_fwd", grid=(SSM_CHUNKS, nb),
        in_specs=[pl.BlockSpec((tm, CHUNK_U), lambda j, i: (i, j)),
                  pl.BlockSpec((tm, CHUNK_U), lambda j, i: (i, j)),
                  wspec(CHUNK_U, CHUNK_S), wspec(CHUNK_U, CHUNK_S), wspec(1, CHUNK_S), wspec(1, CHUNK_S),
                  wspec(CHUNK_S, CHUNK_U), wspec(CHUNK_S, CHUNK_U),
                  pl.BlockSpec((1, CHUNK_U), lambda j, i: (0, j))],
        out_specs=[pl.BlockSpec((tm, CHUNK_S), lambda j, i: (i, j)), pl.BlockSpec((tm, CHUNK_S), lambda j, i: (i, j)),
                   pl.BlockSpec((tm, CHUNK_U), lambda j, i: (i, j))],
        out_shape=[_sds((s, N_STATE)), _sds((s, N_STATE)), _sds((s, SSM_W))],
        scratch_shapes=[pltpu.VMEM((1, CHUNK_S), F32)] * 2,
        dims=("parallel", "arbitrary"), operands=(ub, uf, bbr, bbi, ar, ai, ccr, cci, dsk))


def _ssm_glu(y, w_glu, b_glu):
    ge = _gelu(y)
    sg = _sigmoid(_dot(ge.astype(BF16), w_glu) + b_glu)
    return ge, sg


def _mix_out(y, att, x, w_glu, b_glu, g_att, g_ssm, w_out, g_ffn, hosted=None):
    s = x.shape[0]
    tm = _wide_tile(s)

    def body(y_ref, att_ref, x_ref, wg_ref, bg_ref, ga_ref, gs_ref, wo_ref, gf_ref, x1_ref, mix_ref, h2_ref):
        ge, sg = _ssm_glu(y_ref[...], wg_ref[...], bg_ref[...])
        ms = _rms(ge * sg, gs_ref[...]).astype(BF16)
        ma = _rms(_merge_heads(att_ref), ga_ref[...]).astype(BF16)
        mix_ref[:, 0:ATTN_W] = ma
        mix_ref[:, ATTN_W:D_MODEL] = ms
        x1 = x_ref[...] + (_dot(ma, wo_ref[0:ATTN_W, :]) + _dot(ms, wo_ref[ATTN_W:D_MODEL, :]))
        x1_ref[...] = x1
        h2_ref[...] = _rms(x1, gf_ref[...]).astype(BF16)

    row = lambda w: pl.BlockSpec((tm, w), lambda i: (i, 0))
    const = lambda a, b: pl.BlockSpec((a, b), lambda i: (0, 0))
    nb = s // tm
    return _host_pcall(body, hosted, lambda: pl.program_id(0) == 0, lambda: pl.program_id(0) == nb - 1,
                       n_in=9, n_out=3, n_scratch=0, name="mix_out", grid=(nb,),
                       in_specs=[row(SSM_W), pl.BlockSpec((HEADS, tm, HEAD_DIM), lambda i: (0, i, 0)), row(D_MODEL),
                                 const(SSM_W, SSM_W), const(1, SSM_W),
                                 const(1, ATTN_W), const(1, SSM_W), const(D_MODEL, D_MODEL), const(1, D_MODEL)],
                       out_specs=[row(D_MODEL)] * 3,
                       out_shape=[_sds((s, D_MODEL)), _sds((s, D_MODEL), BF16), _sds((s, D_MODEL), BF16)],
                       scratch_shapes=[], dims=("parallel",), operands=(y, att, x, w_glu, b_glu, g_att, g_ssm, w_out, g_ffn))


CONV_CHUNK = 64


def _conv_rows(pad_ref, w, b, r0, n):
    y = b + pad_ref[pl.ds(r0 + SUBLANES - 2, n), :] * w[0:1, :]
    y = y + pad_ref[pl.ds(r0 + SUBLANES - 1, n), :] * w[1:2, :]
    return y + pad_ref[pl.ds(r0 + SUBLANES, n), :] * w[2:3, :]


def _fill_front_pad(pad_ref, strip_ref, s):
    pad_ref[0:SUBLANES, :] = jnp.zeros((SUBLANES, STRIP), F32)
    for r0 in range(0, s, CONV_CHUNK):
        pad_ref[pl.ds(SUBLANES + r0, CONV_CHUNK), :] = strip_ref[pl.ds(r0, CONV_CHUNK), :]


def _conv_act(up, conv_w, conv_b):
    s = up.shape[0]

    def body(ug_ref, uv_ref, wg_ref, wv_ref, bg_ref, bv_ref, act_ref, pg_ref, pv_ref):
        _fill_front_pad(pg_ref, ug_ref, s)
        _fill_front_pad(pv_ref, uv_ref, s)
        wg, wv, bg, bv = wg_ref[...], wv_ref[...], bg_ref[...], bv_ref[...]
        for r0 in range(0, s, CONV_CHUNK):
            hg = _conv_rows(pg_ref, wg, bg, r0, CONV_CHUNK)
            hv = _conv_rows(pv_ref, wv, bv, r0, CONV_CHUNK)
            act_ref[pl.ds(r0, CONV_CHUNK), :] = (hg * _sigmoid(hg) * hv).astype(BF16)

    strip = lambda off: pl.BlockSpec((s, STRIP), lambda j: (0, j + off))
    wsp = lambda off: pl.BlockSpec((3, STRIP), lambda j: (0, j + off))
    bsp = lambda off: pl.BlockSpec((1, STRIP), lambda j: (0, j + off))
    return _pcall(body, name="conv_act", grid=(N_STRIPS,),
                  in_specs=[strip(0), strip(N_STRIPS), wsp(0), wsp(N_STRIPS), bsp(0), bsp(N_STRIPS)],
                  out_specs=pl.BlockSpec((s, STRIP), lambda j: (0, j)), out_shape=_sds((s, D_FF), BF16),
                  scratch_shapes=[pltpu.VMEM((s + SUBLANES, STRIP), F32)] * 2,
                  dims=("parallel",))(up, up, conv_w, conv_w, conv_b, conv_b)


def _down_loss(act, w_down, x1, tgt):
    s = x1.shape[0]
    tm = _wide_tile(s)

    def body(a_ref, w_ref, x1_ref, t_ref, dy_ref, dyb_ref, loss_ref):
        i = pl.program_id(0)

        @pl.when(i == 0)
        def _():
            loss_ref[...] = jnp.zeros_like(loss_ref)

        diff = x1_ref[...] + _dot(a_ref[...], w_ref[...]) - t_ref[...]
        dy = diff * (1.0 / D_MODEL)
        dy_ref[...] = dy
        dyb_ref[...] = dy.astype(BF16)
        loss_ref[...] += 0.5 * jnp.sum(diff * dy)

    row = lambda w: pl.BlockSpec((tm, w), lambda i: (i, 0))
    return _pcall(body, name="down_loss", grid=(s // tm,),
                  in_specs=[row(D_FF), pl.BlockSpec((D_FF, D_MODEL), lambda i: (0, 0)), row(D_MODEL), row(D_MODEL)],
                  out_specs=[row(D_MODEL), row(D_MODEL), pl.BlockSpec((SUBLANES, LANES), lambda i: (0, 0))],
                  out_shape=[_sds((s, D_MODEL)), _sds((s, D_MODEL), BF16), _sds((SUBLANES, LANES))],
                  dims=("arbitrary",))(act, w_down, x1, tgt)


def _conv_act_bwd(up, dact, conv_w, conv_b):
    s = up.shape[0]
    ch = CONV_CHUNK

    def body(ug_ref, uv_ref, da_ref, wg_ref, wv_ref, bg_ref, bv_ref, dup_ref, dcw_ref, pg_ref, pv_ref, dg_ref, dv_ref):
        _fill_front_pad(pg_ref, ug_ref, s)
        _fill_front_pad(pv_ref, uv_ref, s)
        zero = jnp.zeros((SUBLANES, STRIP), F32)
        dg_ref[pl.ds(s, SUBLANES), :] = zero
        dv_ref[pl.ds(s, SUBLANES), :] = zero
        wg, wv, bg, bv = wg_ref[...], wv_ref[...], bg_ref[...], bv_ref[...]
        tile_sum = lambda t: jnp.sum(t.reshape(ch // SUBLANES, SUBLANES, STRIP), axis=0)
        accs = [[zero] * 4, [zero] * 4]
        for r0 in range(0, s, ch):
            hg = _conv_rows(pg_ref, wg, bg, r0, ch)
            hv = _conv_rows(pv_ref, wv, bv, r0, ch)
            sg = _sigmoid(hg)
            da = da_ref[pl.ds(r0, ch), :]
            dhs = (da * hv * (sg * (1.0 + hg * (1.0 - sg))), da * (hg * sg))
            for half, (dh, d_ref, p_ref) in enumerate(zip(dhs, (dg_ref, dv_ref), (pg_ref, pv_ref))):
                d_ref[pl.ds(r0, ch), :] = dh
                for k in range(3):
                    accs[half][k] = accs[half][k] + tile_sum(dh * p_ref[pl.ds(r0 + SUBLANES - 2 + k, ch), :])
                accs[half][3] = accs[half][3] + tile_sum(dh)
        for half, (d_ref, w) in enumerate(((dg_ref, wg), (dv_ref, wv))):
            for r0 in range(0, s, ch):
                dup = (d_ref[pl.ds(r0, ch), :] * w[2:3, :] + d_ref[pl.ds(r0 + 1, ch), :] * w[1:2, :]
                       + d_ref[pl.ds(r0 + 2, ch), :] * w[0:1, :])
                dup_ref[half, pl.ds(r0, ch), :] = dup.astype(BF16)
            rid = lax.broadcasted_iota(jnp.int32, (SUBLANES, STRIP), 0)
            out = zero
            for k in range(4):
                out = jnp.where(rid == k, jnp.sum(accs[half][k], axis=0, keepdims=True), out)
            dcw_ref[half] = out

    strip = lambda off: pl.BlockSpec((s, STRIP), lambda j: (0, j + off))
    wsp = lambda off: pl.BlockSpec((3, STRIP), lambda j: (0, j + off))
    bsp = lambda off: pl.BlockSpec((1, STRIP), lambda j: (0, j + off))
    return _pcall(body, name="conv_act_bwd", grid=(N_STRIPS,),
                  in_specs=[strip(0), strip(N_STRIPS), strip(0), wsp(0), wsp(N_STRIPS), bsp(0), bsp(N_STRIPS)],
                  out_specs=[pl.BlockSpec((2, s, STRIP), lambda j: (0, 0, j)), pl.BlockSpec((2, SUBLANES, STRIP), lambda j: (0, 0, j))],
                  out_shape=[_sds((2, s, D_FF), BF16), _sds((2, SUBLANES, D_FF))],
                  scratch_shapes=[pltpu.VMEM((s + SUBLANES, STRIP), F32)] * 4,
                  dims=("parallel",))(up, up, dact, conv_w, conv_w, conv_b, conv_b)


def _mix_bwd(dy, dh2, x1, g_ffn, w_out, y, att, w_glu, b_glu, g_att, g_ssm, hosted=None):
    s = dy.shape[0]
    tm = _wide_tile(s)

    def body(dy_ref, dh2_ref, x1_ref, gf_ref, wo_ref, y_ref, att_ref, wg_ref, bg_ref, ga_ref, gs_ref,
             dx1_ref, dx1b_ref, datt_ref, dys_ref, dwg_ref, dgf_ref, dga_ref, dgs_ref, dbg_ref):
        i = pl.program_id(0)

        @pl.when(i == 0)
        def _():
            for r in (dwg_ref, dgf_ref, dga_ref, dgs_ref, dbg_ref):
                r[...] = jnp.zeros_like(r)

        dxn, dgf = _rms_bwd(x1_ref[...], gf_ref[...], dh2_ref[...])
        dx1 = dy_ref[...] + dxn
        dx1_ref[...] = dx1
        dx1b = dx1.astype(BF16)
        dx1b_ref[...] = dx1b
        dgf_ref[...] += dgf
        dma = _dot_nt(dx1b, wo_ref[0:ATTN_W, :])
        dms = _dot_nt(dx1b, wo_ref[ATTN_W:D_MODEL, :])
        datt, dga = _rms_bwd(_merge_heads(att_ref), ga_ref[...], dma)
        _split_heads(datt_ref, datt)
        dga_ref[...] += dga
        yv = y_ref[...]
        ge, sg = _ssm_glu(yv, wg_ref[...], bg_ref[...])
        dssm, dgs = _rms_bwd(ge * sg, gs_ref[...], dms)
        dgs_ref[...] += dgs
        dgl = dssm * ge * sg * (1.0 - sg)
        dglb = dgl.astype(BF16)
        dge = dssm * sg + _dot_nt(dglb, wg_ref[...])
        dbg_ref[...] += jnp.sum(dgl, axis=0, keepdims=True)
        dwg_ref[...] += _dot_tn(ge.astype(BF16), dglb)
        dys_ref[...] = dge * _gelu_grad(yv)

    row = lambda w: pl.BlockSpec((tm, w), lambda i: (i, 0))
    const = lambda a, b: pl.BlockSpec((a, b), lambda i: (0, 0))
    heads = pl.BlockSpec((HEADS, tm, HEAD_DIM), lambda i: (0, i, 0))
    nb = s // tm
    return _host_pcall(
        body, hosted, lambda: pl.program_id(0) == 0, lambda: pl.program_id(0) == nb - 1, n_in=11, n_out=9, n_scratch=0,
        name="mix_bwd", grid=(nb,),
        in_specs=[row(D_MODEL), row(D_MODEL), row(D_MODEL), const(1, D_MODEL), const(D_MODEL, D_MODEL), row(SSM_W),
                  heads, const(SSM_W, SSM_W), const(1, SSM_W), const(1, ATTN_W), const(1, SSM_W)],
        out_specs=[row(D_MODEL), row(D_MODEL), heads, row(SSM_W), const(SSM_W, SSM_W), const(1, D_MODEL),
                   const(1, ATTN_W), const(1, SSM_W), const(1, SSM_W)],
        out_shape=[_sds((s, D_MODEL)), _sds((s, D_MODEL), BF16), _sds((HEADS, s, HEAD_DIM)), _sds((s, SSM_W)),
                   _sds((SSM_W, SSM_W)), _sds((1, D_MODEL)), _sds((1, ATTN_W)), _sds((1, SSM_W)), _sds((1, SSM_W))],
        scratch_shapes=[], dims=("arbitrary",), operands=(dy, dh2, x1, g_ffn, w_out, y, att, w_glu, b_glu, g_att, g_ssm))


def _ssm_bwd(dys, uf, ub, xr, xi, bbr, bbi, ar, ai, ccr, cci, dsk, hosted=None):
    s = dys.shape[0]
    tm = min(SSM_ROWS, s)
    nb = s // tm
    nt = tm // SUBLANES

    def body(dy_ref, u_ref, ub_ref, xr_ref, xi_ref, xrp_ref, xip_ref, bbr_ref, bbi_ref, ar_ref, ai_ref, ccr_ref,
             cci_ref, dsk_ref, du_ref, dbbr_ref, dbbi_ref, dccr_ref, dcci_ref, dar_ref, dai_ref, dd_ref,
             gr_s, gi_s, cr_s, ci_s, accr_s, acci_s):
        i = pl.program_id(1)
        first_block = i == nb - 1

        @pl.when(i == 0)
        def _():
            for r in (cr_s, ci_s, accr_s, acci_s, dbbr_ref, dbbi_ref, dccr_ref, dcci_ref, dd_ref):
                r[...] = jnp.zeros_like(r)

        dy = dy_ref[...]
        dyb = dy.astype(BF16)
        gr_s[...] = _dot_nt(dyb, ccr_ref[0])
        gi_s[...] = -_dot_nt(dyb, cci_ref[0])
        consts = _scan_consts(ar_ref[0], -ai_ref[0], CHUNK_S, True)
        row = lax.broadcasted_iota(jnp.int32, (SUBLANES, CHUNK_S), 0)

        def tile(kk, carry):
            cr, ci, accr, acci = carry
            k = nt - 1 - kk
            sl = pl.ds(pl.multiple_of(k * SUBLANES, SUBLANES), SUBLANES)
            gr, gi = _scan_tile(gr_s[sl, :], gi_s[sl, :], cr, ci, consts, True)
            gr_s[sl, :] = gr
            gi_s[sl, :] = gi
            slp = pl.ds(pl.multiple_of(jnp.maximum(k - 1, 0) * SUBLANES, SUBLANES), SUBLANES)
            inner = k > 0
            pr_t = jnp.where(inner, xr_ref[slp, :], xrp_ref[...])
            pi_t = jnp.where(inner, xi_ref[slp, :], xip_ref[...])
            live = jnp.logical_or(inner, jnp.logical_not(first_block))
            top_r = jnp.where(live, pltpu.roll(pr_t, 1, 0), 0.0)
            top_i = jnp.where(live, pltpu.roll(pi_t, 1, 0), 0.0)
            xpr = jnp.where(row == 0, top_r, pltpu.roll(xr_ref[sl, :], 1, 0))
            xpi = jnp.where(row == 0, top_i, pltpu.roll(xi_ref[sl, :], 1, 0))
            accr = accr + gr * xpr + gi * xpi
            acci = acci + gi * xpr - gr * xpi
            return gr[0:1, :], gi[0:1, :], accr, acci

        zeros = jnp.zeros((SUBLANES, CHUNK_S), F32)
        cr, ci, accr, acci = lax.fori_loop(0, nt, tile, (cr_s[...], ci_s[...], zeros, zeros))
        cr_s[...] = cr
        ci_s[...] = ci
        accr_s[...] += accr
        acci_s[...] += acci
        grb = gr_s[...].astype(BF16)
        gib = gi_s[...].astype(BF16)
        u_b = ub_ref[...]
        du_ref[...] = _dot_nt(grb, bbr_ref[0]) + _dot_nt(gib, bbi_ref[0]) + dsk_ref[...] * dy
        dbbr_ref[0] += _dot_tn(u_b, grb)
        dbbi_ref[0] += _dot_tn(u_b, gib)
        dccr_ref[0] += _dot_tn(xr_ref[...].astype(BF16), dyb)
        dcci_ref[0] -= _dot_tn(xi_ref[...].astype(BF16), dyb)
        dd_ref[...] += jnp.sum(dy * u_ref[...], axis=0, keepdims=True)

        @pl.when(i == nb - 1)
        def _():
            dar_ref[0] = jnp.sum(accr_s[...], axis=0, keepdims=True)
            dai_ref[0] = jnp.sum(acci_s[...], axis=0, keepdims=True)

    tiles_per_block = tm // SUBLANES
    rb = lambda i: nb - 1 - i
    wspec = lambda a, b: pl.BlockSpec((1, a, b), lambda j, i: (j, 0, 0))
    xblk = pl.BlockSpec((tm, CHUNK_S), lambda j, i: (rb(i), j))
    xprev = pl.BlockSpec((SUBLANES, CHUNK_S), lambda j, i: (jnp.maximum(rb(i) * tiles_per_block - 1, 0), j))
    ublk = pl.BlockSpec((tm, CHUNK_U), lambda j, i: (rb(i), j))
    first = lambda: jnp.logical_and(pl.program_id(0) == 0, pl.program_id(1) == 0)
    last = lambda: jnp.logical_and(pl.program_id(0) == SSM_CHUNKS - 1, pl.program_id(1) == nb - 1)
    return _host_pcall(
        body, hosted, first, last, n_in=14, n_out=8, n_scratch=6, name="ssm_bwd", grid=(SSM_CHUNKS, nb),
        in_specs=[ublk, ublk, ublk, xblk, xblk, xprev, xprev,
                  wspec(CHUNK_U, CHUNK_S), wspec(CHUNK_U, CHUNK_S), wspec(1, CHUNK_S), wspec(1, CHUNK_S),
                  wspec(CHUNK_S, CHUNK_U), wspec(CHUNK_S, CHUNK_U), pl.BlockSpec((1, CHUNK_U), lambda j, i: (0, j))],
        out_specs=[ublk, wspec(CHUNK_U, CHUNK_S), wspec(CHUNK_U, CHUNK_S), wspec(CHUNK_S, CHUNK_U),
                   wspec(CHUNK_S, CHUNK_U), wspec(1, CHUNK_S), wspec(1, CHUNK_S),
                   pl.BlockSpec((1, CHUNK_U), lambda j, i: (0, j))],
        out_shape=[_sds((s, SSM_W)), _sds((SSM_CHUNKS, CHUNK_U, CHUNK_S)), _sds((SSM_CHUNKS, CHUNK_U, CHUNK_S)),
                   _sds((SSM_CHUNKS, CHUNK_S, CHUNK_U)), _sds((SSM_CHUNKS, CHUNK_S, CHUNK_U)),
                   _sds((SSM_CHUNKS, 1, CHUNK_S)), _sds((SSM_CHUNKS, 1, CHUNK_S)), _sds((1, SSM_W))],
        scratch_shapes=[pltpu.VMEM((tm, CHUNK_S), F32)] * 2 + [pltpu.VMEM((1, CHUNK_S), F32)] * 2
                       + [pltpu.VMEM((SUBLANES, CHUNK_S), F32)] * 2,
        dims=("parallel", "arbitrary"), operands=(dys, uf, ub, xr, xi, xr, xi, bbr, bbi, ar, ai, ccr, cci, dsk))


def _attn_probs(q, ks, cs, lse, scale, diagonal):
    p = jnp.exp(_dot_nt(q, ks) * scale - cs - lse)
    if diagonal:
        tq, tk = p.shape
        causal = lax.broadcasted_iota(jnp.int32, (tq, tk), 1) <= lax.broadcasted_iota(jnp.int32, (tq, tk), 0)
        p = jnp.where(causal, p, 0.0)
    return p


def _attn_bwd(qh, kh, vh, crow, lse, doh, hosted=None):
    _, s, _ = qh.shape
    tq = _row_tile(s)
    nq = s // tq
    scale = HEAD_DIM ** -0.5
    hp = HEADS_PER_STEP

    def body(q_ref, k_ref, v_ref, c_ref, lse_ref, do_ref, dq_ref, dk_ref, dv_ref, dc_ref, p_s, dp_s):
        i = pl.program_id(1)

        @pl.when(i == 0)
        def _():
            for r in (dk_ref, dv_ref, dc_ref):
                r[...] = jnp.zeros_like(r)

        dobs = [do_ref[hh].astype(BF16) for hh in range(hp)]

        def first(j, dls, diagonal):
            off = pl.multiple_of(j * tq, tq)
            out = []
            for hh in range(hp):
                p = _attn_probs(q_ref[hh], k_ref[hh, pl.ds(off, tq), :], c_ref[hh, :, pl.ds(off, tq)], lse_ref[hh],
                                scale, diagonal)
                dp = _dot_nt(dobs[hh], v_ref[hh, pl.ds(off, tq), :])
                p_s[hh, j] = p
                dp_s[hh, j] = dp
                out.append(dls[hh] + jnp.sum(p * dp, axis=-1, keepdims=True))
            return tuple(out)

        zero_col = jnp.zeros((tq, 1), F32)
        dls = lax.fori_loop(0, i, lambda j, c: first(j, c, False), (zero_col,) * hp)
        dls = first(i, dls, True)

        def second(j, dqs):
            rows = pl.ds(pl.multiple_of(j * tq, tq), tq)
            out = []
            for hh in range(hp):
                p = p_s[hh, j]
                ds = p * (dp_s[hh, j] - dls[hh])
                dsb = ds.astype(BF16)
                dv_ref[hh, rows, :] += _dot_tn(p.astype(BF16), dobs[hh])
                dk_ref[hh, rows, :] += _dot_tn(dsb, q_ref[hh]) * scale
                dc_ref[hh, :, rows] -= jnp.sum(ds, axis=0, keepdims=True)
                out.append(dqs[hh] + _dot(dsb, k_ref[hh, rows, :]))
            return tuple(out)

        dqs = lax.fori_loop(0, i + 1, second, (jnp.zeros((tq, HEAD_DIM), F32),) * hp)
        for hh in range(hp):
            dq_ref[hh] = dqs[hh] * scale

    blk = pl.BlockSpec((hp, tq, HEAD_DIM), lambda h, i: (h, i, 0))
    full = pl.BlockSpec((hp, s, HEAD_DIM), lambda h, i: (h, 0, 0))
    crow_spec = pl.BlockSpec((hp, 1, s), lambda h, i: (h, 0, 0))
    nh = HEADS // hp
    first = lambda: jnp.logical_and(pl.program_id(0) == 0, pl.program_id(1) == 0)
    last = lambda: jnp.logical_and(pl.program_id(0) == nh - 1, pl.program_id(1) == nq - 1)
    return _host_pcall(body, hosted, first, last, n_in=6, n_out=4, n_scratch=2, name="attn_bwd", grid=(nh, nq),
                       in_specs=[blk, full, full, crow_spec, pl.BlockSpec((hp, tq, 1), lambda h, i: (h, i, 0)), blk],
                       out_specs=[blk, full, full, crow_spec],
                       out_shape=[_sds((HEADS, s, HEAD_DIM))] * 3 + [_sds((HEADS, 1, s))],
                       scratch_shapes=[pltpu.VMEM((hp, nq, tq, tq), F32)] * 2,
                       dims=("parallel", "arbitrary"), operands=(qh, kh, vh, crow, lse, doh))


def _prep_bwd(z, dqn, dkn, dv, du, dc, gq, gk, bf, gg, hosted=None):
    s = z.shape[0]
    tm = _row_tile(s)
    nb = s // tm

    def body(z_ref, dqn_ref, dkn_ref, dv_ref, du_ref, dc_ref, gq_ref, gk_ref, bf_ref, gg_ref,
             dz_ref, dgq_ref, dgk_ref, dbf_ref, carry_ref):
        i = pl.program_id(0)

        @pl.when(i == 0)
        def _():
            for r in (dgq_ref, dgk_ref, dbf_ref, carry_ref):
                r[...] = jnp.zeros_like(r)

        gg_m = gg_ref[...]

        def head_norm_bwd(t, g, dn):
            r = lax.rsqrt(_dot_hi_lo(t * t, gg_m) * (1.0 / HEAD_DIM) + EPS)
            w = dn * g
            mean_wt = _dot_hi_lo(w * t, gg_m) * (1.0 / HEAD_DIM)
            return r * w - t * (r * r * r) * mean_wt, jnp.sum(dn * t * r, axis=0, keepdims=True)

        dq, dgq = head_norm_bwd(z_ref[:, 0:ATTN_W], gq_ref[...], _merge_heads(dqn_ref))
        dk, dgk = head_norm_bwd(z_ref[:, ATTN_W:2 * ATTN_W], gk_ref[...], _merge_heads(dkn_ref))
        dgq_ref[...] += dgq
        dgk_ref[...] += dgk
        row = lax.broadcasted_iota(jnp.int32, (tm, tm), 0)
        col = lax.broadcasted_iota(jnp.int32, (tm, tm), 1)
        triu = (col >= row).astype(BF16)
        dlf = _dot_exact_l(triu, dc_ref[...]) + carry_ref[...]
        carry_ref[...] = dlf[0:1, :]
        df = dlf * _sigmoid(-_forget_logits(z_ref, bf_ref))
        dbf_ref[...] += jnp.sum(df, axis=0, keepdims=True)
        dz_ref[:, 0:ATTN_W] = dq.astype(BF16)
        dz_ref[:, ATTN_W:2 * ATTN_W] = dk.astype(BF16)
        dz_ref[:, 2 * ATTN_W:3 * ATTN_W] = _merge_heads(dv_ref).astype(BF16)
        tail = jnp.concatenate([df[:, :HEADS], du_ref[...], jnp.zeros((tm, Z_COLS - IN_COLS), F32)], axis=-1)
        dz_ref[:, F_COL0:Z_COLS] = tail.astype(BF16)

    row_spec = lambda w: pl.BlockSpec((tm, w), lambda i: (nb - 1 - i, 0))
    const = lambda shape: pl.BlockSpec(shape, lambda i: (0, 0))
    return _host_pcall(
        body, hosted, lambda: pl.program_id(0) == 0, lambda: pl.program_id(0) == nb - 1, n_in=10, n_out=4, n_scratch=1,
        name="prep_bwd", grid=(nb,),
        in_specs=[row_spec(Z_COLS)] + [pl.BlockSpec((HEADS, tm, HEAD_DIM), lambda i: (0, nb - 1 - i, 0))] * 3
                 + [row_spec(ATTN_W), row_spec(LANES), const((1, ATTN_W)),
                    const((1, ATTN_W)), const((1, LANES)), const((ATTN_W, ATTN_W))],
        out_specs=[row_spec(Z_COLS), const((1, ATTN_W)), const((1, ATTN_W)), const((1, LANES))],
        out_shape=[_sds((s, Z_COLS), BF16), _sds((1, ATTN_W)), _sds((1, ATTN_W)), _sds((1, LANES))],
        scratch_shapes=[pltpu.VMEM((1, LANES), F32)], dims=("arbitrary",),
        operands=(z, dqn, dkn, dv, du, dc, gq, gk, bf, gg))


def _in_norm_bwd(x, g_mix, dh, dx1, hosted=None):
    s = x.shape[0]
    tm = _wide_tile(s)

    def body(x_ref, g_ref, dh_ref, dx1_ref, dx_ref, dg_ref):
        i = pl.program_id(0)

        @pl.when(i == 0)
        def _():
            dg_ref[...] = jnp.zeros_like(dg_ref)

        dxn, dg = _rms_bwd(x_ref[...], g_ref[...], dh_ref[...])
        dx_ref[...] = dx1_ref[...] + dxn
        dg_ref[...] += dg

    row = pl.BlockSpec((tm, D_MODEL), lambda i: (i, 0))
    vec = pl.BlockSpec((1, D_MODEL), lambda i: (0, 0))
    nb = s // tm
    return _host_pcall(body, hosted, lambda: pl.program_id(0) == 0, lambda: pl.program_id(0) == nb - 1,
                       n_in=4, n_out=2, n_scratch=0, name="in_norm_bwd", grid=(nb,), in_specs=[row, vec, row, row],
                       out_specs=[row, vec], out_shape=[_sds((s, D_MODEL)), _sds((1, D_MODEL))], scratch_shapes=[],
                       dims=("arbitrary",), operands=(x, g_mix, dh, dx1))


def _adamw_refs(w_ref, g_ref, m_ref, v_ref, d_ref, mo_ref, vo_ref):
    gv = g_ref[...]
    mn = ADAM_B1 * m_ref[...] + (1.0 - ADAM_B1) * gv
    vn = ADAM_B2 * v_ref[...] + (1.0 - ADAM_B2) * (gv * gv)
    m_hat = mn / (1.0 - ADAM_B1 ** ADAM_STEP)
    v_hat = vn / (1.0 - ADAM_B2 ** ADAM_STEP)
    d_ref[...] = -ADAM_LR * (m_hat / (jnp.sqrt(v_hat) + ADAM_EPS) + ADAM_WD * w_ref[...])
    mo_ref[...] = mn
    vo_ref[...] = vn


def _adamw_small(ws, gs, ms, vs):
    n = len(ws)

    def body(*refs):
        ins, outs = refs[:4 * n], refs[4 * n:]
        for i in range(n):
            _adamw_refs(ins[i], ins[n + i], ins[2 * n + i], ins[3 * n + i], *outs[3 * i:3 * i + 3])

    vm = pl.BlockSpec(memory_space=pltpu.VMEM)
    out_shape = [_sds(w.shape) for w in ws for _ in range(3)]
    return _pallas(body, name="adamw_small", in_specs=[vm] * (4 * n), out_specs=[vm] * (3 * n), out_shape=out_shape,
                   compiler_params=pltpu.CompilerParams(vmem_limit_bytes=VMEM_LIMIT))(*ws, *gs, *ms, *vs)


def _adamw(w, g, m, v, *, name):
    r, c = w.shape
    tr = r
    for cand in (256, 176, 128, 64):
        if r > cand and r % cand == 0:
            tr = cand
            break

    def body(w_ref, g_ref, m_ref, v_ref, d_ref, mo_ref, vo_ref):
        _adamw_refs(w_ref, g_ref, m_ref, v_ref, d_ref, mo_ref, vo_ref)

    spec = pl.BlockSpec((tr, c), lambda i: (i, 0))
    return _pcall(body, name=name, grid=(r // tr,), in_specs=[spec] * 4, out_specs=[spec] * 3,
                  out_shape=[_sds((r, c))] * 3, dims=("parallel",))(w, g, m, v)


def _prefetch_call(body, *, name, grid, in_specs, out_specs, out_shape, operands):
    grid_spec = pltpu.PrefetchScalarGridSpec(num_scalar_prefetch=1, grid=grid, in_specs=in_specs, out_specs=out_specs)
    params = pltpu.CompilerParams(dimension_semantics=("parallel",) * len(grid), vmem_limit_bytes=VMEM_LIMIT)
    return _pallas(body, name=name, grid_spec=grid_spec, out_shape=out_shape, compiler_params=params)(*operands)


def _place_cols(buf, shard, place):
    rows, cols = shard.shape
    tr = 256

    def body(place_ref, s_ref, b_ref, o_ref):
        o_ref[...] = s_ref[...]

    grid_spec = pltpu.PrefetchScalarGridSpec(
        num_scalar_prefetch=1, grid=(rows // tr,),
        in_specs=[pl.BlockSpec((tr, cols), lambda i, p: (i, 0)), pl.BlockSpec(memory_space=pltpu.HBM)],
        out_specs=pl.BlockSpec((tr, cols), lambda i, p: (i, p[0])))
    return _pallas(body, name="place_own_cols", grid_spec=grid_spec, out_shape=_sds(buf.shape, buf.dtype),
                   input_output_aliases={2: 0},
                   compiler_params=pltpu.CompilerParams(dimension_semantics=("parallel",),
                                                        vmem_limit_bytes=VMEM_LIMIT))(place, shard, buf)


def _half_rows_tile(hr):
    return hr if hr <= 256 else 176 if hr % 176 == 0 else 256


def _add_half(g, landed, place, *, name):
    def body(place_ref, g_ref, l_ref, o_ref):
        own = g_ref[0] if len(g_ref.shape) == 4 else g_ref[...]
        o_ref[...] = (own + l_ref[...]).astype(BF16)

    if g.ndim == 4:
        _, _, hr, c = g.shape
        tr = _half_rows_tile(hr)
        blk = (1, tr, c)
        return _prefetch_call(
            body, name=name, grid=(N_CHIPS, hr // tr),
            in_specs=[pl.BlockSpec((1,) + blk, lambda j, i, p: (j, p[1], i, 0)), pl.BlockSpec(blk, lambda j, i, p: (j, i, 0))],
            out_specs=pl.BlockSpec(blk, lambda j, i, p: (j, i, 0)), out_shape=_sds(landed.shape, BF16),
            operands=(place, g, landed))
    hr, c = landed.shape
    tr, tc = 256, _tile(c, 2176)
    nb = hr // tr
    return _prefetch_call(
        body, name=name, grid=(nb, c // tc),
        in_specs=[pl.BlockSpec((tr, tc), lambda i, j, p: (p[1] * nb + i, j)), pl.BlockSpec((tr, tc), lambda i, j, p: (i, j))],
        out_specs=pl.BlockSpec((tr, tc), lambda i, j, p: (i, j)), out_shape=_sds(landed.shape, BF16),
        operands=(place, g, landed))


def _sum_chips(chip_sum, lands, place, *, name, tc, window_stride=0):
    _, hr, c = lands.shape
    tr = _half_rows_tile(hr)
    nb = hr // tr
    ncb = c // tc

    def body(place_ref, own_ref, a_ref, b_ref, c_ref, o_ref):
        own = own_ref[0] if len(own_ref.shape) == 3 else own_ref[...]
        o_ref[...] = ((own.astype(F32) + a_ref[0].astype(F32)) + b_ref[0].astype(F32)) + c_ref[0].astype(F32)

    land = lambda k: pl.BlockSpec((1, tr, tc), lambda i, j, p: ((p[0] + k) % N_CHIPS, i, j))
    if chip_sum.ndim == 3:
        own_spec = land(0)
    else:
        stride = window_stride // tc
        own_spec = pl.BlockSpec((tr, tc), lambda i, j, p: (i, p[0] * stride + j))
    return _prefetch_call(
        body, name=name, grid=(nb, ncb), in_specs=[own_spec, land(1), land(2), land(3)],
        out_specs=pl.BlockSpec((tr, tc), lambda i, j, p: (p[1] * nb + i, j)), out_shape=_sds((2 * hr, c)),
        operands=(place, chip_sum, lands, lands, lands))


_HBM = pl.BlockSpec(memory_space=pltpu.HBM)


def _place():
    x, y, c = lax.axis_index("x"), lax.axis_index("y"), lax.axis_index("c")
    chips = [(1 - x, y), (x, 1 - y), (1 - x, 1 - y)]
    return x, y, c, chips


def _rcopy(src, dst, send_sem, recv_sem, to):
    return pltpu.make_async_remote_copy(src_ref=src, dst_ref=dst, send_sem=send_sem, recv_sem=recv_sem,
                                        device_id=to, device_id_type=MESH)


UP_COLS = 2 * D_FF // N_CHIPS
IN_WINDOW = 640
IN_STRIDE = 512


class _Hosted:
    def __init__(self, operands, out_shapes, n_sems, start, finish, aliases=None, local_sems=0):
        self.operands, self.out_shapes, self.n_sems = list(operands), list(out_shapes), n_sems
        self.start, self.finish, self.aliases, self.local_sems = start, finish, dict(aliases or {}), local_sems

    def scratch(self):
        return ([pltpu.SemaphoreType.DMA((self.n_sems,)), pltpu.SemaphoreType.DMA((self.n_sems,))]
                + [pltpu.SemaphoreType.DMA] * self.local_sems)


def _both(a, b):
    na, nao, nas = len(a.operands), len(a.out_shapes), len(a.scratch())

    def start(ins, outs, sems):
        a.start(ins[:na], outs[:nao], sems[:nas])
        b.start(ins[na:], outs[nao:], sems[nas:])

    def finish(ins, outs, sems):
        a.finish(ins[:na], outs[:nao], sems[:nas])
        b.finish(ins[na:], outs[nao:], sems[nas:])

    both = _Hosted(a.operands + b.operands, a.out_shapes + b.out_shapes, 0, start, finish,
                   aliases={**a.aliases, **{na + i: nao + o for i, o in b.aliases.items()}})
    both.scratch = lambda: a.scratch() + b.scratch()
    return both


def _then(a, b):
    nas = len(a.scratch())

    def finish(ins, outs, sems):
        a.finish(ins, outs, sems[:nas])
        b.start(ins, outs, sems[nas:])
        b.finish(ins, outs, sems[nas:])

    chain = _Hosted(a.operands, a.out_shapes, 0, lambda ins, outs, sems: a.start(ins, outs, sems[:nas]), finish,
                    aliases=a.aliases)
    chain.scratch = lambda: a.scratch() + b.scratch()
    return chain


def _run_hosted(hosted, *, name):
    n_in, n_out = len(hosted.operands), len(hosted.out_shapes)

    def body(*refs):
        parts = (refs[:n_in], refs[n_in:n_in + n_out], refs[n_in + n_out:])
        hosted.start(*parts)
        hosted.finish(*parts)

    return _pallas(body, name=name, in_specs=[_HBM] * n_in, out_specs=[_HBM] * n_out, out_shape=hosted.out_shapes,
                   input_output_aliases=hosted.aliases, scratch_shapes=hosted.scratch())(*hosted.operands)


def _host_pcall(core_body, hosted, first, last, *, n_in, n_out, n_scratch, name, grid, in_specs, out_specs, out_shape,
                scratch_shapes, dims, operands):
    if hosted is None:
        outs = _pcall(core_body, name=name, grid=grid, in_specs=in_specs, out_specs=out_specs, out_shape=out_shape,
                      scratch_shapes=scratch_shapes, dims=dims)(*operands)
        return outs, []
    hi, ho = len(hosted.operands), len(hosted.out_shapes)

    def body(*refs):
        a, b = n_in, n_in + hi
        c, d = b + n_out, b + n_out + ho
        e = d + n_scratch
        parts = (refs[a:b], refs[c:d], refs[e:])

        @pl.when(first())
        def _():
            hosted.start(*parts)

        core_body(*refs[:a], *refs[b:c], *refs[d:e])

        @pl.when(last())
        def _():
            hosted.finish(*parts)

    params = pltpu.CompilerParams(dimension_semantics=("arbitrary",) * len(grid), vmem_limit_bytes=VMEM_LIMIT)
    outs = _pallas(body, name=name, grid=grid, in_specs=list(in_specs) + [_HBM] * hi, out_specs=list(out_specs) + [_HBM] * ho,
                   out_shape=list(out_shape) + hosted.out_shapes, scratch_shapes=list(scratch_shapes) + hosted.scratch(),
                   input_output_aliases={n_in + a: n_out + b for a, b in hosted.aliases.items()},
                   compiler_params=params)(*operands, *hosted.operands)
    return outs[:n_out], outs[n_out:]


WHOLE_HALF = (0, 1, 1)


def _band_rows(src, hc, band):
    first, count, of = band
    hr = src.shape[0] // 2
    return pl.ds(hc * hr + first * (hr // of), count * (hr // of))


def _gather_slot(src, out, chip, hc, band=WHOLE_HALF):
    cols = src.shape[1]
    if len(out.shape) == 2:
        return out.at[_band_rows(src, hc, band), pl.ds(pl.multiple_of(chip * cols, LANES), cols)]
    return out.at[chip, _band_rows(src, hc, band), :]


def _gathered_shape(shard, by_cols):
    if by_cols:
        return _sds((shard.shape[0], N_CHIPS * shard.shape[1]), shard.dtype)
    return _sds((N_CHIPS,) + shard.shape, shard.dtype)


def _plan_gather_ici(shards, by_cols, whole=(), bands=None, into=None):
    n = len(shards)
    bands = bands or [WHOLE_HALF] * n
    into = into or [None] * n
    given = [w for w in range(n) if into[w] is not None]
    n_ops = n + len(whole)

    def copies(ins, outs, sems):
        send_sems, recv_sems = sems[0], sems[1]
        x, y, c, chips = _place()
        me = 2 * x + y
        sends, waits = [], []
        for w in range(n + len(whole)):
            for k, (cx, cy) in enumerate(chips):
                sem = (send_sems.at[3 * w + k], recv_sems.at[3 * w + k])
                if w < n:
                    sends.append(_rcopy(ins[w].at[_band_rows(ins[w], c, bands[w]), :],
                                        _gather_slot(ins[w], outs[w], me, c, bands[w]), *sem, (cx, cy, c)))
                    landed = _gather_slot(ins[w], outs[w], 2 * cx + cy, c, bands[w])
                else:
                    sends.append(_rcopy(ins[w], outs[w].at[me], *sem, (cx, cy, c)))
                    landed = outs[w].at[2 * cx + cy]
                waits.append(_rcopy(landed, landed, *sem, (cx, cy, c)))
        return sends, waits

    def start(ins, outs, sems):
        for cp in copies(ins, outs, sems)[0]:
            cp.start()

    def finish(ins, outs, sems):
        sends, waits = copies(ins, outs, sems)
        for cp in waits:
            cp.wait_recv()
        for cp in sends:
            cp.wait_send()

    out_shapes = [_gathered_shape(s, bc) for s, bc in zip(shards, by_cols)] + [_sds((N_CHIPS,) + a.shape, a.dtype) for a in whole]
    return _Hosted(list(shards) + list(whole) + [into[w] for w in given], out_shapes, 3 * n_ops, start, finish,
                   aliases={n_ops + i: w for i, w in enumerate(given)})


def _plan_gather_d2d(bufs, shard_shapes, bands=None):
    n = len(bufs)
    bands = bands or [WHOLE_HALF] * n

    def copies(ins, outs, sems):
        send_sems, recv_sems = sems
        x, y, c, chips = _place()
        sibling = (x, y, 1 - c)
        sends, waits = [], []
        for w in range(n):
            for k, (cx, cy) in enumerate(chips):
                sem = (send_sems.at[3 * w + k], recv_sems.at[3 * w + k])
                landed = _gather_slot(shard_shapes[w], outs[w], 2 * cx + cy, c, bands[w])
                other = _gather_slot(shard_shapes[w], outs[w], 2 * cx + cy, 1 - c, bands[w])
                sends.append(_rcopy(landed, landed, *sem, sibling))
                waits.append(_rcopy(other, other, *sem, sibling))
        return sends, waits

    def start(ins, outs, sems):
        for cp in copies(ins, outs, sems)[0]:
            cp.start()

    def finish(ins, outs, sems):
        sends, waits = copies(ins, outs, sems)
        for cp in waits:
            cp.wait_recv()
        for cp in sends:
            cp.wait_send()

    return _Hosted(bufs, [_sds(b.shape, b.dtype) for b in bufs], 3 * n, start, finish, aliases={w: w for w in range(n)})


def _plan_allgather_first(block):
    def copies(ins, outs, sems):
        send_sems, recv_sems = sems
        x, y, c, chips = _place()
        me = 4 * x + 2 * y + c
        peers = [(x, y, 1 - c)] + [(cx, cy, c) for cx, cy in chips]
        sends = [_rcopy(ins[0], outs[0].at[me], send_sems.at[k], recv_sems.at[k], p) for k, p in enumerate(peers)]
        waits = [_rcopy(outs[0].at[4 * px + 2 * py + pc], outs[0].at[4 * px + 2 * py + pc], send_sems.at[k],
                        recv_sems.at[k], (px, py, pc)) for k, (px, py, pc) in enumerate(peers)]
        return sends, waits

    def start(ins, outs, sems):
        for cp in copies(ins, outs, sems)[0]:
            cp.start()

    def finish(ins, outs, sems):
        sends, waits = copies(ins, outs, sems)
        for cp in waits:
            cp.wait_recv()
        for cp in sends:
            cp.wait_send()

    return _Hosted([block], [_sds((8,) + block.shape)], 4, start, finish)


def _plan_allgather_second(gathered):
    def copies(ins, outs, sems):
        send_sems, recv_sems = sems
        x, y, c, chips = _place()
        sends, waits = [], []
        for k, (cx, cy) in enumerate(chips):
            landed = outs[0].at[4 * cx + 2 * cy + c]
            other = outs[0].at[4 * cx + 2 * cy + 1 - c]
            sends.append(_rcopy(landed, landed, send_sems.at[k], recv_sems.at[k], (x, y, 1 - c)))
            waits.append(_rcopy(other, other, send_sems.at[k], recv_sems.at[k], (x, y, 1 - c)))
        return sends, waits

    def start(ins, outs, sems):
        for cp in copies(ins, outs, sems)[0]:
            cp.start()

    def finish(ins, outs, sems):
        sends, waits = copies(ins, outs, sems)
        for cp in waits:
            cp.wait_recv()
        for cp in sends:
            cp.wait_send()

    return _Hosted([gathered], [_sds(gathered.shape)], 3, start, finish, aliases={0: 0})


def _place_block(gathered, block, device):
    rows, lanes = block.shape

    def body(dev_ref, b_ref, g_ref, o_ref):
        o_ref[0] = b_ref[...]

    grid_spec = pltpu.PrefetchScalarGridSpec(
        num_scalar_prefetch=1, grid=(1,),
        in_specs=[pl.BlockSpec((rows, lanes), lambda i, d: (0, 0)), pl.BlockSpec(memory_space=pltpu.HBM)],
        out_specs=pl.BlockSpec((1, rows, lanes), lambda i, d: (d[0], 0, 0)))
    return _pallas(body, name="place_own_block", grid_spec=grid_spec, out_shape=_sds(gathered.shape),
                   input_output_aliases={2: 0},
                   compiler_params=pltpu.CompilerParams(dimension_semantics=("arbitrary",),
                                                        vmem_limit_bytes=VMEM_LIMIT))(device, block, gathered)


def _sum_devices(gathered):
    _, rows, lanes = gathered.shape
    tr = rows // 2 if rows % 16 == 0 else rows

    def body(g_ref, o_ref):
        acc = g_ref[0]
        for d in range(1, 8):
            acc = acc + g_ref[d]
        o_ref[...] = acc

    return _pcall(body, name="sum_devices", grid=(rows // tr,), in_specs=[pl.BlockSpec((8, tr, lanes), lambda i: (0, i, 0))],
                  out_specs=pl.BlockSpec((tr, lanes), lambda i: (i, 0)), out_shape=_sds((rows, lanes)), dims=("parallel",))(gathered)


def _plan_swap(grads):
    def copies(ins, outs, sems):
        send_sems, recv_sems = sems
        x, y, c, _ = _place()
        cps = []
        for w, g_ref in enumerate(ins):
            if len(g_ref.shape) == 4:
                theirs = g_ref.at[:, 1 - c]
            else:
                hr = g_ref.shape[0] // 2
                theirs = g_ref.at[pl.ds((1 - c) * hr, hr), :]
            cps.append(_rcopy(theirs, outs[w], send_sems.at[w], recv_sems.at[w], (x, y, 1 - c)))
        return cps

    def start(ins, outs, sems):
        for cp in copies(ins, outs, sems):
            cp.start()

    def finish(ins, outs, sems):
        for cp in copies(ins, outs, sems):
            cp.wait()

    out_shapes = [_sds((g.shape[0], g.shape[2], g.shape[3])) if g.ndim == 4 else _sds((g.shape[0] // 2, g.shape[1]))
                  for g in grads]
    return _Hosted(grads, out_shapes, len(grads), start, finish)


def _plan_scatter(chip_sums, windows):
    def copies(ins, outs, sems):
        send_sems, recv_sems = sems
        x, y, c, chips = _place()
        me = 2 * x + y
        sends, waits = [], []
        for w, s_ref in enumerate(ins):
            for k, (cx, cy) in enumerate(chips):
                tgt = 2 * cx + cy
                if windows[w] is not None:
                    stride, width = windows[w]
                    part = s_ref.at[:, pl.ds(pl.multiple_of(tgt * stride, LANES), width)]
                else:
                    part = s_ref.at[tgt]
                sem = (send_sems.at[3 * w + k], recv_sems.at[3 * w + k])
                sends.append(_rcopy(part, outs[w].at[me], *sem, (cx, cy, c)))
                slot = outs[w].at[tgt]
                waits.append(_rcopy(slot, slot, *sem, (cx, cy, c)))
        return sends, waits

    def start(ins, outs, sems):
        for cp in copies(ins, outs, sems)[0]:
            cp.start()

    def finish(ins, outs, sems):
        sends, waits = copies(ins, outs, sems)
        for cp in waits:
            cp.wait_recv()
        for cp in sends:
            cp.wait_send()

    out_shapes = [_sds((N_CHIPS, s.shape[0], win[1]), BF16) if win is not None else _sds(s.shape, BF16)
                  for s, win in zip(chip_sums, windows)]
    return _Hosted(chip_sums, out_shapes, 3 * len(chip_sums), start, finish)


def _plan_join(reds):
    def copies(ins, outs, sems):
        send_sems, recv_sems = sems
        x, y, c, _ = _place()
        sends, waits = [], []
        for w, out in enumerate(outs):
            hr = out.shape[0] // 2
            mine = out.at[pl.ds(c * hr, hr), :]
            theirs = out.at[pl.ds((1 - c) * hr, hr), :]
            sends.append(_rcopy(mine, mine, send_sems.at[w], recv_sems.at[w], (x, y, 1 - c)))
            waits.append(_rcopy(theirs, theirs, send_sems.at[w], recv_sems.at[w], (x, y, 1 - c)))
        return sends, waits

    def start(ins, outs, sems):
        for cp in copies(ins, outs, sems)[0]:
            cp.start()

    def finish(ins, outs, sems):
        sends, waits = copies(ins, outs, sems)
        for cp in waits:
            cp.wait_recv()
        for cp in sends:
            cp.wait_send()

    return _Hosted(reds, [_sds(r.shape) for r in reds], len(reds), start, finish, aliases={w: w for w in range(len(reds))})


def _allreduce_small(v):
    m_per = v.shape[0]

    def body(v_ref, out_ref, all_ref, send_sems, recv_sems, local_sem):
        x, y, c, chips = _place()
        me, sibling = (x, y, c), (x, y, 1 - c)

        def rows(px, py, pc):
            return all_ref.at[pl.ds((4 * px + 2 * py + pc) * m_per, m_per), :]

        def copy(k, block, to, src=None):
            return _rcopy(rows(*block) if src is None else src, rows(*block), send_sems.at[k], recv_sems.at[k], to)

        mine = pltpu.make_async_copy(v_ref, rows(*me), local_sem)
        mine.start()
        first = [copy(0, me, sibling, src=v_ref)]
        first += [copy(1 + k, me, (*chip, c), src=v_ref) for k, chip in enumerate(chips)]
        for cp in first:
            cp.start()
        passed = [copy(4 + k, (*chip, c), sibling) for k, chip in enumerate(chips)]
        for k, chip in enumerate(chips):
            copy(1 + k, (*chip, c), me).wait_recv()
            passed[k].start()
        copy(0, sibling, me).wait_recv()
        for k, chip in enumerate(chips):
            copy(4 + k, (*chip, 1 - c), me).wait_recv()
        for cp in first + passed:
            cp.wait_send()
        mine.wait()
        acc = all_ref[pl.ds(0, m_per), :]
        for d in range(1, 8):
            acc = acc + all_ref[pl.ds(d * m_per, m_per), :]
        out_ref[...] = acc

    vm = pl.BlockSpec(memory_space=pltpu.VMEM)
    return _pallas(body, name="allreduce_small", in_specs=[vm], out_specs=vm, out_shape=_sds((m_per, LANES)),
                          scratch_shapes=[pltpu.VMEM((8 * m_per, LANES), F32), pltpu.SemaphoreType.DMA((7,)),
                                          pltpu.SemaphoreType.DMA((7,)), pltpu.SemaphoreType.DMA],
                          compiler_params=pltpu.CompilerParams(vmem_limit_bytes=VMEM_LIMIT))(v)


def _block_diag(blocks):
    j, g, a, b = blocks.shape
    eye = jnp.eye(g, dtype=bool)[None, :, None, :, None]
    return jnp.where(eye, blocks[:, :, :, None, :], jnp.zeros((), blocks.dtype)).reshape(j, g * a, g * b)


def _diag_blocks(m, a, b):
    j = m.shape[0]
    g = m.shape[1] // a
    t = m.reshape(j, g, a, g, b)
    eye = jnp.eye(g, dtype=bool)[None, :, None, :, None]
    return jnp.sum(jnp.where(eye, t, 0.0), axis=3)


_SMALL = (("g_mix", (1024,)), ("b_f", (8,)), ("g_q", (64,)), ("g_k", (64,)), ("lambda_re", (32, 64)),
          ("lambda_im", (32, 64)), ("log_step", (32,)), ("b_re", (32, 64, 16)), ("b_im", (32, 64, 16)),
          ("c_re", (32, 16, 64)), ("c_im", (32, 16, 64)), ("d_skip", (32, 16)), ("b_glu", (512,)),
          ("g_attn_out", (512,)), ("g_ssm_out", (512,)), ("g_ffn", (1024,)), ("conv_b", (5632,)))


_LATE_SMALL = ("g_mix", "b_f", "g_q", "g_k")
_EARLY_SMALL = tuple(n for n, _ in _SMALL if n not in _LATE_SMALL)


def _packed_rows(n):
    tile = SUBLANES * LANES
    return -(-n // tile) * SUBLANES


def _pack_small(arrs):
    parts = []
    for a in arrs:
        flat = a.reshape(-1)
        rows = _packed_rows(flat.shape[0])
        parts.append(jnp.pad(flat, (0, rows * LANES - flat.shape[0])).reshape(rows, LANES))
    return jnp.concatenate(parts, axis=0)


def _unpack_small(buf, shapes):
    out, r = [], 0
    for shape in shapes:
        n = math.prod(shape)
        out.append(buf[r:r + _packed_rows(n)].reshape(-1)[:n].reshape(shape))
        r += _packed_rows(n)
    return out


def _halves(t):
    return t.reshape(N_CHIPS, 2, t.shape[0] // (2 * N_CHIPS), t.shape[1])


class _MeshComm:
    def __init__(self, args):
        x, y, self.core = lax.axis_index("x"), lax.axis_index("y"), lax.axis_index("c")
        self.chip = 2 * x + y
        self.place = jnp.stack([self.chip, self.core]).astype(jnp.int32)
        self.shards = {n: args[n].astype(BF16) for n in ("w_in", "w_glu", "w_out", "w_up", "w_down")}
        self.conv_w = args["conv_w"]

    def _own(self, stacked, mine):
        return lax.dynamic_update_slice(stacked, mine[None], (self.chip,) + (0,) * mine.ndim)

    def gather_w_in(self):
        sh = self.shards["w_in"]
        return _then(_plan_gather_ici([sh], [False]), _plan_gather_d2d([sh], [sh]))

    def w_in(self, gathered):
        whole = self._own(gathered[0], self.shards["w_in"]).transpose(1, 0, 2).reshape(D_MODEL, IN_COLS)
        return jnp.pad(whole, ((0, 0), (0, Z_COLS - IN_COLS)))

    def gather_first(self):
        self.mid = [self.shards[n] for n in ("w_glu", "w_out", "w_down")]
        return _plan_gather_ici(self.mid + [self.shards["w_up"]], [False, False, False, True], whole=[self.conv_w],
                                bands=[WHOLE_HALF] * 3 + [(0, 1, 4)])

    def gather_second(self, landed):
        self.g_cw = landed[4]
        return _both(_plan_gather_d2d(list(landed[:3]), self.mid),
                     _plan_gather_ici([self.shards["w_up"]], [True], bands=[(1, 3, 4)], into=[landed[3]]))

    def weights(self, gathered):
        g_glu, g_out, g_down = gathered[:3]
        own = self._own
        return (own(g_glu, self.mid[0]).reshape(SSM_W, SSM_W), own(g_out, self.mid[1]).reshape(D_MODEL, D_MODEL),
                own(g_down, self.mid[2]).reshape(D_FF, D_MODEL),
                own(self.g_cw, self.conv_w).transpose(1, 0, 2).reshape(3, 2 * D_FF))

    def gather_third(self, gathered):
        return _plan_gather_d2d([gathered[3]], [self.shards["w_up"]])

    def w_up(self, passed):
        return _place_cols(passed[0], self.shards["w_up"], self.place)

    def swap_down(self, d_w_down):
        self.d_down = _halves(d_w_down)
        return _plan_swap([self.d_down])

    def swap(self, landed_down, d_w_up, d_w_glu, d_w_out):
        self.sum_down = _add_half(self.d_down, landed_down[0], self.place, name="add_w_down")
        self.early = [d_w_up, _halves(d_w_glu), _halves(d_w_out)]
        return _both(_plan_scatter([self.sum_down], [None]), _plan_swap(self.early))

    def scatter(self, landed, small_block):
        self.land_down = landed[0]
        self.early_sums = [_add_half(g, l, self.place, name="add_" + n)
                           for g, l, n in zip(self.early, landed[1:], ("w_up", "w_glu", "w_out"))]
        return _both(_plan_scatter(self.early_sums, [(UP_COLS, UP_COLS), None, None]), _plan_allgather_first(small_block))

    def swap_in(self, d_w_in):
        self.d_in = d_w_in
        return _plan_swap([d_w_in])

    def scatter_in(self, landed):
        self.sum_in = _add_half(self.d_in, landed[0], self.place, name="add_w_in")
        return _plan_scatter([self.sum_in], [(IN_STRIDE, IN_WINDOW)])

    def small_second(self, landed_small):
        return _plan_allgather_second(landed_small[0])

    def reduce(self, lands):
        early_lands, (land_in,) = lands
        sum_in = self.sum_in
        es, el = self.early_sums, early_lands
        todo = [(sum_in, land_in, "w_in", LANES, IN_STRIDE), (es[1], el[1], "w_glu", SSM_W, 0),
                (es[2], el[2], "w_out", D_MODEL, 0), (es[0], el[0], "w_up", UP_COLS, UP_COLS),
                (self.sum_down, self.land_down, "w_down", D_MODEL, 0)]
        reds = _run_hosted(_plan_join([_sum_chips(s, l, self.place, name="sum_" + n, tc=tc, window_stride=st)
                                       for s, l, n, tc, st in todo]), name="join_halves")
        g_big = dict(zip(("w_in", "w_glu", "w_out", "w_up", "w_down"), reds))
        g_big["w_in"] = lax.dynamic_slice_in_dim(reds[0], 2 * self.chip, IN_COLS // N_CHIPS, axis=1)
        return g_big


def _local_step(x, tgt, p, comm):
    s = x.shape[0]
    row = lambda v: v.reshape(1, -1)
    g_mix, g_ffn = row(p["g_mix"]), row(p["g_ffn"])
    g_att, g_ssm, b_glu, conv_b = row(p["g_attn_out"]), row(p["g_ssm_out"]), row(p["b_glu"]), row(p["conv_b"])
    gq = row(jnp.tile(p["g_q"], HEADS))
    gk = row(jnp.tile(p["g_k"], HEADS))
    bf = row(jnp.pad(p["b_f"], (0, LANES - HEADS)))
    gg = jnp.kron(jnp.eye(HEADS, dtype=F32), jnp.ones((HEAD_DIM, HEAD_DIM), F32)).astype(BF16)
    dsk = row(p["d_skip"])

    rep = lambda a: jnp.repeat(a, SSM_GROUP, axis=0)
    lr, li = rep(p["lambda_re"]), rep(p["lambda_im"])
    ls = rep(jnp.broadcast_to(p["log_step"][:, None], (SSM_GROUPS, SSM_STATE)))
    bt_re = p["b_re"].transpose(0, 2, 1).reshape(_PARAM_SHAPE)
    bt_im = p["b_im"].transpose(0, 2, 1).reshape(_PARAM_SHAPE)
    a_re_rep, a_im_rep, bb_re, bb_im = _ssm_params(lr, li, ls, bt_re, bt_im)
    ar = a_re_rep[::SSM_GROUP].reshape(SSM_CHUNKS, 1, CHUNK_S)
    ai = a_im_rep[::SSM_GROUP].reshape(SSM_CHUNKS, 1, CHUNK_S)
    chunked = lambda t: t.reshape(SSM_CHUNKS, SSM_GROUPS // SSM_CHUNKS, SSM_GROUP, SSM_STATE)
    bbr = _block_diag(chunked(bb_re)).astype(BF16)
    bbi = _block_diag(chunked(bb_im)).astype(BF16)
    to_cc = lambda c: _block_diag(chunked(c).transpose(0, 1, 3, 2)).astype(BF16)
    ccr, cci = to_cc(p["c_re"]), to_cc(p["c_im"])

    (hb,), gathered_in = _in_norm(x, g_mix, comm.gather_w_in())
    w_in_r = comm.w_in(gathered_in)
    z = _mm(hb, w_in_r, name="in_proj", tm=512, tn=Z_COLS, tk=D_MODEL)
    qh, kh, vh, ub, uf, c128 = _attn_prep(z, gq, gk, bf, gg)
    crow = c128[:, :HEADS].T.reshape(HEADS, 1, s)
    (oh, lse), landed = _attn_fwd(qh, kh, vh, crow, comm.gather_first())
    (xr, xi, y), gathered = _ssm_fwd(ub, uf, bbr, bbi, ar, ai, ccr, cci, dsk, comm.gather_second(landed))
    w_glu_b, w_out_b, w_down_b, conv_w_full = comm.weights(gathered)
    (x1, mixb, h2b), passed = _mix_out(y, oh, x, w_glu_b, b_glu, g_att, g_ssm, w_out_b, g_ffn, comm.gather_third(gathered))
    w_up_b = comm.w_up(passed)
    up = _mm(h2b, w_up_b, name="ffn_up", tm=1024, tn=1408, tk=1024)
    act = _conv_act(up, conv_w_full, conv_b)
    dy, dyb, loss_blk = _down_loss(act, w_down_b, x1, tgt)

    d_w_down = _mm(act, dyb, ta=True, name="d_w_down", tm=1408, tn=1024, tk=2048)
    dact = _mm(dyb, w_down_b, tb=True, name="d_act", tm=1024, tn=1408, tk=1024)
    dupb, dcw = _conv_act_bwd(up, dact, conv_w_full, conv_b)
    d_w_up = _mm(h2b, dupb, ta=True, b_parts=2, name="d_w_up", tm=1024, tn=1408, tk=2048)
    dh2 = _mm(dupb, w_up_b, tb=True, a_parts=2, name="d_h2", tm=1024, tn=1024, tk=1408)
    (dx1, dx1b, doh, dys, d_w_glu, d_g_ffn, d_g_att, d_g_ssm, d_b_glu), landed_down = _mix_bwd(
        dy, dh2, x1, g_ffn, w_out_b, y, oh, w_glu_b, b_glu, g_att, g_ssm, comm.swap_down(d_w_down))
    d_w_out = _mm(mixb, dx1b, ta=True, name="d_w_out", tm=1024, tn=1024, tk=2048)
    (du, dbbr, dbbi, dccr, dcci, dar, dai, dd), swapped = _ssm_bwd(dys, uf, ub, xr, xi, bbr, bbi, ar, ai, ccr, cci, dsk,
                                                                comm.swap(landed_down, d_w_up, d_w_glu, d_w_out))
    unchunk = lambda t: t.reshape(_PARAM_SHAPE)
    dbb_re = unchunk(_diag_blocks(dbbr, SSM_GROUP, SSM_STATE))
    dbb_im = unchunk(_diag_blocks(dbbi, SSM_GROUP, SSM_STATE))
    first_row = (jnp.arange(_PARAM_SHAPE[0]) % SSM_GROUP == 0)[:, None]
    da_re = jnp.where(first_row, rep(dar.reshape(SSM_GROUPS, SSM_STATE)), 0.0)
    da_im = jnp.where(first_row, rep(dai.reshape(SSM_GROUPS, SSM_STATE)), 0.0)
    expand_t = (jnp.arange(SSM_GROUPS)[:, None] == (jnp.arange(_PARAM_SHAPE[0]) // SSM_GROUP)[None, :]).astype(BF16)
    d_lr, d_li, d_ls, d_bt_re, d_bt_im = _ssm_params_bwd(lr, li, ls, bt_re, bt_im, da_re, da_im, dbb_re, dbb_im, expand_t)
    from_bt = lambda t: t.reshape(SSM_GROUPS, SSM_GROUP, SSM_STATE).transpose(0, 2, 1)
    from_cc = lambda t: _diag_blocks(t, SSM_STATE, SSM_GROUP).transpose(0, 1, 3, 2).reshape(SSM_GROUPS, SSM_GROUP, SSM_STATE)

    small = {
        "lambda_re": d_lr, "lambda_im": d_li, "log_step": d_ls,
        "b_re": from_bt(d_bt_re), "b_im": from_bt(d_bt_im), "c_re": from_cc(dccr), "c_im": from_cc(dcci),
        "d_skip": dd, "b_glu": d_b_glu, "g_attn_out": d_g_att, "g_ssm_out": d_g_ssm, "g_ffn": d_g_ffn,
        "conv_b": dcw[:, 3],
    }
    d_conv_w = dcw[:, 0:3].transpose(1, 0, 2).reshape(3, 2 * D_FF)
    early_small = _pack_small([small[n] for n in _EARLY_SMALL] + [d_conv_w])

    (dqh, dkh, dvh, dcrow), landed = _attn_bwd(qh, kh, vh, crow, lse, doh, comm.scatter(swapped, early_small))
    early_lands, small_landed = landed[:3], landed[3:]
    dc128 = jnp.pad(dcrow.reshape(HEADS, s).T, ((0, 0), (0, LANES - HEADS)))
    (dzb, d_gq, d_gk, d_bf), small_gathered = _prep_bwd(z, dqh, dkh, dvh, du, dc128, gq, gk, bf, gg,
                                                        comm.small_second(small_landed))
    d_w_in_r = _mm(hb, dzb, ta=True, name="d_w_in", tm=512, tn=Z_COLS, tk=2048)
    dh, swapped_in = _mm(dzb, w_in_r, tb=True, name="d_h", tm=1024, tn=1024, tk=Z_COLS, carry=True,
                         hosted=comm.swap_in(d_w_in_r))
    (dx, d_g_mix), land_in = _in_norm_bwd(x, g_mix, dh, dx1, comm.scatter_in(swapped_in))
    small.update({"g_mix": d_g_mix, "b_f": d_bf[0, :HEADS], "g_q": d_gq.reshape(HEADS, HEAD_DIM).sum(0),
                  "g_k": d_gk.reshape(HEADS, HEAD_DIM).sum(0)})
    big = {"w_in": d_w_in_r, "w_glu": d_w_glu, "w_out": d_w_out, "w_up": d_w_up, "w_down": d_w_down}
    return loss_blk[0, 0], dx, big, small, d_conv_w, (early_lands, land_in, small_gathered, early_small)


def kernel(x, g_mix, w_in, b_f, g_q, g_k, lambda_re, lambda_im, log_step, b_re, b_im, c_re, c_im, d_skip, w_glu, b_glu, g_attn_out, g_ssm_out, w_out, g_ffn, w_up, conv_w, conv_b, w_down, loss_target, m_g_mix, m_w_in, m_b_f, m_g_q, m_g_k, m_lambda_re, m_lambda_im, m_log_step, m_b_re, m_b_im, m_c_re, m_c_im, m_d_skip, m_w_glu, m_b_glu, m_g_attn_out, m_g_ssm_out, m_w_out, m_g_ffn, m_w_up, m_conv_w, m_conv_b, m_w_down, v_g_mix, v_w_in, v_b_f, v_g_q, v_g_k, v_lambda_re, v_lambda_im, v_log_step, v_b_re, v_b_im, v_c_re, v_c_im, v_d_skip, v_w_glu, v_b_glu, v_g_attn_out, v_g_ssm_out, v_w_out, v_g_ffn, v_w_up, v_conv_w, v_conv_b, v_w_down):
    args = dict(locals())
    order = ["g_mix", "w_in", "b_f", "g_q", "g_k", "lambda_re", "lambda_im", "log_step", "b_re", "b_im", "c_re", "c_im",
             "d_skip", "w_glu", "b_glu", "g_attn_out", "g_ssm_out", "w_out", "g_ffn", "w_up", "conv_w", "conv_b", "w_down"]
    comm = _MeshComm(args)
    chip = comm.chip
    loss_part, dx, big, small, d_conv_w, lands = _local_step(x[0], loss_target[0], args, comm)

    g_big = comm.reduce(lands[:2])

    shapes = dict(_SMALL)
    small_names = [n for n, _ in _SMALL]
    device = (2 * chip + comm.core).reshape(1).astype(jnp.int32)
    early = _unpack_small(_sum_devices(_place_block(lands[2][0], lands[3], device)),
                          [shapes[n] for n in _EARLY_SMALL] + [(3, 2 * D_FF)])
    late = _unpack_small(_allreduce_small(_pack_small([small[n] for n in _LATE_SMALL] + [loss_part])),
                         [shapes[n] for n in _LATE_SMALL] + [()])
    loss = late[-1]
    g_conv_w = lax.dynamic_slice_in_dim(early[-1], chip * (2 * D_FF // N_CHIPS), 2 * D_FF // N_CHIPS, axis=1)
    g_small = {**dict(zip(_EARLY_SMALL, early[:-1])), **dict(zip(_LATE_SMALL, late[:-1]))}

    grad, delta, new_m, new_v = {}, {}, {}, {}
    for n in ("w_in", "w_glu", "w_out", "w_up", "w_down"):
        grad[n] = g_big[n]
        delta[n], new_m[n], new_v[n] = _adamw(args[n], g_big[n], args["m_" + n], args["v_" + n], name="adamw_" + n)
    grad["conv_w"] = g_conv_w
    delta["conv_w"], new_m["conv_w"], new_v["conv_w"] = _adamw(conv_w, g_conv_w, m_conv_w, v_conv_w, name="adamw_conv_w")
    stepped = _adamw_small([args[n] for n in small_names], [g_small[n] for n in small_names],
                           [args["m_" + n] for n in small_names], [args["v_" + n] for n in small_names])
    for i, n in enumerate(small_names):
        grad[n] = g_small[n]
        delta[n], new_m[n], new_v[n] = stepped[3 * i:3 * i + 3]

    return (loss, dx[None], *[grad[n] for n in order], *[delta[n] for n in order], *[new_m[n] for n in order],
            *[new_v[n] for n in order])
```

```python
import math

import jax
import jax.numpy as jnp
from jax import lax
from jax.experimental import pallas as pl
from jax.experimental.pallas import tpu as pltpu

F32 = jnp.float32
BF16 = jnp.bfloat16

D_MODEL = 1024
HEADS = 8
HEAD_DIM = 64
ATTN_W = 512
SSM_W = 512
SSM_GROUPS = 32
SSM_GROUP = 16
SSM_STATE = 64
N_STATE = SSM_GROUPS * SSM_STATE
D_FF = 2816
IN_COLS = 2056
Z_COLS = 2176
F_COL0 = 1536
U_COL0 = 1544
EPS = 1e-6
NEG_INF = -1e30
N_CHIPS = 4
LANES = 128
SUBLANES = 8
SSM_CHUNKS = 2
SSM_ROWS = 512
CHUNK_U = SSM_W // SSM_CHUNKS
CHUNK_S = N_STATE // SSM_CHUNKS
HEADS_PER_STEP = 4
STRIP = 128
N_STRIPS = D_FF // STRIP

ADAM_LR = 0.001
ADAM_B1 = 0.9
ADAM_B2 = 0.999
ADAM_EPS = 1e-08
ADAM_WD = 0.01
ADAM_STEP = 10

VMEM_LIMIT = 56 * 1024 * 1024
MESH = pl.DeviceIdType.MESH


def _pallas(body, **kw):
    return pl.pallas_call(body, **kw)


def _pcall(body, *, name, out_shape, in_specs, out_specs, grid=(), scratch_shapes=(), dims=None):
    params = pltpu.CompilerParams(dimension_semantics=dims, vmem_limit_bytes=VMEM_LIMIT)
    return _pallas(body, name=name, grid=grid, in_specs=in_specs, out_specs=out_specs,
                   out_shape=out_shape, scratch_shapes=scratch_shapes, compiler_params=params)


def _sds(shape, dtype=F32):
    return jax.ShapeDtypeStruct(shape, dtype)


def _dot(a, b):
    return jnp.dot(a, b, preferred_element_type=F32)


def _dot_nt(a, b):
    return lax.dot_general(a, b, (((1,), (1,)), ((), ())), preferred_element_type=F32)


def _dot_tn(a, b):
    return lax.dot_general(a, b, (((0,), (0,)), ((), ())), preferred_element_type=F32)


def _split3(x):
    hi = x.astype(BF16)
    r = x - hi.astype(F32)
    mid = r.astype(BF16)
    lo = (r - mid.astype(F32)).astype(BF16)
    return hi, mid, lo


def _dot_hi_lo(x, m01):
    hi = x.astype(BF16)
    lo = (x - hi.astype(F32)).astype(BF16)
    return _dot(hi, m01) + _dot(lo, m01)


def _dot_exact_l(m01, x):
    hi, mid, lo = _split3(x)
    return _dot(m01, hi) + _dot(m01, mid) + _dot(m01, lo)


def _sigmoid(x):
    return 1.0 / (1.0 + jnp.exp(-x))


def _rms(x, g):
    r = lax.rsqrt(jnp.mean(x * x, axis=-1, keepdims=True) + EPS)
    return x * r * g


def _rms_bwd(x, g, dy):
    r = lax.rsqrt(jnp.mean(x * x, axis=-1, keepdims=True) + EPS)
    w = dy * g
    dx = r * w - x * (r * r * r) * jnp.mean(w * x, axis=-1, keepdims=True)
    dg = jnp.sum(dy * x * r, axis=0, keepdims=True)
    return dx, dg


_GELU_K = math.sqrt(2.0 / math.pi)
_GELU_C = 0.044715


def _gelu(y):
    return y * (0.5 * (1.0 + jnp.tanh(_GELU_K * (y + _GELU_C * (y * y * y)))))


def _gelu_grad(y):
    t = jnp.tanh(_GELU_K * (y + _GELU_C * (y * y * y)))
    return 0.5 * (1.0 + t) + 0.5 * y * (1.0 - t * t) * (_GELU_K * (1.0 + 3.0 * _GELU_C * y * y))


def _tile(n, pref):
    if n <= pref:
        return n
    divs = [t for t in range(LANES, n + 1, LANES) if n % t == 0]
    below = [t for t in divs if t <= pref]
    if below and 2 * below[-1] >= pref:
        return below[-1]
    above = [t for t in divs if t > pref]
    return above[0] if above else n


def _row_tile(s):
    return min(256, s)


def _wide_tile(s):
    return min(512, s)


def _mm(a, b, *, name, tm, tn, tk, ta=False, tb=False, a_parts=1, b_parts=1, carry=False, hosted=None):
    if a_parts > 1:
        m, kk = a.shape[1], a.shape[2] * a_parts
    elif ta:
        kk, m = a.shape
    else:
        m, kk = a.shape
    if b_parts > 1:
        n = b.shape[2] * b_parts
    else:
        n = b.shape[0] if tb else b.shape[1]
    tm, tn, tk = _tile(m, tm), _tile(n // b_parts, tn), _tile(kk // a_parts, tk)
    k_per, n_per = kk // a_parts // tk, n // b_parts // tn

    def body(a_ref, b_ref, o_ref):
        k = pl.program_id(2)
        if ta:
            part = _dot_tn(a_ref[...], b_ref[...])
        elif tb:
            part = _dot_nt(a_ref[...], b_ref[...])
        else:
            part = _dot(a_ref[...], b_ref[...])

        @pl.when(k == 0)
        def _():
            o_ref[...] = part

        @pl.when(k > 0)
        def _():
            o_ref[...] += part

    if a_parts > 1:
        a_spec = pl.BlockSpec((None, tm, tk), lambda i, j, k: (k // k_per, i, k % k_per))
    else:
        a_spec = pl.BlockSpec((tk, tm), lambda i, j, k: (k, i)) if ta else pl.BlockSpec((tm, tk), lambda i, j, k: (i, k))
    if b_parts > 1:
        b_spec = pl.BlockSpec((None, tk, tn), lambda i, j, k: (j // n_per, k, j % n_per))
    else:
        b_spec = pl.BlockSpec((tn, tk), lambda i, j, k: (j, k)) if tb else pl.BlockSpec((tk, tn), lambda i, j, k: (k, j))
    grid = (m // tm, n // tn, kk // tk)
    at = lambda step: (lambda: jnp.logical_and(jnp.logical_and(pl.program_id(0) == step[0], pl.program_id(1) == step[1]),
                                               pl.program_id(2) == step[2]))
    (out,), carried = _host_pcall(body, hosted, at((0, 0, 0)), at(tuple(g - 1 for g in grid)), n_in=2, n_out=1, n_scratch=0,
                                  name=name, grid=grid, in_specs=[a_spec, b_spec],
                                  out_specs=[pl.BlockSpec((tm, tn), lambda i, j, k: (i, j))], out_shape=[_sds((m, n))],
                                  scratch_shapes=[], dims=("parallel", "parallel", "arbitrary"), operands=(a, b))
    return (out, carried) if carry else out


def _in_proj(x, g_mix, w_in_r):
    s = x.shape[0]
    tm = _wide_tile(s)

    def body(x_ref, g_ref, w_ref, h_ref, z_ref):
        h = _rms(x_ref[...], g_ref[...]).astype(BF16)
        h_ref[...] = h
        z_ref[...] = _dot(h, w_ref[...])

    return _pcall(body, name="in_proj", grid=(s // tm,),
                  in_specs=[pl.BlockSpec((tm, D_MODEL), lambda i: (i, 0)), pl.BlockSpec((1, D_MODEL), lambda i: (0, 0)),
                            pl.BlockSpec((D_MODEL, Z_COLS), lambda i: (0, 0))],
                  out_specs=[pl.BlockSpec((tm, D_MODEL), lambda i: (i, 0)), pl.BlockSpec((tm, Z_COLS), lambda i: (i, 0))],
                  out_shape=[_sds((s, D_MODEL), BF16), _sds((s, Z_COLS))], dims=("parallel",))(x, g_mix, w_in_r)


def _split_heads(ref, val):
    for h in range(HEADS):
        ref[h] = val[:, h * HEAD_DIM:(h + 1) * HEAD_DIM].astype(ref.dtype)


def _merge_heads(ref):
    return jnp.concatenate([ref[h].astype(F32) for h in range(HEADS)], axis=-1)


def _forget_logits(z_ref, bf_ref):
    fl = z_ref[:, F_COL0:F_COL0 + LANES] + bf_ref[...]
    return jnp.where(lax.broadcasted_iota(jnp.int32, fl.shape, 1) < HEADS, fl, 0.0)


def _attn_prep(z, gq, gk, bf, gg):
    s = z.shape[0]
    tm = _row_tile(s)

    def body(z_ref, gq_ref, gk_ref, bf_ref, gg_ref, qn_ref, kn_ref, vb_ref, ub_ref, uf_ref, c_ref, carry_ref):
        i = pl.program_id(0)

        @pl.when(i == 0)
        def _():
            carry_ref[...] = jnp.zeros_like(carry_ref)

        gg_m = gg_ref[...]

        def head_norm(t, g):
            ssq = _dot_hi_lo(t * t, gg_m)
            return t * lax.rsqrt(ssq * (1.0 / HEAD_DIM) + EPS) * g

        _split_heads(qn_ref, head_norm(z_ref[:, 0:ATTN_W], gq_ref[...]))
        _split_heads(kn_ref, head_norm(z_ref[:, ATTN_W:2 * ATTN_W], gk_ref[...]))
        _split_heads(vb_ref, z_ref[:, 2 * ATTN_W:3 * ATTN_W])
        u = z_ref[:, U_COL0:U_COL0 + SSM_W]
        uf_ref[...] = u
        ub_ref[...] = u.astype(BF16)
        fl = _forget_logits(z_ref, bf_ref)
        lf = jnp.minimum(fl, 0.0) - jnp.log1p(jnp.exp(-jnp.abs(fl)))
        row = lax.broadcasted_iota(jnp.int32, (tm, tm), 0)
        col = lax.broadcasted_iota(jnp.int32, (tm, tm), 1)
        tri = (row >= col).astype(BF16)
        c = _dot_exact_l(tri, lf) + carry_ref[...]
        c_ref[...] = c
        carry_ref[...] = c[tm - 1:tm, :]

    row_spec = lambda w: pl.BlockSpec((tm, w), lambda i: (i, 0))
    const = lambda shape: pl.BlockSpec(shape, lambda i: (0, 0))
    heads = pl.BlockSpec((HEADS, tm, HEAD_DIM), lambda i: (0, i, 0))
    return _pcall(body, name="attn_prep", grid=(s // tm,),
                  in_specs=[row_spec(Z_COLS), const((1, ATTN_W)), const((1, ATTN_W)), const((1, LANES)), const((ATTN_W, ATTN_W))],
                  out_specs=[heads] * 3 + [row_spec(SSM_W), row_spec(SSM_W), row_spec(LANES)],
                  out_shape=[_sds((HEADS, s, HEAD_DIM), BF16)] * 3 + [_sds((s, SSM_W), BF16), _sds((s, SSM_W)), _sds((s, LANES))],
                  scratch_shapes=[pltpu.VMEM((1, LANES), F32)], dims=("arbitrary",))(z, gq, gk, bf, gg)


def _attn_fwd(qh, kh, vh, crow, hosted=None):
    _, s, _ = qh.shape
    tq = _row_tile(s)
    scale = HEAD_DIM ** -0.5

    hp = HEADS
    nq = s // tq
    fold = lambda t, op: op(t[:, :tq // 2], t[:, tq // 2:])

    def body(q_ref, k_ref, v_ref, c_ref, o_ref, lse_ref, s_s):
        i = pl.program_id(1)

        def first(j, ms, diagonal):
            off = pl.multiple_of(j * tq, tq)
            out = []
            for hh in range(hp):
                sc = _dot_nt(q_ref[hh], k_ref[hh, pl.ds(off, tq), :]) * scale - c_ref[hh, :, pl.ds(off, tq)]
                if diagonal:
                    causal = lax.broadcasted_iota(jnp.int32, (tq, tq), 1) <= lax.broadcasted_iota(jnp.int32, (tq, tq), 0)
                    sc = jnp.where(causal, sc, NEG_INF)
                s_s[hh, j] = sc
                out.append(jnp.maximum(ms[hh], fold(sc, jnp.maximum)))
            return tuple(out)

        ms = lax.fori_loop(0, i, lambda j, c: first(j, c, False), (jnp.full((tq, tq // 2), NEG_INF, F32),) * hp)
        ms = [jnp.max(t, axis=-1, keepdims=True) for t in first(i, ms, True)]

        def second(j, carry):
            rows = pl.ds(pl.multiple_of(j * tq, tq), tq)
            out = []
            for hh in range(hp):
                ls, acc = carry[hh]
                p = jnp.exp(s_s[hh, j] - ms[hh])
                out.append((ls + fold(p, jnp.add), acc + _dot(p.astype(BF16), v_ref[hh, rows, :])))
            return tuple(out)

        zero = (jnp.zeros((tq, tq // 2), F32), jnp.zeros((tq, HEAD_DIM), F32))
        for hh, (ls, acc) in enumerate(lax.fori_loop(0, i + 1, second, (zero,) * hp)):
            l = jnp.sum(ls, axis=-1, keepdims=True)
            o_ref[hh] = acc / l
            lse_ref[hh] = ms[hh] + jnp.log(l)

    blk = pl.BlockSpec((hp, tq, HEAD_DIM), lambda h, i: (h, i, 0))
    full = pl.BlockSpec((hp, s, HEAD_DIM), lambda h, i: (h, 0, 0))
    nh = HEADS // hp
    first = lambda: jnp.logical_and(pl.program_id(0) == 0, pl.program_id(1) == 0)
    last = lambda: jnp.logical_and(pl.program_id(0) == nh - 1, pl.program_id(1) == nq - 1)
    return _host_pcall(body, hosted, first, last, n_in=4, n_out=2, n_scratch=1, name="attn_fwd", grid=(nh, nq),
                       in_specs=[blk, full, full, pl.BlockSpec((hp, 1, s), lambda h, i: (h, 0, 0))],
                       out_specs=[blk, pl.BlockSpec((hp, tq, 1), lambda h, i: (h, i, 0))],
                       out_shape=[_sds((HEADS, s, HEAD_DIM)), _sds((HEADS, s, 1))],
                       scratch_shapes=[pltpu.VMEM((hp, nq, tq, tq), F32)],
                       dims=("parallel", "parallel"), operands=(qh, kh, vh, crow))


def _ssm_param_fn(lr, li, ls, br, bi):
    step = jnp.exp(ls)
    er = jnp.exp(lr * step)
    ab_re = er * jnp.cos(li * step)
    ab_im = er * jnp.sin(li * step)
    num_re = ab_re - 1.0
    num_im = ab_im
    den = lr * lr + li * li
    f_re = (num_re * lr + num_im * li) / den
    f_im = (num_im * lr - num_re * li) / den
    bb_re = f_re * br - f_im * bi
    bb_im = f_re * bi + f_im * br
    return ab_re, ab_im, bb_re, bb_im


_PARAM_SHAPE = (SSM_GROUPS * SSM_GROUP, SSM_STATE)


def _ssm_params(lr, li, ls, br, bi):
    def body(lr_ref, li_ref, ls_ref, br_ref, bi_ref, ar_ref, ai_ref, bbr_ref, bbi_ref):
        ar, ai, bbr, bbi = _ssm_param_fn(lr_ref[...], li_ref[...], ls_ref[...], br_ref[...], bi_ref[...])
        ar_ref[...] = ar
        ai_ref[...] = ai
        bbr_ref[...] = bbr
        bbi_ref[...] = bbi

    spec = pl.BlockSpec(_PARAM_SHAPE, lambda: (0, 0))
    return _pcall(body, name="ssm_params", in_specs=[spec] * 5, out_specs=[spec] * 4,
                  out_shape=[_sds(_PARAM_SHAPE)] * 4)(lr, li, ls, br, bi)


def _ssm_params_bwd(lr, li, ls, br, bi, dar, dai, dbbr, dbbi, expand_t):
    def body(lr_ref, li_ref, ls_ref, br_ref, bi_ref, dar_ref, dai_ref, dbbr_ref, dbbi_ref, et_ref,
             dlr_ref, dli_ref, dls_ref, dbr_ref, dbi_ref):
        _, vjp = jax.vjp(_ssm_param_fn, lr_ref[...], li_ref[...], ls_ref[...], br_ref[...], bi_ref[...])
        dlr, dli, dls, dbr, dbi = vjp((dar_ref[...], dai_ref[...], dbbr_ref[...], dbbi_ref[...]))
        et = et_ref[...]
        dlr_ref[...] = _dot_exact_l(et, dlr)
        dli_ref[...] = _dot_exact_l(et, dli)
        dls_ref[...] = jnp.sum(_dot_exact_l(et, dls), axis=-1, keepdims=True)
        dbr_ref[...] = dbr
        dbi_ref[...] = dbi

    spec = pl.BlockSpec(_PARAM_SHAPE, lambda: (0, 0))
    gspec = pl.BlockSpec((SSM_GROUPS, SSM_STATE), lambda: (0, 0))
    return _pcall(body, name="ssm_params_bwd",
                  in_specs=[spec] * 9 + [pl.BlockSpec((SSM_GROUPS, _PARAM_SHAPE[0]), lambda: (0, 0))],
                  out_specs=[gspec, gspec, pl.BlockSpec((SSM_GROUPS, 1), lambda: (0, 0)), spec, spec],
                  out_shape=[_sds((SSM_GROUPS, SSM_STATE))] * 2 + [_sds((SSM_GROUPS, 1))] + [_sds(_PARAM_SHAPE)] * 2,
                  )(lr, li, ls, br, bi, dar, dai, dbbr, dbbi, expand_t)


def _cmul(ar, ai, br, bi):
    return ar * br - ai * bi, ar * bi + ai * br


def _scan_consts(ar, ai, width, reverse):
    row = lax.broadcasted_iota(jnp.int32, (SUBLANES, width), 0)
    pw = [(ar, ai)]
    for _ in range(SUBLANES - 1):
        pw.append(_cmul(pw[-1][0], pw[-1][1], ar, ai))
    steps = []
    for d in (1, 2, 4):
        keep = (row < SUBLANES - d) if reverse else (row >= d)
        steps.append((d, jnp.where(keep, pw[d - 1][0], 0.0), jnp.where(keep, pw[d - 1][1], 0.0)))
    pr = jnp.zeros((SUBLANES, width), F32)
    pi = jnp.zeros((SUBLANES, width), F32)
    for r in range(SUBLANES):
        e = (SUBLANES - r) if reverse else (r + 1)
        pr = jnp.where(row == r, pw[e - 1][0], pr)
        pi = jnp.where(row == r, pw[e - 1][1], pi)
    return steps, pr, pi


def _scan_tile(xr, xi, cr, ci, consts, reverse):
    steps, pr, pi = consts
    for d, mr, mi in steps:
        sh = (SUBLANES - d) if reverse else d
        sr = pltpu.roll(xr, sh, 0)
        si = pltpu.roll(xi, sh, 0)
        xr, xi = xr + mr * sr - mi * si, xi + mr * si + mi * sr
    return xr + pr * cr - pi * ci, xi + pr * ci + pi * cr


def _ssm_fwd(ub, uf, bbr, bbi, ar, ai, ccr, cci, dsk, hosted=None):
    s = ub.shape[0]
    tm = min(SSM_ROWS, s)
    nt = tm // SUBLANES

    def body(ub_ref, u_ref, bbr_ref, bbi_ref, ar_ref, ai_ref, ccr_ref, cci_ref, dsk_ref,
             xr_ref, xi_ref, y_ref, cr_s, ci_s):
        i = pl.program_id(1)

        @pl.when(i == 0)
        def _():
            cr_s[...] = jnp.zeros_like(cr_s)
            ci_s[...] = jnp.zeros_like(ci_s)

        u_b = ub_ref[...]
        xr_ref[...] = _dot(u_b, bbr_ref[0])
        xi_ref[...] = _dot(u_b, bbi_ref[0])
        consts = _scan_consts(ar_ref[0], ai_ref[0], CHUNK_S, False)

        def tile(k, carry):
            cr, ci = carry
            sl = pl.ds(pl.multiple_of(k * SUBLANES, SUBLANES), SUBLANES)
            xr, xi = _scan_tile(xr_ref[sl, :], xi_ref[sl, :], cr, ci, consts, False)
            xr_ref[sl, :] = xr
            xi_ref[sl, :] = xi
            return xr[SUBLANES - 1:SUBLANES, :], xi[SUBLANES - 1:SUBLANES, :]

        cr, ci = lax.fori_loop(0, nt, tile, (cr_s[...], ci_s[...]))
        cr_s[...] = cr
        ci_s[...] = ci
        y_ref[...] = (_dot(xr_ref[...].astype(BF16), ccr_ref[0]) - _dot(xi_ref[...].astype(BF16), cci_ref[0])
                      + dsk_ref[...] * u_ref[...])

    wspec = lambda a, b: pl.BlockSpec((1, a, b), lambda j, i: (j, 0, 0))
    nb = s // tm
    first = lambda: jnp.logical_and(pl.program_id(0) == 0, pl.program_id(1) == 0)
    last = lambda: jnp.logical_and(pl.program_id(0) == SSM_CHUNKS - 1, pl.program_id(1) == nb - 1)
    return _host_pcall(
        body, hosted, first, last, n_in=9, n_out=3, n_scratch=2, name="ssm_fwd", grid=(SSM_CHUNKS, nb),
        in_specs=[pl.BlockSpec((tm, CHUNK_U), lambda j, i: (i, j)),
                  pl.BlockSpec((tm, CHUNK_U), lambda j, i: (i, j)),
                  wspec(CHUNK_U, CHUNK_S), wspec(CHUNK_U, CHUNK_S), wspec(1, CHUNK_S), wspec(1, CHUNK_S),
                  wspec(CHUNK_S, CHUNK_U), wspec(CHUNK_S, CHUNK_U),
                  pl.BlockSpec((1, CHUNK_U), lambda j, i: (0, j))],
        out_specs=[pl.BlockSpec((tm, CHUNK_S), lambda j, i: (i, j)), pl.BlockSpec((tm, CHUNK_S), lambda j, i: (i, j)),
                   pl.BlockSpec((tm, CHUNK_U), lambda j, i: (i, j))],
        out_shape=[_sds((s, N_STATE)), _sds((s, N_STATE)), _sds((s, SSM_W))],
        scratch_shapes=[pltpu.VMEM((1, CHUNK_S), F32)] * 2,
        dims=("parallel", "arbitrary"), operands=(ub, uf, bbr, bbi, ar, ai, ccr, cci, dsk))


def _ssm_glu(y, w_glu, b_glu):
    ge = _gelu(y)
    sg = _sigmoid(_dot(ge.astype(BF16), w_glu) + b_glu)
    return ge, sg


def _mix_out(y, att, x, w_glu, b_glu, g_att, g_ssm, w_out, g_ffn, hosted=None):
    s = x.shape[0]
    tm = _wide_tile(s)

    def body(y_ref, att_ref, x_ref, wg_ref, bg_ref, ga_ref, gs_ref, wo_ref, gf_ref, x1_ref, mix_ref, h2_ref):
        ge, sg = _ssm_glu(y_ref[...], wg_ref[...], bg_ref[...])
        ms = _rms(ge * sg, gs_ref[...]).astype(BF16)
        ma = _rms(_merge_heads(att_ref), ga_ref[...]).astype(BF16)
        mix_ref[:, 0:ATTN_W] = ma
        mix_ref[:, ATTN_W:D_MODEL] = ms
        x1 = x_ref[...] + (_dot(ma, wo_ref[0:ATTN_W, :]) + _dot(ms, wo_ref[ATTN_W:D_MODEL, :]))
        x1_ref[...] = x1
        h2_ref[...] = _rms(x1, gf_ref[...]).astype(BF16)

    row = lambda w: pl.BlockSpec((tm, w), lambda i: (i, 0))
    const = lambda a, b: pl.BlockSpec((a, b), lambda i: (0, 0))
    nb = s // tm
    return _host_pcall(body, hosted, lambda: pl.program_id(0) == 0, lambda: pl.program_id(0) == nb - 1,
                       n_in=9, n_out=3, n_scratch=0, name="mix_out", grid=(nb,),
                       in_specs=[row(SSM_W), pl.BlockSpec((HEADS, tm, HEAD_DIM), lambda i: (0, i, 0)), row(D_MODEL),
                                 const(SSM_W, SSM_W), const(1, SSM_W),
                                 const(1, ATTN_W), const(1, SSM_W), const(D_MODEL, D_MODEL), const(1, D_MODEL)],
                       out_specs=[row(D_MODEL)] * 3,
                       out_shape=[_sds((s, D_MODEL)), _sds((s, D_MODEL), BF16), _sds((s, D_MODEL), BF16)],
                       scratch_shapes=[], dims=("parallel",), operands=(y, att, x, w_glu, b_glu, g_att, g_ssm, w_out, g_ffn))


CONV_CHUNK = 64


def _conv_rows(pad_ref, w, b, r0, n):
    y = b + pad_ref[pl.ds(r0 + SUBLANES - 2, n), :] * w[0:1, :]
    y = y + pad_ref[pl.ds(r0 + SUBLANES - 1, n), :] * w[1:2, :]
    return y + pad_ref[pl.ds(r0 + SUBLANES, n), :] * w[2:3, :]


def _fill_front_pad(pad_ref, strip_ref, s):
    pad_ref[0:SUBLANES, :] = jnp.zeros((SUBLANES, STRIP), F32)
    for r0 in range(0, s, CONV_CHUNK):
        pad_ref[pl.ds(SUBLANES + r0, CONV_CHUNK), :] = strip_ref[pl.ds(r0, CONV_CHUNK), :]


def _conv_act(up, conv_w, conv_b):
    s = up.shape[0]

    def body(ug_ref, uv_ref, wg_ref, wv_ref, bg_ref, bv_ref, act_ref, pg_ref, pv_ref):
        _fill_front_pad(pg_ref, ug_ref, s)
        _fill_front_pad(pv_ref, uv_ref, s)
        wg, wv, bg, bv = wg_ref[...], wv_ref[...], bg_ref[...], bv_ref[...]
        for r0 in range(0, s, CONV_CHUNK):
            hg = _conv_rows(pg_ref, wg, bg, r0, CONV_CHUNK)
            hv = _conv_rows(pv_ref, wv, bv, r0, CONV_CHUNK)
            act_ref[pl.ds(r0, CONV_CHUNK), :] = (hg * _sigmoid(hg) * hv).astype(BF16)

    strip = lambda off: pl.BlockSpec((s, STRIP), lambda j: (0, j + off))
    wsp = lambda off: pl.BlockSpec((3, STRIP), lambda j: (0, j + off))
    bsp = lambda off: pl.BlockSpec((1, STRIP), lambda j: (0, j + off))
    return _pcall(body, name="conv_act", grid=(N_STRIPS,),
                  in_specs=[strip(0), strip(N_STRIPS), wsp(0), wsp(N_STRIPS), bsp(0), bsp(N_STRIPS)],
                  out_specs=pl.BlockSpec((s, STRIP), lambda j: (0, j)), out_shape=_sds((s, D_FF), BF16),
                  scratch_shapes=[pltpu.VMEM((s + SUBLANES, STRIP), F32)] * 2,
                  dims=("parallel",))(up, up, conv_w, conv_w, conv_b, conv_b)


def _down_loss(act, w_down, x1, tgt):
    s = x1.shape[0]
    tm = _wide_tile(s)

    def body(a_ref, w_ref, x1_ref, t_ref, dy_ref, dyb_ref, loss_ref):
        i = pl.program_id(0)

        @pl.when(i == 0)
        def _():
            loss_ref[...] = jnp.zeros_like(loss_ref)

        diff = x1_ref[...] + _dot(a_ref[...], w_ref[...]) - t_ref[...]
        dy = diff * (1.0 / D_MODEL)
        dy_ref[...] = dy
        dyb_ref[...] = dy.astype(BF16)
        loss_ref[...] += 0.5 * jnp.sum(diff * dy)

    row = lambda w: pl.BlockSpec((tm, w), lambda i: (i, 0))
    return _pcall(body, name="down_loss", grid=(s // tm,),
                  in_specs=[row(D_FF), pl.BlockSpec((D_FF, D_MODEL), lambda i: (0, 0)), row(D_MODEL), row(D_MODEL)],
                  out_specs=[row(D_MODEL), row(D_MODEL), pl.BlockSpec((SUBLANES, LANES), lambda i: (0, 0))],
                  out_shape=[_sds((s, D_MODEL)), _sds((s, D_MODEL), BF16), _sds((SUBLANES, LANES))],
                  dims=("arbitrary",))(act, w_down, x1, tgt)


def _conv_act_bwd(up, dact, conv_w, conv_b):
    s = up.shape[0]
    ch = CONV_CHUNK

    def body(ug_ref, uv_ref, da_ref, wg_ref, wv_ref, bg_ref, bv_ref, dup_ref, dcw_ref, pg_ref, pv_ref, dg_ref, dv_ref):
        _fill_front_pad(pg_ref, ug_ref, s)
        _fill_front_pad(pv_ref, uv_ref, s)
        zero = jnp.zeros((SUBLANES, STRIP), F32)
        dg_ref[pl.ds(s, SUBLANES), :] = zero
        dv_ref[pl.ds(s, SUBLANES), :] = zero
        wg, wv, bg, bv = wg_ref[...], wv_ref[...], bg_ref[...], bv_ref[...]
        tile_sum = lambda t: jnp.sum(t.reshape(ch // SUBLANES, SUBLANES, STRIP), axis=0)
        accs = [[zero] * 4, [zero] * 4]
        for r0 in range(0, s, ch):
            hg = _conv_rows(pg_ref, wg, bg, r0, ch)
            hv = _conv_rows(pv_ref, wv, bv, r0, ch)
            sg = _sigmoid(hg)
            da = da_ref[pl.ds(r0, ch), :]
            dhs = (da * hv * (sg * (1.0 + hg * (1.0 - sg))), da * (hg * sg))
            for half, (dh, d_ref, p_ref) in enumerate(zip(dhs, (dg_ref, dv_ref), (pg_ref, pv_ref))):
                d_ref[pl.ds(r0, ch), :] = dh
                for k in range(3):
                    accs[half][k] = accs[half][k] + tile_sum(dh * p_ref[pl.ds(r0 + SUBLANES - 2 + k, ch), :])
                accs[half][3] = accs[half][3] + tile_sum(dh)
        for half, (d_ref, w) in enumerate(((dg_ref, wg), (dv_ref, wv))):
            for r0 in range(0, s, ch):
                dup = (d_ref[pl.ds(r0, ch), :] * w[2:3, :] + d_ref[pl.ds(r0 + 1, ch), :] * w[1:2, :]
                       + d_ref[pl.ds(r0 + 2, ch), :] * w[0:1, :])
                dup_ref[half, pl.ds(r0, ch), :] = dup.astype(BF16)
            rid = lax.broadcasted_iota(jnp.int32, (SUBLANES, STRIP), 0)
            out = zero
            for k in range(4):
                out = jnp.where(rid == k, jnp.sum(accs[half][k], axis=0, keepdims=True), out)
            dcw_ref[half] = out

    strip = lambda off: pl.BlockSpec((s, STRIP), lambda j: (0, j + off))
    wsp = lambda off: pl.BlockSpec((3, STRIP), lambda j: (0, j + off))
    bsp = lambda off: pl.BlockSpec((1, STRIP), lambda j: (0, j + off))
    return _pcall(body, name="conv_act_bwd", grid=(N_STRIPS,),
                  in_specs=[strip(0), strip(N_STRIPS), strip(0), wsp(0), wsp(N_STRIPS), bsp(0), bsp(N_STRIPS)],
                  out_specs=[pl.BlockSpec((2, s, STRIP), lambda j: (0, 0, j)), pl.BlockSpec((2, SUBLANES, STRIP), lambda j: (0, 0, j))],
                  out_shape=[_sds((2, s, D_FF), BF16), _sds((2, SUBLANES, D_FF))],
                  scratch_shapes=[pltpu.VMEM((s + SUBLANES, STRIP), F32)] * 4,
                  dims=("parallel",))(up, up, dact, conv_w, conv_w, conv_b, conv_b)


def _mix_bwd(dy, dh2, x1, g_ffn, w_out, y, att, w_glu, b_glu, g_att, g_ssm, hosted=None):
    s = dy.shape[0]
    tm = _wide_tile(s)

    def body(dy_ref, dh2_ref, x1_ref, gf_ref, wo_ref, y_ref, att_ref, wg_ref, bg_ref, ga_ref, gs_ref,
             dx1_ref, dx1b_ref, datt_ref, dys_ref, dwg_ref, dgf_ref, dga_ref, dgs_ref, dbg_ref):
        i = pl.program_id(0)

        @pl.when(i == 0)
        def _():
            for r in (dwg_ref, dgf_ref, dga_ref, dgs_ref, dbg_ref):
                r[...] = jnp.zeros_like(r)

        dxn, dgf = _rms_bwd(x1_ref[...], gf_ref[...], dh2_ref[...])
        dx1 = dy_ref[...] + dxn
        dx1_ref[...] = dx1
        dx1b = dx1.astype(BF16)
        dx1b_ref[...] = dx1b
        dgf_ref[...] += dgf
        dma = _dot_nt(dx1b, wo_ref[0:ATTN_W, :])
        dms = _dot_nt(dx1b, wo_ref[ATTN_W:D_MODEL, :])
        datt, dga = _rms_bwd(_merge_heads(att_ref), ga_ref[...], dma)
        _split_heads(datt_ref, datt)
        dga_ref[...] += dga
        yv = y_ref[...]
        ge, sg = _ssm_glu(yv, wg_ref[...], bg_ref[...])
        dssm, dgs = _rms_bwd(ge * sg, gs_ref[...], dms)
        dgs_ref[...] += dgs
        dgl = dssm * ge * sg * (1.0 - sg)
        dglb = dgl.astype(BF16)
        dge = dssm * sg + _dot_nt(dglb, wg_ref[...])
        dbg_ref[...] += jnp.sum(dgl, axis=0, keepdims=True)
        dwg_ref[...] += _dot_tn(ge.astype(BF16), dglb)
        dys_ref[...] = dge * _gelu_grad(yv)

    row = lambda w: pl.BlockSpec((tm, w), lambda i: (i, 0))
    const = lambda a, b: pl.BlockSpec((a, b), lambda i: (0, 0))
    heads = pl.BlockSpec((HEADS, tm, HEAD_DIM), lambda i: (0, i, 0))
    nb = s // tm
    return _host_pcall(
        body, hosted, lambda: pl.program_id(0) == 0, lambda: pl.program_id(0) == nb - 1, n_in=11, n_out=9, n_scratch=0,
        name="mix_bwd", grid=(nb,),
        in_specs=[row(D_MODEL), row(D_MODEL), row(D_MODEL), const(1, D_MODEL), const(D_MODEL, D_MODEL), row(SSM_W),
                  heads, const(SSM_W, SSM_W), const(1, SSM_W), const(1, ATTN_W), const(1, SSM_W)],
        out_specs=[row(D_MODEL), row(D_MODEL), heads, row(SSM_W), const(SSM_W, SSM_W), const(1, D_MODEL),
                   const(1, ATTN_W), const(1, SSM_W), const(1, SSM_W)],
        out_shape=[_sds((s, D_MODEL)), _sds((s, D_MODEL), BF16), _sds((HEADS, s, HEAD_DIM)), _sds((s, SSM_W)),
                   _sds((SSM_W, SSM_W)), _sds((1, D_MODEL)), _sds((1, ATTN_W)), _sds((1, SSM_W)), _sds((1, SSM_W))],
        scratch_shapes=[], dims=("arbitrary",), operands=(dy, dh2, x1, g_ffn, w_out, y, att, w_glu, b_glu, g_att, g_ssm))


def _ssm_bwd(dys, uf, ub, xr, xi, bbr, bbi, ar, ai, ccr, cci, dsk, hosted=None):
    s = dys.shape[0]
    tm = min(SSM_ROWS, s)
    nb = s // tm
    nt = tm // SUBLANES

    def body(dy_ref, u_ref, ub_ref, xr_ref, xi_ref, xrp_ref, xip_ref, bbr_ref, bbi_ref, ar_ref, ai_ref, ccr_ref,
             cci_ref, dsk_ref, du_ref, dbbr_ref, dbbi_ref, dccr_ref, dcci_ref, dar_ref, dai_ref, dd_ref,
             gr_s, gi_s, cr_s, ci_s, accr_s, acci_s):
        i = pl.program_id(1)
        first_block = i == nb - 1

        @pl.when(i == 0)
        def _():
            for r in (cr_s, ci_s, accr_s, acci_s, dbbr_ref, dbbi_ref, dccr_ref, dcci_ref, dd_ref):
                r[...] = jnp.zeros_like(r)

        dy = dy_ref[...]
        dyb = dy.astype(BF16)
        gr_s[...] = _dot_nt(dyb, ccr_ref[0])
        gi_s[...] = -_dot_nt(dyb, cci_ref[0])
        consts = _scan_consts(ar_ref[0], -ai_ref[0], CHUNK_S, True)
        row = lax.broadcasted_iota(jnp.int32, (SUBLANES, CHUNK_S), 0)

        def tile(kk, carry):
            cr, ci, accr, acci = carry
            k = nt - 1 - kk
            sl = pl.ds(pl.multiple_of(k * SUBLANES, SUBLANES), SUBLANES)
            gr, gi = _scan_tile(gr_s[sl, :], gi_s[sl, :], cr, ci, consts, True)
            gr_s[sl, :] = gr
            gi_s[sl, :] = gi
            slp = pl.ds(pl.multiple_of(jnp.maximum(k - 1, 0) * SUBLANES, SUBLANES), SUBLANES)
            inner = k > 0
            pr_t = jnp.where(inner, xr_ref[slp, :], xrp_ref[...])
            pi_t = jnp.where(inner, xi_ref[slp, :], xip_ref[...])
            live = jnp.logical_or(inner, jnp.logical_not(first_block))
            top_r = jnp.where(live, pltpu.roll(pr_t, 1, 0), 0.0)
            top_i = jnp.where(live, pltpu.roll(pi_t, 1, 0), 0.0)
            xpr = jnp.where(row == 0, top_r, pltpu.roll(xr_ref[sl, :], 1, 0))
            xpi = jnp.where(row == 0, top_i, pltpu.roll(xi_ref[sl, :], 1, 0))
            accr = accr + gr * xpr + gi * xpi
            acci = acci + gi * xpr - gr * xpi
            return gr[0:1, :], gi[0:1, :], accr, acci

        zeros = jnp.zeros((SUBLANES, CHUNK_S), F32)
        cr, ci, accr, acci = lax.fori_loop(0, nt, tile, (cr_s[...], ci_s[...], zeros, zeros))
        cr_s[...] = cr
        ci_s[...] = ci
        accr_s[...] += accr
        acci_s[...] += acci
        grb = gr_s[...].astype(BF16)
        gib = gi_s[...].astype(BF16)
        u_b = ub_ref[...]
        du_ref[...] = _dot_nt(grb, bbr_ref[0]) + _dot_nt(gib, bbi_ref[0]) + dsk_ref[...] * dy
        dbbr_ref[0] += _dot_tn(u_b, grb)
        dbbi_ref[0] += _dot_tn(u_b, gib)
        dccr_ref[0] += _dot_tn(xr_ref[...].astype(BF16), dyb)
        dcci_ref[0] -= _dot_tn(xi_ref[...].astype(BF16), dyb)
        dd_ref[...] += jnp.sum(dy * u_ref[...], axis=0, keepdims=True)

        @pl.when(i == nb - 1)
        def _():
            dar_ref[0] = jnp.sum(accr_s[...], axis=0, keepdims=True)
            dai_ref[0] = jnp.sum(acci_s[...], axis=0, keepdims=True)

    tiles_per_block = tm // SUBLANES
    rb = lambda i: nb - 1 - i
    wspec = lambda a, b: pl.BlockSpec((1, a, b), lambda j, i: (j, 0, 0))
    xblk = pl.BlockSpec((tm, CHUNK_S), lambda j, i: (rb(i), j))
    xprev = pl.BlockSpec((SUBLANES, CHUNK_S), lambda j, i: (jnp.maximum(rb(i) * tiles_per_block - 1, 0), j))
    ublk = pl.BlockSpec((tm, CHUNK_U), lambda j, i: (rb(i), j))
    first = lambda: jnp.logical_and(pl.program_id(0) == 0, pl.program_id(1) == 0)
    last = lambda: jnp.logical_and(pl.program_id(0) == SSM_CHUNKS - 1, pl.program_id(1) == nb - 1)
    return _host_pcall(
        body, hosted, first, last, n_in=14, n_out=8, n_scratch=6, name="ssm_bwd", grid=(SSM_CHUNKS, nb),
        in_specs=[ublk, ublk, ublk, xblk, xblk, xprev, xprev,
                  wspec(CHUNK_U, CHUNK_S), wspec(CHUNK_U, CHUNK_S), wspec(1, CHUNK_S), wspec(1, CHUNK_S),
                  wspec(CHUNK_S, CHUNK_U), wspec(CHUNK_S, CHUNK_U), pl.BlockSpec((1, CHUNK_U), lambda j, i: (0, j))],
        out_specs=[ublk, wspec(CHUNK_U, CHUNK_S), wspec(CHUNK_U, CHUNK_S), wspec(CHUNK_S, CHUNK_U),
                   wspec(CHUNK_S, CHUNK_U), wspec(1, CHUNK_S), wspec(1, CHUNK_S),
                   pl.BlockSpec((1, CHUNK_U), lambda j, i: (0, j))],
        out_shape=[_sds((s, SSM_W)), _sds((SSM_CHUNKS, CHUNK_U, CHUNK_S)), _sds((SSM_CHUNKS, CHUNK_U, CHUNK_S)),
                   _sds((SSM_CHUNKS, CHUNK_S, CHUNK_U)), _sds((SSM_CHUNKS, CHUNK_S, CHUNK_U)),
                   _sds((SSM_CHUNKS, 1, CHUNK_S)), _sds((SSM_CHUNKS, 1, CHUNK_S)), _sds((1, SSM_W))],
        scratch_shapes=[pltpu.VMEM((tm, CHUNK_S), F32)] * 2 + [pltpu.VMEM((1, CHUNK_S), F32)] * 2
                       + [pltpu.VMEM((SUBLANES, CHUNK_S), F32)] * 2,
        dims=("parallel", "arbitrary"), operands=(dys, uf, ub, xr, xi, xr, xi, bbr, bbi, ar, ai, ccr, cci, dsk))


def _attn_probs(q, ks, cs, lse, scale, diagonal):
    p = jnp.exp(_dot_nt(q, ks) * scale - cs - lse)
    if diagonal:
        tq, tk = p.shape
        causal = lax.broadcasted_iota(jnp.int32, (tq, tk), 1) <= lax.broadcasted_iota(jnp.int32, (tq, tk), 0)
        p = jnp.where(causal, p, 0.0)
    return p


def _attn_bwd(qh, kh, vh, crow, lse, doh, hosted=None):
    _, s, _ = qh.shape
    tq = _row_tile(s)
    nq = s // tq
    scale = HEAD_DIM ** -0.5
    hp = HEADS_PER_STEP

    def body(q_ref, k_ref, v_ref, c_ref, lse_ref, do_ref, dq_ref, dk_ref, dv_ref, dc_ref, p_s, dp_s):
        i = pl.program_id(1)

        @pl.when(i == 0)
        def _():
            for r in (dk_ref, dv_ref, dc_ref):
                r[...] = jnp.zeros_like(r)

        dobs = [do_ref[hh].astype(BF16) for hh in range(hp)]

        def first(j, dls, diagonal):
            off = pl.multiple_of(j * tq, tq)
            out = []
            for hh in range(hp):
                p = _attn_probs(q_ref[hh], k_ref[hh, pl.ds(off, tq), :], c_ref[hh, :, pl.ds(off, tq)], lse_ref[hh],
                                scale, diagonal)
                dp = _dot_nt(dobs[hh], v_ref[hh, pl.ds(off, tq), :])
                p_s[hh, j] = p
                dp_s[hh, j] = dp
                out.append(dls[hh] + jnp.sum(p * dp, axis=-1, keepdims=True))
            return tuple(out)

        zero_col = jnp.zeros((tq, 1), F32)
        dls = lax.fori_loop(0, i, lambda j, c: first(j, c, False), (zero_col,) * hp)
        dls = first(i, dls, True)

        def second(j, dqs):
            rows = pl.ds(pl.multiple_of(j * tq, tq), tq)
            out = []
            for hh in range(hp):
                p = p_s[hh, j]
                ds = p * (dp_s[hh, j] - dls[hh])
                dsb = ds.astype(BF16)
                dv_ref[hh, rows, :] += _dot_tn(p.astype(BF16), dobs[hh])
                dk_ref[hh, rows, :] += _dot_tn(dsb, q_ref[hh]) * scale
                dc_ref[hh, :, rows] -= jnp.sum(ds, axis=0, keepdims=True)
                out.append(dqs[hh] + _dot(dsb, k_ref[hh, rows, :]))
            return tuple(out)

        dqs = lax.fori_loop(0, i + 1, second, (jnp.zeros((tq, HEAD_DIM), F32),) * hp)
        for hh in range(hp):
            dq_ref[hh] = dqs[hh] * scale

    blk = pl.BlockSpec((hp, tq, HEAD_DIM), lambda h, i: (h, i, 0))
    full = pl.BlockSpec((hp, s, HEAD_DIM), lambda h, i: (h, 0, 0))
    crow_spec = pl.BlockSpec((hp, 1, s), lambda h, i: (h, 0, 0))
    nh = HEADS // hp
    first = lambda: jnp.logical_and(pl.program_id(0) == 0, pl.program_id(1) == 0)
    last = lambda: jnp.logical_and(pl.program_id(0) == nh - 1, pl.program_id(1) == nq - 1)
    return _host_pcall(body, hosted, first, last, n_in=6, n_out=4, n_scratch=2, name="attn_bwd", grid=(nh, nq),
                       in_specs=[blk, full, full, crow_spec, pl.BlockSpec((hp, tq, 1), lambda h, i: (h, i, 0)), blk],
                       out_specs=[blk, full, full, crow_spec],
                       out_shape=[_sds((HEADS, s, HEAD_DIM))] * 3 + [_sds((HEADS, 1, s))],
                       scratch_shapes=[pltpu.VMEM((hp, nq, tq, tq), F32)] * 2,
                       dims=("parallel", "arbitrary"), operands=(qh, kh, vh, crow, lse, doh))


def _prep_bwd(z, dqn, dkn, dv, du, dc, gq, gk, bf, gg, hosted=None):
    s = z.shape[0]
    tm = _row_tile(s)
    nb = s // tm

    def body(z_ref, dqn_ref, dkn_ref, dv_ref, du_ref, dc_ref, gq_ref, gk_ref, bf_ref, gg_ref,
             dz_ref, dgq_ref, dgk_ref, dbf_ref, carry_ref):
        i = pl.program_id(0)

        @pl.when(i == 0)
        def _():
            for r in (dgq_ref, dgk_ref, dbf_ref, carry_ref):
                r[...] = jnp.zeros_like(r)

        gg_m = gg_ref[...]

        def head_norm_bwd(t, g, dn):
            r = lax.rsqrt(_dot_hi_lo(t * t, gg_m) * (1.0 / HEAD_DIM) + EPS)
            w = dn * g
            mean_wt = _dot_hi_lo(w * t, gg_m) * (1.0 / HEAD_DIM)
            return r * w - t * (r * r * r) * mean_wt, jnp.sum(dn * t * r, axis=0, keepdims=True)

        dq, dgq = head_norm_bwd(z_ref[:, 0:ATTN_W], gq_ref[...], _merge_heads(dqn_ref))
        dk, dgk = head_norm_bwd(z_ref[:, ATTN_W:2 * ATTN_W], gk_ref[...], _merge_heads(dkn_ref))
        dgq_ref[...] += dgq
        dgk_ref[...] += dgk
        row = lax.broadcasted_iota(jnp.int32, (tm, tm), 0)
        col = lax.broadcasted_iota(jnp.int32, (tm, tm), 1)
        triu = (col >= row).astype(BF16)
        dlf = _dot_exact_l(triu, dc_ref[...]) + carry_ref[...]
        carry_ref[...] = dlf[0:1, :]
        df = dlf * _sigmoid(-_forget_logits(z_ref, bf_ref))
        dbf_ref[...] += jnp.sum(df, axis=0, keepdims=True)
        dz_ref[:, 0:ATTN_W] = dq.astype(BF16)
        dz_ref[:, ATTN_W:2 * ATTN_W] = dk.astype(BF16)
        dz_ref[:, 2 * ATTN_W:3 * ATTN_W] = _merge_heads(dv_ref).astype(BF16)
        tail = jnp.concatenate([df[:, :HEADS], du_ref[...], jnp.zeros((tm, Z_COLS - IN_COLS), F32)], axis=-1)
        dz_ref[:, F_COL0:Z_COLS] = tail.astype(BF16)

    row_spec = lambda w: pl.BlockSpec((tm, w), lambda i: (nb - 1 - i, 0))
    const = lambda shape: pl.BlockSpec(shape, lambda i: (0, 0))
    return _host_pcall(
        body, hosted, lambda: pl.program_id(0) == 0, lambda: pl.program_id(0) == nb - 1, n_in=10, n_out=4, n_scratch=1,
        name="prep_bwd", grid=(nb,),
        in_specs=[row_spec(Z_COLS)] + [pl.BlockSpec((HEADS, tm, HEAD_DIM), lambda i: (0, nb - 1 - i, 0))] * 3
                 + [row_spec(ATTN_W), row_spec(LANES), const((1, ATTN_W)),
                    const((1, ATTN_W)), const((1, LANES)), const((ATTN_W, ATTN_W))],
        out_specs=[row_spec(Z_COLS), const((1, ATTN_W)), const((1, ATTN_W)), const((1, LANES))],
        out_shape=[_sds((s, Z_COLS), BF16), _sds((1, ATTN_W)), _sds((1, ATTN_W)), _sds((1, LANES))],
        scratch_shapes=[pltpu.VMEM((1, LANES), F32)], dims=("arbitrary",),
        operands=(z, dqn, dkn, dv, du, dc, gq, gk, bf, gg))


def _in_norm_bwd(x, g_mix, dh, dx1, hosted=None):
    s = x.shape[0]
    tm = _wide_tile(s)

    def body(x_ref, g_ref, dh_ref, dx1_ref, dx_ref, dg_ref):
        i = pl.program_id(0)

        @pl.when(i == 0)
        def _():
            dg_ref[...] = jnp.zeros_like(dg_ref)

        dxn, dg = _rms_bwd(x_ref[...], g_ref[...], dh_ref[...])
        dx_ref[...] = dx1_ref[...] + dxn
        dg_ref[...] += dg

    row = pl.BlockSpec((tm, D_MODEL), lambda i: (i, 0))
    vec = pl.BlockSpec((1, D_MODEL), lambda i: (0, 0))
    nb = s // tm
    return _host_pcall(body, hosted, lambda: pl.program_id(0) == 0, lambda: pl.program_id(0) == nb - 1,
                       n_in=4, n_out=2, n_scratch=0, name="in_norm_bwd", grid=(nb,), in_specs=[row, vec, row, row],
                       out_specs=[row, vec], out_shape=[_sds((s, D_MODEL)), _sds((1, D_MODEL))], scratch_shapes=[],
                       dims=("arbitrary",), operands=(x, g_mix, dh, dx1))


def _adamw_refs(w_ref, g_ref, m_ref, v_ref, d_ref, mo_ref, vo_ref):
    gv = g_ref[...]
    mn = ADAM_B1 * m_ref[...] + (1.0 - ADAM_B1) * gv
    vn = ADAM_B2 * v_ref[...] + (1.0 - ADAM_B2) * (gv * gv)
    m_hat = mn / (1.0 - ADAM_B1 ** ADAM_STEP)
    v_hat = vn / (1.0 - ADAM_B2 ** ADAM_STEP)
    d_ref[...] = -ADAM_LR * (m_hat / (jnp.sqrt(v_hat) + ADAM_EPS) + ADAM_WD * w_ref[...])
    mo_ref[...] = mn
    vo_ref[...] = vn


def _adamw_small(ws, gs, ms, vs):
    n = len(ws)

    def body(*refs):
        ins, outs = refs[:4 * n], refs[4 * n:]
        for i in range(n):
            _adamw_refs(ins[i], ins[n + i], ins[2 * n + i], ins[3 * n + i], *outs[3 * i:3 * i + 3])

    vm = pl.BlockSpec(memory_space=pltpu.VMEM)
    out_shape = [_sds(w.shape) for w in ws for _ in range(3)]
    return _pallas(body, name="adamw_small", in_specs=[vm] * (4 * n), out_specs=[vm] * (3 * n), out_shape=out_shape,
                   compiler_params=pltpu.CompilerParams(vmem_limit_bytes=VMEM_LIMIT))(*ws, *gs, *ms, *vs)


def _adamw(w, g, m, v, *, name):
    r, c = w.shape
    tr = r
    for cand in (256, 176, 128, 64):
        if r > cand and r % cand == 0:
            tr = cand
            break

    def body(w_ref, g_ref, m_ref, v_ref, d_ref, mo_ref, vo_ref):
        _adamw_refs(w_ref, g_ref, m_ref, v_ref, d_ref, mo_ref, vo_ref)

    spec = pl.BlockSpec((tr, c), lambda i: (i, 0))
    return _pcall(body, name=name, grid=(r // tr,), in_specs=[spec] * 4, out_specs=[spec] * 3,
                  out_shape=[_sds((r, c))] * 3, dims=("parallel",))(w, g, m, v)


def _prefetch_call(body, *, name, grid, in_specs, out_specs, out_shape, operands):
    grid_spec = pltpu.PrefetchScalarGridSpec(num_scalar_prefetch=1, grid=grid, in_specs=in_specs, out_specs=out_specs)
    params = pltpu.CompilerParams(dimension_semantics=("parallel",) * len(grid), vmem_limit_bytes=VMEM_LIMIT)
    return _pallas(body, name=name, grid_spec=grid_spec, out_shape=out_shape, compiler_params=params)(*operands)


def _place_cols(buf, shard, place):
    rows, cols = shard.shape
    tr = 256

    def body(place_ref, s_ref, b_ref, o_ref):
        o_ref[...] = s_ref[...]

    grid_spec = pltpu.PrefetchScalarGridSpec(
        num_scalar_prefetch=1, grid=(rows // tr,),
        in_specs=[pl.BlockSpec((tr, cols), lambda i, p: (i, 0)), pl.BlockSpec(memory_space=pltpu.HBM)],
        out_specs=pl.BlockSpec((tr, cols), lambda i, p: (i, p[0])))
    return _pallas(body, name="place_own_cols", grid_spec=grid_spec, out_shape=_sds(buf.shape, buf.dtype),
                   input_output_aliases={2: 0},
                   compiler_params=pltpu.CompilerParams(dimension_semantics=("parallel",),
                                                        vmem_limit_bytes=VMEM_LIMIT))(place, shard, buf)


def _half_rows_tile(hr):
    return hr if hr <= 256 else 176 if hr % 176 == 0 else 256


def _add_half(g, landed, place, *, name):
    def body(place_ref, g_ref, l_ref, o_ref):
        own = g_ref[0] if len(g_ref.shape) == 4 else g_ref[...]
        o_ref[...] = (own + l_ref[...]).astype(BF16)

    if g.ndim == 4:
        _, _, hr, c = g.shape
        tr = _half_rows_tile(hr)
        blk = (1, tr, c)
        return _prefetch_call(
            body, name=name, grid=(N_CHIPS, hr // tr),
            in_specs=[pl.BlockSpec((1,) + blk, lambda j, i, p: (j, p[1], i, 0)), pl.BlockSpec(blk, lambda j, i, p: (j, i, 0))],
            out_specs=pl.BlockSpec(blk, lambda j, i, p: (j, i, 0)), out_shape=_sds(landed.shape, BF16),
            operands=(place, g, landed))
    hr, c = landed.shape
    tr, tc = 256, _tile(c, 2176)
    nb = hr // tr
    return _prefetch_call(
        body, name=name, grid=(nb, c // tc),
        in_specs=[pl.BlockSpec((tr, tc), lambda i, j, p: (p[1] * nb + i, j)), pl.BlockSpec((tr, tc), lambda i, j, p: (i, j))],
        out_specs=pl.BlockSpec((tr, tc), lambda i, j, p: (i, j)), out_shape=_sds(landed.shape, BF16),
        operands=(place, g, landed))


def _sum_chips(chip_sum, lands, place, *, name, tc, window_stride=0):
    _, hr, c = lands.shape
    tr = _half_rows_tile(hr)
    nb = hr // tr
    ncb = c // tc

    def body(place_ref, own_ref, a_ref, b_ref, c_ref, o_ref):
        own = own_ref[0] if len(own_ref.shape) == 3 else own_ref[...]
        o_ref[...] = ((own.astype(F32) + a_ref[0].astype(F32)) + b_ref[0].astype(F32)) + c_ref[0].astype(F32)

    land = lambda k: pl.BlockSpec((1, tr, tc), lambda i, j, p: ((p[0] + k) % N_CHIPS, i, j))
    if chip_sum.ndim == 3:
        own_spec = land(0)
    else:
        stride = window_stride // tc
        own_spec = pl.BlockSpec((tr, tc), lambda i, j, p: (i, p[0] * stride + j))
    return _prefetch_call(
        body, name=name, grid=(nb, ncb), in_specs=[own_spec, land(1), land(2), land(3)],
        out_specs=pl.BlockSpec((tr, tc), lambda i, j, p: (p[1] * nb + i, j)), out_shape=_sds((2 * hr, c)),
        operands=(place, chip_sum, lands, lands, lands))


_HBM = pl.BlockSpec(memory_space=pltpu.HBM)


def _place():
    x, y, c = lax.axis_index("x"), lax.axis_index("y"), lax.axis_index("c")
    chips = [(1 - x, y), (x, 1 - y), (1 - x, 1 - y)]
    return x, y, c, chips


def _rcopy(src, dst, send_sem, recv_sem, to):
    return pltpu.make_async_remote_copy(src_ref=src, dst_ref=dst, send_sem=send_sem, recv_sem=recv_sem,
                                        device_id=to, device_id_type=MESH)


UP_COLS = 2 * D_FF // N_CHIPS
IN_WINDOW = 640
IN_STRIDE = 512


class _Hosted:
    def __init__(self, operands, out_shapes, n_sems, start, finish, aliases=None, local_sems=0):
        self.operands, self.out_shapes, self.n_sems = list(operands), list(out_shapes), n_sems
        self.start, self.finish, self.aliases, self.local_sems = start, finish, dict(aliases or {}), local_sems

    def scratch(self):
        return ([pltpu.SemaphoreType.DMA((self.n_sems,)), pltpu.SemaphoreType.DMA((self.n_sems,))]
                + [pltpu.SemaphoreType.DMA] * self.local_sems)


def _both(a, b):
    na, nao, nas = len(a.operands), len(a.out_shapes), len(a.scratch())

    def start(ins, outs, sems):
        a.start(ins[:na], outs[:nao], sems[:nas])
        b.start(ins[na:], outs[nao:], sems[nas:])

    def finish(ins, outs, sems):
        a.finish(ins[:na], outs[:nao], sems[:nas])
        b.finish(ins[na:], outs[nao:], sems[nas:])

    both = _Hosted(a.operands + b.operands, a.out_shapes + b.out_shapes, 0, start, finish,
                   aliases={**a.aliases, **{na + i: nao + o for i, o in b.aliases.items()}})
    both.scratch = lambda: a.scratch() + b.scratch()
    return both


def _then(a, b):
    nas = len(a.scratch())

    def finish(ins, outs, sems):
        a.finish(ins, outs, sems[:nas])
        b.start(ins, outs, sems[nas:])
        b.finish(ins, outs, sems[nas:])

    chain = _Hosted(a.operands, a.out_shapes, 0, lambda ins, outs, sems: a.start(ins, outs, sems[:nas]), finish,
                    aliases=a.aliases)
    chain.scratch = lambda: a.scratch() + b.scratch()
    return chain


def _run_hosted(hosted, *, name):
    n_in, n_out = len(hosted.operands), len(hosted.out_shapes)

    def body(*refs):
        parts = (refs[:n_in], refs[n_in:n_in + n_out], refs[n_in + n_out:])
        hosted.start(*parts)
        hosted.finish(*parts)

    return _pallas(body, name=name, in_specs=[_HBM] * n_in, out_specs=[_HBM] * n_out, out_shape=hosted.out_shapes,
                   input_output_aliases=hosted.aliases, scratch_shapes=hosted.scratch())(*hosted.operands)


def _host_pcall(core_body, hosted, first, last, *, n_in, n_out, n_scratch, name, grid, in_specs, out_specs, out_shape,
                scratch_shapes, dims, operands):
    if hosted is None:
        outs = _pcall(core_body, name=name, grid=grid, in_specs=in_specs, out_specs=out_specs, out_shape=out_shape,
                      scratch_shapes=scratch_shapes, dims=dims)(*operands)
        return outs, []
    hi, ho = len(hosted.operands), len(hosted.out_shapes)

    def body(*refs):
        a, b = n_in, n_in + hi
        c, d = b + n_out, b + n_out + ho
        e = d + n_scratch
        parts = (refs[a:b], refs[c:d], refs[e:])

        @pl.when(first())
        def _():
            hosted.start(*parts)

        core_body(*refs[:a], *refs[b:c], *refs[d:e])

        @pl.when(last())
        def _():
            hosted.finish(*parts)

    params = pltpu.CompilerParams(dimension_semantics=("arbitrary",) * len(grid), vmem_limit_bytes=VMEM_LIMIT)
    outs = _pallas(body, name=name, grid=grid, in_specs=list(in_specs) + [_HBM] * hi, out_specs=list(out_specs) + [_HBM] * ho,
                   out_shape=list(out_shape) + hosted.out_shapes, scratch_shapes=list(scratch_shapes) + hosted.scratch(),
                   input_output_aliases={n_in + a: n_out + b for a, b in hosted.aliases.items()},
                   compiler_params=params)(*operands, *hosted.operands)
    return outs[:n_out], outs[n_out:]


WHOLE_HALF = (0, 1, 1)


def _band_rows(src, hc, band):
    first, count, of = band
    hr = src.shape[0] // 2
    return pl.ds(hc * hr + first * (hr // of), count * (hr // of))


def _gather_slot(src, out, chip, hc, band=WHOLE_HALF):
    cols = src.shape[1]
    if len(out.shape) == 2:
        return out.at[_band_rows(src, hc, band), pl.ds(pl.multiple_of(chip * cols, LANES), cols)]
    return out.at[chip, _band_rows(src, hc, band), :]


def _gathered_shape(shard, by_cols):
    if by_cols:
        return _sds((shard.shape[0], N_CHIPS * shard.shape[1]), shard.dtype)
    return _sds((N_CHIPS,) + shard.shape, shard.dtype)


def _plan_gather_ici(shards, by_cols, whole=(), bands=None, into=None):
    n = len(shards)
    bands = bands or [WHOLE_HALF] * n
    into = into or [None] * n
    given = [w for w in range(n) if into[w] is not None]
    n_ops = n + len(whole)

    def copies(ins, outs, sems):
        send_sems, recv_sems = sems[0], sems[1]
        x, y, c, chips = _place()
        me = 2 * x + y
        sends, waits = [], []
        for w in range(n + len(whole)):
            for k, (cx, cy) in enumerate(chips):
                sem = (send_sems.at[3 * w + k], recv_sems.at[3 * w + k])
                if w < n:
                    sends.append(_rcopy(ins[w].at[_band_rows(ins[w], c, bands[w]), :],
                                        _gather_slot(ins[w], outs[w], me, c, bands[w]), *sem, (cx, cy, c)))
                    landed = _gather_slot(ins[w], outs[w], 2 * cx + cy, c, bands[w])
                else:
                    sends.append(_rcopy(ins[w], outs[w].at[me], *sem, (cx, cy, c)))
                    landed = outs[w].at[2 * cx + cy]
                waits.append(_rcopy(landed, landed, *sem, (cx, cy, c)))
        return sends, waits

    def start(ins, outs, sems):
        for cp in copies(ins, outs, sems)[0]:
            cp.start()

    def finish(ins, outs, sems):
        sends, waits = copies(ins, outs, sems)
        for cp in waits:
            cp.wait_recv()
        for cp in sends:
            cp.wait_send()

    out_shapes = [_gathered_shape(s, bc) for s, bc in zip(shards, by_cols)] + [_sds((N_CHIPS,) + a.shape, a.dtype) for a in whole]
    return _Hosted(list(shards) + list(whole) + [into[w] for w in given], out_shapes, 3 * n_ops, start, finish,
                   aliases={n_ops + i: w for i, w in enumerate(given)})


def _plan_gather_d2d(bufs, shard_shapes, bands=None):
    n = len(bufs)
    bands = bands or [WHOLE_HALF] * n

    def copies(ins, outs, sems):
        send_sems, recv_sems = sems
        x, y, c, chips = _place()
        sibling = (x, y, 1 - c)
        sends, waits = [], []
        for w in range(n):
            for k, (cx, cy) in enumerate(chips):
                sem = (send_sems.at[3 * w + k], recv_sems.at[3 * w + k])
                landed = _gather_slot(shard_shapes[w], outs[w], 2 * cx + cy, c, bands[w])
                other = _gather_slot(shard_shapes[w], outs[w], 2 * cx + cy, 1 - c, bands[w])
                sends.append(_rcopy(landed, landed, *sem, sibling))
                waits.append(_rcopy(other, other, *sem, sibling))
        return sends, waits

    def start(ins, outs, sems):
        for cp in copies(ins, outs, sems)[0]:
            cp.start()

    def finish(ins, outs, sems):
        sends, waits = copies(ins, outs, sems)
        for cp in waits:
            cp.wait_recv()
        for cp in sends:
            cp.wait_send()

    return _Hosted(bufs, [_sds(b.shape, b.dtype) for b in bufs], 3 * n, start, finish, aliases={w: w for w in range(n)})


def _plan_allgather_first(block):
    def copies(ins, outs, sems):
        send_sems, recv_sems = sems
        x, y, c, chips = _place()
        me = 4 * x + 2 * y + c
        peers = [(x, y, 1 - c)] + [(cx, cy, c) for cx, cy in chips]
        sends = [_rcopy(ins[0], outs[0].at[me], send_sems.at[k], recv_sems.at[k], p) for k, p in enumerate(peers)]
        waits = [_rcopy(outs[0].at[4 * px + 2 * py + pc], outs[0].at[4 * px + 2 * py + pc], send_sems.at[k],
                        recv_sems.at[k], (px, py, pc)) for k, (px, py, pc) in enumerate(peers)]
        return sends, waits

    def start(ins, outs, sems):
        for cp in copies(ins, outs, sems)[0]:
            cp.start()

    def finish(ins, outs, sems):
        sends, waits = copies(ins, outs, sems)
        for cp in waits:
            cp.wait_recv()
        for cp in sends:
            cp.wait_send()

    return _Hosted([block], [_sds((8,) + block.shape)], 4, start, finish)


def _plan_allgather_second(gathered):
    def copies(ins, outs, sems):
        send_sems, recv_sems = sems
        x, y, c, chips = _place()
        sends, waits = [], []
        for k, (cx, cy) in enumerate(chips):
            landed = outs[0].at[4 * cx + 2 * cy + c]
            other = outs[0].at[4 * cx + 2 * cy + 1 - c]
            sends.append(_rcopy(landed, landed, send_sems.at[k], recv_sems.at[k], (x, y, 1 - c)))
            waits.append(_rcopy(other, other, send_sems.at[k], recv_sems.at[k], (x, y, 1 - c)))
        return sends, waits

    def start(ins, outs, sems):
        for cp in copies(ins, outs, sems)[0]:
            cp.start()

    def finish(ins, outs, sems):
        sends, waits = copies(ins, outs, sems)
        for cp in waits:
            cp.wait_recv()
        for cp in sends:
            cp.wait_send()

    return _Hosted([gathered], [_sds(gathered.shape)], 3, start, finish, aliases={0: 0})


def _place_block(gathered, block, device):
    rows, lanes = block.shape

    def body(dev_ref, b_ref, g_ref, o_ref):
        o_ref[0] = b_ref[...]

    grid_spec = pltpu.PrefetchScalarGridSpec(
        num_scalar_prefetch=1, grid=(1,),
        in_specs=[pl.BlockSpec((rows, lanes), lambda i, d: (0, 0)), pl.BlockSpec(memory_space=pltpu.HBM)],
        out_specs=pl.BlockSpec((1, rows, lanes), lambda i, d: (d[0], 0, 0)))
    return _pallas(body, name="place_own_block", grid_spec=grid_spec, out_shape=_sds(gathered.shape),
                   input_output_aliases={2: 0},
                   compiler_params=pltpu.CompilerParams(dimension_semantics=("arbitrary",),
                                                        vmem_limit_bytes=VMEM_LIMIT))(device, block, gathered)


def _sum_devices(gathered):
    _, rows, lanes = gathered.shape
    tr = rows // 2 if rows % 16 == 0 else rows

    def body(g_ref, o_ref):
        acc = g_ref[0]
        for d in range(1, 8):
            acc = acc + g_ref[d]
        o_ref[...] = acc

    return _pcall(body, name="sum_devices", grid=(rows // tr,), in_specs=[pl.BlockSpec((8, tr, lanes), lambda i: (0, i, 0))],
                  out_specs=pl.BlockSpec((tr, lanes), lambda i: (i, 0)), out_shape=_sds((rows, lanes)), dims=("parallel",))(gathered)


def _plan_swap(grads):
    def copies(ins, outs, sems):
        send_sems, recv_sems = sems
        x, y, c, _ = _place()
        cps = []
        for w, g_ref in enumerate(ins):
            if len(g_ref.shape) == 4:
                theirs = g_ref.at[:, 1 - c]
            else:
                hr = g_ref.shape[0] // 2
                theirs = g_ref.at[pl.ds((1 - c) * hr, hr), :]
            cps.append(_rcopy(theirs, outs[w], send_sems.at[w], recv_sems.at[w], (x, y, 1 - c)))
        return cps

    def start(ins, outs, sems):
        for cp in copies(ins, outs, sems):
            cp.start()

    def finish(ins, outs, sems):
        for cp in copies(ins, outs, sems):
            cp.wait()

    out_shapes = [_sds((g.shape[0], g.shape[2], g.shape[3])) if g.ndim == 4 else _sds((g.shape[0] // 2, g.shape[1]))
                  for g in grads]
    return _Hosted(grads, out_shapes, len(grads), start, finish)


def _plan_scatter(chip_sums, windows):
    def copies(ins, outs, sems):
        send_sems, recv_sems = sems
        x, y, c, chips = _place()
        me = 2 * x + y
        sends, waits = [], []
        for w, s_ref in enumerate(ins):
            for k, (cx, cy) in enumerate(chips):
                tgt = 2 * cx + cy
                if windows[w] is not None:
                    stride, width = windows[w]
                    part = s_ref.at[:, pl.ds(pl.multiple_of(tgt * stride, LANES), width)]
                else:
                    part = s_ref.at[tgt]
                sem = (send_sems.at[3 * w + k], recv_sems.at[3 * w + k])
                sends.append(_rcopy(part, outs[w].at[me], *sem, (cx, cy, c)))
                slot = outs[w].at[tgt]
                waits.append(_rcopy(slot, slot, *sem, (cx, cy, c)))
        return sends, waits

    def start(ins, outs, sems):
        for cp in copies(ins, outs, sems)[0]:
            cp.start()

    def finish(ins, outs, sems):
        sends, waits = copies(ins, outs, sems)
        for cp in waits:
            cp.wait_recv()
        for cp in sends:
            cp.wait_send()

    out_shapes = [_sds((N_CHIPS, s.shape[0], win[1]), BF16) if win is not None else _sds(s.shape, BF16)
                  for s, win in zip(chip_sums, windows)]
    return _Hosted(chip_sums, out_shapes, 3 * len(chip_sums), start, finish)


def _plan_join(reds):
    def copies(ins, outs, sems):
        send_sems, recv_sems = sems
        x, y, c, _ = _place()
        sends, waits = [], []
        for w, out in enumerate(outs):
            hr = out.shape[0] // 2
            mine = out.at[pl.ds(c * hr, hr), :]
            theirs = out.at[pl.ds((1 - c) * hr, hr), :]
            sends.append(_rcopy(mine, mine, send_sems.at[w], recv_sems.at[w], (x, y, 1 - c)))
            waits.append(_rcopy(theirs, theirs, send_sems.at[w], recv_sems.at[w], (x, y, 1 - c)))
        return sends, waits

    def start(ins, outs, sems):
        for cp in copies(ins, outs, sems)[0]:
            cp.start()

    def finish(ins, outs, sems):
        sends, waits = copies(ins, outs, sems)
        for cp in waits:
            cp.wait_recv()
        for cp in sends:
            cp.wait_send()

    return _Hosted(reds, [_sds(r.shape) for r in reds], len(reds), start, finish, aliases={w: w for w in range(len(reds))})


def _allreduce_small(v, hosted):
    m_per = v.shape[0]
    hi, ho = len(hosted.operands), len(hosted.out_shapes)

    def body(*refs):
        v_ref, out_ref = refs[0], refs[1 + hi]
        all_ref, send_sems, recv_sems, local_sem = refs[2 + hi + ho:6 + hi + ho]
        carried = (refs[1:1 + hi], refs[2 + hi:2 + hi + ho], refs[6 + hi + ho:])
        hosted.start(*carried)
        x, y, c, chips = _place()
        me, sibling = (x, y, c), (x, y, 1 - c)

        def rows(px, py, pc):
            return all_ref.at[pl.ds((4 * px + 2 * py + pc) * m_per, m_per), :]

        def copy(k, block, to, src=None):
            return _rcopy(rows(*block) if src is None else src, rows(*block), send_sems.at[k], recv_sems.at[k], to)

        mine = pltpu.make_async_copy(v_ref, rows(*me), local_sem)
        mine.start()
        first = [copy(0, me, sibling, src=v_ref)]
        first += [copy(1 + k, me, (*chip, c), src=v_ref) for k, chip in enumerate(chips)]
        for cp in first:
            cp.start()
        passed = [copy(4 + k, (*chip, c), sibling) for k, chip in enumerate(chips)]
        for k, chip in enumerate(chips):
            copy(1 + k, (*chip, c), me).wait_recv()
            passed[k].start()
        copy(0, sibling, me).wait_recv()
        for k, chip in enumerate(chips):
            copy(4 + k, (*chip, 1 - c), me).wait_recv()
        for cp in first + passed:
            cp.wait_send()
        mine.wait()
        acc = all_ref[pl.ds(0, m_per), :]
        for d in range(1, 8):
            acc = acc + all_ref[pl.ds(d * m_per, m_per), :]
        out_ref[...] = acc
        hosted.finish(*carried)

    vm = pl.BlockSpec(memory_space=pltpu.VMEM)
    outs = _pallas(body, name="allreduce_small", in_specs=[vm] + [_HBM] * hi, out_specs=[vm] + [_HBM] * ho,
                   out_shape=[_sds((m_per, LANES))] + hosted.out_shapes,
                   input_output_aliases={1 + a: 1 + b for a, b in hosted.aliases.items()},
                   scratch_shapes=[pltpu.VMEM((8 * m_per, LANES), F32), pltpu.SemaphoreType.DMA((7,)),
                                   pltpu.SemaphoreType.DMA((7,)), pltpu.SemaphoreType.DMA] + hosted.scratch(),
                   compiler_params=pltpu.CompilerParams(vmem_limit_bytes=VMEM_LIMIT))(v, *hosted.operands)
    return outs[0], outs[1:]


def _block_diag(blocks):
    j, g, a, b = blocks.shape
    eye = jnp.eye(g, dtype=bool)[None, :, None, :, None]
    return jnp.where(eye, blocks[:, :, :, None, :], jnp.zeros((), blocks.dtype)).reshape(j, g * a, g * b)


def _diag_blocks(m, a, b):
    j = m.shape[0]
    g = m.shape[1] // a
    t = m.reshape(j, g, a, g, b)
    eye = jnp.eye(g, dtype=bool)[None, :, None, :, None]
    return jnp.sum(jnp.where(eye, t, 0.0), axis=3)


_SMALL = (("g_mix", (1024,)), ("b_f", (8,)), ("g_q", (64,)), ("g_k", (64,)), ("lambda_re", (32, 64)),
          ("lambda_im", (32, 64)), ("log_step", (32,)), ("b_re", (32, 64, 16)), ("b_im", (32, 64, 16)),
          ("c_re", (32, 16, 64)), ("c_im", (32, 16, 64)), ("d_skip", (32, 16)), ("b_glu", (512,)),
          ("g_attn_out", (512,)), ("g_ssm_out", (512,)), ("g_ffn", (1024,)), ("conv_b", (5632,)))


_LATE_SMALL = ("g_mix", "b_f", "g_q", "g_k")
_EARLY_SMALL = tuple(n for n, _ in _SMALL if n not in _LATE_SMALL)


def _packed_rows(n):
    tile = SUBLANES * LANES
    return -(-n // tile) * SUBLANES


def _pack_small(arrs):
    parts = []
    for a in arrs:
        flat = a.reshape(-1)
        rows = _packed_rows(flat.shape[0])
        parts.append(jnp.pad(flat, (0, rows * LANES - flat.shape[0])).reshape(rows, LANES))
    return jnp.concatenate(parts, axis=0)


def _unpack_small(buf, shapes):
    out, r = [], 0
    for shape in shapes:
        n = math.prod(shape)
        out.append(buf[r:r + _packed_rows(n)].reshape(-1)[:n].reshape(shape))
        r += _packed_rows(n)
    return out


def _halves(t):
    return t.reshape(N_CHIPS, 2, t.shape[0] // (2 * N_CHIPS), t.shape[1])


class _MeshComm:
    def __init__(self, args):
        x, y, self.core = lax.axis_index("x"), lax.axis_index("y"), lax.axis_index("c")
        self.chip = 2 * x + y
        self.place = jnp.stack([self.chip, self.core]).astype(jnp.int32)
        self.shards = {n: args[n].astype(BF16) for n in ("w_in", "w_glu", "w_out", "w_up", "w_down")}
        self.conv_w = args["conv_w"]

    def _own(self, stacked, mine):
        return lax.dynamic_update_slice(stacked, mine[None], (self.chip,) + (0,) * mine.ndim)

    def w_in(self):
        sh = self.shards["w_in"]
        (buf,) = _run_hosted(_then(_plan_gather_ici([sh], [False]), _plan_gather_d2d([sh], [sh])), name="gather_w_in")
        whole = self._own(buf, sh).transpose(1, 0, 2).reshape(D_MODEL, IN_COLS)
        return jnp.pad(whole, ((0, 0), (0, Z_COLS - IN_COLS)))

    def gather_first(self):
        self.mid = [self.shards[n] for n in ("w_glu", "w_out", "w_down")]
        return _plan_gather_ici(self.mid + [self.shards["w_up"]], [False, False, False, True], whole=[self.conv_w],
                                bands=[WHOLE_HALF] * 3 + [(0, 1, 4)])

    def gather_second(self, landed):
        self.g_cw = landed[4]
        return _both(_plan_gather_d2d(list(landed[:3]), self.mid),
                     _plan_gather_ici([self.shards["w_up"]], [True], bands=[(1, 3, 4)], into=[landed[3]]))

    def weights(self, gathered):
        g_glu, g_out, g_down = gathered[:3]
        own = self._own
        return (own(g_glu, self.mid[0]).reshape(SSM_W, SSM_W), own(g_out, self.mid[1]).reshape(D_MODEL, D_MODEL),
                own(g_down, self.mid[2]).reshape(D_FF, D_MODEL),
                own(self.g_cw, self.conv_w).transpose(1, 0, 2).reshape(3, 2 * D_FF))

    def gather_third(self, gathered):
        return _plan_gather_d2d([gathered[3]], [self.shards["w_up"]])

    def w_up(self, passed):
        return _place_cols(passed[0], self.shards["w_up"], self.place)

    def swap_down(self, d_w_down):
        self.d_down = _halves(d_w_down)
        return _plan_swap([self.d_down])

    def swap(self, landed_down, d_w_up, d_w_glu, d_w_out):
        self.sum_down = _add_half(self.d_down, landed_down[0], self.place, name="add_w_down")
        self.early = [d_w_up, _halves(d_w_glu), _halves(d_w_out)]
        return _both(_plan_scatter([self.sum_down], [None]), _plan_swap(self.early))

    def scatter(self, landed, small_block):
        self.land_down = landed[0]
        self.early_sums = [_add_half(g, l, self.place, name="add_" + n)
                           for g, l, n in zip(self.early, landed[1:], ("w_up", "w_glu", "w_out"))]
        return _both(_plan_scatter(self.early_sums, [(UP_COLS, UP_COLS), None, None]), _plan_allgather_first(small_block))

    def swap_in(self, d_w_in):
        self.d_in = d_w_in
        return _plan_swap([d_w_in])

    def scatter_in(self, landed):
        self.sum_in = _add_half(self.d_in, landed[0], self.place, name="add_w_in")
        return _plan_scatter([self.sum_in], [(IN_STRIDE, IN_WINDOW)])

    def small_second(self, landed_small):
        return _plan_allgather_second(landed_small[0])

    def reduce(self, lands):
        early_lands, (land_in,) = lands
        sum_in = self.sum_in
        es, el = self.early_sums, early_lands
        todo = [(sum_in, land_in, "w_in", LANES, IN_STRIDE), (es[1], el[1], "w_glu", SSM_W, 0),
                (es[2], el[2], "w_out", D_MODEL, 0), (es[0], el[0], "w_up", UP_COLS, UP_COLS),
                (self.sum_down, self.land_down, "w_down", D_MODEL, 0)]
        return _plan_join([_sum_chips(s, l, self.place, name="sum_" + n, tc=tc, window_stride=st)
                           for s, l, n, tc, st in todo])

    def reduced(self, reds):
        g_big = dict(zip(("w_in", "w_glu", "w_out", "w_up", "w_down"), reds))
        g_big["w_in"] = lax.dynamic_slice_in_dim(reds[0], 2 * self.chip, IN_COLS // N_CHIPS, axis=1)
        return g_big


def _local_step(x, tgt, p, comm):
    s = x.shape[0]
    row = lambda v: v.reshape(1, -1)
    g_mix, g_ffn = row(p["g_mix"]), row(p["g_ffn"])
    g_att, g_ssm, b_glu, conv_b = row(p["g_attn_out"]), row(p["g_ssm_out"]), row(p["b_glu"]), row(p["conv_b"])
    gq = row(jnp.tile(p["g_q"], HEADS))
    gk = row(jnp.tile(p["g_k"], HEADS))
    bf = row(jnp.pad(p["b_f"], (0, LANES - HEADS)))
    gg = jnp.kron(jnp.eye(HEADS, dtype=F32), jnp.ones((HEAD_DIM, HEAD_DIM), F32)).astype(BF16)
    dsk = row(p["d_skip"])

    rep = lambda a: jnp.repeat(a, SSM_GROUP, axis=0)
    lr, li = rep(p["lambda_re"]), rep(p["lambda_im"])
    ls = rep(jnp.broadcast_to(p["log_step"][:, None], (SSM_GROUPS, SSM_STATE)))
    bt_re = p["b_re"].transpose(0, 2, 1).reshape(_PARAM_SHAPE)
    bt_im = p["b_im"].transpose(0, 2, 1).reshape(_PARAM_SHAPE)
    a_re_rep, a_im_rep, bb_re, bb_im = _ssm_params(lr, li, ls, bt_re, bt_im)
    ar = a_re_rep[::SSM_GROUP].reshape(SSM_CHUNKS, 1, CHUNK_S)
    ai = a_im_rep[::SSM_GROUP].reshape(SSM_CHUNKS, 1, CHUNK_S)
    chunked = lambda t: t.reshape(SSM_CHUNKS, SSM_GROUPS // SSM_CHUNKS, SSM_GROUP, SSM_STATE)
    bbr = _block_diag(chunked(bb_re)).astype(BF16)
    bbi = _block_diag(chunked(bb_im)).astype(BF16)
    to_cc = lambda c: _block_diag(chunked(c).transpose(0, 1, 3, 2)).astype(BF16)
    ccr, cci = to_cc(p["c_re"]), to_cc(p["c_im"])

    w_in_r = comm.w_in()
    hb, z = _in_proj(x, g_mix, w_in_r)
    qh, kh, vh, ub, uf, c128 = _attn_prep(z, gq, gk, bf, gg)
    crow = c128[:, :HEADS].T.reshape(HEADS, 1, s)
    (oh, lse), landed = _attn_fwd(qh, kh, vh, crow, comm.gather_first())
    (xr, xi, y), gathered = _ssm_fwd(ub, uf, bbr, bbi, ar, ai, ccr, cci, dsk, comm.gather_second(landed))
    w_glu_b, w_out_b, w_down_b, conv_w_full = comm.weights(gathered)
    (x1, mixb, h2b), passed = _mix_out(y, oh, x, w_glu_b, b_glu, g_att, g_ssm, w_out_b, g_ffn, comm.gather_third(gathered))
    w_up_b = comm.w_up(passed)
    up = _mm(h2b, w_up_b, name="ffn_up", tm=1024, tn=1408, tk=1024)
    act = _conv_act(up, conv_w_full, conv_b)
    dy, dyb, loss_blk = _down_loss(act, w_down_b, x1, tgt)

    d_w_down = _mm(act, dyb, ta=True, name="d_w_down", tm=1408, tn=1024, tk=2048)
    dact = _mm(dyb, w_down_b, tb=True, name="d_act", tm=1024, tn=1408, tk=1024)
    dupb, dcw = _conv_act_bwd(up, dact, conv_w_full, conv_b)
    d_w_up = _mm(h2b, dupb, ta=True, b_parts=2, name="d_w_up", tm=1024, tn=1408, tk=2048)
    dh2 = _mm(dupb, w_up_b, tb=True, a_parts=2, name="d_h2", tm=1024, tn=1024, tk=1408)
    (dx1, dx1b, doh, dys, d_w_glu, d_g_ffn, d_g_att, d_g_ssm, d_b_glu), landed_down = _mix_bwd(
        dy, dh2, x1, g_ffn, w_out_b, y, oh, w_glu_b, b_glu, g_att, g_ssm, comm.swap_down(d_w_down))
    d_w_out = _mm(mixb, dx1b, ta=True, name="d_w_out", tm=1024, tn=1024, tk=2048)
    (du, dbbr, dbbi, dccr, dcci, dar, dai, dd), swapped = _ssm_bwd(dys, uf, ub, xr, xi, bbr, bbi, ar, ai, ccr, cci, dsk,
                                                                comm.swap(landed_down, d_w_up, d_w_glu, d_w_out))
    unchunk = lambda t: t.reshape(_PARAM_SHAPE)
    dbb_re = unchunk(_diag_blocks(dbbr, SSM_GROUP, SSM_STATE))
    dbb_im = unchunk(_diag_blocks(dbbi, SSM_GROUP, SSM_STATE))
    first_row = (jnp.arange(_PARAM_SHAPE[0]) % SSM_GROUP == 0)[:, None]
    da_re = jnp.where(first_row, rep(dar.reshape(SSM_GROUPS, SSM_STATE)), 0.0)
    da_im = jnp.where(first_row, rep(dai.reshape(SSM_GROUPS, SSM_STATE)), 0.0)
    expand_t = (jnp.arange(SSM_GROUPS)[:, None] == (jnp.arange(_PARAM_SHAPE[0]) // SSM_GROUP)[None, :]).astype(BF16)
    d_lr, d_li, d_ls, d_bt_re, d_bt_im = _ssm_params_bwd(lr, li, ls, bt_re, bt_im, da_re, da_im, dbb_re, dbb_im, expand_t)
    from_bt = lambda t: t.reshape(SSM_GROUPS, SSM_GROUP, SSM_STATE).transpose(0, 2, 1)
    from_cc = lambda t: _diag_blocks(t, SSM_STATE, SSM_GROUP).transpose(0, 1, 3, 2).reshape(SSM_GROUPS, SSM_GROUP, SSM_STATE)

    small = {
        "lambda_re": d_lr, "lambda_im": d_li, "log_step": d_ls,
        "b_re": from_bt(d_bt_re), "b_im": from_bt(d_bt_im), "c_re": from_cc(dccr), "c_im": from_cc(dcci),
        "d_skip": dd, "b_glu": d_b_glu, "g_attn_out": d_g_att, "g_ssm_out": d_g_ssm, "g_ffn": d_g_ffn,
        "conv_b": dcw[:, 3],
    }
    d_conv_w = dcw[:, 0:3].transpose(1, 0, 2).reshape(3, 2 * D_FF)
    early_small = _pack_small([small[n] for n in _EARLY_SMALL] + [d_conv_w])

    (dqh, dkh, dvh, dcrow), landed = _attn_bwd(qh, kh, vh, crow, lse, doh, comm.scatter(swapped, early_small))
    early_lands, small_landed = landed[:3], landed[3:]
    dc128 = jnp.pad(dcrow.reshape(HEADS, s).T, ((0, 0), (0, LANES - HEADS)))
    (dzb, d_gq, d_gk, d_bf), small_gathered = _prep_bwd(z, dqh, dkh, dvh, du, dc128, gq, gk, bf, gg,
                                                        comm.small_second(small_landed))
    d_w_in_r = _mm(hb, dzb, ta=True, name="d_w_in", tm=512, tn=Z_COLS, tk=2048)
    dh, swapped_in = _mm(dzb, w_in_r, tb=True, name="d_h", tm=1024, tn=1024, tk=Z_COLS, carry=True,
                         hosted=comm.swap_in(d_w_in_r))
    (dx, d_g_mix), land_in = _in_norm_bwd(x, g_mix, dh, dx1, comm.scatter_in(swapped_in))
    small.update({"g_mix": d_g_mix, "b_f": d_bf[0, :HEADS], "g_q": d_gq.reshape(HEADS, HEAD_DIM).sum(0),
                  "g_k": d_gk.reshape(HEADS, HEAD_DIM).sum(0)})
    big = {"w_in": d_w_in_r, "w_glu": d_w_glu, "w_out": d_w_out, "w_up": d_w_up, "w_down": d_w_down}
    return loss_blk[0, 0], dx, big, small, d_conv_w, (early_lands, land_in, small_gathered, early_small)


def kernel(x, g_mix, w_in, b_f, g_q, g_k, lambda_re, lambda_im, log_step, b_re, b_im, c_re, c_im, d_skip, w_glu, b_glu, g_attn_out, g_ssm_out, w_out, g_ffn, w_up, conv_w, conv_b, w_down, loss_target, m_g_mix, m_w_in, m_b_f, m_g_q, m_g_k, m_lambda_re, m_lambda_im, m_log_step, m_b_re, m_b_im, m_c_re, m_c_im, m_d_skip, m_w_glu, m_b_glu, m_g_attn_out, m_g_ssm_out, m_w_out, m_g_ffn, m_w_up, m_conv_w, m_conv_b, m_w_down, v_g_mix, v_w_in, v_b_f, v_g_q, v_g_k, v_lambda_re, v_lambda_im, v_log_step, v_b_re, v_b_im, v_c_re, v_c_im, v_d_skip, v_w_glu, v_b_glu, v_g_attn_out, v_g_ssm_out, v_w_out, v_g_ffn, v_w_up, v_conv_w, v_conv_b, v_w_down):
    args = dict(locals())
    order = ["g_mix", "w_in", "b_f", "g_q", "g_k", "lambda_re", "lambda_im", "log_step", "b_re", "b_im", "c_re", "c_im",
             "d_skip", "w_glu", "b_glu", "g_attn_out", "g_ssm_out", "w_out", "g_ffn", "w_up", "conv_w", "conv_b", "w_down"]
    comm = _MeshComm(args)
    chip = comm.chip
    loss_part, dx, big, small, d_conv_w, lands = _local_step(x[0], loss_target[0], args, comm)

    join = comm.reduce(lands[:2])

    shapes = dict(_SMALL)
    small_names = [n for n, _ in _SMALL]
    device = (2 * chip + comm.core).reshape(1).astype(jnp.int32)
    early = _unpack_small(_sum_devices(_place_block(lands[2][0], lands[3], device)),
                          [shapes[n] for n in _EARLY_SMALL] + [(3, 2 * D_FF)])
    late_sum, reds = _allreduce_small(_pack_small([small[n] for n in _LATE_SMALL] + [loss_part]), join)
    g_big = comm.reduced(reds)
    late = _unpack_small(late_sum, [shapes[n] for n in _LATE_SMALL] + [()])
    loss = late[-1]
    g_conv_w = lax.dynamic_slice_in_dim(early[-1], chip * (2 * D_FF // N_CHIPS), 2 * D_FF // N_CHIPS, axis=1)
    g_small = {**dict(zip(_EARLY_SMALL, early[:-1])), **dict(zip(_LATE_SMALL, late[:-1]))}

    grad, delta, new_m, new_v = {}, {}, {}, {}
    for n in ("w_in", "w_glu", "w_out", "w_up", "w_down"):
        grad[n] = g_big[n]
        delta[n], new_m[n], new_v[n] = _adamw(args[n], g_big[n], args["m_" + n], args["v_" + n], name="adamw_" + n)
    grad["conv_w"] = g_conv_w
    delta["conv_w"], new_m["conv_w"], new_v["conv_w"] = _adamw(conv_w, g_conv_w, m_conv_w, v_conv_w, name="adamw_conv_w")
    stepped = _adamw_small([args[n] for n in small_names], [g_small[n] for n in small_names],
                           [args["m_" + n] for n in small_names], [args["v_" + n] for n in small_names])
    for i, n in enumerate(small_names):
        grad[n] = g_small[n]
        delta[n], new_m[n], new_v[n] = stepped[3 * i:3 * i + 3]

    return (loss, dx[None], *[grad[n] for n in order], *[delta[n] for n in order], *[new_m[n] for n in order],
            *[new_v[n] for n in order])
```
